```python
import jax, jax.numpy as jnp
from jax import lax
import numpy as np

D_MODEL = 1024
BATCH = 8
SEQ = 4096
DEPTH = 2

N_CONV_LAYERS = DEPTH // 2
N_ATTN_LAYERS = DEPTH - N_CONV_LAYERS
CONV_WIDTH = 3
HEAD_DIM = 64
N_HEADS = D_MODEL // HEAD_DIM
BRANCHES = ((128, 1), (512, 4), (2048, 16))
N_BRANCHES = len(BRANCHES)
Q_WIDTH = N_BRANCHES * N_HEADS * HEAD_DIM
D_FF = -(-8 * D_MODEL // (3 * 256)) * 256
ROPE_THETA = 10000.0
RMS_EPS = 1e-6
NEG_INF = -1e30

kernel_name = "yoco_shortconv_dilated_attention_trunk"


def rms_norm(x, g):
    xf = x.astype(jnp.float32)
    y = xf * lax.rsqrt(jnp.mean(xf * xf, axis=-1, keepdims=True) + RMS_EPS)
    return (y * g.astype(jnp.float32)).astype(x.dtype)


def rope(t, positions):
    half = HEAD_DIM // 2
    inv_freq = ROPE_THETA ** (-jnp.arange(half, dtype=jnp.float32) / half)
    ang = positions.astype(jnp.float32)[..., None] * inv_freq
    cos = jnp.cos(ang)[:, :, None, :]
    sin = jnp.sin(ang)[:, :, None, :]
    t1 = t[..., :half].astype(jnp.float32)
    t2 = t[..., half:].astype(jnp.float32)
    out = jnp.concatenate([t1 * cos - t2 * sin, t2 * cos + t1 * sin], axis=-1)
    return out.astype(t.dtype)


def short_conv_mixer(x, w_in, conv_w, w_out):
    b_gate, c_gate, h = jnp.split(x @ w_in, 3, axis=-1)
    u = c_gate * h
    rhs = conv_w[:, None, :].astype(u.dtype)
    conv = lax.conv_general_dilated(
        u, rhs, window_strides=(1,), padding=[(CONV_WIDTH - 1, 0)],
        dimension_numbers=("NWC", "WIO", "NWC"), feature_group_count=u.shape[-1])
    return (b_gate * conv) @ w_out


def swiglu(x, w_gate_up, w_down):
    g, u = jnp.split(x @ w_gate_up, 2, axis=-1)
    return (jax.nn.silu(g) * u) @ w_down


def dilated_branch(q, k, v, window, dilation):
    band = window // dilation
    B, S, H, Dh = q.shape
    chunk = dilation * band
    Sp = -(-S // chunk) * chunk
    nb = Sp // chunk
    pad = ((0, 0), (0, Sp - S), (0, 0), (0, 0))

    def to_blocks(t):
        t = jnp.pad(t, pad).reshape(B, nb, band, dilation, H, Dh)
        return t.transpose(0, 3, 4, 1, 2, 5)

    def with_prev(t):
        prev = jnp.pad(t, ((0, 0), (0, 0), (0, 0), (1, 0), (0, 0), (0, 0)))[:, :, :, :-1]
        return jnp.concatenate([prev, t], axis=4)

    qb = to_blocks(q * (HEAD_DIM ** -0.5))
    kk = with_prev(to_blocks(k))
    vv = with_prev(to_blocks(v))
    s = jnp.einsum("brhnqd,brhnkd->brhnqk", qb, kk).astype(jnp.float32)
    qi = jnp.arange(band)[:, None]
    kj = jnp.arange(2 * band)[None, :]
    dist = qi + band - kj
    in_band = (dist >= 0) & (dist <= band)
    has_prev = (kj >= band)[None] | (jnp.arange(nb)[:, None, None] > 0)
    mask = in_band[None] & has_prev
    s = jnp.where(mask, s, NEG_INF)
    m = jnp.max(s, axis=-1)
    p = jnp.exp(s - m[..., None])
    l = jnp.sum(p, axis=-1)
    o = jnp.einsum("brhnqk,brhnkd->brhnqd", p, vv.astype(jnp.float32)) / l[..., None]
    lse = m + jnp.log(l)
    o = o.transpose(0, 3, 4, 1, 2, 5).reshape(B, Sp, H, Dh)[:, :S]
    lse = lse.transpose(0, 3, 4, 1, 2).reshape(B, Sp, H)[:, :S]
    return o, lse


def dilated_attention_mixer(x, positions, k_sh, v_sh, w_q, w_o):
    B, S, _ = x.shape
    q = (x @ w_q).reshape(B, S, N_BRANCHES * N_HEADS, HEAD_DIM)
    q = rope(q, positions).reshape(B, S, N_BRANCHES, N_HEADS, HEAD_DIM)
    outs, lses = [], []
    for g, (window, dilation) in enumerate(BRANCHES):
        o, lse = dilated_branch(q[:, :, g], k_sh[:, :, g], v_sh[:, :, g], window, dilation)
        outs.append(o)
        lses.append(lse)
    wts = jax.nn.softmax(jnp.stack(lses, axis=0), axis=0)
    o = jnp.einsum("gbsh,gbshd->bshd", wts, jnp.stack(outs, axis=0))
    return o.astype(x.dtype).reshape(B, S, N_HEADS * HEAD_DIM) @ w_o


def shared_kv(h, positions, kv_norm, w_kv):
    B, S, _ = h.shape
    kv = (rms_norm(h, kv_norm) @ w_kv).reshape(B, S, 2, N_BRANCHES * N_HEADS, HEAD_DIM)
    k = rope(kv[:, :, 0], positions).reshape(B, S, N_BRANCHES, N_HEADS, HEAD_DIM)
    v = kv[:, :, 1].reshape(B, S, N_BRANCHES, N_HEADS, HEAD_DIM)
    return k, v


def _fwd_setup_inputs(seed: int = 0) -> dict:
    key = jax.random.key(seed)
    ks = jax.random.split(key, 20)
    f32 = jnp.float32

    def w(k, shape, fan_in):
        return jax.random.normal(k, shape, f32) * (fan_in ** -0.5)

    def gain(k, shape):
        return 1.0 + 0.05 * jax.random.normal(k, shape, f32)

    x = jax.random.normal(ks[0], (BATCH, SEQ, D_MODEL), f32)
    offset = jax.random.randint(ks[1], (BATCH, 1), 0, 4096, dtype=jnp.int32)
    positions = offset + jnp.arange(SEQ, dtype=jnp.int32)[None, :]
    nA, nB = N_CONV_LAYERS, N_ATTN_LAYERS
    return {
        "x": x,
        "positions": positions,
        "mix_norm_pre": gain(ks[2], (DEPTH, D_MODEL)),
        "mix_norm_post": gain(ks[3], (DEPTH, D_MODEL)),
        "ffn_norm_pre": gain(ks[4], (DEPTH, D_MODEL)),
        "ffn_norm_post": gain(ks[5], (DEPTH, D_MODEL)),
        "ffn_w_gate_up": w(ks[6], (DEPTH, D_MODEL, 2 * D_FF), D_MODEL),
        "ffn_w_down": w(ks[7], (DEPTH, D_FF, D_MODEL), D_FF),
        "conv_w_in": w(ks[8], (nA, D_MODEL, 3 * D_MODEL), D_MODEL),
        "conv_w": w(ks[9], (nA, CONV_WIDTH, D_MODEL), CONV_WIDTH),
        "conv_w_out": w(ks[10], (nA, D_MODEL, D_MODEL), D_MODEL),
        "kv_norm": gain(ks[11], (D_MODEL,)),
        "w_kv": w(ks[12], (D_MODEL, 2 * Q_WIDTH), D_MODEL),
        "w_q": w(ks[13], (nB, D_MODEL, Q_WIDTH), D_MODEL),
        "w_o": w(ks[14], (nB, N_HEADS * HEAD_DIM, D_MODEL), N_HEADS * HEAD_DIM),
    }


def _fwd_reference(x, positions, mix_norm_pre, mix_norm_post, ffn_norm_pre, ffn_norm_post,
              ffn_w_gate_up, ffn_w_down, conv_w_in, conv_w, conv_w_out,
              kv_norm, w_kv, w_q, w_o):
    h = x
    for layer in range(DEPTH):
        if layer == N_CONV_LAYERS:
            k_sh, v_sh = shared_kv(h, positions, kv_norm, w_kv)
        hn = rms_norm(h, mix_norm_pre[layer])
        if layer < N_CONV_LAYERS:
            y = short_conv_mixer(hn, conv_w_in[layer], conv_w[layer], conv_w_out[layer])
        else:
            j = layer - N_CONV_LAYERS
            y = dilated_attention_mixer(hn, positions, k_sh, v_sh, w_q[j], w_o[j])
        h = h + rms_norm(y, mix_norm_post[layer])
        f = swiglu(rms_norm(h, ffn_norm_pre[layer]), ffn_w_gate_up[layer], ffn_w_down[layer])
        h = h + rms_norm(f, ffn_norm_post[layer])
    return h


import jax as _jax
import jax.numpy as _jnp

TWIN_FORMAT = 'train_step'
FWD_PARAMS = ['x', 'positions', 'mix_norm_pre', 'mix_norm_post', 'ffn_norm_pre', 'ffn_norm_post', 'ffn_w_gate_up', 'ffn_w_down', 'conv_w_in', 'conv_w', 'conv_w_out', 'kv_norm', 'w_kv', 'w_q', 'w_o']
TWIN_WEIGHTS = ['mix_norm_pre', 'mix_norm_post', 'ffn_norm_pre', 'ffn_norm_post', 'ffn_w_gate_up', 'ffn_w_down', 'conv_w_in', 'conv_w', 'conv_w_out', 'kv_norm', 'w_kv', 'w_q', 'w_o']
TWIN_DIFF_INPUT = 'x'
TWIN_INPUTS = ['x', 'positions', 'mix_norm_pre', 'mix_norm_post', 'ffn_norm_pre', 'ffn_norm_post', 'ffn_w_gate_up', 'ffn_w_down', 'conv_w_in', 'conv_w', 'conv_w_out', 'kv_norm', 'w_kv', 'w_q', 'w_o', 'loss_target', 'm_mix_norm_pre', 'm_mix_norm_post', 'm_ffn_norm_pre', 'm_ffn_norm_post', 'm_ffn_w_gate_up', 'm_ffn_w_down', 'm_conv_w_in', 'm_conv_w', 'm_conv_w_out', 'm_kv_norm', 'm_w_kv', 'm_w_q', 'm_w_o', 'v_mix_norm_pre', 'v_mix_norm_post', 'v_ffn_norm_pre', 'v_ffn_norm_post', 'v_ffn_w_gate_up', 'v_ffn_w_down', 'v_conv_w_in', 'v_conv_w', 'v_conv_w_out', 'v_kv_norm', 'v_w_kv', 'v_w_q', 'v_w_o']
TWIN_OUTPUTS = ['loss', 'grad_x', 'grad_mix_norm_pre', 'grad_mix_norm_post', 'grad_ffn_norm_pre', 'grad_ffn_norm_post', 'grad_ffn_w_gate_up', 'grad_ffn_w_down', 'grad_conv_w_in', 'grad_conv_w', 'grad_conv_w_out', 'grad_kv_norm', 'grad_w_kv', 'grad_w_q', 'grad_w_o', 'delta_mix_norm_pre', 'delta_mix_norm_post', 'delta_ffn_norm_pre', 'delta_ffn_norm_post', 'delta_ffn_w_gate_up', 'delta_ffn_w_down', 'delta_conv_w_in', 'delta_conv_w', 'delta_conv_w_out', 'delta_kv_norm', 'delta_w_kv', 'delta_w_q', 'delta_w_o', 'new_m_mix_norm_pre', 'new_m_mix_norm_post', 'new_m_ffn_norm_pre', 'new_m_ffn_norm_post', 'new_m_ffn_w_gate_up', 'new_m_ffn_w_down', 'new_m_conv_w_in', 'new_m_conv_w', 'new_m_conv_w_out', 'new_m_kv_norm', 'new_m_w_kv', 'new_m_w_q', 'new_m_w_o', 'new_v_mix_norm_pre', 'new_v_mix_norm_post', 'new_v_ffn_norm_pre', 'new_v_ffn_norm_post', 'new_v_ffn_w_gate_up', 'new_v_ffn_w_down', 'new_v_conv_w_in', 'new_v_conv_w', 'new_v_conv_w_out', 'new_v_kv_norm', 'new_v_w_kv', 'new_v_w_q', 'new_v_w_o']
TWIN_LEAF_KINDS = {'loss': 'loss', 'grad_x': 'grad_x', 'grad_mix_norm_pre': 'grad_w', 'grad_mix_norm_post': 'grad_w', 'grad_ffn_norm_pre': 'grad_w', 'grad_ffn_norm_post': 'grad_w', 'grad_ffn_w_gate_up': 'grad_w', 'grad_ffn_w_down': 'grad_w', 'grad_conv_w_in': 'grad_w', 'grad_conv_w': 'grad_w', 'grad_conv_w_out': 'grad_w', 'grad_kv_norm': 'grad_w', 'grad_w_kv': 'grad_w', 'grad_w_q': 'grad_w', 'grad_w_o': 'grad_w', 'delta_mix_norm_pre': 'delta_w', 'delta_mix_norm_post': 'delta_w', 'delta_ffn_norm_pre': 'delta_w', 'delta_ffn_norm_post': 'delta_w', 'delta_ffn_w_gate_up': 'delta_w', 'delta_ffn_w_down': 'delta_w', 'delta_conv_w_in': 'delta_w', 'delta_conv_w': 'delta_w', 'delta_conv_w_out': 'delta_w', 'delta_kv_norm': 'delta_w', 'delta_w_kv': 'delta_w', 'delta_w_q': 'delta_w', 'delta_w_o': 'delta_w', 'new_m_mix_norm_pre': 'new_m', 'new_m_mix_norm_post': 'new_m', 'new_m_ffn_norm_pre': 'new_m', 'new_m_ffn_norm_post': 'new_m', 'new_m_ffn_w_gate_up': 'new_m', 'new_m_ffn_w_down': 'new_m', 'new_m_conv_w_in': 'new_m', 'new_m_conv_w': 'new_m', 'new_m_conv_w_out': 'new_m', 'new_m_kv_norm': 'new_m', 'new_m_w_kv': 'new_m', 'new_m_w_q': 'new_m', 'new_m_w_o': 'new_m', 'new_v_mix_norm_pre': 'new_v', 'new_v_mix_norm_post': 'new_v', 'new_v_ffn_norm_pre': 'new_v', 'new_v_ffn_norm_post': 'new_v', 'new_v_ffn_w_gate_up': 'new_v', 'new_v_ffn_w_down': 'new_v', 'new_v_conv_w_in': 'new_v', 'new_v_conv_w': 'new_v', 'new_v_conv_w_out': 'new_v', 'new_v_kv_norm': 'new_v', 'new_v_w_kv': 'new_v', 'new_v_w_q': 'new_v', 'new_v_w_o': 'new_v'}


def _forward(args):
    return _fwd_reference(*[args[k] for k in FWD_PARAMS])


def _output_shape():
    def fwd():
        inp = _fwd_setup_inputs(0)
        return _fwd_reference(*[inp[k] for k in FWD_PARAMS])
    out = _jax.eval_shape(fwd)
    return out.shape, out.dtype

N_MICROBATCH = 1
ADAM_LR = 0.001
ADAM_B1 = 0.9
ADAM_B2 = 0.999
ADAM_EPS = 1e-08
ADAM_WD = 0.01
ADAM_STEP = 10
PER_EXAMPLE_BATCH_AXIS = {'x': 0, 'positions': 0, 'loss_target': 0}
SHARED_INPUTS = []
_WEIGHT_DTYPES = {'mix_norm_pre': _jnp.float32, 'mix_norm_post': _jnp.float32, 'ffn_norm_pre': _jnp.float32, 'ffn_norm_post': _jnp.float32, 'ffn_w_gate_up': _jnp.float32, 'ffn_w_down': _jnp.float32, 'conv_w_in': _jnp.float32, 'conv_w': _jnp.float32, 'conv_w_out': _jnp.float32, 'kv_norm': _jnp.float32, 'w_kv': _jnp.float32, 'w_q': _jnp.float32, 'w_o': _jnp.float32}
MOMENT_SCALE = {'mix_norm_pre': 1.276483e+00, 'mix_norm_post': 3.184998e+01, 'ffn_norm_pre': 1.025715e+00, 'ffn_norm_post': 3.202064e+01, 'ffn_w_gate_up': 4.192972e-01, 'ffn_w_down': 7.822086e-01, 'conv_w_in': 9.999882e-01, 'conv_w': 1.013897e+00, 'conv_w_out': 1.095713e+00, 'kv_norm': 8.669345e-01, 'w_kv': 3.258784e-01, 'w_q': 3.124469e-01, 'w_o': 6.136607e-01}


def _to_microbatches(a, axis):
    t = _jnp.moveaxis(a, axis, 0)
    t = t.reshape((N_MICROBATCH, t.shape[0] // N_MICROBATCH) + t.shape[1:])
    return _jnp.moveaxis(t, 1, axis + 1)


def setup_inputs(seed: int = 0) -> dict:
    inp = _fwd_setup_inputs(seed)
    key = _jax.random.fold_in(_jax.random.key(seed), 7919)
    shape, _ = _output_shape()
    out = dict(inp)
    out["loss_target"] = _jax.random.normal(_jax.random.fold_in(key, 0), shape, _jnp.float32)
    for i, name in enumerate(TWIN_WEIGHTS):
        w = inp[name].astype(_jnp.float32)
        if MOMENT_SCALE is None:
            s = _jnp.sqrt(_jnp.mean(_jnp.square(w)) + 1e-30)
        else:
            s = MOMENT_SCALE[name]
        km, kv = _jax.random.split(_jax.random.fold_in(key, i + 1))
        out[name] = w
        out["m_" + name] = s * _jax.random.normal(km, w.shape, _jnp.float32)
        out["v_" + name] = (s * s) * _jax.random.uniform(kv, w.shape, _jnp.float32, 0.5, 1.5)
    if N_MICROBATCH > 1:
        for name, axis in PER_EXAMPLE_BATCH_AXIS.items():
            out[name] = _to_microbatches(out[name], axis)
    return {'x': out['x'], 'positions': out['positions'], 'mix_norm_pre': out['mix_norm_pre'], 'mix_norm_post': out['mix_norm_post'], 'ffn_norm_pre': out['ffn_norm_pre'], 'ffn_norm_post': out['ffn_norm_post'], 'ffn_w_gate_up': out['ffn_w_gate_up'], 'ffn_w_down': out['ffn_w_down'], 'conv_w_in': out['conv_w_in'], 'conv_w': out['conv_w'], 'conv_w_out': out['conv_w_out'], 'kv_norm': out['kv_norm'], 'w_kv': out['w_kv'], 'w_q': out['w_q'], 'w_o': out['w_o'], 'loss_target': out['loss_target'], 'm_mix_norm_pre': out['m_mix_norm_pre'], 'm_mix_norm_post': out['m_mix_norm_post'], 'm_ffn_norm_pre': out['m_ffn_norm_pre'], 'm_ffn_norm_post': out['m_ffn_norm_post'], 'm_ffn_w_gate_up': out['m_ffn_w_gate_up'], 'm_ffn_w_down': out['m_ffn_w_down'], 'm_conv_w_in': out['m_conv_w_in'], 'm_conv_w': out['m_conv_w'], 'm_conv_w_out': out['m_conv_w_out'], 'm_kv_norm': out['m_kv_norm'], 'm_w_kv': out['m_w_kv'], 'm_w_q': out['m_w_q'], 'm_w_o': out['m_w_o'], 'v_mix_norm_pre': out['v_mix_norm_pre'], 'v_mix_norm_post': out['v_mix_norm_post'], 'v_ffn_norm_pre': out['v_ffn_norm_pre'], 'v_ffn_norm_post': out['v_ffn_norm_post'], 'v_ffn_w_gate_up': out['v_ffn_w_gate_up'], 'v_ffn_w_down': out['v_ffn_w_down'], 'v_conv_w_in': out['v_conv_w_in'], 'v_conv_w': out['v_conv_w'], 'v_conv_w_out': out['v_conv_w_out'], 'v_kv_norm': out['v_kv_norm'], 'v_w_kv': out['v_w_kv'], 'v_w_q': out['v_w_q'], 'v_w_o': out['v_w_o']}


def _loss(weights, diff, rest, loss_target):
    with _jax.named_scope("forward"):
        args = {**rest, TWIN_DIFF_INPUT: diff, **{k: w.astype(_WEIGHT_DTYPES[k]) for k, w in weights.items()}}
        y = _forward(args)
    with _jax.named_scope("loss_head"):
        err = _jnp.square(y.astype(_jnp.float32) - loss_target)
        return 0.5 * _jnp.sum(_jnp.mean(err, axis=-1)) if err.ndim else 0.5 * err


def _adamw(w, g, m, v):
    m = ADAM_B1 * m + (1.0 - ADAM_B1) * g
    v = ADAM_B2 * v + (1.0 - ADAM_B2) * _jnp.square(g)
    m_hat = m / (1.0 - ADAM_B1 ** ADAM_STEP)
    v_hat = v / (1.0 - ADAM_B2 ** ADAM_STEP)
    delta = -ADAM_LR * (m_hat / (_jnp.sqrt(v_hat) + ADAM_EPS) + ADAM_WD * w)
    return delta, m, v


def reference(x, positions, mix_norm_pre, mix_norm_post, ffn_norm_pre, ffn_norm_post, ffn_w_gate_up, ffn_w_down, conv_w_in, conv_w, conv_w_out, kv_norm, w_kv, w_q, w_o, loss_target, m_mix_norm_pre, m_mix_norm_post, m_ffn_norm_pre, m_ffn_norm_post, m_ffn_w_gate_up, m_ffn_w_down, m_conv_w_in, m_conv_w, m_conv_w_out, m_kv_norm, m_w_kv, m_w_q, m_w_o, v_mix_norm_pre, v_mix_norm_post, v_ffn_norm_pre, v_ffn_norm_post, v_ffn_w_gate_up, v_ffn_w_down, v_conv_w_in, v_conv_w, v_conv_w_out, v_kv_norm, v_w_kv, v_w_q, v_w_o):
    given = dict(x=x, positions=positions, mix_norm_pre=mix_norm_pre, mix_norm_post=mix_norm_post, ffn_norm_pre=ffn_norm_pre, ffn_norm_post=ffn_norm_post, ffn_w_gate_up=ffn_w_gate_up, ffn_w_down=ffn_w_down, conv_w_in=conv_w_in, conv_w=conv_w, conv_w_out=conv_w_out, kv_norm=kv_norm, w_kv=w_kv, w_q=w_q, w_o=w_o, loss_target=loss_target, m_mix_norm_pre=m_mix_norm_pre, m_mix_norm_post=m_mix_norm_post, m_ffn_norm_pre=m_ffn_norm_pre, m_ffn_norm_post=m_ffn_norm_post, m_ffn_w_gate_up=m_ffn_w_gate_up, m_ffn_w_down=m_ffn_w_down, m_conv_w_in=m_conv_w_in, m_conv_w=m_conv_w, m_conv_w_out=m_conv_w_out, m_kv_norm=m_kv_norm, m_w_kv=m_w_kv, m_w_q=m_w_q, m_w_o=m_w_o, v_mix_norm_pre=v_mix_norm_pre, v_mix_norm_post=v_mix_norm_post, v_ffn_norm_pre=v_ffn_norm_pre, v_ffn_norm_post=v_ffn_norm_post, v_ffn_w_gate_up=v_ffn_w_gate_up, v_ffn_w_down=v_ffn_w_down, v_conv_w_in=v_conv_w_in, v_conv_w=v_conv_w, v_conv_w_out=v_conv_w_out, v_kv_norm=v_kv_norm, v_w_kv=v_w_kv, v_w_q=v_w_q, v_w_o=v_w_o)
    weights = {n: given[n] for n in TWIN_WEIGHTS}
    shared = {n: given[n] for n in SHARED_INPUTS}
    per_example = {n: given[n] for n in ['x', 'positions']}
    grad_fn = _jax.value_and_grad(_loss, argnums=(0, 1))

    def one_microbatch(ex, loss_target):
        ex = dict(ex)
        diff = ex.pop(TWIN_DIFF_INPUT)
        return grad_fn(weights, diff, {**shared, **ex}, loss_target)

    if N_MICROBATCH == 1:
        loss, (grad_w, grad_x) = one_microbatch(per_example, given["loss_target"])
    else:
        def body(carry, xs):
            loss_sum, grad_sum = carry
            l_k, (gw_k, gx_k) = one_microbatch(xs[0], xs[1])
            with _jax.named_scope("update"):
                return (loss_sum + l_k, _jax.tree.map(_jnp.add, grad_sum, gw_k)), gx_k

        init = (_jnp.zeros((), _jnp.float32), _jax.tree.map(_jnp.zeros_like, weights))
        (loss, grad_w), grad_x = _jax.lax.scan(body, init, (per_example, given["loss_target"]))
    with _jax.named_scope("update"):
        delta_w, new_m, new_v = {}, {}, {}
        for n in TWIN_WEIGHTS:
            delta_w[n], new_m[n], new_v[n] = _adamw(weights[n], grad_w[n], given["m_" + n], given["v_" + n])
    return (loss, grad_x, *[grad_w[n] for n in TWIN_WEIGHTS], *[delta_w[n] for n in TWIN_WEIGHTS],
            *[new_m[n] for n in TWIN_WEIGHTS], *[new_v[n] for n in TWIN_WEIGHTS])
```

```python
import jax
import jax.numpy as jnp
from jax import lax
from jax.experimental import pallas as pl
from jax.experimental.pallas import tpu as pltpu

F32 = jnp.float32
BF16 = jnp.bfloat16

S = 4096
D = 1024
NDEV = 8
HEAD_DIM = 64
QW = 3072
DFF = 2816
FB = 704
NFB = 4
BRANCHES = ((128, 1), (512, 4), (2048, 16))
BAND = 128
ROPE_THETA = 10000.0
RMS_EPS = 1e-6
NEG_INF = -1e30
ADAM_LR, ADAM_B1, ADAM_B2, ADAM_EPS, ADAM_WD, ADAM_STEP = 0.001, 0.9, 0.999, 1e-08, 0.01, 10

VMEM_LIMIT_BYTES = 52 * 1024 * 1024
ROW_TILE = 512
MESH = pl.DeviceIdType.MESH


def _cparams(ngrid):
    return pltpu.CompilerParams(dimension_semantics=("arbitrary",) * ngrid,
                                vmem_limit_bytes=VMEM_LIMIT_BYTES)


def _sds(shape, dtype):
    return jax.ShapeDtypeStruct(tuple(shape), dtype)


_DIMS = {"nn": (((1,), (0,)), ((), ())),
         "nt": (((1,), (1,)), ((), ())),
         "tn": (((0,), (0,)), ((), ()))}


def _matmul(name, a, b, *, mode, grid, a_blk, a_map, b_blk, b_map, o_shape, o_blk, o_map, out_dtype):
    nk = grid[2]
    dims = _DIMS[mode]
    acc_shape = tuple(s for s in o_blk if s is not None)

    def body(a_ref, b_ref, o_ref, *scratch):
        part = lax.dot_general(a_ref[...], b_ref[...], dims, preferred_element_type=F32)
        if nk == 1:
            o_ref[...] = part.astype(o_ref.dtype)
            return
        acc_ref = scratch[0]
        k = pl.program_id(2)

        @pl.when(k == 0)
        def _():
            acc_ref[...] = part

        @pl.when(k > 0)
        def _():
            acc_ref[...] += part

        @pl.when(k == nk - 1)
        def _():
            o_ref[...] = acc_ref[...].astype(o_ref.dtype)

    return pl.pallas_call(
        body, name=name, grid=grid,
        in_specs=[pl.BlockSpec(a_blk, a_map), pl.BlockSpec(b_blk, b_map)],
        out_specs=pl.BlockSpec(o_blk, o_map),
        out_shape=_sds(o_shape, out_dtype),
        scratch_shapes=[] if nk == 1 else [pltpu.VMEM(acc_shape, F32)],
        compiler_params=_cparams(3),
    )(a, b)


TM = 1024
TK = 1024


def _fwd_cols(name, a, wg, out_dtype=BF16):
    _, kdim, n = wg.shape
    return _matmul(name, a, wg, mode="nn", grid=(S // TM, NDEV, 1),
                   a_blk=(TM, kdim), a_map=lambda i, j, k: (i, 0),
                   b_blk=(None, kdim, n), b_map=lambda i, j, k: (j, 0, 0),
                   o_shape=(S, NDEV * n), o_blk=(TM, n), o_map=lambda i, j, k: (i, j), out_dtype=out_dtype)


def _fwd_cols_blocked(name, a, wg):
    _, kdim, n = wg.shape
    return _matmul(name, a, wg, mode="nn", grid=(S // TM, NDEV, 1),
                   a_blk=(TM, kdim), a_map=lambda i, j, k: (i, 0),
                   b_blk=(None, kdim, n), b_map=lambda i, j, k: (j, 0, 0),
                   o_shape=(NDEV, S, n), o_blk=(None, TM, n), o_map=lambda i, j, k: (j, i, 0), out_dtype=BF16)


def _fwd_rows(name, a, w, out_dtype=F32):
    kdim, n = w.shape
    tn = 512
    return _matmul(name, a, w, mode="nn", grid=(S // TM, n // tn, 1),
                   a_blk=(TM, kdim), a_map=lambda i, j, k: (i, 0),
                   b_blk=(kdim, tn), b_map=lambda i, j, k: (0, j),
                   o_shape=(S, n), o_blk=(TM, tn), o_map=lambda i, j, k: (i, j), out_dtype=out_dtype)


def _fwd_kblocked(name, a4, w4):
    nb, _, kb = a4.shape
    n = w4.shape[2]
    return _matmul(name, a4, w4, mode="nn", grid=(S // TM, 1, nb),
                   a_blk=(None, TM, kb), a_map=lambda i, j, k: (k, i, 0),
                   b_blk=(None, kb, n), b_map=lambda i, j, k: (k, 0, 0),
                   o_shape=(S, n), o_blk=(TM, n), o_map=lambda i, j, k: (i, 0), out_dtype=F32)


def _bwd_x_cols(name, dy, wg):
    _, kdim, n = wg.shape
    return _matmul(name, dy, wg, mode="nt", grid=(S // TM, 1, NDEV),
                   a_blk=(TM, n), a_map=lambda i, j, k: (i, k),
                   b_blk=(None, kdim, n), b_map=lambda i, j, k: (k, 0, 0),
                   o_shape=(S, kdim), o_blk=(TM, kdim), o_map=lambda i, j, k: (i, 0), out_dtype=F32)


def _bwd_x_cols_blocked(name, dy8, wg):
    _, kdim, n = wg.shape
    return _matmul(name, dy8, wg, mode="nt", grid=(S // TM, 1, NDEV),
                   a_blk=(None, TM, n), a_map=lambda i, j, k: (k, i, 0),
                   b_blk=(None, kdim, n), b_map=lambda i, j, k: (k, 0, 0),
                   o_shape=(S, kdim), o_blk=(TM, kdim), o_map=lambda i, j, k: (i, 0), out_dtype=F32)


def _bwd_x_rows(name, dy, w, out_dtype):
    kdim, n = w.shape
    tkk = 512
    return _matmul(name, dy, w, mode="nt", grid=(S // TM, kdim // tkk, 1),
                   a_blk=(TM, n), a_map=lambda i, j, k: (i, 0),
                   b_blk=(tkk, n), b_map=lambda i, j, k: (j, 0),
                   o_shape=(S, kdim), o_blk=(TM, tkk), o_map=lambda i, j, k: (i, j), out_dtype=out_dtype)


def _bwd_x_kblocked(name, dy, w4):
    nb, kb, n = w4.shape
    return _matmul(name, dy, w4, mode="nt", grid=(S // TM, nb, 1),
                   a_blk=(TM, n), a_map=lambda i, j, k: (i, 0),
                   b_blk=(None, kb, n), b_map=lambda i, j, k: (j, 0, 0),
                   o_shape=(nb, S, kb), o_blk=(None, TM, kb), o_map=lambda i, j, k: (j, i, 0), out_dtype=BF16)


def _bwd_w_cols(name, a, dy, n):
    kdim = a.shape[1]
    return _matmul(name, a, dy, mode="tn", grid=(1, NDEV, S // TK),
                   a_blk=(TK, kdim), a_map=lambda i, j, k: (k, 0),
                   b_blk=(TK, n), b_map=lambda i, j, k: (k, j),
                   o_shape=(NDEV, kdim, n), o_blk=(None, kdim, n), o_map=lambda i, j, k: (j, 0, 0), out_dtype=BF16)


def _bwd_w_cols_blocked(name, a, dy8):
    kdim = a.shape[1]
    n = dy8.shape[2]
    return _matmul(name, a, dy8, mode="tn", grid=(1, NDEV, S // TK),
                   a_blk=(TK, kdim), a_map=lambda i, j, k: (k, 0),
                   b_blk=(None, TK, n), b_map=lambda i, j, k: (j, k, 0),
                   o_shape=(NDEV, kdim, n), o_blk=(None, kdim, n), o_map=lambda i, j, k: (j, 0, 0), out_dtype=BF16)


def _bwd_w_rows(name, a, dy):
    kdim = a.shape[1]
    n = dy.shape[1]
    tmm = 512
    return _matmul(name, a, dy, mode="tn", grid=(kdim // tmm, 1, S // TK),
                   a_blk=(TK, tmm), a_map=lambda i, j, k: (k, i),
                   b_blk=(TK, n), b_map=lambda i, j, k: (k, 0),
                   o_shape=(kdim, n), o_blk=(tmm, n), o_map=lambda i, j, k: (i, 0), out_dtype=BF16)


def _bwd_w_kblocked(name, a4, dy):
    nb, _, kb = a4.shape
    n = dy.shape[1]
    return _matmul(name, a4, dy, mode="tn", grid=(nb, 1, S // TK),
                   a_blk=(None, TK, kb), a_map=lambda i, j, k: (i, k, 0),
                   b_blk=(TK, n), b_map=lambda i, j, k: (k, 0),
                   o_shape=(nb, kb, n), o_blk=(None, kb, n), o_map=lambda i, j, k: (i, 0, 0), out_dtype=BF16)


def _rstd(x):
    return lax.rsqrt(jnp.mean(x * x, axis=-1, keepdims=True) + RMS_EPS)


def _row_spec(tm=ROW_TILE, width=D):
    return pl.BlockSpec((tm, width), lambda i: (i, 0))


def _vec_spec(rows=1, width=D):
    return pl.BlockSpec((rows, width), lambda i: (0, 0))


def _rms_fwd(name, x, gains):
    n = len(gains)

    def body(x_ref, *refs):
        x_val = x_ref[...]
        xh = x_val * _rstd(x_val)
        for g_ref, o_ref in zip(refs[:n], refs[n:]):
            o_ref[...] = (xh * g_ref[...]).astype(o_ref.dtype)

    outs = pl.pallas_call(
        body, name=name, grid=(S // ROW_TILE,),
        in_specs=[_row_spec()] + [_vec_spec()] * n,
        out_specs=[_row_spec()] * n,
        out_shape=[_sds((S, D), BF16)] * n,
        compiler_params=_cparams(1),
    )(x, *gains)
    return list(outs)


def _resid_rms(name, h, y, g):
    def body(h_ref, y_ref, g_ref, o_ref):
        y_val = y_ref[...]
        o_ref[...] = h_ref[...] + (y_val * _rstd(y_val)) * g_ref[...]

    return pl.pallas_call(
        body, name=name, grid=(S // ROW_TILE,),
        in_specs=[_row_spec(), _row_spec(), _vec_spec()],
        out_specs=_row_spec(), out_shape=_sds((S, D), F32),
        compiler_params=_cparams(1),
    )(h, y, g)


def _rms_bwd(name, x, pairs, dres, out_dtype):
    n = len(pairs)
    has_res = dres is not None

    def body(x_ref, *refs):
        g_refs = refs[0:2 * n:2]
        dn_refs = refs[1:2 * n:2]
        pos = 2 * n
        res_ref = refs[pos] if has_res else None
        pos += int(has_res)
        dx_ref = refs[pos]
        dg_refs = refs[pos + 1:]
        step = pl.program_id(0)
        x_val = x_ref[...]
        r = _rstd(x_val)
        xh = x_val * r
        acc = res_ref[...] if has_res else jnp.zeros_like(x_val)
        for g_ref, dn_ref, dg_ref in zip(g_refs, dn_refs, dg_refs):
            dn = dn_ref[...].astype(F32)
            dxh = dn * g_ref[...]
            acc = acc + r * (dxh - xh * jnp.mean(dxh * xh, axis=-1, keepdims=True))
            part = jnp.sum(dn * xh, axis=0, keepdims=True)

            @pl.when(step == 0)
            def _():
                dg_ref[...] = part

            @pl.when(step > 0)
            def _():
                dg_ref[...] += part

        dx_ref[...] = acc.astype(dx_ref.dtype)

    operands = [x]
    in_specs = [_row_spec()]
    for g, dn in pairs:
        operands += [g, dn]
        in_specs += [_vec_spec(), _row_spec()]
    if has_res:
        operands.append(dres)
        in_specs.append(_row_spec())
    outs = pl.pallas_call(
        body, name=name, grid=(S // ROW_TILE,),
        in_specs=in_specs,
        out_specs=[_row_spec()] + [_vec_spec()] * n,
        out_shape=[_sds((S, D), out_dtype)] + [_sds((1, D), F32)] * n,
        compiler_params=_cparams(1),
    )(*operands)
    return outs[0], list(outs[1:])


def _loss_grad(name, h, target):
    def body(h_ref, t_ref, dh_ref, part_ref):
        e = h_ref[...] - t_ref[...]
        dh_ref[...] = e * (1.0 / D)
        part = jnp.sum(e * e, axis=0, keepdims=True)
        step = pl.program_id(0)

        @pl.when(step == 0)
        def _():
            part_ref[...] = part

        @pl.when(step > 0)
        def _():
            part_ref[...] += part

    return pl.pallas_call(
        body, name=name, grid=(S // ROW_TILE,),
        in_specs=[_row_spec(), _row_spec()],
        out_specs=[_row_spec(), _vec_spec()],
        out_shape=[_sds((S, D), F32), _sds((1, D), F32)],
        compiler_params=_cparams(1),
    )(h, target)


def _shift_down(u, prev8, k):
    r = pltpu.roll(u, k, 0)
    p = pltpu.roll(prev8, k, 0)
    row = lax.broadcasted_iota(jnp.int32, prev8.shape, 0)
    top = jnp.where(row < k, p, r[0:8])
    return jnp.concatenate([top, r[8:]], axis=0)


def _shift_up(u, next8, k):
    tm = u.shape[0]
    r = pltpu.roll(u, tm - k, 0)
    p = pltpu.roll(next8, 8 - k, 0)
    row = lax.broadcasted_iota(jnp.int32, next8.shape, 0)
    bot = jnp.where(row >= 8 - k, p, r[tm - 8:tm])
    return jnp.concatenate([r[:tm - 8], bot], axis=0)


CONV_TILE = 512


def _halo_prev(col):
    return pl.BlockSpec((8, D), lambda i: (jnp.maximum(i * (CONV_TILE // 8) - 1, 0), col))


def _halo_next(col):
    last = S // 8 - 1
    return pl.BlockSpec((8, D), lambda i: (jnp.minimum((i + 1) * (CONV_TILE // 8), last), col))


def _conv_fwd(name, z, cw):
    def body(b_ref, c_ref, h_ref, cp_ref, hp_ref, cw_ref, o_ref):
        i = pl.program_id(0)
        u = c_ref[...].astype(F32) * h_ref[...].astype(F32)
        up = cp_ref[...].astype(F32) * hp_ref[...].astype(F32)
        up = jnp.where(i > 0, up, 0.0)
        cv = cw_ref[0:1, :] * _shift_down(u, up, 2) + cw_ref[1:2, :] * _shift_down(u, up, 1) + cw_ref[2:3, :] * u
        o_ref[...] = (b_ref[...].astype(F32) * cv).astype(o_ref.dtype)

    col = lambda c: pl.BlockSpec((CONV_TILE, D), lambda i: (i, c))
    return pl.pallas_call(
        body, name=name, grid=(S // CONV_TILE,),
        in_specs=[col(0), col(1), col(2), _halo_prev(1), _halo_prev(2), _vec_spec(8)],
        out_specs=_row_spec(CONV_TILE), out_shape=_sds((S, D), BF16),
        compiler_params=_cparams(1),
    )(z, z, z, z, z, cw)


def _conv_bwd(name, z, dpre, cw):
    nsteps = S // CONV_TILE

    def body(b_ref, c_ref, h_ref, cp_ref, hp_ref, dp_ref, dpn_ref, bn_ref, cw_ref, dz_ref, dcw_ref):
        i = pl.program_id(0)
        b = b_ref[...].astype(F32)
        c = c_ref[...].astype(F32)
        h = h_ref[...].astype(F32)
        dp = dp_ref[...].astype(F32)
        u = c * h
        up = jnp.where(i > 0, cp_ref[...].astype(F32) * hp_ref[...].astype(F32), 0.0)
        s1 = _shift_down(u, up, 1)
        s2 = _shift_down(u, up, 2)
        w0, w1, w2 = cw_ref[0:1, :], cw_ref[1:2, :], cw_ref[2:3, :]
        cv = w0 * s2 + w1 * s1 + w2 * u
        dcv = dp * b
        dcvn = jnp.where(i < nsteps - 1, dpn_ref[...].astype(F32) * bn_ref[...].astype(F32), 0.0)
        du = w2 * dcv + w1 * _shift_up(dcv, dcvn, 1) + w0 * _shift_up(dcv, dcvn, 2)
        dz_ref[:, 0:D] = (dp * cv).astype(dz_ref.dtype)
        dz_ref[:, D:2 * D] = (du * h).astype(dz_ref.dtype)
        dz_ref[:, 2 * D:3 * D] = (du * c).astype(dz_ref.dtype)

        @pl.when(i == 0)
        def _():
            dcw_ref[...] = jnp.zeros_like(dcw_ref)

        dcw_ref[0:1, :] += jnp.sum(dcv * s2, axis=0, keepdims=True)
        dcw_ref[1:2, :] += jnp.sum(dcv * s1, axis=0, keepdims=True)
        dcw_ref[2:3, :] += jnp.sum(dcv * u, axis=0, keepdims=True)

    col = lambda c: pl.BlockSpec((CONV_TILE, D), lambda i: (i, c))
    return pl.pallas_call(
        body, name=name, grid=(nsteps,),
        in_specs=[col(0), col(1), col(2), _halo_prev(1), _halo_prev(2),
                  _row_spec(CONV_TILE), _halo_next(0), _halo_next(0), _vec_spec(8)],
        out_specs=[pl.BlockSpec((CONV_TILE, 3 * D), lambda i: (i, 0)), _vec_spec(8)],
        out_shape=[_sds((S, 3 * D), BF16), _sds((8, D), F32)],
        compiler_params=_cparams(1),
    )(z, z, z, z, z, dpre, dpre, z, cw)


def _swiglu_fwd(name, gu):
    def body(gu_ref, o_ref):
        g = gu_ref[0].astype(F32)
        u = gu_ref[1].astype(F32)
        o_ref[...] = (g * jax.nn.sigmoid(g) * u).astype(o_ref.dtype)

    return pl.pallas_call(
        body, name=name, grid=(NFB, S // ROW_TILE),
        in_specs=[pl.BlockSpec((2, None, ROW_TILE, FB), lambda j, i: (0, j, i, 0))],
        out_specs=pl.BlockSpec((None, ROW_TILE, FB), lambda j, i: (j, i, 0)),
        out_shape=_sds((NFB, S, FB), BF16),
        compiler_params=_cparams(2),
    )(gu)


def _swiglu_bwd(name, gu, da):
    def body(gu_ref, da_ref, o_ref):
        g = gu_ref[0].astype(F32)
        u = gu_ref[1].astype(F32)
        d = da_ref[...].astype(F32)
        sg = jax.nn.sigmoid(g)
        o_ref[0] = (d * u * sg * (1.0 + g * (1.0 - sg))).astype(o_ref.dtype)
        o_ref[1] = (d * g * sg).astype(o_ref.dtype)

    blk = pl.BlockSpec((2, None, ROW_TILE, FB), lambda j, i: (0, j, i, 0))
    return pl.pallas_call(
        body, name=name, grid=(NFB, S // ROW_TILE),
        in_specs=[blk, pl.BlockSpec((None, ROW_TILE, FB), lambda j, i: (j, i, 0))],
        out_specs=blk, out_shape=_sds((2, NFB, S, FB), BF16),
        compiler_params=_cparams(2),
    )(gu, da)


def _rope_tables(name, pos_col, inv_freq_row):
    def body(pos_ref, f_ref, cos_ref, sin_ref):
        ang = pos_ref[...].astype(F32) * f_ref[...]
        lane = lax.broadcasted_iota(jnp.int32, ang.shape, 1)
        s = jnp.sin(ang)
        cos_ref[...] = jnp.cos(ang)
        sin_ref[...] = jnp.where((lane % HEAD_DIM) < HEAD_DIM // 2, -s, s)

    tab = pl.BlockSpec((ROW_TILE, 128), lambda i: (i, 0))
    return pl.pallas_call(
        body, name=name, grid=(S // ROW_TILE,),
        in_specs=[pl.BlockSpec((ROW_TILE, 1), lambda i: (i, 0)), _vec_spec(1, 128)],
        out_specs=[tab, tab], out_shape=[_sds((S, 128), F32)] * 2,
        compiler_params=_cparams(1),
    )(pos_col, inv_freq_row)


def _swap_halves(t):
    lane = lax.broadcasted_iota(jnp.int32, t.shape, 1)
    first = (lane % HEAD_DIM) < HEAD_DIM // 2
    return jnp.where(first, pltpu.roll(t, 128 - HEAD_DIM // 2, 1), pltpu.roll(t, HEAD_DIM // 2, 1))


ROPE_COLS = 768


def _rope_fwd(name, t, col_off, cos_t, sin_t, scale):
    def body(t_ref, cos_ref, sin_ref, o_ref):
        cs = cos_ref[...]
        sn = sin_ref[...]
        for j in range(ROPE_COLS // 128):
            x = t_ref[:, j * 128:(j + 1) * 128].astype(F32)
            o_ref[:, j * 128:(j + 1) * 128] = ((x * cs + _swap_halves(x) * sn) * scale).astype(o_ref.dtype)

    off = col_off // ROPE_COLS
    tab = pl.BlockSpec((ROW_TILE, 128), lambda i, j: (i, 0))
    return pl.pallas_call(
        body, name=name, grid=(S // ROW_TILE, QW // ROPE_COLS),
        in_specs=[pl.BlockSpec((ROW_TILE, ROPE_COLS), lambda i, j: (i, j + off)), tab, tab],
        out_specs=pl.BlockSpec((ROW_TILE, ROPE_COLS), lambda i, j: (i, j)),
        out_shape=_sds((S, QW), BF16),
        compiler_params=_cparams(2),
    )(t, cos_t, sin_t)


def _attn_tile(d):
    return min(512, S // d)


def _attn_specs(d):
    tq = _attn_tile(d)
    main = pl.BlockSpec((tq, 128), lambda cb, n: (n, cb))
    prev = pl.BlockSpec((BAND, 128), lambda cb, n: (jnp.maximum(n * (tq // BAND) - 1, 0), cb))
    return tq, main, prev


def _dot_nt(a, b):
    return lax.dot_general(a, b, _DIMS["nt"], preferred_element_type=F32)


def _dot_tn(a, b):
    return lax.dot_general(a, b, _DIMS["tn"], preferred_element_type=F32)


def _dot_nn(a, b):
    return lax.dot_general(a, b, _DIMS["nn"], preferred_element_type=F32)


def _band_masks():
    qi = lax.broadcasted_iota(jnp.int32, (BAND, BAND), 0)
    kj = lax.broadcasted_iota(jnp.int32, (BAND, BAND), 1)
    return kj >= qi, kj <= qi, kj < HEAD_DIM


def _attn_fwd(name, q, k, v, d):
    tq, main, prev = _attn_specs(d)
    nsb = tq // BAND

    def body(q_ref, k_ref, kp_ref, v_ref, vp_ref, o_ref, lse_ref):
        n = pl.program_id(1)
        m_prev, m_cur, first_head = _band_masks()
        for sb in range(nsb):
            rows = slice(sb * BAND, (sb + 1) * BAND)
            qb = q_ref[rows, :]
            kc = k_ref[rows, :]
            vc = v_ref[rows, :]
            if sb == 0:
                kp, vp = kp_ref[...], vp_ref[...]
                ok_prev = m_prev & (n > 0)
            else:
                before = slice((sb - 1) * BAND, sb * BAND)
                kp, vp = k_ref[before, :], v_ref[before, :]
                ok_prev = m_prev
            o_heads, lse_heads = [], []
            for head_lanes in (first_head, ~first_head):
                qh = jnp.where(head_lanes, qb, jnp.zeros_like(qb))
                sp = jnp.where(ok_prev, _dot_nt(qh, kp), NEG_INF)
                sc = jnp.where(m_cur, _dot_nt(qh, kc), NEG_INF)
                m = jnp.maximum(jnp.max(sp, axis=-1, keepdims=True), jnp.max(sc, axis=-1, keepdims=True))
                pp = jnp.exp(sp - m)
                pc = jnp.exp(sc - m)
                l = jnp.sum(pp, axis=-1, keepdims=True) + jnp.sum(pc, axis=-1, keepdims=True)
                o_heads.append((_dot_nn(pp.astype(BF16), vp) + _dot_nn(pc.astype(BF16), vc)) / l)
                lse_heads.append(jnp.broadcast_to(m + jnp.log(l), (BAND, 128)))
            o_ref[rows, :] = jnp.where(first_head, o_heads[0], o_heads[1])
            lse_ref[rows, :] = jnp.where(first_head, lse_heads[0], lse_heads[1])

    rdim = S // d
    return pl.pallas_call(
        body, name=name, grid=(8 * d, rdim // tq),
        in_specs=[main, main, prev, main, prev],
        out_specs=[main, main],
        out_shape=[_sds((rdim, d * D), F32)] * 2,
        compiler_params=_cparams(2),
    )(q, k, k, v, v)


def _attn_bwd(name, q, k, v, do, lse, dd, d):
    tq, main, prev = _attn_specs(d)
    nsb = tq // BAND

    def body(q_ref, k_ref, kp_ref, v_ref, vp_ref, do_ref, lse_ref, dd_ref,
             dq_ref, dkc_ref, dkp_ref, dvc_ref, dvp_ref):
        n = pl.program_id(1)
        m_prev, m_cur, first_head = _band_masks()
        for sb in range(nsb):
            rows = slice(sb * BAND, (sb + 1) * BAND)
            qb = q_ref[rows, :]
            dob = do_ref[rows, :]
            kc = k_ref[rows, :]
            vc = v_ref[rows, :]
            if sb == 0:
                kp, vp = kp_ref[...], vp_ref[...]
                ok_prev = m_prev & (n > 0)
            else:
                before = slice((sb - 1) * BAND, sb * BAND)
                kp, vp = k_ref[before, :], v_ref[before, :]
                ok_prev = m_prev
            dq_heads = []
            dkc = dkp = dvc = dvp = None
            for hi, head_lanes in enumerate((first_head, ~first_head)):
                col = slice(hi * HEAD_DIM, hi * HEAD_DIM + 1)
                lse_col = lse_ref[rows, col]
                dd_col = dd_ref[rows, col]
                qh = jnp.where(head_lanes, qb, jnp.zeros_like(qb))
                doh = jnp.where(head_lanes, dob, jnp.zeros_like(dob))
                sp = jnp.where(ok_prev, _dot_nt(qh, kp), NEG_INF)
                sc = jnp.where(m_cur, _dot_nt(qh, kc), NEG_INF)
                pp = jnp.exp(sp - lse_col)
                pc = jnp.exp(sc - lse_col)
                dsp = (pp * (_dot_nt(doh, vp) - dd_col)).astype(BF16)
                dsc = (pc * (_dot_nt(doh, vc) - dd_col)).astype(BF16)
                dq_heads.append(_dot_nn(dsp, kp) + _dot_nn(dsc, kc))
                parts = (_dot_tn(dsc, qh), _dot_tn(dsp, qh),
                         _dot_tn(pc.astype(BF16), doh), _dot_tn(pp.astype(BF16), doh))
                if hi == 0:
                    dkc, dkp, dvc, dvp = parts
                else:
                    dkc, dkp, dvc, dvp = dkc + parts[0], dkp + parts[1], dvc + parts[2], dvp + parts[3]
            dq_ref[rows, :] = jnp.where(first_head, dq_heads[0], dq_heads[1])
            dkc_ref[rows, :] = dkc
            dkp_ref[rows, :] = dkp
            dvc_ref[rows, :] = dvc
            dvp_ref[rows, :] = dvp

    rdim = S // d
    return pl.pallas_call(
        body, name=name, grid=(8 * d, rdim // tq),
        in_specs=[main, main, prev, main, prev, main, main, main],
        out_specs=[main] * 5,
        out_shape=[_sds((rdim, d * D), F32)] * 5,
        compiler_params=_cparams(2),
    )(q, k, k, v, v, do, lse, dd)


def _mix_fwd(name, outs, lses):
    def body(o0, o1, o2, l0, l1, l2, o_ref):
        a, b, c = l0[...], l1[...], l2[...]
        m = jnp.maximum(jnp.maximum(a, b), c)
        ea, eb, ec = jnp.exp(a - m), jnp.exp(b - m), jnp.exp(c - m)
        o_ref[...] = ((ea * o0[...] + eb * o1[...] + ec * o2[...]) / (ea + eb + ec)).astype(o_ref.dtype)

    return pl.pallas_call(
        body, name=name, grid=(S // ROW_TILE,),
        in_specs=[_row_spec()] * 6, out_specs=_row_spec(), out_shape=_sds((S, D), BF16),
        compiler_params=_cparams(1),
    )(*outs, *lses)


def _head_sums(x, ones_blockdiag):
    cols = []
    for j in range(D // 128):
        xj = x[:, j * 128:(j + 1) * 128]
        hi = xj.astype(BF16)
        r1 = xj - hi.astype(F32)
        mid = r1.astype(BF16)
        lo = (r1 - mid.astype(F32)).astype(BF16)
        cols.append(_dot_nn(hi, ones_blockdiag) + _dot_nn(mid, ones_blockdiag) + _dot_nn(lo, ones_blockdiag))
    return jnp.concatenate(cols, axis=1)


def _mix_bwd(name, do, outs, lses, ones_blockdiag):
    tm = 256

    def body(do_ref, o0, o1, o2, l0, l1, l2, ones_ref, d0, d1, d2, t0, t1, t2):
        a, b, c = l0[...], l1[...], l2[...]
        m = jnp.maximum(jnp.maximum(a, b), c)
        ea, eb, ec = jnp.exp(a - m), jnp.exp(b - m), jnp.exp(c - m)
        den = ea + eb + ec
        wa, wb, wc = ea / den, eb / den, ec / den
        dov = do_ref[...]
        o = wa * o0[...] + wb * o1[...] + wc * o2[...]
        t = _head_sums(dov * o, ones_ref[...])
        for w, d_ref, t_ref in ((wa, d0, t0), (wb, d1, t1), (wc, d2, t2)):
            d_ref[...] = (w * dov).astype(d_ref.dtype)
            t_ref[...] = w * t

    return pl.pallas_call(
        body, name=name, grid=(S // tm,),
        in_specs=[_row_spec(tm)] * 7 + [_vec_spec(128, 128)],
        out_specs=[_row_spec(tm)] * 6,
        out_shape=[_sds((S, D), BF16)] * 3 + [_sds((S, D), F32)] * 3,
        compiler_params=_cparams(1),
    )(do, *outs, *lses, ones_blockdiag)


def _attn_bwd_post(name, grads, cos_t, sin_t):
    nblk = S // BAND
    scale = HEAD_DIM ** -0.5

    def unrope(x, cs, sn):
        return x * cs - _swap_halves(x) * sn

    def body(*refs):
        in_refs = refs[:15]
        cos_ref, sin_ref, dq_ref, dkv_ref = refs[15:]
        i = pl.program_id(0)
        cs = cos_ref[...]
        sn = sin_ref[...]
        for g, (_, d) in enumerate(BRANCHES):
            dq_g, dkc, dkp, dvc, dvp = in_refs[5 * g:5 * g + 5]
            has_next = (i + d) < nblk
            for j in range(D // 128):
                lanes = slice(j * 128, (j + 1) * 128)
                out_lanes = slice(g * D + j * 128, g * D + (j + 1) * 128)
                v_lanes = slice(QW + g * D + j * 128, QW + g * D + (j + 1) * 128)
                dq_ref[:, out_lanes] = (unrope(dq_g[:, lanes], cs, sn) * scale).astype(dq_ref.dtype)
                dk = dkc[:, lanes] + jnp.where(has_next, dkp[:, lanes], 0.0)
                dkv_ref[:, out_lanes] = unrope(dk, cs, sn).astype(dkv_ref.dtype)
                dv = dvc[:, lanes] + jnp.where(has_next, dvp[:, lanes], 0.0)
                dkv_ref[:, v_lanes] = dv.astype(dkv_ref.dtype)

    here = pl.BlockSpec((BAND, D), lambda i: (i, 0))
    in_specs, operands = [], []
    for (_, d), branch in zip(BRANCHES, grads):
        later = pl.BlockSpec((BAND, D), lambda i, d=d: (jnp.minimum(i + d, nblk - 1), 0))
        in_specs += [here, here, later, here, later]
        operands += list(branch)
    tab = pl.BlockSpec((BAND, 128), lambda i: (i, 0))
    return pl.pallas_call(
        body, name=name, grid=(nblk,),
        in_specs=in_specs + [tab, tab],
        out_specs=[pl.BlockSpec((BAND, QW), lambda i: (i, 0)), pl.BlockSpec((BAND, 2 * QW), lambda i: (i, 0))],
        out_shape=[_sds((S, QW), BF16), _sds((S, 2 * QW), BF16)],
        compiler_params=_cparams(1),
    )(*operands, cos_t, sin_t)


def _adamw(name, parts, w, m, v):
    n, rows, cols = parts.shape
    tr = rows
    for cand in (256, 176, 128, 64, 32, 16, 8):
        if rows % cand == 0:
            tr = cand
            break
    c1 = 1.0 / (1.0 - ADAM_B1 ** ADAM_STEP)
    c2 = 1.0 / (1.0 - ADAM_B2 ** ADAM_STEP)

    def body(p_ref, w_ref, m_ref, v_ref, g_ref, d_ref, nm_ref, nv_ref):
        g = p_ref[0].astype(F32)
        for j in range(1, n):
            g = g + p_ref[j].astype(F32)
        nm = ADAM_B1 * m_ref[...] + (1.0 - ADAM_B1) * g
        nv = ADAM_B2 * v_ref[...] + (1.0 - ADAM_B2) * (g * g)
        g_ref[...] = g
        nm_ref[...] = nm
        nv_ref[...] = nv
        d_ref[...] = -ADAM_LR * ((nm * c1) / (jnp.sqrt(nv * c2) + ADAM_EPS) + ADAM_WD * w_ref[...])

    blk = pl.BlockSpec((tr, cols), lambda i: (i, 0))
    return pl.pallas_call(
        body, name=name, grid=(rows // tr,),
        in_specs=[pl.BlockSpec((n, tr, cols), lambda i: (0, i, 0)), blk, blk, blk],
        out_specs=[blk] * 4, out_shape=[_sds((rows, cols), F32)] * 4,
        compiler_params=_cparams(1),
    )(parts, w, m, v)


def _exchange(name, arrays, kind):
    n = len(arrays)
    gather = kind == "gather"
    out_shape = [_sds((NDEV,) + a.shape if gather else a.shape, a.dtype) for a in arrays]

    def body(*refs):
        srcs, outs = refs[:n], refs[n:2 * n]
        send_sems, recv_sems, local_sems = refs[2 * n:]
        x, y, c = lax.axis_index("x"), lax.axis_index("y"), lax.axis_index("c")
        me = 4 * x + 2 * y + c
        pending = []
        for t in range(n):
            own = pltpu.make_async_copy(srcs[t] if gather else srcs[t].at[me], outs[t].at[me], local_sems.at[t])
            own.start()
            pending.append(own)
            for rel in range(1, NDEV):
                px = 1 - x if rel & 4 else x
                py = 1 - y if rel & 2 else y
                pc = 1 - c if rel & 1 else c
                peer = 4 * px + 2 * py + pc
                send = pltpu.make_async_remote_copy(
                    src_ref=srcs[t] if gather else srcs[t].at[peer], dst_ref=outs[t].at[me],
                    send_sem=send_sems.at[t, rel - 1], recv_sem=recv_sems.at[t, rel - 1],
                    device_id=(px, py, pc), device_id_type=MESH)
                send.start()
                arrive = pltpu.make_async_remote_copy(
                    src_ref=srcs[t] if gather else srcs[t].at[me], dst_ref=outs[t].at[peer],
                    send_sem=send_sems.at[t, rel - 1], recv_sem=recv_sems.at[t, rel - 1],
                    device_id=(px, py, pc), device_id_type=MESH)
                pending.append((send, arrive))
        for item in pending:
            if isinstance(item, tuple):
                item[0].wait_send()
                item[1].wait_recv()
            else:
                item.wait()

    any_spec = pl.BlockSpec(memory_space=pl.ANY)
    outs = pl.pallas_call(
        body, name=name,
        in_specs=[any_spec] * n, out_specs=[any_spec] * n, out_shape=out_shape,
        scratch_shapes=[pltpu.SemaphoreType.DMA((n, NDEV - 1)), pltpu.SemaphoreType.DMA((n, NDEV - 1)),
                        pltpu.SemaphoreType.DMA((n,))],
    )(*arrays)
    return list(outs)


def _ffn_fwd(tag, h, g_pre, wg, wd4):
    n = _rms_fwd(f"ffn_prenorm_{tag}", h, [g_pre])[0]
    gu = _fwd_cols_blocked(f"ffn_gate_up_{tag}", n, wg).reshape(2, NFB, S, FB)
    act = _swiglu_fwd(f"ffn_act_{tag}", gu)
    f = _fwd_kblocked(f"ffn_down_{tag}", act, wd4)
    return n, gu, act, f


def _ffn_bwd(tag, dh_out, h_in, f, saved, g_pre, g_post, wg, wd4):
    n, gu, act = saved
    df, (dg_post,) = _rms_bwd(f"ffn_postnorm_bwd_{tag}", f, [(g_post, dh_out)], None, BF16)
    da = _bwd_x_kblocked(f"ffn_down_dx_{tag}", df, wd4)
    dwd = _bwd_w_kblocked(f"ffn_down_dw_{tag}", act, df).reshape(NDEV, DFF // NDEV, D)
    dgu = _swiglu_bwd(f"ffn_act_bwd_{tag}", gu, da).reshape(NDEV, S, FB)
    dn = _bwd_x_cols_blocked(f"ffn_gate_up_dx_{tag}", dgu, wg)
    dwg = _bwd_w_cols_blocked(f"ffn_gate_up_dw_{tag}", n, dgu)
    dh_in, (dg_pre,) = _rms_bwd(f"ffn_prenorm_bwd_{tag}", h_in, [(g_pre, dn)], dh_out, F32)
    return dh_in, dwg, dwd, dg_pre, dg_post


def _to_classes(t, d):
    return t.reshape(S // d, d * D)


def kernel(x, positions, mix_norm_pre, mix_norm_post, ffn_norm_pre, ffn_norm_post, ffn_w_gate_up, ffn_w_down, conv_w_in, conv_w, conv_w_out, kv_norm, w_kv, w_q, w_o, loss_target, m_mix_norm_pre, m_mix_norm_post, m_ffn_norm_pre, m_ffn_norm_post, m_ffn_w_gate_up, m_ffn_w_down, m_conv_w_in, m_conv_w, m_conv_w_out, m_kv_norm, m_w_kv, m_w_q, m_w_o, v_mix_norm_pre, v_mix_norm_post, v_ffn_norm_pre, v_ffn_norm_post, v_ffn_w_gate_up, v_ffn_w_down, v_conv_w_in, v_conv_w, v_conv_w_out, v_kv_norm, v_w_kv, v_w_q, v_w_o):
    me = 4 * lax.axis_index("x") + 2 * lax.axis_index("y") + lax.axis_index("c")
    h0 = x.reshape(S, D)
    target = loss_target.reshape(S, D)
    row = lambda a, l: a[l].reshape(1, D)
    g_kv = kv_norm.reshape(1, D)

    cw_shard = jnp.pad(conv_w[0], ((0, 5), (0, 0)))
    shards = [conv_w_in[0], conv_w_out[0], ffn_w_gate_up[0], ffn_w_down[0],
              w_kv, w_q[0], w_o[0], ffn_w_gate_up[1], ffn_w_down[1]]
    gathered = _exchange("gather_weights", [s.astype(BF16) for s in shards] + [cw_shard], "gather")
    win_g, wout_g, wg0, wd0_g, wkv_g, wq_g, wo_g, wg1, wd1_g, cw_g = gathered
    wout = wout_g.reshape(D, D)
    wo = wo_g.reshape(D, D)
    wd0 = wd0_g.reshape(NFB, FB, D)
    wd1 = wd1_g.reshape(NFB, FB, D)
    cw = cw_g.transpose(1, 0, 2).reshape(8, D)

    n0 = _rms_fwd("mix_prenorm_0", h0, [row(mix_norm_pre, 0)])[0]
    z = _fwd_cols("conv_in", n0, win_g)
    pre = _conv_fwd("conv_gate", z, cw)
    y0 = _fwd_rows("conv_out", pre, wout)
    h1 = _resid_rms("mix_postnorm_0", h0, y0, row(mix_norm_post, 0))
    n1, gu0, act0, f0 = _ffn_fwd("0", h1, row(ffn_norm_pre, 0), wg0, wd0)
    h2 = _resid_rms("ffn_postnorm_0", h1, f0, row(ffn_norm_post, 0))

    nk, n2 = _rms_fwd("kv_and_mix_prenorm_1", h2, [g_kv, row(mix_norm_pre, 1)])
    kv = _fwd_cols("kv_proj", nk, wkv_g)
    q_raw = _fwd_cols("q_proj", n2, wq_g)
    half = HEAD_DIM // 2
    inv_freq = ROPE_THETA ** (-jnp.arange(half, dtype=F32) / half)
    cos_t, sin_t = _rope_tables("rope_tables", positions.reshape(S, 1), jnp.tile(inv_freq, 4).reshape(1, 128))
    q = _rope_fwd("rope_q", q_raw, 0, cos_t, sin_t, HEAD_DIM ** -0.5)
    k = _rope_fwd("rope_k", kv, 0, cos_t, sin_t, 1.0)
    qc, kc, vc, lse_c, o_br, lse_br = [], [], [], [], [], []
    for g, (_, d) in enumerate(BRANCHES):
        qc.append(_to_classes(q[:, g * D:(g + 1) * D], d))
        kc.append(_to_classes(k[:, g * D:(g + 1) * D], d))
        vc.append(_to_classes(kv[:, QW + g * D:QW + (g + 1) * D], d))
        o_g, lse_g = _attn_fwd(f"attn_fwd_{g}", qc[g], kc[g], vc[g], d)
        lse_c.append(lse_g)
        o_br.append(o_g.reshape(S, D))
        lse_br.append(lse_g.reshape(S, D))
    o_mix = _mix_fwd("attn_mix", o_br, lse_br)
    y1 = _fwd_rows("attn_out", o_mix, wo)
    h3 = _resid_rms("mix_postnorm_1", h2, y1, row(mix_norm_post, 1))
    n3, gu1, act1, f1 = _ffn_fwd("1", h3, row(ffn_norm_pre, 1), wg1, wd1)
    h4 = _resid_rms("ffn_postnorm_1", h3, f1, row(ffn_norm_post, 1))

    dh4, sq = _loss_grad("loss", h4, target)
    loss = lax.psum(jnp.sum(sq) * (0.5 / D), ("x", "y", "c"))

    dh3, dwg1, dwd1, dg_fpre1, dg_fpost1 = _ffn_bwd(
        "1", dh4, h3, f1, (n3, gu1, act1), row(ffn_norm_pre, 1), row(ffn_norm_post, 1), wg1, wd1)
    dy1, (dg_mpost1,) = _rms_bwd("mix_postnorm_bwd_1", y1, [(row(mix_norm_post, 1), dh3)], None, BF16)
    do = _bwd_x_rows("attn_out_dx", dy1, wo, F32)
    dwo = _bwd_w_rows("attn_out_dw", o_mix, dy1).reshape(NDEV, D // NDEV, D)
    lane = jnp.arange(128)
    ones_blockdiag = (lane[:, None] // HEAD_DIM == lane[None, :] // HEAD_DIM).astype(BF16)
    mixed = _mix_bwd("attn_mix_bwd", do, o_br, lse_br, ones_blockdiag)
    branch_grads = []
    for g, (_, d) in enumerate(BRANCHES):
        res = _attn_bwd(f"attn_bwd_{g}", qc[g], kc[g], vc[g], _to_classes(mixed[g], d),
                        lse_c[g], _to_classes(mixed[3 + g], d), d)
        branch_grads.append([r.reshape(S, D) for r in res])
    dq_raw, dkv = _attn_bwd_post("attn_bwd_post", branch_grads, cos_t, sin_t)
    dn2 = _bwd_x_cols("q_proj_dx", dq_raw, wq_g)
    dwq = _bwd_w_cols("q_proj_dw", n2, dq_raw, QW // NDEV)
    dnk = _bwd_x_cols("kv_proj_dx", dkv, wkv_g)
    dwkv = _bwd_w_cols("kv_proj_dw", nk, dkv, 2 * QW // NDEV)
    dh2, (dg_kv, dg_mpre1) = _rms_bwd("kv_and_mix_prenorm_bwd_1", h2,
                                      [(g_kv, dnk), (row(mix_norm_pre, 1), dn2)], dh3, F32)

    dh1, dwg0, dwd0, dg_fpre0, dg_fpost0 = _ffn_bwd(
        "0", dh2, h1, f0, (n1, gu0, act0), row(ffn_norm_pre, 0), row(ffn_norm_post, 0), wg0, wd0)
    dy0, (dg_mpost0,) = _rms_bwd("mix_postnorm_bwd_0", y0, [(row(mix_norm_post, 0), dh1)], None, BF16)
    dpre = _bwd_x_rows("conv_out_dx", dy0, wout, BF16)
    dwout = _bwd_w_rows("conv_out_dw", pre, dy0).reshape(NDEV, D // NDEV, D)
    dz, dcw = _conv_bwd("conv_gate_bwd", z, dpre, cw)
    dn0 = _bwd_x_cols("conv_in_dx", dz, win_g)
    dwin = _bwd_w_cols("conv_in_dw", n0, dz, 3 * D // NDEV)
    dh0, (dg_mpre0,) = _rms_bwd("mix_prenorm_bwd_0", h0, [(row(mix_norm_pre, 0), dn0)], dh1, F32)

    small = jnp.concatenate([dg_mpre0, dg_mpre1, dg_mpost0, dg_mpost1, dg_fpre0, dg_fpre1, dg_fpost0, dg_fpost1,
                             dg_kv, jnp.zeros((7, D), F32), dcw], axis=0)
    recv = _exchange("scatter_grads", [dwin, dwout, dwg0, dwd0, dwkv, dwq, dwo, dwg1, dwd1], "scatter")
    rwin, rwout, rwg0, rwd0, rwkv, rwq, rwo, rwg1, rwd1 = recv
    small_all = _exchange("gather_small_grads", [small], "gather")[0]

    def upd(tag, parts, w, m, v):
        shape = w.shape
        flat = lambda a: a.reshape(parts.shape[1:])
        return [r.reshape(shape) for r in _adamw(f"adamw_{tag}", parts, flat(w), flat(m), flat(v))]

    def upd_layers(tag, parts_per_layer, w, m, v):
        res = [upd(f"{tag}_{l}", p, w[l], m[l], v[l]) for l, p in enumerate(parts_per_layer)]
        return [jnp.stack([r[i] for r in res]) for i in range(4)]

    gains_w = jnp.concatenate([mix_norm_pre, mix_norm_post, ffn_norm_pre, ffn_norm_post, g_kv, jnp.zeros((15, D), F32)])
    gains_m = jnp.concatenate([m_mix_norm_pre, m_mix_norm_post, m_ffn_norm_pre, m_ffn_norm_post,
                               m_kv_norm.reshape(1, D), jnp.zeros((15, D), F32)])
    gains_v = jnp.concatenate([v_mix_norm_pre, v_mix_norm_post, v_ffn_norm_pre, v_ffn_norm_post,
                               v_kv_norm.reshape(1, D), jnp.ones((15, D), F32)])
    small_res = _adamw("adamw_gains", small_all, gains_w, gains_m, gains_v)
    dcw_mine = lax.dynamic_slice(small_res[0], (16, me * 128), (8, 128))
    pad8 = lambda a, fill: jnp.pad(a[0], ((0, 5), (0, 0)), constant_values=fill)
    cw_res = [r[0:3].reshape(1, 3, 128) for r in
              _adamw("adamw_conv_w", dcw_mine.reshape(1, 8, 128), cw_shard, pad8(m_conv_w, 0.0), pad8(v_conv_w, 1.0))]

    res = {
        "mix_norm_pre": [r[0:2] for r in small_res],
        "mix_norm_post": [r[2:4] for r in small_res],
        "ffn_norm_pre": [r[4:6] for r in small_res],
        "ffn_norm_post": [r[6:8] for r in small_res],
        "kv_norm": [r[8] for r in small_res],
        "ffn_w_gate_up": upd_layers("gate_up", [rwg0, rwg1], ffn_w_gate_up, m_ffn_w_gate_up, v_ffn_w_gate_up),
        "ffn_w_down": upd_layers("down", [rwd0, rwd1], ffn_w_down, m_ffn_w_down, v_ffn_w_down),
        "conv_w_in": upd("conv_in", rwin, conv_w_in, m_conv_w_in, v_conv_w_in),
        "conv_w": cw_res,
        "conv_w_out": upd("conv_out", rwout, conv_w_out, m_conv_w_out, v_conv_w_out),
        "w_kv": upd("kv", rwkv, w_kv, m_w_kv, v_w_kv),
        "w_q": upd("q", rwq, w_q, m_w_q, v_w_q),
        "w_o": upd("o", rwo, w_o, m_w_o, v_w_o),
    }
    order = ["mix_norm_pre", "mix_norm_post", "ffn_norm_pre", "ffn_norm_post", "ffn_w_gate_up", "ffn_w_down",
             "conv_w_in", "conv_w", "conv_w_out", "kv_norm", "w_kv", "w_q", "w_o"]
    out = [loss, dh0.reshape(1, S, D)]
    for i in range(4):
        out += [res[name][i] for name in order]
    return tuple(out)
```

```python
import jax
import jax.numpy as jnp
from jax import lax
from jax.experimental import pallas as pl
from jax.experimental.pallas import tpu as pltpu

F32 = jnp.float32
BF16 = jnp.bfloat16

S = 4096
D = 1024
NDEV = 8
HEAD_DIM = 64
QW = 3072
DFF = 2816
FB = 704
NFB = 4
BRANCHES = ((128, 1), (512, 4), (2048, 16))
BAND = 128
ROPE_THETA = 10000.0
RMS_EPS = 1e-6
NEG_INF = -1e30
ADAM_LR, ADAM_B1, ADAM_B2, ADAM_EPS, ADAM_WD, ADAM_STEP = 0.001, 0.9, 0.999, 1e-08, 0.01, 10

VMEM_LIMIT_BYTES = 52 * 1024 * 1024
ROW_TILE = 512
MESH = pl.DeviceIdType.MESH


def _cparams(ngrid):
    return pltpu.CompilerParams(dimension_semantics=("arbitrary",) * ngrid,
                                vmem_limit_bytes=VMEM_LIMIT_BYTES)


def _sds(shape, dtype):
    return jax.ShapeDtypeStruct(tuple(shape), dtype)


_DIMS = {"nn": (((1,), (0,)), ((), ())),
         "nt": (((1,), (1,)), ((), ())),
         "tn": (((0,), (0,)), ((), ()))}


def _matmul(name, a, b, *, mode, grid, a_blk, a_map, b_blk, b_map, o_shape, o_blk, o_map, out_dtype):
    nk = grid[2]
    dims = _DIMS[mode]
    acc_shape = tuple(s for s in o_blk if s is not None)

    def body(a_ref, b_ref, o_ref, *scratch):
        part = lax.dot_general(a_ref[...], b_ref[...], dims, preferred_element_type=F32)
        if nk == 1:
            o_ref[...] = part.astype(o_ref.dtype)
            return
        acc_ref = scratch[0]
        k = pl.program_id(2)

        @pl.when(k == 0)
        def _():
            acc_ref[...] = part

        @pl.when(k > 0)
        def _():
            acc_ref[...] += part

        @pl.when(k == nk - 1)
        def _():
            o_ref[...] = acc_ref[...].astype(o_ref.dtype)

    return pl.pallas_call(
        body, name=name, grid=grid,
        in_specs=[pl.BlockSpec(a_blk, a_map), pl.BlockSpec(b_blk, b_map)],
        out_specs=pl.BlockSpec(o_blk, o_map),
        out_shape=_sds(o_shape, out_dtype),
        scratch_shapes=[] if nk == 1 else [pltpu.VMEM(acc_shape, F32)],
        compiler_params=_cparams(3),
    )(a, b)


TM = 1024
TK = 1024


def _fwd_cols(name, a, wg, out_dtype=BF16):
    _, kdim, n = wg.shape
    return _matmul(name, a, wg, mode="nn", grid=(S // TM, NDEV, 1),
                   a_blk=(TM, kdim), a_map=lambda i, j, k: (i, 0),
                   b_blk=(None, kdim, n), b_map=lambda i, j, k: (j, 0, 0),
                   o_shape=(S, NDEV * n), o_blk=(TM, n), o_map=lambda i, j, k: (i, j), out_dtype=out_dtype)


def _fwd_cols_blocked(name, a, wg):
    _, kdim, n = wg.shape
    return _matmul(name, a, wg, mode="nn", grid=(S // TM, NDEV, 1),
                   a_blk=(TM, kdim), a_map=lambda i, j, k: (i, 0),
                   b_blk=(None, kdim, n), b_map=lambda i, j, k: (j, 0, 0),
                   o_shape=(NDEV, S, n), o_blk=(None, TM, n), o_map=lambda i, j, k: (j, i, 0), out_dtype=BF16)


def _fwd_rows(name, a, w, out_dtype=F32):
    kdim, n = w.shape
    tn = 512
    return _matmul(name, a, w, mode="nn", grid=(S // TM, n // tn, 1),
                   a_blk=(TM, kdim), a_map=lambda i, j, k: (i, 0),
                   b_blk=(kdim, tn), b_map=lambda i, j, k: (0, j),
                   o_shape=(S, n), o_blk=(TM, tn), o_map=lambda i, j, k: (i, j), out_dtype=out_dtype)


def _fwd_kblocked(name, a4, w4):
    nb, _, kb = a4.shape
    n = w4.shape[2]
    return _matmul(name, a4, w4, mode="nn", grid=(S // TM, 1, nb),
                   a_blk=(None, TM, kb), a_map=lambda i, j, k: (k, i, 0),
                   b_blk=(None, kb, n), b_map=lambda i, j, k: (k, 0, 0),
                   o_shape=(S, n), o_blk=(TM, n), o_map=lambda i, j, k: (i, 0), out_dtype=F32)


def _bwd_x_cols(name, dy, wg):
    _, kdim, n = wg.shape
    return _matmul(name, dy, wg, mode="nt", grid=(S // TM, 1, NDEV),
                   a_blk=(TM, n), a_map=lambda i, j, k: (i, k),
                   b_blk=(None, kdim, n), b_map=lambda i, j, k: (k, 0, 0),
                   o_shape=(S, kdim), o_blk=(TM, kdim), o_map=lambda i, j, k: (i, 0), out_dtype=F32)


def _bwd_x_cols_blocked(name, dy8, wg):
    _, kdim, n = wg.shape
    return _matmul(name, dy8, wg, mode="nt", grid=(S // TM, 1, NDEV),
                   a_blk=(None, TM, n), a_map=lambda i, j, k: (k, i, 0),
                   b_blk=(None, kdim, n), b_map=lambda i, j, k: (k, 0, 0),
                   o_shape=(S, kdim), o_blk=(TM, kdim), o_map=lambda i, j, k: (i, 0), out_dtype=F32)


def _bwd_x_rows(name, dy, w, out_dtype):
    kdim, n = w.shape
    tkk = 512
    return _matmul(name, dy, w, mode="nt", grid=(S // TM, kdim // tkk, 1),
                   a_blk=(TM, n), a_map=lambda i, j, k: (i, 0),
                   b_blk=(tkk, n), b_map=lambda i, j, k: (j, 0),
                   o_shape=(S, kdim), o_blk=(TM, tkk), o_map=lambda i, j, k: (i, j), out_dtype=out_dtype)


def _bwd_x_kblocked(name, dy, w4):
    nb, kb, n = w4.shape
    return _matmul(name, dy, w4, mode="nt", grid=(S // TM, nb, 1),
                   a_blk=(TM, n), a_map=lambda i, j, k: (i, 0),
                   b_blk=(None, kb, n), b_map=lambda i, j, k: (j, 0, 0),
                   o_shape=(nb, S, kb), o_blk=(None, TM, kb), o_map=lambda i, j, k: (j, i, 0), out_dtype=BF16)


def _bwd_w_cols(name, a, dy, n):
    kdim = a.shape[1]
    return _matmul(name, a, dy, mode="tn", grid=(1, NDEV, S // TK),
                   a_blk=(TK, kdim), a_map=lambda i, j, k: (k, 0),
                   b_blk=(TK, n), b_map=lambda i, j, k: (k, j),
                   o_shape=(NDEV, kdim, n), o_blk=(None, kdim, n), o_map=lambda i, j, k: (j, 0, 0), out_dtype=BF16)


def _bwd_w_cols_blocked(name, a, dy8):
    kdim = a.shape[1]
    n = dy8.shape[2]
    return _matmul(name, a, dy8, mode="tn", grid=(1, NDEV, S // TK),
                   a_blk=(TK, kdim), a_map=lambda i, j, k: (k, 0),
                   b_blk=(None, TK, n), b_map=lambda i, j, k: (j, k, 0),
                   o_shape=(NDEV, kdim, n), o_blk=(None, kdim, n), o_map=lambda i, j, k: (j, 0, 0), out_dtype=BF16)


def _bwd_w_rows(name, a, dy):
    kdim = a.shape[1]
    n = dy.shape[1]
    tmm = 512
    return _matmul(name, a, dy, mode="tn", grid=(kdim // tmm, 1, S // TK),
                   a_blk=(TK, tmm), a_map=lambda i, j, k: (k, i),
                   b_blk=(TK, n), b_map=lambda i, j, k: (k, 0),
                   o_shape=(kdim, n), o_blk=(tmm, n), o_map=lambda i, j, k: (i, 0), out_dtype=BF16)


def _bwd_w_kblocked(name, a4, dy):
    nb, _, kb = a4.shape
    n = dy.shape[1]
    return _matmul(name, a4, dy, mode="tn", grid=(nb, 1, S // TK),
                   a_blk=(None, TK, kb), a_map=lambda i, j, k: (i, k, 0),
                   b_blk=(TK, n), b_map=lambda i, j, k: (k, 0),
                   o_shape=(nb, kb, n), o_blk=(None, kb, n), o_map=lambda i, j, k: (i, 0, 0), out_dtype=BF16)


def _rstd(x):
    return lax.rsqrt(jnp.mean(x * x, axis=-1, keepdims=True) + RMS_EPS)


def _row_spec(tm=ROW_TILE, width=D):
    return pl.BlockSpec((tm, width), lambda i: (i, 0))


def _vec_spec(rows=1, width=D):
    return pl.BlockSpec((rows, width), lambda i: (0, 0))


def _rms_fwd(name, x, gains):
    n = len(gains)

    def body(x_ref, *refs):
        x_val = x_ref[...]
        xh = x_val * _rstd(x_val)
        for g_ref, o_ref in zip(refs[:n], refs[n:]):
            o_ref[...] = (xh * g_ref[...]).astype(o_ref.dtype)

    outs = pl.pallas_call(
        body, name=name, grid=(S // ROW_TILE,),
        in_specs=[_row_spec()] + [_vec_spec()] * n,
        out_specs=[_row_spec()] * n,
        out_shape=[_sds((S, D), BF16)] * n,
        compiler_params=_cparams(1),
    )(x, *gains)
    return list(outs)


def _resid_rms(name, h, y, g):
    def body(h_ref, y_ref, g_ref, o_ref):
        y_val = y_ref[...]
        o_ref[...] = h_ref[...] + (y_val * _rstd(y_val)) * g_ref[...]

    return pl.pallas_call(
        body, name=name, grid=(S // ROW_TILE,),
        in_specs=[_row_spec(), _row_spec(), _vec_spec()],
        out_specs=_row_spec(), out_shape=_sds((S, D), F32),
        compiler_params=_cparams(1),
    )(h, y, g)


def _rms_bwd(name, x, pairs, dres, out_dtype):
    n = len(pairs)
    has_res = dres is not None

    def body(x_ref, *refs):
        g_refs = refs[0:2 * n:2]
        dn_refs = refs[1:2 * n:2]
        pos = 2 * n
        res_ref = refs[pos] if has_res else None
        pos += int(has_res)
        dx_ref = refs[pos]
        dg_refs = refs[pos + 1:]
        step = pl.program_id(0)
        x_val = x_ref[...]
        r = _rstd(x_val)
        xh = x_val * r
        acc = res_ref[...] if has_res else jnp.zeros_like(x_val)
        for g_ref, dn_ref, dg_ref in zip(g_refs, dn_refs, dg_refs):
            dn = dn_ref[...].astype(F32)
            dxh = dn * g_ref[...]
            acc = acc + r * (dxh - xh * jnp.mean(dxh * xh, axis=-1, keepdims=True))
            part = jnp.sum(dn * xh, axis=0, keepdims=True)

            @pl.when(step == 0)
            def _():
                dg_ref[...] = part

            @pl.when(step > 0)
            def _():
                dg_ref[...] += part

        dx_ref[...] = acc.astype(dx_ref.dtype)

    operands = [x]
    in_specs = [_row_spec()]
    for g, dn in pairs:
        operands += [g, dn]
        in_specs += [_vec_spec(), _row_spec()]
    if has_res:
        operands.append(dres)
        in_specs.append(_row_spec())
    outs = pl.pallas_call(
        body, name=name, grid=(S // ROW_TILE,),
        in_specs=in_specs,
        out_specs=[_row_spec()] + [_vec_spec()] * n,
        out_shape=[_sds((S, D), out_dtype)] + [_sds((1, D), F32)] * n,
        compiler_params=_cparams(1),
    )(*operands)
    return outs[0], list(outs[1:])


def _loss_grad(name, h, target):
    def body(h_ref, t_ref, dh_ref, part_ref):
        e = h_ref[...] - t_ref[...]
        dh_ref[...] = e * (1.0 / D)
        part = jnp.sum(e * e, axis=0, keepdims=True)
        step = pl.program_id(0)

        @pl.when(step == 0)
        def _():
            part_ref[...] = part

        @pl.when(step > 0)
        def _():
            part_ref[...] += part

    return pl.pallas_call(
        body, name=name, grid=(S // ROW_TILE,),
        in_specs=[_row_spec(), _row_spec()],
        out_specs=[_row_spec(), _vec_spec()],
        out_shape=[_sds((S, D), F32), _sds((1, D), F32)],
        compiler_params=_cparams(1),
    )(h, target)


def _shift_down(u, prev8, k):
    r = pltpu.roll(u, k, 0)
    p = pltpu.roll(prev8, k, 0)
    row = lax.broadcasted_iota(jnp.int32, prev8.shape, 0)
    top = jnp.where(row < k, p, r[0:8])
    return jnp.concatenate([top, r[8:]], axis=0)


def _shift_up(u, next8, k):
    tm = u.shape[0]
    r = pltpu.roll(u, tm - k, 0)
    p = pltpu.roll(next8, 8 - k, 0)
    row = lax.broadcasted_iota(jnp.int32, next8.shape, 0)
    bot = jnp.where(row >= 8 - k, p, r[tm - 8:tm])
    return jnp.concatenate([r[:tm - 8], bot], axis=0)


CONV_TILE = 512


def _halo_prev(col):
    return pl.BlockSpec((8, D), lambda i: (jnp.maximum(i * (CONV_TILE // 8) - 1, 0), col))


def _halo_next(col):
    last = S // 8 - 1
    return pl.BlockSpec((8, D), lambda i: (jnp.minimum((i + 1) * (CONV_TILE // 8), last), col))


def _conv_fwd(name, z, cw):
    def body(b_ref, c_ref, h_ref, cp_ref, hp_ref, cw_ref, o_ref):
        i = pl.program_id(0)
        u = c_ref[...].astype(F32) * h_ref[...].astype(F32)
        up = cp_ref[...].astype(F32) * hp_ref[...].astype(F32)
        up = jnp.where(i > 0, up, 0.0)
        cv = cw_ref[0:1, :] * _shift_down(u, up, 2) + cw_ref[1:2, :] * _shift_down(u, up, 1) + cw_ref[2:3, :] * u
        o_ref[...] = (b_ref[...].astype(F32) * cv).astype(o_ref.dtype)

    col = lambda c: pl.BlockSpec((CONV_TILE, D), lambda i: (i, c))
    return pl.pallas_call(
        body, name=name, grid=(S // CONV_TILE,),
        in_specs=[col(0), col(1), col(2), _halo_prev(1), _halo_prev(2), _vec_spec(8)],
        out_specs=_row_spec(CONV_TILE), out_shape=_sds((S, D), BF16),
        compiler_params=_cparams(1),
    )(z, z, z, z, z, cw)


def _conv_bwd(name, z, dpre, cw):
    nsteps = S // CONV_TILE

    def body(b_ref, c_ref, h_ref, cp_ref, hp_ref, dp_ref, dpn_ref, bn_ref, cw_ref, dz_ref, dcw_ref):
        i = pl.program_id(0)
        b = b_ref[...].astype(F32)
        c = c_ref[...].astype(F32)
        h = h_ref[...].astype(F32)
        dp = dp_ref[...].astype(F32)
        u = c * h
        up = jnp.where(i > 0, cp_ref[...].astype(F32) * hp_ref[...].astype(F32), 0.0)
        s1 = _shift_down(u, up, 1)
        s2 = _shift_down(u, up, 2)
        w0, w1, w2 = cw_ref[0:1, :], cw_ref[1:2, :], cw_ref[2:3, :]
        cv = w0 * s2 + w1 * s1 + w2 * u
        dcv = dp * b
        dcvn = jnp.where(i < nsteps - 1, dpn_ref[...].astype(F32) * bn_ref[...].astype(F32), 0.0)
        du = w2 * dcv + w1 * _shift_up(dcv, dcvn, 1) + w0 * _shift_up(dcv, dcvn, 2)
        dz_ref[:, 0:D] = (dp * cv).astype(dz_ref.dtype)
        dz_ref[:, D:2 * D] = (du * h).astype(dz_ref.dtype)
        dz_ref[:, 2 * D:3 * D] = (du * c).astype(dz_ref.dtype)

        @pl.when(i == 0)
        def _():
            dcw_ref[...] = jnp.zeros_like(dcw_ref)

        dcw_ref[0:1, :] += jnp.sum(dcv * s2, axis=0, keepdims=True)
        dcw_ref[1:2, :] += jnp.sum(dcv * s1, axis=0, keepdims=True)
        dcw_ref[2:3, :] += jnp.sum(dcv * u, axis=0, keepdims=True)

    col = lambda c: pl.BlockSpec((CONV_TILE, D), lambda i: (i, c))
    return pl.pallas_call(
        body, name=name, grid=(nsteps,),
        in_specs=[col(0), col(1), col(2), _halo_prev(1), _halo_prev(2),
                  _row_spec(CONV_TILE), _halo_next(0), _halo_next(0), _vec_spec(8)],
        out_specs=[pl.BlockSpec((CONV_TILE, 3 * D), lambda i: (i, 0)), _vec_spec(8)],
        out_shape=[_sds((S, 3 * D), BF16), _sds((8, D), F32)],
        compiler_params=_cparams(1),
    )(z, z, z, z, z, dpre, dpre, z, cw)


def _swiglu_fwd(name, gu):
    def body(gu_ref, o_ref):
        g = gu_ref[0].astype(F32)
        u = gu_ref[1].astype(F32)
        o_ref[...] = (g * jax.nn.sigmoid(g) * u).astype(o_ref.dtype)

    return pl.pallas_call(
        body, name=name, grid=(NFB, S // ROW_TILE),
        in_specs=[pl.BlockSpec((2, None, ROW_TILE, FB), lambda j, i: (0, j, i, 0))],
        out_specs=pl.BlockSpec((None, ROW_TILE, FB), lambda j, i: (j, i, 0)),
        out_shape=_sds((NFB, S, FB), BF16),
        compiler_params=_cparams(2),
    )(gu)


def _swiglu_bwd(name, gu, da):
    def body(gu_ref, da_ref, o_ref):
        g = gu_ref[0].astype(F32)
        u = gu_ref[1].astype(F32)
        d = da_ref[...].astype(F32)
        sg = jax.nn.sigmoid(g)
        o_ref[0] = (d * u * sg * (1.0 + g * (1.0 - sg))).astype(o_ref.dtype)
        o_ref[1] = (d * g * sg).astype(o_ref.dtype)

    blk = pl.BlockSpec((2, None, ROW_TILE, FB), lambda j, i: (0, j, i, 0))
    return pl.pallas_call(
        body, name=name, grid=(NFB, S // ROW_TILE),
        in_specs=[blk, pl.BlockSpec((None, ROW_TILE, FB), lambda j, i: (j, i, 0))],
        out_specs=blk, out_shape=_sds((2, NFB, S, FB), BF16),
        compiler_params=_cparams(2),
    )(gu, da)


def _rope_tables(name, pos_col, inv_freq_row):
    def body(pos_ref, f_ref, cos_ref, sin_ref):
        ang = pos_ref[...].astype(F32) * f_ref[...]
        lane = lax.broadcasted_iota(jnp.int32, ang.shape, 1)
        s = jnp.sin(ang)
        cos_ref[...] = jnp.cos(ang)
        sin_ref[...] = jnp.where((lane % HEAD_DIM) < HEAD_DIM // 2, -s, s)

    tab = pl.BlockSpec((ROW_TILE, 128), lambda i: (i, 0))
    return pl.pallas_call(
        body, name=name, grid=(S // ROW_TILE,),
        in_specs=[pl.BlockSpec((ROW_TILE, 1), lambda i: (i, 0)), _vec_spec(1, 128)],
        out_specs=[tab, tab], out_shape=[_sds((S, 128), F32)] * 2,
        compiler_params=_cparams(1),
    )(pos_col, inv_freq_row)


def _swap_halves(t):
    lane = lax.broadcasted_iota(jnp.int32, t.shape, 1)
    first = (lane % HEAD_DIM) < HEAD_DIM // 2
    return jnp.where(first, pltpu.roll(t, 128 - HEAD_DIM // 2, 1), pltpu.roll(t, HEAD_DIM // 2, 1))


ROPE_COLS = 768


def _rope_fwd(name, t, col_off, cos_t, sin_t, scale):
    def body(t_ref, cos_ref, sin_ref, o_ref):
        cs = cos_ref[...]
        sn = sin_ref[...]
        for j in range(ROPE_COLS // 128):
            x = t_ref[:, j * 128:(j + 1) * 128].astype(F32)
            o_ref[:, j * 128:(j + 1) * 128] = ((x * cs + _swap_halves(x) * sn) * scale).astype(o_ref.dtype)

    off = col_off // ROPE_COLS
    tab = pl.BlockSpec((ROW_TILE, 128), lambda i, j: (i, 0))
    return pl.pallas_call(
        body, name=name, grid=(S // ROW_TILE, QW // ROPE_COLS),
        in_specs=[pl.BlockSpec((ROW_TILE, ROPE_COLS), lambda i, j: (i, j + off)), tab, tab],
        out_specs=pl.BlockSpec((ROW_TILE, ROPE_COLS), lambda i, j: (i, j)),
        out_shape=_sds((S, QW), BF16),
        compiler_params=_cparams(2),
    )(t, cos_t, sin_t)


def _attn_tile(d):
    return min(512, S // d)


def _attn_specs(d):
    tq = _attn_tile(d)
    main = pl.BlockSpec((tq, 128), lambda cb, n: (n, cb))
    prev = pl.BlockSpec((BAND, 128), lambda cb, n: (jnp.maximum(n * (tq // BAND) - 1, 0), cb))
    return tq, main, prev


def _dot_nt(a, b):
    return lax.dot_general(a, b, _DIMS["nt"], preferred_element_type=F32)


def _dot_tn(a, b):
    return lax.dot_general(a, b, _DIMS["tn"], preferred_element_type=F32)


def _dot_nn(a, b):
    return lax.dot_general(a, b, _DIMS["nn"], preferred_element_type=F32)


def _band_masks():
    qi = lax.broadcasted_iota(jnp.int32, (BAND, BAND), 0)
    kj = lax.broadcasted_iota(jnp.int32, (BAND, BAND), 1)
    return kj >= qi, kj <= qi, kj < HEAD_DIM


def _attn_fwd(name, q, k, v, d):
    tq, main, prev = _attn_specs(d)
    nsb = tq // BAND

    def body(q_ref, k_ref, kp_ref, v_ref, vp_ref, o_ref, lse_ref):
        n = pl.program_id(1)
        m_prev, m_cur, first_head = _band_masks()
        for sb in range(nsb):
            rows = slice(sb * BAND, (sb + 1) * BAND)
            qb = q_ref[rows, :]
            kc = k_ref[rows, :]
            vc = v_ref[rows, :]
            if sb == 0:
                kp, vp = kp_ref[...], vp_ref[...]
                ok_prev = m_prev & (n > 0)
            else:
                before = slice((sb - 1) * BAND, sb * BAND)
                kp, vp = k_ref[before, :], v_ref[before, :]
                ok_prev = m_prev
            o_heads, lse_heads = [], []
            for head_lanes in (first_head, ~first_head):
                qh = jnp.where(head_lanes, qb, jnp.zeros_like(qb))
                sp = jnp.where(ok_prev, _dot_nt(qh, kp), NEG_INF)
                sc = jnp.where(m_cur, _dot_nt(qh, kc), NEG_INF)
                m = jnp.maximum(jnp.max(sp, axis=-1, keepdims=True), jnp.max(sc, axis=-1, keepdims=True))
                pp = jnp.exp(sp - m)
                pc = jnp.exp(sc - m)
                l = jnp.sum(pp, axis=-1, keepdims=True) + jnp.sum(pc, axis=-1, keepdims=True)
                o_heads.append((_dot_nn(pp.astype(BF16), vp) + _dot_nn(pc.astype(BF16), vc)) / l)
                lse_heads.append(jnp.broadcast_to(m + jnp.log(l), (BAND, 128)))
            o_ref[rows, :] = jnp.where(first_head, o_heads[0], o_heads[1])
            lse_ref[rows, :] = jnp.where(first_head, lse_heads[0], lse_heads[1])

    rdim = S // d
    return pl.pallas_call(
        body, name=name, grid=(8 * d, rdim // tq),
        in_specs=[main, main, prev, main, prev],
        out_specs=[main, main],
        out_shape=[_sds((rdim, d * D), F32)] * 2,
        compiler_params=_cparams(2),
    )(q, k, k, v, v)


def _attn_bwd(name, q, k, v, do, lse, dd, d):
    tq, main, prev = _attn_specs(d)
    nsb = tq // BAND

    def body(q_ref, k_ref, kp_ref, v_ref, vp_ref, do_ref, lse_ref, dd_ref,
             dq_ref, dkc_ref, dkp_ref, dvc_ref, dvp_ref):
        n = pl.program_id(1)
        m_prev, m_cur, first_head = _band_masks()
        for sb in range(nsb):
            rows = slice(sb * BAND, (sb + 1) * BAND)
            qb = q_ref[rows, :]
            dob = do_ref[rows, :]
            kc = k_ref[rows, :]
            vc = v_ref[rows, :]
            if sb == 0:
                kp, vp = kp_ref[...], vp_ref[...]
                ok_prev = m_prev & (n > 0)
            else:
                before = slice((sb - 1) * BAND, sb * BAND)
                kp, vp = k_ref[before, :], v_ref[before, :]
                ok_prev = m_prev
            dq_heads = []
            dkc = dkp = dvc = dvp = None
            for hi, head_lanes in enumerate((first_head, ~first_head)):
                col = slice(hi * HEAD_DIM, hi * HEAD_DIM + 1)
                lse_col = lse_ref[rows, col]
                dd_col = dd_ref[rows, col]
                qh = jnp.where(head_lanes, qb, jnp.zeros_like(qb))
                doh = jnp.where(head_lanes, dob, jnp.zeros_like(dob))
                sp = jnp.where(ok_prev, _dot_nt(qh, kp), NEG_INF)
                sc = jnp.where(m_cur, _dot_nt(qh, kc), NEG_INF)
                pp = jnp.exp(sp - lse_col)
                pc = jnp.exp(sc - lse_col)
                dsp = (pp * (_dot_nt(doh, vp) - dd_col)).astype(BF16)
                dsc = (pc * (_dot_nt(doh, vc) - dd_col)).astype(BF16)
                dq_heads.append(_dot_nn(dsp, kp) + _dot_nn(dsc, kc))
                parts = (_dot_tn(dsc, qh), _dot_tn(dsp, qh),
                         _dot_tn(pc.astype(BF16), doh), _dot_tn(pp.astype(BF16), doh))
                if hi == 0:
                    dkc, dkp, dvc, dvp = parts
                else:
                    dkc, dkp, dvc, dvp = dkc + parts[0], dkp + parts[1], dvc + parts[2], dvp + parts[3]
            dq_ref[rows, :] = jnp.where(first_head, dq_heads[0], dq_heads[1])
            dkc_ref[rows, :] = dkc
            dkp_ref[rows, :] = dkp
            dvc_ref[rows, :] = dvc
            dvp_ref[rows, :] = dvp

    rdim = S // d
    return pl.pallas_call(
        body, name=name, grid=(8 * d, rdim // tq),
        in_specs=[main, main, prev, main, prev, main, main, main],
        out_specs=[main] * 5,
        out_shape=[_sds((rdim, d * D), F32)] * 5,
        compiler_params=_cparams(2),
    )(q, k, k, v, v, do, lse, dd)


def _mix_fwd(name, outs, lses):
    def body(o0, o1, o2, l0, l1, l2, o_ref):
        a, b, c = l0[...], l1[...], l2[...]
        m = jnp.maximum(jnp.maximum(a, b), c)
        ea, eb, ec = jnp.exp(a - m), jnp.exp(b - m), jnp.exp(c - m)
        o_ref[...] = ((ea * o0[...] + eb * o1[...] + ec * o2[...]) / (ea + eb + ec)).astype(o_ref.dtype)

    return pl.pallas_call(
        body, name=name, grid=(S // ROW_TILE,),
        in_specs=[_row_spec()] * 6, out_specs=_row_spec(), out_shape=_sds((S, D), BF16),
        compiler_params=_cparams(1),
    )(*outs, *lses)


def _head_sums(x, ones_blockdiag):
    cols = []
    for j in range(D // 128):
        xj = x[:, j * 128:(j + 1) * 128]
        hi = xj.astype(BF16)
        r1 = xj - hi.astype(F32)
        mid = r1.astype(BF16)
        lo = (r1 - mid.astype(F32)).astype(BF16)
        cols.append(_dot_nn(hi, ones_blockdiag) + _dot_nn(mid, ones_blockdiag) + _dot_nn(lo, ones_blockdiag))
    return jnp.concatenate(cols, axis=1)


def _mix_bwd(name, do, outs, lses, ones_blockdiag):
    tm = 256

    def body(do_ref, o0, o1, o2, l0, l1, l2, ones_ref, d0, d1, d2, t0, t1, t2):
        a, b, c = l0[...], l1[...], l2[...]
        m = jnp.maximum(jnp.maximum(a, b), c)
        ea, eb, ec = jnp.exp(a - m), jnp.exp(b - m), jnp.exp(c - m)
        den = ea + eb + ec
        wa, wb, wc = ea / den, eb / den, ec / den
        dov = do_ref[...]
        o = wa * o0[...] + wb * o1[...] + wc * o2[...]
        t = _head_sums(dov * o, ones_ref[...])
        for w, d_ref, t_ref in ((wa, d0, t0), (wb, d1, t1), (wc, d2, t2)):
            d_ref[...] = (w * dov).astype(d_ref.dtype)
            t_ref[...] = w * t

    return pl.pallas_call(
        body, name=name, grid=(S // tm,),
        in_specs=[_row_spec(tm)] * 7 + [_vec_spec(128, 128)],
        out_specs=[_row_spec(tm)] * 6,
        out_shape=[_sds((S, D), BF16)] * 3 + [_sds((S, D), F32)] * 3,
        compiler_params=_cparams(1),
    )(do, *outs, *lses, ones_blockdiag)


def _attn_bwd_post(name, grads, cos_t, sin_t):
    nblk = S // BAND
    scale = HEAD_DIM ** -0.5

    def unrope(x, cs, sn):
        return x * cs - _swap_halves(x) * sn

    def body(*refs):
        in_refs = refs[:15]
        cos_ref, sin_ref, dq_ref, dkv_ref = refs[15:]
        i = pl.program_id(0)
        cs = cos_ref[...]
        sn = sin_ref[...]
        for g, (_, d) in enumerate(BRANCHES):
            dq_g, dkc, dkp, dvc, dvp = in_refs[5 * g:5 * g + 5]
            has_next = (i + d) < nblk
            for j in range(D // 128):
                lanes = slice(j * 128, (j + 1) * 128)
                out_lanes = slice(g * D + j * 128, g * D + (j + 1) * 128)
                v_lanes = slice(QW + g * D + j * 128, QW + g * D + (j + 1) * 128)
                dq_ref[:, out_lanes] = (unrope(dq_g[:, lanes], cs, sn) * scale).astype(dq_ref.dtype)
                dk = dkc[:, lanes] + jnp.where(has_next, dkp[:, lanes], 0.0)
                dkv_ref[:, out_lanes] = unrope(dk, cs, sn).astype(dkv_ref.dtype)
                dv = dvc[:, lanes] + jnp.where(has_next, dvp[:, lanes], 0.0)
                dkv_ref[:, v_lanes] = dv.astype(dkv_ref.dtype)

    here = pl.BlockSpec((BAND, D), lambda i: (i, 0))
    in_specs, operands = [], []
    for (_, d), branch in zip(BRANCHES, grads):
        later = pl.BlockSpec((BAND, D), lambda i, d=d: (jnp.minimum(i + d, nblk - 1), 0))
        in_specs += [here, here, later, here, later]
        operands += list(branch)
    tab = pl.BlockSpec((BAND, 128), lambda i: (i, 0))
    return pl.pallas_call(
        body, name=name, grid=(nblk,),
        in_specs=in_specs + [tab, tab],
        out_specs=[pl.BlockSpec((BAND, QW), lambda i: (i, 0)), pl.BlockSpec((BAND, 2 * QW), lambda i: (i, 0))],
        out_shape=[_sds((S, QW), BF16), _sds((S, 2 * QW), BF16)],
        compiler_params=_cparams(1),
    )(*operands, cos_t, sin_t)


def _adamw(name, parts, w, m, v):
    n, rows, cols = parts.shape
    tr = rows
    for cand in (256, 176, 128, 64, 32, 16, 8):
        if rows % cand == 0:
            tr = cand
            break
    c1 = 1.0 / (1.0 - ADAM_B1 ** ADAM_STEP)
    c2 = 1.0 / (1.0 - ADAM_B2 ** ADAM_STEP)

    def body(p_ref, w_ref, m_ref, v_ref, g_ref, d_ref, nm_ref, nv_ref):
        g = p_ref[0].astype(F32)
        for j in range(1, n):
            g = g + p_ref[j].astype(F32)
        nm = ADAM_B1 * m_ref[...] + (1.0 - ADAM_B1) * g
        nv = ADAM_B2 * v_ref[...] + (1.0 - ADAM_B2) * (g * g)
        g_ref[...] = g
        nm_ref[...] = nm
        nv_ref[...] = nv
        d_ref[...] = -ADAM_LR * ((nm * c1) / (jnp.sqrt(nv * c2) + ADAM_EPS) + ADAM_WD * w_ref[...])

    blk = pl.BlockSpec((tr, cols), lambda i: (i, 0))
    return pl.pallas_call(
        body, name=name, grid=(rows // tr,),
        in_specs=[pl.BlockSpec((n, tr, cols), lambda i: (0, i, 0)), blk, blk, blk],
        out_specs=[blk] * 4, out_shape=[_sds((rows, cols), F32)] * 4,
        compiler_params=_cparams(1),
    )(parts, w, m, v)


def _exchange(name, arrays, kind):
    n = len(arrays)
    gather = kind == "gather"
    out_shape = [_sds((NDEV,) + a.shape if gather else a.shape, a.dtype) for a in arrays]

    def body(*refs):
        srcs, outs = refs[:n], refs[n:2 * n]
        send_sems, recv_sems, local_sems = refs[2 * n:]
        x, y, c = lax.axis_index("x"), lax.axis_index("y"), lax.axis_index("c")
        me = 4 * x + 2 * y + c
        pending = []
        for t in range(n):
            own = pltpu.make_async_copy(srcs[t] if gather else srcs[t].at[me], outs[t].at[me], local_sems.at[t])
            own.start()
            pending.append(own)
            for rel in range(1, NDEV):
                px = 1 - x if rel & 4 else x
                py = 1 - y if rel & 2 else y
                pc = 1 - c if rel & 1 else c
                peer = 4 * px + 2 * py + pc
                send = pltpu.make_async_remote_copy(
                    src_ref=srcs[t] if gather else srcs[t].at[peer], dst_ref=outs[t].at[me],
                    send_sem=send_sems.at[t, rel - 1], recv_sem=recv_sems.at[t, rel - 1],
                    device_id=(px, py, pc), device_id_type=MESH)
                send.start()
                arrive = pltpu.make_async_remote_copy(
                    src_ref=srcs[t] if gather else srcs[t].at[me], dst_ref=outs[t].at[peer],
                    send_sem=send_sems.at[t, rel - 1], recv_sem=recv_sems.at[t, rel - 1],
                    device_id=(px, py, pc), device_id_type=MESH)
                pending.append((send, arrive))
        for item in pending:
            if isinstance(item, tuple):
                item[0].wait_send()
                item[1].wait_recv()
            else:
                item.wait()

    any_spec = pl.BlockSpec(memory_space=pl.ANY)
    outs = pl.pallas_call(
        body, name=name,
        in_specs=[any_spec] * n, out_specs=[any_spec] * n, out_shape=out_shape,
        scratch_shapes=[pltpu.SemaphoreType.DMA((n, NDEV - 1)), pltpu.SemaphoreType.DMA((n, NDEV - 1)),
                        pltpu.SemaphoreType.DMA((n,))],
    )(*arrays)
    return list(outs)


_HBM_SPEC = pl.BlockSpec(memory_space=pltpu.HBM)
_SEM_SPEC = pl.BlockSpec(memory_space=pltpu.SEMAPHORE)
_DATAFLOW = pltpu.SideEffectType.DATAFLOW_SIDE_EFFECTING


def _peers():
    x, y, c = lax.axis_index("x"), lax.axis_index("y"), lax.axis_index("c")
    out = []
    for rel in range(1, NDEV):
        px = 1 - x if rel & 4 else x
        py = 1 - y if rel & 2 else y
        pc = 1 - c if rel & 1 else c
        out.append((rel - 1, (px, py, pc), 4 * px + 2 * py + pc))
    return 4 * x + 2 * y + c, out


def _hbm(a):
    return pltpu.HBM(a.shape, a.dtype)


def _own_slot(a, me, kind):
    mine = a[None] if kind == "gather" else lax.dynamic_slice_in_dim(a, me, 1, axis=0)
    shape = (NDEV,) + mine.shape[1:]
    return lax.dynamic_update_slice_in_dim(lax.empty(shape, a.dtype), mine, me, axis=0)


def _exchange_start(name, arrays, me, kind):
    n = len(arrays)
    gather = kind == "gather"
    lands = [_own_slot(a, me, kind) for a in arrays]

    def body(*refs):
        src_refs, land_refs = refs[:n], refs[n:2 * n]
        send_sems, recv_sems = refs[2 * n], refs[2 * n + 1]
        my_block, peers = _peers()
        for t in range(n):
            for slot, dev, block in peers:
                pltpu.make_async_remote_copy(
                    src_ref=src_refs[t] if gather else src_refs[t].at[block], dst_ref=land_refs[t].at[my_block],
                    send_sem=send_sems.at[t * (NDEV - 1) + slot], recv_sem=recv_sems.at[t * (NDEV - 1) + slot],
                    device_id=dev, device_id_type=MESH).start()

    operands = [pltpu.with_memory_space_constraint(a, pltpu.HBM) for a in list(arrays) + lands]
    outs = pl.pallas_call(
        body, name=name,
        out_shape=(pltpu.SemaphoreType.DMA((n * (NDEV - 1),)), pltpu.SemaphoreType.DMA((n * (NDEV - 1),)),
                   *[_hbm(a) for a in operands]),
        in_specs=[_HBM_SPEC] * (2 * n), out_specs=(_SEM_SPEC, _SEM_SPEC, *[_HBM_SPEC] * (2 * n)),
        input_output_aliases={i: 2 + i for i in range(2 * n)},
        compiler_params=pltpu.CompilerParams(has_side_effects=_DATAFLOW),
    )(*operands)
    return outs[0], outs[1], list(outs[2:2 + n]), list(outs[2 + n:])


def _exchange_wait(name, started, t, after, kind):
    send_sems, recv_sems, srcs, lands = started
    gather = kind == "gather"

    def body(src_ref, land_ref, send_ref, recv_ref, after_ref, src_out, land_out):
        _, peers = _peers()
        for slot, dev, block in peers:
            copy = pltpu.make_async_remote_copy(
                src_ref=src_ref if gather else src_ref.at[block], dst_ref=land_ref.at[block],
                send_sem=send_ref.at[t * (NDEV - 1) + slot], recv_sem=recv_ref.at[t * (NDEV - 1) + slot],
                device_id=dev, device_id_type=MESH)
            copy.wait_send()
            copy.wait_recv()

    return pl.pallas_call(
        body, name=name, out_shape=(_hbm(srcs[t]), _hbm(lands[t])),
        in_specs=(_HBM_SPEC, _HBM_SPEC, _SEM_SPEC, _SEM_SPEC, pl.BlockSpec(memory_space=pl.ANY)),
        out_specs=(_HBM_SPEC, _HBM_SPEC), input_output_aliases={0: 0, 1: 1},
        compiler_params=pltpu.CompilerParams(has_side_effects=_DATAFLOW),
    )(srcs[t], lands[t], send_sems, recv_sems, after)[1]


def _ffn_fwd(tag, h, g_pre, weight):
    n = _rms_fwd(f"ffn_prenorm_{tag}", h, [g_pre])[0]
    wg = weight(f"gate_up_{tag}", n)
    gu = _fwd_cols_blocked(f"ffn_gate_up_{tag}", n, wg).reshape(2, NFB, S, FB)
    act = _swiglu_fwd(f"ffn_act_{tag}", gu)
    wd4 = weight(f"down_{tag}", act).reshape(NFB, FB, D)
    f = _fwd_kblocked(f"ffn_down_{tag}", act, wd4)
    return (n, gu, act, wg, wd4), f


def _ffn_bwd(tag, dh_out, h_in, f, saved, g_pre, g_post, send):
    n, gu, act, wg, wd4 = saved
    df, (dg_post,) = _rms_bwd(f"ffn_postnorm_bwd_{tag}", f, [(g_post, dh_out)], None, BF16)
    send(f"down_{tag}", _bwd_w_kblocked(f"ffn_down_dw_{tag}", act, df).reshape(NDEV, DFF // NDEV, D))
    da = _bwd_x_kblocked(f"ffn_down_dx_{tag}", df, wd4)
    dgu = _swiglu_bwd(f"ffn_act_bwd_{tag}", gu, da).reshape(NDEV, S, FB)
    send(f"gate_up_{tag}", _bwd_w_cols_blocked(f"ffn_gate_up_dw_{tag}", n, dgu))
    dn = _bwd_x_cols_blocked(f"ffn_gate_up_dx_{tag}", dgu, wg)
    dh_in, (dg_pre,) = _rms_bwd(f"ffn_prenorm_bwd_{tag}", h_in, [(g_pre, dn)], dh_out, F32)
    return dh_in, dg_pre, dg_post


def _to_classes(t, d):
    return t.reshape(S // d, d * D)


def kernel(x, positions, mix_norm_pre, mix_norm_post, ffn_norm_pre, ffn_norm_post, ffn_w_gate_up, ffn_w_down, conv_w_in, conv_w, conv_w_out, kv_norm, w_kv, w_q, w_o, loss_target, m_mix_norm_pre, m_mix_norm_post, m_ffn_norm_pre, m_ffn_norm_post, m_ffn_w_gate_up, m_ffn_w_down, m_conv_w_in, m_conv_w, m_conv_w_out, m_kv_norm, m_w_kv, m_w_q, m_w_o, v_mix_norm_pre, v_mix_norm_post, v_ffn_norm_pre, v_ffn_norm_post, v_ffn_w_gate_up, v_ffn_w_down, v_conv_w_in, v_conv_w, v_conv_w_out, v_kv_norm, v_w_kv, v_w_q, v_w_o):
    me = 4 * lax.axis_index("x") + 2 * lax.axis_index("y") + lax.axis_index("c")
    h0 = x.reshape(S, D)
    target = loss_target.reshape(S, D)
    row = lambda a, l: a[l].reshape(1, D)
    g_kv = kv_norm.reshape(1, D)

    cw_shard = jnp.pad(conv_w[0], ((0, 5), (0, 0)))
    names = ["conv_in", "conv_w", "conv_out", "gate_up_0", "down_0", "kv", "q", "o", "gate_up_1", "down_1"]
    shards = [conv_w_in[0], cw_shard, conv_w_out[0], ffn_w_gate_up[0], ffn_w_down[0],
              w_kv, w_q[0], w_o[0], ffn_w_gate_up[1], ffn_w_down[1]]
    shards = [s if n == "conv_w" else s.astype(BF16) for n, s in zip(names, shards)]
    gather = _exchange_start("gather_weights_start", shards, me, "gather")

    def weight(name, after):
        return _exchange_wait(f"gather_wait_{name}", gather, names.index(name), after, "gather")

    sent = {}

    def send(name, grad):
        sent[name] = _exchange_start(f"scatter_start_{name}", [grad], me, "scatter")

    n0 = _rms_fwd("mix_prenorm_0", h0, [row(mix_norm_pre, 0)])[0]
    win_g = weight("conv_in", n0)
    cw = weight("conv_w", n0).transpose(1, 0, 2).reshape(8, D)
    z = _fwd_cols("conv_in", n0, win_g)
    pre = _conv_fwd("conv_gate", z, cw)
    wout = weight("conv_out", pre).reshape(D, D)
    y0 = _fwd_rows("conv_out", pre, wout)
    h1 = _resid_rms("mix_postnorm_0", h0, y0, row(mix_norm_post, 0))
    ffn0, f0 = _ffn_fwd("0", h1, row(ffn_norm_pre, 0), weight)
    h2 = _resid_rms("ffn_postnorm_0", h1, f0, row(ffn_norm_post, 0))

    nk, n2 = _rms_fwd("kv_and_mix_prenorm_1", h2, [g_kv, row(mix_norm_pre, 1)])
    wkv_g = weight("kv", nk)
    kv = _fwd_cols("kv_proj", nk, wkv_g)
    wq_g = weight("q", kv)
    q_raw = _fwd_cols("q_proj", n2, wq_g)
    half = HEAD_DIM // 2
    inv_freq = ROPE_THETA ** (-jnp.arange(half, dtype=F32) / half)
    cos_t, sin_t = _rope_tables("rope_tables", positions.reshape(S, 1), jnp.tile(inv_freq, 4).reshape(1, 128))
    q = _rope_fwd("rope_q", q_raw, 0, cos_t, sin_t, HEAD_DIM ** -0.5)
    k = _rope_fwd("rope_k", kv, 0, cos_t, sin_t, 1.0)
    qc, kc, vc, lse_c, o_br, lse_br = [], [], [], [], [], []
    for g, (_, d) in enumerate(BRANCHES):
        qc.append(_to_classes(q[:, g * D:(g + 1) * D], d))
        kc.append(_to_classes(k[:, g * D:(g + 1) * D], d))
        vc.append(_to_classes(kv[:, QW + g * D:QW + (g + 1) * D], d))
        o_g, lse_g = _attn_fwd(f"attn_fwd_{g}", qc[g], kc[g], vc[g], d)
        lse_c.append(lse_g)
        o_br.append(o_g.reshape(S, D))
        lse_br.append(lse_g.reshape(S, D))
    o_mix = _mix_fwd("attn_mix", o_br, lse_br)
    wo = weight("o", o_mix).reshape(D, D)
    y1 = _fwd_rows("attn_out", o_mix, wo)
    h3 = _resid_rms("mix_postnorm_1", h2, y1, row(mix_norm_post, 1))
    ffn1, f1 = _ffn_fwd("1", h3, row(ffn_norm_pre, 1), weight)
    h4 = _resid_rms("ffn_postnorm_1", h3, f1, row(ffn_norm_post, 1))

    dh4, sq = _loss_grad("loss", h4, target)
    loss = lax.psum(jnp.sum(sq) * (0.5 / D), ("x", "y", "c"))

    dh3, dg_fpre1, dg_fpost1 = _ffn_bwd(
        "1", dh4, h3, f1, ffn1, row(ffn_norm_pre, 1), row(ffn_norm_post, 1), send)
    dy1, (dg_mpost1,) = _rms_bwd("mix_postnorm_bwd_1", y1, [(row(mix_norm_post, 1), dh3)], None, BF16)
    send("o", _bwd_w_rows("attn_out_dw", o_mix, dy1).reshape(NDEV, D // NDEV, D))
    do = _bwd_x_rows("attn_out_dx", dy1, wo, F32)
    lane = jnp.arange(128)
    ones_blockdiag = (lane[:, None] // HEAD_DIM == lane[None, :] // HEAD_DIM).astype(BF16)
    mixed = _mix_bwd("attn_mix_bwd", do, o_br, lse_br, ones_blockdiag)
    branch_grads = []
    for g, (_, d) in enumerate(BRANCHES):
        res = _attn_bwd(f"attn_bwd_{g}", qc[g], kc[g], vc[g], _to_classes(mixed[g], d),
                        lse_c[g], _to_classes(mixed[3 + g], d), d)
        branch_grads.append([r.reshape(S, D) for r in res])
    dq_raw, dkv = _attn_bwd_post("attn_bwd_post", branch_grads, cos_t, sin_t)
    send("q", _bwd_w_cols("q_proj_dw", n2, dq_raw, QW // NDEV))
    send("kv", _bwd_w_cols("kv_proj_dw", nk, dkv, 2 * QW // NDEV))
    dn2 = _bwd_x_cols("q_proj_dx", dq_raw, wq_g)
    dnk = _bwd_x_cols("kv_proj_dx", dkv, wkv_g)
    dh2, (dg_kv, dg_mpre1) = _rms_bwd("kv_and_mix_prenorm_bwd_1", h2,
                                      [(g_kv, dnk), (row(mix_norm_pre, 1), dn2)], dh3, F32)

    dh1, dg_fpre0, dg_fpost0 = _ffn_bwd(
        "0", dh2, h1, f0, ffn0, row(ffn_norm_pre, 0), row(ffn_norm_post, 0), send)
    dy0, (dg_mpost0,) = _rms_bwd("mix_postnorm_bwd_0", y0, [(row(mix_norm_post, 0), dh1)], None, BF16)
    send("conv_out", _bwd_w_rows("conv_out_dw", pre, dy0).reshape(NDEV, D // NDEV, D))
    dpre = _bwd_x_rows("conv_out_dx", dy0, wout, BF16)
    dz, dcw = _conv_bwd("conv_gate_bwd", z, dpre, cw)
    send("conv_in", _bwd_w_cols("conv_in_dw", n0, dz, 3 * D // NDEV))
    dn0 = _bwd_x_cols("conv_in_dx", dz, win_g)
    dh0, (dg_mpre0,) = _rms_bwd("mix_prenorm_bwd_0", h0, [(row(mix_norm_pre, 0), dn0)], dh1, F32)

    small = jnp.concatenate([dg_mpre0, dg_mpre1, dg_mpost0, dg_mpost1, dg_fpre0, dg_fpre1, dg_fpost0, dg_fpost1,
                             dg_kv, jnp.zeros((7, D), F32), dcw], axis=0)
    small_all = _exchange("gather_small_grads", [small], "gather")[0]

    done = [small_all]

    def upd(tag, w, m, v):
        parts = _exchange_wait(f"scatter_wait_{tag}", sent[tag], 0, done[-1], "scatter")
        shape = w.shape
        flat = lambda a: a.reshape(parts.shape[1:])
        res = _adamw(f"adamw_{tag}", parts, flat(w), flat(m), flat(v))
        done.append(res[0])
        return [r.reshape(shape) for r in res]

    def upd_layer(tag, l, w, m, v):
        return upd(f"{tag}_{l}", w[l], m[l], v[l])

    def stack(per_layer):
        return [jnp.stack([per_layer[0][i], per_layer[1][i]]) for i in range(4)]

    gains_w = jnp.concatenate([mix_norm_pre, mix_norm_post, ffn_norm_pre, ffn_norm_post, g_kv, jnp.zeros((15, D), F32)])
    gains_m = jnp.concatenate([m_mix_norm_pre, m_mix_norm_post, m_ffn_norm_pre, m_ffn_norm_post,
                               m_kv_norm.reshape(1, D), jnp.zeros((15, D), F32)])
    gains_v = jnp.concatenate([v_mix_norm_pre, v_mix_norm_post, v_ffn_norm_pre, v_ffn_norm_post,
                               v_kv_norm.reshape(1, D), jnp.ones((15, D), F32)])
    small_res = _adamw("adamw_gains", small_all, gains_w, gains_m, gains_v)
    dcw_mine = lax.dynamic_slice(small_res[0], (16, me * 128), (8, 128))
    pad8 = lambda a, fill: jnp.pad(a[0], ((0, 5), (0, 0)), constant_values=fill)
    cw_res = [r[0:3].reshape(1, 3, 128) for r in
              _adamw("adamw_conv_w", dcw_mine.reshape(1, 8, 128), cw_shard, pad8(m_conv_w, 0.0), pad8(v_conv_w, 1.0))]

    res = {
        "mix_norm_pre": [r[0:2] for r in small_res],
        "mix_norm_post": [r[2:4] for r in small_res],
        "ffn_norm_pre": [r[4:6] for r in small_res],
        "ffn_norm_post": [r[6:8] for r in small_res],
        "kv_norm": [r[8] for r in small_res],
        "conv_w": cw_res,
    }
    down, gate_up = {}, {}
    down[1] = upd_layer("down", 1, ffn_w_down, m_ffn_w_down, v_ffn_w_down)
    gate_up[1] = upd_layer("gate_up", 1, ffn_w_gate_up, m_ffn_w_gate_up, v_ffn_w_gate_up)
    res["w_o"] = upd("o", w_o, m_w_o, v_w_o)
    res["w_q"] = upd("q", w_q, m_w_q, v_w_q)
    res["w_kv"] = upd("kv", w_kv, m_w_kv, v_w_kv)
    down[0] = upd_layer("down", 0, ffn_w_down, m_ffn_w_down, v_ffn_w_down)
    gate_up[0] = upd_layer("gate_up", 0, ffn_w_gate_up, m_ffn_w_gate_up, v_ffn_w_gate_up)
    res["ffn_w_down"] = stack(down)
    res["ffn_w_gate_up"] = stack(gate_up)
    res["conv_w_out"] = upd("conv_out", conv_w_out, m_conv_w_out, v_conv_w_out)
    res["conv_w_in"] = upd("conv_in", conv_w_in, m_conv_w_in, v_conv_w_in)
    order = ["mix_norm_pre", "mix_norm_post", "ffn_norm_pre", "ffn_norm_post", "ffn_w_gate_up", "ffn_w_down",
             "conv_w_in", "conv_w", "conv_w_out", "kv_norm", "w_kv", "w_q", "w_o"]
    out = [loss, dh0.reshape(1, S, D)]
    for i in range(4):
        out += [res[name][i] for name in order]
    return tuple(out)
```

```python
import jax
import jax.numpy as jnp
from jax import lax
from jax.experimental import pallas as pl
from jax.experimental.pallas import tpu as pltpu

F32 = jnp.float32
BF16 = jnp.bfloat16

S = 4096
D = 1024
NDEV = 8
HEAD_DIM = 64
QW = 3072
DFF = 2816
FB = 704
NFB = 4
BRANCHES = ((128, 1), (512, 4), (2048, 16))
BAND = 128
ROPE_THETA = 10000.0
RMS_EPS = 1e-6
NEG_INF = -1e30
ADAM_LR, ADAM_B1, ADAM_B2, ADAM_EPS, ADAM_WD, ADAM_STEP = 0.001, 0.9, 0.999, 1e-08, 0.01, 10

VMEM_LIMIT_BYTES = 52 * 1024 * 1024
ROW_TILE = 512
MESH = pl.DeviceIdType.MESH


def _cparams(ngrid):
    return pltpu.CompilerParams(dimension_semantics=("arbitrary",) * ngrid,
                                vmem_limit_bytes=VMEM_LIMIT_BYTES)


def _sds(shape, dtype):
    return jax.ShapeDtypeStruct(tuple(shape), dtype)


_DIMS = {"nn": (((1,), (0,)), ((), ())),
         "nt": (((1,), (1,)), ((), ())),
         "tn": (((0,), (0,)), ((), ()))}


def _matmul(name, a, b, *, mode, grid, a_blk, a_map, b_blk, b_map, o_shape, o_blk, o_map, out_dtype, after=None):
    nk = grid[2]
    dims = _DIMS[mode]
    acc_shape = tuple(s for s in o_blk if s is not None)
    extra = [] if after is None else [after]

    def body(a_ref, b_ref, *rest):
        o_ref, scratch = rest[len(extra)], rest[len(extra) + 1:]
        part = lax.dot_general(a_ref[...], b_ref[...], dims, preferred_element_type=F32)
        if nk == 1:
            o_ref[...] = part.astype(o_ref.dtype)
            return
        acc_ref = scratch[0]
        k = pl.program_id(2)

        @pl.when(k == 0)
        def _():
            acc_ref[...] = part

        @pl.when(k > 0)
        def _():
            acc_ref[...] += part

        @pl.when(k == nk - 1)
        def _():
            o_ref[...] = acc_ref[...].astype(o_ref.dtype)

    return pl.pallas_call(
        body, name=name, grid=grid,
        in_specs=[pl.BlockSpec(a_blk, a_map), pl.BlockSpec(b_blk, b_map)] + [pl.BlockSpec(memory_space=pl.ANY)] * len(extra),
        out_specs=pl.BlockSpec(o_blk, o_map),
        out_shape=_sds(o_shape, out_dtype),
        scratch_shapes=[] if nk == 1 else [pltpu.VMEM(acc_shape, F32)],
        compiler_params=_cparams(3),
    )(a, b, *extra)


TM = 1024
TK = 1024


def _fwd_cols(name, a, wg, out_dtype=BF16):
    _, kdim, n = wg.shape
    return _matmul(name, a, wg, mode="nn", grid=(S // TM, NDEV, 1),
                   a_blk=(TM, kdim), a_map=lambda i, j, k: (i, 0),
                   b_blk=(None, kdim, n), b_map=lambda i, j, k: (j, 0, 0),
                   o_shape=(S, NDEV * n), o_blk=(TM, n), o_map=lambda i, j, k: (i, j), out_dtype=out_dtype)


def _fwd_cols_blocked(name, a, wg):
    _, kdim, n = wg.shape
    return _matmul(name, a, wg, mode="nn", grid=(S // TM, NDEV, 1),
                   a_blk=(TM, kdim), a_map=lambda i, j, k: (i, 0),
                   b_blk=(None, kdim, n), b_map=lambda i, j, k: (j, 0, 0),
                   o_shape=(NDEV, S, n), o_blk=(None, TM, n), o_map=lambda i, j, k: (j, i, 0), out_dtype=BF16)


def _fwd_rows(name, a, w, out_dtype=F32):
    kdim, n = w.shape
    tn = 512
    return _matmul(name, a, w, mode="nn", grid=(S // TM, n // tn, 1),
                   a_blk=(TM, kdim), a_map=lambda i, j, k: (i, 0),
                   b_blk=(kdim, tn), b_map=lambda i, j, k: (0, j),
                   o_shape=(S, n), o_blk=(TM, tn), o_map=lambda i, j, k: (i, j), out_dtype=out_dtype)


def _fwd_kblocked(name, a4, w4):
    nb, _, kb = a4.shape
    n = w4.shape[2]
    return _matmul(name, a4, w4, mode="nn", grid=(S // TM, 1, nb),
                   a_blk=(None, TM, kb), a_map=lambda i, j, k: (k, i, 0),
                   b_blk=(None, kb, n), b_map=lambda i, j, k: (k, 0, 0),
                   o_shape=(S, n), o_blk=(TM, n), o_map=lambda i, j, k: (i, 0), out_dtype=F32)


def _bwd_x_cols(name, dy, wg, after=None):
    _, kdim, n = wg.shape
    return _matmul(name, dy, wg, mode="nt", grid=(S // TM, 1, NDEV),
                   a_blk=(TM, n), a_map=lambda i, j, k: (i, k),
                   b_blk=(None, kdim, n), b_map=lambda i, j, k: (k, 0, 0),
                   o_shape=(S, kdim), o_blk=(TM, kdim), o_map=lambda i, j, k: (i, 0), out_dtype=F32, after=after)


def _bwd_x_cols_blocked(name, dy8, wg, after=None):
    _, kdim, n = wg.shape
    return _matmul(name, dy8, wg, mode="nt", grid=(S // TM, 1, NDEV),
                   a_blk=(None, TM, n), a_map=lambda i, j, k: (k, i, 0),
                   b_blk=(None, kdim, n), b_map=lambda i, j, k: (k, 0, 0),
                   o_shape=(S, kdim), o_blk=(TM, kdim), o_map=lambda i, j, k: (i, 0), out_dtype=F32, after=after)


def _bwd_x_rows(name, dy, w, out_dtype, after=None):
    kdim, n = w.shape
    tkk = 512
    return _matmul(name, dy, w, mode="nt", grid=(S // TM, kdim // tkk, 1),
                   a_blk=(TM, n), a_map=lambda i, j, k: (i, 0),
                   b_blk=(tkk, n), b_map=lambda i, j, k: (j, 0),
                   o_shape=(S, kdim), o_blk=(TM, tkk), o_map=lambda i, j, k: (i, j), out_dtype=out_dtype, after=after)


def _bwd_x_kblocked(name, dy, w4, after=None):
    nb, kb, n = w4.shape
    return _matmul(name, dy, w4, mode="nt", grid=(S // TM, nb, 1),
                   a_blk=(TM, n), a_map=lambda i, j, k: (i, 0),
                   b_blk=(None, kb, n), b_map=lambda i, j, k: (j, 0, 0),
                   o_shape=(nb, S, kb), o_blk=(None, TM, kb), o_map=lambda i, j, k: (j, i, 0), out_dtype=BF16, after=after)


def _bwd_w_cols(name, a, dy, n):
    kdim = a.shape[1]
    return _matmul(name, a, dy, mode="tn", grid=(1, NDEV, S // TK),
                   a_blk=(TK, kdim), a_map=lambda i, j, k: (k, 0),
                   b_blk=(TK, n), b_map=lambda i, j, k: (k, j),
                   o_shape=(NDEV, kdim, n), o_blk=(None, kdim, n), o_map=lambda i, j, k: (j, 0, 0), out_dtype=BF16)


def _bwd_w_cols_blocked(name, a, dy8):
    kdim = a.shape[1]
    n = dy8.shape[2]
    return _matmul(name, a, dy8, mode="tn", grid=(1, NDEV, S // TK),
                   a_blk=(TK, kdim), a_map=lambda i, j, k: (k, 0),
                   b_blk=(None, TK, n), b_map=lambda i, j, k: (j, k, 0),
                   o_shape=(NDEV, kdim, n), o_blk=(None, kdim, n), o_map=lambda i, j, k: (j, 0, 0), out_dtype=BF16)


def _bwd_w_rows(name, a, dy):
    kdim = a.shape[1]
    n = dy.shape[1]
    tmm = 512
    return _matmul(name, a, dy, mode="tn", grid=(kdim // tmm, 1, S // TK),
                   a_blk=(TK, tmm), a_map=lambda i, j, k: (k, i),
                   b_blk=(TK, n), b_map=lambda i, j, k: (k, 0),
                   o_shape=(kdim, n), o_blk=(tmm, n), o_map=lambda i, j, k: (i, 0), out_dtype=BF16)


def _bwd_w_kblocked(name, a4, dy):
    nb, _, kb = a4.shape
    n = dy.shape[1]
    return _matmul(name, a4, dy, mode="tn", grid=(nb, 1, S // TK),
                   a_blk=(None, TK, kb), a_map=lambda i, j, k: (i, k, 0),
                   b_blk=(TK, n), b_map=lambda i, j, k: (k, 0),
                   o_shape=(nb, kb, n), o_blk=(None, kb, n), o_map=lambda i, j, k: (i, 0, 0), out_dtype=BF16)


def _rstd(x):
    return lax.rsqrt(jnp.mean(x * x, axis=-1, keepdims=True) + RMS_EPS)


def _row_spec(tm=ROW_TILE, width=D):
    return pl.BlockSpec((tm, width), lambda i: (i, 0))


def _vec_spec(rows=1, width=D):
    return pl.BlockSpec((rows, width), lambda i: (0, 0))


def _rms_fwd(name, x, gains):
    n = len(gains)

    def body(x_ref, *refs):
        x_val = x_ref[...]
        xh = x_val * _rstd(x_val)
        for g_ref, o_ref in zip(refs[:n], refs[n:]):
            o_ref[...] = (xh * g_ref[...]).astype(o_ref.dtype)

    outs = pl.pallas_call(
        body, name=name, grid=(S // ROW_TILE,),
        in_specs=[_row_spec()] + [_vec_spec()] * n,
        out_specs=[_row_spec()] * n,
        out_shape=[_sds((S, D), BF16)] * n,
        compiler_params=_cparams(1),
    )(x, *gains)
    return list(outs)


def _resid_rms(name, h, y, g):
    def body(h_ref, y_ref, g_ref, o_ref):
        y_val = y_ref[...]
        o_ref[...] = h_ref[...] + (y_val * _rstd(y_val)) * g_ref[...]

    return pl.pallas_call(
        body, name=name, grid=(S // ROW_TILE,),
        in_specs=[_row_spec(), _row_spec(), _vec_spec()],
        out_specs=_row_spec(), out_shape=_sds((S, D), F32),
        compiler_params=_cparams(1),
    )(h, y, g)


def _rms_bwd(name, x, pairs, dres, out_dtype):
    n = len(pairs)
    has_res = dres is not None

    def body(x_ref, *refs):
        g_refs = refs[0:2 * n:2]
        dn_refs = refs[1:2 * n:2]
        pos = 2 * n
        res_ref = refs[pos] if has_res else None
        pos += int(has_res)
        dx_ref = refs[pos]
        dg_refs = refs[pos + 1:]
        step = pl.program_id(0)
        x_val = x_ref[...]
        r = _rstd(x_val)
        xh = x_val * r
        acc = res_ref[...] if has_res else jnp.zeros_like(x_val)
        for g_ref, dn_ref, dg_ref in zip(g_refs, dn_refs, dg_refs):
            dn = dn_ref[...].astype(F32)
            dxh = dn * g_ref[...]
            acc = acc + r * (dxh - xh * jnp.mean(dxh * xh, axis=-1, keepdims=True))
            part = jnp.sum(dn * xh, axis=0, keepdims=True)

            @pl.when(step == 0)
            def _():
                dg_ref[...] = part

            @pl.when(step > 0)
            def _():
                dg_ref[...] += part

        dx_ref[...] = acc.astype(dx_ref.dtype)

    operands = [x]
    in_specs = [_row_spec()]
    for g, dn in pairs:
        operands += [g, dn]
        in_specs += [_vec_spec(), _row_spec()]
    if has_res:
        operands.append(dres)
        in_specs.append(_row_spec())
    outs = pl.pallas_call(
        body, name=name, grid=(S // ROW_TILE,),
        in_specs=in_specs,
        out_specs=[_row_spec()] + [_vec_spec()] * n,
        out_shape=[_sds((S, D), out_dtype)] + [_sds((1, D), F32)] * n,
        compiler_params=_cparams(1),
    )(*operands)
    return outs[0], list(outs[1:])


def _loss_grad(name, h, target):
    def body(h_ref, t_ref, dh_ref, part_ref):
        e = h_ref[...] - t_ref[...]
        dh_ref[...] = e * (1.0 / D)
        part = jnp.sum(e * e, axis=0, keepdims=True)
        step = pl.program_id(0)

        @pl.when(step == 0)
        def _():
            part_ref[...] = part

        @pl.when(step > 0)
        def _():
            part_ref[...] += part

    return pl.pallas_call(
        body, name=name, grid=(S // ROW_TILE,),
        in_specs=[_row_spec(), _row_spec()],
        out_specs=[_row_spec(), _vec_spec()],
        out_shape=[_sds((S, D), F32), _sds((1, D), F32)],
        compiler_params=_cparams(1),
    )(h, target)


def _shift_down(u, prev8, k):
    r = pltpu.roll(u, k, 0)
    p = pltpu.roll(prev8, k, 0)
    row = lax.broadcasted_iota(jnp.int32, prev8.shape, 0)
    top = jnp.where(row < k, p, r[0:8])
    return jnp.concatenate([top, r[8:]], axis=0)


def _shift_up(u, next8, k):
    tm = u.shape[0]
    r = pltpu.roll(u, tm - k, 0)
    p = pltpu.roll(next8, 8 - k, 0)
    row = lax.broadcasted_iota(jnp.int32, next8.shape, 0)
    bot = jnp.where(row >= 8 - k, p, r[tm - 8:tm])
    return jnp.concatenate([r[:tm - 8], bot], axis=0)


CONV_TILE = 512


def _halo_prev(col):
    return pl.BlockSpec((8, D), lambda i: (jnp.maximum(i * (CONV_TILE // 8) - 1, 0), col))


def _halo_next(col):
    last = S // 8 - 1
    return pl.BlockSpec((8, D), lambda i: (jnp.minimum((i + 1) * (CONV_TILE // 8), last), col))


def _conv_fwd(name, z, cw):
    def body(b_ref, c_ref, h_ref, cp_ref, hp_ref, cw_ref, o_ref):
        i = pl.program_id(0)
        u = c_ref[...].astype(F32) * h_ref[...].astype(F32)
        up = cp_ref[...].astype(F32) * hp_ref[...].astype(F32)
        up = jnp.where(i > 0, up, 0.0)
        cv = cw_ref[0:1, :] * _shift_down(u, up, 2) + cw_ref[1:2, :] * _shift_down(u, up, 1) + cw_ref[2:3, :] * u
        o_ref[...] = (b_ref[...].astype(F32) * cv).astype(o_ref.dtype)

    col = lambda c: pl.BlockSpec((CONV_TILE, D), lambda i: (i, c))
    return pl.pallas_call(
        body, name=name, grid=(S // CONV_TILE,),
        in_specs=[col(0), col(1), col(2), _halo_prev(1), _halo_prev(2), _vec_spec(8)],
        out_specs=_row_spec(CONV_TILE), out_shape=_sds((S, D), BF16),
        compiler_params=_cparams(1),
    )(z, z, z, z, z, cw)


def _conv_bwd(name, z, dpre, cw):
    nsteps = S // CONV_TILE

    def body(b_ref, c_ref, h_ref, cp_ref, hp_ref, dp_ref, dpn_ref, bn_ref, cw_ref, dz_ref, dcw_ref):
        i = pl.program_id(0)
        b = b_ref[...].astype(F32)
        c = c_ref[...].astype(F32)
        h = h_ref[...].astype(F32)
        dp = dp_ref[...].astype(F32)
        u = c * h
        up = jnp.where(i > 0, cp_ref[...].astype(F32) * hp_ref[...].astype(F32), 0.0)
        s1 = _shift_down(u, up, 1)
        s2 = _shift_down(u, up, 2)
        w0, w1, w2 = cw_ref[0:1, :], cw_ref[1:2, :], cw_ref[2:3, :]
        cv = w0 * s2 + w1 * s1 + w2 * u
        dcv = dp * b
        dcvn = jnp.where(i < nsteps - 1, dpn_ref[...].astype(F32) * bn_ref[...].astype(F32), 0.0)
        du = w2 * dcv + w1 * _shift_up(dcv, dcvn, 1) + w0 * _shift_up(dcv, dcvn, 2)
        dz_ref[:, 0:D] = (dp * cv).astype(dz_ref.dtype)
        dz_ref[:, D:2 * D] = (du * h).astype(dz_ref.dtype)
        dz_ref[:, 2 * D:3 * D] = (du * c).astype(dz_ref.dtype)

        @pl.when(i == 0)
        def _():
            dcw_ref[...] = jnp.zeros_like(dcw_ref)

        dcw_ref[0:1, :] += jnp.sum(dcv * s2, axis=0, keepdims=True)
        dcw_ref[1:2, :] += jnp.sum(dcv * s1, axis=0, keepdims=True)
        dcw_ref[2:3, :] += jnp.sum(dcv * u, axis=0, keepdims=True)

    col = lambda c: pl.BlockSpec((CONV_TILE, D), lambda i: (i, c))
    return pl.pallas_call(
        body, name=name, grid=(nsteps,),
        in_specs=[col(0), col(1), col(2), _halo_prev(1), _halo_prev(2),
                  _row_spec(CONV_TILE), _halo_next(0), _halo_next(0), _vec_spec(8)],
        out_specs=[pl.BlockSpec((CONV_TILE, 3 * D), lambda i: (i, 0)), _vec_spec(8)],
        out_shape=[_sds((S, 3 * D), BF16), _sds((8, D), F32)],
        compiler_params=_cparams(1),
    )(z, z, z, z, z, dpre, dpre, z, cw)


def _swiglu_fwd(name, gu):
    def body(gu_ref, o_ref):
        g = gu_ref[0].astype(F32)
        u = gu_ref[1].astype(F32)
        o_ref[...] = (g * jax.nn.sigmoid(g) * u).astype(o_ref.dtype)

    return pl.pallas_call(
        body, name=name, grid=(NFB, S // ROW_TILE),
        in_specs=[pl.BlockSpec((2, None, ROW_TILE, FB), lambda j, i: (0, j, i, 0))],
        out_specs=pl.BlockSpec((None, ROW_TILE, FB), lambda j, i: (j, i, 0)),
        out_shape=_sds((NFB, S, FB), BF16),
        compiler_params=_cparams(2),
    )(gu)


def _swiglu_bwd(name, gu, da):
    def body(gu_ref, da_ref, o_ref):
        g = gu_ref[0].astype(F32)
        u = gu_ref[1].astype(F32)
        d = da_ref[...].astype(F32)
        sg = jax.nn.sigmoid(g)
        o_ref[0] = (d * u * sg * (1.0 + g * (1.0 - sg))).astype(o_ref.dtype)
        o_ref[1] = (d * g * sg).astype(o_ref.dtype)

    blk = pl.BlockSpec((2, None, ROW_TILE, FB), lambda j, i: (0, j, i, 0))
    return pl.pallas_call(
        body, name=name, grid=(NFB, S // ROW_TILE),
        in_specs=[blk, pl.BlockSpec((None, ROW_TILE, FB), lambda j, i: (j, i, 0))],
        out_specs=blk, out_shape=_sds((2, NFB, S, FB), BF16),
        compiler_params=_cparams(2),
    )(gu, da)


def _rope_tables(name, pos_col, inv_freq_row):
    def body(pos_ref, f_ref, cos_ref, sin_ref):
        ang = pos_ref[...].astype(F32) * f_ref[...]
        lane = lax.broadcasted_iota(jnp.int32, ang.shape, 1)
        s = jnp.sin(ang)
        cos_ref[...] = jnp.cos(ang)
        sin_ref[...] = jnp.where((lane % HEAD_DIM) < HEAD_DIM // 2, -s, s)

    tab = pl.BlockSpec((ROW_TILE, 128), lambda i: (i, 0))
    return pl.pallas_call(
        body, name=name, grid=(S // ROW_TILE,),
        in_specs=[pl.BlockSpec((ROW_TILE, 1), lambda i: (i, 0)), _vec_spec(1, 128)],
        out_specs=[tab, tab], out_shape=[_sds((S, 128), F32)] * 2,
        compiler_params=_cparams(1),
    )(pos_col, inv_freq_row)


def _swap_halves(t):
    lane = lax.broadcasted_iota(jnp.int32, t.shape, 1)
    first = (lane % HEAD_DIM) < HEAD_DIM // 2
    return jnp.where(first, pltpu.roll(t, 128 - HEAD_DIM // 2, 1), pltpu.roll(t, HEAD_DIM // 2, 1))


ROPE_COLS = 768


def _rope_fwd(name, t, col_off, cos_t, sin_t, scale):
    def body(t_ref, cos_ref, sin_ref, o_ref):
        cs = cos_ref[...]
        sn = sin_ref[...]
        for j in range(ROPE_COLS // 128):
            x = t_ref[:, j * 128:(j + 1) * 128].astype(F32)
            o_ref[:, j * 128:(j + 1) * 128] = ((x * cs + _swap_halves(x) * sn) * scale).astype(o_ref.dtype)

    off = col_off // ROPE_COLS
    tab = pl.BlockSpec((ROW_TILE, 128), lambda i, j: (i, 0))
    return pl.pallas_call(
        body, name=name, grid=(S // ROW_TILE, QW // ROPE_COLS),
        in_specs=[pl.BlockSpec((ROW_TILE, ROPE_COLS), lambda i, j: (i, j + off)), tab, tab],
        out_specs=pl.BlockSpec((ROW_TILE, ROPE_COLS), lambda i, j: (i, j)),
        out_shape=_sds((S, QW), BF16),
        compiler_params=_cparams(2),
    )(t, cos_t, sin_t)


def _attn_tile(d):
    return min(512, S // d)


def _attn_specs(d):
    tq = _attn_tile(d)
    main = pl.BlockSpec((tq, 128), lambda cb, n: (n, cb))
    prev = pl.BlockSpec((BAND, 128), lambda cb, n: (jnp.maximum(n * (tq // BAND) - 1, 0), cb))
    return tq, main, prev


def _dot_nt(a, b):
    return lax.dot_general(a, b, _DIMS["nt"], preferred_element_type=F32)


def _dot_tn(a, b):
    return lax.dot_general(a, b, _DIMS["tn"], preferred_element_type=F32)


def _dot_nn(a, b):
    return lax.dot_general(a, b, _DIMS["nn"], preferred_element_type=F32)


def _band_masks():
    qi = lax.broadcasted_iota(jnp.int32, (BAND, BAND), 0)
    kj = lax.broadcasted_iota(jnp.int32, (BAND, BAND), 1)
    return kj >= qi, kj <= qi, kj < HEAD_DIM


def _attn_fwd(name, q, k, v, d):
    tq, main, prev = _attn_specs(d)
    nsb = tq // BAND

    def body(q_ref, k_ref, kp_ref, v_ref, vp_ref, o_ref, lse_ref):
        n = pl.program_id(1)
        m_prev, m_cur, first_head = _band_masks()
        for sb in range(nsb):
            rows = slice(sb * BAND, (sb + 1) * BAND)
            qb = q_ref[rows, :]
            kc = k_ref[rows, :]
            vc = v_ref[rows, :]
            if sb == 0:
                kp, vp = kp_ref[...], vp_ref[...]
                ok_prev = m_prev & (n > 0)
            else:
                before = slice((sb - 1) * BAND, sb * BAND)
                kp, vp = k_ref[before, :], v_ref[before, :]
                ok_prev = m_prev
            o_heads, lse_heads = [], []
            for head_lanes in (first_head, ~first_head):
                qh = jnp.where(head_lanes, qb, jnp.zeros_like(qb))
                sp = jnp.where(ok_prev, _dot_nt(qh, kp), NEG_INF)
                sc = jnp.where(m_cur, _dot_nt(qh, kc), NEG_INF)
                m = jnp.maximum(jnp.max(sp, axis=-1, keepdims=True), jnp.max(sc, axis=-1, keepdims=True))
                pp = jnp.exp(sp - m)
                pc = jnp.exp(sc - m)
                l = jnp.sum(pp, axis=-1, keepdims=True) + jnp.sum(pc, axis=-1, keepdims=True)
                o_heads.append((_dot_nn(pp.astype(BF16), vp) + _dot_nn(pc.astype(BF16), vc)) / l)
                lse_heads.append(jnp.broadcast_to(m + jnp.log(l), (BAND, 128)))
            o_ref[rows, :] = jnp.where(first_head, o_heads[0], o_heads[1])
            lse_ref[rows, :] = jnp.where(first_head, lse_heads[0], lse_heads[1])

    rdim = S // d
    return pl.pallas_call(
        body, name=name, grid=(8 * d, rdim // tq),
        in_specs=[main, main, prev, main, prev],
        out_specs=[main, main],
        out_shape=[_sds((rdim, d * D), F32)] * 2,
        compiler_params=_cparams(2),
    )(q, k, k, v, v)


def _attn_bwd(name, q, k, v, do, lse, dd, d):
    tq, main, prev = _attn_specs(d)
    nsb = tq // BAND

    def body(q_ref, k_ref, kp_ref, v_ref, vp_ref, do_ref, lse_ref, dd_ref,
             dq_ref, dkc_ref, dkp_ref, dvc_ref, dvp_ref):
        n = pl.program_id(1)
        m_prev, m_cur, first_head = _band_masks()
        for sb in range(nsb):
            rows = slice(sb * BAND, (sb + 1) * BAND)
            qb = q_ref[rows, :]
            dob = do_ref[rows, :]
            kc = k_ref[rows, :]
            vc = v_ref[rows, :]
            if sb == 0:
                kp, vp = kp_ref[...], vp_ref[...]
                ok_prev = m_prev & (n > 0)
            else:
                before = slice((sb - 1) * BAND, sb * BAND)
                kp, vp = k_ref[before, :], v_ref[before, :]
                ok_prev = m_prev
            dq_heads = []
            dkc = dkp = dvc = dvp = None
            for hi, head_lanes in enumerate((first_head, ~first_head)):
                col = slice(hi * HEAD_DIM, hi * HEAD_DIM + 1)
                lse_col = lse_ref[rows, col]
                dd_col = dd_ref[rows, col]
                qh = jnp.where(head_lanes, qb, jnp.zeros_like(qb))
                doh = jnp.where(head_lanes, dob, jnp.zeros_like(dob))
                sp = jnp.where(ok_prev, _dot_nt(qh, kp), NEG_INF)
                sc = jnp.where(m_cur, _dot_nt(qh, kc), NEG_INF)
                pp = jnp.exp(sp - lse_col)
                pc = jnp.exp(sc - lse_col)
                dsp = (pp * (_dot_nt(doh, vp) - dd_col)).astype(BF16)
                dsc = (pc * (_dot_nt(doh, vc) - dd_col)).astype(BF16)
                dq_heads.append(_dot_nn(dsp, kp) + _dot_nn(dsc, kc))
                parts = (_dot_tn(dsc, qh), _dot_tn(dsp, qh),
                         _dot_tn(pc.astype(BF16), doh), _dot_tn(pp.astype(BF16), doh))
                if hi == 0:
                    dkc, dkp, dvc, dvp = parts
                else:
                    dkc, dkp, dvc, dvp = dkc + parts[0], dkp + parts[1], dvc + parts[2], dvp + parts[3]
            dq_ref[rows, :] = jnp.where(first_head, dq_heads[0], dq_heads[1])
            dkc_ref[rows, :] = dkc
            dkp_ref[rows, :] = dkp
            dvc_ref[rows, :] = dvc
            dvp_ref[rows, :] = dvp

    rdim = S // d
    return pl.pallas_call(
        body, name=name, grid=(8 * d, rdim // tq),
        in_specs=[main, main, prev, main, prev, main, main, main],
        out_specs=[main] * 5,
        out_shape=[_sds((rdim, d * D), F32)] * 5,
        compiler_params=_cparams(2),
    )(q, k, k, v, v, do, lse, dd)


def _mix_fwd(name, outs, lses):
    def body(o0, o1, o2, l0, l1, l2, o_ref):
        a, b, c = l0[...], l1[...], l2[...]
        m = jnp.maximum(jnp.maximum(a, b), c)
        ea, eb, ec = jnp.exp(a - m), jnp.exp(b - m), jnp.exp(c - m)
        o_ref[...] = ((ea * o0[...] + eb * o1[...] + ec * o2[...]) / (ea + eb + ec)).astype(o_ref.dtype)

    return pl.pallas_call(
        body, name=name, grid=(S // ROW_TILE,),
        in_specs=[_row_spec()] * 6, out_specs=_row_spec(), out_shape=_sds((S, D), BF16),
        compiler_params=_cparams(1),
    )(*outs, *lses)


def _head_sums(x, ones_blockdiag):
    cols = []
    for j in range(D // 128):
        xj = x[:, j * 128:(j + 1) * 128]
        hi = xj.astype(BF16)
        r1 = xj - hi.astype(F32)
        mid = r1.astype(BF16)
        lo = (r1 - mid.astype(F32)).astype(BF16)
        cols.append(_dot_nn(hi, ones_blockdiag) + _dot_nn(mid, ones_blockdiag) + _dot_nn(lo, ones_blockdiag))
    return jnp.concatenate(cols, axis=1)


def _mix_bwd(name, do, outs, lses, ones_blockdiag):
    tm = 256

    def body(do_ref, o0, o1, o2, l0, l1, l2, ones_ref, d0, d1, d2, t0, t1, t2):
        a, b, c = l0[...], l1[...], l2[...]
        m = jnp.maximum(jnp.maximum(a, b), c)
        ea, eb, ec = jnp.exp(a - m), jnp.exp(b - m), jnp.exp(c - m)
        den = ea + eb + ec
        wa, wb, wc = ea / den, eb / den, ec / den
        dov = do_ref[...]
        o = wa * o0[...] + wb * o1[...] + wc * o2[...]
        t = _head_sums(dov * o, ones_ref[...])
        for w, d_ref, t_ref in ((wa, d0, t0), (wb, d1, t1), (wc, d2, t2)):
            d_ref[...] = (w * dov).astype(d_ref.dtype)
            t_ref[...] = w * t

    return pl.pallas_call(
        body, name=name, grid=(S // tm,),
        in_specs=[_row_spec(tm)] * 7 + [_vec_spec(128, 128)],
        out_specs=[_row_spec(tm)] * 6,
        out_shape=[_sds((S, D), BF16)] * 3 + [_sds((S, D), F32)] * 3,
        compiler_params=_cparams(1),
    )(do, *outs, *lses, ones_blockdiag)


def _attn_bwd_post(name, grads, cos_t, sin_t):
    nblk = S // BAND
    scale = HEAD_DIM ** -0.5

    def unrope(x, cs, sn):
        return x * cs - _swap_halves(x) * sn

    def body(*refs):
        in_refs = refs[:15]
        cos_ref, sin_ref, dq_ref, dkv_ref = refs[15:]
        i = pl.program_id(0)
        cs = cos_ref[...]
        sn = sin_ref[...]
        for g, (_, d) in enumerate(BRANCHES):
            dq_g, dkc, dkp, dvc, dvp = in_refs[5 * g:5 * g + 5]
            has_next = (i + d) < nblk
            for j in range(D // 128):
                lanes = slice(j * 128, (j + 1) * 128)
                out_lanes = slice(g * D + j * 128, g * D + (j + 1) * 128)
                v_lanes = slice(QW + g * D + j * 128, QW + g * D + (j + 1) * 128)
                dq_ref[:, out_lanes] = (unrope(dq_g[:, lanes], cs, sn) * scale).astype(dq_ref.dtype)
                dk = dkc[:, lanes] + jnp.where(has_next, dkp[:, lanes], 0.0)
                dkv_ref[:, out_lanes] = unrope(dk, cs, sn).astype(dkv_ref.dtype)
                dv = dvc[:, lanes] + jnp.where(has_next, dvp[:, lanes], 0.0)
                dkv_ref[:, v_lanes] = dv.astype(dkv_ref.dtype)

    here = pl.BlockSpec((BAND, D), lambda i: (i, 0))
    in_specs, operands = [], []
    for (_, d), branch in zip(BRANCHES, grads):
        later = pl.BlockSpec((BAND, D), lambda i, d=d: (jnp.minimum(i + d, nblk - 1), 0))
        in_specs += [here, here, later, here, later]
        operands += list(branch)
    tab = pl.BlockSpec((BAND, 128), lambda i: (i, 0))
    return pl.pallas_call(
        body, name=name, grid=(nblk,),
        in_specs=in_specs + [tab, tab],
        out_specs=[pl.BlockSpec((BAND, QW), lambda i: (i, 0)), pl.BlockSpec((BAND, 2 * QW), lambda i: (i, 0))],
        out_shape=[_sds((S, QW), BF16), _sds((S, 2 * QW), BF16)],
        compiler_params=_cparams(1),
    )(*operands, cos_t, sin_t)


def _adamw(name, parts, w, m, v):
    n, rows, cols = parts.shape
    tr = rows
    for cand in (256, 176, 128, 64, 32, 16, 8):
        if rows % cand == 0:
            tr = cand
            break
    c1 = 1.0 / (1.0 - ADAM_B1 ** ADAM_STEP)
    c2 = 1.0 / (1.0 - ADAM_B2 ** ADAM_STEP)

    def body(p_ref, w_ref, m_ref, v_ref, g_ref, d_ref, nm_ref, nv_ref):
        g = p_ref[0].astype(F32)
        for j in range(1, n):
            g = g + p_ref[j].astype(F32)
        nm = ADAM_B1 * m_ref[...] + (1.0 - ADAM_B1) * g
        nv = ADAM_B2 * v_ref[...] + (1.0 - ADAM_B2) * (g * g)
        g_ref[...] = g
        nm_ref[...] = nm
        nv_ref[...] = nv
        d_ref[...] = -ADAM_LR * ((nm * c1) / (jnp.sqrt(nv * c2) + ADAM_EPS) + ADAM_WD * w_ref[...])

    blk = pl.BlockSpec((tr, cols), lambda i: (i, 0))
    return pl.pallas_call(
        body, name=name, grid=(rows // tr,),
        in_specs=[pl.BlockSpec((n, tr, cols), lambda i: (0, i, 0)), blk, blk, blk],
        out_specs=[blk] * 4, out_shape=[_sds((rows, cols), F32)] * 4,
        compiler_params=_cparams(1),
    )(parts, w, m, v)


def _exchange(name, arrays, kind):
    n = len(arrays)
    gather = kind == "gather"
    out_shape = [_sds((NDEV,) + a.shape if gather else a.shape, a.dtype) for a in arrays]

    def body(*refs):
        srcs, outs = refs[:n], refs[n:2 * n]
        send_sems, recv_sems, local_sems = refs[2 * n:]
        x, y, c = lax.axis_index("x"), lax.axis_index("y"), lax.axis_index("c")
        me = 4 * x + 2 * y + c
        pending = []
        for t in range(n):
            own = pltpu.make_async_copy(srcs[t] if gather else srcs[t].at[me], outs[t].at[me], local_sems.at[t])
            own.start()
            pending.append(own)
            for rel in range(1, NDEV):
                px = 1 - x if rel & 4 else x
                py = 1 - y if rel & 2 else y
                pc = 1 - c if rel & 1 else c
                peer = 4 * px + 2 * py + pc
                send = pltpu.make_async_remote_copy(
                    src_ref=srcs[t] if gather else srcs[t].at[peer], dst_ref=outs[t].at[me],
                    send_sem=send_sems.at[t, rel - 1], recv_sem=recv_sems.at[t, rel - 1],
                    device_id=(px, py, pc), device_id_type=MESH)
                send.start()
                arrive = pltpu.make_async_remote_copy(
                    src_ref=srcs[t] if gather else srcs[t].at[me], dst_ref=outs[t].at[peer],
                    send_sem=send_sems.at[t, rel - 1], recv_sem=recv_sems.at[t, rel - 1],
                    device_id=(px, py, pc), device_id_type=MESH)
                pending.append((send, arrive))
        for item in pending:
            if isinstance(item, tuple):
                item[0].wait_send()
                item[1].wait_recv()
            else:
                item.wait()

    any_spec = pl.BlockSpec(memory_space=pl.ANY)
    outs = pl.pallas_call(
        body, name=name,
        in_specs=[any_spec] * n, out_specs=[any_spec] * n, out_shape=out_shape,
        scratch_shapes=[pltpu.SemaphoreType.DMA((n, NDEV - 1)), pltpu.SemaphoreType.DMA((n, NDEV - 1)),
                        pltpu.SemaphoreType.DMA((n,))],
    )(*arrays)
    return list(outs)


_HBM_SPEC = pl.BlockSpec(memory_space=pltpu.HBM)
_SEM_SPEC = pl.BlockSpec(memory_space=pltpu.SEMAPHORE)
_DATAFLOW = pltpu.SideEffectType.DATAFLOW_SIDE_EFFECTING


def _peers():
    x, y, c = lax.axis_index("x"), lax.axis_index("y"), lax.axis_index("c")
    out = []
    for rel in range(1, NDEV):
        px = 1 - x if rel & 4 else x
        py = 1 - y if rel & 2 else y
        pc = 1 - c if rel & 1 else c
        out.append((rel - 1, (px, py, pc), 4 * px + 2 * py + pc))
    return 4 * x + 2 * y + c, out


def _hbm(a):
    return pltpu.HBM(a.shape, a.dtype)


def _own_slot(a, me, kind):
    mine = a[None] if kind == "gather" else lax.dynamic_slice_in_dim(a, me, 1, axis=0)
    shape = (NDEV,) + mine.shape[1:]
    return lax.dynamic_update_slice_in_dim(lax.empty(shape, a.dtype), mine, me, axis=0)


def _exchange_start(name, arrays, me, kind):
    n = len(arrays)
    gather = kind == "gather"
    lands = [_own_slot(a, me, kind) for a in arrays]

    def body(*refs):
        src_refs, land_refs = refs[:n], refs[n:2 * n]
        send_sems, recv_sems = refs[2 * n], refs[2 * n + 1]
        token = refs[-1]
        my_block, peers = _peers()
        for t in range(n):
            for slot, dev, block in peers:
                pltpu.make_async_remote_copy(
                    src_ref=src_refs[t] if gather else src_refs[t].at[block], dst_ref=land_refs[t].at[my_block],
                    send_sem=send_sems.at[t * (NDEV - 1) + slot], recv_sem=recv_sems.at[t * (NDEV - 1) + slot],
                    device_id=dev, device_id_type=MESH).start()
        token[...] = jnp.zeros_like(token)

    operands = [pltpu.with_memory_space_constraint(a, pltpu.HBM) for a in list(arrays) + lands]
    outs = pl.pallas_call(
        body, name=name,
        out_shape=(pltpu.SemaphoreType.DMA((n * (NDEV - 1),)), pltpu.SemaphoreType.DMA((n * (NDEV - 1),)),
                   *[_hbm(a) for a in operands], _sds((8, 128), F32)),
        in_specs=[_HBM_SPEC] * (2 * n),
        out_specs=(_SEM_SPEC, _SEM_SPEC, *[_HBM_SPEC] * (2 * n), pl.BlockSpec(memory_space=pltpu.VMEM)),
        input_output_aliases={i: 2 + i for i in range(2 * n)},
        compiler_params=pltpu.CompilerParams(has_side_effects=_DATAFLOW),
    )(*operands)
    return (outs[0], outs[1], list(outs[2:2 + n]), list(outs[2 + n:2 + 2 * n])), outs[-1]


def _exchange_wait(name, started, t, after, kind):
    send_sems, recv_sems, srcs, lands = started
    gather = kind == "gather"

    def body(src_ref, land_ref, send_ref, recv_ref, after_ref, src_out, land_out):
        _, peers = _peers()
        for slot, dev, block in peers:
            copy = pltpu.make_async_remote_copy(
                src_ref=src_ref if gather else src_ref.at[block], dst_ref=land_ref.at[block],
                send_sem=send_ref.at[t * (NDEV - 1) + slot], recv_sem=recv_ref.at[t * (NDEV - 1) + slot],
                device_id=dev, device_id_type=MESH)
            copy.wait_send()
            copy.wait_recv()

    return pl.pallas_call(
        body, name=name, out_shape=(_hbm(srcs[t]), _hbm(lands[t])),
        in_specs=(_HBM_SPEC, _HBM_SPEC, _SEM_SPEC, _SEM_SPEC, pl.BlockSpec(memory_space=pl.ANY)),
        out_specs=(_HBM_SPEC, _HBM_SPEC), input_output_aliases={0: 0, 1: 1},
        compiler_params=pltpu.CompilerParams(has_side_effects=_DATAFLOW),
    )(srcs[t], lands[t], send_sems, recv_sems, after)[1]


def _ffn_fwd(tag, h, g_pre, weight):
    n = _rms_fwd(f"ffn_prenorm_{tag}", h, [g_pre])[0]
    wg = weight(f"gate_up_{tag}", n)
    gu = _fwd_cols_blocked(f"ffn_gate_up_{tag}", n, wg).reshape(2, NFB, S, FB)
    act = _swiglu_fwd(f"ffn_act_{tag}", gu)
    wd4 = weight(f"down_{tag}", act).reshape(NFB, FB, D)
    f = _fwd_kblocked(f"ffn_down_{tag}", act, wd4)
    return (n, gu, act, wg, wd4), f


def _ffn_bwd(tag, dh_out, h_in, f, saved, g_pre, g_post, send):
    n, gu, act, wg, wd4 = saved
    df, (dg_post,) = _rms_bwd(f"ffn_postnorm_bwd_{tag}", f, [(g_post, dh_out)], None, BF16)
    tok = send(f"down_{tag}", _bwd_w_kblocked(f"ffn_down_dw_{tag}", act, df).reshape(NDEV, DFF // NDEV, D))
    da = _bwd_x_kblocked(f"ffn_down_dx_{tag}", df, wd4, after=tok)
    dgu = _swiglu_bwd(f"ffn_act_bwd_{tag}", gu, da).reshape(NDEV, S, FB)
    tok = send(f"gate_up_{tag}", _bwd_w_cols_blocked(f"ffn_gate_up_dw_{tag}", n, dgu))
    dn = _bwd_x_cols_blocked(f"ffn_gate_up_dx_{tag}", dgu, wg, after=tok)
    dh_in, (dg_pre,) = _rms_bwd(f"ffn_prenorm_bwd_{tag}", h_in, [(g_pre, dn)], dh_out, F32)
    return dh_in, dg_pre, dg_post


def _to_classes(t, d):
    return t.reshape(S // d, d * D)


def kernel(x, positions, mix_norm_pre, mix_norm_post, ffn_norm_pre, ffn_norm_post, ffn_w_gate_up, ffn_w_down, conv_w_in, conv_w, conv_w_out, kv_norm, w_kv, w_q, w_o, loss_target, m_mix_norm_pre, m_mix_norm_post, m_ffn_norm_pre, m_ffn_norm_post, m_ffn_w_gate_up, m_ffn_w_down, m_conv_w_in, m_conv_w, m_conv_w_out, m_kv_norm, m_w_kv, m_w_q, m_w_o, v_mix_norm_pre, v_mix_norm_post, v_ffn_norm_pre, v_ffn_norm_post, v_ffn_w_gate_up, v_ffn_w_down, v_conv_w_in, v_conv_w, v_conv_w_out, v_kv_norm, v_w_kv, v_w_q, v_w_o):
    me = 4 * lax.axis_index("x") + 2 * lax.axis_index("y") + lax.axis_index("c")
    h0 = x.reshape(S, D)
    target = loss_target.reshape(S, D)
    row = lambda a, l: a[l].reshape(1, D)
    g_kv = kv_norm.reshape(1, D)

    cw_shard = jnp.pad(conv_w[0], ((0, 5), (0, 0)))
    names = ["conv_in", "conv_w", "conv_out", "gate_up_0", "down_0", "kv", "q", "o", "gate_up_1", "down_1"]
    shards = [conv_w_in[0], cw_shard, conv_w_out[0], ffn_w_gate_up[0], ffn_w_down[0],
              w_kv, w_q[0], w_o[0], ffn_w_gate_up[1], ffn_w_down[1]]
    shards = [s if n == "conv_w" else s.astype(BF16) for n, s in zip(names, shards)]
    gather, _ = _exchange_start("gather_weights_start", shards, me, "gather")

    def weight(name, after):
        return _exchange_wait(f"gather_wait_{name}", gather, names.index(name), after, "gather")

    sent = {}

    def send(name, grad):
        sent[name], token = _exchange_start(f"scatter_start_{name}", [grad], me, "scatter")
        return token

    n0 = _rms_fwd("mix_prenorm_0", h0, [row(mix_norm_pre, 0)])[0]
    win_g = weight("conv_in", n0)
    cw = weight("conv_w", n0).transpose(1, 0, 2).reshape(8, D)
    z = _fwd_cols("conv_in", n0, win_g)
    pre = _conv_fwd("conv_gate", z, cw)
    wout = weight("conv_out", pre).reshape(D, D)
    y0 = _fwd_rows("conv_out", pre, wout)
    h1 = _resid_rms("mix_postnorm_0", h0, y0, row(mix_norm_post, 0))
    ffn0, f0 = _ffn_fwd("0", h1, row(ffn_norm_pre, 0), weight)
    h2 = _resid_rms("ffn_postnorm_0", h1, f0, row(ffn_norm_post, 0))

    nk, n2 = _rms_fwd("kv_and_mix_prenorm_1", h2, [g_kv, row(mix_norm_pre, 1)])
    wkv_g = weight("kv", nk)
    kv = _fwd_cols("kv_proj", nk, wkv_g)
    wq_g = weight("q", kv)
    q_raw = _fwd_cols("q_proj", n2, wq_g)
    half = HEAD_DIM // 2
    inv_freq = ROPE_THETA ** (-jnp.arange(half, dtype=F32) / half)
    cos_t, sin_t = _rope_tables("rope_tables", positions.reshape(S, 1), jnp.tile(inv_freq, 4).reshape(1, 128))
    q = _rope_fwd("rope_q", q_raw, 0, cos_t, sin_t, HEAD_DIM ** -0.5)
    k = _rope_fwd("rope_k", kv, 0, cos_t, sin_t, 1.0)
    qc, kc, vc, lse_c, o_br, lse_br = [], [], [], [], [], []
    for g, (_, d) in enumerate(BRANCHES):
        qc.append(_to_classes(q[:, g * D:(g + 1) * D], d))
        kc.append(_to_classes(k[:, g * D:(g + 1) * D], d))
        vc.append(_to_classes(kv[:, QW + g * D:QW + (g + 1) * D], d))
        o_g, lse_g = _attn_fwd(f"attn_fwd_{g}", qc[g], kc[g], vc[g], d)
        lse_c.append(lse_g)
        o_br.append(o_g.reshape(S, D))
        lse_br.append(lse_g.reshape(S, D))
    o_mix = _mix_fwd("attn_mix", o_br, lse_br)
    wo = weight("o", o_mix).reshape(D, D)
    y1 = _fwd_rows("attn_out", o_mix, wo)
    h3 = _resid_rms("mix_postnorm_1", h2, y1, row(mix_norm_post, 1))
    ffn1, f1 = _ffn_fwd("1", h3, row(ffn_norm_pre, 1), weight)
    h4 = _resid_rms("ffn_postnorm_1", h3, f1, row(ffn_norm_post, 1))

    dh4, sq = _loss_grad("loss", h4, target)
    loss = lax.psum(jnp.sum(sq) * (0.5 / D), ("x", "y", "c"))

    dh3, dg_fpre1, dg_fpost1 = _ffn_bwd(
        "1", dh4, h3, f1, ffn1, row(ffn_norm_pre, 1), row(ffn_norm_post, 1), send)
    dy1, (dg_mpost1,) = _rms_bwd("mix_postnorm_bwd_1", y1, [(row(mix_norm_post, 1), dh3)], None, BF16)
    tok = send("o", _bwd_w_rows("attn_out_dw", o_mix, dy1).reshape(NDEV, D // NDEV, D))
    do = _bwd_x_rows("attn_out_dx", dy1, wo, F32, after=tok)
    lane = jnp.arange(128)
    ones_blockdiag = (lane[:, None] // HEAD_DIM == lane[None, :] // HEAD_DIM).astype(BF16)
    mixed = _mix_bwd("attn_mix_bwd", do, o_br, lse_br, ones_blockdiag)
    branch_grads = []
    for g, (_, d) in enumerate(BRANCHES):
        res = _attn_bwd(f"attn_bwd_{g}", qc[g], kc[g], vc[g], _to_classes(mixed[g], d),
                        lse_c[g], _to_classes(mixed[3 + g], d), d)
        branch_grads.append([r.reshape(S, D) for r in res])
    dq_raw, dkv = _attn_bwd_post("attn_bwd_post", branch_grads, cos_t, sin_t)
    tok = send("kv", _bwd_w_cols("kv_proj_dw", nk, dkv, 2 * QW // NDEV))
    dnk = _bwd_x_cols("kv_proj_dx", dkv, wkv_g, after=tok)
    tok = send("q", _bwd_w_cols("q_proj_dw", n2, dq_raw, QW // NDEV))
    dn2 = _bwd_x_cols("q_proj_dx", dq_raw, wq_g, after=tok)
    dh2, (dg_kv, dg_mpre1) = _rms_bwd("kv_and_mix_prenorm_bwd_1", h2,
                                      [(g_kv, dnk), (row(mix_norm_pre, 1), dn2)], dh3, F32)

    dh1, dg_fpre0, dg_fpost0 = _ffn_bwd(
        "0", dh2, h1, f0, ffn0, row(ffn_norm_pre, 0), row(ffn_norm_post, 0), send)
    dy0, (dg_mpost0,) = _rms_bwd("mix_postnorm_bwd_0", y0, [(row(mix_norm_post, 0), dh1)], None, BF16)
    tok = send("conv_out", _bwd_w_rows("conv_out_dw", pre, dy0).reshape(NDEV, D // NDEV, D))
    dpre = _bwd_x_rows("conv_out_dx", dy0, wout, BF16, after=tok)
    dz, dcw = _conv_bwd("conv_gate_bwd", z, dpre, cw)
    tok = send("conv_in", _bwd_w_cols("conv_in_dw", n0, dz, 3 * D // NDEV))
    dn0 = _bwd_x_cols("conv_in_dx", dz, win_g, after=tok)
    dh0, (dg_mpre0,) = _rms_bwd("mix_prenorm_bwd_0", h0, [(row(mix_norm_pre, 0), dn0)], dh1, F32)

    small = jnp.concatenate([dg_mpre0, dg_mpre1, dg_mpost0, dg_mpost1, dg_fpre0, dg_fpre1, dg_fpost0, dg_fpost1,
                             dg_kv, jnp.zeros((7, D), F32), dcw], axis=0)
    small_all = _exchange("gather_small_grads", [small], "gather")[0]

    done = [small_all]

    def upd(tag, w, m, v):
        parts = _exchange_wait(f"scatter_wait_{tag}", sent[tag], 0, done[-1], "scatter")
        shape = w.shape
        flat = lambda a: a.reshape(parts.shape[1:])
        res = _adamw(f"adamw_{tag}", parts, flat(w), flat(m), flat(v))
        done.append(res[0])
        return [r.reshape(shape) for r in res]

    def upd_layer(tag, l, w, m, v):
        return upd(f"{tag}_{l}", w[l], m[l], v[l])

    def stack(per_layer):
        return [jnp.stack([per_layer[0][i], per_layer[1][i]]) for i in range(4)]

    gains_w = jnp.concatenate([mix_norm_pre, mix_norm_post, ffn_norm_pre, ffn_norm_post, g_kv, jnp.zeros((15, D), F32)])
    gains_m = jnp.concatenate([m_mix_norm_pre, m_mix_norm_post, m_ffn_norm_pre, m_ffn_norm_post,
                               m_kv_norm.reshape(1, D), jnp.zeros((15, D), F32)])
    gains_v = jnp.concatenate([v_mix_norm_pre, v_mix_norm_post, v_ffn_norm_pre, v_ffn_norm_post,
                               v_kv_norm.reshape(1, D), jnp.ones((15, D), F32)])
    small_res = _adamw("adamw_gains", small_all, gains_w, gains_m, gains_v)
    dcw_mine = lax.dynamic_slice(small_res[0], (16, me * 128), (8, 128))
    pad8 = lambda a, fill: jnp.pad(a[0], ((0, 5), (0, 0)), constant_values=fill)
    cw_res = [r[0:3].reshape(1, 3, 128) for r in
              _adamw("adamw_conv_w", dcw_mine.reshape(1, 8, 128), cw_shard, pad8(m_conv_w, 0.0), pad8(v_conv_w, 1.0))]

    res = {
        "mix_norm_pre": [r[0:2] for r in small_res],
        "mix_norm_post": [r[2:4] for r in small_res],
        "ffn_norm_pre": [r[4:6] for r in small_res],
        "ffn_norm_post": [r[6:8] for r in small_res],
        "kv_norm": [r[8] for r in small_res],
        "conv_w": cw_res,
    }
    down, gate_up = {}, {}
    down[1] = upd_layer("down", 1, ffn_w_down, m_ffn_w_down, v_ffn_w_down)
    gate_up[1] = upd_layer("gate_up", 1, ffn_w_gate_up, m_ffn_w_gate_up, v_ffn_w_gate_up)
    res["w_o"] = upd("o", w_o, m_w_o, v_w_o)
    res["w_q"] = upd("q", w_q, m_w_q, v_w_q)
    res["w_kv"] = upd("kv", w_kv, m_w_kv, v_w_kv)
    down[0] = upd_layer("down", 0, ffn_w_down, m_ffn_w_down, v_ffn_w_down)
    gate_up[0] = upd_layer("gate_up", 0, ffn_w_gate_up, m_ffn_w_gate_up, v_ffn_w_gate_up)
    res["ffn_w_down"] = stack(down)
    res["ffn_w_gate_up"] = stack(gate_up)
    res["conv_w_out"] = upd("conv_out", conv_w_out, m_conv_w_out, v_conv_w_out)
    res["conv_w_in"] = upd("conv_in", conv_w_in, m_conv_w_in, v_conv_w_in)
    order = ["mix_norm_pre", "mix_norm_post", "ffn_norm_pre", "ffn_norm_post", "ffn_w_gate_up", "ffn_w_down",
             "conv_w_in", "conv_w", "conv_w_out", "kv_norm", "w_kv", "w_q", "w_o"]
    out = [loss, dh0.reshape(1, S, D)]
    for i in range(4):
        out += [res[name][i] for name in order]
    return tuple(out)
```

```python
import jax
import jax.numpy as jnp
from jax import lax
from jax.experimental import pallas as pl
from jax.experimental.pallas import tpu as pltpu

F32 = jnp.float32
BF16 = jnp.bfloat16

S = 4096
D = 1024
NDEV = 8
HEAD_DIM = 64
QW = 3072
DFF = 2816
FB = 704
NFB = 4
BRANCHES = ((128, 1), (512, 4), (2048, 16))
BAND = 128
ROPE_THETA = 10000.0
RMS_EPS = 1e-6
NEG_INF = -1e30
ADAM_LR, ADAM_B1, ADAM_B2, ADAM_EPS, ADAM_WD, ADAM_STEP = 0.001, 0.9, 0.999, 1e-08, 0.01, 10

VMEM_LIMIT_BYTES = 52 * 1024 * 1024
ROW_TILE = 512
MESH = pl.DeviceIdType.MESH


def _cparams(ngrid):
    return pltpu.CompilerParams(dimension_semantics=("arbitrary",) * ngrid,
                                vmem_limit_bytes=VMEM_LIMIT_BYTES)


def _sds(shape, dtype):
    return jax.ShapeDtypeStruct(tuple(shape), dtype)


_DIMS = {"nn": (((1,), (0,)), ((), ())),
         "nt": (((1,), (1,)), ((), ())),
         "tn": (((0,), (0,)), ((), ()))}


def _matmul(name, a, b, *, mode, grid, a_blk, a_map, b_blk, b_map, o_shape, o_blk, o_map, out_dtype, after=None):
    nk = grid[2]
    dims = _DIMS[mode]
    acc_shape = tuple(s for s in o_blk if s is not None)
    extra = [] if after is None else [after]

    def body(a_ref, b_ref, *rest):
        o_ref, scratch = rest[len(extra)], rest[len(extra) + 1:]
        part = lax.dot_general(a_ref[...], b_ref[...], dims, preferred_element_type=F32)
        if nk == 1:
            o_ref[...] = part.astype(o_ref.dtype)
            return
        acc_ref = scratch[0]
        k = pl.program_id(2)

        @pl.when(k == 0)
        def _():
            acc_ref[...] = part

        @pl.when(k > 0)
        def _():
            acc_ref[...] += part

        @pl.when(k == nk - 1)
        def _():
            o_ref[...] = acc_ref[...].astype(o_ref.dtype)

    return pl.pallas_call(
        body, name=name, grid=grid,
        in_specs=[pl.BlockSpec(a_blk, a_map), pl.BlockSpec(b_blk, b_map)] + [pl.BlockSpec(memory_space=pl.ANY)] * len(extra),
        out_specs=pl.BlockSpec(o_blk, o_map),
        out_shape=_sds(o_shape, out_dtype),
        scratch_shapes=[] if nk == 1 else [pltpu.VMEM(acc_shape, F32)],
        compiler_params=_cparams(3),
    )(a, b, *extra)


TM = 1024
TK = 1024


def _fwd_cols(name, a, wg, out_dtype=BF16):
    _, kdim, n = wg.shape
    return _matmul(name, a, wg, mode="nn", grid=(S // TM, NDEV, 1),
                   a_blk=(TM, kdim), a_map=lambda i, j, k: (i, 0),
                   b_blk=(None, kdim, n), b_map=lambda i, j, k: (j, 0, 0),
                   o_shape=(S, NDEV * n), o_blk=(TM, n), o_map=lambda i, j, k: (i, j), out_dtype=out_dtype)


def _fwd_cols_blocked(name, a, wg):
    _, kdim, n = wg.shape
    return _matmul(name, a, wg, mode="nn", grid=(S // TM, NDEV, 1),
                   a_blk=(TM, kdim), a_map=lambda i, j, k: (i, 0),
                   b_blk=(None, kdim, n), b_map=lambda i, j, k: (j, 0, 0),
                   o_shape=(NDEV, S, n), o_blk=(None, TM, n), o_map=lambda i, j, k: (j, i, 0), out_dtype=BF16)


def _fwd_rows(name, a, w, out_dtype=F32):
    kdim, n = w.shape
    tn = 512
    return _matmul(name, a, w, mode="nn", grid=(S // TM, n // tn, 1),
                   a_blk=(TM, kdim), a_map=lambda i, j, k: (i, 0),
                   b_blk=(kdim, tn), b_map=lambda i, j, k: (0, j),
                   o_shape=(S, n), o_blk=(TM, tn), o_map=lambda i, j, k: (i, j), out_dtype=out_dtype)


def _fwd_kblocked(name, a4, w4):
    nb, _, kb = a4.shape
    n = w4.shape[2]
    return _matmul(name, a4, w4, mode="nn", grid=(S // TM, 1, nb),
                   a_blk=(None, TM, kb), a_map=lambda i, j, k: (k, i, 0),
                   b_blk=(None, kb, n), b_map=lambda i, j, k: (k, 0, 0),
                   o_shape=(S, n), o_blk=(TM, n), o_map=lambda i, j, k: (i, 0), out_dtype=F32)


def _bwd_x_cols(name, dy, wg, after=None):
    _, kdim, n = wg.shape
    return _matmul(name, dy, wg, mode="nt", grid=(S // TM, 1, NDEV),
                   a_blk=(TM, n), a_map=lambda i, j, k: (i, k),
                   b_blk=(None, kdim, n), b_map=lambda i, j, k: (k, 0, 0),
                   o_shape=(S, kdim), o_blk=(TM, kdim), o_map=lambda i, j, k: (i, 0), out_dtype=F32, after=after)


def _bwd_x_cols_blocked(name, dy8, wg, after=None):
    _, kdim, n = wg.shape
    return _matmul(name, dy8, wg, mode="nt", grid=(S // TM, 1, NDEV),
                   a_blk=(None, TM, n), a_map=lambda i, j, k: (k, i, 0),
                   b_blk=(None, kdim, n), b_map=lambda i, j, k: (k, 0, 0),
                   o_shape=(S, kdim), o_blk=(TM, kdim), o_map=lambda i, j, k: (i, 0), out_dtype=F32, after=after)


def _bwd_x_rows(name, dy, w, out_dtype, after=None):
    kdim, n = w.shape
    tkk = 512
    return _matmul(name, dy, w, mode="nt", grid=(S // TM, kdim // tkk, 1),
                   a_blk=(TM, n), a_map=lambda i, j, k: (i, 0),
                   b_blk=(tkk, n), b_map=lambda i, j, k: (j, 0),
                   o_shape=(S, kdim), o_blk=(TM, tkk), o_map=lambda i, j, k: (i, j), out_dtype=out_dtype, after=after)


def _bwd_x_kblocked(name, dy, w4, after=None):
    nb, kb, n = w4.shape
    return _matmul(name, dy, w4, mode="nt", grid=(S // TM, nb, 1),
                   a_blk=(TM, n), a_map=lambda i, j, k: (i, 0),
                   b_blk=(None, kb, n), b_map=lambda i, j, k: (j, 0, 0),
                   o_shape=(nb, S, kb), o_blk=(None, TM, kb), o_map=lambda i, j, k: (j, i, 0), out_dtype=BF16, after=after)


def _bwd_w_cols(name, a, dy, n):
    kdim = a.shape[1]
    return _matmul(name, a, dy, mode="tn", grid=(1, NDEV, S // TK),
                   a_blk=(TK, kdim), a_map=lambda i, j, k: (k, 0),
                   b_blk=(TK, n), b_map=lambda i, j, k: (k, j),
                   o_shape=(NDEV, kdim, n), o_blk=(None, kdim, n), o_map=lambda i, j, k: (j, 0, 0), out_dtype=BF16)


def _bwd_w_cols_blocked(name, a, dy8):
    kdim = a.shape[1]
    n = dy8.shape[2]
    return _matmul(name, a, dy8, mode="tn", grid=(1, NDEV, S // TK),
                   a_blk=(TK, kdim), a_map=lambda i, j, k: (k, 0),
                   b_blk=(None, TK, n), b_map=lambda i, j, k: (j, k, 0),
                   o_shape=(NDEV, kdim, n), o_blk=(None, kdim, n), o_map=lambda i, j, k: (j, 0, 0), out_dtype=BF16)


def _bwd_w_rows(name, a, dy):
    kdim = a.shape[1]
    n = dy.shape[1]
    tmm = 512
    return _matmul(name, a, dy, mode="tn", grid=(kdim // tmm, 1, S // TK),
                   a_blk=(TK, tmm), a_map=lambda i, j, k: (k, i),
                   b_blk=(TK, n), b_map=lambda i, j, k: (k, 0),
                   o_shape=(kdim, n), o_blk=(tmm, n), o_map=lambda i, j, k: (i, 0), out_dtype=BF16)


def _bwd_w_kblocked(name, a4, dy):
    nb, _, kb = a4.shape
    n = dy.shape[1]
    return _matmul(name, a4, dy, mode="tn", grid=(nb, 1, S // TK),
                   a_blk=(None, TK, kb), a_map=lambda i, j, k: (i, k, 0),
                   b_blk=(TK, n), b_map=lambda i, j, k: (k, 0),
                   o_shape=(nb, kb, n), o_blk=(None, kb, n), o_map=lambda i, j, k: (i, 0, 0), out_dtype=BF16)


def _rstd(x):
    return lax.rsqrt(jnp.mean(x * x, axis=-1, keepdims=True) + RMS_EPS)


def _row_spec(tm=ROW_TILE, width=D):
    return pl.BlockSpec((tm, width), lambda i: (i, 0))


def _vec_spec(rows=1, width=D):
    return pl.BlockSpec((rows, width), lambda i: (0, 0))


def _rms_fwd(name, x, gains):
    n = len(gains)

    def body(x_ref, *refs):
        x_val = x_ref[...]
        xh = x_val * _rstd(x_val)
        for g_ref, o_ref in zip(refs[:n], refs[n:]):
            o_ref[...] = (xh * g_ref[...]).astype(o_ref.dtype)

    outs = pl.pallas_call(
        body, name=name, grid=(S // ROW_TILE,),
        in_specs=[_row_spec()] + [_vec_spec()] * n,
        out_specs=[_row_spec()] * n,
        out_shape=[_sds((S, D), BF16)] * n,
        compiler_params=_cparams(1),
    )(x, *gains)
    return list(outs)


def _resid_rms(name, h, y, g):
    def body(h_ref, y_ref, g_ref, o_ref):
        y_val = y_ref[...]
        o_ref[...] = h_ref[...] + (y_val * _rstd(y_val)) * g_ref[...]

    return pl.pallas_call(
        body, name=name, grid=(S // ROW_TILE,),
        in_specs=[_row_spec(), _row_spec(), _vec_spec()],
        out_specs=_row_spec(), out_shape=_sds((S, D), F32),
        compiler_params=_cparams(1),
    )(h, y, g)


def _rms_bwd(name, x, pairs, dres, out_dtype):
    n = len(pairs)
    has_res = dres is not None

    def body(x_ref, *refs):
        g_refs = refs[0:2 * n:2]
        dn_refs = refs[1:2 * n:2]
        pos = 2 * n
        res_ref = refs[pos] if has_res else None
        pos += int(has_res)
        dx_ref = refs[pos]
        dg_refs = refs[pos + 1:]
        step = pl.program_id(0)
        x_val = x_ref[...]
        r = _rstd(x_val)
        xh = x_val * r
        acc = res_ref[...] if has_res else jnp.zeros_like(x_val)
        for g_ref, dn_ref, dg_ref in zip(g_refs, dn_refs, dg_refs):
            dn = dn_ref[...].astype(F32)
            dxh = dn * g_ref[...]
            acc = acc + r * (dxh - xh * jnp.mean(dxh * xh, axis=-1, keepdims=True))
            part = jnp.sum(dn * xh, axis=0, keepdims=True)

            @pl.when(step == 0)
            def _():
                dg_ref[...] = part

            @pl.when(step > 0)
            def _():
                dg_ref[...] += part

        dx_ref[...] = acc.astype(dx_ref.dtype)

    operands = [x]
    in_specs = [_row_spec()]
    for g, dn in pairs:
        operands += [g, dn]
        in_specs += [_vec_spec(), _row_spec()]
    if has_res:
        operands.append(dres)
        in_specs.append(_row_spec())
    outs = pl.pallas_call(
        body, name=name, grid=(S // ROW_TILE,),
        in_specs=in_specs,
        out_specs=[_row_spec()] + [_vec_spec()] * n,
        out_shape=[_sds((S, D), out_dtype)] + [_sds((1, D), F32)] * n,
        compiler_params=_cparams(1),
    )(*operands)
    return outs[0], list(outs[1:])


def _loss_grad(name, h, target):
    def body(h_ref, t_ref, dh_ref, part_ref):
        e = h_ref[...] - t_ref[...]
        dh_ref[...] = e * (1.0 / D)
        part = jnp.sum(e * e, axis=0, keepdims=True)
        step = pl.program_id(0)

        @pl.when(step == 0)
        def _():
            part_ref[...] = part

        @pl.when(step > 0)
        def _():
            part_ref[...] += part

    return pl.pallas_call(
        body, name=name, grid=(S // ROW_TILE,),
        in_specs=[_row_spec(), _row_spec()],
        out_specs=[_row_spec(), _vec_spec()],
        out_shape=[_sds((S, D), F32), _sds((1, D), F32)],
        compiler_params=_cparams(1),
    )(h, target)


def _shift_down(u, prev8, k):
    r = pltpu.roll(u, k, 0)
    p = pltpu.roll(prev8, k, 0)
    row = lax.broadcasted_iota(jnp.int32, prev8.shape, 0)
    top = jnp.where(row < k, p, r[0:8])
    return jnp.concatenate([top, r[8:]], axis=0)


def _shift_up(u, next8, k):
    tm = u.shape[0]
    r = pltpu.roll(u, tm - k, 0)
    p = pltpu.roll(next8, 8 - k, 0)
    row = lax.broadcasted_iota(jnp.int32, next8.shape, 0)
    bot = jnp.where(row >= 8 - k, p, r[tm - 8:tm])
    return jnp.concatenate([r[:tm - 8], bot], axis=0)


CONV_TILE = 512


def _halo_prev(col):
    return pl.BlockSpec((8, D), lambda i: (jnp.maximum(i * (CONV_TILE // 8) - 1, 0), col))


def _halo_next(col):
    last = S // 8 - 1
    return pl.BlockSpec((8, D), lambda i: (jnp.minimum((i + 1) * (CONV_TILE // 8), last), col))


def _conv_fwd(name, z, cw):
    def body(b_ref, c_ref, h_ref, cp_ref, hp_ref, cw_ref, o_ref):
        i = pl.program_id(0)
        u = c_ref[...].astype(F32) * h_ref[...].astype(F32)
        up = cp_ref[...].astype(F32) * hp_ref[...].astype(F32)
        up = jnp.where(i > 0, up, 0.0)
        cv = cw_ref[0:1, :] * _shift_down(u, up, 2) + cw_ref[1:2, :] * _shift_down(u, up, 1) + cw_ref[2:3, :] * u
        o_ref[...] = (b_ref[...].astype(F32) * cv).astype(o_ref.dtype)

    col = lambda c: pl.BlockSpec((CONV_TILE, D), lambda i: (i, c))
    return pl.pallas_call(
        body, name=name, grid=(S // CONV_TILE,),
        in_specs=[col(0), col(1), col(2), _halo_prev(1), _halo_prev(2), _vec_spec(8)],
        out_specs=_row_spec(CONV_TILE), out_shape=_sds((S, D), BF16),
        compiler_params=_cparams(1),
    )(z, z, z, z, z, cw)


def _conv_bwd(name, z, dpre, cw):
    nsteps = S // CONV_TILE

    def body(b_ref, c_ref, h_ref, cp_ref, hp_ref, dp_ref, dpn_ref, bn_ref, cw_ref, dz_ref, dcw_ref):
        i = pl.program_id(0)
        b = b_ref[...].astype(F32)
        c = c_ref[...].astype(F32)
        h = h_ref[...].astype(F32)
        dp = dp_ref[...].astype(F32)
        u = c * h
        up = jnp.where(i > 0, cp_ref[...].astype(F32) * hp_ref[...].astype(F32), 0.0)
        s1 = _shift_down(u, up, 1)
        s2 = _shift_down(u, up, 2)
        w0, w1, w2 = cw_ref[0:1, :], cw_ref[1:2, :], cw_ref[2:3, :]
        cv = w0 * s2 + w1 * s1 + w2 * u
        dcv = dp * b
        dcvn = jnp.where(i < nsteps - 1, dpn_ref[...].astype(F32) * bn_ref[...].astype(F32), 0.0)
        du = w2 * dcv + w1 * _shift_up(dcv, dcvn, 1) + w0 * _shift_up(dcv, dcvn, 2)
        dz_ref[:, 0:D] = (dp * cv).astype(dz_ref.dtype)
        dz_ref[:, D:2 * D] = (du * h).astype(dz_ref.dtype)
        dz_ref[:, 2 * D:3 * D] = (du * c).astype(dz_ref.dtype)

        @pl.when(i == 0)
        def _():
            dcw_ref[...] = jnp.zeros_like(dcw_ref)

        dcw_ref[0:1, :] += jnp.sum(dcv * s2, axis=0, keepdims=True)
        dcw_ref[1:2, :] += jnp.sum(dcv * s1, axis=0, keepdims=True)
        dcw_ref[2:3, :] += jnp.sum(dcv * u, axis=0, keepdims=True)

    col = lambda c: pl.BlockSpec((CONV_TILE, D), lambda i: (i, c))
    return pl.pallas_call(
        body, name=name, grid=(nsteps,),
        in_specs=[col(0), col(1), col(2), _halo_prev(1), _halo_prev(2),
                  _row_spec(CONV_TILE), _halo_next(0), _halo_next(0), _vec_spec(8)],
        out_specs=[pl.BlockSpec((CONV_TILE, 3 * D), lambda i: (i, 0)), _vec_spec(8)],
        out_shape=[_sds((S, 3 * D), BF16), _sds((8, D), F32)],
        compiler_params=_cparams(1),
    )(z, z, z, z, z, dpre, dpre, z, cw)


def _swiglu_fwd(name, gu):
    def body(gu_ref, o_ref):
        g = gu_ref[0].astype(F32)
        u = gu_ref[1].astype(F32)
        o_ref[...] = (g * jax.nn.sigmoid(g) * u).astype(o_ref.dtype)

    return pl.pallas_call(
        body, name=name, grid=(NFB, S // ROW_TILE),
        in_specs=[pl.BlockSpec((2, None, ROW_TILE, FB), lambda j, i: (0, j, i, 0))],
        out_specs=pl.BlockSpec((None, ROW_TILE, FB), lambda j, i: (j, i, 0)),
        out_shape=_sds((NFB, S, FB), BF16),
        compiler_params=_cparams(2),
    )(gu)


def _swiglu_bwd(name, gu, da):
    def body(gu_ref, da_ref, o_ref):
        g = gu_ref[0].astype(F32)
        u = gu_ref[1].astype(F32)
        d = da_ref[...].astype(F32)
        sg = jax.nn.sigmoid(g)
        o_ref[0] = (d * u * sg * (1.0 + g * (1.0 - sg))).astype(o_ref.dtype)
        o_ref[1] = (d * g * sg).astype(o_ref.dtype)

    blk = pl.BlockSpec((2, None, ROW_TILE, FB), lambda j, i: (0, j, i, 0))
    return pl.pallas_call(
        body, name=name, grid=(NFB, S // ROW_TILE),
        in_specs=[blk, pl.BlockSpec((None, ROW_TILE, FB), lambda j, i: (j, i, 0))],
        out_specs=blk, out_shape=_sds((2, NFB, S, FB), BF16),
        compiler_params=_cparams(2),
    )(gu, da)


def _rope_tables(name, pos_col, inv_freq_row):
    def body(pos_ref, f_ref, cos_ref, sin_ref):
        ang = pos_ref[...].astype(F32) * f_ref[...]
        lane = lax.broadcasted_iota(jnp.int32, ang.shape, 1)
        s = jnp.sin(ang)
        cos_ref[...] = jnp.cos(ang)
        sin_ref[...] = jnp.where((lane % HEAD_DIM) < HEAD_DIM // 2, -s, s)

    tab = pl.BlockSpec((ROW_TILE, 128), lambda i: (i, 0))
    return pl.pallas_call(
        body, name=name, grid=(S // ROW_TILE,),
        in_specs=[pl.BlockSpec((ROW_TILE, 1), lambda i: (i, 0)), _vec_spec(1, 128)],
        out_specs=[tab, tab], out_shape=[_sds((S, 128), F32)] * 2,
        compiler_params=_cparams(1),
    )(pos_col, inv_freq_row)


def _swap_halves(t):
    lane = lax.broadcasted_iota(jnp.int32, t.shape, 1)
    first = (lane % HEAD_DIM) < HEAD_DIM // 2
    return jnp.where(first, pltpu.roll(t, 128 - HEAD_DIM // 2, 1), pltpu.roll(t, HEAD_DIM // 2, 1))


ROPE_COLS = 768


def _rope_fwd(name, t, col_off, cos_t, sin_t, scale):
    def body(t_ref, cos_ref, sin_ref, o_ref):
        cs = cos_ref[...]
        sn = sin_ref[...]
        for j in range(ROPE_COLS // 128):
            x = t_ref[:, j * 128:(j + 1) * 128].astype(F32)
            o_ref[:, j * 128:(j + 1) * 128] = ((x * cs + _swap_halves(x) * sn) * scale).astype(o_ref.dtype)

    off = col_off // ROPE_COLS
    tab = pl.BlockSpec((ROW_TILE, 128), lambda i, j: (i, 0))
    return pl.pallas_call(
        body, name=name, grid=(S // ROW_TILE, QW // ROPE_COLS),
        in_specs=[pl.BlockSpec((ROW_TILE, ROPE_COLS), lambda i, j: (i, j + off)), tab, tab],
        out_specs=pl.BlockSpec((ROW_TILE, ROPE_COLS), lambda i, j: (i, j)),
        out_shape=_sds((S, QW), BF16),
        compiler_params=_cparams(2),
    )(t, cos_t, sin_t)


def _class_spec(d):
    return pl.BlockSpec((S // d, 128), lambda cb: (0, cb))


def _dot_nt(a, b):
    return lax.dot_general(a, b, _DIMS["nt"], preferred_element_type=F32)


def _dot_tn(a, b):
    return lax.dot_general(a, b, _DIMS["tn"], preferred_element_type=F32)


def _dot_nn(a, b):
    return lax.dot_general(a, b, _DIMS["nn"], preferred_element_type=F32)


def _band_mask(nkeys):
    qi = lax.broadcasted_iota(jnp.int32, (2 * BAND, nkeys), 0) % BAND
    kj = lax.broadcasted_iota(jnp.int32, (2 * BAND, nkeys), 1)
    if nkeys == BAND:
        return kj <= qi
    dist = qi + BAND - kj
    return (dist >= 0) & (dist <= BAND)


def _stack_heads(x):
    row = lax.broadcasted_iota(jnp.int32, (2 * BAND, 128), 0)
    lane = lax.broadcasted_iota(jnp.int32, (2 * BAND, 128), 1)
    keep = (row < BAND) == (lane < HEAD_DIM)
    return jnp.where(keep, jnp.concatenate([x, x], axis=0), jnp.zeros((), x.dtype))


def _unstack(x2):
    first_head = lax.broadcasted_iota(jnp.int32, (BAND, 128), 1) < HEAD_DIM
    return jnp.where(first_head, x2[:BAND], x2[BAND:])


ATTN_UNROLL = 2


def _for_later_blocks(nblk, fn):
    trips = (nblk - 1) // ATTN_UNROLL
    if trips > 0:
        def step(i, carry):
            for u in range(ATTN_UNROLL):
                fn(pl.multiple_of((1 + i * ATTN_UNROLL + u) * BAND, BAND))
            return carry

        lax.fori_loop(0, trips, step, 0)
    for sb in range(1 + trips * ATTN_UNROLL, nblk):
        fn(sb * BAND)


def _attn_fwd(name, q, k, v, d):
    nblk = S // d // BAND

    def body(q_ref, k_ref, v_ref, o_ref, lse_ref):
        def block(r0, k0, nkeys):
            q2 = _stack_heads(q_ref[pl.ds(r0, BAND), :])
            s = jnp.where(_band_mask(nkeys), _dot_nt(q2, k_ref[pl.ds(k0, nkeys), :]), NEG_INF)
            m = jnp.max(s, axis=-1, keepdims=True)
            p = jnp.exp(s - m)
            l = jnp.sum(p, axis=-1, keepdims=True)
            o2 = _dot_nn(p.astype(BF16), v_ref[pl.ds(k0, nkeys), :]) / l
            lse2 = jnp.broadcast_to(m + jnp.log(l), (2 * BAND, 128))
            o_ref[pl.ds(r0, BAND), :] = _unstack(o2)
            lse_ref[pl.ds(r0, BAND), :] = _unstack(lse2)

        block(0, 0, BAND)

        _for_later_blocks(nblk, lambda r0: block(r0, r0 - BAND, 2 * BAND))

    spec = _class_spec(d)
    return pl.pallas_call(
        body, name=name, grid=(8 * d,),
        in_specs=[spec] * 3, out_specs=[spec] * 2,
        out_shape=[_sds((S // d, d * D), F32)] * 2,
        compiler_params=_cparams(1),
    )(q, k, v)


def _attn_bwd(name, q, k, v, do, lse, dd, d):
    nblk = S // d // BAND

    def body(q_ref, k_ref, v_ref, do_ref, lse_ref, dd_ref, dq_ref, dk_ref, dv_ref):
        def column(ref, r0):
            rows = pl.ds(r0, BAND)
            return jnp.concatenate([ref[rows, 0:1], ref[rows, HEAD_DIM:HEAD_DIM + 1]], axis=0)

        def block(r0, k0, nkeys, first):
            q2 = _stack_heads(q_ref[pl.ds(r0, BAND), :])
            do2 = _stack_heads(do_ref[pl.ds(r0, BAND), :])
            kk = k_ref[pl.ds(k0, nkeys), :]
            vv = v_ref[pl.ds(k0, nkeys), :]
            s = jnp.where(_band_mask(nkeys), _dot_nt(q2, kk), NEG_INF)
            p = jnp.exp(s - column(lse_ref, r0))
            ds = (p * (_dot_nt(do2, vv) - column(dd_ref, r0))).astype(BF16)
            dq_ref[pl.ds(r0, BAND), :] = _unstack(_dot_nn(ds, kk))
            dk_part = _dot_tn(ds, q2)
            dv_part = _dot_tn(p.astype(BF16), do2)
            if first:
                dk_ref[pl.ds(k0, nkeys), :] = dk_part
                dv_ref[pl.ds(k0, nkeys), :] = dv_part
            else:
                dk_ref[pl.ds(k0, BAND), :] += dk_part[:BAND]
                dv_ref[pl.ds(k0, BAND), :] += dv_part[:BAND]
                dk_ref[pl.ds(k0 + BAND, BAND), :] = dk_part[BAND:]
                dv_ref[pl.ds(k0 + BAND, BAND), :] = dv_part[BAND:]

        block(0, 0, BAND, True)

        _for_later_blocks(nblk, lambda r0: block(r0, r0 - BAND, 2 * BAND, False))

    spec = _class_spec(d)
    return pl.pallas_call(
        body, name=name, grid=(8 * d,),
        in_specs=[spec] * 6, out_specs=[spec] * 3,
        out_shape=[_sds((S // d, d * D), F32)] * 3,
        compiler_params=_cparams(1),
    )(q, k, v, do, lse, dd)


def _mix_fwd(name, outs, lses):
    def body(o0, o1, o2, l0, l1, l2, o_ref):
        a, b, c = l0[...], l1[...], l2[...]
        m = jnp.maximum(jnp.maximum(a, b), c)
        ea, eb, ec = jnp.exp(a - m), jnp.exp(b - m), jnp.exp(c - m)
        o_ref[...] = ((ea * o0[...] + eb * o1[...] + ec * o2[...]) / (ea + eb + ec)).astype(o_ref.dtype)

    return pl.pallas_call(
        body, name=name, grid=(S // ROW_TILE,),
        in_specs=[_row_spec()] * 6, out_specs=_row_spec(), out_shape=_sds((S, D), BF16),
        compiler_params=_cparams(1),
    )(*outs, *lses)


def _head_sums(x, ones_blockdiag):
    cols = []
    for j in range(D // 128):
        xj = x[:, j * 128:(j + 1) * 128]
        hi = xj.astype(BF16)
        r1 = xj - hi.astype(F32)
        mid = r1.astype(BF16)
        lo = (r1 - mid.astype(F32)).astype(BF16)
        cols.append(_dot_nn(hi, ones_blockdiag) + _dot_nn(mid, ones_blockdiag) + _dot_nn(lo, ones_blockdiag))
    return jnp.concatenate(cols, axis=1)


def _mix_bwd(name, do, outs, lses, ones_blockdiag):
    tm = 256

    def body(do_ref, o0, o1, o2, l0, l1, l2, ones_ref, d0, d1, d2, t0, t1, t2):
        a, b, c = l0[...], l1[...], l2[...]
        m = jnp.maximum(jnp.maximum(a, b), c)
        ea, eb, ec = jnp.exp(a - m), jnp.exp(b - m), jnp.exp(c - m)
        den = ea + eb + ec
        wa, wb, wc = ea / den, eb / den, ec / den
        dov = do_ref[...]
        o = wa * o0[...] + wb * o1[...] + wc * o2[...]
        t = _head_sums(dov * o, ones_ref[...])
        for w, d_ref, t_ref in ((wa, d0, t0), (wb, d1, t1), (wc, d2, t2)):
            d_ref[...] = (w * dov).astype(d_ref.dtype)
            t_ref[...] = w * t

    return pl.pallas_call(
        body, name=name, grid=(S // tm,),
        in_specs=[_row_spec(tm)] * 7 + [_vec_spec(128, 128)],
        out_specs=[_row_spec(tm)] * 6,
        out_shape=[_sds((S, D), BF16)] * 3 + [_sds((S, D), F32)] * 3,
        compiler_params=_cparams(1),
    )(do, *outs, *lses, ones_blockdiag)


def _attn_bwd_post(name, grads, cos_t, sin_t):
    tm = 256
    scale = HEAD_DIM ** -0.5

    def unrope(x, cs, sn):
        return x * cs - _swap_halves(x) * sn

    def body(*refs):
        in_refs = refs[:9]
        cos_ref, sin_ref, dq_ref, dkv_ref = refs[9:]
        cs = cos_ref[...]
        sn = sin_ref[...]
        for g in range(len(BRANCHES)):
            dq_g, dk_g, dv_g = in_refs[3 * g:3 * g + 3]
            for j in range(D // 128):
                lanes = slice(j * 128, (j + 1) * 128)
                out_lanes = slice(g * D + j * 128, g * D + (j + 1) * 128)
                v_lanes = slice(QW + g * D + j * 128, QW + g * D + (j + 1) * 128)
                dq_ref[:, out_lanes] = (unrope(dq_g[:, lanes], cs, sn) * scale).astype(dq_ref.dtype)
                dkv_ref[:, out_lanes] = unrope(dk_g[:, lanes], cs, sn).astype(dkv_ref.dtype)
                dkv_ref[:, v_lanes] = dv_g[:, lanes].astype(dkv_ref.dtype)

    operands = [a for branch in grads for a in branch]
    tab = pl.BlockSpec((tm, 128), lambda i: (i, 0))
    return pl.pallas_call(
        body, name=name, grid=(S // tm,),
        in_specs=[_row_spec(tm)] * 9 + [tab, tab],
        out_specs=[pl.BlockSpec((tm, QW), lambda i: (i, 0)), pl.BlockSpec((tm, 2 * QW), lambda i: (i, 0))],
        out_shape=[_sds((S, QW), BF16), _sds((S, 2 * QW), BF16)],
        compiler_params=_cparams(1),
    )(*operands, cos_t, sin_t)


def _adamw(name, parts, w, m, v):
    n, rows, cols = parts.shape
    tr = rows
    for cand in (256, 176, 128, 64, 32, 16, 8):
        if rows % cand == 0:
            tr = cand
            break
    c1 = 1.0 / (1.0 - ADAM_B1 ** ADAM_STEP)
    c2 = 1.0 / (1.0 - ADAM_B2 ** ADAM_STEP)

    def body(p_ref, w_ref, m_ref, v_ref, g_ref, d_ref, nm_ref, nv_ref):
        g = p_ref[0].astype(F32)
        for j in range(1, n):
            g = g + p_ref[j].astype(F32)
        nm = ADAM_B1 * m_ref[...] + (1.0 - ADAM_B1) * g
        nv = ADAM_B2 * v_ref[...] + (1.0 - ADAM_B2) * (g * g)
        g_ref[...] = g
        nm_ref[...] = nm
        nv_ref[...] = nv
        d_ref[...] = -ADAM_LR * ((nm * c1) / (jnp.sqrt(nv * c2) + ADAM_EPS) + ADAM_WD * w_ref[...])

    blk = pl.BlockSpec((tr, cols), lambda i: (i, 0))
    return pl.pallas_call(
        body, name=name, grid=(rows // tr,),
        in_specs=[pl.BlockSpec((n, tr, cols), lambda i: (0, i, 0)), blk, blk, blk],
        out_specs=[blk] * 4, out_shape=[_sds((rows, cols), F32)] * 4,
        compiler_params=_cparams(1),
    )(parts, w, m, v)


def _exchange(name, arrays, kind):
    n = len(arrays)
    gather = kind == "gather"
    out_shape = [_sds((NDEV,) + a.shape if gather else a.shape, a.dtype) for a in arrays]

    def body(*refs):
        srcs, outs = refs[:n], refs[n:2 * n]
        send_sems, recv_sems, local_sems = refs[2 * n:]
        x, y, c = lax.axis_index("x"), lax.axis_index("y"), lax.axis_index("c")
        me = 4 * x + 2 * y + c
        pending = []
        for t in range(n):
            own = pltpu.make_async_copy(srcs[t] if gather else srcs[t].at[me], outs[t].at[me], local_sems.at[t])
            own.start()
            pending.append(own)
            for rel in range(1, NDEV):
                px = 1 - x if rel & 4 else x
                py = 1 - y if rel & 2 else y
                pc = 1 - c if rel & 1 else c
                peer = 4 * px + 2 * py + pc
                send = pltpu.make_async_remote_copy(
                    src_ref=srcs[t] if gather else srcs[t].at[peer], dst_ref=outs[t].at[me],
                    send_sem=send_sems.at[t, rel - 1], recv_sem=recv_sems.at[t, rel - 1],
                    device_id=(px, py, pc), device_id_type=MESH)
                send.start()
                arrive = pltpu.make_async_remote_copy(
                    src_ref=srcs[t] if gather else srcs[t].at[me], dst_ref=outs[t].at[peer],
                    send_sem=send_sems.at[t, rel - 1], recv_sem=recv_sems.at[t, rel - 1],
                    device_id=(px, py, pc), device_id_type=MESH)
                pending.append((send, arrive))
        for item in pending:
            if isinstance(item, tuple):
                item[0].wait_send()
                item[1].wait_recv()
            else:
                item.wait()

    any_spec = pl.BlockSpec(memory_space=pl.ANY)
    outs = pl.pallas_call(
        body, name=name,
        in_specs=[any_spec] * n, out_specs=[any_spec] * n, out_shape=out_shape,
        scratch_shapes=[pltpu.SemaphoreType.DMA((n, NDEV - 1)), pltpu.SemaphoreType.DMA((n, NDEV - 1)),
                        pltpu.SemaphoreType.DMA((n,))],
    )(*arrays)
    return list(outs)


_HBM_SPEC = pl.BlockSpec(memory_space=pltpu.HBM)
_SEM_SPEC = pl.BlockSpec(memory_space=pltpu.SEMAPHORE)
_DATAFLOW = pltpu.SideEffectType.DATAFLOW_SIDE_EFFECTING


def _peers():
    x, y, c = lax.axis_index("x"), lax.axis_index("y"), lax.axis_index("c")
    out = []
    for rel in range(1, NDEV):
        px = 1 - x if rel & 4 else x
        py = 1 - y if rel & 2 else y
        pc = 1 - c if rel & 1 else c
        out.append((rel - 1, (px, py, pc), 4 * px + 2 * py + pc))
    return 4 * x + 2 * y + c, out


def _hbm(a):
    return pltpu.HBM(a.shape, a.dtype)


def _own_slot(a, me, kind):
    mine = a[None] if kind == "gather" else lax.dynamic_slice_in_dim(a, me, 1, axis=0)
    shape = (NDEV,) + mine.shape[1:]
    return lax.dynamic_update_slice_in_dim(lax.empty(shape, a.dtype), mine, me, axis=0)


def _exchange_start(name, arrays, me, kind):
    n = len(arrays)
    gather = kind == "gather"
    lands = [_own_slot(a, me, kind) for a in arrays]

    def body(*refs):
        src_refs, land_refs = refs[:n], refs[n:2 * n]
        send_sems, recv_sems = refs[2 * n], refs[2 * n + 1]
        token = refs[-1]
        my_block, peers = _peers()
        for t in range(n):
            for slot, dev, block in peers:
                pltpu.make_async_remote_copy(
                    src_ref=src_refs[t] if gather else src_refs[t].at[block], dst_ref=land_refs[t].at[my_block],
                    send_sem=send_sems.at[t * (NDEV - 1) + slot], recv_sem=recv_sems.at[t * (NDEV - 1) + slot],
                    device_id=dev, device_id_type=MESH).start()
        token[...] = jnp.zeros_like(token)

    operands = [pltpu.with_memory_space_constraint(a, pltpu.HBM) for a in list(arrays) + lands]
    outs = pl.pallas_call(
        body, name=name,
        out_shape=(pltpu.SemaphoreType.DMA((n * (NDEV - 1),)), pltpu.SemaphoreType.DMA((n * (NDEV - 1),)),
                   *[_hbm(a) for a in operands], _sds((8, 128), F32)),
        in_specs=[_HBM_SPEC] * (2 * n),
        out_specs=(_SEM_SPEC, _SEM_SPEC, *[_HBM_SPEC] * (2 * n), pl.BlockSpec(memory_space=pltpu.VMEM)),
        input_output_aliases={i: 2 + i for i in range(2 * n)},
        compiler_params=pltpu.CompilerParams(has_side_effects=_DATAFLOW),
    )(*operands)
    return (outs[0], outs[1], list(outs[2:2 + n]), list(outs[2 + n:2 + 2 * n])), outs[-1]


def _exchange_wait(name, started, t, after, kind):
    send_sems, recv_sems, srcs, lands = started
    gather = kind == "gather"

    def body(src_ref, land_ref, send_ref, recv_ref, after_ref, src_out, land_out):
        _, peers = _peers()
        for slot, dev, block in peers:
            copy = pltpu.make_async_remote_copy(
                src_ref=src_ref if gather else src_ref.at[block], dst_ref=land_ref.at[block],
                send_sem=send_ref.at[t * (NDEV - 1) + slot], recv_sem=recv_ref.at[t * (NDEV - 1) + slot],
                device_id=dev, device_id_type=MESH)
            copy.wait_send()
            copy.wait_recv()

    return pl.pallas_call(
        body, name=name, out_shape=(_hbm(srcs[t]), _hbm(lands[t])),
        in_specs=(_HBM_SPEC, _HBM_SPEC, _SEM_SPEC, _SEM_SPEC, pl.BlockSpec(memory_space=pl.ANY)),
        out_specs=(_HBM_SPEC, _HBM_SPEC), input_output_aliases={0: 0, 1: 1},
        compiler_params=pltpu.CompilerParams(has_side_effects=_DATAFLOW),
    )(srcs[t], lands[t], send_sems, recv_sems, after)[1]


def _ffn_fwd(tag, h, g_pre, weight):
    n = _rms_fwd(f"ffn_prenorm_{tag}", h, [g_pre])[0]
    wg = weight(f"gate_up_{tag}", n)
    gu = _fwd_cols_blocked(f"ffn_gate_up_{tag}", n, wg).reshape(2, NFB, S, FB)
    act = _swiglu_fwd(f"ffn_act_{tag}", gu)
    wd4 = weight(f"down_{tag}", act).reshape(NFB, FB, D)
    f = _fwd_kblocked(f"ffn_down_{tag}", act, wd4)
    return (n, gu, act, wg, wd4), f


def _ffn_bwd(tag, dh_out, h_in, f, saved, g_pre, g_post, send):
    n, gu, act, wg, wd4 = saved
    df, (dg_post,) = _rms_bwd(f"ffn_postnorm_bwd_{tag}", f, [(g_post, dh_out)], None, BF16)
    tok = send(f"down_{tag}", _bwd_w_kblocked(f"ffn_down_dw_{tag}", act, df).reshape(NDEV, DFF // NDEV, D))
    da = _bwd_x_kblocked(f"ffn_down_dx_{tag}", df, wd4, after=tok)
    dgu = _swiglu_bwd(f"ffn_act_bwd_{tag}", gu, da).reshape(NDEV, S, FB)
    tok = send(f"gate_up_{tag}", _bwd_w_cols_blocked(f"ffn_gate_up_dw_{tag}", n, dgu))
    dn = _bwd_x_cols_blocked(f"ffn_gate_up_dx_{tag}", dgu, wg, after=tok)
    dh_in, (dg_pre,) = _rms_bwd(f"ffn_prenorm_bwd_{tag}", h_in, [(g_pre, dn)], dh_out, F32)
    return dh_in, dg_pre, dg_post


def _to_classes(t, d):
    return t.reshape(S // d, d * D)


def kernel(x, positions, mix_norm_pre, mix_norm_post, ffn_norm_pre, ffn_norm_post, ffn_w_gate_up, ffn_w_down, conv_w_in, conv_w, conv_w_out, kv_norm, w_kv, w_q, w_o, loss_target, m_mix_norm_pre, m_mix_norm_post, m_ffn_norm_pre, m_ffn_norm_post, m_ffn_w_gate_up, m_ffn_w_down, m_conv_w_in, m_conv_w, m_conv_w_out, m_kv_norm, m_w_kv, m_w_q, m_w_o, v_mix_norm_pre, v_mix_norm_post, v_ffn_norm_pre, v_ffn_norm_post, v_ffn_w_gate_up, v_ffn_w_down, v_conv_w_in, v_conv_w, v_conv_w_out, v_kv_norm, v_w_kv, v_w_q, v_w_o):
    me = 4 * lax.axis_index("x") + 2 * lax.axis_index("y") + lax.axis_index("c")
    h0 = x.reshape(S, D)
    target = loss_target.reshape(S, D)
    row = lambda a, l: a[l].reshape(1, D)
    g_kv = kv_norm.reshape(1, D)

    cw_shard = jnp.pad(conv_w[0], ((0, 5), (0, 0)))
    names = ["conv_in", "conv_w", "conv_out", "gate_up_0", "down_0", "kv", "q", "o", "gate_up_1", "down_1"]
    shards = [conv_w_in[0], cw_shard, conv_w_out[0], ffn_w_gate_up[0], ffn_w_down[0],
              w_kv, w_q[0], w_o[0], ffn_w_gate_up[1], ffn_w_down[1]]
    shards = [s if n == "conv_w" else s.astype(BF16) for n, s in zip(names, shards)]
    gather, _ = _exchange_start("gather_weights_start", shards, me, "gather")

    def weight(name, after):
        return _exchange_wait(f"gather_wait_{name}", gather, names.index(name), after, "gather")

    sent = {}

    def send(name, grad):
        sent[name], token = _exchange_start(f"scatter_start_{name}", [grad], me, "scatter")
        return token

    n0 = _rms_fwd("mix_prenorm_0", h0, [row(mix_norm_pre, 0)])[0]
    win_g = weight("conv_in", n0)
    cw = weight("conv_w", n0).transpose(1, 0, 2).reshape(8, D)
    z = _fwd_cols("conv_in", n0, win_g)
    pre = _conv_fwd("conv_gate", z, cw)
    wout = weight("conv_out", pre).reshape(D, D)
    y0 = _fwd_rows("conv_out", pre, wout)
    h1 = _resid_rms("mix_postnorm_0", h0, y0, row(mix_norm_post, 0))
    ffn0, f0 = _ffn_fwd("0", h1, row(ffn_norm_pre, 0), weight)
    h2 = _resid_rms("ffn_postnorm_0", h1, f0, row(ffn_norm_post, 0))

    nk, n2 = _rms_fwd("kv_and_mix_prenorm_1", h2, [g_kv, row(mix_norm_pre, 1)])
    wkv_g = weight("kv", nk)
    kv = _fwd_cols("kv_proj", nk, wkv_g)
    wq_g = weight("q", kv)
    q_raw = _fwd_cols("q_proj", n2, wq_g)
    half = HEAD_DIM // 2
    inv_freq = ROPE_THETA ** (-jnp.arange(half, dtype=F32) / half)
    cos_t, sin_t = _rope_tables("rope_tables", positions.reshape(S, 1), jnp.tile(inv_freq, 4).reshape(1, 128))
    q = _rope_fwd("rope_q", q_raw, 0, cos_t, sin_t, HEAD_DIM ** -0.5)
    k = _rope_fwd("rope_k", kv, 0, cos_t, sin_t, 1.0)
    qc, kc, vc, lse_c, o_br, lse_br = [], [], [], [], [], []
    for g, (_, d) in enumerate(BRANCHES):
        qc.append(_to_classes(q[:, g * D:(g + 1) * D], d))
        kc.append(_to_classes(k[:, g * D:(g + 1) * D], d))
        vc.append(_to_classes(kv[:, QW + g * D:QW + (g + 1) * D], d))
        o_g, lse_g = _attn_fwd(f"attn_fwd_{g}", qc[g], kc[g], vc[g], d)
        lse_c.append(lse_g)
        o_br.append(o_g.reshape(S, D))
        lse_br.append(lse_g.reshape(S, D))
    o_mix = _mix_fwd("attn_mix", o_br, lse_br)
    wo = weight("o", o_mix).reshape(D, D)
    y1 = _fwd_rows("attn_out", o_mix, wo)
    h3 = _resid_rms("mix_postnorm_1", h2, y1, row(mix_norm_post, 1))
    ffn1, f1 = _ffn_fwd("1", h3, row(ffn_norm_pre, 1), weight)
    h4 = _resid_rms("ffn_postnorm_1", h3, f1, row(ffn_norm_post, 1))

    dh4, sq = _loss_grad("loss", h4, target)
    loss = lax.psum(jnp.sum(sq) * (0.5 / D), ("x", "y", "c"))

    dh3, dg_fpre1, dg_fpost1 = _ffn_bwd(
        "1", dh4, h3, f1, ffn1, row(ffn_norm_pre, 1), row(ffn_norm_post, 1), send)
    dy1, (dg_mpost1,) = _rms_bwd("mix_postnorm_bwd_1", y1, [(row(mix_norm_post, 1), dh3)], None, BF16)
    tok = send("o", _bwd_w_rows("attn_out_dw", o_mix, dy1).reshape(NDEV, D // NDEV, D))
    do = _bwd_x_rows("attn_out_dx", dy1, wo, F32, after=tok)
    lane = jnp.arange(128)
    ones_blockdiag = (lane[:, None] // HEAD_DIM == lane[None, :] // HEAD_DIM).astype(BF16)
    mixed = _mix_bwd("attn_mix_bwd", do, o_br, lse_br, ones_blockdiag)
    branch_grads = []
    for g, (_, d) in enumerate(BRANCHES):
        res = _attn_bwd(f"attn_bwd_{g}", qc[g], kc[g], vc[g], _to_classes(mixed[g], d),
                        lse_c[g], _to_classes(mixed[3 + g], d), d)
        branch_grads.append([r.reshape(S, D) for r in res])
    dq_raw, dkv = _attn_bwd_post("attn_bwd_post", branch_grads, cos_t, sin_t)
    tok = send("kv", _bwd_w_cols("kv_proj_dw", nk, dkv, 2 * QW // NDEV))
    dnk = _bwd_x_cols("kv_proj_dx", dkv, wkv_g, after=tok)
    tok = send("q", _bwd_w_cols("q_proj_dw", n2, dq_raw, QW // NDEV))
    dn2 = _bwd_x_cols("q_proj_dx", dq_raw, wq_g, after=tok)
    dh2, (dg_kv, dg_mpre1) = _rms_bwd("kv_and_mix_prenorm_bwd_1", h2,
                                      [(g_kv, dnk), (row(mix_norm_pre, 1), dn2)], dh3, F32)

    dh1, dg_fpre0, dg_fpost0 = _ffn_bwd(
        "0", dh2, h1, f0, ffn0, row(ffn_norm_pre, 0), row(ffn_norm_post, 0), send)
    dy0, (dg_mpost0,) = _rms_bwd("mix_postnorm_bwd_0", y0, [(row(mix_norm_post, 0), dh1)], None, BF16)
    tok = send("conv_out", _bwd_w_rows("conv_out_dw", pre, dy0).reshape(NDEV, D // NDEV, D))
    dpre = _bwd_x_rows("conv_out_dx", dy0, wout, BF16, after=tok)
    dz, dcw = _conv_bwd("conv_gate_bwd", z, dpre, cw)
    tok = send("conv_in", _bwd_w_cols("conv_in_dw", n0, dz, 3 * D // NDEV))
    dn0 = _bwd_x_cols("conv_in_dx", dz, win_g, after=tok)
    dh0, (dg_mpre0,) = _rms_bwd("mix_prenorm_bwd_0", h0, [(row(mix_norm_pre, 0), dn0)], dh1, F32)

    small = jnp.concatenate([dg_mpre0, dg_mpre1, dg_mpost0, dg_mpost1, dg_fpre0, dg_fpre1, dg_fpost0, dg_fpost1,
                             dg_kv, jnp.zeros((7, D), F32), dcw], axis=0)
    small_all = _exchange("gather_small_grads", [small], "gather")[0]

    done = [small_all]

    def upd(tag, w, m, v):
        parts = _exchange_wait(f"scatter_wait_{tag}", sent[tag], 0, done[-1], "scatter")
        shape = w.shape
        flat = lambda a: a.reshape(parts.shape[1:])
        res = _adamw(f"adamw_{tag}", parts, flat(w), flat(m), flat(v))
        done.append(res[0])
        return [r.reshape(shape) for r in res]

    def upd_layer(tag, l, w, m, v):
        return upd(f"{tag}_{l}", w[l], m[l], v[l])

    def stack(per_layer):
        return [jnp.stack([per_layer[0][i], per_layer[1][i]]) for i in range(4)]

    gains_w = jnp.concatenate([mix_norm_pre, mix_norm_post, ffn_norm_pre, ffn_norm_post, g_kv, jnp.zeros((15, D), F32)])
    gains_m = jnp.concatenate([m_mix_norm_pre, m_mix_norm_post, m_ffn_norm_pre, m_ffn_norm_post,
                               m_kv_norm.reshape(1, D), jnp.zeros((15, D), F32)])
    gains_v = jnp.concatenate([v_mix_norm_pre, v_mix_norm_post, v_ffn_norm_pre, v_ffn_norm_post,
                               v_kv_norm.reshape(1, D), jnp.ones((15, D), F32)])
    small_res = _adamw("adamw_gains", small_all, gains_w, gains_m, gains_v)
    dcw_mine = lax.dynamic_slice(small_res[0], (16, me * 128), (8, 128))
    pad8 = lambda a, fill: jnp.pad(a[0], ((0, 5), (0, 0)), constant_values=fill)
    cw_res = [r[0:3].reshape(1, 3, 128) for r in
              _adamw("adamw_conv_w", dcw_mine.reshape(1, 8, 128), cw_shard, pad8(m_conv_w, 0.0), pad8(v_conv_w, 1.0))]

    res = {
        "mix_norm_pre": [r[0:2] for r in small_res],
        "mix_norm_post": [r[2:4] for r in small_res],
        "ffn_norm_pre": [r[4:6] for r in small_res],
        "ffn_norm_post": [r[6:8] for r in small_res],
        "kv_norm": [r[8] for r in small_res],
        "conv_w": cw_res,
    }
    down, gate_up = {}, {}
    down[1] = upd_layer("down", 1, ffn_w_down, m_ffn_w_down, v_ffn_w_down)
    gate_up[1] = upd_layer("gate_up", 1, ffn_w_gate_up, m_ffn_w_gate_up, v_ffn_w_gate_up)
    res["w_o"] = upd("o", w_o, m_w_o, v_w_o)
    res["w_q"] = upd("q", w_q, m_w_q, v_w_q)
    res["w_kv"] = upd("kv", w_kv, m_w_kv, v_w_kv)
    down[0] = upd_layer("down", 0, ffn_w_down, m_ffn_w_down, v_ffn_w_down)
    gate_up[0] = upd_layer("gate_up", 0, ffn_w_gate_up, m_ffn_w_gate_up, v_ffn_w_gate_up)
    res["ffn_w_down"] = stack(down)
    res["ffn_w_gate_up"] = stack(gate_up)
    res["conv_w_out"] = upd("conv_out", conv_w_out, m_conv_w_out, v_conv_w_out)
    res["conv_w_in"] = upd("conv_in", conv_w_in, m_conv_w_in, v_conv_w_in)
    order = ["mix_norm_pre", "mix_norm_post", "ffn_norm_pre", "ffn_norm_post", "ffn_w_gate_up", "ffn_w_down",
             "conv_w_in", "conv_w", "conv_w_out", "kv_norm", "w_kv", "w_q", "w_o"]
    out = [loss, dh0.reshape(1, S, D)]
    for i in range(4):
        out += [res[name][i] for name in order]
    return tuple(out)
```

```python
import jax
import jax.numpy as jnp
from jax import lax
from jax.experimental import pallas as pl
from jax.experimental.pallas import tpu as pltpu

F32 = jnp.float32
BF16 = jnp.bfloat16

S = 4096
D = 1024
NDEV = 8
HEAD_DIM = 64
QW = 3072
DFF = 2816
FB = 704
NFB = 4
BRANCHES = ((128, 1), (512, 4), (2048, 16))
BAND = 128
ROPE_THETA = 10000.0
RMS_EPS = 1e-6
NEG_INF = -1e30
ADAM_LR, ADAM_B1, ADAM_B2, ADAM_EPS, ADAM_WD, ADAM_STEP = 0.001, 0.9, 0.999, 1e-08, 0.01, 10

VMEM_LIMIT_BYTES = 52 * 1024 * 1024
ROW_TILE = 512
MESH = pl.DeviceIdType.MESH


def _cparams(ngrid):
    return pltpu.CompilerParams(dimension_semantics=("arbitrary",) * ngrid,
                                vmem_limit_bytes=VMEM_LIMIT_BYTES)


def _sds(shape, dtype):
    return jax.ShapeDtypeStruct(tuple(shape), dtype)


_DIMS = {"nn": (((1,), (0,)), ((), ())),
         "nt": (((1,), (1,)), ((), ())),
         "tn": (((0,), (0,)), ((), ()))}


def _matmul(name, a, b, *, mode, grid, a_blk, a_map, b_blk, b_map, o_shape, o_blk, o_map, out_dtype, after=None):
    nk = grid[2]
    dims = _DIMS[mode]
    acc_shape = tuple(s for s in o_blk if s is not None)
    extra = [] if after is None else [after]

    def body(a_ref, b_ref, *rest):
        o_ref, scratch = rest[len(extra)], rest[len(extra) + 1:]
        part = lax.dot_general(a_ref[...], b_ref[...], dims, preferred_element_type=F32)
        if nk == 1:
            o_ref[...] = part.astype(o_ref.dtype)
            return
        acc_ref = scratch[0]
        k = pl.program_id(2)

        @pl.when(k == 0)
        def _():
            acc_ref[...] = part

        @pl.when(k > 0)
        def _():
            acc_ref[...] += part

        @pl.when(k == nk - 1)
        def _():
            o_ref[...] = acc_ref[...].astype(o_ref.dtype)

    return pl.pallas_call(
        body, name=name, grid=grid,
        in_specs=[pl.BlockSpec(a_blk, a_map), pl.BlockSpec(b_blk, b_map)] + [pl.BlockSpec(memory_space=pl.ANY)] * len(extra),
        out_specs=pl.BlockSpec(o_blk, o_map),
        out_shape=_sds(o_shape, out_dtype),
        scratch_shapes=[] if nk == 1 else [pltpu.VMEM(acc_shape, F32)],
        compiler_params=_cparams(3),
    )(a, b, *extra)


TM = 1024
TK = 1024


def _fwd_cols(name, a, wg, out_dtype=BF16):
    _, kdim, n = wg.shape
    return _matmul(name, a, wg, mode="nn", grid=(S // TM, NDEV, 1),
                   a_blk=(TM, kdim), a_map=lambda i, j, k: (i, 0),
                   b_blk=(None, kdim, n), b_map=lambda i, j, k: (j, 0, 0),
                   o_shape=(S, NDEV * n), o_blk=(TM, n), o_map=lambda i, j, k: (i, j), out_dtype=out_dtype)


def _fwd_cols_blocked(name, a, wg):
    _, kdim, n = wg.shape
    return _matmul(name, a, wg, mode="nn", grid=(S // TM, NDEV, 1),
                   a_blk=(TM, kdim), a_map=lambda i, j, k: (i, 0),
                   b_blk=(None, kdim, n), b_map=lambda i, j, k: (j, 0, 0),
                   o_shape=(NDEV, S, n), o_blk=(None, TM, n), o_map=lambda i, j, k: (j, i, 0), out_dtype=BF16)


def _fwd_rows(name, a, w, out_dtype=F32):
    kdim, n = w.shape
    tn = 512
    return _matmul(name, a, w, mode="nn", grid=(S // TM, n // tn, 1),
                   a_blk=(TM, kdim), a_map=lambda i, j, k: (i, 0),
                   b_blk=(kdim, tn), b_map=lambda i, j, k: (0, j),
                   o_shape=(S, n), o_blk=(TM, tn), o_map=lambda i, j, k: (i, j), out_dtype=out_dtype)


def _fwd_kblocked(name, a4, w4):
    nb, _, kb = a4.shape
    n = w4.shape[2]
    return _matmul(name, a4, w4, mode="nn", grid=(S // TM, 1, nb),
                   a_blk=(None, TM, kb), a_map=lambda i, j, k: (k, i, 0),
                   b_blk=(None, kb, n), b_map=lambda i, j, k: (k, 0, 0),
                   o_shape=(S, n), o_blk=(TM, n), o_map=lambda i, j, k: (i, 0), out_dtype=F32)


def _bwd_x_cols(name, dy, wg, after=None):
    _, kdim, n = wg.shape
    return _matmul(name, dy, wg, mode="nt", grid=(S // TM, 1, NDEV),
                   a_blk=(TM, n), a_map=lambda i, j, k: (i, k),
                   b_blk=(None, kdim, n), b_map=lambda i, j, k: (k, 0, 0),
                   o_shape=(S, kdim), o_blk=(TM, kdim), o_map=lambda i, j, k: (i, 0), out_dtype=F32, after=after)


def _bwd_x_cols_blocked(name, dy8, wg, after=None):
    _, kdim, n = wg.shape
    return _matmul(name, dy8, wg, mode="nt", grid=(S // TM, 1, NDEV),
                   a_blk=(None, TM, n), a_map=lambda i, j, k: (k, i, 0),
                   b_blk=(None, kdim, n), b_map=lambda i, j, k: (k, 0, 0),
                   o_shape=(S, kdim), o_blk=(TM, kdim), o_map=lambda i, j, k: (i, 0), out_dtype=F32, after=after)


def _bwd_x_rows(name, dy, w, out_dtype, after=None):
    kdim, n = w.shape
    tkk = 512
    return _matmul(name, dy, w, mode="nt", grid=(S // TM, kdim // tkk, 1),
                   a_blk=(TM, n), a_map=lambda i, j, k: (i, 0),
                   b_blk=(tkk, n), b_map=lambda i, j, k: (j, 0),
                   o_shape=(S, kdim), o_blk=(TM, tkk), o_map=lambda i, j, k: (i, j), out_dtype=out_dtype, after=after)


def _bwd_x_kblocked(name, dy, w4, after=None):
    nb, kb, n = w4.shape
    return _matmul(name, dy, w4, mode="nt", grid=(S // TM, nb, 1),
                   a_blk=(TM, n), a_map=lambda i, j, k: (i, 0),
                   b_blk=(None, kb, n), b_map=lambda i, j, k: (j, 0, 0),
                   o_shape=(nb, S, kb), o_blk=(None, TM, kb), o_map=lambda i, j, k: (j, i, 0), out_dtype=BF16, after=after)


def _bwd_w_cols(name, a, dy, n):
    kdim = a.shape[1]
    return _matmul(name, a, dy, mode="tn", grid=(1, NDEV, S // TK),
                   a_blk=(TK, kdim), a_map=lambda i, j, k: (k, 0),
                   b_blk=(TK, n), b_map=lambda i, j, k: (k, j),
                   o_shape=(NDEV, kdim, n), o_blk=(None, kdim, n), o_map=lambda i, j, k: (j, 0, 0), out_dtype=BF16)


def _bwd_w_cols_blocked(name, a, dy8):
    kdim = a.shape[1]
    n = dy8.shape[2]
    return _matmul(name, a, dy8, mode="tn", grid=(1, NDEV, S // TK),
                   a_blk=(TK, kdim), a_map=lambda i, j, k: (k, 0),
                   b_blk=(None, TK, n), b_map=lambda i, j, k: (j, k, 0),
                   o_shape=(NDEV, kdim, n), o_blk=(None, kdim, n), o_map=lambda i, j, k: (j, 0, 0), out_dtype=BF16)


def _bwd_w_rows(name, a, dy):
    kdim = a.shape[1]
    n = dy.shape[1]
    tmm = 512
    return _matmul(name, a, dy, mode="tn", grid=(kdim // tmm, 1, S // TK),
                   a_blk=(TK, tmm), a_map=lambda i, j, k: (k, i),
                   b_blk=(TK, n), b_map=lambda i, j, k: (k, 0),
                   o_shape=(kdim, n), o_blk=(tmm, n), o_map=lambda i, j, k: (i, 0), out_dtype=BF16)


def _bwd_w_kblocked(name, a4, dy):
    nb, _, kb = a4.shape
    n = dy.shape[1]
    return _matmul(name, a4, dy, mode="tn", grid=(nb, 1, S // TK),
                   a_blk=(None, TK, kb), a_map=lambda i, j, k: (i, k, 0),
                   b_blk=(TK, n), b_map=lambda i, j, k: (k, 0),
                   o_shape=(nb, kb, n), o_blk=(None, kb, n), o_map=lambda i, j, k: (i, 0, 0), out_dtype=BF16)


def _rstd(x):
    return lax.rsqrt(jnp.mean(x * x, axis=-1, keepdims=True) + RMS_EPS)


def _row_spec(tm=ROW_TILE, width=D):
    return pl.BlockSpec((tm, width), lambda i: (i, 0))


def _vec_spec(rows=1, width=D):
    return pl.BlockSpec((rows, width), lambda i: (0, 0))


def _rms_fwd(name, x, gains):
    n = len(gains)

    def body(x_ref, *refs):
        x_val = x_ref[...]
        xh = x_val * _rstd(x_val)
        for g_ref, o_ref in zip(refs[:n], refs[n:]):
            o_ref[...] = (xh * g_ref[...]).astype(o_ref.dtype)

    outs = pl.pallas_call(
        body, name=name, grid=(S // ROW_TILE,),
        in_specs=[_row_spec()] + [_vec_spec()] * n,
        out_specs=[_row_spec()] * n,
        out_shape=[_sds((S, D), BF16)] * n,
        compiler_params=_cparams(1),
    )(x, *gains)
    return list(outs)


def _resid_rms(name, h, y, g):
    def body(h_ref, y_ref, g_ref, o_ref):
        y_val = y_ref[...]
        o_ref[...] = h_ref[...] + (y_val * _rstd(y_val)) * g_ref[...]

    return pl.pallas_call(
        body, name=name, grid=(S // ROW_TILE,),
        in_specs=[_row_spec(), _row_spec(), _vec_spec()],
        out_specs=_row_spec(), out_shape=_sds((S, D), F32),
        compiler_params=_cparams(1),
    )(h, y, g)


def _rms_bwd(name, x, pairs, dres, out_dtype):
    n = len(pairs)
    has_res = dres is not None

    def body(x_ref, *refs):
        g_refs = refs[0:2 * n:2]
        dn_refs = refs[1:2 * n:2]
        pos = 2 * n
        res_ref = refs[pos] if has_res else None
        pos += int(has_res)
        dx_ref = refs[pos]
        dg_refs = refs[pos + 1:]
        step = pl.program_id(0)
        x_val = x_ref[...]
        r = _rstd(x_val)
        xh = x_val * r
        acc = res_ref[...] if has_res else jnp.zeros_like(x_val)
        for g_ref, dn_ref, dg_ref in zip(g_refs, dn_refs, dg_refs):
            dn = dn_ref[...].astype(F32)
            dxh = dn * g_ref[...]
            acc = acc + r * (dxh - xh * jnp.mean(dxh * xh, axis=-1, keepdims=True))
            part = jnp.sum(dn * xh, axis=0, keepdims=True)

            @pl.when(step == 0)
            def _():
                dg_ref[...] = jnp.zeros_like(dg_ref)

            dg_ref[0:1, :] += part

        dx_ref[...] = acc.astype(dx_ref.dtype)

    operands = [x]
    in_specs = [_row_spec()]
    for g, dn in pairs:
        operands += [g, dn]
        in_specs += [_vec_spec(), _row_spec()]
    if has_res:
        operands.append(dres)
        in_specs.append(_row_spec())
    outs = pl.pallas_call(
        body, name=name, grid=(S // ROW_TILE,),
        in_specs=in_specs,
        out_specs=[_row_spec()] + [_vec_spec(8)] * n,
        out_shape=[_sds((S, D), out_dtype)] + [_sds((8, D), F32)] * n,
        compiler_params=_cparams(1),
    )(*operands)
    return outs[0], list(outs[1:])


def _loss_grad(name, h, target):
    def body(h_ref, t_ref, dh_ref, part_ref):
        e = h_ref[...] - t_ref[...]
        dh_ref[...] = e * (1.0 / D)
        part = jnp.sum(e * e, axis=0, keepdims=True)
        step = pl.program_id(0)

        @pl.when(step == 0)
        def _():
            part_ref[...] = part

        @pl.when(step > 0)
        def _():
            part_ref[...] += part

    return pl.pallas_call(
        body, name=name, grid=(S // ROW_TILE,),
        in_specs=[_row_spec(), _row_spec()],
        out_specs=[_row_spec(), _vec_spec()],
        out_shape=[_sds((S, D), F32), _sds((1, D), F32)],
        compiler_params=_cparams(1),
    )(h, target)


def _shift_down(u, prev8, k):
    r = pltpu.roll(u, k, 0)
    p = pltpu.roll(prev8, k, 0)
    row = lax.broadcasted_iota(jnp.int32, prev8.shape, 0)
    top = jnp.where(row < k, p, r[0:8])
    return jnp.concatenate([top, r[8:]], axis=0)


def _shift_up(u, next8, k):
    tm = u.shape[0]
    r = pltpu.roll(u, tm - k, 0)
    p = pltpu.roll(next8, 8 - k, 0)
    row = lax.broadcasted_iota(jnp.int32, next8.shape, 0)
    bot = jnp.where(row >= 8 - k, p, r[tm - 8:tm])
    return jnp.concatenate([r[:tm - 8], bot], axis=0)


CONV_TILE = 512


def _halo_prev(col):
    return pl.BlockSpec((8, D), lambda i: (jnp.maximum(i * (CONV_TILE // 8) - 1, 0), col))


def _halo_next(col):
    last = S // 8 - 1
    return pl.BlockSpec((8, D), lambda i: (jnp.minimum((i + 1) * (CONV_TILE // 8), last), col))


def _conv_fwd(name, z, cw):
    def body(b_ref, c_ref, h_ref, cp_ref, hp_ref, cw_ref, o_ref):
        i = pl.program_id(0)
        u = c_ref[...].astype(F32) * h_ref[...].astype(F32)
        up = cp_ref[...].astype(F32) * hp_ref[...].astype(F32)
        up = jnp.where(i > 0, up, 0.0)
        cv = cw_ref[0:1, :] * _shift_down(u, up, 2) + cw_ref[1:2, :] * _shift_down(u, up, 1) + cw_ref[2:3, :] * u
        o_ref[...] = (b_ref[...].astype(F32) * cv).astype(o_ref.dtype)

    col = lambda c: pl.BlockSpec((CONV_TILE, D), lambda i: (i, c))
    return pl.pallas_call(
        body, name=name, grid=(S // CONV_TILE,),
        in_specs=[col(0), col(1), col(2), _halo_prev(1), _halo_prev(2), _vec_spec(8)],
        out_specs=_row_spec(CONV_TILE), out_shape=_sds((S, D), BF16),
        compiler_params=_cparams(1),
    )(z, z, z, z, z, cw)


def _conv_bwd(name, z, dpre, cw):
    nsteps = S // CONV_TILE

    def body(b_ref, c_ref, h_ref, cp_ref, hp_ref, dp_ref, dpn_ref, bn_ref, cw_ref, dz_ref, dcw_ref):
        i = pl.program_id(0)
        b = b_ref[...].astype(F32)
        c = c_ref[...].astype(F32)
        h = h_ref[...].astype(F32)
        dp = dp_ref[...].astype(F32)
        u = c * h
        up = jnp.where(i > 0, cp_ref[...].astype(F32) * hp_ref[...].astype(F32), 0.0)
        s1 = _shift_down(u, up, 1)
        s2 = _shift_down(u, up, 2)
        w0, w1, w2 = cw_ref[0:1, :], cw_ref[1:2, :], cw_ref[2:3, :]
        cv = w0 * s2 + w1 * s1 + w2 * u
        dcv = dp * b
        dcvn = jnp.where(i < nsteps - 1, dpn_ref[...].astype(F32) * bn_ref[...].astype(F32), 0.0)
        du = w2 * dcv + w1 * _shift_up(dcv, dcvn, 1) + w0 * _shift_up(dcv, dcvn, 2)
        dz_ref[:, 0:D] = (dp * cv).astype(dz_ref.dtype)
        dz_ref[:, D:2 * D] = (du * h).astype(dz_ref.dtype)
        dz_ref[:, 2 * D:3 * D] = (du * c).astype(dz_ref.dtype)

        @pl.when(i == 0)
        def _():
            dcw_ref[...] = jnp.zeros_like(dcw_ref)

        dcw_ref[0:1, :] += jnp.sum(dcv * s2, axis=0, keepdims=True)
        dcw_ref[1:2, :] += jnp.sum(dcv * s1, axis=0, keepdims=True)
        dcw_ref[2:3, :] += jnp.sum(dcv * u, axis=0, keepdims=True)

    col = lambda c: pl.BlockSpec((CONV_TILE, D), lambda i: (i, c))
    return pl.pallas_call(
        body, name=name, grid=(nsteps,),
        in_specs=[col(0), col(1), col(2), _halo_prev(1), _halo_prev(2),
                  _row_spec(CONV_TILE), _halo_next(0), _halo_next(0), _vec_spec(8)],
        out_specs=[pl.BlockSpec((CONV_TILE, 3 * D), lambda i: (i, 0)), _vec_spec(8)],
        out_shape=[_sds((S, 3 * D), BF16), _sds((8, D), F32)],
        compiler_params=_cparams(1),
    )(z, z, z, z, z, dpre, dpre, z, cw)


def _swiglu_fwd(name, gu):
    def body(gu_ref, o_ref):
        g = gu_ref[0].astype(F32)
        u = gu_ref[1].astype(F32)
        o_ref[...] = (g * jax.nn.sigmoid(g) * u).astype(o_ref.dtype)

    return pl.pallas_call(
        body, name=name, grid=(NFB, S // ROW_TILE),
        in_specs=[pl.BlockSpec((2, None, ROW_TILE, FB), lambda j, i: (0, j, i, 0))],
        out_specs=pl.BlockSpec((None, ROW_TILE, FB), lambda j, i: (j, i, 0)),
        out_shape=_sds((NFB, S, FB), BF16),
        compiler_params=_cparams(2),
    )(gu)


def _swiglu_bwd(name, gu, da):
    def body(gu_ref, da_ref, o_ref):
        g = gu_ref[0].astype(F32)
        u = gu_ref[1].astype(F32)
        d = da_ref[...].astype(F32)
        sg = jax.nn.sigmoid(g)
        o_ref[0] = (d * u * sg * (1.0 + g * (1.0 - sg))).astype(o_ref.dtype)
        o_ref[1] = (d * g * sg).astype(o_ref.dtype)

    blk = pl.BlockSpec((2, None, ROW_TILE, FB), lambda j, i: (0, j, i, 0))
    return pl.pallas_call(
        body, name=name, grid=(NFB, S // ROW_TILE),
        in_specs=[blk, pl.BlockSpec((None, ROW_TILE, FB), lambda j, i: (j, i, 0))],
        out_specs=blk, out_shape=_sds((2, NFB, S, FB), BF16),
        compiler_params=_cparams(2),
    )(gu, da)


def _rope_tables(name, pos_col, inv_freq_row):
    def body(pos_ref, f_ref, cos_ref, sin_ref):
        ang = pos_ref[...].astype(F32) * f_ref[...]
        lane = lax.broadcasted_iota(jnp.int32, ang.shape, 1)
        s = jnp.sin(ang)
        cos_ref[...] = jnp.cos(ang)
        sin_ref[...] = jnp.where((lane % HEAD_DIM) < HEAD_DIM // 2, -s, s)

    tab = pl.BlockSpec((ROW_TILE, 128), lambda i: (i, 0))
    return pl.pallas_call(
        body, name=name, grid=(S // ROW_TILE,),
        in_specs=[pl.BlockSpec((ROW_TILE, 1), lambda i: (i, 0)), _vec_spec(1, 128)],
        out_specs=[tab, tab], out_shape=[_sds((S, 128), F32)] * 2,
        compiler_params=_cparams(1),
    )(pos_col, inv_freq_row)


def _swap_halves(t):
    lane = lax.broadcasted_iota(jnp.int32, t.shape, 1)
    first = (lane % HEAD_DIM) < HEAD_DIM // 2
    return jnp.where(first, pltpu.roll(t, 128 - HEAD_DIM // 2, 1), pltpu.roll(t, HEAD_DIM // 2, 1))


NCHUNK = D // 128


def _chunk(c, base=0):
    return slice(base + c * 128, base + (c + 1) * 128)


def _class_rows(r, d, tm):
    return pl.ds(r, tm // d, stride=d) if d > 1 else slice(None)


def _class_block(d, tm):
    return pl.BlockSpec((tm // d, d * D), lambda i: (i, 0))


def _tokens_from_classes(blk_ref, tmp_ref, d, tm):
    for r in range(d):
        for c in range(NCHUNK):
            tmp_ref[c, _class_rows(r, d, tm), :] = blk_ref[:, _chunk(c, r * D)]


def _classes_from_tokens(tmp_ref, blk_ref, d, tm):
    for r in range(d):
        for c in range(NCHUNK):
            blk_ref[:, _chunk(c, r * D)] = tmp_ref[c, _class_rows(r, d, tm), :].astype(blk_ref.dtype)


def _proj_classes(name, a, w, col, d, tables, scale):
    kdim = a.shape[1]
    rope = tables is not None

    def body(a_ref, w_ref, *refs):
        if rope:
            cos_ref, sin_ref, o_ref, tmp_ref = refs
        else:
            o_ref, tmp_ref = refs
        acc = _dot_nn(a_ref[...], w_ref[...])
        for c in range(NCHUNK):
            tmp_ref[c] = acc[:, _chunk(c)]
        for r in range(d):
            rows = _class_rows(r, d, TM)
            if rope:
                cs = cos_ref[rows, :]
                sn = sin_ref[rows, :]
            for c in range(NCHUNK):
                x = tmp_ref[c, rows, :]
                if rope:
                    x = (x * cs + _swap_halves(x) * sn) * scale
                o_ref[:, _chunk(c, r * D)] = x.astype(o_ref.dtype)

    tab = pl.BlockSpec((TM, 128), lambda i: (i, 0))
    return pl.pallas_call(
        body, name=name, grid=(S // TM,),
        in_specs=[pl.BlockSpec((TM, kdim), lambda i: (i, 0)), pl.BlockSpec((kdim, D), lambda i: (0, col))]
                 + ([tab, tab] if rope else []),
        out_specs=_class_block(d, TM), out_shape=_sds((S // d, d * D), BF16),
        scratch_shapes=[pltpu.VMEM((NCHUNK, TM, 128), F32)],
        compiler_params=_cparams(1),
    )(a, w, *(tables if rope else ()))


def _class_spec(d):
    return pl.BlockSpec((S // d, 128), lambda cb: (0, cb))


def _dot_nt(a, b):
    return lax.dot_general(a, b, _DIMS["nt"], preferred_element_type=F32)


def _dot_tn(a, b):
    return lax.dot_general(a, b, _DIMS["tn"], preferred_element_type=F32)


def _dot_nn(a, b):
    return lax.dot_general(a, b, _DIMS["nn"], preferred_element_type=F32)


def _band_mask(nkeys):
    qi = lax.broadcasted_iota(jnp.int32, (2 * BAND, nkeys), 0) % BAND
    kj = lax.broadcasted_iota(jnp.int32, (2 * BAND, nkeys), 1)
    if nkeys == BAND:
        return kj <= qi
    dist = qi + BAND - kj
    return (dist >= 0) & (dist <= BAND)


def _stack_heads(x):
    row = lax.broadcasted_iota(jnp.int32, (2 * BAND, 128), 0)
    lane = lax.broadcasted_iota(jnp.int32, (2 * BAND, 128), 1)
    keep = (row < BAND) == (lane < HEAD_DIM)
    return jnp.where(keep, jnp.concatenate([x, x], axis=0), jnp.zeros((), x.dtype))


def _unstack(x2):
    first_head = lax.broadcasted_iota(jnp.int32, (BAND, 128), 1) < HEAD_DIM
    return jnp.where(first_head, x2[:BAND], x2[BAND:])


ATTN_UNROLL = 2


def _for_later_blocks(nblk, fn):
    trips = (nblk - 1) // ATTN_UNROLL
    if trips > 0:
        def step(i, carry):
            for u in range(ATTN_UNROLL):
                fn(pl.multiple_of((1 + i * ATTN_UNROLL + u) * BAND, BAND))
            return carry

        lax.fori_loop(0, trips, step, 0)
    for sb in range(1 + trips * ATTN_UNROLL, nblk):
        fn(sb * BAND)


def _attn_fwd(name, q, k, v, d):
    nblk = S // d // BAND

    def body(q_ref, k_ref, v_ref, o_ref, lse_ref):
        def block(r0, k0, nkeys):
            q2 = _stack_heads(q_ref[pl.ds(r0, BAND), :])
            s = jnp.where(_band_mask(nkeys), _dot_nt(q2, k_ref[pl.ds(k0, nkeys), :]), NEG_INF)
            m = jnp.max(s, axis=-1, keepdims=True)
            p = jnp.exp(s - m)
            l = jnp.sum(p, axis=-1, keepdims=True)
            o2 = _dot_nn(p.astype(BF16), v_ref[pl.ds(k0, nkeys), :]) / l
            lse2 = jnp.broadcast_to(m + jnp.log(l), (2 * BAND, 128))
            o_ref[pl.ds(r0, BAND), :] = _unstack(o2)
            lse_ref[pl.ds(r0, BAND), :] = _unstack(lse2)

        block(0, 0, BAND)

        _for_later_blocks(nblk, lambda r0: block(r0, r0 - BAND, 2 * BAND))

    spec = _class_spec(d)
    return pl.pallas_call(
        body, name=name, grid=(8 * d,),
        in_specs=[spec] * 3, out_specs=[spec] * 2,
        out_shape=[_sds((S // d, d * D), F32)] * 2,
        compiler_params=_cparams(1),
    )(q, k, v)


def _attn_bwd(name, q, k, v, do, lse, dd, d):
    nblk = S // d // BAND

    def body(q_ref, k_ref, v_ref, do_ref, lse_ref, dd_ref, dq_ref, dk_ref, dv_ref):
        def column(ref, r0):
            rows = pl.ds(r0, BAND)
            return jnp.concatenate([ref[rows, 0:1], ref[rows, HEAD_DIM:HEAD_DIM + 1]], axis=0)

        def block(r0, k0, nkeys, first):
            q2 = _stack_heads(q_ref[pl.ds(r0, BAND), :])
            do2 = _stack_heads(do_ref[pl.ds(r0, BAND), :])
            kk = k_ref[pl.ds(k0, nkeys), :]
            vv = v_ref[pl.ds(k0, nkeys), :]
            s = jnp.where(_band_mask(nkeys), _dot_nt(q2, kk), NEG_INF)
            p = jnp.exp(s - column(lse_ref, r0))
            ds = (p * (_dot_nt(do2, vv) - column(dd_ref, r0))).astype(BF16)
            dq_ref[pl.ds(r0, BAND), :] = _unstack(_dot_nn(ds, kk))
            dk_part = _dot_tn(ds, q2)
            dv_part = _dot_tn(p.astype(BF16), do2)
            if first:
                dk_ref[pl.ds(k0, nkeys), :] = dk_part
                dv_ref[pl.ds(k0, nkeys), :] = dv_part
            else:
                dk_ref[pl.ds(k0, BAND), :] += dk_part[:BAND]
                dv_ref[pl.ds(k0, BAND), :] += dv_part[:BAND]
                dk_ref[pl.ds(k0 + BAND, BAND), :] = dk_part[BAND:]
                dv_ref[pl.ds(k0 + BAND, BAND), :] = dv_part[BAND:]

        block(0, 0, BAND, True)

        _for_later_blocks(nblk, lambda r0: block(r0, r0 - BAND, 2 * BAND, False))

    spec = _class_spec(d)
    return pl.pallas_call(
        body, name=name, grid=(8 * d,),
        in_specs=[spec] * 6, out_specs=[spec] * 3,
        out_shape=[_sds((S // d, d * D), F32)] * 3,
        compiler_params=_cparams(1),
    )(q, k, v, do, lse, dd)


MIX_TILE = 256
DILATIONS = tuple(d for _, d in BRANCHES)


def _branch_weights(la, lb, lc):
    m = jnp.maximum(jnp.maximum(la, lb), lc)
    ea, eb, ec = jnp.exp(la - m), jnp.exp(lb - m), jnp.exp(lc - m)
    den = ea + eb + ec
    return ea / den, eb / den, ec / den


def _mix_operands(outs, lses):
    specs = [_class_block(d, MIX_TILE) for d in DILATIONS] * 2
    scratch = [pltpu.VMEM((NCHUNK, MIX_TILE, 128), F32)] * 4
    return list(outs) + list(lses), specs, scratch


def _mix_fwd(name, outs, lses):
    def body(o0, o1, o2, l0, l1, l2, o_ref, to1, to2, tl1, tl2):
        for blk, tmp, d in ((o1, to1, DILATIONS[1]), (o2, to2, DILATIONS[2]), (l1, tl1, DILATIONS[1]), (l2, tl2, DILATIONS[2])):
            _tokens_from_classes(blk, tmp, d, MIX_TILE)
        for c in range(NCHUNK):
            wa, wb, wc = _branch_weights(l0[:, _chunk(c)], tl1[c], tl2[c])
            o_ref[:, _chunk(c)] = (wa * o0[:, _chunk(c)] + wb * to1[c] + wc * to2[c]).astype(o_ref.dtype)

    operands, specs, scratch = _mix_operands(outs, lses)
    return pl.pallas_call(
        body, name=name, grid=(S // MIX_TILE,),
        in_specs=specs, out_specs=_row_spec(MIX_TILE), out_shape=_sds((S, D), BF16),
        scratch_shapes=scratch, compiler_params=_cparams(1),
    )(*operands)


def _head_sum(x, ones_blockdiag):
    hi = x.astype(BF16)
    r1 = x - hi.astype(F32)
    mid = r1.astype(BF16)
    lo = (r1 - mid.astype(F32)).astype(BF16)
    return _dot_nn(hi, ones_blockdiag) + _dot_nn(mid, ones_blockdiag) + _dot_nn(lo, ones_blockdiag)


def _mix_bwd(name, do, outs, lses, ones_blockdiag):
    def body(do_ref, o0, o1, o2, l0, l1, l2, ones_ref, d0, d1, d2, t0, t1, t2,
             to1, to2, tl1, tl2, td1, td2, tt1, tt2):
        for blk, tmp, d in ((o1, to1, DILATIONS[1]), (o2, to2, DILATIONS[2]), (l1, tl1, DILATIONS[1]), (l2, tl2, DILATIONS[2])):
            _tokens_from_classes(blk, tmp, d, MIX_TILE)
        ones = ones_ref[...]
        for c in range(NCHUNK):
            w = _branch_weights(l0[:, _chunk(c)], tl1[c], tl2[c])
            dov = do_ref[:, _chunk(c)]
            o = w[0] * o0[:, _chunk(c)] + w[1] * to1[c] + w[2] * to2[c]
            t = _head_sum(dov * o, ones)
            d0[:, _chunk(c)] = (w[0] * dov).astype(d0.dtype)
            t0[:, _chunk(c)] = w[0] * t
            td1[c], tt1[c] = w[1] * dov, w[1] * t
            td2[c], tt2[c] = w[2] * dov, w[2] * t
        for tmp, blk, d in ((td1, d1, DILATIONS[1]), (tt1, t1, DILATIONS[1]), (td2, d2, DILATIONS[2]), (tt2, t2, DILATIONS[2])):
            _classes_from_tokens(tmp, blk, d, MIX_TILE)

    operands, specs, scratch = _mix_operands(outs, lses)
    out_specs = [_class_block(d, MIX_TILE) for d in DILATIONS] * 2
    out_shape = [_sds((S // d, d * D), BF16) for d in DILATIONS] + [_sds((S // d, d * D), F32) for d in DILATIONS]
    return pl.pallas_call(
        body, name=name, grid=(S // MIX_TILE,),
        in_specs=[_row_spec(MIX_TILE)] + specs + [_vec_spec(128, 128)],
        out_specs=out_specs, out_shape=out_shape,
        scratch_shapes=scratch + [pltpu.VMEM((NCHUNK, MIX_TILE, 128), F32)] * 4,
        compiler_params=_cparams(1),
    )(do, *operands, ones_blockdiag)


def _attn_bwd_post(name, grads, cos_t, sin_t):
    tm = MIX_TILE
    scale = HEAD_DIM ** -0.5

    def unrope(x, cs, sn):
        return x * cs - _swap_halves(x) * sn

    def body(*refs):
        in_refs = refs[:9]
        cos_ref, sin_ref, dq_ref, dkv_ref, tmp_ref = refs[9:]
        cs = cos_ref[...]
        sn = sin_ref[...]
        for g, d in enumerate(DILATIONS):
            for which, blk in enumerate(in_refs[3 * g:3 * g + 3]):
                if d > 1:
                    _tokens_from_classes(blk, tmp_ref, d, tm)
                for c in range(NCHUNK):
                    x = tmp_ref[c] if d > 1 else blk[:, _chunk(c)]
                    if which == 0:
                        dq_ref[:, _chunk(c, g * D)] = (unrope(x, cs, sn) * scale).astype(dq_ref.dtype)
                    elif which == 1:
                        dkv_ref[:, _chunk(c, g * D)] = unrope(x, cs, sn).astype(dkv_ref.dtype)
                    else:
                        dkv_ref[:, _chunk(c, QW + g * D)] = x.astype(dkv_ref.dtype)

    operands = [a for branch in grads for a in branch]
    tab = pl.BlockSpec((tm, 128), lambda i: (i, 0))
    return pl.pallas_call(
        body, name=name, grid=(S // tm,),
        in_specs=[_class_block(d, tm) for d in DILATIONS for _ in range(3)] + [tab, tab],
        out_specs=[pl.BlockSpec((tm, QW), lambda i: (i, 0)), pl.BlockSpec((tm, 2 * QW), lambda i: (i, 0))],
        out_shape=[_sds((S, QW), BF16), _sds((S, 2 * QW), BF16)],
        scratch_shapes=[pltpu.VMEM((NCHUNK, tm, 128), F32)],
        compiler_params=_cparams(1),
    )(*operands, cos_t, sin_t)


def _adamw(name, parts, w, m, v):
    n, rows, cols = parts.shape
    tr = rows
    for cand in (256, 176, 128, 64, 32, 16, 8):
        if rows % cand == 0:
            tr = cand
            break
    c1 = 1.0 / (1.0 - ADAM_B1 ** ADAM_STEP)
    c2 = 1.0 / (1.0 - ADAM_B2 ** ADAM_STEP)

    def body(p_ref, w_ref, m_ref, v_ref, g_ref, d_ref, nm_ref, nv_ref):
        g = p_ref[0].astype(F32)
        for j in range(1, n):
            g = g + p_ref[j].astype(F32)
        nm = ADAM_B1 * m_ref[...] + (1.0 - ADAM_B1) * g
        nv = ADAM_B2 * v_ref[...] + (1.0 - ADAM_B2) * (g * g)
        g_ref[...] = g
        nm_ref[...] = nm
        nv_ref[...] = nv
        d_ref[...] = -ADAM_LR * ((nm * c1) / (jnp.sqrt(nv * c2) + ADAM_EPS) + ADAM_WD * w_ref[...])

    blk = pl.BlockSpec((tr, cols), lambda i: (i, 0))
    return pl.pallas_call(
        body, name=name, grid=(rows // tr,),
        in_specs=[pl.BlockSpec((n, tr, cols), lambda i: (0, i, 0)), blk, blk, blk],
        out_specs=[blk] * 4, out_shape=[_sds((rows, cols), F32)] * 4,
        compiler_params=_cparams(1),
    )(parts, w, m, v)


def _exchange(name, arrays, kind):
    n = len(arrays)
    gather = kind == "gather"
    out_shape = [_sds((NDEV,) + a.shape if gather else a.shape, a.dtype) for a in arrays]

    def body(*refs):
        srcs, outs = refs[:n], refs[n:2 * n]
        send_sems, recv_sems, local_sems = refs[2 * n:]
        x, y, c = lax.axis_index("x"), lax.axis_index("y"), lax.axis_index("c")
        me = 4 * x + 2 * y + c
        pending = []
        for t in range(n):
            own = pltpu.make_async_copy(srcs[t] if gather else srcs[t].at[me], outs[t].at[me], local_sems.at[t])
            own.start()
            pending.append(own)
            for rel in range(1, NDEV):
                px = 1 - x if rel & 4 else x
                py = 1 - y if rel & 2 else y
                pc = 1 - c if rel & 1 else c
                peer = 4 * px + 2 * py + pc
                send = pltpu.make_async_remote_copy(
                    src_ref=srcs[t] if gather else srcs[t].at[peer], dst_ref=outs[t].at[me],
                    send_sem=send_sems.at[t, rel - 1], recv_sem=recv_sems.at[t, rel - 1],
                    device_id=(px, py, pc), device_id_type=MESH)
                send.start()
                arrive = pltpu.make_async_remote_copy(
                    src_ref=srcs[t] if gather else srcs[t].at[me], dst_ref=outs[t].at[peer],
                    send_sem=send_sems.at[t, rel - 1], recv_sem=recv_sems.at[t, rel - 1],
                    device_id=(px, py, pc), device_id_type=MESH)
                pending.append((send, arrive))
        for item in pending:
            if isinstance(item, tuple):
                item[0].wait_send()
                item[1].wait_recv()
            else:
                item.wait()

    any_spec = pl.BlockSpec(memory_space=pl.ANY)
    outs = pl.pallas_call(
        body, name=name,
        in_specs=[any_spec] * n, out_specs=[any_spec] * n, out_shape=out_shape,
        scratch_shapes=[pltpu.SemaphoreType.DMA((n, NDEV - 1)), pltpu.SemaphoreType.DMA((n, NDEV - 1)),
                        pltpu.SemaphoreType.DMA((n,))],
    )(*arrays)
    return list(outs)


_HBM_SPEC = pl.BlockSpec(memory_space=pltpu.HBM)
_SEM_SPEC = pl.BlockSpec(memory_space=pltpu.SEMAPHORE)
_DATAFLOW = pltpu.SideEffectType.DATAFLOW_SIDE_EFFECTING


def _peers():
    x, y, c = lax.axis_index("x"), lax.axis_index("y"), lax.axis_index("c")
    out = []
    for rel in range(1, NDEV):
        px = 1 - x if rel & 4 else x
        py = 1 - y if rel & 2 else y
        pc = 1 - c if rel & 1 else c
        out.append((rel - 1, (px, py, pc), 4 * px + 2 * py + pc))
    return 4 * x + 2 * y + c, out


def _hbm(a):
    return pltpu.HBM(a.shape, a.dtype)


def _own_slot(a, me, kind):
    mine = a[None] if kind == "gather" else lax.dynamic_slice_in_dim(a, me, 1, axis=0)
    shape = (NDEV,) + mine.shape[1:]
    return lax.dynamic_update_slice_in_dim(lax.empty(shape, a.dtype), mine, me, axis=0)


def _exchange_start(name, arrays, me, kind):
    n = len(arrays)
    gather = kind == "gather"
    lands = [_own_slot(a, me, kind) for a in arrays]

    def body(*refs):
        src_refs, land_refs = refs[:n], refs[n:2 * n]
        send_sems, recv_sems = refs[2 * n], refs[2 * n + 1]
        token = refs[-1]
        my_block, peers = _peers()
        for t in range(n):
            for slot, dev, block in peers:
                pltpu.make_async_remote_copy(
                    src_ref=src_refs[t] if gather else src_refs[t].at[block], dst_ref=land_refs[t].at[my_block],
                    send_sem=send_sems.at[t * (NDEV - 1) + slot], recv_sem=recv_sems.at[t * (NDEV - 1) + slot],
                    device_id=dev, device_id_type=MESH).start()
        token[...] = jnp.zeros_like(token)

    operands = [pltpu.with_memory_space_constraint(a, pltpu.HBM) for a in list(arrays) + lands]
    outs = pl.pallas_call(
        body, name=name,
        out_shape=(pltpu.SemaphoreType.DMA((n * (NDEV - 1),)), pltpu.SemaphoreType.DMA((n * (NDEV - 1),)),
                   *[_hbm(a) for a in operands], _sds((8, 128), F32)),
        in_specs=[_HBM_SPEC] * (2 * n),
        out_specs=(_SEM_SPEC, _SEM_SPEC, *[_HBM_SPEC] * (2 * n), pl.BlockSpec(memory_space=pltpu.VMEM)),
        input_output_aliases={i: 2 + i for i in range(2 * n)},
        compiler_params=pltpu.CompilerParams(has_side_effects=_DATAFLOW),
    )(*operands)
    return (outs[0], outs[1], list(outs[2:2 + n]), list(outs[2 + n:2 + 2 * n])), outs[-1]


def _exchange_wait(name, started, t, after, kind):
    send_sems, recv_sems, srcs, lands = started
    gather = kind == "gather"

    def body(src_ref, land_ref, send_ref, recv_ref, after_ref, src_out, land_out):
        _, peers = _peers()
        for slot, dev, block in peers:
            copy = pltpu.make_async_remote_copy(
                src_ref=src_ref if gather else src_ref.at[block], dst_ref=land_ref.at[block],
                send_sem=send_ref.at[t * (NDEV - 1) + slot], recv_sem=recv_ref.at[t * (NDEV - 1) + slot],
                device_id=dev, device_id_type=MESH)
            copy.wait_send()
            copy.wait_recv()

    return pl.pallas_call(
        body, name=name, out_shape=(_hbm(srcs[t]), _hbm(lands[t])),
        in_specs=(_HBM_SPEC, _HBM_SPEC, _SEM_SPEC, _SEM_SPEC, pl.BlockSpec(memory_space=pl.ANY)),
        out_specs=(_HBM_SPEC, _HBM_SPEC), input_output_aliases={0: 0, 1: 1},
        compiler_params=pltpu.CompilerParams(has_side_effects=_DATAFLOW),
    )(srcs[t], lands[t], send_sems, recv_sems, after)[1]


def _ffn_fwd(tag, h, g_pre, weight):
    n = _rms_fwd(f"ffn_prenorm_{tag}", h, [g_pre])[0]
    wg = weight(f"gate_up_{tag}", n)
    gu = _fwd_cols_blocked(f"ffn_gate_up_{tag}", n, wg).reshape(2, NFB, S, FB)
    act = _swiglu_fwd(f"ffn_act_{tag}", gu)
    wd4 = weight(f"down_{tag}", act).reshape(NFB, FB, D)
    f = _fwd_kblocked(f"ffn_down_{tag}", act, wd4)
    return (n, gu, act, wg, wd4), f


def _ffn_bwd(tag, dh_out, h_in, f, saved, g_pre, g_post, send):
    n, gu, act, wg, wd4 = saved
    df, (dg_post,) = _rms_bwd(f"ffn_postnorm_bwd_{tag}", f, [(g_post, dh_out)], None, BF16)
    tok = send(f"down_{tag}", _bwd_w_kblocked(f"ffn_down_dw_{tag}", act, df).reshape(NDEV, DFF // NDEV, D))
    da = _bwd_x_kblocked(f"ffn_down_dx_{tag}", df, wd4, after=tok)
    dgu = _swiglu_bwd(f"ffn_act_bwd_{tag}", gu, da).reshape(NDEV, S, FB)
    tok = send(f"gate_up_{tag}", _bwd_w_cols_blocked(f"ffn_gate_up_dw_{tag}", n, dgu))
    dn = _bwd_x_cols_blocked(f"ffn_gate_up_dx_{tag}", dgu, wg, after=tok)
    dh_in, (dg_pre,) = _rms_bwd(f"ffn_prenorm_bwd_{tag}", h_in, [(g_pre, dn)], dh_out, F32)
    return dh_in, dg_pre, dg_post


def kernel(x, positions, mix_norm_pre, mix_norm_post, ffn_norm_pre, ffn_norm_post, ffn_w_gate_up, ffn_w_down, conv_w_in, conv_w, conv_w_out, kv_norm, w_kv, w_q, w_o, loss_target, m_mix_norm_pre, m_mix_norm_post, m_ffn_norm_pre, m_ffn_norm_post, m_ffn_w_gate_up, m_ffn_w_down, m_conv_w_in, m_conv_w, m_conv_w_out, m_kv_norm, m_w_kv, m_w_q, m_w_o, v_mix_norm_pre, v_mix_norm_post, v_ffn_norm_pre, v_ffn_norm_post, v_ffn_w_gate_up, v_ffn_w_down, v_conv_w_in, v_conv_w, v_conv_w_out, v_kv_norm, v_w_kv, v_w_q, v_w_o):
    me = 4 * lax.axis_index("x") + 2 * lax.axis_index("y") + lax.axis_index("c")
    h0 = x.reshape(S, D)
    target = loss_target.reshape(S, D)
    row = lambda a, l: a[l].reshape(1, D)
    g_kv = kv_norm.reshape(1, D)

    cw_shard = jnp.pad(conv_w[0], ((0, 5), (0, 0)))
    names = ["conv_in", "conv_w", "conv_out", "gate_up_0", "down_0", "kv", "q", "o", "gate_up_1", "down_1"]
    shards = [conv_w_in[0], cw_shard, conv_w_out[0], ffn_w_gate_up[0], ffn_w_down[0],
              w_kv, w_q[0], w_o[0], ffn_w_gate_up[1], ffn_w_down[1]]
    shards = [s if n == "conv_w" else s.astype(BF16) for n, s in zip(names, shards)]
    gather, _ = _exchange_start("gather_weights_start", shards, me, "gather")

    def weight(name, after):
        return _exchange_wait(f"gather_wait_{name}", gather, names.index(name), after, "gather")

    sent = {}

    def send(name, grad):
        sent[name], token = _exchange_start(f"scatter_start_{name}", [grad], me, "scatter")
        return token

    n0 = _rms_fwd("mix_prenorm_0", h0, [row(mix_norm_pre, 0)])[0]
    win_g = weight("conv_in", n0)
    cw = weight("conv_w", n0).transpose(1, 0, 2).reshape(8, D)
    z = _fwd_cols("conv_in", n0, win_g)
    pre = _conv_fwd("conv_gate", z, cw)
    wout = weight("conv_out", pre).reshape(D, D)
    y0 = _fwd_rows("conv_out", pre, wout)
    h1 = _resid_rms("mix_postnorm_0", h0, y0, row(mix_norm_post, 0))
    ffn0, f0 = _ffn_fwd("0", h1, row(ffn_norm_pre, 0), weight)
    h2 = _resid_rms("ffn_postnorm_0", h1, f0, row(ffn_norm_post, 0))

    nk, n2 = _rms_fwd("kv_and_mix_prenorm_1", h2, [g_kv, row(mix_norm_pre, 1)])
    wkv_g = weight("kv", nk)
    wq_g = weight("q", nk)
    wkv = wkv_g.transpose(1, 0, 2).reshape(D, 2 * QW)
    wq = wq_g.transpose(1, 0, 2).reshape(D, QW)
    half = HEAD_DIM // 2
    inv_freq = ROPE_THETA ** (-jnp.arange(half, dtype=F32) / half)
    tables = _rope_tables("rope_tables", positions.reshape(S, 1), jnp.tile(inv_freq, 4).reshape(1, 128))
    qc, kc, vc, o_c, lse_c = [], [], [], [], []
    for g, d in enumerate(DILATIONS):
        qc.append(_proj_classes(f"q_proj_{g}", n2, wq, g, d, tables, HEAD_DIM ** -0.5))
        kc.append(_proj_classes(f"k_proj_{g}", nk, wkv, g, d, tables, 1.0))
        vc.append(_proj_classes(f"v_proj_{g}", nk, wkv, len(DILATIONS) + g, d, None, None))
        o_g, lse_g = _attn_fwd(f"attn_fwd_{g}", qc[g], kc[g], vc[g], d)
        o_c.append(o_g)
        lse_c.append(lse_g)
    o_mix = _mix_fwd("attn_mix", o_c, lse_c)
    wo = weight("o", o_mix).reshape(D, D)
    y1 = _fwd_rows("attn_out", o_mix, wo)
    h3 = _resid_rms("mix_postnorm_1", h2, y1, row(mix_norm_post, 1))
    ffn1, f1 = _ffn_fwd("1", h3, row(ffn_norm_pre, 1), weight)
    h4 = _resid_rms("ffn_postnorm_1", h3, f1, row(ffn_norm_post, 1))

    dh4, sq = _loss_grad("loss", h4, target)
    loss = lax.psum(jnp.sum(sq) * (0.5 / D), ("x", "y", "c"))

    dh3, dg_fpre1, dg_fpost1 = _ffn_bwd(
        "1", dh4, h3, f1, ffn1, row(ffn_norm_pre, 1), row(ffn_norm_post, 1), send)
    dy1, (dg_mpost1,) = _rms_bwd("mix_postnorm_bwd_1", y1, [(row(mix_norm_post, 1), dh3)], None, BF16)
    tok = send("o", _bwd_w_rows("attn_out_dw", o_mix, dy1).reshape(NDEV, D // NDEV, D))
    do = _bwd_x_rows("attn_out_dx", dy1, wo, F32, after=tok)
    lane = jnp.arange(128)
    ones_blockdiag = (lane[:, None] // HEAD_DIM == lane[None, :] // HEAD_DIM).astype(BF16)
    mixed = _mix_bwd("attn_mix_bwd", do, o_c, lse_c, ones_blockdiag)
    branch_grads = [_attn_bwd(f"attn_bwd_{g}", qc[g], kc[g], vc[g], mixed[g], lse_c[g], mixed[3 + g], d)
                    for g, d in enumerate(DILATIONS)]
    dq_raw, dkv = _attn_bwd_post("attn_bwd_post", branch_grads, *tables)
    tok = send("kv", _bwd_w_cols("kv_proj_dw", nk, dkv, 2 * QW // NDEV))
    dnk = _bwd_x_cols("kv_proj_dx", dkv, wkv_g, after=tok)
    tok = send("q", _bwd_w_cols("q_proj_dw", n2, dq_raw, QW // NDEV))
    dn2 = _bwd_x_cols("q_proj_dx", dq_raw, wq_g, after=tok)
    dh2, (dg_kv, dg_mpre1) = _rms_bwd("kv_and_mix_prenorm_bwd_1", h2,
                                      [(g_kv, dnk), (row(mix_norm_pre, 1), dn2)], dh3, F32)

    dh1, dg_fpre0, dg_fpost0 = _ffn_bwd(
        "0", dh2, h1, f0, ffn0, row(ffn_norm_pre, 0), row(ffn_norm_post, 0), send)
    dy0, (dg_mpost0,) = _rms_bwd("mix_postnorm_bwd_0", y0, [(row(mix_norm_post, 0), dh1)], None, BF16)
    tok = send("conv_out", _bwd_w_rows("conv_out_dw", pre, dy0).reshape(NDEV, D // NDEV, D))
    dpre = _bwd_x_rows("conv_out_dx", dy0, wout, BF16, after=tok)
    dz, dcw = _conv_bwd("conv_gate_bwd", z, dpre, cw)
    tok = send("conv_in", _bwd_w_cols("conv_in_dw", n0, dz, 3 * D // NDEV))
    dn0 = _bwd_x_cols("conv_in_dx", dz, win_g, after=tok)
    dh0, (dg_mpre0,) = _rms_bwd("mix_prenorm_bwd_0", h0, [(row(mix_norm_pre, 0), dn0)], dh1, F32)

    small = jnp.concatenate([dg_mpre0, dg_mpre1, dg_mpost0, dg_mpost1, dg_fpre0, dg_fpre1, dg_fpost0, dg_fpost1,
                             dg_kv, dcw], axis=0)
    small_all = _exchange("gather_small_grads", [small], "gather")[0]

    done = [small_all]

    def upd(tag, w, m, v):
        parts = _exchange_wait(f"scatter_wait_{tag}", sent[tag], 0, done[-1], "scatter")
        shape = w.shape
        flat = lambda a: a.reshape(parts.shape[1:])
        res = _adamw(f"adamw_{tag}", parts, flat(w), flat(m), flat(v))
        done.append(res[0])
        return [r.reshape(shape) for r in res]

    def upd_layer(tag, l, w, m, v):
        return upd(f"{tag}_{l}", w[l], m[l], v[l])

    def stack(per_layer):
        return [jnp.stack([per_layer[0][i], per_layer[1][i]]) for i in range(4)]

    def tiles(per_layer, kv, fill):
        rows = [p[l:l + 1] for p in per_layer for l in range(2)] + [kv.reshape(1, D)]
        padded = [jnp.pad(r, ((0, 7), (0, 0)), constant_values=fill) for r in rows]
        return jnp.concatenate(padded + [jnp.full((8, D), fill, F32)], axis=0)

    gains_w = tiles([mix_norm_pre, mix_norm_post, ffn_norm_pre, ffn_norm_post], kv_norm, 0.0)
    gains_m = tiles([m_mix_norm_pre, m_mix_norm_post, m_ffn_norm_pre, m_ffn_norm_post], m_kv_norm, 0.0)
    gains_v = tiles([v_mix_norm_pre, v_mix_norm_post, v_ffn_norm_pre, v_ffn_norm_post], v_kv_norm, 1.0)
    small_res = _adamw("adamw_gains", small_all, gains_w, gains_m, gains_v)
    dcw_mine = lax.dynamic_slice(small_res[0], (72, me * 128), (8, 128))
    pad8 = lambda a, fill: jnp.pad(a[0], ((0, 5), (0, 0)), constant_values=fill)
    cw_res = [r[0:3].reshape(1, 3, 128) for r in
              _adamw("adamw_conv_w", dcw_mine.reshape(1, 8, 128), cw_shard, pad8(m_conv_w, 0.0), pad8(v_conv_w, 1.0))]

    res = {
        "mix_norm_pre": [r[0:16:8] for r in small_res],
        "mix_norm_post": [r[16:32:8] for r in small_res],
        "ffn_norm_pre": [r[32:48:8] for r in small_res],
        "ffn_norm_post": [r[48:64:8] for r in small_res],
        "kv_norm": [r[64] for r in small_res],
        "conv_w": cw_res,
    }
    down, gate_up = {}, {}
    down[1] = upd_layer("down", 1, ffn_w_down, m_ffn_w_down, v_ffn_w_down)
    gate_up[1] = upd_layer("gate_up", 1, ffn_w_gate_up, m_ffn_w_gate_up, v_ffn_w_gate_up)
    res["w_o"] = upd("o", w_o, m_w_o, v_w_o)
    res["w_q"] = upd("q", w_q, m_w_q, v_w_q)
    res["w_kv"] = upd("kv", w_kv, m_w_kv, v_w_kv)
    down[0] = upd_layer("down", 0, ffn_w_down, m_ffn_w_down, v_ffn_w_down)
    gate_up[0] = upd_layer("gate_up", 0, ffn_w_gate_up, m_ffn_w_gate_up, v_ffn_w_gate_up)
    res["ffn_w_down"] = stack(down)
    res["ffn_w_gate_up"] = stack(gate_up)
    res["conv_w_out"] = upd("conv_out", conv_w_out, m_conv_w_out, v_conv_w_out)
    res["conv_w_in"] = upd("conv_in", conv_w_in, m_conv_w_in, v_conv_w_in)
    order = ["mix_norm_pre", "mix_norm_post", "ffn_norm_pre", "ffn_norm_post", "ffn_w_gate_up", "ffn_w_down",
             "conv_w_in", "conv_w", "conv_w_out", "kv_norm", "w_kv", "w_q", "w_o"]
    out = [loss, dh0.reshape(1, S, D)]
    for i in range(4):
        out += [res[name][i] for name in order]
    return tuple(out)
```

```python
import jax
import jax.numpy as jnp
from jax import lax
from jax.experimental import pallas as pl
from jax.experimental.pallas import tpu as pltpu

F32 = jnp.float32
BF16 = jnp.bfloat16

S = 4096
D = 1024
NDEV = 8
HEAD_DIM = 64
QW = 3072
DFF = 2816
FB = 704
NFB = 4
BRANCHES = ((128, 1), (512, 4), (2048, 16))
BAND = 128
ROPE_THETA = 10000.0
RMS_EPS = 1e-6
NEG_INF = -1e30
ADAM_LR, ADAM_B1, ADAM_B2, ADAM_EPS, ADAM_WD, ADAM_STEP = 0.001, 0.9, 0.999, 1e-08, 0.01, 10

VMEM_LIMIT_BYTES = 52 * 1024 * 1024
ROW_TILE = 512
MESH = pl.DeviceIdType.MESH


def _cparams(ngrid):
    return pltpu.CompilerParams(dimension_semantics=("arbitrary",) * ngrid,
                                vmem_limit_bytes=VMEM_LIMIT_BYTES)


def _sds(shape, dtype):
    return jax.ShapeDtypeStruct(tuple(shape), dtype)


_DIMS = {"nn": (((1,), (0,)), ((), ())),
         "nt": (((1,), (1,)), ((), ())),
         "tn": (((0,), (0,)), ((), ()))}


def _matmul(name, a, b, *, mode, grid, a_blk, a_map, b_blk, b_map, o_shape, o_blk, o_map, out_dtype, after=None,
            out_groups=1):
    nk = grid[2]
    dims = _DIMS[mode]
    acc_shape = tuple(s for s in o_blk if s is not None)
    if out_groups > 1:
        acc_shape = (acc_shape[1], out_groups * acc_shape[2])
    extra = [] if after is None else [after]

    def store(o_ref, val):
        if out_groups == 1:
            o_ref[...] = val.astype(o_ref.dtype)
        else:
            n = o_ref.shape[-1]
            for grp in range(out_groups):
                o_ref[grp] = val[:, grp * n:(grp + 1) * n].astype(o_ref.dtype)

    def body(a_ref, b_ref, *rest):
        o_ref, scratch = rest[len(extra)], rest[len(extra) + 1:]
        part = lax.dot_general(a_ref[...], b_ref[...], dims, preferred_element_type=F32)
        if nk == 1:
            store(o_ref, part)
            return
        acc_ref = scratch[0]
        k = pl.program_id(2)

        @pl.when(k == 0)
        def _():
            acc_ref[...] = part

        @pl.when(k > 0)
        def _():
            acc_ref[...] += part

        @pl.when(k == nk - 1)
        def _():
            store(o_ref, acc_ref[...])

    return pl.pallas_call(
        body, name=name, grid=grid,
        in_specs=[pl.BlockSpec(a_blk, a_map), pl.BlockSpec(b_blk, b_map)] + [pl.BlockSpec(memory_space=pl.ANY)] * len(extra),
        out_specs=pl.BlockSpec(o_blk, o_map),
        out_shape=_sds(o_shape, out_dtype),
        scratch_shapes=[] if nk == 1 else [pltpu.VMEM(acc_shape, F32)],
        compiler_params=_cparams(3),
    )(a, b, *extra)


TM = 1024
TK = 1024


def _fwd_cols(name, a, wg, out_dtype=BF16):
    _, kdim, n = wg.shape
    return _matmul(name, a, wg, mode="nn", grid=(S // TM, NDEV, 1),
                   a_blk=(TM, kdim), a_map=lambda i, j, k: (i, 0),
                   b_blk=(None, kdim, n), b_map=lambda i, j, k: (j, 0, 0),
                   o_shape=(S, NDEV * n), o_blk=(TM, n), o_map=lambda i, j, k: (i, j), out_dtype=out_dtype)


def _fwd_cols_blocked(name, a, wg):
    _, kdim, n = wg.shape
    return _matmul(name, a, wg, mode="nn", grid=(S // TM, NDEV, 1),
                   a_blk=(TM, kdim), a_map=lambda i, j, k: (i, 0),
                   b_blk=(None, kdim, n), b_map=lambda i, j, k: (j, 0, 0),
                   o_shape=(NDEV, S, n), o_blk=(None, TM, n), o_map=lambda i, j, k: (j, i, 0), out_dtype=BF16)


def _fwd_rows(name, a, w, out_dtype=F32):
    kdim, n = w.shape
    tn = 512
    return _matmul(name, a, w, mode="nn", grid=(S // TM, n // tn, 1),
                   a_blk=(TM, kdim), a_map=lambda i, j, k: (i, 0),
                   b_blk=(kdim, tn), b_map=lambda i, j, k: (0, j),
                   o_shape=(S, n), o_blk=(TM, tn), o_map=lambda i, j, k: (i, j), out_dtype=out_dtype)


def _fwd_kblocked(name, a4, w4):
    nb, _, kb = a4.shape
    n = w4.shape[2]
    return _matmul(name, a4, w4, mode="nn", grid=(S // TM, 1, nb),
                   a_blk=(None, TM, kb), a_map=lambda i, j, k: (k, i, 0),
                   b_blk=(None, kb, n), b_map=lambda i, j, k: (k, 0, 0),
                   o_shape=(S, n), o_blk=(TM, n), o_map=lambda i, j, k: (i, 0), out_dtype=F32)


def _bwd_x_cols_blocked(name, dy8, wg, after=None):
    _, kdim, n = wg.shape
    return _matmul(name, dy8, wg, mode="nt", grid=(S // TM, 1, NDEV),
                   a_blk=(None, TM, n), a_map=lambda i, j, k: (k, i, 0),
                   b_blk=(None, kdim, n), b_map=lambda i, j, k: (k, 0, 0),
                   o_shape=(S, kdim), o_blk=(TM, kdim), o_map=lambda i, j, k: (i, 0), out_dtype=F32, after=after)


def _bwd_x_rows(name, dy, w, out_dtype, after=None):
    kdim, n = w.shape
    tkk = 512
    return _matmul(name, dy, w, mode="nt", grid=(S // TM, kdim // tkk, 1),
                   a_blk=(TM, n), a_map=lambda i, j, k: (i, 0),
                   b_blk=(tkk, n), b_map=lambda i, j, k: (j, 0),
                   o_shape=(S, kdim), o_blk=(TM, tkk), o_map=lambda i, j, k: (i, j), out_dtype=out_dtype, after=after)


def _bwd_x_kblocked(name, dy, w4, after=None):
    nb, kb, n = w4.shape
    return _matmul(name, dy, w4, mode="nt", grid=(S // TM, nb, 1),
                   a_blk=(TM, n), a_map=lambda i, j, k: (i, 0),
                   b_blk=(None, kb, n), b_map=lambda i, j, k: (j, 0, 0),
                   o_shape=(nb, S, kb), o_blk=(None, TM, kb), o_map=lambda i, j, k: (j, i, 0), out_dtype=BF16, after=after)


DW_COLS = 768


def _bwd_w_cols(name, a, dy, n):
    kdim = a.shape[1]
    groups = DW_COLS // n
    return _matmul(name, a, dy, mode="tn", grid=(1, NDEV // groups, S // TK),
                   a_blk=(TK, kdim), a_map=lambda i, j, k: (k, 0),
                   b_blk=(TK, DW_COLS), b_map=lambda i, j, k: (k, j),
                   o_shape=(NDEV, kdim, n), o_blk=(groups, kdim, n) if groups > 1 else (None, kdim, n),
                   o_map=lambda i, j, k: (j, 0, 0), out_dtype=BF16, out_groups=groups)


def _bwd_x_plain(name, dy, w, nk, after=None):
    kdim, n = w.shape
    return _matmul(name, dy, w, mode="nt", grid=(S // TM, 1, nk),
                   a_blk=(TM, n // nk), a_map=lambda i, j, k: (i, k),
                   b_blk=(kdim, n // nk), b_map=lambda i, j, k: (0, k),
                   o_shape=(S, kdim), o_blk=(TM, kdim), o_map=lambda i, j, k: (i, 0), out_dtype=F32, after=after)


def _bwd_w_cols_blocked(name, a, dy8):
    kdim = a.shape[1]
    n = dy8.shape[2]
    return _matmul(name, a, dy8, mode="tn", grid=(1, NDEV, S // TK),
                   a_blk=(TK, kdim), a_map=lambda i, j, k: (k, 0),
                   b_blk=(None, TK, n), b_map=lambda i, j, k: (j, k, 0),
                   o_shape=(NDEV, kdim, n), o_blk=(None, kdim, n), o_map=lambda i, j, k: (j, 0, 0), out_dtype=BF16)


def _bwd_w_rows(name, a, dy):
    kdim = a.shape[1]
    n = dy.shape[1]
    tmm = 512
    return _matmul(name, a, dy, mode="tn", grid=(kdim // tmm, 1, S // TK),
                   a_blk=(TK, tmm), a_map=lambda i, j, k: (k, i),
                   b_blk=(TK, n), b_map=lambda i, j, k: (k, 0),
                   o_shape=(kdim, n), o_blk=(tmm, n), o_map=lambda i, j, k: (i, 0), out_dtype=BF16)


def _bwd_w_kblocked(name, a4, dy):
    nb, _, kb = a4.shape
    n = dy.shape[1]
    return _matmul(name, a4, dy, mode="tn", grid=(nb, 1, S // TK),
                   a_blk=(None, TK, kb), a_map=lambda i, j, k: (i, k, 0),
                   b_blk=(TK, n), b_map=lambda i, j, k: (k, 0),
                   o_shape=(nb, kb, n), o_blk=(None, kb, n), o_map=lambda i, j, k: (i, 0, 0), out_dtype=BF16)


def _rstd(x):
    return lax.rsqrt(jnp.mean(x * x, axis=-1, keepdims=True) + RMS_EPS)


def _row_spec(tm=ROW_TILE, width=D):
    return pl.BlockSpec((tm, width), lambda i: (i, 0))


def _vec_spec(rows=1, width=D):
    return pl.BlockSpec((rows, width), lambda i: (0, 0))


def _rms_fwd(name, x, gains):
    n = len(gains)

    def body(x_ref, *refs):
        x_val = x_ref[...]
        xh = x_val * _rstd(x_val)
        for g_ref, o_ref in zip(refs[:n], refs[n:]):
            o_ref[...] = (xh * g_ref[...]).astype(o_ref.dtype)

    outs = pl.pallas_call(
        body, name=name, grid=(S // ROW_TILE,),
        in_specs=[_row_spec()] + [_vec_spec()] * n,
        out_specs=[_row_spec()] * n,
        out_shape=[_sds((S, D), BF16)] * n,
        compiler_params=_cparams(1),
    )(x, *gains)
    return list(outs)


def _resid_rms(name, h, y, g):
    def body(h_ref, y_ref, g_ref, o_ref):
        y_val = y_ref[...]
        o_ref[...] = h_ref[...] + (y_val * _rstd(y_val)) * g_ref[...]

    return pl.pallas_call(
        body, name=name, grid=(S // ROW_TILE,),
        in_specs=[_row_spec(), _row_spec(), _vec_spec()],
        out_specs=_row_spec(), out_shape=_sds((S, D), F32),
        compiler_params=_cparams(1),
    )(h, y, g)


def _rms_bwd(name, x, pairs, dres, out_dtype):
    n = len(pairs)
    has_res = dres is not None

    def body(x_ref, *refs):
        g_refs = refs[0:2 * n:2]
        dn_refs = refs[1:2 * n:2]
        pos = 2 * n
        res_ref = refs[pos] if has_res else None
        pos += int(has_res)
        dx_ref = refs[pos]
        dg_refs = refs[pos + 1:]
        step = pl.program_id(0)
        x_val = x_ref[...]
        r = _rstd(x_val)
        xh = x_val * r
        acc = res_ref[...] if has_res else jnp.zeros_like(x_val)
        for g_ref, dn_ref, dg_ref in zip(g_refs, dn_refs, dg_refs):
            dn = dn_ref[...].astype(F32)
            dxh = dn * g_ref[...]
            acc = acc + r * (dxh - xh * jnp.mean(dxh * xh, axis=-1, keepdims=True))
            part = jnp.sum(dn * xh, axis=0, keepdims=True)

            @pl.when(step == 0)
            def _():
                dg_ref[...] = jnp.zeros_like(dg_ref)

            dg_ref[0:1, :] += part

        dx_ref[...] = acc.astype(dx_ref.dtype)

    operands = [x]
    in_specs = [_row_spec()]
    for g, dn in pairs:
        operands += [g, dn]
        in_specs += [_vec_spec(), _row_spec()]
    if has_res:
        operands.append(dres)
        in_specs.append(_row_spec())
    outs = pl.pallas_call(
        body, name=name, grid=(S // ROW_TILE,),
        in_specs=in_specs,
        out_specs=[_row_spec()] + [_vec_spec(8)] * n,
        out_shape=[_sds((S, D), out_dtype)] + [_sds((8, D), F32)] * n,
        compiler_params=_cparams(1),
    )(*operands)
    return outs[0], list(outs[1:])


def _loss_grad(name, h, target):
    def body(h_ref, t_ref, dh_ref, part_ref):
        e = h_ref[...] - t_ref[...]
        dh_ref[...] = e * (1.0 / D)
        part = jnp.sum(e * e, axis=0, keepdims=True)
        step = pl.program_id(0)

        @pl.when(step == 0)
        def _():
            part_ref[...] = part

        @pl.when(step > 0)
        def _():
            part_ref[...] += part

    return pl.pallas_call(
        body, name=name, grid=(S // ROW_TILE,),
        in_specs=[_row_spec(), _row_spec()],
        out_specs=[_row_spec(), _vec_spec()],
        out_shape=[_sds((S, D), F32), _sds((1, D), F32)],
        compiler_params=_cparams(1),
    )(h, target)


def _shift_down(u, prev8, k):
    r = pltpu.roll(u, k, 0)
    p = pltpu.roll(prev8, k, 0)
    row = lax.broadcasted_iota(jnp.int32, prev8.shape, 0)
    top = jnp.where(row < k, p, r[0:8])
    return jnp.concatenate([top, r[8:]], axis=0)


def _shift_up(u, next8, k):
    tm = u.shape[0]
    r = pltpu.roll(u, tm - k, 0)
    p = pltpu.roll(next8, 8 - k, 0)
    row = lax.broadcasted_iota(jnp.int32, next8.shape, 0)
    bot = jnp.where(row >= 8 - k, p, r[tm - 8:tm])
    return jnp.concatenate([r[:tm - 8], bot], axis=0)


CONV_TILE = 512


def _halo_prev(col):
    return pl.BlockSpec((8, D), lambda i: (jnp.maximum(i * (CONV_TILE // 8) - 1, 0), col))


def _halo_next(col):
    last = S // 8 - 1
    return pl.BlockSpec((8, D), lambda i: (jnp.minimum((i + 1) * (CONV_TILE // 8), last), col))


def _conv_fwd(name, z, cw):
    def body(b_ref, c_ref, h_ref, cp_ref, hp_ref, cw_ref, o_ref):
        i = pl.program_id(0)
        u = c_ref[...].astype(F32) * h_ref[...].astype(F32)
        up = cp_ref[...].astype(F32) * hp_ref[...].astype(F32)
        up = jnp.where(i > 0, up, 0.0)
        cv = cw_ref[0:1, :] * _shift_down(u, up, 2) + cw_ref[1:2, :] * _shift_down(u, up, 1) + cw_ref[2:3, :] * u
        o_ref[...] = (b_ref[...].astype(F32) * cv).astype(o_ref.dtype)

    col = lambda c: pl.BlockSpec((CONV_TILE, D), lambda i: (i, c))
    return pl.pallas_call(
        body, name=name, grid=(S // CONV_TILE,),
        in_specs=[col(0), col(1), col(2), _halo_prev(1), _halo_prev(2), _vec_spec(8)],
        out_specs=_row_spec(CONV_TILE), out_shape=_sds((S, D), BF16),
        compiler_params=_cparams(1),
    )(z, z, z, z, z, cw)


def _conv_bwd(name, z, dpre, cw):
    nsteps = S // CONV_TILE

    def body(b_ref, c_ref, h_ref, cp_ref, hp_ref, dp_ref, dpn_ref, bn_ref, cw_ref, dz_ref, dcw_ref):
        i = pl.program_id(0)
        b = b_ref[...].astype(F32)
        c = c_ref[...].astype(F32)
        h = h_ref[...].astype(F32)
        dp = dp_ref[...].astype(F32)
        u = c * h
        up = jnp.where(i > 0, cp_ref[...].astype(F32) * hp_ref[...].astype(F32), 0.0)
        s1 = _shift_down(u, up, 1)
        s2 = _shift_down(u, up, 2)
        w0, w1, w2 = cw_ref[0:1, :], cw_ref[1:2, :], cw_ref[2:3, :]
        cv = w0 * s2 + w1 * s1 + w2 * u
        dcv = dp * b
        dcvn = jnp.where(i < nsteps - 1, dpn_ref[...].astype(F32) * bn_ref[...].astype(F32), 0.0)
        du = w2 * dcv + w1 * _shift_up(dcv, dcvn, 1) + w0 * _shift_up(dcv, dcvn, 2)
        dz_ref[:, 0:D] = (dp * cv).astype(dz_ref.dtype)
        dz_ref[:, D:2 * D] = (du * h).astype(dz_ref.dtype)
        dz_ref[:, 2 * D:3 * D] = (du * c).astype(dz_ref.dtype)

        @pl.when(i == 0)
        def _():
            dcw_ref[...] = jnp.zeros_like(dcw_ref)

        dcw_ref[0:1, :] += jnp.sum(dcv * s2, axis=0, keepdims=True)
        dcw_ref[1:2, :] += jnp.sum(dcv * s1, axis=0, keepdims=True)
        dcw_ref[2:3, :] += jnp.sum(dcv * u, axis=0, keepdims=True)

    col = lambda c: pl.BlockSpec((CONV_TILE, D), lambda i: (i, c))
    return pl.pallas_call(
        body, name=name, grid=(nsteps,),
        in_specs=[col(0), col(1), col(2), _halo_prev(1), _halo_prev(2),
                  _row_spec(CONV_TILE), _halo_next(0), _halo_next(0), _vec_spec(8)],
        out_specs=[pl.BlockSpec((CONV_TILE, 3 * D), lambda i: (i, 0)), _vec_spec(8)],
        out_shape=[_sds((S, 3 * D), BF16), _sds((8, D), F32)],
        compiler_params=_cparams(1),
    )(z, z, z, z, z, dpre, dpre, z, cw)


def _swiglu_fwd(name, gu):
    def body(gu_ref, o_ref):
        g = gu_ref[0].astype(F32)
        u = gu_ref[1].astype(F32)
        o_ref[...] = (g * jax.nn.sigmoid(g) * u).astype(o_ref.dtype)

    return pl.pallas_call(
        body, name=name, grid=(NFB, S // ROW_TILE),
        in_specs=[pl.BlockSpec((2, None, ROW_TILE, FB), lambda j, i: (0, j, i, 0))],
        out_specs=pl.BlockSpec((None, ROW_TILE, FB), lambda j, i: (j, i, 0)),
        out_shape=_sds((NFB, S, FB), BF16),
        compiler_params=_cparams(2),
    )(gu)


def _swiglu_bwd(name, gu, da):
    def body(gu_ref, da_ref, o_ref):
        g = gu_ref[0].astype(F32)
        u = gu_ref[1].astype(F32)
        d = da_ref[...].astype(F32)
        sg = jax.nn.sigmoid(g)
        o_ref[0] = (d * u * sg * (1.0 + g * (1.0 - sg))).astype(o_ref.dtype)
        o_ref[1] = (d * g * sg).astype(o_ref.dtype)

    blk = pl.BlockSpec((2, None, ROW_TILE, FB), lambda j, i: (0, j, i, 0))
    return pl.pallas_call(
        body, name=name, grid=(NFB, S // ROW_TILE),
        in_specs=[blk, pl.BlockSpec((None, ROW_TILE, FB), lambda j, i: (j, i, 0))],
        out_specs=blk, out_shape=_sds((2, NFB, S, FB), BF16),
        compiler_params=_cparams(2),
    )(gu, da)


def _rope_tables(name, pos_col, inv_freq_row):
    def body(pos_ref, f_ref, cos_ref, sin_ref):
        ang = pos_ref[...].astype(F32) * f_ref[...]
        lane = lax.broadcasted_iota(jnp.int32, ang.shape, 1)
        s = jnp.sin(ang)
        cos_ref[...] = jnp.cos(ang)
        sin_ref[...] = jnp.where((lane % HEAD_DIM) < HEAD_DIM // 2, -s, s)

    tab = pl.BlockSpec((ROW_TILE, 128), lambda i: (i, 0))
    return pl.pallas_call(
        body, name=name, grid=(S // ROW_TILE,),
        in_specs=[pl.BlockSpec((ROW_TILE, 1), lambda i: (i, 0)), _vec_spec(1, 128)],
        out_specs=[tab, tab], out_shape=[_sds((S, 128), F32)] * 2,
        compiler_params=_cparams(1),
    )(pos_col, inv_freq_row)


def _swap_halves(t):
    lane = lax.broadcasted_iota(jnp.int32, t.shape, 1)
    first = (lane % HEAD_DIM) < HEAD_DIM // 2
    return jnp.where(first, pltpu.roll(t, 128 - HEAD_DIM // 2, 1), pltpu.roll(t, HEAD_DIM // 2, 1))


NCHUNK = D // 128


def _chunk(c, base=0):
    return slice(base + c * 128, base + (c + 1) * 128)


def _class_rows(r, d, tm):
    return pl.ds(r, tm // d, stride=d) if d > 1 else slice(None)


def _class_block(d, tm):
    return pl.BlockSpec((tm // d, d * D), lambda i: (i, 0))


def _tokens_from_classes(blk_ref, tmp_ref, d, tm):
    for r in range(d):
        for c in range(NCHUNK):
            tmp_ref[c, _class_rows(r, d, tm), :] = blk_ref[:, _chunk(c, r * D)]


def _classes_from_tokens(tmp_ref, blk_ref, d, tm):
    for r in range(d):
        for c in range(NCHUNK):
            blk_ref[:, _chunk(c, r * D)] = tmp_ref[c, _class_rows(r, d, tm), :].astype(blk_ref.dtype)


def _proj_classes(name, a, w, col, d, tables, scale):
    kdim = a.shape[1]
    rope = tables is not None

    def body(a_ref, w_ref, *refs):
        if rope:
            cos_ref, sin_ref, o_ref, tmp_ref = refs
        else:
            o_ref, tmp_ref = refs
        acc = _dot_nn(a_ref[...], w_ref[...])
        for c in range(NCHUNK):
            tmp_ref[c] = acc[:, _chunk(c)]
        for r in range(d):
            rows = _class_rows(r, d, TM)
            if rope:
                cs = cos_ref[rows, :]
                sn = sin_ref[rows, :]
            for c in range(NCHUNK):
                x = tmp_ref[c, rows, :]
                if rope:
                    x = (x * cs + _swap_halves(x) * sn) * scale
                o_ref[:, _chunk(c, r * D)] = x.astype(o_ref.dtype)

    tab = pl.BlockSpec((TM, 128), lambda i: (i, 0))
    return pl.pallas_call(
        body, name=name, grid=(S // TM,),
        in_specs=[pl.BlockSpec((TM, kdim), lambda i: (i, 0)), pl.BlockSpec((kdim, D), lambda i: (0, col))]
                 + ([tab, tab] if rope else []),
        out_specs=_class_block(d, TM), out_shape=_sds((S // d, d * D), BF16),
        scratch_shapes=[pltpu.VMEM((NCHUNK, TM, 128), F32)],
        compiler_params=_cparams(1),
    )(a, w, *(tables if rope else ()))


def _class_spec(d):
    return pl.BlockSpec((S // d, 128), lambda cb: (0, cb))


def _dot_nt(a, b):
    return lax.dot_general(a, b, _DIMS["nt"], preferred_element_type=F32)


def _dot_tn(a, b):
    return lax.dot_general(a, b, _DIMS["tn"], preferred_element_type=F32)


def _dot_nn(a, b):
    return lax.dot_general(a, b, _DIMS["nn"], preferred_element_type=F32)


def _band_mask(nkeys):
    qi = lax.broadcasted_iota(jnp.int32, (2 * BAND, nkeys), 0) % BAND
    kj = lax.broadcasted_iota(jnp.int32, (2 * BAND, nkeys), 1)
    if nkeys == BAND:
        return kj <= qi
    dist = qi + BAND - kj
    return (dist >= 0) & (dist <= BAND)


def _stack_heads(x):
    row = lax.broadcasted_iota(jnp.int32, (2 * BAND, 128), 0)
    lane = lax.broadcasted_iota(jnp.int32, (2 * BAND, 128), 1)
    keep = (row < BAND) == (lane < HEAD_DIM)
    return jnp.where(keep, jnp.concatenate([x, x], axis=0), jnp.zeros((), x.dtype))


def _unstack(x2):
    first_head = lax.broadcasted_iota(jnp.int32, (BAND, 128), 1) < HEAD_DIM
    return jnp.where(first_head, x2[:BAND], x2[BAND:])


ATTN_UNROLL = 2


def _for_later_blocks(nblk, fn):
    trips = (nblk - 1) // ATTN_UNROLL
    if trips > 0:
        def step(i, carry):
            for u in range(ATTN_UNROLL):
                fn(pl.multiple_of((1 + i * ATTN_UNROLL + u) * BAND, BAND))
            return carry

        lax.fori_loop(0, trips, step, 0)
    for sb in range(1 + trips * ATTN_UNROLL, nblk):
        fn(sb * BAND)


def _attn_fwd(name, q, k, v, d):
    nblk = S // d // BAND

    def body(q_ref, k_ref, v_ref, o_ref, lse_ref):
        def block(r0, k0, nkeys):
            q2 = _stack_heads(q_ref[pl.ds(r0, BAND), :])
            s = jnp.where(_band_mask(nkeys), _dot_nt(q2, k_ref[pl.ds(k0, nkeys), :]), NEG_INF)
            m = jnp.max(s, axis=-1, keepdims=True)
            p = jnp.exp(s - m)
            l = jnp.sum(p, axis=-1, keepdims=True)
            o2 = _dot_nn(p.astype(BF16), v_ref[pl.ds(k0, nkeys), :]) / l
            lse2 = jnp.broadcast_to(m + jnp.log(l), (2 * BAND, 128))
            o_ref[pl.ds(r0, BAND), :] = _unstack(o2)
            lse_ref[pl.ds(r0, BAND), :] = _unstack(lse2)

        block(0, 0, BAND)

        _for_later_blocks(nblk, lambda r0: block(r0, r0 - BAND, 2 * BAND))

    spec = _class_spec(d)
    return pl.pallas_call(
        body, name=name, grid=(8 * d,),
        in_specs=[spec] * 3, out_specs=[spec] * 2,
        out_shape=[_sds((S // d, d * D), F32)] * 2,
        compiler_params=_cparams(1),
    )(q, k, v)


def _attn_bwd(name, q, k, v, do, lse, dd, d):
    nblk = S // d // BAND

    def body(q_ref, k_ref, v_ref, do_ref, lse_ref, dd_ref, dq_ref, dk_ref, dv_ref):
        def column(ref, r0):
            rows = pl.ds(r0, BAND)
            return jnp.concatenate([ref[rows, 0:1], ref[rows, HEAD_DIM:HEAD_DIM + 1]], axis=0)

        def block(r0, k0, nkeys, first):
            q2 = _stack_heads(q_ref[pl.ds(r0, BAND), :])
            do2 = _stack_heads(do_ref[pl.ds(r0, BAND), :])
            kk = k_ref[pl.ds(k0, nkeys), :]
            vv = v_ref[pl.ds(k0, nkeys), :]
            s = jnp.where(_band_mask(nkeys), _dot_nt(q2, kk), NEG_INF)
            p = jnp.exp(s - column(lse_ref, r0))
            ds = (p * (_dot_nt(do2, vv) - column(dd_ref, r0))).astype(BF16)
            dq_ref[pl.ds(r0, BAND), :] = _unstack(_dot_nn(ds, kk))
            dk_part = _dot_tn(ds, q2)
            dv_part = _dot_tn(p.astype(BF16), do2)
            if first:
                dk_ref[pl.ds(k0, nkeys), :] = dk_part
                dv_ref[pl.ds(k0, nkeys), :] = dv_part
            else:
                dk_ref[pl.ds(k0, BAND), :] += dk_part[:BAND]
                dv_ref[pl.ds(k0, BAND), :] += dv_part[:BAND]
                dk_ref[pl.ds(k0 + BAND, BAND), :] = dk_part[BAND:]
                dv_ref[pl.ds(k0 + BAND, BAND), :] = dv_part[BAND:]

        block(0, 0, BAND, True)

        _for_later_blocks(nblk, lambda r0: block(r0, r0 - BAND, 2 * BAND, False))

    spec = _class_spec(d)
    return pl.pallas_call(
        body, name=name, grid=(8 * d,),
        in_specs=[spec] * 6, out_specs=[spec] * 3,
        out_shape=[_sds((S // d, d * D), F32)] * 3,
        compiler_params=_cparams(1),
    )(q, k, v, do, lse, dd)


MIX_TILE = 256
DILATIONS = tuple(d for _, d in BRANCHES)


def _branch_weights(la, lb, lc):
    m = jnp.maximum(jnp.maximum(la, lb), lc)
    ea, eb, ec = jnp.exp(la - m), jnp.exp(lb - m), jnp.exp(lc - m)
    den = ea + eb + ec
    return ea / den, eb / den, ec / den


def _mix_operands(outs, lses):
    specs = [_class_block(d, MIX_TILE) for d in DILATIONS] * 2
    scratch = [pltpu.VMEM((NCHUNK, MIX_TILE, 128), F32)] * 4
    return list(outs) + list(lses), specs, scratch


def _mix_fwd(name, outs, lses):
    def body(o0, o1, o2, l0, l1, l2, o_ref, to1, to2, tl1, tl2):
        for blk, tmp, d in ((o1, to1, DILATIONS[1]), (o2, to2, DILATIONS[2]), (l1, tl1, DILATIONS[1]), (l2, tl2, DILATIONS[2])):
            _tokens_from_classes(blk, tmp, d, MIX_TILE)
        for c in range(NCHUNK):
            wa, wb, wc = _branch_weights(l0[:, _chunk(c)], tl1[c], tl2[c])
            o_ref[:, _chunk(c)] = (wa * o0[:, _chunk(c)] + wb * to1[c] + wc * to2[c]).astype(o_ref.dtype)

    operands, specs, scratch = _mix_operands(outs, lses)
    return pl.pallas_call(
        body, name=name, grid=(S // MIX_TILE,),
        in_specs=specs, out_specs=_row_spec(MIX_TILE), out_shape=_sds((S, D), BF16),
        scratch_shapes=scratch, compiler_params=_cparams(1),
    )(*operands)


def _head_sum(x, ones_blockdiag):
    hi = x.astype(BF16)
    r1 = x - hi.astype(F32)
    mid = r1.astype(BF16)
    lo = (r1 - mid.astype(F32)).astype(BF16)
    return _dot_nn(hi, ones_blockdiag) + _dot_nn(mid, ones_blockdiag) + _dot_nn(lo, ones_blockdiag)


def _mix_bwd(name, do, outs, lses, ones_blockdiag):
    def body(do_ref, o0, o1, o2, l0, l1, l2, ones_ref, d0, d1, d2, t0, t1, t2,
             to1, to2, tl1, tl2, td1, td2, tt1, tt2):
        for blk, tmp, d in ((o1, to1, DILATIONS[1]), (o2, to2, DILATIONS[2]), (l1, tl1, DILATIONS[1]), (l2, tl2, DILATIONS[2])):
            _tokens_from_classes(blk, tmp, d, MIX_TILE)
        ones = ones_ref[...]
        for c in range(NCHUNK):
            w = _branch_weights(l0[:, _chunk(c)], tl1[c], tl2[c])
            dov = do_ref[:, _chunk(c)]
            o = w[0] * o0[:, _chunk(c)] + w[1] * to1[c] + w[2] * to2[c]
            t = _head_sum(dov * o, ones)
            d0[:, _chunk(c)] = (w[0] * dov).astype(d0.dtype)
            t0[:, _chunk(c)] = w[0] * t
            td1[c], tt1[c] = w[1] * dov, w[1] * t
            td2[c], tt2[c] = w[2] * dov, w[2] * t
        for tmp, blk, d in ((td1, d1, DILATIONS[1]), (tt1, t1, DILATIONS[1]), (td2, d2, DILATIONS[2]), (tt2, t2, DILATIONS[2])):
            _classes_from_tokens(tmp, blk, d, MIX_TILE)

    operands, specs, scratch = _mix_operands(outs, lses)
    out_specs = [_class_block(d, MIX_TILE) for d in DILATIONS] * 2
    out_shape = [_sds((S // d, d * D), BF16) for d in DILATIONS] + [_sds((S // d, d * D), F32) for d in DILATIONS]
    return pl.pallas_call(
        body, name=name, grid=(S // MIX_TILE,),
        in_specs=[_row_spec(MIX_TILE)] + specs + [_vec_spec(128, 128)],
        out_specs=out_specs, out_shape=out_shape,
        scratch_shapes=scratch + [pltpu.VMEM((NCHUNK, MIX_TILE, 128), F32)] * 4,
        compiler_params=_cparams(1),
    )(do, *operands, ones_blockdiag)


def _attn_bwd_post(name, grads, cos_t, sin_t):
    tm = MIX_TILE
    scale = HEAD_DIM ** -0.5

    def unrope(x, cs, sn):
        return x * cs - _swap_halves(x) * sn

    def body(*refs):
        in_refs = refs[:9]
        cos_ref, sin_ref, dq_ref, dkv_ref, tmp_ref = refs[9:]
        cs = cos_ref[...]
        sn = sin_ref[...]
        for g, d in enumerate(DILATIONS):
            for which, blk in enumerate(in_refs[3 * g:3 * g + 3]):
                if d > 1:
                    _tokens_from_classes(blk, tmp_ref, d, tm)
                for c in range(NCHUNK):
                    x = tmp_ref[c] if d > 1 else blk[:, _chunk(c)]
                    if which == 0:
                        dq_ref[:, _chunk(c, g * D)] = (unrope(x, cs, sn) * scale).astype(dq_ref.dtype)
                    elif which == 1:
                        dkv_ref[:, _chunk(c, g * D)] = unrope(x, cs, sn).astype(dkv_ref.dtype)
                    else:
                        dkv_ref[:, _chunk(c, QW + g * D)] = x.astype(dkv_ref.dtype)

    operands = [a for branch in grads for a in branch]
    tab = pl.BlockSpec((tm, 128), lambda i: (i, 0))
    return pl.pallas_call(
        body, name=name, grid=(S // tm,),
        in_specs=[_class_block(d, tm) for d in DILATIONS for _ in range(3)] + [tab, tab],
        out_specs=[pl.BlockSpec((tm, QW), lambda i: (i, 0)), pl.BlockSpec((tm, 2 * QW), lambda i: (i, 0))],
        out_shape=[_sds((S, QW), BF16), _sds((S, 2 * QW), BF16)],
        scratch_shapes=[pltpu.VMEM((NCHUNK, tm, 128), F32)],
        compiler_params=_cparams(1),
    )(*operands, cos_t, sin_t)


def _adamw(name, parts, w, m, v):
    n, rows, cols = parts.shape
    tr = rows
    for cand in (256, 176, 128, 64, 32, 16, 8):
        if rows % cand == 0:
            tr = cand
            break
    def body(p_ref, w_ref, m_ref, v_ref, g_ref, d_ref, nm_ref, nv_ref):
        g = p_ref[0].astype(F32)
        for j in range(1, n):
            g = g + p_ref[j].astype(F32)
        g_ref[...] = g
        d_ref[...], nm_ref[...], nv_ref[...] = _adam_update(g, w_ref[...], m_ref[...], v_ref[...])

    blk = pl.BlockSpec((tr, cols), lambda i: (i, 0))
    return pl.pallas_call(
        body, name=name, grid=(rows // tr,),
        in_specs=[pl.BlockSpec((n, tr, cols), lambda i: (0, i, 0)), blk, blk, blk],
        out_specs=[blk] * 4, out_shape=[_sds((rows, cols), F32)] * 4,
        compiler_params=_cparams(1),
    )(parts, w, m, v)


def _adam_update(g, w, m, v):
    c1 = 1.0 / (1.0 - ADAM_B1 ** ADAM_STEP)
    c2 = 1.0 / (1.0 - ADAM_B2 ** ADAM_STEP)
    nm = ADAM_B1 * m + (1.0 - ADAM_B1) * g
    nv = ADAM_B2 * v + (1.0 - ADAM_B2) * (g * g)
    return -ADAM_LR * ((nm * c1) / (jnp.sqrt(nv * c2) + ADAM_EPS) + ADAM_WD * w), nm, nv


GAIN_ROWS = 16


def _pack_small(name, gain_tiles, taps):
    ng = len(gain_tiles)

    def body(*refs):
        o_ref = refs[-1]
        o_ref[...] = jnp.zeros_like(o_ref)
        for i in range(ng):
            o_ref[i:i + 1, :] = refs[i][0:1, :]
        o_ref[ng:ng + 3, :] = refs[ng][0:3, :]

    return pl.pallas_call(body, name=name, out_shape=_sds((GAIN_ROWS, D), F32))(*gain_tiles, taps)


def _adamw_gains(name, parts, params):
    np_ = len(params)
    shapes = [w.shape for w, _, _ in params]

    def body(p_ref, *refs):
        ins, outs = refs[:3 * np_], refs[3 * np_:]

        def total(lo, rows):
            g = p_ref[0, lo:lo + rows, :]
            for j in range(1, NDEV):
                g = g + p_ref[j, lo:lo + rows, :]
            return g

        lo = 0
        for i, shape in enumerate(shapes):
            g = total(lo, shape[0])
            lo += shape[0]
            w_ref, m_ref, v_ref = ins[3 * i:3 * i + 3]
            g_ref, d_ref, nm_ref, nv_ref = outs[4 * i:4 * i + 4]
            g_ref[...] = g
            d_ref[...], nm_ref[...], nv_ref[...] = _adam_update(g, w_ref[...], m_ref[...], v_ref[...])
        taps_ref = outs[-1]
        taps_ref[...] = jnp.zeros_like(taps_ref)
        taps_ref[0:3, :] = total(lo, 3)

    out_shape = [_sds(shape, F32) for shape in shapes for _ in range(4)] + [_sds((8, D), F32)]
    outs = pl.pallas_call(body, name=name, out_shape=out_shape)(parts, *[a for p in params for a in p])
    return [list(outs[4 * i:4 * i + 4]) for i in range(np_)], outs[-1]


def _exchange(name, arrays, kind):
    n = len(arrays)
    gather = kind == "gather"
    out_shape = [_sds((NDEV,) + a.shape if gather else a.shape, a.dtype) for a in arrays]

    def body(*refs):
        srcs, outs = refs[:n], refs[n:2 * n]
        send_sems, recv_sems, local_sems = refs[2 * n:]
        x, y, c = lax.axis_index("x"), lax.axis_index("y"), lax.axis_index("c")
        me = 4 * x + 2 * y + c
        pending = []
        for t in range(n):
            own = pltpu.make_async_copy(srcs[t] if gather else srcs[t].at[me], outs[t].at[me], local_sems.at[t])
            own.start()
            pending.append(own)
            for rel in range(1, NDEV):
                px = 1 - x if rel & 4 else x
                py = 1 - y if rel & 2 else y
                pc = 1 - c if rel & 1 else c
                peer = 4 * px + 2 * py + pc
                send = pltpu.make_async_remote_copy(
                    src_ref=srcs[t] if gather else srcs[t].at[peer], dst_ref=outs[t].at[me],
                    send_sem=send_sems.at[t, rel - 1], recv_sem=recv_sems.at[t, rel - 1],
                    device_id=(px, py, pc), device_id_type=MESH)
                send.start()
                arrive = pltpu.make_async_remote_copy(
                    src_ref=srcs[t] if gather else srcs[t].at[me], dst_ref=outs[t].at[peer],
                    send_sem=send_sems.at[t, rel - 1], recv_sem=recv_sems.at[t, rel - 1],
                    device_id=(px, py, pc), device_id_type=MESH)
                pending.append((send, arrive))
        for item in pending:
            if isinstance(item, tuple):
                item[0].wait_send()
                item[1].wait_recv()
            else:
                item.wait()

    any_spec = pl.BlockSpec(memory_space=pl.ANY)
    outs = pl.pallas_call(
        body, name=name,
        in_specs=[any_spec] * n, out_specs=[any_spec] * n, out_shape=out_shape,
        scratch_shapes=[pltpu.SemaphoreType.DMA((n, NDEV - 1)), pltpu.SemaphoreType.DMA((n, NDEV - 1)),
                        pltpu.SemaphoreType.DMA((n,))],
    )(*arrays)
    return list(outs)


_HBM_SPEC = pl.BlockSpec(memory_space=pltpu.HBM)
_SEM_SPEC = pl.BlockSpec(memory_space=pltpu.SEMAPHORE)
_DATAFLOW = pltpu.SideEffectType.DATAFLOW_SIDE_EFFECTING


def _peers():
    x, y, c = lax.axis_index("x"), lax.axis_index("y"), lax.axis_index("c")
    out = []
    for rel in range(1, NDEV):
        px = 1 - x if rel & 4 else x
        py = 1 - y if rel & 2 else y
        pc = 1 - c if rel & 1 else c
        out.append((rel - 1, (px, py, pc), 4 * px + 2 * py + pc))
    return 4 * x + 2 * y + c, out


def _hbm(a):
    return pltpu.HBM(a.shape, a.dtype)


def _own_slot(a, me, kind):
    mine = a[None] if kind == "gather" else lax.dynamic_slice_in_dim(a, me, 1, axis=0)
    shape = (NDEV,) + mine.shape[1:]
    return lax.dynamic_update_slice_in_dim(lax.empty(shape, a.dtype), mine, me, axis=0)


def _exchange_start(name, arrays, me, kind):
    n = len(arrays)
    gather = kind == "gather"
    lands = [_own_slot(a, me, kind) for a in arrays]

    def body(*refs):
        src_refs, land_refs = refs[:n], refs[n:2 * n]
        send_sems, recv_sems = refs[2 * n], refs[2 * n + 1]
        token = refs[-1]
        my_block, peers = _peers()
        for t in range(n):
            for slot, dev, block in peers:
                pltpu.make_async_remote_copy(
                    src_ref=src_refs[t] if gather else src_refs[t].at[block], dst_ref=land_refs[t].at[my_block],
                    send_sem=send_sems.at[t * (NDEV - 1) + slot], recv_sem=recv_sems.at[t * (NDEV - 1) + slot],
                    device_id=dev, device_id_type=MESH).start()
        token[...] = jnp.zeros_like(token)

    operands = [pltpu.with_memory_space_constraint(a, pltpu.HBM) for a in list(arrays) + lands]
    outs = pl.pallas_call(
        body, name=name,
        out_shape=(pltpu.SemaphoreType.DMA((n * (NDEV - 1),)), pltpu.SemaphoreType.DMA((n * (NDEV - 1),)),
                   *[_hbm(a) for a in operands], _sds((8, 128), F32)),
        in_specs=[_HBM_SPEC] * (2 * n),
        out_specs=(_SEM_SPEC, _SEM_SPEC, *[_HBM_SPEC] * (2 * n), pl.BlockSpec(memory_space=pltpu.VMEM)),
        input_output_aliases={i: 2 + i for i in range(2 * n)},
        compiler_params=pltpu.CompilerParams(has_side_effects=_DATAFLOW),
    )(*operands)
    return (outs[0], outs[1], list(outs[2:2 + n]), list(outs[2 + n:2 + 2 * n])), outs[-1]


def _exchange_wait(name, started, t, after, kind):
    send_sems, recv_sems, srcs, lands = started
    gather = kind == "gather"

    def body(src_ref, land_ref, send_ref, recv_ref, after_ref, src_out, land_out):
        _, peers = _peers()
        for slot, dev, block in peers:
            copy = pltpu.make_async_remote_copy(
                src_ref=src_ref if gather else src_ref.at[block], dst_ref=land_ref.at[block],
                send_sem=send_ref.at[t * (NDEV - 1) + slot], recv_sem=recv_ref.at[t * (NDEV - 1) + slot],
                device_id=dev, device_id_type=MESH)
            copy.wait_send()
            copy.wait_recv()

    return pl.pallas_call(
        body, name=name, out_shape=(_hbm(srcs[t]), _hbm(lands[t])),
        in_specs=(_HBM_SPEC, _HBM_SPEC, _SEM_SPEC, _SEM_SPEC, pl.BlockSpec(memory_space=pl.ANY)),
        out_specs=(_HBM_SPEC, _HBM_SPEC), input_output_aliases={0: 0, 1: 1},
        compiler_params=pltpu.CompilerParams(has_side_effects=_DATAFLOW),
    )(srcs[t], lands[t], send_sems, recv_sems, after)[1]


def _ffn_fwd(tag, h, g_pre, weight):
    n = _rms_fwd(f"ffn_prenorm_{tag}", h, [g_pre])[0]
    wg = weight(f"gate_up_{tag}", n)
    gu = _fwd_cols_blocked(f"ffn_gate_up_{tag}", n, wg).reshape(2, NFB, S, FB)
    act = _swiglu_fwd(f"ffn_act_{tag}", gu)
    wd4 = weight(f"down_{tag}", act).reshape(NFB, FB, D)
    f = _fwd_kblocked(f"ffn_down_{tag}", act, wd4)
    return (n, gu, act, wg, wd4), f


def _ffn_bwd(tag, dh_out, h_in, f, saved, g_pre, g_post, send):
    n, gu, act, wg, wd4 = saved
    df, (dg_post,) = _rms_bwd(f"ffn_postnorm_bwd_{tag}", f, [(g_post, dh_out)], None, BF16)
    tok = send(f"down_{tag}", _bwd_w_kblocked(f"ffn_down_dw_{tag}", act, df).reshape(NDEV, DFF // NDEV, D))
    da = _bwd_x_kblocked(f"ffn_down_dx_{tag}", df, wd4, after=tok)
    dgu = _swiglu_bwd(f"ffn_act_bwd_{tag}", gu, da).reshape(NDEV, S, FB)
    tok = send(f"gate_up_{tag}", _bwd_w_cols_blocked(f"ffn_gate_up_dw_{tag}", n, dgu))
    dn = _bwd_x_cols_blocked(f"ffn_gate_up_dx_{tag}", dgu, wg, after=tok)
    dh_in, (dg_pre,) = _rms_bwd(f"ffn_prenorm_bwd_{tag}", h_in, [(g_pre, dn)], dh_out, F32)
    return dh_in, dg_pre, dg_post


def kernel(x, positions, mix_norm_pre, mix_norm_post, ffn_norm_pre, ffn_norm_post, ffn_w_gate_up, ffn_w_down, conv_w_in, conv_w, conv_w_out, kv_norm, w_kv, w_q, w_o, loss_target, m_mix_norm_pre, m_mix_norm_post, m_ffn_norm_pre, m_ffn_norm_post, m_ffn_w_gate_up, m_ffn_w_down, m_conv_w_in, m_conv_w, m_conv_w_out, m_kv_norm, m_w_kv, m_w_q, m_w_o, v_mix_norm_pre, v_mix_norm_post, v_ffn_norm_pre, v_ffn_norm_post, v_ffn_w_gate_up, v_ffn_w_down, v_conv_w_in, v_conv_w, v_conv_w_out, v_kv_norm, v_w_kv, v_w_q, v_w_o):
    me = 4 * lax.axis_index("x") + 2 * lax.axis_index("y") + lax.axis_index("c")
    h0 = x.reshape(S, D)
    target = loss_target.reshape(S, D)
    row = lambda a, l: a[l].reshape(1, D)
    g_kv = kv_norm.reshape(1, D)

    cw_shard = jnp.pad(conv_w[0], ((0, 5), (0, 0)))
    names = ["conv_in", "conv_w", "conv_out", "gate_up_0", "down_0", "kv", "q", "o", "gate_up_1", "down_1"]
    shards = [conv_w_in[0], cw_shard, conv_w_out[0], ffn_w_gate_up[0], ffn_w_down[0],
              w_kv, w_q[0], w_o[0], ffn_w_gate_up[1], ffn_w_down[1]]
    shards = [s if n == "conv_w" else s.astype(BF16) for n, s in zip(names, shards)]
    gather, _ = _exchange_start("gather_weights_start", shards, me, "gather")

    def weight(name, after):
        return _exchange_wait(f"gather_wait_{name}", gather, names.index(name), after, "gather")

    sent = {}

    def send(name, grad):
        sent[name], token = _exchange_start(f"scatter_start_{name}", [grad], me, "scatter")
        return token

    n0 = _rms_fwd("mix_prenorm_0", h0, [row(mix_norm_pre, 0)])[0]
    win_g = weight("conv_in", n0)
    cw = weight("conv_w", n0).transpose(1, 0, 2).reshape(8, D)
    z = _fwd_cols("conv_in", n0, win_g)
    pre = _conv_fwd("conv_gate", z, cw)
    wout = weight("conv_out", pre).reshape(D, D)
    y0 = _fwd_rows("conv_out", pre, wout)
    h1 = _resid_rms("mix_postnorm_0", h0, y0, row(mix_norm_post, 0))
    ffn0, f0 = _ffn_fwd("0", h1, row(ffn_norm_pre, 0), weight)
    h2 = _resid_rms("ffn_postnorm_0", h1, f0, row(ffn_norm_post, 0))

    nk, n2 = _rms_fwd("kv_and_mix_prenorm_1", h2, [g_kv, row(mix_norm_pre, 1)])
    wkv_g = weight("kv", nk)
    wkv = wkv_g.transpose(1, 0, 2).reshape(D, 2 * QW)
    half = HEAD_DIM // 2
    inv_freq = ROPE_THETA ** (-jnp.arange(half, dtype=F32) / half)
    tables = _rope_tables("rope_tables", positions.reshape(S, 1), jnp.tile(inv_freq, 4).reshape(1, 128))
    qc, kc, vc, o_c, lse_c = [], [], [], [], []
    for g, d in enumerate(DILATIONS):
        kc.append(_proj_classes(f"k_proj_{g}", nk, wkv, g, d, tables, 1.0))
        vc.append(_proj_classes(f"v_proj_{g}", nk, wkv, len(DILATIONS) + g, d, None, None))
    wq_g = weight("q", vc[-1])
    wq = wq_g.transpose(1, 0, 2).reshape(D, QW)
    for g, d in enumerate(DILATIONS):
        qc.append(_proj_classes(f"q_proj_{g}", n2, wq, g, d, tables, HEAD_DIM ** -0.5))
        o_g, lse_g = _attn_fwd(f"attn_fwd_{g}", qc[g], kc[g], vc[g], d)
        o_c.append(o_g)
        lse_c.append(lse_g)
    o_mix = _mix_fwd("attn_mix", o_c, lse_c)
    wo = weight("o", o_mix).reshape(D, D)
    y1 = _fwd_rows("attn_out", o_mix, wo)
    h3 = _resid_rms("mix_postnorm_1", h2, y1, row(mix_norm_post, 1))
    ffn1, f1 = _ffn_fwd("1", h3, row(ffn_norm_pre, 1), weight)
    h4 = _resid_rms("ffn_postnorm_1", h3, f1, row(ffn_norm_post, 1))

    dh4, sq = _loss_grad("loss", h4, target)
    loss = lax.psum(jnp.sum(sq) * (0.5 / D), ("x", "y", "c"))

    dh3, dg_fpre1, dg_fpost1 = _ffn_bwd(
        "1", dh4, h3, f1, ffn1, row(ffn_norm_pre, 1), row(ffn_norm_post, 1), send)
    dy1, (dg_mpost1,) = _rms_bwd("mix_postnorm_bwd_1", y1, [(row(mix_norm_post, 1), dh3)], None, BF16)
    tok = send("o", _bwd_w_rows("attn_out_dw", o_mix, dy1).reshape(NDEV, D // NDEV, D))
    do = _bwd_x_rows("attn_out_dx", dy1, wo, F32, after=tok)
    lane = jnp.arange(128)
    ones_blockdiag = (lane[:, None] // HEAD_DIM == lane[None, :] // HEAD_DIM).astype(BF16)
    mixed = _mix_bwd("attn_mix_bwd", do, o_c, lse_c, ones_blockdiag)
    branch_grads = [_attn_bwd(f"attn_bwd_{g}", qc[g], kc[g], vc[g], mixed[g], lse_c[g], mixed[3 + g], d)
                    for g, d in enumerate(DILATIONS)]
    dq_raw, dkv = _attn_bwd_post("attn_bwd_post", branch_grads, *tables)
    tok = send("kv", _bwd_w_cols("kv_proj_dw", nk, dkv, 2 * QW // NDEV))
    dnk = _bwd_x_plain("kv_proj_dx", dkv, wkv, 2, after=tok)
    tok = send("q", _bwd_w_cols("q_proj_dw", n2, dq_raw, QW // NDEV))
    dn2 = _bwd_x_plain("q_proj_dx", dq_raw, wq, 1, after=tok)
    dh2, (dg_kv, dg_mpre1) = _rms_bwd("kv_and_mix_prenorm_bwd_1", h2,
                                      [(g_kv, dnk), (row(mix_norm_pre, 1), dn2)], dh3, F32)

    dh1, dg_fpre0, dg_fpost0 = _ffn_bwd(
        "0", dh2, h1, f0, ffn0, row(ffn_norm_pre, 0), row(ffn_norm_post, 0), send)
    dy0, (dg_mpost0,) = _rms_bwd("mix_postnorm_bwd_0", y0, [(row(mix_norm_post, 0), dh1)], None, BF16)
    tok = send("conv_out", _bwd_w_rows("conv_out_dw", pre, dy0).reshape(NDEV, D // NDEV, D))
    dpre = _bwd_x_rows("conv_out_dx", dy0, wout, BF16, after=tok)
    dz, dcw = _conv_bwd("conv_gate_bwd", z, dpre, cw)
    tok = send("conv_in", _bwd_w_cols("conv_in_dw", n0, dz, 3 * D // NDEV))
    dn0 = _bwd_x_plain("conv_in_dx", dz, win_g.transpose(1, 0, 2).reshape(D, 3 * D), 1, after=tok)
    dh0, (dg_mpre0,) = _rms_bwd("mix_prenorm_bwd_0", h0, [(row(mix_norm_pre, 0), dn0)], dh1, F32)

    small = _pack_small("pack_small_grads", [dg_mpre0, dg_mpre1, dg_mpost0, dg_mpost1, dg_fpre0, dg_fpre1,
                                             dg_fpost0, dg_fpost1, dg_kv], dcw)
    small_all = _exchange("gather_small_grads", [small], "gather")[0]

    done = [small_all]

    def upd(tag, w, m, v):
        parts = _exchange_wait(f"scatter_wait_{tag}", sent[tag], 0, done[-1], "scatter")
        shape = w.shape
        flat = lambda a: a.reshape(parts.shape[1:])
        res = _adamw(f"adamw_{tag}", parts, flat(w), flat(m), flat(v))
        done.append(res[0])
        return [r.reshape(shape) for r in res]

    def upd_layer(tag, l, w, m, v):
        return upd(f"{tag}_{l}", w[l], m[l], v[l])

    def stack(per_layer):
        return [jnp.stack([per_layer[0][i], per_layer[1][i]]) for i in range(4)]

    vec = lambda a: a.reshape(1, D)
    gain_res, taps = _adamw_gains("adamw_gains", small_all, [
        (mix_norm_pre, m_mix_norm_pre, v_mix_norm_pre), (mix_norm_post, m_mix_norm_post, v_mix_norm_post),
        (ffn_norm_pre, m_ffn_norm_pre, v_ffn_norm_pre), (ffn_norm_post, m_ffn_norm_post, v_ffn_norm_post),
        (vec(kv_norm), vec(m_kv_norm), vec(v_kv_norm))])
    dcw_mine = lax.dynamic_slice(taps, (0, me * 128), (8, 128))
    pad8 = lambda a, fill: jnp.pad(a[0], ((0, 5), (0, 0)), constant_values=fill)
    cw_res = [r[0:3].reshape(1, 3, 128) for r in
              _adamw("adamw_conv_w", dcw_mine.reshape(1, 8, 128), cw_shard, pad8(m_conv_w, 0.0), pad8(v_conv_w, 1.0))]

    res = {
        "mix_norm_pre": gain_res[0],
        "mix_norm_post": gain_res[1],
        "ffn_norm_pre": gain_res[2],
        "ffn_norm_post": gain_res[3],
        "kv_norm": [r.reshape(D) for r in gain_res[4]],
        "conv_w": cw_res,
    }
    down, gate_up = {}, {}
    down[1] = upd_layer("down", 1, ffn_w_down, m_ffn_w_down, v_ffn_w_down)
    gate_up[1] = upd_layer("gate_up", 1, ffn_w_gate_up, m_ffn_w_gate_up, v_ffn_w_gate_up)
    res["w_o"] = upd("o", w_o, m_w_o, v_w_o)
    res["w_q"] = upd("q", w_q, m_w_q, v_w_q)
    res["w_kv"] = upd("kv", w_kv, m_w_kv, v_w_kv)
    down[0] = upd_layer("down", 0, ffn_w_down, m_ffn_w_down, v_ffn_w_down)
    gate_up[0] = upd_layer("gate_up", 0, ffn_w_gate_up, m_ffn_w_gate_up, v_ffn_w_gate_up)
    res["ffn_w_down"] = stack(down)
    res["ffn_w_gate_up"] = stack(gate_up)
    res["conv_w_out"] = upd("conv_out", conv_w_out, m_conv_w_out, v_conv_w_out)
    res["conv_w_in"] = upd("conv_in", conv_w_in, m_conv_w_in, v_conv_w_in)
    order = ["mix_norm_pre", "mix_norm_post", "ffn_norm_pre", "ffn_norm_post", "ffn_w_gate_up", "ffn_w_down",
             "conv_w_in", "conv_w", "conv_w_out", "kv_norm", "w_kv", "w_q", "w_o"]
    out = [loss, dh0.reshape(1, S, D)]
    for i in range(4):
        out += [res[name][i] for name in order]
    return tuple(out)
```

```python
import jax
import jax.numpy as jnp
from jax import lax
from jax.experimental import pallas as pl
from jax.experimental.pallas import tpu as pltpu

F32 = jnp.float32
BF16 = jnp.bfloat16

S = 4096
D = 1024
NDEV = 8
HEAD_DIM = 64
QW = 3072
DFF = 2816
FB = 704
NFB = 4
BRANCHES = ((128, 1), (512, 4), (2048, 16))
BAND = 128
ROPE_THETA = 10000.0
RMS_EPS = 1e-6
NEG_INF = -1e30
ADAM_LR, ADAM_B1, ADAM_B2, ADAM_EPS, ADAM_WD, ADAM_STEP = 0.001, 0.9, 0.999, 1e-08, 0.01, 10

VMEM_LIMIT_BYTES = 52 * 1024 * 1024
ROW_TILE = 512
MESH = pl.DeviceIdType.MESH


def _cparams(ngrid):
    return pltpu.CompilerParams(dimension_semantics=("arbitrary",) * ngrid,
                                vmem_limit_bytes=VMEM_LIMIT_BYTES)


def _sds(shape, dtype):
    return jax.ShapeDtypeStruct(tuple(shape), dtype)


_DIMS = {"nn": (((1,), (0,)), ((), ())),
         "nt": (((1,), (1,)), ((), ())),
         "tn": (((0,), (0,)), ((), ()))}


def _matmul(name, a, b, *, mode, grid, a_blk, a_map, b_blk, b_map, o_shape, o_blk, o_map, out_dtype, after=None,
            out_groups=1):
    nk = grid[2]
    dims = _DIMS[mode]
    acc_shape = tuple(s for s in o_blk if s is not None)
    if out_groups > 1:
        acc_shape = (acc_shape[1], out_groups * acc_shape[2])
    extra = [] if after is None else [after]

    def store(o_ref, val):
        if out_groups == 1:
            o_ref[...] = val.astype(o_ref.dtype)
        else:
            n = o_ref.shape[-1]
            for grp in range(out_groups):
                o_ref[grp] = val[:, grp * n:(grp + 1) * n].astype(o_ref.dtype)

    def body(a_ref, b_ref, *rest):
        o_ref, scratch = rest[len(extra)], rest[len(extra) + 1:]
        part = lax.dot_general(a_ref[...], b_ref[...], dims, preferred_element_type=F32)
        if nk == 1:
            store(o_ref, part)
            return
        acc_ref = scratch[0]
        k = pl.program_id(2)

        @pl.when(k == 0)
        def _():
            acc_ref[...] = part

        @pl.when(k > 0)
        def _():
            acc_ref[...] += part

        @pl.when(k == nk - 1)
        def _():
            store(o_ref, acc_ref[...])

    return pl.pallas_call(
        body, name=name, grid=grid,
        in_specs=[pl.BlockSpec(a_blk, a_map), pl.BlockSpec(b_blk, b_map)] + [pl.BlockSpec(memory_space=pl.ANY)] * len(extra),
        out_specs=pl.BlockSpec(o_blk, o_map),
        out_shape=_sds(o_shape, out_dtype),
        scratch_shapes=[] if nk == 1 else [pltpu.VMEM(acc_shape, F32)],
        compiler_params=_cparams(3),
    )(a, b, *extra)


TM = 1024
TK = 1024


def _fwd_cols(name, a, wg, out_dtype=BF16):
    _, kdim, n = wg.shape
    return _matmul(name, a, wg, mode="nn", grid=(S // TM, NDEV, 1),
                   a_blk=(TM, kdim), a_map=lambda i, j, k: (i, 0),
                   b_blk=(None, kdim, n), b_map=lambda i, j, k: (j, 0, 0),
                   o_shape=(S, NDEV * n), o_blk=(TM, n), o_map=lambda i, j, k: (i, j), out_dtype=out_dtype)


def _fwd_cols_blocked(name, a, wg):
    _, kdim, n = wg.shape
    return _matmul(name, a, wg, mode="nn", grid=(S // TM, NDEV, 1),
                   a_blk=(TM, kdim), a_map=lambda i, j, k: (i, 0),
                   b_blk=(None, kdim, n), b_map=lambda i, j, k: (j, 0, 0),
                   o_shape=(NDEV, S, n), o_blk=(None, TM, n), o_map=lambda i, j, k: (j, i, 0), out_dtype=BF16)


def _fwd_rows(name, a, w, out_dtype=F32):
    kdim, n = w.shape
    tn = 512
    return _matmul(name, a, w, mode="nn", grid=(S // TM, n // tn, 1),
                   a_blk=(TM, kdim), a_map=lambda i, j, k: (i, 0),
                   b_blk=(kdim, tn), b_map=lambda i, j, k: (0, j),
                   o_shape=(S, n), o_blk=(TM, tn), o_map=lambda i, j, k: (i, j), out_dtype=out_dtype)


def _fwd_kblocked(name, a4, w4):
    nb, _, kb = a4.shape
    n = w4.shape[2]
    return _matmul(name, a4, w4, mode="nn", grid=(S // TM, 1, nb),
                   a_blk=(None, TM, kb), a_map=lambda i, j, k: (k, i, 0),
                   b_blk=(None, kb, n), b_map=lambda i, j, k: (k, 0, 0),
                   o_shape=(S, n), o_blk=(TM, n), o_map=lambda i, j, k: (i, 0), out_dtype=F32)


def _bwd_x_cols_blocked(name, dy8, wg, after=None):
    _, kdim, n = wg.shape
    return _matmul(name, dy8, wg, mode="nt", grid=(S // TM, 1, NDEV),
                   a_blk=(None, TM, n), a_map=lambda i, j, k: (k, i, 0),
                   b_blk=(None, kdim, n), b_map=lambda i, j, k: (k, 0, 0),
                   o_shape=(S, kdim), o_blk=(TM, kdim), o_map=lambda i, j, k: (i, 0), out_dtype=F32, after=after)


def _bwd_x_rows(name, dy, w, out_dtype, after=None):
    kdim, n = w.shape
    tkk = 512
    return _matmul(name, dy, w, mode="nt", grid=(S // TM, kdim // tkk, 1),
                   a_blk=(TM, n), a_map=lambda i, j, k: (i, 0),
                   b_blk=(tkk, n), b_map=lambda i, j, k: (j, 0),
                   o_shape=(S, kdim), o_blk=(TM, tkk), o_map=lambda i, j, k: (i, j), out_dtype=out_dtype, after=after)


def _bwd_x_kblocked(name, dy, w4, after=None):
    nb, kb, n = w4.shape
    return _matmul(name, dy, w4, mode="nt", grid=(S // TM, nb, 1),
                   a_blk=(TM, n), a_map=lambda i, j, k: (i, 0),
                   b_blk=(None, kb, n), b_map=lambda i, j, k: (j, 0, 0),
                   o_shape=(nb, S, kb), o_blk=(None, TM, kb), o_map=lambda i, j, k: (j, i, 0), out_dtype=BF16, after=after)


DW_COLS = 768


def _bwd_w_cols(name, a, dy, n):
    kdim = a.shape[1]
    groups = DW_COLS // n
    return _matmul(name, a, dy, mode="tn", grid=(1, NDEV // groups, S // TK),
                   a_blk=(TK, kdim), a_map=lambda i, j, k: (k, 0),
                   b_blk=(TK, DW_COLS), b_map=lambda i, j, k: (k, j),
                   o_shape=(NDEV, kdim, n), o_blk=(groups, kdim, n) if groups > 1 else (None, kdim, n),
                   o_map=lambda i, j, k: (j, 0, 0), out_dtype=BF16, out_groups=groups)


def _bwd_x_plain(name, dy, w, nk, after=None):
    kdim, n = w.shape
    return _matmul(name, dy, w, mode="nt", grid=(S // TM, 1, nk),
                   a_blk=(TM, n // nk), a_map=lambda i, j, k: (i, k),
                   b_blk=(kdim, n // nk), b_map=lambda i, j, k: (0, k),
                   o_shape=(S, kdim), o_blk=(TM, kdim), o_map=lambda i, j, k: (i, 0), out_dtype=F32, after=after)


def _bwd_w_cols_blocked(name, a, dy8):
    kdim = a.shape[1]
    n = dy8.shape[2]
    return _matmul(name, a, dy8, mode="tn", grid=(1, NDEV, S // TK),
                   a_blk=(TK, kdim), a_map=lambda i, j, k: (k, 0),
                   b_blk=(None, TK, n), b_map=lambda i, j, k: (j, k, 0),
                   o_shape=(NDEV, kdim, n), o_blk=(None, kdim, n), o_map=lambda i, j, k: (j, 0, 0), out_dtype=BF16)


def _bwd_w_rows(name, a, dy):
    kdim = a.shape[1]
    n = dy.shape[1]
    tmm = 512
    return _matmul(name, a, dy, mode="tn", grid=(kdim // tmm, 1, S // TK),
                   a_blk=(TK, tmm), a_map=lambda i, j, k: (k, i),
                   b_blk=(TK, n), b_map=lambda i, j, k: (k, 0),
                   o_shape=(kdim, n), o_blk=(tmm, n), o_map=lambda i, j, k: (i, 0), out_dtype=BF16)


def _bwd_w_kblocked(name, a4, dy):
    nb, _, kb = a4.shape
    n = dy.shape[1]
    return _matmul(name, a4, dy, mode="tn", grid=(nb, 1, S // TK),
                   a_blk=(None, TK, kb), a_map=lambda i, j, k: (i, k, 0),
                   b_blk=(TK, n), b_map=lambda i, j, k: (k, 0),
                   o_shape=(nb, kb, n), o_blk=(None, kb, n), o_map=lambda i, j, k: (i, 0, 0), out_dtype=BF16)


def _rstd(x):
    return lax.rsqrt(jnp.mean(x * x, axis=-1, keepdims=True) + RMS_EPS)


def _row_spec(tm=ROW_TILE, width=D):
    return pl.BlockSpec((tm, width), lambda i: (i, 0))


def _vec_spec(rows=1, width=D):
    return pl.BlockSpec((rows, width), lambda i: (0, 0))


def _rms_fwd(name, x, gains):
    n = len(gains)

    def body(x_ref, *refs):
        x_val = x_ref[...]
        xh = x_val * _rstd(x_val)
        for g_ref, o_ref in zip(refs[:n], refs[n:]):
            o_ref[...] = (xh * g_ref[...]).astype(o_ref.dtype)

    outs = pl.pallas_call(
        body, name=name, grid=(S // ROW_TILE,),
        in_specs=[_row_spec()] + [_vec_spec()] * n,
        out_specs=[_row_spec()] * n,
        out_shape=[_sds((S, D), BF16)] * n,
        compiler_params=_cparams(1),
    )(x, *gains)
    return list(outs)


def _resid_rms(name, h, y, g):
    def body(h_ref, y_ref, g_ref, o_ref):
        y_val = y_ref[...]
        o_ref[...] = h_ref[...] + (y_val * _rstd(y_val)) * g_ref[...]

    return pl.pallas_call(
        body, name=name, grid=(S // ROW_TILE,),
        in_specs=[_row_spec(), _row_spec(), _vec_spec()],
        out_specs=_row_spec(), out_shape=_sds((S, D), F32),
        compiler_params=_cparams(1),
    )(h, y, g)


def _rms_bwd(name, x, pairs, dres, out_dtype):
    n = len(pairs)
    has_res = dres is not None

    def body(x_ref, *refs):
        g_refs = refs[0:2 * n:2]
        dn_refs = refs[1:2 * n:2]
        pos = 2 * n
        res_ref = refs[pos] if has_res else None
        pos += int(has_res)
        dx_ref = refs[pos]
        dg_refs = refs[pos + 1:]
        step = pl.program_id(0)
        x_val = x_ref[...]
        r = _rstd(x_val)
        xh = x_val * r
        acc = res_ref[...] if has_res else jnp.zeros_like(x_val)
        for g_ref, dn_ref, dg_ref in zip(g_refs, dn_refs, dg_refs):
            dn = dn_ref[...].astype(F32)
            dxh = dn * g_ref[...]
            acc = acc + r * (dxh - xh * jnp.mean(dxh * xh, axis=-1, keepdims=True))
            part = jnp.sum(dn * xh, axis=0, keepdims=True)

            @pl.when(step == 0)
            def _():
                dg_ref[...] = jnp.zeros_like(dg_ref)

            dg_ref[0:1, :] += part

        dx_ref[...] = acc.astype(dx_ref.dtype)

    operands = [x]
    in_specs = [_row_spec()]
    for g, dn in pairs:
        operands += [g, dn]
        in_specs += [_vec_spec(), _row_spec()]
    if has_res:
        operands.append(dres)
        in_specs.append(_row_spec())
    outs = pl.pallas_call(
        body, name=name, grid=(S // ROW_TILE,),
        in_specs=in_specs,
        out_specs=[_row_spec()] + [_vec_spec(8)] * n,
        out_shape=[_sds((S, D), out_dtype)] + [_sds((8, D), F32)] * n,
        compiler_params=_cparams(1),
    )(*operands)
    return outs[0], list(outs[1:])


def _loss_grad(name, h, target):
    def body(h_ref, t_ref, dh_ref, part_ref):
        e = h_ref[...] - t_ref[...]
        dh_ref[...] = e * (1.0 / D)
        part = jnp.sum(e * e, axis=0, keepdims=True)
        step = pl.program_id(0)

        @pl.when(step == 0)
        def _():
            part_ref[...] = part

        @pl.when(step > 0)
        def _():
            part_ref[...] += part

    return pl.pallas_call(
        body, name=name, grid=(S // ROW_TILE,),
        in_specs=[_row_spec(), _row_spec()],
        out_specs=[_row_spec(), _vec_spec()],
        out_shape=[_sds((S, D), F32), _sds((1, D), F32)],
        compiler_params=_cparams(1),
    )(h, target)


def _shift_down(u, prev8, k):
    r = pltpu.roll(u, k, 0)
    p = pltpu.roll(prev8, k, 0)
    row = lax.broadcasted_iota(jnp.int32, prev8.shape, 0)
    top = jnp.where(row < k, p, r[0:8])
    return jnp.concatenate([top, r[8:]], axis=0)


def _shift_up(u, next8, k):
    tm = u.shape[0]
    r = pltpu.roll(u, tm - k, 0)
    p = pltpu.roll(next8, 8 - k, 0)
    row = lax.broadcasted_iota(jnp.int32, next8.shape, 0)
    bot = jnp.where(row >= 8 - k, p, r[tm - 8:tm])
    return jnp.concatenate([r[:tm - 8], bot], axis=0)


CONV_TILE = 512


def _halo_prev(col):
    return pl.BlockSpec((8, D), lambda i: (jnp.maximum(i * (CONV_TILE // 8) - 1, 0), col))


def _halo_next(col):
    last = S // 8 - 1
    return pl.BlockSpec((8, D), lambda i: (jnp.minimum((i + 1) * (CONV_TILE // 8), last), col))


def _conv_fwd(name, z, cw):
    def body(b_ref, c_ref, h_ref, cp_ref, hp_ref, cw_ref, o_ref):
        i = pl.program_id(0)
        u = c_ref[...].astype(F32) * h_ref[...].astype(F32)
        up = cp_ref[...].astype(F32) * hp_ref[...].astype(F32)
        up = jnp.where(i > 0, up, 0.0)
        cv = cw_ref[0:1, :] * _shift_down(u, up, 2) + cw_ref[1:2, :] * _shift_down(u, up, 1) + cw_ref[2:3, :] * u
        o_ref[...] = (b_ref[...].astype(F32) * cv).astype(o_ref.dtype)

    col = lambda c: pl.BlockSpec((CONV_TILE, D), lambda i: (i, c))
    return pl.pallas_call(
        body, name=name, grid=(S // CONV_TILE,),
        in_specs=[col(0), col(1), col(2), _halo_prev(1), _halo_prev(2), _vec_spec(8)],
        out_specs=_row_spec(CONV_TILE), out_shape=_sds((S, D), BF16),
        compiler_params=_cparams(1),
    )(z, z, z, z, z, cw)


def _conv_bwd(name, z, dpre, cw):
    nsteps = S // CONV_TILE

    def body(b_ref, c_ref, h_ref, cp_ref, hp_ref, dp_ref, dpn_ref, bn_ref, cw_ref, dz_ref, dcw_ref):
        i = pl.program_id(0)
        b = b_ref[...].astype(F32)
        c = c_ref[...].astype(F32)
        h = h_ref[...].astype(F32)
        dp = dp_ref[...].astype(F32)
        u = c * h
        up = jnp.where(i > 0, cp_ref[...].astype(F32) * hp_ref[...].astype(F32), 0.0)
        s1 = _shift_down(u, up, 1)
        s2 = _shift_down(u, up, 2)
        w0, w1, w2 = cw_ref[0:1, :], cw_ref[1:2, :], cw_ref[2:3, :]
        cv = w0 * s2 + w1 * s1 + w2 * u
        dcv = dp * b
        dcvn = jnp.where(i < nsteps - 1, dpn_ref[...].astype(F32) * bn_ref[...].astype(F32), 0.0)
        du = w2 * dcv + w1 * _shift_up(dcv, dcvn, 1) + w0 * _shift_up(dcv, dcvn, 2)
        dz_ref[:, 0:D] = (dp * cv).astype(dz_ref.dtype)
        dz_ref[:, D:2 * D] = (du * h).astype(dz_ref.dtype)
        dz_ref[:, 2 * D:3 * D] = (du * c).astype(dz_ref.dtype)

        @pl.when(i == 0)
        def _():
            dcw_ref[...] = jnp.zeros_like(dcw_ref)

        dcw_ref[0:1, :] += jnp.sum(dcv * s2, axis=0, keepdims=True)
        dcw_ref[1:2, :] += jnp.sum(dcv * s1, axis=0, keepdims=True)
        dcw_ref[2:3, :] += jnp.sum(dcv * u, axis=0, keepdims=True)

    col = lambda c: pl.BlockSpec((CONV_TILE, D), lambda i: (i, c))
    return pl.pallas_call(
        body, name=name, grid=(nsteps,),
        in_specs=[col(0), col(1), col(2), _halo_prev(1), _halo_prev(2),
                  _row_spec(CONV_TILE), _halo_next(0), _halo_next(0), _vec_spec(8)],
        out_specs=[pl.BlockSpec((CONV_TILE, 3 * D), lambda i: (i, 0)), _vec_spec(8)],
        out_shape=[_sds((S, 3 * D), BF16), _sds((8, D), F32)],
        compiler_params=_cparams(1),
    )(z, z, z, z, z, dpre, dpre, z, cw)


def _swiglu_fwd(name, gu):
    def body(gu_ref, o_ref):
        g = gu_ref[0].astype(F32)
        u = gu_ref[1].astype(F32)
        o_ref[...] = (g * jax.nn.sigmoid(g) * u).astype(o_ref.dtype)

    return pl.pallas_call(
        body, name=name, grid=(NFB, S // ROW_TILE),
        in_specs=[pl.BlockSpec((2, None, ROW_TILE, FB), lambda j, i: (0, j, i, 0))],
        out_specs=pl.BlockSpec((None, ROW_TILE, FB), lambda j, i: (j, i, 0)),
        out_shape=_sds((NFB, S, FB), BF16),
        compiler_params=_cparams(2),
    )(gu)


def _swiglu_bwd(name, gu, da):
    def body(gu_ref, da_ref, o_ref):
        g = gu_ref[0].astype(F32)
        u = gu_ref[1].astype(F32)
        d = da_ref[...].astype(F32)
        sg = jax.nn.sigmoid(g)
        o_ref[0] = (d * u * sg * (1.0 + g * (1.0 - sg))).astype(o_ref.dtype)
        o_ref[1] = (d * g * sg).astype(o_ref.dtype)

    blk = pl.BlockSpec((2, None, ROW_TILE, FB), lambda j, i: (0, j, i, 0))
    return pl.pallas_call(
        body, name=name, grid=(NFB, S // ROW_TILE),
        in_specs=[blk, pl.BlockSpec((None, ROW_TILE, FB), lambda j, i: (j, i, 0))],
        out_specs=blk, out_shape=_sds((2, NFB, S, FB), BF16),
        compiler_params=_cparams(2),
    )(gu, da)


def _rope_tables(name, pos_col, inv_freq_row):
    def body(pos_ref, f_ref, cos_ref, sin_ref):
        ang = pos_ref[...].astype(F32) * f_ref[...]
        lane = lax.broadcasted_iota(jnp.int32, ang.shape, 1)
        s = jnp.sin(ang)
        cos_ref[...] = jnp.cos(ang)
        sin_ref[...] = jnp.where((lane % HEAD_DIM) < HEAD_DIM // 2, -s, s)

    tab = pl.BlockSpec((ROW_TILE, 128), lambda i: (i, 0))
    return pl.pallas_call(
        body, name=name, grid=(S // ROW_TILE,),
        in_specs=[pl.BlockSpec((ROW_TILE, 1), lambda i: (i, 0)), _vec_spec(1, 128)],
        out_specs=[tab, tab], out_shape=[_sds((S, 128), F32)] * 2,
        compiler_params=_cparams(1),
    )(pos_col, inv_freq_row)


def _swap_halves(t):
    lane = lax.broadcasted_iota(jnp.int32, t.shape, 1)
    first = (lane % HEAD_DIM) < HEAD_DIM // 2
    return jnp.where(first, pltpu.roll(t, 128 - HEAD_DIM // 2, 1), pltpu.roll(t, HEAD_DIM // 2, 1))


NCHUNK = D // 128


def _chunk(c, base=0):
    return slice(base + c * 128, base + (c + 1) * 128)


def _class_rows(r, d, tm):
    return pl.ds(r, tm // d, stride=d) if d > 1 else slice(None)


def _class_block(d, tm):
    return pl.BlockSpec((tm // d, d * D), lambda i: (i, 0))


def _tokens_from_classes(blk_ref, tmp_ref, d, tm):
    for r in range(d):
        for c in range(NCHUNK):
            tmp_ref[c, _class_rows(r, d, tm), :] = blk_ref[:, _chunk(c, r * D)]


def _classes_from_tokens(tmp_ref, blk_ref, d, tm):
    for r in range(d):
        for c in range(NCHUNK):
            blk_ref[:, _chunk(c, r * D)] = tmp_ref[c, _class_rows(r, d, tm), :].astype(blk_ref.dtype)


def _proj_classes(name, a, w, col, d, tables, scale):
    kdim = a.shape[1]
    rope = tables is not None

    def body(a_ref, w_ref, *refs):
        if rope:
            cos_ref, sin_ref, o_ref, tmp_ref = refs
        else:
            o_ref, tmp_ref = refs
        acc = _dot_nn(a_ref[...], w_ref[...])
        for c in range(NCHUNK):
            tmp_ref[c] = acc[:, _chunk(c)]
        for r in range(d):
            rows = _class_rows(r, d, TM)
            if rope:
                cs = cos_ref[rows, :]
                sn = sin_ref[rows, :]
            for c in range(NCHUNK):
                x = tmp_ref[c, rows, :]
                if rope:
                    x = (x * cs + _swap_halves(x) * sn) * scale
                o_ref[:, _chunk(c, r * D)] = x.astype(o_ref.dtype)

    tab = pl.BlockSpec((TM, 128), lambda i: (i, 0))
    return pl.pallas_call(
        body, name=name, grid=(S // TM,),
        in_specs=[pl.BlockSpec((TM, kdim), lambda i: (i, 0)), pl.BlockSpec((kdim, D), lambda i: (0, col))]
                 + ([tab, tab] if rope else []),
        out_specs=_class_block(d, TM), out_shape=_sds((S // d, d * D), BF16),
        scratch_shapes=[pltpu.VMEM((NCHUNK, TM, 128), F32)],
        compiler_params=_cparams(1),
    )(a, w, *(tables if rope else ()))


ATTN_CHAINS = 4


def _attn_units(d):
    nblk = S // d // BAND
    return max(1, 2 * ATTN_CHAINS // nblk)


def _class_spec(d):
    return pl.BlockSpec((S // d, 128 * _attn_units(d)), lambda cb: (0, cb))


def _dot_nt(a, b):
    return lax.dot_general(a, b, _DIMS["nt"], preferred_element_type=F32)


def _dot_tn(a, b):
    return lax.dot_general(a, b, _DIMS["tn"], preferred_element_type=F32)


def _dot_nn(a, b):
    return lax.dot_general(a, b, _DIMS["nn"], preferred_element_type=F32)


def _band_mask(nkeys):
    qi = lax.broadcasted_iota(jnp.int32, (2 * BAND, nkeys), 0) % BAND
    kj = lax.broadcasted_iota(jnp.int32, (2 * BAND, nkeys), 1)
    if nkeys == BAND:
        return kj <= qi
    dist = qi + BAND - kj
    return (dist >= 0) & (dist <= BAND)


def _stack_heads(x):
    row = lax.broadcasted_iota(jnp.int32, (2 * BAND, 128), 0)
    lane = lax.broadcasted_iota(jnp.int32, (2 * BAND, 128), 1)
    keep = (row < BAND) == (lane < HEAD_DIM)
    return jnp.where(keep, jnp.concatenate([x, x], axis=0), jnp.zeros((), x.dtype))


def _unstack(x2):
    first_head = lax.broadcasted_iota(jnp.int32, (BAND, 128), 1) < HEAD_DIM
    return jnp.where(first_head, x2[:BAND], x2[BAND:])


def _for_later_blocks(nblk, units, fn):
    all_lanes = [slice(u * 128, (u + 1) * 128) for u in range(units)]
    unroll = max(1, ATTN_CHAINS // units)
    trips = (nblk - 1) // unroll
    if trips > 1:
        def step(i, carry):
            for j in range(unroll):
                for lanes in all_lanes:
                    fn(pl.multiple_of((1 + i * unroll + j) * BAND, BAND), lanes)
            return carry

        lax.fori_loop(0, trips, step, 0)
    else:
        trips = 0
    for sb in range(1 + trips * unroll, nblk):
        for lanes in all_lanes:
            fn(sb * BAND, lanes)


def _attn_fwd(name, q, k, v, d):
    nblk = S // d // BAND
    units = _attn_units(d)

    def body(q_ref, k_ref, v_ref, o_ref, lse_ref):
        def block(r0, k0, nkeys, lanes):
            q2 = _stack_heads(q_ref[pl.ds(r0, BAND), lanes])
            s = jnp.where(_band_mask(nkeys), _dot_nt(q2, k_ref[pl.ds(k0, nkeys), lanes]), NEG_INF)
            m = jnp.max(s, axis=-1, keepdims=True)
            p = jnp.exp(s - m)
            l = jnp.sum(p, axis=-1, keepdims=True)
            o2 = _dot_nn(p.astype(BF16), v_ref[pl.ds(k0, nkeys), lanes]) / l
            lse2 = jnp.broadcast_to(m + jnp.log(l), (2 * BAND, 128))
            o_ref[pl.ds(r0, BAND), lanes] = _unstack(o2)
            lse_ref[pl.ds(r0, BAND), lanes] = _unstack(lse2)

        for u in range(units):
            block(0, 0, BAND, slice(u * 128, (u + 1) * 128))

        _for_later_blocks(nblk, units, lambda r0, lanes: block(r0, r0 - BAND, 2 * BAND, lanes))

    spec = _class_spec(d)
    return pl.pallas_call(
        body, name=name, grid=(8 * d // units,),
        in_specs=[spec] * 3, out_specs=[spec] * 2,
        out_shape=[_sds((S // d, d * D), F32)] * 2,
        compiler_params=_cparams(1),
    )(q, k, v)


def _attn_bwd(name, q, k, v, do, lse, dd, d):
    nblk = S // d // BAND
    units = _attn_units(d)

    def body(q_ref, k_ref, v_ref, do_ref, lse_ref, dd_ref, dq_ref, dk_ref, dv_ref):
        def column(ref, r0, lanes):
            rows = pl.ds(r0, BAND)
            first = slice(lanes.start, lanes.start + 1)
            second = slice(lanes.start + HEAD_DIM, lanes.start + HEAD_DIM + 1)
            return jnp.concatenate([ref[rows, first], ref[rows, second]], axis=0)

        def block(r0, k0, nkeys, lanes, first):
            q2 = _stack_heads(q_ref[pl.ds(r0, BAND), lanes])
            do2 = _stack_heads(do_ref[pl.ds(r0, BAND), lanes])
            kk = k_ref[pl.ds(k0, nkeys), lanes]
            vv = v_ref[pl.ds(k0, nkeys), lanes]
            s = jnp.where(_band_mask(nkeys), _dot_nt(q2, kk), NEG_INF)
            p = jnp.exp(s - column(lse_ref, r0, lanes))
            ds = (p * (_dot_nt(do2, vv) - column(dd_ref, r0, lanes))).astype(BF16)
            dq_ref[pl.ds(r0, BAND), lanes] = _unstack(_dot_nn(ds, kk))
            dk_part = _dot_tn(ds, q2)
            dv_part = _dot_tn(p.astype(BF16), do2)
            if first:
                dk_ref[pl.ds(k0, nkeys), lanes] = dk_part
                dv_ref[pl.ds(k0, nkeys), lanes] = dv_part
            else:
                dk_ref[pl.ds(k0, BAND), lanes] += dk_part[:BAND]
                dv_ref[pl.ds(k0, BAND), lanes] += dv_part[:BAND]
                dk_ref[pl.ds(k0 + BAND, BAND), lanes] = dk_part[BAND:]
                dv_ref[pl.ds(k0 + BAND, BAND), lanes] = dv_part[BAND:]

        for u in range(units):
            block(0, 0, BAND, slice(u * 128, (u + 1) * 128), True)

        _for_later_blocks(nblk, units, lambda r0, lanes: block(r0, r0 - BAND, 2 * BAND, lanes, False))

    spec = _class_spec(d)
    return pl.pallas_call(
        body, name=name, grid=(8 * d // units,),
        in_specs=[spec] * 6, out_specs=[spec] * 3,
        out_shape=[_sds((S // d, d * D), F32)] * 3,
        compiler_params=_cparams(1),
    )(q, k, v, do, lse, dd)


MIX_TILE = 256
DILATIONS = tuple(d for _, d in BRANCHES)


def _branch_weights(la, lb, lc):
    m = jnp.maximum(jnp.maximum(la, lb), lc)
    ea, eb, ec = jnp.exp(la - m), jnp.exp(lb - m), jnp.exp(lc - m)
    den = ea + eb + ec
    return ea / den, eb / den, ec / den


def _mix_operands(outs, lses):
    specs = [_class_block(d, MIX_TILE) for d in DILATIONS] * 2
    scratch = [pltpu.VMEM((NCHUNK, MIX_TILE, 128), F32)] * 4
    return list(outs) + list(lses), specs, scratch


def _mix_fwd(name, outs, lses):
    def body(o0, o1, o2, l0, l1, l2, o_ref, to1, to2, tl1, tl2):
        for blk, tmp, d in ((o1, to1, DILATIONS[1]), (o2, to2, DILATIONS[2]), (l1, tl1, DILATIONS[1]), (l2, tl2, DILATIONS[2])):
            _tokens_from_classes(blk, tmp, d, MIX_TILE)
        for c in range(NCHUNK):
            wa, wb, wc = _branch_weights(l0[:, _chunk(c)], tl1[c], tl2[c])
            o_ref[:, _chunk(c)] = (wa * o0[:, _chunk(c)] + wb * to1[c] + wc * to2[c]).astype(o_ref.dtype)

    operands, specs, scratch = _mix_operands(outs, lses)
    return pl.pallas_call(
        body, name=name, grid=(S // MIX_TILE,),
        in_specs=specs, out_specs=_row_spec(MIX_TILE), out_shape=_sds((S, D), BF16),
        scratch_shapes=scratch, compiler_params=_cparams(1),
    )(*operands)


def _head_sum(x, ones_blockdiag):
    hi = x.astype(BF16)
    r1 = x - hi.astype(F32)
    mid = r1.astype(BF16)
    lo = (r1 - mid.astype(F32)).astype(BF16)
    return _dot_nn(hi, ones_blockdiag) + _dot_nn(mid, ones_blockdiag) + _dot_nn(lo, ones_blockdiag)


def _mix_bwd(name, do, outs, lses, ones_blockdiag):
    def body(do_ref, o0, o1, o2, l0, l1, l2, ones_ref, d0, d1, d2, t0, t1, t2,
             to1, to2, tl1, tl2, td1, td2, tt1, tt2):
        for blk, tmp, d in ((o1, to1, DILATIONS[1]), (o2, to2, DILATIONS[2]), (l1, tl1, DILATIONS[1]), (l2, tl2, DILATIONS[2])):
            _tokens_from_classes(blk, tmp, d, MIX_TILE)
        ones = ones_ref[...]
        for c in range(NCHUNK):
            w = _branch_weights(l0[:, _chunk(c)], tl1[c], tl2[c])
            dov = do_ref[:, _chunk(c)]
            o = w[0] * o0[:, _chunk(c)] + w[1] * to1[c] + w[2] * to2[c]
            t = _head_sum(dov * o, ones)
            d0[:, _chunk(c)] = (w[0] * dov).astype(d0.dtype)
            t0[:, _chunk(c)] = w[0] * t
            td1[c], tt1[c] = w[1] * dov, w[1] * t
            td2[c], tt2[c] = w[2] * dov, w[2] * t
        for tmp, blk, d in ((td1, d1, DILATIONS[1]), (tt1, t1, DILATIONS[1]), (td2, d2, DILATIONS[2]), (tt2, t2, DILATIONS[2])):
            _classes_from_tokens(tmp, blk, d, MIX_TILE)

    operands, specs, scratch = _mix_operands(outs, lses)
    out_specs = [_class_block(d, MIX_TILE) for d in DILATIONS] * 2
    out_shape = [_sds((S // d, d * D), BF16) for d in DILATIONS] + [_sds((S // d, d * D), F32) for d in DILATIONS]
    return pl.pallas_call(
        body, name=name, grid=(S // MIX_TILE,),
        in_specs=[_row_spec(MIX_TILE)] + specs + [_vec_spec(128, 128)],
        out_specs=out_specs, out_shape=out_shape,
        scratch_shapes=scratch + [pltpu.VMEM((NCHUNK, MIX_TILE, 128), F32)] * 4,
        compiler_params=_cparams(1),
    )(do, *operands, ones_blockdiag)


def _attn_bwd_post(name, grads, cos_t, sin_t):
    tm = MIX_TILE
    scale = HEAD_DIM ** -0.5

    def unrope(x, cs, sn):
        return x * cs - _swap_halves(x) * sn

    def body(*refs):
        in_refs = refs[:9]
        cos_ref, sin_ref, dq_ref, dkv_ref, tmp_ref = refs[9:]
        cs = cos_ref[...]
        sn = sin_ref[...]
        for g, d in enumerate(DILATIONS):
            for which, blk in enumerate(in_refs[3 * g:3 * g + 3]):
                if d > 1:
                    _tokens_from_classes(blk, tmp_ref, d, tm)
                for c in range(NCHUNK):
                    x = tmp_ref[c] if d > 1 else blk[:, _chunk(c)]
                    if which == 0:
                        dq_ref[:, _chunk(c, g * D)] = (unrope(x, cs, sn) * scale).astype(dq_ref.dtype)
                    elif which == 1:
                        dkv_ref[:, _chunk(c, g * D)] = unrope(x, cs, sn).astype(dkv_ref.dtype)
                    else:
                        dkv_ref[:, _chunk(c, QW + g * D)] = x.astype(dkv_ref.dtype)

    operands = [a for branch in grads for a in branch]
    tab = pl.BlockSpec((tm, 128), lambda i: (i, 0))
    return pl.pallas_call(
        body, name=name, grid=(S // tm,),
        in_specs=[_class_block(d, tm) for d in DILATIONS for _ in range(3)] + [tab, tab],
        out_specs=[pl.BlockSpec((tm, QW), lambda i: (i, 0)), pl.BlockSpec((tm, 2 * QW), lambda i: (i, 0))],
        out_shape=[_sds((S, QW), BF16), _sds((S, 2 * QW), BF16)],
        scratch_shapes=[pltpu.VMEM((NCHUNK, tm, 128), F32)],
        compiler_params=_cparams(1),
    )(*operands, cos_t, sin_t)


def _adamw(name, parts, w, m, v):
    n, rows, cols = parts.shape
    tr = rows
    for cand in (256, 176, 128, 64, 32, 16, 8):
        if rows % cand == 0:
            tr = cand
            break
    def body(p_ref, w_ref, m_ref, v_ref, g_ref, d_ref, nm_ref, nv_ref):
        g = p_ref[0].astype(F32)
        for j in range(1, n):
            g = g + p_ref[j].astype(F32)
        g_ref[...] = g
        d_ref[...], nm_ref[...], nv_ref[...] = _adam_update(g, w_ref[...], m_ref[...], v_ref[...])

    blk = pl.BlockSpec((tr, cols), lambda i: (i, 0))
    return pl.pallas_call(
        body, name=name, grid=(rows // tr,),
        in_specs=[pl.BlockSpec((n, tr, cols), lambda i: (0, i, 0)), blk, blk, blk],
        out_specs=[blk] * 4, out_shape=[_sds((rows, cols), F32)] * 4,
        compiler_params=_cparams(1),
    )(parts, w, m, v)


def _adam_update(g, w, m, v):
    c1 = 1.0 / (1.0 - ADAM_B1 ** ADAM_STEP)
    c2 = 1.0 / (1.0 - ADAM_B2 ** ADAM_STEP)
    nm = ADAM_B1 * m + (1.0 - ADAM_B1) * g
    nv = ADAM_B2 * v + (1.0 - ADAM_B2) * (g * g)
    return -ADAM_LR * ((nm * c1) / (jnp.sqrt(nv * c2) + ADAM_EPS) + ADAM_WD * w), nm, nv


GAIN_ROWS = 16


def _pack_small(name, gain_tiles, taps):
    ng = len(gain_tiles)

    def body(*refs):
        o_ref = refs[-1]
        o_ref[...] = jnp.zeros_like(o_ref)
        for i in range(ng):
            o_ref[i:i + 1, :] = refs[i][0:1, :]
        o_ref[ng:ng + 3, :] = refs[ng][0:3, :]

    return pl.pallas_call(body, name=name, out_shape=_sds((GAIN_ROWS, D), F32))(*gain_tiles, taps)


def _adamw_gains(name, parts, params):
    np_ = len(params)
    shapes = [w.shape for w, _, _ in params]

    def body(p_ref, *refs):
        ins, outs = refs[:3 * np_], refs[3 * np_:]

        def total(lo, rows):
            g = p_ref[0, lo:lo + rows, :]
            for j in range(1, NDEV):
                g = g + p_ref[j, lo:lo + rows, :]
            return g

        lo = 0
        for i, shape in enumerate(shapes):
            g = total(lo, shape[0])
            lo += shape[0]
            w_ref, m_ref, v_ref = ins[3 * i:3 * i + 3]
            g_ref, d_ref, nm_ref, nv_ref = outs[4 * i:4 * i + 4]
            g_ref[...] = g
            d_ref[...], nm_ref[...], nv_ref[...] = _adam_update(g, w_ref[...], m_ref[...], v_ref[...])
        taps_ref = outs[-1]
        taps_ref[...] = jnp.zeros_like(taps_ref)
        taps_ref[0:3, :] = total(lo, 3)

    out_shape = [_sds(shape, F32) for shape in shapes for _ in range(4)] + [_sds((8, D), F32)]
    outs = pl.pallas_call(body, name=name, out_shape=out_shape)(parts, *[a for p in params for a in p])
    return [list(outs[4 * i:4 * i + 4]) for i in range(np_)], outs[-1]


def _exchange(name, arrays, kind):
    n = len(arrays)
    gather = kind == "gather"
    out_shape = [_sds((NDEV,) + a.shape if gather else a.shape, a.dtype) for a in arrays]

    def body(*refs):
        srcs, outs = refs[:n], refs[n:2 * n]
        send_sems, recv_sems, local_sems = refs[2 * n:]
        x, y, c = lax.axis_index("x"), lax.axis_index("y"), lax.axis_index("c")
        me = 4 * x + 2 * y + c
        pending = []
        for t in range(n):
            own = pltpu.make_async_copy(srcs[t] if gather else srcs[t].at[me], outs[t].at[me], local_sems.at[t])
            own.start()
            pending.append(own)
            for rel in range(1, NDEV):
                px = 1 - x if rel & 4 else x
                py = 1 - y if rel & 2 else y
                pc = 1 - c if rel & 1 else c
                peer = 4 * px + 2 * py + pc
                send = pltpu.make_async_remote_copy(
                    src_ref=srcs[t] if gather else srcs[t].at[peer], dst_ref=outs[t].at[me],
                    send_sem=send_sems.at[t, rel - 1], recv_sem=recv_sems.at[t, rel - 1],
                    device_id=(px, py, pc), device_id_type=MESH)
                send.start()
                arrive = pltpu.make_async_remote_copy(
                    src_ref=srcs[t] if gather else srcs[t].at[me], dst_ref=outs[t].at[peer],
                    send_sem=send_sems.at[t, rel - 1], recv_sem=recv_sems.at[t, rel - 1],
                    device_id=(px, py, pc), device_id_type=MESH)
                pending.append((send, arrive))
        for item in pending:
            if isinstance(item, tuple):
                item[0].wait_send()
                item[1].wait_recv()
            else:
                item.wait()

    any_spec = pl.BlockSpec(memory_space=pl.ANY)
    outs = pl.pallas_call(
        body, name=name,
        in_specs=[any_spec] * n, out_specs=[any_spec] * n, out_shape=out_shape,
        scratch_shapes=[pltpu.SemaphoreType.DMA((n, NDEV - 1)), pltpu.SemaphoreType.DMA((n, NDEV - 1)),
                        pltpu.SemaphoreType.DMA((n,))],
    )(*arrays)
    return list(outs)


_HBM_SPEC = pl.BlockSpec(memory_space=pltpu.HBM)
_SEM_SPEC = pl.BlockSpec(memory_space=pltpu.SEMAPHORE)
_DATAFLOW = pltpu.SideEffectType.DATAFLOW_SIDE_EFFECTING


def _peers():
    x, y, c = lax.axis_index("x"), lax.axis_index("y"), lax.axis_index("c")
    out = []
    for rel in range(1, NDEV):
        px = 1 - x if rel & 4 else x
        py = 1 - y if rel & 2 else y
        pc = 1 - c if rel & 1 else c
        out.append((rel - 1, (px, py, pc), 4 * px + 2 * py + pc))
    return 4 * x + 2 * y + c, out


def _hbm(a):
    return pltpu.HBM(a.shape, a.dtype)


def _own_slot(a, me, kind):
    mine = a[None] if kind == "gather" else lax.dynamic_slice_in_dim(a, me, 1, axis=0)
    shape = (NDEV,) + mine.shape[1:]
    return lax.dynamic_update_slice_in_dim(lax.empty(shape, a.dtype), mine, me, axis=0)


def _exchange_start(name, arrays, me, kind):
    n = len(arrays)
    gather = kind == "gather"
    lands = [_own_slot(a, me, kind) for a in arrays]

    def body(*refs):
        src_refs, land_refs = refs[:n], refs[n:2 * n]
        send_sems, recv_sems = refs[2 * n], refs[2 * n + 1]
        token = refs[-1]
        my_block, peers = _peers()
        for t in range(n):
            for slot, dev, block in peers:
                pltpu.make_async_remote_copy(
                    src_ref=src_refs[t] if gather else src_refs[t].at[block], dst_ref=land_refs[t].at[my_block],
                    send_sem=send_sems.at[t * (NDEV - 1) + slot], recv_sem=recv_sems.at[t * (NDEV - 1) + slot],
                    device_id=dev, device_id_type=MESH).start()
        token[...] = jnp.zeros_like(token)

    operands = [pltpu.with_memory_space_constraint(a, pltpu.HBM) for a in list(arrays) + lands]
    outs = pl.pallas_call(
        body, name=name,
        out_shape=(pltpu.SemaphoreType.DMA((n * (NDEV - 1),)), pltpu.SemaphoreType.DMA((n * (NDEV - 1),)),
                   *[_hbm(a) for a in operands], _sds((8, 128), F32)),
        in_specs=[_HBM_SPEC] * (2 * n),
        out_specs=(_SEM_SPEC, _SEM_SPEC, *[_HBM_SPEC] * (2 * n), pl.BlockSpec(memory_space=pltpu.VMEM)),
        input_output_aliases={i: 2 + i for i in range(2 * n)},
        compiler_params=pltpu.CompilerParams(has_side_effects=_DATAFLOW),
    )(*operands)
    return (outs[0], outs[1], list(outs[2:2 + n]), list(outs[2 + n:2 + 2 * n])), outs[-1]


def _exchange_wait(name, started, t, after, kind):
    send_sems, recv_sems, srcs, lands = started
    gather = kind == "gather"

    def body(src_ref, land_ref, send_ref, recv_ref, after_ref, src_out, land_out):
        _, peers = _peers()
        for slot, dev, block in peers:
            copy = pltpu.make_async_remote_copy(
                src_ref=src_ref if gather else src_ref.at[block], dst_ref=land_ref.at[block],
                send_sem=send_ref.at[t * (NDEV - 1) + slot], recv_sem=recv_ref.at[t * (NDEV - 1) + slot],
                device_id=dev, device_id_type=MESH)
            copy.wait_send()
            copy.wait_recv()

    return pl.pallas_call(
        body, name=name, out_shape=(_hbm(srcs[t]), _hbm(lands[t])),
        in_specs=(_HBM_SPEC, _HBM_SPEC, _SEM_SPEC, _SEM_SPEC, pl.BlockSpec(memory_space=pl.ANY)),
        out_specs=(_HBM_SPEC, _HBM_SPEC), input_output_aliases={0: 0, 1: 1},
        compiler_params=pltpu.CompilerParams(has_side_effects=_DATAFLOW),
    )(srcs[t], lands[t], send_sems, recv_sems, after)[1]


def _ffn_fwd(tag, h, g_pre, weight):
    n = _rms_fwd(f"ffn_prenorm_{tag}", h, [g_pre])[0]
    wg = weight(f"gate_up_{tag}", n)
    gu = _fwd_cols_blocked(f"ffn_gate_up_{tag}", n, wg).reshape(2, NFB, S, FB)
    act = _swiglu_fwd(f"ffn_act_{tag}", gu)
    wd4 = weight(f"down_{tag}", act).reshape(NFB, FB, D)
    f = _fwd_kblocked(f"ffn_down_{tag}", act, wd4)
    return (n, gu, act, wg, wd4), f


def _ffn_bwd(tag, dh_out, h_in, f, saved, g_pre, g_post, send):
    n, gu, act, wg, wd4 = saved
    df, (dg_post,) = _rms_bwd(f"ffn_postnorm_bwd_{tag}", f, [(g_post, dh_out)], None, BF16)
    tok = send(f"down_{tag}", _bwd_w_kblocked(f"ffn_down_dw_{tag}", act, df).reshape(NDEV, DFF // NDEV, D))
    da = _bwd_x_kblocked(f"ffn_down_dx_{tag}", df, wd4, after=tok)
    dgu = _swiglu_bwd(f"ffn_act_bwd_{tag}", gu, da).reshape(NDEV, S, FB)
    tok = send(f"gate_up_{tag}", _bwd_w_cols_blocked(f"ffn_gate_up_dw_{tag}", n, dgu))
    dn = _bwd_x_cols_blocked(f"ffn_gate_up_dx_{tag}", dgu, wg, after=tok)
    dh_in, (dg_pre,) = _rms_bwd(f"ffn_prenorm_bwd_{tag}", h_in, [(g_pre, dn)], dh_out, F32)
    return dh_in, dg_pre, dg_post


def kernel(x, positions, mix_norm_pre, mix_norm_post, ffn_norm_pre, ffn_norm_post, ffn_w_gate_up, ffn_w_down, conv_w_in, conv_w, conv_w_out, kv_norm, w_kv, w_q, w_o, loss_target, m_mix_norm_pre, m_mix_norm_post, m_ffn_norm_pre, m_ffn_norm_post, m_ffn_w_gate_up, m_ffn_w_down, m_conv_w_in, m_conv_w, m_conv_w_out, m_kv_norm, m_w_kv, m_w_q, m_w_o, v_mix_norm_pre, v_mix_norm_post, v_ffn_norm_pre, v_ffn_norm_post, v_ffn_w_gate_up, v_ffn_w_down, v_conv_w_in, v_conv_w, v_conv_w_out, v_kv_norm, v_w_kv, v_w_q, v_w_o):
    me = 4 * lax.axis_index("x") + 2 * lax.axis_index("y") + lax.axis_index("c")
    h0 = x.reshape(S, D)
    target = loss_target.reshape(S, D)
    row = lambda a, l: a[l].reshape(1, D)
    g_kv = kv_norm.reshape(1, D)

    cw_shard = jnp.pad(conv_w[0], ((0, 5), (0, 0)))
    names = ["conv_in", "conv_w", "conv_out", "gate_up_0", "down_0", "kv", "q", "o", "gate_up_1", "down_1"]
    shards = [conv_w_in[0], cw_shard, conv_w_out[0], ffn_w_gate_up[0], ffn_w_down[0],
              w_kv, w_q[0], w_o[0], ffn_w_gate_up[1], ffn_w_down[1]]
    shards = [s if n == "conv_w" else s.astype(BF16) for n, s in zip(names, shards)]
    gather, _ = _exchange_start("gather_weights_start", shards, me, "gather")

    def weight(name, after):
        return _exchange_wait(f"gather_wait_{name}", gather, names.index(name), after, "gather")

    sent = {}

    def send(name, grad):
        sent[name], token = _exchange_start(f"scatter_start_{name}", [grad], me, "scatter")
        return token

    n0 = _rms_fwd("mix_prenorm_0", h0, [row(mix_norm_pre, 0)])[0]
    win_g = weight("conv_in", n0)
    cw = weight("conv_w", n0).transpose(1, 0, 2).reshape(8, D)
    z = _fwd_cols("conv_in", n0, win_g)
    pre = _conv_fwd("conv_gate", z, cw)
    wout = weight("conv_out", pre).reshape(D, D)
    y0 = _fwd_rows("conv_out", pre, wout)
    h1 = _resid_rms("mix_postnorm_0", h0, y0, row(mix_norm_post, 0))
    ffn0, f0 = _ffn_fwd("0", h1, row(ffn_norm_pre, 0), weight)
    h2 = _resid_rms("ffn_postnorm_0", h1, f0, row(ffn_norm_post, 0))

    nk, n2 = _rms_fwd("kv_and_mix_prenorm_1", h2, [g_kv, row(mix_norm_pre, 1)])
    wkv_g = weight("kv", nk)
    wkv = wkv_g.transpose(1, 0, 2).reshape(D, 2 * QW)
    half = HEAD_DIM // 2
    inv_freq = ROPE_THETA ** (-jnp.arange(half, dtype=F32) / half)
    tables = _rope_tables("rope_tables", positions.reshape(S, 1), jnp.tile(inv_freq, 4).reshape(1, 128))
    qc, kc, vc, o_c, lse_c = [], [], [], [], []
    for g, d in enumerate(DILATIONS):
        kc.append(_proj_classes(f"k_proj_{g}", nk, wkv, g, d, tables, 1.0))
        vc.append(_proj_classes(f"v_proj_{g}", nk, wkv, len(DILATIONS) + g, d, None, None))
    wq_g = weight("q", vc[-1])
    wq = wq_g.transpose(1, 0, 2).reshape(D, QW)
    for g, d in enumerate(DILATIONS):
        qc.append(_proj_classes(f"q_proj_{g}", n2, wq, g, d, tables, HEAD_DIM ** -0.5))
        o_g, lse_g = _attn_fwd(f"attn_fwd_{g}", qc[g], kc[g], vc[g], d)
        o_c.append(o_g)
        lse_c.append(lse_g)
    o_mix = _mix_fwd("attn_mix", o_c, lse_c)
    wo = weight("o", o_mix).reshape(D, D)
    y1 = _fwd_rows("attn_out", o_mix, wo)
    h3 = _resid_rms("mix_postnorm_1", h2, y1, row(mix_norm_post, 1))
    ffn1, f1 = _ffn_fwd("1", h3, row(ffn_norm_pre, 1), weight)
    h4 = _resid_rms("ffn_postnorm_1", h3, f1, row(ffn_norm_post, 1))

    dh4, sq = _loss_grad("loss", h4, target)
    loss = lax.psum(jnp.sum(sq) * (0.5 / D), ("x", "y", "c"))

    dh3, dg_fpre1, dg_fpost1 = _ffn_bwd(
        "1", dh4, h3, f1, ffn1, row(ffn_norm_pre, 1), row(ffn_norm_post, 1), send)
    dy1, (dg_mpost1,) = _rms_bwd("mix_postnorm_bwd_1", y1, [(row(mix_norm_post, 1), dh3)], None, BF16)
    tok = send("o", _bwd_w_rows("attn_out_dw", o_mix, dy1).reshape(NDEV, D // NDEV, D))
    do = _bwd_x_rows("attn_out_dx", dy1, wo, F32, after=tok)
    lane = jnp.arange(128)
    ones_blockdiag = (lane[:, None] // HEAD_DIM == lane[None, :] // HEAD_DIM).astype(BF16)
    mixed = _mix_bwd("attn_mix_bwd", do, o_c, lse_c, ones_blockdiag)
    branch_grads = [_attn_bwd(f"attn_bwd_{g}", qc[g], kc[g], vc[g], mixed[g], lse_c[g], mixed[3 + g], d)
                    for g, d in enumerate(DILATIONS)]
    dq_raw, dkv = _attn_bwd_post("attn_bwd_post", branch_grads, *tables)
    tok = send("kv", _bwd_w_cols("kv_proj_dw", nk, dkv, 2 * QW // NDEV))
    dnk = _bwd_x_plain("kv_proj_dx", dkv, wkv, 2, after=tok)
    tok = send("q", _bwd_w_cols("q_proj_dw", n2, dq_raw, QW // NDEV))
    dn2 = _bwd_x_plain("q_proj_dx", dq_raw, wq, 1, after=tok)
    dh2, (dg_kv, dg_mpre1) = _rms_bwd("kv_and_mix_prenorm_bwd_1", h2,
                                      [(g_kv, dnk), (row(mix_norm_pre, 1), dn2)], dh3, F32)

    dh1, dg_fpre0, dg_fpost0 = _ffn_bwd(
        "0", dh2, h1, f0, ffn0, row(ffn_norm_pre, 0), row(ffn_norm_post, 0), send)
    dy0, (dg_mpost0,) = _rms_bwd("mix_postnorm_bwd_0", y0, [(row(mix_norm_post, 0), dh1)], None, BF16)
    tok = send("conv_out", _bwd_w_rows("conv_out_dw", pre, dy0).reshape(NDEV, D // NDEV, D))
    dpre = _bwd_x_rows("conv_out_dx", dy0, wout, BF16, after=tok)
    dz, dcw = _conv_bwd("conv_gate_bwd", z, dpre, cw)
    tok = send("conv_in", _bwd_w_cols("conv_in_dw", n0, dz, 3 * D // NDEV))
    dn0 = _bwd_x_plain("conv_in_dx", dz, win_g.transpose(1, 0, 2).reshape(D, 3 * D), 1, after=tok)
    dh0, (dg_mpre0,) = _rms_bwd("mix_prenorm_bwd_0", h0, [(row(mix_norm_pre, 0), dn0)], dh1, F32)

    small = _pack_small("pack_small_grads", [dg_mpre0, dg_mpre1, dg_mpost0, dg_mpost1, dg_fpre0, dg_fpre1,
                                             dg_fpost0, dg_fpost1, dg_kv], dcw)
    small_all = _exchange("gather_small_grads", [small], "gather")[0]

    done = [small_all]

    def upd(tag, w, m, v):
        parts = _exchange_wait(f"scatter_wait_{tag}", sent[tag], 0, done[-1], "scatter")
        shape = w.shape
        flat = lambda a: a.reshape(parts.shape[1:])
        res = _adamw(f"adamw_{tag}", parts, flat(w), flat(m), flat(v))
        done.append(res[0])
        return [r.reshape(shape) for r in res]

    def upd_layer(tag, l, w, m, v):
        return upd(f"{tag}_{l}", w[l], m[l], v[l])

    def stack(per_layer):
        return [jnp.stack([per_layer[0][i], per_layer[1][i]]) for i in range(4)]

    vec = lambda a: a.reshape(1, D)
    gain_res, taps = _adamw_gains("adamw_gains", small_all, [
        (mix_norm_pre, m_mix_norm_pre, v_mix_norm_pre), (mix_norm_post, m_mix_norm_post, v_mix_norm_post),
        (ffn_norm_pre, m_ffn_norm_pre, v_ffn_norm_pre), (ffn_norm_post, m_ffn_norm_post, v_ffn_norm_post),
        (vec(kv_norm), vec(m_kv_norm), vec(v_kv_norm))])
    dcw_mine = lax.dynamic_slice(taps, (0, me * 128), (8, 128))
    pad8 = lambda a, fill: jnp.pad(a[0], ((0, 5), (0, 0)), constant_values=fill)
    cw_res = [r[0:3].reshape(1, 3, 128) for r in
              _adamw("adamw_conv_w", dcw_mine.reshape(1, 8, 128), cw_shard, pad8(m_conv_w, 0.0), pad8(v_conv_w, 1.0))]

    res = {
        "mix_norm_pre": gain_res[0],
        "mix_norm_post": gain_res[1],
        "ffn_norm_pre": gain_res[2],
        "ffn_norm_post": gain_res[3],
        "kv_norm": [r.reshape(D) for r in gain_res[4]],
        "conv_w": cw_res,
    }
    down, gate_up = {}, {}
    down[1] = upd_layer("down", 1, ffn_w_down, m_ffn_w_down, v_ffn_w_down)
    gate_up[1] = upd_layer("gate_up", 1, ffn_w_gate_up, m_ffn_w_gate_up, v_ffn_w_gate_up)
    res["w_o"] = upd("o", w_o, m_w_o, v_w_o)
    res["w_q"] = upd("q", w_q, m_w_q, v_w_q)
    res["w_kv"] = upd("kv", w_kv, m_w_kv, v_w_kv)
    down[0] = upd_layer("down", 0, ffn_w_down, m_ffn_w_down, v_ffn_w_down)
    gate_up[0] = upd_layer("gate_up", 0, ffn_w_gate_up, m_ffn_w_gate_up, v_ffn_w_gate_up)
    res["ffn_w_down"] = stack(down)
    res["ffn_w_gate_up"] = stack(gate_up)
    res["conv_w_out"] = upd("conv_out", conv_w_out, m_conv_w_out, v_conv_w_out)
    res["conv_w_in"] = upd("conv_in", conv_w_in, m_conv_w_in, v_conv_w_in)
    order = ["mix_norm_pre", "mix_norm_post", "ffn_norm_pre", "ffn_norm_post", "ffn_w_gate_up", "ffn_w_down",
             "conv_w_in", "conv_w", "conv_w_out", "kv_norm", "w_kv", "w_q", "w_o"]
    out = [loss, dh0.reshape(1, S, D)]
    for i in range(4):
        out += [res[name][i] for name in order]
    return tuple(out)
```

```python
import jax
import jax.numpy as jnp
from jax import lax
from jax.experimental import pallas as pl
from jax.experimental.pallas import tpu as pltpu

F32 = jnp.float32
BF16 = jnp.bfloat16

S = 4096
D = 1024
NDEV = 8
HEAD_DIM = 64
QW = 3072
DFF = 2816
FB = 704
NFB = 4
BRANCHES = ((128, 1), (512, 4), (2048, 16))
BAND = 128
ROPE_THETA = 10000.0
RMS_EPS = 1e-6
NEG_INF = -1e30
ADAM_LR, ADAM_B1, ADAM_B2, ADAM_EPS, ADAM_WD, ADAM_STEP = 0.001, 0.9, 0.999, 1e-08, 0.01, 10

VMEM_LIMIT_BYTES = 52 * 1024 * 1024
ROW_TILE = 512
MESH = pl.DeviceIdType.MESH


def _cparams(ngrid):
    return pltpu.CompilerParams(dimension_semantics=("arbitrary",) * ngrid,
                                vmem_limit_bytes=VMEM_LIMIT_BYTES)


def _sds(shape, dtype):
    return jax.ShapeDtypeStruct(tuple(shape), dtype)


_DIMS = {"nn": (((1,), (0,)), ((), ())),
         "nt": (((1,), (1,)), ((), ())),
         "tn": (((0,), (0,)), ((), ()))}


def _matmul(name, a, b, *, mode, grid, a_blk, a_map, b_blk, b_map, o_shape, o_blk, o_map, out_dtype, after=None,
            out_groups=1):
    nk = grid[2]
    dims = _DIMS[mode]
    acc_shape = tuple(s for s in o_blk if s is not None)
    if out_groups > 1:
        acc_shape = (acc_shape[1], out_groups * acc_shape[2])
    extra = [] if after is None else [after]

    def store(o_ref, val):
        if out_groups == 1:
            o_ref[...] = val.astype(o_ref.dtype)
        else:
            n = o_ref.shape[-1]
            for grp in range(out_groups):
                o_ref[grp] = val[:, grp * n:(grp + 1) * n].astype(o_ref.dtype)

    def body(a_ref, b_ref, *rest):
        o_ref, scratch = rest[len(extra)], rest[len(extra) + 1:]
        part = lax.dot_general(a_ref[...], b_ref[...], dims, preferred_element_type=F32)
        if nk == 1:
            store(o_ref, part)
            return
        acc_ref = scratch[0]
        k = pl.program_id(2)

        @pl.when(k == 0)
        def _():
            acc_ref[...] = part

        @pl.when(k > 0)
        def _():
            acc_ref[...] += part

        @pl.when(k == nk - 1)
        def _():
            store(o_ref, acc_ref[...])

    return pl.pallas_call(
        body, name=name, grid=grid,
        in_specs=[pl.BlockSpec(a_blk, a_map), pl.BlockSpec(b_blk, b_map)] + [pl.BlockSpec(memory_space=pl.ANY)] * len(extra),
        out_specs=pl.BlockSpec(o_blk, o_map),
        out_shape=_sds(o_shape, out_dtype),
        scratch_shapes=[] if nk == 1 else [pltpu.VMEM(acc_shape, F32)],
        compiler_params=_cparams(3),
    )(a, b, *extra)


TM = 1024
TK = 1024


def _fwd_cols(name, a, wg, out_dtype=BF16):
    _, kdim, n = wg.shape
    return _matmul(name, a, wg, mode="nn", grid=(S // TM, NDEV, 1),
                   a_blk=(TM, kdim), a_map=lambda i, j, k: (i, 0),
                   b_blk=(None, kdim, n), b_map=lambda i, j, k: (j, 0, 0),
                   o_shape=(S, NDEV * n), o_blk=(TM, n), o_map=lambda i, j, k: (i, j), out_dtype=out_dtype)


def _fwd_rows(name, a, w, out_dtype=F32):
    kdim, n = w.shape
    tn = 512
    return _matmul(name, a, w, mode="nn", grid=(S // TM, n // tn, 1),
                   a_blk=(TM, kdim), a_map=lambda i, j, k: (i, 0),
                   b_blk=(kdim, tn), b_map=lambda i, j, k: (0, j),
                   o_shape=(S, n), o_blk=(TM, tn), o_map=lambda i, j, k: (i, j), out_dtype=out_dtype)


def _fwd_kblocked(name, a4, w4):
    nb, _, kb = a4.shape
    n = w4.shape[2]
    return _matmul(name, a4, w4, mode="nn", grid=(S // TM, 1, nb),
                   a_blk=(None, TM, kb), a_map=lambda i, j, k: (k, i, 0),
                   b_blk=(None, kb, n), b_map=lambda i, j, k: (k, 0, 0),
                   o_shape=(S, n), o_blk=(TM, n), o_map=lambda i, j, k: (i, 0), out_dtype=F32)


def _bwd_x_cols_blocked(name, dy8, wg, after):
    _, kdim, n = wg.shape
    nk = NDEV // 2

    def body(a_ref, b_ref, after_ref, o_ref, acc_ref):
        k = pl.program_id(1)
        part = _dot_nt(a_ref[0], b_ref[0]) + _dot_nt(a_ref[1], b_ref[1])

        @pl.when(k == 0)
        def _():
            acc_ref[...] = part

        @pl.when(k > 0)
        def _():
            acc_ref[...] += part

        @pl.when(k == nk - 1)
        def _():
            o_ref[...] = acc_ref[...]

    return pl.pallas_call(
        body, name=name, grid=(S // TM, nk),
        in_specs=[pl.BlockSpec((2, None, TM, n), lambda i, k: (0, k, i, 0)),
                  pl.BlockSpec((2, None, kdim, n), lambda i, k: (0, k, 0, 0)),
                  pl.BlockSpec(memory_space=pl.ANY)],
        out_specs=pl.BlockSpec((TM, kdim), lambda i, k: (i, 0)), out_shape=_sds((S, kdim), F32),
        scratch_shapes=[pltpu.VMEM((TM, kdim), F32)],
        compiler_params=_cparams(2),
    )(dy8.reshape(2, nk, S, n), wg.reshape(2, nk, kdim, n), after)


def _bwd_x_rows(name, dy, w, out_dtype, after=None):
    kdim, n = w.shape
    tkk = 512
    return _matmul(name, dy, w, mode="nt", grid=(S // TM, kdim // tkk, 1),
                   a_blk=(TM, n), a_map=lambda i, j, k: (i, 0),
                   b_blk=(tkk, n), b_map=lambda i, j, k: (j, 0),
                   o_shape=(S, kdim), o_blk=(TM, tkk), o_map=lambda i, j, k: (i, j), out_dtype=out_dtype, after=after)


DW_COLS = 768


def _bwd_w_cols(name, a, dy, n):
    kdim = a.shape[1]
    groups = DW_COLS // n
    return _matmul(name, a, dy, mode="tn", grid=(1, NDEV // groups, S // TK),
                   a_blk=(TK, kdim), a_map=lambda i, j, k: (k, 0),
                   b_blk=(TK, DW_COLS), b_map=lambda i, j, k: (k, j),
                   o_shape=(NDEV, kdim, n), o_blk=(groups, kdim, n) if groups > 1 else (None, kdim, n),
                   o_map=lambda i, j, k: (j, 0, 0), out_dtype=BF16, out_groups=groups)


def _bwd_x_plain(name, dy, w, nk, after=None):
    kdim, n = w.shape
    return _matmul(name, dy, w, mode="nt", grid=(S // TM, 1, nk),
                   a_blk=(TM, n // nk), a_map=lambda i, j, k: (i, k),
                   b_blk=(kdim, n // nk), b_map=lambda i, j, k: (0, k),
                   o_shape=(S, kdim), o_blk=(TM, kdim), o_map=lambda i, j, k: (i, 0), out_dtype=F32, after=after)


def _bwd_w_cols_blocked(name, a, dy8):
    kdim = a.shape[1]
    n = dy8.shape[2]
    return _matmul(name, a, dy8, mode="tn", grid=(1, NDEV, S // TK),
                   a_blk=(TK, kdim), a_map=lambda i, j, k: (k, 0),
                   b_blk=(None, TK, n), b_map=lambda i, j, k: (j, k, 0),
                   o_shape=(NDEV, kdim, n), o_blk=(None, kdim, n), o_map=lambda i, j, k: (j, 0, 0), out_dtype=BF16)


def _bwd_w_rows(name, a, dy):
    kdim = a.shape[1]
    n = dy.shape[1]
    tmm = 512
    return _matmul(name, a, dy, mode="tn", grid=(kdim // tmm, 1, S // TK),
                   a_blk=(TK, tmm), a_map=lambda i, j, k: (k, i),
                   b_blk=(TK, n), b_map=lambda i, j, k: (k, 0),
                   o_shape=(kdim, n), o_blk=(tmm, n), o_map=lambda i, j, k: (i, 0), out_dtype=BF16)


def _bwd_w_kblocked(name, a4, dy):
    nb, _, kb = a4.shape
    n = dy.shape[1]
    return _matmul(name, a4, dy, mode="tn", grid=(nb, 1, S // TK),
                   a_blk=(None, TK, kb), a_map=lambda i, j, k: (i, k, 0),
                   b_blk=(TK, n), b_map=lambda i, j, k: (k, 0),
                   o_shape=(nb, kb, n), o_blk=(None, kb, n), o_map=lambda i, j, k: (i, 0, 0), out_dtype=BF16)


def _rstd(x):
    return lax.rsqrt(jnp.mean(x * x, axis=-1, keepdims=True) + RMS_EPS)


def _row_spec(tm=ROW_TILE, width=D):
    return pl.BlockSpec((tm, width), lambda i: (i, 0))


def _vec_spec(rows=1, width=D):
    return pl.BlockSpec((rows, width), lambda i: (0, 0))


def _rms_fwd(name, x, gains):
    n = len(gains)

    def body(x_ref, *refs):
        x_val = x_ref[...]
        xh = x_val * _rstd(x_val)
        for g_ref, o_ref in zip(refs[:n], refs[n:]):
            o_ref[...] = (xh * g_ref[...]).astype(o_ref.dtype)

    outs = pl.pallas_call(
        body, name=name, grid=(S // ROW_TILE,),
        in_specs=[_row_spec()] + [_vec_spec()] * n,
        out_specs=[_row_spec()] * n,
        out_shape=[_sds((S, D), BF16)] * n,
        compiler_params=_cparams(1),
    )(x, *gains)
    return list(outs)


def _resid_rms(name, h, y, g, next_gains):
    n = len(next_gains)

    def body(h_ref, y_ref, g_ref, *refs):
        y_val = y_ref[...]
        h_new = h_ref[...] + (y_val * _rstd(y_val)) * g_ref[...]
        refs[n][...] = h_new
        hh = h_new * _rstd(h_new)
        for g2_ref, o_ref in zip(refs[:n], refs[n + 1:]):
            o_ref[...] = (hh * g2_ref[...]).astype(o_ref.dtype)

    outs = pl.pallas_call(
        body, name=name, grid=(S // ROW_TILE,),
        in_specs=[_row_spec(), _row_spec(), _vec_spec()] + [_vec_spec()] * n,
        out_specs=[_row_spec()] * (n + 1), out_shape=[_sds((S, D), F32)] + [_sds((S, D), BF16)] * n,
        compiler_params=_cparams(1),
    )(h, y, g, *next_gains)
    return outs[0], list(outs[1:])


def _resid_rms_loss(name, h, y, g, target):
    def body(h_ref, y_ref, g_ref, t_ref, dh_ref, part_ref):
        y_val = y_ref[...]
        e = h_ref[...] + (y_val * _rstd(y_val)) * g_ref[...] - t_ref[...]
        dh_ref[...] = e * (1.0 / D)
        part = jnp.sum(e * e, axis=0, keepdims=True)
        step = pl.program_id(0)

        @pl.when(step == 0)
        def _():
            part_ref[...] = part

        @pl.when(step > 0)
        def _():
            part_ref[...] += part

    return pl.pallas_call(
        body, name=name, grid=(S // ROW_TILE,),
        in_specs=[_row_spec(), _row_spec(), _vec_spec(), _row_spec()],
        out_specs=[_row_spec(), _vec_spec()],
        out_shape=[_sds((S, D), F32), _sds((1, D), F32)],
        compiler_params=_cparams(1),
    )(h, y, g, target)


def _rms_bwd(name, x, pairs, dres, out_dtype):
    n = len(pairs)
    has_res = dres is not None

    def body(x_ref, *refs):
        g_refs = refs[0:2 * n:2]
        dn_refs = refs[1:2 * n:2]
        pos = 2 * n
        res_ref = refs[pos] if has_res else None
        pos += int(has_res)
        dx_ref = refs[pos]
        dg_refs = refs[pos + 1:]
        step = pl.program_id(0)
        x_val = x_ref[...]
        r = _rstd(x_val)
        xh = x_val * r
        acc = res_ref[...] if has_res else jnp.zeros_like(x_val)
        for g_ref, dn_ref, dg_ref in zip(g_refs, dn_refs, dg_refs):
            dn = dn_ref[...].astype(F32)
            dxh = dn * g_ref[...]
            acc = acc + r * (dxh - xh * jnp.mean(dxh * xh, axis=-1, keepdims=True))
            part = jnp.sum(dn * xh, axis=0, keepdims=True)

            @pl.when(step == 0)
            def _():
                dg_ref[...] = jnp.zeros_like(dg_ref)

            dg_ref[0:1, :] += part

        dx_ref[...] = acc.astype(dx_ref.dtype)

    operands = [x]
    in_specs = [_row_spec()]
    for g, dn in pairs:
        operands += [g, dn]
        in_specs += [_vec_spec(), _row_spec()]
    if has_res:
        operands.append(dres)
        in_specs.append(_row_spec())
    outs = pl.pallas_call(
        body, name=name, grid=(S // ROW_TILE,),
        in_specs=in_specs,
        out_specs=[_row_spec()] + [_vec_spec(8)] * n,
        out_shape=[_sds((S, D), out_dtype)] + [_sds((8, D), F32)] * n,
        compiler_params=_cparams(1),
    )(*operands)
    return outs[0], list(outs[1:])


def _shift_down(u, prev8, k):
    r = pltpu.roll(u, k, 0)
    p = pltpu.roll(prev8, k, 0)
    row = lax.broadcasted_iota(jnp.int32, prev8.shape, 0)
    top = jnp.where(row < k, p, r[0:8])
    return jnp.concatenate([top, r[8:]], axis=0)


def _shift_up(u, next8, k):
    tm = u.shape[0]
    r = pltpu.roll(u, tm - k, 0)
    p = pltpu.roll(next8, 8 - k, 0)
    row = lax.broadcasted_iota(jnp.int32, next8.shape, 0)
    bot = jnp.where(row >= 8 - k, p, r[tm - 8:tm])
    return jnp.concatenate([r[:tm - 8], bot], axis=0)


CONV_TILE = 512


def _halo_prev(col):
    return pl.BlockSpec((8, D), lambda i: (jnp.maximum(i * (CONV_TILE // 8) - 1, 0), col))


def _halo_next(col):
    last = S // 8 - 1
    return pl.BlockSpec((8, D), lambda i: (jnp.minimum((i + 1) * (CONV_TILE // 8), last), col))


def _conv_fwd(name, z, cw):
    def body(b_ref, c_ref, h_ref, cp_ref, hp_ref, cw_ref, o_ref):
        i = pl.program_id(0)
        u = c_ref[...].astype(F32) * h_ref[...].astype(F32)
        up = cp_ref[...].astype(F32) * hp_ref[...].astype(F32)
        up = jnp.where(i > 0, up, 0.0)
        cv = cw_ref[0:1, :] * _shift_down(u, up, 2) + cw_ref[1:2, :] * _shift_down(u, up, 1) + cw_ref[2:3, :] * u
        o_ref[...] = (b_ref[...].astype(F32) * cv).astype(o_ref.dtype)

    col = lambda c: pl.BlockSpec((CONV_TILE, D), lambda i: (i, c))
    return pl.pallas_call(
        body, name=name, grid=(S // CONV_TILE,),
        in_specs=[col(0), col(1), col(2), _halo_prev(1), _halo_prev(2), _vec_spec(8)],
        out_specs=_row_spec(CONV_TILE), out_shape=_sds((S, D), BF16),
        compiler_params=_cparams(1),
    )(z, z, z, z, z, cw)


def _conv_bwd(name, z, dpre, cw):
    nsteps = S // CONV_TILE

    def body(b_ref, c_ref, h_ref, cp_ref, hp_ref, dp_ref, dpn_ref, bn_ref, cw_ref, dz_ref, dcw_ref):
        i = pl.program_id(0)
        b = b_ref[...].astype(F32)
        c = c_ref[...].astype(F32)
        h = h_ref[...].astype(F32)
        dp = dp_ref[...].astype(F32)
        u = c * h
        up = jnp.where(i > 0, cp_ref[...].astype(F32) * hp_ref[...].astype(F32), 0.0)
        s1 = _shift_down(u, up, 1)
        s2 = _shift_down(u, up, 2)
        w0, w1, w2 = cw_ref[0:1, :], cw_ref[1:2, :], cw_ref[2:3, :]
        cv = w0 * s2 + w1 * s1 + w2 * u
        dcv = dp * b
        dcvn = jnp.where(i < nsteps - 1, dpn_ref[...].astype(F32) * bn_ref[...].astype(F32), 0.0)
        du = w2 * dcv + w1 * _shift_up(dcv, dcvn, 1) + w0 * _shift_up(dcv, dcvn, 2)
        dz_ref[:, 0:D] = (dp * cv).astype(dz_ref.dtype)
        dz_ref[:, D:2 * D] = (du * h).astype(dz_ref.dtype)
        dz_ref[:, 2 * D:3 * D] = (du * c).astype(dz_ref.dtype)

        @pl.when(i == 0)
        def _():
            dcw_ref[...] = jnp.zeros_like(dcw_ref)

        dcw_ref[0:1, :] += jnp.sum(dcv * s2, axis=0, keepdims=True)
        dcw_ref[1:2, :] += jnp.sum(dcv * s1, axis=0, keepdims=True)
        dcw_ref[2:3, :] += jnp.sum(dcv * u, axis=0, keepdims=True)

    col = lambda c: pl.BlockSpec((CONV_TILE, D), lambda i: (i, c))
    return pl.pallas_call(
        body, name=name, grid=(nsteps,),
        in_specs=[col(0), col(1), col(2), _halo_prev(1), _halo_prev(2),
                  _row_spec(CONV_TILE), _halo_next(0), _halo_next(0), _vec_spec(8)],
        out_specs=[pl.BlockSpec((CONV_TILE, 3 * D), lambda i: (i, 0)), _vec_spec(8)],
        out_shape=[_sds((S, 3 * D), BF16), _sds((8, D), F32)],
        compiler_params=_cparams(1),
    )(z, z, z, z, z, dpre, dpre, z, cw)


_GU_BLOCK = pl.BlockSpec((2, None, TM, FB), lambda i, j: (0, j, i, 0))


def _gate_up_act(name, a, wg):
    kdim = a.shape[1]

    def body(a_ref, wgate_ref, wup_ref, gu_ref, act_ref):
        x = a_ref[...]
        g = _dot_nn(x, wgate_ref[...])
        u = _dot_nn(x, wup_ref[...])
        gu_ref[0] = g.astype(gu_ref.dtype)
        gu_ref[1] = u.astype(gu_ref.dtype)
        act_ref[...] = (g * jax.nn.sigmoid(g) * u).astype(act_ref.dtype)

    return pl.pallas_call(
        body, name=name, grid=(S // TM, NFB),
        in_specs=[pl.BlockSpec((TM, kdim), lambda i, j: (i, 0)),
                  pl.BlockSpec((None, kdim, FB), lambda i, j: (j, 0, 0)),
                  pl.BlockSpec((None, kdim, FB), lambda i, j: (j + NFB, 0, 0))],
        out_specs=[_GU_BLOCK, pl.BlockSpec((None, TM, FB), lambda i, j: (j, i, 0))],
        out_shape=[_sds((2, NFB, S, FB), BF16), _sds((NFB, S, FB), BF16)],
        compiler_params=_cparams(2),
    )(a, wg, wg)


def _down_dx_act_bwd(name, df, w4, gu, after):
    _, kb, n = w4.shape

    def body(df_ref, w_ref, gu_ref, after_ref, o_ref):
        d = _dot_nt(df_ref[...], w_ref[...])
        g = gu_ref[0].astype(F32)
        u = gu_ref[1].astype(F32)
        sg = jax.nn.sigmoid(g)
        o_ref[0] = (d * u * sg * (1.0 + g * (1.0 - sg))).astype(o_ref.dtype)
        o_ref[1] = (d * g * sg).astype(o_ref.dtype)

    return pl.pallas_call(
        body, name=name, grid=(S // TM, NFB),
        in_specs=[pl.BlockSpec((TM, n), lambda i, j: (i, 0)), pl.BlockSpec((None, kb, n), lambda i, j: (j, 0, 0)),
                  _GU_BLOCK, pl.BlockSpec(memory_space=pl.ANY)],
        out_specs=_GU_BLOCK, out_shape=_sds((2, NFB, S, FB), BF16),
        compiler_params=_cparams(2),
    )(df, w4, gu, after)


def _rope_tables(name, pos_col, inv_freq_row):
    def body(pos_ref, f_ref, cos_ref, sin_ref):
        ang = pos_ref[...].astype(F32) * f_ref[...]
        lane = lax.broadcasted_iota(jnp.int32, ang.shape, 1)
        s = jnp.sin(ang)
        cos_ref[...] = jnp.cos(ang)
        sin_ref[...] = jnp.where((lane % HEAD_DIM) < HEAD_DIM // 2, -s, s)

    tab = pl.BlockSpec((ROW_TILE, 128), lambda i: (i, 0))
    return pl.pallas_call(
        body, name=name, grid=(S // ROW_TILE,),
        in_specs=[pl.BlockSpec((ROW_TILE, 1), lambda i: (i, 0)), _vec_spec(1, 128)],
        out_specs=[tab, tab], out_shape=[_sds((S, 128), F32)] * 2,
        compiler_params=_cparams(1),
    )(pos_col, inv_freq_row)


def _swap_halves(t):
    lane = lax.broadcasted_iota(jnp.int32, t.shape, 1)
    first = (lane % HEAD_DIM) < HEAD_DIM // 2
    return jnp.where(first, pltpu.roll(t, 128 - HEAD_DIM // 2, 1), pltpu.roll(t, HEAD_DIM // 2, 1))


NCHUNK = D // 128


def _chunk(c, base=0):
    return slice(base + c * 128, base + (c + 1) * 128)


def _class_rows(r, d, tm):
    return pl.ds(r, tm // d, stride=d) if d > 1 else slice(None)


def _class_block(d, tm):
    return pl.BlockSpec((tm // d, d * D), lambda i: (i, 0))


def _tokens_from_classes(blk_ref, tmp_ref, d, tm):
    for r in range(d):
        for c in range(NCHUNK):
            tmp_ref[c, _class_rows(r, d, tm), :] = blk_ref[:, _chunk(c, r * D)]


def _classes_from_tokens(tmp_ref, blk_ref, d, tm):
    for r in range(d):
        for c in range(NCHUNK):
            blk_ref[:, _chunk(c, r * D)] = tmp_ref[c, _class_rows(r, d, tm), :].astype(blk_ref.dtype)


def _proj_classes(name, a, w, col, d, tables, scale):
    kdim = a.shape[1]
    rope = tables is not None

    def body(a_ref, w_ref, *refs):
        if rope:
            cos_ref, sin_ref, o_ref, tmp_ref = refs
        else:
            o_ref, tmp_ref = refs
        acc = _dot_nn(a_ref[...], w_ref[...])
        for c in range(NCHUNK):
            tmp_ref[c] = acc[:, _chunk(c)]
        for r in range(d):
            rows = _class_rows(r, d, TM)
            if rope:
                cs = cos_ref[rows, :]
                sn = sin_ref[rows, :]
            for c in range(NCHUNK):
                x = tmp_ref[c, rows, :]
                if rope:
                    x = (x * cs + _swap_halves(x) * sn) * scale
                o_ref[:, _chunk(c, r * D)] = x.astype(o_ref.dtype)

    tab = pl.BlockSpec((TM, 128), lambda i: (i, 0))
    return pl.pallas_call(
        body, name=name, grid=(S // TM,),
        in_specs=[pl.BlockSpec((TM, kdim), lambda i: (i, 0)), pl.BlockSpec((kdim, D), lambda i: (0, col))]
                 + ([tab, tab] if rope else []),
        out_specs=_class_block(d, TM), out_shape=_sds((S // d, d * D), BF16),
        scratch_shapes=[pltpu.VMEM((NCHUNK, TM, 128), F32)],
        compiler_params=_cparams(1),
    )(a, w, *(tables if rope else ()))


ATTN_CHAINS = 4


def _attn_units(d):
    nblk = S // d // BAND
    return max(1, 2 * ATTN_CHAINS // nblk)


def _class_spec(d):
    return pl.BlockSpec((S // d, 128 * _attn_units(d)), lambda cb: (0, cb))


def _dot_nt(a, b):
    return lax.dot_general(a, b, _DIMS["nt"], preferred_element_type=F32)


def _dot_tn(a, b):
    return lax.dot_general(a, b, _DIMS["tn"], preferred_element_type=F32)


def _dot_nn(a, b):
    return lax.dot_general(a, b, _DIMS["nn"], preferred_element_type=F32)


def _band_mask(nkeys):
    qi = lax.broadcasted_iota(jnp.int32, (2 * BAND, nkeys), 0) % BAND
    kj = lax.broadcasted_iota(jnp.int32, (2 * BAND, nkeys), 1)
    if nkeys == BAND:
        return kj <= qi
    dist = qi + BAND - kj
    return (dist >= 0) & (dist <= BAND)


def _stack_heads(x):
    row = lax.broadcasted_iota(jnp.int32, (2 * BAND, 128), 0)
    lane = lax.broadcasted_iota(jnp.int32, (2 * BAND, 128), 1)
    keep = (row < BAND) == (lane < HEAD_DIM)
    return jnp.where(keep, jnp.concatenate([x, x], axis=0), jnp.zeros((), x.dtype))


def _unstack(x2):
    first_head = lax.broadcasted_iota(jnp.int32, (BAND, 128), 1) < HEAD_DIM
    return jnp.where(first_head, x2[:BAND], x2[BAND:])


def _for_later_blocks(nblk, units, fn):
    all_lanes = [slice(u * 128, (u + 1) * 128) for u in range(units)]
    unroll = max(1, ATTN_CHAINS // units)
    trips = (nblk - 1) // unroll
    if trips > 1:
        def step(i, carry):
            for j in range(unroll):
                for lanes in all_lanes:
                    fn(pl.multiple_of((1 + i * unroll + j) * BAND, BAND), lanes)
            return carry

        lax.fori_loop(0, trips, step, 0)
    else:
        trips = 0
    for sb in range(1 + trips * unroll, nblk):
        for lanes in all_lanes:
            fn(sb * BAND, lanes)


def _attn_fwd(name, q, k, v, d):
    nblk = S // d // BAND
    units = _attn_units(d)

    def body(q_ref, k_ref, v_ref, o_ref, lse_ref):
        def block(r0, k0, nkeys, lanes):
            q2 = _stack_heads(q_ref[pl.ds(r0, BAND), lanes])
            s = jnp.where(_band_mask(nkeys), _dot_nt(q2, k_ref[pl.ds(k0, nkeys), lanes]), NEG_INF)
            m = jnp.max(s, axis=-1, keepdims=True)
            p = jnp.exp(s - m)
            l = jnp.sum(p, axis=-1, keepdims=True)
            o2 = _dot_nn(p.astype(BF16), v_ref[pl.ds(k0, nkeys), lanes]) / l
            lse2 = jnp.broadcast_to(m + jnp.log(l), (2 * BAND, 128))
            o_ref[pl.ds(r0, BAND), lanes] = _unstack(o2)
            lse_ref[pl.ds(r0, BAND), lanes] = _unstack(lse2)

        for u in range(units):
            block(0, 0, BAND, slice(u * 128, (u + 1) * 128))

        _for_later_blocks(nblk, units, lambda r0, lanes: block(r0, r0 - BAND, 2 * BAND, lanes))

    spec = _class_spec(d)
    return pl.pallas_call(
        body, name=name, grid=(8 * d // units,),
        in_specs=[spec] * 3, out_specs=[spec] * 2,
        out_shape=[_sds((S // d, d * D), F32)] * 2,
        compiler_params=_cparams(1),
    )(q, k, v)


def _attn_bwd(name, q, k, v, do, lse, dd, d):
    nblk = S // d // BAND
    units = _attn_units(d)

    def body(q_ref, k_ref, v_ref, do_ref, lse_ref, dd_ref, dq_ref, dk_ref, dv_ref):
        def column(ref, r0, lanes):
            rows = pl.ds(r0, BAND)
            first = slice(lanes.start, lanes.start + 1)
            second = slice(lanes.start + HEAD_DIM, lanes.start + HEAD_DIM + 1)
            return jnp.concatenate([ref[rows, first], ref[rows, second]], axis=0)

        def block(r0, k0, nkeys, lanes, first):
            q2 = _stack_heads(q_ref[pl.ds(r0, BAND), lanes])
            do2 = _stack_heads(do_ref[pl.ds(r0, BAND), lanes])
            kk = k_ref[pl.ds(k0, nkeys), lanes]
            vv = v_ref[pl.ds(k0, nkeys), lanes]
            s = jnp.where(_band_mask(nkeys), _dot_nt(q2, kk), NEG_INF)
            p = jnp.exp(s - column(lse_ref, r0, lanes))
            ds = (p * (_dot_nt(do2, vv) - column(dd_ref, r0, lanes))).astype(BF16)
            dq_ref[pl.ds(r0, BAND), lanes] = _unstack(_dot_nn(ds, kk))
            dk_part = _dot_tn(ds, q2)
            dv_part = _dot_tn(p.astype(BF16), do2)
            if first:
                dk_ref[pl.ds(k0, nkeys), lanes] = dk_part
                dv_ref[pl.ds(k0, nkeys), lanes] = dv_part
            else:
                dk_ref[pl.ds(k0, BAND), lanes] += dk_part[:BAND]
                dv_ref[pl.ds(k0, BAND), lanes] += dv_part[:BAND]
                dk_ref[pl.ds(k0 + BAND, BAND), lanes] = dk_part[BAND:]
                dv_ref[pl.ds(k0 + BAND, BAND), lanes] = dv_part[BAND:]

        for u in range(units):
            block(0, 0, BAND, slice(u * 128, (u + 1) * 128), True)

        _for_later_blocks(nblk, units, lambda r0, lanes: block(r0, r0 - BAND, 2 * BAND, lanes, False))

    spec = _class_spec(d)
    return pl.pallas_call(
        body, name=name, grid=(8 * d // units,),
        in_specs=[spec] * 6, out_specs=[spec] * 3,
        out_shape=[_sds((S // d, d * D), F32)] * 3,
        compiler_params=_cparams(1),
    )(q, k, v, do, lse, dd)


MIX_TILE = 256
DILATIONS = tuple(d for _, d in BRANCHES)


def _branch_weights(la, lb, lc):
    m = jnp.maximum(jnp.maximum(la, lb), lc)
    ea, eb, ec = jnp.exp(la - m), jnp.exp(lb - m), jnp.exp(lc - m)
    den = ea + eb + ec
    return ea / den, eb / den, ec / den


def _mix_operands(outs, lses):
    specs = [_class_block(d, MIX_TILE) for d in DILATIONS] * 2
    scratch = [pltpu.VMEM((NCHUNK, MIX_TILE, 128), F32)] * 4
    return list(outs) + list(lses), specs, scratch


def _mix_fwd(name, outs, lses):
    def body(o0, o1, o2, l0, l1, l2, o_ref, to1, to2, tl1, tl2):
        for blk, tmp, d in ((o1, to1, DILATIONS[1]), (o2, to2, DILATIONS[2]), (l1, tl1, DILATIONS[1]), (l2, tl2, DILATIONS[2])):
            _tokens_from_classes(blk, tmp, d, MIX_TILE)
        for c in range(NCHUNK):
            wa, wb, wc = _branch_weights(l0[:, _chunk(c)], tl1[c], tl2[c])
            o_ref[:, _chunk(c)] = (wa * o0[:, _chunk(c)] + wb * to1[c] + wc * to2[c]).astype(o_ref.dtype)

    operands, specs, scratch = _mix_operands(outs, lses)
    return pl.pallas_call(
        body, name=name, grid=(S // MIX_TILE,),
        in_specs=specs, out_specs=_row_spec(MIX_TILE), out_shape=_sds((S, D), BF16),
        scratch_shapes=scratch, compiler_params=_cparams(1),
    )(*operands)


def _head_sum(x, ones_blockdiag):
    hi = x.astype(BF16)
    r1 = x - hi.astype(F32)
    mid = r1.astype(BF16)
    lo = (r1 - mid.astype(F32)).astype(BF16)
    return _dot_nn(hi, ones_blockdiag) + _dot_nn(mid, ones_blockdiag) + _dot_nn(lo, ones_blockdiag)


def _mix_bwd(name, do, outs, lses, ones_blockdiag):
    def body(do_ref, o0, o1, o2, l0, l1, l2, ones_ref, d0, d1, d2, t0, t1, t2,
             to1, to2, tl1, tl2, td1, td2, tt1, tt2):
        for blk, tmp, d in ((o1, to1, DILATIONS[1]), (o2, to2, DILATIONS[2]), (l1, tl1, DILATIONS[1]), (l2, tl2, DILATIONS[2])):
            _tokens_from_classes(blk, tmp, d, MIX_TILE)
        ones = ones_ref[...]
        for c in range(NCHUNK):
            w = _branch_weights(l0[:, _chunk(c)], tl1[c], tl2[c])
            dov = do_ref[:, _chunk(c)]
            o = w[0] * o0[:, _chunk(c)] + w[1] * to1[c] + w[2] * to2[c]
            t = _head_sum(dov * o, ones)
            d0[:, _chunk(c)] = (w[0] * dov).astype(d0.dtype)
            t0[:, _chunk(c)] = w[0] * t
            td1[c], tt1[c] = w[1] * dov, w[1] * t
            td2[c], tt2[c] = w[2] * dov, w[2] * t
        for tmp, blk, d in ((td1, d1, DILATIONS[1]), (tt1, t1, DILATIONS[1]), (td2, d2, DILATIONS[2]), (tt2, t2, DILATIONS[2])):
            _classes_from_tokens(tmp, blk, d, MIX_TILE)

    operands, specs, scratch = _mix_operands(outs, lses)
    out_specs = [_class_block(d, MIX_TILE) for d in DILATIONS] * 2
    out_shape = [_sds((S // d, d * D), BF16) for d in DILATIONS] + [_sds((S // d, d * D), F32) for d in DILATIONS]
    return pl.pallas_call(
        body, name=name, grid=(S // MIX_TILE,),
        in_specs=[_row_spec(MIX_TILE)] + specs + [_vec_spec(128, 128)],
        out_specs=out_specs, out_shape=out_shape,
        scratch_shapes=scratch + [pltpu.VMEM((NCHUNK, MIX_TILE, 128), F32)] * 4,
        compiler_params=_cparams(1),
    )(do, *operands, ones_blockdiag)


def _attn_bwd_post(name, grads, cos_t, sin_t):
    tm = MIX_TILE
    scale = HEAD_DIM ** -0.5

    def unrope(x, cs, sn):
        return x * cs - _swap_halves(x) * sn

    def body(*refs):
        in_refs = refs[:9]
        cos_ref, sin_ref, dq_ref, dkv_ref, tmp_ref = refs[9:]
        cs = cos_ref[...]
        sn = sin_ref[...]
        for g, d in enumerate(DILATIONS):
            for which, blk in enumerate(in_refs[3 * g:3 * g + 3]):
                if d > 1:
                    _tokens_from_classes(blk, tmp_ref, d, tm)
                for c in range(NCHUNK):
                    x = tmp_ref[c] if d > 1 else blk[:, _chunk(c)]
                    if which == 0:
                        dq_ref[:, _chunk(c, g * D)] = (unrope(x, cs, sn) * scale).astype(dq_ref.dtype)
                    elif which == 1:
                        dkv_ref[:, _chunk(c, g * D)] = unrope(x, cs, sn).astype(dkv_ref.dtype)
                    else:
                        dkv_ref[:, _chunk(c, QW + g * D)] = x.astype(dkv_ref.dtype)

    operands = [a for branch in grads for a in branch]
    tab = pl.BlockSpec((tm, 128), lambda i: (i, 0))
    return pl.pallas_call(
        body, name=name, grid=(S // tm,),
        in_specs=[_class_block(d, tm) for d in DILATIONS for _ in range(3)] + [tab, tab],
        out_specs=[pl.BlockSpec((tm, QW), lambda i: (i, 0)), pl.BlockSpec((tm, 2 * QW), lambda i: (i, 0))],
        out_shape=[_sds((S, QW), BF16), _sds((S, 2 * QW), BF16)],
        scratch_shapes=[pltpu.VMEM((NCHUNK, tm, 128), F32)],
        compiler_params=_cparams(1),
    )(*operands, cos_t, sin_t)


def _adamw(name, parts, w, m, v):
    n, rows, cols = parts.shape
    tr = rows
    for cand in (256, 176, 128, 64, 32, 16, 8):
        if rows % cand == 0:
            tr = cand
            break
    def body(p_ref, w_ref, m_ref, v_ref, g_ref, d_ref, nm_ref, nv_ref):
        g = p_ref[0].astype(F32)
        for j in range(1, n):
            g = g + p_ref[j].astype(F32)
        g_ref[...] = g
        d_ref[...], nm_ref[...], nv_ref[...] = _adam_update(g, w_ref[...], m_ref[...], v_ref[...])

    blk = pl.BlockSpec((tr, cols), lambda i: (i, 0))
    return pl.pallas_call(
        body, name=name, grid=(rows // tr,),
        in_specs=[pl.BlockSpec((n, tr, cols), lambda i: (0, i, 0)), blk, blk, blk],
        out_specs=[blk] * 4, out_shape=[_sds((rows, cols), F32)] * 4,
        compiler_params=_cparams(1),
    )(parts, w, m, v)


def _adam_update(g, w, m, v):
    c1 = 1.0 / (1.0 - ADAM_B1 ** ADAM_STEP)
    c2 = 1.0 / (1.0 - ADAM_B2 ** ADAM_STEP)
    nm = ADAM_B1 * m + (1.0 - ADAM_B1) * g
    nv = ADAM_B2 * v + (1.0 - ADAM_B2) * (g * g)
    return -ADAM_LR * ((nm * c1) / (jnp.sqrt(nv * c2) + ADAM_EPS) + ADAM_WD * w), nm, nv


GAIN_ROWS = 16


def _pack_small(name, gain_tiles, taps):
    ng = len(gain_tiles)

    def body(*refs):
        o_ref = refs[-1]
        o_ref[...] = jnp.zeros_like(o_ref)
        for i in range(ng):
            o_ref[i:i + 1, :] = refs[i][0:1, :]
        o_ref[ng:ng + 3, :] = refs[ng][0:3, :]

    return pl.pallas_call(body, name=name, out_shape=_sds((GAIN_ROWS, D), F32))(*gain_tiles, taps)


def _adamw_gains(name, parts, params):
    np_ = len(params)
    shapes = [w.shape for w, _, _ in params]

    def body(p_ref, *refs):
        ins, outs = refs[:3 * np_], refs[3 * np_:]

        def total(lo, rows):
            g = p_ref[0, lo:lo + rows, :]
            for j in range(1, NDEV):
                g = g + p_ref[j, lo:lo + rows, :]
            return g

        lo = 0
        for i, shape in enumerate(shapes):
            g = total(lo, shape[0])
            lo += shape[0]
            w_ref, m_ref, v_ref = ins[3 * i:3 * i + 3]
            g_ref, d_ref, nm_ref, nv_ref = outs[4 * i:4 * i + 4]
            g_ref[...] = g
            d_ref[...], nm_ref[...], nv_ref[...] = _adam_update(g, w_ref[...], m_ref[...], v_ref[...])
        taps_ref = outs[-1]
        taps_ref[...] = jnp.zeros_like(taps_ref)
        taps_ref[0:3, :] = total(lo, 3)

    out_shape = [_sds(shape, F32) for shape in shapes for _ in range(4)] + [_sds((8, D), F32)]
    outs = pl.pallas_call(body, name=name, out_shape=out_shape)(parts, *[a for p in params for a in p])
    return [list(outs[4 * i:4 * i + 4]) for i in range(np_)], outs[-1]


def _exchange(name, arrays, kind):
    n = len(arrays)
    gather = kind == "gather"
    out_shape = [_sds((NDEV,) + a.shape if gather else a.shape, a.dtype) for a in arrays]

    def body(*refs):
        srcs, outs = refs[:n], refs[n:2 * n]
        send_sems, recv_sems, local_sems = refs[2 * n:]
        x, y, c = lax.axis_index("x"), lax.axis_index("y"), lax.axis_index("c")
        me = 4 * x + 2 * y + c
        pending = []
        for t in range(n):
            own = pltpu.make_async_copy(srcs[t] if gather else srcs[t].at[me], outs[t].at[me], local_sems.at[t])
            own.start()
            pending.append(own)
            for rel in range(1, NDEV):
                px = 1 - x if rel & 4 else x
                py = 1 - y if rel & 2 else y
                pc = 1 - c if rel & 1 else c
                peer = 4 * px + 2 * py + pc
                send = pltpu.make_async_remote_copy(
                    src_ref=srcs[t] if gather else srcs[t].at[peer], dst_ref=outs[t].at[me],
                    send_sem=send_sems.at[t, rel - 1], recv_sem=recv_sems.at[t, rel - 1],
                    device_id=(px, py, pc), device_id_type=MESH)
                send.start()
                arrive = pltpu.make_async_remote_copy(
                    src_ref=srcs[t] if gather else srcs[t].at[me], dst_ref=outs[t].at[peer],
                    send_sem=send_sems.at[t, rel - 1], recv_sem=recv_sems.at[t, rel - 1],
                    device_id=(px, py, pc), device_id_type=MESH)
                pending.append((send, arrive))
        for item in pending:
            if isinstance(item, tuple):
                item[0].wait_send()
                item[1].wait_recv()
            else:
                item.wait()

    any_spec = pl.BlockSpec(memory_space=pl.ANY)
    outs = pl.pallas_call(
        body, name=name,
        in_specs=[any_spec] * n, out_specs=[any_spec] * n, out_shape=out_shape,
        scratch_shapes=[pltpu.SemaphoreType.DMA((n, NDEV - 1)), pltpu.SemaphoreType.DMA((n, NDEV - 1)),
                        pltpu.SemaphoreType.DMA((n,))],
    )(*arrays)
    return list(outs)


_HBM_SPEC = pl.BlockSpec(memory_space=pltpu.HBM)
_SEM_SPEC = pl.BlockSpec(memory_space=pltpu.SEMAPHORE)
_DATAFLOW = pltpu.SideEffectType.DATAFLOW_SIDE_EFFECTING


def _peers():
    x, y, c = lax.axis_index("x"), lax.axis_index("y"), lax.axis_index("c")
    out = []
    for rel in range(1, NDEV):
        px = 1 - x if rel & 4 else x
        py = 1 - y if rel & 2 else y
        pc = 1 - c if rel & 1 else c
        out.append((rel - 1, (px, py, pc), 4 * px + 2 * py + pc))
    return 4 * x + 2 * y + c, out


def _hbm(a):
    return pltpu.HBM(a.shape, a.dtype)


def _own_slot(a, me, kind):
    mine = a[None] if kind == "gather" else lax.dynamic_slice_in_dim(a, me, 1, axis=0)
    shape = (NDEV,) + mine.shape[1:]
    return lax.dynamic_update_slice_in_dim(lax.empty(shape, a.dtype), mine, me, axis=0)


def _exchange_start(name, arrays, me, kind):
    n = len(arrays)
    gather = kind == "gather"
    lands = [_own_slot(a, me, kind) for a in arrays]

    def body(*refs):
        src_refs, land_refs = refs[:n], refs[n:2 * n]
        send_sems, recv_sems = refs[2 * n], refs[2 * n + 1]
        token = refs[-1]
        my_block, peers = _peers()
        for t in range(n):
            for slot, dev, block in peers:
                pltpu.make_async_remote_copy(
                    src_ref=src_refs[t] if gather else src_refs[t].at[block], dst_ref=land_refs[t].at[my_block],
                    send_sem=send_sems.at[t * (NDEV - 1) + slot], recv_sem=recv_sems.at[t * (NDEV - 1) + slot],
                    device_id=dev, device_id_type=MESH).start()
        token[...] = jnp.zeros_like(token)

    operands = [pltpu.with_memory_space_constraint(a, pltpu.HBM) for a in list(arrays) + lands]
    outs = pl.pallas_call(
        body, name=name,
        out_shape=(pltpu.SemaphoreType.DMA((n * (NDEV - 1),)), pltpu.SemaphoreType.DMA((n * (NDEV - 1),)),
                   *[_hbm(a) for a in operands], _sds((8, 128), F32)),
        in_specs=[_HBM_SPEC] * (2 * n),
        out_specs=(_SEM_SPEC, _SEM_SPEC, *[_HBM_SPEC] * (2 * n), pl.BlockSpec(memory_space=pltpu.VMEM)),
        input_output_aliases={i: 2 + i for i in range(2 * n)},
        compiler_params=pltpu.CompilerParams(has_side_effects=_DATAFLOW),
    )(*operands)
    return (outs[0], outs[1], list(outs[2:2 + n]), list(outs[2 + n:2 + 2 * n])), outs[-1]


def _exchange_wait(name, started, t, after, kind):
    send_sems, recv_sems, srcs, lands = started
    gather = kind == "gather"

    def body(src_ref, land_ref, send_ref, recv_ref, after_ref, src_out, land_out):
        _, peers = _peers()
        for slot, dev, block in peers:
            copy = pltpu.make_async_remote_copy(
                src_ref=src_ref if gather else src_ref.at[block], dst_ref=land_ref.at[block],
                send_sem=send_ref.at[t * (NDEV - 1) + slot], recv_sem=recv_ref.at[t * (NDEV - 1) + slot],
                device_id=dev, device_id_type=MESH)
            copy.wait_send()
            copy.wait_recv()

    return pl.pallas_call(
        body, name=name, out_shape=(_hbm(srcs[t]), _hbm(lands[t])),
        in_specs=(_HBM_SPEC, _HBM_SPEC, _SEM_SPEC, _SEM_SPEC, pl.BlockSpec(memory_space=pl.ANY)),
        out_specs=(_HBM_SPEC, _HBM_SPEC), input_output_aliases={0: 0, 1: 1},
        compiler_params=pltpu.CompilerParams(has_side_effects=_DATAFLOW),
    )(srcs[t], lands[t], send_sems, recv_sems, after)[1]


def _ffn_fwd(tag, n, weight):
    wg = weight(f"gate_up_{tag}", n)
    gu, act = _gate_up_act(f"ffn_gate_up_{tag}", n, wg)
    wd4 = weight(f"down_{tag}", act).reshape(NFB, FB, D)
    f = _fwd_kblocked(f"ffn_down_{tag}", act, wd4)
    return (n, gu, act, wg, wd4), f


def _ffn_bwd(tag, dh_out, h_in, f, saved, g_pre, g_post, send):
    n, gu, act, wg, wd4 = saved
    df, (dg_post,) = _rms_bwd(f"ffn_postnorm_bwd_{tag}", f, [(g_post, dh_out)], None, BF16)
    tok = send(f"down_{tag}", _bwd_w_kblocked(f"ffn_down_dw_{tag}", act, df).reshape(NDEV, DFF // NDEV, D))
    dgu = _down_dx_act_bwd(f"ffn_down_dx_{tag}", df, wd4, gu, tok).reshape(NDEV, S, FB)
    tok = send(f"gate_up_{tag}", _bwd_w_cols_blocked(f"ffn_gate_up_dw_{tag}", n, dgu))
    dn = _bwd_x_cols_blocked(f"ffn_gate_up_dx_{tag}", dgu, wg, after=tok)
    dh_in, (dg_pre,) = _rms_bwd(f"ffn_prenorm_bwd_{tag}", h_in, [(g_pre, dn)], dh_out, F32)
    return dh_in, dg_pre, dg_post


def kernel(x, positions, mix_norm_pre, mix_norm_post, ffn_norm_pre, ffn_norm_post, ffn_w_gate_up, ffn_w_down, conv_w_in, conv_w, conv_w_out, kv_norm, w_kv, w_q, w_o, loss_target, m_mix_norm_pre, m_mix_norm_post, m_ffn_norm_pre, m_ffn_norm_post, m_ffn_w_gate_up, m_ffn_w_down, m_conv_w_in, m_conv_w, m_conv_w_out, m_kv_norm, m_w_kv, m_w_q, m_w_o, v_mix_norm_pre, v_mix_norm_post, v_ffn_norm_pre, v_ffn_norm_post, v_ffn_w_gate_up, v_ffn_w_down, v_conv_w_in, v_conv_w, v_conv_w_out, v_kv_norm, v_w_kv, v_w_q, v_w_o):
    me = 4 * lax.axis_index("x") + 2 * lax.axis_index("y") + lax.axis_index("c")
    h0 = x.reshape(S, D)
    target = loss_target.reshape(S, D)
    row = lambda a, l: a[l].reshape(1, D)
    g_kv = kv_norm.reshape(1, D)

    cw_shard = jnp.pad(conv_w[0], ((0, 5), (0, 0)))
    names = ["conv_in", "conv_w", "conv_out", "gate_up_0", "down_0", "kv", "q", "o", "gate_up_1", "down_1"]
    shards = [conv_w_in[0], cw_shard, conv_w_out[0], ffn_w_gate_up[0], ffn_w_down[0],
              w_kv, w_q[0], w_o[0], ffn_w_gate_up[1], ffn_w_down[1]]
    shards = [s if n == "conv_w" else s.astype(BF16) for n, s in zip(names, shards)]
    gather, _ = _exchange_start("gather_weights_start", shards, me, "gather")

    def weight(name, after):
        return _exchange_wait(f"gather_wait_{name}", gather, names.index(name), after, "gather")

    sent = {}

    def send(name, grad):
        sent[name], token = _exchange_start(f"scatter_start_{name}", [grad], me, "scatter")
        return token

    n0 = _rms_fwd("mix_prenorm_0", h0, [row(mix_norm_pre, 0)])[0]
    win_g = weight("conv_in", n0)
    cw = weight("conv_w", n0).transpose(1, 0, 2).reshape(8, D)
    z = _fwd_cols("conv_in", n0, win_g)
    pre = _conv_fwd("conv_gate", z, cw)
    wout = weight("conv_out", pre).reshape(D, D)
    y0 = _fwd_rows("conv_out", pre, wout)
    h1, (n1,) = _resid_rms("mix_postnorm_0", h0, y0, row(mix_norm_post, 0), [row(ffn_norm_pre, 0)])
    ffn0, f0 = _ffn_fwd("0", n1, weight)
    h2, (nk, n2) = _resid_rms("ffn_postnorm_0", h1, f0, row(ffn_norm_post, 0), [g_kv, row(mix_norm_pre, 1)])

    wkv_g = weight("kv", nk)
    wkv = wkv_g.transpose(1, 0, 2).reshape(D, 2 * QW)
    half = HEAD_DIM // 2
    inv_freq = ROPE_THETA ** (-jnp.arange(half, dtype=F32) / half)
    tables = _rope_tables("rope_tables", positions.reshape(S, 1), jnp.tile(inv_freq, 4).reshape(1, 128))
    qc, kc, vc, o_c, lse_c = [], [], [], [], []
    for g, d in enumerate(DILATIONS):
        kc.append(_proj_classes(f"k_proj_{g}", nk, wkv, g, d, tables, 1.0))
        vc.append(_proj_classes(f"v_proj_{g}", nk, wkv, len(DILATIONS) + g, d, None, None))
    wq_g = weight("q", vc[-1])
    wq = wq_g.transpose(1, 0, 2).reshape(D, QW)
    for g, d in enumerate(DILATIONS):
        qc.append(_proj_classes(f"q_proj_{g}", n2, wq, g, d, tables, HEAD_DIM ** -0.5))
        o_g, lse_g = _attn_fwd(f"attn_fwd_{g}", qc[g], kc[g], vc[g], d)
        o_c.append(o_g)
        lse_c.append(lse_g)
    o_mix = _mix_fwd("attn_mix", o_c, lse_c)
    wo = weight("o", o_mix).reshape(D, D)
    y1 = _fwd_rows("attn_out", o_mix, wo)
    h3, (n3,) = _resid_rms("mix_postnorm_1", h2, y1, row(mix_norm_post, 1), [row(ffn_norm_pre, 1)])
    ffn1, f1 = _ffn_fwd("1", n3, weight)

    dh4, sq = _resid_rms_loss("ffn_postnorm_1_loss", h3, f1, row(ffn_norm_post, 1), target)
    loss = lax.psum(jnp.sum(sq) * (0.5 / D), ("x", "y", "c"))

    dh3, dg_fpre1, dg_fpost1 = _ffn_bwd(
        "1", dh4, h3, f1, ffn1, row(ffn_norm_pre, 1), row(ffn_norm_post, 1), send)
    dy1, (dg_mpost1,) = _rms_bwd("mix_postnorm_bwd_1", y1, [(row(mix_norm_post, 1), dh3)], None, BF16)
    tok = send("o", _bwd_w_rows("attn_out_dw", o_mix, dy1).reshape(NDEV, D // NDEV, D))
    do = _bwd_x_rows("attn_out_dx", dy1, wo, F32, after=tok)
    lane = jnp.arange(128)
    ones_blockdiag = (lane[:, None] // HEAD_DIM == lane[None, :] // HEAD_DIM).astype(BF16)
    mixed = _mix_bwd("attn_mix_bwd", do, o_c, lse_c, ones_blockdiag)
    branch_grads = [_attn_bwd(f"attn_bwd_{g}", qc[g], kc[g], vc[g], mixed[g], lse_c[g], mixed[3 + g], d)
                    for g, d in enumerate(DILATIONS)]
    dq_raw, dkv = _attn_bwd_post("attn_bwd_post", branch_grads, *tables)
    tok = send("kv", _bwd_w_cols("kv_proj_dw", nk, dkv, 2 * QW // NDEV))
    dnk = _bwd_x_plain("kv_proj_dx", dkv, wkv, 2, after=tok)
    tok = send("q", _bwd_w_cols("q_proj_dw", n2, dq_raw, QW // NDEV))
    dn2 = _bwd_x_plain("q_proj_dx", dq_raw, wq, 1, after=tok)
    dh2, (dg_kv, dg_mpre1) = _rms_bwd("kv_and_mix_prenorm_bwd_1", h2,
                                      [(g_kv, dnk), (row(mix_norm_pre, 1), dn2)], dh3, F32)

    dh1, dg_fpre0, dg_fpost0 = _ffn_bwd(
        "0", dh2, h1, f0, ffn0, row(ffn_norm_pre, 0), row(ffn_norm_post, 0), send)
    dy0, (dg_mpost0,) = _rms_bwd("mix_postnorm_bwd_0", y0, [(row(mix_norm_post, 0), dh1)], None, BF16)
    tok = send("conv_out", _bwd_w_rows("conv_out_dw", pre, dy0).reshape(NDEV, D // NDEV, D))
    dpre = _bwd_x_rows("conv_out_dx", dy0, wout, BF16, after=tok)
    dz, dcw = _conv_bwd("conv_gate_bwd", z, dpre, cw)
    tok = send("conv_in", _bwd_w_cols("conv_in_dw", n0, dz, 3 * D // NDEV))
    dn0 = _bwd_x_plain("conv_in_dx", dz, win_g.transpose(1, 0, 2).reshape(D, 3 * D), 1, after=tok)
    dh0, (dg_mpre0,) = _rms_bwd("mix_prenorm_bwd_0", h0, [(row(mix_norm_pre, 0), dn0)], dh1, F32)

    small = _pack_small("pack_small_grads", [dg_mpre0, dg_mpre1, dg_mpost0, dg_mpost1, dg_fpre0, dg_fpre1,
                                             dg_fpost0, dg_fpost1, dg_kv], dcw)
    small_all = _exchange("gather_small_grads", [small], "gather")[0]

    done = [small_all]

    def upd(tag, w, m, v):
        parts = _exchange_wait(f"scatter_wait_{tag}", sent[tag], 0, done[-1], "scatter")
        shape = w.shape
        flat = lambda a: a.reshape(parts.shape[1:])
        res = _adamw(f"adamw_{tag}", parts, flat(w), flat(m), flat(v))
        done.append(res[0])
        return [r.reshape(shape) for r in res]

    def upd_layer(tag, l, w, m, v):
        return upd(f"{tag}_{l}", w[l], m[l], v[l])

    def stack(per_layer):
        return [jnp.stack([per_layer[0][i], per_layer[1][i]]) for i in range(4)]

    vec = lambda a: a.reshape(1, D)
    gain_res, taps = _adamw_gains("adamw_gains", small_all, [
        (mix_norm_pre, m_mix_norm_pre, v_mix_norm_pre), (mix_norm_post, m_mix_norm_post, v_mix_norm_post),
        (ffn_norm_pre, m_ffn_norm_pre, v_ffn_norm_pre), (ffn_norm_post, m_ffn_norm_post, v_ffn_norm_post),
        (vec(kv_norm), vec(m_kv_norm), vec(v_kv_norm))])
    dcw_mine = lax.dynamic_slice(taps, (0, me * 128), (8, 128))
    pad8 = lambda a, fill: jnp.pad(a[0], ((0, 5), (0, 0)), constant_values=fill)
    cw_res = [r[0:3].reshape(1, 3, 128) for r in
              _adamw("adamw_conv_w", dcw_mine.reshape(1, 8, 128), cw_shard, pad8(m_conv_w, 0.0), pad8(v_conv_w, 1.0))]

    res = {
        "mix_norm_pre": gain_res[0],
        "mix_norm_post": gain_res[1],
        "ffn_norm_pre": gain_res[2],
        "ffn_norm_post": gain_res[3],
        "kv_norm": [r.reshape(D) for r in gain_res[4]],
        "conv_w": cw_res,
    }
    down, gate_up = {}, {}
    down[1] = upd_layer("down", 1, ffn_w_down, m_ffn_w_down, v_ffn_w_down)
    gate_up[1] = upd_layer("gate_up", 1, ffn_w_gate_up, m_ffn_w_gate_up, v_ffn_w_gate_up)
    res["w_o"] = upd("o", w_o, m_w_o, v_w_o)
    res["w_q"] = upd("q", w_q, m_w_q, v_w_q)
    res["w_kv"] = upd("kv", w_kv, m_w_kv, v_w_kv)
    down[0] = upd_layer("down", 0, ffn_w_down, m_ffn_w_down, v_ffn_w_down)
    gate_up[0] = upd_layer("gate_up", 0, ffn_w_gate_up, m_ffn_w_gate_up, v_ffn_w_gate_up)
    res["ffn_w_down"] = stack(down)
    res["ffn_w_gate_up"] = stack(gate_up)
    res["conv_w_out"] = upd("conv_out", conv_w_out, m_conv_w_out, v_conv_w_out)
    res["conv_w_in"] = upd("conv_in", conv_w_in, m_conv_w_in, v_conv_w_in)
    order = ["mix_norm_pre", "mix_norm_post", "ffn_norm_pre", "ffn_norm_post", "ffn_w_gate_up", "ffn_w_down",
             "conv_w_in", "conv_w", "conv_w_out", "kv_norm", "w_kv", "w_q", "w_o"]
    out = [loss, dh0.reshape(1, S, D)]
    for i in range(4):
        out += [res[name][i] for name in order]
    return tuple(out)
```

```python
import jax
import jax.numpy as jnp
from jax import lax
from jax.experimental import pallas as pl
from jax.experimental.pallas import tpu as pltpu

F32 = jnp.float32
BF16 = jnp.bfloat16

S = 4096
D = 1024
NDEV = 8
HEAD_DIM = 64
QW = 3072
DFF = 2816
FB = 704
NFB = 4
BRANCHES = ((128, 1), (512, 4), (2048, 16))
BAND = 128
ROPE_THETA = 10000.0
RMS_EPS = 1e-6
NEG_INF = -1e30
ADAM_LR, ADAM_B1, ADAM_B2, ADAM_EPS, ADAM_WD, ADAM_STEP = 0.001, 0.9, 0.999, 1e-08, 0.01, 10

VMEM_LIMIT_BYTES = 52 * 1024 * 1024
ROW_TILE = 512
MESH = pl.DeviceIdType.MESH


def _cparams(ngrid):
    return pltpu.CompilerParams(dimension_semantics=("arbitrary",) * ngrid,
                                vmem_limit_bytes=VMEM_LIMIT_BYTES)


def _sds(shape, dtype):
    return jax.ShapeDtypeStruct(tuple(shape), dtype)


_DIMS = {"nn": (((1,), (0,)), ((), ())),
         "nt": (((1,), (1,)), ((), ())),
         "tn": (((0,), (0,)), ((), ()))}


def _matmul(name, a, b, *, mode, grid, a_blk, a_map, b_blk, b_map, o_shape, o_blk, o_map, out_dtype, after=None,
            out_groups=1):
    nk = grid[2]
    dims = _DIMS[mode]
    acc_shape = tuple(s for s in o_blk if s is not None)
    if out_groups > 1:
        acc_shape = (acc_shape[1], out_groups * acc_shape[2])
    extra = [] if after is None else [after]

    def store(o_ref, val):
        if out_groups == 1:
            o_ref[...] = val.astype(o_ref.dtype)
        else:
            n = o_ref.shape[-1]
            for grp in range(out_groups):
                o_ref[grp] = val[:, grp * n:(grp + 1) * n].astype(o_ref.dtype)

    def body(a_ref, b_ref, *rest):
        o_ref, scratch = rest[len(extra)], rest[len(extra) + 1:]
        part = lax.dot_general(a_ref[...], b_ref[...], dims, preferred_element_type=F32)
        if nk == 1:
            store(o_ref, part)
            return
        acc_ref = scratch[0]
        k = pl.program_id(2)

        @pl.when(k == 0)
        def _():
            acc_ref[...] = part

        @pl.when(k > 0)
        def _():
            acc_ref[...] += part

        @pl.when(k == nk - 1)
        def _():
            store(o_ref, acc_ref[...])

    return pl.pallas_call(
        body, name=name, grid=grid,
        in_specs=[pl.BlockSpec(a_blk, a_map), pl.BlockSpec(b_blk, b_map)] + [pl.BlockSpec(memory_space=pl.ANY)] * len(extra),
        out_specs=pl.BlockSpec(o_blk, o_map),
        out_shape=_sds(o_shape, out_dtype),
        scratch_shapes=[] if nk == 1 else [pltpu.VMEM(acc_shape, F32)],
        compiler_params=_cparams(3),
    )(a, b, *extra)


TM = 1024
TK = S


def _fwd_cols(name, a, wg, out_dtype=BF16):
    _, kdim, n = wg.shape
    return _matmul(name, a, wg, mode="nn", grid=(S // TM, NDEV, 1),
                   a_blk=(TM, kdim), a_map=lambda i, j, k: (i, 0),
                   b_blk=(None, kdim, n), b_map=lambda i, j, k: (j, 0, 0),
                   o_shape=(S, NDEV * n), o_blk=(TM, n), o_map=lambda i, j, k: (i, j), out_dtype=out_dtype)


def _fwd_rows(name, a, w, out_dtype=F32):
    kdim, n = w.shape
    tn = 512
    return _matmul(name, a, w, mode="nn", grid=(S // TM, n // tn, 1),
                   a_blk=(TM, kdim), a_map=lambda i, j, k: (i, 0),
                   b_blk=(kdim, tn), b_map=lambda i, j, k: (0, j),
                   o_shape=(S, n), o_blk=(TM, tn), o_map=lambda i, j, k: (i, j), out_dtype=out_dtype)


def _fwd_kblocked(name, a4, w4):
    nb, _, kb = a4.shape
    n = w4.shape[2]

    def body(a_ref, w_ref, o_ref):
        acc = _dot_nn(a_ref[0], w_ref[0])
        for j in range(1, nb):
            acc = acc + _dot_nn(a_ref[j], w_ref[j])
        o_ref[...] = acc

    return pl.pallas_call(
        body, name=name, grid=(S // TM,),
        in_specs=[pl.BlockSpec((nb, TM, kb), lambda i: (0, i, 0)), pl.BlockSpec((nb, kb, n), lambda i: (0, 0, 0))],
        out_specs=pl.BlockSpec((TM, n), lambda i: (i, 0)), out_shape=_sds((S, n), F32),
        compiler_params=_cparams(1),
    )(a4, w4)


def _bwd_x_cols_blocked(name, dy8, wg, after):
    _, kdim, n = wg.shape
    nk = NDEV // 2

    def body(a_ref, b_ref, after_ref, o_ref, acc_ref):
        k = pl.program_id(1)
        part = _dot_nt(a_ref[0], b_ref[0]) + _dot_nt(a_ref[1], b_ref[1])

        @pl.when(k == 0)
        def _():
            acc_ref[...] = part

        @pl.when(k > 0)
        def _():
            acc_ref[...] += part

        @pl.when(k == nk - 1)
        def _():
            o_ref[...] = acc_ref[...]

    return pl.pallas_call(
        body, name=name, grid=(S // TM, nk),
        in_specs=[pl.BlockSpec((2, None, TM, n), lambda i, k: (0, k, i, 0)),
                  pl.BlockSpec((2, None, kdim, n), lambda i, k: (0, k, 0, 0)),
                  pl.BlockSpec(memory_space=pl.ANY)],
        out_specs=pl.BlockSpec((TM, kdim), lambda i, k: (i, 0)), out_shape=_sds((S, kdim), F32),
        scratch_shapes=[pltpu.VMEM((TM, kdim), F32)],
        compiler_params=_cparams(2),
    )(dy8.reshape(2, nk, S, n), wg.reshape(2, nk, kdim, n), after)


def _bwd_x_rows(name, dy, w, out_dtype, after=None):
    kdim, n = w.shape
    tkk = 512
    return _matmul(name, dy, w, mode="nt", grid=(S // TM, kdim // tkk, 1),
                   a_blk=(TM, n), a_map=lambda i, j, k: (i, 0),
                   b_blk=(tkk, n), b_map=lambda i, j, k: (j, 0),
                   o_shape=(S, kdim), o_blk=(TM, tkk), o_map=lambda i, j, k: (i, j), out_dtype=out_dtype, after=after)


DW_COLS = 768


def _bwd_w_cols(name, a, dy, n):
    kdim = a.shape[1]
    groups = DW_COLS // n
    return _matmul(name, a, dy, mode="tn", grid=(1, NDEV // groups, S // TK),
                   a_blk=(TK, kdim), a_map=lambda i, j, k: (k, 0),
                   b_blk=(TK, DW_COLS), b_map=lambda i, j, k: (k, j),
                   o_shape=(NDEV, kdim, n), o_blk=(groups, kdim, n) if groups > 1 else (None, kdim, n),
                   o_map=lambda i, j, k: (j, 0, 0), out_dtype=BF16, out_groups=groups)


def _bwd_x_plain(name, dy, w, nk, after=None):
    kdim, n = w.shape
    return _matmul(name, dy, w, mode="nt", grid=(S // TM, 1, nk),
                   a_blk=(TM, n // nk), a_map=lambda i, j, k: (i, k),
                   b_blk=(kdim, n // nk), b_map=lambda i, j, k: (0, k),
                   o_shape=(S, kdim), o_blk=(TM, kdim), o_map=lambda i, j, k: (i, 0), out_dtype=F32, after=after)


def _bwd_w_cols_blocked(name, a, dy8):
    kdim = a.shape[1]
    n = dy8.shape[2]
    return _matmul(name, a, dy8, mode="tn", grid=(1, NDEV, S // TK),
                   a_blk=(TK, kdim), a_map=lambda i, j, k: (k, 0),
                   b_blk=(None, TK, n), b_map=lambda i, j, k: (j, k, 0),
                   o_shape=(NDEV, kdim, n), o_blk=(None, kdim, n), o_map=lambda i, j, k: (j, 0, 0), out_dtype=BF16)


def _bwd_w_rows(name, a, dy):
    kdim = a.shape[1]
    n = dy.shape[1]
    tmm = 512
    return _matmul(name, a, dy, mode="tn", grid=(kdim // tmm, 1, S // TK),
                   a_blk=(TK, tmm), a_map=lambda i, j, k: (k, i),
                   b_blk=(TK, n), b_map=lambda i, j, k: (k, 0),
                   o_shape=(kdim, n), o_blk=(tmm, n), o_map=lambda i, j, k: (i, 0), out_dtype=BF16)


def _bwd_w_kblocked(name, a4, dy):
    nb, _, kb = a4.shape
    n = dy.shape[1]
    return _matmul(name, a4, dy, mode="tn", grid=(nb, 1, S // TK),
                   a_blk=(None, TK, kb), a_map=lambda i, j, k: (i, k, 0),
                   b_blk=(TK, n), b_map=lambda i, j, k: (k, 0),
                   o_shape=(nb, kb, n), o_blk=(None, kb, n), o_map=lambda i, j, k: (i, 0, 0), out_dtype=BF16)


def _rstd(x):
    return lax.rsqrt(jnp.mean(x * x, axis=-1, keepdims=True) + RMS_EPS)


def _row_spec(tm=ROW_TILE, width=D):
    return pl.BlockSpec((tm, width), lambda i: (i, 0))


def _vec_spec(rows=1, width=D):
    return pl.BlockSpec((rows, width), lambda i: (0, 0))


def _rms_fwd(name, x, gains):
    n = len(gains)

    def body(x_ref, *refs):
        x_val = x_ref[...]
        xh = x_val * _rstd(x_val)
        for g_ref, o_ref in zip(refs[:n], refs[n:]):
            o_ref[...] = (xh * g_ref[...]).astype(o_ref.dtype)

    outs = pl.pallas_call(
        body, name=name, grid=(S // ROW_TILE,),
        in_specs=[_row_spec()] + [_vec_spec()] * n,
        out_specs=[_row_spec()] * n,
        out_shape=[_sds((S, D), BF16)] * n,
        compiler_params=_cparams(1),
    )(x, *gains)
    return list(outs)


def _resid_rms(name, h, y, g, next_gains):
    n = len(next_gains)

    def body(h_ref, y_ref, g_ref, *refs):
        y_val = y_ref[...]
        h_new = h_ref[...] + (y_val * _rstd(y_val)) * g_ref[...]
        refs[n][...] = h_new
        hh = h_new * _rstd(h_new)
        for g2_ref, o_ref in zip(refs[:n], refs[n + 1:]):
            o_ref[...] = (hh * g2_ref[...]).astype(o_ref.dtype)

    outs = pl.pallas_call(
        body, name=name, grid=(S // ROW_TILE,),
        in_specs=[_row_spec(), _row_spec(), _vec_spec()] + [_vec_spec()] * n,
        out_specs=[_row_spec()] * (n + 1), out_shape=[_sds((S, D), F32)] + [_sds((S, D), BF16)] * n,
        compiler_params=_cparams(1),
    )(h, y, g, *next_gains)
    return outs[0], list(outs[1:])


def _resid_rms_loss(name, h, y, g, target):
    def body(h_ref, y_ref, g_ref, t_ref, dh_ref, part_ref):
        y_val = y_ref[...]
        e = h_ref[...] + (y_val * _rstd(y_val)) * g_ref[...] - t_ref[...]
        dh_ref[...] = e * (1.0 / D)
        part = jnp.sum(e * e, axis=0, keepdims=True)
        step = pl.program_id(0)

        @pl.when(step == 0)
        def _():
            part_ref[...] = part

        @pl.when(step > 0)
        def _():
            part_ref[...] += part

    return pl.pallas_call(
        body, name=name, grid=(S // ROW_TILE,),
        in_specs=[_row_spec(), _row_spec(), _vec_spec(), _row_spec()],
        out_specs=[_row_spec(), _vec_spec()],
        out_shape=[_sds((S, D), F32), _sds((1, D), F32)],
        compiler_params=_cparams(1),
    )(h, y, g, target)


def _rms_bwd(name, x, pairs, dres, out_dtype):
    n = len(pairs)
    has_res = dres is not None

    def body(x_ref, *refs):
        g_refs = refs[0:2 * n:2]
        dn_refs = refs[1:2 * n:2]
        pos = 2 * n
        res_ref = refs[pos] if has_res else None
        pos += int(has_res)
        dx_ref = refs[pos]
        dg_refs = refs[pos + 1:]
        step = pl.program_id(0)
        x_val = x_ref[...]
        r = _rstd(x_val)
        xh = x_val * r
        acc = res_ref[...] if has_res else jnp.zeros_like(x_val)
        for g_ref, dn_ref, dg_ref in zip(g_refs, dn_refs, dg_refs):
            dn = dn_ref[...].astype(F32)
            dxh = dn * g_ref[...]
            acc = acc + r * (dxh - xh * jnp.mean(dxh * xh, axis=-1, keepdims=True))
            part = jnp.sum(dn * xh, axis=0, keepdims=True)

            @pl.when(step == 0)
            def _():
                dg_ref[...] = jnp.zeros_like(dg_ref)

            dg_ref[0:1, :] += part

        dx_ref[...] = acc.astype(dx_ref.dtype)

    operands = [x]
    in_specs = [_row_spec()]
    for g, dn in pairs:
        operands += [g, dn]
        in_specs += [_vec_spec(), _row_spec()]
    if has_res:
        operands.append(dres)
        in_specs.append(_row_spec())
    outs = pl.pallas_call(
        body, name=name, grid=(S // ROW_TILE,),
        in_specs=in_specs,
        out_specs=[_row_spec()] + [_vec_spec(8)] * n,
        out_shape=[_sds((S, D), out_dtype)] + [_sds((8, D), F32)] * n,
        compiler_params=_cparams(1),
    )(*operands)
    return outs[0], list(outs[1:])


def _shift_down(u, prev8, k):
    r = pltpu.roll(u, k, 0)
    p = pltpu.roll(prev8, k, 0)
    row = lax.broadcasted_iota(jnp.int32, prev8.shape, 0)
    top = jnp.where(row < k, p, r[0:8])
    return jnp.concatenate([top, r[8:]], axis=0)


def _shift_up(u, next8, k):
    tm = u.shape[0]
    r = pltpu.roll(u, tm - k, 0)
    p = pltpu.roll(next8, 8 - k, 0)
    row = lax.broadcasted_iota(jnp.int32, next8.shape, 0)
    bot = jnp.where(row >= 8 - k, p, r[tm - 8:tm])
    return jnp.concatenate([r[:tm - 8], bot], axis=0)


CONV_TILE = 512


def _halo_prev(col):
    return pl.BlockSpec((8, D), lambda i: (jnp.maximum(i * (CONV_TILE // 8) - 1, 0), col))


def _halo_next(col):
    last = S // 8 - 1
    return pl.BlockSpec((8, D), lambda i: (jnp.minimum((i + 1) * (CONV_TILE // 8), last), col))


def _conv_fwd(name, z, cw):
    def body(b_ref, c_ref, h_ref, cp_ref, hp_ref, cw_ref, o_ref):
        i = pl.program_id(0)
        u = c_ref[...].astype(F32) * h_ref[...].astype(F32)
        up = cp_ref[...].astype(F32) * hp_ref[...].astype(F32)
        up = jnp.where(i > 0, up, 0.0)
        cv = cw_ref[0:1, :] * _shift_down(u, up, 2) + cw_ref[1:2, :] * _shift_down(u, up, 1) + cw_ref[2:3, :] * u
        o_ref[...] = (b_ref[...].astype(F32) * cv).astype(o_ref.dtype)

    col = lambda c: pl.BlockSpec((CONV_TILE, D), lambda i: (i, c))
    return pl.pallas_call(
        body, name=name, grid=(S // CONV_TILE,),
        in_specs=[col(0), col(1), col(2), _halo_prev(1), _halo_prev(2), _vec_spec(8)],
        out_specs=_row_spec(CONV_TILE), out_shape=_sds((S, D), BF16),
        compiler_params=_cparams(1),
    )(z, z, z, z, z, cw)


def _conv_bwd(name, z, dpre, cw):
    nsteps = S // CONV_TILE

    def body(b_ref, c_ref, h_ref, cp_ref, hp_ref, dp_ref, dpn_ref, bn_ref, cw_ref, dz_ref, dcw_ref):
        i = pl.program_id(0)
        b = b_ref[...].astype(F32)
        c = c_ref[...].astype(F32)
        h = h_ref[...].astype(F32)
        dp = dp_ref[...].astype(F32)
        u = c * h
        up = jnp.where(i > 0, cp_ref[...].astype(F32) * hp_ref[...].astype(F32), 0.0)
        s1 = _shift_down(u, up, 1)
        s2 = _shift_down(u, up, 2)
        w0, w1, w2 = cw_ref[0:1, :], cw_ref[1:2, :], cw_ref[2:3, :]
        cv = w0 * s2 + w1 * s1 + w2 * u
        dcv = dp * b
        dcvn = jnp.where(i < nsteps - 1, dpn_ref[...].astype(F32) * bn_ref[...].astype(F32), 0.0)
        du = w2 * dcv + w1 * _shift_up(dcv, dcvn, 1) + w0 * _shift_up(dcv, dcvn, 2)
        dz_ref[:, 0:D] = (dp * cv).astype(dz_ref.dtype)
        dz_ref[:, D:2 * D] = (du * h).astype(dz_ref.dtype)
        dz_ref[:, 2 * D:3 * D] = (du * c).astype(dz_ref.dtype)

        @pl.when(i == 0)
        def _():
            dcw_ref[...] = jnp.zeros_like(dcw_ref)

        dcw_ref[0:1, :] += jnp.sum(dcv * s2, axis=0, keepdims=True)
        dcw_ref[1:2, :] += jnp.sum(dcv * s1, axis=0, keepdims=True)
        dcw_ref[2:3, :] += jnp.sum(dcv * u, axis=0, keepdims=True)

    col = lambda c: pl.BlockSpec((CONV_TILE, D), lambda i: (i, c))
    return pl.pallas_call(
        body, name=name, grid=(nsteps,),
        in_specs=[col(0), col(1), col(2), _halo_prev(1), _halo_prev(2),
                  _row_spec(CONV_TILE), _halo_next(0), _halo_next(0), _vec_spec(8)],
        out_specs=[pl.BlockSpec((CONV_TILE, 3 * D), lambda i: (i, 0)), _vec_spec(8)],
        out_shape=[_sds((S, 3 * D), BF16), _sds((8, D), F32)],
        compiler_params=_cparams(1),
    )(z, z, z, z, z, dpre, dpre, z, cw)


_GU_BLOCK = pl.BlockSpec((2, None, TM, FB), lambda i, j: (0, j, i, 0))


def _gate_up_act(name, a, wg):
    kdim = a.shape[1]

    def body(a_ref, wgate_ref, wup_ref, gu_ref, act_ref):
        x = a_ref[...]
        g = _dot_nn(x, wgate_ref[...])
        u = _dot_nn(x, wup_ref[...])
        gu_ref[0] = g.astype(gu_ref.dtype)
        gu_ref[1] = u.astype(gu_ref.dtype)
        act_ref[...] = (g * jax.nn.sigmoid(g) * u).astype(act_ref.dtype)

    return pl.pallas_call(
        body, name=name, grid=(S // TM, NFB),
        in_specs=[pl.BlockSpec((TM, kdim), lambda i, j: (i, 0)),
                  pl.BlockSpec((None, kdim, FB), lambda i, j: (j, 0, 0)),
                  pl.BlockSpec((None, kdim, FB), lambda i, j: (j + NFB, 0, 0))],
        out_specs=[_GU_BLOCK, pl.BlockSpec((None, TM, FB), lambda i, j: (j, i, 0))],
        out_shape=[_sds((2, NFB, S, FB), BF16), _sds((NFB, S, FB), BF16)],
        compiler_params=_cparams(2),
    )(a, wg, wg)


def _down_dx_act_bwd(name, df, w4, gu, after):
    _, kb, n = w4.shape

    def body(df_ref, w_ref, gu_ref, after_ref, o_ref):
        d = _dot_nt(df_ref[...], w_ref[...])
        g = gu_ref[0].astype(F32)
        u = gu_ref[1].astype(F32)
        sg = jax.nn.sigmoid(g)
        o_ref[0] = (d * u * sg * (1.0 + g * (1.0 - sg))).astype(o_ref.dtype)
        o_ref[1] = (d * g * sg).astype(o_ref.dtype)

    return pl.pallas_call(
        body, name=name, grid=(S // TM, NFB),
        in_specs=[pl.BlockSpec((TM, n), lambda i, j: (i, 0)), pl.BlockSpec((None, kb, n), lambda i, j: (j, 0, 0)),
                  _GU_BLOCK, pl.BlockSpec(memory_space=pl.ANY)],
        out_specs=_GU_BLOCK, out_shape=_sds((2, NFB, S, FB), BF16),
        compiler_params=_cparams(2),
    )(df, w4, gu, after)


def _rope_tables(name, pos_col, inv_freq_row):
    def body(pos_ref, f_ref, cos_ref, sin_ref):
        ang = pos_ref[...].astype(F32) * f_ref[...]
        lane = lax.broadcasted_iota(jnp.int32, ang.shape, 1)
        s = jnp.sin(ang)
        cos_ref[...] = jnp.cos(ang)
        sin_ref[...] = jnp.where((lane % HEAD_DIM) < HEAD_DIM // 2, -s, s)

    tab = pl.BlockSpec((ROW_TILE, 128), lambda i: (i, 0))
    return pl.pallas_call(
        body, name=name, grid=(S // ROW_TILE,),
        in_specs=[pl.BlockSpec((ROW_TILE, 1), lambda i: (i, 0)), _vec_spec(1, 128)],
        out_specs=[tab, tab], out_shape=[_sds((S, 128), F32)] * 2,
        compiler_params=_cparams(1),
    )(pos_col, inv_freq_row)


def _swap_halves(t):
    lane = lax.broadcasted_iota(jnp.int32, t.shape, 1)
    first = (lane % HEAD_DIM) < HEAD_DIM // 2
    return jnp.where(first, pltpu.roll(t, 128 - HEAD_DIM // 2, 1), pltpu.roll(t, HEAD_DIM // 2, 1))


NCHUNK = D // 128


def _chunk(c, base=0):
    return slice(base + c * 128, base + (c + 1) * 128)


def _class_rows(r, d, tm):
    return pl.ds(r, tm // d, stride=d) if d > 1 else slice(None)


def _class_block(d, tm):
    return pl.BlockSpec((tm // d, d * D), lambda i: (i, 0))


def _tokens_from_classes(blk_ref, tmp_ref, d, tm):
    for r in range(d):
        for c in range(NCHUNK):
            tmp_ref[c, _class_rows(r, d, tm), :] = blk_ref[:, _chunk(c, r * D)]


def _classes_from_tokens(tmp_ref, blk_ref, d, tm):
    for r in range(d):
        for c in range(NCHUNK):
            blk_ref[:, _chunk(c, r * D)] = tmp_ref[c, _class_rows(r, d, tm), :].astype(blk_ref.dtype)


def _proj_classes(name, a, w, col, d, tables, scale):
    kdim = a.shape[1]
    rope = tables is not None

    def body(a_ref, w_ref, *refs):
        if rope:
            cos_ref, sin_ref, o_ref, tmp_ref = refs
        else:
            o_ref, tmp_ref = refs
        acc = _dot_nn(a_ref[...], w_ref[...])
        for c in range(NCHUNK):
            tmp_ref[c] = acc[:, _chunk(c)]
        for r in range(d):
            rows = _class_rows(r, d, TM)
            if rope:
                cs = cos_ref[rows, :]
                sn = sin_ref[rows, :]
            for c in range(NCHUNK):
                x = tmp_ref[c, rows, :]
                if rope:
                    x = (x * cs + _swap_halves(x) * sn) * scale
                o_ref[:, _chunk(c, r * D)] = x.astype(o_ref.dtype)

    tab = pl.BlockSpec((TM, 128), lambda i: (i, 0))
    return pl.pallas_call(
        body, name=name, grid=(S // TM,),
        in_specs=[pl.BlockSpec((TM, kdim), lambda i: (i, 0)), pl.BlockSpec((kdim, D), lambda i: (0, col))]
                 + ([tab, tab] if rope else []),
        out_specs=_class_block(d, TM), out_shape=_sds((S // d, d * D), BF16),
        scratch_shapes=[pltpu.VMEM((NCHUNK, TM, 128), F32)],
        compiler_params=_cparams(1),
    )(a, w, *(tables if rope else ()))


ATTN_CHAINS = 4


def _attn_units(d):
    nblk = S // d // BAND
    return max(1, 2 * ATTN_CHAINS // nblk)


def _class_spec(d):
    return pl.BlockSpec((S // d, 128 * _attn_units(d)), lambda cb: (0, cb))


def _dot_nt(a, b):
    return lax.dot_general(a, b, _DIMS["nt"], preferred_element_type=F32)


def _dot_tn(a, b):
    return lax.dot_general(a, b, _DIMS["tn"], preferred_element_type=F32)


def _dot_nn(a, b):
    return lax.dot_general(a, b, _DIMS["nn"], preferred_element_type=F32)


def _band_mask(nkeys):
    qi = lax.broadcasted_iota(jnp.int32, (2 * BAND, nkeys), 0) % BAND
    kj = lax.broadcasted_iota(jnp.int32, (2 * BAND, nkeys), 1)
    if nkeys == BAND:
        return kj <= qi
    dist = qi + BAND - kj
    return (dist >= 0) & (dist <= BAND)


def _stack_heads(x):
    row = lax.broadcasted_iota(jnp.int32, (2 * BAND, 128), 0)
    lane = lax.broadcasted_iota(jnp.int32, (2 * BAND, 128), 1)
    keep = (row < BAND) == (lane < HEAD_DIM)
    return jnp.where(keep, jnp.concatenate([x, x], axis=0), jnp.zeros((), x.dtype))


def _unstack(x2):
    first_head = lax.broadcasted_iota(jnp.int32, (BAND, 128), 1) < HEAD_DIM
    return jnp.where(first_head, x2[:BAND], x2[BAND:])


def _for_later_blocks(nblk, units, fn):
    all_lanes = [slice(u * 128, (u + 1) * 128) for u in range(units)]
    unroll = max(1, ATTN_CHAINS // units)
    trips = (nblk - 1) // unroll
    if trips > 1:
        def step(i, carry):
            for j in range(unroll):
                for lanes in all_lanes:
                    fn(pl.multiple_of((1 + i * unroll + j) * BAND, BAND), lanes)
            return carry

        lax.fori_loop(0, trips, step, 0)
    else:
        trips = 0
    for sb in range(1 + trips * unroll, nblk):
        for lanes in all_lanes:
            fn(sb * BAND, lanes)


def _attn_fwd(name, q, k, v, d):
    nblk = S // d // BAND
    units = _attn_units(d)

    def body(q_ref, k_ref, v_ref, o_ref, lse_ref):
        def block(r0, k0, nkeys, lanes):
            q2 = _stack_heads(q_ref[pl.ds(r0, BAND), lanes])
            s = jnp.where(_band_mask(nkeys), _dot_nt(q2, k_ref[pl.ds(k0, nkeys), lanes]), NEG_INF)
            m = jnp.max(s, axis=-1, keepdims=True)
            p = jnp.exp(s - m)
            l = jnp.sum(p, axis=-1, keepdims=True)
            o2 = _dot_nn(p.astype(BF16), v_ref[pl.ds(k0, nkeys), lanes]) / l
            lse2 = jnp.broadcast_to(m + jnp.log(l), (2 * BAND, 128))
            o_ref[pl.ds(r0, BAND), lanes] = _unstack(o2)
            lse_ref[pl.ds(r0, BAND), lanes] = _unstack(lse2)

        for u in range(units):
            block(0, 0, BAND, slice(u * 128, (u + 1) * 128))

        _for_later_blocks(nblk, units, lambda r0, lanes: block(r0, r0 - BAND, 2 * BAND, lanes))

    spec = _class_spec(d)
    return pl.pallas_call(
        body, name=name, grid=(8 * d // units,),
        in_specs=[spec] * 3, out_specs=[spec] * 2,
        out_shape=[_sds((S // d, d * D), F32)] * 2,
        compiler_params=_cparams(1),
    )(q, k, v)


def _attn_bwd(name, q, k, v, do, lse, dd, d):
    nblk = S // d // BAND
    units = _attn_units(d)

    def body(q_ref, k_ref, v_ref, do_ref, lse_ref, dd_ref, dq_ref, dk_ref, dv_ref):
        def column(ref, r0, lanes):
            rows = pl.ds(r0, BAND)
            first = slice(lanes.start, lanes.start + 1)
            second = slice(lanes.start + HEAD_DIM, lanes.start + HEAD_DIM + 1)
            return jnp.concatenate([ref[rows, first], ref[rows, second]], axis=0)

        def block(r0, k0, nkeys, lanes, first):
            q2 = _stack_heads(q_ref[pl.ds(r0, BAND), lanes])
            do2 = _stack_heads(do_ref[pl.ds(r0, BAND), lanes])
            kk = k_ref[pl.ds(k0, nkeys), lanes]
            vv = v_ref[pl.ds(k0, nkeys), lanes]
            s = jnp.where(_band_mask(nkeys), _dot_nt(q2, kk), NEG_INF)
            p = jnp.exp(s - column(lse_ref, r0, lanes))
            ds = (p * (_dot_nt(do2, vv) - column(dd_ref, r0, lanes))).astype(BF16)
            dq_ref[pl.ds(r0, BAND), lanes] = _unstack(_dot_nn(ds, kk))
            dk_part = _dot_tn(ds, q2)
            dv_part = _dot_tn(p.astype(BF16), do2)
            if first:
                dk_ref[pl.ds(k0, nkeys), lanes] = dk_part
                dv_ref[pl.ds(k0, nkeys), lanes] = dv_part
            else:
                dk_ref[pl.ds(k0, BAND), lanes] += dk_part[:BAND]
                dv_ref[pl.ds(k0, BAND), lanes] += dv_part[:BAND]
                dk_ref[pl.ds(k0 + BAND, BAND), lanes] = dk_part[BAND:]
                dv_ref[pl.ds(k0 + BAND, BAND), lanes] = dv_part[BAND:]

        for u in range(units):
            block(0, 0, BAND, slice(u * 128, (u + 1) * 128), True)

        _for_later_blocks(nblk, units, lambda r0, lanes: block(r0, r0 - BAND, 2 * BAND, lanes, False))

    spec = _class_spec(d)
    return pl.pallas_call(
        body, name=name, grid=(8 * d // units,),
        in_specs=[spec] * 6, out_specs=[spec] * 3,
        out_shape=[_sds((S // d, d * D), F32)] * 3,
        compiler_params=_cparams(1),
    )(q, k, v, do, lse, dd)


MIX_TILE = 256
DILATIONS = tuple(d for _, d in BRANCHES)


def _branch_weights(la, lb, lc):
    m = jnp.maximum(jnp.maximum(la, lb), lc)
    ea, eb, ec = jnp.exp(la - m), jnp.exp(lb - m), jnp.exp(lc - m)
    den = ea + eb + ec
    return ea / den, eb / den, ec / den


def _mix_operands(outs, lses):
    specs = [_class_block(d, MIX_TILE) for d in DILATIONS] * 2
    scratch = [pltpu.VMEM((NCHUNK, MIX_TILE, 128), F32)] * 4
    return list(outs) + list(lses), specs, scratch


def _mix_fwd(name, outs, lses):
    def body(o0, o1, o2, l0, l1, l2, o_ref, to1, to2, tl1, tl2):
        for blk, tmp, d in ((o1, to1, DILATIONS[1]), (o2, to2, DILATIONS[2]), (l1, tl1, DILATIONS[1]), (l2, tl2, DILATIONS[2])):
            _tokens_from_classes(blk, tmp, d, MIX_TILE)
        for c in range(NCHUNK):
            wa, wb, wc = _branch_weights(l0[:, _chunk(c)], tl1[c], tl2[c])
            o_ref[:, _chunk(c)] = (wa * o0[:, _chunk(c)] + wb * to1[c] + wc * to2[c]).astype(o_ref.dtype)

    operands, specs, scratch = _mix_operands(outs, lses)
    return pl.pallas_call(
        body, name=name, grid=(S // MIX_TILE,),
        in_specs=specs, out_specs=_row_spec(MIX_TILE), out_shape=_sds((S, D), BF16),
        scratch_shapes=scratch, compiler_params=_cparams(1),
    )(*operands)


def _head_sum(x, ones_blockdiag):
    hi = x.astype(BF16)
    r1 = x - hi.astype(F32)
    mid = r1.astype(BF16)
    lo = (r1 - mid.astype(F32)).astype(BF16)
    return _dot_nn(hi, ones_blockdiag) + _dot_nn(mid, ones_blockdiag) + _dot_nn(lo, ones_blockdiag)


def _mix_bwd(name, do, outs, lses, ones_blockdiag):
    def body(do_ref, o0, o1, o2, l0, l1, l2, ones_ref, d0, d1, d2, t0, t1, t2,
             to1, to2, tl1, tl2, td1, td2, tt1, tt2):
        for blk, tmp, d in ((o1, to1, DILATIONS[1]), (o2, to2, DILATIONS[2]), (l1, tl1, DILATIONS[1]), (l2, tl2, DILATIONS[2])):
            _tokens_from_classes(blk, tmp, d, MIX_TILE)
        ones = ones_ref[...]
        for c in range(NCHUNK):
            w = _branch_weights(l0[:, _chunk(c)], tl1[c], tl2[c])
            dov = do_ref[:, _chunk(c)]
            o = w[0] * o0[:, _chunk(c)] + w[1] * to1[c] + w[2] * to2[c]
            t = _head_sum(dov * o, ones)
            d0[:, _chunk(c)] = (w[0] * dov).astype(d0.dtype)
            t0[:, _chunk(c)] = w[0] * t
            td1[c], tt1[c] = w[1] * dov, w[1] * t
            td2[c], tt2[c] = w[2] * dov, w[2] * t
        for tmp, blk, d in ((td1, d1, DILATIONS[1]), (tt1, t1, DILATIONS[1]), (td2, d2, DILATIONS[2]), (tt2, t2, DILATIONS[2])):
            _classes_from_tokens(tmp, blk, d, MIX_TILE)

    operands, specs, scratch = _mix_operands(outs, lses)
    out_specs = [_class_block(d, MIX_TILE) for d in DILATIONS] * 2
    out_shape = [_sds((S // d, d * D), BF16) for d in DILATIONS] + [_sds((S // d, d * D), F32) for d in DILATIONS]
    return pl.pallas_call(
        body, name=name, grid=(S // MIX_TILE,),
        in_specs=[_row_spec(MIX_TILE)] + specs + [_vec_spec(128, 128)],
        out_specs=out_specs, out_shape=out_shape,
        scratch_shapes=scratch + [pltpu.VMEM((NCHUNK, MIX_TILE, 128), F32)] * 4,
        compiler_params=_cparams(1),
    )(do, *operands, ones_blockdiag)


def _attn_bwd_post(name, grads, cos_t, sin_t):
    tm = MIX_TILE
    scale = HEAD_DIM ** -0.5

    def unrope(x, cs, sn):
        return x * cs - _swap_halves(x) * sn

    def body(*refs):
        in_refs = refs[:9]
        cos_ref, sin_ref, dq_ref, dkv_ref, tmp_ref = refs[9:]
        cs = cos_ref[...]
        sn = sin_ref[...]
        for g, d in enumerate(DILATIONS):
            for which, blk in enumerate(in_refs[3 * g:3 * g + 3]):
                if d > 1:
                    _tokens_from_classes(blk, tmp_ref, d, tm)
                for c in range(NCHUNK):
                    x = tmp_ref[c] if d > 1 else blk[:, _chunk(c)]
                    if which == 0:
                        dq_ref[:, _chunk(c, g * D)] = (unrope(x, cs, sn) * scale).astype(dq_ref.dtype)
                    elif which == 1:
                        dkv_ref[:, _chunk(c, g * D)] = unrope(x, cs, sn).astype(dkv_ref.dtype)
                    else:
                        dkv_ref[:, _chunk(c, QW + g * D)] = x.astype(dkv_ref.dtype)

    operands = [a for branch in grads for a in branch]
    tab = pl.BlockSpec((tm, 128), lambda i: (i, 0))
    return pl.pallas_call(
        body, name=name, grid=(S // tm,),
        in_specs=[_class_block(d, tm) for d in DILATIONS for _ in range(3)] + [tab, tab],
        out_specs=[pl.BlockSpec((tm, QW), lambda i: (i, 0)), pl.BlockSpec((tm, 2 * QW), lambda i: (i, 0))],
        out_shape=[_sds((S, QW), BF16), _sds((S, 2 * QW), BF16)],
        scratch_shapes=[pltpu.VMEM((NCHUNK, tm, 128), F32)],
        compiler_params=_cparams(1),
    )(*operands, cos_t, sin_t)


def _adamw(name, parts, w, m, v):
    n, rows, cols = parts.shape
    tr = rows
    for cand in (256, 176, 128, 64, 32, 16, 8):
        if rows % cand == 0:
            tr = cand
            break
    def body(p_ref, w_ref, m_ref, v_ref, g_ref, d_ref, nm_ref, nv_ref):
        g = p_ref[0].astype(F32)
        for j in range(1, n):
            g = g + p_ref[j].astype(F32)
        g_ref[...] = g
        d_ref[...], nm_ref[...], nv_ref[...] = _adam_update(g, w_ref[...], m_ref[...], v_ref[...])

    blk = pl.BlockSpec((tr, cols), lambda i: (i, 0))
    return pl.pallas_call(
        body, name=name, grid=(rows // tr,),
        in_specs=[pl.BlockSpec((n, tr, cols), lambda i: (0, i, 0)), blk, blk, blk],
        out_specs=[blk] * 4, out_shape=[_sds((rows, cols), F32)] * 4,
        compiler_params=_cparams(1),
    )(parts, w, m, v)


def _adam_update(g, w, m, v):
    c1 = 1.0 / (1.0 - ADAM_B1 ** ADAM_STEP)
    c2 = 1.0 / (1.0 - ADAM_B2 ** ADAM_STEP)
    nm = ADAM_B1 * m + (1.0 - ADAM_B1) * g
    nv = ADAM_B2 * v + (1.0 - ADAM_B2) * (g * g)
    return -ADAM_LR * ((nm * c1) / (jnp.sqrt(nv * c2) + ADAM_EPS) + ADAM_WD * w), nm, nv


GAIN_ROWS = 16


def _pack_small(name, gain_tiles, taps):
    ng = len(gain_tiles)

    def body(*refs):
        o_ref = refs[-1]
        o_ref[...] = jnp.zeros_like(o_ref)
        for i in range(ng):
            o_ref[i:i + 1, :] = refs[i][0:1, :]
        o_ref[ng:ng + 3, :] = refs[ng][0:3, :]

    return pl.pallas_call(body, name=name, out_shape=_sds((GAIN_ROWS, D), F32))(*gain_tiles, taps)


def _adamw_gains(name, parts, params):
    np_ = len(params)
    shapes = [w.shape for w, _, _ in params]

    def body(p_ref, *refs):
        ins, outs = refs[:3 * np_], refs[3 * np_:]

        def total(lo, rows):
            g = p_ref[0, lo:lo + rows, :]
            for j in range(1, NDEV):
                g = g + p_ref[j, lo:lo + rows, :]
            return g

        lo = 0
        for i, shape in enumerate(shapes):
            g = total(lo, shape[0])
            lo += shape[0]
            w_ref, m_ref, v_ref = ins[3 * i:3 * i + 3]
            g_ref, d_ref, nm_ref, nv_ref = outs[4 * i:4 * i + 4]
            g_ref[...] = g
            d_ref[...], nm_ref[...], nv_ref[...] = _adam_update(g, w_ref[...], m_ref[...], v_ref[...])
        taps_ref = outs[-1]
        taps_ref[...] = jnp.zeros_like(taps_ref)
        taps_ref[0:3, :] = total(lo, 3)

    out_shape = [_sds(shape, F32) for shape in shapes for _ in range(4)] + [_sds((8, D), F32)]
    outs = pl.pallas_call(body, name=name, out_shape=out_shape)(parts, *[a for p in params for a in p])
    return [list(outs[4 * i:4 * i + 4]) for i in range(np_)], outs[-1]


def _exchange(name, arrays, kind):
    n = len(arrays)
    gather = kind == "gather"
    out_shape = [_sds((NDEV,) + a.shape if gather else a.shape, a.dtype) for a in arrays]

    def body(*refs):
        srcs, outs = refs[:n], refs[n:2 * n]
        send_sems, recv_sems, local_sems = refs[2 * n:]
        x, y, c = lax.axis_index("x"), lax.axis_index("y"), lax.axis_index("c")
        me = 4 * x + 2 * y + c
        pending = []
        for t in range(n):
            own = pltpu.make_async_copy(srcs[t] if gather else srcs[t].at[me], outs[t].at[me], local_sems.at[t])
            own.start()
            pending.append(own)
            for rel in range(1, NDEV):
                px = 1 - x if rel & 4 else x
                py = 1 - y if rel & 2 else y
                pc = 1 - c if rel & 1 else c
                peer = 4 * px + 2 * py + pc
                send = pltpu.make_async_remote_copy(
                    src_ref=srcs[t] if gather else srcs[t].at[peer], dst_ref=outs[t].at[me],
                    send_sem=send_sems.at[t, rel - 1], recv_sem=recv_sems.at[t, rel - 1],
                    device_id=(px, py, pc), device_id_type=MESH)
                send.start()
                arrive = pltpu.make_async_remote_copy(
                    src_ref=srcs[t] if gather else srcs[t].at[me], dst_ref=outs[t].at[peer],
                    send_sem=send_sems.at[t, rel - 1], recv_sem=recv_sems.at[t, rel - 1],
                    device_id=(px, py, pc), device_id_type=MESH)
                pending.append((send, arrive))
        for item in pending:
            if isinstance(item, tuple):
                item[0].wait_send()
                item[1].wait_recv()
            else:
                item.wait()

    any_spec = pl.BlockSpec(memory_space=pl.ANY)
    outs = pl.pallas_call(
        body, name=name,
        in_specs=[any_spec] * n, out_specs=[any_spec] * n, out_shape=out_shape,
        scratch_shapes=[pltpu.SemaphoreType.DMA((n, NDEV - 1)), pltpu.SemaphoreType.DMA((n, NDEV - 1)),
                        pltpu.SemaphoreType.DMA((n,))],
    )(*arrays)
    return list(outs)


_HBM_SPEC = pl.BlockSpec(memory_space=pltpu.HBM)
_SEM_SPEC = pl.BlockSpec(memory_space=pltpu.SEMAPHORE)
_DATAFLOW = pltpu.SideEffectType.DATAFLOW_SIDE_EFFECTING


def _peers():
    x, y, c = lax.axis_index("x"), lax.axis_index("y"), lax.axis_index("c")
    out = []
    for rel in range(1, NDEV):
        px = 1 - x if rel & 4 else x
        py = 1 - y if rel & 2 else y
        pc = 1 - c if rel & 1 else c
        out.append((rel - 1, (px, py, pc), 4 * px + 2 * py + pc))
    return 4 * x + 2 * y + c, out


def _hbm(a):
    return pltpu.HBM(a.shape, a.dtype)


def _own_slot(a, me, kind):
    mine = a[None] if kind == "gather" else lax.dynamic_slice_in_dim(a, me, 1, axis=0)
    shape = (NDEV,) + mine.shape[1:]
    return lax.dynamic_update_slice_in_dim(lax.empty(shape, a.dtype), mine, me, axis=0)


def _exchange_start(name, arrays, me, kind):
    n = len(arrays)
    gather = kind == "gather"
    lands = [_own_slot(a, me, kind) for a in arrays]

    def body(*refs):
        src_refs, land_refs = refs[:n], refs[n:2 * n]
        send_sems, recv_sems = refs[2 * n], refs[2 * n + 1]
        token = refs[-1]
        my_block, peers = _peers()
        for t in range(n):
            for slot, dev, block in peers:
                pltpu.make_async_remote_copy(
                    src_ref=src_refs[t] if gather else src_refs[t].at[block], dst_ref=land_refs[t].at[my_block],
                    send_sem=send_sems.at[t * (NDEV - 1) + slot], recv_sem=recv_sems.at[t * (NDEV - 1) + slot],
                    device_id=dev, device_id_type=MESH).start()
        token[...] = jnp.zeros_like(token)

    operands = [pltpu.with_memory_space_constraint(a, pltpu.HBM) for a in list(arrays) + lands]
    outs = pl.pallas_call(
        body, name=name,
        out_shape=(pltpu.SemaphoreType.DMA((n * (NDEV - 1),)), pltpu.SemaphoreType.DMA((n * (NDEV - 1),)),
                   *[_hbm(a) for a in operands], _sds((8, 128), F32)),
        in_specs=[_HBM_SPEC] * (2 * n),
        out_specs=(_SEM_SPEC, _SEM_SPEC, *[_HBM_SPEC] * (2 * n), pl.BlockSpec(memory_space=pltpu.VMEM)),
        input_output_aliases={i: 2 + i for i in range(2 * n)},
        compiler_params=pltpu.CompilerParams(has_side_effects=_DATAFLOW),
    )(*operands)
    return (outs[0], outs[1], list(outs[2:2 + n]), list(outs[2 + n:2 + 2 * n])), outs[-1]


def _exchange_wait(name, started, t, after, kind):
    send_sems, recv_sems, srcs, lands = started
    gather = kind == "gather"

    def body(src_ref, land_ref, send_ref, recv_ref, after_ref, src_out, land_out):
        _, peers = _peers()
        for slot, dev, block in peers:
            copy = pltpu.make_async_remote_copy(
                src_ref=src_ref if gather else src_ref.at[block], dst_ref=land_ref.at[block],
                send_sem=send_ref.at[t * (NDEV - 1) + slot], recv_sem=recv_ref.at[t * (NDEV - 1) + slot],
                device_id=dev, device_id_type=MESH)
            copy.wait_send()
            copy.wait_recv()

    return pl.pallas_call(
        body, name=name, out_shape=(_hbm(srcs[t]), _hbm(lands[t])),
        in_specs=(_HBM_SPEC, _HBM_SPEC, _SEM_SPEC, _SEM_SPEC, pl.BlockSpec(memory_space=pl.ANY)),
        out_specs=(_HBM_SPEC, _HBM_SPEC), input_output_aliases={0: 0, 1: 1},
        compiler_params=pltpu.CompilerParams(has_side_effects=_DATAFLOW),
    )(srcs[t], lands[t], send_sems, recv_sems, after)[1]


def _ffn_fwd(tag, n, weight):
    wg = weight(f"gate_up_{tag}", n)
    gu, act = _gate_up_act(f"ffn_gate_up_{tag}", n, wg)
    wd4 = weight(f"down_{tag}", act).reshape(NFB, FB, D)
    f = _fwd_kblocked(f"ffn_down_{tag}", act, wd4)
    return (n, gu, act, wg, wd4), f


def _ffn_bwd(tag, dh_out, h_in, f, saved, g_pre, g_post, send):
    n, gu, act, wg, wd4 = saved
    df, (dg_post,) = _rms_bwd(f"ffn_postnorm_bwd_{tag}", f, [(g_post, dh_out)], None, BF16)
    tok = send(f"down_{tag}", _bwd_w_kblocked(f"ffn_down_dw_{tag}", act, df).reshape(NDEV, DFF // NDEV, D))
    dgu = _down_dx_act_bwd(f"ffn_down_dx_{tag}", df, wd4, gu, tok).reshape(NDEV, S, FB)
    tok = send(f"gate_up_{tag}", _bwd_w_cols_blocked(f"ffn_gate_up_dw_{tag}", n, dgu))
    dn = _bwd_x_cols_blocked(f"ffn_gate_up_dx_{tag}", dgu, wg, after=tok)
    dh_in, (dg_pre,) = _rms_bwd(f"ffn_prenorm_bwd_{tag}", h_in, [(g_pre, dn)], dh_out, F32)
    return dh_in, dg_pre, dg_post


def kernel(x, positions, mix_norm_pre, mix_norm_post, ffn_norm_pre, ffn_norm_post, ffn_w_gate_up, ffn_w_down, conv_w_in, conv_w, conv_w_out, kv_norm, w_kv, w_q, w_o, loss_target, m_mix_norm_pre, m_mix_norm_post, m_ffn_norm_pre, m_ffn_norm_post, m_ffn_w_gate_up, m_ffn_w_down, m_conv_w_in, m_conv_w, m_conv_w_out, m_kv_norm, m_w_kv, m_w_q, m_w_o, v_mix_norm_pre, v_mix_norm_post, v_ffn_norm_pre, v_ffn_norm_post, v_ffn_w_gate_up, v_ffn_w_down, v_conv_w_in, v_conv_w, v_conv_w_out, v_kv_norm, v_w_kv, v_w_q, v_w_o):
    me = 4 * lax.axis_index("x") + 2 * lax.axis_index("y") + lax.axis_index("c")
    h0 = x.reshape(S, D)
    target = loss_target.reshape(S, D)
    row = lambda a, l: a[l].reshape(1, D)
    g_kv = kv_norm.reshape(1, D)

    cw_shard = jnp.pad(conv_w[0], ((0, 5), (0, 0)))
    names = ["conv_in", "conv_w", "conv_out", "gate_up_0", "down_0", "kv", "q", "o", "gate_up_1", "down_1"]
    shards = [conv_w_in[0], cw_shard, conv_w_out[0], ffn_w_gate_up[0], ffn_w_down[0],
              w_kv, w_q[0], w_o[0], ffn_w_gate_up[1], ffn_w_down[1]]
    shards = [s if n == "conv_w" else s.astype(BF16) for n, s in zip(names, shards)]
    gather, _ = _exchange_start("gather_weights_start", shards, me, "gather")

    def weight(name, after):
        return _exchange_wait(f"gather_wait_{name}", gather, names.index(name), after, "gather")

    sent = {}

    def send(name, grad):
        sent[name], token = _exchange_start(f"scatter_start_{name}", [grad], me, "scatter")
        return token

    n0 = _rms_fwd("mix_prenorm_0", h0, [row(mix_norm_pre, 0)])[0]
    win_g = weight("conv_in", n0)
    cw = weight("conv_w", n0).transpose(1, 0, 2).reshape(8, D)
    z = _fwd_cols("conv_in", n0, win_g)
    pre = _conv_fwd("conv_gate", z, cw)
    wout = weight("conv_out", pre).reshape(D, D)
    y0 = _fwd_rows("conv_out", pre, wout)
    h1, (n1,) = _resid_rms("mix_postnorm_0", h0, y0, row(mix_norm_post, 0), [row(ffn_norm_pre, 0)])
    ffn0, f0 = _ffn_fwd("0", n1, weight)
    h2, (nk, n2) = _resid_rms("ffn_postnorm_0", h1, f0, row(ffn_norm_post, 0), [g_kv, row(mix_norm_pre, 1)])

    wkv_g = weight("kv", nk)
    wkv = wkv_g.transpose(1, 0, 2).reshape(D, 2 * QW)
    half = HEAD_DIM // 2
    inv_freq = ROPE_THETA ** (-jnp.arange(half, dtype=F32) / half)
    tables = _rope_tables("rope_tables", positions.reshape(S, 1), jnp.tile(inv_freq, 4).reshape(1, 128))
    qc, kc, vc, o_c, lse_c = [], [], [], [], []
    for g, d in enumerate(DILATIONS):
        kc.append(_proj_classes(f"k_proj_{g}", nk, wkv, g, d, tables, 1.0))
        vc.append(_proj_classes(f"v_proj_{g}", nk, wkv, len(DILATIONS) + g, d, None, None))
    wq_g = weight("q", vc[-1])
    wq = wq_g.transpose(1, 0, 2).reshape(D, QW)
    for g, d in enumerate(DILATIONS):
        qc.append(_proj_classes(f"q_proj_{g}", n2, wq, g, d, tables, HEAD_DIM ** -0.5))
        o_g, lse_g = _attn_fwd(f"attn_fwd_{g}", qc[g], kc[g], vc[g], d)
        o_c.append(o_g)
        lse_c.append(lse_g)
    o_mix = _mix_fwd("attn_mix", o_c, lse_c)
    wo = weight("o", o_mix).reshape(D, D)
    y1 = _fwd_rows("attn_out", o_mix, wo)
    h3, (n3,) = _resid_rms("mix_postnorm_1", h2, y1, row(mix_norm_post, 1), [row(ffn_norm_pre, 1)])
    ffn1, f1 = _ffn_fwd("1", n3, weight)

    dh4, sq = _resid_rms_loss("ffn_postnorm_1_loss", h3, f1, row(ffn_norm_post, 1), target)
    loss = lax.psum(jnp.sum(sq) * (0.5 / D), ("x", "y", "c"))

    dh3, dg_fpre1, dg_fpost1 = _ffn_bwd(
        "1", dh4, h3, f1, ffn1, row(ffn_norm_pre, 1), row(ffn_norm_post, 1), send)
    dy1, (dg_mpost1,) = _rms_bwd("mix_postnorm_bwd_1", y1, [(row(mix_norm_post, 1), dh3)], None, BF16)
    tok = send("o", _bwd_w_rows("attn_out_dw", o_mix, dy1).reshape(NDEV, D // NDEV, D))
    do = _bwd_x_rows("attn_out_dx", dy1, wo, F32, after=tok)
    lane = jnp.arange(128)
    ones_blockdiag = (lane[:, None] // HEAD_DIM == lane[None, :] // HEAD_DIM).astype(BF16)
    mixed = _mix_bwd("attn_mix_bwd", do, o_c, lse_c, ones_blockdiag)
    branch_grads = [_attn_bwd(f"attn_bwd_{g}", qc[g], kc[g], vc[g], mixed[g], lse_c[g], mixed[3 + g], d)
                    for g, d in enumerate(DILATIONS)]
    dq_raw, dkv = _attn_bwd_post("attn_bwd_post", branch_grads, *tables)
    tok = send("kv", _bwd_w_cols("kv_proj_dw", nk, dkv, 2 * QW // NDEV))
    dnk = _bwd_x_plain("kv_proj_dx", dkv, wkv, 2, after=tok)
    tok = send("q", _bwd_w_cols("q_proj_dw", n2, dq_raw, QW // NDEV))
    dn2 = _bwd_x_plain("q_proj_dx", dq_raw, wq, 1, after=tok)
    dh2, (dg_kv, dg_mpre1) = _rms_bwd("kv_and_mix_prenorm_bwd_1", h2,
                                      [(g_kv, dnk), (row(mix_norm_pre, 1), dn2)], dh3, F32)

    dh1, dg_fpre0, dg_fpost0 = _ffn_bwd(
        "0", dh2, h1, f0, ffn0, row(ffn_norm_pre, 0), row(ffn_norm_post, 0), send)
    dy0, (dg_mpost0,) = _rms_bwd("mix_postnorm_bwd_0", y0, [(row(mix_norm_post, 0), dh1)], None, BF16)
    tok = send("conv_out", _bwd_w_rows("conv_out_dw", pre, dy0).reshape(NDEV, D // NDEV, D))
    dpre = _bwd_x_rows("conv_out_dx", dy0, wout, BF16, after=tok)
    dz, dcw = _conv_bwd("conv_gate_bwd", z, dpre, cw)
    tok = send("conv_in", _bwd_w_cols("conv_in_dw", n0, dz, 3 * D // NDEV))
    dn0 = _bwd_x_plain("conv_in_dx", dz, win_g.transpose(1, 0, 2).reshape(D, 3 * D), 1, after=tok)
    dh0, (dg_mpre0,) = _rms_bwd("mix_prenorm_bwd_0", h0, [(row(mix_norm_pre, 0), dn0)], dh1, F32)

    small = _pack_small("pack_small_grads", [dg_mpre0, dg_mpre1, dg_mpost0, dg_mpost1, dg_fpre0, dg_fpre1,
                                             dg_fpost0, dg_fpost1, dg_kv], dcw)
    small_all = _exchange("gather_small_grads", [small], "gather")[0]

    done = [small_all]

    def upd(tag, w, m, v):
        parts = _exchange_wait(f"scatter_wait_{tag}", sent[tag], 0, done[-1], "scatter")
        shape = w.shape
        flat = lambda a: a.reshape(parts.shape[1:])
        res = _adamw(f"adamw_{tag}", parts, flat(w), flat(m), flat(v))
        done.append(res[0])
        return [r.reshape(shape) for r in res]

    def upd_layer(tag, l, w, m, v):
        return upd(f"{tag}_{l}", w[l], m[l], v[l])

    def stack(per_layer):
        return [jnp.stack([per_layer[0][i], per_layer[1][i]]) for i in range(4)]

    vec = lambda a: a.reshape(1, D)
    gain_res, taps = _adamw_gains("adamw_gains", small_all, [
        (mix_norm_pre, m_mix_norm_pre, v_mix_norm_pre), (mix_norm_post, m_mix_norm_post, v_mix_norm_post),
        (ffn_norm_pre, m_ffn_norm_pre, v_ffn_norm_pre), (ffn_norm_post, m_ffn_norm_post, v_ffn_norm_post),
        (vec(kv_norm), vec(m_kv_norm), vec(v_kv_norm))])
    dcw_mine = lax.dynamic_slice(taps, (0, me * 128), (8, 128))
    pad8 = lambda a, fill: jnp.pad(a[0], ((0, 5), (0, 0)), constant_values=fill)
    cw_res = [r[0:3].reshape(1, 3, 128) for r in
              _adamw("adamw_conv_w", dcw_mine.reshape(1, 8, 128), cw_shard, pad8(m_conv_w, 0.0), pad8(v_conv_w, 1.0))]

    res = {
        "mix_norm_pre": gain_res[0],
        "mix_norm_post": gain_res[1],
        "ffn_norm_pre": gain_res[2],
        "ffn_norm_post": gain_res[3],
        "kv_norm": [r.reshape(D) for r in gain_res[4]],
        "conv_w": cw_res,
    }
    down, gate_up = {}, {}
    down[1] = upd_layer("down", 1, ffn_w_down, m_ffn_w_down, v_ffn_w_down)
    gate_up[1] = upd_layer("gate_up", 1, ffn_w_gate_up, m_ffn_w_gate_up, v_ffn_w_gate_up)
    res["w_o"] = upd("o", w_o, m_w_o, v_w_o)
    res["w_q"] = upd("q", w_q, m_w_q, v_w_q)
    res["w_kv"] = upd("kv", w_kv, m_w_kv, v_w_kv)
    down[0] = upd_layer("down", 0, ffn_w_down, m_ffn_w_down, v_ffn_w_down)
    gate_up[0] = upd_layer("gate_up", 0, ffn_w_gate_up, m_ffn_w_gate_up, v_ffn_w_gate_up)
    res["ffn_w_down"] = stack(down)
    res["ffn_w_gate_up"] = stack(gate_up)
    res["conv_w_out"] = upd("conv_out", conv_w_out, m_conv_w_out, v_conv_w_out)
    res["conv_w_in"] = upd("conv_in", conv_w_in, m_conv_w_in, v_conv_w_in)
    order = ["mix_norm_pre", "mix_norm_post", "ffn_norm_pre", "ffn_norm_post", "ffn_w_gate_up", "ffn_w_down",
             "conv_w_in", "conv_w", "conv_w_out", "kv_norm", "w_kv", "w_q", "w_o"]
    out = [loss, dh0.reshape(1, S, D)]
    for i in range(4):
        out += [res[name][i] for name in order]
    return tuple(out)
```

```python
import jax
import jax.numpy as jnp
from jax import lax
from jax.experimental import pallas as pl
from jax.experimental.pallas import tpu as pltpu

F32 = jnp.float32
BF16 = jnp.bfloat16

S = 4096
D = 1024
NDEV = 8
HEAD_DIM = 64
QW = 3072
DFF = 2816
FB = 704
NFB = 4
BRANCHES = ((128, 1), (512, 4), (2048, 16))
BAND = 128
ROPE_THETA = 10000.0
RMS_EPS = 1e-6
NEG_INF = -1e30
ADAM_LR, ADAM_B1, ADAM_B2, ADAM_EPS, ADAM_WD, ADAM_STEP = 0.001, 0.9, 0.999, 1e-08, 0.01, 10

VMEM_LIMIT_BYTES = 52 * 1024 * 1024
ROW_TILE = 512
MESH = pl.DeviceIdType.MESH


def _cparams(ngrid):
    return pltpu.CompilerParams(dimension_semantics=("arbitrary",) * ngrid,
                                vmem_limit_bytes=VMEM_LIMIT_BYTES)


def _sds(shape, dtype):
    return jax.ShapeDtypeStruct(tuple(shape), dtype)


_DIMS = {"nn": (((1,), (0,)), ((), ())),
         "nt": (((1,), (1,)), ((), ())),
         "tn": (((0,), (0,)), ((), ()))}


def _matmul(name, a, b, *, mode, grid, a_blk, a_map, b_blk, b_map, o_shape, o_blk, o_map, out_dtype, after=None,
            out_groups=1):
    nk = grid[2]
    dims = _DIMS[mode]
    acc_shape = tuple(s for s in o_blk if s is not None)
    if out_groups > 1:
        acc_shape = (acc_shape[1], out_groups * acc_shape[2])
    extra = [] if after is None else [after]

    def store(o_ref, val):
        if out_groups == 1:
            o_ref[...] = val.astype(o_ref.dtype)
        else:
            n = o_ref.shape[-1]
            for grp in range(out_groups):
                o_ref[grp] = val[:, grp * n:(grp + 1) * n].astype(o_ref.dtype)

    def body(a_ref, b_ref, *rest):
        o_ref, scratch = rest[len(extra)], rest[len(extra) + 1:]
        part = lax.dot_general(a_ref[...], b_ref[...], dims, preferred_element_type=F32)
        if nk == 1:
            store(o_ref, part)
            return
        acc_ref = scratch[0]
        k = pl.program_id(2)

        @pl.when(k == 0)
        def _():
            acc_ref[...] = part

        @pl.when(k > 0)
        def _():
            acc_ref[...] += part

        @pl.when(k == nk - 1)
        def _():
            store(o_ref, acc_ref[...])

    return pl.pallas_call(
        body, name=name, grid=grid,
        in_specs=[pl.BlockSpec(a_blk, a_map), pl.BlockSpec(b_blk, b_map)] + [pl.BlockSpec(memory_space=pl.ANY)] * len(extra),
        out_specs=pl.BlockSpec(o_blk, o_map),
        out_shape=_sds(o_shape, out_dtype),
        scratch_shapes=[] if nk == 1 else [pltpu.VMEM(acc_shape, F32)],
        compiler_params=_cparams(3),
    )(a, b, *extra)


TM = 1024
TK = S


def _fwd_cols(name, a, wg, out_dtype=BF16):
    _, kdim, n = wg.shape
    return _matmul(name, a, wg, mode="nn", grid=(S // TM, NDEV, 1),
                   a_blk=(TM, kdim), a_map=lambda i, j, k: (i, 0),
                   b_blk=(None, kdim, n), b_map=lambda i, j, k: (j, 0, 0),
                   o_shape=(S, NDEV * n), o_blk=(TM, n), o_map=lambda i, j, k: (i, j), out_dtype=out_dtype)


def _fwd_rows(name, a, w, out_dtype=F32):
    kdim, n = w.shape
    tn = 512
    return _matmul(name, a, w, mode="nn", grid=(S // TM, n // tn, 1),
                   a_blk=(TM, kdim), a_map=lambda i, j, k: (i, 0),
                   b_blk=(kdim, tn), b_map=lambda i, j, k: (0, j),
                   o_shape=(S, n), o_blk=(TM, tn), o_map=lambda i, j, k: (i, j), out_dtype=out_dtype)


def _fwd_kblocked(name, a4, w4):
    nb, _, kb = a4.shape
    n = w4.shape[2]

    def body(a_ref, w_ref, o_ref):
        acc = _dot_nn(a_ref[0], w_ref[0])
        for j in range(1, nb):
            acc = acc + _dot_nn(a_ref[j], w_ref[j])
        o_ref[...] = acc

    return pl.pallas_call(
        body, name=name, grid=(S // TM,),
        in_specs=[pl.BlockSpec((nb, TM, kb), lambda i: (0, i, 0)), pl.BlockSpec((nb, kb, n), lambda i: (0, 0, 0))],
        out_specs=pl.BlockSpec((TM, n), lambda i: (i, 0)), out_shape=_sds((S, n), F32),
        compiler_params=_cparams(1),
    )(a4, w4)


def _bwd_x_cols_blocked(name, dy8, wg, after):
    _, kdim, n = wg.shape
    nk = NDEV // 2

    def body(a_ref, b_ref, after_ref, o_ref, acc_ref):
        k = pl.program_id(1)
        part = _dot_nt(a_ref[0], b_ref[0]) + _dot_nt(a_ref[1], b_ref[1])

        @pl.when(k == 0)
        def _():
            acc_ref[...] = part

        @pl.when(k > 0)
        def _():
            acc_ref[...] += part

        @pl.when(k == nk - 1)
        def _():
            o_ref[...] = acc_ref[...]

    return pl.pallas_call(
        body, name=name, grid=(S // TM, nk),
        in_specs=[pl.BlockSpec((2, None, TM, n), lambda i, k: (0, k, i, 0)),
                  pl.BlockSpec((2, None, kdim, n), lambda i, k: (0, k, 0, 0)),
                  pl.BlockSpec(memory_space=pl.ANY)],
        out_specs=pl.BlockSpec((TM, kdim), lambda i, k: (i, 0)), out_shape=_sds((S, kdim), F32),
        scratch_shapes=[pltpu.VMEM((TM, kdim), F32)],
        compiler_params=_cparams(2),
    )(dy8.reshape(2, nk, S, n), wg.reshape(2, nk, kdim, n), after)


def _bwd_x_rows(name, dy, w, out_dtype, after=None):
    kdim, n = w.shape
    tkk = 512
    return _matmul(name, dy, w, mode="nt", grid=(S // TM, kdim // tkk, 1),
                   a_blk=(TM, n), a_map=lambda i, j, k: (i, 0),
                   b_blk=(tkk, n), b_map=lambda i, j, k: (j, 0),
                   o_shape=(S, kdim), o_blk=(TM, tkk), o_map=lambda i, j, k: (i, j), out_dtype=out_dtype, after=after)


DW_COLS = 768


def _bwd_w_cols(name, a, dy, n):
    kdim = a.shape[1]
    groups = DW_COLS // n
    return _matmul(name, a, dy, mode="tn", grid=(1, NDEV // groups, S // TK),
                   a_blk=(TK, kdim), a_map=lambda i, j, k: (k, 0),
                   b_blk=(TK, DW_COLS), b_map=lambda i, j, k: (k, j),
                   o_shape=(NDEV, kdim, n), o_blk=(groups, kdim, n) if groups > 1 else (None, kdim, n),
                   o_map=lambda i, j, k: (j, 0, 0), out_dtype=BF16, out_groups=groups)


def _bwd_x_plain(name, dy, w, nk, after=None):
    kdim, n = w.shape
    return _matmul(name, dy, w, mode="nt", grid=(S // TM, 1, nk),
                   a_blk=(TM, n // nk), a_map=lambda i, j, k: (i, k),
                   b_blk=(kdim, n // nk), b_map=lambda i, j, k: (0, k),
                   o_shape=(S, kdim), o_blk=(TM, kdim), o_map=lambda i, j, k: (i, 0), out_dtype=F32, after=after)


def _bwd_w_cols_blocked(name, a, dy8):
    kdim = a.shape[1]
    n = dy8.shape[2]
    return _matmul(name, a, dy8, mode="tn", grid=(1, NDEV, S // TK),
                   a_blk=(TK, kdim), a_map=lambda i, j, k: (k, 0),
                   b_blk=(None, TK, n), b_map=lambda i, j, k: (j, k, 0),
                   o_shape=(NDEV, kdim, n), o_blk=(None, kdim, n), o_map=lambda i, j, k: (j, 0, 0), out_dtype=BF16)


def _bwd_w_rows(name, a, dy):
    kdim = a.shape[1]
    n = dy.shape[1]
    tmm = 512
    return _matmul(name, a, dy, mode="tn", grid=(kdim // tmm, 1, S // TK),
                   a_blk=(TK, tmm), a_map=lambda i, j, k: (k, i),
                   b_blk=(TK, n), b_map=lambda i, j, k: (k, 0),
                   o_shape=(kdim, n), o_blk=(tmm, n), o_map=lambda i, j, k: (i, 0), out_dtype=BF16)


def _bwd_w_kblocked(name, a4, dy):
    nb, _, kb = a4.shape
    n = dy.shape[1]
    return _matmul(name, a4, dy, mode="tn", grid=(nb, 1, S // TK),
                   a_blk=(None, TK, kb), a_map=lambda i, j, k: (i, k, 0),
                   b_blk=(TK, n), b_map=lambda i, j, k: (k, 0),
                   o_shape=(nb, kb, n), o_blk=(None, kb, n), o_map=lambda i, j, k: (i, 0, 0), out_dtype=BF16)


def _rstd(x):
    return lax.rsqrt(jnp.mean(x * x, axis=-1, keepdims=True) + RMS_EPS)


def _row_spec(tm=ROW_TILE, width=D):
    return pl.BlockSpec((tm, width), lambda i: (i, 0))


def _vec_spec(rows=1, width=D):
    return pl.BlockSpec((rows, width), lambda i: (0, 0))


def _rms_fwd(name, x, gains):
    n = len(gains)

    def body(x_ref, *refs):
        x_val = x_ref[...]
        xh = x_val * _rstd(x_val)
        for g_ref, o_ref in zip(refs[:n], refs[n:]):
            o_ref[...] = (xh * g_ref[...]).astype(o_ref.dtype)

    outs = pl.pallas_call(
        body, name=name, grid=(S // ROW_TILE,),
        in_specs=[_row_spec()] + [_vec_spec()] * n,
        out_specs=[_row_spec()] * n,
        out_shape=[_sds((S, D), BF16)] * n,
        compiler_params=_cparams(1),
    )(x, *gains)
    return list(outs)


def _resid_rms(name, h, y, g, next_gains):
    n = len(next_gains)

    def body(h_ref, y_ref, g_ref, *refs):
        y_val = y_ref[...]
        h_new = h_ref[...] + (y_val * _rstd(y_val)) * g_ref[...]
        refs[n][...] = h_new
        hh = h_new * _rstd(h_new)
        for g2_ref, o_ref in zip(refs[:n], refs[n + 1:]):
            o_ref[...] = (hh * g2_ref[...]).astype(o_ref.dtype)

    outs = pl.pallas_call(
        body, name=name, grid=(S // ROW_TILE,),
        in_specs=[_row_spec(), _row_spec(), _vec_spec()] + [_vec_spec()] * n,
        out_specs=[_row_spec()] * (n + 1), out_shape=[_sds((S, D), F32)] + [_sds((S, D), BF16)] * n,
        compiler_params=_cparams(1),
    )(h, y, g, *next_gains)
    return outs[0], list(outs[1:])


def _resid_rms_loss(name, h, y, g, target):
    def body(h_ref, y_ref, g_ref, t_ref, dh_ref, part_ref):
        y_val = y_ref[...]
        e = h_ref[...] + (y_val * _rstd(y_val)) * g_ref[...] - t_ref[...]
        dh_ref[...] = e * (1.0 / D)
        part = jnp.sum(e * e, axis=0, keepdims=True)
        step = pl.program_id(0)

        @pl.when(step == 0)
        def _():
            part_ref[...] = part

        @pl.when(step > 0)
        def _():
            part_ref[...] += part

    return pl.pallas_call(
        body, name=name, grid=(S // ROW_TILE,),
        in_specs=[_row_spec(), _row_spec(), _vec_spec(), _row_spec()],
        out_specs=[_row_spec(), _vec_spec()],
        out_shape=[_sds((S, D), F32), _sds((1, D), F32)],
        compiler_params=_cparams(1),
    )(h, y, g, target)


def _rms_bwd(name, x, pairs, dres, out_dtype):
    n = len(pairs)
    has_res = dres is not None

    def body(x_ref, *refs):
        g_refs = refs[0:2 * n:2]
        dn_refs = refs[1:2 * n:2]
        pos = 2 * n
        res_ref = refs[pos] if has_res else None
        pos += int(has_res)
        dx_ref = refs[pos]
        dg_refs = refs[pos + 1:]
        step = pl.program_id(0)
        x_val = x_ref[...]
        r = _rstd(x_val)
        xh = x_val * r
        acc = res_ref[...] if has_res else jnp.zeros_like(x_val)
        for g_ref, dn_ref, dg_ref in zip(g_refs, dn_refs, dg_refs):
            dn = dn_ref[...].astype(F32)
            dxh = dn * g_ref[...]
            acc = acc + r * (dxh - xh * jnp.mean(dxh * xh, axis=-1, keepdims=True))
            part = jnp.sum(dn * xh, axis=0, keepdims=True)

            @pl.when(step == 0)
            def _():
                dg_ref[...] = jnp.zeros_like(dg_ref)

            dg_ref[0:1, :] += part

        dx_ref[...] = acc.astype(dx_ref.dtype)

    operands = [x]
    in_specs = [_row_spec()]
    for g, dn in pairs:
        operands += [g, dn]
        in_specs += [_vec_spec(), _row_spec()]
    if has_res:
        operands.append(dres)
        in_specs.append(_row_spec())
    outs = pl.pallas_call(
        body, name=name, grid=(S // ROW_TILE,),
        in_specs=in_specs,
        out_specs=[_row_spec()] + [_vec_spec(8)] * n,
        out_shape=[_sds((S, D), out_dtype)] + [_sds((8, D), F32)] * n,
        compiler_params=_cparams(1),
    )(*operands)
    return outs[0], list(outs[1:])


def _shift_down(u, prev8, k):
    r = pltpu.roll(u, k, 0)
    p = pltpu.roll(prev8, k, 0)
    row = lax.broadcasted_iota(jnp.int32, prev8.shape, 0)
    top = jnp.where(row < k, p, r[0:8])
    return jnp.concatenate([top, r[8:]], axis=0)


def _shift_up(u, next8, k):
    tm = u.shape[0]
    r = pltpu.roll(u, tm - k, 0)
    p = pltpu.roll(next8, 8 - k, 0)
    row = lax.broadcasted_iota(jnp.int32, next8.shape, 0)
    bot = jnp.where(row >= 8 - k, p, r[tm - 8:tm])
    return jnp.concatenate([r[:tm - 8], bot], axis=0)


CONV_TILE = 512


def _halo_prev(col):
    return pl.BlockSpec((8, D), lambda i: (jnp.maximum(i * (CONV_TILE // 8) - 1, 0), col))


def _halo_next(col):
    last = S // 8 - 1
    return pl.BlockSpec((8, D), lambda i: (jnp.minimum((i + 1) * (CONV_TILE // 8), last), col))


def _conv_fwd(name, z, cw):
    def body(b_ref, c_ref, h_ref, cp_ref, hp_ref, cw_ref, o_ref):
        i = pl.program_id(0)
        u = c_ref[...].astype(F32) * h_ref[...].astype(F32)
        up = cp_ref[...].astype(F32) * hp_ref[...].astype(F32)
        up = jnp.where(i > 0, up, 0.0)
        cv = cw_ref[0:1, :] * _shift_down(u, up, 2) + cw_ref[1:2, :] * _shift_down(u, up, 1) + cw_ref[2:3, :] * u
        o_ref[...] = (b_ref[...].astype(F32) * cv).astype(o_ref.dtype)

    col = lambda c: pl.BlockSpec((CONV_TILE, D), lambda i: (i, c))
    return pl.pallas_call(
        body, name=name, grid=(S // CONV_TILE,),
        in_specs=[col(0), col(1), col(2), _halo_prev(1), _halo_prev(2), _vec_spec(8)],
        out_specs=_row_spec(CONV_TILE), out_shape=_sds((S, D), BF16),
        compiler_params=_cparams(1),
    )(z, z, z, z, z, cw)


def _conv_bwd(name, z, dpre, cw):
    nsteps = S // CONV_TILE

    def body(b_ref, c_ref, h_ref, cp_ref, hp_ref, dp_ref, dpn_ref, bn_ref, cw_ref, dz_ref, dcw_ref):
        i = pl.program_id(0)
        b = b_ref[...].astype(F32)
        c = c_ref[...].astype(F32)
        h = h_ref[...].astype(F32)
        dp = dp_ref[...].astype(F32)
        u = c * h
        up = jnp.where(i > 0, cp_ref[...].astype(F32) * hp_ref[...].astype(F32), 0.0)
        s1 = _shift_down(u, up, 1)
        s2 = _shift_down(u, up, 2)
        w0, w1, w2 = cw_ref[0:1, :], cw_ref[1:2, :], cw_ref[2:3, :]
        cv = w0 * s2 + w1 * s1 + w2 * u
        dcv = dp * b
        dcvn = jnp.where(i < nsteps - 1, dpn_ref[...].astype(F32) * bn_ref[...].astype(F32), 0.0)
        du = w2 * dcv + w1 * _shift_up(dcv, dcvn, 1) + w0 * _shift_up(dcv, dcvn, 2)
        dz_ref[:, 0:D] = (dp * cv).astype(dz_ref.dtype)
        dz_ref[:, D:2 * D] = (du * h).astype(dz_ref.dtype)
        dz_ref[:, 2 * D:3 * D] = (du * c).astype(dz_ref.dtype)

        @pl.when(i == 0)
        def _():
            dcw_ref[...] = jnp.zeros_like(dcw_ref)

        dcw_ref[0:1, :] += jnp.sum(dcv * s2, axis=0, keepdims=True)
        dcw_ref[1:2, :] += jnp.sum(dcv * s1, axis=0, keepdims=True)
        dcw_ref[2:3, :] += jnp.sum(dcv * u, axis=0, keepdims=True)

    col = lambda c: pl.BlockSpec((CONV_TILE, D), lambda i: (i, c))
    return pl.pallas_call(
        body, name=name, grid=(nsteps,),
        in_specs=[col(0), col(1), col(2), _halo_prev(1), _halo_prev(2),
                  _row_spec(CONV_TILE), _halo_next(0), _halo_next(0), _vec_spec(8)],
        out_specs=[pl.BlockSpec((CONV_TILE, 3 * D), lambda i: (i, 0)), _vec_spec(8)],
        out_shape=[_sds((S, 3 * D), BF16), _sds((8, D), F32)],
        compiler_params=_cparams(1),
    )(z, z, z, z, z, dpre, dpre, z, cw)


_GU_BLOCK = pl.BlockSpec((2, None, TM, FB), lambda i, j: (0, j, i, 0))


def _gate_up_act(name, a, wg):
    kdim = a.shape[1]

    def body(a_ref, wgate_ref, wup_ref, gu_ref, act_ref):
        x = a_ref[...]
        g = _dot_nn(x, wgate_ref[...])
        u = _dot_nn(x, wup_ref[...])
        gu_ref[0] = g.astype(gu_ref.dtype)
        gu_ref[1] = u.astype(gu_ref.dtype)
        act_ref[...] = (g * jax.nn.sigmoid(g) * u).astype(act_ref.dtype)

    return pl.pallas_call(
        body, name=name, grid=(S // TM, NFB),
        in_specs=[pl.BlockSpec((TM, kdim), lambda i, j: (i, 0)),
                  pl.BlockSpec((None, kdim, FB), lambda i, j: (j, 0, 0)),
                  pl.BlockSpec((None, kdim, FB), lambda i, j: (j + NFB, 0, 0))],
        out_specs=[_GU_BLOCK, pl.BlockSpec((None, TM, FB), lambda i, j: (j, i, 0))],
        out_shape=[_sds((2, NFB, S, FB), BF16), _sds((NFB, S, FB), BF16)],
        compiler_params=_cparams(2),
    )(a, wg, wg)


def _down_dx_act_bwd(name, df, w4, gu, after):
    _, kb, n = w4.shape

    def body(df_ref, w_ref, gu_ref, after_ref, o_ref):
        d = _dot_nt(df_ref[...], w_ref[...])
        g = gu_ref[0].astype(F32)
        u = gu_ref[1].astype(F32)
        sg = jax.nn.sigmoid(g)
        o_ref[0] = (d * u * sg * (1.0 + g * (1.0 - sg))).astype(o_ref.dtype)
        o_ref[1] = (d * g * sg).astype(o_ref.dtype)

    return pl.pallas_call(
        body, name=name, grid=(S // TM, NFB),
        in_specs=[pl.BlockSpec((TM, n), lambda i, j: (i, 0)), pl.BlockSpec((None, kb, n), lambda i, j: (j, 0, 0)),
                  _GU_BLOCK, pl.BlockSpec(memory_space=pl.ANY)],
        out_specs=_GU_BLOCK, out_shape=_sds((2, NFB, S, FB), BF16),
        compiler_params=_cparams(2),
    )(df, w4, gu, after)


def _rope_tables(name, pos_col, inv_freq_row):
    def body(pos_ref, f_ref, cos_ref, sin_ref):
        ang = pos_ref[...].astype(F32) * f_ref[...]
        lane = lax.broadcasted_iota(jnp.int32, ang.shape, 1)
        s = jnp.sin(ang)
        cos_ref[...] = jnp.cos(ang)
        sin_ref[...] = jnp.where((lane % HEAD_DIM) < HEAD_DIM // 2, -s, s)

    tab = pl.BlockSpec((ROW_TILE, 128), lambda i: (i, 0))
    return pl.pallas_call(
        body, name=name, grid=(S // ROW_TILE,),
        in_specs=[pl.BlockSpec((ROW_TILE, 1), lambda i: (i, 0)), _vec_spec(1, 128)],
        out_specs=[tab, tab], out_shape=[_sds((S, 128), F32)] * 2,
        compiler_params=_cparams(1),
    )(pos_col, inv_freq_row)


def _swap_halves(t):
    lane = lax.broadcasted_iota(jnp.int32, t.shape, 1)
    first = (lane % HEAD_DIM) < HEAD_DIM // 2
    return jnp.where(first, pltpu.roll(t, 128 - HEAD_DIM // 2, 1), pltpu.roll(t, HEAD_DIM // 2, 1))


NCHUNK = D // 128


def _chunk(c, base=0):
    return slice(base + c * 128, base + (c + 1) * 128)


def _class_rows(r, d, tm):
    return pl.ds(r, tm // d, stride=d) if d > 1 else slice(None)


def _class_block(d, tm):
    return pl.BlockSpec((tm // d, d * D), lambda i: (i, 0))


def _tokens_from_classes(blk_ref, tmp_ref, d, tm):
    for r in range(d):
        for c in range(NCHUNK):
            tmp_ref[c, _class_rows(r, d, tm), :] = blk_ref[:, _chunk(c, r * D)]


def _classes_from_tokens(tmp_ref, blk_ref, d, tm):
    for r in range(d):
        for c in range(NCHUNK):
            blk_ref[:, _chunk(c, r * D)] = tmp_ref[c, _class_rows(r, d, tm), :].astype(blk_ref.dtype)


def _proj_classes(name, a, w, col, d, tables, scale):
    kdim = a.shape[1]
    rope = tables is not None

    def body(a_ref, w_ref, *refs):
        if rope:
            cos_ref, sin_ref, o_ref, tmp_ref = refs
        else:
            o_ref, tmp_ref = refs
        acc = _dot_nn(a_ref[...], w_ref[...])
        for c in range(NCHUNK):
            tmp_ref[c] = acc[:, _chunk(c)]
        for r in range(d):
            rows = _class_rows(r, d, TM)
            if rope:
                cs = cos_ref[rows, :]
                sn = sin_ref[rows, :]
            for c in range(NCHUNK):
                x = tmp_ref[c, rows, :]
                if rope:
                    x = (x * cs + _swap_halves(x) * sn) * scale
                o_ref[:, _chunk(c, r * D)] = x.astype(o_ref.dtype)

    tab = pl.BlockSpec((TM, 128), lambda i: (i, 0))
    return pl.pallas_call(
        body, name=name, grid=(S // TM,),
        in_specs=[pl.BlockSpec((TM, kdim), lambda i: (i, 0)), pl.BlockSpec((kdim, D), lambda i: (0, col))]
                 + ([tab, tab] if rope else []),
        out_specs=_class_block(d, TM), out_shape=_sds((S // d, d * D), BF16),
        scratch_shapes=[pltpu.VMEM((NCHUNK, TM, 128), F32)],
        compiler_params=_cparams(1),
    )(a, w, *(tables if rope else ()))


ATTN_CHAINS = 4


def _attn_units(d):
    nblk = S // d // BAND
    return max(1, 2 * ATTN_CHAINS // nblk)


def _class_spec(d):
    return pl.BlockSpec((S // d, 128 * _attn_units(d)), lambda cb: (0, cb))


def _dot_nt(a, b):
    return lax.dot_general(a, b, _DIMS["nt"], preferred_element_type=F32)


def _dot_tn(a, b):
    return lax.dot_general(a, b, _DIMS["tn"], preferred_element_type=F32)


def _dot_nn(a, b):
    return lax.dot_general(a, b, _DIMS["nn"], preferred_element_type=F32)


def _band_mask(nkeys):
    qi = lax.broadcasted_iota(jnp.int32, (2 * BAND, nkeys), 0) % BAND
    kj = lax.broadcasted_iota(jnp.int32, (2 * BAND, nkeys), 1)
    if nkeys == BAND:
        return kj <= qi
    dist = qi + BAND - kj
    return (dist >= 0) & (dist <= BAND)


def _stack_heads(x):
    row = lax.broadcasted_iota(jnp.int32, (2 * BAND, 128), 0)
    lane = lax.broadcasted_iota(jnp.int32, (2 * BAND, 128), 1)
    keep = (row < BAND) == (lane < HEAD_DIM)
    return jnp.where(keep, jnp.concatenate([x, x], axis=0), jnp.zeros((), x.dtype))


def _unstack(x2):
    first_head = lax.broadcasted_iota(jnp.int32, (BAND, 128), 1) < HEAD_DIM
    return jnp.where(first_head, x2[:BAND], x2[BAND:])


def _for_later_blocks(nblk, units, fn):
    all_lanes = [slice(u * 128, (u + 1) * 128) for u in range(units)]
    unroll = max(1, ATTN_CHAINS // units)
    trips = (nblk - 1) // unroll
    if trips > 1:
        def step(i, carry):
            for j in range(unroll):
                for lanes in all_lanes:
                    fn(pl.multiple_of((1 + i * unroll + j) * BAND, BAND), lanes)
            return carry

        lax.fori_loop(0, trips, step, 0)
    else:
        trips = 0
    for sb in range(1 + trips * unroll, nblk):
        for lanes in all_lanes:
            fn(sb * BAND, lanes)


def _attn_fwd(name, q, k, v, d):
    nblk = S // d // BAND
    units = _attn_units(d)

    def body(q_ref, k_ref, v_ref, o_ref, lse_ref):
        def block(r0, k0, nkeys, lanes):
            q2 = _stack_heads(q_ref[pl.ds(r0, BAND), lanes])
            s = jnp.where(_band_mask(nkeys), _dot_nt(q2, k_ref[pl.ds(k0, nkeys), lanes]), NEG_INF)
            m = jnp.max(s, axis=-1, keepdims=True)
            p = jnp.exp(s - m)
            l = jnp.sum(p, axis=-1, keepdims=True)
            o2 = _dot_nn(p.astype(BF16), v_ref[pl.ds(k0, nkeys), lanes]) / l
            lse2 = jnp.broadcast_to(m + jnp.log(l), (2 * BAND, 128))
            o_ref[pl.ds(r0, BAND), lanes] = _unstack(o2)
            lse_ref[pl.ds(r0, BAND), lanes] = _unstack(lse2)

        for u in range(units):
            block(0, 0, BAND, slice(u * 128, (u + 1) * 128))

        _for_later_blocks(nblk, units, lambda r0, lanes: block(r0, r0 - BAND, 2 * BAND, lanes))

    spec = _class_spec(d)
    return pl.pallas_call(
        body, name=name, grid=(8 * d // units,),
        in_specs=[spec] * 3, out_specs=[spec] * 2,
        out_shape=[_sds((S // d, d * D), F32)] * 2,
        compiler_params=_cparams(1),
    )(q, k, v)


def _attn_bwd(name, q, k, v, do, lse, dd, d):
    nblk = S // d // BAND
    units = _attn_units(d)

    def body(q_ref, k_ref, v_ref, do_ref, lse_ref, dd_ref, dq_ref, dk_ref, dv_ref):
        def column(ref, r0, lanes):
            rows = pl.ds(r0, BAND)
            first = slice(lanes.start, lanes.start + 1)
            second = slice(lanes.start + HEAD_DIM, lanes.start + HEAD_DIM + 1)
            return jnp.concatenate([ref[rows, first], ref[rows, second]], axis=0)

        def block(r0, k0, nkeys, lanes, first):
            q2 = _stack_heads(q_ref[pl.ds(r0, BAND), lanes])
            do2 = _stack_heads(do_ref[pl.ds(r0, BAND), lanes])
            kk = k_ref[pl.ds(k0, nkeys), lanes]
            vv = v_ref[pl.ds(k0, nkeys), lanes]
            s = jnp.where(_band_mask(nkeys), _dot_nt(q2, kk), NEG_INF)
            p = jnp.exp(s - column(lse_ref, r0, lanes))
            ds = (p * (_dot_nt(do2, vv) - column(dd_ref, r0, lanes))).astype(BF16)
            dq_ref[pl.ds(r0, BAND), lanes] = _unstack(_dot_nn(ds, kk))
            dk_part = _dot_tn(ds, q2)
            dv_part = _dot_tn(p.astype(BF16), do2)
            if first:
                dk_ref[pl.ds(k0, nkeys), lanes] = dk_part
                dv_ref[pl.ds(k0, nkeys), lanes] = dv_part
            else:
                dk_ref[pl.ds(k0, BAND), lanes] += dk_part[:BAND]
                dv_ref[pl.ds(k0, BAND), lanes] += dv_part[:BAND]
                dk_ref[pl.ds(k0 + BAND, BAND), lanes] = dk_part[BAND:]
                dv_ref[pl.ds(k0 + BAND, BAND), lanes] = dv_part[BAND:]

        for u in range(units):
            block(0, 0, BAND, slice(u * 128, (u + 1) * 128), True)

        _for_later_blocks(nblk, units, lambda r0, lanes: block(r0, r0 - BAND, 2 * BAND, lanes, False))

    spec = _class_spec(d)
    return pl.pallas_call(
        body, name=name, grid=(8 * d // units,),
        in_specs=[spec] * 6, out_specs=[spec] * 3,
        out_shape=[_sds((S // d, d * D), F32)] * 3,
        compiler_params=_cparams(1),
    )(q, k, v, do, lse, dd)


MIX_TILE = 256
DILATIONS = tuple(d for _, d in BRANCHES)


def _branch_weights(la, lb, lc):
    m = jnp.maximum(jnp.maximum(la, lb), lc)
    ea, eb, ec = jnp.exp(la - m), jnp.exp(lb - m), jnp.exp(lc - m)
    den = ea + eb + ec
    return ea / den, eb / den, ec / den


def _mix_operands(outs, lses):
    specs = [_class_block(d, MIX_TILE) for d in DILATIONS] * 2
    scratch = [pltpu.VMEM((NCHUNK, MIX_TILE, 128), F32)] * 4
    return list(outs) + list(lses), specs, scratch


def _mix_fwd(name, outs, lses):
    def body(o0, o1, o2, l0, l1, l2, o_ref, to1, to2, tl1, tl2):
        for blk, tmp, d in ((o1, to1, DILATIONS[1]), (o2, to2, DILATIONS[2]), (l1, tl1, DILATIONS[1]), (l2, tl2, DILATIONS[2])):
            _tokens_from_classes(blk, tmp, d, MIX_TILE)
        for c in range(NCHUNK):
            wa, wb, wc = _branch_weights(l0[:, _chunk(c)], tl1[c], tl2[c])
            o_ref[:, _chunk(c)] = (wa * o0[:, _chunk(c)] + wb * to1[c] + wc * to2[c]).astype(o_ref.dtype)

    operands, specs, scratch = _mix_operands(outs, lses)
    return pl.pallas_call(
        body, name=name, grid=(S // MIX_TILE,),
        in_specs=specs, out_specs=_row_spec(MIX_TILE), out_shape=_sds((S, D), BF16),
        scratch_shapes=scratch, compiler_params=_cparams(1),
    )(*operands)


def _head_sum(x, ones_blockdiag):
    hi = x.astype(BF16)
    r1 = x - hi.astype(F32)
    mid = r1.astype(BF16)
    lo = (r1 - mid.astype(F32)).astype(BF16)
    return _dot_nn(hi, ones_blockdiag) + _dot_nn(mid, ones_blockdiag) + _dot_nn(lo, ones_blockdiag)


def _mix_bwd(name, do, outs, lses, ones_blockdiag):
    def body(do_ref, o0, o1, o2, l0, l1, l2, ones_ref, d0, d1, d2, t0, t1, t2,
             to1, to2, tl1, tl2, td1, td2, tt1, tt2):
        for blk, tmp, d in ((o1, to1, DILATIONS[1]), (o2, to2, DILATIONS[2]), (l1, tl1, DILATIONS[1]), (l2, tl2, DILATIONS[2])):
            _tokens_from_classes(blk, tmp, d, MIX_TILE)
        ones = ones_ref[...]
        for c in range(NCHUNK):
            w = _branch_weights(l0[:, _chunk(c)], tl1[c], tl2[c])
            dov = do_ref[:, _chunk(c)]
            o = w[0] * o0[:, _chunk(c)] + w[1] * to1[c] + w[2] * to2[c]
            t = _head_sum(dov * o, ones)
            d0[:, _chunk(c)] = (w[0] * dov).astype(d0.dtype)
            t0[:, _chunk(c)] = w[0] * t
            td1[c], tt1[c] = w[1] * dov, w[1] * t
            td2[c], tt2[c] = w[2] * dov, w[2] * t
        for tmp, blk, d in ((td1, d1, DILATIONS[1]), (tt1, t1, DILATIONS[1]), (td2, d2, DILATIONS[2]), (tt2, t2, DILATIONS[2])):
            _classes_from_tokens(tmp, blk, d, MIX_TILE)

    operands, specs, scratch = _mix_operands(outs, lses)
    out_specs = [_class_block(d, MIX_TILE) for d in DILATIONS] * 2
    out_shape = [_sds((S // d, d * D), BF16) for d in DILATIONS] + [_sds((S // d, d * D), F32) for d in DILATIONS]
    return pl.pallas_call(
        body, name=name, grid=(S // MIX_TILE,),
        in_specs=[_row_spec(MIX_TILE)] + specs + [_vec_spec(128, 128)],
        out_specs=out_specs, out_shape=out_shape,
        scratch_shapes=scratch + [pltpu.VMEM((NCHUNK, MIX_TILE, 128), F32)] * 4,
        compiler_params=_cparams(1),
    )(do, *operands, ones_blockdiag)


def _attn_bwd_post(name, grads, cos_t, sin_t):
    tm = MIX_TILE
    scale = HEAD_DIM ** -0.5

    def unrope(x, cs, sn):
        return x * cs - _swap_halves(x) * sn

    def body(*refs):
        in_refs = refs[:9]
        cos_ref, sin_ref, dq_ref, dkv_ref, tmp_ref = refs[9:]
        cs = cos_ref[...]
        sn = sin_ref[...]
        for g, d in enumerate(DILATIONS):
            for which, blk in enumerate(in_refs[3 * g:3 * g + 3]):
                if d > 1:
                    _tokens_from_classes(blk, tmp_ref, d, tm)
                for c in range(NCHUNK):
                    x = tmp_ref[c] if d > 1 else blk[:, _chunk(c)]
                    if which == 0:
                        dq_ref[:, _chunk(c, g * D)] = (unrope(x, cs, sn) * scale).astype(dq_ref.dtype)
                    elif which == 1:
                        dkv_ref[:, _chunk(c, g * D)] = unrope(x, cs, sn).astype(dkv_ref.dtype)
                    else:
                        dkv_ref[:, _chunk(c, QW + g * D)] = x.astype(dkv_ref.dtype)

    operands = [a for branch in grads for a in branch]
    tab = pl.BlockSpec((tm, 128), lambda i: (i, 0))
    return pl.pallas_call(
        body, name=name, grid=(S // tm,),
        in_specs=[_class_block(d, tm) for d in DILATIONS for _ in range(3)] + [tab, tab],
        out_specs=[pl.BlockSpec((tm, QW), lambda i: (i, 0)), pl.BlockSpec((tm, 2 * QW), lambda i: (i, 0))],
        out_shape=[_sds((S, QW), BF16), _sds((S, 2 * QW), BF16)],
        scratch_shapes=[pltpu.VMEM((NCHUNK, tm, 128), F32)],
        compiler_params=_cparams(1),
    )(*operands, cos_t, sin_t)


def _adamw(name, parts, w, m, v):
    n, rows, cols = parts.shape
    tr = rows
    for cand in (256, 176, 128, 64, 32, 16, 8):
        if rows % cand == 0:
            tr = cand
            break
    def body(p_ref, w_ref, m_ref, v_ref, g_ref, d_ref, nm_ref, nv_ref):
        g = p_ref[0].astype(F32)
        for j in range(1, n):
            g = g + p_ref[j].astype(F32)
        g_ref[...] = g
        d_ref[...], nm_ref[...], nv_ref[...] = _adam_update(g, w_ref[...], m_ref[...], v_ref[...])

    blk = pl.BlockSpec((tr, cols), lambda i: (i, 0))
    return pl.pallas_call(
        body, name=name, grid=(rows // tr,),
        in_specs=[pl.BlockSpec((n, tr, cols), lambda i: (0, i, 0)), blk, blk, blk],
        out_specs=[blk] * 4, out_shape=[_sds((rows, cols), F32)] * 4,
        compiler_params=_cparams(1),
    )(parts, w, m, v)


def _adam_update(g, w, m, v):
    c1 = 1.0 / (1.0 - ADAM_B1 ** ADAM_STEP)
    c2 = 1.0 / (1.0 - ADAM_B2 ** ADAM_STEP)
    nm = ADAM_B1 * m + (1.0 - ADAM_B1) * g
    nv = ADAM_B2 * v + (1.0 - ADAM_B2) * (g * g)
    return -ADAM_LR * ((nm * c1) / (jnp.sqrt(nv * c2) + ADAM_EPS) + ADAM_WD * w), nm, nv


GAIN_ROWS = 16


def _pack_small(name, gain_tiles, taps):
    ng = len(gain_tiles)

    def body(*refs):
        o_ref = refs[-1]
        o_ref[...] = jnp.zeros_like(o_ref)
        for i in range(ng):
            o_ref[i:i + 1, :] = refs[i][0:1, :]
        o_ref[ng:ng + 3, :] = refs[ng][0:3, :]

    return pl.pallas_call(body, name=name, out_shape=_sds((GAIN_ROWS, D), F32))(*gain_tiles, taps)


def _adamw_gains(name, parts, params):
    np_ = len(params)
    shapes = [w.shape for w, _, _ in params]

    def body(p_ref, *refs):
        ins, outs = refs[:3 * np_], refs[3 * np_:]

        def total(lo, rows):
            g = p_ref[0, lo:lo + rows, :]
            for j in range(1, NDEV):
                g = g + p_ref[j, lo:lo + rows, :]
            return g

        lo = 0
        for i, shape in enumerate(shapes):
            g = total(lo, shape[0])
            lo += shape[0]
            w_ref, m_ref, v_ref = ins[3 * i:3 * i + 3]
            g_ref, d_ref, nm_ref, nv_ref = outs[4 * i:4 * i + 4]
            g_ref[...] = g
            d_ref[...], nm_ref[...], nv_ref[...] = _adam_update(g, w_ref[...], m_ref[...], v_ref[...])
        taps_ref = outs[-1]
        taps_ref[...] = jnp.zeros_like(taps_ref)
        taps_ref[0:3, :] = total(lo, 3)

    out_shape = [_sds(shape, F32) for shape in shapes for _ in range(4)] + [_sds((8, D), F32)]
    outs = pl.pallas_call(body, name=name, out_shape=out_shape)(parts, *[a for p in params for a in p])
    return [list(outs[4 * i:4 * i + 4]) for i in range(np_)], outs[-1]


def _exchange(name, arrays, kind):
    n = len(arrays)
    gather = kind == "gather"
    out_shape = [_sds((NDEV,) + a.shape if gather else a.shape, a.dtype) for a in arrays]

    def body(*refs):
        srcs, outs = refs[:n], refs[n:2 * n]
        send_sems, recv_sems, local_sems = refs[2 * n:]
        x, y, c = lax.axis_index("x"), lax.axis_index("y"), lax.axis_index("c")
        me = 4 * x + 2 * y + c
        pending = []
        for t in range(n):
            own = pltpu.make_async_copy(srcs[t] if gather else srcs[t].at[me], outs[t].at[me], local_sems.at[t])
            own.start()
            pending.append(own)
            for rel in range(1, NDEV):
                px = 1 - x if rel & 4 else x
                py = 1 - y if rel & 2 else y
                pc = 1 - c if rel & 1 else c
                peer = 4 * px + 2 * py + pc
                send = pltpu.make_async_remote_copy(
                    src_ref=srcs[t] if gather else srcs[t].at[peer], dst_ref=outs[t].at[me],
                    send_sem=send_sems.at[t, rel - 1], recv_sem=recv_sems.at[t, rel - 1],
                    device_id=(px, py, pc), device_id_type=MESH)
                send.start()
                arrive = pltpu.make_async_remote_copy(
                    src_ref=srcs[t] if gather else srcs[t].at[me], dst_ref=outs[t].at[peer],
                    send_sem=send_sems.at[t, rel - 1], recv_sem=recv_sems.at[t, rel - 1],
                    device_id=(px, py, pc), device_id_type=MESH)
                pending.append((send, arrive))
        for item in pending:
            if isinstance(item, tuple):
                item[0].wait_send()
                item[1].wait_recv()
            else:
                item.wait()

    any_spec = pl.BlockSpec(memory_space=pl.ANY)
    outs = pl.pallas_call(
        body, name=name,
        in_specs=[any_spec] * n, out_specs=[any_spec] * n, out_shape=out_shape,
        scratch_shapes=[pltpu.SemaphoreType.DMA((n, NDEV - 1)), pltpu.SemaphoreType.DMA((n, NDEV - 1)),
                        pltpu.SemaphoreType.DMA((n,))],
    )(*arrays)
    return list(outs)


_HBM_SPEC = pl.BlockSpec(memory_space=pltpu.HBM)
_SEM_SPEC = pl.BlockSpec(memory_space=pltpu.SEMAPHORE)
_DATAFLOW = pltpu.SideEffectType.DATAFLOW_SIDE_EFFECTING


def _peers():
    x, y, c = lax.axis_index("x"), lax.axis_index("y"), lax.axis_index("c")
    out = []
    for rel in range(1, NDEV):
        px = 1 - x if rel & 4 else x
        py = 1 - y if rel & 2 else y
        pc = 1 - c if rel & 1 else c
        out.append((rel - 1, (px, py, pc), 4 * px + 2 * py + pc))
    return 4 * x + 2 * y + c, out


def _hbm(a):
    return pltpu.HBM(a.shape, a.dtype)


def _own_slot(a, me, kind):
    mine = a[None] if kind == "gather" else lax.dynamic_slice_in_dim(a, me, 1, axis=0)
    shape = (NDEV,) + mine.shape[1:]
    return lax.dynamic_update_slice_in_dim(lax.empty(shape, a.dtype), mine, me, axis=0)


def _exchange_start(name, arrays, me, kind):
    n = len(arrays)
    gather = kind == "gather"
    lands = [_own_slot(a, me, kind) for a in arrays]

    def body(*refs):
        src_refs, land_refs = refs[:n], refs[n:2 * n]
        send_sems, recv_sems = refs[2 * n], refs[2 * n + 1]
        token = refs[-1]
        my_block, peers = _peers()
        for t in range(n):
            for slot, dev, block in peers:
                pltpu.make_async_remote_copy(
                    src_ref=src_refs[t] if gather else src_refs[t].at[block], dst_ref=land_refs[t].at[my_block],
                    send_sem=send_sems.at[t * (NDEV - 1) + slot], recv_sem=recv_sems.at[t * (NDEV - 1) + slot],
                    device_id=dev, device_id_type=MESH).start()
        token[...] = jnp.zeros_like(token)

    operands = [pltpu.with_memory_space_constraint(a, pltpu.HBM) for a in list(arrays) + lands]
    outs = pl.pallas_call(
        body, name=name,
        out_shape=(pltpu.SemaphoreType.DMA((n * (NDEV - 1),)), pltpu.SemaphoreType.DMA((n * (NDEV - 1),)),
                   *[_hbm(a) for a in operands], _sds((8, 128), F32)),
        in_specs=[_HBM_SPEC] * (2 * n),
        out_specs=(_SEM_SPEC, _SEM_SPEC, *[_HBM_SPEC] * (2 * n), pl.BlockSpec(memory_space=pltpu.VMEM)),
        input_output_aliases={i: 2 + i for i in range(2 * n)},
        compiler_params=pltpu.CompilerParams(has_side_effects=_DATAFLOW),
    )(*operands)
    return (outs[0], outs[1], list(outs[2:2 + n]), list(outs[2 + n:2 + 2 * n])), outs[-1]


def _exchange_wait(name, started, t, after, kind):
    send_sems, recv_sems, srcs, lands = started
    gather = kind == "gather"

    def body(src_ref, land_ref, send_ref, recv_ref, after_ref, src_out, land_out):
        _, peers = _peers()
        for slot, dev, block in peers:
            copy = pltpu.make_async_remote_copy(
                src_ref=src_ref if gather else src_ref.at[block], dst_ref=land_ref.at[block],
                send_sem=send_ref.at[t * (NDEV - 1) + slot], recv_sem=recv_ref.at[t * (NDEV - 1) + slot],
                device_id=dev, device_id_type=MESH)
            copy.wait_send()
            copy.wait_recv()

    return pl.pallas_call(
        body, name=name, out_shape=(_hbm(srcs[t]), _hbm(lands[t])),
        in_specs=(_HBM_SPEC, _HBM_SPEC, _SEM_SPEC, _SEM_SPEC, pl.BlockSpec(memory_space=pl.ANY)),
        out_specs=(_HBM_SPEC, _HBM_SPEC), input_output_aliases={0: 0, 1: 1},
        compiler_params=pltpu.CompilerParams(has_side_effects=_DATAFLOW),
    )(srcs[t], lands[t], send_sems, recv_sems, after)[1]


DIRECT_RELS = (1, 2, 4, 6)
RELAY_RELS = (2, 4, 6)


def _rel_peer(rel):
    x, y, c = lax.axis_index("x"), lax.axis_index("y"), lax.axis_index("c")
    px = 1 - x if rel & 4 else x
    py = 1 - y if rel & 2 else y
    pc = 1 - c if rel & 1 else c
    return (px, py, pc), 4 * px + 2 * py + pc


def _gather_start(name, shards, me):
    n, nr = len(shards), len(DIRECT_RELS)
    lands = [_own_slot(a, me, "gather") for a in shards]

    def body(*refs):
        src_refs, land_refs = refs[:n], refs[n:2 * n]
        send_sems, recv_sems = refs[2 * n], refs[2 * n + 1]
        _, my_block = _rel_peer(0)
        for t in range(n):
            for s, rel in enumerate(DIRECT_RELS):
                dev, _ = _rel_peer(rel)
                pltpu.make_async_remote_copy(
                    src_ref=src_refs[t], dst_ref=land_refs[t].at[my_block],
                    send_sem=send_sems.at[t * nr + s], recv_sem=recv_sems.at[t * nr + s],
                    device_id=dev, device_id_type=MESH).start()

    operands = [pltpu.with_memory_space_constraint(a, pltpu.HBM) for a in list(shards) + lands]
    outs = pl.pallas_call(
        body, name=name,
        out_shape=(pltpu.SemaphoreType.DMA((n * nr,)), pltpu.SemaphoreType.DMA((n * nr,)), *[_hbm(a) for a in operands]),
        in_specs=[_HBM_SPEC] * (2 * n), out_specs=(_SEM_SPEC, _SEM_SPEC, *[_HBM_SPEC] * (2 * n)),
        input_output_aliases={i: 2 + i for i in range(2 * n)},
        compiler_params=pltpu.CompilerParams(has_side_effects=_DATAFLOW),
    )(*operands)
    return outs[0], outs[1], list(outs[2:2 + n]), list(outs[2 + n:2 + 2 * n])


def _gather_wait(name, started, ts, after):
    send_sems, recv_sems, srcs, lands = started
    m, nr = len(ts), len(DIRECT_RELS)

    def body(*refs):
        src_refs, land_refs = refs[:m], refs[m:2 * m]
        send_ref, recv_ref = refs[2 * m], refs[2 * m + 1]
        for i, t in enumerate(ts):
            for s, rel in enumerate(DIRECT_RELS):
                dev, block = _rel_peer(rel)
                copy = pltpu.make_async_remote_copy(
                    src_ref=src_refs[i], dst_ref=land_refs[i].at[block],
                    send_sem=send_ref.at[t * nr + s], recv_sem=recv_ref.at[t * nr + s],
                    device_id=dev, device_id_type=MESH)
                copy.wait_send()
                copy.wait_recv()

    operands = [srcs[t] for t in ts] + [lands[t] for t in ts]
    outs = pl.pallas_call(
        body, name=name, out_shape=tuple(_hbm(a) for a in operands),
        in_specs=[_HBM_SPEC] * (2 * m) + [_SEM_SPEC, _SEM_SPEC, pl.BlockSpec(memory_space=pl.ANY)],
        out_specs=tuple([_HBM_SPEC] * (2 * m)), input_output_aliases={i: i for i in range(2 * m)},
        compiler_params=pltpu.CompilerParams(has_side_effects=_DATAFLOW),
    )(*operands, send_sems, recv_sems, after)
    return list(outs[m:])


def _relay_start(name, lands):
    m, nr = len(lands), len(RELAY_RELS)

    def body(*refs):
        land_refs, send_sems, recv_sems = refs[:m], refs[m], refs[m + 1]
        sibling, _ = _rel_peer(1)
        for i in range(m):
            for s, rel in enumerate(RELAY_RELS):
                _, block = _rel_peer(rel)
                pltpu.make_async_remote_copy(
                    src_ref=land_refs[i].at[block], dst_ref=land_refs[i].at[block],
                    send_sem=send_sems.at[i * nr + s], recv_sem=recv_sems.at[i * nr + s],
                    device_id=sibling, device_id_type=MESH).start()

    outs = pl.pallas_call(
        body, name=name,
        out_shape=(pltpu.SemaphoreType.DMA((m * nr,)), pltpu.SemaphoreType.DMA((m * nr,)), *[_hbm(a) for a in lands]),
        in_specs=[_HBM_SPEC] * m, out_specs=(_SEM_SPEC, _SEM_SPEC, *[_HBM_SPEC] * m),
        input_output_aliases={i: 2 + i for i in range(m)},
        compiler_params=pltpu.CompilerParams(has_side_effects=_DATAFLOW),
    )(*lands)
    return outs[0], outs[1], list(outs[2:])


def _relay_wait(name, relayed, after):
    send_sems, recv_sems, lands = relayed
    m, nr = len(lands), len(RELAY_RELS)

    def body(*refs):
        land_refs, send_ref, recv_ref = refs[:m], refs[m], refs[m + 1]
        sibling, _ = _rel_peer(1)
        for i in range(m):
            for s, rel in enumerate(RELAY_RELS):
                _, sent = _rel_peer(rel)
                _, arriving = _rel_peer(rel ^ 1)
                copy = pltpu.make_async_remote_copy(
                    src_ref=land_refs[i].at[sent], dst_ref=land_refs[i].at[arriving],
                    send_sem=send_ref.at[i * nr + s], recv_sem=recv_ref.at[i * nr + s],
                    device_id=sibling, device_id_type=MESH)
                copy.wait_send()
                copy.wait_recv()

    outs = pl.pallas_call(
        body, name=name, out_shape=tuple(_hbm(a) for a in lands),
        in_specs=[_HBM_SPEC] * m + [_SEM_SPEC, _SEM_SPEC, pl.BlockSpec(memory_space=pl.ANY)],
        out_specs=tuple([_HBM_SPEC] * m), input_output_aliases={i: i for i in range(m)},
        compiler_params=pltpu.CompilerParams(has_side_effects=_DATAFLOW),
    )(*lands, send_sems, recv_sems, after)
    return list(outs)


def _ffn_fwd(tag, n, wg, wd):
    gu, act = _gate_up_act(f"ffn_gate_up_{tag}", n, wg)
    wd4 = wd.reshape(NFB, FB, D)
    f = _fwd_kblocked(f"ffn_down_{tag}", act, wd4)
    return (n, gu, act, wg, wd4), f


def _ffn_bwd(tag, dh_out, h_in, f, saved, g_pre, g_post, send):
    n, gu, act, wg, wd4 = saved
    df, (dg_post,) = _rms_bwd(f"ffn_postnorm_bwd_{tag}", f, [(g_post, dh_out)], None, BF16)
    tok = send(f"down_{tag}", _bwd_w_kblocked(f"ffn_down_dw_{tag}", act, df).reshape(NDEV, DFF // NDEV, D))
    dgu = _down_dx_act_bwd(f"ffn_down_dx_{tag}", df, wd4, gu, tok).reshape(NDEV, S, FB)
    tok = send(f"gate_up_{tag}", _bwd_w_cols_blocked(f"ffn_gate_up_dw_{tag}", n, dgu))
    dn = _bwd_x_cols_blocked(f"ffn_gate_up_dx_{tag}", dgu, wg, after=tok)
    dh_in, (dg_pre,) = _rms_bwd(f"ffn_prenorm_bwd_{tag}", h_in, [(g_pre, dn)], dh_out, F32)
    return dh_in, dg_pre, dg_post


def kernel(x, positions, mix_norm_pre, mix_norm_post, ffn_norm_pre, ffn_norm_post, ffn_w_gate_up, ffn_w_down, conv_w_in, conv_w, conv_w_out, kv_norm, w_kv, w_q, w_o, loss_target, m_mix_norm_pre, m_mix_norm_post, m_ffn_norm_pre, m_ffn_norm_post, m_ffn_w_gate_up, m_ffn_w_down, m_conv_w_in, m_conv_w, m_conv_w_out, m_kv_norm, m_w_kv, m_w_q, m_w_o, v_mix_norm_pre, v_mix_norm_post, v_ffn_norm_pre, v_ffn_norm_post, v_ffn_w_gate_up, v_ffn_w_down, v_conv_w_in, v_conv_w, v_conv_w_out, v_kv_norm, v_w_kv, v_w_q, v_w_o):
    me = 4 * lax.axis_index("x") + 2 * lax.axis_index("y") + lax.axis_index("c")
    h0 = x.reshape(S, D)
    target = loss_target.reshape(S, D)
    row = lambda a, l: a[l].reshape(1, D)
    g_kv = kv_norm.reshape(1, D)

    cw_shard = jnp.pad(conv_w[0], ((0, 5), (0, 0)))
    names = ["conv_in", "conv_w", "conv_out", "gate_up_0", "down_0", "kv", "q", "o", "gate_up_1", "down_1"]
    shards = [conv_w_in[0], cw_shard, conv_w_out[0], ffn_w_gate_up[0], ffn_w_down[0],
              w_kv, w_q[0], w_o[0], ffn_w_gate_up[1], ffn_w_down[1]]
    shards = [s if n == "conv_w" else s.astype(BF16) for n, s in zip(names, shards)]
    gather = _gather_start("gather_weights_start", shards, me)

    def direct(group, after):
        lands = _gather_wait(f"gather_wait_{group[0]}", gather, [names.index(n) for n in group], after)
        return _relay_start(f"relay_start_{group[0]}", lands)

    def finish(group, relayed, after):
        return dict(zip(group, _relay_wait(f"relay_wait_{group[0]}", relayed, after)))

    sent = {}

    def send(name, grad):
        sent[name], token = _exchange_start(f"scatter_start_{name}", [grad], me, "scatter")
        return token

    groups = [["conv_in", "conv_w", "conv_out"], ["gate_up_0", "down_0"], ["kv", "q"], ["o", "gate_up_1", "down_1"]]
    n0 = _rms_fwd("mix_prenorm_0", h0, [row(mix_norm_pre, 0)])[0]
    w = finish(groups[0], direct(groups[0], n0), n0)
    win_g = w["conv_in"]
    cw = w["conv_w"].transpose(1, 0, 2).reshape(8, D)
    wout = w["conv_out"].reshape(D, D)
    z = _fwd_cols("conv_in", n0, win_g)
    pre = _conv_fwd("conv_gate", z, cw)
    relayed = direct(groups[1], pre)
    y0 = _fwd_rows("conv_out", pre, wout)
    h1, (n1,) = _resid_rms("mix_postnorm_0", h0, y0, row(mix_norm_post, 0), [row(ffn_norm_pre, 0)])
    w = finish(groups[1], relayed, n1)
    ffn0, f0 = _ffn_fwd("0", n1, w["gate_up_0"], w["down_0"])
    relayed = direct(groups[2], ffn0[2])
    h2, (nk, n2) = _resid_rms("ffn_postnorm_0", h1, f0, row(ffn_norm_post, 0), [g_kv, row(mix_norm_pre, 1)])

    w = finish(groups[2], relayed, nk)
    wkv = w["kv"].transpose(1, 0, 2).reshape(D, 2 * QW)
    wq = w["q"].transpose(1, 0, 2).reshape(D, QW)
    half = HEAD_DIM // 2
    inv_freq = ROPE_THETA ** (-jnp.arange(half, dtype=F32) / half)
    tables = _rope_tables("rope_tables", positions.reshape(S, 1), jnp.tile(inv_freq, 4).reshape(1, 128))
    qc, kc, vc, o_c, lse_c = [], [], [], [], []
    for g, d in enumerate(DILATIONS):
        kc.append(_proj_classes(f"k_proj_{g}", nk, wkv, g, d, tables, 1.0))
        vc.append(_proj_classes(f"v_proj_{g}", nk, wkv, len(DILATIONS) + g, d, None, None))
    relayed = direct(groups[3], vc[-1])
    for g, d in enumerate(DILATIONS):
        qc.append(_proj_classes(f"q_proj_{g}", n2, wq, g, d, tables, HEAD_DIM ** -0.5))
        o_g, lse_g = _attn_fwd(f"attn_fwd_{g}", qc[g], kc[g], vc[g], d)
        o_c.append(o_g)
        lse_c.append(lse_g)
    o_mix = _mix_fwd("attn_mix", o_c, lse_c)
    w = finish(groups[3], relayed, o_mix)
    wo = w["o"].reshape(D, D)
    y1 = _fwd_rows("attn_out", o_mix, wo)
    h3, (n3,) = _resid_rms("mix_postnorm_1", h2, y1, row(mix_norm_post, 1), [row(ffn_norm_pre, 1)])
    ffn1, f1 = _ffn_fwd("1", n3, w["gate_up_1"], w["down_1"])

    dh4, sq = _resid_rms_loss("ffn_postnorm_1_loss", h3, f1, row(ffn_norm_post, 1), target)
    loss = lax.psum(jnp.sum(sq) * (0.5 / D), ("x", "y", "c"))

    dh3, dg_fpre1, dg_fpost1 = _ffn_bwd(
        "1", dh4, h3, f1, ffn1, row(ffn_norm_pre, 1), row(ffn_norm_post, 1), send)
    dy1, (dg_mpost1,) = _rms_bwd("mix_postnorm_bwd_1", y1, [(row(mix_norm_post, 1), dh3)], None, BF16)
    tok = send("o", _bwd_w_rows("attn_out_dw", o_mix, dy1).reshape(NDEV, D // NDEV, D))
    do = _bwd_x_rows("attn_out_dx", dy1, wo, F32, after=tok)
    lane = jnp.arange(128)
    ones_blockdiag = (lane[:, None] // HEAD_DIM == lane[None, :] // HEAD_DIM).astype(BF16)
    mixed = _mix_bwd("attn_mix_bwd", do, o_c, lse_c, ones_blockdiag)
    branch_grads = [_attn_bwd(f"attn_bwd_{g}", qc[g], kc[g], vc[g], mixed[g], lse_c[g], mixed[3 + g], d)
                    for g, d in enumerate(DILATIONS)]
    dq_raw, dkv = _attn_bwd_post("attn_bwd_post", branch_grads, *tables)
    tok = send("kv", _bwd_w_cols("kv_proj_dw", nk, dkv, 2 * QW // NDEV))
    dnk = _bwd_x_plain("kv_proj_dx", dkv, wkv, 2, after=tok)
    tok = send("q", _bwd_w_cols("q_proj_dw", n2, dq_raw, QW // NDEV))
    dn2 = _bwd_x_plain("q_proj_dx", dq_raw, wq, 1, after=tok)
    dh2, (dg_kv, dg_mpre1) = _rms_bwd("kv_and_mix_prenorm_bwd_1", h2,
                                      [(g_kv, dnk), (row(mix_norm_pre, 1), dn2)], dh3, F32)

    dh1, dg_fpre0, dg_fpost0 = _ffn_bwd(
        "0", dh2, h1, f0, ffn0, row(ffn_norm_pre, 0), row(ffn_norm_post, 0), send)
    dy0, (dg_mpost0,) = _rms_bwd("mix_postnorm_bwd_0", y0, [(row(mix_norm_post, 0), dh1)], None, BF16)
    tok = send("conv_out", _bwd_w_rows("conv_out_dw", pre, dy0).reshape(NDEV, D // NDEV, D))
    dpre = _bwd_x_rows("conv_out_dx", dy0, wout, BF16, after=tok)
    dz, dcw = _conv_bwd("conv_gate_bwd", z, dpre, cw)
    tok = send("conv_in", _bwd_w_cols("conv_in_dw", n0, dz, 3 * D // NDEV))
    dn0 = _bwd_x_plain("conv_in_dx", dz, win_g.transpose(1, 0, 2).reshape(D, 3 * D), 1, after=tok)
    dh0, (dg_mpre0,) = _rms_bwd("mix_prenorm_bwd_0", h0, [(row(mix_norm_pre, 0), dn0)], dh1, F32)

    small = _pack_small("pack_small_grads", [dg_mpre0, dg_mpre1, dg_mpost0, dg_mpost1, dg_fpre0, dg_fpre1,
                                             dg_fpost0, dg_fpost1, dg_kv], dcw)
    small_all = _exchange("gather_small_grads", [small], "gather")[0]

    done = [small_all]

    def upd(tag, w, m, v):
        parts = _exchange_wait(f"scatter_wait_{tag}", sent[tag], 0, done[-1], "scatter")
        shape = w.shape
        flat = lambda a: a.reshape(parts.shape[1:])
        res = _adamw(f"adamw_{tag}", parts, flat(w), flat(m), flat(v))
        done.append(res[0])
        return [r.reshape(shape) for r in res]

    def upd_layer(tag, l, w, m, v):
        return upd(f"{tag}_{l}", w[l], m[l], v[l])

    def stack(per_layer):
        return [jnp.stack([per_layer[0][i], per_layer[1][i]]) for i in range(4)]

    vec = lambda a: a.reshape(1, D)
    gain_res, taps = _adamw_gains("adamw_gains", small_all, [
        (mix_norm_pre, m_mix_norm_pre, v_mix_norm_pre), (mix_norm_post, m_mix_norm_post, v_mix_norm_post),
        (ffn_norm_pre, m_ffn_norm_pre, v_ffn_norm_pre), (ffn_norm_post, m_ffn_norm_post, v_ffn_norm_post),
        (vec(kv_norm), vec(m_kv_norm), vec(v_kv_norm))])
    dcw_mine = lax.dynamic_slice(taps, (0, me * 128), (8, 128))
    pad8 = lambda a, fill: jnp.pad(a[0], ((0, 5), (0, 0)), constant_values=fill)
    cw_res = [r[0:3].reshape(1, 3, 128) for r in
              _adamw("adamw_conv_w", dcw_mine.reshape(1, 8, 128), cw_shard, pad8(m_conv_w, 0.0), pad8(v_conv_w, 1.0))]

    res = {
        "mix_norm_pre": gain_res[0],
        "mix_norm_post": gain_res[1],
        "ffn_norm_pre": gain_res[2],
        "ffn_norm_post": gain_res[3],
        "kv_norm": [r.reshape(D) for r in gain_res[4]],
        "conv_w": cw_res,
    }
    down, gate_up = {}, {}
    down[1] = upd_layer("down", 1, ffn_w_down, m_ffn_w_down, v_ffn_w_down)
    gate_up[1] = upd_layer("gate_up", 1, ffn_w_gate_up, m_ffn_w_gate_up, v_ffn_w_gate_up)
    res["w_o"] = upd("o", w_o, m_w_o, v_w_o)
    res["w_q"] = upd("q", w_q, m_w_q, v_w_q)
    res["w_kv"] = upd("kv", w_kv, m_w_kv, v_w_kv)
    down[0] = upd_layer("down", 0, ffn_w_down, m_ffn_w_down, v_ffn_w_down)
    gate_up[0] = upd_layer("gate_up", 0, ffn_w_gate_up, m_ffn_w_gate_up, v_ffn_w_gate_up)
    res["ffn_w_down"] = stack(down)
    res["ffn_w_gate_up"] = stack(gate_up)
    res["conv_w_out"] = upd("conv_out", conv_w_out, m_conv_w_out, v_conv_w_out)
    res["conv_w_in"] = upd("conv_in", conv_w_in, m_conv_w_in, v_conv_w_in)
    order = ["mix_norm_pre", "mix_norm_post", "ffn_norm_pre", "ffn_norm_post", "ffn_w_gate_up", "ffn_w_down",
             "conv_w_in", "conv_w", "conv_w_out", "kv_norm", "w_kv", "w_q", "w_o"]
    out = [loss, dh0.reshape(1, S, D)]
    for i in range(4):
        out += [res[name][i] for name in order]
    return tuple(out)
```

```python
import jax
import jax.numpy as jnp
from jax import lax
from jax.experimental import pallas as pl
from jax.experimental.pallas import tpu as pltpu

F32 = jnp.float32
BF16 = jnp.bfloat16

S = 4096
D = 1024
NDEV = 8
HEAD_DIM = 64
QW = 3072
DFF = 2816
FB = 704
NFB = 4
BRANCHES = ((128, 1), (512, 4), (2048, 16))
BAND = 128
ROPE_THETA = 10000.0
RMS_EPS = 1e-6
NEG_INF = -1e30
ADAM_LR, ADAM_B1, ADAM_B2, ADAM_EPS, ADAM_WD, ADAM_STEP = 0.001, 0.9, 0.999, 1e-08, 0.01, 10

VMEM_LIMIT_BYTES = 52 * 1024 * 1024
ROW_TILE = 512
MESH = pl.DeviceIdType.MESH


def _cparams(ngrid):
    return pltpu.CompilerParams(dimension_semantics=("arbitrary",) * ngrid,
                                vmem_limit_bytes=VMEM_LIMIT_BYTES)


def _sds(shape, dtype):
    return jax.ShapeDtypeStruct(tuple(shape), dtype)


_DIMS = {"nn": (((1,), (0,)), ((), ())),
         "nt": (((1,), (1,)), ((), ())),
         "tn": (((0,), (0,)), ((), ()))}


def _matmul(name, a, b, *, mode, grid, a_blk, a_map, b_blk, b_map, o_shape, o_blk, o_map, out_dtype, after=None,
            out_groups=1):
    nk = grid[2]
    dims = _DIMS[mode]
    acc_shape = tuple(s for s in o_blk if s is not None)
    if out_groups > 1:
        acc_shape = (acc_shape[1], out_groups * acc_shape[2])
    extra = [] if after is None else [after]

    def store(o_ref, val):
        if out_groups == 1:
            o_ref[...] = val.astype(o_ref.dtype)
        else:
            n = o_ref.shape[-1]
            for grp in range(out_groups):
                o_ref[grp] = val[:, grp * n:(grp + 1) * n].astype(o_ref.dtype)

    def body(a_ref, b_ref, *rest):
        o_ref, scratch = rest[len(extra)], rest[len(extra) + 1:]
        part = lax.dot_general(a_ref[...], b_ref[...], dims, preferred_element_type=F32)
        if nk == 1:
            store(o_ref, part)
            return
        acc_ref = scratch[0]
        k = pl.program_id(2)

        @pl.when(k == 0)
        def _():
            acc_ref[...] = part

        @pl.when(k > 0)
        def _():
            acc_ref[...] += part

        @pl.when(k == nk - 1)
        def _():
            store(o_ref, acc_ref[...])

    return pl.pallas_call(
        body, name=name, grid=grid,
        in_specs=[pl.BlockSpec(a_blk, a_map), pl.BlockSpec(b_blk, b_map)] + [pl.BlockSpec(memory_space=pl.ANY)] * len(extra),
        out_specs=pl.BlockSpec(o_blk, o_map),
        out_shape=_sds(o_shape, out_dtype),
        scratch_shapes=[] if nk == 1 else [pltpu.VMEM(acc_shape, F32)],
        compiler_params=_cparams(3),
    )(a, b, *extra)


TM = 1024
TK = S


def _fwd_rows(name, a, w, out_dtype=F32):
    kdim, n = w.shape
    tn = 512
    return _matmul(name, a, w, mode="nn", grid=(S // TM, n // tn, 1),
                   a_blk=(TM, kdim), a_map=lambda i, j, k: (i, 0),
                   b_blk=(kdim, tn), b_map=lambda i, j, k: (0, j),
                   o_shape=(S, n), o_blk=(TM, tn), o_map=lambda i, j, k: (i, j), out_dtype=out_dtype)


def _fwd_kblocked(name, a4, w4):
    nb, _, kb = a4.shape
    n = w4.shape[2]

    def body(a_ref, w_ref, o_ref):
        acc = _dot_nn(a_ref[0], w_ref[0])
        for j in range(1, nb):
            acc = acc + _dot_nn(a_ref[j], w_ref[j])
        o_ref[...] = acc

    return pl.pallas_call(
        body, name=name, grid=(S // TM,),
        in_specs=[pl.BlockSpec((nb, TM, kb), lambda i: (0, i, 0)), pl.BlockSpec((nb, kb, n), lambda i: (0, 0, 0))],
        out_specs=pl.BlockSpec((TM, n), lambda i: (i, 0)), out_shape=_sds((S, n), F32),
        compiler_params=_cparams(1),
    )(a4, w4)


def _bwd_x_cols_blocked(name, dy8, wg, after):
    _, kdim, n = wg.shape
    nk = NDEV // 2

    def body(a_ref, b_ref, after_ref, o_ref, acc_ref):
        k = pl.program_id(1)
        part = _dot_nt(a_ref[0], b_ref[0]) + _dot_nt(a_ref[1], b_ref[1])

        @pl.when(k == 0)
        def _():
            acc_ref[...] = part

        @pl.when(k > 0)
        def _():
            acc_ref[...] += part

        @pl.when(k == nk - 1)
        def _():
            o_ref[...] = acc_ref[...]

    return pl.pallas_call(
        body, name=name, grid=(S // TM, nk),
        in_specs=[pl.BlockSpec((2, None, TM, n), lambda i, k: (0, k, i, 0)),
                  pl.BlockSpec((2, None, kdim, n), lambda i, k: (0, k, 0, 0)),
                  pl.BlockSpec(memory_space=pl.ANY)],
        out_specs=pl.BlockSpec((TM, kdim), lambda i, k: (i, 0)), out_shape=_sds((S, kdim), F32),
        scratch_shapes=[pltpu.VMEM((TM, kdim), F32)],
        compiler_params=_cparams(2),
    )(dy8.reshape(2, nk, S, n), wg.reshape(2, nk, kdim, n), after)


def _bwd_x_rows(name, dy, w, out_dtype, after=None):
    kdim, n = w.shape
    tkk = 512
    return _matmul(name, dy, w, mode="nt", grid=(S // TM, kdim // tkk, 1),
                   a_blk=(TM, n), a_map=lambda i, j, k: (i, 0),
                   b_blk=(tkk, n), b_map=lambda i, j, k: (j, 0),
                   o_shape=(S, kdim), o_blk=(TM, tkk), o_map=lambda i, j, k: (i, j), out_dtype=out_dtype, after=after)


DW_COLS = 768


def _bwd_w_cols(name, a, dy, n):
    kdim = a.shape[1]
    groups = DW_COLS // n
    return _matmul(name, a, dy, mode="tn", grid=(1, NDEV // groups, S // TK),
                   a_blk=(TK, kdim), a_map=lambda i, j, k: (k, 0),
                   b_blk=(TK, DW_COLS), b_map=lambda i, j, k: (k, j),
                   o_shape=(NDEV, kdim, n), o_blk=(groups, kdim, n) if groups > 1 else (None, kdim, n),
                   o_map=lambda i, j, k: (j, 0, 0), out_dtype=BF16, out_groups=groups)


def _bwd_x_plain(name, dy, w, after=None):
    kdim, n = w.shape
    tm = TM if n <= 3 * D else TM // 2
    return _matmul(name, dy, w, mode="nt", grid=(S // tm, 1, 1),
                   a_blk=(tm, n), a_map=lambda i, j, k: (i, 0),
                   b_blk=(kdim, n), b_map=lambda i, j, k: (0, 0),
                   o_shape=(S, kdim), o_blk=(tm, kdim), o_map=lambda i, j, k: (i, 0), out_dtype=F32, after=after)


def _bwd_w_cols_blocked(name, a, dy8):
    kdim = a.shape[1]
    n = dy8.shape[2]
    return _matmul(name, a, dy8, mode="tn", grid=(1, NDEV, S // TK),
                   a_blk=(TK, kdim), a_map=lambda i, j, k: (k, 0),
                   b_blk=(None, TK, n), b_map=lambda i, j, k: (j, k, 0),
                   o_shape=(NDEV, kdim, n), o_blk=(None, kdim, n), o_map=lambda i, j, k: (j, 0, 0), out_dtype=BF16)


def _bwd_w_rows(name, a, dy):
    kdim = a.shape[1]
    n = dy.shape[1]
    tmm = 512
    return _matmul(name, a, dy, mode="tn", grid=(kdim // tmm, 1, S // TK),
                   a_blk=(TK, tmm), a_map=lambda i, j, k: (k, i),
                   b_blk=(TK, n), b_map=lambda i, j, k: (k, 0),
                   o_shape=(kdim, n), o_blk=(tmm, n), o_map=lambda i, j, k: (i, 0), out_dtype=BF16)


def _bwd_w_kblocked(name, a4, dy):
    nb, _, kb = a4.shape
    n = dy.shape[1]
    return _matmul(name, a4, dy, mode="tn", grid=(nb, 1, S // TK),
                   a_blk=(None, TK, kb), a_map=lambda i, j, k: (i, k, 0),
                   b_blk=(TK, n), b_map=lambda i, j, k: (k, 0),
                   o_shape=(nb, kb, n), o_blk=(None, kb, n), o_map=lambda i, j, k: (i, 0, 0), out_dtype=BF16)


def _rstd(x):
    return lax.rsqrt(jnp.mean(x * x, axis=-1, keepdims=True) + RMS_EPS)


def _row_spec(tm=ROW_TILE, width=D):
    return pl.BlockSpec((tm, width), lambda i: (i, 0))


def _vec_spec(rows=1, width=D):
    return pl.BlockSpec((rows, width), lambda i: (0, 0))


def _rms_fwd(name, x, gains):
    n = len(gains)

    def body(x_ref, *refs):
        x_val = x_ref[...]
        xh = x_val * _rstd(x_val)
        for g_ref, o_ref in zip(refs[:n], refs[n:]):
            o_ref[...] = (xh * g_ref[...]).astype(o_ref.dtype)

    outs = pl.pallas_call(
        body, name=name, grid=(S // ROW_TILE,),
        in_specs=[_row_spec()] + [_vec_spec()] * n,
        out_specs=[_row_spec()] * n,
        out_shape=[_sds((S, D), BF16)] * n,
        compiler_params=_cparams(1),
    )(x, *gains)
    return list(outs)


def _resid_rms(name, h, y, g, next_gains):
    n = len(next_gains)

    def body(h_ref, y_ref, g_ref, *refs):
        y_val = y_ref[...]
        h_new = h_ref[...] + (y_val * _rstd(y_val)) * g_ref[...]
        refs[n][...] = h_new
        hh = h_new * _rstd(h_new)
        for g2_ref, o_ref in zip(refs[:n], refs[n + 1:]):
            o_ref[...] = (hh * g2_ref[...]).astype(o_ref.dtype)

    outs = pl.pallas_call(
        body, name=name, grid=(S // ROW_TILE,),
        in_specs=[_row_spec(), _row_spec(), _vec_spec()] + [_vec_spec()] * n,
        out_specs=[_row_spec()] * (n + 1), out_shape=[_sds((S, D), F32)] + [_sds((S, D), BF16)] * n,
        compiler_params=_cparams(1),
    )(h, y, g, *next_gains)
    return outs[0], list(outs[1:])


def _resid_rms_loss(name, h, y, g, target):
    def body(h_ref, y_ref, g_ref, t_ref, dh_ref, part_ref):
        y_val = y_ref[...]
        e = h_ref[...] + (y_val * _rstd(y_val)) * g_ref[...] - t_ref[...]
        dh_ref[...] = e * (1.0 / D)
        part = jnp.sum(e * e, axis=0, keepdims=True)
        step = pl.program_id(0)

        @pl.when(step == 0)
        def _():
            part_ref[...] = part

        @pl.when(step > 0)
        def _():
            part_ref[...] += part

    return pl.pallas_call(
        body, name=name, grid=(S // ROW_TILE,),
        in_specs=[_row_spec(), _row_spec(), _vec_spec(), _row_spec()],
        out_specs=[_row_spec(), _vec_spec()],
        out_shape=[_sds((S, D), F32), _sds((1, D), F32)],
        compiler_params=_cparams(1),
    )(h, y, g, target)


def _rms_bwd(name, x, pairs, dres, out_dtype):
    n = len(pairs)
    has_res = dres is not None

    def body(x_ref, *refs):
        g_refs = refs[0:2 * n:2]
        dn_refs = refs[1:2 * n:2]
        pos = 2 * n
        res_ref = refs[pos] if has_res else None
        pos += int(has_res)
        dx_ref = refs[pos]
        dg_refs = refs[pos + 1:]
        step = pl.program_id(0)
        x_val = x_ref[...]
        r = _rstd(x_val)
        xh = x_val * r
        acc = res_ref[...] if has_res else jnp.zeros_like(x_val)
        for g_ref, dn_ref, dg_ref in zip(g_refs, dn_refs, dg_refs):
            dn = dn_ref[...].astype(F32)
            dxh = dn * g_ref[...]
            acc = acc + r * (dxh - xh * jnp.mean(dxh * xh, axis=-1, keepdims=True))
            part = jnp.sum(dn * xh, axis=0, keepdims=True)

            @pl.when(step == 0)
            def _():
                dg_ref[...] = jnp.zeros_like(dg_ref)

            dg_ref[0:1, :] += part

        dx_ref[...] = acc.astype(dx_ref.dtype)

    operands = [x]
    in_specs = [_row_spec()]
    for g, dn in pairs:
        operands += [g, dn]
        in_specs += [_vec_spec(), _row_spec()]
    if has_res:
        operands.append(dres)
        in_specs.append(_row_spec())
    outs = pl.pallas_call(
        body, name=name, grid=(S // ROW_TILE,),
        in_specs=in_specs,
        out_specs=[_row_spec()] + [_vec_spec(8)] * n,
        out_shape=[_sds((S, D), out_dtype)] + [_sds((8, D), F32)] * n,
        compiler_params=_cparams(1),
    )(*operands)
    return outs[0], list(outs[1:])


def _shift_down(u, prev8, k):
    r = pltpu.roll(u, k, 0)
    p = pltpu.roll(prev8, k, 0)
    row = lax.broadcasted_iota(jnp.int32, prev8.shape, 0)
    top = jnp.where(row < k, p, r[0:8])
    return jnp.concatenate([top, r[8:]], axis=0)


def _shift_up(u, next8, k):
    tm = u.shape[0]
    r = pltpu.roll(u, tm - k, 0)
    p = pltpu.roll(next8, 8 - k, 0)
    row = lax.broadcasted_iota(jnp.int32, next8.shape, 0)
    bot = jnp.where(row >= 8 - k, p, r[tm - 8:tm])
    return jnp.concatenate([r[:tm - 8], bot], axis=0)


CONV_TILE = 512


def _halo_prev(col):
    return pl.BlockSpec((8, D), lambda i: (jnp.maximum(i * (CONV_TILE // 8) - 1, 0), col))


def _halo_next(col):
    last = S // 8 - 1
    return pl.BlockSpec((8, D), lambda i: (jnp.minimum((i + 1) * (CONV_TILE // 8), last), col))


def _conv_fwd(name, z, cw):
    def body(b_ref, c_ref, h_ref, cp_ref, hp_ref, cw_ref, o_ref):
        i = pl.program_id(0)
        u = c_ref[...].astype(F32) * h_ref[...].astype(F32)
        up = cp_ref[...].astype(F32) * hp_ref[...].astype(F32)
        up = jnp.where(i > 0, up, 0.0)
        cv = cw_ref[0:1, :] * _shift_down(u, up, 2) + cw_ref[1:2, :] * _shift_down(u, up, 1) + cw_ref[2:3, :] * u
        o_ref[...] = (b_ref[...].astype(F32) * cv).astype(o_ref.dtype)

    col = lambda c: pl.BlockSpec((CONV_TILE, D), lambda i: (i, c))
    return pl.pallas_call(
        body, name=name, grid=(S // CONV_TILE,),
        in_specs=[col(0), col(1), col(2), _halo_prev(1), _halo_prev(2), _vec_spec(8)],
        out_specs=_row_spec(CONV_TILE), out_shape=_sds((S, D), BF16),
        compiler_params=_cparams(1),
    )(z, z, z, z, z, cw)


def _conv_bwd(name, z, dpre, cw):
    nsteps = S // CONV_TILE

    def body(b_ref, c_ref, h_ref, cp_ref, hp_ref, dp_ref, dpn_ref, bn_ref, cw_ref, dz_ref, dcw_ref):
        i = pl.program_id(0)
        b = b_ref[...].astype(F32)
        c = c_ref[...].astype(F32)
        h = h_ref[...].astype(F32)
        dp = dp_ref[...].astype(F32)
        u = c * h
        up = jnp.where(i > 0, cp_ref[...].astype(F32) * hp_ref[...].astype(F32), 0.0)
        s1 = _shift_down(u, up, 1)
        s2 = _shift_down(u, up, 2)
        w0, w1, w2 = cw_ref[0:1, :], cw_ref[1:2, :], cw_ref[2:3, :]
        cv = w0 * s2 + w1 * s1 + w2 * u
        dcv = dp * b
        dcvn = jnp.where(i < nsteps - 1, dpn_ref[...].astype(F32) * bn_ref[...].astype(F32), 0.0)
        du = w2 * dcv + w1 * _shift_up(dcv, dcvn, 1) + w0 * _shift_up(dcv, dcvn, 2)
        dz_ref[:, 0:D] = (dp * cv).astype(dz_ref.dtype)
        dz_ref[:, D:2 * D] = (du * h).astype(dz_ref.dtype)
        dz_ref[:, 2 * D:3 * D] = (du * c).astype(dz_ref.dtype)

        @pl.when(i == 0)
        def _():
            dcw_ref[...] = jnp.zeros_like(dcw_ref)

        dcw_ref[0:1, :] += jnp.sum(dcv * s2, axis=0, keepdims=True)
        dcw_ref[1:2, :] += jnp.sum(dcv * s1, axis=0, keepdims=True)
        dcw_ref[2:3, :] += jnp.sum(dcv * u, axis=0, keepdims=True)

    col = lambda c: pl.BlockSpec((CONV_TILE, D), lambda i: (i, c))
    return pl.pallas_call(
        body, name=name, grid=(nsteps,),
        in_specs=[col(0), col(1), col(2), _halo_prev(1), _halo_prev(2),
                  _row_spec(CONV_TILE), _halo_next(0), _halo_next(0), _vec_spec(8)],
        out_specs=[pl.BlockSpec((CONV_TILE, 3 * D), lambda i: (i, 0)), _vec_spec(8)],
        out_shape=[_sds((S, 3 * D), BF16), _sds((8, D), F32)],
        compiler_params=_cparams(1),
    )(z, z, z, z, z, dpre, dpre, z, cw)


_GU_BLOCK = pl.BlockSpec((2, None, TM, FB), lambda i, j: (0, j, i, 0))


def _gate_up_act(name, a, wg):
    kdim = a.shape[1]

    def body(a_ref, wgate_ref, wup_ref, gu_ref, act_ref):
        x = a_ref[...]
        g = _dot_nn(x, wgate_ref[...])
        u = _dot_nn(x, wup_ref[...])
        gu_ref[0] = g.astype(gu_ref.dtype)
        gu_ref[1] = u.astype(gu_ref.dtype)
        act_ref[...] = (g * jax.nn.sigmoid(g) * u).astype(act_ref.dtype)

    return pl.pallas_call(
        body, name=name, grid=(S // TM, NFB),
        in_specs=[pl.BlockSpec((TM, kdim), lambda i, j: (i, 0)),
                  pl.BlockSpec((None, kdim, FB), lambda i, j: (j, 0, 0)),
                  pl.BlockSpec((None, kdim, FB), lambda i, j: (j + NFB, 0, 0))],
        out_specs=[_GU_BLOCK, pl.BlockSpec((None, TM, FB), lambda i, j: (j, i, 0))],
        out_shape=[_sds((2, NFB, S, FB), BF16), _sds((NFB, S, FB), BF16)],
        compiler_params=_cparams(2),
    )(a, wg, wg)


def _down_dx_act_bwd(name, df, w4, gu, after):
    _, kb, n = w4.shape

    def body(df_ref, w_ref, gu_ref, after_ref, o_ref):
        d = _dot_nt(df_ref[...], w_ref[...])
        g = gu_ref[0].astype(F32)
        u = gu_ref[1].astype(F32)
        sg = jax.nn.sigmoid(g)
        o_ref[0] = (d * u * sg * (1.0 + g * (1.0 - sg))).astype(o_ref.dtype)
        o_ref[1] = (d * g * sg).astype(o_ref.dtype)

    return pl.pallas_call(
        body, name=name, grid=(S // TM, NFB),
        in_specs=[pl.BlockSpec((TM, n), lambda i, j: (i, 0)), pl.BlockSpec((None, kb, n), lambda i, j: (j, 0, 0)),
                  _GU_BLOCK, pl.BlockSpec(memory_space=pl.ANY)],
        out_specs=_GU_BLOCK, out_shape=_sds((2, NFB, S, FB), BF16),
        compiler_params=_cparams(2),
    )(df, w4, gu, after)


def _rope_tables(name, pos_col, inv_freq_row):
    def body(pos_ref, f_ref, cos_ref, sin_ref):
        ang = pos_ref[...].astype(F32) * f_ref[...]
        lane = lax.broadcasted_iota(jnp.int32, ang.shape, 1)
        s = jnp.sin(ang)
        cos_ref[...] = jnp.cos(ang)
        sin_ref[...] = jnp.where((lane % HEAD_DIM) < HEAD_DIM // 2, -s, s)

    tab = pl.BlockSpec((ROW_TILE, 128), lambda i: (i, 0))
    return pl.pallas_call(
        body, name=name, grid=(S // ROW_TILE,),
        in_specs=[pl.BlockSpec((ROW_TILE, 1), lambda i: (i, 0)), _vec_spec(1, 128)],
        out_specs=[tab, tab], out_shape=[_sds((S, 128), F32)] * 2,
        compiler_params=_cparams(1),
    )(pos_col, inv_freq_row)


def _swap_halves(t):
    lane = lax.broadcasted_iota(jnp.int32, t.shape, 1)
    first = (lane % HEAD_DIM) < HEAD_DIM // 2
    return jnp.where(first, pltpu.roll(t, 128 - HEAD_DIM // 2, 1), pltpu.roll(t, HEAD_DIM // 2, 1))


NCHUNK = D // 128


def _chunk(c, base=0):
    return slice(base + c * 128, base + (c + 1) * 128)


def _class_rows(r, d, tm):
    return pl.ds(r, tm // d, stride=d) if d > 1 else slice(None)


def _class_block(d, tm):
    return pl.BlockSpec((tm // d, d * D), lambda i: (i, 0))


def _tokens_from_classes(blk_ref, tmp_ref, d, tm):
    for r in range(d):
        for c in range(NCHUNK):
            tmp_ref[c, _class_rows(r, d, tm), :] = blk_ref[:, _chunk(c, r * D)].astype(F32)


def _classes_from_tokens(tmp_ref, blk_ref, d, tm):
    for r in range(d):
        for c in range(NCHUNK):
            blk_ref[:, _chunk(c, r * D)] = tmp_ref[c, _class_rows(r, d, tm), :].astype(blk_ref.dtype)


def _proj_classes(name, a, w, col, d, tables, scale):
    kdim = a.shape[1]
    rope = tables is not None

    def body(a_ref, w_ref, *refs):
        if rope:
            cos_ref, sin_ref, o_ref, tmp_ref = refs
        else:
            o_ref, tmp_ref = refs
        acc = _dot_nn(a_ref[...], w_ref[...])
        for c in range(NCHUNK):
            tmp_ref[c] = acc[:, _chunk(c)]
        for r in range(d):
            rows = _class_rows(r, d, TM)
            if rope:
                cs = cos_ref[rows, :]
                sn = sin_ref[rows, :]
            for c in range(NCHUNK):
                x = tmp_ref[c, rows, :]
                if rope:
                    x = (x * cs + _swap_halves(x) * sn) * scale
                o_ref[:, _chunk(c, r * D)] = x.astype(o_ref.dtype)

    tab = pl.BlockSpec((TM, 128), lambda i: (i, 0))
    return pl.pallas_call(
        body, name=name, grid=(S // TM,),
        in_specs=[pl.BlockSpec((TM, kdim), lambda i: (i, 0)), pl.BlockSpec((kdim, D), lambda i: (0, col))]
                 + ([tab, tab] if rope else []),
        out_specs=_class_block(d, TM), out_shape=_sds((S // d, d * D), BF16),
        scratch_shapes=[pltpu.VMEM((NCHUNK, TM, 128), F32)],
        compiler_params=_cparams(1),
    )(a, w, *(tables if rope else ()))


ATTN_CHAINS = 4


def _attn_units(d):
    nblk = S // d // BAND
    return max(1, 2 * ATTN_CHAINS // nblk)


def _class_spec(d):
    return pl.BlockSpec((S // d, 128 * _attn_units(d)), lambda cb: (0, cb))


def _dot_nt(a, b):
    return lax.dot_general(a, b, _DIMS["nt"], preferred_element_type=F32)


def _dot_tn(a, b):
    return lax.dot_general(a, b, _DIMS["tn"], preferred_element_type=F32)


def _dot_nn(a, b):
    return lax.dot_general(a, b, _DIMS["nn"], preferred_element_type=F32)


def _band_mask(nkeys):
    qi = lax.broadcasted_iota(jnp.int32, (2 * BAND, nkeys), 0) % BAND
    kj = lax.broadcasted_iota(jnp.int32, (2 * BAND, nkeys), 1)
    if nkeys == BAND:
        return kj <= qi
    dist = qi + BAND - kj
    return (dist >= 0) & (dist <= BAND)


def _stack_heads(x):
    row = lax.broadcasted_iota(jnp.int32, (2 * BAND, 128), 0)
    lane = lax.broadcasted_iota(jnp.int32, (2 * BAND, 128), 1)
    keep = (row < BAND) == (lane < HEAD_DIM)
    return jnp.where(keep, jnp.concatenate([x, x], axis=0), jnp.zeros((), x.dtype))


def _unstack(x2):
    first_head = lax.broadcasted_iota(jnp.int32, (BAND, 128), 1) < HEAD_DIM
    return jnp.where(first_head, x2[:BAND], x2[BAND:])


def _for_later_blocks(nblk, units, fn):
    all_lanes = [slice(u * 128, (u + 1) * 128) for u in range(units)]
    unroll = max(1, ATTN_CHAINS // units)
    trips = (nblk - 1) // unroll
    if trips > 1:
        def step(i, carry):
            for j in range(unroll):
                for lanes in all_lanes:
                    fn(pl.multiple_of((1 + i * unroll + j) * BAND, BAND), lanes)
            return carry

        lax.fori_loop(0, trips, step, 0)
    else:
        trips = 0
    for sb in range(1 + trips * unroll, nblk):
        for lanes in all_lanes:
            fn(sb * BAND, lanes)


def _attn_fwd(name, q, k, v, d):
    nblk = S // d // BAND
    units = _attn_units(d)

    def body(q_ref, k_ref, v_ref, o_ref, lse_ref):
        def block(r0, k0, nkeys, lanes):
            q2 = _stack_heads(q_ref[pl.ds(r0, BAND), lanes])
            s = jnp.where(_band_mask(nkeys), _dot_nt(q2, k_ref[pl.ds(k0, nkeys), lanes]), NEG_INF)
            m = jnp.max(s, axis=-1, keepdims=True)
            p = jnp.exp(s - m)
            l = jnp.sum(p, axis=-1, keepdims=True)
            o2 = _dot_nn(p.astype(BF16), v_ref[pl.ds(k0, nkeys), lanes]) / l
            lse2 = jnp.broadcast_to(m + jnp.log(l), (2 * BAND, 128))
            o_ref[pl.ds(r0, BAND), lanes] = _unstack(o2).astype(o_ref.dtype)
            lse_ref[pl.ds(r0, BAND), lanes] = _unstack(lse2)

        for u in range(units):
            block(0, 0, BAND, slice(u * 128, (u + 1) * 128))

        _for_later_blocks(nblk, units, lambda r0, lanes: block(r0, r0 - BAND, 2 * BAND, lanes))

    spec = _class_spec(d)
    return pl.pallas_call(
        body, name=name, grid=(8 * d // units,),
        in_specs=[spec] * 3, out_specs=[spec] * 2,
        out_shape=[_sds((S // d, d * D), BF16), _sds((S // d, d * D), F32)],
        compiler_params=_cparams(1),
    )(q, k, v)


def _attn_bwd(name, q, k, v, do, lse, dd, d):
    nblk = S // d // BAND
    units = _attn_units(d)

    def body(q_ref, k_ref, v_ref, do_ref, lse_ref, dd_ref, dq_ref, dk_ref, dv_ref):
        def column(ref, r0, lanes):
            rows = pl.ds(r0, BAND)
            first = slice(lanes.start, lanes.start + 1)
            second = slice(lanes.start + HEAD_DIM, lanes.start + HEAD_DIM + 1)
            return jnp.concatenate([ref[rows, first], ref[rows, second]], axis=0)

        def block(r0, k0, nkeys, lanes, first):
            q2 = _stack_heads(q_ref[pl.ds(r0, BAND), lanes])
            do2 = _stack_heads(do_ref[pl.ds(r0, BAND), lanes])
            kk = k_ref[pl.ds(k0, nkeys), lanes]
            vv = v_ref[pl.ds(k0, nkeys), lanes]
            s = jnp.where(_band_mask(nkeys), _dot_nt(q2, kk), NEG_INF)
            p = jnp.exp(s - column(lse_ref, r0, lanes))
            ds = (p * (_dot_nt(do2, vv) - column(dd_ref, r0, lanes))).astype(BF16)
            dq_ref[pl.ds(r0, BAND), lanes] = _unstack(_dot_nn(ds, kk)).astype(dq_ref.dtype)
            dk_part = _dot_tn(ds, q2)
            dv_part = _dot_tn(p.astype(BF16), do2)
            if first:
                dk_ref[pl.ds(k0, nkeys), lanes] = dk_part
                dv_ref[pl.ds(k0, nkeys), lanes] = dv_part
            else:
                dk_ref[pl.ds(k0, BAND), lanes] += dk_part[:BAND]
                dv_ref[pl.ds(k0, BAND), lanes] += dv_part[:BAND]
                dk_ref[pl.ds(k0 + BAND, BAND), lanes] = dk_part[BAND:]
                dv_ref[pl.ds(k0 + BAND, BAND), lanes] = dv_part[BAND:]

        for u in range(units):
            block(0, 0, BAND, slice(u * 128, (u + 1) * 128), True)

        _for_later_blocks(nblk, units, lambda r0, lanes: block(r0, r0 - BAND, 2 * BAND, lanes, False))

    spec = _class_spec(d)
    return pl.pallas_call(
        body, name=name, grid=(8 * d // units,),
        in_specs=[spec] * 6, out_specs=[spec] * 3,
        out_shape=[_sds((S // d, d * D), BF16)] + [_sds((S // d, d * D), F32)] * 2,
        compiler_params=_cparams(1),
    )(q, k, v, do, lse, dd)


MIX_TILE = 256
DILATIONS = tuple(d for _, d in BRANCHES)


def _branch_weights(la, lb, lc):
    m = jnp.maximum(jnp.maximum(la, lb), lc)
    ea, eb, ec = jnp.exp(la - m), jnp.exp(lb - m), jnp.exp(lc - m)
    den = ea + eb + ec
    return ea / den, eb / den, ec / den


def _mix_operands(outs, lses):
    specs = [_class_block(d, MIX_TILE) for d in DILATIONS] * 2
    scratch = [pltpu.VMEM((NCHUNK, MIX_TILE, 128), F32)] * 4
    return list(outs) + list(lses), specs, scratch


def _mix_fwd(name, outs, lses):
    def body(o0, o1, o2, l0, l1, l2, o_ref, to1, to2, tl1, tl2):
        for blk, tmp, d in ((o1, to1, DILATIONS[1]), (o2, to2, DILATIONS[2]), (l1, tl1, DILATIONS[1]), (l2, tl2, DILATIONS[2])):
            _tokens_from_classes(blk, tmp, d, MIX_TILE)
        for c in range(NCHUNK):
            wa, wb, wc = _branch_weights(l0[:, _chunk(c)], tl1[c], tl2[c])
            o_ref[:, _chunk(c)] = (wa * o0[:, _chunk(c)].astype(F32) + wb * to1[c] + wc * to2[c]).astype(o_ref.dtype)

    operands, specs, scratch = _mix_operands(outs, lses)
    return pl.pallas_call(
        body, name=name, grid=(S // MIX_TILE,),
        in_specs=specs, out_specs=_row_spec(MIX_TILE), out_shape=_sds((S, D), BF16),
        scratch_shapes=scratch, compiler_params=_cparams(1),
    )(*operands)


def _head_sum(x, ones_blockdiag):
    hi = x.astype(BF16)
    r1 = x - hi.astype(F32)
    mid = r1.astype(BF16)
    lo = (r1 - mid.astype(F32)).astype(BF16)
    return _dot_nn(hi, ones_blockdiag) + _dot_nn(mid, ones_blockdiag) + _dot_nn(lo, ones_blockdiag)


def _mix_bwd(name, do, outs, lses, ones_blockdiag):
    def body(do_ref, o0, o1, o2, l0, l1, l2, ones_ref, d0, d1, d2, t0, t1, t2,
             to1, to2, tl1, tl2, td1, td2, tt1, tt2):
        for blk, tmp, d in ((o1, to1, DILATIONS[1]), (o2, to2, DILATIONS[2]), (l1, tl1, DILATIONS[1]), (l2, tl2, DILATIONS[2])):
            _tokens_from_classes(blk, tmp, d, MIX_TILE)
        ones = ones_ref[...]
        for c in range(NCHUNK):
            w = _branch_weights(l0[:, _chunk(c)], tl1[c], tl2[c])
            dov = do_ref[:, _chunk(c)]
            o = w[0] * o0[:, _chunk(c)].astype(F32) + w[1] * to1[c] + w[2] * to2[c]
            t = _head_sum(dov * o, ones)
            d0[:, _chunk(c)] = (w[0] * dov).astype(d0.dtype)
            t0[:, _chunk(c)] = w[0] * t
            td1[c], tt1[c] = w[1] * dov, w[1] * t
            td2[c], tt2[c] = w[2] * dov, w[2] * t
        for tmp, blk, d in ((td1, d1, DILATIONS[1]), (tt1, t1, DILATIONS[1]), (td2, d2, DILATIONS[2]), (tt2, t2, DILATIONS[2])):
            _classes_from_tokens(tmp, blk, d, MIX_TILE)

    operands, specs, scratch = _mix_operands(outs, lses)
    out_specs = [_class_block(d, MIX_TILE) for d in DILATIONS] * 2
    out_shape = [_sds((S // d, d * D), BF16) for d in DILATIONS] + [_sds((S // d, d * D), F32) for d in DILATIONS]
    return pl.pallas_call(
        body, name=name, grid=(S // MIX_TILE,),
        in_specs=[_row_spec(MIX_TILE)] + specs + [_vec_spec(128, 128)],
        out_specs=out_specs, out_shape=out_shape,
        scratch_shapes=scratch + [pltpu.VMEM((NCHUNK, MIX_TILE, 128), F32)] * 4,
        compiler_params=_cparams(1),
    )(do, *operands, ones_blockdiag)


def _attn_bwd_post(name, grads, cos_t, sin_t):
    tm = MIX_TILE
    scale = HEAD_DIM ** -0.5

    def unrope(x, cs, sn):
        return x * cs - _swap_halves(x) * sn

    def body(*refs):
        in_refs = refs[:9]
        cos_ref, sin_ref, dq_ref, dkv_ref, tmp_ref = refs[9:]
        cs = cos_ref[...]
        sn = sin_ref[...]
        for g, d in enumerate(DILATIONS):
            for which, blk in enumerate(in_refs[3 * g:3 * g + 3]):
                if d > 1:
                    _tokens_from_classes(blk, tmp_ref, d, tm)
                for c in range(NCHUNK):
                    x = tmp_ref[c] if d > 1 else blk[:, _chunk(c)].astype(F32)
                    if which == 0:
                        dq_ref[:, _chunk(c, g * D)] = (unrope(x, cs, sn) * scale).astype(dq_ref.dtype)
                    elif which == 1:
                        dkv_ref[:, _chunk(c, g * D)] = unrope(x, cs, sn).astype(dkv_ref.dtype)
                    else:
                        dkv_ref[:, _chunk(c, QW + g * D)] = x.astype(dkv_ref.dtype)

    operands = [a for branch in grads for a in branch]
    tab = pl.BlockSpec((tm, 128), lambda i: (i, 0))
    return pl.pallas_call(
        body, name=name, grid=(S // tm,),
        in_specs=[_class_block(d, tm) for d in DILATIONS for _ in range(3)] + [tab, tab],
        out_specs=[pl.BlockSpec((tm, QW), lambda i: (i, 0)), pl.BlockSpec((tm, 2 * QW), lambda i: (i, 0))],
        out_shape=[_sds((S, QW), BF16), _sds((S, 2 * QW), BF16)],
        scratch_shapes=[pltpu.VMEM((NCHUNK, tm, 128), F32)],
        compiler_params=_cparams(1),
    )(*operands, cos_t, sin_t)


def _adamw(name, parts, w, m, v):
    n, rows, cols = parts.shape
    tr = rows
    for cand in (256, 176, 128, 64, 32, 16, 8):
        if rows % cand == 0:
            tr = cand
            break
    def body(p_ref, w_ref, m_ref, v_ref, g_ref, d_ref, nm_ref, nv_ref):
        g = p_ref[0].astype(F32)
        for j in range(1, n):
            g = g + p_ref[j].astype(F32)
        g_ref[...] = g
        d_ref[...], nm_ref[...], nv_ref[...] = _adam_update(g, w_ref[...], m_ref[...], v_ref[...])

    blk = pl.BlockSpec((tr, cols), lambda i: (i, 0))
    return pl.pallas_call(
        body, name=name, grid=(rows // tr,),
        in_specs=[pl.BlockSpec((n, tr, cols), lambda i: (0, i, 0)), blk, blk, blk],
        out_specs=[blk] * 4, out_shape=[_sds((rows, cols), F32)] * 4,
        compiler_params=_cparams(1),
    )(parts, w, m, v)


def _adam_update(g, w, m, v):
    c1 = 1.0 / (1.0 - ADAM_B1 ** ADAM_STEP)
    c2 = 1.0 / (1.0 - ADAM_B2 ** ADAM_STEP)
    nm = ADAM_B1 * m + (1.0 - ADAM_B1) * g
    nv = ADAM_B2 * v + (1.0 - ADAM_B2) * (g * g)
    return -ADAM_LR * ((nm * c1) / (jnp.sqrt(nv * c2) + ADAM_EPS) + ADAM_WD * w), nm, nv


GAIN_ROWS = 16


def _pack_small(name, gain_tiles, taps, sq):
    ng = len(gain_tiles)

    def body(*refs):
        o_ref = refs[-1]
        o_ref[...] = jnp.zeros_like(o_ref)
        for i in range(ng):
            o_ref[i:i + 1, :] = refs[i][0:1, :]
        o_ref[ng:ng + 3, :] = refs[ng][0:3, :]
        o_ref[ng + 3:ng + 4, :] = refs[ng + 1][...]

    return pl.pallas_call(body, name=name, out_shape=_sds((GAIN_ROWS, D), F32))(*gain_tiles, taps, sq)


def _adamw_gains(name, parts, params):
    np_ = len(params)
    shapes = [w.shape for w, _, _ in params]

    def body(p_ref, *refs):
        ins, outs = refs[:3 * np_], refs[3 * np_:]

        def total(lo, rows):
            g = p_ref[0, lo:lo + rows, :]
            for j in range(1, NDEV):
                g = g + p_ref[j, lo:lo + rows, :]
            return g

        lo = 0
        for i, shape in enumerate(shapes):
            g = total(lo, shape[0])
            lo += shape[0]
            w_ref, m_ref, v_ref = ins[3 * i:3 * i + 3]
            g_ref, d_ref, nm_ref, nv_ref = outs[4 * i:4 * i + 4]
            g_ref[...] = g
            d_ref[...], nm_ref[...], nv_ref[...] = _adam_update(g, w_ref[...], m_ref[...], v_ref[...])
        taps_ref, loss_ref = outs[-2], outs[-1]
        taps_ref[...] = jnp.zeros_like(taps_ref)
        taps_ref[0:3, :] = total(lo, 3)
        loss_ref[...] = jnp.sum(total(lo + 3, 1), axis=-1, keepdims=True) * (0.5 / D)

    out_shape = [_sds(shape, F32) for shape in shapes for _ in range(4)] + [_sds((8, D), F32), _sds((1, 1), F32)]
    outs = pl.pallas_call(body, name=name, out_shape=out_shape)(parts, *[a for p in params for a in p])
    return [list(outs[4 * i:4 * i + 4]) for i in range(np_)], outs[-2], outs[-1].reshape(())


def _exchange(name, arrays, kind):
    n = len(arrays)
    gather = kind == "gather"
    out_shape = [_sds((NDEV,) + a.shape if gather else a.shape, a.dtype) for a in arrays]

    def body(*refs):
        srcs, outs = refs[:n], refs[n:2 * n]
        send_sems, recv_sems, local_sems = refs[2 * n:]
        x, y, c = lax.axis_index("x"), lax.axis_index("y"), lax.axis_index("c")
        me = 4 * x + 2 * y + c
        pending = []
        for t in range(n):
            own = pltpu.make_async_copy(srcs[t] if gather else srcs[t].at[me], outs[t].at[me], local_sems.at[t])
            own.start()
            pending.append(own)
            for rel in range(1, NDEV):
                px = 1 - x if rel & 4 else x
                py = 1 - y if rel & 2 else y
                pc = 1 - c if rel & 1 else c
                peer = 4 * px + 2 * py + pc
                send = pltpu.make_async_remote_copy(
                    src_ref=srcs[t] if gather else srcs[t].at[peer], dst_ref=outs[t].at[me],
                    send_sem=send_sems.at[t, rel - 1], recv_sem=recv_sems.at[t, rel - 1],
                    device_id=(px, py, pc), device_id_type=MESH)
                send.start()
                arrive = pltpu.make_async_remote_copy(
                    src_ref=srcs[t] if gather else srcs[t].at[me], dst_ref=outs[t].at[peer],
                    send_sem=send_sems.at[t, rel - 1], recv_sem=recv_sems.at[t, rel - 1],
                    device_id=(px, py, pc), device_id_type=MESH)
                pending.append((send, arrive))
        for item in pending:
            if isinstance(item, tuple):
                item[0].wait_send()
                item[1].wait_recv()
            else:
                item.wait()

    any_spec = pl.BlockSpec(memory_space=pl.ANY)
    outs = pl.pallas_call(
        body, name=name,
        in_specs=[any_spec] * n, out_specs=[any_spec] * n, out_shape=out_shape,
        scratch_shapes=[pltpu.SemaphoreType.DMA((n, NDEV - 1)), pltpu.SemaphoreType.DMA((n, NDEV - 1)),
                        pltpu.SemaphoreType.DMA((n,))],
    )(*arrays)
    return list(outs)


_HBM_SPEC = pl.BlockSpec(memory_space=pltpu.HBM)
_SEM_SPEC = pl.BlockSpec(memory_space=pltpu.SEMAPHORE)
_DATAFLOW = pltpu.SideEffectType.DATAFLOW_SIDE_EFFECTING


def _peers():
    x, y, c = lax.axis_index("x"), lax.axis_index("y"), lax.axis_index("c")
    out = []
    for rel in range(1, NDEV):
        px = 1 - x if rel & 4 else x
        py = 1 - y if rel & 2 else y
        pc = 1 - c if rel & 1 else c
        out.append((rel - 1, (px, py, pc), 4 * px + 2 * py + pc))
    return 4 * x + 2 * y + c, out


def _hbm(a):
    return pltpu.HBM(a.shape, a.dtype)


def _own_slot(a, me, kind):
    mine = a[None] if kind == "gather" else lax.dynamic_slice_in_dim(a, me, 1, axis=0)
    shape = (NDEV,) + mine.shape[1:]
    return lax.dynamic_update_slice_in_dim(lax.empty(shape, a.dtype), mine, me, axis=0)


def _exchange_start(name, arrays, me, kind):
    n = len(arrays)
    gather = kind == "gather"
    lands = [_own_slot(a, me, kind) for a in arrays]

    def body(*refs):
        src_refs, land_refs = refs[:n], refs[n:2 * n]
        send_sems, recv_sems = refs[2 * n], refs[2 * n + 1]
        token = refs[-1]
        my_block, peers = _peers()
        for t in range(n):
            for slot, dev, block in peers:
                pltpu.make_async_remote_copy(
                    src_ref=src_refs[t] if gather else src_refs[t].at[block], dst_ref=land_refs[t].at[my_block],
                    send_sem=send_sems.at[t * (NDEV - 1) + slot], recv_sem=recv_sems.at[t * (NDEV - 1) + slot],
                    device_id=dev, device_id_type=MESH).start()
        token[...] = jnp.zeros_like(token)

    operands = [pltpu.with_memory_space_constraint(a, pltpu.HBM) for a in list(arrays) + lands]
    outs = pl.pallas_call(
        body, name=name,
        out_shape=(pltpu.SemaphoreType.DMA((n * (NDEV - 1),)), pltpu.SemaphoreType.DMA((n * (NDEV - 1),)),
                   *[_hbm(a) for a in operands], _sds((8, 128), F32)),
        in_specs=[_HBM_SPEC] * (2 * n),
        out_specs=(_SEM_SPEC, _SEM_SPEC, *[_HBM_SPEC] * (2 * n), pl.BlockSpec(memory_space=pltpu.VMEM)),
        input_output_aliases={i: 2 + i for i in range(2 * n)},
        compiler_params=pltpu.CompilerParams(has_side_effects=_DATAFLOW),
    )(*operands)
    return (outs[0], outs[1], list(outs[2:2 + n]), list(outs[2 + n:2 + 2 * n])), outs[-1]


def _exchange_wait(name, started, t, after, kind):
    send_sems, recv_sems, srcs, lands = started
    gather = kind == "gather"

    def body(src_ref, land_ref, send_ref, recv_ref, after_ref, src_out, land_out):
        _, peers = _peers()
        for slot, dev, block in peers:
            copy = pltpu.make_async_remote_copy(
                src_ref=src_ref if gather else src_ref.at[block], dst_ref=land_ref.at[block],
                send_sem=send_ref.at[t * (NDEV - 1) + slot], recv_sem=recv_ref.at[t * (NDEV - 1) + slot],
                device_id=dev, device_id_type=MESH)
            copy.wait_send()
            copy.wait_recv()

    return pl.pallas_call(
        body, name=name, out_shape=(_hbm(srcs[t]), _hbm(lands[t])),
        in_specs=(_HBM_SPEC, _HBM_SPEC, _SEM_SPEC, _SEM_SPEC, pl.BlockSpec(memory_space=pl.ANY)),
        out_specs=(_HBM_SPEC, _HBM_SPEC), input_output_aliases={0: 0, 1: 1},
        compiler_params=pltpu.CompilerParams(has_side_effects=_DATAFLOW),
    )(srcs[t], lands[t], send_sems, recv_sems, after)[1]


DIRECT_RELS = (1, 2, 4, 6)
RELAY_RELS = (2, 4, 6)


def _rel_peer(rel):
    x, y, c = lax.axis_index("x"), lax.axis_index("y"), lax.axis_index("c")
    px = 1 - x if rel & 4 else x
    py = 1 - y if rel & 2 else y
    pc = 1 - c if rel & 1 else c
    return (px, py, pc), 4 * px + 2 * py + pc


def _gather_start(name, shards, me):
    n, nr = len(shards), len(DIRECT_RELS)
    lands = [_own_slot(a, me, "gather") for a in shards]

    def body(*refs):
        src_refs, land_refs = refs[:n], refs[n:2 * n]
        send_sems, recv_sems = refs[2 * n], refs[2 * n + 1]
        _, my_block = _rel_peer(0)
        for t in range(n):
            for s, rel in enumerate(DIRECT_RELS):
                dev, _ = _rel_peer(rel)
                pltpu.make_async_remote_copy(
                    src_ref=src_refs[t], dst_ref=land_refs[t].at[my_block],
                    send_sem=send_sems.at[t * nr + s], recv_sem=recv_sems.at[t * nr + s],
                    device_id=dev, device_id_type=MESH).start()

    operands = [pltpu.with_memory_space_constraint(a, pltpu.HBM) for a in list(shards) + lands]
    outs = pl.pallas_call(
        body, name=name,
        out_shape=(pltpu.SemaphoreType.DMA((n * nr,)), pltpu.SemaphoreType.DMA((n * nr,)), *[_hbm(a) for a in operands]),
        in_specs=[_HBM_SPEC] * (2 * n), out_specs=(_SEM_SPEC, _SEM_SPEC, *[_HBM_SPEC] * (2 * n)),
        input_output_aliases={i: 2 + i for i in range(2 * n)},
        compiler_params=pltpu.CompilerParams(has_side_effects=_DATAFLOW),
    )(*operands)
    return outs[0], outs[1], list(outs[2:2 + n]), list(outs[2 + n:2 + 2 * n])


def _gather_wait(name, started, ts, after):
    send_sems, recv_sems, srcs, lands = started
    m, nr = len(ts), len(DIRECT_RELS)

    def body(*refs):
        src_refs, land_refs = refs[:m], refs[m:2 * m]
        send_ref, recv_ref = refs[2 * m], refs[2 * m + 1]
        for i, t in enumerate(ts):
            for s, rel in enumerate(DIRECT_RELS):
                dev, block = _rel_peer(rel)
                copy = pltpu.make_async_remote_copy(
                    src_ref=src_refs[i], dst_ref=land_refs[i].at[block],
                    send_sem=send_ref.at[t * nr + s], recv_sem=recv_ref.at[t * nr + s],
                    device_id=dev, device_id_type=MESH)
                copy.wait_send()
                copy.wait_recv()

    operands = [srcs[t] for t in ts] + [lands[t] for t in ts]
    outs = pl.pallas_call(
        body, name=name, out_shape=tuple(_hbm(a) for a in operands),
        in_specs=[_HBM_SPEC] * (2 * m) + [_SEM_SPEC, _SEM_SPEC, pl.BlockSpec(memory_space=pl.ANY)],
        out_specs=tuple([_HBM_SPEC] * (2 * m)), input_output_aliases={i: i for i in range(2 * m)},
        compiler_params=pltpu.CompilerParams(has_side_effects=_DATAFLOW),
    )(*operands, send_sems, recv_sems, after)
    return list(outs[m:])


def _relay_start(name, lands):
    m, nr = len(lands), len(RELAY_RELS)

    def body(*refs):
        land_refs, send_sems, recv_sems = refs[:m], refs[m], refs[m + 1]
        sibling, _ = _rel_peer(1)
        for i in range(m):
            for s, rel in enumerate(RELAY_RELS):
                _, block = _rel_peer(rel)
                pltpu.make_async_remote_copy(
                    src_ref=land_refs[i].at[block], dst_ref=land_refs[i].at[block],
                    send_sem=send_sems.at[i * nr + s], recv_sem=recv_sems.at[i * nr + s],
                    device_id=sibling, device_id_type=MESH).start()

    outs = pl.pallas_call(
        body, name=name,
        out_shape=(pltpu.SemaphoreType.DMA((m * nr,)), pltpu.SemaphoreType.DMA((m * nr,)), *[_hbm(a) for a in lands]),
        in_specs=[_HBM_SPEC] * m, out_specs=(_SEM_SPEC, _SEM_SPEC, *[_HBM_SPEC] * m),
        input_output_aliases={i: 2 + i for i in range(m)},
        compiler_params=pltpu.CompilerParams(has_side_effects=_DATAFLOW),
    )(*lands)
    return outs[0], outs[1], list(outs[2:])


def _relay_wait(name, relayed, after):
    send_sems, recv_sems, lands = relayed
    m, nr = len(lands), len(RELAY_RELS)

    def body(*refs):
        land_refs, send_ref, recv_ref = refs[:m], refs[m], refs[m + 1]
        sibling, _ = _rel_peer(1)
        for i in range(m):
            for s, rel in enumerate(RELAY_RELS):
                _, sent = _rel_peer(rel)
                _, arriving = _rel_peer(rel ^ 1)
                copy = pltpu.make_async_remote_copy(
                    src_ref=land_refs[i].at[sent], dst_ref=land_refs[i].at[arriving],
                    send_sem=send_ref.at[i * nr + s], recv_sem=recv_ref.at[i * nr + s],
                    device_id=sibling, device_id_type=MESH)
                copy.wait_send()
                copy.wait_recv()

    outs = pl.pallas_call(
        body, name=name, out_shape=tuple(_hbm(a) for a in lands),
        in_specs=[_HBM_SPEC] * m + [_SEM_SPEC, _SEM_SPEC, pl.BlockSpec(memory_space=pl.ANY)],
        out_specs=tuple([_HBM_SPEC] * m), input_output_aliases={i: i for i in range(m)},
        compiler_params=pltpu.CompilerParams(has_side_effects=_DATAFLOW),
    )(*lands, send_sems, recv_sems, after)
    return list(outs)


def _ffn_fwd(tag, n, wg, wd):
    gu, act = _gate_up_act(f"ffn_gate_up_{tag}", n, wg)
    wd4 = wd.reshape(NFB, FB, D)
    f = _fwd_kblocked(f"ffn_down_{tag}", act, wd4)
    return (n, gu, act, wg, wd4), f


def _ffn_bwd(tag, dh_out, h_in, f, saved, g_pre, g_post, send):
    n, gu, act, wg, wd4 = saved
    df, (dg_post,) = _rms_bwd(f"ffn_postnorm_bwd_{tag}", f, [(g_post, dh_out)], None, BF16)
    tok = send(f"down_{tag}", _bwd_w_kblocked(f"ffn_down_dw_{tag}", act, df).reshape(NDEV, DFF // NDEV, D))
    dgu = _down_dx_act_bwd(f"ffn_down_dx_{tag}", df, wd4, gu, tok).reshape(NDEV, S, FB)
    tok = send(f"gate_up_{tag}", _bwd_w_cols_blocked(f"ffn_gate_up_dw_{tag}", n, dgu))
    dn = _bwd_x_cols_blocked(f"ffn_gate_up_dx_{tag}", dgu, wg, after=tok)
    dh_in, (dg_pre,) = _rms_bwd(f"ffn_prenorm_bwd_{tag}", h_in, [(g_pre, dn)], dh_out, F32)
    return dh_in, dg_pre, dg_post


def kernel(x, positions, mix_norm_pre, mix_norm_post, ffn_norm_pre, ffn_norm_post, ffn_w_gate_up, ffn_w_down, conv_w_in, conv_w, conv_w_out, kv_norm, w_kv, w_q, w_o, loss_target, m_mix_norm_pre, m_mix_norm_post, m_ffn_norm_pre, m_ffn_norm_post, m_ffn_w_gate_up, m_ffn_w_down, m_conv_w_in, m_conv_w, m_conv_w_out, m_kv_norm, m_w_kv, m_w_q, m_w_o, v_mix_norm_pre, v_mix_norm_post, v_ffn_norm_pre, v_ffn_norm_post, v_ffn_w_gate_up, v_ffn_w_down, v_conv_w_in, v_conv_w, v_conv_w_out, v_kv_norm, v_w_kv, v_w_q, v_w_o):
    me = 4 * lax.axis_index("x") + 2 * lax.axis_index("y") + lax.axis_index("c")
    h0 = x.reshape(S, D)
    target = loss_target.reshape(S, D)
    row = lambda a, l: a[l].reshape(1, D)
    g_kv = kv_norm.reshape(1, D)

    cw_shard = jnp.pad(conv_w[0], ((0, 5), (0, 0)))
    names = ["conv_in", "conv_w", "conv_out", "gate_up_0", "down_0", "kv", "q", "o", "gate_up_1", "down_1"]
    shards = [conv_w_in[0], cw_shard, conv_w_out[0], ffn_w_gate_up[0], ffn_w_down[0],
              w_kv, w_q[0], w_o[0], ffn_w_gate_up[1], ffn_w_down[1]]
    shards = [s if n == "conv_w" else s.astype(BF16) for n, s in zip(names, shards)]
    gather = _gather_start("gather_weights_start", shards, me)

    def direct(group, after):
        lands = _gather_wait(f"gather_wait_{group[0]}", gather, [names.index(n) for n in group], after)
        return _relay_start(f"relay_start_{group[0]}", lands)

    def finish(group, relayed, after):
        return dict(zip(group, _relay_wait(f"relay_wait_{group[0]}", relayed, after)))

    sent = {}

    def send(name, grad):
        sent[name], token = _exchange_start(f"scatter_start_{name}", [grad], me, "scatter")
        return token

    groups = [["conv_in", "conv_w", "conv_out"], ["gate_up_0", "down_0"], ["kv", "q"], ["o", "gate_up_1", "down_1"]]
    n0 = _rms_fwd("mix_prenorm_0", h0, [row(mix_norm_pre, 0)])[0]
    w = finish(groups[0], direct(groups[0], n0), n0)
    win = w["conv_in"].transpose(1, 0, 2).reshape(D, 3 * D)
    cw = w["conv_w"].transpose(1, 0, 2).reshape(8, D)
    wout = w["conv_out"].reshape(D, D)
    z = _fwd_rows("conv_in", n0, win, BF16)
    pre = _conv_fwd("conv_gate", z, cw)
    relayed = direct(groups[1], pre)
    y0 = _fwd_rows("conv_out", pre, wout)
    h1, (n1,) = _resid_rms("mix_postnorm_0", h0, y0, row(mix_norm_post, 0), [row(ffn_norm_pre, 0)])
    w = finish(groups[1], relayed, n1)
    ffn0, f0 = _ffn_fwd("0", n1, w["gate_up_0"], w["down_0"])
    relayed = direct(groups[2], ffn0[2])
    h2, (nk, n2) = _resid_rms("ffn_postnorm_0", h1, f0, row(ffn_norm_post, 0), [g_kv, row(mix_norm_pre, 1)])

    w = finish(groups[2], relayed, nk)
    wkv = w["kv"].transpose(1, 0, 2).reshape(D, 2 * QW)
    wq = w["q"].transpose(1, 0, 2).reshape(D, QW)
    half = HEAD_DIM // 2
    inv_freq = ROPE_THETA ** (-jnp.arange(half, dtype=F32) / half)
    tables = _rope_tables("rope_tables", positions.reshape(S, 1), jnp.tile(inv_freq, 4).reshape(1, 128))
    qc, kc, vc, o_c, lse_c = [], [], [], [], []
    for g, d in enumerate(DILATIONS):
        kc.append(_proj_classes(f"k_proj_{g}", nk, wkv, g, d, tables, 1.0))
        vc.append(_proj_classes(f"v_proj_{g}", nk, wkv, len(DILATIONS) + g, d, None, None))
    relayed = direct(groups[3], vc[-1])
    for g, d in enumerate(DILATIONS):
        qc.append(_proj_classes(f"q_proj_{g}", n2, wq, g, d, tables, HEAD_DIM ** -0.5))
        o_g, lse_g = _attn_fwd(f"attn_fwd_{g}", qc[g], kc[g], vc[g], d)
        o_c.append(o_g)
        lse_c.append(lse_g)
    o_mix = _mix_fwd("attn_mix", o_c, lse_c)
    w = finish(groups[3], relayed, o_mix)
    wo = w["o"].reshape(D, D)
    y1 = _fwd_rows("attn_out", o_mix, wo)
    h3, (n3,) = _resid_rms("mix_postnorm_1", h2, y1, row(mix_norm_post, 1), [row(ffn_norm_pre, 1)])
    ffn1, f1 = _ffn_fwd("1", n3, w["gate_up_1"], w["down_1"])

    dh4, sq = _resid_rms_loss("ffn_postnorm_1_loss", h3, f1, row(ffn_norm_post, 1), target)

    dh3, dg_fpre1, dg_fpost1 = _ffn_bwd(
        "1", dh4, h3, f1, ffn1, row(ffn_norm_pre, 1), row(ffn_norm_post, 1), send)
    dy1, (dg_mpost1,) = _rms_bwd("mix_postnorm_bwd_1", y1, [(row(mix_norm_post, 1), dh3)], None, BF16)
    tok = send("o", _bwd_w_rows("attn_out_dw", o_mix, dy1).reshape(NDEV, D // NDEV, D))
    do = _bwd_x_rows("attn_out_dx", dy1, wo, F32, after=tok)
    lane = jnp.arange(128)
    ones_blockdiag = (lane[:, None] // HEAD_DIM == lane[None, :] // HEAD_DIM).astype(BF16)
    mixed = _mix_bwd("attn_mix_bwd", do, o_c, lse_c, ones_blockdiag)
    branch_grads = [_attn_bwd(f"attn_bwd_{g}", qc[g], kc[g], vc[g], mixed[g], lse_c[g], mixed[3 + g], d)
                    for g, d in enumerate(DILATIONS)]
    dq_raw, dkv = _attn_bwd_post("attn_bwd_post", branch_grads, *tables)
    tok = send("kv", _bwd_w_cols("kv_proj_dw", nk, dkv, 2 * QW // NDEV))
    dnk = _bwd_x_plain("kv_proj_dx", dkv, wkv, after=tok)
    tok = send("q", _bwd_w_cols("q_proj_dw", n2, dq_raw, QW // NDEV))
    dn2 = _bwd_x_plain("q_proj_dx", dq_raw, wq, after=tok)
    dh2, (dg_kv, dg_mpre1) = _rms_bwd("kv_and_mix_prenorm_bwd_1", h2,
                                      [(g_kv, dnk), (row(mix_norm_pre, 1), dn2)], dh3, F32)

    dh1, dg_fpre0, dg_fpost0 = _ffn_bwd(
        "0", dh2, h1, f0, ffn0, row(ffn_norm_pre, 0), row(ffn_norm_post, 0), send)
    dy0, (dg_mpost0,) = _rms_bwd("mix_postnorm_bwd_0", y0, [(row(mix_norm_post, 0), dh1)], None, BF16)
    tok = send("conv_out", _bwd_w_rows("conv_out_dw", pre, dy0).reshape(NDEV, D // NDEV, D))
    dpre = _bwd_x_rows("conv_out_dx", dy0, wout, BF16, after=tok)
    dz, dcw = _conv_bwd("conv_gate_bwd", z, dpre, cw)
    tok = send("conv_in", _bwd_w_cols("conv_in_dw", n0, dz, 3 * D // NDEV))
    dn0 = _bwd_x_plain("conv_in_dx", dz, win, after=tok)
    dh0, (dg_mpre0,) = _rms_bwd("mix_prenorm_bwd_0", h0, [(row(mix_norm_pre, 0), dn0)], dh1, F32)

    small = _pack_small("pack_small_grads", [dg_mpre0, dg_mpre1, dg_mpost0, dg_mpost1, dg_fpre0, dg_fpre1,
                                             dg_fpost0, dg_fpost1, dg_kv], dcw, sq)
    small_all = _exchange("gather_small_grads", [small], "gather")[0]

    done = [small_all]

    def upd(tag, w, m, v):
        parts = _exchange_wait(f"scatter_wait_{tag}", sent[tag], 0, done[-1], "scatter")
        shape = w.shape
        flat = lambda a: a.reshape(parts.shape[1:])
        res = _adamw(f"adamw_{tag}", parts, flat(w), flat(m), flat(v))
        done.append(res[0])
        return [r.reshape(shape) for r in res]

    def upd_layer(tag, l, w, m, v):
        return upd(f"{tag}_{l}", w[l], m[l], v[l])

    def stack(per_layer):
        return [jnp.stack([per_layer[0][i], per_layer[1][i]]) for i in range(4)]

    vec = lambda a: a.reshape(1, D)
    gain_res, taps, loss = _adamw_gains("adamw_gains", small_all, [
        (mix_norm_pre, m_mix_norm_pre, v_mix_norm_pre), (mix_norm_post, m_mix_norm_post, v_mix_norm_post),
        (ffn_norm_pre, m_ffn_norm_pre, v_ffn_norm_pre), (ffn_norm_post, m_ffn_norm_post, v_ffn_norm_post),
        (vec(kv_norm), vec(m_kv_norm), vec(v_kv_norm))])
    dcw_mine = lax.dynamic_slice(taps, (0, me * 128), (8, 128))
    pad8 = lambda a, fill: jnp.pad(a[0], ((0, 5), (0, 0)), constant_values=fill)
    cw_res = [r[0:3].reshape(1, 3, 128) for r in
              _adamw("adamw_conv_w", dcw_mine.reshape(1, 8, 128), cw_shard, pad8(m_conv_w, 0.0), pad8(v_conv_w, 1.0))]

    res = {
        "mix_norm_pre": gain_res[0],
        "mix_norm_post": gain_res[1],
        "ffn_norm_pre": gain_res[2],
        "ffn_norm_post": gain_res[3],
        "kv_norm": [r.reshape(D) for r in gain_res[4]],
        "conv_w": cw_res,
    }
    down, gate_up = {}, {}
    down[1] = upd_layer("down", 1, ffn_w_down, m_ffn_w_down, v_ffn_w_down)
    gate_up[1] = upd_layer("gate_up", 1, ffn_w_gate_up, m_ffn_w_gate_up, v_ffn_w_gate_up)
    res["w_o"] = upd("o", w_o, m_w_o, v_w_o)
    res["w_q"] = upd("q", w_q, m_w_q, v_w_q)
    res["w_kv"] = upd("kv", w_kv, m_w_kv, v_w_kv)
    down[0] = upd_layer("down", 0, ffn_w_down, m_ffn_w_down, v_ffn_w_down)
    gate_up[0] = upd_layer("gate_up", 0, ffn_w_gate_up, m_ffn_w_gate_up, v_ffn_w_gate_up)
    res["ffn_w_down"] = stack(down)
    res["ffn_w_gate_up"] = stack(gate_up)
    res["conv_w_out"] = upd("conv_out", conv_w_out, m_conv_w_out, v_conv_w_out)
    res["conv_w_in"] = upd("conv_in", conv_w_in, m_conv_w_in, v_conv_w_in)
    order = ["mix_norm_pre", "mix_norm_post", "ffn_norm_pre", "ffn_norm_post", "ffn_w_gate_up", "ffn_w_down",
             "conv_w_in", "conv_w", "conv_w_out", "kv_norm", "w_kv", "w_q", "w_o"]
    out = [loss, dh0.reshape(1, S, D)]
    for i in range(4):
        out += [res[name][i] for name in order]
    return tuple(out)
```

```python
import jax
import jax.numpy as jnp
from jax import lax
from jax.experimental import pallas as pl
from jax.experimental.pallas import tpu as pltpu

F32 = jnp.float32
BF16 = jnp.bfloat16

S = 4096
D = 1024
NDEV = 8
HEAD_DIM = 64
QW = 3072
DFF = 2816
FB = 704
NFB = 4
BRANCHES = ((128, 1), (512, 4), (2048, 16))
BAND = 128
ROPE_THETA = 10000.0
RMS_EPS = 1e-6
NEG_INF = -1e30
ADAM_LR, ADAM_B1, ADAM_B2, ADAM_EPS, ADAM_WD, ADAM_STEP = 0.001, 0.9, 0.999, 1e-08, 0.01, 10

VMEM_LIMIT_BYTES = 52 * 1024 * 1024
ROW_TILE = 512
MESH = pl.DeviceIdType.MESH


def _cparams(ngrid):
    return pltpu.CompilerParams(dimension_semantics=("arbitrary",) * ngrid,
                                vmem_limit_bytes=VMEM_LIMIT_BYTES)


def _sds(shape, dtype):
    return jax.ShapeDtypeStruct(tuple(shape), dtype)


_DIMS = {"nn": (((1,), (0,)), ((), ())),
         "nt": (((1,), (1,)), ((), ())),
         "tn": (((0,), (0,)), ((), ()))}


def _matmul(name, a, b, *, mode, grid, a_blk, a_map, b_blk, b_map, o_shape, o_blk, o_map, out_dtype, after=None,
            out_groups=1):
    nk = grid[2]
    dims = _DIMS[mode]
    acc_shape = tuple(s for s in o_blk if s is not None)
    if out_groups > 1:
        acc_shape = (acc_shape[1], out_groups * acc_shape[2])
    extra = [] if after is None else [after]

    def store(o_ref, val):
        if out_groups == 1:
            o_ref[...] = val.astype(o_ref.dtype)
        else:
            n = o_ref.shape[-1]
            for grp in range(out_groups):
                o_ref[grp] = val[:, grp * n:(grp + 1) * n].astype(o_ref.dtype)

    def body(a_ref, b_ref, *rest):
        o_ref, scratch = rest[len(extra)], rest[len(extra) + 1:]
        part = lax.dot_general(a_ref[...], b_ref[...], dims, preferred_element_type=F32)
        if nk == 1:
            store(o_ref, part)
            return
        acc_ref = scratch[0]
        k = pl.program_id(2)

        @pl.when(k == 0)
        def _():
            acc_ref[...] = part

        @pl.when(k > 0)
        def _():
            acc_ref[...] += part

        @pl.when(k == nk - 1)
        def _():
            store(o_ref, acc_ref[...])

    return pl.pallas_call(
        body, name=name, grid=grid,
        in_specs=[pl.BlockSpec(a_blk, a_map), pl.BlockSpec(b_blk, b_map)] + [pl.BlockSpec(memory_space=pl.ANY)] * len(extra),
        out_specs=pl.BlockSpec(o_blk, o_map),
        out_shape=_sds(o_shape, out_dtype),
        scratch_shapes=[] if nk == 1 else [pltpu.VMEM(acc_shape, F32)],
        compiler_params=_cparams(3),
    )(a, b, *extra)


TM = 1024
TK = S


def _fwd_rows(name, a, w, out_dtype=F32):
    kdim, n = w.shape
    tn = 512
    return _matmul(name, a, w, mode="nn", grid=(S // TM, n // tn, 1),
                   a_blk=(TM, kdim), a_map=lambda i, j, k: (i, 0),
                   b_blk=(kdim, tn), b_map=lambda i, j, k: (0, j),
                   o_shape=(S, n), o_blk=(TM, tn), o_map=lambda i, j, k: (i, j), out_dtype=out_dtype)


def _fwd_kblocked(name, a4, w4):
    nb, _, kb = a4.shape
    n = w4.shape[2]

    def body(a_ref, w_ref, o_ref):
        acc = _dot_nn(a_ref[0], w_ref[0])
        for j in range(1, nb):
            acc = acc + _dot_nn(a_ref[j], w_ref[j])
        o_ref[...] = acc

    return pl.pallas_call(
        body, name=name, grid=(S // TM,),
        in_specs=[pl.BlockSpec((nb, TM, kb), lambda i: (0, i, 0)), pl.BlockSpec((nb, kb, n), lambda i: (0, 0, 0))],
        out_specs=pl.BlockSpec((TM, n), lambda i: (i, 0)), out_shape=_sds((S, n), F32),
        compiler_params=_cparams(1),
    )(a4, w4)


def _bwd_x_cols_blocked(name, dy8, wg, after):
    _, kdim, n = wg.shape
    nk = NDEV // 2

    def body(a_ref, b_ref, after_ref, o_ref, acc_ref):
        k = pl.program_id(1)
        part = _dot_nt(a_ref[0], b_ref[0]) + _dot_nt(a_ref[1], b_ref[1])

        @pl.when(k == 0)
        def _():
            acc_ref[...] = part

        @pl.when(k > 0)
        def _():
            acc_ref[...] += part

        @pl.when(k == nk - 1)
        def _():
            o_ref[...] = acc_ref[...]

    return pl.pallas_call(
        body, name=name, grid=(S // TM, nk),
        in_specs=[pl.BlockSpec((2, None, TM, n), lambda i, k: (0, k, i, 0)),
                  pl.BlockSpec((2, None, kdim, n), lambda i, k: (0, k, 0, 0)),
                  pl.BlockSpec(memory_space=pl.ANY)],
        out_specs=pl.BlockSpec((TM, kdim), lambda i, k: (i, 0)), out_shape=_sds((S, kdim), F32),
        scratch_shapes=[pltpu.VMEM((TM, kdim), F32)],
        compiler_params=_cparams(2),
    )(dy8.reshape(2, nk, S, n), wg.reshape(2, nk, kdim, n), after)


def _bwd_x_rows(name, dy, w, out_dtype, after=None):
    kdim, n = w.shape
    tkk = 512
    return _matmul(name, dy, w, mode="nt", grid=(S // TM, kdim // tkk, 1),
                   a_blk=(TM, n), a_map=lambda i, j, k: (i, 0),
                   b_blk=(tkk, n), b_map=lambda i, j, k: (j, 0),
                   o_shape=(S, kdim), o_blk=(TM, tkk), o_map=lambda i, j, k: (i, j), out_dtype=out_dtype, after=after)


DW_COLS = 768


def _bwd_w_cols(name, a, dy, n):
    kdim = a.shape[1]
    groups = DW_COLS // n
    return _matmul(name, a, dy, mode="tn", grid=(1, NDEV // groups, S // TK),
                   a_blk=(TK, kdim), a_map=lambda i, j, k: (k, 0),
                   b_blk=(TK, DW_COLS), b_map=lambda i, j, k: (k, j),
                   o_shape=(NDEV, kdim, n), o_blk=(groups, kdim, n) if groups > 1 else (None, kdim, n),
                   o_map=lambda i, j, k: (j, 0, 0), out_dtype=BF16, out_groups=groups)


def _bwd_x_plain(name, dy, w, after=None):
    kdim, n = w.shape
    tm = TM if n <= 3 * D else TM // 2
    return _matmul(name, dy, w, mode="nt", grid=(S // tm, 1, 1),
                   a_blk=(tm, n), a_map=lambda i, j, k: (i, 0),
                   b_blk=(kdim, n), b_map=lambda i, j, k: (0, 0),
                   o_shape=(S, kdim), o_blk=(tm, kdim), o_map=lambda i, j, k: (i, 0), out_dtype=F32, after=after)


def _bwd_w_cols_blocked(name, a, dy8):
    kdim = a.shape[1]
    n = dy8.shape[2]
    return _matmul(name, a, dy8, mode="tn", grid=(1, NDEV, S // TK),
                   a_blk=(TK, kdim), a_map=lambda i, j, k: (k, 0),
                   b_blk=(None, TK, n), b_map=lambda i, j, k: (j, k, 0),
                   o_shape=(NDEV, kdim, n), o_blk=(None, kdim, n), o_map=lambda i, j, k: (j, 0, 0), out_dtype=BF16)


def _bwd_w_rows(name, a, dy):
    kdim = a.shape[1]
    n = dy.shape[1]
    tmm = 512
    return _matmul(name, a, dy, mode="tn", grid=(kdim // tmm, 1, S // TK),
                   a_blk=(TK, tmm), a_map=lambda i, j, k: (k, i),
                   b_blk=(TK, n), b_map=lambda i, j, k: (k, 0),
                   o_shape=(kdim, n), o_blk=(tmm, n), o_map=lambda i, j, k: (i, 0), out_dtype=BF16)


def _bwd_w_kblocked(name, a4, dy):
    nb, _, kb = a4.shape
    n = dy.shape[1]
    return _matmul(name, a4, dy, mode="tn", grid=(nb, 1, S // TK),
                   a_blk=(None, TK, kb), a_map=lambda i, j, k: (i, k, 0),
                   b_blk=(TK, n), b_map=lambda i, j, k: (k, 0),
                   o_shape=(nb, kb, n), o_blk=(None, kb, n), o_map=lambda i, j, k: (i, 0, 0), out_dtype=BF16)


def _rstd(x):
    return lax.rsqrt(jnp.mean(x * x, axis=-1, keepdims=True) + RMS_EPS)


def _row_spec(tm=ROW_TILE, width=D):
    return pl.BlockSpec((tm, width), lambda i: (i, 0))


def _vec_spec(rows=1, width=D):
    return pl.BlockSpec((rows, width), lambda i: (0, 0))


def _rms_fwd(name, x, gains):
    n = len(gains)

    def body(x_ref, *refs):
        x_val = x_ref[...]
        xh = x_val * _rstd(x_val)
        for g_ref, o_ref in zip(refs[:n], refs[n:]):
            o_ref[...] = (xh * g_ref[...]).astype(o_ref.dtype)

    outs = pl.pallas_call(
        body, name=name, grid=(S // ROW_TILE,),
        in_specs=[_row_spec()] + [_vec_spec()] * n,
        out_specs=[_row_spec()] * n,
        out_shape=[_sds((S, D), BF16)] * n,
        compiler_params=_cparams(1),
    )(x, *gains)
    return list(outs)


def _resid_rms(name, h, y, g, next_gains):
    n = len(next_gains)

    def body(h_ref, y_ref, g_ref, *refs):
        y_val = y_ref[...]
        h_new = h_ref[...] + (y_val * _rstd(y_val)) * g_ref[...]
        refs[n][...] = h_new
        hh = h_new * _rstd(h_new)
        for g2_ref, o_ref in zip(refs[:n], refs[n + 1:]):
            o_ref[...] = (hh * g2_ref[...]).astype(o_ref.dtype)

    outs = pl.pallas_call(
        body, name=name, grid=(S // ROW_TILE,),
        in_specs=[_row_spec(), _row_spec(), _vec_spec()] + [_vec_spec()] * n,
        out_specs=[_row_spec()] * (n + 1), out_shape=[_sds((S, D), F32)] + [_sds((S, D), BF16)] * n,
        compiler_params=_cparams(1),
    )(h, y, g, *next_gains)
    return outs[0], list(outs[1:])


def _resid_rms_loss(name, h, y, g, target):
    def body(h_ref, y_ref, g_ref, t_ref, dh_ref, part_ref):
        y_val = y_ref[...]
        e = h_ref[...] + (y_val * _rstd(y_val)) * g_ref[...] - t_ref[...]
        dh_ref[...] = e * (1.0 / D)
        part = jnp.sum(e * e, axis=0, keepdims=True)
        step = pl.program_id(0)

        @pl.when(step == 0)
        def _():
            part_ref[...] = part

        @pl.when(step > 0)
        def _():
            part_ref[...] += part

    return pl.pallas_call(
        body, name=name, grid=(S // ROW_TILE,),
        in_specs=[_row_spec(), _row_spec(), _vec_spec(), _row_spec()],
        out_specs=[_row_spec(), _vec_spec()],
        out_shape=[_sds((S, D), F32), _sds((1, D), F32)],
        compiler_params=_cparams(1),
    )(h, y, g, target)


def _rms_bwd(name, x, pairs, dres, out_dtype):
    n = len(pairs)
    has_res = dres is not None

    def body(x_ref, *refs):
        g_refs = refs[0:2 * n:2]
        dn_refs = refs[1:2 * n:2]
        pos = 2 * n
        res_ref = refs[pos] if has_res else None
        pos += int(has_res)
        dx_ref = refs[pos]
        dg_refs = refs[pos + 1:]
        step = pl.program_id(0)
        x_val = x_ref[...]
        r = _rstd(x_val)
        xh = x_val * r
        acc = res_ref[...] if has_res else jnp.zeros_like(x_val)
        for g_ref, dn_ref, dg_ref in zip(g_refs, dn_refs, dg_refs):
            dn = dn_ref[...].astype(F32)
            dxh = dn * g_ref[...]
            acc = acc + r * (dxh - xh * jnp.mean(dxh * xh, axis=-1, keepdims=True))
            part = jnp.sum(dn * xh, axis=0, keepdims=True)

            @pl.when(step == 0)
            def _():
                dg_ref[...] = jnp.zeros_like(dg_ref)

            dg_ref[0:1, :] += part

        dx_ref[...] = acc.astype(dx_ref.dtype)

    operands = [x]
    in_specs = [_row_spec()]
    for g, dn in pairs:
        operands += [g, dn]
        in_specs += [_vec_spec(), _row_spec()]
    if has_res:
        operands.append(dres)
        in_specs.append(_row_spec())
    outs = pl.pallas_call(
        body, name=name, grid=(S // ROW_TILE,),
        in_specs=in_specs,
        out_specs=[_row_spec()] + [_vec_spec(8)] * n,
        out_shape=[_sds((S, D), out_dtype)] + [_sds((8, D), F32)] * n,
        compiler_params=_cparams(1),
    )(*operands)
    return outs[0], list(outs[1:])


def _shift_down(u, prev8, k):
    r = pltpu.roll(u, k, 0)
    p = pltpu.roll(prev8, k, 0)
    row = lax.broadcasted_iota(jnp.int32, prev8.shape, 0)
    top = jnp.where(row < k, p, r[0:8])
    return jnp.concatenate([top, r[8:]], axis=0)


def _shift_up(u, next8, k):
    tm = u.shape[0]
    r = pltpu.roll(u, tm - k, 0)
    p = pltpu.roll(next8, 8 - k, 0)
    row = lax.broadcasted_iota(jnp.int32, next8.shape, 0)
    bot = jnp.where(row >= 8 - k, p, r[tm - 8:tm])
    return jnp.concatenate([r[:tm - 8], bot], axis=0)


CONV_TILE = 512


def _halo_prev(col):
    return pl.BlockSpec((8, D), lambda i: (jnp.maximum(i * (CONV_TILE // 8) - 1, 0), col))


def _halo_next(col):
    last = S // 8 - 1
    return pl.BlockSpec((8, D), lambda i: (jnp.minimum((i + 1) * (CONV_TILE // 8), last), col))


def _conv_fwd(name, z, cw):
    def body(b_ref, c_ref, h_ref, cp_ref, hp_ref, cw_ref, o_ref):
        i = pl.program_id(0)
        u = c_ref[...].astype(F32) * h_ref[...].astype(F32)
        up = cp_ref[...].astype(F32) * hp_ref[...].astype(F32)
        up = jnp.where(i > 0, up, 0.0)
        cv = cw_ref[0:1, :] * _shift_down(u, up, 2) + cw_ref[1:2, :] * _shift_down(u, up, 1) + cw_ref[2:3, :] * u
        o_ref[...] = (b_ref[...].astype(F32) * cv).astype(o_ref.dtype)

    col = lambda c: pl.BlockSpec((CONV_TILE, D), lambda i: (i, c))
    return pl.pallas_call(
        body, name=name, grid=(S // CONV_TILE,),
        in_specs=[col(0), col(1), col(2), _halo_prev(1), _halo_prev(2), _vec_spec(8)],
        out_specs=_row_spec(CONV_TILE), out_shape=_sds((S, D), BF16),
        compiler_params=_cparams(1),
    )(z, z, z, z, z, cw)


def _conv_bwd(name, z, dpre, cw):
    nsteps = S // CONV_TILE

    def body(b_ref, c_ref, h_ref, cp_ref, hp_ref, dp_ref, dpn_ref, bn_ref, cw_ref, dz_ref, dcw_ref):
        i = pl.program_id(0)
        b = b_ref[...].astype(F32)
        c = c_ref[...].astype(F32)
        h = h_ref[...].astype(F32)
        dp = dp_ref[...].astype(F32)
        u = c * h
        up = jnp.where(i > 0, cp_ref[...].astype(F32) * hp_ref[...].astype(F32), 0.0)
        s1 = _shift_down(u, up, 1)
        s2 = _shift_down(u, up, 2)
        w0, w1, w2 = cw_ref[0:1, :], cw_ref[1:2, :], cw_ref[2:3, :]
        cv = w0 * s2 + w1 * s1 + w2 * u
        dcv = dp * b
        dcvn = jnp.where(i < nsteps - 1, dpn_ref[...].astype(F32) * bn_ref[...].astype(F32), 0.0)
        du = w2 * dcv + w1 * _shift_up(dcv, dcvn, 1) + w0 * _shift_up(dcv, dcvn, 2)
        dz_ref[:, 0:D] = (dp * cv).astype(dz_ref.dtype)
        dz_ref[:, D:2 * D] = (du * h).astype(dz_ref.dtype)
        dz_ref[:, 2 * D:3 * D] = (du * c).astype(dz_ref.dtype)

        @pl.when(i == 0)
        def _():
            dcw_ref[...] = jnp.zeros_like(dcw_ref)

        dcw_ref[0:1, :] += jnp.sum(dcv * s2, axis=0, keepdims=True)
        dcw_ref[1:2, :] += jnp.sum(dcv * s1, axis=0, keepdims=True)
        dcw_ref[2:3, :] += jnp.sum(dcv * u, axis=0, keepdims=True)

    col = lambda c: pl.BlockSpec((CONV_TILE, D), lambda i: (i, c))
    return pl.pallas_call(
        body, name=name, grid=(nsteps,),
        in_specs=[col(0), col(1), col(2), _halo_prev(1), _halo_prev(2),
                  _row_spec(CONV_TILE), _halo_next(0), _halo_next(0), _vec_spec(8)],
        out_specs=[pl.BlockSpec((CONV_TILE, 3 * D), lambda i: (i, 0)), _vec_spec(8)],
        out_shape=[_sds((S, 3 * D), BF16), _sds((8, D), F32)],
        compiler_params=_cparams(1),
    )(z, z, z, z, z, dpre, dpre, z, cw)


FFN_TM = 2048
_GU_BLOCK = pl.BlockSpec((2, None, FFN_TM, FB), lambda i, j: (0, j, i, 0))


def _gate_up_act(name, a, wg):
    kdim = a.shape[1]

    def body(a_ref, wgate_ref, wup_ref, gu_ref, act_ref):
        x = a_ref[...]
        g = _dot_nn(x, wgate_ref[...])
        u = _dot_nn(x, wup_ref[...])
        gu_ref[0] = g.astype(gu_ref.dtype)
        gu_ref[1] = u.astype(gu_ref.dtype)
        act_ref[...] = (g * jax.nn.sigmoid(g) * u).astype(act_ref.dtype)

    return pl.pallas_call(
        body, name=name, grid=(S // FFN_TM, NFB),
        in_specs=[pl.BlockSpec((FFN_TM, kdim), lambda i, j: (i, 0)),
                  pl.BlockSpec((None, kdim, FB), lambda i, j: (j, 0, 0)),
                  pl.BlockSpec((None, kdim, FB), lambda i, j: (j + NFB, 0, 0))],
        out_specs=[_GU_BLOCK, pl.BlockSpec((None, FFN_TM, FB), lambda i, j: (j, i, 0))],
        out_shape=[_sds((2, NFB, S, FB), BF16), _sds((NFB, S, FB), BF16)],
        compiler_params=_cparams(2),
    )(a, wg, wg)


def _down_dx_act_bwd(name, df, w4, gu, after):
    _, kb, n = w4.shape

    def body(df_ref, w_ref, gu_ref, after_ref, o_ref):
        d = _dot_nt(df_ref[...], w_ref[...])
        g = gu_ref[0].astype(F32)
        u = gu_ref[1].astype(F32)
        sg = jax.nn.sigmoid(g)
        o_ref[0] = (d * u * sg * (1.0 + g * (1.0 - sg))).astype(o_ref.dtype)
        o_ref[1] = (d * g * sg).astype(o_ref.dtype)

    return pl.pallas_call(
        body, name=name, grid=(S // FFN_TM, NFB),
        in_specs=[pl.BlockSpec((FFN_TM, n), lambda i, j: (i, 0)), pl.BlockSpec((None, kb, n), lambda i, j: (j, 0, 0)),
                  _GU_BLOCK, pl.BlockSpec(memory_space=pl.ANY)],
        out_specs=_GU_BLOCK, out_shape=_sds((2, NFB, S, FB), BF16),
        compiler_params=_cparams(2),
    )(df, w4, gu, after)


def _rope_tables(name, pos_col, inv_freq_row):
    def body(pos_ref, f_ref, cos_ref, sin_ref):
        ang = pos_ref[...].astype(F32) * f_ref[...]
        lane = lax.broadcasted_iota(jnp.int32, ang.shape, 1)
        s = jnp.sin(ang)
        cos_ref[...] = jnp.cos(ang)
        sin_ref[...] = jnp.where((lane % HEAD_DIM) < HEAD_DIM // 2, -s, s)

    tab = pl.BlockSpec((ROW_TILE, 128), lambda i: (i, 0))
    return pl.pallas_call(
        body, name=name, grid=(S // ROW_TILE,),
        in_specs=[pl.BlockSpec((ROW_TILE, 1), lambda i: (i, 0)), _vec_spec(1, 128)],
        out_specs=[tab, tab], out_shape=[_sds((S, 128), F32)] * 2,
        compiler_params=_cparams(1),
    )(pos_col, inv_freq_row)


def _swap_halves(t):
    lane = lax.broadcasted_iota(jnp.int32, t.shape, 1)
    first = (lane % HEAD_DIM) < HEAD_DIM // 2
    return jnp.where(first, pltpu.roll(t, 128 - HEAD_DIM // 2, 1), pltpu.roll(t, HEAD_DIM // 2, 1))


NCHUNK = D // 128


def _chunk(c, base=0):
    return slice(base + c * 128, base + (c + 1) * 128)


def _class_rows(r, d, tm):
    return pl.ds(r, tm // d, stride=d) if d > 1 else slice(None)


def _class_block(d, tm):
    return pl.BlockSpec((tm // d, d * D), lambda i: (i, 0))


def _tokens_from_classes(blk_ref, tmp_ref, d, tm):
    for r in range(d):
        for c in range(NCHUNK):
            tmp_ref[c, _class_rows(r, d, tm), :] = blk_ref[:, _chunk(c, r * D)].astype(F32)


def _classes_from_tokens(tmp_ref, blk_ref, d, tm):
    for r in range(d):
        for c in range(NCHUNK):
            blk_ref[:, _chunk(c, r * D)] = tmp_ref[c, _class_rows(r, d, tm), :].astype(blk_ref.dtype)


def _proj_classes(name, a, w, col, d, tables, scale):
    kdim = a.shape[1]
    rope = tables is not None

    def body(a_ref, w_ref, *refs):
        if rope:
            cos_ref, sin_ref, o_ref, tmp_ref = refs
        else:
            o_ref, tmp_ref = refs
        acc = _dot_nn(a_ref[...], w_ref[...])
        for c in range(NCHUNK):
            tmp_ref[c] = acc[:, _chunk(c)]
        for r in range(d):
            rows = _class_rows(r, d, TM)
            if rope:
                cs = cos_ref[rows, :]
                sn = sin_ref[rows, :]
            for c in range(NCHUNK):
                x = tmp_ref[c, rows, :]
                if rope:
                    x = (x * cs + _swap_halves(x) * sn) * scale
                o_ref[:, _chunk(c, r * D)] = x.astype(o_ref.dtype)

    tab = pl.BlockSpec((TM, 128), lambda i: (i, 0))
    return pl.pallas_call(
        body, name=name, grid=(S // TM,),
        in_specs=[pl.BlockSpec((TM, kdim), lambda i: (i, 0)), pl.BlockSpec((kdim, D), lambda i: (0, col))]
                 + ([tab, tab] if rope else []),
        out_specs=_class_block(d, TM), out_shape=_sds((S // d, d * D), BF16),
        scratch_shapes=[pltpu.VMEM((NCHUNK, TM, 128), F32)],
        compiler_params=_cparams(1),
    )(a, w, *(tables if rope else ()))


ATTN_CHAINS = 4


def _attn_units(d):
    nblk = S // d // BAND
    return max(1, 2 * ATTN_CHAINS // nblk)


def _class_spec(d):
    return pl.BlockSpec((S // d, 128 * _attn_units(d)), lambda cb: (0, cb))


def _dot_nt(a, b):
    return lax.dot_general(a, b, _DIMS["nt"], preferred_element_type=F32)


def _dot_tn(a, b):
    return lax.dot_general(a, b, _DIMS["tn"], preferred_element_type=F32)


def _dot_nn(a, b):
    return lax.dot_general(a, b, _DIMS["nn"], preferred_element_type=F32)


def _band_mask(nkeys):
    qi = lax.broadcasted_iota(jnp.int32, (2 * BAND, nkeys), 0) % BAND
    kj = lax.broadcasted_iota(jnp.int32, (2 * BAND, nkeys), 1)
    if nkeys == BAND:
        return kj <= qi
    dist = qi + BAND - kj
    return (dist >= 0) & (dist <= BAND)


def _stack_heads(x):
    row = lax.broadcasted_iota(jnp.int32, (2 * BAND, 128), 0)
    lane = lax.broadcasted_iota(jnp.int32, (2 * BAND, 128), 1)
    keep = (row < BAND) == (lane < HEAD_DIM)
    return jnp.where(keep, jnp.concatenate([x, x], axis=0), jnp.zeros((), x.dtype))


def _unstack(x2):
    first_head = lax.broadcasted_iota(jnp.int32, (BAND, 128), 1) < HEAD_DIM
    return jnp.where(first_head, x2[:BAND], x2[BAND:])


def _for_later_blocks(nblk, units, fn):
    all_lanes = [slice(u * 128, (u + 1) * 128) for u in range(units)]
    unroll = max(1, ATTN_CHAINS // units)
    trips = (nblk - 1) // unroll
    if trips > 1:
        def step(i, carry):
            for j in range(unroll):
                for lanes in all_lanes:
                    fn(pl.multiple_of((1 + i * unroll + j) * BAND, BAND), lanes)
            return carry

        lax.fori_loop(0, trips, step, 0)
    else:
        trips = 0
    for sb in range(1 + trips * unroll, nblk):
        for lanes in all_lanes:
            fn(sb * BAND, lanes)


def _attn_fwd(name, q, k, v, d):
    nblk = S // d // BAND
    units = _attn_units(d)

    def body(q_ref, k_ref, v_ref, o_ref, lse_ref):
        def block(r0, k0, nkeys, lanes):
            q2 = _stack_heads(q_ref[pl.ds(r0, BAND), lanes])
            s = jnp.where(_band_mask(nkeys), _dot_nt(q2, k_ref[pl.ds(k0, nkeys), lanes]), NEG_INF)
            m = jnp.max(s, axis=-1, keepdims=True)
            p = jnp.exp(s - m)
            l = jnp.sum(p, axis=-1, keepdims=True)
            o2 = _dot_nn(p.astype(BF16), v_ref[pl.ds(k0, nkeys), lanes]) / l
            lse2 = jnp.broadcast_to(m + jnp.log(l), (2 * BAND, 128))
            o_ref[pl.ds(r0, BAND), lanes] = _unstack(o2).astype(o_ref.dtype)
            lse_ref[pl.ds(r0, BAND), lanes] = _unstack(lse2)

        for u in range(units):
            block(0, 0, BAND, slice(u * 128, (u + 1) * 128))

        _for_later_blocks(nblk, units, lambda r0, lanes: block(r0, r0 - BAND, 2 * BAND, lanes))

    spec = _class_spec(d)
    return pl.pallas_call(
        body, name=name, grid=(8 * d // units,),
        in_specs=[spec] * 3, out_specs=[spec] * 2,
        out_shape=[_sds((S // d, d * D), BF16), _sds((S // d, d * D), F32)],
        compiler_params=_cparams(1),
    )(q, k, v)


def _attn_bwd(name, q, k, v, do, lse, dd, d):
    nblk = S // d // BAND
    units = _attn_units(d)

    def body(q_ref, k_ref, v_ref, do_ref, lse_ref, dd_ref, dq_ref, dk_ref, dv_ref):
        def column(ref, r0, lanes, nkeys):
            tile = ref[pl.ds(r0, BAND), lanes]
            other = pltpu.roll(tile, HEAD_DIM, 1)
            first_head = lax.broadcasted_iota(jnp.int32, tile.shape, 1) < HEAD_DIM
            both = jnp.concatenate([jnp.where(first_head, tile, other), jnp.where(first_head, other, tile)], axis=0)
            return both if nkeys == BAND else jnp.concatenate([both, both], axis=1)

        def block(r0, k0, nkeys, lanes, first):
            q2 = _stack_heads(q_ref[pl.ds(r0, BAND), lanes])
            do2 = _stack_heads(do_ref[pl.ds(r0, BAND), lanes])
            kk = k_ref[pl.ds(k0, nkeys), lanes]
            vv = v_ref[pl.ds(k0, nkeys), lanes]
            s = jnp.where(_band_mask(nkeys), _dot_nt(q2, kk), NEG_INF)
            p = jnp.exp(s - column(lse_ref, r0, lanes, nkeys))
            ds = (p * (_dot_nt(do2, vv) - column(dd_ref, r0, lanes, nkeys))).astype(BF16)
            dq_ref[pl.ds(r0, BAND), lanes] = _unstack(_dot_nn(ds, kk)).astype(dq_ref.dtype)
            dk_part = _dot_tn(ds, q2)
            dv_part = _dot_tn(p.astype(BF16), do2)
            if first:
                dk_ref[pl.ds(k0, nkeys), lanes] = dk_part
                dv_ref[pl.ds(k0, nkeys), lanes] = dv_part
            else:
                dk_ref[pl.ds(k0, BAND), lanes] += dk_part[:BAND]
                dv_ref[pl.ds(k0, BAND), lanes] += dv_part[:BAND]
                dk_ref[pl.ds(k0 + BAND, BAND), lanes] = dk_part[BAND:]
                dv_ref[pl.ds(k0 + BAND, BAND), lanes] = dv_part[BAND:]

        for u in range(units):
            block(0, 0, BAND, slice(u * 128, (u + 1) * 128), True)

        _for_later_blocks(nblk, units, lambda r0, lanes: block(r0, r0 - BAND, 2 * BAND, lanes, False))

    spec = _class_spec(d)
    return pl.pallas_call(
        body, name=name, grid=(8 * d // units,),
        in_specs=[spec] * 6, out_specs=[spec] * 3,
        out_shape=[_sds((S // d, d * D), BF16)] + [_sds((S // d, d * D), F32)] * 2,
        compiler_params=_cparams(1),
    )(q, k, v, do, lse, dd)


MIX_TILE = 256
DILATIONS = tuple(d for _, d in BRANCHES)


def _branch_weights(la, lb, lc):
    m = jnp.maximum(jnp.maximum(la, lb), lc)
    ea, eb, ec = jnp.exp(la - m), jnp.exp(lb - m), jnp.exp(lc - m)
    den = ea + eb + ec
    return ea / den, eb / den, ec / den


def _mix_operands(outs, lses):
    specs = [_class_block(d, MIX_TILE) for d in DILATIONS] * 2
    scratch = [pltpu.VMEM((NCHUNK, MIX_TILE, 128), F32)] * 4
    return list(outs) + list(lses), specs, scratch


def _mix_fwd(name, outs, lses):
    def body(o0, o1, o2, l0, l1, l2, o_ref, to1, to2, tl1, tl2):
        for blk, tmp, d in ((o1, to1, DILATIONS[1]), (o2, to2, DILATIONS[2]), (l1, tl1, DILATIONS[1]), (l2, tl2, DILATIONS[2])):
            _tokens_from_classes(blk, tmp, d, MIX_TILE)
        for c in range(NCHUNK):
            wa, wb, wc = _branch_weights(l0[:, _chunk(c)], tl1[c], tl2[c])
            o_ref[:, _chunk(c)] = (wa * o0[:, _chunk(c)].astype(F32) + wb * to1[c] + wc * to2[c]).astype(o_ref.dtype)

    operands, specs, scratch = _mix_operands(outs, lses)
    return pl.pallas_call(
        body, name=name, grid=(S // MIX_TILE,),
        in_specs=specs, out_specs=_row_spec(MIX_TILE), out_shape=_sds((S, D), BF16),
        scratch_shapes=scratch, compiler_params=_cparams(1),
    )(*operands)


def _head_sum(x, ones_blockdiag):
    hi = x.astype(BF16)
    r1 = x - hi.astype(F32)
    mid = r1.astype(BF16)
    lo = (r1 - mid.astype(F32)).astype(BF16)
    return _dot_nn(hi, ones_blockdiag) + _dot_nn(mid, ones_blockdiag) + _dot_nn(lo, ones_blockdiag)


def _mix_bwd(name, do, outs, lses, ones_blockdiag):
    def body(do_ref, o0, o1, o2, l0, l1, l2, ones_ref, d0, d1, d2, t0, t1, t2,
             to1, to2, tl1, tl2, td1, td2, tt1, tt2):
        for blk, tmp, d in ((o1, to1, DILATIONS[1]), (o2, to2, DILATIONS[2]), (l1, tl1, DILATIONS[1]), (l2, tl2, DILATIONS[2])):
            _tokens_from_classes(blk, tmp, d, MIX_TILE)
        ones = ones_ref[...]
        for c in range(NCHUNK):
            w = _branch_weights(l0[:, _chunk(c)], tl1[c], tl2[c])
            dov = do_ref[:, _chunk(c)]
            o = w[0] * o0[:, _chunk(c)].astype(F32) + w[1] * to1[c] + w[2] * to2[c]
            t = _head_sum(dov * o, ones)
            d0[:, _chunk(c)] = (w[0] * dov).astype(d0.dtype)
            t0[:, _chunk(c)] = w[0] * t
            td1[c], tt1[c] = w[1] * dov, w[1] * t
            td2[c], tt2[c] = w[2] * dov, w[2] * t
        for tmp, blk, d in ((td1, d1, DILATIONS[1]), (tt1, t1, DILATIONS[1]), (td2, d2, DILATIONS[2]), (tt2, t2, DILATIONS[2])):
            _classes_from_tokens(tmp, blk, d, MIX_TILE)

    operands, specs, scratch = _mix_operands(outs, lses)
    out_specs = [_class_block(d, MIX_TILE) for d in DILATIONS] * 2
    out_shape = [_sds((S // d, d * D), BF16) for d in DILATIONS] + [_sds((S // d, d * D), F32) for d in DILATIONS]
    return pl.pallas_call(
        body, name=name, grid=(S // MIX_TILE,),
        in_specs=[_row_spec(MIX_TILE)] + specs + [_vec_spec(128, 128)],
        out_specs=out_specs, out_shape=out_shape,
        scratch_shapes=scratch + [pltpu.VMEM((NCHUNK, MIX_TILE, 128), F32)] * 4,
        compiler_params=_cparams(1),
    )(do, *operands, ones_blockdiag)


def _attn_bwd_post(name, grads, cos_t, sin_t):
    tm = MIX_TILE
    scale = HEAD_DIM ** -0.5

    def unrope(x, cs, sn):
        return x * cs - _swap_halves(x) * sn

    def body(*refs):
        in_refs = refs[:9]
        cos_ref, sin_ref, dq_ref, dkv_ref, tmp_ref = refs[9:]
        cs = cos_ref[...]
        sn = sin_ref[...]
        for g, d in enumerate(DILATIONS):
            for which, blk in enumerate(in_refs[3 * g:3 * g + 3]):
                if d > 1:
                    _tokens_from_classes(blk, tmp_ref, d, tm)
                for c in range(NCHUNK):
                    x = tmp_ref[c] if d > 1 else blk[:, _chunk(c)].astype(F32)
                    if which == 0:
                        dq_ref[:, _chunk(c, g * D)] = (unrope(x, cs, sn) * scale).astype(dq_ref.dtype)
                    elif which == 1:
                        dkv_ref[:, _chunk(c, g * D)] = unrope(x, cs, sn).astype(dkv_ref.dtype)
                    else:
                        dkv_ref[:, _chunk(c, QW + g * D)] = x.astype(dkv_ref.dtype)

    operands = [a for branch in grads for a in branch]
    tab = pl.BlockSpec((tm, 128), lambda i: (i, 0))
    return pl.pallas_call(
        body, name=name, grid=(S // tm,),
        in_specs=[_class_block(d, tm) for d in DILATIONS for _ in range(3)] + [tab, tab],
        out_specs=[pl.BlockSpec((tm, QW), lambda i: (i, 0)), pl.BlockSpec((tm, 2 * QW), lambda i: (i, 0))],
        out_shape=[_sds((S, QW), BF16), _sds((S, 2 * QW), BF16)],
        scratch_shapes=[pltpu.VMEM((NCHUNK, tm, 128), F32)],
        compiler_params=_cparams(1),
    )(*operands, cos_t, sin_t)


def _adamw(name, parts, w, m, v):
    n, rows, cols = parts.shape
    tr = rows
    for cand in (256, 176, 128, 64, 32, 16, 8):
        if rows % cand == 0:
            tr = cand
            break
    def body(p_ref, w_ref, m_ref, v_ref, g_ref, d_ref, nm_ref, nv_ref):
        g = p_ref[0].astype(F32)
        for j in range(1, n):
            g = g + p_ref[j].astype(F32)
        g_ref[...] = g
        d_ref[...], nm_ref[...], nv_ref[...] = _adam_update(g, w_ref[...], m_ref[...], v_ref[...])

    blk = pl.BlockSpec((tr, cols), lambda i: (i, 0))
    return pl.pallas_call(
        body, name=name, grid=(rows // tr,),
        in_specs=[pl.BlockSpec((n, tr, cols), lambda i: (0, i, 0)), blk, blk, blk],
        out_specs=[blk] * 4, out_shape=[_sds((rows, cols), F32)] * 4,
        compiler_params=_cparams(1),
    )(parts, w, m, v)


def _adam_update(g, w, m, v):
    c1 = 1.0 / (1.0 - ADAM_B1 ** ADAM_STEP)
    c2 = 1.0 / (1.0 - ADAM_B2 ** ADAM_STEP)
    nm = ADAM_B1 * m + (1.0 - ADAM_B1) * g
    nv = ADAM_B2 * v + (1.0 - ADAM_B2) * (g * g)
    return -ADAM_LR * ((nm * c1) / (jnp.sqrt(nv * c2) + ADAM_EPS) + ADAM_WD * w), nm, nv


GAIN_ROWS = 16


def _pack_small(name, gain_tiles, taps, sq):
    ng = len(gain_tiles)

    def body(*refs):
        o_ref = refs[-1]
        o_ref[...] = jnp.zeros_like(o_ref)
        for i in range(ng):
            o_ref[i:i + 1, :] = refs[i][0:1, :]
        o_ref[ng:ng + 3, :] = refs[ng][0:3, :]
        o_ref[ng + 3:ng + 4, :] = refs[ng + 1][...]

    return pl.pallas_call(body, name=name, out_shape=_sds((GAIN_ROWS, D), F32))(*gain_tiles, taps, sq)


def _adamw_gains(name, parts, params):
    np_ = len(params)
    shapes = [w.shape for w, _, _ in params]

    def body(p_ref, *refs):
        ins, outs = refs[:3 * np_], refs[3 * np_:]

        def total(lo, rows):
            g = p_ref[0, lo:lo + rows, :]
            for j in range(1, NDEV):
                g = g + p_ref[j, lo:lo + rows, :]
            return g

        lo = 0
        for i, shape in enumerate(shapes):
            g = total(lo, shape[0])
            lo += shape[0]
            w_ref, m_ref, v_ref = ins[3 * i:3 * i + 3]
            g_ref, d_ref, nm_ref, nv_ref = outs[4 * i:4 * i + 4]
            g_ref[...] = g
            d_ref[...], nm_ref[...], nv_ref[...] = _adam_update(g, w_ref[...], m_ref[...], v_ref[...])
        taps_ref, loss_ref = outs[-2], outs[-1]
        taps_ref[...] = jnp.zeros_like(taps_ref)
        taps_ref[0:3, :] = total(lo, 3)
        loss_ref[...] = jnp.sum(total(lo + 3, 1), axis=-1, keepdims=True) * (0.5 / D)

    out_shape = [_sds(shape, F32) for shape in shapes for _ in range(4)] + [_sds((8, D), F32), _sds((1, 1), F32)]
    outs = pl.pallas_call(body, name=name, out_shape=out_shape)(parts, *[a for p in params for a in p])
    return [list(outs[4 * i:4 * i + 4]) for i in range(np_)], outs[-2], outs[-1].reshape(())


def _exchange(name, arrays, kind):
    n = len(arrays)
    gather = kind == "gather"
    out_shape = [_sds((NDEV,) + a.shape if gather else a.shape, a.dtype) for a in arrays]

    def body(*refs):
        srcs, outs = refs[:n], refs[n:2 * n]
        send_sems, recv_sems, local_sems = refs[2 * n:]
        x, y, c = lax.axis_index("x"), lax.axis_index("y"), lax.axis_index("c")
        me = 4 * x + 2 * y + c
        pending = []
        for t in range(n):
            own = pltpu.make_async_copy(srcs[t] if gather else srcs[t].at[me], outs[t].at[me], local_sems.at[t])
            own.start()
            pending.append(own)
            for rel in range(1, NDEV):
                px = 1 - x if rel & 4 else x
                py = 1 - y if rel & 2 else y
                pc = 1 - c if rel & 1 else c
                peer = 4 * px + 2 * py + pc
                send = pltpu.make_async_remote_copy(
                    src_ref=srcs[t] if gather else srcs[t].at[peer], dst_ref=outs[t].at[me],
                    send_sem=send_sems.at[t, rel - 1], recv_sem=recv_sems.at[t, rel - 1],
                    device_id=(px, py, pc), device_id_type=MESH)
                send.start()
                arrive = pltpu.make_async_remote_copy(
                    src_ref=srcs[t] if gather else srcs[t].at[me], dst_ref=outs[t].at[peer],
                    send_sem=send_sems.at[t, rel - 1], recv_sem=recv_sems.at[t, rel - 1],
                    device_id=(px, py, pc), device_id_type=MESH)
                pending.append((send, arrive))
        for item in pending:
            if isinstance(item, tuple):
                item[0].wait_send()
                item[1].wait_recv()
            else:
                item.wait()

    any_spec = pl.BlockSpec(memory_space=pl.ANY)
    outs = pl.pallas_call(
        body, name=name,
        in_specs=[any_spec] * n, out_specs=[any_spec] * n, out_shape=out_shape,
        scratch_shapes=[pltpu.SemaphoreType.DMA((n, NDEV - 1)), pltpu.SemaphoreType.DMA((n, NDEV - 1)),
                        pltpu.SemaphoreType.DMA((n,))],
    )(*arrays)
    return list(outs)


_HBM_SPEC = pl.BlockSpec(memory_space=pltpu.HBM)
_SEM_SPEC = pl.BlockSpec(memory_space=pltpu.SEMAPHORE)
_DATAFLOW = pltpu.SideEffectType.DATAFLOW_SIDE_EFFECTING


def _peers():
    x, y, c = lax.axis_index("x"), lax.axis_index("y"), lax.axis_index("c")
    out = []
    for rel in range(1, NDEV):
        px = 1 - x if rel & 4 else x
        py = 1 - y if rel & 2 else y
        pc = 1 - c if rel & 1 else c
        out.append((rel - 1, (px, py, pc), 4 * px + 2 * py + pc))
    return 4 * x + 2 * y + c, out


def _hbm(a):
    return pltpu.HBM(a.shape, a.dtype)


def _own_slot(a, me, kind):
    mine = a[None] if kind == "gather" else lax.dynamic_slice_in_dim(a, me, 1, axis=0)
    shape = (NDEV,) + mine.shape[1:]
    return lax.dynamic_update_slice_in_dim(lax.empty(shape, a.dtype), mine, me, axis=0)


def _exchange_start(name, arrays, me, kind):
    n = len(arrays)
    gather = kind == "gather"
    lands = [_own_slot(a, me, kind) for a in arrays]

    def body(*refs):
        src_refs, land_refs = refs[:n], refs[n:2 * n]
        send_sems, recv_sems = refs[2 * n], refs[2 * n + 1]
        token = refs[-1]
        my_block, peers = _peers()
        for t in range(n):
            for slot, dev, block in peers:
                pltpu.make_async_remote_copy(
                    src_ref=src_refs[t] if gather else src_refs[t].at[block], dst_ref=land_refs[t].at[my_block],
                    send_sem=send_sems.at[t * (NDEV - 1) + slot], recv_sem=recv_sems.at[t * (NDEV - 1) + slot],
                    device_id=dev, device_id_type=MESH).start()
        token[...] = jnp.zeros_like(token)

    operands = [pltpu.with_memory_space_constraint(a, pltpu.HBM) for a in list(arrays) + lands]
    outs = pl.pallas_call(
        body, name=name,
        out_shape=(pltpu.SemaphoreType.DMA((n * (NDEV - 1),)), pltpu.SemaphoreType.DMA((n * (NDEV - 1),)),
                   *[_hbm(a) for a in operands], _sds((8, 128), F32)),
        in_specs=[_HBM_SPEC] * (2 * n),
        out_specs=(_SEM_SPEC, _SEM_SPEC, *[_HBM_SPEC] * (2 * n), pl.BlockSpec(memory_space=pltpu.VMEM)),
        input_output_aliases={i: 2 + i for i in range(2 * n)},
        compiler_params=pltpu.CompilerParams(has_side_effects=_DATAFLOW),
    )(*operands)
    return (outs[0], outs[1], list(outs[2:2 + n]), list(outs[2 + n:2 + 2 * n])), outs[-1]


def _exchange_wait(name, started, t, after, kind):
    send_sems, recv_sems, srcs, lands = started
    gather = kind == "gather"

    def body(src_ref, land_ref, send_ref, recv_ref, after_ref, src_out, land_out):
        _, peers = _peers()
        for slot, dev, block in peers:
            copy = pltpu.make_async_remote_copy(
                src_ref=src_ref if gather else src_ref.at[block], dst_ref=land_ref.at[block],
                send_sem=send_ref.at[t * (NDEV - 1) + slot], recv_sem=recv_ref.at[t * (NDEV - 1) + slot],
                device_id=dev, device_id_type=MESH)
            copy.wait_send()
            copy.wait_recv()

    return pl.pallas_call(
        body, name=name, out_shape=(_hbm(srcs[t]), _hbm(lands[t])),
        in_specs=(_HBM_SPEC, _HBM_SPEC, _SEM_SPEC, _SEM_SPEC, pl.BlockSpec(memory_space=pl.ANY)),
        out_specs=(_HBM_SPEC, _HBM_SPEC), input_output_aliases={0: 0, 1: 1},
        compiler_params=pltpu.CompilerParams(has_side_effects=_DATAFLOW),
    )(srcs[t], lands[t], send_sems, recv_sems, after)[1]


DIRECT_RELS = (1, 2, 4, 6)
RELAY_RELS = (2, 4, 6)


def _rel_peer(rel):
    x, y, c = lax.axis_index("x"), lax.axis_index("y"), lax.axis_index("c")
    px = 1 - x if rel & 4 else x
    py = 1 - y if rel & 2 else y
    pc = 1 - c if rel & 1 else c
    return (px, py, pc), 4 * px + 2 * py + pc


def _gather_start(name, shards, me):
    n, nr = len(shards), len(DIRECT_RELS)
    lands = [_own_slot(a, me, "gather") for a in shards]

    def body(*refs):
        src_refs, land_refs = refs[:n], refs[n:2 * n]
        send_sems, recv_sems = refs[2 * n], refs[2 * n + 1]
        _, my_block = _rel_peer(0)
        for t in range(n):
            for s, rel in enumerate(DIRECT_RELS):
                dev, _ = _rel_peer(rel)
                pltpu.make_async_remote_copy(
                    src_ref=src_refs[t], dst_ref=land_refs[t].at[my_block],
                    send_sem=send_sems.at[t * nr + s], recv_sem=recv_sems.at[t * nr + s],
                    device_id=dev, device_id_type=MESH).start()

    operands = [pltpu.with_memory_space_constraint(a, pltpu.HBM) for a in list(shards) + lands]
    outs = pl.pallas_call(
        body, name=name,
        out_shape=(pltpu.SemaphoreType.DMA((n * nr,)), pltpu.SemaphoreType.DMA((n * nr,)), *[_hbm(a) for a in operands]),
        in_specs=[_HBM_SPEC] * (2 * n), out_specs=(_SEM_SPEC, _SEM_SPEC, *[_HBM_SPEC] * (2 * n)),
        input_output_aliases={i: 2 + i for i in range(2 * n)},
        compiler_params=pltpu.CompilerParams(has_side_effects=_DATAFLOW),
    )(*operands)
    return outs[0], outs[1], list(outs[2:2 + n]), list(outs[2 + n:2 + 2 * n])


def _gather_wait(name, started, ts, after):
    send_sems, recv_sems, srcs, lands = started
    m, nr = len(ts), len(DIRECT_RELS)

    def body(*refs):
        src_refs, land_refs = refs[:m], refs[m:2 * m]
        send_ref, recv_ref = refs[2 * m], refs[2 * m + 1]
        for i, t in enumerate(ts):
            for s, rel in enumerate(DIRECT_RELS):
                dev, block = _rel_peer(rel)
                copy = pltpu.make_async_remote_copy(
                    src_ref=src_refs[i], dst_ref=land_refs[i].at[block],
                    send_sem=send_ref.at[t * nr + s], recv_sem=recv_ref.at[t * nr + s],
                    device_id=dev, device_id_type=MESH)
                copy.wait_send()
                copy.wait_recv()

    operands = [srcs[t] for t in ts] + [lands[t] for t in ts]
    outs = pl.pallas_call(
        body, name=name, out_shape=tuple(_hbm(a) for a in operands),
        in_specs=[_HBM_SPEC] * (2 * m) + [_SEM_SPEC, _SEM_SPEC, pl.BlockSpec(memory_space=pl.ANY)],
        out_specs=tuple([_HBM_SPEC] * (2 * m)), input_output_aliases={i: i for i in range(2 * m)},
        compiler_params=pltpu.CompilerParams(has_side_effects=_DATAFLOW),
    )(*operands, send_sems, recv_sems, after)
    return list(outs[m:])


def _relay_start(name, lands):
    m, nr = len(lands), len(RELAY_RELS)

    def body(*refs):
        land_refs, send_sems, recv_sems = refs[:m], refs[m], refs[m + 1]
        sibling, _ = _rel_peer(1)
        for i in range(m):
            for s, rel in enumerate(RELAY_RELS):
                _, block = _rel_peer(rel)
                pltpu.make_async_remote_copy(
                    src_ref=land_refs[i].at[block], dst_ref=land_refs[i].at[block],
                    send_sem=send_sems.at[i * nr + s], recv_sem=recv_sems.at[i * nr + s],
                    device_id=sibling, device_id_type=MESH).start()

    outs = pl.pallas_call(
        body, name=name,
        out_shape=(pltpu.SemaphoreType.DMA((m * nr,)), pltpu.SemaphoreType.DMA((m * nr,)), *[_hbm(a) for a in lands]),
        in_specs=[_HBM_SPEC] * m, out_specs=(_SEM_SPEC, _SEM_SPEC, *[_HBM_SPEC] * m),
        input_output_aliases={i: 2 + i for i in range(m)},
        compiler_params=pltpu.CompilerParams(has_side_effects=_DATAFLOW),
    )(*lands)
    return outs[0], outs[1], list(outs[2:])


def _relay_wait(name, relayed, after):
    send_sems, recv_sems, lands = relayed
    m, nr = len(lands), len(RELAY_RELS)

    def body(*refs):
        land_refs, send_ref, recv_ref = refs[:m], refs[m], refs[m + 1]
        sibling, _ = _rel_peer(1)
        for i in range(m):
            for s, rel in enumerate(RELAY_RELS):
                _, sent = _rel_peer(rel)
                _, arriving = _rel_peer(rel ^ 1)
                copy = pltpu.make_async_remote_copy(
                    src_ref=land_refs[i].at[sent], dst_ref=land_refs[i].at[arriving],
                    send_sem=send_ref.at[i * nr + s], recv_sem=recv_ref.at[i * nr + s],
                    device_id=sibling, device_id_type=MESH)
                copy.wait_send()
                copy.wait_recv()

    outs = pl.pallas_call(
        body, name=name, out_shape=tuple(_hbm(a) for a in lands),
        in_specs=[_HBM_SPEC] * m + [_SEM_SPEC, _SEM_SPEC, pl.BlockSpec(memory_space=pl.ANY)],
        out_specs=tuple([_HBM_SPEC] * m), input_output_aliases={i: i for i in range(m)},
        compiler_params=pltpu.CompilerParams(has_side_effects=_DATAFLOW),
    )(*lands, send_sems, recv_sems, after)
    return list(outs)


def _ffn_fwd(tag, n, wg, wd):
    gu, act = _gate_up_act(f"ffn_gate_up_{tag}", n, wg)
    wd4 = wd.reshape(NFB, FB, D)
    f = _fwd_kblocked(f"ffn_down_{tag}", act, wd4)
    return (n, gu, act, wg, wd4), f


def _ffn_bwd(tag, dh_out, h_in, f, saved, g_pre, g_post, send):
    n, gu, act, wg, wd4 = saved
    df, (dg_post,) = _rms_bwd(f"ffn_postnorm_bwd_{tag}", f, [(g_post, dh_out)], None, BF16)
    tok = send(f"down_{tag}", _bwd_w_kblocked(f"ffn_down_dw_{tag}", act, df).reshape(NDEV, DFF // NDEV, D))
    dgu = _down_dx_act_bwd(f"ffn_down_dx_{tag}", df, wd4, gu, tok).reshape(NDEV, S, FB)
    tok = send(f"gate_up_{tag}", _bwd_w_cols_blocked(f"ffn_gate_up_dw_{tag}", n, dgu))
    dn = _bwd_x_cols_blocked(f"ffn_gate_up_dx_{tag}", dgu, wg, after=tok)
    dh_in, (dg_pre,) = _rms_bwd(f"ffn_prenorm_bwd_{tag}", h_in, [(g_pre, dn)], dh_out, F32)
    return dh_in, dg_pre, dg_post


def kernel(x, positions, mix_norm_pre, mix_norm_post, ffn_norm_pre, ffn_norm_post, ffn_w_gate_up, ffn_w_down, conv_w_in, conv_w, conv_w_out, kv_norm, w_kv, w_q, w_o, loss_target, m_mix_norm_pre, m_mix_norm_post, m_ffn_norm_pre, m_ffn_norm_post, m_ffn_w_gate_up, m_ffn_w_down, m_conv_w_in, m_conv_w, m_conv_w_out, m_kv_norm, m_w_kv, m_w_q, m_w_o, v_mix_norm_pre, v_mix_norm_post, v_ffn_norm_pre, v_ffn_norm_post, v_ffn_w_gate_up, v_ffn_w_down, v_conv_w_in, v_conv_w, v_conv_w_out, v_kv_norm, v_w_kv, v_w_q, v_w_o):
    me = 4 * lax.axis_index("x") + 2 * lax.axis_index("y") + lax.axis_index("c")
    h0 = x.reshape(S, D)
    target = loss_target.reshape(S, D)
    row = lambda a, l: a[l].reshape(1, D)
    g_kv = kv_norm.reshape(1, D)

    cw_shard = jnp.pad(conv_w[0], ((0, 5), (0, 0)))
    names = ["conv_in", "conv_w", "conv_out", "gate_up_0", "down_0", "kv", "q", "o", "gate_up_1", "down_1"]
    shards = [conv_w_in[0], cw_shard, conv_w_out[0], ffn_w_gate_up[0], ffn_w_down[0],
              w_kv, w_q[0], w_o[0], ffn_w_gate_up[1], ffn_w_down[1]]
    shards = [s if n == "conv_w" else s.astype(BF16) for n, s in zip(names, shards)]
    gather = _gather_start("gather_weights_start", shards, me)

    def direct(group, after):
        lands = _gather_wait(f"gather_wait_{group[0]}", gather, [names.index(n) for n in group], after)
        return _relay_start(f"relay_start_{group[0]}", lands)

    def finish(group, relayed, after):
        return dict(zip(group, _relay_wait(f"relay_wait_{group[0]}", relayed, after)))

    sent = {}

    def send(name, grad):
        sent[name], token = _exchange_start(f"scatter_start_{name}", [grad], me, "scatter")
        return token

    groups = [["conv_in", "conv_w", "conv_out"], ["gate_up_0", "down_0"], ["kv", "q"], ["o", "gate_up_1", "down_1"]]
    n0 = _rms_fwd("mix_prenorm_0", h0, [row(mix_norm_pre, 0)])[0]
    w = finish(groups[0], direct(groups[0], n0), n0)
    win = w["conv_in"].transpose(1, 0, 2).reshape(D, 3 * D)
    cw = w["conv_w"].transpose(1, 0, 2).reshape(8, D)
    wout = w["conv_out"].reshape(D, D)
    z = _fwd_rows("conv_in", n0, win, BF16)
    pre = _conv_fwd("conv_gate", z, cw)
    relayed = direct(groups[1], pre)
    y0 = _fwd_rows("conv_out", pre, wout)
    h1, (n1,) = _resid_rms("mix_postnorm_0", h0, y0, row(mix_norm_post, 0), [row(ffn_norm_pre, 0)])
    w = finish(groups[1], relayed, n1)
    ffn0, f0 = _ffn_fwd("0", n1, w["gate_up_0"], w["down_0"])
    relayed = direct(groups[2], ffn0[2])
    h2, (nk, n2) = _resid_rms("ffn_postnorm_0", h1, f0, row(ffn_norm_post, 0), [g_kv, row(mix_norm_pre, 1)])

    w = finish(groups[2], relayed, nk)
    wkv = w["kv"].transpose(1, 0, 2).reshape(D, 2 * QW)
    wq = w["q"].transpose(1, 0, 2).reshape(D, QW)
    half = HEAD_DIM // 2
    inv_freq = ROPE_THETA ** (-jnp.arange(half, dtype=F32) / half)
    tables = _rope_tables("rope_tables", positions.reshape(S, 1), jnp.tile(inv_freq, 4).reshape(1, 128))
    qc, kc, vc, o_c, lse_c = [], [], [], [], []
    for g, d in enumerate(DILATIONS):
        kc.append(_proj_classes(f"k_proj_{g}", nk, wkv, g, d, tables, 1.0))
        vc.append(_proj_classes(f"v_proj_{g}", nk, wkv, len(DILATIONS) + g, d, None, None))
    relayed = direct(groups[3], vc[-1])
    for g, d in enumerate(DILATIONS):
        qc.append(_proj_classes(f"q_proj_{g}", n2, wq, g, d, tables, HEAD_DIM ** -0.5))
        o_g, lse_g = _attn_fwd(f"attn_fwd_{g}", qc[g], kc[g], vc[g], d)
        o_c.append(o_g)
        lse_c.append(lse_g)
    o_mix = _mix_fwd("attn_mix", o_c, lse_c)
    w = finish(groups[3], relayed, o_mix)
    wo = w["o"].reshape(D, D)
    y1 = _fwd_rows("attn_out", o_mix, wo)
    h3, (n3,) = _resid_rms("mix_postnorm_1", h2, y1, row(mix_norm_post, 1), [row(ffn_norm_pre, 1)])
    ffn1, f1 = _ffn_fwd("1", n3, w["gate_up_1"], w["down_1"])

    dh4, sq = _resid_rms_loss("ffn_postnorm_1_loss", h3, f1, row(ffn_norm_post, 1), target)

    dh3, dg_fpre1, dg_fpost1 = _ffn_bwd(
        "1", dh4, h3, f1, ffn1, row(ffn_norm_pre, 1), row(ffn_norm_post, 1), send)
    dy1, (dg_mpost1,) = _rms_bwd("mix_postnorm_bwd_1", y1, [(row(mix_norm_post, 1), dh3)], None, BF16)
    tok = send("o", _bwd_w_rows("attn_out_dw", o_mix, dy1).reshape(NDEV, D // NDEV, D))
    do = _bwd_x_rows("attn_out_dx", dy1, wo, F32, after=tok)
    lane = jnp.arange(128)
    ones_blockdiag = (lane[:, None] // HEAD_DIM == lane[None, :] // HEAD_DIM).astype(BF16)
    mixed = _mix_bwd("attn_mix_bwd", do, o_c, lse_c, ones_blockdiag)
    branch_grads = [_attn_bwd(f"attn_bwd_{g}", qc[g], kc[g], vc[g], mixed[g], lse_c[g], mixed[3 + g], d)
                    for g, d in enumerate(DILATIONS)]
    dq_raw, dkv = _attn_bwd_post("attn_bwd_post", branch_grads, *tables)
    tok = send("kv", _bwd_w_cols("kv_proj_dw", nk, dkv, 2 * QW // NDEV))
    dnk = _bwd_x_plain("kv_proj_dx", dkv, wkv, after=tok)
    tok = send("q", _bwd_w_cols("q_proj_dw", n2, dq_raw, QW // NDEV))
    dn2 = _bwd_x_plain("q_proj_dx", dq_raw, wq, after=tok)
    dh2, (dg_kv, dg_mpre1) = _rms_bwd("kv_and_mix_prenorm_bwd_1", h2,
                                      [(g_kv, dnk), (row(mix_norm_pre, 1), dn2)], dh3, F32)

    dh1, dg_fpre0, dg_fpost0 = _ffn_bwd(
        "0", dh2, h1, f0, ffn0, row(ffn_norm_pre, 0), row(ffn_norm_post, 0), send)
    dy0, (dg_mpost0,) = _rms_bwd("mix_postnorm_bwd_0", y0, [(row(mix_norm_post, 0), dh1)], None, BF16)
    tok = send("conv_out", _bwd_w_rows("conv_out_dw", pre, dy0).reshape(NDEV, D // NDEV, D))
    dpre = _bwd_x_rows("conv_out_dx", dy0, wout, BF16, after=tok)
    dz, dcw = _conv_bwd("conv_gate_bwd", z, dpre, cw)
    tok = send("conv_in", _bwd_w_cols("conv_in_dw", n0, dz, 3 * D // NDEV))
    dn0 = _bwd_x_plain("conv_in_dx", dz, win, after=tok)
    dh0, (dg_mpre0,) = _rms_bwd("mix_prenorm_bwd_0", h0, [(row(mix_norm_pre, 0), dn0)], dh1, F32)

    small = _pack_small("pack_small_grads", [dg_mpre0, dg_mpre1, dg_mpost0, dg_mpost1, dg_fpre0, dg_fpre1,
                                             dg_fpost0, dg_fpost1, dg_kv], dcw, sq)
    small_all = _exchange("gather_small_grads", [small], "gather")[0]

    done = [small_all]

    def upd(tag, w, m, v):
        parts = _exchange_wait(f"scatter_wait_{tag}", sent[tag], 0, done[-1], "scatter")
        shape = w.shape
        flat = lambda a: a.reshape(parts.shape[1:])
        res = _adamw(f"adamw_{tag}", parts, flat(w), flat(m), flat(v))
        done.append(res[0])
        return [r.reshape(shape) for r in res]

    def upd_layer(tag, l, w, m, v):
        return upd(f"{tag}_{l}", w[l], m[l], v[l])

    def stack(per_layer):
        return [jnp.stack([per_layer[0][i], per_layer[1][i]]) for i in range(4)]

    vec = lambda a: a.reshape(1, D)
    gain_res, taps, loss = _adamw_gains("adamw_gains", small_all, [
        (mix_norm_pre, m_mix_norm_pre, v_mix_norm_pre), (mix_norm_post, m_mix_norm_post, v_mix_norm_post),
        (ffn_norm_pre, m_ffn_norm_pre, v_ffn_norm_pre), (ffn_norm_post, m_ffn_norm_post, v_ffn_norm_post),
        (vec(kv_norm), vec(m_kv_norm), vec(v_kv_norm))])
    dcw_mine = lax.dynamic_slice(taps, (0, me * 128), (8, 128))
    pad8 = lambda a, fill: jnp.pad(a[0], ((0, 5), (0, 0)), constant_values=fill)
    cw_res = [r[0:3].reshape(1, 3, 128) for r in
              _adamw("adamw_conv_w", dcw_mine.reshape(1, 8, 128), cw_shard, pad8(m_conv_w, 0.0), pad8(v_conv_w, 1.0))]

    res = {
        "mix_norm_pre": gain_res[0],
        "mix_norm_post": gain_res[1],
        "ffn_norm_pre": gain_res[2],
        "ffn_norm_post": gain_res[3],
        "kv_norm": [r.reshape(D) for r in gain_res[4]],
        "conv_w": cw_res,
    }
    down, gate_up = {}, {}
    down[1] = upd_layer("down", 1, ffn_w_down, m_ffn_w_down, v_ffn_w_down)
    gate_up[1] = upd_layer("gate_up", 1, ffn_w_gate_up, m_ffn_w_gate_up, v_ffn_w_gate_up)
    res["w_o"] = upd("o", w_o, m_w_o, v_w_o)
    res["w_q"] = upd("q", w_q, m_w_q, v_w_q)
    res["w_kv"] = upd("kv", w_kv, m_w_kv, v_w_kv)
    down[0] = upd_layer("down", 0, ffn_w_down, m_ffn_w_down, v_ffn_w_down)
    gate_up[0] = upd_layer("gate_up", 0, ffn_w_gate_up, m_ffn_w_gate_up, v_ffn_w_gate_up)
    res["ffn_w_down"] = stack(down)
    res["ffn_w_gate_up"] = stack(gate_up)
    res["conv_w_out"] = upd("conv_out", conv_w_out, m_conv_w_out, v_conv_w_out)
    res["conv_w_in"] = upd("conv_in", conv_w_in, m_conv_w_in, v_conv_w_in)
    order = ["mix_norm_pre", "mix_norm_post", "ffn_norm_pre", "ffn_norm_post", "ffn_w_gate_up", "ffn_w_down",
             "conv_w_in", "conv_w", "conv_w_out", "kv_norm", "w_kv", "w_q", "w_o"]
    out = [loss, dh0.reshape(1, S, D)]
    for i in range(4):
        out += [res[name][i] for name in order]
    return tuple(out)
```

```python
import jax
import jax.numpy as jnp
from jax import lax
from jax.experimental import pallas as pl
from jax.experimental.pallas import tpu as pltpu

F32 = jnp.float32
BF16 = jnp.bfloat16

S = 4096
D = 1024
NDEV = 8
HEAD_DIM = 64
QW = 3072
DFF = 2816
FB = 704
NFB = 4
BRANCHES = ((128, 1), (512, 4), (2048, 16))
BAND = 128
ROPE_THETA = 10000.0
RMS_EPS = 1e-6
NEG_INF = -1e30
ADAM_LR, ADAM_B1, ADAM_B2, ADAM_EPS, ADAM_WD, ADAM_STEP = 0.001, 0.9, 0.999, 1e-08, 0.01, 10

VMEM_LIMIT_BYTES = 52 * 1024 * 1024
ROW_TILE = 512
MESH = pl.DeviceIdType.MESH


def _cparams(ngrid):
    return pltpu.CompilerParams(dimension_semantics=("arbitrary",) * ngrid,
                                vmem_limit_bytes=VMEM_LIMIT_BYTES)


def _sds(shape, dtype):
    return jax.ShapeDtypeStruct(tuple(shape), dtype)


_DIMS = {"nn": (((1,), (0,)), ((), ())),
         "nt": (((1,), (1,)), ((), ())),
         "tn": (((0,), (0,)), ((), ()))}


def _matmul(name, a, b, *, mode, grid, a_blk, a_map, b_blk, b_map, o_shape, o_blk, o_map, out_dtype, after=None,
            out_groups=1):
    nk = grid[2]
    dims = _DIMS[mode]
    acc_shape = tuple(s for s in o_blk if s is not None)
    if out_groups > 1:
        acc_shape = (acc_shape[1], out_groups * acc_shape[2])
    extra = [] if after is None else [after]

    def store(o_ref, val):
        if out_groups == 1:
            o_ref[...] = val.astype(o_ref.dtype)
        else:
            n = o_ref.shape[-1]
            for grp in range(out_groups):
                o_ref[grp] = val[:, grp * n:(grp + 1) * n].astype(o_ref.dtype)

    def body(a_ref, b_ref, *rest):
        o_ref, scratch = rest[len(extra)], rest[len(extra) + 1:]
        part = lax.dot_general(a_ref[...], b_ref[...], dims, preferred_element_type=F32)
        if nk == 1:
            store(o_ref, part)
            return
        acc_ref = scratch[0]
        k = pl.program_id(2)

        @pl.when(k == 0)
        def _():
            acc_ref[...] = part

        @pl.when(k > 0)
        def _():
            acc_ref[...] += part

        @pl.when(k == nk - 1)
        def _():
            store(o_ref, acc_ref[...])

    return pl.pallas_call(
        body, name=name, grid=grid,
        in_specs=[pl.BlockSpec(a_blk, a_map), pl.BlockSpec(b_blk, b_map)] + [pl.BlockSpec(memory_space=pl.ANY)] * len(extra),
        out_specs=pl.BlockSpec(o_blk, o_map),
        out_shape=_sds(o_shape, out_dtype),
        scratch_shapes=[] if nk == 1 else [pltpu.VMEM(acc_shape, F32)],
        compiler_params=_cparams(3),
    )(a, b, *extra)


TM = 1024
TK = S


def _fwd_rows(name, a, w, out_dtype=F32):
    kdim, n = w.shape
    tn = 512
    return _matmul(name, a, w, mode="nn", grid=(S // TM, n // tn, 1),
                   a_blk=(TM, kdim), a_map=lambda i, j, k: (i, 0),
                   b_blk=(kdim, tn), b_map=lambda i, j, k: (0, j),
                   o_shape=(S, n), o_blk=(TM, tn), o_map=lambda i, j, k: (i, j), out_dtype=out_dtype)


def _fwd_kblocked(name, a4, w4):
    nb, _, kb = a4.shape
    n = w4.shape[2]

    def body(a_ref, w_ref, o_ref):
        acc = _dot_nn(a_ref[0], w_ref[0])
        for j in range(1, nb):
            acc = acc + _dot_nn(a_ref[j], w_ref[j])
        o_ref[...] = acc

    return pl.pallas_call(
        body, name=name, grid=(S // TM,),
        in_specs=[pl.BlockSpec((nb, TM, kb), lambda i: (0, i, 0)), pl.BlockSpec((nb, kb, n), lambda i: (0, 0, 0))],
        out_specs=pl.BlockSpec((TM, n), lambda i: (i, 0)), out_shape=_sds((S, n), F32),
        compiler_params=_cparams(1),
    )(a4, w4)


def _bwd_x_cols_blocked(name, dy8, wg, after):
    _, kdim, n = wg.shape
    nk = NDEV // 2

    def body(a_ref, b_ref, after_ref, o_ref, acc_ref):
        k = pl.program_id(1)
        part = _dot_nt(a_ref[0], b_ref[0]) + _dot_nt(a_ref[1], b_ref[1])

        @pl.when(k == 0)
        def _():
            acc_ref[...] = part

        @pl.when(k > 0)
        def _():
            acc_ref[...] += part

        @pl.when(k == nk - 1)
        def _():
            o_ref[...] = acc_ref[...]

    return pl.pallas_call(
        body, name=name, grid=(S // TM, nk),
        in_specs=[pl.BlockSpec((2, None, TM, n), lambda i, k: (0, k, i, 0)),
                  pl.BlockSpec((2, None, kdim, n), lambda i, k: (0, k, 0, 0)),
                  pl.BlockSpec(memory_space=pl.ANY)],
        out_specs=pl.BlockSpec((TM, kdim), lambda i, k: (i, 0)), out_shape=_sds((S, kdim), F32),
        scratch_shapes=[pltpu.VMEM((TM, kdim), F32)],
        compiler_params=_cparams(2),
    )(dy8.reshape(2, nk, S, n), wg.reshape(2, nk, kdim, n), after)


def _bwd_x_rows(name, dy, w, out_dtype, after=None):
    kdim, n = w.shape
    tkk = 512
    return _matmul(name, dy, w, mode="nt", grid=(S // TM, kdim // tkk, 1),
                   a_blk=(TM, n), a_map=lambda i, j, k: (i, 0),
                   b_blk=(tkk, n), b_map=lambda i, j, k: (j, 0),
                   o_shape=(S, kdim), o_blk=(TM, tkk), o_map=lambda i, j, k: (i, j), out_dtype=out_dtype, after=after)


DW_COLS = 768


def _bwd_w_cols(name, a, dy, n):
    kdim = a.shape[1]
    groups = DW_COLS // n
    return _matmul(name, a, dy, mode="tn", grid=(1, NDEV // groups, S // TK),
                   a_blk=(TK, kdim), a_map=lambda i, j, k: (k, 0),
                   b_blk=(TK, DW_COLS), b_map=lambda i, j, k: (k, j),
                   o_shape=(NDEV, kdim, n), o_blk=(groups, kdim, n) if groups > 1 else (None, kdim, n),
                   o_map=lambda i, j, k: (j, 0, 0), out_dtype=BF16, out_groups=groups)


def _bwd_x_plain(name, dy, w, after=None):
    kdim, n = w.shape
    tm = TM if n <= 3 * D else TM // 2
    return _matmul(name, dy, w, mode="nt", grid=(S // tm, 1, 1),
                   a_blk=(tm, n), a_map=lambda i, j, k: (i, 0),
                   b_blk=(kdim, n), b_map=lambda i, j, k: (0, 0),
                   o_shape=(S, kdim), o_blk=(tm, kdim), o_map=lambda i, j, k: (i, 0), out_dtype=F32, after=after)


def _bwd_w_cols_blocked(name, a, dy8):
    kdim = a.shape[1]
    n = dy8.shape[2]
    return _matmul(name, a, dy8, mode="tn", grid=(1, NDEV, S // TK),
                   a_blk=(TK, kdim), a_map=lambda i, j, k: (k, 0),
                   b_blk=(None, TK, n), b_map=lambda i, j, k: (j, k, 0),
                   o_shape=(NDEV, kdim, n), o_blk=(None, kdim, n), o_map=lambda i, j, k: (j, 0, 0), out_dtype=BF16)


def _bwd_w_rows(name, a, dy):
    kdim = a.shape[1]
    n = dy.shape[1]
    tmm = 512
    return _matmul(name, a, dy, mode="tn", grid=(kdim // tmm, 1, S // TK),
                   a_blk=(TK, tmm), a_map=lambda i, j, k: (k, i),
                   b_blk=(TK, n), b_map=lambda i, j, k: (k, 0),
                   o_shape=(kdim, n), o_blk=(tmm, n), o_map=lambda i, j, k: (i, 0), out_dtype=BF16)


def _bwd_w_kblocked(name, a4, dy):
    nb, _, kb = a4.shape
    n = dy.shape[1]
    return _matmul(name, a4, dy, mode="tn", grid=(nb, 1, S // TK),
                   a_blk=(None, TK, kb), a_map=lambda i, j, k: (i, k, 0),
                   b_blk=(TK, n), b_map=lambda i, j, k: (k, 0),
                   o_shape=(nb, kb, n), o_blk=(None, kb, n), o_map=lambda i, j, k: (i, 0, 0), out_dtype=BF16)


def _rstd(x):
    return lax.rsqrt(jnp.mean(x * x, axis=-1, keepdims=True) + RMS_EPS)


def _row_spec(tm=ROW_TILE, width=D):
    return pl.BlockSpec((tm, width), lambda i: (i, 0))


def _vec_spec(rows=1, width=D):
    return pl.BlockSpec((rows, width), lambda i: (0, 0))


def _rms_fwd(name, x, gains):
    n = len(gains)

    def body(x_ref, *refs):
        x_val = x_ref[...]
        xh = x_val * _rstd(x_val)
        for g_ref, o_ref in zip(refs[:n], refs[n:]):
            o_ref[...] = (xh * g_ref[...]).astype(o_ref.dtype)

    outs = pl.pallas_call(
        body, name=name, grid=(S // ROW_TILE,),
        in_specs=[_row_spec()] + [_vec_spec()] * n,
        out_specs=[_row_spec()] * n,
        out_shape=[_sds((S, D), BF16)] * n,
        compiler_params=_cparams(1),
    )(x, *gains)
    return list(outs)


def _resid_rms(name, h, y, g, next_gains):
    n = len(next_gains)

    def body(h_ref, y_ref, g_ref, *refs):
        y_val = y_ref[...]
        h_new = h_ref[...] + (y_val * _rstd(y_val)) * g_ref[...]
        refs[n][...] = h_new
        hh = h_new * _rstd(h_new)
        for g2_ref, o_ref in zip(refs[:n], refs[n + 1:]):
            o_ref[...] = (hh * g2_ref[...]).astype(o_ref.dtype)

    outs = pl.pallas_call(
        body, name=name, grid=(S // ROW_TILE,),
        in_specs=[_row_spec(), _row_spec(), _vec_spec()] + [_vec_spec()] * n,
        out_specs=[_row_spec()] * (n + 1), out_shape=[_sds((S, D), F32)] + [_sds((S, D), BF16)] * n,
        compiler_params=_cparams(1),
    )(h, y, g, *next_gains)
    return outs[0], list(outs[1:])


def _resid_rms_loss(name, h, y, g, target):
    def body(h_ref, y_ref, g_ref, t_ref, dh_ref, dy_ref, dg_ref, part_ref):
        y_val = y_ref[...]
        gain = g_ref[...]
        e = h_ref[...] + (y_val * _rstd(y_val)) * gain - t_ref[...]
        dh = e * (1.0 / D)
        dh_ref[...] = dh
        step = pl.program_id(0)
        dy_ref[...] = _norm_bwd_rows(y_val, gain, dh, dg_ref, step).astype(dy_ref.dtype)
        part = jnp.sum(e * e, axis=0, keepdims=True)

        @pl.when(step == 0)
        def _():
            part_ref[...] = part

        @pl.when(step > 0)
        def _():
            part_ref[...] += part

    return pl.pallas_call(
        body, name=name, grid=(S // ROW_TILE,),
        in_specs=[_row_spec(), _row_spec(), _vec_spec(), _row_spec()],
        out_specs=[_row_spec(), _row_spec(), _vec_spec(8), _vec_spec()],
        out_shape=[_sds((S, D), F32), _sds((S, D), BF16), _sds((8, D), F32), _sds((1, D), F32)],
        compiler_params=_cparams(1),
    )(h, y, g, target)


def _norm_bwd_rows(x_val, g, dn, dg_ref, step):
    r = _rstd(x_val)
    xh = x_val * r
    dxh = dn * g
    part = jnp.sum(dn * xh, axis=0, keepdims=True)

    @pl.when(step == 0)
    def _():
        dg_ref[...] = jnp.zeros_like(dg_ref)

    dg_ref[0:1, :] += part
    return r * (dxh - xh * jnp.mean(dxh * xh, axis=-1, keepdims=True))


def _rms_bwd(name, x, pairs, dres, out_dtype, then=None):
    n = len(pairs)
    has_res = dres is not None
    chained = then is not None

    def body(x_ref, *refs):
        g_refs = refs[0:2 * n:2]
        dn_refs = refs[1:2 * n:2]
        pos = 2 * n
        res_ref = refs[pos] if has_res else None
        pos += int(has_res)
        if chained:
            y_ref, gy_ref = refs[pos], refs[pos + 1]
            pos += 2
        dx_ref = refs[pos]
        dg_refs = refs[pos + 1:pos + 1 + n]
        step = pl.program_id(0)
        x_val = x_ref[...]
        acc = res_ref[...] if has_res else jnp.zeros_like(x_val)
        for g_ref, dn_ref, dg_ref in zip(g_refs, dn_refs, dg_refs):
            acc = acc + _norm_bwd_rows(x_val, g_ref[...], dn_ref[...].astype(F32), dg_ref, step)
        dx_ref[...] = acc.astype(dx_ref.dtype)
        if chained:
            dy_ref, dgy_ref = refs[pos + 1 + n], refs[pos + 2 + n]
            dy_ref[...] = _norm_bwd_rows(y_ref[...], gy_ref[...], acc, dgy_ref, step).astype(dy_ref.dtype)

    operands = [x]
    in_specs = [_row_spec()]
    for g, dn in pairs:
        operands += [g, dn]
        in_specs += [_vec_spec(), _row_spec()]
    if has_res:
        operands.append(dres)
        in_specs.append(_row_spec())
    if chained:
        operands += [then[0], then[1]]
        in_specs += [_row_spec(), _vec_spec()]
    extra = int(chained)
    outs = pl.pallas_call(
        body, name=name, grid=(S // ROW_TILE,),
        in_specs=in_specs,
        out_specs=[_row_spec()] + [_vec_spec(8)] * n + [_row_spec(), _vec_spec(8)] * extra,
        out_shape=[_sds((S, D), out_dtype)] + [_sds((8, D), F32)] * n + [_sds((S, D), BF16), _sds((8, D), F32)] * extra,
        compiler_params=_cparams(1),
    )(*operands)
    if chained:
        return outs[0], list(outs[1:1 + n]), outs[1 + n], outs[2 + n]
    return outs[0], list(outs[1:])


def _shift_down(u, prev8, k):
    r = pltpu.roll(u, k, 0)
    p = pltpu.roll(prev8, k, 0)
    row = lax.broadcasted_iota(jnp.int32, prev8.shape, 0)
    top = jnp.where(row < k, p, r[0:8])
    return jnp.concatenate([top, r[8:]], axis=0)


def _shift_up(u, next8, k):
    tm = u.shape[0]
    r = pltpu.roll(u, tm - k, 0)
    p = pltpu.roll(next8, 8 - k, 0)
    row = lax.broadcasted_iota(jnp.int32, next8.shape, 0)
    bot = jnp.where(row >= 8 - k, p, r[tm - 8:tm])
    return jnp.concatenate([r[:tm - 8], bot], axis=0)


CONV_TILE = 512


def _halo_prev(col):
    return pl.BlockSpec((8, D), lambda i: (jnp.maximum(i * (CONV_TILE // 8) - 1, 0), col))


def _halo_next(col):
    last = S // 8 - 1
    return pl.BlockSpec((8, D), lambda i: (jnp.minimum((i + 1) * (CONV_TILE // 8), last), col))


def _conv_fwd(name, z, cw):
    def body(b_ref, c_ref, h_ref, cp_ref, hp_ref, cw_ref, o_ref):
        i = pl.program_id(0)
        u = c_ref[...].astype(F32) * h_ref[...].astype(F32)
        up = cp_ref[...].astype(F32) * hp_ref[...].astype(F32)
        up = jnp.where(i > 0, up, 0.0)
        cv = cw_ref[0:1, :] * _shift_down(u, up, 2) + cw_ref[1:2, :] * _shift_down(u, up, 1) + cw_ref[2:3, :] * u
        o_ref[...] = (b_ref[...].astype(F32) * cv).astype(o_ref.dtype)

    col = lambda c: pl.BlockSpec((CONV_TILE, D), lambda i: (i, c))
    return pl.pallas_call(
        body, name=name, grid=(S // CONV_TILE,),
        in_specs=[col(0), col(1), col(2), _halo_prev(1), _halo_prev(2), _vec_spec(8)],
        out_specs=_row_spec(CONV_TILE), out_shape=_sds((S, D), BF16),
        compiler_params=_cparams(1),
    )(z, z, z, z, z, cw)


def _conv_bwd(name, z, dpre, cw):
    nsteps = S // CONV_TILE

    def body(b_ref, c_ref, h_ref, cp_ref, hp_ref, dp_ref, dpn_ref, bn_ref, cw_ref, dz_ref, dcw_ref):
        i = pl.program_id(0)
        b = b_ref[...].astype(F32)
        c = c_ref[...].astype(F32)
        h = h_ref[...].astype(F32)
        dp = dp_ref[...].astype(F32)
        u = c * h
        up = jnp.where(i > 0, cp_ref[...].astype(F32) * hp_ref[...].astype(F32), 0.0)
        s1 = _shift_down(u, up, 1)
        s2 = _shift_down(u, up, 2)
        w0, w1, w2 = cw_ref[0:1, :], cw_ref[1:2, :], cw_ref[2:3, :]
        cv = w0 * s2 + w1 * s1 + w2 * u
        dcv = dp * b
        dcvn = jnp.where(i < nsteps - 1, dpn_ref[...].astype(F32) * bn_ref[...].astype(F32), 0.0)
        du = w2 * dcv + w1 * _shift_up(dcv, dcvn, 1) + w0 * _shift_up(dcv, dcvn, 2)
        dz_ref[:, 0:D] = (dp * cv).astype(dz_ref.dtype)
        dz_ref[:, D:2 * D] = (du * h).astype(dz_ref.dtype)
        dz_ref[:, 2 * D:3 * D] = (du * c).astype(dz_ref.dtype)

        @pl.when(i == 0)
        def _():
            dcw_ref[...] = jnp.zeros_like(dcw_ref)

        dcw_ref[0:1, :] += jnp.sum(dcv * s2, axis=0, keepdims=True)
        dcw_ref[1:2, :] += jnp.sum(dcv * s1, axis=0, keepdims=True)
        dcw_ref[2:3, :] += jnp.sum(dcv * u, axis=0, keepdims=True)

    col = lambda c: pl.BlockSpec((CONV_TILE, D), lambda i: (i, c))
    return pl.pallas_call(
        body, name=name, grid=(nsteps,),
        in_specs=[col(0), col(1), col(2), _halo_prev(1), _halo_prev(2),
                  _row_spec(CONV_TILE), _halo_next(0), _halo_next(0), _vec_spec(8)],
        out_specs=[pl.BlockSpec((CONV_TILE, 3 * D), lambda i: (i, 0)), _vec_spec(8)],
        out_shape=[_sds((S, 3 * D), BF16), _sds((8, D), F32)],
        compiler_params=_cparams(1),
    )(z, z, z, z, z, dpre, dpre, z, cw)


FFN_TM = 2048
_GU_BLOCK = pl.BlockSpec((2, None, FFN_TM, FB), lambda i, j: (0, j, i, 0))


def _gate_up_act(name, a, wg):
    kdim = a.shape[1]

    def body(a_ref, wgate_ref, wup_ref, gu_ref, act_ref):
        x = a_ref[...]
        g = _dot_nn(x, wgate_ref[...])
        u = _dot_nn(x, wup_ref[...])
        gu_ref[0] = g.astype(gu_ref.dtype)
        gu_ref[1] = u.astype(gu_ref.dtype)
        act_ref[...] = (g * jax.nn.sigmoid(g) * u).astype(act_ref.dtype)

    return pl.pallas_call(
        body, name=name, grid=(S // FFN_TM, NFB),
        in_specs=[pl.BlockSpec((FFN_TM, kdim), lambda i, j: (i, 0)),
                  pl.BlockSpec((None, kdim, FB), lambda i, j: (j, 0, 0)),
                  pl.BlockSpec((None, kdim, FB), lambda i, j: (j + NFB, 0, 0))],
        out_specs=[_GU_BLOCK, pl.BlockSpec((None, FFN_TM, FB), lambda i, j: (j, i, 0))],
        out_shape=[_sds((2, NFB, S, FB), BF16), _sds((NFB, S, FB), BF16)],
        compiler_params=_cparams(2),
    )(a, wg, wg)


def _down_dx_act_bwd(name, df, w4, gu, after):
    _, kb, n = w4.shape

    def body(df_ref, w_ref, gu_ref, after_ref, o_ref):
        d = _dot_nt(df_ref[...], w_ref[...])
        g = gu_ref[0].astype(F32)
        u = gu_ref[1].astype(F32)
        sg = jax.nn.sigmoid(g)
        o_ref[0] = (d * u * sg * (1.0 + g * (1.0 - sg))).astype(o_ref.dtype)
        o_ref[1] = (d * g * sg).astype(o_ref.dtype)

    return pl.pallas_call(
        body, name=name, grid=(S // FFN_TM, NFB),
        in_specs=[pl.BlockSpec((FFN_TM, n), lambda i, j: (i, 0)), pl.BlockSpec((None, kb, n), lambda i, j: (j, 0, 0)),
                  _GU_BLOCK, pl.BlockSpec(memory_space=pl.ANY)],
        out_specs=_GU_BLOCK, out_shape=_sds((2, NFB, S, FB), BF16),
        compiler_params=_cparams(2),
    )(df, w4, gu, after)


def _rope_tables(name, pos_col, inv_freq_row):
    def body(pos_ref, f_ref, cos_ref, sin_ref):
        ang = pos_ref[...].astype(F32) * f_ref[...]
        lane = lax.broadcasted_iota(jnp.int32, ang.shape, 1)
        s = jnp.sin(ang)
        cos_ref[...] = jnp.cos(ang)
        sin_ref[...] = jnp.where((lane % HEAD_DIM) < HEAD_DIM // 2, -s, s)

    tab = pl.BlockSpec((ROW_TILE, 128), lambda i: (i, 0))
    return pl.pallas_call(
        body, name=name, grid=(S // ROW_TILE,),
        in_specs=[pl.BlockSpec((ROW_TILE, 1), lambda i: (i, 0)), _vec_spec(1, 128)],
        out_specs=[tab, tab], out_shape=[_sds((S, 128), F32)] * 2,
        compiler_params=_cparams(1),
    )(pos_col, inv_freq_row)


def _swap_halves(t):
    lane = lax.broadcasted_iota(jnp.int32, t.shape, 1)
    first = (lane % HEAD_DIM) < HEAD_DIM // 2
    return jnp.where(first, pltpu.roll(t, 128 - HEAD_DIM // 2, 1), pltpu.roll(t, HEAD_DIM // 2, 1))


NCHUNK = D // 128


def _chunk(c, base=0):
    return slice(base + c * 128, base + (c + 1) * 128)


def _class_rows(r, d, tm):
    return pl.ds(r, tm // d, stride=d) if d > 1 else slice(None)


def _class_block(d, tm):
    return pl.BlockSpec((tm // d, d * D), lambda i: (i, 0))


def _tokens_from_classes(blk_ref, tmp_ref, d, tm):
    for r in range(d):
        for c in range(NCHUNK):
            tmp_ref[c, _class_rows(r, d, tm), :] = blk_ref[:, _chunk(c, r * D)].astype(F32)


def _classes_from_tokens(tmp_ref, blk_ref, d, tm):
    for r in range(d):
        for c in range(NCHUNK):
            blk_ref[:, _chunk(c, r * D)] = tmp_ref[c, _class_rows(r, d, tm), :].astype(blk_ref.dtype)


def _proj_classes(name, a, w, col, d, tables, scale):
    kdim = a.shape[1]
    rope = tables is not None

    def body(a_ref, w_ref, *refs):
        if rope:
            cos_ref, sin_ref, o_ref, tmp_ref = refs
        else:
            o_ref, tmp_ref = refs
        acc = _dot_nn(a_ref[...], w_ref[...])
        for c in range(NCHUNK):
            tmp_ref[c] = acc[:, _chunk(c)]
        for r in range(d):
            rows = _class_rows(r, d, TM)
            if rope:
                cs = cos_ref[rows, :]
                sn = sin_ref[rows, :]
            for c in range(NCHUNK):
                x = tmp_ref[c, rows, :]
                if rope:
                    x = (x * cs + _swap_halves(x) * sn) * scale
                o_ref[:, _chunk(c, r * D)] = x.astype(o_ref.dtype)

    tab = pl.BlockSpec((TM, 128), lambda i: (i, 0))
    return pl.pallas_call(
        body, name=name, grid=(S // TM,),
        in_specs=[pl.BlockSpec((TM, kdim), lambda i: (i, 0)), pl.BlockSpec((kdim, D), lambda i: (0, col))]
                 + ([tab, tab] if rope else []),
        out_specs=_class_block(d, TM), out_shape=_sds((S // d, d * D), BF16),
        scratch_shapes=[pltpu.VMEM((NCHUNK, TM, 128), F32)],
        compiler_params=_cparams(1),
    )(a, w, *(tables if rope else ()))


ATTN_CHAINS = 8


def _attn_units(d):
    nblk = S // d // BAND
    return max(1, 2 * ATTN_CHAINS // nblk)


def _class_spec(d):
    return pl.BlockSpec((S // d, 128 * _attn_units(d)), lambda cb: (0, cb))


def _dot_nt(a, b):
    return lax.dot_general(a, b, _DIMS["nt"], preferred_element_type=F32)


def _dot_tn(a, b):
    return lax.dot_general(a, b, _DIMS["tn"], preferred_element_type=F32)


def _dot_nn(a, b):
    return lax.dot_general(a, b, _DIMS["nn"], preferred_element_type=F32)


def _band_mask(nkeys):
    qi = lax.broadcasted_iota(jnp.int32, (2 * BAND, nkeys), 0) % BAND
    kj = lax.broadcasted_iota(jnp.int32, (2 * BAND, nkeys), 1)
    if nkeys == BAND:
        return kj <= qi
    dist = qi + BAND - kj
    return (dist >= 0) & (dist <= BAND)


def _stack_heads(x):
    row = lax.broadcasted_iota(jnp.int32, (2 * BAND, 128), 0)
    lane = lax.broadcasted_iota(jnp.int32, (2 * BAND, 128), 1)
    keep = (row < BAND) == (lane < HEAD_DIM)
    return jnp.where(keep, jnp.concatenate([x, x], axis=0), jnp.zeros((), x.dtype))


def _unstack(x2):
    first_head = lax.broadcasted_iota(jnp.int32, (BAND, 128), 1) < HEAD_DIM
    return jnp.where(first_head, x2[:BAND], x2[BAND:])


def _for_later_blocks(nblk, units, fn):
    all_lanes = [slice(u * 128, (u + 1) * 128) for u in range(units)]
    unroll = max(1, ATTN_CHAINS // units)
    trips = (nblk - 1) // unroll
    if trips > 1:
        def step(i, carry):
            for j in range(unroll):
                for lanes in all_lanes:
                    fn(pl.multiple_of((1 + i * unroll + j) * BAND, BAND), lanes)
            return carry

        lax.fori_loop(0, trips, step, 0)
    else:
        trips = 0
    for sb in range(1 + trips * unroll, nblk):
        for lanes in all_lanes:
            fn(sb * BAND, lanes)


def _attn_fwd(name, q, k, v, d):
    nblk = S // d // BAND
    units = _attn_units(d)

    def body(q_ref, k_ref, v_ref, o_ref, lse_ref):
        def block(r0, k0, nkeys, lanes):
            q2 = _stack_heads(q_ref[pl.ds(r0, BAND), lanes])
            s = jnp.where(_band_mask(nkeys), _dot_nt(q2, k_ref[pl.ds(k0, nkeys), lanes]), NEG_INF)
            m = jnp.max(s, axis=-1, keepdims=True)
            p = jnp.exp(s - m)
            l = jnp.sum(p, axis=-1, keepdims=True)
            o2 = _dot_nn(p.astype(BF16), v_ref[pl.ds(k0, nkeys), lanes]) / l
            lse2 = jnp.broadcast_to(m + jnp.log(l), (2 * BAND, 128))
            o_ref[pl.ds(r0, BAND), lanes] = _unstack(o2).astype(o_ref.dtype)
            lse_ref[pl.ds(r0, BAND), lanes] = _unstack(lse2)

        for u in range(units):
            block(0, 0, BAND, slice(u * 128, (u + 1) * 128))

        _for_later_blocks(nblk, units, lambda r0, lanes: block(r0, r0 - BAND, 2 * BAND, lanes))

    spec = _class_spec(d)
    return pl.pallas_call(
        body, name=name, grid=(8 * d // units,),
        in_specs=[spec] * 3, out_specs=[spec] * 2,
        out_shape=[_sds((S // d, d * D), BF16), _sds((S // d, d * D), F32)],
        compiler_params=_cparams(1),
    )(q, k, v)


def _attn_bwd(name, q, k, v, do, lse, dd, d):
    nblk = S // d // BAND
    units = _attn_units(d)

    def body(q_ref, k_ref, v_ref, do_ref, lse_ref, dd_ref, dq_ref, dk_ref, dv_ref):
        def column(ref, r0, lanes, nkeys):
            tile = ref[pl.ds(r0, BAND), lanes]
            other = pltpu.roll(tile, HEAD_DIM, 1)
            first_head = lax.broadcasted_iota(jnp.int32, tile.shape, 1) < HEAD_DIM
            both = jnp.concatenate([jnp.where(first_head, tile, other), jnp.where(first_head, other, tile)], axis=0)
            return both if nkeys == BAND else jnp.concatenate([both, both], axis=1)

        def block(r0, k0, nkeys, lanes, first):
            q2 = _stack_heads(q_ref[pl.ds(r0, BAND), lanes])
            do2 = _stack_heads(do_ref[pl.ds(r0, BAND), lanes])
            kk = k_ref[pl.ds(k0, nkeys), lanes]
            vv = v_ref[pl.ds(k0, nkeys), lanes]
            s = jnp.where(_band_mask(nkeys), _dot_nt(q2, kk), NEG_INF)
            p = jnp.exp(s - column(lse_ref, r0, lanes, nkeys))
            ds = (p * (_dot_nt(do2, vv) - column(dd_ref, r0, lanes, nkeys))).astype(BF16)
            dq_ref[pl.ds(r0, BAND), lanes] = _unstack(_dot_nn(ds, kk)).astype(dq_ref.dtype)
            dk_part = _dot_tn(ds, q2)
            dv_part = _dot_tn(p.astype(BF16), do2)
            if first:
                dk_ref[pl.ds(k0, nkeys), lanes] = dk_part
                dv_ref[pl.ds(k0, nkeys), lanes] = dv_part
            else:
                dk_ref[pl.ds(k0, BAND), lanes] += dk_part[:BAND]
                dv_ref[pl.ds(k0, BAND), lanes] += dv_part[:BAND]
                dk_ref[pl.ds(k0 + BAND, BAND), lanes] = dk_part[BAND:]
                dv_ref[pl.ds(k0 + BAND, BAND), lanes] = dv_part[BAND:]

        for u in range(units):
            block(0, 0, BAND, slice(u * 128, (u + 1) * 128), True)

        _for_later_blocks(nblk, units, lambda r0, lanes: block(r0, r0 - BAND, 2 * BAND, lanes, False))

    spec = _class_spec(d)
    return pl.pallas_call(
        body, name=name, grid=(8 * d // units,),
        in_specs=[spec] * 6, out_specs=[spec] * 3,
        out_shape=[_sds((S // d, d * D), BF16)] + [_sds((S // d, d * D), F32)] * 2,
        compiler_params=_cparams(1),
    )(q, k, v, do, lse, dd)


MIX_TILE = 256
DILATIONS = tuple(d for _, d in BRANCHES)


def _branch_weights(la, lb, lc):
    m = jnp.maximum(jnp.maximum(la, lb), lc)
    ea, eb, ec = jnp.exp(la - m), jnp.exp(lb - m), jnp.exp(lc - m)
    den = ea + eb + ec
    return ea / den, eb / den, ec / den


def _mix_operands(outs, lses):
    specs = [_class_block(d, MIX_TILE) for d in DILATIONS] * 2
    scratch = [pltpu.VMEM((NCHUNK, MIX_TILE, 128), F32)] * 4
    return list(outs) + list(lses), specs, scratch


def _mix_fwd(name, outs, lses):
    def body(o0, o1, o2, l0, l1, l2, o_ref, to1, to2, tl1, tl2):
        for blk, tmp, d in ((o1, to1, DILATIONS[1]), (o2, to2, DILATIONS[2]), (l1, tl1, DILATIONS[1]), (l2, tl2, DILATIONS[2])):
            _tokens_from_classes(blk, tmp, d, MIX_TILE)
        for c in range(NCHUNK):
            wa, wb, wc = _branch_weights(l0[:, _chunk(c)], tl1[c], tl2[c])
            o_ref[:, _chunk(c)] = (wa * o0[:, _chunk(c)].astype(F32) + wb * to1[c] + wc * to2[c]).astype(o_ref.dtype)

    operands, specs, scratch = _mix_operands(outs, lses)
    return pl.pallas_call(
        body, name=name, grid=(S // MIX_TILE,),
        in_specs=specs, out_specs=_row_spec(MIX_TILE), out_shape=_sds((S, D), BF16),
        scratch_shapes=scratch, compiler_params=_cparams(1),
    )(*operands)


def _head_sum(x, ones_blockdiag):
    hi = x.astype(BF16)
    r1 = x - hi.astype(F32)
    mid = r1.astype(BF16)
    lo = (r1 - mid.astype(F32)).astype(BF16)
    return _dot_nn(hi, ones_blockdiag) + _dot_nn(mid, ones_blockdiag) + _dot_nn(lo, ones_blockdiag)


def _mix_bwd(name, do, outs, lses, ones_blockdiag):
    def body(do_ref, o0, o1, o2, l0, l1, l2, ones_ref, d0, d1, d2, t0, t1, t2,
             to1, to2, tl1, tl2, td1, td2, tt1, tt2):
        for blk, tmp, d in ((o1, to1, DILATIONS[1]), (o2, to2, DILATIONS[2]), (l1, tl1, DILATIONS[1]), (l2, tl2, DILATIONS[2])):
            _tokens_from_classes(blk, tmp, d, MIX_TILE)
        ones = ones_ref[...]
        for c in range(NCHUNK):
            w = _branch_weights(l0[:, _chunk(c)], tl1[c], tl2[c])
            dov = do_ref[:, _chunk(c)]
            o = w[0] * o0[:, _chunk(c)].astype(F32) + w[1] * to1[c] + w[2] * to2[c]
            t = _head_sum(dov * o, ones)
            d0[:, _chunk(c)] = (w[0] * dov).astype(d0.dtype)
            t0[:, _chunk(c)] = w[0] * t
            td1[c], tt1[c] = w[1] * dov, w[1] * t
            td2[c], tt2[c] = w[2] * dov, w[2] * t
        for tmp, blk, d in ((td1, d1, DILATIONS[1]), (tt1, t1, DILATIONS[1]), (td2, d2, DILATIONS[2]), (tt2, t2, DILATIONS[2])):
            _classes_from_tokens(tmp, blk, d, MIX_TILE)

    operands, specs, scratch = _mix_operands(outs, lses)
    out_specs = [_class_block(d, MIX_TILE) for d in DILATIONS] * 2
    out_shape = [_sds((S // d, d * D), BF16) for d in DILATIONS] + [_sds((S // d, d * D), F32) for d in DILATIONS]
    return pl.pallas_call(
        body, name=name, grid=(S // MIX_TILE,),
        in_specs=[_row_spec(MIX_TILE)] + specs + [_vec_spec(128, 128)],
        out_specs=out_specs, out_shape=out_shape,
        scratch_shapes=scratch + [pltpu.VMEM((NCHUNK, MIX_TILE, 128), F32)] * 4,
        compiler_params=_cparams(1),
    )(do, *operands, ones_blockdiag)


def _attn_bwd_post(name, grads, cos_t, sin_t):
    tm = MIX_TILE
    scale = HEAD_DIM ** -0.5

    def unrope(x, cs, sn):
        return x * cs - _swap_halves(x) * sn

    def body(*refs):
        in_refs = refs[:9]
        cos_ref, sin_ref, dq_ref, dkv_ref, tmp_ref = refs[9:]
        cs = cos_ref[...]
        sn = sin_ref[...]
        for g, d in enumerate(DILATIONS):
            for which, blk in enumerate(in_refs[3 * g:3 * g + 3]):
                if d > 1:
                    _tokens_from_classes(blk, tmp_ref, d, tm)
                for c in range(NCHUNK):
                    x = tmp_ref[c] if d > 1 else blk[:, _chunk(c)].astype(F32)
                    if which == 0:
                        dq_ref[:, _chunk(c, g * D)] = (unrope(x, cs, sn) * scale).astype(dq_ref.dtype)
                    elif which == 1:
                        dkv_ref[:, _chunk(c, g * D)] = unrope(x, cs, sn).astype(dkv_ref.dtype)
                    else:
                        dkv_ref[:, _chunk(c, QW + g * D)] = x.astype(dkv_ref.dtype)

    operands = [a for branch in grads for a in branch]
    tab = pl.BlockSpec((tm, 128), lambda i: (i, 0))
    return pl.pallas_call(
        body, name=name, grid=(S // tm,),
        in_specs=[_class_block(d, tm) for d in DILATIONS for _ in range(3)] + [tab, tab],
        out_specs=[pl.BlockSpec((tm, QW), lambda i: (i, 0)), pl.BlockSpec((tm, 2 * QW), lambda i: (i, 0))],
        out_shape=[_sds((S, QW), BF16), _sds((S, 2 * QW), BF16)],
        scratch_shapes=[pltpu.VMEM((NCHUNK, tm, 128), F32)],
        compiler_params=_cparams(1),
    )(*operands, cos_t, sin_t)


def _adamw(name, parts, w, m, v):
    n, rows, cols = parts.shape
    tr = rows
    for cand in (256, 176, 128, 64, 32, 16, 8):
        if rows % cand == 0:
            tr = cand
            break
    def body(p_ref, w_ref, m_ref, v_ref, g_ref, d_ref, nm_ref, nv_ref):
        g = p_ref[0].astype(F32)
        for j in range(1, n):
            g = g + p_ref[j].astype(F32)
        g_ref[...] = g
        d_ref[...], nm_ref[...], nv_ref[...] = _adam_update(g, w_ref[...], m_ref[...], v_ref[...])

    blk = pl.BlockSpec((tr, cols), lambda i: (i, 0))
    return pl.pallas_call(
        body, name=name, grid=(rows // tr,),
        in_specs=[pl.BlockSpec((n, tr, cols), lambda i: (0, i, 0)), blk, blk, blk],
        out_specs=[blk] * 4, out_shape=[_sds((rows, cols), F32)] * 4,
        compiler_params=_cparams(1),
    )(parts, w, m, v)


def _adam_update(g, w, m, v):
    c1 = 1.0 / (1.0 - ADAM_B1 ** ADAM_STEP)
    c2 = 1.0 / (1.0 - ADAM_B2 ** ADAM_STEP)
    nm = ADAM_B1 * m + (1.0 - ADAM_B1) * g
    nv = ADAM_B2 * v + (1.0 - ADAM_B2) * (g * g)
    return -ADAM_LR * ((nm * c1) / (jnp.sqrt(nv * c2) + ADAM_EPS) + ADAM_WD * w), nm, nv


GAIN_ROWS = 16


def _pack_small(name, gain_tiles, taps, sq):
    ng = len(gain_tiles)

    def body(*refs):
        o_ref = refs[-1]
        o_ref[...] = jnp.zeros_like(o_ref)
        for i in range(ng):
            o_ref[i:i + 1, :] = refs[i][0:1, :]
        o_ref[ng:ng + 3, :] = refs[ng][0:3, :]
        o_ref[ng + 3:ng + 4, :] = refs[ng + 1][...]

    return pl.pallas_call(body, name=name, out_shape=_sds((GAIN_ROWS, D), F32))(*gain_tiles, taps, sq)


def _adamw_gains(name, parts, params):
    np_ = len(params)
    shapes = [w.shape for w, _, _ in params]

    def body(p_ref, *refs):
        ins, outs = refs[:3 * np_], refs[3 * np_:]

        def total(lo, rows):
            g = p_ref[0, lo:lo + rows, :]
            for j in range(1, NDEV):
                g = g + p_ref[j, lo:lo + rows, :]
            return g

        lo = 0
        for i, shape in enumerate(shapes):
            g = total(lo, shape[0])
            lo += shape[0]
            w_ref, m_ref, v_ref = ins[3 * i:3 * i + 3]
            g_ref, d_ref, nm_ref, nv_ref = outs[4 * i:4 * i + 4]
            g_ref[...] = g
            d_ref[...], nm_ref[...], nv_ref[...] = _adam_update(g, w_ref[...], m_ref[...], v_ref[...])
        taps_ref, loss_ref = outs[-2], outs[-1]
        taps_ref[...] = jnp.zeros_like(taps_ref)
        taps_ref[0:3, :] = total(lo, 3)
        loss_ref[...] = jnp.sum(total(lo + 3, 1), axis=-1, keepdims=True) * (0.5 / D)

    out_shape = [_sds(shape, F32) for shape in shapes for _ in range(4)] + [_sds((8, D), F32), _sds((1, 1), F32)]
    outs = pl.pallas_call(body, name=name, out_shape=out_shape)(parts, *[a for p in params for a in p])
    return [list(outs[4 * i:4 * i + 4]) for i in range(np_)], outs[-2], outs[-1].reshape(())


def _exchange(name, arrays, kind):
    n = len(arrays)
    gather = kind == "gather"
    out_shape = [_sds((NDEV,) + a.shape if gather else a.shape, a.dtype) for a in arrays]

    def body(*refs):
        srcs, outs = refs[:n], refs[n:2 * n]
        send_sems, recv_sems, local_sems = refs[2 * n:]
        x, y, c = lax.axis_index("x"), lax.axis_index("y"), lax.axis_index("c")
        me = 4 * x + 2 * y + c
        pending = []
        for t in range(n):
            own = pltpu.make_async_copy(srcs[t] if gather else srcs[t].at[me], outs[t].at[me], local_sems.at[t])
            own.start()
            pending.append(own)
            for rel in range(1, NDEV):
                px = 1 - x if rel & 4 else x
                py = 1 - y if rel & 2 else y
                pc = 1 - c if rel & 1 else c
                peer = 4 * px + 2 * py + pc
                send = pltpu.make_async_remote_copy(
                    src_ref=srcs[t] if gather else srcs[t].at[peer], dst_ref=outs[t].at[me],
                    send_sem=send_sems.at[t, rel - 1], recv_sem=recv_sems.at[t, rel - 1],
                    device_id=(px, py, pc), device_id_type=MESH)
                send.start()
                arrive = pltpu.make_async_remote_copy(
                    src_ref=srcs[t] if gather else srcs[t].at[me], dst_ref=outs[t].at[peer],
                    send_sem=send_sems.at[t, rel - 1], recv_sem=recv_sems.at[t, rel - 1],
                    device_id=(px, py, pc), device_id_type=MESH)
                pending.append((send, arrive))
        for item in pending:
            if isinstance(item, tuple):
                item[0].wait_send()
                item[1].wait_recv()
            else:
                item.wait()

    any_spec = pl.BlockSpec(memory_space=pl.ANY)
    outs = pl.pallas_call(
        body, name=name,
        in_specs=[any_spec] * n, out_specs=[any_spec] * n, out_shape=out_shape,
        scratch_shapes=[pltpu.SemaphoreType.DMA((n, NDEV - 1)), pltpu.SemaphoreType.DMA((n, NDEV - 1)),
                        pltpu.SemaphoreType.DMA((n,))],
    )(*arrays)
    return list(outs)


_HBM_SPEC = pl.BlockSpec(memory_space=pltpu.HBM)
_SEM_SPEC = pl.BlockSpec(memory_space=pltpu.SEMAPHORE)
_DATAFLOW = pltpu.SideEffectType.DATAFLOW_SIDE_EFFECTING


def _peers():
    x, y, c = lax.axis_index("x"), lax.axis_index("y"), lax.axis_index("c")
    out = []
    for rel in range(1, NDEV):
        px = 1 - x if rel & 4 else x
        py = 1 - y if rel & 2 else y
        pc = 1 - c if rel & 1 else c
        out.append((rel - 1, (px, py, pc), 4 * px + 2 * py + pc))
    return 4 * x + 2 * y + c, out


def _hbm(a):
    return pltpu.HBM(a.shape, a.dtype)


def _own_slot(a, me, kind):
    mine = a[None] if kind == "gather" else lax.dynamic_slice_in_dim(a, me, 1, axis=0)
    shape = (NDEV,) + mine.shape[1:]
    return lax.dynamic_update_slice_in_dim(lax.empty(shape, a.dtype), mine, me, axis=0)


def _exchange_start(name, arrays, me, kind):
    n = len(arrays)
    gather = kind == "gather"
    lands = [_own_slot(a, me, kind) for a in arrays]

    def body(*refs):
        src_refs, land_refs = refs[:n], refs[n:2 * n]
        send_sems, recv_sems = refs[2 * n], refs[2 * n + 1]
        token = refs[-1]
        my_block, peers = _peers()
        for t in range(n):
            for slot, dev, block in peers:
                pltpu.make_async_remote_copy(
                    src_ref=src_refs[t] if gather else src_refs[t].at[block], dst_ref=land_refs[t].at[my_block],
                    send_sem=send_sems.at[t * (NDEV - 1) + slot], recv_sem=recv_sems.at[t * (NDEV - 1) + slot],
                    device_id=dev, device_id_type=MESH).start()
        token[...] = jnp.zeros_like(token)

    operands = [pltpu.with_memory_space_constraint(a, pltpu.HBM) for a in list(arrays) + lands]
    outs = pl.pallas_call(
        body, name=name,
        out_shape=(pltpu.SemaphoreType.DMA((n * (NDEV - 1),)), pltpu.SemaphoreType.DMA((n * (NDEV - 1),)),
                   *[_hbm(a) for a in operands], _sds((8, 128), F32)),
        in_specs=[_HBM_SPEC] * (2 * n),
        out_specs=(_SEM_SPEC, _SEM_SPEC, *[_HBM_SPEC] * (2 * n), pl.BlockSpec(memory_space=pltpu.VMEM)),
        input_output_aliases={i: 2 + i for i in range(2 * n)},
        compiler_params=pltpu.CompilerParams(has_side_effects=_DATAFLOW),
    )(*operands)
    return (outs[0], outs[1], list(outs[2:2 + n]), list(outs[2 + n:2 + 2 * n])), outs[-1]


def _exchange_wait(name, started, t, after, kind):
    send_sems, recv_sems, srcs, lands = started
    gather = kind == "gather"

    def body(src_ref, land_ref, send_ref, recv_ref, after_ref, src_out, land_out):
        _, peers = _peers()
        for slot, dev, block in peers:
            copy = pltpu.make_async_remote_copy(
                src_ref=src_ref if gather else src_ref.at[block], dst_ref=land_ref.at[block],
                send_sem=send_ref.at[t * (NDEV - 1) + slot], recv_sem=recv_ref.at[t * (NDEV - 1) + slot],
                device_id=dev, device_id_type=MESH)
            copy.wait_send()
            copy.wait_recv()

    return pl.pallas_call(
        body, name=name, out_shape=(_hbm(srcs[t]), _hbm(lands[t])),
        in_specs=(_HBM_SPEC, _HBM_SPEC, _SEM_SPEC, _SEM_SPEC, pl.BlockSpec(memory_space=pl.ANY)),
        out_specs=(_HBM_SPEC, _HBM_SPEC), input_output_aliases={0: 0, 1: 1},
        compiler_params=pltpu.CompilerParams(has_side_effects=_DATAFLOW),
    )(srcs[t], lands[t], send_sems, recv_sems, after)[1]


DIRECT_RELS = (1, 2, 4, 6)
RELAY_RELS = (2, 4, 6)


def _rel_peer(rel):
    x, y, c = lax.axis_index("x"), lax.axis_index("y"), lax.axis_index("c")
    px = 1 - x if rel & 4 else x
    py = 1 - y if rel & 2 else y
    pc = 1 - c if rel & 1 else c
    return (px, py, pc), 4 * px + 2 * py + pc


def _gather_start(name, shards, me):
    n, nr = len(shards), len(DIRECT_RELS)
    lands = [_own_slot(a, me, "gather") for a in shards]

    def body(*refs):
        src_refs, land_refs = refs[:n], refs[n:2 * n]
        send_sems, recv_sems = refs[2 * n], refs[2 * n + 1]
        _, my_block = _rel_peer(0)
        for t in range(n):
            for s, rel in enumerate(DIRECT_RELS):
                dev, _ = _rel_peer(rel)
                pltpu.make_async_remote_copy(
                    src_ref=src_refs[t], dst_ref=land_refs[t].at[my_block],
                    send_sem=send_sems.at[t * nr + s], recv_sem=recv_sems.at[t * nr + s],
                    device_id=dev, device_id_type=MESH).start()

    operands = [pltpu.with_memory_space_constraint(a, pltpu.HBM) for a in list(shards) + lands]
    outs = pl.pallas_call(
        body, name=name,
        out_shape=(pltpu.SemaphoreType.DMA((n * nr,)), pltpu.SemaphoreType.DMA((n * nr,)), *[_hbm(a) for a in operands]),
        in_specs=[_HBM_SPEC] * (2 * n), out_specs=(_SEM_SPEC, _SEM_SPEC, *[_HBM_SPEC] * (2 * n)),
        input_output_aliases={i: 2 + i for i in range(2 * n)},
        compiler_params=pltpu.CompilerParams(has_side_effects=_DATAFLOW),
    )(*operands)
    return outs[0], outs[1], list(outs[2:2 + n]), list(outs[2 + n:2 + 2 * n])


def _gather_wait(name, started, ts, after):
    send_sems, recv_sems, srcs, lands = started
    m, nr = len(ts), len(DIRECT_RELS)

    def body(*refs):
        src_refs, land_refs = refs[:m], refs[m:2 * m]
        send_ref, recv_ref = refs[2 * m], refs[2 * m + 1]
        for i, t in enumerate(ts):
            for s, rel in enumerate(DIRECT_RELS):
                dev, block = _rel_peer(rel)
                copy = pltpu.make_async_remote_copy(
                    src_ref=src_refs[i], dst_ref=land_refs[i].at[block],
                    send_sem=send_ref.at[t * nr + s], recv_sem=recv_ref.at[t * nr + s],
                    device_id=dev, device_id_type=MESH)
                copy.wait_send()
                copy.wait_recv()

    operands = [srcs[t] for t in ts] + [lands[t] for t in ts]
    outs = pl.pallas_call(
        body, name=name, out_shape=tuple(_hbm(a) for a in operands),
        in_specs=[_HBM_SPEC] * (2 * m) + [_SEM_SPEC, _SEM_SPEC, pl.BlockSpec(memory_space=pl.ANY)],
        out_specs=tuple([_HBM_SPEC] * (2 * m)), input_output_aliases={i: i for i in range(2 * m)},
        compiler_params=pltpu.CompilerParams(has_side_effects=_DATAFLOW),
    )(*operands, send_sems, recv_sems, after)
    return list(outs[m:])


def _relay_start(name, lands):
    m, nr = len(lands), len(RELAY_RELS)

    def body(*refs):
        land_refs, send_sems, recv_sems = refs[:m], refs[m], refs[m + 1]
        sibling, _ = _rel_peer(1)
        for i in range(m):
            for s, rel in enumerate(RELAY_RELS):
                _, block = _rel_peer(rel)
                pltpu.make_async_remote_copy(
                    src_ref=land_refs[i].at[block], dst_ref=land_refs[i].at[block],
                    send_sem=send_sems.at[i * nr + s], recv_sem=recv_sems.at[i * nr + s],
                    device_id=sibling, device_id_type=MESH).start()

    outs = pl.pallas_call(
        body, name=name,
        out_shape=(pltpu.SemaphoreType.DMA((m * nr,)), pltpu.SemaphoreType.DMA((m * nr,)), *[_hbm(a) for a in lands]),
        in_specs=[_HBM_SPEC] * m, out_specs=(_SEM_SPEC, _SEM_SPEC, *[_HBM_SPEC] * m),
        input_output_aliases={i: 2 + i for i in range(m)},
        compiler_params=pltpu.CompilerParams(has_side_effects=_DATAFLOW),
    )(*lands)
    return outs[0], outs[1], list(outs[2:])


def _relay_wait(name, relayed, after):
    send_sems, recv_sems, lands = relayed
    m, nr = len(lands), len(RELAY_RELS)

    def body(*refs):
        land_refs, send_ref, recv_ref = refs[:m], refs[m], refs[m + 1]
        sibling, _ = _rel_peer(1)
        for i in range(m):
            for s, rel in enumerate(RELAY_RELS):
                _, sent = _rel_peer(rel)
                _, arriving = _rel_peer(rel ^ 1)
                copy = pltpu.make_async_remote_copy(
                    src_ref=land_refs[i].at[sent], dst_ref=land_refs[i].at[arriving],
                    send_sem=send_ref.at[i * nr + s], recv_sem=recv_ref.at[i * nr + s],
                    device_id=sibling, device_id_type=MESH)
                copy.wait_send()
                copy.wait_recv()

    outs = pl.pallas_call(
        body, name=name, out_shape=tuple(_hbm(a) for a in lands),
        in_specs=[_HBM_SPEC] * m + [_SEM_SPEC, _SEM_SPEC, pl.BlockSpec(memory_space=pl.ANY)],
        out_specs=tuple([_HBM_SPEC] * m), input_output_aliases={i: i for i in range(m)},
        compiler_params=pltpu.CompilerParams(has_side_effects=_DATAFLOW),
    )(*lands, send_sems, recv_sems, after)
    return list(outs)


def _ffn_fwd(tag, n, wg, wd):
    gu, act = _gate_up_act(f"ffn_gate_up_{tag}", n, wg)
    wd4 = wd.reshape(NFB, FB, D)
    f = _fwd_kblocked(f"ffn_down_{tag}", act, wd4)
    return (n, gu, act, wg, wd4), f


def _ffn_bwd(tag, dh_out, df, h_in, saved, g_pre, send, mixer):
    n, gu, act, wg, wd4 = saved
    tok = send(f"down_{tag}", _bwd_w_kblocked(f"ffn_down_dw_{tag}", act, df).reshape(NDEV, DFF // NDEV, D))
    dgu = _down_dx_act_bwd(f"ffn_down_dx_{tag}", df, wd4, gu, tok).reshape(NDEV, S, FB)
    tok = send(f"gate_up_{tag}", _bwd_w_cols_blocked(f"ffn_gate_up_dw_{tag}", n, dgu))
    dn = _bwd_x_cols_blocked(f"ffn_gate_up_dx_{tag}", dgu, wg, after=tok)
    dh_in, (dg_pre,), dy, dg_mixer = _rms_bwd(f"ffn_prenorm_bwd_{tag}", h_in, [(g_pre, dn)], dh_out, F32, then=mixer)
    return dh_in, dg_pre, dy, dg_mixer


def kernel(x, positions, mix_norm_pre, mix_norm_post, ffn_norm_pre, ffn_norm_post, ffn_w_gate_up, ffn_w_down, conv_w_in, conv_w, conv_w_out, kv_norm, w_kv, w_q, w_o, loss_target, m_mix_norm_pre, m_mix_norm_post, m_ffn_norm_pre, m_ffn_norm_post, m_ffn_w_gate_up, m_ffn_w_down, m_conv_w_in, m_conv_w, m_conv_w_out, m_kv_norm, m_w_kv, m_w_q, m_w_o, v_mix_norm_pre, v_mix_norm_post, v_ffn_norm_pre, v_ffn_norm_post, v_ffn_w_gate_up, v_ffn_w_down, v_conv_w_in, v_conv_w, v_conv_w_out, v_kv_norm, v_w_kv, v_w_q, v_w_o):
    me = 4 * lax.axis_index("x") + 2 * lax.axis_index("y") + lax.axis_index("c")
    h0 = x.reshape(S, D)
    target = loss_target.reshape(S, D)
    row = lambda a, l: a[l].reshape(1, D)
    g_kv = kv_norm.reshape(1, D)

    cw_shard = jnp.pad(conv_w[0], ((0, 5), (0, 0)))
    names = ["conv_in", "conv_w", "conv_out", "gate_up_0", "down_0", "kv", "q", "o", "gate_up_1", "down_1"]
    shards = [conv_w_in[0], cw_shard, conv_w_out[0], ffn_w_gate_up[0], ffn_w_down[0],
              w_kv, w_q[0], w_o[0], ffn_w_gate_up[1], ffn_w_down[1]]
    shards = [s if n == "conv_w" else s.astype(BF16) for n, s in zip(names, shards)]
    gather = _gather_start("gather_weights_start", shards, me)

    def direct(group, after):
        lands = _gather_wait(f"gather_wait_{group[0]}", gather, [names.index(n) for n in group], after)
        return _relay_start(f"relay_start_{group[0]}", lands)

    def finish(group, relayed, after):
        return dict(zip(group, _relay_wait(f"relay_wait_{group[0]}", relayed, after)))

    sent = {}

    def send(name, grad):
        sent[name], token = _exchange_start(f"scatter_start_{name}", [grad], me, "scatter")
        return token

    groups = [["conv_in", "conv_w", "conv_out"], ["gate_up_0", "down_0"], ["kv", "q"], ["o", "gate_up_1", "down_1"]]
    n0 = _rms_fwd("mix_prenorm_0", h0, [row(mix_norm_pre, 0)])[0]
    w = finish(groups[0], direct(groups[0], n0), n0)
    win = w["conv_in"].transpose(1, 0, 2).reshape(D, 3 * D)
    cw = w["conv_w"].transpose(1, 0, 2).reshape(8, D)
    wout = w["conv_out"].reshape(D, D)
    z = _fwd_rows("conv_in", n0, win, BF16)
    pre = _conv_fwd("conv_gate", z, cw)
    relayed = direct(groups[1], pre)
    y0 = _fwd_rows("conv_out", pre, wout)
    h1, (n1,) = _resid_rms("mix_postnorm_0", h0, y0, row(mix_norm_post, 0), [row(ffn_norm_pre, 0)])
    w = finish(groups[1], relayed, n1)
    ffn0, f0 = _ffn_fwd("0", n1, w["gate_up_0"], w["down_0"])
    relayed = direct(groups[2], ffn0[2])
    h2, (nk, n2) = _resid_rms("ffn_postnorm_0", h1, f0, row(ffn_norm_post, 0), [g_kv, row(mix_norm_pre, 1)])

    w = finish(groups[2], relayed, nk)
    wkv = w["kv"].transpose(1, 0, 2).reshape(D, 2 * QW)
    wq = w["q"].transpose(1, 0, 2).reshape(D, QW)
    half = HEAD_DIM // 2
    inv_freq = ROPE_THETA ** (-jnp.arange(half, dtype=F32) / half)
    tables = _rope_tables("rope_tables", positions.reshape(S, 1), jnp.tile(inv_freq, 4).reshape(1, 128))
    qc, kc, vc, o_c, lse_c = [], [], [], [], []
    for g, d in enumerate(DILATIONS):
        kc.append(_proj_classes(f"k_proj_{g}", nk, wkv, g, d, tables, 1.0))
        vc.append(_proj_classes(f"v_proj_{g}", nk, wkv, len(DILATIONS) + g, d, None, None))
    relayed = direct(groups[3], vc[-1])
    for g, d in enumerate(DILATIONS):
        qc.append(_proj_classes(f"q_proj_{g}", n2, wq, g, d, tables, HEAD_DIM ** -0.5))
        o_g, lse_g = _attn_fwd(f"attn_fwd_{g}", qc[g], kc[g], vc[g], d)
        o_c.append(o_g)
        lse_c.append(lse_g)
    o_mix = _mix_fwd("attn_mix", o_c, lse_c)
    w = finish(groups[3], relayed, o_mix)
    wo = w["o"].reshape(D, D)
    y1 = _fwd_rows("attn_out", o_mix, wo)
    h3, (n3,) = _resid_rms("mix_postnorm_1", h2, y1, row(mix_norm_post, 1), [row(ffn_norm_pre, 1)])
    ffn1, f1 = _ffn_fwd("1", n3, w["gate_up_1"], w["down_1"])

    dh4, df1, dg_fpost1, sq = _resid_rms_loss("ffn_postnorm_1_loss", h3, f1, row(ffn_norm_post, 1), target)

    dh3, dg_fpre1, dy1, dg_mpost1 = _ffn_bwd(
        "1", dh4, df1, h3, ffn1, row(ffn_norm_pre, 1), send, (y1, row(mix_norm_post, 1)))
    tok = send("o", _bwd_w_rows("attn_out_dw", o_mix, dy1).reshape(NDEV, D // NDEV, D))
    do = _bwd_x_rows("attn_out_dx", dy1, wo, F32, after=tok)
    lane = jnp.arange(128)
    ones_blockdiag = (lane[:, None] // HEAD_DIM == lane[None, :] // HEAD_DIM).astype(BF16)
    mixed = _mix_bwd("attn_mix_bwd", do, o_c, lse_c, ones_blockdiag)
    branch_grads = [_attn_bwd(f"attn_bwd_{g}", qc[g], kc[g], vc[g], mixed[g], lse_c[g], mixed[3 + g], d)
                    for g, d in enumerate(DILATIONS)]
    dq_raw, dkv = _attn_bwd_post("attn_bwd_post", branch_grads, *tables)
    tok = send("kv", _bwd_w_cols("kv_proj_dw", nk, dkv, 2 * QW // NDEV))
    dnk = _bwd_x_plain("kv_proj_dx", dkv, wkv, after=tok)
    tok = send("q", _bwd_w_cols("q_proj_dw", n2, dq_raw, QW // NDEV))
    dn2 = _bwd_x_plain("q_proj_dx", dq_raw, wq, after=tok)
    dh2, (dg_kv, dg_mpre1), df0, dg_fpost0 = _rms_bwd(
        "kv_and_mix_prenorm_bwd_1", h2, [(g_kv, dnk), (row(mix_norm_pre, 1), dn2)], dh3, F32,
        then=(f0, row(ffn_norm_post, 0)))

    dh1, dg_fpre0, dy0, dg_mpost0 = _ffn_bwd(
        "0", dh2, df0, h1, ffn0, row(ffn_norm_pre, 0), send, (y0, row(mix_norm_post, 0)))
    tok = send("conv_out", _bwd_w_rows("conv_out_dw", pre, dy0).reshape(NDEV, D // NDEV, D))
    dpre = _bwd_x_rows("conv_out_dx", dy0, wout, BF16, after=tok)
    dz, dcw = _conv_bwd("conv_gate_bwd", z, dpre, cw)
    tok = send("conv_in", _bwd_w_cols("conv_in_dw", n0, dz, 3 * D // NDEV))
    dn0 = _bwd_x_plain("conv_in_dx", dz, win, after=tok)
    dh0, (dg_mpre0,) = _rms_bwd("mix_prenorm_bwd_0", h0, [(row(mix_norm_pre, 0), dn0)], dh1, F32)

    small = _pack_small("pack_small_grads", [dg_mpre0, dg_mpre1, dg_mpost0, dg_mpost1, dg_fpre0, dg_fpre1,
                                             dg_fpost0, dg_fpost1, dg_kv], dcw, sq)
    small_all = _exchange("gather_small_grads", [small], "gather")[0]

    done = [small_all]

    def upd(tag, w, m, v):
        parts = _exchange_wait(f"scatter_wait_{tag}", sent[tag], 0, done[-1], "scatter")
        shape = w.shape
        flat = lambda a: a.reshape(parts.shape[1:])
        res = _adamw(f"adamw_{tag}", parts, flat(w), flat(m), flat(v))
        done.append(res[0])
        return [r.reshape(shape) for r in res]

    def upd_layer(tag, l, w, m, v):
        return upd(f"{tag}_{l}", w[l], m[l], v[l])

    def stack(per_layer):
        return [jnp.stack([per_layer[0][i], per_layer[1][i]]) for i in range(4)]

    vec = lambda a: a.reshape(1, D)
    gain_res, taps, loss = _adamw_gains("adamw_gains", small_all, [
        (mix_norm_pre, m_mix_norm_pre, v_mix_norm_pre), (mix_norm_post, m_mix_norm_post, v_mix_norm_post),
        (ffn_norm_pre, m_ffn_norm_pre, v_ffn_norm_pre), (ffn_norm_post, m_ffn_norm_post, v_ffn_norm_post),
        (vec(kv_norm), vec(m_kv_norm), vec(v_kv_norm))])
    dcw_mine = lax.dynamic_slice(taps, (0, me * 128), (8, 128))
    pad8 = lambda a, fill: jnp.pad(a[0], ((0, 5), (0, 0)), constant_values=fill)
    cw_res = [r[0:3].reshape(1, 3, 128) for r in
              _adamw("adamw_conv_w", dcw_mine.reshape(1, 8, 128), cw_shard, pad8(m_conv_w, 0.0), pad8(v_conv_w, 1.0))]

    res = {
        "mix_norm_pre": gain_res[0],
        "mix_norm_post": gain_res[1],
        "ffn_norm_pre": gain_res[2],
        "ffn_norm_post": gain_res[3],
        "kv_norm": [r.reshape(D) for r in gain_res[4]],
        "conv_w": cw_res,
    }
    down, gate_up = {}, {}
    down[1] = upd_layer("down", 1, ffn_w_down, m_ffn_w_down, v_ffn_w_down)
    gate_up[1] = upd_layer("gate_up", 1, ffn_w_gate_up, m_ffn_w_gate_up, v_ffn_w_gate_up)
    res["w_o"] = upd("o", w_o, m_w_o, v_w_o)
    res["w_q"] = upd("q", w_q, m_w_q, v_w_q)
    res["w_kv"] = upd("kv", w_kv, m_w_kv, v_w_kv)
    down[0] = upd_layer("down", 0, ffn_w_down, m_ffn_w_down, v_ffn_w_down)
    gate_up[0] = upd_layer("gate_up", 0, ffn_w_gate_up, m_ffn_w_gate_up, v_ffn_w_gate_up)
    res["ffn_w_down"] = stack(down)
    res["ffn_w_gate_up"] = stack(gate_up)
    res["conv_w_out"] = upd("conv_out", conv_w_out, m_conv_w_out, v_conv_w_out)
    res["conv_w_in"] = upd("conv_in", conv_w_in, m_conv_w_in, v_conv_w_in)
    order = ["mix_norm_pre", "mix_norm_post", "ffn_norm_pre", "ffn_norm_post", "ffn_w_gate_up", "ffn_w_down",
             "conv_w_in", "conv_w", "conv_w_out", "kv_norm", "w_kv", "w_q", "w_o"]
    out = [loss, dh0.reshape(1, S, D)]
    for i in range(4):
        out += [res[name][i] for name in order]
    return tuple(out)
```

```python
import jax
import jax.numpy as jnp
from jax import lax
from jax.experimental import pallas as pl
from jax.experimental.pallas import tpu as pltpu

F32 = jnp.float32
BF16 = jnp.bfloat16

S = 4096
D = 1024
NDEV = 8
HEAD_DIM = 64
QW = 3072
DFF = 2816
FB = 704
NFB = 4
BRANCHES = ((128, 1), (512, 4), (2048, 16))
BAND = 128
ROPE_THETA = 10000.0
RMS_EPS = 1e-6
NEG_INF = -1e30
ADAM_LR, ADAM_B1, ADAM_B2, ADAM_EPS, ADAM_WD, ADAM_STEP = 0.001, 0.9, 0.999, 1e-08, 0.01, 10

VMEM_LIMIT_BYTES = 52 * 1024 * 1024
ROW_TILE = 512
MESH = pl.DeviceIdType.MESH


def _cparams(ngrid):
    return pltpu.CompilerParams(dimension_semantics=("arbitrary",) * ngrid,
                                vmem_limit_bytes=VMEM_LIMIT_BYTES)


def _sds(shape, dtype):
    return jax.ShapeDtypeStruct(tuple(shape), dtype)


_DIMS = {"nn": (((1,), (0,)), ((), ())),
         "nt": (((1,), (1,)), ((), ())),
         "tn": (((0,), (0,)), ((), ()))}


def _matmul(name, a, b, *, mode, grid, a_blk, a_map, b_blk, b_map, o_shape, o_blk, o_map, out_dtype, after=None,
            out_groups=1):
    nk = grid[2]
    dims = _DIMS[mode]
    acc_shape = tuple(s for s in o_blk if s is not None)
    if out_groups > 1:
        acc_shape = (acc_shape[1], out_groups * acc_shape[2])
    extra = [] if after is None else [after]

    def store(o_ref, val):
        if out_groups == 1:
            o_ref[...] = val.astype(o_ref.dtype)
        else:
            n = o_ref.shape[-1]
            for grp in range(out_groups):
                o_ref[grp] = val[:, grp * n:(grp + 1) * n].astype(o_ref.dtype)

    def body(a_ref, b_ref, *rest):
        o_ref, scratch = rest[len(extra)], rest[len(extra) + 1:]
        part = lax.dot_general(a_ref[...], b_ref[...], dims, preferred_element_type=F32)
        if nk == 1:
            store(o_ref, part)
            return
        acc_ref = scratch[0]
        k = pl.program_id(2)

        @pl.when(k == 0)
        def _():
            acc_ref[...] = part

        @pl.when(k > 0)
        def _():
            acc_ref[...] += part

        @pl.when(k == nk - 1)
        def _():
            store(o_ref, acc_ref[...])

    return pl.pallas_call(
        body, name=name, grid=grid,
        in_specs=[pl.BlockSpec(a_blk, a_map), pl.BlockSpec(b_blk, b_map)] + [pl.BlockSpec(memory_space=pl.ANY)] * len(extra),
        out_specs=pl.BlockSpec(o_blk, o_map),
        out_shape=_sds(o_shape, out_dtype),
        scratch_shapes=[] if nk == 1 else [pltpu.VMEM(acc_shape, F32)],
        compiler_params=_cparams(3),
    )(a, b, *extra)


TM = 1024
TK = S


def _fwd_rows(name, a, w, out_dtype=F32):
    kdim, n = w.shape
    tn = 512
    return _matmul(name, a, w, mode="nn", grid=(S // TM, n // tn, 1),
                   a_blk=(TM, kdim), a_map=lambda i, j, k: (i, 0),
                   b_blk=(kdim, tn), b_map=lambda i, j, k: (0, j),
                   o_shape=(S, n), o_blk=(TM, tn), o_map=lambda i, j, k: (i, j), out_dtype=out_dtype)


def _fwd_kblocked(name, a4, w4):
    nb, _, kb = a4.shape
    n = w4.shape[2]

    def body(a_ref, w_ref, o_ref):
        acc = _dot_nn(a_ref[0], w_ref[0])
        for j in range(1, nb):
            acc = acc + _dot_nn(a_ref[j], w_ref[j])
        o_ref[...] = acc

    return pl.pallas_call(
        body, name=name, grid=(S // TM,),
        in_specs=[pl.BlockSpec((nb, TM, kb), lambda i: (0, i, 0)), pl.BlockSpec((nb, kb, n), lambda i: (0, 0, 0))],
        out_specs=pl.BlockSpec((TM, n), lambda i: (i, 0)), out_shape=_sds((S, n), F32),
        compiler_params=_cparams(1),
    )(a4, w4)


def _bwd_x_cols_blocked(name, dy8, wg, after):
    _, kdim, n = wg.shape
    nk = NDEV // 2

    def body(a_ref, b_ref, after_ref, o_ref, acc_ref):
        k = pl.program_id(1)
        part = _dot_nt(a_ref[0], b_ref[0]) + _dot_nt(a_ref[1], b_ref[1])

        @pl.when(k == 0)
        def _():
            acc_ref[...] = part

        @pl.when(k > 0)
        def _():
            acc_ref[...] += part

        @pl.when(k == nk - 1)
        def _():
            o_ref[...] = acc_ref[...]

    return pl.pallas_call(
        body, name=name, grid=(S // TM, nk),
        in_specs=[pl.BlockSpec((2, None, TM, n), lambda i, k: (0, k, i, 0)),
                  pl.BlockSpec((2, None, kdim, n), lambda i, k: (0, k, 0, 0)),
                  pl.BlockSpec(memory_space=pl.ANY)],
        out_specs=pl.BlockSpec((TM, kdim), lambda i, k: (i, 0)), out_shape=_sds((S, kdim), F32),
        scratch_shapes=[pltpu.VMEM((TM, kdim), F32)],
        compiler_params=_cparams(2),
    )(dy8.reshape(2, nk, S, n), wg.reshape(2, nk, kdim, n), after)


def _bwd_x_rows(name, dy, w, out_dtype, after=None):
    kdim, n = w.shape
    tkk = 512
    return _matmul(name, dy, w, mode="nt", grid=(S // TM, kdim // tkk, 1),
                   a_blk=(TM, n), a_map=lambda i, j, k: (i, 0),
                   b_blk=(tkk, n), b_map=lambda i, j, k: (j, 0),
                   o_shape=(S, kdim), o_blk=(TM, tkk), o_map=lambda i, j, k: (i, j), out_dtype=out_dtype, after=after)


DW_COLS = 768


def _bwd_w_cols(name, a, dy, n):
    kdim = a.shape[1]
    groups = DW_COLS // n
    return _matmul(name, a, dy, mode="tn", grid=(1, NDEV // groups, S // TK),
                   a_blk=(TK, kdim), a_map=lambda i, j, k: (k, 0),
                   b_blk=(TK, DW_COLS), b_map=lambda i, j, k: (k, j),
                   o_shape=(NDEV, kdim, n), o_blk=(groups, kdim, n) if groups > 1 else (None, kdim, n),
                   o_map=lambda i, j, k: (j, 0, 0), out_dtype=BF16, out_groups=groups)


def _bwd_x_plain(name, dy, w, after=None):
    kdim, n = w.shape
    tm = TM if n <= 3 * D else TM // 2
    return _matmul(name, dy, w, mode="nt", grid=(S // tm, 1, 1),
                   a_blk=(tm, n), a_map=lambda i, j, k: (i, 0),
                   b_blk=(kdim, n), b_map=lambda i, j, k: (0, 0),
                   o_shape=(S, kdim), o_blk=(tm, kdim), o_map=lambda i, j, k: (i, 0), out_dtype=F32, after=after)


def _bwd_w_cols_blocked(name, a, dy8):
    kdim = a.shape[1]
    n = dy8.shape[2]
    return _matmul(name, a, dy8, mode="tn", grid=(1, NDEV, S // TK),
                   a_blk=(TK, kdim), a_map=lambda i, j, k: (k, 0),
                   b_blk=(None, TK, n), b_map=lambda i, j, k: (j, k, 0),
                   o_shape=(NDEV, kdim, n), o_blk=(None, kdim, n), o_map=lambda i, j, k: (j, 0, 0), out_dtype=BF16)


def _bwd_w_rows(name, a, dy):
    kdim = a.shape[1]
    n = dy.shape[1]
    tmm = 512
    return _matmul(name, a, dy, mode="tn", grid=(kdim // tmm, 1, S // TK),
                   a_blk=(TK, tmm), a_map=lambda i, j, k: (k, i),
                   b_blk=(TK, n), b_map=lambda i, j, k: (k, 0),
                   o_shape=(kdim, n), o_blk=(tmm, n), o_map=lambda i, j, k: (i, 0), out_dtype=BF16)


def _bwd_w_kblocked(name, a4, dy):
    nb, _, kb = a4.shape
    n = dy.shape[1]
    return _matmul(name, a4, dy, mode="tn", grid=(nb, 1, S // TK),
                   a_blk=(None, TK, kb), a_map=lambda i, j, k: (i, k, 0),
                   b_blk=(TK, n), b_map=lambda i, j, k: (k, 0),
                   o_shape=(nb, kb, n), o_blk=(None, kb, n), o_map=lambda i, j, k: (i, 0, 0), out_dtype=BF16)


def _rstd(x):
    return lax.rsqrt(jnp.mean(x * x, axis=-1, keepdims=True) + RMS_EPS)


def _row_spec(tm=ROW_TILE, width=D):
    return pl.BlockSpec((tm, width), lambda i: (i, 0))


def _vec_spec(rows=1, width=D):
    return pl.BlockSpec((rows, width), lambda i: (0, 0))


def _rms_fwd(name, x, gains):
    n = len(gains)

    def body(x_ref, *refs):
        x_val = x_ref[...]
        xh = x_val * _rstd(x_val)
        for g_ref, o_ref in zip(refs[:n], refs[n:]):
            o_ref[...] = (xh * g_ref[...]).astype(o_ref.dtype)

    outs = pl.pallas_call(
        body, name=name, grid=(S // ROW_TILE,),
        in_specs=[_row_spec()] + [_vec_spec()] * n,
        out_specs=[_row_spec()] * n,
        out_shape=[_sds((S, D), BF16)] * n,
        compiler_params=_cparams(1),
    )(x, *gains)
    return list(outs)


def _resid_rms(name, h, y, g, next_gains):
    n = len(next_gains)

    def body(h_ref, y_ref, g_ref, *refs):
        y_val = y_ref[...]
        h_new = h_ref[...] + (y_val * _rstd(y_val)) * g_ref[...]
        refs[n][...] = h_new
        hh = h_new * _rstd(h_new)
        for g2_ref, o_ref in zip(refs[:n], refs[n + 1:]):
            o_ref[...] = (hh * g2_ref[...]).astype(o_ref.dtype)

    outs = pl.pallas_call(
        body, name=name, grid=(S // ROW_TILE,),
        in_specs=[_row_spec(), _row_spec(), _vec_spec()] + [_vec_spec()] * n,
        out_specs=[_row_spec()] * (n + 1), out_shape=[_sds((S, D), F32)] + [_sds((S, D), BF16)] * n,
        compiler_params=_cparams(1),
    )(h, y, g, *next_gains)
    return outs[0], list(outs[1:])


def _resid_rms_loss(name, h, y, g, target):
    def body(h_ref, y_ref, g_ref, t_ref, dh_ref, dy_ref, dg_ref, part_ref):
        y_val = y_ref[...]
        gain = g_ref[...]
        e = h_ref[...] + (y_val * _rstd(y_val)) * gain - t_ref[...]
        dh = e * (1.0 / D)
        dh_ref[...] = dh
        step = pl.program_id(0)
        dy_ref[...] = _norm_bwd_rows(y_val, gain, dh, dg_ref, step).astype(dy_ref.dtype)
        part = jnp.sum(e * e, axis=0, keepdims=True)

        @pl.when(step == 0)
        def _():
            part_ref[...] = part

        @pl.when(step > 0)
        def _():
            part_ref[...] += part

    return pl.pallas_call(
        body, name=name, grid=(S // ROW_TILE,),
        in_specs=[_row_spec(), _row_spec(), _vec_spec(), _row_spec()],
        out_specs=[_row_spec(), _row_spec(), _vec_spec(8), _vec_spec()],
        out_shape=[_sds((S, D), F32), _sds((S, D), BF16), _sds((8, D), F32), _sds((1, D), F32)],
        compiler_params=_cparams(1),
    )(h, y, g, target)


def _norm_bwd_rows(x_val, g, dn, dg_ref, step):
    r = _rstd(x_val)
    xh = x_val * r
    dxh = dn * g
    part = jnp.sum(dn * xh, axis=0, keepdims=True)

    @pl.when(step == 0)
    def _():
        dg_ref[...] = jnp.zeros_like(dg_ref)

    dg_ref[0:1, :] += part
    return r * (dxh - xh * jnp.mean(dxh * xh, axis=-1, keepdims=True))


def _rms_bwd(name, x, pairs, dres, out_dtype, then=None):
    n = len(pairs)
    has_res = dres is not None
    chained = then is not None

    def body(x_ref, *refs):
        g_refs = refs[0:2 * n:2]
        dn_refs = refs[1:2 * n:2]
        pos = 2 * n
        res_ref = refs[pos] if has_res else None
        pos += int(has_res)
        if chained:
            y_ref, gy_ref = refs[pos], refs[pos + 1]
            pos += 2
        dx_ref = refs[pos]
        dg_refs = refs[pos + 1:pos + 1 + n]
        step = pl.program_id(0)
        x_val = x_ref[...]
        acc = res_ref[...] if has_res else jnp.zeros_like(x_val)
        for g_ref, dn_ref, dg_ref in zip(g_refs, dn_refs, dg_refs):
            acc = acc + _norm_bwd_rows(x_val, g_ref[...], dn_ref[...].astype(F32), dg_ref, step)
        dx_ref[...] = acc.astype(dx_ref.dtype)
        if chained:
            dy_ref, dgy_ref = refs[pos + 1 + n], refs[pos + 2 + n]
            dy_ref[...] = _norm_bwd_rows(y_ref[...], gy_ref[...], acc, dgy_ref, step).astype(dy_ref.dtype)

    operands = [x]
    in_specs = [_row_spec()]
    for g, dn in pairs:
        operands += [g, dn]
        in_specs += [_vec_spec(), _row_spec()]
    if has_res:
        operands.append(dres)
        in_specs.append(_row_spec())
    if chained:
        operands += [then[0], then[1]]
        in_specs += [_row_spec(), _vec_spec()]
    extra = int(chained)
    outs = pl.pallas_call(
        body, name=name, grid=(S // ROW_TILE,),
        in_specs=in_specs,
        out_specs=[_row_spec()] + [_vec_spec(8)] * n + [_row_spec(), _vec_spec(8)] * extra,
        out_shape=[_sds((S, D), out_dtype)] + [_sds((8, D), F32)] * n + [_sds((S, D), BF16), _sds((8, D), F32)] * extra,
        compiler_params=_cparams(1),
    )(*operands)
    if chained:
        return outs[0], list(outs[1:1 + n]), outs[1 + n], outs[2 + n]
    return outs[0], list(outs[1:])


def _shift_down(u, prev8, k):
    r = pltpu.roll(u, k, 0)
    p = pltpu.roll(prev8, k, 0)
    row = lax.broadcasted_iota(jnp.int32, prev8.shape, 0)
    top = jnp.where(row < k, p, r[0:8])
    return jnp.concatenate([top, r[8:]], axis=0)


def _shift_up(u, next8, k):
    tm = u.shape[0]
    r = pltpu.roll(u, tm - k, 0)
    p = pltpu.roll(next8, 8 - k, 0)
    row = lax.broadcasted_iota(jnp.int32, next8.shape, 0)
    bot = jnp.where(row >= 8 - k, p, r[tm - 8:tm])
    return jnp.concatenate([r[:tm - 8], bot], axis=0)


CONV_TILE = 512


def _halo_prev(col):
    return pl.BlockSpec((8, D), lambda i: (jnp.maximum(i * (CONV_TILE // 8) - 1, 0), col))


def _halo_next(col):
    last = S // 8 - 1
    return pl.BlockSpec((8, D), lambda i: (jnp.minimum((i + 1) * (CONV_TILE // 8), last), col))


def _conv_fwd(name, z, cw):
    def body(b_ref, c_ref, h_ref, cp_ref, hp_ref, cw_ref, o_ref):
        i = pl.program_id(0)
        u = c_ref[...].astype(F32) * h_ref[...].astype(F32)
        up = cp_ref[...].astype(F32) * hp_ref[...].astype(F32)
        up = jnp.where(i > 0, up, 0.0)
        cv = cw_ref[0:1, :] * _shift_down(u, up, 2) + cw_ref[1:2, :] * _shift_down(u, up, 1) + cw_ref[2:3, :] * u
        o_ref[...] = (b_ref[...].astype(F32) * cv).astype(o_ref.dtype)

    col = lambda c: pl.BlockSpec((CONV_TILE, D), lambda i: (i, c))
    return pl.pallas_call(
        body, name=name, grid=(S // CONV_TILE,),
        in_specs=[col(0), col(1), col(2), _halo_prev(1), _halo_prev(2), _vec_spec(8)],
        out_specs=_row_spec(CONV_TILE), out_shape=_sds((S, D), BF16),
        compiler_params=_cparams(1),
    )(z, z, z, z, z, cw)


def _conv_bwd(name, z, dpre, cw):
    nsteps = S // CONV_TILE

    def body(b_ref, c_ref, h_ref, cp_ref, hp_ref, dp_ref, dpn_ref, bn_ref, cw_ref, dz_ref, dcw_ref):
        i = pl.program_id(0)
        b = b_ref[...].astype(F32)
        c = c_ref[...].astype(F32)
        h = h_ref[...].astype(F32)
        dp = dp_ref[...].astype(F32)
        u = c * h
        up = jnp.where(i > 0, cp_ref[...].astype(F32) * hp_ref[...].astype(F32), 0.0)
        s1 = _shift_down(u, up, 1)
        s2 = _shift_down(u, up, 2)
        w0, w1, w2 = cw_ref[0:1, :], cw_ref[1:2, :], cw_ref[2:3, :]
        cv = w0 * s2 + w1 * s1 + w2 * u
        dcv = dp * b
        dcvn = jnp.where(i < nsteps - 1, dpn_ref[...].astype(F32) * bn_ref[...].astype(F32), 0.0)
        du = w2 * dcv + w1 * _shift_up(dcv, dcvn, 1) + w0 * _shift_up(dcv, dcvn, 2)
        dz_ref[:, 0:D] = (dp * cv).astype(dz_ref.dtype)
        dz_ref[:, D:2 * D] = (du * h).astype(dz_ref.dtype)
        dz_ref[:, 2 * D:3 * D] = (du * c).astype(dz_ref.dtype)

        @pl.when(i == 0)
        def _():
            dcw_ref[...] = jnp.zeros_like(dcw_ref)

        dcw_ref[0:1, :] += jnp.sum(dcv * s2, axis=0, keepdims=True)
        dcw_ref[1:2, :] += jnp.sum(dcv * s1, axis=0, keepdims=True)
        dcw_ref[2:3, :] += jnp.sum(dcv * u, axis=0, keepdims=True)

    col = lambda c: pl.BlockSpec((CONV_TILE, D), lambda i: (i, c))
    return pl.pallas_call(
        body, name=name, grid=(nsteps,),
        in_specs=[col(0), col(1), col(2), _halo_prev(1), _halo_prev(2),
                  _row_spec(CONV_TILE), _halo_next(0), _halo_next(0), _vec_spec(8)],
        out_specs=[pl.BlockSpec((CONV_TILE, 3 * D), lambda i: (i, 0)), _vec_spec(8)],
        out_shape=[_sds((S, 3 * D), BF16), _sds((8, D), F32)],
        compiler_params=_cparams(1),
    )(z, z, z, z, z, dpre, dpre, z, cw)


FFN_TM = 2048
_GU_BLOCK = pl.BlockSpec((2, None, FFN_TM, FB), lambda i, j: (0, j, i, 0))


def _gate_up_act(name, a, wg):
    kdim = a.shape[1]

    def body(a_ref, wgate_ref, wup_ref, gu_ref, act_ref):
        x = a_ref[...]
        g = _dot_nn(x, wgate_ref[...])
        u = _dot_nn(x, wup_ref[...])
        gu_ref[0] = g.astype(gu_ref.dtype)
        gu_ref[1] = u.astype(gu_ref.dtype)
        act_ref[...] = (g * jax.nn.sigmoid(g) * u).astype(act_ref.dtype)

    return pl.pallas_call(
        body, name=name, grid=(S // FFN_TM, NFB),
        in_specs=[pl.BlockSpec((FFN_TM, kdim), lambda i, j: (i, 0)),
                  pl.BlockSpec((None, kdim, FB), lambda i, j: (j, 0, 0)),
                  pl.BlockSpec((None, kdim, FB), lambda i, j: (j + NFB, 0, 0))],
        out_specs=[_GU_BLOCK, pl.BlockSpec((None, FFN_TM, FB), lambda i, j: (j, i, 0))],
        out_shape=[_sds((2, NFB, S, FB), BF16), _sds((NFB, S, FB), BF16)],
        compiler_params=_cparams(2),
    )(a, wg, wg)


def _down_dx_act_bwd(name, df, w4, gu, after):
    _, kb, n = w4.shape

    def body(df_ref, w_ref, gu_ref, after_ref, o_ref):
        d = _dot_nt(df_ref[...], w_ref[...])
        g = gu_ref[0].astype(F32)
        u = gu_ref[1].astype(F32)
        sg = jax.nn.sigmoid(g)
        o_ref[0] = (d * u * sg * (1.0 + g * (1.0 - sg))).astype(o_ref.dtype)
        o_ref[1] = (d * g * sg).astype(o_ref.dtype)

    return pl.pallas_call(
        body, name=name, grid=(S // FFN_TM, NFB),
        in_specs=[pl.BlockSpec((FFN_TM, n), lambda i, j: (i, 0)), pl.BlockSpec((None, kb, n), lambda i, j: (j, 0, 0)),
                  _GU_BLOCK, pl.BlockSpec(memory_space=pl.ANY)],
        out_specs=_GU_BLOCK, out_shape=_sds((2, NFB, S, FB), BF16),
        compiler_params=_cparams(2),
    )(df, w4, gu, after)


def _rope_tables(name, pos_col, inv_freq_row):
    def body(pos_ref, f_ref, cos_ref, sin_ref):
        ang = pos_ref[...].astype(F32) * f_ref[...]
        lane = lax.broadcasted_iota(jnp.int32, ang.shape, 1)
        s = jnp.sin(ang)
        cos_ref[...] = jnp.cos(ang)
        sin_ref[...] = jnp.where((lane % HEAD_DIM) < HEAD_DIM // 2, -s, s)

    tab = pl.BlockSpec((ROW_TILE, 128), lambda i: (i, 0))
    return pl.pallas_call(
        body, name=name, grid=(S // ROW_TILE,),
        in_specs=[pl.BlockSpec((ROW_TILE, 1), lambda i: (i, 0)), _vec_spec(1, 128)],
        out_specs=[tab, tab], out_shape=[_sds((S, 128), F32)] * 2,
        compiler_params=_cparams(1),
    )(pos_col, inv_freq_row)


def _swap_halves(t):
    lane = lax.broadcasted_iota(jnp.int32, t.shape, 1)
    first = (lane % HEAD_DIM) < HEAD_DIM // 2
    return jnp.where(first, pltpu.roll(t, 128 - HEAD_DIM // 2, 1), pltpu.roll(t, HEAD_DIM // 2, 1))


NCHUNK = D // 128


def _chunk(c, base=0):
    return slice(base + c * 128, base + (c + 1) * 128)


def _class_rows(r, d, tm):
    return pl.ds(r, tm // d, stride=d) if d > 1 else slice(None)


def _class_block(d, tm):
    return pl.BlockSpec((tm // d, d * D), lambda i: (i, 0))


def _tokens_from_classes(blk_ref, tmp_ref, d, tm):
    for r in range(d):
        for c in range(NCHUNK):
            tmp_ref[c, _class_rows(r, d, tm), :] = blk_ref[:, _chunk(c, r * D)].astype(F32)


def _classes_from_tokens(tmp_ref, blk_ref, d, tm):
    for r in range(d):
        for c in range(NCHUNK):
            blk_ref[:, _chunk(c, r * D)] = tmp_ref[c, _class_rows(r, d, tm), :].astype(blk_ref.dtype)


def _proj_classes(name, a, w, col, d, tables, scale):
    kdim = a.shape[1]
    rope = tables is not None

    def body(a_ref, w_ref, *refs):
        if rope:
            cos_ref, sin_ref, o_ref, tmp_ref = refs
        else:
            o_ref, tmp_ref = refs
        acc = _dot_nn(a_ref[...], w_ref[...])
        for c in range(NCHUNK):
            tmp_ref[c] = acc[:, _chunk(c)]
        for r in range(d):
            rows = _class_rows(r, d, TM)
            if rope:
                cs = cos_ref[rows, :]
                sn = sin_ref[rows, :]
            for c in range(NCHUNK):
                x = tmp_ref[c, rows, :]
                if rope:
                    x = (x * cs + _swap_halves(x) * sn) * scale
                o_ref[:, _chunk(c, r * D)] = x.astype(o_ref.dtype)

    tab = pl.BlockSpec((TM, 128), lambda i: (i, 0))
    return pl.pallas_call(
        body, name=name, grid=(S // TM,),
        in_specs=[pl.BlockSpec((TM, kdim), lambda i: (i, 0)), pl.BlockSpec((kdim, D), lambda i: (0, col))]
                 + ([tab, tab] if rope else []),
        out_specs=_class_block(d, TM), out_shape=_sds((S // d, d * D), BF16),
        scratch_shapes=[pltpu.VMEM((NCHUNK, TM, 128), F32)],
        compiler_params=_cparams(1),
    )(a, w, *(tables if rope else ()))


ATTN_CHAINS = 8


def _attn_units(d):
    nblk = S // d // BAND
    return max(1, 2 * ATTN_CHAINS // nblk)


def _class_spec(d):
    return pl.BlockSpec((S // d, 128 * _attn_units(d)), lambda cb: (0, cb))


def _dot_nt(a, b):
    return lax.dot_general(a, b, _DIMS["nt"], preferred_element_type=F32)


def _dot_tn(a, b):
    return lax.dot_general(a, b, _DIMS["tn"], preferred_element_type=F32)


def _dot_nn(a, b):
    return lax.dot_general(a, b, _DIMS["nn"], preferred_element_type=F32)


def _band_mask(nkeys):
    qi = lax.broadcasted_iota(jnp.int32, (2 * BAND, nkeys), 0) % BAND
    kj = lax.broadcasted_iota(jnp.int32, (2 * BAND, nkeys), 1)
    if nkeys == BAND:
        return kj <= qi
    dist = qi + BAND - kj
    return (dist >= 0) & (dist <= BAND)


def _stack_heads(x):
    row = lax.broadcasted_iota(jnp.int32, (2 * BAND, 128), 0)
    lane = lax.broadcasted_iota(jnp.int32, (2 * BAND, 128), 1)
    keep = (row < BAND) == (lane < HEAD_DIM)
    return jnp.where(keep, jnp.concatenate([x, x], axis=0), jnp.zeros((), x.dtype))


def _unstack(x2):
    first_head = lax.broadcasted_iota(jnp.int32, (BAND, 128), 1) < HEAD_DIM
    return jnp.where(first_head, x2[:BAND], x2[BAND:])


def _for_later_blocks(nblk, units, fn):
    all_lanes = [slice(u * 128, (u + 1) * 128) for u in range(units)]
    unroll = max(1, ATTN_CHAINS // units)
    trips = (nblk - 1) // unroll
    if trips > 1:
        def step(i, carry):
            for j in range(unroll):
                for lanes in all_lanes:
                    fn(pl.multiple_of((1 + i * unroll + j) * BAND, BAND), lanes)
            return carry

        lax.fori_loop(0, trips, step, 0)
    else:
        trips = 0
    for sb in range(1 + trips * unroll, nblk):
        for lanes in all_lanes:
            fn(sb * BAND, lanes)


def _attn_fwd(name, q, k, v, d):
    nblk = S // d // BAND
    units = _attn_units(d)

    def body(q_ref, k_ref, v_ref, o_ref, lse_ref):
        def block(r0, k0, nkeys, lanes):
            q2 = _stack_heads(q_ref[pl.ds(r0, BAND), lanes])
            s = jnp.where(_band_mask(nkeys), _dot_nt(q2, k_ref[pl.ds(k0, nkeys), lanes]), NEG_INF)
            m = jnp.max(s, axis=-1, keepdims=True)
            p = jnp.exp(s - m)
            l = jnp.sum(p, axis=-1, keepdims=True)
            o2 = _dot_nn(p.astype(BF16), v_ref[pl.ds(k0, nkeys), lanes]) / l
            lse2 = jnp.broadcast_to(m + jnp.log(l), (2 * BAND, 128))
            o_ref[pl.ds(r0, BAND), lanes] = _unstack(o2).astype(o_ref.dtype)
            lse_ref[pl.ds(r0, BAND), lanes] = _unstack(lse2)

        for u in range(units):
            block(0, 0, BAND, slice(u * 128, (u + 1) * 128))

        _for_later_blocks(nblk, units, lambda r0, lanes: block(r0, r0 - BAND, 2 * BAND, lanes))

    spec = _class_spec(d)
    return pl.pallas_call(
        body, name=name, grid=(8 * d // units,),
        in_specs=[spec] * 3, out_specs=[spec] * 2,
        out_shape=[_sds((S // d, d * D), BF16), _sds((S // d, d * D), F32)],
        compiler_params=_cparams(1),
    )(q, k, v)


def _attn_bwd(name, q, k, v, do, lse, dd, d):
    nblk = S // d // BAND
    units = _attn_units(d)

    def body(q_ref, k_ref, v_ref, do_ref, lse_ref, dd_ref, dq_ref, dk_ref, dv_ref):
        def column(ref, r0, lanes, nkeys):
            tile = ref[pl.ds(r0, BAND), lanes]
            other = pltpu.roll(tile, HEAD_DIM, 1)
            first_head = lax.broadcasted_iota(jnp.int32, tile.shape, 1) < HEAD_DIM
            both = jnp.concatenate([jnp.where(first_head, tile, other), jnp.where(first_head, other, tile)], axis=0)
            return both if nkeys == BAND else jnp.concatenate([both, both], axis=1)

        def block(r0, k0, nkeys, lanes, first):
            q2 = _stack_heads(q_ref[pl.ds(r0, BAND), lanes])
            do2 = _stack_heads(do_ref[pl.ds(r0, BAND), lanes])
            kk = k_ref[pl.ds(k0, nkeys), lanes]
            vv = v_ref[pl.ds(k0, nkeys), lanes]
            s = jnp.where(_band_mask(nkeys), _dot_nt(q2, kk), NEG_INF)
            p = jnp.exp(s - column(lse_ref, r0, lanes, nkeys))
            ds = (p * (_dot_nt(do2, vv) - column(dd_ref, r0, lanes, nkeys))).astype(BF16)
            dq_ref[pl.ds(r0, BAND), lanes] = _unstack(_dot_nn(ds, kk)).astype(dq_ref.dtype)
            dk_part = _dot_tn(ds, q2)
            dv_part = _dot_tn(p.astype(BF16), do2)
            if first:
                dk_ref[pl.ds(k0, nkeys), lanes] = dk_part
                dv_ref[pl.ds(k0, nkeys), lanes] = dv_part
            else:
                dk_ref[pl.ds(k0, BAND), lanes] += dk_part[:BAND]
                dv_ref[pl.ds(k0, BAND), lanes] += dv_part[:BAND]
                dk_ref[pl.ds(k0 + BAND, BAND), lanes] = dk_part[BAND:]
                dv_ref[pl.ds(k0 + BAND, BAND), lanes] = dv_part[BAND:]

        for u in range(units):
            block(0, 0, BAND, slice(u * 128, (u + 1) * 128), True)

        _for_later_blocks(nblk, units, lambda r0, lanes: block(r0, r0 - BAND, 2 * BAND, lanes, False))

    spec = _class_spec(d)
    return pl.pallas_call(
        body, name=name, grid=(8 * d // units,),
        in_specs=[spec] * 6, out_specs=[spec] * 3,
        out_shape=[_sds((S // d, d * D), BF16)] + [_sds((S // d, d * D), F32)] * 2,
        compiler_params=_cparams(1),
    )(q, k, v, do, lse, dd)


MIX_TILE = 256
DILATIONS = tuple(d for _, d in BRANCHES)


def _branch_weights(la, lb, lc):
    m = jnp.maximum(jnp.maximum(la, lb), lc)
    ea, eb, ec = jnp.exp(la - m), jnp.exp(lb - m), jnp.exp(lc - m)
    den = ea + eb + ec
    return ea / den, eb / den, ec / den


def _mix_operands(outs, lses):
    specs = [_class_block(d, MIX_TILE) for d in DILATIONS] * 2
    scratch = [pltpu.VMEM((NCHUNK, MIX_TILE, 128), F32)] * 4
    return list(outs) + list(lses), specs, scratch


def _mix_fwd(name, outs, lses):
    def body(o0, o1, o2, l0, l1, l2, o_ref, to1, to2, tl1, tl2):
        for blk, tmp, d in ((o1, to1, DILATIONS[1]), (o2, to2, DILATIONS[2]), (l1, tl1, DILATIONS[1]), (l2, tl2, DILATIONS[2])):
            _tokens_from_classes(blk, tmp, d, MIX_TILE)
        for c in range(NCHUNK):
            wa, wb, wc = _branch_weights(l0[:, _chunk(c)], tl1[c], tl2[c])
            o_ref[:, _chunk(c)] = (wa * o0[:, _chunk(c)].astype(F32) + wb * to1[c] + wc * to2[c]).astype(o_ref.dtype)

    operands, specs, scratch = _mix_operands(outs, lses)
    return pl.pallas_call(
        body, name=name, grid=(S // MIX_TILE,),
        in_specs=specs, out_specs=_row_spec(MIX_TILE), out_shape=_sds((S, D), BF16),
        scratch_shapes=scratch, compiler_params=_cparams(1),
    )(*operands)


def _head_sum(x, ones_blockdiag):
    hi = x.astype(BF16)
    r1 = x - hi.astype(F32)
    mid = r1.astype(BF16)
    lo = (r1 - mid.astype(F32)).astype(BF16)
    return _dot_nn(hi, ones_blockdiag) + _dot_nn(mid, ones_blockdiag) + _dot_nn(lo, ones_blockdiag)


def _mix_bwd(name, do, outs, lses, ones_blockdiag):
    def body(do_ref, o0, o1, o2, l0, l1, l2, ones_ref, d0, d1, d2, t0, t1, t2,
             to1, to2, tl1, tl2, td1, td2, tt1, tt2):
        for blk, tmp, d in ((o1, to1, DILATIONS[1]), (o2, to2, DILATIONS[2]), (l1, tl1, DILATIONS[1]), (l2, tl2, DILATIONS[2])):
            _tokens_from_classes(blk, tmp, d, MIX_TILE)
        ones = ones_ref[...]
        for c in range(NCHUNK):
            w = _branch_weights(l0[:, _chunk(c)], tl1[c], tl2[c])
            dov = do_ref[:, _chunk(c)]
            o = w[0] * o0[:, _chunk(c)].astype(F32) + w[1] * to1[c] + w[2] * to2[c]
            t = _head_sum(dov * o, ones)
            d0[:, _chunk(c)] = (w[0] * dov).astype(d0.dtype)
            t0[:, _chunk(c)] = w[0] * t
            td1[c], tt1[c] = w[1] * dov, w[1] * t
            td2[c], tt2[c] = w[2] * dov, w[2] * t
        for tmp, blk, d in ((td1, d1, DILATIONS[1]), (tt1, t1, DILATIONS[1]), (td2, d2, DILATIONS[2]), (tt2, t2, DILATIONS[2])):
            _classes_from_tokens(tmp, blk, d, MIX_TILE)

    operands, specs, scratch = _mix_operands(outs, lses)
    out_specs = [_class_block(d, MIX_TILE) for d in DILATIONS] * 2
    out_shape = [_sds((S // d, d * D), BF16) for d in DILATIONS] + [_sds((S // d, d * D), F32) for d in DILATIONS]
    return pl.pallas_call(
        body, name=name, grid=(S // MIX_TILE,),
        in_specs=[_row_spec(MIX_TILE)] + specs + [_vec_spec(128, 128)],
        out_specs=out_specs, out_shape=out_shape,
        scratch_shapes=scratch + [pltpu.VMEM((NCHUNK, MIX_TILE, 128), F32)] * 4,
        compiler_params=_cparams(1),
    )(do, *operands, ones_blockdiag)


def _attn_bwd_post(name, grads, cos_t, sin_t):
    tm = MIX_TILE
    scale = HEAD_DIM ** -0.5

    def unrope(x, cs, sn):
        return x * cs - _swap_halves(x) * sn

    def body(*refs):
        in_refs = refs[:9]
        cos_ref, sin_ref, dq_ref, dkv_ref, tmp_ref = refs[9:]
        cs = cos_ref[...]
        sn = sin_ref[...]
        for g, d in enumerate(DILATIONS):
            for which, blk in enumerate(in_refs[3 * g:3 * g + 3]):
                if d > 1:
                    _tokens_from_classes(blk, tmp_ref, d, tm)
                for c in range(NCHUNK):
                    x = tmp_ref[c] if d > 1 else blk[:, _chunk(c)].astype(F32)
                    if which == 0:
                        dq_ref[:, _chunk(c, g * D)] = (unrope(x, cs, sn) * scale).astype(dq_ref.dtype)
                    elif which == 1:
                        dkv_ref[:, _chunk(c, g * D)] = unrope(x, cs, sn).astype(dkv_ref.dtype)
                    else:
                        dkv_ref[:, _chunk(c, QW + g * D)] = x.astype(dkv_ref.dtype)

    operands = [a for branch in grads for a in branch]
    tab = pl.BlockSpec((tm, 128), lambda i: (i, 0))
    return pl.pallas_call(
        body, name=name, grid=(S // tm,),
        in_specs=[_class_block(d, tm) for d in DILATIONS for _ in range(3)] + [tab, tab],
        out_specs=[pl.BlockSpec((tm, QW), lambda i: (i, 0)), pl.BlockSpec((tm, 2 * QW), lambda i: (i, 0))],
        out_shape=[_sds((S, QW), BF16), _sds((S, 2 * QW), BF16)],
        scratch_shapes=[pltpu.VMEM((NCHUNK, tm, 128), F32)],
        compiler_params=_cparams(1),
    )(*operands, cos_t, sin_t)


def _adamw(name, parts, w, m, v, layer=None, other=None):
    n, rows, cols = parts.shape
    tr = rows
    for cand in (256, 176, 128, 64, 32, 16, 8):
        if rows % cand == 0:
            tr = cand
            break
    n_other = 0 if other is None else len(other)

    def body(p_ref, w_ref, m_ref, v_ref, *refs):
        g_ref, d_ref, nm_ref, nv_ref = refs[n_other:]
        g = p_ref[0].astype(F32)
        for j in range(1, n):
            g = g + p_ref[j].astype(F32)
        g_ref[...] = g
        d_ref[...], nm_ref[...], nv_ref[...] = _adam_update(g, w_ref[...], m_ref[...], v_ref[...])

    if layer is None:
        blk = pl.BlockSpec((tr, cols), lambda i: (i, 0))
        shape = (rows, cols)
    else:
        blk = pl.BlockSpec((None, tr, cols), lambda i: (layer, i, 0))
        shape = w.shape
    return pl.pallas_call(
        body, name=name, grid=(rows // tr,),
        in_specs=[pl.BlockSpec((n, tr, cols), lambda i: (0, i, 0)), blk, blk, blk]
                 + [pl.BlockSpec(memory_space=pl.ANY)] * n_other,
        out_specs=[blk] * 4, out_shape=[_sds(shape, F32)] * 4,
        input_output_aliases={4 + i: i for i in range(n_other)},
        compiler_params=_cparams(1),
    )(parts, w, m, v, *(other or ()))


def _adam_update(g, w, m, v):
    c1 = 1.0 / (1.0 - ADAM_B1 ** ADAM_STEP)
    c2 = 1.0 / (1.0 - ADAM_B2 ** ADAM_STEP)
    nm = ADAM_B1 * m + (1.0 - ADAM_B1) * g
    nv = ADAM_B2 * v + (1.0 - ADAM_B2) * (g * g)
    return -ADAM_LR * ((nm * c1) / (jnp.sqrt(nv * c2) + ADAM_EPS) + ADAM_WD * w), nm, nv


GAIN_ROWS = 16


def _pack_small(name, gain_tiles, taps, sq):
    ng = len(gain_tiles)

    def body(*refs):
        o_ref = refs[-1]
        o_ref[...] = jnp.zeros_like(o_ref)
        for i in range(ng):
            o_ref[i:i + 1, :] = refs[i][0:1, :]
        o_ref[ng:ng + 3, :] = refs[ng][0:3, :]
        o_ref[ng + 3:ng + 4, :] = refs[ng + 1][...]

    return pl.pallas_call(body, name=name, out_shape=_sds((GAIN_ROWS, D), F32))(*gain_tiles, taps, sq)


def _adamw_gains(name, parts, params):
    np_ = len(params)
    shapes = [w.shape for w, _, _ in params]

    def body(p_ref, *refs):
        ins, outs = refs[:3 * np_], refs[3 * np_:]

        def total(lo, rows):
            g = p_ref[0, lo:lo + rows, :]
            for j in range(1, NDEV):
                g = g + p_ref[j, lo:lo + rows, :]
            return g

        lo = 0
        for i, shape in enumerate(shapes):
            g = total(lo, shape[0])
            lo += shape[0]
            w_ref, m_ref, v_ref = ins[3 * i:3 * i + 3]
            g_ref, d_ref, nm_ref, nv_ref = outs[4 * i:4 * i + 4]
            g_ref[...] = g
            d_ref[...], nm_ref[...], nv_ref[...] = _adam_update(g, w_ref[...], m_ref[...], v_ref[...])
        taps_ref, loss_ref = outs[-2], outs[-1]
        taps_ref[...] = jnp.zeros_like(taps_ref)
        taps_ref[0:3, :] = total(lo, 3)
        loss_ref[...] = jnp.sum(total(lo + 3, 1), axis=-1, keepdims=True) * (0.5 / D)

    out_shape = [_sds(shape, F32) for shape in shapes for _ in range(4)] + [_sds((8, D), F32), _sds((1, 1), F32)]
    outs = pl.pallas_call(body, name=name, out_shape=out_shape)(parts, *[a for p in params for a in p])
    return [list(outs[4 * i:4 * i + 4]) for i in range(np_)], outs[-2], outs[-1].reshape(())


def _exchange(name, arrays, kind):
    n = len(arrays)
    gather = kind == "gather"
    out_shape = [_sds((NDEV,) + a.shape if gather else a.shape, a.dtype) for a in arrays]

    def body(*refs):
        srcs, outs = refs[:n], refs[n:2 * n]
        send_sems, recv_sems, local_sems = refs[2 * n:]
        x, y, c = lax.axis_index("x"), lax.axis_index("y"), lax.axis_index("c")
        me = 4 * x + 2 * y + c
        pending = []
        for t in range(n):
            own = pltpu.make_async_copy(srcs[t] if gather else srcs[t].at[me], outs[t].at[me], local_sems.at[t])
            own.start()
            pending.append(own)
            for rel in range(1, NDEV):
                px = 1 - x if rel & 4 else x
                py = 1 - y if rel & 2 else y
                pc = 1 - c if rel & 1 else c
                peer = 4 * px + 2 * py + pc
                send = pltpu.make_async_remote_copy(
                    src_ref=srcs[t] if gather else srcs[t].at[peer], dst_ref=outs[t].at[me],
                    send_sem=send_sems.at[t, rel - 1], recv_sem=recv_sems.at[t, rel - 1],
                    device_id=(px, py, pc), device_id_type=MESH)
                send.start()
                arrive = pltpu.make_async_remote_copy(
                    src_ref=srcs[t] if gather else srcs[t].at[me], dst_ref=outs[t].at[peer],
                    send_sem=send_sems.at[t, rel - 1], recv_sem=recv_sems.at[t, rel - 1],
                    device_id=(px, py, pc), device_id_type=MESH)
                pending.append((send, arrive))
        for item in pending:
            if isinstance(item, tuple):
                item[0].wait_send()
                item[1].wait_recv()
            else:
                item.wait()

    any_spec = pl.BlockSpec(memory_space=pl.ANY)
    outs = pl.pallas_call(
        body, name=name,
        in_specs=[any_spec] * n, out_specs=[any_spec] * n, out_shape=out_shape,
        scratch_shapes=[pltpu.SemaphoreType.DMA((n, NDEV - 1)), pltpu.SemaphoreType.DMA((n, NDEV - 1)),
                        pltpu.SemaphoreType.DMA((n,))],
    )(*arrays)
    return list(outs)


_HBM_SPEC = pl.BlockSpec(memory_space=pltpu.HBM)
_SEM_SPEC = pl.BlockSpec(memory_space=pltpu.SEMAPHORE)
_DATAFLOW = pltpu.SideEffectType.DATAFLOW_SIDE_EFFECTING


def _peers():
    x, y, c = lax.axis_index("x"), lax.axis_index("y"), lax.axis_index("c")
    out = []
    for rel in range(1, NDEV):
        px = 1 - x if rel & 4 else x
        py = 1 - y if rel & 2 else y
        pc = 1 - c if rel & 1 else c
        out.append((rel - 1, (px, py, pc), 4 * px + 2 * py + pc))
    return 4 * x + 2 * y + c, out


def _hbm(a):
    return pltpu.HBM(a.shape, a.dtype)


def _own_slot(a, me, kind):
    mine = a[None] if kind == "gather" else lax.dynamic_slice_in_dim(a, me, 1, axis=0)
    shape = (NDEV,) + mine.shape[1:]
    return lax.dynamic_update_slice_in_dim(lax.empty(shape, a.dtype), mine, me, axis=0)


def _exchange_start(name, arrays, me, kind):
    n = len(arrays)
    gather = kind == "gather"
    lands = [_own_slot(a, me, kind) for a in arrays]

    def body(*refs):
        src_refs, land_refs = refs[:n], refs[n:2 * n]
        send_sems, recv_sems = refs[2 * n], refs[2 * n + 1]
        token = refs[-1]
        my_block, peers = _peers()
        for t in range(n):
            for slot, dev, block in peers:
                pltpu.make_async_remote_copy(
                    src_ref=src_refs[t] if gather else src_refs[t].at[block], dst_ref=land_refs[t].at[my_block],
                    send_sem=send_sems.at[t * (NDEV - 1) + slot], recv_sem=recv_sems.at[t * (NDEV - 1) + slot],
                    device_id=dev, device_id_type=MESH).start()
        token[...] = jnp.zeros_like(token)

    operands = [pltpu.with_memory_space_constraint(a, pltpu.HBM) for a in list(arrays) + lands]
    outs = pl.pallas_call(
        body, name=name,
        out_shape=(pltpu.SemaphoreType.DMA((n * (NDEV - 1),)), pltpu.SemaphoreType.DMA((n * (NDEV - 1),)),
                   *[_hbm(a) for a in operands], _sds((8, 128), F32)),
        in_specs=[_HBM_SPEC] * (2 * n),
        out_specs=(_SEM_SPEC, _SEM_SPEC, *[_HBM_SPEC] * (2 * n), pl.BlockSpec(memory_space=pltpu.VMEM)),
        input_output_aliases={i: 2 + i for i in range(2 * n)},
        compiler_params=pltpu.CompilerParams(has_side_effects=_DATAFLOW),
    )(*operands)
    return (outs[0], outs[1], list(outs[2:2 + n]), list(outs[2 + n:2 + 2 * n])), outs[-1]


def _exchange_wait(name, started, t, after, kind):
    send_sems, recv_sems, srcs, lands = started
    gather = kind == "gather"

    def body(src_ref, land_ref, send_ref, recv_ref, after_ref, src_out, land_out):
        _, peers = _peers()
        for slot, dev, block in peers:
            copy = pltpu.make_async_remote_copy(
                src_ref=src_ref if gather else src_ref.at[block], dst_ref=land_ref.at[block],
                send_sem=send_ref.at[t * (NDEV - 1) + slot], recv_sem=recv_ref.at[t * (NDEV - 1) + slot],
                device_id=dev, device_id_type=MESH)
            copy.wait_send()
            copy.wait_recv()

    return pl.pallas_call(
        body, name=name, out_shape=(_hbm(srcs[t]), _hbm(lands[t])),
        in_specs=(_HBM_SPEC, _HBM_SPEC, _SEM_SPEC, _SEM_SPEC, pl.BlockSpec(memory_space=pl.ANY)),
        out_specs=(_HBM_SPEC, _HBM_SPEC), input_output_aliases={0: 0, 1: 1},
        compiler_params=pltpu.CompilerParams(has_side_effects=_DATAFLOW),
    )(srcs[t], lands[t], send_sems, recv_sems, after)[1]


DIRECT_RELS = (1, 2, 4, 6)
RELAY_RELS = (2, 4, 6)


def _rel_peer(rel):
    x, y, c = lax.axis_index("x"), lax.axis_index("y"), lax.axis_index("c")
    px = 1 - x if rel & 4 else x
    py = 1 - y if rel & 2 else y
    pc = 1 - c if rel & 1 else c
    return (px, py, pc), 4 * px + 2 * py + pc


def _gather_start(name, shards, me):
    n, nr = len(shards), len(DIRECT_RELS)
    lands = [_own_slot(a, me, "gather") for a in shards]

    def body(*refs):
        src_refs, land_refs = refs[:n], refs[n:2 * n]
        send_sems, recv_sems = refs[2 * n], refs[2 * n + 1]
        _, my_block = _rel_peer(0)
        for t in range(n):
            for s, rel in enumerate(DIRECT_RELS):
                dev, _ = _rel_peer(rel)
                pltpu.make_async_remote_copy(
                    src_ref=src_refs[t], dst_ref=land_refs[t].at[my_block],
                    send_sem=send_sems.at[t * nr + s], recv_sem=recv_sems.at[t * nr + s],
                    device_id=dev, device_id_type=MESH).start()

    operands = [pltpu.with_memory_space_constraint(a, pltpu.HBM) for a in list(shards) + lands]
    outs = pl.pallas_call(
        body, name=name,
        out_shape=(pltpu.SemaphoreType.DMA((n * nr,)), pltpu.SemaphoreType.DMA((n * nr,)), *[_hbm(a) for a in operands]),
        in_specs=[_HBM_SPEC] * (2 * n), out_specs=(_SEM_SPEC, _SEM_SPEC, *[_HBM_SPEC] * (2 * n)),
        input_output_aliases={i: 2 + i for i in range(2 * n)},
        compiler_params=pltpu.CompilerParams(has_side_effects=_DATAFLOW),
    )(*operands)
    return outs[0], outs[1], list(outs[2:2 + n]), list(outs[2 + n:2 + 2 * n])


def _gather_wait(name, started, ts, after):
    send_sems, recv_sems, srcs, lands = started
    m, nr = len(ts), len(DIRECT_RELS)

    def body(*refs):
        src_refs, land_refs = refs[:m], refs[m:2 * m]
        send_ref, recv_ref = refs[2 * m], refs[2 * m + 1]
        for i, t in enumerate(ts):
            for s, rel in enumerate(DIRECT_RELS):
                dev, block = _rel_peer(rel)
                copy = pltpu.make_async_remote_copy(
                    src_ref=src_refs[i], dst_ref=land_refs[i].at[block],
                    send_sem=send_ref.at[t * nr + s], recv_sem=recv_ref.at[t * nr + s],
                    device_id=dev, device_id_type=MESH)
                copy.wait_send()
                copy.wait_recv()

    operands = [srcs[t] for t in ts] + [lands[t] for t in ts]
    outs = pl.pallas_call(
        body, name=name, out_shape=tuple(_hbm(a) for a in operands),
        in_specs=[_HBM_SPEC] * (2 * m) + [_SEM_SPEC, _SEM_SPEC, pl.BlockSpec(memory_space=pl.ANY)],
        out_specs=tuple([_HBM_SPEC] * (2 * m)), input_output_aliases={i: i for i in range(2 * m)},
        compiler_params=pltpu.CompilerParams(has_side_effects=_DATAFLOW),
    )(*operands, send_sems, recv_sems, after)
    return list(outs[m:])


def _relay_start(name, lands):
    m, nr = len(lands), len(RELAY_RELS)

    def body(*refs):
        land_refs, send_sems, recv_sems = refs[:m], refs[m], refs[m + 1]
        sibling, _ = _rel_peer(1)
        for i in range(m):
            for s, rel in enumerate(RELAY_RELS):
                _, block = _rel_peer(rel)
                pltpu.make_async_remote_copy(
                    src_ref=land_refs[i].at[block], dst_ref=land_refs[i].at[block],
                    send_sem=send_sems.at[i * nr + s], recv_sem=recv_sems.at[i * nr + s],
                    device_id=sibling, device_id_type=MESH).start()

    outs = pl.pallas_call(
        body, name=name,
        out_shape=(pltpu.SemaphoreType.DMA((m * nr,)), pltpu.SemaphoreType.DMA((m * nr,)), *[_hbm(a) for a in lands]),
        in_specs=[_HBM_SPEC] * m, out_specs=(_SEM_SPEC, _SEM_SPEC, *[_HBM_SPEC] * m),
        input_output_aliases={i: 2 + i for i in range(m)},
        compiler_params=pltpu.CompilerParams(has_side_effects=_DATAFLOW),
    )(*lands)
    return outs[0], outs[1], list(outs[2:])


def _relay_wait(name, relayed, after):
    send_sems, recv_sems, lands = relayed
    m, nr = len(lands), len(RELAY_RELS)

    def body(*refs):
        land_refs, send_ref, recv_ref = refs[:m], refs[m], refs[m + 1]
        sibling, _ = _rel_peer(1)
        for i in range(m):
            for s, rel in enumerate(RELAY_RELS):
                _, sent = _rel_peer(rel)
                _, arriving = _rel_peer(rel ^ 1)
                copy = pltpu.make_async_remote_copy(
                    src_ref=land_refs[i].at[sent], dst_ref=land_refs[i].at[arriving],
                    send_sem=send_ref.at[i * nr + s], recv_sem=recv_ref.at[i * nr + s],
                    device_id=sibling, device_id_type=MESH)
                copy.wait_send()
                copy.wait_recv()

    outs = pl.pallas_call(
        body, name=name, out_shape=tuple(_hbm(a) for a in lands),
        in_specs=[_HBM_SPEC] * m + [_SEM_SPEC, _SEM_SPEC, pl.BlockSpec(memory_space=pl.ANY)],
        out_specs=tuple([_HBM_SPEC] * m), input_output_aliases={i: i for i in range(m)},
        compiler_params=pltpu.CompilerParams(has_side_effects=_DATAFLOW),
    )(*lands, send_sems, recv_sems, after)
    return list(outs)


def _ffn_fwd(tag, n, wg, wd):
    gu, act = _gate_up_act(f"ffn_gate_up_{tag}", n, wg)
    wd4 = wd.reshape(NFB, FB, D)
    f = _fwd_kblocked(f"ffn_down_{tag}", act, wd4)
    return (n, gu, act, wg, wd4), f


def _ffn_bwd(tag, dh_out, df, h_in, saved, g_pre, send, mixer):
    n, gu, act, wg, wd4 = saved
    tok = send(f"down_{tag}", _bwd_w_kblocked(f"ffn_down_dw_{tag}", act, df).reshape(NDEV, DFF // NDEV, D))
    dgu = _down_dx_act_bwd(f"ffn_down_dx_{tag}", df, wd4, gu, tok).reshape(NDEV, S, FB)
    tok = send(f"gate_up_{tag}", _bwd_w_cols_blocked(f"ffn_gate_up_dw_{tag}", n, dgu))
    dn = _bwd_x_cols_blocked(f"ffn_gate_up_dx_{tag}", dgu, wg, after=tok)
    dh_in, (dg_pre,), dy, dg_mixer = _rms_bwd(f"ffn_prenorm_bwd_{tag}", h_in, [(g_pre, dn)], dh_out, F32, then=mixer)
    return dh_in, dg_pre, dy, dg_mixer


def kernel(x, positions, mix_norm_pre, mix_norm_post, ffn_norm_pre, ffn_norm_post, ffn_w_gate_up, ffn_w_down, conv_w_in, conv_w, conv_w_out, kv_norm, w_kv, w_q, w_o, loss_target, m_mix_norm_pre, m_mix_norm_post, m_ffn_norm_pre, m_ffn_norm_post, m_ffn_w_gate_up, m_ffn_w_down, m_conv_w_in, m_conv_w, m_conv_w_out, m_kv_norm, m_w_kv, m_w_q, m_w_o, v_mix_norm_pre, v_mix_norm_post, v_ffn_norm_pre, v_ffn_norm_post, v_ffn_w_gate_up, v_ffn_w_down, v_conv_w_in, v_conv_w, v_conv_w_out, v_kv_norm, v_w_kv, v_w_q, v_w_o):
    me = 4 * lax.axis_index("x") + 2 * lax.axis_index("y") + lax.axis_index("c")
    h0 = x.reshape(S, D)
    target = loss_target.reshape(S, D)
    row = lambda a, l: a[l].reshape(1, D)
    g_kv = kv_norm.reshape(1, D)

    cw_shard = jnp.pad(conv_w[0], ((0, 5), (0, 0)))
    names = ["conv_in", "conv_w", "conv_out", "gate_up_0", "down_0", "kv", "q", "o", "gate_up_1", "down_1"]
    shards = [conv_w_in[0], cw_shard, conv_w_out[0], ffn_w_gate_up[0], ffn_w_down[0],
              w_kv, w_q[0], w_o[0], ffn_w_gate_up[1], ffn_w_down[1]]
    shards = [s if n == "conv_w" else s.astype(BF16) for n, s in zip(names, shards)]
    first = 3
    gather_first = _gather_start("gather_start_conv", shards[:first], me)
    gather_rest = _gather_start("gather_start_rest", shards[first:], me)

    def direct(group, after):
        ts = [names.index(n) for n in group]
        started, ts = (gather_first, ts) if ts[0] < first else (gather_rest, [t - first for t in ts])
        lands = _gather_wait(f"gather_wait_{group[0]}", started, ts, after)
        return _relay_start(f"relay_start_{group[0]}", lands)

    def finish(group, relayed, after):
        return dict(zip(group, _relay_wait(f"relay_wait_{group[0]}", relayed, after)))

    sent = {}

    def send(name, grad):
        sent[name], token = _exchange_start(f"scatter_start_{name}", [grad], me, "scatter")
        return token

    groups = [["conv_in", "conv_w", "conv_out"], ["gate_up_0", "down_0"], ["kv", "q"], ["o", "gate_up_1", "down_1"]]
    n0 = _rms_fwd("mix_prenorm_0", h0, [row(mix_norm_pre, 0)])[0]
    w = finish(groups[0], direct(groups[0], n0), n0)
    win = w["conv_in"].transpose(1, 0, 2).reshape(D, 3 * D)
    cw = w["conv_w"].transpose(1, 0, 2).reshape(8, D)
    wout = w["conv_out"].reshape(D, D)
    z = _fwd_rows("conv_in", n0, win, BF16)
    pre = _conv_fwd("conv_gate", z, cw)
    relayed = direct(groups[1], pre)
    y0 = _fwd_rows("conv_out", pre, wout)
    h1, (n1,) = _resid_rms("mix_postnorm_0", h0, y0, row(mix_norm_post, 0), [row(ffn_norm_pre, 0)])
    w = finish(groups[1], relayed, n1)
    ffn0, f0 = _ffn_fwd("0", n1, w["gate_up_0"], w["down_0"])
    relayed = direct(groups[2], ffn0[2])
    h2, (nk, n2) = _resid_rms("ffn_postnorm_0", h1, f0, row(ffn_norm_post, 0), [g_kv, row(mix_norm_pre, 1)])

    w = finish(groups[2], relayed, nk)
    wkv = w["kv"].transpose(1, 0, 2).reshape(D, 2 * QW)
    wq = w["q"].transpose(1, 0, 2).reshape(D, QW)
    half = HEAD_DIM // 2
    inv_freq = ROPE_THETA ** (-jnp.arange(half, dtype=F32) / half)
    tables = _rope_tables("rope_tables", positions.reshape(S, 1), jnp.tile(inv_freq, 4).reshape(1, 128))
    qc, kc, vc, o_c, lse_c = [], [], [], [], []
    for g, d in enumerate(DILATIONS):
        kc.append(_proj_classes(f"k_proj_{g}", nk, wkv, g, d, tables, 1.0))
        vc.append(_proj_classes(f"v_proj_{g}", nk, wkv, len(DILATIONS) + g, d, None, None))
    relayed = direct(groups[3], vc[-1])
    for g, d in enumerate(DILATIONS):
        qc.append(_proj_classes(f"q_proj_{g}", n2, wq, g, d, tables, HEAD_DIM ** -0.5))
        o_g, lse_g = _attn_fwd(f"attn_fwd_{g}", qc[g], kc[g], vc[g], d)
        o_c.append(o_g)
        lse_c.append(lse_g)
    o_mix = _mix_fwd("attn_mix", o_c, lse_c)
    w = finish(groups[3], relayed, o_mix)
    wo = w["o"].reshape(D, D)
    y1 = _fwd_rows("attn_out", o_mix, wo)
    h3, (n3,) = _resid_rms("mix_postnorm_1", h2, y1, row(mix_norm_post, 1), [row(ffn_norm_pre, 1)])
    ffn1, f1 = _ffn_fwd("1", n3, w["gate_up_1"], w["down_1"])

    dh4, df1, dg_fpost1, sq = _resid_rms_loss("ffn_postnorm_1_loss", h3, f1, row(ffn_norm_post, 1), target)

    dh3, dg_fpre1, dy1, dg_mpost1 = _ffn_bwd(
        "1", dh4, df1, h3, ffn1, row(ffn_norm_pre, 1), send, (y1, row(mix_norm_post, 1)))
    tok = send("o", _bwd_w_rows("attn_out_dw", o_mix, dy1).reshape(NDEV, D // NDEV, D))
    do = _bwd_x_rows("attn_out_dx", dy1, wo, F32, after=tok)
    lane = jnp.arange(128)
    ones_blockdiag = (lane[:, None] // HEAD_DIM == lane[None, :] // HEAD_DIM).astype(BF16)
    mixed = _mix_bwd("attn_mix_bwd", do, o_c, lse_c, ones_blockdiag)
    branch_grads = [_attn_bwd(f"attn_bwd_{g}", qc[g], kc[g], vc[g], mixed[g], lse_c[g], mixed[3 + g], d)
                    for g, d in enumerate(DILATIONS)]
    dq_raw, dkv = _attn_bwd_post("attn_bwd_post", branch_grads, *tables)
    tok = send("kv", _bwd_w_cols("kv_proj_dw", nk, dkv, 2 * QW // NDEV))
    dnk = _bwd_x_plain("kv_proj_dx", dkv, wkv, after=tok)
    tok = send("q", _bwd_w_cols("q_proj_dw", n2, dq_raw, QW // NDEV))
    dn2 = _bwd_x_plain("q_proj_dx", dq_raw, wq, after=tok)
    dh2, (dg_kv, dg_mpre1), df0, dg_fpost0 = _rms_bwd(
        "kv_and_mix_prenorm_bwd_1", h2, [(g_kv, dnk), (row(mix_norm_pre, 1), dn2)], dh3, F32,
        then=(f0, row(ffn_norm_post, 0)))

    dh1, dg_fpre0, dy0, dg_mpost0 = _ffn_bwd(
        "0", dh2, df0, h1, ffn0, row(ffn_norm_pre, 0), send, (y0, row(mix_norm_post, 0)))
    tok = send("conv_out", _bwd_w_rows("conv_out_dw", pre, dy0).reshape(NDEV, D // NDEV, D))
    dpre = _bwd_x_rows("conv_out_dx", dy0, wout, BF16, after=tok)
    dz, dcw = _conv_bwd("conv_gate_bwd", z, dpre, cw)
    tok = send("conv_in", _bwd_w_cols("conv_in_dw", n0, dz, 3 * D // NDEV))
    dn0 = _bwd_x_plain("conv_in_dx", dz, win, after=tok)
    dh0, (dg_mpre0,) = _rms_bwd("mix_prenorm_bwd_0", h0, [(row(mix_norm_pre, 0), dn0)], dh1, F32)

    small = _pack_small("pack_small_grads", [dg_mpre0, dg_mpre1, dg_mpost0, dg_mpost1, dg_fpre0, dg_fpre1,
                                             dg_fpost0, dg_fpost1, dg_kv], dcw, sq)
    small_all = _exchange("gather_small_grads", [small], "gather")[0]

    done = [small_all]

    def upd(tag, w, m, v):
        parts = _exchange_wait(f"scatter_wait_{tag}", sent[tag], 0, done[-1], "scatter")
        shape = w.shape
        flat = lambda a: a.reshape(parts.shape[1:])
        res = _adamw(f"adamw_{tag}", parts, flat(w), flat(m), flat(v))
        done.append(res[0])
        return [r.reshape(shape) for r in res]

    def upd_layer(tag, l, w, m, v, other):
        parts = _exchange_wait(f"scatter_wait_{tag}_{l}", sent[f"{tag}_{l}"], 0, done[-1], "scatter")
        res = _adamw(f"adamw_{tag}_{l}", parts, w, m, v, layer=l, other=other)
        done.append(res[0])
        return list(res)

    vec = lambda a: a.reshape(1, D)
    gain_res, taps, loss = _adamw_gains("adamw_gains", small_all, [
        (mix_norm_pre, m_mix_norm_pre, v_mix_norm_pre), (mix_norm_post, m_mix_norm_post, v_mix_norm_post),
        (ffn_norm_pre, m_ffn_norm_pre, v_ffn_norm_pre), (ffn_norm_post, m_ffn_norm_post, v_ffn_norm_post),
        (vec(kv_norm), vec(m_kv_norm), vec(v_kv_norm))])
    dcw_mine = lax.dynamic_slice(taps, (0, me * 128), (8, 128))
    pad8 = lambda a, fill: jnp.pad(a[0], ((0, 5), (0, 0)), constant_values=fill)
    cw_res = [r[0:3].reshape(1, 3, 128) for r in
              _adamw("adamw_conv_w", dcw_mine.reshape(1, 8, 128), cw_shard, pad8(m_conv_w, 0.0), pad8(v_conv_w, 1.0))]

    res = {
        "mix_norm_pre": gain_res[0],
        "mix_norm_post": gain_res[1],
        "ffn_norm_pre": gain_res[2],
        "ffn_norm_post": gain_res[3],
        "kv_norm": [r.reshape(D) for r in gain_res[4]],
        "conv_w": cw_res,
    }
    down_1 = upd_layer("down", 1, ffn_w_down, m_ffn_w_down, v_ffn_w_down, None)
    gate_up_1 = upd_layer("gate_up", 1, ffn_w_gate_up, m_ffn_w_gate_up, v_ffn_w_gate_up, None)
    res["w_o"] = upd("o", w_o, m_w_o, v_w_o)
    res["w_q"] = upd("q", w_q, m_w_q, v_w_q)
    res["w_kv"] = upd("kv", w_kv, m_w_kv, v_w_kv)
    res["ffn_w_down"] = upd_layer("down", 0, ffn_w_down, m_ffn_w_down, v_ffn_w_down, down_1)
    res["ffn_w_gate_up"] = upd_layer("gate_up", 0, ffn_w_gate_up, m_ffn_w_gate_up, v_ffn_w_gate_up, gate_up_1)
    res["conv_w_out"] = upd("conv_out", conv_w_out, m_conv_w_out, v_conv_w_out)
    res["conv_w_in"] = upd("conv_in", conv_w_in, m_conv_w_in, v_conv_w_in)
    order = ["mix_norm_pre", "mix_norm_post", "ffn_norm_pre", "ffn_norm_post", "ffn_w_gate_up", "ffn_w_down",
             "conv_w_in", "conv_w", "conv_w_out", "kv_norm", "w_kv", "w_q", "w_o"]
    out = [loss, dh0.reshape(1, S, D)]
    for i in range(4):
        out += [res[name][i] for name in order]
    return tuple(out)
```

```python
import jax
import jax.numpy as jnp
from jax import lax
from jax.experimental import pallas as pl
from jax.experimental.pallas import tpu as pltpu

F32 = jnp.float32
BF16 = jnp.bfloat16

S = 4096
D = 1024
NDEV = 8
HEAD_DIM = 64
QW = 3072
DFF = 2816
FB = 704
NFB = 4
BRANCHES = ((128, 1), (512, 4), (2048, 16))
BAND = 128
ROPE_THETA = 10000.0
RMS_EPS = 1e-6
NEG_INF = -1e30
ADAM_LR, ADAM_B1, ADAM_B2, ADAM_EPS, ADAM_WD, ADAM_STEP = 0.001, 0.9, 0.999, 1e-08, 0.01, 10

VMEM_LIMIT_BYTES = 52 * 1024 * 1024
ROW_TILE = 512
MESH = pl.DeviceIdType.MESH


def _cparams(ngrid):
    return pltpu.CompilerParams(dimension_semantics=("arbitrary",) * ngrid,
                                vmem_limit_bytes=VMEM_LIMIT_BYTES)


def _sds(shape, dtype):
    return jax.ShapeDtypeStruct(tuple(shape), dtype)


_DIMS = {"nn": (((1,), (0,)), ((), ())),
         "nt": (((1,), (1,)), ((), ())),
         "tn": (((0,), (0,)), ((), ()))}


def _matmul(name, a, b, *, mode, grid, a_blk, a_map, b_blk, b_map, o_shape, o_blk, o_map, out_dtype, after=None,
            out_groups=1):
    nk = grid[2]
    dims = _DIMS[mode]
    acc_shape = tuple(s for s in o_blk if s is not None)
    if out_groups > 1:
        acc_shape = (acc_shape[1], out_groups * acc_shape[2])
    extra = [] if after is None else [after]

    def store(o_ref, val):
        if out_groups == 1:
            o_ref[...] = val.astype(o_ref.dtype)
        else:
            n = o_ref.shape[-1]
            for grp in range(out_groups):
                o_ref[grp] = val[:, grp * n:(grp + 1) * n].astype(o_ref.dtype)

    def body(a_ref, b_ref, *rest):
        o_ref, scratch = rest[len(extra)], rest[len(extra) + 1:]
        part = lax.dot_general(a_ref[...], b_ref[...], dims, preferred_element_type=F32)
        if nk == 1:
            store(o_ref, part)
            return
        acc_ref = scratch[0]
        k = pl.program_id(2)

        @pl.when(k == 0)
        def _():
            acc_ref[...] = part

        @pl.when(k > 0)
        def _():
            acc_ref[...] += part

        @pl.when(k == nk - 1)
        def _():
            store(o_ref, acc_ref[...])

    return pl.pallas_call(
        body, name=name, grid=grid,
        in_specs=[pl.BlockSpec(a_blk, a_map), pl.BlockSpec(b_blk, b_map)] + [pl.BlockSpec(memory_space=pl.ANY)] * len(extra),
        out_specs=pl.BlockSpec(o_blk, o_map),
        out_shape=_sds(o_shape, out_dtype),
        scratch_shapes=[] if nk == 1 else [pltpu.VMEM(acc_shape, F32)],
        compiler_params=_cparams(3),
    )(a, b, *extra)


TM = 1024
TK = S


def _fwd_rows(name, a, w, out_dtype=F32):
    kdim, n = w.shape
    tn = 512
    return _matmul(name, a, w, mode="nn", grid=(S // TM, n // tn, 1),
                   a_blk=(TM, kdim), a_map=lambda i, j, k: (i, 0),
                   b_blk=(kdim, tn), b_map=lambda i, j, k: (0, j),
                   o_shape=(S, n), o_blk=(TM, tn), o_map=lambda i, j, k: (i, j), out_dtype=out_dtype)


def _fwd_kblocked(name, a4, w4):
    nb, _, kb = a4.shape
    n = w4.shape[2]

    def body(a_ref, w_ref, o_ref):
        acc = _dot_nn(a_ref[0], w_ref[0])
        for j in range(1, nb):
            acc = acc + _dot_nn(a_ref[j], w_ref[j])
        o_ref[...] = acc

    return pl.pallas_call(
        body, name=name, grid=(S // TM,),
        in_specs=[pl.BlockSpec((nb, TM, kb), lambda i: (0, i, 0)), pl.BlockSpec((nb, kb, n), lambda i: (0, 0, 0))],
        out_specs=pl.BlockSpec((TM, n), lambda i: (i, 0)), out_shape=_sds((S, n), F32),
        compiler_params=_cparams(1),
    )(a4, w4)


def _bwd_x_cols_blocked(name, dy8, wg, after):
    _, kdim, n = wg.shape
    nk = NDEV // 2

    def body(a_ref, b_ref, after_ref, o_ref, acc_ref):
        k = pl.program_id(1)
        part = _dot_nt(a_ref[0], b_ref[0]) + _dot_nt(a_ref[1], b_ref[1])

        @pl.when(k == 0)
        def _():
            acc_ref[...] = part

        @pl.when(k > 0)
        def _():
            acc_ref[...] += part

        @pl.when(k == nk - 1)
        def _():
            o_ref[...] = acc_ref[...]

    return pl.pallas_call(
        body, name=name, grid=(S // TM, nk),
        in_specs=[pl.BlockSpec((2, None, TM, n), lambda i, k: (0, k, i, 0)),
                  pl.BlockSpec((2, None, kdim, n), lambda i, k: (0, k, 0, 0)),
                  pl.BlockSpec(memory_space=pl.ANY)],
        out_specs=pl.BlockSpec((TM, kdim), lambda i, k: (i, 0)), out_shape=_sds((S, kdim), F32),
        scratch_shapes=[pltpu.VMEM((TM, kdim), F32)],
        compiler_params=_cparams(2),
    )(dy8.reshape(2, nk, S, n), wg.reshape(2, nk, kdim, n), after)


def _bwd_x_rows(name, dy, w, out_dtype, after=None):
    kdim, n = w.shape
    tkk = 512
    return _matmul(name, dy, w, mode="nt", grid=(S // TM, kdim // tkk, 1),
                   a_blk=(TM, n), a_map=lambda i, j, k: (i, 0),
                   b_blk=(tkk, n), b_map=lambda i, j, k: (j, 0),
                   o_shape=(S, kdim), o_blk=(TM, tkk), o_map=lambda i, j, k: (i, j), out_dtype=out_dtype, after=after)


DW_COLS = 768


def _bwd_w_cols(name, a, dy, n):
    kdim = a.shape[1]
    groups = DW_COLS // n
    return _matmul(name, a, dy, mode="tn", grid=(1, NDEV // groups, S // TK),
                   a_blk=(TK, kdim), a_map=lambda i, j, k: (k, 0),
                   b_blk=(TK, DW_COLS), b_map=lambda i, j, k: (k, j),
                   o_shape=(NDEV, kdim, n), o_blk=(groups, kdim, n) if groups > 1 else (None, kdim, n),
                   o_map=lambda i, j, k: (j, 0, 0), out_dtype=BF16, out_groups=groups)


def _bwd_x_plain(name, dy, w, after=None):
    kdim, n = w.shape
    tm = TM if n <= 3 * D else TM // 2
    return _matmul(name, dy, w, mode="nt", grid=(S // tm, 1, 1),
                   a_blk=(tm, n), a_map=lambda i, j, k: (i, 0),
                   b_blk=(kdim, n), b_map=lambda i, j, k: (0, 0),
                   o_shape=(S, kdim), o_blk=(tm, kdim), o_map=lambda i, j, k: (i, 0), out_dtype=F32, after=after)


def _bwd_w_cols_blocked(name, a, dy8):
    kdim = a.shape[1]
    n = dy8.shape[2]
    return _matmul(name, dy8, a, mode="tn", grid=(1, NDEV, S // TK),
                   a_blk=(None, TK, n), a_map=lambda i, j, k: (j, k, 0),
                   b_blk=(TK, kdim), b_map=lambda i, j, k: (k, 0),
                   o_shape=(NDEV, n, kdim), o_blk=(None, n, kdim), o_map=lambda i, j, k: (j, 0, 0), out_dtype=BF16)


def _bwd_w_rows(name, a, dy):
    kdim = a.shape[1]
    n = dy.shape[1]
    tmm = 512
    return _matmul(name, a, dy, mode="tn", grid=(kdim // tmm, 1, S // TK),
                   a_blk=(TK, tmm), a_map=lambda i, j, k: (k, i),
                   b_blk=(TK, n), b_map=lambda i, j, k: (k, 0),
                   o_shape=(kdim, n), o_blk=(tmm, n), o_map=lambda i, j, k: (i, 0), out_dtype=BF16)


def _bwd_w_kblocked(name, a4, dy):
    nb, _, kb = a4.shape
    n = dy.shape[1]
    return _matmul(name, a4, dy, mode="tn", grid=(nb, 1, S // TK),
                   a_blk=(None, TK, kb), a_map=lambda i, j, k: (i, k, 0),
                   b_blk=(TK, n), b_map=lambda i, j, k: (k, 0),
                   o_shape=(nb, kb, n), o_blk=(None, kb, n), o_map=lambda i, j, k: (i, 0, 0), out_dtype=BF16)


def _rstd(x):
    return lax.rsqrt(jnp.mean(x * x, axis=-1, keepdims=True) + RMS_EPS)


def _row_spec(tm=ROW_TILE, width=D):
    return pl.BlockSpec((tm, width), lambda i: (i, 0))


def _vec_spec(rows=1, width=D):
    return pl.BlockSpec((rows, width), lambda i: (0, 0))


def _rms_fwd(name, x, gains):
    n = len(gains)

    def body(x_ref, *refs):
        x_val = x_ref[...]
        xh = x_val * _rstd(x_val)
        for g_ref, o_ref in zip(refs[:n], refs[n:]):
            o_ref[...] = (xh * g_ref[...]).astype(o_ref.dtype)

    outs = pl.pallas_call(
        body, name=name, grid=(S // ROW_TILE,),
        in_specs=[_row_spec()] + [_vec_spec()] * n,
        out_specs=[_row_spec()] * n,
        out_shape=[_sds((S, D), BF16)] * n,
        compiler_params=_cparams(1),
    )(x, *gains)
    return list(outs)


def _resid_rms(name, h, y, g, next_gains):
    n = len(next_gains)

    def body(h_ref, y_ref, g_ref, *refs):
        y_val = y_ref[...]
        h_new = h_ref[...] + (y_val * _rstd(y_val)) * g_ref[...]
        refs[n][...] = h_new
        hh = h_new * _rstd(h_new)
        for g2_ref, o_ref in zip(refs[:n], refs[n + 1:]):
            o_ref[...] = (hh * g2_ref[...]).astype(o_ref.dtype)

    outs = pl.pallas_call(
        body, name=name, grid=(S // ROW_TILE,),
        in_specs=[_row_spec(), _row_spec(), _vec_spec()] + [_vec_spec()] * n,
        out_specs=[_row_spec()] * (n + 1), out_shape=[_sds((S, D), F32)] + [_sds((S, D), BF16)] * n,
        compiler_params=_cparams(1),
    )(h, y, g, *next_gains)
    return outs[0], list(outs[1:])


def _resid_rms_loss(name, h, y, g, target):
    def body(h_ref, y_ref, g_ref, t_ref, dh_ref, dy_ref, dg_ref, part_ref):
        y_val = y_ref[...]
        gain = g_ref[...]
        e = h_ref[...] + (y_val * _rstd(y_val)) * gain - t_ref[...]
        dh = e * (1.0 / D)
        dh_ref[...] = dh
        step = pl.program_id(0)
        dy_ref[...] = _norm_bwd_rows(y_val, gain, dh, dg_ref, step).astype(dy_ref.dtype)
        part = jnp.sum(e * e, axis=0, keepdims=True)

        @pl.when(step == 0)
        def _():
            part_ref[...] = part

        @pl.when(step > 0)
        def _():
            part_ref[...] += part

    return pl.pallas_call(
        body, name=name, grid=(S // ROW_TILE,),
        in_specs=[_row_spec(), _row_spec(), _vec_spec(), _row_spec()],
        out_specs=[_row_spec(), _row_spec(), _vec_spec(8), _vec_spec()],
        out_shape=[_sds((S, D), F32), _sds((S, D), BF16), _sds((8, D), F32), _sds((1, D), F32)],
        compiler_params=_cparams(1),
    )(h, y, g, target)


def _norm_bwd_rows(x_val, g, dn, dg_ref, step):
    r = _rstd(x_val)
    xh = x_val * r
    dxh = dn * g
    part = jnp.sum(dn * xh, axis=0, keepdims=True)

    @pl.when(step == 0)
    def _():
        dg_ref[...] = jnp.zeros_like(dg_ref)

    dg_ref[0:1, :] += part
    return r * (dxh - xh * jnp.mean(dxh * xh, axis=-1, keepdims=True))


def _rms_bwd(name, x, pairs, dres, out_dtype, then=None):
    n = len(pairs)
    has_res = dres is not None
    chained = then is not None

    def body(x_ref, *refs):
        g_refs = refs[0:2 * n:2]
        dn_refs = refs[1:2 * n:2]
        pos = 2 * n
        res_ref = refs[pos] if has_res else None
        pos += int(has_res)
        if chained:
            y_ref, gy_ref = refs[pos], refs[pos + 1]
            pos += 2
        dx_ref = refs[pos]
        dg_refs = refs[pos + 1:pos + 1 + n]
        step = pl.program_id(0)
        x_val = x_ref[...]
        acc = res_ref[...] if has_res else jnp.zeros_like(x_val)
        for g_ref, dn_ref, dg_ref in zip(g_refs, dn_refs, dg_refs):
            acc = acc + _norm_bwd_rows(x_val, g_ref[...], dn_ref[...].astype(F32), dg_ref, step)
        dx_ref[...] = acc.astype(dx_ref.dtype)
        if chained:
            dy_ref, dgy_ref = refs[pos + 1 + n], refs[pos + 2 + n]
            dy_ref[...] = _norm_bwd_rows(y_ref[...], gy_ref[...], acc, dgy_ref, step).astype(dy_ref.dtype)

    operands = [x]
    in_specs = [_row_spec()]
    for g, dn in pairs:
        operands += [g, dn]
        in_specs += [_vec_spec(), _row_spec()]
    if has_res:
        operands.append(dres)
        in_specs.append(_row_spec())
    if chained:
        operands += [then[0], then[1]]
        in_specs += [_row_spec(), _vec_spec()]
    extra = int(chained)
    outs = pl.pallas_call(
        body, name=name, grid=(S // ROW_TILE,),
        in_specs=in_specs,
        out_specs=[_row_spec()] + [_vec_spec(8)] * n + [_row_spec(), _vec_spec(8)] * extra,
        out_shape=[_sds((S, D), out_dtype)] + [_sds((8, D), F32)] * n + [_sds((S, D), BF16), _sds((8, D), F32)] * extra,
        compiler_params=_cparams(1),
    )(*operands)
    if chained:
        return outs[0], list(outs[1:1 + n]), outs[1 + n], outs[2 + n]
    return outs[0], list(outs[1:])


def _shift_down(u, prev8, k):
    r = pltpu.roll(u, k, 0)
    p = pltpu.roll(prev8, k, 0)
    row = lax.broadcasted_iota(jnp.int32, prev8.shape, 0)
    top = jnp.where(row < k, p, r[0:8])
    return jnp.concatenate([top, r[8:]], axis=0)


def _shift_up(u, next8, k):
    tm = u.shape[0]
    r = pltpu.roll(u, tm - k, 0)
    p = pltpu.roll(next8, 8 - k, 0)
    row = lax.broadcasted_iota(jnp.int32, next8.shape, 0)
    bot = jnp.where(row >= 8 - k, p, r[tm - 8:tm])
    return jnp.concatenate([r[:tm - 8], bot], axis=0)


CONV_TILE = 512


def _halo_prev(col):
    return pl.BlockSpec((8, D), lambda i: (jnp.maximum(i * (CONV_TILE // 8) - 1, 0), col))


def _halo_next(col):
    last = S // 8 - 1
    return pl.BlockSpec((8, D), lambda i: (jnp.minimum((i + 1) * (CONV_TILE // 8), last), col))


def _conv_fwd(name, z, cw):
    def body(b_ref, c_ref, h_ref, cp_ref, hp_ref, cw_ref, o_ref):
        i = pl.program_id(0)
        u = c_ref[...].astype(F32) * h_ref[...].astype(F32)
        up = cp_ref[...].astype(F32) * hp_ref[...].astype(F32)
        up = jnp.where(i > 0, up, 0.0)
        cv = cw_ref[0:1, :] * _shift_down(u, up, 2) + cw_ref[1:2, :] * _shift_down(u, up, 1) + cw_ref[2:3, :] * u
        o_ref[...] = (b_ref[...].astype(F32) * cv).astype(o_ref.dtype)

    col = lambda c: pl.BlockSpec((CONV_TILE, D), lambda i: (i, c))
    return pl.pallas_call(
        body, name=name, grid=(S // CONV_TILE,),
        in_specs=[col(0), col(1), col(2), _halo_prev(1), _halo_prev(2), _vec_spec(8)],
        out_specs=_row_spec(CONV_TILE), out_shape=_sds((S, D), BF16),
        compiler_params=_cparams(1),
    )(z, z, z, z, z, cw)


def _conv_bwd(name, z, dpre, cw):
    nsteps = S // CONV_TILE

    def body(b_ref, c_ref, h_ref, cp_ref, hp_ref, dp_ref, dpn_ref, bn_ref, cw_ref, dz_ref, dcw_ref):
        i = pl.program_id(0)
        b = b_ref[...].astype(F32)
        c = c_ref[...].astype(F32)
        h = h_ref[...].astype(F32)
        dp = dp_ref[...].astype(F32)
        u = c * h
        up = jnp.where(i > 0, cp_ref[...].astype(F32) * hp_ref[...].astype(F32), 0.0)
        s1 = _shift_down(u, up, 1)
        s2 = _shift_down(u, up, 2)
        w0, w1, w2 = cw_ref[0:1, :], cw_ref[1:2, :], cw_ref[2:3, :]
        cv = w0 * s2 + w1 * s1 + w2 * u
        dcv = dp * b
        dcvn = jnp.where(i < nsteps - 1, dpn_ref[...].astype(F32) * bn_ref[...].astype(F32), 0.0)
        du = w2 * dcv + w1 * _shift_up(dcv, dcvn, 1) + w0 * _shift_up(dcv, dcvn, 2)
        dz_ref[:, 0:D] = (dp * cv).astype(dz_ref.dtype)
        dz_ref[:, D:2 * D] = (du * h).astype(dz_ref.dtype)
        dz_ref[:, 2 * D:3 * D] = (du * c).astype(dz_ref.dtype)

        @pl.when(i == 0)
        def _():
            dcw_ref[...] = jnp.zeros_like(dcw_ref)

        dcw_ref[0:1, :] += jnp.sum(dcv * s2, axis=0, keepdims=True)
        dcw_ref[1:2, :] += jnp.sum(dcv * s1, axis=0, keepdims=True)
        dcw_ref[2:3, :] += jnp.sum(dcv * u, axis=0, keepdims=True)

    col = lambda c: pl.BlockSpec((CONV_TILE, D), lambda i: (i, c))
    return pl.pallas_call(
        body, name=name, grid=(nsteps,),
        in_specs=[col(0), col(1), col(2), _halo_prev(1), _halo_prev(2),
                  _row_spec(CONV_TILE), _halo_next(0), _halo_next(0), _vec_spec(8)],
        out_specs=[pl.BlockSpec((CONV_TILE, 3 * D), lambda i: (i, 0)), _vec_spec(8)],
        out_shape=[_sds((S, 3 * D), BF16), _sds((8, D), F32)],
        compiler_params=_cparams(1),
    )(z, z, z, z, z, dpre, dpre, z, cw)


FFN_TM = 2048
_GU_BLOCK = pl.BlockSpec((2, None, FFN_TM, FB), lambda i, j: (0, j, i, 0))


def _gate_up_act(name, a, wg):
    kdim = a.shape[1]

    def body(a_ref, wgate_ref, wup_ref, gu_ref, act_ref):
        x = a_ref[...]
        g = _dot_nn(x, wgate_ref[...])
        u = _dot_nn(x, wup_ref[...])
        gu_ref[0] = g.astype(gu_ref.dtype)
        gu_ref[1] = u.astype(gu_ref.dtype)
        act_ref[...] = (g * jax.nn.sigmoid(g) * u).astype(act_ref.dtype)

    return pl.pallas_call(
        body, name=name, grid=(S // FFN_TM, NFB),
        in_specs=[pl.BlockSpec((FFN_TM, kdim), lambda i, j: (i, 0)),
                  pl.BlockSpec((None, kdim, FB), lambda i, j: (j, 0, 0)),
                  pl.BlockSpec((None, kdim, FB), lambda i, j: (j + NFB, 0, 0))],
        out_specs=[_GU_BLOCK, pl.BlockSpec((None, FFN_TM, FB), lambda i, j: (j, i, 0))],
        out_shape=[_sds((2, NFB, S, FB), BF16), _sds((NFB, S, FB), BF16)],
        compiler_params=_cparams(2),
    )(a, wg, wg)


def _down_dx_act_bwd(name, df, w4, gu, after):
    _, kb, n = w4.shape

    def body(df_ref, w_ref, gu_ref, after_ref, o_ref):
        d = _dot_nt(df_ref[...], w_ref[...])
        g = gu_ref[0].astype(F32)
        u = gu_ref[1].astype(F32)
        sg = jax.nn.sigmoid(g)
        o_ref[0] = (d * u * sg * (1.0 + g * (1.0 - sg))).astype(o_ref.dtype)
        o_ref[1] = (d * g * sg).astype(o_ref.dtype)

    return pl.pallas_call(
        body, name=name, grid=(S // FFN_TM, NFB),
        in_specs=[pl.BlockSpec((FFN_TM, n), lambda i, j: (i, 0)), pl.BlockSpec((None, kb, n), lambda i, j: (j, 0, 0)),
                  _GU_BLOCK, pl.BlockSpec(memory_space=pl.ANY)],
        out_specs=_GU_BLOCK, out_shape=_sds((2, NFB, S, FB), BF16),
        compiler_params=_cparams(2),
    )(df, w4, gu, after)


def _rope_tables(name, pos_col, inv_freq_row):
    def body(pos_ref, f_ref, cos_ref, sin_ref):
        ang = pos_ref[...].astype(F32) * f_ref[...]
        lane = lax.broadcasted_iota(jnp.int32, ang.shape, 1)
        s = jnp.sin(ang)
        cos_ref[...] = jnp.cos(ang)
        sin_ref[...] = jnp.where((lane % HEAD_DIM) < HEAD_DIM // 2, -s, s)

    tab = pl.BlockSpec((ROW_TILE, 128), lambda i: (i, 0))
    return pl.pallas_call(
        body, name=name, grid=(S // ROW_TILE,),
        in_specs=[pl.BlockSpec((ROW_TILE, 1), lambda i: (i, 0)), _vec_spec(1, 128)],
        out_specs=[tab, tab], out_shape=[_sds((S, 128), F32)] * 2,
        compiler_params=_cparams(1),
    )(pos_col, inv_freq_row)


def _swap_halves(t):
    lane = lax.broadcasted_iota(jnp.int32, t.shape, 1)
    first = (lane % HEAD_DIM) < HEAD_DIM // 2
    return jnp.where(first, pltpu.roll(t, 128 - HEAD_DIM // 2, 1), pltpu.roll(t, HEAD_DIM // 2, 1))


NCHUNK = D // 128


def _chunk(c, base=0):
    return slice(base + c * 128, base + (c + 1) * 128)


def _class_rows(r, d, tm):
    return pl.ds(r, tm // d, stride=d) if d > 1 else slice(None)


def _class_block(d, tm):
    return pl.BlockSpec((tm // d, d * D), lambda i: (i, 0))


def _tokens_from_classes(blk_ref, tmp_ref, d, tm):
    for r in range(d):
        for c in range(NCHUNK):
            tmp_ref[c, _class_rows(r, d, tm), :] = blk_ref[:, _chunk(c, r * D)].astype(F32)


def _classes_from_tokens(tmp_ref, blk_ref, d, tm):
    for r in range(d):
        for c in range(NCHUNK):
            blk_ref[:, _chunk(c, r * D)] = tmp_ref[c, _class_rows(r, d, tm), :].astype(blk_ref.dtype)


def _proj_classes(name, a, w, col, d, tables, scale):
    kdim = a.shape[1]
    rope = tables is not None

    def body(a_ref, w_ref, *refs):
        if rope:
            cos_ref, sin_ref, o_ref, tmp_ref = refs
        else:
            o_ref, tmp_ref = refs
        acc = _dot_nn(a_ref[...], w_ref[...])
        for c in range(NCHUNK):
            tmp_ref[c] = acc[:, _chunk(c)]
        for r in range(d):
            rows = _class_rows(r, d, TM)
            if rope:
                cs = cos_ref[rows, :]
                sn = sin_ref[rows, :]
            for c in range(NCHUNK):
                x = tmp_ref[c, rows, :]
                if rope:
                    x = (x * cs + _swap_halves(x) * sn) * scale
                o_ref[:, _chunk(c, r * D)] = x.astype(o_ref.dtype)

    tab = pl.BlockSpec((TM, 128), lambda i: (i, 0))
    return pl.pallas_call(
        body, name=name, grid=(S // TM,),
        in_specs=[pl.BlockSpec((TM, kdim), lambda i: (i, 0)), pl.BlockSpec((kdim, D), lambda i: (0, col))]
                 + ([tab, tab] if rope else []),
        out_specs=_class_block(d, TM), out_shape=_sds((S // d, d * D), BF16),
        scratch_shapes=[pltpu.VMEM((NCHUNK, TM, 128), F32)],
        compiler_params=_cparams(1),
    )(a, w, *(tables if rope else ()))


ATTN_CHAINS = 8


def _attn_units(d):
    nblk = S // d // BAND
    return max(1, 2 * ATTN_CHAINS // nblk)


def _class_spec(d):
    return pl.BlockSpec((S // d, 128 * _attn_units(d)), lambda cb: (0, cb))


def _dot_nt(a, b):
    return lax.dot_general(a, b, _DIMS["nt"], preferred_element_type=F32)


def _dot_tn(a, b):
    return lax.dot_general(a, b, _DIMS["tn"], preferred_element_type=F32)


def _dot_nn(a, b):
    return lax.dot_general(a, b, _DIMS["nn"], preferred_element_type=F32)


def _band_mask(nkeys):
    qi = lax.broadcasted_iota(jnp.int32, (2 * BAND, nkeys), 0) % BAND
    kj = lax.broadcasted_iota(jnp.int32, (2 * BAND, nkeys), 1)
    if nkeys == BAND:
        return kj <= qi
    dist = qi + BAND - kj
    return (dist >= 0) & (dist <= BAND)


def _stack_heads(x):
    row = lax.broadcasted_iota(jnp.int32, (2 * BAND, 128), 0)
    lane = lax.broadcasted_iota(jnp.int32, (2 * BAND, 128), 1)
    keep = (row < BAND) == (lane < HEAD_DIM)
    return jnp.where(keep, jnp.concatenate([x, x], axis=0), jnp.zeros((), x.dtype))


def _unstack(x2):
    first_head = lax.broadcasted_iota(jnp.int32, (BAND, 128), 1) < HEAD_DIM
    return jnp.where(first_head, x2[:BAND], x2[BAND:])


def _for_later_blocks(nblk, units, fn):
    all_lanes = [slice(u * 128, (u + 1) * 128) for u in range(units)]
    unroll = max(1, ATTN_CHAINS // units)
    trips = (nblk - 1) // unroll
    if trips > 1:
        def step(i, carry):
            for j in range(unroll):
                for lanes in all_lanes:
                    fn(pl.multiple_of((1 + i * unroll + j) * BAND, BAND), lanes)
            return carry

        lax.fori_loop(0, trips, step, 0)
    else:
        trips = 0
    for sb in range(1 + trips * unroll, nblk):
        for lanes in all_lanes:
            fn(sb * BAND, lanes)


def _attn_fwd(name, q, k, v, d):
    nblk = S // d // BAND
    units = _attn_units(d)

    def body(q_ref, k_ref, v_ref, o_ref, lse_ref):
        def block(r0, k0, nkeys, lanes):
            q2 = _stack_heads(q_ref[pl.ds(r0, BAND), lanes])
            s = jnp.where(_band_mask(nkeys), _dot_nt(q2, k_ref[pl.ds(k0, nkeys), lanes]), NEG_INF)
            m = jnp.max(s, axis=-1, keepdims=True)
            p = jnp.exp(s - m)
            l = jnp.sum(p, axis=-1, keepdims=True)
            o2 = _dot_nn(p.astype(BF16), v_ref[pl.ds(k0, nkeys), lanes]) / l
            lse2 = jnp.broadcast_to(m + jnp.log(l), (2 * BAND, 128))
            o_ref[pl.ds(r0, BAND), lanes] = _unstack(o2).astype(o_ref.dtype)
            lse_ref[pl.ds(r0, BAND), lanes] = _unstack(lse2)

        for u in range(units):
            block(0, 0, BAND, slice(u * 128, (u + 1) * 128))

        _for_later_blocks(nblk, units, lambda r0, lanes: block(r0, r0 - BAND, 2 * BAND, lanes))

    spec = _class_spec(d)
    return pl.pallas_call(
        body, name=name, grid=(8 * d // units,),
        in_specs=[spec] * 3, out_specs=[spec] * 2,
        out_shape=[_sds((S // d, d * D), BF16), _sds((S // d, d * D), F32)],
        compiler_params=_cparams(1),
    )(q, k, v)


def _attn_bwd(name, q, k, v, do, lse, dd, d):
    nblk = S // d // BAND
    units = _attn_units(d)

    def body(q_ref, k_ref, v_ref, do_ref, lse_ref, dd_ref, dq_ref, dk_ref, dv_ref):
        def column(ref, r0, lanes, nkeys):
            tile = ref[pl.ds(r0, BAND), lanes]
            other = pltpu.roll(tile, HEAD_DIM, 1)
            first_head = lax.broadcasted_iota(jnp.int32, tile.shape, 1) < HEAD_DIM
            both = jnp.concatenate([jnp.where(first_head, tile, other), jnp.where(first_head, other, tile)], axis=0)
            return both if nkeys == BAND else jnp.concatenate([both, both], axis=1)

        def block(r0, k0, nkeys, lanes, first):
            q2 = _stack_heads(q_ref[pl.ds(r0, BAND), lanes])
            do2 = _stack_heads(do_ref[pl.ds(r0, BAND), lanes])
            kk = k_ref[pl.ds(k0, nkeys), lanes]
            vv = v_ref[pl.ds(k0, nkeys), lanes]
            s = jnp.where(_band_mask(nkeys), _dot_nt(q2, kk), NEG_INF)
            p = jnp.exp(s - column(lse_ref, r0, lanes, nkeys))
            ds = (p * (_dot_nt(do2, vv) - column(dd_ref, r0, lanes, nkeys))).astype(BF16)
            dq_ref[pl.ds(r0, BAND), lanes] = _unstack(_dot_nn(ds, kk)).astype(dq_ref.dtype)
            dk_part = _dot_tn(ds, q2)
            dv_part = _dot_tn(p.astype(BF16), do2)
            if first:
                dk_ref[pl.ds(k0, nkeys), lanes] = dk_part
                dv_ref[pl.ds(k0, nkeys), lanes] = dv_part
            else:
                dk_ref[pl.ds(k0, BAND), lanes] += dk_part[:BAND]
                dv_ref[pl.ds(k0, BAND), lanes] += dv_part[:BAND]
                dk_ref[pl.ds(k0 + BAND, BAND), lanes] = dk_part[BAND:]
                dv_ref[pl.ds(k0 + BAND, BAND), lanes] = dv_part[BAND:]

        for u in range(units):
            block(0, 0, BAND, slice(u * 128, (u + 1) * 128), True)

        _for_later_blocks(nblk, units, lambda r0, lanes: block(r0, r0 - BAND, 2 * BAND, lanes, False))

    spec = _class_spec(d)
    return pl.pallas_call(
        body, name=name, grid=(8 * d // units,),
        in_specs=[spec] * 6, out_specs=[spec] * 3,
        out_shape=[_sds((S // d, d * D), BF16)] + [_sds((S // d, d * D), F32)] * 2,
        compiler_params=_cparams(1),
    )(q, k, v, do, lse, dd)


MIX_TILE = 256
DILATIONS = tuple(d for _, d in BRANCHES)


def _branch_weights(la, lb, lc):
    m = jnp.maximum(jnp.maximum(la, lb), lc)
    ea, eb, ec = jnp.exp(la - m), jnp.exp(lb - m), jnp.exp(lc - m)
    den = ea + eb + ec
    return ea / den, eb / den, ec / den


def _mix_operands(outs, lses):
    specs = [_class_block(d, MIX_TILE) for d in DILATIONS] * 2
    scratch = [pltpu.VMEM((NCHUNK, MIX_TILE, 128), F32)] * 4
    return list(outs) + list(lses), specs, scratch


def _mix_fwd(name, outs, lses):
    def body(o0, o1, o2, l0, l1, l2, o_ref, to1, to2, tl1, tl2):
        for blk, tmp, d in ((o1, to1, DILATIONS[1]), (o2, to2, DILATIONS[2]), (l1, tl1, DILATIONS[1]), (l2, tl2, DILATIONS[2])):
            _tokens_from_classes(blk, tmp, d, MIX_TILE)
        for c in range(NCHUNK):
            wa, wb, wc = _branch_weights(l0[:, _chunk(c)], tl1[c], tl2[c])
            o_ref[:, _chunk(c)] = (wa * o0[:, _chunk(c)].astype(F32) + wb * to1[c] + wc * to2[c]).astype(o_ref.dtype)

    operands, specs, scratch = _mix_operands(outs, lses)
    return pl.pallas_call(
        body, name=name, grid=(S // MIX_TILE,),
        in_specs=specs, out_specs=_row_spec(MIX_TILE), out_shape=_sds((S, D), BF16),
        scratch_shapes=scratch, compiler_params=_cparams(1),
    )(*operands)


def _head_sum(x, ones_blockdiag):
    hi = x.astype(BF16)
    r1 = x - hi.astype(F32)
    mid = r1.astype(BF16)
    lo = (r1 - mid.astype(F32)).astype(BF16)
    return _dot_nn(hi, ones_blockdiag) + _dot_nn(mid, ones_blockdiag) + _dot_nn(lo, ones_blockdiag)


def _mix_bwd(name, do, outs, lses, ones_blockdiag):
    def body(do_ref, o0, o1, o2, l0, l1, l2, ones_ref, d0, d1, d2, t0, t1, t2,
             to1, to2, tl1, tl2, td1, td2, tt1, tt2):
        for blk, tmp, d in ((o1, to1, DILATIONS[1]), (o2, to2, DILATIONS[2]), (l1, tl1, DILATIONS[1]), (l2, tl2, DILATIONS[2])):
            _tokens_from_classes(blk, tmp, d, MIX_TILE)
        ones = ones_ref[...]
        for c in range(NCHUNK):
            w = _branch_weights(l0[:, _chunk(c)], tl1[c], tl2[c])
            dov = do_ref[:, _chunk(c)]
            o = w[0] * o0[:, _chunk(c)].astype(F32) + w[1] * to1[c] + w[2] * to2[c]
            t = _head_sum(dov * o, ones)
            d0[:, _chunk(c)] = (w[0] * dov).astype(d0.dtype)
            t0[:, _chunk(c)] = w[0] * t
            td1[c], tt1[c] = w[1] * dov, w[1] * t
            td2[c], tt2[c] = w[2] * dov, w[2] * t
        for tmp, blk, d in ((td1, d1, DILATIONS[1]), (tt1, t1, DILATIONS[1]), (td2, d2, DILATIONS[2]), (tt2, t2, DILATIONS[2])):
            _classes_from_tokens(tmp, blk, d, MIX_TILE)

    operands, specs, scratch = _mix_operands(outs, lses)
    out_specs = [_class_block(d, MIX_TILE) for d in DILATIONS] * 2
    out_shape = [_sds((S // d, d * D), BF16) for d in DILATIONS] + [_sds((S // d, d * D), F32) for d in DILATIONS]
    return pl.pallas_call(
        body, name=name, grid=(S // MIX_TILE,),
        in_specs=[_row_spec(MIX_TILE)] + specs + [_vec_spec(128, 128)],
        out_specs=out_specs, out_shape=out_shape,
        scratch_shapes=scratch + [pltpu.VMEM((NCHUNK, MIX_TILE, 128), F32)] * 4,
        compiler_params=_cparams(1),
    )(do, *operands, ones_blockdiag)


def _attn_bwd_post(name, grads, cos_t, sin_t):
    tm = MIX_TILE
    scale = HEAD_DIM ** -0.5

    def unrope(x, cs, sn):
        return x * cs - _swap_halves(x) * sn

    def body(*refs):
        in_refs = refs[:9]
        cos_ref, sin_ref, dq_ref, dkv_ref, tmp_ref = refs[9:]
        cs = cos_ref[...]
        sn = sin_ref[...]
        for g, d in enumerate(DILATIONS):
            for which, blk in enumerate(in_refs[3 * g:3 * g + 3]):
                if d > 1:
                    _tokens_from_classes(blk, tmp_ref, d, tm)
                for c in range(NCHUNK):
                    x = tmp_ref[c] if d > 1 else blk[:, _chunk(c)].astype(F32)
                    if which == 0:
                        dq_ref[:, _chunk(c, g * D)] = (unrope(x, cs, sn) * scale).astype(dq_ref.dtype)
                    elif which == 1:
                        dkv_ref[:, _chunk(c, g * D)] = unrope(x, cs, sn).astype(dkv_ref.dtype)
                    else:
                        dkv_ref[:, _chunk(c, QW + g * D)] = x.astype(dkv_ref.dtype)

    operands = [a for branch in grads for a in branch]
    tab = pl.BlockSpec((tm, 128), lambda i: (i, 0))
    return pl.pallas_call(
        body, name=name, grid=(S // tm,),
        in_specs=[_class_block(d, tm) for d in DILATIONS for _ in range(3)] + [tab, tab],
        out_specs=[pl.BlockSpec((tm, QW), lambda i: (i, 0)), pl.BlockSpec((tm, 2 * QW), lambda i: (i, 0))],
        out_shape=[_sds((S, QW), BF16), _sds((S, 2 * QW), BF16)],
        scratch_shapes=[pltpu.VMEM((NCHUNK, tm, 128), F32)],
        compiler_params=_cparams(1),
    )(*operands, cos_t, sin_t)


def _adamw(name, parts, w, m, v, layer=None, other=None):
    n, rows, cols = parts.shape
    tr = rows
    for cand in (256, 176, 128, 64, 32, 16, 8):
        if rows % cand == 0:
            tr = cand
            break
    n_other = 0 if other is None else len(other)

    def body(p_ref, w_ref, m_ref, v_ref, *refs):
        g_ref, d_ref, nm_ref, nv_ref = refs[n_other:]
        g = p_ref[0].astype(F32)
        for j in range(1, n):
            g = g + p_ref[j].astype(F32)
        g_ref[...] = g
        d_ref[...], nm_ref[...], nv_ref[...] = _adam_update(g, w_ref[...], m_ref[...], v_ref[...])

    if layer is None:
        blk = pl.BlockSpec((tr, cols), lambda i: (i, 0))
        shape = (rows, cols)
    else:
        blk = pl.BlockSpec((None, tr, cols), lambda i: (layer, i, 0))
        shape = w.shape
    return pl.pallas_call(
        body, name=name, grid=(rows // tr,),
        in_specs=[pl.BlockSpec((n, tr, cols), lambda i: (0, i, 0)), blk, blk, blk]
                 + [pl.BlockSpec(memory_space=pl.ANY)] * n_other,
        out_specs=[blk] * 4, out_shape=[_sds(shape, F32)] * 4,
        input_output_aliases={4 + i: i for i in range(n_other)},
        compiler_params=_cparams(1),
    )(parts, w, m, v, *(other or ()))


def _adam_update(g, w, m, v):
    c1 = 1.0 / (1.0 - ADAM_B1 ** ADAM_STEP)
    c2 = 1.0 / (1.0 - ADAM_B2 ** ADAM_STEP)
    nm = ADAM_B1 * m + (1.0 - ADAM_B1) * g
    nv = ADAM_B2 * v + (1.0 - ADAM_B2) * (g * g)
    return -ADAM_LR * ((nm * c1) / (jnp.sqrt(nv * c2) + ADAM_EPS) + ADAM_WD * w), nm, nv


GAIN_ROWS = 16


def _pack_small(name, gain_tiles, taps, sq):
    ng = len(gain_tiles)

    def body(*refs):
        o_ref = refs[-1]
        o_ref[...] = jnp.zeros_like(o_ref)
        for i in range(ng):
            o_ref[i:i + 1, :] = refs[i][0:1, :]
        o_ref[ng:ng + 3, :] = refs[ng][0:3, :]
        o_ref[ng + 3:ng + 4, :] = refs[ng + 1][...]

    return pl.pallas_call(body, name=name, out_shape=_sds((GAIN_ROWS, D), F32))(*gain_tiles, taps, sq)


def _adamw_gains(name, parts, params):
    np_ = len(params)
    shapes = [w.shape for w, _, _ in params]

    def body(p_ref, *refs):
        ins, outs = refs[:3 * np_], refs[3 * np_:]

        def total(lo, rows):
            g = p_ref[0, lo:lo + rows, :]
            for j in range(1, NDEV):
                g = g + p_ref[j, lo:lo + rows, :]
            return g

        lo = 0
        for i, shape in enumerate(shapes):
            g = total(lo, shape[0])
            lo += shape[0]
            w_ref, m_ref, v_ref = ins[3 * i:3 * i + 3]
            g_ref, d_ref, nm_ref, nv_ref = outs[4 * i:4 * i + 4]
            g_ref[...] = g
            d_ref[...], nm_ref[...], nv_ref[...] = _adam_update(g, w_ref[...], m_ref[...], v_ref[...])
        taps_ref, loss_ref = outs[-2], outs[-1]
        taps_ref[...] = jnp.zeros_like(taps_ref)
        taps_ref[0:3, :] = total(lo, 3)
        loss_ref[...] = jnp.sum(total(lo + 3, 1), axis=-1, keepdims=True) * (0.5 / D)

    out_shape = [_sds(shape, F32) for shape in shapes for _ in range(4)] + [_sds((8, D), F32), _sds((1, 1), F32)]
    outs = pl.pallas_call(body, name=name, out_shape=out_shape)(parts, *[a for p in params for a in p])
    return [list(outs[4 * i:4 * i + 4]) for i in range(np_)], outs[-2], outs[-1].reshape(())


def _exchange(name, arrays, kind):
    n = len(arrays)
    gather = kind == "gather"
    out_shape = [_sds((NDEV,) + a.shape if gather else a.shape, a.dtype) for a in arrays]

    def body(*refs):
        srcs, outs = refs[:n], refs[n:2 * n]
        send_sems, recv_sems, local_sems = refs[2 * n:]
        x, y, c = lax.axis_index("x"), lax.axis_index("y"), lax.axis_index("c")
        me = 4 * x + 2 * y + c
        pending = []
        for t in range(n):
            own = pltpu.make_async_copy(srcs[t] if gather else srcs[t].at[me], outs[t].at[me], local_sems.at[t])
            own.start()
            pending.append(own)
            for rel in range(1, NDEV):
                px = 1 - x if rel & 4 else x
                py = 1 - y if rel & 2 else y
                pc = 1 - c if rel & 1 else c
                peer = 4 * px + 2 * py + pc
                send = pltpu.make_async_remote_copy(
                    src_ref=srcs[t] if gather else srcs[t].at[peer], dst_ref=outs[t].at[me],
                    send_sem=send_sems.at[t, rel - 1], recv_sem=recv_sems.at[t, rel - 1],
                    device_id=(px, py, pc), device_id_type=MESH)
                send.start()
                arrive = pltpu.make_async_remote_copy(
                    src_ref=srcs[t] if gather else srcs[t].at[me], dst_ref=outs[t].at[peer],
                    send_sem=send_sems.at[t, rel - 1], recv_sem=recv_sems.at[t, rel - 1],
                    device_id=(px, py, pc), device_id_type=MESH)
                pending.append((send, arrive))
        for item in pending:
            if isinstance(item, tuple):
                item[0].wait_send()
                item[1].wait_recv()
            else:
                item.wait()

    any_spec = pl.BlockSpec(memory_space=pl.ANY)
    outs = pl.pallas_call(
        body, name=name,
        in_specs=[any_spec] * n, out_specs=[any_spec] * n, out_shape=out_shape,
        scratch_shapes=[pltpu.SemaphoreType.DMA((n, NDEV - 1)), pltpu.SemaphoreType.DMA((n, NDEV - 1)),
                        pltpu.SemaphoreType.DMA((n,))],
    )(*arrays)
    return list(outs)


_HBM_SPEC = pl.BlockSpec(memory_space=pltpu.HBM)
_SEM_SPEC = pl.BlockSpec(memory_space=pltpu.SEMAPHORE)
_DATAFLOW = pltpu.SideEffectType.DATAFLOW_SIDE_EFFECTING


def _peers():
    x, y, c = lax.axis_index("x"), lax.axis_index("y"), lax.axis_index("c")
    out = []
    for rel in range(1, NDEV):
        px = 1 - x if rel & 4 else x
        py = 1 - y if rel & 2 else y
        pc = 1 - c if rel & 1 else c
        out.append((rel - 1, (px, py, pc), 4 * px + 2 * py + pc))
    return 4 * x + 2 * y + c, out


def _hbm(a):
    return pltpu.HBM(a.shape, a.dtype)


def _own_slot(a, me, kind):
    mine = a[None] if kind == "gather" else lax.dynamic_slice_in_dim(a, me, 1, axis=0)
    shape = (NDEV,) + mine.shape[1:]
    return lax.dynamic_update_slice_in_dim(lax.empty(shape, a.dtype), mine, me, axis=0)


def _exchange_start(name, arrays, me, kind):
    n = len(arrays)
    gather = kind == "gather"
    lands = [_own_slot(a, me, kind) for a in arrays]

    def body(*refs):
        src_refs, land_refs = refs[:n], refs[n:2 * n]
        send_sems, recv_sems = refs[2 * n], refs[2 * n + 1]
        token = refs[-1]
        my_block, peers = _peers()
        for t in range(n):
            for slot, dev, block in peers:
                pltpu.make_async_remote_copy(
                    src_ref=src_refs[t] if gather else src_refs[t].at[block], dst_ref=land_refs[t].at[my_block],
                    send_sem=send_sems.at[t * (NDEV - 1) + slot], recv_sem=recv_sems.at[t * (NDEV - 1) + slot],
                    device_id=dev, device_id_type=MESH).start()
        token[...] = jnp.zeros_like(token)

    operands = [pltpu.with_memory_space_constraint(a, pltpu.HBM) for a in list(arrays) + lands]
    outs = pl.pallas_call(
        body, name=name,
        out_shape=(pltpu.SemaphoreType.DMA((n * (NDEV - 1),)), pltpu.SemaphoreType.DMA((n * (NDEV - 1),)),
                   *[_hbm(a) for a in operands], _sds((8, 128), F32)),
        in_specs=[_HBM_SPEC] * (2 * n),
        out_specs=(_SEM_SPEC, _SEM_SPEC, *[_HBM_SPEC] * (2 * n), pl.BlockSpec(memory_space=pltpu.VMEM)),
        input_output_aliases={i: 2 + i for i in range(2 * n)},
        compiler_params=pltpu.CompilerParams(has_side_effects=_DATAFLOW),
    )(*operands)
    return (outs[0], outs[1], list(outs[2:2 + n]), list(outs[2 + n:2 + 2 * n])), outs[-1]


def _exchange_wait(name, started, t, after, kind):
    send_sems, recv_sems, srcs, lands = started
    gather = kind == "gather"

    def body(src_ref, land_ref, send_ref, recv_ref, after_ref, src_out, land_out):
        _, peers = _peers()
        for slot, dev, block in peers:
            copy = pltpu.make_async_remote_copy(
                src_ref=src_ref if gather else src_ref.at[block], dst_ref=land_ref.at[block],
                send_sem=send_ref.at[t * (NDEV - 1) + slot], recv_sem=recv_ref.at[t * (NDEV - 1) + slot],
                device_id=dev, device_id_type=MESH)
            copy.wait_send()
            copy.wait_recv()

    return pl.pallas_call(
        body, name=name, out_shape=(_hbm(srcs[t]), _hbm(lands[t])),
        in_specs=(_HBM_SPEC, _HBM_SPEC, _SEM_SPEC, _SEM_SPEC, pl.BlockSpec(memory_space=pl.ANY)),
        out_specs=(_HBM_SPEC, _HBM_SPEC), input_output_aliases={0: 0, 1: 1},
        compiler_params=pltpu.CompilerParams(has_side_effects=_DATAFLOW),
    )(srcs[t], lands[t], send_sems, recv_sems, after)[1]


DIRECT_RELS = (1, 2, 4, 6)
RELAY_RELS = (2, 4, 6)


def _rel_peer(rel):
    x, y, c = lax.axis_index("x"), lax.axis_index("y"), lax.axis_index("c")
    px = 1 - x if rel & 4 else x
    py = 1 - y if rel & 2 else y
    pc = 1 - c if rel & 1 else c
    return (px, py, pc), 4 * px + 2 * py + pc


def _gather_start(name, shards, me):
    n, nr = len(shards), len(DIRECT_RELS)
    lands = [_own_slot(a, me, "gather") for a in shards]

    def body(*refs):
        src_refs, land_refs = refs[:n], refs[n:2 * n]
        send_sems, recv_sems = refs[2 * n], refs[2 * n + 1]
        _, my_block = _rel_peer(0)
        for t in range(n):
            for s, rel in enumerate(DIRECT_RELS):
                dev, _ = _rel_peer(rel)
                pltpu.make_async_remote_copy(
                    src_ref=src_refs[t], dst_ref=land_refs[t].at[my_block],
                    send_sem=send_sems.at[t * nr + s], recv_sem=recv_sems.at[t * nr + s],
                    device_id=dev, device_id_type=MESH).start()

    operands = [pltpu.with_memory_space_constraint(a, pltpu.HBM) for a in list(shards) + lands]
    outs = pl.pallas_call(
        body, name=name,
        out_shape=(pltpu.SemaphoreType.DMA((n * nr,)), pltpu.SemaphoreType.DMA((n * nr,)), *[_hbm(a) for a in operands]),
        in_specs=[_HBM_SPEC] * (2 * n), out_specs=(_SEM_SPEC, _SEM_SPEC, *[_HBM_SPEC] * (2 * n)),
        input_output_aliases={i: 2 + i for i in range(2 * n)},
        compiler_params=pltpu.CompilerParams(has_side_effects=_DATAFLOW),
    )(*operands)
    return outs[0], outs[1], list(outs[2:2 + n]), list(outs[2 + n:2 + 2 * n])


def _gather_wait(name, started, ts, after):
    send_sems, recv_sems, srcs, lands = started
    m, nr = len(ts), len(DIRECT_RELS)

    def body(*refs):
        src_refs, land_refs = refs[:m], refs[m:2 * m]
        send_ref, recv_ref = refs[2 * m], refs[2 * m + 1]
        for i, t in enumerate(ts):
            for s, rel in enumerate(DIRECT_RELS):
                dev, block = _rel_peer(rel)
                copy = pltpu.make_async_remote_copy(
                    src_ref=src_refs[i], dst_ref=land_refs[i].at[block],
                    send_sem=send_ref.at[t * nr + s], recv_sem=recv_ref.at[t * nr + s],
                    device_id=dev, device_id_type=MESH)
                copy.wait_send()
                copy.wait_recv()

    operands = [srcs[t] for t in ts] + [lands[t] for t in ts]
    outs = pl.pallas_call(
        body, name=name, out_shape=tuple(_hbm(a) for a in operands),
        in_specs=[_HBM_SPEC] * (2 * m) + [_SEM_SPEC, _SEM_SPEC, pl.BlockSpec(memory_space=pl.ANY)],
        out_specs=tuple([_HBM_SPEC] * (2 * m)), input_output_aliases={i: i for i in range(2 * m)},
        compiler_params=pltpu.CompilerParams(has_side_effects=_DATAFLOW),
    )(*operands, send_sems, recv_sems, after)
    return list(outs[m:])


def _relay_start(name, lands):
    m, nr = len(lands), len(RELAY_RELS)

    def body(*refs):
        land_refs, send_sems, recv_sems = refs[:m], refs[m], refs[m + 1]
        sibling, _ = _rel_peer(1)
        for i in range(m):
            for s, rel in enumerate(RELAY_RELS):
                _, block = _rel_peer(rel)
                pltpu.make_async_remote_copy(
                    src_ref=land_refs[i].at[block], dst_ref=land_refs[i].at[block],
                    send_sem=send_sems.at[i * nr + s], recv_sem=recv_sems.at[i * nr + s],
                    device_id=sibling, device_id_type=MESH).start()

    outs = pl.pallas_call(
        body, name=name,
        out_shape=(pltpu.SemaphoreType.DMA((m * nr,)), pltpu.SemaphoreType.DMA((m * nr,)), *[_hbm(a) for a in lands]),
        in_specs=[_HBM_SPEC] * m, out_specs=(_SEM_SPEC, _SEM_SPEC, *[_HBM_SPEC] * m),
        input_output_aliases={i: 2 + i for i in range(m)},
        compiler_params=pltpu.CompilerParams(has_side_effects=_DATAFLOW),
    )(*lands)
    return outs[0], outs[1], list(outs[2:])


def _relay_wait(name, relayed, after):
    send_sems, recv_sems, lands = relayed
    m, nr = len(lands), len(RELAY_RELS)

    def body(*refs):
        land_refs, send_ref, recv_ref = refs[:m], refs[m], refs[m + 1]
        sibling, _ = _rel_peer(1)
        for i in range(m):
            for s, rel in enumerate(RELAY_RELS):
                _, sent = _rel_peer(rel)
                _, arriving = _rel_peer(rel ^ 1)
                copy = pltpu.make_async_remote_copy(
                    src_ref=land_refs[i].at[sent], dst_ref=land_refs[i].at[arriving],
                    send_sem=send_ref.at[i * nr + s], recv_sem=recv_ref.at[i * nr + s],
                    device_id=sibling, device_id_type=MESH)
                copy.wait_send()
                copy.wait_recv()

    outs = pl.pallas_call(
        body, name=name, out_shape=tuple(_hbm(a) for a in lands),
        in_specs=[_HBM_SPEC] * m + [_SEM_SPEC, _SEM_SPEC, pl.BlockSpec(memory_space=pl.ANY)],
        out_specs=tuple([_HBM_SPEC] * m), input_output_aliases={i: i for i in range(m)},
        compiler_params=pltpu.CompilerParams(has_side_effects=_DATAFLOW),
    )(*lands, send_sems, recv_sems, after)
    return list(outs)


def _ffn_fwd(tag, n, wg, wd):
    gu, act = _gate_up_act(f"ffn_gate_up_{tag}", n, wg)
    wd4 = wd.reshape(NFB, FB, D)
    f = _fwd_kblocked(f"ffn_down_{tag}", act, wd4)
    return (n, gu, act, wg, wd4), f


def _ffn_bwd(tag, dh_out, df, h_in, saved, g_pre, send, mixer):
    n, gu, act, wg, wd4 = saved
    tok = send(f"down_{tag}", _bwd_w_kblocked(f"ffn_down_dw_{tag}", act, df).reshape(NDEV, DFF // NDEV, D))
    dgu = _down_dx_act_bwd(f"ffn_down_dx_{tag}", df, wd4, gu, tok).reshape(NDEV, S, FB)
    tok = send(f"gate_up_{tag}", _bwd_w_cols_blocked(f"ffn_gate_up_dw_{tag}", n, dgu))
    dn = _bwd_x_cols_blocked(f"ffn_gate_up_dx_{tag}", dgu, wg, after=tok)
    dh_in, (dg_pre,), dy, dg_mixer = _rms_bwd(f"ffn_prenorm_bwd_{tag}", h_in, [(g_pre, dn)], dh_out, F32, then=mixer)
    return dh_in, dg_pre, dy, dg_mixer


def kernel(x, positions, mix_norm_pre, mix_norm_post, ffn_norm_pre, ffn_norm_post, ffn_w_gate_up, ffn_w_down, conv_w_in, conv_w, conv_w_out, kv_norm, w_kv, w_q, w_o, loss_target, m_mix_norm_pre, m_mix_norm_post, m_ffn_norm_pre, m_ffn_norm_post, m_ffn_w_gate_up, m_ffn_w_down, m_conv_w_in, m_conv_w, m_conv_w_out, m_kv_norm, m_w_kv, m_w_q, m_w_o, v_mix_norm_pre, v_mix_norm_post, v_ffn_norm_pre, v_ffn_norm_post, v_ffn_w_gate_up, v_ffn_w_down, v_conv_w_in, v_conv_w, v_conv_w_out, v_kv_norm, v_w_kv, v_w_q, v_w_o):
    me = 4 * lax.axis_index("x") + 2 * lax.axis_index("y") + lax.axis_index("c")
    h0 = x.reshape(S, D)
    target = loss_target.reshape(S, D)
    row = lambda a, l: a[l].reshape(1, D)
    g_kv = kv_norm.reshape(1, D)

    cw_shard = jnp.pad(conv_w[0], ((0, 5), (0, 0)))
    names = ["conv_in", "conv_w", "conv_out", "gate_up_0", "down_0", "kv", "q", "o", "gate_up_1", "down_1"]
    shards = [conv_w_in[0], cw_shard, conv_w_out[0], ffn_w_gate_up[0], ffn_w_down[0],
              w_kv, w_q[0], w_o[0], ffn_w_gate_up[1], ffn_w_down[1]]
    shards = [s if n == "conv_w" else s.astype(BF16) for n, s in zip(names, shards)]
    first = 3
    gather_first = _gather_start("gather_start_conv", shards[:first], me)
    gather_rest = _gather_start("gather_start_rest", shards[first:], me)

    def direct(group, after):
        ts = [names.index(n) for n in group]
        started, ts = (gather_first, ts) if ts[0] < first else (gather_rest, [t - first for t in ts])
        lands = _gather_wait(f"gather_wait_{group[0]}", started, ts, after)
        return _relay_start(f"relay_start_{group[0]}", lands)

    def finish(group, relayed, after):
        return dict(zip(group, _relay_wait(f"relay_wait_{group[0]}", relayed, after)))

    sent = {}

    def send(name, grad):
        sent[name], token = _exchange_start(f"scatter_start_{name}", [grad], me, "scatter")
        return token

    groups = [["conv_in", "conv_w", "conv_out"], ["gate_up_0", "down_0"], ["kv", "q"], ["o", "gate_up_1", "down_1"]]
    n0 = _rms_fwd("mix_prenorm_0", h0, [row(mix_norm_pre, 0)])[0]
    w = finish(groups[0], direct(groups[0], n0), n0)
    win = w["conv_in"].transpose(1, 0, 2).reshape(D, 3 * D)
    cw = w["conv_w"].transpose(1, 0, 2).reshape(8, D)
    wout = w["conv_out"].reshape(D, D)
    z = _fwd_rows("conv_in", n0, win, BF16)
    pre = _conv_fwd("conv_gate", z, cw)
    relayed = direct(groups[1], pre)
    y0 = _fwd_rows("conv_out", pre, wout)
    h1, (n1,) = _resid_rms("mix_postnorm_0", h0, y0, row(mix_norm_post, 0), [row(ffn_norm_pre, 0)])
    w = finish(groups[1], relayed, n1)
    ffn0, f0 = _ffn_fwd("0", n1, w["gate_up_0"], w["down_0"])
    relayed = direct(groups[2], ffn0[2])
    h2, (nk, n2) = _resid_rms("ffn_postnorm_0", h1, f0, row(ffn_norm_post, 0), [g_kv, row(mix_norm_pre, 1)])

    w = finish(groups[2], relayed, nk)
    wkv = w["kv"].transpose(1, 0, 2).reshape(D, 2 * QW)
    wq = w["q"].transpose(1, 0, 2).reshape(D, QW)
    half = HEAD_DIM // 2
    inv_freq = ROPE_THETA ** (-jnp.arange(half, dtype=F32) / half)
    tables = _rope_tables("rope_tables", positions.reshape(S, 1), jnp.tile(inv_freq, 4).reshape(1, 128))
    qc, kc, vc, o_c, lse_c = [], [], [], [], []
    for g, d in enumerate(DILATIONS):
        kc.append(_proj_classes(f"k_proj_{g}", nk, wkv, g, d, tables, 1.0))
        vc.append(_proj_classes(f"v_proj_{g}", nk, wkv, len(DILATIONS) + g, d, None, None))
    relayed = direct(groups[3], vc[-1])
    for g, d in enumerate(DILATIONS):
        qc.append(_proj_classes(f"q_proj_{g}", n2, wq, g, d, tables, HEAD_DIM ** -0.5))
        o_g, lse_g = _attn_fwd(f"attn_fwd_{g}", qc[g], kc[g], vc[g], d)
        o_c.append(o_g)
        lse_c.append(lse_g)
    o_mix = _mix_fwd("attn_mix", o_c, lse_c)
    w = finish(groups[3], relayed, o_mix)
    wo = w["o"].reshape(D, D)
    y1 = _fwd_rows("attn_out", o_mix, wo)
    h3, (n3,) = _resid_rms("mix_postnorm_1", h2, y1, row(mix_norm_post, 1), [row(ffn_norm_pre, 1)])
    ffn1, f1 = _ffn_fwd("1", n3, w["gate_up_1"], w["down_1"])

    dh4, df1, dg_fpost1, sq = _resid_rms_loss("ffn_postnorm_1_loss", h3, f1, row(ffn_norm_post, 1), target)

    dh3, dg_fpre1, dy1, dg_mpost1 = _ffn_bwd(
        "1", dh4, df1, h3, ffn1, row(ffn_norm_pre, 1), send, (y1, row(mix_norm_post, 1)))
    tok = send("o", _bwd_w_rows("attn_out_dw", o_mix, dy1).reshape(NDEV, D // NDEV, D))
    do = _bwd_x_rows("attn_out_dx", dy1, wo, F32, after=tok)
    lane = jnp.arange(128)
    ones_blockdiag = (lane[:, None] // HEAD_DIM == lane[None, :] // HEAD_DIM).astype(BF16)
    mixed = _mix_bwd("attn_mix_bwd", do, o_c, lse_c, ones_blockdiag)
    branch_grads = [_attn_bwd(f"attn_bwd_{g}", qc[g], kc[g], vc[g], mixed[g], lse_c[g], mixed[3 + g], d)
                    for g, d in enumerate(DILATIONS)]
    dq_raw, dkv = _attn_bwd_post("attn_bwd_post", branch_grads, *tables)
    tok = send("kv", _bwd_w_cols("kv_proj_dw", nk, dkv, 2 * QW // NDEV))
    dnk = _bwd_x_plain("kv_proj_dx", dkv, wkv, after=tok)
    tok = send("q", _bwd_w_cols("q_proj_dw", n2, dq_raw, QW // NDEV))
    dn2 = _bwd_x_plain("q_proj_dx", dq_raw, wq, after=tok)
    dh2, (dg_kv, dg_mpre1), df0, dg_fpost0 = _rms_bwd(
        "kv_and_mix_prenorm_bwd_1", h2, [(g_kv, dnk), (row(mix_norm_pre, 1), dn2)], dh3, F32,
        then=(f0, row(ffn_norm_post, 0)))

    dh1, dg_fpre0, dy0, dg_mpost0 = _ffn_bwd(
        "0", dh2, df0, h1, ffn0, row(ffn_norm_pre, 0), send, (y0, row(mix_norm_post, 0)))
    tok = send("conv_out", _bwd_w_rows("conv_out_dw", pre, dy0).reshape(NDEV, D // NDEV, D))
    dpre = _bwd_x_rows("conv_out_dx", dy0, wout, BF16, after=tok)
    dz, dcw = _conv_bwd("conv_gate_bwd", z, dpre, cw)
    tok = send("conv_in", _bwd_w_cols("conv_in_dw", n0, dz, 3 * D // NDEV))
    dn0 = _bwd_x_plain("conv_in_dx", dz, win, after=tok)
    dh0, (dg_mpre0,) = _rms_bwd("mix_prenorm_bwd_0", h0, [(row(mix_norm_pre, 0), dn0)], dh1, F32)

    small = _pack_small("pack_small_grads", [dg_mpre0, dg_mpre1, dg_mpost0, dg_mpost1, dg_fpre0, dg_fpre1,
                                             dg_fpost0, dg_fpost1, dg_kv], dcw, sq)
    small_all = _exchange("gather_small_grads", [small], "gather")[0]

    done = [small_all]

    def upd(tag, w, m, v):
        parts = _exchange_wait(f"scatter_wait_{tag}", sent[tag], 0, done[-1], "scatter")
        shape = w.shape
        flat = lambda a: a.reshape(parts.shape[1:])
        res = _adamw(f"adamw_{tag}", parts, flat(w), flat(m), flat(v))
        done.append(res[0])
        return [r.reshape(shape) for r in res]

    def upd_layer(tag, l, w, m, v, other):
        parts = _exchange_wait(f"scatter_wait_{tag}_{l}", sent[f"{tag}_{l}"], 0, done[-1], "scatter")
        res = _adamw(f"adamw_{tag}_{l}", parts, w, m, v, layer=l, other=other)
        done.append(res[0])
        return list(res)

    vec = lambda a: a.reshape(1, D)
    gain_res, taps, loss = _adamw_gains("adamw_gains", small_all, [
        (mix_norm_pre, m_mix_norm_pre, v_mix_norm_pre), (mix_norm_post, m_mix_norm_post, v_mix_norm_post),
        (ffn_norm_pre, m_ffn_norm_pre, v_ffn_norm_pre), (ffn_norm_post, m_ffn_norm_post, v_ffn_norm_post),
        (vec(kv_norm), vec(m_kv_norm), vec(v_kv_norm))])
    dcw_mine = lax.dynamic_slice(taps, (0, me * 128), (8, 128))
    pad8 = lambda a, fill: jnp.pad(a[0], ((0, 5), (0, 0)), constant_values=fill)
    cw_res = [r[0:3].reshape(1, 3, 128) for r in
              _adamw("adamw_conv_w", dcw_mine.reshape(1, 8, 128), cw_shard, pad8(m_conv_w, 0.0), pad8(v_conv_w, 1.0))]

    res = {
        "mix_norm_pre": gain_res[0],
        "mix_norm_post": gain_res[1],
        "ffn_norm_pre": gain_res[2],
        "ffn_norm_post": gain_res[3],
        "kv_norm": [r.reshape(D) for r in gain_res[4]],
        "conv_w": cw_res,
    }
    down_1 = upd_layer("down", 1, ffn_w_down, m_ffn_w_down, v_ffn_w_down, None)
    gate_up_t = [jnp.swapaxes(a, 1, 2) for a in (ffn_w_gate_up, m_ffn_w_gate_up, v_ffn_w_gate_up)]
    gate_up_1 = upd_layer("gate_up", 1, *gate_up_t, None)
    res["w_o"] = upd("o", w_o, m_w_o, v_w_o)
    res["w_q"] = upd("q", w_q, m_w_q, v_w_q)
    res["w_kv"] = upd("kv", w_kv, m_w_kv, v_w_kv)
    res["ffn_w_down"] = upd_layer("down", 0, ffn_w_down, m_ffn_w_down, v_ffn_w_down, down_1)
    res["ffn_w_gate_up"] = [jnp.swapaxes(r, 1, 2) for r in upd_layer("gate_up", 0, *gate_up_t, gate_up_1)]
    res["conv_w_out"] = upd("conv_out", conv_w_out, m_conv_w_out, v_conv_w_out)
    res["conv_w_in"] = upd("conv_in", conv_w_in, m_conv_w_in, v_conv_w_in)
    order = ["mix_norm_pre", "mix_norm_post", "ffn_norm_pre", "ffn_norm_post", "ffn_w_gate_up", "ffn_w_down",
             "conv_w_in", "conv_w", "conv_w_out", "kv_norm", "w_kv", "w_q", "w_o"]
    out = [loss, dh0.reshape(1, S, D)]
    for i in range(4):
        out += [res[name][i] for name in order]
    return tuple(out)
```

```python
import jax
import jax.numpy as jnp
from jax import lax
from jax.experimental import pallas as pl
from jax.experimental.pallas import tpu as pltpu

F32 = jnp.float32
BF16 = jnp.bfloat16

S = 4096
D = 1024
NDEV = 8
HEAD_DIM = 64
QW = 3072
DFF = 2816
FB = 704
NFB = 4
BRANCHES = ((128, 1), (512, 4), (2048, 16))
BAND = 128
ROPE_THETA = 10000.0
RMS_EPS = 1e-6
NEG_INF = -1e30
ADAM_LR, ADAM_B1, ADAM_B2, ADAM_EPS, ADAM_WD, ADAM_STEP = 0.001, 0.9, 0.999, 1e-08, 0.01, 10

VMEM_LIMIT_BYTES = 52 * 1024 * 1024
ROW_TILE = 512
MESH = pl.DeviceIdType.MESH


def _cparams(ngrid):
    return pltpu.CompilerParams(dimension_semantics=("arbitrary",) * ngrid,
                                vmem_limit_bytes=VMEM_LIMIT_BYTES)


def _sds(shape, dtype):
    return jax.ShapeDtypeStruct(tuple(shape), dtype)


_DIMS = {"nn": (((1,), (0,)), ((), ())),
         "nt": (((1,), (1,)), ((), ())),
         "tn": (((0,), (0,)), ((), ()))}


def _matmul(name, a, b, *, mode, grid, a_blk, a_map, b_blk, b_map, o_shape, o_blk, o_map, out_dtype, after=None,
            out_groups=1):
    nk = grid[2]
    dims = _DIMS[mode]
    acc_shape = tuple(s for s in o_blk if s is not None)
    if out_groups > 1:
        acc_shape = (acc_shape[1], out_groups * acc_shape[2])
    extra = [] if after is None else [after]

    def store(o_ref, val):
        if out_groups == 1:
            o_ref[...] = val.astype(o_ref.dtype)
        else:
            n = o_ref.shape[-1]
            for grp in range(out_groups):
                o_ref[grp] = val[:, grp * n:(grp + 1) * n].astype(o_ref.dtype)

    def body(a_ref, b_ref, *rest):
        o_ref, scratch = rest[len(extra)], rest[len(extra) + 1:]
        part = lax.dot_general(a_ref[...], b_ref[...], dims, preferred_element_type=F32)
        if nk == 1:
            store(o_ref, part)
            return
        acc_ref = scratch[0]
        k = pl.program_id(2)

        @pl.when(k == 0)
        def _():
            acc_ref[...] = part

        @pl.when(k > 0)
        def _():
            acc_ref[...] += part

        @pl.when(k == nk - 1)
        def _():
            store(o_ref, acc_ref[...])

    return pl.pallas_call(
        body, name=name, grid=grid,
        in_specs=[pl.BlockSpec(a_blk, a_map), pl.BlockSpec(b_blk, b_map)] + [pl.BlockSpec(memory_space=pl.ANY)] * len(extra),
        out_specs=pl.BlockSpec(o_blk, o_map),
        out_shape=_sds(o_shape, out_dtype),
        scratch_shapes=[] if nk == 1 else [pltpu.VMEM(acc_shape, F32)],
        compiler_params=_cparams(3),
    )(a, b, *extra)


TM = 1024
TK = S


def _fwd_rows(name, a, w, out_dtype=F32):
    kdim, n = w.shape
    tn = 512
    return _matmul(name, a, w, mode="nn", grid=(S // TM, n // tn, 1),
                   a_blk=(TM, kdim), a_map=lambda i, j, k: (i, 0),
                   b_blk=(kdim, tn), b_map=lambda i, j, k: (0, j),
                   o_shape=(S, n), o_blk=(TM, tn), o_map=lambda i, j, k: (i, j), out_dtype=out_dtype)


def _fwd_kblocked(name, a4, w4):
    nb, _, kb = a4.shape
    n = w4.shape[2]

    def body(a_ref, w_ref, o_ref):
        acc = _dot_nn(a_ref[0], w_ref[0])
        for j in range(1, nb):
            acc = acc + _dot_nn(a_ref[j], w_ref[j])
        o_ref[...] = acc

    return pl.pallas_call(
        body, name=name, grid=(S // TM,),
        in_specs=[pl.BlockSpec((nb, TM, kb), lambda i: (0, i, 0)), pl.BlockSpec((nb, kb, n), lambda i: (0, 0, 0))],
        out_specs=pl.BlockSpec((TM, n), lambda i: (i, 0)), out_shape=_sds((S, n), F32),
        compiler_params=_cparams(1),
    )(a4, w4)


def _bwd_x_cols_blocked(name, dy8, wg, after):
    _, kdim, n = wg.shape
    nk = NDEV // 2

    def body(a_ref, b_ref, after_ref, o_ref, acc_ref):
        k = pl.program_id(1)
        part = _dot_nt(a_ref[0], b_ref[0]) + _dot_nt(a_ref[1], b_ref[1])

        @pl.when(k == 0)
        def _():
            acc_ref[...] = part

        @pl.when(k > 0)
        def _():
            acc_ref[...] += part

        @pl.when(k == nk - 1)
        def _():
            o_ref[...] = acc_ref[...]

    return pl.pallas_call(
        body, name=name, grid=(S // TM, nk),
        in_specs=[pl.BlockSpec((2, None, TM, n), lambda i, k: (0, k, i, 0)),
                  pl.BlockSpec((2, None, kdim, n), lambda i, k: (0, k, 0, 0)),
                  pl.BlockSpec(memory_space=pl.ANY)],
        out_specs=pl.BlockSpec((TM, kdim), lambda i, k: (i, 0)), out_shape=_sds((S, kdim), F32),
        scratch_shapes=[pltpu.VMEM((TM, kdim), F32)],
        compiler_params=_cparams(2),
    )(dy8.reshape(2, nk, S, n), wg.reshape(2, nk, kdim, n), after)


def _bwd_x_rows(name, dy, w, out_dtype, after=None):
    kdim, n = w.shape
    tkk = 512
    return _matmul(name, dy, w, mode="nt", grid=(S // TM, kdim // tkk, 1),
                   a_blk=(TM, n), a_map=lambda i, j, k: (i, 0),
                   b_blk=(tkk, n), b_map=lambda i, j, k: (j, 0),
                   o_shape=(S, kdim), o_blk=(TM, tkk), o_map=lambda i, j, k: (i, j), out_dtype=out_dtype, after=after)


DW_COLS = 768


def _bwd_w_cols(name, a, dy, n):
    kdim = a.shape[1]
    groups = DW_COLS // n
    return _matmul(name, a, dy, mode="tn", grid=(1, NDEV // groups, S // TK),
                   a_blk=(TK, kdim), a_map=lambda i, j, k: (k, 0),
                   b_blk=(TK, DW_COLS), b_map=lambda i, j, k: (k, j),
                   o_shape=(NDEV, kdim, n), o_blk=(groups, kdim, n) if groups > 1 else (None, kdim, n),
                   o_map=lambda i, j, k: (j, 0, 0), out_dtype=BF16, out_groups=groups)


def _bwd_x_plain(name, dy, w, after=None):
    kdim, n = w.shape
    tm = TM if n <= 3 * D else TM // 2
    return _matmul(name, dy, w, mode="nt", grid=(S // tm, 1, 1),
                   a_blk=(tm, n), a_map=lambda i, j, k: (i, 0),
                   b_blk=(kdim, n), b_map=lambda i, j, k: (0, 0),
                   o_shape=(S, kdim), o_blk=(tm, kdim), o_map=lambda i, j, k: (i, 0), out_dtype=F32, after=after)


def _bwd_w_cols_blocked(name, a, dy8):
    kdim = a.shape[1]
    n = dy8.shape[2]
    return _matmul(name, dy8, a, mode="tn", grid=(1, NDEV, S // TK),
                   a_blk=(None, TK, n), a_map=lambda i, j, k: (j, k, 0),
                   b_blk=(TK, kdim), b_map=lambda i, j, k: (k, 0),
                   o_shape=(NDEV, n, kdim), o_blk=(None, n, kdim), o_map=lambda i, j, k: (j, 0, 0), out_dtype=BF16)


def _bwd_w_rows(name, a, dy):
    kdim = a.shape[1]
    n = dy.shape[1]
    tmm = 512
    return _matmul(name, a, dy, mode="tn", grid=(kdim // tmm, 1, S // TK),
                   a_blk=(TK, tmm), a_map=lambda i, j, k: (k, i),
                   b_blk=(TK, n), b_map=lambda i, j, k: (k, 0),
                   o_shape=(kdim, n), o_blk=(tmm, n), o_map=lambda i, j, k: (i, 0), out_dtype=BF16)


def _bwd_w_kblocked(name, a4, dy):
    nb, _, kb = a4.shape
    n = dy.shape[1]
    return _matmul(name, a4, dy, mode="tn", grid=(nb, 1, S // TK),
                   a_blk=(None, TK, kb), a_map=lambda i, j, k: (i, k, 0),
                   b_blk=(TK, n), b_map=lambda i, j, k: (k, 0),
                   o_shape=(nb, kb, n), o_blk=(None, kb, n), o_map=lambda i, j, k: (i, 0, 0), out_dtype=BF16)


def _rstd(x):
    return lax.rsqrt(jnp.mean(x * x, axis=-1, keepdims=True) + RMS_EPS)


def _row_spec(tm=ROW_TILE, width=D):
    return pl.BlockSpec((tm, width), lambda i: (i, 0))


def _vec_spec(rows=1, width=D):
    return pl.BlockSpec((rows, width), lambda i: (0, 0))


def _rms_fwd(name, x, gains):
    n = len(gains)

    def body(x_ref, *refs):
        x_val = x_ref[...]
        xh = x_val * _rstd(x_val)
        for g_ref, o_ref in zip(refs[:n], refs[n:]):
            o_ref[...] = (xh * g_ref[...]).astype(o_ref.dtype)

    outs = pl.pallas_call(
        body, name=name, grid=(S // ROW_TILE,),
        in_specs=[_row_spec()] + [_vec_spec()] * n,
        out_specs=[_row_spec()] * n,
        out_shape=[_sds((S, D), BF16)] * n,
        compiler_params=_cparams(1),
    )(x, *gains)
    return list(outs)


def _resid_rms(name, h, y, g, next_gains):
    n = len(next_gains)

    def body(h_ref, y_ref, g_ref, *refs):
        y_val = y_ref[...]
        h_new = h_ref[...] + (y_val * _rstd(y_val)) * g_ref[...]
        refs[n][...] = h_new
        hh = h_new * _rstd(h_new)
        for g2_ref, o_ref in zip(refs[:n], refs[n + 1:]):
            o_ref[...] = (hh * g2_ref[...]).astype(o_ref.dtype)

    outs = pl.pallas_call(
        body, name=name, grid=(S // ROW_TILE,),
        in_specs=[_row_spec(), _row_spec(), _vec_spec()] + [_vec_spec()] * n,
        out_specs=[_row_spec()] * (n + 1), out_shape=[_sds((S, D), F32)] + [_sds((S, D), BF16)] * n,
        compiler_params=_cparams(1),
    )(h, y, g, *next_gains)
    return outs[0], list(outs[1:])


def _resid_rms_loss(name, h, y, g, target):
    def body(h_ref, y_ref, g_ref, t_ref, dh_ref, dy_ref, dg_ref, part_ref):
        y_val = y_ref[...]
        gain = g_ref[...]
        e = h_ref[...] + (y_val * _rstd(y_val)) * gain - t_ref[...]
        dh = e * (1.0 / D)
        dh_ref[...] = dh
        step = pl.program_id(0)
        dy_ref[...] = _norm_bwd_rows(y_val, gain, dh, dg_ref, step).astype(dy_ref.dtype)
        part = jnp.sum(e * e, axis=0, keepdims=True)

        @pl.when(step == 0)
        def _():
            part_ref[...] = part

        @pl.when(step > 0)
        def _():
            part_ref[...] += part

    return pl.pallas_call(
        body, name=name, grid=(S // ROW_TILE,),
        in_specs=[_row_spec(), _row_spec(), _vec_spec(), _row_spec()],
        out_specs=[_row_spec(), _row_spec(), _vec_spec(8), _vec_spec()],
        out_shape=[_sds((S, D), F32), _sds((S, D), BF16), _sds((8, D), F32), _sds((1, D), F32)],
        compiler_params=_cparams(1),
    )(h, y, g, target)


def _norm_bwd_rows(x_val, g, dn, dg_ref, step):
    r = _rstd(x_val)
    xh = x_val * r
    dxh = dn * g
    part = jnp.sum(dn * xh, axis=0, keepdims=True)

    @pl.when(step == 0)
    def _():
        dg_ref[...] = jnp.zeros_like(dg_ref)

    dg_ref[0:1, :] += part
    return r * (dxh - xh * jnp.mean(dxh * xh, axis=-1, keepdims=True))


def _rms_bwd(name, x, pairs, dres, out_dtype, then=None):
    n = len(pairs)
    has_res = dres is not None
    chained = then is not None

    def body(x_ref, *refs):
        g_refs = refs[0:2 * n:2]
        dn_refs = refs[1:2 * n:2]
        pos = 2 * n
        res_ref = refs[pos] if has_res else None
        pos += int(has_res)
        if chained:
            y_ref, gy_ref = refs[pos], refs[pos + 1]
            pos += 2
        dx_ref = refs[pos]
        dg_refs = refs[pos + 1:pos + 1 + n]
        step = pl.program_id(0)
        x_val = x_ref[...]
        acc = res_ref[...] if has_res else jnp.zeros_like(x_val)
        for g_ref, dn_ref, dg_ref in zip(g_refs, dn_refs, dg_refs):
            acc = acc + _norm_bwd_rows(x_val, g_ref[...], dn_ref[...].astype(F32), dg_ref, step)
        dx_ref[...] = acc.astype(dx_ref.dtype)
        if chained:
            dy_ref, dgy_ref = refs[pos + 1 + n], refs[pos + 2 + n]
            dy_ref[...] = _norm_bwd_rows(y_ref[...], gy_ref[...], acc, dgy_ref, step).astype(dy_ref.dtype)

    operands = [x]
    in_specs = [_row_spec()]
    for g, dn in pairs:
        operands += [g, dn]
        in_specs += [_vec_spec(), _row_spec()]
    if has_res:
        operands.append(dres)
        in_specs.append(_row_spec())
    if chained:
        operands += [then[0], then[1]]
        in_specs += [_row_spec(), _vec_spec()]
    extra = int(chained)
    outs = pl.pallas_call(
        body, name=name, grid=(S // ROW_TILE,),
        in_specs=in_specs,
        out_specs=[_row_spec()] + [_vec_spec(8)] * n + [_row_spec(), _vec_spec(8)] * extra,
        out_shape=[_sds((S, D), out_dtype)] + [_sds((8, D), F32)] * n + [_sds((S, D), BF16), _sds((8, D), F32)] * extra,
        compiler_params=_cparams(1),
    )(*operands)
    if chained:
        return outs[0], list(outs[1:1 + n]), outs[1 + n], outs[2 + n]
    return outs[0], list(outs[1:])


def _shift_down(u, prev8, k):
    r = pltpu.roll(u, k, 0)
    p = pltpu.roll(prev8, k, 0)
    row = lax.broadcasted_iota(jnp.int32, prev8.shape, 0)
    top = jnp.where(row < k, p, r[0:8])
    return jnp.concatenate([top, r[8:]], axis=0)


def _shift_up(u, next8, k):
    tm = u.shape[0]
    r = pltpu.roll(u, tm - k, 0)
    p = pltpu.roll(next8, 8 - k, 0)
    row = lax.broadcasted_iota(jnp.int32, next8.shape, 0)
    bot = jnp.where(row >= 8 - k, p, r[tm - 8:tm])
    return jnp.concatenate([r[:tm - 8], bot], axis=0)


CONV_TILE = 512


def _halo_prev(col):
    return pl.BlockSpec((8, D), lambda i: (jnp.maximum(i * (CONV_TILE // 8) - 1, 0), col))


def _halo_next(col):
    last = S // 8 - 1
    return pl.BlockSpec((8, D), lambda i: (jnp.minimum((i + 1) * (CONV_TILE // 8), last), col))


def _conv_fwd(name, z, cw):
    def body(b_ref, c_ref, h_ref, cp_ref, hp_ref, cw_ref, o_ref):
        i = pl.program_id(0)
        u = c_ref[...].astype(F32) * h_ref[...].astype(F32)
        up = cp_ref[...].astype(F32) * hp_ref[...].astype(F32)
        up = jnp.where(i > 0, up, 0.0)
        cv = cw_ref[0:1, :] * _shift_down(u, up, 2) + cw_ref[1:2, :] * _shift_down(u, up, 1) + cw_ref[2:3, :] * u
        o_ref[...] = (b_ref[...].astype(F32) * cv).astype(o_ref.dtype)

    col = lambda c: pl.BlockSpec((CONV_TILE, D), lambda i: (i, c))
    return pl.pallas_call(
        body, name=name, grid=(S // CONV_TILE,),
        in_specs=[col(0), col(1), col(2), _halo_prev(1), _halo_prev(2), _vec_spec(8)],
        out_specs=_row_spec(CONV_TILE), out_shape=_sds((S, D), BF16),
        compiler_params=_cparams(1),
    )(z, z, z, z, z, cw)


def _conv_bwd(name, z, dpre, cw):
    nsteps = S // CONV_TILE

    def body(b_ref, c_ref, h_ref, cp_ref, hp_ref, dp_ref, dpn_ref, bn_ref, cw_ref, dz_ref, dcw_ref):
        i = pl.program_id(0)
        b = b_ref[...].astype(F32)
        c = c_ref[...].astype(F32)
        h = h_ref[...].astype(F32)
        dp = dp_ref[...].astype(F32)
        u = c * h
        up = jnp.where(i > 0, cp_ref[...].astype(F32) * hp_ref[...].astype(F32), 0.0)
        s1 = _shift_down(u, up, 1)
        s2 = _shift_down(u, up, 2)
        w0, w1, w2 = cw_ref[0:1, :], cw_ref[1:2, :], cw_ref[2:3, :]
        cv = w0 * s2 + w1 * s1 + w2 * u
        dcv = dp * b
        dcvn = jnp.where(i < nsteps - 1, dpn_ref[...].astype(F32) * bn_ref[...].astype(F32), 0.0)
        du = w2 * dcv + w1 * _shift_up(dcv, dcvn, 1) + w0 * _shift_up(dcv, dcvn, 2)
        dz_ref[:, 0:D] = (dp * cv).astype(dz_ref.dtype)
        dz_ref[:, D:2 * D] = (du * h).astype(dz_ref.dtype)
        dz_ref[:, 2 * D:3 * D] = (du * c).astype(dz_ref.dtype)

        @pl.when(i == 0)
        def _():
            dcw_ref[...] = jnp.zeros_like(dcw_ref)

        dcw_ref[0:1, :] += jnp.sum(dcv * s2, axis=0, keepdims=True)
        dcw_ref[1:2, :] += jnp.sum(dcv * s1, axis=0, keepdims=True)
        dcw_ref[2:3, :] += jnp.sum(dcv * u, axis=0, keepdims=True)

    col = lambda c: pl.BlockSpec((CONV_TILE, D), lambda i: (i, c))
    return pl.pallas_call(
        body, name=name, grid=(nsteps,),
        in_specs=[col(0), col(1), col(2), _halo_prev(1), _halo_prev(2),
                  _row_spec(CONV_TILE), _halo_next(0), _halo_next(0), _vec_spec(8)],
        out_specs=[pl.BlockSpec((CONV_TILE, 3 * D), lambda i: (i, 0)), _vec_spec(8)],
        out_shape=[_sds((S, 3 * D), BF16), _sds((8, D), F32)],
        compiler_params=_cparams(1),
    )(z, z, z, z, z, dpre, dpre, z, cw)


FFN_TM = 2048
_GU_BLOCK = pl.BlockSpec((2, None, FFN_TM, FB), lambda i, j: (0, j, i, 0))


def _gate_up_act(name, a, wg):
    kdim = a.shape[1]

    def body(a_ref, wgate_ref, wup_ref, gu_ref, act_ref):
        x = a_ref[...]
        g = _dot_nn(x, wgate_ref[...])
        u = _dot_nn(x, wup_ref[...])
        gu_ref[0] = g.astype(gu_ref.dtype)
        gu_ref[1] = u.astype(gu_ref.dtype)
        act_ref[...] = (g * jax.nn.sigmoid(g) * u).astype(act_ref.dtype)

    return pl.pallas_call(
        body, name=name, grid=(S // FFN_TM, NFB),
        in_specs=[pl.BlockSpec((FFN_TM, kdim), lambda i, j: (i, 0)),
                  pl.BlockSpec((None, kdim, FB), lambda i, j: (j, 0, 0)),
                  pl.BlockSpec((None, kdim, FB), lambda i, j: (j + NFB, 0, 0))],
        out_specs=[_GU_BLOCK, pl.BlockSpec((None, FFN_TM, FB), lambda i, j: (j, i, 0))],
        out_shape=[_sds((2, NFB, S, FB), BF16), _sds((NFB, S, FB), BF16)],
        compiler_params=_cparams(2),
    )(a, wg, wg)


def _down_dx_act_bwd(name, df, w4, gu):
    _, kb, n = w4.shape

    def body(df_ref, w_ref, gu_ref, o_ref):
        d = _dot_nt(df_ref[...], w_ref[...])
        g = gu_ref[0].astype(F32)
        u = gu_ref[1].astype(F32)
        sg = jax.nn.sigmoid(g)
        o_ref[0] = (d * u * sg * (1.0 + g * (1.0 - sg))).astype(o_ref.dtype)
        o_ref[1] = (d * g * sg).astype(o_ref.dtype)

    return pl.pallas_call(
        body, name=name, grid=(S // FFN_TM, NFB),
        in_specs=[pl.BlockSpec((FFN_TM, n), lambda i, j: (i, 0)), pl.BlockSpec((None, kb, n), lambda i, j: (j, 0, 0)),
                  _GU_BLOCK],
        out_specs=_GU_BLOCK, out_shape=_sds((2, NFB, S, FB), BF16),
        compiler_params=_cparams(2),
    )(df, w4, gu)


def _rope_tables(name, pos_col, inv_freq_row):
    def body(pos_ref, f_ref, cos_ref, sin_ref):
        ang = pos_ref[...].astype(F32) * f_ref[...]
        lane = lax.broadcasted_iota(jnp.int32, ang.shape, 1)
        s = jnp.sin(ang)
        cos_ref[...] = jnp.cos(ang)
        sin_ref[...] = jnp.where((lane % HEAD_DIM) < HEAD_DIM // 2, -s, s)

    tab = pl.BlockSpec((ROW_TILE, 128), lambda i: (i, 0))
    return pl.pallas_call(
        body, name=name, grid=(S // ROW_TILE,),
        in_specs=[pl.BlockSpec((ROW_TILE, 1), lambda i: (i, 0)), _vec_spec(1, 128)],
        out_specs=[tab, tab], out_shape=[_sds((S, 128), F32)] * 2,
        compiler_params=_cparams(1),
    )(pos_col, inv_freq_row)


def _swap_halves(t):
    lane = lax.broadcasted_iota(jnp.int32, t.shape, 1)
    first = (lane % HEAD_DIM) < HEAD_DIM // 2
    return jnp.where(first, pltpu.roll(t, 128 - HEAD_DIM // 2, 1), pltpu.roll(t, HEAD_DIM // 2, 1))


NCHUNK = D // 128


def _chunk(c, base=0):
    return slice(base + c * 128, base + (c + 1) * 128)


def _class_rows(r, d, tm):
    return pl.ds(r, tm // d, stride=d) if d > 1 else slice(None)


def _class_block(d, tm):
    return pl.BlockSpec((tm // d, d * D), lambda i: (i, 0))


def _tokens_from_classes(blk_ref, tmp_ref, d, tm):
    for r in range(d):
        for c in range(NCHUNK):
            tmp_ref[c, _class_rows(r, d, tm), :] = blk_ref[:, _chunk(c, r * D)].astype(F32)


def _classes_from_tokens(tmp_ref, blk_ref, d, tm):
    for r in range(d):
        for c in range(NCHUNK):
            blk_ref[:, _chunk(c, r * D)] = tmp_ref[c, _class_rows(r, d, tm), :].astype(blk_ref.dtype)


def _proj_classes(name, a, w, col, d, tables, scale):
    kdim = a.shape[1]
    rope = tables is not None

    def body(a_ref, w_ref, *refs):
        if rope:
            cos_ref, sin_ref, o_ref, tmp_ref = refs
        else:
            o_ref, tmp_ref = refs
        acc = _dot_nn(a_ref[...], w_ref[...])
        for c in range(NCHUNK):
            tmp_ref[c] = acc[:, _chunk(c)]
        for r in range(d):
            rows = _class_rows(r, d, TM)
            if rope:
                cs = cos_ref[rows, :]
                sn = sin_ref[rows, :]
            for c in range(NCHUNK):
                x = tmp_ref[c, rows, :]
                if rope:
                    x = (x * cs + _swap_halves(x) * sn) * scale
                o_ref[:, _chunk(c, r * D)] = x.astype(o_ref.dtype)

    tab = pl.BlockSpec((TM, 128), lambda i: (i, 0))
    return pl.pallas_call(
        body, name=name, grid=(S // TM,),
        in_specs=[pl.BlockSpec((TM, kdim), lambda i: (i, 0)), pl.BlockSpec((kdim, D), lambda i: (0, col))]
                 + ([tab, tab] if rope else []),
        out_specs=_class_block(d, TM), out_shape=_sds((S // d, d * D), BF16),
        scratch_shapes=[pltpu.VMEM((NCHUNK, TM, 128), F32)],
        compiler_params=_cparams(1),
    )(a, w, *(tables if rope else ()))


ATTN_CHAINS = 8


def _attn_units(d):
    nblk = S // d // BAND
    return max(1, 2 * ATTN_CHAINS // nblk)


def _class_spec(d):
    return pl.BlockSpec((S // d, 128 * _attn_units(d)), lambda cb: (0, cb))


def _dot_nt(a, b):
    return lax.dot_general(a, b, _DIMS["nt"], preferred_element_type=F32)


def _dot_tn(a, b):
    return lax.dot_general(a, b, _DIMS["tn"], preferred_element_type=F32)


def _dot_nn(a, b):
    return lax.dot_general(a, b, _DIMS["nn"], preferred_element_type=F32)


def _band_mask(nkeys):
    qi = lax.broadcasted_iota(jnp.int32, (2 * BAND, nkeys), 0) % BAND
    kj = lax.broadcasted_iota(jnp.int32, (2 * BAND, nkeys), 1)
    if nkeys == BAND:
        return kj <= qi
    dist = qi + BAND - kj
    return (dist >= 0) & (dist <= BAND)


def _stack_heads(x):
    row = lax.broadcasted_iota(jnp.int32, (2 * BAND, 128), 0)
    lane = lax.broadcasted_iota(jnp.int32, (2 * BAND, 128), 1)
    keep = (row < BAND) == (lane < HEAD_DIM)
    return jnp.where(keep, jnp.concatenate([x, x], axis=0), jnp.zeros((), x.dtype))


def _unstack(x2):
    first_head = lax.broadcasted_iota(jnp.int32, (BAND, 128), 1) < HEAD_DIM
    return jnp.where(first_head, x2[:BAND], x2[BAND:])


def _for_later_blocks(nblk, units, fn):
    all_lanes = [slice(u * 128, (u + 1) * 128) for u in range(units)]
    unroll = max(1, ATTN_CHAINS // units)
    trips = (nblk - 1) // unroll
    if trips > 1:
        def step(i, carry):
            for j in range(unroll):
                for lanes in all_lanes:
                    fn(pl.multiple_of((1 + i * unroll + j) * BAND, BAND), lanes)
            return carry

        lax.fori_loop(0, trips, step, 0)
    else:
        trips = 0
    for sb in range(1 + trips * unroll, nblk):
        for lanes in all_lanes:
            fn(sb * BAND, lanes)


def _attn_fwd(name, q, k, v, d):
    nblk = S // d // BAND
    units = _attn_units(d)

    def body(q_ref, k_ref, v_ref, o_ref, lse_ref):
        def block(r0, k0, nkeys, lanes):
            q2 = _stack_heads(q_ref[pl.ds(r0, BAND), lanes])
            s = jnp.where(_band_mask(nkeys), _dot_nt(q2, k_ref[pl.ds(k0, nkeys), lanes]), NEG_INF)
            m = jnp.max(s, axis=-1, keepdims=True)
            p = jnp.exp(s - m)
            l = jnp.sum(p, axis=-1, keepdims=True)
            o2 = _dot_nn(p.astype(BF16), v_ref[pl.ds(k0, nkeys), lanes]) / l
            lse2 = jnp.broadcast_to(m + jnp.log(l), (2 * BAND, 128))
            o_ref[pl.ds(r0, BAND), lanes] = _unstack(o2).astype(o_ref.dtype)
            lse_ref[pl.ds(r0, BAND), lanes] = _unstack(lse2)

        for u in range(units):
            block(0, 0, BAND, slice(u * 128, (u + 1) * 128))

        _for_later_blocks(nblk, units, lambda r0, lanes: block(r0, r0 - BAND, 2 * BAND, lanes))

    spec = _class_spec(d)
    return pl.pallas_call(
        body, name=name, grid=(8 * d // units,),
        in_specs=[spec] * 3, out_specs=[spec] * 2,
        out_shape=[_sds((S // d, d * D), BF16), _sds((S // d, d * D), F32)],
        compiler_params=_cparams(1),
    )(q, k, v)


def _attn_bwd(name, q, k, v, do, lse, dd, d):
    nblk = S // d // BAND
    units = _attn_units(d)

    def body(q_ref, k_ref, v_ref, do_ref, lse_ref, dd_ref, dq_ref, dk_out, dv_out, dk_ref, dv_ref):
        def column(ref, r0, lanes, nkeys):
            tile = ref[pl.ds(r0, BAND), lanes]
            other = pltpu.roll(tile, HEAD_DIM, 1)
            first_head = lax.broadcasted_iota(jnp.int32, tile.shape, 1) < HEAD_DIM
            both = jnp.concatenate([jnp.where(first_head, tile, other), jnp.where(first_head, other, tile)], axis=0)
            return both if nkeys == BAND else jnp.concatenate([both, both], axis=1)

        def block(r0, k0, nkeys, lanes, first):
            q2 = _stack_heads(q_ref[pl.ds(r0, BAND), lanes])
            do2 = _stack_heads(do_ref[pl.ds(r0, BAND), lanes])
            kk = k_ref[pl.ds(k0, nkeys), lanes]
            vv = v_ref[pl.ds(k0, nkeys), lanes]
            s = jnp.where(_band_mask(nkeys), _dot_nt(q2, kk), NEG_INF)
            p = jnp.exp(s - column(lse_ref, r0, lanes, nkeys))
            ds = (p * (_dot_nt(do2, vv) - column(dd_ref, r0, lanes, nkeys))).astype(BF16)
            dq_ref[pl.ds(r0, BAND), lanes] = _unstack(_dot_nn(ds, kk)).astype(dq_ref.dtype)
            dk_part = _dot_tn(ds, q2)
            dv_part = _dot_tn(p.astype(BF16), do2)
            if first:
                dk_ref[pl.ds(k0, nkeys), lanes] = dk_part
                dv_ref[pl.ds(k0, nkeys), lanes] = dv_part
            else:
                dk_ref[pl.ds(k0, BAND), lanes] += dk_part[:BAND]
                dv_ref[pl.ds(k0, BAND), lanes] += dv_part[:BAND]
                dk_ref[pl.ds(k0 + BAND, BAND), lanes] = dk_part[BAND:]
                dv_ref[pl.ds(k0 + BAND, BAND), lanes] = dv_part[BAND:]

        for u in range(units):
            block(0, 0, BAND, slice(u * 128, (u + 1) * 128), True)

        _for_later_blocks(nblk, units, lambda r0, lanes: block(r0, r0 - BAND, 2 * BAND, lanes, False))
        dk_out[...] = dk_ref[...].astype(dk_out.dtype)
        dv_out[...] = dv_ref[...].astype(dv_out.dtype)

    spec = _class_spec(d)
    return pl.pallas_call(
        body, name=name, grid=(8 * d // units,),
        in_specs=[spec] * 6, out_specs=[spec] * 3,
        out_shape=[_sds((S // d, d * D), BF16)] * 3,
        scratch_shapes=[pltpu.VMEM((S // d, 128 * units), F32)] * 2,
        compiler_params=_cparams(1),
    )(q, k, v, do, lse, dd)


MIX_TILE = 256
DILATIONS = tuple(d for _, d in BRANCHES)


def _branch_weights(la, lb, lc):
    m = jnp.maximum(jnp.maximum(la, lb), lc)
    ea, eb, ec = jnp.exp(la - m), jnp.exp(lb - m), jnp.exp(lc - m)
    den = ea + eb + ec
    return ea / den, eb / den, ec / den


def _mix_operands(outs, lses):
    specs = [_class_block(d, MIX_TILE) for d in DILATIONS] * 2
    scratch = [pltpu.VMEM((NCHUNK, MIX_TILE, 128), F32)] * 4
    return list(outs) + list(lses), specs, scratch


def _mix_fwd(name, outs, lses):
    def body(o0, o1, o2, l0, l1, l2, o_ref, to1, to2, tl1, tl2):
        for blk, tmp, d in ((o1, to1, DILATIONS[1]), (o2, to2, DILATIONS[2]), (l1, tl1, DILATIONS[1]), (l2, tl2, DILATIONS[2])):
            _tokens_from_classes(blk, tmp, d, MIX_TILE)
        for c in range(NCHUNK):
            wa, wb, wc = _branch_weights(l0[:, _chunk(c)], tl1[c], tl2[c])
            o_ref[:, _chunk(c)] = (wa * o0[:, _chunk(c)].astype(F32) + wb * to1[c] + wc * to2[c]).astype(o_ref.dtype)

    operands, specs, scratch = _mix_operands(outs, lses)
    return pl.pallas_call(
        body, name=name, grid=(S // MIX_TILE,),
        in_specs=specs, out_specs=_row_spec(MIX_TILE), out_shape=_sds((S, D), BF16),
        scratch_shapes=scratch, compiler_params=_cparams(1),
    )(*operands)


def _head_sum(x, ones_blockdiag):
    hi = x.astype(BF16)
    r1 = x - hi.astype(F32)
    mid = r1.astype(BF16)
    lo = (r1 - mid.astype(F32)).astype(BF16)
    return _dot_nn(hi, ones_blockdiag) + _dot_nn(mid, ones_blockdiag) + _dot_nn(lo, ones_blockdiag)


def _mix_bwd(name, do, outs, lses, ones_blockdiag):
    def body(do_ref, o0, o1, o2, l0, l1, l2, ones_ref, d0, d1, d2, t0, t1, t2,
             to1, to2, tl1, tl2, td1, td2, tt1, tt2):
        for blk, tmp, d in ((o1, to1, DILATIONS[1]), (o2, to2, DILATIONS[2]), (l1, tl1, DILATIONS[1]), (l2, tl2, DILATIONS[2])):
            _tokens_from_classes(blk, tmp, d, MIX_TILE)
        ones = ones_ref[...]
        for c in range(NCHUNK):
            w = _branch_weights(l0[:, _chunk(c)], tl1[c], tl2[c])
            dov = do_ref[:, _chunk(c)]
            o = w[0] * o0[:, _chunk(c)].astype(F32) + w[1] * to1[c] + w[2] * to2[c]
            t = _head_sum(dov * o, ones)
            d0[:, _chunk(c)] = (w[0] * dov).astype(d0.dtype)
            t0[:, _chunk(c)] = w[0] * t
            td1[c], tt1[c] = w[1] * dov, w[1] * t
            td2[c], tt2[c] = w[2] * dov, w[2] * t
        for tmp, blk, d in ((td1, d1, DILATIONS[1]), (tt1, t1, DILATIONS[1]), (td2, d2, DILATIONS[2]), (tt2, t2, DILATIONS[2])):
            _classes_from_tokens(tmp, blk, d, MIX_TILE)

    operands, specs, scratch = _mix_operands(outs, lses)
    out_specs = [_class_block(d, MIX_TILE) for d in DILATIONS] * 2
    out_shape = [_sds((S // d, d * D), BF16) for d in DILATIONS] + [_sds((S // d, d * D), F32) for d in DILATIONS]
    return pl.pallas_call(
        body, name=name, grid=(S // MIX_TILE,),
        in_specs=[_row_spec(MIX_TILE)] + specs + [_vec_spec(128, 128)],
        out_specs=out_specs, out_shape=out_shape,
        scratch_shapes=scratch + [pltpu.VMEM((NCHUNK, MIX_TILE, 128), F32)] * 4,
        compiler_params=_cparams(1),
    )(do, *operands, ones_blockdiag)


def _attn_bwd_post(name, grads, cos_t, sin_t):
    tm = MIX_TILE
    scale = HEAD_DIM ** -0.5

    def unrope(x, cs, sn):
        return x * cs - _swap_halves(x) * sn

    def body(*refs):
        in_refs = refs[:9]
        cos_ref, sin_ref, dq_ref, dkv_ref, tmp_ref = refs[9:]
        cs = cos_ref[...]
        sn = sin_ref[...]
        for g, d in enumerate(DILATIONS):
            for which, blk in enumerate(in_refs[3 * g:3 * g + 3]):
                if d > 1:
                    _tokens_from_classes(blk, tmp_ref, d, tm)
                for c in range(NCHUNK):
                    x = tmp_ref[c] if d > 1 else blk[:, _chunk(c)].astype(F32)
                    if which == 0:
                        dq_ref[:, _chunk(c, g * D)] = (unrope(x, cs, sn) * scale).astype(dq_ref.dtype)
                    elif which == 1:
                        dkv_ref[:, _chunk(c, g * D)] = unrope(x, cs, sn).astype(dkv_ref.dtype)
                    else:
                        dkv_ref[:, _chunk(c, QW + g * D)] = x.astype(dkv_ref.dtype)

    operands = [a for branch in grads for a in branch]
    tab = pl.BlockSpec((tm, 128), lambda i: (i, 0))
    return pl.pallas_call(
        body, name=name, grid=(S // tm,),
        in_specs=[_class_block(d, tm) for d in DILATIONS for _ in range(3)] + [tab, tab],
        out_specs=[pl.BlockSpec((tm, QW), lambda i: (i, 0)), pl.BlockSpec((tm, 2 * QW), lambda i: (i, 0))],
        out_shape=[_sds((S, QW), BF16), _sds((S, 2 * QW), BF16)],
        scratch_shapes=[pltpu.VMEM((NCHUNK, tm, 128), F32)],
        compiler_params=_cparams(1),
    )(*operands, cos_t, sin_t)


def _adamw(name, parts, w, m, v, layer=None, other=None):
    n, rows, cols = parts.shape
    tr = rows
    for cand in (256, 176, 128, 64, 32, 16, 8):
        if rows % cand == 0:
            tr = cand
            break
    n_other = 0 if other is None else len(other)

    def body(p_ref, w_ref, m_ref, v_ref, *refs):
        g_ref, d_ref, nm_ref, nv_ref = refs[n_other:]
        g = p_ref[0].astype(F32)
        for j in range(1, n):
            g = g + p_ref[j].astype(F32)
        g_ref[...] = g
        d_ref[...], nm_ref[...], nv_ref[...] = _adam_update(g, w_ref[...], m_ref[...], v_ref[...])

    if layer is None:
        blk = pl.BlockSpec((tr, cols), lambda i: (i, 0))
        shape = (rows, cols)
    else:
        blk = pl.BlockSpec((None, tr, cols), lambda i: (layer, i, 0))
        shape = w.shape
    return pl.pallas_call(
        body, name=name, grid=(rows // tr,),
        in_specs=[pl.BlockSpec((n, tr, cols), lambda i: (0, i, 0)), blk, blk, blk]
                 + [pl.BlockSpec(memory_space=pl.ANY)] * n_other,
        out_specs=[blk] * 4, out_shape=[_sds(shape, F32)] * 4,
        input_output_aliases={4 + i: i for i in range(n_other)},
        compiler_params=_cparams(1),
    )(parts, w, m, v, *(other or ()))


def _adam_update(g, w, m, v):
    c1 = 1.0 / (1.0 - ADAM_B1 ** ADAM_STEP)
    c2 = 1.0 / (1.0 - ADAM_B2 ** ADAM_STEP)
    nm = ADAM_B1 * m + (1.0 - ADAM_B1) * g
    nv = ADAM_B2 * v + (1.0 - ADAM_B2) * (g * g)
    return -ADAM_LR * ((nm * c1) / (jnp.sqrt(nv * c2) + ADAM_EPS) + ADAM_WD * w), nm, nv


GAIN_ROWS = 16


def _pack_small(name, gain_tiles, taps, sq):
    ng = len(gain_tiles)

    def body(*refs):
        o_ref = refs[-1]
        o_ref[...] = jnp.zeros_like(o_ref)
        for i in range(ng):
            o_ref[i:i + 1, :] = refs[i][0:1, :]
        o_ref[ng:ng + 3, :] = refs[ng][0:3, :]
        o_ref[ng + 3:ng + 4, :] = refs[ng + 1][...]

    return pl.pallas_call(body, name=name, out_shape=_sds((GAIN_ROWS, D), F32))(*gain_tiles, taps, sq)


def _adamw_gains(name, parts, params):
    np_ = len(params)
    shapes = [w.shape for w, _, _ in params]

    def body(p_ref, *refs):
        ins, outs = refs[:3 * np_], refs[3 * np_:]

        def total(lo, rows):
            g = p_ref[0, lo:lo + rows, :]
            for j in range(1, NDEV):
                g = g + p_ref[j, lo:lo + rows, :]
            return g

        lo = 0
        for i, shape in enumerate(shapes):
            g = total(lo, shape[0])
            lo += shape[0]
            w_ref, m_ref, v_ref = ins[3 * i:3 * i + 3]
            g_ref, d_ref, nm_ref, nv_ref = outs[4 * i:4 * i + 4]
            g_ref[...] = g
            d_ref[...], nm_ref[...], nv_ref[...] = _adam_update(g, w_ref[...], m_ref[...], v_ref[...])
        taps_ref, loss_ref = outs[-2], outs[-1]
        taps_ref[...] = jnp.zeros_like(taps_ref)
        taps_ref[0:3, :] = total(lo, 3)
        loss_ref[...] = jnp.sum(total(lo + 3, 1), axis=-1, keepdims=True) * (0.5 / D)

    out_shape = [_sds(shape, F32) for shape in shapes for _ in range(4)] + [_sds((8, D), F32), _sds((1, 1), F32)]
    outs = pl.pallas_call(body, name=name, out_shape=out_shape)(parts, *[a for p in params for a in p])
    return [list(outs[4 * i:4 * i + 4]) for i in range(np_)], outs[-2], outs[-1].reshape(())


def _exchange(name, arrays, kind):
    n = len(arrays)
    gather = kind == "gather"
    out_shape = [_sds((NDEV,) + a.shape if gather else a.shape, a.dtype) for a in arrays]

    def body(*refs):
        srcs, outs = refs[:n], refs[n:2 * n]
        send_sems, recv_sems, local_sems = refs[2 * n:]
        x, y, c = lax.axis_index("x"), lax.axis_index("y"), lax.axis_index("c")
        me = 4 * x + 2 * y + c
        pending = []
        for t in range(n):
            own = pltpu.make_async_copy(srcs[t] if gather else srcs[t].at[me], outs[t].at[me], local_sems.at[t])
            own.start()
            pending.append(own)
            for rel in range(1, NDEV):
                px = 1 - x if rel & 4 else x
                py = 1 - y if rel & 2 else y
                pc = 1 - c if rel & 1 else c
                peer = 4 * px + 2 * py + pc
                send = pltpu.make_async_remote_copy(
                    src_ref=srcs[t] if gather else srcs[t].at[peer], dst_ref=outs[t].at[me],
                    send_sem=send_sems.at[t, rel - 1], recv_sem=recv_sems.at[t, rel - 1],
                    device_id=(px, py, pc), device_id_type=MESH)
                send.start()
                arrive = pltpu.make_async_remote_copy(
                    src_ref=srcs[t] if gather else srcs[t].at[me], dst_ref=outs[t].at[peer],
                    send_sem=send_sems.at[t, rel - 1], recv_sem=recv_sems.at[t, rel - 1],
                    device_id=(px, py, pc), device_id_type=MESH)
                pending.append((send, arrive))
        for item in pending:
            if isinstance(item, tuple):
                item[0].wait_send()
                item[1].wait_recv()
            else:
                item.wait()

    any_spec = pl.BlockSpec(memory_space=pl.ANY)
    outs = pl.pallas_call(
        body, name=name,
        in_specs=[any_spec] * n, out_specs=[any_spec] * n, out_shape=out_shape,
        scratch_shapes=[pltpu.SemaphoreType.DMA((n, NDEV - 1)), pltpu.SemaphoreType.DMA((n, NDEV - 1)),
                        pltpu.SemaphoreType.DMA((n,))],
    )(*arrays)
    return list(outs)


_HBM_SPEC = pl.BlockSpec(memory_space=pltpu.HBM)
_SEM_SPEC = pl.BlockSpec(memory_space=pltpu.SEMAPHORE)
_DATAFLOW = pltpu.SideEffectType.DATAFLOW_SIDE_EFFECTING


def _peers():
    x, y, c = lax.axis_index("x"), lax.axis_index("y"), lax.axis_index("c")
    out = []
    for rel in range(1, NDEV):
        px = 1 - x if rel & 4 else x
        py = 1 - y if rel & 2 else y
        pc = 1 - c if rel & 1 else c
        out.append((rel - 1, (px, py, pc), 4 * px + 2 * py + pc))
    return 4 * x + 2 * y + c, out


def _hbm(a):
    return pltpu.HBM(a.shape, a.dtype)


def _own_slot(a, me, kind):
    mine = a[None] if kind == "gather" else lax.dynamic_slice_in_dim(a, me, 1, axis=0)
    shape = (NDEV,) + mine.shape[1:]
    return lax.dynamic_update_slice_in_dim(lax.empty(shape, a.dtype), mine, me, axis=0)


def _exchange_start(name, arrays, me, kind):
    n = len(arrays)
    gather = kind == "gather"
    lands = [_own_slot(a, me, kind) for a in arrays]

    def body(*refs):
        src_refs, land_refs = refs[:n], refs[n:2 * n]
        send_sems, recv_sems = refs[2 * n], refs[2 * n + 1]
        token = refs[-1]
        my_block, peers = _peers()
        for t in range(n):
            for slot, dev, block in peers:
                pltpu.make_async_remote_copy(
                    src_ref=src_refs[t] if gather else src_refs[t].at[block], dst_ref=land_refs[t].at[my_block],
                    send_sem=send_sems.at[t * (NDEV - 1) + slot], recv_sem=recv_sems.at[t * (NDEV - 1) + slot],
                    device_id=dev, device_id_type=MESH).start()
        token[...] = jnp.zeros_like(token)

    operands = [pltpu.with_memory_space_constraint(a, pltpu.HBM) for a in list(arrays) + lands]
    outs = pl.pallas_call(
        body, name=name,
        out_shape=(pltpu.SemaphoreType.DMA((n * (NDEV - 1),)), pltpu.SemaphoreType.DMA((n * (NDEV - 1),)),
                   *[_hbm(a) for a in operands], _sds((8, 128), F32)),
        in_specs=[_HBM_SPEC] * (2 * n),
        out_specs=(_SEM_SPEC, _SEM_SPEC, *[_HBM_SPEC] * (2 * n), pl.BlockSpec(memory_space=pltpu.VMEM)),
        input_output_aliases={i: 2 + i for i in range(2 * n)},
        compiler_params=pltpu.CompilerParams(has_side_effects=_DATAFLOW),
    )(*operands)
    return (outs[0], outs[1], list(outs[2:2 + n]), list(outs[2 + n:2 + 2 * n])), outs[-1]


def _exchange_wait(name, started, t, after, kind):
    send_sems, recv_sems, srcs, lands = started
    gather = kind == "gather"

    def body(src_ref, land_ref, send_ref, recv_ref, after_ref, src_out, land_out):
        _, peers = _peers()
        for slot, dev, block in peers:
            copy = pltpu.make_async_remote_copy(
                src_ref=src_ref if gather else src_ref.at[block], dst_ref=land_ref.at[block],
                send_sem=send_ref.at[t * (NDEV - 1) + slot], recv_sem=recv_ref.at[t * (NDEV - 1) + slot],
                device_id=dev, device_id_type=MESH)
            copy.wait_send()
            copy.wait_recv()

    return pl.pallas_call(
        body, name=name, out_shape=(_hbm(srcs[t]), _hbm(lands[t])),
        in_specs=(_HBM_SPEC, _HBM_SPEC, _SEM_SPEC, _SEM_SPEC, pl.BlockSpec(memory_space=pl.ANY)),
        out_specs=(_HBM_SPEC, _HBM_SPEC), input_output_aliases={0: 0, 1: 1},
        compiler_params=pltpu.CompilerParams(has_side_effects=_DATAFLOW),
    )(srcs[t], lands[t], send_sems, recv_sems, after)[1]


DIRECT_RELS = (1, 2, 4, 6)
RELAY_RELS = (2, 4, 6)


def _rel_peer(rel):
    x, y, c = lax.axis_index("x"), lax.axis_index("y"), lax.axis_index("c")
    px = 1 - x if rel & 4 else x
    py = 1 - y if rel & 2 else y
    pc = 1 - c if rel & 1 else c
    return (px, py, pc), 4 * px + 2 * py + pc


def _gather_start(name, shards, me):
    n, nr = len(shards), len(DIRECT_RELS)
    lands = [_own_slot(a, me, "gather") for a in shards]

    def body(*refs):
        src_refs, land_refs = refs[:n], refs[n:2 * n]
        send_sems, recv_sems = refs[2 * n], refs[2 * n + 1]
        _, my_block = _rel_peer(0)
        for t in range(n):
            for s, rel in enumerate(DIRECT_RELS):
                dev, _ = _rel_peer(rel)
                pltpu.make_async_remote_copy(
                    src_ref=src_refs[t], dst_ref=land_refs[t].at[my_block],
                    send_sem=send_sems.at[t * nr + s], recv_sem=recv_sems.at[t * nr + s],
                    device_id=dev, device_id_type=MESH).start()

    operands = [pltpu.with_memory_space_constraint(a, pltpu.HBM) for a in list(shards) + lands]
    outs = pl.pallas_call(
        body, name=name,
        out_shape=(pltpu.SemaphoreType.DMA((n * nr,)), pltpu.SemaphoreType.DMA((n * nr,)), *[_hbm(a) for a in operands]),
        in_specs=[_HBM_SPEC] * (2 * n), out_specs=(_SEM_SPEC, _SEM_SPEC, *[_HBM_SPEC] * (2 * n)),
        input_output_aliases={i: 2 + i for i in range(2 * n)},
        compiler_params=pltpu.CompilerParams(has_side_effects=_DATAFLOW),
    )(*operands)
    return outs[0], outs[1], list(outs[2:2 + n]), list(outs[2 + n:2 + 2 * n])


def _gather_wait(name, started, ts, after):
    send_sems, recv_sems, srcs, lands = started
    m, nr = len(ts), len(DIRECT_RELS)

    def body(*refs):
        src_refs, land_refs = refs[:m], refs[m:2 * m]
        send_ref, recv_ref = refs[2 * m], refs[2 * m + 1]
        for i, t in enumerate(ts):
            for s, rel in enumerate(DIRECT_RELS):
                dev, block = _rel_peer(rel)
                copy = pltpu.make_async_remote_copy(
                    src_ref=src_refs[i], dst_ref=land_refs[i].at[block],
                    send_sem=send_ref.at[t * nr + s], recv_sem=recv_ref.at[t * nr + s],
                    device_id=dev, device_id_type=MESH)
                copy.wait_send()
                copy.wait_recv()

    operands = [srcs[t] for t in ts] + [lands[t] for t in ts]
    outs = pl.pallas_call(
        body, name=name, out_shape=tuple(_hbm(a) for a in operands),
        in_specs=[_HBM_SPEC] * (2 * m) + [_SEM_SPEC, _SEM_SPEC, pl.BlockSpec(memory_space=pl.ANY)],
        out_specs=tuple([_HBM_SPEC] * (2 * m)), input_output_aliases={i: i for i in range(2 * m)},
        compiler_params=pltpu.CompilerParams(has_side_effects=_DATAFLOW),
    )(*operands, send_sems, recv_sems, after)
    return list(outs[m:])


def _relay_start(name, lands):
    m, nr = len(lands), len(RELAY_RELS)

    def body(*refs):
        land_refs, send_sems, recv_sems = refs[:m], refs[m], refs[m + 1]
        sibling, _ = _rel_peer(1)
        for i in range(m):
            for s, rel in enumerate(RELAY_RELS):
                _, block = _rel_peer(rel)
                pltpu.make_async_remote_copy(
                    src_ref=land_refs[i].at[block], dst_ref=land_refs[i].at[block],
                    send_sem=send_sems.at[i * nr + s], recv_sem=recv_sems.at[i * nr + s],
                    device_id=sibling, device_id_type=MESH).start()

    outs = pl.pallas_call(
        body, name=name,
        out_shape=(pltpu.SemaphoreType.DMA((m * nr,)), pltpu.SemaphoreType.DMA((m * nr,)), *[_hbm(a) for a in lands]),
        in_specs=[_HBM_SPEC] * m, out_specs=(_SEM_SPEC, _SEM_SPEC, *[_HBM_SPEC] * m),
        input_output_aliases={i: 2 + i for i in range(m)},
        compiler_params=pltpu.CompilerParams(has_side_effects=_DATAFLOW),
    )(*lands)
    return outs[0], outs[1], list(outs[2:])


def _relay_wait(name, relayed, after):
    send_sems, recv_sems, lands = relayed
    m, nr = len(lands), len(RELAY_RELS)

    def body(*refs):
        land_refs, send_ref, recv_ref = refs[:m], refs[m], refs[m + 1]
        sibling, _ = _rel_peer(1)
        for i in range(m):
            for s, rel in enumerate(RELAY_RELS):
                _, sent = _rel_peer(rel)
                _, arriving = _rel_peer(rel ^ 1)
                copy = pltpu.make_async_remote_copy(
                    src_ref=land_refs[i].at[sent], dst_ref=land_refs[i].at[arriving],
                    send_sem=send_ref.at[i * nr + s], recv_sem=recv_ref.at[i * nr + s],
                    device_id=sibling, device_id_type=MESH)
                copy.wait_send()
                copy.wait_recv()

    outs = pl.pallas_call(
        body, name=name, out_shape=tuple(_hbm(a) for a in lands),
        in_specs=[_HBM_SPEC] * m + [_SEM_SPEC, _SEM_SPEC, pl.BlockSpec(memory_space=pl.ANY)],
        out_specs=tuple([_HBM_SPEC] * m), input_output_aliases={i: i for i in range(m)},
        compiler_params=pltpu.CompilerParams(has_side_effects=_DATAFLOW),
    )(*lands, send_sems, recv_sems, after)
    return list(outs)


def _ffn_fwd(tag, n, wg, wd):
    gu, act = _gate_up_act(f"ffn_gate_up_{tag}", n, wg)
    wd4 = wd.reshape(NFB, FB, D)
    f = _fwd_kblocked(f"ffn_down_{tag}", act, wd4)
    return (n, gu, act, wg, wd4), f


def _ffn_bwd(tag, dh_out, df, h_in, saved, g_pre, send, mixer):
    n, gu, act, wg, wd4 = saved
    dwd = _bwd_w_kblocked(f"ffn_down_dw_{tag}", act, df).reshape(NDEV, DFF // NDEV, D)
    dgu = _down_dx_act_bwd(f"ffn_down_dx_{tag}", df, wd4, gu).reshape(NDEV, S, FB)
    tok = send({f"down_{tag}": dwd, f"gate_up_{tag}": _bwd_w_cols_blocked(f"ffn_gate_up_dw_{tag}", n, dgu)})
    dn = _bwd_x_cols_blocked(f"ffn_gate_up_dx_{tag}", dgu, wg, after=tok)
    dh_in, (dg_pre,), dy, dg_mixer = _rms_bwd(f"ffn_prenorm_bwd_{tag}", h_in, [(g_pre, dn)], dh_out, F32, then=mixer)
    return dh_in, dg_pre, dy, dg_mixer


def kernel(x, positions, mix_norm_pre, mix_norm_post, ffn_norm_pre, ffn_norm_post, ffn_w_gate_up, ffn_w_down, conv_w_in, conv_w, conv_w_out, kv_norm, w_kv, w_q, w_o, loss_target, m_mix_norm_pre, m_mix_norm_post, m_ffn_norm_pre, m_ffn_norm_post, m_ffn_w_gate_up, m_ffn_w_down, m_conv_w_in, m_conv_w, m_conv_w_out, m_kv_norm, m_w_kv, m_w_q, m_w_o, v_mix_norm_pre, v_mix_norm_post, v_ffn_norm_pre, v_ffn_norm_post, v_ffn_w_gate_up, v_ffn_w_down, v_conv_w_in, v_conv_w, v_conv_w_out, v_kv_norm, v_w_kv, v_w_q, v_w_o):
    me = 4 * lax.axis_index("x") + 2 * lax.axis_index("y") + lax.axis_index("c")
    h0 = x.reshape(S, D)
    target = loss_target.reshape(S, D)
    row = lambda a, l: a[l].reshape(1, D)
    g_kv = kv_norm.reshape(1, D)

    cw_shard = jnp.pad(conv_w[0], ((0, 5), (0, 0)))
    names = ["conv_in", "conv_w", "conv_out", "gate_up_0", "down_0", "kv", "q", "o", "gate_up_1", "down_1"]
    shards = [conv_w_in[0], cw_shard, conv_w_out[0], ffn_w_gate_up[0], ffn_w_down[0],
              w_kv, w_q[0], w_o[0], ffn_w_gate_up[1], ffn_w_down[1]]
    shards = [s if n == "conv_w" else s.astype(BF16) for n, s in zip(names, shards)]
    first = 3
    gather_first = _gather_start("gather_start_conv", shards[:first], me)
    gather_rest = _gather_start("gather_start_rest", shards[first:], me)

    def direct(group, after):
        ts = [names.index(n) for n in group]
        started, ts = (gather_first, ts) if ts[0] < first else (gather_rest, [t - first for t in ts])
        lands = _gather_wait(f"gather_wait_{group[0]}", started, ts, after)
        return _relay_start(f"relay_start_{group[0]}", lands)

    def finish(group, relayed, after):
        return dict(zip(group, _relay_wait(f"relay_wait_{group[0]}", relayed, after)))

    sent = {}

    def send(grads):
        started, token = _exchange_start(f"scatter_start_{next(iter(grads))}", list(grads.values()), me, "scatter")
        for i, name in enumerate(grads):
            sent[name] = (started, i)
        return token

    groups = [["conv_in", "conv_w", "conv_out"], ["gate_up_0", "down_0"], ["kv", "q"], ["o", "gate_up_1", "down_1"]]
    n0 = _rms_fwd("mix_prenorm_0", h0, [row(mix_norm_pre, 0)])[0]
    w = finish(groups[0], direct(groups[0], n0), n0)
    win = w["conv_in"].transpose(1, 0, 2).reshape(D, 3 * D)
    cw = w["conv_w"].transpose(1, 0, 2).reshape(8, D)
    wout = w["conv_out"].reshape(D, D)
    z = _fwd_rows("conv_in", n0, win, BF16)
    pre = _conv_fwd("conv_gate", z, cw)
    relayed = direct(groups[1], pre)
    y0 = _fwd_rows("conv_out", pre, wout)
    h1, (n1,) = _resid_rms("mix_postnorm_0", h0, y0, row(mix_norm_post, 0), [row(ffn_norm_pre, 0)])
    w = finish(groups[1], relayed, n1)
    ffn0, f0 = _ffn_fwd("0", n1, w["gate_up_0"], w["down_0"])
    relayed = direct(groups[2], ffn0[2])
    h2, (nk, n2) = _resid_rms("ffn_postnorm_0", h1, f0, row(ffn_norm_post, 0), [g_kv, row(mix_norm_pre, 1)])

    w = finish(groups[2], relayed, nk)
    wkv = w["kv"].transpose(1, 0, 2).reshape(D, 2 * QW)
    wq = w["q"].transpose(1, 0, 2).reshape(D, QW)
    half = HEAD_DIM // 2
    inv_freq = ROPE_THETA ** (-jnp.arange(half, dtype=F32) / half)
    tables = _rope_tables("rope_tables", positions.reshape(S, 1), jnp.tile(inv_freq, 4).reshape(1, 128))
    qc, kc, vc, o_c, lse_c = [], [], [], [], []
    for g, d in enumerate(DILATIONS):
        kc.append(_proj_classes(f"k_proj_{g}", nk, wkv, g, d, tables, 1.0))
        vc.append(_proj_classes(f"v_proj_{g}", nk, wkv, len(DILATIONS) + g, d, None, None))
    relayed = direct(groups[3], vc[-1])
    for g, d in enumerate(DILATIONS):
        qc.append(_proj_classes(f"q_proj_{g}", n2, wq, g, d, tables, HEAD_DIM ** -0.5))
        o_g, lse_g = _attn_fwd(f"attn_fwd_{g}", qc[g], kc[g], vc[g], d)
        o_c.append(o_g)
        lse_c.append(lse_g)
    o_mix = _mix_fwd("attn_mix", o_c, lse_c)
    w = finish(groups[3], relayed, o_mix)
    wo = w["o"].reshape(D, D)
    y1 = _fwd_rows("attn_out", o_mix, wo)
    h3, (n3,) = _resid_rms("mix_postnorm_1", h2, y1, row(mix_norm_post, 1), [row(ffn_norm_pre, 1)])
    ffn1, f1 = _ffn_fwd("1", n3, w["gate_up_1"], w["down_1"])

    dh4, df1, dg_fpost1, sq = _resid_rms_loss("ffn_postnorm_1_loss", h3, f1, row(ffn_norm_post, 1), target)

    dh3, dg_fpre1, dy1, dg_mpost1 = _ffn_bwd(
        "1", dh4, df1, h3, ffn1, row(ffn_norm_pre, 1), send, (y1, row(mix_norm_post, 1)))
    dwo = _bwd_w_rows("attn_out_dw", o_mix, dy1).reshape(NDEV, D // NDEV, D)
    do = _bwd_x_rows("attn_out_dx", dy1, wo, F32)
    lane = jnp.arange(128)
    ones_blockdiag = (lane[:, None] // HEAD_DIM == lane[None, :] // HEAD_DIM).astype(BF16)
    mixed = _mix_bwd("attn_mix_bwd", do, o_c, lse_c, ones_blockdiag)
    branch_grads = [_attn_bwd(f"attn_bwd_{g}", qc[g], kc[g], vc[g], mixed[g], lse_c[g], mixed[3 + g], d)
                    for g, d in enumerate(DILATIONS)]
    dq_raw, dkv = _attn_bwd_post("attn_bwd_post", branch_grads, *tables)
    tok = send({"o": dwo, "kv": _bwd_w_cols("kv_proj_dw", nk, dkv, 2 * QW // NDEV),
                "q": _bwd_w_cols("q_proj_dw", n2, dq_raw, QW // NDEV)})
    dnk = _bwd_x_plain("kv_proj_dx", dkv, wkv, after=tok)
    dn2 = _bwd_x_plain("q_proj_dx", dq_raw, wq)
    dh2, (dg_kv, dg_mpre1), df0, dg_fpost0 = _rms_bwd(
        "kv_and_mix_prenorm_bwd_1", h2, [(g_kv, dnk), (row(mix_norm_pre, 1), dn2)], dh3, F32,
        then=(f0, row(ffn_norm_post, 0)))

    dh1, dg_fpre0, dy0, dg_mpost0 = _ffn_bwd(
        "0", dh2, df0, h1, ffn0, row(ffn_norm_pre, 0), send, (y0, row(mix_norm_post, 0)))
    dwout = _bwd_w_rows("conv_out_dw", pre, dy0).reshape(NDEV, D // NDEV, D)
    dpre = _bwd_x_rows("conv_out_dx", dy0, wout, BF16)
    dz, dcw = _conv_bwd("conv_gate_bwd", z, dpre, cw)
    tok = send({"conv_out": dwout, "conv_in": _bwd_w_cols("conv_in_dw", n0, dz, 3 * D // NDEV)})
    dn0 = _bwd_x_plain("conv_in_dx", dz, win, after=tok)
    dh0, (dg_mpre0,) = _rms_bwd("mix_prenorm_bwd_0", h0, [(row(mix_norm_pre, 0), dn0)], dh1, F32)

    small = _pack_small("pack_small_grads", [dg_mpre0, dg_mpre1, dg_mpost0, dg_mpost1, dg_fpre0, dg_fpre1,
                                             dg_fpost0, dg_fpost1, dg_kv], dcw, sq)
    small_all = _exchange("gather_small_grads", [small], "gather")[0]

    done = [small_all]

    def upd(tag, w, m, v):
        parts = _exchange_wait(f"scatter_wait_{tag}", *sent[tag], done[-1], "scatter")
        shape = w.shape
        flat = lambda a: a.reshape(parts.shape[1:])
        res = _adamw(f"adamw_{tag}", parts, flat(w), flat(m), flat(v))
        done.append(res[0])
        return [r.reshape(shape) for r in res]

    def upd_layer(tag, l, w, m, v, other):
        parts = _exchange_wait(f"scatter_wait_{tag}_{l}", *sent[f"{tag}_{l}"], done[-1], "scatter")
        res = _adamw(f"adamw_{tag}_{l}", parts, w, m, v, layer=l, other=other)
        done.append(res[0])
        return list(res)

    vec = lambda a: a.reshape(1, D)
    gain_res, taps, loss = _adamw_gains("adamw_gains", small_all, [
        (mix_norm_pre, m_mix_norm_pre, v_mix_norm_pre), (mix_norm_post, m_mix_norm_post, v_mix_norm_post),
        (ffn_norm_pre, m_ffn_norm_pre, v_ffn_norm_pre), (ffn_norm_post, m_ffn_norm_post, v_ffn_norm_post),
        (vec(kv_norm), vec(m_kv_norm), vec(v_kv_norm))])
    dcw_mine = lax.dynamic_slice(taps, (0, me * 128), (8, 128))
    pad8 = lambda a, fill: jnp.pad(a[0], ((0, 5), (0, 0)), constant_values=fill)
    cw_res = [r[0:3].reshape(1, 3, 128) for r in
              _adamw("adamw_conv_w", dcw_mine.reshape(1, 8, 128), cw_shard, pad8(m_conv_w, 0.0), pad8(v_conv_w, 1.0))]

    res = {
        "mix_norm_pre": gain_res[0],
        "mix_norm_post": gain_res[1],
        "ffn_norm_pre": gain_res[2],
        "ffn_norm_post": gain_res[3],
        "kv_norm": [r.reshape(D) for r in gain_res[4]],
        "conv_w": cw_res,
    }
    down_1 = upd_layer("down", 1, ffn_w_down, m_ffn_w_down, v_ffn_w_down, None)
    gate_up_t = [jnp.swapaxes(a, 1, 2) for a in (ffn_w_gate_up, m_ffn_w_gate_up, v_ffn_w_gate_up)]
    gate_up_1 = upd_layer("gate_up", 1, *gate_up_t, None)
    res["w_o"] = upd("o", w_o, m_w_o, v_w_o)
    res["w_q"] = upd("q", w_q, m_w_q, v_w_q)
    res["w_kv"] = upd("kv", w_kv, m_w_kv, v_w_kv)
    res["ffn_w_down"] = upd_layer("down", 0, ffn_w_down, m_ffn_w_down, v_ffn_w_down, down_1)
    res["ffn_w_gate_up"] = [jnp.swapaxes(r, 1, 2) for r in upd_layer("gate_up", 0, *gate_up_t, gate_up_1)]
    res["conv_w_out"] = upd("conv_out", conv_w_out, m_conv_w_out, v_conv_w_out)
    res["conv_w_in"] = upd("conv_in", conv_w_in, m_conv_w_in, v_conv_w_in)
    order = ["mix_norm_pre", "mix_norm_post", "ffn_norm_pre", "ffn_norm_post", "ffn_w_gate_up", "ffn_w_down",
             "conv_w_in", "conv_w", "conv_w_out", "kv_norm", "w_kv", "w_q", "w_o"]
    out = [loss, dh0.reshape(1, S, D)]
    for i in range(4):
        out += [res[name][i] for name in order]
    return tuple(out)
```

```python
import jax
import jax.numpy as jnp
from jax import lax
from jax.experimental import pallas as pl
from jax.experimental.pallas import tpu as pltpu

F32 = jnp.float32
BF16 = jnp.bfloat16

S = 4096
D = 1024
NDEV = 8
HEAD_DIM = 64
QW = 3072
DFF = 2816
FB = 704
NFB = 4
BRANCHES = ((128, 1), (512, 4), (2048, 16))
BAND = 128
ROPE_THETA = 10000.0
RMS_EPS = 1e-6
NEG_INF = -1e30
ADAM_LR, ADAM_B1, ADAM_B2, ADAM_EPS, ADAM_WD, ADAM_STEP = 0.001, 0.9, 0.999, 1e-08, 0.01, 10

VMEM_LIMIT_BYTES = 52 * 1024 * 1024
ROW_TILE = 512
MESH = pl.DeviceIdType.MESH


def _cparams(ngrid):
    return pltpu.CompilerParams(dimension_semantics=("arbitrary",) * ngrid,
                                vmem_limit_bytes=VMEM_LIMIT_BYTES)


def _sds(shape, dtype):
    return jax.ShapeDtypeStruct(tuple(shape), dtype)


_DIMS = {"nn": (((1,), (0,)), ((), ())),
         "nt": (((1,), (1,)), ((), ())),
         "tn": (((0,), (0,)), ((), ()))}


def _matmul(name, a, b, *, mode, grid, a_blk, a_map, b_blk, b_map, o_shape, o_blk, o_map, out_dtype, after=None,
            out_groups=1):
    nk = grid[2]
    dims = _DIMS[mode]
    acc_shape = tuple(s for s in o_blk if s is not None)
    if out_groups > 1:
        acc_shape = (acc_shape[1], out_groups * acc_shape[2])
    extra = [] if after is None else [after]

    def store(o_ref, val):
        if out_groups == 1:
            o_ref[...] = val.astype(o_ref.dtype)
        else:
            n = o_ref.shape[-1]
            for grp in range(out_groups):
                o_ref[grp] = val[:, grp * n:(grp + 1) * n].astype(o_ref.dtype)

    def body(a_ref, b_ref, *rest):
        o_ref, scratch = rest[len(extra)], rest[len(extra) + 1:]
        part = lax.dot_general(a_ref[...], b_ref[...], dims, preferred_element_type=F32)
        if nk == 1:
            store(o_ref, part)
            return
        acc_ref = scratch[0]
        k = pl.program_id(2)

        @pl.when(k == 0)
        def _():
            acc_ref[...] = part

        @pl.when(k > 0)
        def _():
            acc_ref[...] += part

        @pl.when(k == nk - 1)
        def _():
            store(o_ref, acc_ref[...])

    return pl.pallas_call(
        body, name=name, grid=grid,
        in_specs=[pl.BlockSpec(a_blk, a_map), pl.BlockSpec(b_blk, b_map)] + [pl.BlockSpec(memory_space=pl.ANY)] * len(extra),
        out_specs=pl.BlockSpec(o_blk, o_map),
        out_shape=_sds(o_shape, out_dtype),
        scratch_shapes=[] if nk == 1 else [pltpu.VMEM(acc_shape, F32)],
        compiler_params=_cparams(3),
    )(a, b, *extra)


TM = 1024
TK = S


def _fwd_rows(name, a, w, out_dtype=F32):
    kdim, n = w.shape
    tn = 512
    return _matmul(name, a, w, mode="nn", grid=(S // TM, n // tn, 1),
                   a_blk=(TM, kdim), a_map=lambda i, j, k: (i, 0),
                   b_blk=(kdim, tn), b_map=lambda i, j, k: (0, j),
                   o_shape=(S, n), o_blk=(TM, tn), o_map=lambda i, j, k: (i, j), out_dtype=out_dtype)


def _fwd_kblocked(name, a4, w4):
    nb, _, kb = a4.shape
    n = w4.shape[2]

    def body(a_ref, w_ref, o_ref):
        acc = _dot_nn(a_ref[0], w_ref[0])
        for j in range(1, nb):
            acc = acc + _dot_nn(a_ref[j], w_ref[j])
        o_ref[...] = acc

    return pl.pallas_call(
        body, name=name, grid=(S // TM,),
        in_specs=[pl.BlockSpec((nb, TM, kb), lambda i: (0, i, 0)), pl.BlockSpec((nb, kb, n), lambda i: (0, 0, 0))],
        out_specs=pl.BlockSpec((TM, n), lambda i: (i, 0)), out_shape=_sds((S, n), F32),
        compiler_params=_cparams(1),
    )(a4, w4)


def _bwd_x_cols_blocked(name, dy8, wg, after):
    _, kdim, n = wg.shape
    nk = NDEV // 2

    def body(a_ref, b_ref, after_ref, o_ref, acc_ref):
        k = pl.program_id(1)
        part = _dot_nt(a_ref[0], b_ref[0]) + _dot_nt(a_ref[1], b_ref[1])

        @pl.when(k == 0)
        def _():
            acc_ref[...] = part

        @pl.when(k > 0)
        def _():
            acc_ref[...] += part

        @pl.when(k == nk - 1)
        def _():
            o_ref[...] = acc_ref[...]

    return pl.pallas_call(
        body, name=name, grid=(S // TM, nk),
        in_specs=[pl.BlockSpec((2, None, TM, n), lambda i, k: (0, k, i, 0)),
                  pl.BlockSpec((2, None, kdim, n), lambda i, k: (0, k, 0, 0)),
                  pl.BlockSpec(memory_space=pl.ANY)],
        out_specs=pl.BlockSpec((TM, kdim), lambda i, k: (i, 0)), out_shape=_sds((S, kdim), F32),
        scratch_shapes=[pltpu.VMEM((TM, kdim), F32)],
        compiler_params=_cparams(2),
    )(dy8.reshape(2, nk, S, n), wg.reshape(2, nk, kdim, n), after)


def _bwd_x_rows(name, dy, w, out_dtype, after=None):
    kdim, n = w.shape
    tkk = 512
    return _matmul(name, dy, w, mode="nt", grid=(S // TM, kdim // tkk, 1),
                   a_blk=(TM, n), a_map=lambda i, j, k: (i, 0),
                   b_blk=(tkk, n), b_map=lambda i, j, k: (j, 0),
                   o_shape=(S, kdim), o_blk=(TM, tkk), o_map=lambda i, j, k: (i, j), out_dtype=out_dtype, after=after)


DW_COLS = 768


def _bwd_w_cols(name, a, dy, n):
    kdim = a.shape[1]
    groups = DW_COLS // n
    return _matmul(name, a, dy, mode="tn", grid=(1, NDEV // groups, S // TK),
                   a_blk=(TK, kdim), a_map=lambda i, j, k: (k, 0),
                   b_blk=(TK, DW_COLS), b_map=lambda i, j, k: (k, j),
                   o_shape=(NDEV, kdim, n), o_blk=(groups, kdim, n) if groups > 1 else (None, kdim, n),
                   o_map=lambda i, j, k: (j, 0, 0), out_dtype=BF16, out_groups=groups)


def _bwd_x_plain(name, dy, w, after=None):
    kdim, n = w.shape
    tm = TM if n <= 3 * D else TM // 2
    return _matmul(name, dy, w, mode="nt", grid=(S // tm, 1, 1),
                   a_blk=(tm, n), a_map=lambda i, j, k: (i, 0),
                   b_blk=(kdim, n), b_map=lambda i, j, k: (0, 0),
                   o_shape=(S, kdim), o_blk=(tm, kdim), o_map=lambda i, j, k: (i, 0), out_dtype=F32, after=after)


def _bwd_w_cols_blocked(name, a, dy8):
    kdim = a.shape[1]
    n = dy8.shape[2]
    return _matmul(name, dy8, a, mode="tn", grid=(1, NDEV, S // TK),
                   a_blk=(None, TK, n), a_map=lambda i, j, k: (j, k, 0),
                   b_blk=(TK, kdim), b_map=lambda i, j, k: (k, 0),
                   o_shape=(NDEV, n, kdim), o_blk=(None, n, kdim), o_map=lambda i, j, k: (j, 0, 0), out_dtype=BF16)


def _bwd_w_rows(name, a, dy):
    kdim = a.shape[1]
    n = dy.shape[1]
    tmm = 512
    return _matmul(name, a, dy, mode="tn", grid=(kdim // tmm, 1, S // TK),
                   a_blk=(TK, tmm), a_map=lambda i, j, k: (k, i),
                   b_blk=(TK, n), b_map=lambda i, j, k: (k, 0),
                   o_shape=(kdim, n), o_blk=(tmm, n), o_map=lambda i, j, k: (i, 0), out_dtype=BF16)


def _bwd_w_kblocked(name, a4, dy):
    nb, _, kb = a4.shape
    n = dy.shape[1]
    return _matmul(name, a4, dy, mode="tn", grid=(nb, 1, S // TK),
                   a_blk=(None, TK, kb), a_map=lambda i, j, k: (i, k, 0),
                   b_blk=(TK, n), b_map=lambda i, j, k: (k, 0),
                   o_shape=(nb, kb, n), o_blk=(None, kb, n), o_map=lambda i, j, k: (i, 0, 0), out_dtype=BF16)


def _rstd(x):
    return lax.rsqrt(jnp.mean(x * x, axis=-1, keepdims=True) + RMS_EPS)


def _row_spec(tm=ROW_TILE, width=D):
    return pl.BlockSpec((tm, width), lambda i: (i, 0))


def _vec_spec(rows=1, width=D):
    return pl.BlockSpec((rows, width), lambda i: (0, 0))


def _rms_fwd(name, x, gains):
    n = len(gains)

    def body(x_ref, *refs):
        x_val = x_ref[...]
        xh = x_val * _rstd(x_val)
        for g_ref, o_ref in zip(refs[:n], refs[n:]):
            o_ref[...] = (xh * g_ref[...]).astype(o_ref.dtype)

    outs = pl.pallas_call(
        body, name=name, grid=(S // ROW_TILE,),
        in_specs=[_row_spec()] + [_vec_spec()] * n,
        out_specs=[_row_spec()] * n,
        out_shape=[_sds((S, D), BF16)] * n,
        compiler_params=_cparams(1),
    )(x, *gains)
    return list(outs)


def _resid_rms(name, h, y, g, next_gains):
    n = len(next_gains)

    def body(h_ref, y_ref, g_ref, *refs):
        y_val = y_ref[...]
        h_new = h_ref[...] + (y_val * _rstd(y_val)) * g_ref[...]
        refs[n][...] = h_new
        hh = h_new * _rstd(h_new)
        for g2_ref, o_ref in zip(refs[:n], refs[n + 1:]):
            o_ref[...] = (hh * g2_ref[...]).astype(o_ref.dtype)

    outs = pl.pallas_call(
        body, name=name, grid=(S // ROW_TILE,),
        in_specs=[_row_spec(), _row_spec(), _vec_spec()] + [_vec_spec()] * n,
        out_specs=[_row_spec()] * (n + 1), out_shape=[_sds((S, D), F32)] + [_sds((S, D), BF16)] * n,
        compiler_params=_cparams(1),
    )(h, y, g, *next_gains)
    return outs[0], list(outs[1:])


def _resid_rms_loss(name, h, y, g, target):
    def body(h_ref, y_ref, g_ref, t_ref, dh_ref, dy_ref, dg_ref, part_ref):
        y_val = y_ref[...]
        gain = g_ref[...]
        e = h_ref[...] + (y_val * _rstd(y_val)) * gain - t_ref[...]
        dh = e * (1.0 / D)
        dh_ref[...] = dh
        step = pl.program_id(0)
        dy_ref[...] = _norm_bwd_rows(y_val, gain, dh, dg_ref, step).astype(dy_ref.dtype)
        part = jnp.sum(e * e, axis=0, keepdims=True)

        @pl.when(step == 0)
        def _():
            part_ref[...] = part

        @pl.when(step > 0)
        def _():
            part_ref[...] += part

    return pl.pallas_call(
        body, name=name, grid=(S // ROW_TILE,),
        in_specs=[_row_spec(), _row_spec(), _vec_spec(), _row_spec()],
        out_specs=[_row_spec(), _row_spec(), _vec_spec(8), _vec_spec()],
        out_shape=[_sds((S, D), F32), _sds((S, D), BF16), _sds((8, D), F32), _sds((1, D), F32)],
        compiler_params=_cparams(1),
    )(h, y, g, target)


def _norm_bwd_rows(x_val, g, dn, dg_ref, step):
    r = _rstd(x_val)
    xh = x_val * r
    dxh = dn * g
    part = jnp.sum(dn * xh, axis=0, keepdims=True)

    @pl.when(step == 0)
    def _():
        dg_ref[...] = jnp.zeros_like(dg_ref)

    dg_ref[0:1, :] += part
    return r * (dxh - xh * jnp.mean(dxh * xh, axis=-1, keepdims=True))


def _rms_bwd(name, x, pairs, dres, out_dtype, then=None):
    n = len(pairs)
    has_res = dres is not None
    chained = then is not None

    def body(x_ref, *refs):
        g_refs = refs[0:2 * n:2]
        dn_refs = refs[1:2 * n:2]
        pos = 2 * n
        res_ref = refs[pos] if has_res else None
        pos += int(has_res)
        if chained:
            y_ref, gy_ref = refs[pos], refs[pos + 1]
            pos += 2
        dx_ref = refs[pos]
        dg_refs = refs[pos + 1:pos + 1 + n]
        step = pl.program_id(0)
        x_val = x_ref[...]
        acc = res_ref[...] if has_res else jnp.zeros_like(x_val)
        for g_ref, dn_ref, dg_ref in zip(g_refs, dn_refs, dg_refs):
            acc = acc + _norm_bwd_rows(x_val, g_ref[...], dn_ref[...].astype(F32), dg_ref, step)
        dx_ref[...] = acc.astype(dx_ref.dtype)
        if chained:
            dy_ref, dgy_ref = refs[pos + 1 + n], refs[pos + 2 + n]
            dy_ref[...] = _norm_bwd_rows(y_ref[...], gy_ref[...], acc, dgy_ref, step).astype(dy_ref.dtype)

    operands = [x]
    in_specs = [_row_spec()]
    for g, dn in pairs:
        operands += [g, dn]
        in_specs += [_vec_spec(), _row_spec()]
    if has_res:
        operands.append(dres)
        in_specs.append(_row_spec())
    if chained:
        operands += [then[0], then[1]]
        in_specs += [_row_spec(), _vec_spec()]
    extra = int(chained)
    outs = pl.pallas_call(
        body, name=name, grid=(S // ROW_TILE,),
        in_specs=in_specs,
        out_specs=[_row_spec()] + [_vec_spec(8)] * n + [_row_spec(), _vec_spec(8)] * extra,
        out_shape=[_sds((S, D), out_dtype)] + [_sds((8, D), F32)] * n + [_sds((S, D), BF16), _sds((8, D), F32)] * extra,
        compiler_params=_cparams(1),
    )(*operands)
    if chained:
        return outs[0], list(outs[1:1 + n]), outs[1 + n], outs[2 + n]
    return outs[0], list(outs[1:])


def _shift_down(u, prev8, k):
    r = pltpu.roll(u, k, 0)
    p = pltpu.roll(prev8, k, 0)
    row = lax.broadcasted_iota(jnp.int32, prev8.shape, 0)
    top = jnp.where(row < k, p, r[0:8])
    return jnp.concatenate([top, r[8:]], axis=0)


def _shift_up(u, next8, k):
    tm = u.shape[0]
    r = pltpu.roll(u, tm - k, 0)
    p = pltpu.roll(next8, 8 - k, 0)
    row = lax.broadcasted_iota(jnp.int32, next8.shape, 0)
    bot = jnp.where(row >= 8 - k, p, r[tm - 8:tm])
    return jnp.concatenate([r[:tm - 8], bot], axis=0)


CONV_TILE = 512


def _halo_prev(col):
    return pl.BlockSpec((8, D), lambda i: (jnp.maximum(i * (CONV_TILE // 8) - 1, 0), col))


def _halo_next(col):
    last = S // 8 - 1
    return pl.BlockSpec((8, D), lambda i: (jnp.minimum((i + 1) * (CONV_TILE // 8), last), col))


def _conv_fwd(name, z, cw):
    def body(b_ref, c_ref, h_ref, cp_ref, hp_ref, cw_ref, o_ref):
        i = pl.program_id(0)
        u = c_ref[...].astype(F32) * h_ref[...].astype(F32)
        up = cp_ref[...].astype(F32) * hp_ref[...].astype(F32)
        up = jnp.where(i > 0, up, 0.0)
        cv = cw_ref[0:1, :] * _shift_down(u, up, 2) + cw_ref[1:2, :] * _shift_down(u, up, 1) + cw_ref[2:3, :] * u
        o_ref[...] = (b_ref[...].astype(F32) * cv).astype(o_ref.dtype)

    col = lambda c: pl.BlockSpec((CONV_TILE, D), lambda i: (i, c))
    return pl.pallas_call(
        body, name=name, grid=(S // CONV_TILE,),
        in_specs=[col(0), col(1), col(2), _halo_prev(1), _halo_prev(2), _vec_spec(8)],
        out_specs=_row_spec(CONV_TILE), out_shape=_sds((S, D), BF16),
        compiler_params=_cparams(1),
    )(z, z, z, z, z, cw)


def _conv_bwd(name, z, dpre, cw):
    nsteps = S // CONV_TILE

    def body(b_ref, c_ref, h_ref, cp_ref, hp_ref, dp_ref, dpn_ref, bn_ref, cw_ref, dz_ref, dcw_ref):
        i = pl.program_id(0)
        b = b_ref[...].astype(F32)
        c = c_ref[...].astype(F32)
        h = h_ref[...].astype(F32)
        dp = dp_ref[...].astype(F32)
        u = c * h
        up = jnp.where(i > 0, cp_ref[...].astype(F32) * hp_ref[...].astype(F32), 0.0)
        s1 = _shift_down(u, up, 1)
        s2 = _shift_down(u, up, 2)
        w0, w1, w2 = cw_ref[0:1, :], cw_ref[1:2, :], cw_ref[2:3, :]
        cv = w0 * s2 + w1 * s1 + w2 * u
        dcv = dp * b
        dcvn = jnp.where(i < nsteps - 1, dpn_ref[...].astype(F32) * bn_ref[...].astype(F32), 0.0)
        du = w2 * dcv + w1 * _shift_up(dcv, dcvn, 1) + w0 * _shift_up(dcv, dcvn, 2)
        dz_ref[:, 0:D] = (dp * cv).astype(dz_ref.dtype)
        dz_ref[:, D:2 * D] = (du * h).astype(dz_ref.dtype)
        dz_ref[:, 2 * D:3 * D] = (du * c).astype(dz_ref.dtype)

        @pl.when(i == 0)
        def _():
            dcw_ref[...] = jnp.zeros_like(dcw_ref)

        dcw_ref[0:1, :] += jnp.sum(dcv * s2, axis=0, keepdims=True)
        dcw_ref[1:2, :] += jnp.sum(dcv * s1, axis=0, keepdims=True)
        dcw_ref[2:3, :] += jnp.sum(dcv * u, axis=0, keepdims=True)

    col = lambda c: pl.BlockSpec((CONV_TILE, D), lambda i: (i, c))
    return pl.pallas_call(
        body, name=name, grid=(nsteps,),
        in_specs=[col(0), col(1), col(2), _halo_prev(1), _halo_prev(2),
                  _row_spec(CONV_TILE), _halo_next(0), _halo_next(0), _vec_spec(8)],
        out_specs=[pl.BlockSpec((CONV_TILE, 3 * D), lambda i: (i, 0)), _vec_spec(8)],
        out_shape=[_sds((S, 3 * D), BF16), _sds((8, D), F32)],
        compiler_params=_cparams(1),
    )(z, z, z, z, z, dpre, dpre, z, cw)


FFN_TM = 2048
_GU_BLOCK = pl.BlockSpec((2, None, FFN_TM, FB), lambda i, j: (0, j, i, 0))


def _gate_up_act(name, a, wg):
    kdim = a.shape[1]

    def body(a_ref, wgate_ref, wup_ref, gu_ref, act_ref):
        x = a_ref[...]
        g = _dot_nn(x, wgate_ref[...])
        u = _dot_nn(x, wup_ref[...])
        gu_ref[0] = g.astype(gu_ref.dtype)
        gu_ref[1] = u.astype(gu_ref.dtype)
        act_ref[...] = (g * jax.nn.sigmoid(g) * u).astype(act_ref.dtype)

    return pl.pallas_call(
        body, name=name, grid=(S // FFN_TM, NFB),
        in_specs=[pl.BlockSpec((FFN_TM, kdim), lambda i, j: (i, 0)),
                  pl.BlockSpec((None, kdim, FB), lambda i, j: (j, 0, 0)),
                  pl.BlockSpec((None, kdim, FB), lambda i, j: (j + NFB, 0, 0))],
        out_specs=[_GU_BLOCK, pl.BlockSpec((None, FFN_TM, FB), lambda i, j: (j, i, 0))],
        out_shape=[_sds((2, NFB, S, FB), BF16), _sds((NFB, S, FB), BF16)],
        compiler_params=_cparams(2),
    )(a, wg, wg)


def _down_dx_act_bwd(name, df, w4, gu):
    _, kb, n = w4.shape

    def body(df_ref, w_ref, gu_ref, o_ref):
        d = _dot_nt(df_ref[...], w_ref[...])
        g = gu_ref[0].astype(F32)
        u = gu_ref[1].astype(F32)
        sg = jax.nn.sigmoid(g)
        o_ref[0] = (d * u * sg * (1.0 + g * (1.0 - sg))).astype(o_ref.dtype)
        o_ref[1] = (d * g * sg).astype(o_ref.dtype)

    return pl.pallas_call(
        body, name=name, grid=(S // FFN_TM, NFB),
        in_specs=[pl.BlockSpec((FFN_TM, n), lambda i, j: (i, 0)), pl.BlockSpec((None, kb, n), lambda i, j: (j, 0, 0)),
                  _GU_BLOCK],
        out_specs=_GU_BLOCK, out_shape=_sds((2, NFB, S, FB), BF16),
        compiler_params=_cparams(2),
    )(df, w4, gu)


def _rope_tables(name, pos_col, inv_freq_row):
    def body(pos_ref, f_ref, cos_ref, sin_ref):
        ang = pos_ref[...].astype(F32) * f_ref[...]
        lane = lax.broadcasted_iota(jnp.int32, ang.shape, 1)
        s = jnp.sin(ang)
        cos_ref[...] = jnp.cos(ang)
        sin_ref[...] = jnp.where((lane % HEAD_DIM) < HEAD_DIM // 2, -s, s)

    tab = pl.BlockSpec((ROW_TILE, 128), lambda i: (i, 0))
    return pl.pallas_call(
        body, name=name, grid=(S // ROW_TILE,),
        in_specs=[pl.BlockSpec((ROW_TILE, 1), lambda i: (i, 0)), _vec_spec(1, 128)],
        out_specs=[tab, tab], out_shape=[_sds((S, 128), F32)] * 2,
        compiler_params=_cparams(1),
    )(pos_col, inv_freq_row)


def _swap_halves(t):
    lane = lax.broadcasted_iota(jnp.int32, t.shape, 1)
    first = (lane % HEAD_DIM) < HEAD_DIM // 2
    return jnp.where(first, pltpu.roll(t, 128 - HEAD_DIM // 2, 1), pltpu.roll(t, HEAD_DIM // 2, 1))


NCHUNK = D // 128


def _chunk(c, base=0):
    return slice(base + c * 128, base + (c + 1) * 128)


def _class_rows(r, d, tm):
    return pl.ds(r, tm // d, stride=d) if d > 1 else slice(None)


def _class_block(d, tm):
    return pl.BlockSpec((tm // d, d * D), lambda i: (i, 0))


def _tokens_from_classes(blk_ref, tmp_ref, d, tm):
    for r in range(d):
        for c in range(NCHUNK):
            tmp_ref[c, _class_rows(r, d, tm), :] = blk_ref[:, _chunk(c, r * D)].astype(F32)


def _classes_from_tokens(tmp_ref, blk_ref, d, tm):
    for r in range(d):
        for c in range(NCHUNK):
            blk_ref[:, _chunk(c, r * D)] = tmp_ref[c, _class_rows(r, d, tm), :].astype(blk_ref.dtype)


def _qkv_classes(name, n2, nk, wq, wkv, g, d, tables):
    def emit(acc, cos_ref, sin_ref, o_ref, tmp_ref, scale):
        for c in range(NCHUNK):
            tmp_ref[c] = acc[:, _chunk(c)]
        for r in range(d):
            rows = _class_rows(r, d, TM)
            if scale is not None:
                cs = cos_ref[rows, :]
                sn = sin_ref[rows, :]
            for c in range(NCHUNK):
                x = tmp_ref[c, rows, :]
                if scale is not None:
                    x = (x * cs + _swap_halves(x) * sn) * scale
                o_ref[:, _chunk(c, r * D)] = x.astype(o_ref.dtype)

    def body(n2_ref, nk_ref, wq_ref, wk_ref, wv_ref, cos_ref, sin_ref, q_ref, k_ref, v_ref, tmp_ref):
        emit(_dot_nn(n2_ref[...], wq_ref[...]), cos_ref, sin_ref, q_ref, tmp_ref, HEAD_DIM ** -0.5)
        x = nk_ref[...]
        emit(_dot_nn(x, wk_ref[...]), cos_ref, sin_ref, k_ref, tmp_ref, 1.0)
        emit(_dot_nn(x, wv_ref[...]), cos_ref, sin_ref, v_ref, tmp_ref, None)

    nbr = len(DILATIONS)
    act = pl.BlockSpec((TM, D), lambda i: (i, 0))
    tab = pl.BlockSpec((TM, 128), lambda i: (i, 0))
    wcol = lambda col: pl.BlockSpec((D, D), lambda i: (0, col))
    return pl.pallas_call(
        body, name=name, grid=(S // TM,),
        in_specs=[act, act, wcol(g), wcol(g), wcol(nbr + g), tab, tab],
        out_specs=[_class_block(d, TM)] * 3, out_shape=[_sds((S // d, d * D), BF16)] * 3,
        scratch_shapes=[pltpu.VMEM((NCHUNK, TM, 128), F32)],
        compiler_params=_cparams(1),
    )(n2, nk, wq, wkv, wkv, *tables)


ATTN_CHAINS = 8


def _attn_units(d):
    nblk = S // d // BAND
    return max(1, 2 * ATTN_CHAINS // nblk)


def _class_spec(d):
    return pl.BlockSpec((S // d, 128 * _attn_units(d)), lambda cb: (0, cb))


def _dot_nt(a, b):
    return lax.dot_general(a, b, _DIMS["nt"], preferred_element_type=F32)


def _dot_tn(a, b):
    return lax.dot_general(a, b, _DIMS["tn"], preferred_element_type=F32)


def _dot_nn(a, b):
    return lax.dot_general(a, b, _DIMS["nn"], preferred_element_type=F32)


def _band_mask(nkeys):
    qi = lax.broadcasted_iota(jnp.int32, (2 * BAND, nkeys), 0) % BAND
    kj = lax.broadcasted_iota(jnp.int32, (2 * BAND, nkeys), 1)
    if nkeys == BAND:
        return kj <= qi
    dist = qi + BAND - kj
    return (dist >= 0) & (dist <= BAND)


def _stack_heads(x):
    row = lax.broadcasted_iota(jnp.int32, (2 * BAND, 128), 0)
    lane = lax.broadcasted_iota(jnp.int32, (2 * BAND, 128), 1)
    keep = (row < BAND) == (lane < HEAD_DIM)
    return jnp.where(keep, jnp.concatenate([x, x], axis=0), jnp.zeros((), x.dtype))


def _unstack(x2):
    first_head = lax.broadcasted_iota(jnp.int32, (BAND, 128), 1) < HEAD_DIM
    return jnp.where(first_head, x2[:BAND], x2[BAND:])


def _for_later_blocks(nblk, units, fn):
    all_lanes = [slice(u * 128, (u + 1) * 128) for u in range(units)]
    unroll = max(1, ATTN_CHAINS // units)
    trips = (nblk - 1) // unroll
    if trips > 1:
        def step(i, carry):
            for j in range(unroll):
                for lanes in all_lanes:
                    fn(pl.multiple_of((1 + i * unroll + j) * BAND, BAND), lanes)
            return carry

        lax.fori_loop(0, trips, step, 0)
    else:
        trips = 0
    for sb in range(1 + trips * unroll, nblk):
        for lanes in all_lanes:
            fn(sb * BAND, lanes)


def _attn_fwd(name, q, k, v, d):
    nblk = S // d // BAND
    units = _attn_units(d)

    def body(q_ref, k_ref, v_ref, o_ref, lse_ref):
        def block(r0, k0, nkeys, lanes):
            q2 = _stack_heads(q_ref[pl.ds(r0, BAND), lanes])
            s = jnp.where(_band_mask(nkeys), _dot_nt(q2, k_ref[pl.ds(k0, nkeys), lanes]), NEG_INF)
            m = jnp.max(s, axis=-1, keepdims=True)
            p = jnp.exp(s - m)
            l = jnp.sum(p, axis=-1, keepdims=True)
            o2 = _dot_nn(p.astype(BF16), v_ref[pl.ds(k0, nkeys), lanes]) / l
            lse2 = jnp.broadcast_to(m + jnp.log(l), (2 * BAND, 128))
            o_ref[pl.ds(r0, BAND), lanes] = _unstack(o2).astype(o_ref.dtype)
            lse_ref[pl.ds(r0, BAND), lanes] = _unstack(lse2)

        for u in range(units):
            block(0, 0, BAND, slice(u * 128, (u + 1) * 128))

        _for_later_blocks(nblk, units, lambda r0, lanes: block(r0, r0 - BAND, 2 * BAND, lanes))

    spec = _class_spec(d)
    return pl.pallas_call(
        body, name=name, grid=(8 * d // units,),
        in_specs=[spec] * 3, out_specs=[spec] * 2,
        out_shape=[_sds((S // d, d * D), BF16), _sds((S // d, d * D), F32)],
        compiler_params=_cparams(1),
    )(q, k, v)


def _attn_bwd(name, q, k, v, do, lse, dd, d):
    nblk = S // d // BAND
    units = _attn_units(d)

    def body(q_ref, k_ref, v_ref, do_ref, lse_ref, dd_ref, dq_ref, dk_out, dv_out, dk_ref, dv_ref):
        def column(ref, r0, lanes, nkeys):
            tile = ref[pl.ds(r0, BAND), lanes]
            other = pltpu.roll(tile, HEAD_DIM, 1)
            first_head = lax.broadcasted_iota(jnp.int32, tile.shape, 1) < HEAD_DIM
            both = jnp.concatenate([jnp.where(first_head, tile, other), jnp.where(first_head, other, tile)], axis=0)
            return both if nkeys == BAND else jnp.concatenate([both, both], axis=1)

        def block(r0, k0, nkeys, lanes, first):
            q2 = _stack_heads(q_ref[pl.ds(r0, BAND), lanes])
            do2 = _stack_heads(do_ref[pl.ds(r0, BAND), lanes])
            kk = k_ref[pl.ds(k0, nkeys), lanes]
            vv = v_ref[pl.ds(k0, nkeys), lanes]
            s = jnp.where(_band_mask(nkeys), _dot_nt(q2, kk), NEG_INF)
            p = jnp.exp(s - column(lse_ref, r0, lanes, nkeys))
            ds = (p * (_dot_nt(do2, vv) - column(dd_ref, r0, lanes, nkeys))).astype(BF16)
            dq_ref[pl.ds(r0, BAND), lanes] = _unstack(_dot_nn(ds, kk)).astype(dq_ref.dtype)
            dk_part = _dot_tn(ds, q2)
            dv_part = _dot_tn(p.astype(BF16), do2)
            if first:
                dk_ref[pl.ds(k0, nkeys), lanes] = dk_part
                dv_ref[pl.ds(k0, nkeys), lanes] = dv_part
            else:
                dk_ref[pl.ds(k0, BAND), lanes] += dk_part[:BAND]
                dv_ref[pl.ds(k0, BAND), lanes] += dv_part[:BAND]
                dk_ref[pl.ds(k0 + BAND, BAND), lanes] = dk_part[BAND:]
                dv_ref[pl.ds(k0 + BAND, BAND), lanes] = dv_part[BAND:]

        for u in range(units):
            block(0, 0, BAND, slice(u * 128, (u + 1) * 128), True)

        _for_later_blocks(nblk, units, lambda r0, lanes: block(r0, r0 - BAND, 2 * BAND, lanes, False))
        dk_out[...] = dk_ref[...].astype(dk_out.dtype)
        dv_out[...] = dv_ref[...].astype(dv_out.dtype)

    spec = _class_spec(d)
    return pl.pallas_call(
        body, name=name, grid=(8 * d // units,),
        in_specs=[spec] * 6, out_specs=[spec] * 3,
        out_shape=[_sds((S // d, d * D), BF16)] * 3,
        scratch_shapes=[pltpu.VMEM((S // d, 128 * units), F32)] * 2,
        compiler_params=_cparams(1),
    )(q, k, v, do, lse, dd)


MIX_TILE = 256
DILATIONS = tuple(d for _, d in BRANCHES)


def _branch_weights(la, lb, lc):
    m = jnp.maximum(jnp.maximum(la, lb), lc)
    ea, eb, ec = jnp.exp(la - m), jnp.exp(lb - m), jnp.exp(lc - m)
    den = ea + eb + ec
    return ea / den, eb / den, ec / den


def _mix_operands(outs, lses):
    specs = [_class_block(d, MIX_TILE) for d in DILATIONS] * 2
    scratch = [pltpu.VMEM((NCHUNK, MIX_TILE, 128), F32)] * 4
    return list(outs) + list(lses), specs, scratch


def _mix_fwd(name, outs, lses):
    def body(o0, o1, o2, l0, l1, l2, o_ref, to1, to2, tl1, tl2):
        for blk, tmp, d in ((o1, to1, DILATIONS[1]), (o2, to2, DILATIONS[2]), (l1, tl1, DILATIONS[1]), (l2, tl2, DILATIONS[2])):
            _tokens_from_classes(blk, tmp, d, MIX_TILE)
        for c in range(NCHUNK):
            wa, wb, wc = _branch_weights(l0[:, _chunk(c)], tl1[c], tl2[c])
            o_ref[:, _chunk(c)] = (wa * o0[:, _chunk(c)].astype(F32) + wb * to1[c] + wc * to2[c]).astype(o_ref.dtype)

    operands, specs, scratch = _mix_operands(outs, lses)
    return pl.pallas_call(
        body, name=name, grid=(S // MIX_TILE,),
        in_specs=specs, out_specs=_row_spec(MIX_TILE), out_shape=_sds((S, D), BF16),
        scratch_shapes=scratch, compiler_params=_cparams(1),
    )(*operands)


def _head_sum(x, ones_blockdiag):
    hi = x.astype(BF16)
    r1 = x - hi.astype(F32)
    mid = r1.astype(BF16)
    lo = (r1 - mid.astype(F32)).astype(BF16)
    return _dot_nn(hi, ones_blockdiag) + _dot_nn(mid, ones_blockdiag) + _dot_nn(lo, ones_blockdiag)


def _mix_bwd(name, do, outs, lses, ones_blockdiag):
    def body(do_ref, o0, o1, o2, l0, l1, l2, ones_ref, d0, d1, d2, t0, t1, t2,
             to1, to2, tl1, tl2, td1, td2, tt1, tt2):
        for blk, tmp, d in ((o1, to1, DILATIONS[1]), (o2, to2, DILATIONS[2]), (l1, tl1, DILATIONS[1]), (l2, tl2, DILATIONS[2])):
            _tokens_from_classes(blk, tmp, d, MIX_TILE)
        ones = ones_ref[...]
        for c in range(NCHUNK):
            w = _branch_weights(l0[:, _chunk(c)], tl1[c], tl2[c])
            dov = do_ref[:, _chunk(c)]
            o = w[0] * o0[:, _chunk(c)].astype(F32) + w[1] * to1[c] + w[2] * to2[c]
            t = _head_sum(dov * o, ones)
            d0[:, _chunk(c)] = (w[0] * dov).astype(d0.dtype)
            t0[:, _chunk(c)] = w[0] * t
            td1[c], tt1[c] = w[1] * dov, w[1] * t
            td2[c], tt2[c] = w[2] * dov, w[2] * t
        for tmp, blk, d in ((td1, d1, DILATIONS[1]), (tt1, t1, DILATIONS[1]), (td2, d2, DILATIONS[2]), (tt2, t2, DILATIONS[2])):
            _classes_from_tokens(tmp, blk, d, MIX_TILE)

    operands, specs, scratch = _mix_operands(outs, lses)
    out_specs = [_class_block(d, MIX_TILE) for d in DILATIONS] * 2
    out_shape = [_sds((S // d, d * D), BF16) for d in DILATIONS] + [_sds((S // d, d * D), F32) for d in DILATIONS]
    return pl.pallas_call(
        body, name=name, grid=(S // MIX_TILE,),
        in_specs=[_row_spec(MIX_TILE)] + specs + [_vec_spec(128, 128)],
        out_specs=out_specs, out_shape=out_shape,
        scratch_shapes=scratch + [pltpu.VMEM((NCHUNK, MIX_TILE, 128), F32)] * 4,
        compiler_params=_cparams(1),
    )(do, *operands, ones_blockdiag)


def _attn_bwd_post(name, grads, cos_t, sin_t):
    tm = MIX_TILE
    scale = HEAD_DIM ** -0.5

    def unrope(x, cs, sn):
        return x * cs - _swap_halves(x) * sn

    def body(*refs):
        in_refs = refs[:9]
        cos_ref, sin_ref, dq_ref, dkv_ref, tmp_ref = refs[9:]
        cs = cos_ref[...]
        sn = sin_ref[...]
        for g, d in enumerate(DILATIONS):
            for which, blk in enumerate(in_refs[3 * g:3 * g + 3]):
                if d > 1:
                    _tokens_from_classes(blk, tmp_ref, d, tm)
                for c in range(NCHUNK):
                    x = tmp_ref[c] if d > 1 else blk[:, _chunk(c)].astype(F32)
                    if which == 0:
                        dq_ref[:, _chunk(c, g * D)] = (unrope(x, cs, sn) * scale).astype(dq_ref.dtype)
                    elif which == 1:
                        dkv_ref[:, _chunk(c, g * D)] = unrope(x, cs, sn).astype(dkv_ref.dtype)
                    else:
                        dkv_ref[:, _chunk(c, QW + g * D)] = x.astype(dkv_ref.dtype)

    operands = [a for branch in grads for a in branch]
    tab = pl.BlockSpec((tm, 128), lambda i: (i, 0))
    return pl.pallas_call(
        body, name=name, grid=(S // tm,),
        in_specs=[_class_block(d, tm) for d in DILATIONS for _ in range(3)] + [tab, tab],
        out_specs=[pl.BlockSpec((tm, QW), lambda i: (i, 0)), pl.BlockSpec((tm, 2 * QW), lambda i: (i, 0))],
        out_shape=[_sds((S, QW), BF16), _sds((S, 2 * QW), BF16)],
        scratch_shapes=[pltpu.VMEM((NCHUNK, tm, 128), F32)],
        compiler_params=_cparams(1),
    )(*operands, cos_t, sin_t)


def _adamw(name, parts, w, m, v, layer=None, other=None):
    n, rows, cols = parts.shape
    tr = rows
    for cand in (256, 176, 128, 64, 32, 16, 8):
        if rows % cand == 0:
            tr = cand
            break
    n_other = 0 if other is None else len(other)

    def body(p_ref, w_ref, m_ref, v_ref, *refs):
        g_ref, d_ref, nm_ref, nv_ref = refs[n_other:]
        g = p_ref[0].astype(F32)
        for j in range(1, n):
            g = g + p_ref[j].astype(F32)
        g_ref[...] = g
        d_ref[...], nm_ref[...], nv_ref[...] = _adam_update(g, w_ref[...], m_ref[...], v_ref[...])

    if layer is None:
        blk = pl.BlockSpec((tr, cols), lambda i: (i, 0))
        shape = (rows, cols)
    else:
        blk = pl.BlockSpec((None, tr, cols), lambda i: (layer, i, 0))
        shape = w.shape
    return pl.pallas_call(
        body, name=name, grid=(rows // tr,),
        in_specs=[pl.BlockSpec((n, tr, cols), lambda i: (0, i, 0)), blk, blk, blk]
                 + [pl.BlockSpec(memory_space=pl.ANY)] * n_other,
        out_specs=[blk] * 4, out_shape=[_sds(shape, F32)] * 4,
        input_output_aliases={4 + i: i for i in range(n_other)},
        compiler_params=_cparams(1),
    )(parts, w, m, v, *(other or ()))


def _adam_update(g, w, m, v):
    c1 = 1.0 / (1.0 - ADAM_B1 ** ADAM_STEP)
    c2 = 1.0 / (1.0 - ADAM_B2 ** ADAM_STEP)
    nm = ADAM_B1 * m + (1.0 - ADAM_B1) * g
    nv = ADAM_B2 * v + (1.0 - ADAM_B2) * (g * g)
    return -ADAM_LR * ((nm * c1) / (jnp.sqrt(nv * c2) + ADAM_EPS) + ADAM_WD * w), nm, nv


GAIN_ROWS = 16


def _pack_small(name, gain_tiles, taps, sq):
    ng = len(gain_tiles)

    def body(*refs):
        o_ref = refs[-1]
        o_ref[...] = jnp.zeros_like(o_ref)
        for i in range(ng):
            o_ref[i:i + 1, :] = refs[i][0:1, :]
        o_ref[ng:ng + 3, :] = refs[ng][0:3, :]
        o_ref[ng + 3:ng + 4, :] = refs[ng + 1][...]

    return pl.pallas_call(body, name=name, out_shape=_sds((GAIN_ROWS, D), F32))(*gain_tiles, taps, sq)


def _adamw_gains(name, parts, params):
    np_ = len(params)
    shapes = [w.shape for w, _, _ in params]

    def body(p_ref, *refs):
        ins, outs = refs[:3 * np_], refs[3 * np_:]

        def total(lo, rows):
            g = p_ref[0, lo:lo + rows, :]
            for j in range(1, NDEV):
                g = g + p_ref[j, lo:lo + rows, :]
            return g

        lo = 0
        for i, shape in enumerate(shapes):
            g = total(lo, shape[0])
            lo += shape[0]
            w_ref, m_ref, v_ref = ins[3 * i:3 * i + 3]
            g_ref, d_ref, nm_ref, nv_ref = outs[4 * i:4 * i + 4]
            g_ref[...] = g
            d_ref[...], nm_ref[...], nv_ref[...] = _adam_update(g, w_ref[...], m_ref[...], v_ref[...])
        taps_ref, loss_ref = outs[-2], outs[-1]
        taps_ref[...] = jnp.zeros_like(taps_ref)
        taps_ref[0:3, :] = total(lo, 3)
        loss_ref[...] = jnp.sum(total(lo + 3, 1), axis=-1, keepdims=True) * (0.5 / D)

    out_shape = [_sds(shape, F32) for shape in shapes for _ in range(4)] + [_sds((8, D), F32), _sds((1, 1), F32)]
    outs = pl.pallas_call(body, name=name, out_shape=out_shape)(parts, *[a for p in params for a in p])
    return [list(outs[4 * i:4 * i + 4]) for i in range(np_)], outs[-2], outs[-1].reshape(())


def _exchange(name, arrays, kind):
    n = len(arrays)
    gather = kind == "gather"
    out_shape = [_sds((NDEV,) + a.shape if gather else a.shape, a.dtype) for a in arrays]

    def body(*refs):
        srcs, outs = refs[:n], refs[n:2 * n]
        send_sems, recv_sems, local_sems = refs[2 * n:]
        x, y, c = lax.axis_index("x"), lax.axis_index("y"), lax.axis_index("c")
        me = 4 * x + 2 * y + c
        pending = []
        for t in range(n):
            own = pltpu.make_async_copy(srcs[t] if gather else srcs[t].at[me], outs[t].at[me], local_sems.at[t])
            own.start()
            pending.append(own)
            for rel in range(1, NDEV):
                px = 1 - x if rel & 4 else x
                py = 1 - y if rel & 2 else y
                pc = 1 - c if rel & 1 else c
                peer = 4 * px + 2 * py + pc
                send = pltpu.make_async_remote_copy(
                    src_ref=srcs[t] if gather else srcs[t].at[peer], dst_ref=outs[t].at[me],
                    send_sem=send_sems.at[t, rel - 1], recv_sem=recv_sems.at[t, rel - 1],
                    device_id=(px, py, pc), device_id_type=MESH)
                send.start()
                arrive = pltpu.make_async_remote_copy(
                    src_ref=srcs[t] if gather else srcs[t].at[me], dst_ref=outs[t].at[peer],
                    send_sem=send_sems.at[t, rel - 1], recv_sem=recv_sems.at[t, rel - 1],
                    device_id=(px, py, pc), device_id_type=MESH)
                pending.append((send, arrive))
        for item in pending:
            if isinstance(item, tuple):
                item[0].wait_send()
                item[1].wait_recv()
            else:
                item.wait()

    any_spec = pl.BlockSpec(memory_space=pl.ANY)
    outs = pl.pallas_call(
        body, name=name,
        in_specs=[any_spec] * n, out_specs=[any_spec] * n, out_shape=out_shape,
        scratch_shapes=[pltpu.SemaphoreType.DMA((n, NDEV - 1)), pltpu.SemaphoreType.DMA((n, NDEV - 1)),
                        pltpu.SemaphoreType.DMA((n,))],
    )(*arrays)
    return list(outs)


_HBM_SPEC = pl.BlockSpec(memory_space=pltpu.HBM)
_SEM_SPEC = pl.BlockSpec(memory_space=pltpu.SEMAPHORE)
_DATAFLOW = pltpu.SideEffectType.DATAFLOW_SIDE_EFFECTING


def _peers():
    x, y, c = lax.axis_index("x"), lax.axis_index("y"), lax.axis_index("c")
    out = []
    for rel in range(1, NDEV):
        px = 1 - x if rel & 4 else x
        py = 1 - y if rel & 2 else y
        pc = 1 - c if rel & 1 else c
        out.append((rel - 1, (px, py, pc), 4 * px + 2 * py + pc))
    return 4 * x + 2 * y + c, out


def _hbm(a):
    return pltpu.HBM(a.shape, a.dtype)


def _own_slot(a, me, kind):
    mine = a[None] if kind == "gather" else lax.dynamic_slice_in_dim(a, me, 1, axis=0)
    shape = (NDEV,) + mine.shape[1:]
    return lax.dynamic_update_slice_in_dim(lax.empty(shape, a.dtype), mine, me, axis=0)


def _exchange_start(name, arrays, me, kind):
    n = len(arrays)
    gather = kind == "gather"
    lands = [_own_slot(a, me, kind) for a in arrays]

    def body(*refs):
        src_refs, land_refs = refs[:n], refs[n:2 * n]
        send_sems, recv_sems = refs[2 * n], refs[2 * n + 1]
        token = refs[-1]
        my_block, peers = _peers()
        for t in range(n):
            for slot, dev, block in peers:
                pltpu.make_async_remote_copy(
                    src_ref=src_refs[t] if gather else src_refs[t].at[block], dst_ref=land_refs[t].at[my_block],
                    send_sem=send_sems.at[t * (NDEV - 1) + slot], recv_sem=recv_sems.at[t * (NDEV - 1) + slot],
                    device_id=dev, device_id_type=MESH).start()
        token[...] = jnp.zeros_like(token)

    operands = [pltpu.with_memory_space_constraint(a, pltpu.HBM) for a in list(arrays) + lands]
    outs = pl.pallas_call(
        body, name=name,
        out_shape=(pltpu.SemaphoreType.DMA((n * (NDEV - 1),)), pltpu.SemaphoreType.DMA((n * (NDEV - 1),)),
                   *[_hbm(a) for a in operands], _sds((8, 128), F32)),
        in_specs=[_HBM_SPEC] * (2 * n),
        out_specs=(_SEM_SPEC, _SEM_SPEC, *[_HBM_SPEC] * (2 * n), pl.BlockSpec(memory_space=pltpu.VMEM)),
        input_output_aliases={i: 2 + i for i in range(2 * n)},
        compiler_params=pltpu.CompilerParams(has_side_effects=_DATAFLOW),
    )(*operands)
    return (outs[0], outs[1], list(outs[2:2 + n]), list(outs[2 + n:2 + 2 * n])), outs[-1]


def _exchange_wait(name, started, t, after, kind):
    send_sems, recv_sems, srcs, lands = started
    gather = kind == "gather"

    def body(src_ref, land_ref, send_ref, recv_ref, after_ref, src_out, land_out):
        _, peers = _peers()
        for slot, dev, block in peers:
            copy = pltpu.make_async_remote_copy(
                src_ref=src_ref if gather else src_ref.at[block], dst_ref=land_ref.at[block],
                send_sem=send_ref.at[t * (NDEV - 1) + slot], recv_sem=recv_ref.at[t * (NDEV - 1) + slot],
                device_id=dev, device_id_type=MESH)
            copy.wait_send()
            copy.wait_recv()

    return pl.pallas_call(
        body, name=name, out_shape=(_hbm(srcs[t]), _hbm(lands[t])),
        in_specs=(_HBM_SPEC, _HBM_SPEC, _SEM_SPEC, _SEM_SPEC, pl.BlockSpec(memory_space=pl.ANY)),
        out_specs=(_HBM_SPEC, _HBM_SPEC), input_output_aliases={0: 0, 1: 1},
        compiler_params=pltpu.CompilerParams(has_side_effects=_DATAFLOW),
    )(srcs[t], lands[t], send_sems, recv_sems, after)[1]


DIRECT_RELS = (1, 2, 4, 6)
RELAY_RELS = (2, 4, 6)


def _rel_peer(rel):
    x, y, c = lax.axis_index("x"), lax.axis_index("y"), lax.axis_index("c")
    px = 1 - x if rel & 4 else x
    py = 1 - y if rel & 2 else y
    pc = 1 - c if rel & 1 else c
    return (px, py, pc), 4 * px + 2 * py + pc


def _gather_start(name, shards, me):
    n, nr = len(shards), len(DIRECT_RELS)
    lands = [_own_slot(a, me, "gather") for a in shards]

    def body(*refs):
        src_refs, land_refs = refs[:n], refs[n:2 * n]
        send_sems, recv_sems = refs[2 * n], refs[2 * n + 1]
        _, my_block = _rel_peer(0)
        for t in range(n):
            for s, rel in enumerate(DIRECT_RELS):
                dev, _ = _rel_peer(rel)
                pltpu.make_async_remote_copy(
                    src_ref=src_refs[t], dst_ref=land_refs[t].at[my_block],
                    send_sem=send_sems.at[t * nr + s], recv_sem=recv_sems.at[t * nr + s],
                    device_id=dev, device_id_type=MESH).start()

    operands = [pltpu.with_memory_space_constraint(a, pltpu.HBM) for a in list(shards) + lands]
    outs = pl.pallas_call(
        body, name=name,
        out_shape=(pltpu.SemaphoreType.DMA((n * nr,)), pltpu.SemaphoreType.DMA((n * nr,)), *[_hbm(a) for a in operands]),
        in_specs=[_HBM_SPEC] * (2 * n), out_specs=(_SEM_SPEC, _SEM_SPEC, *[_HBM_SPEC] * (2 * n)),
        input_output_aliases={i: 2 + i for i in range(2 * n)},
        compiler_params=pltpu.CompilerParams(has_side_effects=_DATAFLOW),
    )(*operands)
    return outs[0], outs[1], list(outs[2:2 + n]), list(outs[2 + n:2 + 2 * n])


def _gather_wait(name, started, ts, after):
    send_sems, recv_sems, srcs, lands = started
    m, nr = len(ts), len(DIRECT_RELS)

    def body(*refs):
        src_refs, land_refs = refs[:m], refs[m:2 * m]
        send_ref, recv_ref = refs[2 * m], refs[2 * m + 1]
        for i, t in enumerate(ts):
            for s, rel in enumerate(DIRECT_RELS):
                dev, block = _rel_peer(rel)
                copy = pltpu.make_async_remote_copy(
                    src_ref=src_refs[i], dst_ref=land_refs[i].at[block],
                    send_sem=send_ref.at[t * nr + s], recv_sem=recv_ref.at[t * nr + s],
                    device_id=dev, device_id_type=MESH)
                copy.wait_send()
                copy.wait_recv()

    operands = [srcs[t] for t in ts] + [lands[t] for t in ts]
    outs = pl.pallas_call(
        body, name=name, out_shape=tuple(_hbm(a) for a in operands),
        in_specs=[_HBM_SPEC] * (2 * m) + [_SEM_SPEC, _SEM_SPEC, pl.BlockSpec(memory_space=pl.ANY)],
        out_specs=tuple([_HBM_SPEC] * (2 * m)), input_output_aliases={i: i for i in range(2 * m)},
        compiler_params=pltpu.CompilerParams(has_side_effects=_DATAFLOW),
    )(*operands, send_sems, recv_sems, after)
    return list(outs[m:])


def _relay_start(name, lands):
    m, nr = len(lands), len(RELAY_RELS)

    def body(*refs):
        land_refs, send_sems, recv_sems = refs[:m], refs[m], refs[m + 1]
        sibling, _ = _rel_peer(1)
        for i in range(m):
            for s, rel in enumerate(RELAY_RELS):
                _, block = _rel_peer(rel)
                pltpu.make_async_remote_copy(
                    src_ref=land_refs[i].at[block], dst_ref=land_refs[i].at[block],
                    send_sem=send_sems.at[i * nr + s], recv_sem=recv_sems.at[i * nr + s],
                    device_id=sibling, device_id_type=MESH).start()

    outs = pl.pallas_call(
        body, name=name,
        out_shape=(pltpu.SemaphoreType.DMA((m * nr,)), pltpu.SemaphoreType.DMA((m * nr,)), *[_hbm(a) for a in lands]),
        in_specs=[_HBM_SPEC] * m, out_specs=(_SEM_SPEC, _SEM_SPEC, *[_HBM_SPEC] * m),
        input_output_aliases={i: 2 + i for i in range(m)},
        compiler_params=pltpu.CompilerParams(has_side_effects=_DATAFLOW),
    )(*lands)
    return outs[0], outs[1], list(outs[2:])


def _relay_wait(name, relayed, after):
    send_sems, recv_sems, lands = relayed
    m, nr = len(lands), len(RELAY_RELS)

    def body(*refs):
        land_refs, send_ref, recv_ref = refs[:m], refs[m], refs[m + 1]
        sibling, _ = _rel_peer(1)
        for i in range(m):
            for s, rel in enumerate(RELAY_RELS):
                _, sent = _rel_peer(rel)
                _, arriving = _rel_peer(rel ^ 1)
                copy = pltpu.make_async_remote_copy(
                    src_ref=land_refs[i].at[sent], dst_ref=land_refs[i].at[arriving],
                    send_sem=send_ref.at[i * nr + s], recv_sem=recv_ref.at[i * nr + s],
                    device_id=sibling, device_id_type=MESH)
                copy.wait_send()
                copy.wait_recv()

    outs = pl.pallas_call(
        body, name=name, out_shape=tuple(_hbm(a) for a in lands),
        in_specs=[_HBM_SPEC] * m + [_SEM_SPEC, _SEM_SPEC, pl.BlockSpec(memory_space=pl.ANY)],
        out_specs=tuple([_HBM_SPEC] * m), input_output_aliases={i: i for i in range(m)},
        compiler_params=pltpu.CompilerParams(has_side_effects=_DATAFLOW),
    )(*lands, send_sems, recv_sems, after)
    return list(outs)


def _ffn_fwd(tag, n, wg, wd):
    gu, act = _gate_up_act(f"ffn_gate_up_{tag}", n, wg)
    wd4 = wd.reshape(NFB, FB, D)
    f = _fwd_kblocked(f"ffn_down_{tag}", act, wd4)
    return (n, gu, act, wg, wd4), f


def _ffn_bwd(tag, dh_out, df, h_in, saved, g_pre, send, mixer):
    n, gu, act, wg, wd4 = saved
    dwd = _bwd_w_kblocked(f"ffn_down_dw_{tag}", act, df).reshape(NDEV, DFF // NDEV, D)
    dgu = _down_dx_act_bwd(f"ffn_down_dx_{tag}", df, wd4, gu).reshape(NDEV, S, FB)
    tok = send({f"down_{tag}": dwd, f"gate_up_{tag}": _bwd_w_cols_blocked(f"ffn_gate_up_dw_{tag}", n, dgu)})
    dn = _bwd_x_cols_blocked(f"ffn_gate_up_dx_{tag}", dgu, wg, after=tok)
    dh_in, (dg_pre,), dy, dg_mixer = _rms_bwd(f"ffn_prenorm_bwd_{tag}", h_in, [(g_pre, dn)], dh_out, F32, then=mixer)
    return dh_in, dg_pre, dy, dg_mixer


def kernel(x, positions, mix_norm_pre, mix_norm_post, ffn_norm_pre, ffn_norm_post, ffn_w_gate_up, ffn_w_down, conv_w_in, conv_w, conv_w_out, kv_norm, w_kv, w_q, w_o, loss_target, m_mix_norm_pre, m_mix_norm_post, m_ffn_norm_pre, m_ffn_norm_post, m_ffn_w_gate_up, m_ffn_w_down, m_conv_w_in, m_conv_w, m_conv_w_out, m_kv_norm, m_w_kv, m_w_q, m_w_o, v_mix_norm_pre, v_mix_norm_post, v_ffn_norm_pre, v_ffn_norm_post, v_ffn_w_gate_up, v_ffn_w_down, v_conv_w_in, v_conv_w, v_conv_w_out, v_kv_norm, v_w_kv, v_w_q, v_w_o):
    me = 4 * lax.axis_index("x") + 2 * lax.axis_index("y") + lax.axis_index("c")
    h0 = x.reshape(S, D)
    target = loss_target.reshape(S, D)
    row = lambda a, l: a[l].reshape(1, D)
    g_kv = kv_norm.reshape(1, D)

    cw_shard = jnp.pad(conv_w[0], ((0, 5), (0, 0)))
    names = ["conv_in", "conv_w", "conv_out", "gate_up_0", "down_0", "kv", "q", "o", "gate_up_1", "down_1"]
    shards = [conv_w_in[0], cw_shard, conv_w_out[0], ffn_w_gate_up[0], ffn_w_down[0],
              w_kv, w_q[0], w_o[0], ffn_w_gate_up[1], ffn_w_down[1]]
    shards = [s if n == "conv_w" else s.astype(BF16) for n, s in zip(names, shards)]
    first = 3
    gather_first = _gather_start("gather_start_conv", shards[:first], me)
    gather_rest = _gather_start("gather_start_rest", shards[first:], me)

    def direct(group, after):
        ts = [names.index(n) for n in group]
        started, ts = (gather_first, ts) if ts[0] < first else (gather_rest, [t - first for t in ts])
        lands = _gather_wait(f"gather_wait_{group[0]}", started, ts, after)
        return _relay_start(f"relay_start_{group[0]}", lands)

    def finish(group, relayed, after):
        return dict(zip(group, _relay_wait(f"relay_wait_{group[0]}", relayed, after)))

    sent = {}

    def send(grads):
        started, token = _exchange_start(f"scatter_start_{next(iter(grads))}", list(grads.values()), me, "scatter")
        for i, name in enumerate(grads):
            sent[name] = (started, i)
        return token

    groups = [["conv_in", "conv_w", "conv_out"], ["gate_up_0", "down_0"], ["kv", "q"], ["o", "gate_up_1", "down_1"]]
    n0 = _rms_fwd("mix_prenorm_0", h0, [row(mix_norm_pre, 0)])[0]
    half = HEAD_DIM // 2
    inv_freq = ROPE_THETA ** (-jnp.arange(half, dtype=F32) / half)
    tables = _rope_tables("rope_tables", positions.reshape(S, 1), jnp.tile(inv_freq, 4).reshape(1, 128))
    w = finish(groups[0], direct(groups[0], tables[0]), n0)
    win = w["conv_in"].transpose(1, 0, 2).reshape(D, 3 * D)
    cw = w["conv_w"].transpose(1, 0, 2).reshape(8, D)
    wout = w["conv_out"].reshape(D, D)
    z = _fwd_rows("conv_in", n0, win, BF16)
    pre = _conv_fwd("conv_gate", z, cw)
    relayed = direct(groups[1], pre)
    y0 = _fwd_rows("conv_out", pre, wout)
    h1, (n1,) = _resid_rms("mix_postnorm_0", h0, y0, row(mix_norm_post, 0), [row(ffn_norm_pre, 0)])
    w = finish(groups[1], relayed, n1)
    ffn0, f0 = _ffn_fwd("0", n1, w["gate_up_0"], w["down_0"])
    relayed = direct(groups[2], ffn0[2])
    h2, (nk, n2) = _resid_rms("ffn_postnorm_0", h1, f0, row(ffn_norm_post, 0), [g_kv, row(mix_norm_pre, 1)])

    w = finish(groups[2], relayed, nk)
    wkv = w["kv"].transpose(1, 0, 2).reshape(D, 2 * QW)
    wq = w["q"].transpose(1, 0, 2).reshape(D, QW)
    qc, kc, vc, o_c, lse_c = [], [], [], [], []
    for g, d in enumerate(DILATIONS):
        q_g, k_g, v_g = _qkv_classes(f"qkv_proj_{g}", n2, nk, wq, wkv, g, d, tables)
        qc.append(q_g)
        kc.append(k_g)
        vc.append(v_g)
    relayed = direct(groups[3], vc[-1])
    for g, d in enumerate(DILATIONS):
        o_g, lse_g = _attn_fwd(f"attn_fwd_{g}", qc[g], kc[g], vc[g], d)
        o_c.append(o_g)
        lse_c.append(lse_g)
    o_mix = _mix_fwd("attn_mix", o_c, lse_c)
    w = finish(groups[3], relayed, o_mix)
    wo = w["o"].reshape(D, D)
    y1 = _fwd_rows("attn_out", o_mix, wo)
    h3, (n3,) = _resid_rms("mix_postnorm_1", h2, y1, row(mix_norm_post, 1), [row(ffn_norm_pre, 1)])
    ffn1, f1 = _ffn_fwd("1", n3, w["gate_up_1"], w["down_1"])

    dh4, df1, dg_fpost1, sq = _resid_rms_loss("ffn_postnorm_1_loss", h3, f1, row(ffn_norm_post, 1), target)

    dh3, dg_fpre1, dy1, dg_mpost1 = _ffn_bwd(
        "1", dh4, df1, h3, ffn1, row(ffn_norm_pre, 1), send, (y1, row(mix_norm_post, 1)))
    dwo = _bwd_w_rows("attn_out_dw", o_mix, dy1).reshape(NDEV, D // NDEV, D)
    do = _bwd_x_rows("attn_out_dx", dy1, wo, F32)
    lane = jnp.arange(128)
    ones_blockdiag = (lane[:, None] // HEAD_DIM == lane[None, :] // HEAD_DIM).astype(BF16)
    mixed = _mix_bwd("attn_mix_bwd", do, o_c, lse_c, ones_blockdiag)
    branch_grads = [_attn_bwd(f"attn_bwd_{g}", qc[g], kc[g], vc[g], mixed[g], lse_c[g], mixed[3 + g], d)
                    for g, d in enumerate(DILATIONS)]
    dq_raw, dkv = _attn_bwd_post("attn_bwd_post", branch_grads, *tables)
    tok = send({"o": dwo, "kv": _bwd_w_cols("kv_proj_dw", nk, dkv, 2 * QW // NDEV),
                "q": _bwd_w_cols("q_proj_dw", n2, dq_raw, QW // NDEV)})
    dnk = _bwd_x_plain("kv_proj_dx", dkv, wkv, after=tok)
    dn2 = _bwd_x_plain("q_proj_dx", dq_raw, wq)
    dh2, (dg_kv, dg_mpre1), df0, dg_fpost0 = _rms_bwd(
        "kv_and_mix_prenorm_bwd_1", h2, [(g_kv, dnk), (row(mix_norm_pre, 1), dn2)], dh3, F32,
        then=(f0, row(ffn_norm_post, 0)))

    dh1, dg_fpre0, dy0, dg_mpost0 = _ffn_bwd(
        "0", dh2, df0, h1, ffn0, row(ffn_norm_pre, 0), send, (y0, row(mix_norm_post, 0)))
    dwout = _bwd_w_rows("conv_out_dw", pre, dy0).reshape(NDEV, D // NDEV, D)
    dpre = _bwd_x_rows("conv_out_dx", dy0, wout, BF16)
    dz, dcw = _conv_bwd("conv_gate_bwd", z, dpre, cw)
    tok = send({"conv_out": dwout, "conv_in": _bwd_w_cols("conv_in_dw", n0, dz, 3 * D // NDEV)})
    dn0 = _bwd_x_plain("conv_in_dx", dz, win, after=tok)
    dh0, (dg_mpre0,) = _rms_bwd("mix_prenorm_bwd_0", h0, [(row(mix_norm_pre, 0), dn0)], dh1, F32)

    small = _pack_small("pack_small_grads", [dg_mpre0, dg_mpre1, dg_mpost0, dg_mpost1, dg_fpre0, dg_fpre1,
                                             dg_fpost0, dg_fpost1, dg_kv], dcw, sq)
    small_all = _exchange("gather_small_grads", [small], "gather")[0]

    done = [small_all]

    def upd(tag, w, m, v):
        parts = _exchange_wait(f"scatter_wait_{tag}", *sent[tag], done[-1], "scatter")
        shape = w.shape
        flat = lambda a: a.reshape(parts.shape[1:])
        res = _adamw(f"adamw_{tag}", parts, flat(w), flat(m), flat(v))
        done.append(res[0])
        return [r.reshape(shape) for r in res]

    def upd_layer(tag, l, w, m, v, other):
        parts = _exchange_wait(f"scatter_wait_{tag}_{l}", *sent[f"{tag}_{l}"], done[-1], "scatter")
        res = _adamw(f"adamw_{tag}_{l}", parts, w, m, v, layer=l, other=other)
        done.append(res[0])
        return list(res)

    vec = lambda a: a.reshape(1, D)
    gain_res, taps, loss = _adamw_gains("adamw_gains", small_all, [
        (mix_norm_pre, m_mix_norm_pre, v_mix_norm_pre), (mix_norm_post, m_mix_norm_post, v_mix_norm_post),
        (ffn_norm_pre, m_ffn_norm_pre, v_ffn_norm_pre), (ffn_norm_post, m_ffn_norm_post, v_ffn_norm_post),
        (vec(kv_norm), vec(m_kv_norm), vec(v_kv_norm))])
    dcw_mine = lax.dynamic_slice(taps, (0, me * 128), (8, 128))
    pad8 = lambda a, fill: jnp.pad(a[0], ((0, 5), (0, 0)), constant_values=fill)
    cw_res = [r[0:3].reshape(1, 3, 128) for r in
              _adamw("adamw_conv_w", dcw_mine.reshape(1, 8, 128), cw_shard, pad8(m_conv_w, 0.0), pad8(v_conv_w, 1.0))]

    res = {
        "mix_norm_pre": gain_res[0],
        "mix_norm_post": gain_res[1],
        "ffn_norm_pre": gain_res[2],
        "ffn_norm_post": gain_res[3],
        "kv_norm": [r.reshape(D) for r in gain_res[4]],
        "conv_w": cw_res,
    }
    down_1 = upd_layer("down", 1, ffn_w_down, m_ffn_w_down, v_ffn_w_down, None)
    gate_up_t = [jnp.swapaxes(a, 1, 2) for a in (ffn_w_gate_up, m_ffn_w_gate_up, v_ffn_w_gate_up)]
    gate_up_1 = upd_layer("gate_up", 1, *gate_up_t, None)
    res["w_o"] = upd("o", w_o, m_w_o, v_w_o)
    res["w_q"] = upd("q", w_q, m_w_q, v_w_q)
    res["w_kv"] = upd("kv", w_kv, m_w_kv, v_w_kv)
    res["ffn_w_down"] = upd_layer("down", 0, ffn_w_down, m_ffn_w_down, v_ffn_w_down, down_1)
    res["ffn_w_gate_up"] = [jnp.swapaxes(r, 1, 2) for r in upd_layer("gate_up", 0, *gate_up_t, gate_up_1)]
    res["conv_w_out"] = upd("conv_out", conv_w_out, m_conv_w_out, v_conv_w_out)
    res["conv_w_in"] = upd("conv_in", conv_w_in, m_conv_w_in, v_conv_w_in)
    order = ["mix_norm_pre", "mix_norm_post", "ffn_norm_pre", "ffn_norm_post", "ffn_w_gate_up", "ffn_w_down",
             "conv_w_in", "conv_w", "conv_w_out", "kv_norm", "w_kv", "w_q", "w_o"]
    out = [loss, dh0.reshape(1, S, D)]
    for i in range(4):
        out += [res[name][i] for name in order]
    return tuple(out)
```

```python
import jax
import jax.numpy as jnp
from jax import lax
from jax.experimental import pallas as pl
from jax.experimental.pallas import tpu as pltpu

F32 = jnp.float32
BF16 = jnp.bfloat16

S = 4096
D = 1024
NDEV = 8
HEAD_DIM = 64
QW = 3072
DFF = 2816
FB = 704
NFB = 4
BRANCHES = ((128, 1), (512, 4), (2048, 16))
BAND = 128
ROPE_THETA = 10000.0
RMS_EPS = 1e-6
NEG_INF = -1e30
ADAM_LR, ADAM_B1, ADAM_B2, ADAM_EPS, ADAM_WD, ADAM_STEP = 0.001, 0.9, 0.999, 1e-08, 0.01, 10

VMEM_LIMIT_BYTES = 52 * 1024 * 1024
ROW_TILE = 512
MESH = pl.DeviceIdType.MESH


def _cparams(ngrid):
    return pltpu.CompilerParams(dimension_semantics=("arbitrary",) * ngrid,
                                vmem_limit_bytes=VMEM_LIMIT_BYTES)


def _sds(shape, dtype):
    return jax.ShapeDtypeStruct(tuple(shape), dtype)


_DIMS = {"nn": (((1,), (0,)), ((), ())),
         "nt": (((1,), (1,)), ((), ())),
         "tn": (((0,), (0,)), ((), ()))}


def _matmul(name, a, b, *, mode, grid, a_blk, a_map, b_blk, b_map, o_shape, o_blk, o_map, out_dtype, after=None,
            out_groups=1):
    nk = grid[2]
    dims = _DIMS[mode]
    acc_shape = tuple(s for s in o_blk if s is not None)
    if out_groups > 1:
        acc_shape = (acc_shape[1], out_groups * acc_shape[2])
    extra = [] if after is None else [after]

    def store(o_ref, val):
        if out_groups == 1:
            o_ref[...] = val.astype(o_ref.dtype)
        else:
            n = o_ref.shape[-1]
            for grp in range(out_groups):
                o_ref[grp] = val[:, grp * n:(grp + 1) * n].astype(o_ref.dtype)

    def body(a_ref, b_ref, *rest):
        o_ref, scratch = rest[len(extra)], rest[len(extra) + 1:]
        part = lax.dot_general(a_ref[...], b_ref[...], dims, preferred_element_type=F32)
        if nk == 1:
            store(o_ref, part)
            return
        acc_ref = scratch[0]
        k = pl.program_id(2)

        @pl.when(k == 0)
        def _():
            acc_ref[...] = part

        @pl.when(k > 0)
        def _():
            acc_ref[...] += part

        @pl.when(k == nk - 1)
        def _():
            store(o_ref, acc_ref[...])

    return pl.pallas_call(
        body, name=name, grid=grid,
        in_specs=[pl.BlockSpec(a_blk, a_map), pl.BlockSpec(b_blk, b_map)] + [pl.BlockSpec(memory_space=pl.ANY)] * len(extra),
        out_specs=pl.BlockSpec(o_blk, o_map),
        out_shape=_sds(o_shape, out_dtype),
        scratch_shapes=[] if nk == 1 else [pltpu.VMEM(acc_shape, F32)],
        compiler_params=_cparams(3),
    )(a, b, *extra)


TM = 1024
TK = S


def _fwd_rows(name, a, w, out_dtype=F32):
    kdim, n = w.shape
    tn = 512
    return _matmul(name, a, w, mode="nn", grid=(S // TM, n // tn, 1),
                   a_blk=(TM, kdim), a_map=lambda i, j, k: (i, 0),
                   b_blk=(kdim, tn), b_map=lambda i, j, k: (0, j),
                   o_shape=(S, n), o_blk=(TM, tn), o_map=lambda i, j, k: (i, j), out_dtype=out_dtype)


def _fwd_kblocked(name, a4, w4):
    nb, _, kb = a4.shape
    n = w4.shape[2]

    def body(a_ref, w_ref, o_ref):
        acc = _dot_nn(a_ref[0], w_ref[0])
        for j in range(1, nb):
            acc = acc + _dot_nn(a_ref[j], w_ref[j])
        o_ref[...] = acc

    return pl.pallas_call(
        body, name=name, grid=(S // TM,),
        in_specs=[pl.BlockSpec((nb, TM, kb), lambda i: (0, i, 0)), pl.BlockSpec((nb, kb, n), lambda i: (0, 0, 0))],
        out_specs=pl.BlockSpec((TM, n), lambda i: (i, 0)), out_shape=_sds((S, n), F32),
        compiler_params=_cparams(1),
    )(a4, w4)


def _bwd_x_cols_blocked(name, dy8, wg, after):
    _, kdim, n = wg.shape
    nk = NDEV // 2

    def body(a_ref, b_ref, after_ref, o_ref, acc_ref):
        k = pl.program_id(1)
        part = _dot_nt(a_ref[0], b_ref[0]) + _dot_nt(a_ref[1], b_ref[1])

        @pl.when(k == 0)
        def _():
            acc_ref[...] = part

        @pl.when(k > 0)
        def _():
            acc_ref[...] += part

        @pl.when(k == nk - 1)
        def _():
            o_ref[...] = acc_ref[...]

    return pl.pallas_call(
        body, name=name, grid=(S // TM, nk),
        in_specs=[pl.BlockSpec((2, None, TM, n), lambda i, k: (0, k, i, 0)),
                  pl.BlockSpec((2, None, kdim, n), lambda i, k: (0, k, 0, 0)),
                  pl.BlockSpec(memory_space=pl.ANY)],
        out_specs=pl.BlockSpec((TM, kdim), lambda i, k: (i, 0)), out_shape=_sds((S, kdim), F32),
        scratch_shapes=[pltpu.VMEM((TM, kdim), F32)],
        compiler_params=_cparams(2),
    )(dy8.reshape(2, nk, S, n), wg.reshape(2, nk, kdim, n), after)


def _bwd_x_rows(name, dy, w, out_dtype, after=None):
    kdim, n = w.shape
    tkk = 512
    return _matmul(name, dy, w, mode="nt", grid=(S // TM, kdim // tkk, 1),
                   a_blk=(TM, n), a_map=lambda i, j, k: (i, 0),
                   b_blk=(tkk, n), b_map=lambda i, j, k: (j, 0),
                   o_shape=(S, kdim), o_blk=(TM, tkk), o_map=lambda i, j, k: (i, j), out_dtype=out_dtype, after=after)


DW_COLS = 768


def _bwd_w_cols(name, a, dy, n):
    kdim = a.shape[1]
    groups = DW_COLS // n
    return _matmul(name, a, dy, mode="tn", grid=(1, NDEV // groups, S // TK),
                   a_blk=(TK, kdim), a_map=lambda i, j, k: (k, 0),
                   b_blk=(TK, DW_COLS), b_map=lambda i, j, k: (k, j),
                   o_shape=(NDEV, kdim, n), o_blk=(groups, kdim, n) if groups > 1 else (None, kdim, n),
                   o_map=lambda i, j, k: (j, 0, 0), out_dtype=BF16, out_groups=groups)


def _bwd_x_plain(name, dy, w, after=None):
    kdim, n = w.shape
    tm = TM if n <= 3 * D else TM // 2
    return _matmul(name, dy, w, mode="nt", grid=(S // tm, 1, 1),
                   a_blk=(tm, n), a_map=lambda i, j, k: (i, 0),
                   b_blk=(kdim, n), b_map=lambda i, j, k: (0, 0),
                   o_shape=(S, kdim), o_blk=(tm, kdim), o_map=lambda i, j, k: (i, 0), out_dtype=F32, after=after)


def _bwd_w_cols_blocked(name, a, dy8):
    kdim = a.shape[1]
    n = dy8.shape[2]
    return _matmul(name, dy8, a, mode="tn", grid=(1, NDEV, S // TK),
                   a_blk=(None, TK, n), a_map=lambda i, j, k: (j, k, 0),
                   b_blk=(TK, kdim), b_map=lambda i, j, k: (k, 0),
                   o_shape=(NDEV, n, kdim), o_blk=(None, n, kdim), o_map=lambda i, j, k: (j, 0, 0), out_dtype=BF16)


def _bwd_w_rows(name, a, dy):
    kdim = a.shape[1]
    n = dy.shape[1]
    tmm = 512
    return _matmul(name, a, dy, mode="tn", grid=(kdim // tmm, 1, S // TK),
                   a_blk=(TK, tmm), a_map=lambda i, j, k: (k, i),
                   b_blk=(TK, n), b_map=lambda i, j, k: (k, 0),
                   o_shape=(kdim, n), o_blk=(tmm, n), o_map=lambda i, j, k: (i, 0), out_dtype=BF16)


def _bwd_w_kblocked(name, a4, dy):
    nb, _, kb = a4.shape
    n = dy.shape[1]
    return _matmul(name, a4, dy, mode="tn", grid=(nb, 1, S // TK),
                   a_blk=(None, TK, kb), a_map=lambda i, j, k: (i, k, 0),
                   b_blk=(TK, n), b_map=lambda i, j, k: (k, 0),
                   o_shape=(nb, kb, n), o_blk=(None, kb, n), o_map=lambda i, j, k: (i, 0, 0), out_dtype=BF16)


def _rstd(x):
    return lax.rsqrt(jnp.mean(x * x, axis=-1, keepdims=True) + RMS_EPS)


def _row_spec(tm=ROW_TILE, width=D):
    return pl.BlockSpec((tm, width), lambda i: (i, 0))


def _vec_spec(rows=1, width=D):
    return pl.BlockSpec((rows, width), lambda i: (0, 0))


def _rms_fwd(name, x, gains):
    n = len(gains)

    def body(x_ref, *refs):
        x_val = x_ref[...]
        xh = x_val * _rstd(x_val)
        for g_ref, o_ref in zip(refs[:n], refs[n:]):
            o_ref[...] = (xh * g_ref[...]).astype(o_ref.dtype)

    outs = pl.pallas_call(
        body, name=name, grid=(S // ROW_TILE,),
        in_specs=[_row_spec()] + [_vec_spec()] * n,
        out_specs=[_row_spec()] * n,
        out_shape=[_sds((S, D), BF16)] * n,
        compiler_params=_cparams(1),
    )(x, *gains)
    return list(outs)


def _resid_rms(name, h, y, g, next_gains):
    n = len(next_gains)

    def body(h_ref, y_ref, g_ref, *refs):
        y_val = y_ref[...]
        h_new = h_ref[...] + (y_val * _rstd(y_val)) * g_ref[...]
        refs[n][...] = h_new
        hh = h_new * _rstd(h_new)
        for g2_ref, o_ref in zip(refs[:n], refs[n + 1:]):
            o_ref[...] = (hh * g2_ref[...]).astype(o_ref.dtype)

    outs = pl.pallas_call(
        body, name=name, grid=(S // ROW_TILE,),
        in_specs=[_row_spec(), _row_spec(), _vec_spec()] + [_vec_spec()] * n,
        out_specs=[_row_spec()] * (n + 1), out_shape=[_sds((S, D), F32)] + [_sds((S, D), BF16)] * n,
        compiler_params=_cparams(1),
    )(h, y, g, *next_gains)
    return outs[0], list(outs[1:])


def _resid_rms_loss(name, h, y, g, target):
    def body(h_ref, y_ref, g_ref, t_ref, dh_ref, dy_ref, dg_ref, part_ref):
        y_val = y_ref[...]
        gain = g_ref[...]
        e = h_ref[...] + (y_val * _rstd(y_val)) * gain - t_ref[...]
        dh = e * (1.0 / D)
        dh_ref[...] = dh
        step = pl.program_id(0)
        dy_ref[...] = _norm_bwd_rows(y_val, gain, dh, dg_ref, step).astype(dy_ref.dtype)
        part = jnp.sum(e * e, axis=0, keepdims=True)

        @pl.when(step == 0)
        def _():
            part_ref[...] = part

        @pl.when(step > 0)
        def _():
            part_ref[...] += part

    return pl.pallas_call(
        body, name=name, grid=(S // ROW_TILE,),
        in_specs=[_row_spec(), _row_spec(), _vec_spec(), _row_spec()],
        out_specs=[_row_spec(), _row_spec(), _vec_spec(8), _vec_spec()],
        out_shape=[_sds((S, D), F32), _sds((S, D), BF16), _sds((8, D), F32), _sds((1, D), F32)],
        compiler_params=_cparams(1),
    )(h, y, g, target)


def _norm_bwd_rows(x_val, g, dn, dg_ref, step):
    r = _rstd(x_val)
    xh = x_val * r
    dxh = dn * g
    part = jnp.sum(dn * xh, axis=0, keepdims=True)

    @pl.when(step == 0)
    def _():
        dg_ref[...] = jnp.zeros_like(dg_ref)

    dg_ref[0:1, :] += part
    return r * (dxh - xh * jnp.mean(dxh * xh, axis=-1, keepdims=True))


def _rms_bwd(name, x, pairs, dres, out_dtype, then=None):
    n = len(pairs)
    has_res = dres is not None
    chained = then is not None

    def body(x_ref, *refs):
        g_refs = refs[0:2 * n:2]
        dn_refs = refs[1:2 * n:2]
        pos = 2 * n
        res_ref = refs[pos] if has_res else None
        pos += int(has_res)
        if chained:
            y_ref, gy_ref = refs[pos], refs[pos + 1]
            pos += 2
        dx_ref = refs[pos]
        dg_refs = refs[pos + 1:pos + 1 + n]
        step = pl.program_id(0)
        x_val = x_ref[...]
        acc = res_ref[...] if has_res else jnp.zeros_like(x_val)
        for g_ref, dn_ref, dg_ref in zip(g_refs, dn_refs, dg_refs):
            acc = acc + _norm_bwd_rows(x_val, g_ref[...], dn_ref[...].astype(F32), dg_ref, step)
        dx_ref[...] = acc.astype(dx_ref.dtype)
        if chained:
            dy_ref, dgy_ref = refs[pos + 1 + n], refs[pos + 2 + n]
            dy_ref[...] = _norm_bwd_rows(y_ref[...], gy_ref[...], acc, dgy_ref, step).astype(dy_ref.dtype)

    operands = [x]
    in_specs = [_row_spec()]
    for g, dn in pairs:
        operands += [g, dn]
        in_specs += [_vec_spec(), _row_spec()]
    if has_res:
        operands.append(dres)
        in_specs.append(_row_spec())
    if chained:
        operands += [then[0], then[1]]
        in_specs += [_row_spec(), _vec_spec()]
    extra = int(chained)
    outs = pl.pallas_call(
        body, name=name, grid=(S // ROW_TILE,),
        in_specs=in_specs,
        out_specs=[_row_spec()] + [_vec_spec(8)] * n + [_row_spec(), _vec_spec(8)] * extra,
        out_shape=[_sds((S, D), out_dtype)] + [_sds((8, D), F32)] * n + [_sds((S, D), BF16), _sds((8, D), F32)] * extra,
        compiler_params=_cparams(1),
    )(*operands)
    if chained:
        return outs[0], list(outs[1:1 + n]), outs[1 + n], outs[2 + n]
    return outs[0], list(outs[1:])


def _shift_down(u, prev8, k):
    r = pltpu.roll(u, k, 0)
    p = pltpu.roll(prev8, k, 0)
    row = lax.broadcasted_iota(jnp.int32, prev8.shape, 0)
    top = jnp.where(row < k, p, r[0:8])
    return jnp.concatenate([top, r[8:]], axis=0)


def _shift_up(u, next8, k):
    tm = u.shape[0]
    r = pltpu.roll(u, tm - k, 0)
    p = pltpu.roll(next8, 8 - k, 0)
    row = lax.broadcasted_iota(jnp.int32, next8.shape, 0)
    bot = jnp.where(row >= 8 - k, p, r[tm - 8:tm])
    return jnp.concatenate([r[:tm - 8], bot], axis=0)


CONV_TILE = 512


def _halo_prev(col):
    return pl.BlockSpec((8, D), lambda i: (jnp.maximum(i * (CONV_TILE // 8) - 1, 0), col))


def _halo_next(col):
    last = S // 8 - 1
    return pl.BlockSpec((8, D), lambda i: (jnp.minimum((i + 1) * (CONV_TILE // 8), last), col))


def _conv_fwd(name, z, cw):
    def body(b_ref, c_ref, h_ref, cp_ref, hp_ref, cw_ref, o_ref):
        i = pl.program_id(0)
        u = c_ref[...].astype(F32) * h_ref[...].astype(F32)
        up = cp_ref[...].astype(F32) * hp_ref[...].astype(F32)
        up = jnp.where(i > 0, up, 0.0)
        cv = cw_ref[0:1, :] * _shift_down(u, up, 2) + cw_ref[1:2, :] * _shift_down(u, up, 1) + cw_ref[2:3, :] * u
        o_ref[...] = (b_ref[...].astype(F32) * cv).astype(o_ref.dtype)

    col = lambda c: pl.BlockSpec((CONV_TILE, D), lambda i: (i, c))
    return pl.pallas_call(
        body, name=name, grid=(S // CONV_TILE,),
        in_specs=[col(0), col(1), col(2), _halo_prev(1), _halo_prev(2), _vec_spec(8)],
        out_specs=_row_spec(CONV_TILE), out_shape=_sds((S, D), BF16),
        compiler_params=_cparams(1),
    )(z, z, z, z, z, cw)


def _conv_bwd(name, z, dpre, cw):
    nsteps = S // CONV_TILE

    def body(b_ref, c_ref, h_ref, cp_ref, hp_ref, dp_ref, dpn_ref, bn_ref, cw_ref, dz_ref, dcw_ref):
        i = pl.program_id(0)
        b = b_ref[...].astype(F32)
        c = c_ref[...].astype(F32)
        h = h_ref[...].astype(F32)
        dp = dp_ref[...].astype(F32)
        u = c * h
        up = jnp.where(i > 0, cp_ref[...].astype(F32) * hp_ref[...].astype(F32), 0.0)
        s1 = _shift_down(u, up, 1)
        s2 = _shift_down(u, up, 2)
        w0, w1, w2 = cw_ref[0:1, :], cw_ref[1:2, :], cw_ref[2:3, :]
        cv = w0 * s2 + w1 * s1 + w2 * u
        dcv = dp * b
        dcvn = jnp.where(i < nsteps - 1, dpn_ref[...].astype(F32) * bn_ref[...].astype(F32), 0.0)
        du = w2 * dcv + w1 * _shift_up(dcv, dcvn, 1) + w0 * _shift_up(dcv, dcvn, 2)
        dz_ref[:, 0:D] = (dp * cv).astype(dz_ref.dtype)
        dz_ref[:, D:2 * D] = (du * h).astype(dz_ref.dtype)
        dz_ref[:, 2 * D:3 * D] = (du * c).astype(dz_ref.dtype)

        @pl.when(i == 0)
        def _():
            dcw_ref[...] = jnp.zeros_like(dcw_ref)

        dcw_ref[0:1, :] += jnp.sum(dcv * s2, axis=0, keepdims=True)
        dcw_ref[1:2, :] += jnp.sum(dcv * s1, axis=0, keepdims=True)
        dcw_ref[2:3, :] += jnp.sum(dcv * u, axis=0, keepdims=True)

    col = lambda c: pl.BlockSpec((CONV_TILE, D), lambda i: (i, c))
    return pl.pallas_call(
        body, name=name, grid=(nsteps,),
        in_specs=[col(0), col(1), col(2), _halo_prev(1), _halo_prev(2),
                  _row_spec(CONV_TILE), _halo_next(0), _halo_next(0), _vec_spec(8)],
        out_specs=[pl.BlockSpec((CONV_TILE, 3 * D), lambda i: (i, 0)), _vec_spec(8)],
        out_shape=[_sds((S, 3 * D), BF16), _sds((8, D), F32)],
        compiler_params=_cparams(1),
    )(z, z, z, z, z, dpre, dpre, z, cw)


FFN_TM = 2048
_GU_BLOCK = pl.BlockSpec((2, None, FFN_TM, FB), lambda i, j: (0, j, i, 0))


def _gate_up_act(name, a, wg):
    kdim = a.shape[1]

    def body(a_ref, wgate_ref, wup_ref, gu_ref, act_ref):
        x = a_ref[...]
        g = _dot_nn(x, wgate_ref[...])
        u = _dot_nn(x, wup_ref[...])
        gu_ref[0] = g.astype(gu_ref.dtype)
        gu_ref[1] = u.astype(gu_ref.dtype)
        act_ref[...] = (g * jax.nn.sigmoid(g) * u).astype(act_ref.dtype)

    return pl.pallas_call(
        body, name=name, grid=(S // FFN_TM, NFB),
        in_specs=[pl.BlockSpec((FFN_TM, kdim), lambda i, j: (i, 0)),
                  pl.BlockSpec((None, kdim, FB), lambda i, j: (j, 0, 0)),
                  pl.BlockSpec((None, kdim, FB), lambda i, j: (j + NFB, 0, 0))],
        out_specs=[_GU_BLOCK, pl.BlockSpec((None, FFN_TM, FB), lambda i, j: (j, i, 0))],
        out_shape=[_sds((2, NFB, S, FB), BF16), _sds((NFB, S, FB), BF16)],
        compiler_params=_cparams(2),
    )(a, wg, wg)


def _down_dx_act_bwd(name, df, w4, gu):
    _, kb, n = w4.shape

    def body(df_ref, w_ref, gu_ref, o_ref):
        d = _dot_nt(df_ref[...], w_ref[...])
        g = gu_ref[0].astype(F32)
        u = gu_ref[1].astype(F32)
        sg = jax.nn.sigmoid(g)
        o_ref[0] = (d * u * sg * (1.0 + g * (1.0 - sg))).astype(o_ref.dtype)
        o_ref[1] = (d * g * sg).astype(o_ref.dtype)

    return pl.pallas_call(
        body, name=name, grid=(S // FFN_TM, NFB),
        in_specs=[pl.BlockSpec((FFN_TM, n), lambda i, j: (i, 0)), pl.BlockSpec((None, kb, n), lambda i, j: (j, 0, 0)),
                  _GU_BLOCK],
        out_specs=_GU_BLOCK, out_shape=_sds((2, NFB, S, FB), BF16),
        compiler_params=_cparams(2),
    )(df, w4, gu)


def _rope_tables(name, pos_col, inv_freq_row):
    def body(pos_ref, f_ref, cos_ref, sin_ref):
        ang = pos_ref[...].astype(F32) * f_ref[...]
        lane = lax.broadcasted_iota(jnp.int32, ang.shape, 1)
        s = jnp.sin(ang)
        cos_ref[...] = jnp.cos(ang)
        sin_ref[...] = jnp.where((lane % HEAD_DIM) < HEAD_DIM // 2, -s, s)

    tab = pl.BlockSpec((ROW_TILE, 128), lambda i: (i, 0))
    return pl.pallas_call(
        body, name=name, grid=(S // ROW_TILE,),
        in_specs=[pl.BlockSpec((ROW_TILE, 1), lambda i: (i, 0)), _vec_spec(1, 128)],
        out_specs=[tab, tab], out_shape=[_sds((S, 128), F32)] * 2,
        compiler_params=_cparams(1),
    )(pos_col, inv_freq_row)


def _swap_halves(t):
    lane = lax.broadcasted_iota(jnp.int32, t.shape, 1)
    first = (lane % HEAD_DIM) < HEAD_DIM // 2
    return jnp.where(first, pltpu.roll(t, 128 - HEAD_DIM // 2, 1), pltpu.roll(t, HEAD_DIM // 2, 1))


NCHUNK = D // 128


def _chunk(c, base=0):
    return slice(base + c * 128, base + (c + 1) * 128)


def _class_rows(r, d, tm):
    return pl.ds(r, tm // d, stride=d) if d > 1 else slice(None)


def _class_block(d, tm):
    return pl.BlockSpec((tm // d, d * D), lambda i: (i, 0))


def _tokens_from_classes(blk_ref, tmp_ref, d, tm):
    for r in range(d):
        for c in range(NCHUNK):
            tmp_ref[c, _class_rows(r, d, tm), :] = blk_ref[:, _chunk(c, r * D)].astype(F32)


def _classes_from_tokens(tmp_ref, blk_ref, d, tm):
    for r in range(d):
        for c in range(NCHUNK):
            blk_ref[:, _chunk(c, r * D)] = tmp_ref[c, _class_rows(r, d, tm), :].astype(blk_ref.dtype)


def _qkv_classes(name, n2, nk, wq, wkv, g, d, tables):
    def emit(acc, cos_ref, sin_ref, o_ref, tmp_ref, scale):
        for c in range(NCHUNK):
            tmp_ref[c] = acc[:, _chunk(c)]
        for r in range(d):
            rows = _class_rows(r, d, TM)
            if scale is not None:
                cs = cos_ref[rows, :]
                sn = sin_ref[rows, :]
            for c in range(NCHUNK):
                x = tmp_ref[c, rows, :]
                if scale is not None:
                    x = (x * cs + _swap_halves(x) * sn) * scale
                o_ref[:, _chunk(c, r * D)] = x.astype(o_ref.dtype)

    def body(n2_ref, nk_ref, wq_ref, wk_ref, wv_ref, cos_ref, sin_ref, q_ref, k_ref, v_ref, tmp_ref):
        emit(_dot_nn(n2_ref[...], wq_ref[...]), cos_ref, sin_ref, q_ref, tmp_ref, HEAD_DIM ** -0.5)
        x = nk_ref[...]
        emit(_dot_nn(x, wk_ref[...]), cos_ref, sin_ref, k_ref, tmp_ref, 1.0)
        emit(_dot_nn(x, wv_ref[...]), cos_ref, sin_ref, v_ref, tmp_ref, None)

    nbr = len(DILATIONS)
    act = pl.BlockSpec((TM, D), lambda i: (i, 0))
    tab = pl.BlockSpec((TM, 128), lambda i: (i, 0))
    wcol = lambda col: pl.BlockSpec((D, D), lambda i: (0, col))
    return pl.pallas_call(
        body, name=name, grid=(S // TM,),
        in_specs=[act, act, wcol(g), wcol(g), wcol(nbr + g), tab, tab],
        out_specs=[_class_block(d, TM)] * 3, out_shape=[_sds((S // d, d * D), BF16)] * 3,
        scratch_shapes=[pltpu.VMEM((NCHUNK, TM, 128), F32)],
        compiler_params=_cparams(1),
    )(n2, nk, wq, wkv, wkv, *tables)


ATTN_CHAINS = 8


def _attn_units(d):
    nblk = S // d // BAND
    return max(1, 2 * ATTN_CHAINS // nblk)


def _class_spec(d):
    return pl.BlockSpec((S // d, 128 * _attn_units(d)), lambda cb: (0, cb))


def _dot_nt(a, b):
    return lax.dot_general(a, b, _DIMS["nt"], preferred_element_type=F32)


def _dot_tn(a, b):
    return lax.dot_general(a, b, _DIMS["tn"], preferred_element_type=F32)


def _dot_nn(a, b):
    return lax.dot_general(a, b, _DIMS["nn"], preferred_element_type=F32)


def _band_mask(nkeys):
    qi = lax.broadcasted_iota(jnp.int32, (2 * BAND, nkeys), 0) % BAND
    kj = lax.broadcasted_iota(jnp.int32, (2 * BAND, nkeys), 1)
    if nkeys == BAND:
        return kj <= qi
    dist = qi + BAND - kj
    return (dist >= 0) & (dist <= BAND)


def _stack_heads(x):
    row = lax.broadcasted_iota(jnp.int32, (2 * BAND, 128), 0)
    lane = lax.broadcasted_iota(jnp.int32, (2 * BAND, 128), 1)
    keep = (row < BAND) == (lane < HEAD_DIM)
    return jnp.where(keep, jnp.concatenate([x, x], axis=0), jnp.zeros((), x.dtype))


def _unstack(x2):
    first_head = lax.broadcasted_iota(jnp.int32, (BAND, 128), 1) < HEAD_DIM
    return jnp.where(first_head, x2[:BAND], x2[BAND:])


def _for_later_blocks(nblk, units, fn):
    all_lanes = [slice(u * 128, (u + 1) * 128) for u in range(units)]
    unroll = max(1, ATTN_CHAINS // units)
    trips = (nblk - 1) // unroll
    if trips > 1:
        def step(i, carry):
            for j in range(unroll):
                for lanes in all_lanes:
                    fn(pl.multiple_of((1 + i * unroll + j) * BAND, BAND), lanes)
            return carry

        lax.fori_loop(0, trips, step, 0)
    else:
        trips = 0
    for sb in range(1 + trips * unroll, nblk):
        for lanes in all_lanes:
            fn(sb * BAND, lanes)


def _attn_fwd(name, q, k, v, d):
    nblk = S // d // BAND
    units = _attn_units(d)

    def body(q_ref, k_ref, v_ref, o_ref, lse_ref):
        def block(r0, k0, nkeys, lanes):
            q2 = _stack_heads(q_ref[pl.ds(r0, BAND), lanes])
            s = jnp.where(_band_mask(nkeys), _dot_nt(q2, k_ref[pl.ds(k0, nkeys), lanes]), NEG_INF)
            m = jnp.max(s, axis=-1, keepdims=True)
            p = jnp.exp(s - m)
            l = jnp.sum(p, axis=-1, keepdims=True)
            o2 = _dot_nn(p.astype(BF16), v_ref[pl.ds(k0, nkeys), lanes]) / l
            lse2 = jnp.broadcast_to(m + jnp.log(l), (2 * BAND, 128))
            o_ref[pl.ds(r0, BAND), lanes] = _unstack(o2).astype(o_ref.dtype)
            lse_ref[pl.ds(r0, BAND), lanes] = _unstack(lse2)

        for u in range(units):
            block(0, 0, BAND, slice(u * 128, (u + 1) * 128))

        _for_later_blocks(nblk, units, lambda r0, lanes: block(r0, r0 - BAND, 2 * BAND, lanes))

    spec = _class_spec(d)
    return pl.pallas_call(
        body, name=name, grid=(8 * d // units,),
        in_specs=[spec] * 3, out_specs=[spec] * 2,
        out_shape=[_sds((S // d, d * D), BF16), _sds((S // d, d * D), F32)],
        compiler_params=_cparams(1),
    )(q, k, v)


def _attn_bwd(name, q, k, v, do, lse, dd, d):
    nblk = S // d // BAND
    units = _attn_units(d)

    def body(q_ref, k_ref, v_ref, do_ref, lse_ref, dd_ref, dq_ref, dk_out, dv_out, dk_ref, dv_ref):
        def column(ref, r0, lanes, nkeys):
            tile = ref[pl.ds(r0, BAND), lanes]
            other = pltpu.roll(tile, HEAD_DIM, 1)
            first_head = lax.broadcasted_iota(jnp.int32, tile.shape, 1) < HEAD_DIM
            both = jnp.concatenate([jnp.where(first_head, tile, other), jnp.where(first_head, other, tile)], axis=0)
            return both if nkeys == BAND else jnp.concatenate([both, both], axis=1)

        def block(r0, k0, nkeys, lanes, first):
            q2 = _stack_heads(q_ref[pl.ds(r0, BAND), lanes])
            do2 = _stack_heads(do_ref[pl.ds(r0, BAND), lanes])
            kk = k_ref[pl.ds(k0, nkeys), lanes]
            vv = v_ref[pl.ds(k0, nkeys), lanes]
            s = jnp.where(_band_mask(nkeys), _dot_nt(q2, kk), NEG_INF)
            p = jnp.exp(s - column(lse_ref, r0, lanes, nkeys))
            ds = (p * (_dot_nt(do2, vv) - column(dd_ref, r0, lanes, nkeys))).astype(BF16)
            dq_ref[pl.ds(r0, BAND), lanes] = _unstack(_dot_nn(ds, kk)).astype(dq_ref.dtype)
            dk_part = _dot_tn(ds, q2)
            dv_part = _dot_tn(p.astype(BF16), do2)
            if first:
                dk_ref[pl.ds(k0, nkeys), lanes] = dk_part
                dv_ref[pl.ds(k0, nkeys), lanes] = dv_part
            else:
                dk_ref[pl.ds(k0, BAND), lanes] += dk_part[:BAND]
                dv_ref[pl.ds(k0, BAND), lanes] += dv_part[:BAND]
                dk_ref[pl.ds(k0 + BAND, BAND), lanes] = dk_part[BAND:]
                dv_ref[pl.ds(k0 + BAND, BAND), lanes] = dv_part[BAND:]

        for u in range(units):
            block(0, 0, BAND, slice(u * 128, (u + 1) * 128), True)

        _for_later_blocks(nblk, units, lambda r0, lanes: block(r0, r0 - BAND, 2 * BAND, lanes, False))
        dk_out[...] = dk_ref[...].astype(dk_out.dtype)
        dv_out[...] = dv_ref[...].astype(dv_out.dtype)

    spec = _class_spec(d)
    return pl.pallas_call(
        body, name=name, grid=(8 * d // units,),
        in_specs=[spec] * 6, out_specs=[spec] * 3,
        out_shape=[_sds((S // d, d * D), BF16)] * 3,
        scratch_shapes=[pltpu.VMEM((S // d, 128 * units), F32)] * 2,
        compiler_params=_cparams(1),
    )(q, k, v, do, lse, dd)


MIX_TILE = 256
DILATIONS = tuple(d for _, d in BRANCHES)


def _branch_weights(la, lb, lc):
    m = jnp.maximum(jnp.maximum(la, lb), lc)
    ea, eb, ec = jnp.exp(la - m), jnp.exp(lb - m), jnp.exp(lc - m)
    den = ea + eb + ec
    return ea / den, eb / den, ec / den


def _mix_operands(outs, lses):
    specs = [_class_block(d, MIX_TILE) for d in DILATIONS] * 2
    scratch = [pltpu.VMEM((NCHUNK, MIX_TILE, 128), F32)] * 4
    return list(outs) + list(lses), specs, scratch


def _mix_fwd(name, outs, lses):
    def body(o0, o1, o2, l0, l1, l2, o_ref, to1, to2, tl1, tl2):
        for blk, tmp, d in ((o1, to1, DILATIONS[1]), (o2, to2, DILATIONS[2]), (l1, tl1, DILATIONS[1]), (l2, tl2, DILATIONS[2])):
            _tokens_from_classes(blk, tmp, d, MIX_TILE)
        for c in range(NCHUNK):
            wa, wb, wc = _branch_weights(l0[:, _chunk(c)], tl1[c], tl2[c])
            o_ref[:, _chunk(c)] = (wa * o0[:, _chunk(c)].astype(F32) + wb * to1[c] + wc * to2[c]).astype(o_ref.dtype)

    operands, specs, scratch = _mix_operands(outs, lses)
    return pl.pallas_call(
        body, name=name, grid=(S // MIX_TILE,),
        in_specs=specs, out_specs=_row_spec(MIX_TILE), out_shape=_sds((S, D), BF16),
        scratch_shapes=scratch, compiler_params=_cparams(1),
    )(*operands)


def _head_sum(x, ones_blockdiag):
    hi = x.astype(BF16)
    r1 = x - hi.astype(F32)
    mid = r1.astype(BF16)
    lo = (r1 - mid.astype(F32)).astype(BF16)
    return _dot_nn(hi, ones_blockdiag) + _dot_nn(mid, ones_blockdiag) + _dot_nn(lo, ones_blockdiag)


def _mix_bwd(name, do, outs, lses, ones_blockdiag):
    def body(do_ref, o0, o1, o2, l0, l1, l2, ones_ref, d0, d1, d2, t0, t1, t2,
             to1, to2, tl1, tl2, td1, td2, tt1, tt2):
        for blk, tmp, d in ((o1, to1, DILATIONS[1]), (o2, to2, DILATIONS[2]), (l1, tl1, DILATIONS[1]), (l2, tl2, DILATIONS[2])):
            _tokens_from_classes(blk, tmp, d, MIX_TILE)
        ones = ones_ref[...]
        for c in range(NCHUNK):
            w = _branch_weights(l0[:, _chunk(c)], tl1[c], tl2[c])
            dov = do_ref[:, _chunk(c)]
            o = w[0] * o0[:, _chunk(c)].astype(F32) + w[1] * to1[c] + w[2] * to2[c]
            t = _head_sum(dov * o, ones)
            d0[:, _chunk(c)] = (w[0] * dov).astype(d0.dtype)
            t0[:, _chunk(c)] = w[0] * t
            td1[c], tt1[c] = w[1] * dov, w[1] * t
            td2[c], tt2[c] = w[2] * dov, w[2] * t
        for tmp, blk, d in ((td1, d1, DILATIONS[1]), (tt1, t1, DILATIONS[1]), (td2, d2, DILATIONS[2]), (tt2, t2, DILATIONS[2])):
            _classes_from_tokens(tmp, blk, d, MIX_TILE)

    operands, specs, scratch = _mix_operands(outs, lses)
    out_specs = [_class_block(d, MIX_TILE) for d in DILATIONS] * 2
    out_shape = [_sds((S // d, d * D), BF16) for d in DILATIONS] + [_sds((S // d, d * D), F32) for d in DILATIONS]
    return pl.pallas_call(
        body, name=name, grid=(S // MIX_TILE,),
        in_specs=[_row_spec(MIX_TILE)] + specs + [_vec_spec(128, 128)],
        out_specs=out_specs, out_shape=out_shape,
        scratch_shapes=scratch + [pltpu.VMEM((NCHUNK, MIX_TILE, 128), F32)] * 4,
        compiler_params=_cparams(1),
    )(do, *operands, ones_blockdiag)


def _attn_bwd_post(name, grads, cos_t, sin_t):
    tm = MIX_TILE
    scale = HEAD_DIM ** -0.5

    def unrope(x, cs, sn):
        return x * cs - _swap_halves(x) * sn

    def body(*refs):
        in_refs = refs[:9]
        cos_ref, sin_ref, dq_ref, dkv_ref, tmp_ref = refs[9:]
        cs = cos_ref[...]
        sn = sin_ref[...]
        for g, d in enumerate(DILATIONS):
            for which, blk in enumerate(in_refs[3 * g:3 * g + 3]):
                if d > 1:
                    _tokens_from_classes(blk, tmp_ref, d, tm)
                for c in range(NCHUNK):
                    x = tmp_ref[c] if d > 1 else blk[:, _chunk(c)].astype(F32)
                    if which == 0:
                        dq_ref[:, _chunk(c, g * D)] = (unrope(x, cs, sn) * scale).astype(dq_ref.dtype)
                    elif which == 1:
                        dkv_ref[:, _chunk(c, g * D)] = unrope(x, cs, sn).astype(dkv_ref.dtype)
                    else:
                        dkv_ref[:, _chunk(c, QW + g * D)] = x.astype(dkv_ref.dtype)

    operands = [a for branch in grads for a in branch]
    tab = pl.BlockSpec((tm, 128), lambda i: (i, 0))
    return pl.pallas_call(
        body, name=name, grid=(S // tm,),
        in_specs=[_class_block(d, tm) for d in DILATIONS for _ in range(3)] + [tab, tab],
        out_specs=[pl.BlockSpec((tm, QW), lambda i: (i, 0)), pl.BlockSpec((tm, 2 * QW), lambda i: (i, 0))],
        out_shape=[_sds((S, QW), BF16), _sds((S, 2 * QW), BF16)],
        scratch_shapes=[pltpu.VMEM((NCHUNK, tm, 128), F32)],
        compiler_params=_cparams(1),
    )(*operands, cos_t, sin_t)


def _adamw(name, parts, w, m, v, layer=None, other=None):
    n, rows, cols = parts.shape
    tr = rows
    for cand in (256, 176, 128, 64, 32, 16, 8):
        if rows % cand == 0:
            tr = cand
            break
    n_other = 0 if other is None else len(other)

    def body(p_ref, w_ref, m_ref, v_ref, *refs):
        g_ref, d_ref, nm_ref, nv_ref = refs[n_other:]
        g = p_ref[0].astype(F32)
        for j in range(1, n):
            g = g + p_ref[j].astype(F32)
        g_ref[...] = g
        d_ref[...], nm_ref[...], nv_ref[...] = _adam_update(g, w_ref[...], m_ref[...], v_ref[...])

    if layer is None:
        blk = pl.BlockSpec((tr, cols), lambda i: (i, 0))
        shape = (rows, cols)
    else:
        blk = pl.BlockSpec((None, tr, cols), lambda i: (layer, i, 0))
        shape = w.shape
    return pl.pallas_call(
        body, name=name, grid=(rows // tr,),
        in_specs=[pl.BlockSpec((n, tr, cols), lambda i: (0, i, 0)), blk, blk, blk]
                 + [pl.BlockSpec(memory_space=pl.ANY)] * n_other,
        out_specs=[blk] * 4, out_shape=[_sds(shape, F32)] * 4,
        input_output_aliases={4 + i: i for i in range(n_other)},
        compiler_params=_cparams(1),
    )(parts, w, m, v, *(other or ()))


def _adam_update(g, w, m, v):
    c1 = 1.0 / (1.0 - ADAM_B1 ** ADAM_STEP)
    c2 = 1.0 / (1.0 - ADAM_B2 ** ADAM_STEP)
    nm = ADAM_B1 * m + (1.0 - ADAM_B1) * g
    nv = ADAM_B2 * v + (1.0 - ADAM_B2) * (g * g)
    return -ADAM_LR * ((nm * c1) / (jnp.sqrt(nv * c2) + ADAM_EPS) + ADAM_WD * w), nm, nv


GAIN_ROWS = 16


def _pack_small(name, gain_tiles, taps, sq):
    ng = len(gain_tiles)

    def body(*refs):
        o_ref = refs[-1]
        o_ref[...] = jnp.zeros_like(o_ref)
        for i in range(ng):
            o_ref[i:i + 1, :] = refs[i][0:1, :]
        o_ref[ng:ng + 3, :] = refs[ng][0:3, :]
        o_ref[ng + 3:ng + 4, :] = refs[ng + 1][...]

    return pl.pallas_call(body, name=name, out_shape=_sds((GAIN_ROWS, D), F32))(*gain_tiles, taps, sq)


def _adamw_gains(name, parts, params):
    np_ = len(params)
    shapes = [w.shape for w, _, _ in params]

    def body(p_ref, *refs):
        ins, outs = refs[:3 * np_], refs[3 * np_:]

        def total(lo, rows):
            g = p_ref[0, lo:lo + rows, :]
            for j in range(1, NDEV):
                g = g + p_ref[j, lo:lo + rows, :]
            return g

        lo = 0
        for i, shape in enumerate(shapes):
            g = total(lo, shape[0])
            lo += shape[0]
            w_ref, m_ref, v_ref = ins[3 * i:3 * i + 3]
            g_ref, d_ref, nm_ref, nv_ref = outs[4 * i:4 * i + 4]
            g_ref[...] = g
            d_ref[...], nm_ref[...], nv_ref[...] = _adam_update(g, w_ref[...], m_ref[...], v_ref[...])
        taps_ref, loss_ref = outs[-2], outs[-1]
        taps_ref[...] = jnp.zeros_like(taps_ref)
        taps_ref[0:3, :] = total(lo, 3)
        loss_ref[...] = jnp.sum(total(lo + 3, 1), axis=-1, keepdims=True) * (0.5 / D)

    out_shape = [_sds(shape, F32) for shape in shapes for _ in range(4)] + [_sds((8, D), F32), _sds((1, 1), F32)]
    outs = pl.pallas_call(body, name=name, out_shape=out_shape)(parts, *[a for p in params for a in p])
    return [list(outs[4 * i:4 * i + 4]) for i in range(np_)], outs[-2], outs[-1].reshape(())


def _exchange(name, arrays, kind, after):
    n = len(arrays)
    gather = kind == "gather"
    out_shape = [_sds((NDEV,) + a.shape if gather else a.shape, a.dtype) for a in arrays]

    def body(*refs):
        srcs, outs = refs[:n], refs[n + 1:2 * n + 1]
        send_sems, recv_sems, local_sems = refs[2 * n + 1:]
        x, y, c = lax.axis_index("x"), lax.axis_index("y"), lax.axis_index("c")
        me = 4 * x + 2 * y + c
        pending = []
        for t in range(n):
            own = pltpu.make_async_copy(srcs[t] if gather else srcs[t].at[me], outs[t].at[me], local_sems.at[t])
            own.start()
            pending.append(own)
            for rel in range(1, NDEV):
                px = 1 - x if rel & 4 else x
                py = 1 - y if rel & 2 else y
                pc = 1 - c if rel & 1 else c
                peer = 4 * px + 2 * py + pc
                send = pltpu.make_async_remote_copy(
                    src_ref=srcs[t] if gather else srcs[t].at[peer], dst_ref=outs[t].at[me],
                    send_sem=send_sems.at[t, rel - 1], recv_sem=recv_sems.at[t, rel - 1],
                    device_id=(px, py, pc), device_id_type=MESH)
                send.start()
                arrive = pltpu.make_async_remote_copy(
                    src_ref=srcs[t] if gather else srcs[t].at[me], dst_ref=outs[t].at[peer],
                    send_sem=send_sems.at[t, rel - 1], recv_sem=recv_sems.at[t, rel - 1],
                    device_id=(px, py, pc), device_id_type=MESH)
                pending.append((send, arrive))
        for item in pending:
            if isinstance(item, tuple):
                item[0].wait_send()
                item[1].wait_recv()
            else:
                item.wait()

    any_spec = pl.BlockSpec(memory_space=pl.ANY)
    outs = pl.pallas_call(
        body, name=name,
        in_specs=[any_spec] * (n + 1), out_specs=[any_spec] * n, out_shape=out_shape,
        scratch_shapes=[pltpu.SemaphoreType.DMA((n, NDEV - 1)), pltpu.SemaphoreType.DMA((n, NDEV - 1)),
                        pltpu.SemaphoreType.DMA((n,))],
    )(*arrays, after)
    return list(outs)


def _plain_cols(name, w8):
    _, kdim, n = w8.shape

    def body(w_ref, o_ref, sems):
        copies = [pltpu.make_async_copy(w_ref.at[j], o_ref.at[:, pl.ds(j * n, n)], sems.at[j]) for j in range(NDEV)]
        for copy in copies:
            copy.start()
        for copy in copies:
            copy.wait()

    any_spec = pl.BlockSpec(memory_space=pl.ANY)
    return pl.pallas_call(
        body, name=name, in_specs=[any_spec], out_specs=any_spec, out_shape=_sds((kdim, NDEV * n), w8.dtype),
        scratch_shapes=[pltpu.SemaphoreType.DMA((NDEV,))],
    )(w8)


_HBM_SPEC = pl.BlockSpec(memory_space=pltpu.HBM)
_SEM_SPEC = pl.BlockSpec(memory_space=pltpu.SEMAPHORE)
_DATAFLOW = pltpu.SideEffectType.DATAFLOW_SIDE_EFFECTING


def _peers():
    x, y, c = lax.axis_index("x"), lax.axis_index("y"), lax.axis_index("c")
    out = []
    for rel in range(1, NDEV):
        px = 1 - x if rel & 4 else x
        py = 1 - y if rel & 2 else y
        pc = 1 - c if rel & 1 else c
        out.append((rel - 1, (px, py, pc), 4 * px + 2 * py + pc))
    return 4 * x + 2 * y + c, out


def _hbm(a):
    return pltpu.HBM(a.shape, a.dtype)


def _own_slot(a, me, kind):
    mine = a[None] if kind == "gather" else lax.dynamic_slice_in_dim(a, me, 1, axis=0)
    shape = (NDEV,) + mine.shape[1:]
    return lax.dynamic_update_slice_in_dim(lax.empty(shape, a.dtype), mine, me, axis=0)


def _exchange_start(name, arrays, me, kind):
    n = len(arrays)
    gather = kind == "gather"
    lands = [_own_slot(a, me, kind) for a in arrays]

    def body(*refs):
        src_refs, land_refs = refs[:n], refs[n:2 * n]
        send_sems, recv_sems = refs[2 * n], refs[2 * n + 1]
        token = refs[-1]
        my_block, peers = _peers()
        for t in range(n):
            for slot, dev, block in peers:
                pltpu.make_async_remote_copy(
                    src_ref=src_refs[t] if gather else src_refs[t].at[block], dst_ref=land_refs[t].at[my_block],
                    send_sem=send_sems.at[t * (NDEV - 1) + slot], recv_sem=recv_sems.at[t * (NDEV - 1) + slot],
                    device_id=dev, device_id_type=MESH).start()
        token[...] = jnp.zeros_like(token)

    operands = [pltpu.with_memory_space_constraint(a, pltpu.HBM) for a in list(arrays) + lands]
    outs = pl.pallas_call(
        body, name=name,
        out_shape=(pltpu.SemaphoreType.DMA((n * (NDEV - 1),)), pltpu.SemaphoreType.DMA((n * (NDEV - 1),)),
                   *[_hbm(a) for a in operands], _sds((8, 128), F32)),
        in_specs=[_HBM_SPEC] * (2 * n),
        out_specs=(_SEM_SPEC, _SEM_SPEC, *[_HBM_SPEC] * (2 * n), pl.BlockSpec(memory_space=pltpu.VMEM)),
        input_output_aliases={i: 2 + i for i in range(2 * n)},
        compiler_params=pltpu.CompilerParams(has_side_effects=_DATAFLOW),
    )(*operands)
    return (outs[0], outs[1], list(outs[2:2 + n]), list(outs[2 + n:2 + 2 * n])), outs[-1]


def _exchange_wait(name, started, t, after, kind):
    send_sems, recv_sems, srcs, lands = started
    gather = kind == "gather"

    def body(src_ref, land_ref, send_ref, recv_ref, after_ref, src_out, land_out):
        _, peers = _peers()
        for slot, dev, block in peers:
            copy = pltpu.make_async_remote_copy(
                src_ref=src_ref if gather else src_ref.at[block], dst_ref=land_ref.at[block],
                send_sem=send_ref.at[t * (NDEV - 1) + slot], recv_sem=recv_ref.at[t * (NDEV - 1) + slot],
                device_id=dev, device_id_type=MESH)
            copy.wait_send()
            copy.wait_recv()

    return pl.pallas_call(
        body, name=name, out_shape=(_hbm(srcs[t]), _hbm(lands[t])),
        in_specs=(_HBM_SPEC, _HBM_SPEC, _SEM_SPEC, _SEM_SPEC, pl.BlockSpec(memory_space=pl.ANY)),
        out_specs=(_HBM_SPEC, _HBM_SPEC), input_output_aliases={0: 0, 1: 1},
        compiler_params=pltpu.CompilerParams(has_side_effects=_DATAFLOW),
    )(srcs[t], lands[t], send_sems, recv_sems, after)[1]


DIRECT_RELS = (1, 2, 4, 6)
RELAY_RELS = (2, 4, 6)


def _rel_peer(rel):
    x, y, c = lax.axis_index("x"), lax.axis_index("y"), lax.axis_index("c")
    px = 1 - x if rel & 4 else x
    py = 1 - y if rel & 2 else y
    pc = 1 - c if rel & 1 else c
    return (px, py, pc), 4 * px + 2 * py + pc


def _gather_start(name, shards, me):
    n, nr = len(shards), len(DIRECT_RELS)
    lands = [_own_slot(a, me, "gather") for a in shards]

    def body(*refs):
        src_refs, land_refs = refs[:n], refs[n:2 * n]
        send_sems, recv_sems = refs[2 * n], refs[2 * n + 1]
        _, my_block = _rel_peer(0)
        for t in range(n):
            for s, rel in enumerate(DIRECT_RELS):
                dev, _ = _rel_peer(rel)
                pltpu.make_async_remote_copy(
                    src_ref=src_refs[t], dst_ref=land_refs[t].at[my_block],
                    send_sem=send_sems.at[t * nr + s], recv_sem=recv_sems.at[t * nr + s],
                    device_id=dev, device_id_type=MESH).start()

    operands = [pltpu.with_memory_space_constraint(a, pltpu.HBM) for a in list(shards) + lands]
    outs = pl.pallas_call(
        body, name=name,
        out_shape=(pltpu.SemaphoreType.DMA((n * nr,)), pltpu.SemaphoreType.DMA((n * nr,)), *[_hbm(a) for a in operands]),
        in_specs=[_HBM_SPEC] * (2 * n), out_specs=(_SEM_SPEC, _SEM_SPEC, *[_HBM_SPEC] * (2 * n)),
        input_output_aliases={i: 2 + i for i in range(2 * n)},
        compiler_params=pltpu.CompilerParams(has_side_effects=_DATAFLOW),
    )(*operands)
    return outs[0], outs[1], list(outs[2:2 + n]), list(outs[2 + n:2 + 2 * n])


def _gather_wait(name, started, ts, after):
    send_sems, recv_sems, srcs, lands = started
    m, nr = len(ts), len(DIRECT_RELS)

    def body(*refs):
        src_refs, land_refs = refs[:m], refs[m:2 * m]
        send_ref, recv_ref = refs[2 * m], refs[2 * m + 1]
        for i, t in enumerate(ts):
            for s, rel in enumerate(DIRECT_RELS):
                dev, block = _rel_peer(rel)
                copy = pltpu.make_async_remote_copy(
                    src_ref=src_refs[i], dst_ref=land_refs[i].at[block],
                    send_sem=send_ref.at[t * nr + s], recv_sem=recv_ref.at[t * nr + s],
                    device_id=dev, device_id_type=MESH)
                copy.wait_send()
                copy.wait_recv()

    operands = [srcs[t] for t in ts] + [lands[t] for t in ts]
    outs = pl.pallas_call(
        body, name=name, out_shape=tuple(_hbm(a) for a in operands),
        in_specs=[_HBM_SPEC] * (2 * m) + [_SEM_SPEC, _SEM_SPEC, pl.BlockSpec(memory_space=pl.ANY)],
        out_specs=tuple([_HBM_SPEC] * (2 * m)), input_output_aliases={i: i for i in range(2 * m)},
        compiler_params=pltpu.CompilerParams(has_side_effects=_DATAFLOW),
    )(*operands, send_sems, recv_sems, after)
    return list(outs[m:])


def _relay_start(name, lands):
    m, nr = len(lands), len(RELAY_RELS)

    def body(*refs):
        land_refs, send_sems, recv_sems = refs[:m], refs[m], refs[m + 1]
        sibling, _ = _rel_peer(1)
        for i in range(m):
            for s, rel in enumerate(RELAY_RELS):
                _, block = _rel_peer(rel)
                pltpu.make_async_remote_copy(
                    src_ref=land_refs[i].at[block], dst_ref=land_refs[i].at[block],
                    send_sem=send_sems.at[i * nr + s], recv_sem=recv_sems.at[i * nr + s],
                    device_id=sibling, device_id_type=MESH).start()

    outs = pl.pallas_call(
        body, name=name,
        out_shape=(pltpu.SemaphoreType.DMA((m * nr,)), pltpu.SemaphoreType.DMA((m * nr,)), *[_hbm(a) for a in lands]),
        in_specs=[_HBM_SPEC] * m, out_specs=(_SEM_SPEC, _SEM_SPEC, *[_HBM_SPEC] * m),
        input_output_aliases={i: 2 + i for i in range(m)},
        compiler_params=pltpu.CompilerParams(has_side_effects=_DATAFLOW),
    )(*lands)
    return outs[0], outs[1], list(outs[2:])


def _relay_wait(name, relayed, after):
    send_sems, recv_sems, lands = relayed
    m, nr = len(lands), len(RELAY_RELS)

    def body(*refs):
        land_refs, send_ref, recv_ref = refs[:m], refs[m], refs[m + 1]
        sibling, _ = _rel_peer(1)
        for i in range(m):
            for s, rel in enumerate(RELAY_RELS):
                _, sent = _rel_peer(rel)
                _, arriving = _rel_peer(rel ^ 1)
                copy = pltpu.make_async_remote_copy(
                    src_ref=land_refs[i].at[sent], dst_ref=land_refs[i].at[arriving],
                    send_sem=send_ref.at[i * nr + s], recv_sem=recv_ref.at[i * nr + s],
                    device_id=sibling, device_id_type=MESH)
                copy.wait_send()
                copy.wait_recv()

    outs = pl.pallas_call(
        body, name=name, out_shape=tuple(_hbm(a) for a in lands),
        in_specs=[_HBM_SPEC] * m + [_SEM_SPEC, _SEM_SPEC, pl.BlockSpec(memory_space=pl.ANY)],
        out_specs=tuple([_HBM_SPEC] * m), input_output_aliases={i: i for i in range(m)},
        compiler_params=pltpu.CompilerParams(has_side_effects=_DATAFLOW),
    )(*lands, send_sems, recv_sems, after)
    return list(outs)


def _ffn_fwd(tag, n, wg, wd):
    gu, act = _gate_up_act(f"ffn_gate_up_{tag}", n, wg)
    wd4 = wd.reshape(NFB, FB, D)
    f = _fwd_kblocked(f"ffn_down_{tag}", act, wd4)
    return (n, gu, act, wg, wd4), f


def _ffn_bwd(tag, dh_out, df, h_in, saved, g_pre, send, mixer):
    n, gu, act, wg, wd4 = saved
    dwd = _bwd_w_kblocked(f"ffn_down_dw_{tag}", act, df).reshape(NDEV, DFF // NDEV, D)
    dgu = _down_dx_act_bwd(f"ffn_down_dx_{tag}", df, wd4, gu).reshape(NDEV, S, FB)
    tok = send({f"down_{tag}": dwd, f"gate_up_{tag}": _bwd_w_cols_blocked(f"ffn_gate_up_dw_{tag}", n, dgu)})
    dn = _bwd_x_cols_blocked(f"ffn_gate_up_dx_{tag}", dgu, wg, after=tok)
    dh_in, (dg_pre,), dy, dg_mixer = _rms_bwd(f"ffn_prenorm_bwd_{tag}", h_in, [(g_pre, dn)], dh_out, F32, then=mixer)
    return dh_in, dg_pre, dy, dg_mixer


def kernel(x, positions, mix_norm_pre, mix_norm_post, ffn_norm_pre, ffn_norm_post, ffn_w_gate_up, ffn_w_down, conv_w_in, conv_w, conv_w_out, kv_norm, w_kv, w_q, w_o, loss_target, m_mix_norm_pre, m_mix_norm_post, m_ffn_norm_pre, m_ffn_norm_post, m_ffn_w_gate_up, m_ffn_w_down, m_conv_w_in, m_conv_w, m_conv_w_out, m_kv_norm, m_w_kv, m_w_q, m_w_o, v_mix_norm_pre, v_mix_norm_post, v_ffn_norm_pre, v_ffn_norm_post, v_ffn_w_gate_up, v_ffn_w_down, v_conv_w_in, v_conv_w, v_conv_w_out, v_kv_norm, v_w_kv, v_w_q, v_w_o):
    me = 4 * lax.axis_index("x") + 2 * lax.axis_index("y") + lax.axis_index("c")
    h0 = x.reshape(S, D)
    target = loss_target.reshape(S, D)
    row = lambda a, l: a[l].reshape(1, D)
    g_kv = kv_norm.reshape(1, D)

    cw_shard = jnp.pad(conv_w[0], ((0, 5), (0, 0)))
    names = ["conv_in", "conv_w", "conv_out", "gate_up_0", "down_0", "kv", "q", "o", "gate_up_1", "down_1"]
    shards = [conv_w_in[0], cw_shard, conv_w_out[0], ffn_w_gate_up[0], ffn_w_down[0],
              w_kv, w_q[0], w_o[0], ffn_w_gate_up[1], ffn_w_down[1]]
    shards = [s if n == "conv_w" else s.astype(BF16) for n, s in zip(names, shards)]
    first = 3
    gather_first = _gather_start("gather_start_conv", shards[:first], me)
    gather_rest = _gather_start("gather_start_rest", shards[first:], me)

    def direct(group, after):
        ts = [names.index(n) for n in group]
        started, ts = (gather_first, ts) if ts[0] < first else (gather_rest, [t - first for t in ts])
        lands = _gather_wait(f"gather_wait_{group[0]}", started, ts, after)
        return _relay_start(f"relay_start_{group[0]}", lands)

    def finish(group, relayed, after):
        return dict(zip(group, _relay_wait(f"relay_wait_{group[0]}", relayed, after)))

    sent = {}

    def send(grads):
        started, token = _exchange_start(f"scatter_start_{next(iter(grads))}", list(grads.values()), me, "scatter")
        for i, name in enumerate(grads):
            sent[name] = (started, i)
        return token

    groups = [["conv_in", "conv_w", "conv_out"], ["gate_up_0", "down_0"], ["kv", "q"], ["o", "gate_up_1", "down_1"]]
    n0 = _rms_fwd("mix_prenorm_0", h0, [row(mix_norm_pre, 0)])[0]
    half = HEAD_DIM // 2
    inv_freq = ROPE_THETA ** (-jnp.arange(half, dtype=F32) / half)
    tables = _rope_tables("rope_tables", positions.reshape(S, 1), jnp.tile(inv_freq, 4).reshape(1, 128))
    w = finish(groups[0], direct(groups[0], tables[0]), n0)
    win = _plain_cols("conv_in_weight", w["conv_in"])
    cw = w["conv_w"].transpose(1, 0, 2).reshape(8, D)
    wout = w["conv_out"].reshape(D, D)
    z = _fwd_rows("conv_in", n0, win, BF16)
    pre = _conv_fwd("conv_gate", z, cw)
    relayed = direct(groups[1], pre)
    y0 = _fwd_rows("conv_out", pre, wout)
    h1, (n1,) = _resid_rms("mix_postnorm_0", h0, y0, row(mix_norm_post, 0), [row(ffn_norm_pre, 0)])
    w = finish(groups[1], relayed, n1)
    ffn0, f0 = _ffn_fwd("0", n1, w["gate_up_0"], w["down_0"])
    relayed = direct(groups[2], ffn0[2])
    h2, (nk, n2) = _resid_rms("ffn_postnorm_0", h1, f0, row(ffn_norm_post, 0), [g_kv, row(mix_norm_pre, 1)])

    w = finish(groups[2], relayed, nk)
    wkv = _plain_cols("kv_weight", w["kv"])
    wq = _plain_cols("q_weight", w["q"])
    qc, kc, vc, o_c, lse_c = [], [], [], [], []
    for g, d in enumerate(DILATIONS):
        q_g, k_g, v_g = _qkv_classes(f"qkv_proj_{g}", n2, nk, wq, wkv, g, d, tables)
        qc.append(q_g)
        kc.append(k_g)
        vc.append(v_g)
    relayed = direct(groups[3], vc[-1])
    for g, d in enumerate(DILATIONS):
        o_g, lse_g = _attn_fwd(f"attn_fwd_{g}", qc[g], kc[g], vc[g], d)
        o_c.append(o_g)
        lse_c.append(lse_g)
    o_mix = _mix_fwd("attn_mix", o_c, lse_c)
    w = finish(groups[3], relayed, o_mix)
    wo = w["o"].reshape(D, D)
    y1 = _fwd_rows("attn_out", o_mix, wo)
    h3, (n3,) = _resid_rms("mix_postnorm_1", h2, y1, row(mix_norm_post, 1), [row(ffn_norm_pre, 1)])
    ffn1, f1 = _ffn_fwd("1", n3, w["gate_up_1"], w["down_1"])

    dh4, df1, dg_fpost1, sq = _resid_rms_loss("ffn_postnorm_1_loss", h3, f1, row(ffn_norm_post, 1), target)

    dh3, dg_fpre1, dy1, dg_mpost1 = _ffn_bwd(
        "1", dh4, df1, h3, ffn1, row(ffn_norm_pre, 1), send, (y1, row(mix_norm_post, 1)))
    dwo = _bwd_w_rows("attn_out_dw", o_mix, dy1).reshape(NDEV, D // NDEV, D)
    do = _bwd_x_rows("attn_out_dx", dy1, wo, F32)
    lane = jnp.arange(128)
    ones_blockdiag = (lane[:, None] // HEAD_DIM == lane[None, :] // HEAD_DIM).astype(BF16)
    mixed = _mix_bwd("attn_mix_bwd", do, o_c, lse_c, ones_blockdiag)
    branch_grads = [_attn_bwd(f"attn_bwd_{g}", qc[g], kc[g], vc[g], mixed[g], lse_c[g], mixed[3 + g], d)
                    for g, d in enumerate(DILATIONS)]
    dq_raw, dkv = _attn_bwd_post("attn_bwd_post", branch_grads, *tables)
    tok = send({"o": dwo, "kv": _bwd_w_cols("kv_proj_dw", nk, dkv, 2 * QW // NDEV),
                "q": _bwd_w_cols("q_proj_dw", n2, dq_raw, QW // NDEV)})
    dnk = _bwd_x_plain("kv_proj_dx", dkv, wkv, after=tok)
    dn2 = _bwd_x_plain("q_proj_dx", dq_raw, wq)
    dh2, (dg_kv, dg_mpre1), df0, dg_fpost0 = _rms_bwd(
        "kv_and_mix_prenorm_bwd_1", h2, [(g_kv, dnk), (row(mix_norm_pre, 1), dn2)], dh3, F32,
        then=(f0, row(ffn_norm_post, 0)))

    dh1, dg_fpre0, dy0, dg_mpost0 = _ffn_bwd(
        "0", dh2, df0, h1, ffn0, row(ffn_norm_pre, 0), send, (y0, row(mix_norm_post, 0)))
    dwout = _bwd_w_rows("conv_out_dw", pre, dy0).reshape(NDEV, D // NDEV, D)
    dpre = _bwd_x_rows("conv_out_dx", dy0, wout, BF16)
    dz, dcw = _conv_bwd("conv_gate_bwd", z, dpre, cw)
    tok = send({"conv_out": dwout, "conv_in": _bwd_w_cols("conv_in_dw", n0, dz, 3 * D // NDEV)})
    dn0 = _bwd_x_plain("conv_in_dx", dz, win, after=tok)
    dh0, (dg_mpre0,) = _rms_bwd("mix_prenorm_bwd_0", h0, [(row(mix_norm_pre, 0), dn0)], dh1, F32)

    small = _pack_small("pack_small_grads", [dg_mpre0, dg_mpre1, dg_mpost0, dg_mpost1, dg_fpre0, dg_fpre1,
                                             dg_fpost0, dg_fpost1, dg_kv], dcw, sq)

    done = [small]

    def upd(tag, w, m, v):
        parts = _exchange_wait(f"scatter_wait_{tag}", *sent[tag], done[-1], "scatter")
        shape = w.shape
        flat = lambda a: a.reshape(parts.shape[1:])
        res = _adamw(f"adamw_{tag}", parts, flat(w), flat(m), flat(v))
        done.append(res[0])
        return [r.reshape(shape) for r in res]

    def upd_layer(tag, l, w, m, v, other):
        parts = _exchange_wait(f"scatter_wait_{tag}_{l}", *sent[f"{tag}_{l}"], done[-1], "scatter")
        res = _adamw(f"adamw_{tag}_{l}", parts, w, m, v, layer=l, other=other)
        done.append(res[0])
        return list(res)

    res = {}
    down_1 = upd_layer("down", 1, ffn_w_down, m_ffn_w_down, v_ffn_w_down, None)
    gate_up_t = [jnp.swapaxes(a, 1, 2) for a in (ffn_w_gate_up, m_ffn_w_gate_up, v_ffn_w_gate_up)]
    gate_up_1 = upd_layer("gate_up", 1, *gate_up_t, None)
    res["w_o"] = upd("o", w_o, m_w_o, v_w_o)
    res["w_q"] = upd("q", w_q, m_w_q, v_w_q)
    res["w_kv"] = upd("kv", w_kv, m_w_kv, v_w_kv)

    small_all = _exchange("gather_small_grads", [small], "gather", done[-1])[0]
    vec = lambda a: a.reshape(1, D)
    gain_res, taps, loss = _adamw_gains("adamw_gains", small_all, [
        (mix_norm_pre, m_mix_norm_pre, v_mix_norm_pre), (mix_norm_post, m_mix_norm_post, v_mix_norm_post),
        (ffn_norm_pre, m_ffn_norm_pre, v_ffn_norm_pre), (ffn_norm_post, m_ffn_norm_post, v_ffn_norm_post),
        (vec(kv_norm), vec(m_kv_norm), vec(v_kv_norm))])
    dcw_mine = lax.dynamic_slice(taps, (0, me * 128), (8, 128))
    pad8 = lambda a, fill: jnp.pad(a[0], ((0, 5), (0, 0)), constant_values=fill)
    cw_res = [r[0:3].reshape(1, 3, 128) for r in
              _adamw("adamw_conv_w", dcw_mine.reshape(1, 8, 128), cw_shard, pad8(m_conv_w, 0.0), pad8(v_conv_w, 1.0))]

    res.update({
        "mix_norm_pre": gain_res[0],
        "mix_norm_post": gain_res[1],
        "ffn_norm_pre": gain_res[2],
        "ffn_norm_post": gain_res[3],
        "kv_norm": [r.reshape(D) for r in gain_res[4]],
        "conv_w": cw_res,
    })
    done.append(small_all)
    res["ffn_w_down"] = upd_layer("down", 0, ffn_w_down, m_ffn_w_down, v_ffn_w_down, down_1)
    res["ffn_w_gate_up"] = [jnp.swapaxes(r, 1, 2) for r in upd_layer("gate_up", 0, *gate_up_t, gate_up_1)]
    res["conv_w_out"] = upd("conv_out", conv_w_out, m_conv_w_out, v_conv_w_out)
    res["conv_w_in"] = upd("conv_in", conv_w_in, m_conv_w_in, v_conv_w_in)
    order = ["mix_norm_pre", "mix_norm_post", "ffn_norm_pre", "ffn_norm_post", "ffn_w_gate_up", "ffn_w_down",
             "conv_w_in", "conv_w", "conv_w_out", "kv_norm", "w_kv", "w_q", "w_o"]
    out = [loss, dh0.reshape(1, S, D)]
    for i in range(4):
        out += [res[name][i] for name in order]
    return tuple(out)
```

```python
import jax
import jax.numpy as jnp
from jax import lax
from jax.experimental import pallas as pl
from jax.experimental.pallas import tpu as pltpu

F32 = jnp.float32
BF16 = jnp.bfloat16

S = 4096
D = 1024
NDEV = 8
HEAD_DIM = 64
QW = 3072
DFF = 2816
FB = 704
NFB = 4
BRANCHES = ((128, 1), (512, 4), (2048, 16))
BAND = 128
ROPE_THETA = 10000.0
RMS_EPS = 1e-6
NEG_INF = -1e30
ADAM_LR, ADAM_B1, ADAM_B2, ADAM_EPS, ADAM_WD, ADAM_STEP = 0.001, 0.9, 0.999, 1e-08, 0.01, 10

VMEM_LIMIT_BYTES = 52 * 1024 * 1024
ROW_TILE = 512
MESH = pl.DeviceIdType.MESH


def _cparams(ngrid):
    return pltpu.CompilerParams(dimension_semantics=("arbitrary",) * ngrid,
                                vmem_limit_bytes=VMEM_LIMIT_BYTES)


def _sds(shape, dtype):
    return jax.ShapeDtypeStruct(tuple(shape), dtype)


_DIMS = {"nn": (((1,), (0,)), ((), ())),
         "nt": (((1,), (1,)), ((), ())),
         "tn": (((0,), (0,)), ((), ()))}


def _matmul(name, a, b, *, mode, grid, a_blk, a_map, b_blk, b_map, o_shape, o_blk, o_map, out_dtype, after=None,
            out_groups=1):
    nk = grid[2]
    dims = _DIMS[mode]
    acc_shape = tuple(s for s in o_blk if s is not None)
    if out_groups > 1:
        acc_shape = (acc_shape[1], out_groups * acc_shape[2])
    extra = [] if after is None else [after]

    def store(o_ref, val):
        if out_groups == 1:
            o_ref[...] = val.astype(o_ref.dtype)
        else:
            n = o_ref.shape[-1]
            for grp in range(out_groups):
                o_ref[grp] = val[:, grp * n:(grp + 1) * n].astype(o_ref.dtype)

    def body(a_ref, b_ref, *rest):
        o_ref, scratch = rest[len(extra)], rest[len(extra) + 1:]
        part = lax.dot_general(a_ref[...], b_ref[...], dims, preferred_element_type=F32)
        if nk == 1:
            store(o_ref, part)
            return
        acc_ref = scratch[0]
        k = pl.program_id(2)

        @pl.when(k == 0)
        def _():
            acc_ref[...] = part

        @pl.when(k > 0)
        def _():
            acc_ref[...] += part

        @pl.when(k == nk - 1)
        def _():
            store(o_ref, acc_ref[...])

    return pl.pallas_call(
        body, name=name, grid=grid,
        in_specs=[pl.BlockSpec(a_blk, a_map), pl.BlockSpec(b_blk, b_map)] + [pl.BlockSpec(memory_space=pl.ANY)] * len(extra),
        out_specs=pl.BlockSpec(o_blk, o_map),
        out_shape=_sds(o_shape, out_dtype),
        scratch_shapes=[] if nk == 1 else [pltpu.VMEM(acc_shape, F32)],
        compiler_params=_cparams(3),
    )(a, b, *extra)


TM = 1024
TK = S


def _fwd_rows(name, a, w, out_dtype=F32):
    kdim, n = w.shape
    tn = 512
    return _matmul(name, a, w, mode="nn", grid=(S // TM, n // tn, 1),
                   a_blk=(TM, kdim), a_map=lambda i, j, k: (i, 0),
                   b_blk=(kdim, tn), b_map=lambda i, j, k: (0, j),
                   o_shape=(S, n), o_blk=(TM, tn), o_map=lambda i, j, k: (i, j), out_dtype=out_dtype)


def _fwd_kblocked(name, a4, w4):
    nb, _, kb = a4.shape
    n = w4.shape[2]

    def body(a_ref, w_ref, o_ref):
        acc = _dot_nn(a_ref[0], w_ref[0])
        for j in range(1, nb):
            acc = acc + _dot_nn(a_ref[j], w_ref[j])
        o_ref[...] = acc

    return pl.pallas_call(
        body, name=name, grid=(S // TM,),
        in_specs=[pl.BlockSpec((nb, TM, kb), lambda i: (0, i, 0)), pl.BlockSpec((nb, kb, n), lambda i: (0, 0, 0))],
        out_specs=pl.BlockSpec((TM, n), lambda i: (i, 0)), out_shape=_sds((S, n), F32),
        compiler_params=_cparams(1),
    )(a4, w4)


def _bwd_x_cols_blocked(name, dy8, wg, after):
    _, kdim, n = wg.shape
    nk = NDEV // 2

    def body(a_ref, b_ref, after_ref, o_ref, acc_ref):
        k = pl.program_id(1)
        part = _dot_nt(a_ref[0], b_ref[0]) + _dot_nt(a_ref[1], b_ref[1])

        @pl.when(k == 0)
        def _():
            acc_ref[...] = part

        @pl.when(k > 0)
        def _():
            acc_ref[...] += part

        @pl.when(k == nk - 1)
        def _():
            o_ref[...] = acc_ref[...]

    return pl.pallas_call(
        body, name=name, grid=(S // TM, nk),
        in_specs=[pl.BlockSpec((2, None, TM, n), lambda i, k: (0, k, i, 0)),
                  pl.BlockSpec((2, None, kdim, n), lambda i, k: (0, k, 0, 0)),
                  pl.BlockSpec(memory_space=pl.ANY)],
        out_specs=pl.BlockSpec((TM, kdim), lambda i, k: (i, 0)), out_shape=_sds((S, kdim), F32),
        scratch_shapes=[pltpu.VMEM((TM, kdim), F32)],
        compiler_params=_cparams(2),
    )(dy8.reshape(2, nk, S, n), wg.reshape(2, nk, kdim, n), after)


def _bwd_x_rows(name, dy, w, out_dtype, after=None):
    kdim, n = w.shape
    tkk = 512
    return _matmul(name, dy, w, mode="nt", grid=(S // TM, kdim // tkk, 1),
                   a_blk=(TM, n), a_map=lambda i, j, k: (i, 0),
                   b_blk=(tkk, n), b_map=lambda i, j, k: (j, 0),
                   o_shape=(S, kdim), o_blk=(TM, tkk), o_map=lambda i, j, k: (i, j), out_dtype=out_dtype, after=after)


DW_COLS = 768


def _bwd_w_cols(name, a, dy, n):
    kdim = a.shape[1]
    groups = DW_COLS // n
    return _matmul(name, a, dy, mode="tn", grid=(1, NDEV // groups, S // TK),
                   a_blk=(TK, kdim), a_map=lambda i, j, k: (k, 0),
                   b_blk=(TK, DW_COLS), b_map=lambda i, j, k: (k, j),
                   o_shape=(NDEV, kdim, n), o_blk=(groups, kdim, n) if groups > 1 else (None, kdim, n),
                   o_map=lambda i, j, k: (j, 0, 0), out_dtype=BF16, out_groups=groups)


def _bwd_x_plain(name, dy, w, after=None):
    kdim, n = w.shape
    tm = TM if n <= 3 * D else TM // 2
    return _matmul(name, dy, w, mode="nt", grid=(S // tm, 1, 1),
                   a_blk=(tm, n), a_map=lambda i, j, k: (i, 0),
                   b_blk=(kdim, n), b_map=lambda i, j, k: (0, 0),
                   o_shape=(S, kdim), o_blk=(tm, kdim), o_map=lambda i, j, k: (i, 0), out_dtype=F32, after=after)


def _bwd_w_cols_blocked(name, a, dy8):
    kdim = a.shape[1]
    n = dy8.shape[2]
    return _matmul(name, dy8, a, mode="tn", grid=(1, NDEV, S // TK),
                   a_blk=(None, TK, n), a_map=lambda i, j, k: (j, k, 0),
                   b_blk=(TK, kdim), b_map=lambda i, j, k: (k, 0),
                   o_shape=(NDEV, n, kdim), o_blk=(None, n, kdim), o_map=lambda i, j, k: (j, 0, 0), out_dtype=BF16)


def _bwd_w_rows(name, a, dy):
    kdim = a.shape[1]
    n = dy.shape[1]
    tmm = 512
    return _matmul(name, a, dy, mode="tn", grid=(kdim // tmm, 1, S // TK),
                   a_blk=(TK, tmm), a_map=lambda i, j, k: (k, i),
                   b_blk=(TK, n), b_map=lambda i, j, k: (k, 0),
                   o_shape=(kdim, n), o_blk=(tmm, n), o_map=lambda i, j, k: (i, 0), out_dtype=BF16)


def _bwd_w_kblocked(name, a4, dy):
    nb, _, kb = a4.shape
    n = dy.shape[1]
    return _matmul(name, a4, dy, mode="tn", grid=(nb, 1, S // TK),
                   a_blk=(None, TK, kb), a_map=lambda i, j, k: (i, k, 0),
                   b_blk=(TK, n), b_map=lambda i, j, k: (k, 0),
                   o_shape=(nb, kb, n), o_blk=(None, kb, n), o_map=lambda i, j, k: (i, 0, 0), out_dtype=BF16)


def _rstd(x):
    return lax.rsqrt(jnp.mean(x * x, axis=-1, keepdims=True) + RMS_EPS)


def _row_spec(tm=ROW_TILE, width=D):
    return pl.BlockSpec((tm, width), lambda i: (i, 0))


def _vec_spec(rows=1, width=D):
    return pl.BlockSpec((rows, width), lambda i: (0, 0))


def _rms_fwd(name, x, gains):
    n = len(gains)

    def body(x_ref, *refs):
        x_val = x_ref[...]
        xh = x_val * _rstd(x_val)
        for g_ref, o_ref in zip(refs[:n], refs[n:]):
            o_ref[...] = (xh * g_ref[...]).astype(o_ref.dtype)

    outs = pl.pallas_call(
        body, name=name, grid=(S // ROW_TILE,),
        in_specs=[_row_spec()] + [_vec_spec()] * n,
        out_specs=[_row_spec()] * n,
        out_shape=[_sds((S, D), BF16)] * n,
        compiler_params=_cparams(1),
    )(x, *gains)
    return list(outs)


def _resid_rms(name, h, y, g, next_gains):
    n = len(next_gains)

    def body(h_ref, y_ref, g_ref, *refs):
        y_val = y_ref[...]
        h_new = h_ref[...] + (y_val * _rstd(y_val)) * g_ref[...]
        refs[n][...] = h_new
        hh = h_new * _rstd(h_new)
        for g2_ref, o_ref in zip(refs[:n], refs[n + 1:]):
            o_ref[...] = (hh * g2_ref[...]).astype(o_ref.dtype)

    outs = pl.pallas_call(
        body, name=name, grid=(S // ROW_TILE,),
        in_specs=[_row_spec(), _row_spec(), _vec_spec()] + [_vec_spec()] * n,
        out_specs=[_row_spec()] * (n + 1), out_shape=[_sds((S, D), F32)] + [_sds((S, D), BF16)] * n,
        compiler_params=_cparams(1),
    )(h, y, g, *next_gains)
    return outs[0], list(outs[1:])


def _resid_rms_loss(name, h, y, g, target):
    def body(h_ref, y_ref, g_ref, t_ref, dh_ref, dy_ref, dg_ref, part_ref):
        y_val = y_ref[...]
        gain = g_ref[...]
        e = h_ref[...] + (y_val * _rstd(y_val)) * gain - t_ref[...]
        dh = e * (1.0 / D)
        dh_ref[...] = dh
        step = pl.program_id(0)
        dy_ref[...] = _norm_bwd_rows(y_val, gain, dh, dg_ref, step).astype(dy_ref.dtype)
        part = jnp.sum(e * e, axis=0, keepdims=True)

        @pl.when(step == 0)
        def _():
            part_ref[...] = part

        @pl.when(step > 0)
        def _():
            part_ref[...] += part

    return pl.pallas_call(
        body, name=name, grid=(S // ROW_TILE,),
        in_specs=[_row_spec(), _row_spec(), _vec_spec(), _row_spec()],
        out_specs=[_row_spec(), _row_spec(), _vec_spec(8), _vec_spec()],
        out_shape=[_sds((S, D), F32), _sds((S, D), BF16), _sds((8, D), F32), _sds((1, D), F32)],
        compiler_params=_cparams(1),
    )(h, y, g, target)


def _norm_bwd_rows(x_val, g, dn, dg_ref, step):
    r = _rstd(x_val)
    xh = x_val * r
    dxh = dn * g
    part = jnp.sum(dn * xh, axis=0, keepdims=True)

    @pl.when(step == 0)
    def _():
        dg_ref[...] = jnp.zeros_like(dg_ref)

    dg_ref[0:1, :] += part
    return r * (dxh - xh * jnp.mean(dxh * xh, axis=-1, keepdims=True))


def _rms_bwd(name, x, pairs, dres, out_dtype, then=None):
    n = len(pairs)
    has_res = dres is not None
    chained = then is not None

    def body(x_ref, *refs):
        g_refs = refs[0:2 * n:2]
        dn_refs = refs[1:2 * n:2]
        pos = 2 * n
        res_ref = refs[pos] if has_res else None
        pos += int(has_res)
        if chained:
            y_ref, gy_ref = refs[pos], refs[pos + 1]
            pos += 2
        dx_ref = refs[pos]
        dg_refs = refs[pos + 1:pos + 1 + n]
        step = pl.program_id(0)
        x_val = x_ref[...]
        acc = res_ref[...] if has_res else jnp.zeros_like(x_val)
        for g_ref, dn_ref, dg_ref in zip(g_refs, dn_refs, dg_refs):
            acc = acc + _norm_bwd_rows(x_val, g_ref[...], dn_ref[...].astype(F32), dg_ref, step)
        dx_ref[...] = acc.astype(dx_ref.dtype)
        if chained:
            dy_ref, dgy_ref = refs[pos + 1 + n], refs[pos + 2 + n]
            dy_ref[...] = _norm_bwd_rows(y_ref[...], gy_ref[...], acc, dgy_ref, step).astype(dy_ref.dtype)

    operands = [x]
    in_specs = [_row_spec()]
    for g, dn in pairs:
        operands += [g, dn]
        in_specs += [_vec_spec(), _row_spec()]
    if has_res:
        operands.append(dres)
        in_specs.append(_row_spec())
    if chained:
        operands += [then[0], then[1]]
        in_specs += [_row_spec(), _vec_spec()]
    extra = int(chained)
    outs = pl.pallas_call(
        body, name=name, grid=(S // ROW_TILE,),
        in_specs=in_specs,
        out_specs=[_row_spec()] + [_vec_spec(8)] * n + [_row_spec(), _vec_spec(8)] * extra,
        out_shape=[_sds((S, D), out_dtype)] + [_sds((8, D), F32)] * n + [_sds((S, D), BF16), _sds((8, D), F32)] * extra,
        compiler_params=_cparams(1),
    )(*operands)
    if chained:
        return outs[0], list(outs[1:1 + n]), outs[1 + n], outs[2 + n]
    return outs[0], list(outs[1:])


def _shift_down(u, prev8, k):
    r = pltpu.roll(u, k, 0)
    p = pltpu.roll(prev8, k, 0)
    row = lax.broadcasted_iota(jnp.int32, prev8.shape, 0)
    top = jnp.where(row < k, p, r[0:8])
    return jnp.concatenate([top, r[8:]], axis=0)


def _shift_up(u, next8, k):
    tm = u.shape[0]
    r = pltpu.roll(u, tm - k, 0)
    p = pltpu.roll(next8, 8 - k, 0)
    row = lax.broadcasted_iota(jnp.int32, next8.shape, 0)
    bot = jnp.where(row >= 8 - k, p, r[tm - 8:tm])
    return jnp.concatenate([r[:tm - 8], bot], axis=0)


CONV_TILE = 512


def _halo_prev(col):
    return pl.BlockSpec((8, D), lambda i: (jnp.maximum(i * (CONV_TILE // 8) - 1, 0), col))


def _halo_next(col):
    last = S // 8 - 1
    return pl.BlockSpec((8, D), lambda i: (jnp.minimum((i + 1) * (CONV_TILE // 8), last), col))


def _conv_fwd(name, z, cw):
    def body(b_ref, c_ref, h_ref, cp_ref, hp_ref, cw_ref, o_ref):
        i = pl.program_id(0)
        u = c_ref[...].astype(F32) * h_ref[...].astype(F32)
        up = cp_ref[...].astype(F32) * hp_ref[...].astype(F32)
        up = jnp.where(i > 0, up, 0.0)
        cv = cw_ref[0:1, :] * _shift_down(u, up, 2) + cw_ref[1:2, :] * _shift_down(u, up, 1) + cw_ref[2:3, :] * u
        o_ref[...] = (b_ref[...].astype(F32) * cv).astype(o_ref.dtype)

    col = lambda c: pl.BlockSpec((CONV_TILE, D), lambda i: (i, c))
    return pl.pallas_call(
        body, name=name, grid=(S // CONV_TILE,),
        in_specs=[col(0), col(1), col(2), _halo_prev(1), _halo_prev(2), _vec_spec(8)],
        out_specs=_row_spec(CONV_TILE), out_shape=_sds((S, D), BF16),
        compiler_params=_cparams(1),
    )(z, z, z, z, z, cw)


def _conv_bwd(name, z, dpre, cw):
    nsteps = S // CONV_TILE

    def body(b_ref, c_ref, h_ref, cp_ref, hp_ref, dp_ref, dpn_ref, bn_ref, cw_ref, dz_ref, dcw_ref):
        i = pl.program_id(0)
        b = b_ref[...].astype(F32)
        c = c_ref[...].astype(F32)
        h = h_ref[...].astype(F32)
        dp = dp_ref[...].astype(F32)
        u = c * h
        up = jnp.where(i > 0, cp_ref[...].astype(F32) * hp_ref[...].astype(F32), 0.0)
        s1 = _shift_down(u, up, 1)
        s2 = _shift_down(u, up, 2)
        w0, w1, w2 = cw_ref[0:1, :], cw_ref[1:2, :], cw_ref[2:3, :]
        cv = w0 * s2 + w1 * s1 + w2 * u
        dcv = dp * b
        dcvn = jnp.where(i < nsteps - 1, dpn_ref[...].astype(F32) * bn_ref[...].astype(F32), 0.0)
        du = w2 * dcv + w1 * _shift_up(dcv, dcvn, 1) + w0 * _shift_up(dcv, dcvn, 2)
        dz_ref[:, 0:D] = (dp * cv).astype(dz_ref.dtype)
        dz_ref[:, D:2 * D] = (du * h).astype(dz_ref.dtype)
        dz_ref[:, 2 * D:3 * D] = (du * c).astype(dz_ref.dtype)

        @pl.when(i == 0)
        def _():
            dcw_ref[...] = jnp.zeros_like(dcw_ref)

        dcw_ref[0:1, :] += jnp.sum(dcv * s2, axis=0, keepdims=True)
        dcw_ref[1:2, :] += jnp.sum(dcv * s1, axis=0, keepdims=True)
        dcw_ref[2:3, :] += jnp.sum(dcv * u, axis=0, keepdims=True)

    col = lambda c: pl.BlockSpec((CONV_TILE, D), lambda i: (i, c))
    return pl.pallas_call(
        body, name=name, grid=(nsteps,),
        in_specs=[col(0), col(1), col(2), _halo_prev(1), _halo_prev(2),
                  _row_spec(CONV_TILE), _halo_next(0), _halo_next(0), _vec_spec(8)],
        out_specs=[pl.BlockSpec((CONV_TILE, 3 * D), lambda i: (i, 0)), _vec_spec(8)],
        out_shape=[_sds((S, 3 * D), BF16), _sds((8, D), F32)],
        compiler_params=_cparams(1),
    )(z, z, z, z, z, dpre, dpre, z, cw)


FFN_TM = 2048
_GU_BLOCK = pl.BlockSpec((2, None, FFN_TM, FB), lambda i, j: (0, j, i, 0))


def _gate_up_act(name, a, wg):
    kdim = a.shape[1]

    def body(a_ref, wgate_ref, wup_ref, gu_ref, act_ref):
        x = a_ref[...]
        g = _dot_nn(x, wgate_ref[...])
        u = _dot_nn(x, wup_ref[...])
        gu_ref[0] = g.astype(gu_ref.dtype)
        gu_ref[1] = u.astype(gu_ref.dtype)
        act_ref[...] = (g * jax.nn.sigmoid(g) * u).astype(act_ref.dtype)

    return pl.pallas_call(
        body, name=name, grid=(S // FFN_TM, NFB),
        in_specs=[pl.BlockSpec((FFN_TM, kdim), lambda i, j: (i, 0)),
                  pl.BlockSpec((None, kdim, FB), lambda i, j: (j, 0, 0)),
                  pl.BlockSpec((None, kdim, FB), lambda i, j: (j + NFB, 0, 0))],
        out_specs=[_GU_BLOCK, pl.BlockSpec((None, FFN_TM, FB), lambda i, j: (j, i, 0))],
        out_shape=[_sds((2, NFB, S, FB), BF16), _sds((NFB, S, FB), BF16)],
        compiler_params=_cparams(2),
    )(a, wg, wg)


def _down_dx_act_bwd(name, df, w4, gu):
    _, kb, n = w4.shape

    def body(df_ref, w_ref, gu_ref, o_ref):
        d = _dot_nt(df_ref[...], w_ref[...])
        g = gu_ref[0].astype(F32)
        u = gu_ref[1].astype(F32)
        sg = jax.nn.sigmoid(g)
        o_ref[0] = (d * u * sg * (1.0 + g * (1.0 - sg))).astype(o_ref.dtype)
        o_ref[1] = (d * g * sg).astype(o_ref.dtype)

    return pl.pallas_call(
        body, name=name, grid=(S // FFN_TM, NFB),
        in_specs=[pl.BlockSpec((FFN_TM, n), lambda i, j: (i, 0)), pl.BlockSpec((None, kb, n), lambda i, j: (j, 0, 0)),
                  _GU_BLOCK],
        out_specs=_GU_BLOCK, out_shape=_sds((2, NFB, S, FB), BF16),
        compiler_params=_cparams(2),
    )(df, w4, gu)


def _rope_tables(name, pos_col, inv_freq_row):
    def body(pos_ref, f_ref, cos_ref, sin_ref):
        ang = pos_ref[...].astype(F32) * f_ref[...]
        lane = lax.broadcasted_iota(jnp.int32, ang.shape, 1)
        s = jnp.sin(ang)
        cos_ref[...] = jnp.cos(ang)
        sin_ref[...] = jnp.where((lane % HEAD_DIM) < HEAD_DIM // 2, -s, s)

    tab = pl.BlockSpec((ROW_TILE, 128), lambda i: (i, 0))
    return pl.pallas_call(
        body, name=name, grid=(S // ROW_TILE,),
        in_specs=[pl.BlockSpec((ROW_TILE, 1), lambda i: (i, 0)), _vec_spec(1, 128)],
        out_specs=[tab, tab], out_shape=[_sds((S, 128), F32)] * 2,
        compiler_params=_cparams(1),
    )(pos_col, inv_freq_row)


def _swap_halves(t):
    lane = lax.broadcasted_iota(jnp.int32, t.shape, 1)
    first = (lane % HEAD_DIM) < HEAD_DIM // 2
    return jnp.where(first, pltpu.roll(t, 128 - HEAD_DIM // 2, 1), pltpu.roll(t, HEAD_DIM // 2, 1))


NCHUNK = D // 128


def _chunk(c, base=0):
    return slice(base + c * 128, base + (c + 1) * 128)


def _class_rows(r, d, tm):
    return pl.ds(r, tm // d, stride=d) if d > 1 else slice(None)


def _class_block(d, tm):
    return pl.BlockSpec((tm // d, d * D), lambda i: (i, 0))


def _tokens_from_classes(blk_ref, tmp_ref, d, tm):
    for r in range(d):
        for c in range(NCHUNK):
            tmp_ref[c, _class_rows(r, d, tm), :] = blk_ref[:, _chunk(c, r * D)].astype(F32)


def _classes_from_tokens(tmp_ref, blk_ref, d, tm):
    for r in range(d):
        for c in range(NCHUNK):
            blk_ref[:, _chunk(c, r * D)] = tmp_ref[c, _class_rows(r, d, tm), :].astype(blk_ref.dtype)


def _qkv_classes(name, n2, nk, wq, wkv, g, d, tables):
    def emit(acc, cos_ref, sin_ref, o_ref, tmp_ref, scale):
        for c in range(NCHUNK):
            tmp_ref[c] = acc[:, _chunk(c)]
        for r in range(d):
            rows = _class_rows(r, d, TM)
            if scale is not None:
                cs = cos_ref[rows, :]
                sn = sin_ref[rows, :]
            for c in range(NCHUNK):
                x = tmp_ref[c, rows, :]
                if scale is not None:
                    x = (x * cs + _swap_halves(x) * sn) * scale
                o_ref[:, _chunk(c, r * D)] = x.astype(o_ref.dtype)

    def body(n2_ref, nk_ref, wq_ref, wk_ref, wv_ref, cos_ref, sin_ref, q_ref, k_ref, v_ref, tmp_ref):
        emit(_dot_nn(n2_ref[...], wq_ref[...]), cos_ref, sin_ref, q_ref, tmp_ref, HEAD_DIM ** -0.5)
        x = nk_ref[...]
        emit(_dot_nn(x, wk_ref[...]), cos_ref, sin_ref, k_ref, tmp_ref, 1.0)
        emit(_dot_nn(x, wv_ref[...]), cos_ref, sin_ref, v_ref, tmp_ref, None)

    nbr = len(DILATIONS)
    act = pl.BlockSpec((TM, D), lambda i: (i, 0))
    tab = pl.BlockSpec((TM, 128), lambda i: (i, 0))
    wcol = lambda col: pl.BlockSpec((D, D), lambda i: (0, col))
    return pl.pallas_call(
        body, name=name, grid=(S // TM,),
        in_specs=[act, act, wcol(g), wcol(g), wcol(nbr + g), tab, tab],
        out_specs=[_class_block(d, TM)] * 3, out_shape=[_sds((S // d, d * D), BF16)] * 3,
        scratch_shapes=[pltpu.VMEM((NCHUNK, TM, 128), F32)],
        compiler_params=_cparams(1),
    )(n2, nk, wq, wkv, wkv, *tables)


ATTN_CHAINS = 8


def _attn_units(d):
    nblk = S // d // BAND
    return max(1, 2 * ATTN_CHAINS // nblk)


def _class_spec(d):
    return pl.BlockSpec((S // d, 128 * _attn_units(d)), lambda cb: (0, cb))


def _dot_nt(a, b):
    return lax.dot_general(a, b, _DIMS["nt"], preferred_element_type=F32)


def _dot_tn(a, b):
    return lax.dot_general(a, b, _DIMS["tn"], preferred_element_type=F32)


def _dot_nn(a, b):
    return lax.dot_general(a, b, _DIMS["nn"], preferred_element_type=F32)


def _band_mask(nkeys):
    qi = lax.broadcasted_iota(jnp.int32, (2 * BAND, nkeys), 0) % BAND
    kj = lax.broadcasted_iota(jnp.int32, (2 * BAND, nkeys), 1)
    if nkeys == BAND:
        return kj <= qi
    dist = qi + BAND - kj
    return (dist >= 0) & (dist <= BAND)


def _stack_heads(x):
    row = lax.broadcasted_iota(jnp.int32, (2 * BAND, 128), 0)
    lane = lax.broadcasted_iota(jnp.int32, (2 * BAND, 128), 1)
    keep = (row < BAND) == (lane < HEAD_DIM)
    return jnp.where(keep, jnp.concatenate([x, x], axis=0), jnp.zeros((), x.dtype))


def _unstack(x2):
    first_head = lax.broadcasted_iota(jnp.int32, (BAND, 128), 1) < HEAD_DIM
    return jnp.where(first_head, x2[:BAND], x2[BAND:])


def _for_later_blocks(nblk, units, fn):
    all_lanes = [slice(u * 128, (u + 1) * 128) for u in range(units)]
    unroll = max(1, ATTN_CHAINS // units)
    trips = (nblk - 1) // unroll
    if trips > 1:
        def step(i, carry):
            for j in range(unroll):
                for lanes in all_lanes:
                    fn(pl.multiple_of((1 + i * unroll + j) * BAND, BAND), lanes)
            return carry

        lax.fori_loop(0, trips, step, 0)
    else:
        trips = 0
    for sb in range(1 + trips * unroll, nblk):
        for lanes in all_lanes:
            fn(sb * BAND, lanes)


def _attn_fwd(name, q, k, v, d):
    nblk = S // d // BAND
    units = _attn_units(d)

    def body(q_ref, k_ref, v_ref, o_ref, lse_ref):
        def block(r0, k0, nkeys, lanes):
            q2 = _stack_heads(q_ref[pl.ds(r0, BAND), lanes])
            s = jnp.where(_band_mask(nkeys), _dot_nt(q2, k_ref[pl.ds(k0, nkeys), lanes]), NEG_INF)
            m = jnp.max(s, axis=-1, keepdims=True)
            p = jnp.exp(s - m)
            l = jnp.sum(p, axis=-1, keepdims=True)
            o2 = _dot_nn(p.astype(BF16), v_ref[pl.ds(k0, nkeys), lanes]) / l
            lse2 = jnp.broadcast_to(m + jnp.log(l), (2 * BAND, 128))
            o_ref[pl.ds(r0, BAND), lanes] = _unstack(o2).astype(o_ref.dtype)
            lse_ref[pl.ds(r0, BAND), lanes] = _unstack(lse2)

        for u in range(units):
            block(0, 0, BAND, slice(u * 128, (u + 1) * 128))

        _for_later_blocks(nblk, units, lambda r0, lanes: block(r0, r0 - BAND, 2 * BAND, lanes))

    spec = _class_spec(d)
    return pl.pallas_call(
        body, name=name, grid=(8 * d // units,),
        in_specs=[spec] * 3, out_specs=[spec] * 2,
        out_shape=[_sds((S // d, d * D), BF16), _sds((S // d, d * D), F32)],
        compiler_params=_cparams(1),
    )(q, k, v)


def _attn_bwd(name, q, k, v, do, lse, dd, d):
    nblk = S // d // BAND
    units = _attn_units(d)

    def body(q_ref, k_ref, v_ref, do_ref, lse_ref, dd_ref, dq_ref, dk_out, dv_out, dk_ref, dv_ref):
        def column(ref, r0, lanes, nkeys):
            tile = ref[pl.ds(r0, BAND), lanes]
            other = pltpu.roll(tile, HEAD_DIM, 1)
            first_head = lax.broadcasted_iota(jnp.int32, tile.shape, 1) < HEAD_DIM
            both = jnp.concatenate([jnp.where(first_head, tile, other), jnp.where(first_head, other, tile)], axis=0)
            return both if nkeys == BAND else jnp.concatenate([both, both], axis=1)

        def block(r0, k0, nkeys, lanes, first):
            q2 = _stack_heads(q_ref[pl.ds(r0, BAND), lanes])
            do2 = _stack_heads(do_ref[pl.ds(r0, BAND), lanes])
            kk = k_ref[pl.ds(k0, nkeys), lanes]
            vv = v_ref[pl.ds(k0, nkeys), lanes]
            s = jnp.where(_band_mask(nkeys), _dot_nt(q2, kk), NEG_INF)
            p = jnp.exp(s - column(lse_ref, r0, lanes, nkeys))
            ds = (p * (_dot_nt(do2, vv) - column(dd_ref, r0, lanes, nkeys))).astype(BF16)
            dq_ref[pl.ds(r0, BAND), lanes] = _unstack(_dot_nn(ds, kk)).astype(dq_ref.dtype)
            dk_part = _dot_tn(ds, q2)
            dv_part = _dot_tn(p.astype(BF16), do2)
            if first:
                dk_ref[pl.ds(k0, nkeys), lanes] = dk_part
                dv_ref[pl.ds(k0, nkeys), lanes] = dv_part
            else:
                dk_ref[pl.ds(k0, BAND), lanes] += dk_part[:BAND]
                dv_ref[pl.ds(k0, BAND), lanes] += dv_part[:BAND]
                dk_ref[pl.ds(k0 + BAND, BAND), lanes] = dk_part[BAND:]
                dv_ref[pl.ds(k0 + BAND, BAND), lanes] = dv_part[BAND:]

        for u in range(units):
            block(0, 0, BAND, slice(u * 128, (u + 1) * 128), True)

        _for_later_blocks(nblk, units, lambda r0, lanes: block(r0, r0 - BAND, 2 * BAND, lanes, False))
        dk_out[...] = dk_ref[...].astype(dk_out.dtype)
        dv_out[...] = dv_ref[...].astype(dv_out.dtype)

    spec = _class_spec(d)
    return pl.pallas_call(
        body, name=name, grid=(8 * d // units,),
        in_specs=[spec] * 6, out_specs=[spec] * 3,
        out_shape=[_sds((S // d, d * D), BF16)] * 3,
        scratch_shapes=[pltpu.VMEM((S // d, 128 * units), F32)] * 2,
        compiler_params=_cparams(1),
    )(q, k, v, do, lse, dd)


MIX_TILE = 256
DILATIONS = tuple(d for _, d in BRANCHES)


def _branch_weights(la, lb, lc):
    m = jnp.maximum(jnp.maximum(la, lb), lc)
    ea, eb, ec = jnp.exp(la - m), jnp.exp(lb - m), jnp.exp(lc - m)
    den = ea + eb + ec
    return ea / den, eb / den, ec / den


def _mix_operands(outs, lses):
    specs = [_class_block(d, MIX_TILE) for d in DILATIONS] * 2
    scratch = [pltpu.VMEM((NCHUNK, MIX_TILE, 128), F32)] * 4
    return list(outs) + list(lses), specs, scratch


def _mix_fwd(name, outs, lses):
    def body(o0, o1, o2, l0, l1, l2, o_ref, to1, to2, tl1, tl2):
        for blk, tmp, d in ((o1, to1, DILATIONS[1]), (o2, to2, DILATIONS[2]), (l1, tl1, DILATIONS[1]), (l2, tl2, DILATIONS[2])):
            _tokens_from_classes(blk, tmp, d, MIX_TILE)
        for c in range(NCHUNK):
            wa, wb, wc = _branch_weights(l0[:, _chunk(c)], tl1[c], tl2[c])
            o_ref[:, _chunk(c)] = (wa * o0[:, _chunk(c)].astype(F32) + wb * to1[c] + wc * to2[c]).astype(o_ref.dtype)

    operands, specs, scratch = _mix_operands(outs, lses)
    return pl.pallas_call(
        body, name=name, grid=(S // MIX_TILE,),
        in_specs=specs, out_specs=_row_spec(MIX_TILE), out_shape=_sds((S, D), BF16),
        scratch_shapes=scratch, compiler_params=_cparams(1),
    )(*operands)


def _head_sum(x, ones_blockdiag):
    hi = x.astype(BF16)
    r1 = x - hi.astype(F32)
    mid = r1.astype(BF16)
    lo = (r1 - mid.astype(F32)).astype(BF16)
    return _dot_nn(hi, ones_blockdiag) + _dot_nn(mid, ones_blockdiag) + _dot_nn(lo, ones_blockdiag)


def _mix_bwd(name, do, outs, lses, ones_blockdiag):
    def body(do_ref, o0, o1, o2, l0, l1, l2, ones_ref, d0, d1, d2, t0, t1, t2,
             to1, to2, tl1, tl2, td1, td2, tt1, tt2):
        for blk, tmp, d in ((o1, to1, DILATIONS[1]), (o2, to2, DILATIONS[2]), (l1, tl1, DILATIONS[1]), (l2, tl2, DILATIONS[2])):
            _tokens_from_classes(blk, tmp, d, MIX_TILE)
        ones = ones_ref[...]
        for c in range(NCHUNK):
            w = _branch_weights(l0[:, _chunk(c)], tl1[c], tl2[c])
            dov = do_ref[:, _chunk(c)]
            o = w[0] * o0[:, _chunk(c)].astype(F32) + w[1] * to1[c] + w[2] * to2[c]
            t = _head_sum(dov * o, ones)
            d0[:, _chunk(c)] = (w[0] * dov).astype(d0.dtype)
            t0[:, _chunk(c)] = w[0] * t
            td1[c], tt1[c] = w[1] * dov, w[1] * t
            td2[c], tt2[c] = w[2] * dov, w[2] * t
        for tmp, blk, d in ((td1, d1, DILATIONS[1]), (tt1, t1, DILATIONS[1]), (td2, d2, DILATIONS[2]), (tt2, t2, DILATIONS[2])):
            _classes_from_tokens(tmp, blk, d, MIX_TILE)

    operands, specs, scratch = _mix_operands(outs, lses)
    out_specs = [_class_block(d, MIX_TILE) for d in DILATIONS] * 2
    out_shape = [_sds((S // d, d * D), BF16) for d in DILATIONS] + [_sds((S // d, d * D), F32) for d in DILATIONS]
    return pl.pallas_call(
        body, name=name, grid=(S // MIX_TILE,),
        in_specs=[_row_spec(MIX_TILE)] + specs + [_vec_spec(128, 128)],
        out_specs=out_specs, out_shape=out_shape,
        scratch_shapes=scratch + [pltpu.VMEM((NCHUNK, MIX_TILE, 128), F32)] * 4,
        compiler_params=_cparams(1),
    )(do, *operands, ones_blockdiag)


def _attn_bwd_post(name, grads, cos_t, sin_t):
    tm = MIX_TILE
    scale = HEAD_DIM ** -0.5

    def unrope(x, cs, sn):
        return x * cs - _swap_halves(x) * sn

    def body(*refs):
        in_refs = refs[:9]
        cos_ref, sin_ref, dq_ref, dkv_ref, tmp_ref = refs[9:]
        cs = cos_ref[...]
        sn = sin_ref[...]
        for g, d in enumerate(DILATIONS):
            for which, blk in enumerate(in_refs[3 * g:3 * g + 3]):
                if d > 1:
                    _tokens_from_classes(blk, tmp_ref, d, tm)
                for c in range(NCHUNK):
                    x = tmp_ref[c] if d > 1 else blk[:, _chunk(c)].astype(F32)
                    if which == 0:
                        dq_ref[:, _chunk(c, g * D)] = (unrope(x, cs, sn) * scale).astype(dq_ref.dtype)
                    elif which == 1:
                        dkv_ref[:, _chunk(c, g * D)] = unrope(x, cs, sn).astype(dkv_ref.dtype)
                    else:
                        dkv_ref[:, _chunk(c, QW + g * D)] = x.astype(dkv_ref.dtype)

    operands = [a for branch in grads for a in branch]
    tab = pl.BlockSpec((tm, 128), lambda i: (i, 0))
    return pl.pallas_call(
        body, name=name, grid=(S // tm,),
        in_specs=[_class_block(d, tm) for d in DILATIONS for _ in range(3)] + [tab, tab],
        out_specs=[pl.BlockSpec((tm, QW), lambda i: (i, 0)), pl.BlockSpec((tm, 2 * QW), lambda i: (i, 0))],
        out_shape=[_sds((S, QW), BF16), _sds((S, 2 * QW), BF16)],
        scratch_shapes=[pltpu.VMEM((NCHUNK, tm, 128), F32)],
        compiler_params=_cparams(1),
    )(*operands, cos_t, sin_t)


def _adamw(name, parts, w, m, v, layer=None, other=None):
    n, rows, cols = parts.shape
    tr = rows
    for cand in (256, 176, 128, 64, 32, 16, 8):
        if rows % cand == 0:
            tr = cand
            break
    n_other = 0 if other is None else len(other)

    def body(p_ref, w_ref, m_ref, v_ref, *refs):
        g_ref, d_ref, nm_ref, nv_ref = refs[n_other:]
        g = p_ref[0].astype(F32)
        for j in range(1, n):
            g = g + p_ref[j].astype(F32)
        g_ref[...] = g
        d_ref[...], nm_ref[...], nv_ref[...] = _adam_update(g, w_ref[...], m_ref[...], v_ref[...])

    if layer is None:
        blk = pl.BlockSpec((tr, cols), lambda i: (i, 0))
        shape = (rows, cols)
    else:
        blk = pl.BlockSpec((None, tr, cols), lambda i: (layer, i, 0))
        shape = w.shape
    return pl.pallas_call(
        body, name=name, grid=(rows // tr,),
        in_specs=[pl.BlockSpec((n, tr, cols), lambda i: (0, i, 0)), blk, blk, blk]
                 + [pl.BlockSpec(memory_space=pl.ANY)] * n_other,
        out_specs=[blk] * 4, out_shape=[_sds(shape, F32)] * 4,
        input_output_aliases={4 + i: i for i in range(n_other)},
        compiler_params=_cparams(1),
    )(parts, w, m, v, *(other or ()))


def _adam_update(g, w, m, v):
    c1 = 1.0 / (1.0 - ADAM_B1 ** ADAM_STEP)
    c2 = 1.0 / (1.0 - ADAM_B2 ** ADAM_STEP)
    nm = ADAM_B1 * m + (1.0 - ADAM_B1) * g
    nv = ADAM_B2 * v + (1.0 - ADAM_B2) * (g * g)
    return -ADAM_LR * ((nm * c1) / (jnp.sqrt(nv * c2) + ADAM_EPS) + ADAM_WD * w), nm, nv


GAIN_ROWS = 16


def _pack_small(name, gain_tiles, taps, sq):
    ng = len(gain_tiles)

    def body(*refs):
        o_ref = refs[-1]
        o_ref[...] = jnp.zeros_like(o_ref)
        for i in range(ng):
            o_ref[i:i + 1, :] = refs[i][0:1, :]
        o_ref[ng:ng + 3, :] = refs[ng][0:3, :]
        o_ref[ng + 3:ng + 4, :] = refs[ng + 1][...]

    return pl.pallas_call(body, name=name, out_shape=_sds((GAIN_ROWS, D), F32))(*gain_tiles, taps, sq)


def _adamw_gains(name, parts, params):
    np_ = len(params)
    shapes = [w.shape for w, _, _ in params]

    def body(p_ref, *refs):
        ins, outs = refs[:3 * np_], refs[3 * np_:]

        def total(lo, rows):
            g = p_ref[0, lo:lo + rows, :]
            for j in range(1, NDEV):
                g = g + p_ref[j, lo:lo + rows, :]
            return g

        lo = 0
        for i, shape in enumerate(shapes):
            g = total(lo, shape[0])
            lo += shape[0]
            w_ref, m_ref, v_ref = ins[3 * i:3 * i + 3]
            g_ref, d_ref, nm_ref, nv_ref = outs[4 * i:4 * i + 4]
            g_ref[...] = g
            d_ref[...], nm_ref[...], nv_ref[...] = _adam_update(g, w_ref[...], m_ref[...], v_ref[...])
        taps_ref, loss_ref = outs[-2], outs[-1]
        taps_ref[...] = jnp.zeros_like(taps_ref)
        taps_ref[0:3, :] = total(lo, 3)
        loss_ref[...] = jnp.sum(total(lo + 3, 1), axis=-1, keepdims=True) * (0.5 / D)

    out_shape = [_sds(shape, F32) for shape in shapes for _ in range(4)] + [_sds((8, D), F32), _sds((1, 1), F32)]
    outs = pl.pallas_call(body, name=name, out_shape=out_shape)(parts, *[a for p in params for a in p])
    return [list(outs[4 * i:4 * i + 4]) for i in range(np_)], outs[-2], outs[-1].reshape(())


def _exchange(name, arrays, kind, after):
    n = len(arrays)
    gather = kind == "gather"
    out_shape = [_sds((NDEV,) + a.shape if gather else a.shape, a.dtype) for a in arrays]

    def body(*refs):
        srcs, outs = refs[:n], refs[n + 1:2 * n + 1]
        send_sems, recv_sems, local_sems = refs[2 * n + 1:]
        x, y, c = lax.axis_index("x"), lax.axis_index("y"), lax.axis_index("c")
        me = 4 * x + 2 * y + c
        pending = []
        for t in range(n):
            own = pltpu.make_async_copy(srcs[t] if gather else srcs[t].at[me], outs[t].at[me], local_sems.at[t])
            own.start()
            pending.append(own)
            for rel in range(1, NDEV):
                px = 1 - x if rel & 4 else x
                py = 1 - y if rel & 2 else y
                pc = 1 - c if rel & 1 else c
                peer = 4 * px + 2 * py + pc
                send = pltpu.make_async_remote_copy(
                    src_ref=srcs[t] if gather else srcs[t].at[peer], dst_ref=outs[t].at[me],
                    send_sem=send_sems.at[t, rel - 1], recv_sem=recv_sems.at[t, rel - 1],
                    device_id=(px, py, pc), device_id_type=MESH)
                send.start()
                arrive = pltpu.make_async_remote_copy(
                    src_ref=srcs[t] if gather else srcs[t].at[me], dst_ref=outs[t].at[peer],
                    send_sem=send_sems.at[t, rel - 1], recv_sem=recv_sems.at[t, rel - 1],
                    device_id=(px, py, pc), device_id_type=MESH)
                pending.append((send, arrive))
        for item in pending:
            if isinstance(item, tuple):
                item[0].wait_send()
                item[1].wait_recv()
            else:
                item.wait()

    any_spec = pl.BlockSpec(memory_space=pl.ANY)
    outs = pl.pallas_call(
        body, name=name,
        in_specs=[any_spec] * (n + 1), out_specs=[any_spec] * n, out_shape=out_shape,
        scratch_shapes=[pltpu.SemaphoreType.DMA((n, NDEV - 1)), pltpu.SemaphoreType.DMA((n, NDEV - 1)),
                        pltpu.SemaphoreType.DMA((n,))],
    )(*arrays, after)
    return list(outs)


_HBM_SPEC = pl.BlockSpec(memory_space=pltpu.HBM)
_SEM_SPEC = pl.BlockSpec(memory_space=pltpu.SEMAPHORE)
_DATAFLOW = pltpu.SideEffectType.DATAFLOW_SIDE_EFFECTING


def _peers():
    x, y, c = lax.axis_index("x"), lax.axis_index("y"), lax.axis_index("c")
    out = []
    for rel in range(1, NDEV):
        px = 1 - x if rel & 4 else x
        py = 1 - y if rel & 2 else y
        pc = 1 - c if rel & 1 else c
        out.append((rel - 1, (px, py, pc), 4 * px + 2 * py + pc))
    return 4 * x + 2 * y + c, out


def _hbm(a):
    return pltpu.HBM(a.shape, a.dtype)


def _own_slot(a, me, kind):
    mine = a[None] if kind == "gather" else lax.dynamic_slice_in_dim(a, me, 1, axis=0)
    shape = (NDEV,) + mine.shape[1:]
    return lax.dynamic_update_slice_in_dim(lax.empty(shape, a.dtype), mine, me, axis=0)


def _exchange_start(name, arrays, me, kind):
    n = len(arrays)
    gather = kind == "gather"
    lands = [_own_slot(a, me, kind) for a in arrays]

    def body(*refs):
        src_refs, land_refs = refs[:n], refs[n:2 * n]
        send_sems, recv_sems = refs[2 * n], refs[2 * n + 1]
        token = refs[-1]
        my_block, peers = _peers()
        for t in range(n):
            for slot, dev, block in peers:
                pltpu.make_async_remote_copy(
                    src_ref=src_refs[t] if gather else src_refs[t].at[block], dst_ref=land_refs[t].at[my_block],
                    send_sem=send_sems.at[t * (NDEV - 1) + slot], recv_sem=recv_sems.at[t * (NDEV - 1) + slot],
                    device_id=dev, device_id_type=MESH).start()
        token[...] = jnp.zeros_like(token)

    operands = [pltpu.with_memory_space_constraint(a, pltpu.HBM) for a in list(arrays) + lands]
    outs = pl.pallas_call(
        body, name=name,
        out_shape=(pltpu.SemaphoreType.DMA((n * (NDEV - 1),)), pltpu.SemaphoreType.DMA((n * (NDEV - 1),)),
                   *[_hbm(a) for a in operands], _sds((8, 128), F32)),
        in_specs=[_HBM_SPEC] * (2 * n),
        out_specs=(_SEM_SPEC, _SEM_SPEC, *[_HBM_SPEC] * (2 * n), pl.BlockSpec(memory_space=pltpu.VMEM)),
        input_output_aliases={i: 2 + i for i in range(2 * n)},
        compiler_params=pltpu.CompilerParams(has_side_effects=_DATAFLOW),
    )(*operands)
    return (outs[0], outs[1], list(outs[2:2 + n]), list(outs[2 + n:2 + 2 * n])), outs[-1]


def _exchange_wait(name, started, t, after, kind):
    send_sems, recv_sems, srcs, lands = started
    gather = kind == "gather"

    def body(src_ref, land_ref, send_ref, recv_ref, after_ref, src_out, land_out):
        _, peers = _peers()
        for slot, dev, block in peers:
            copy = pltpu.make_async_remote_copy(
                src_ref=src_ref if gather else src_ref.at[block], dst_ref=land_ref.at[block],
                send_sem=send_ref.at[t * (NDEV - 1) + slot], recv_sem=recv_ref.at[t * (NDEV - 1) + slot],
                device_id=dev, device_id_type=MESH)
            copy.wait_send()
            copy.wait_recv()

    return pl.pallas_call(
        body, name=name, out_shape=(_hbm(srcs[t]), _hbm(lands[t])),
        in_specs=(_HBM_SPEC, _HBM_SPEC, _SEM_SPEC, _SEM_SPEC, pl.BlockSpec(memory_space=pl.ANY)),
        out_specs=(_HBM_SPEC, _HBM_SPEC), input_output_aliases={0: 0, 1: 1},
        compiler_params=pltpu.CompilerParams(has_side_effects=_DATAFLOW),
    )(srcs[t], lands[t], send_sems, recv_sems, after)[1]


DIRECT_RELS = (1, 2, 4, 6)
RELAY_RELS = (2, 4, 6)


def _rel_peer(rel):
    x, y, c = lax.axis_index("x"), lax.axis_index("y"), lax.axis_index("c")
    px = 1 - x if rel & 4 else x
    py = 1 - y if rel & 2 else y
    pc = 1 - c if rel & 1 else c
    return (px, py, pc), 4 * px + 2 * py + pc


def _gather_start(name, shards, me):
    n, nr = len(shards), len(DIRECT_RELS)
    lands = [_own_slot(a, me, "gather") for a in shards]

    def body(*refs):
        src_refs, land_refs = refs[:n], refs[n:2 * n]
        send_sems, recv_sems = refs[2 * n], refs[2 * n + 1]
        _, my_block = _rel_peer(0)
        for t in range(n):
            for s, rel in enumerate(DIRECT_RELS):
                dev, _ = _rel_peer(rel)
                pltpu.make_async_remote_copy(
                    src_ref=src_refs[t], dst_ref=land_refs[t].at[my_block],
                    send_sem=send_sems.at[t * nr + s], recv_sem=recv_sems.at[t * nr + s],
                    device_id=dev, device_id_type=MESH).start()

    operands = [pltpu.with_memory_space_constraint(a, pltpu.HBM) for a in list(shards) + lands]
    outs = pl.pallas_call(
        body, name=name,
        out_shape=(pltpu.SemaphoreType.DMA((n * nr,)), pltpu.SemaphoreType.DMA((n * nr,)), *[_hbm(a) for a in operands]),
        in_specs=[_HBM_SPEC] * (2 * n), out_specs=(_SEM_SPEC, _SEM_SPEC, *[_HBM_SPEC] * (2 * n)),
        input_output_aliases={i: 2 + i for i in range(2 * n)},
        compiler_params=pltpu.CompilerParams(has_side_effects=_DATAFLOW),
    )(*operands)
    return outs[0], outs[1], list(outs[2:2 + n]), list(outs[2 + n:2 + 2 * n])


def _gather_wait(name, started, ts, after):
    send_sems, recv_sems, srcs, lands = started
    m, nr = len(ts), len(DIRECT_RELS)

    def body(*refs):
        src_refs, land_refs = refs[:m], refs[m:2 * m]
        send_ref, recv_ref = refs[2 * m], refs[2 * m + 1]
        for i, t in enumerate(ts):
            for s, rel in enumerate(DIRECT_RELS):
                dev, block = _rel_peer(rel)
                copy = pltpu.make_async_remote_copy(
                    src_ref=src_refs[i], dst_ref=land_refs[i].at[block],
                    send_sem=send_ref.at[t * nr + s], recv_sem=recv_ref.at[t * nr + s],
                    device_id=dev, device_id_type=MESH)
                copy.wait_send()
                copy.wait_recv()

    operands = [srcs[t] for t in ts] + [lands[t] for t in ts]
    outs = pl.pallas_call(
        body, name=name, out_shape=tuple(_hbm(a) for a in operands),
        in_specs=[_HBM_SPEC] * (2 * m) + [_SEM_SPEC, _SEM_SPEC, pl.BlockSpec(memory_space=pl.ANY)],
        out_specs=tuple([_HBM_SPEC] * (2 * m)), input_output_aliases={i: i for i in range(2 * m)},
        compiler_params=pltpu.CompilerParams(has_side_effects=_DATAFLOW),
    )(*operands, send_sems, recv_sems, after)
    return list(outs[m:])


def _relay_start(name, lands):
    m, nr = len(lands), len(RELAY_RELS)

    def body(*refs):
        land_refs, send_sems, recv_sems = refs[:m], refs[m], refs[m + 1]
        sibling, _ = _rel_peer(1)
        for i in range(m):
            for s, rel in enumerate(RELAY_RELS):
                _, block = _rel_peer(rel)
                pltpu.make_async_remote_copy(
                    src_ref=land_refs[i].at[block], dst_ref=land_refs[i].at[block],
                    send_sem=send_sems.at[i * nr + s], recv_sem=recv_sems.at[i * nr + s],
                    device_id=sibling, device_id_type=MESH).start()

    outs = pl.pallas_call(
        body, name=name,
        out_shape=(pltpu.SemaphoreType.DMA((m * nr,)), pltpu.SemaphoreType.DMA((m * nr,)), *[_hbm(a) for a in lands]),
        in_specs=[_HBM_SPEC] * m, out_specs=(_SEM_SPEC, _SEM_SPEC, *[_HBM_SPEC] * m),
        input_output_aliases={i: 2 + i for i in range(m)},
        compiler_params=pltpu.CompilerParams(has_side_effects=_DATAFLOW),
    )(*lands)
    return outs[0], outs[1], list(outs[2:])


def _relay_wait(name, relayed, after):
    send_sems, recv_sems, lands = relayed
    m, nr = len(lands), len(RELAY_RELS)

    def body(*refs):
        land_refs, send_ref, recv_ref = refs[:m], refs[m], refs[m + 1]
        sibling, _ = _rel_peer(1)
        for i in range(m):
            for s, rel in enumerate(RELAY_RELS):
                _, sent = _rel_peer(rel)
                _, arriving = _rel_peer(rel ^ 1)
                copy = pltpu.make_async_remote_copy(
                    src_ref=land_refs[i].at[sent], dst_ref=land_refs[i].at[arriving],
                    send_sem=send_ref.at[i * nr + s], recv_sem=recv_ref.at[i * nr + s],
                    device_id=sibling, device_id_type=MESH)
                copy.wait_send()
                copy.wait_recv()

    outs = pl.pallas_call(
        body, name=name, out_shape=tuple(_hbm(a) for a in lands),
        in_specs=[_HBM_SPEC] * m + [_SEM_SPEC, _SEM_SPEC, pl.BlockSpec(memory_space=pl.ANY)],
        out_specs=tuple([_HBM_SPEC] * m), input_output_aliases={i: i for i in range(m)},
        compiler_params=pltpu.CompilerParams(has_side_effects=_DATAFLOW),
    )(*lands, send_sems, recv_sems, after)
    return list(outs)


def _ffn_fwd(tag, n, wg, wd):
    gu, act = _gate_up_act(f"ffn_gate_up_{tag}", n, wg)
    wd4 = wd.reshape(NFB, FB, D)
    f = _fwd_kblocked(f"ffn_down_{tag}", act, wd4)
    return (n, gu, act, wg, wd4), f


def _ffn_bwd(tag, dh_out, df, h_in, saved, g_pre, send, mixer):
    n, gu, act, wg, wd4 = saved
    dwd = _bwd_w_kblocked(f"ffn_down_dw_{tag}", act, df).reshape(NDEV, DFF // NDEV, D)
    dgu = _down_dx_act_bwd(f"ffn_down_dx_{tag}", df, wd4, gu).reshape(NDEV, S, FB)
    tok = send({f"down_{tag}": dwd, f"gate_up_{tag}": _bwd_w_cols_blocked(f"ffn_gate_up_dw_{tag}", n, dgu)})
    dn = _bwd_x_cols_blocked(f"ffn_gate_up_dx_{tag}", dgu, wg, after=tok)
    dh_in, (dg_pre,), dy, dg_mixer = _rms_bwd(f"ffn_prenorm_bwd_{tag}", h_in, [(g_pre, dn)], dh_out, F32, then=mixer)
    return dh_in, dg_pre, dy, dg_mixer


def kernel(x, positions, mix_norm_pre, mix_norm_post, ffn_norm_pre, ffn_norm_post, ffn_w_gate_up, ffn_w_down, conv_w_in, conv_w, conv_w_out, kv_norm, w_kv, w_q, w_o, loss_target, m_mix_norm_pre, m_mix_norm_post, m_ffn_norm_pre, m_ffn_norm_post, m_ffn_w_gate_up, m_ffn_w_down, m_conv_w_in, m_conv_w, m_conv_w_out, m_kv_norm, m_w_kv, m_w_q, m_w_o, v_mix_norm_pre, v_mix_norm_post, v_ffn_norm_pre, v_ffn_norm_post, v_ffn_w_gate_up, v_ffn_w_down, v_conv_w_in, v_conv_w, v_conv_w_out, v_kv_norm, v_w_kv, v_w_q, v_w_o):
    me = 4 * lax.axis_index("x") + 2 * lax.axis_index("y") + lax.axis_index("c")
    h0 = x.reshape(S, D)
    target = loss_target.reshape(S, D)
    row = lambda a, l: a[l].reshape(1, D)
    g_kv = kv_norm.reshape(1, D)

    cw_shard = jnp.pad(conv_w[0], ((0, 5), (0, 0)))
    names = ["conv_in", "conv_w", "conv_out", "gate_up_0", "down_0", "kv", "q", "o", "gate_up_1", "down_1"]
    shards = [conv_w_in[0], cw_shard, conv_w_out[0], ffn_w_gate_up[0], ffn_w_down[0],
              w_kv, w_q[0], w_o[0], ffn_w_gate_up[1], ffn_w_down[1]]
    shards = [s if n == "conv_w" else s.astype(BF16) for n, s in zip(names, shards)]
    first = 3
    gather_first = _gather_start("gather_start_conv", shards[:first], me)
    gather_rest = _gather_start("gather_start_rest", shards[first:], me)

    def direct(group, after):
        ts = [names.index(n) for n in group]
        started, ts = (gather_first, ts) if ts[0] < first else (gather_rest, [t - first for t in ts])
        lands = _gather_wait(f"gather_wait_{group[0]}", started, ts, after)
        return _relay_start(f"relay_start_{group[0]}", lands)

    def finish(group, relayed, after):
        return dict(zip(group, _relay_wait(f"relay_wait_{group[0]}", relayed, after)))

    sent = {}

    def send(grads):
        started, token = _exchange_start(f"scatter_start_{next(iter(grads))}", list(grads.values()), me, "scatter")
        for i, name in enumerate(grads):
            sent[name] = (started, i)
        return token

    groups = [["conv_in", "conv_w", "conv_out"], ["gate_up_0", "down_0"], ["kv", "q"], ["o", "gate_up_1", "down_1"]]
    n0 = _rms_fwd("mix_prenorm_0", h0, [row(mix_norm_pre, 0)])[0]
    half = HEAD_DIM // 2
    inv_freq = ROPE_THETA ** (-jnp.arange(half, dtype=F32) / half)
    tables = _rope_tables("rope_tables", positions.reshape(S, 1), jnp.tile(inv_freq, 4).reshape(1, 128))
    w = finish(groups[0], direct(groups[0], tables[0]), n0)
    win = w["conv_in"].transpose(1, 0, 2).reshape(D, 3 * D)
    cw = w["conv_w"].transpose(1, 0, 2).reshape(8, D)
    wout = w["conv_out"].reshape(D, D)
    z = _fwd_rows("conv_in", n0, win, BF16)
    pre = _conv_fwd("conv_gate", z, cw)
    relayed = direct(groups[1], pre)
    y0 = _fwd_rows("conv_out", pre, wout)
    h1, (n1,) = _resid_rms("mix_postnorm_0", h0, y0, row(mix_norm_post, 0), [row(ffn_norm_pre, 0)])
    w = finish(groups[1], relayed, n1)
    ffn0, f0 = _ffn_fwd("0", n1, w["gate_up_0"], w["down_0"])
    relayed = direct(groups[2], ffn0[2])
    h2, (nk, n2) = _resid_rms("ffn_postnorm_0", h1, f0, row(ffn_norm_post, 0), [g_kv, row(mix_norm_pre, 1)])

    w = finish(groups[2], relayed, nk)
    wkv = w["kv"].transpose(1, 0, 2).reshape(D, 2 * QW)
    wq = w["q"].transpose(1, 0, 2).reshape(D, QW)
    qc, kc, vc, o_c, lse_c = [], [], [], [], []
    for g, d in enumerate(DILATIONS):
        q_g, k_g, v_g = _qkv_classes(f"qkv_proj_{g}", n2, nk, wq, wkv, g, d, tables)
        qc.append(q_g)
        kc.append(k_g)
        vc.append(v_g)
    relayed = direct(groups[3], vc[-1])
    for g, d in enumerate(DILATIONS):
        o_g, lse_g = _attn_fwd(f"attn_fwd_{g}", qc[g], kc[g], vc[g], d)
        o_c.append(o_g)
        lse_c.append(lse_g)
    o_mix = _mix_fwd("attn_mix", o_c, lse_c)
    w = finish(groups[3], relayed, o_mix)
    wo = w["o"].reshape(D, D)
    y1 = _fwd_rows("attn_out", o_mix, wo)
    h3, (n3,) = _resid_rms("mix_postnorm_1", h2, y1, row(mix_norm_post, 1), [row(ffn_norm_pre, 1)])
    ffn1, f1 = _ffn_fwd("1", n3, w["gate_up_1"], w["down_1"])

    dh4, df1, dg_fpost1, sq = _resid_rms_loss("ffn_postnorm_1_loss", h3, f1, row(ffn_norm_post, 1), target)

    dh3, dg_fpre1, dy1, dg_mpost1 = _ffn_bwd(
        "1", dh4, df1, h3, ffn1, row(ffn_norm_pre, 1), send, (y1, row(mix_norm_post, 1)))
    dwo = _bwd_w_rows("attn_out_dw", o_mix, dy1).reshape(NDEV, D // NDEV, D)
    do = _bwd_x_rows("attn_out_dx", dy1, wo, F32)
    lane = jnp.arange(128)
    ones_blockdiag = (lane[:, None] // HEAD_DIM == lane[None, :] // HEAD_DIM).astype(BF16)
    mixed = _mix_bwd("attn_mix_bwd", do, o_c, lse_c, ones_blockdiag)
    branch_grads = [_attn_bwd(f"attn_bwd_{g}", qc[g], kc[g], vc[g], mixed[g], lse_c[g], mixed[3 + g], d)
                    for g, d in enumerate(DILATIONS)]
    dq_raw, dkv = _attn_bwd_post("attn_bwd_post", branch_grads, *tables)
    tok = send({"o": dwo, "kv": _bwd_w_cols("kv_proj_dw", nk, dkv, 2 * QW // NDEV),
                "q": _bwd_w_cols("q_proj_dw", n2, dq_raw, QW // NDEV)})
    dnk = _bwd_x_plain("kv_proj_dx", dkv, wkv, after=tok)
    dn2 = _bwd_x_plain("q_proj_dx", dq_raw, wq)
    dh2, (dg_kv, dg_mpre1), df0, dg_fpost0 = _rms_bwd(
        "kv_and_mix_prenorm_bwd_1", h2, [(g_kv, dnk), (row(mix_norm_pre, 1), dn2)], dh3, F32,
        then=(f0, row(ffn_norm_post, 0)))

    dh1, dg_fpre0, dy0, dg_mpost0 = _ffn_bwd(
        "0", dh2, df0, h1, ffn0, row(ffn_norm_pre, 0), send, (y0, row(mix_norm_post, 0)))
    dwout = _bwd_w_rows("conv_out_dw", pre, dy0).reshape(NDEV, D // NDEV, D)
    dpre = _bwd_x_rows("conv_out_dx", dy0, wout, BF16)
    dz, dcw = _conv_bwd("conv_gate_bwd", z, dpre, cw)
    tok = send({"conv_out": dwout, "conv_in": _bwd_w_cols("conv_in_dw", n0, dz, 3 * D // NDEV)})
    dn0 = _bwd_x_plain("conv_in_dx", dz, win, after=tok)
    dh0, (dg_mpre0,) = _rms_bwd("mix_prenorm_bwd_0", h0, [(row(mix_norm_pre, 0), dn0)], dh1, F32)

    small = _pack_small("pack_small_grads", [dg_mpre0, dg_mpre1, dg_mpost0, dg_mpost1, dg_fpre0, dg_fpre1,
                                             dg_fpost0, dg_fpost1, dg_kv], dcw, sq)

    done = [small]

    def upd(tag, w, m, v):
        parts = _exchange_wait(f"scatter_wait_{tag}", *sent[tag], done[-1], "scatter")
        shape = w.shape
        flat = lambda a: a.reshape(parts.shape[1:])
        res = _adamw(f"adamw_{tag}", parts, flat(w), flat(m), flat(v))
        done.append(res[0])
        return [r.reshape(shape) for r in res]

    def upd_layer(tag, l, w, m, v, other):
        parts = _exchange_wait(f"scatter_wait_{tag}_{l}", *sent[f"{tag}_{l}"], done[-1], "scatter")
        res = _adamw(f"adamw_{tag}_{l}", parts, w, m, v, layer=l, other=other)
        done.append(res[0])
        return list(res)

    res = {}
    down_1 = upd_layer("down", 1, ffn_w_down, m_ffn_w_down, v_ffn_w_down, None)
    gate_up_t = [jnp.swapaxes(a, 1, 2) for a in (ffn_w_gate_up, m_ffn_w_gate_up, v_ffn_w_gate_up)]
    gate_up_1 = upd_layer("gate_up", 1, *gate_up_t, None)
    res["w_o"] = upd("o", w_o, m_w_o, v_w_o)
    res["w_q"] = upd("q", w_q, m_w_q, v_w_q)
    res["w_kv"] = upd("kv", w_kv, m_w_kv, v_w_kv)

    small_all = _exchange("gather_small_grads", [small], "gather", done[-1])[0]
    vec = lambda a: a.reshape(1, D)
    gain_res, taps, loss = _adamw_gains("adamw_gains", small_all, [
        (mix_norm_pre, m_mix_norm_pre, v_mix_norm_pre), (mix_norm_post, m_mix_norm_post, v_mix_norm_post),
        (ffn_norm_pre, m_ffn_norm_pre, v_ffn_norm_pre), (ffn_norm_post, m_ffn_norm_post, v_ffn_norm_post),
        (vec(kv_norm), vec(m_kv_norm), vec(v_kv_norm))])
    dcw_mine = lax.dynamic_slice(taps, (0, me * 128), (8, 128))
    pad8 = lambda a, fill: jnp.pad(a[0], ((0, 5), (0, 0)), constant_values=fill)
    cw_res = [r[0:3].reshape(1, 3, 128) for r in
              _adamw("adamw_conv_w", dcw_mine.reshape(1, 8, 128), cw_shard, pad8(m_conv_w, 0.0), pad8(v_conv_w, 1.0))]

    res.update({
        "mix_norm_pre": gain_res[0],
        "mix_norm_post": gain_res[1],
        "ffn_norm_pre": gain_res[2],
        "ffn_norm_post": gain_res[3],
        "kv_norm": [r.reshape(D) for r in gain_res[4]],
        "conv_w": cw_res,
    })
    done.append(small_all)
    res["ffn_w_down"] = upd_layer("down", 0, ffn_w_down, m_ffn_w_down, v_ffn_w_down, down_1)
    res["ffn_w_gate_up"] = [jnp.swapaxes(r, 1, 2) for r in upd_layer("gate_up", 0, *gate_up_t, gate_up_1)]
    res["conv_w_out"] = upd("conv_out", conv_w_out, m_conv_w_out, v_conv_w_out)
    res["conv_w_in"] = upd("conv_in", conv_w_in, m_conv_w_in, v_conv_w_in)
    order = ["mix_norm_pre", "mix_norm_post", "ffn_norm_pre", "ffn_norm_post", "ffn_w_gate_up", "ffn_w_down",
             "conv_w_in", "conv_w", "conv_w_out", "kv_norm", "w_kv", "w_q", "w_o"]
    out = [loss, dh0.reshape(1, S, D)]
    for i in range(4):
        out += [res[name][i] for name in order]
    return tuple(out)
```

```python
import jax
import jax.numpy as jnp
from jax import lax
from jax.experimental import pallas as pl
from jax.experimental.pallas import tpu as pltpu

F32 = jnp.float32
BF16 = jnp.bfloat16

S = 4096
D = 1024
NDEV = 8
HEAD_DIM = 64
QW = 3072
DFF = 2816
FB = 704
NFB = 4
BRANCHES = ((128, 1), (512, 4), (2048, 16))
BAND = 128
ROPE_THETA = 10000.0
RMS_EPS = 1e-6
NEG_INF = -1e30
ADAM_LR, ADAM_B1, ADAM_B2, ADAM_EPS, ADAM_WD, ADAM_STEP = 0.001, 0.9, 0.999, 1e-08, 0.01, 10

VMEM_LIMIT_BYTES = 52 * 1024 * 1024
ROW_TILE = 512
MESH = pl.DeviceIdType.MESH


def _cparams(ngrid):
    return pltpu.CompilerParams(dimension_semantics=("arbitrary",) * ngrid,
                                vmem_limit_bytes=VMEM_LIMIT_BYTES)


def _sds(shape, dtype):
    return jax.ShapeDtypeStruct(tuple(shape), dtype)


_DIMS = {"nn": (((1,), (0,)), ((), ())),
         "nt": (((1,), (1,)), ((), ())),
         "tn": (((0,), (0,)), ((), ()))}


def _matmul(name, a, b, *, mode, grid, a_blk, a_map, b_blk, b_map, o_shape, o_blk, o_map, out_dtype, after=None,
            out_groups=1):
    nk = grid[2]
    dims = _DIMS[mode]
    acc_shape = tuple(s for s in o_blk if s is not None)
    if out_groups > 1:
        acc_shape = (acc_shape[1], out_groups * acc_shape[2])
    extra = [] if after is None else [after]

    def store(o_ref, val):
        if out_groups == 1:
            o_ref[...] = val.astype(o_ref.dtype)
        else:
            n = o_ref.shape[-1]
            for grp in range(out_groups):
                o_ref[grp] = val[:, grp * n:(grp + 1) * n].astype(o_ref.dtype)

    def body(a_ref, b_ref, *rest):
        o_ref, scratch = rest[len(extra)], rest[len(extra) + 1:]
        part = lax.dot_general(a_ref[...], b_ref[...], dims, preferred_element_type=F32)
        if nk == 1:
            store(o_ref, part)
            return
        acc_ref = scratch[0]
        k = pl.program_id(2)

        @pl.when(k == 0)
        def _():
            acc_ref[...] = part

        @pl.when(k > 0)
        def _():
            acc_ref[...] += part

        @pl.when(k == nk - 1)
        def _():
            store(o_ref, acc_ref[...])

    return pl.pallas_call(
        body, name=name, grid=grid,
        in_specs=[pl.BlockSpec(a_blk, a_map), pl.BlockSpec(b_blk, b_map)] + [pl.BlockSpec(memory_space=pl.ANY)] * len(extra),
        out_specs=pl.BlockSpec(o_blk, o_map),
        out_shape=_sds(o_shape, out_dtype),
        scratch_shapes=[] if nk == 1 else [pltpu.VMEM(acc_shape, F32)],
        compiler_params=_cparams(3),
    )(a, b, *extra)


TM = 1024
TK = S


def _fwd_rows(name, a, w, out_dtype=F32):
    kdim, n = w.shape
    tn = 512
    return _matmul(name, a, w, mode="nn", grid=(S // TM, n // tn, 1),
                   a_blk=(TM, kdim), a_map=lambda i, j, k: (i, 0),
                   b_blk=(kdim, tn), b_map=lambda i, j, k: (0, j),
                   o_shape=(S, n), o_blk=(TM, tn), o_map=lambda i, j, k: (i, j), out_dtype=out_dtype)


def _fwd_kblocked(name, a4, w4):
    nb, _, kb = a4.shape
    n = w4.shape[2]

    def body(a_ref, w_ref, o_ref):
        acc = _dot_nn(a_ref[0], w_ref[0])
        for j in range(1, nb):
            acc = acc + _dot_nn(a_ref[j], w_ref[j])
        o_ref[...] = acc

    return pl.pallas_call(
        body, name=name, grid=(S // TM,),
        in_specs=[pl.BlockSpec((nb, TM, kb), lambda i: (0, i, 0)), pl.BlockSpec((nb, kb, n), lambda i: (0, 0, 0))],
        out_specs=pl.BlockSpec((TM, n), lambda i: (i, 0)), out_shape=_sds((S, n), F32),
        compiler_params=_cparams(1),
    )(a4, w4)


def _bwd_x_cols_blocked(name, dy8, wg, after):
    _, kdim, n = wg.shape
    nk = NDEV // 2

    def body(a_ref, b_ref, after_ref, o_ref, acc_ref):
        k = pl.program_id(1)
        part = _dot_nt(a_ref[0], b_ref[0]) + _dot_nt(a_ref[1], b_ref[1])

        @pl.when(k == 0)
        def _():
            acc_ref[...] = part

        @pl.when(k > 0)
        def _():
            acc_ref[...] += part

        @pl.when(k == nk - 1)
        def _():
            o_ref[...] = acc_ref[...]

    return pl.pallas_call(
        body, name=name, grid=(S // TM, nk),
        in_specs=[pl.BlockSpec((2, None, TM, n), lambda i, k: (0, k, i, 0)),
                  pl.BlockSpec((2, None, kdim, n), lambda i, k: (0, k, 0, 0)),
                  pl.BlockSpec(memory_space=pl.ANY)],
        out_specs=pl.BlockSpec((TM, kdim), lambda i, k: (i, 0)), out_shape=_sds((S, kdim), F32),
        scratch_shapes=[pltpu.VMEM((TM, kdim), F32)],
        compiler_params=_cparams(2),
    )(dy8.reshape(2, nk, S, n), wg.reshape(2, nk, kdim, n), after)


def _bwd_x_rows(name, dy, w, out_dtype, after=None):
    kdim, n = w.shape
    tkk = 512
    return _matmul(name, dy, w, mode="nt", grid=(S // TM, kdim // tkk, 1),
                   a_blk=(TM, n), a_map=lambda i, j, k: (i, 0),
                   b_blk=(tkk, n), b_map=lambda i, j, k: (j, 0),
                   o_shape=(S, kdim), o_blk=(TM, tkk), o_map=lambda i, j, k: (i, j), out_dtype=out_dtype, after=after)


DW_COLS = 768


def _bwd_w_cols(name, a, dy, n):
    kdim = a.shape[1]
    groups = DW_COLS // n
    return _matmul(name, a, dy, mode="tn", grid=(1, NDEV // groups, S // TK),
                   a_blk=(TK, kdim), a_map=lambda i, j, k: (k, 0),
                   b_blk=(TK, DW_COLS), b_map=lambda i, j, k: (k, j),
                   o_shape=(NDEV, kdim, n), o_blk=(groups, kdim, n) if groups > 1 else (None, kdim, n),
                   o_map=lambda i, j, k: (j, 0, 0), out_dtype=BF16, out_groups=groups)


def _bwd_x_plain(name, dy, w, after=None):
    kdim, n = w.shape
    tm = TM if n <= 3 * D else TM // 2
    return _matmul(name, dy, w, mode="nt", grid=(S // tm, 1, 1),
                   a_blk=(tm, n), a_map=lambda i, j, k: (i, 0),
                   b_blk=(kdim, n), b_map=lambda i, j, k: (0, 0),
                   o_shape=(S, kdim), o_blk=(tm, kdim), o_map=lambda i, j, k: (i, 0), out_dtype=F32, after=after)


def _bwd_w_cols_blocked(name, a, dy8):
    kdim = a.shape[1]
    n = dy8.shape[2]
    return _matmul(name, dy8, a, mode="tn", grid=(1, NDEV, S // TK),
                   a_blk=(None, TK, n), a_map=lambda i, j, k: (j, k, 0),
                   b_blk=(TK, kdim), b_map=lambda i, j, k: (k, 0),
                   o_shape=(NDEV, n, kdim), o_blk=(None, n, kdim), o_map=lambda i, j, k: (j, 0, 0), out_dtype=BF16)


def _bwd_w_rows(name, a, dy):
    kdim = a.shape[1]
    n = dy.shape[1]
    tmm = 512
    return _matmul(name, a, dy, mode="tn", grid=(kdim // tmm, 1, S // TK),
                   a_blk=(TK, tmm), a_map=lambda i, j, k: (k, i),
                   b_blk=(TK, n), b_map=lambda i, j, k: (k, 0),
                   o_shape=(kdim, n), o_blk=(tmm, n), o_map=lambda i, j, k: (i, 0), out_dtype=BF16)


def _bwd_w_kblocked(name, a4, dy):
    nb, _, kb = a4.shape
    n = dy.shape[1]
    return _matmul(name, a4, dy, mode="tn", grid=(nb, 1, S // TK),
                   a_blk=(None, TK, kb), a_map=lambda i, j, k: (i, k, 0),
                   b_blk=(TK, n), b_map=lambda i, j, k: (k, 0),
                   o_shape=(nb, kb, n), o_blk=(None, kb, n), o_map=lambda i, j, k: (i, 0, 0), out_dtype=BF16)


def _rstd(x):
    return lax.rsqrt(jnp.mean(x * x, axis=-1, keepdims=True) + RMS_EPS)


def _row_spec(tm=ROW_TILE, width=D):
    return pl.BlockSpec((tm, width), lambda i: (i, 0))


def _vec_spec(rows=1, width=D):
    return pl.BlockSpec((rows, width), lambda i: (0, 0))


def _rms_fwd(name, x, gains):
    n = len(gains)

    def body(x_ref, *refs):
        x_val = x_ref[...]
        xh = x_val * _rstd(x_val)
        for g_ref, o_ref in zip(refs[:n], refs[n:]):
            o_ref[...] = (xh * g_ref[...]).astype(o_ref.dtype)

    outs = pl.pallas_call(
        body, name=name, grid=(S // ROW_TILE,),
        in_specs=[_row_spec()] + [_vec_spec()] * n,
        out_specs=[_row_spec()] * n,
        out_shape=[_sds((S, D), BF16)] * n,
        compiler_params=_cparams(1),
    )(x, *gains)
    return list(outs)


def _resid_rms(name, h, y, g, next_gains):
    n = len(next_gains)

    def body(h_ref, y_ref, g_ref, *refs):
        y_val = y_ref[...]
        h_new = h_ref[...] + (y_val * _rstd(y_val)) * g_ref[...]
        refs[n][...] = h_new
        hh = h_new * _rstd(h_new)
        for g2_ref, o_ref in zip(refs[:n], refs[n + 1:]):
            o_ref[...] = (hh * g2_ref[...]).astype(o_ref.dtype)

    outs = pl.pallas_call(
        body, name=name, grid=(S // ROW_TILE,),
        in_specs=[_row_spec(), _row_spec(), _vec_spec()] + [_vec_spec()] * n,
        out_specs=[_row_spec()] * (n + 1), out_shape=[_sds((S, D), F32)] + [_sds((S, D), BF16)] * n,
        compiler_params=_cparams(1),
    )(h, y, g, *next_gains)
    return outs[0], list(outs[1:])


def _resid_rms_loss(name, h, y, g, target):
    def body(h_ref, y_ref, g_ref, t_ref, dh_ref, dy_ref, dg_ref, part_ref):
        y_val = y_ref[...]
        gain = g_ref[...]
        e = h_ref[...] + (y_val * _rstd(y_val)) * gain - t_ref[...]
        dh = e * (1.0 / D)
        dh_ref[...] = dh
        step = pl.program_id(0)
        dy_ref[...] = _norm_bwd_rows(y_val, gain, dh, dg_ref, step).astype(dy_ref.dtype)
        part = jnp.sum(e * e, axis=0, keepdims=True)

        @pl.when(step == 0)
        def _():
            part_ref[...] = part

        @pl.when(step > 0)
        def _():
            part_ref[...] += part

    return pl.pallas_call(
        body, name=name, grid=(S // ROW_TILE,),
        in_specs=[_row_spec(), _row_spec(), _vec_spec(), _row_spec()],
        out_specs=[_row_spec(), _row_spec(), _vec_spec(8), _vec_spec()],
        out_shape=[_sds((S, D), F32), _sds((S, D), BF16), _sds((8, D), F32), _sds((1, D), F32)],
        compiler_params=_cparams(1),
    )(h, y, g, target)


def _norm_bwd_rows(x_val, g, dn, dg_ref, step):
    r = _rstd(x_val)
    xh = x_val * r
    dxh = dn * g
    part = jnp.sum(dn * xh, axis=0, keepdims=True)

    @pl.when(step == 0)
    def _():
        dg_ref[...] = jnp.zeros_like(dg_ref)

    dg_ref[0:1, :] += part
    return r * (dxh - xh * jnp.mean(dxh * xh, axis=-1, keepdims=True))


def _rms_bwd(name, x, pairs, dres, out_dtype, then=None):
    n = len(pairs)
    has_res = dres is not None
    chained = then is not None

    def body(x_ref, *refs):
        g_refs = refs[0:2 * n:2]
        dn_refs = refs[1:2 * n:2]
        pos = 2 * n
        res_ref = refs[pos] if has_res else None
        pos += int(has_res)
        if chained:
            y_ref, gy_ref = refs[pos], refs[pos + 1]
            pos += 2
        dx_ref = refs[pos]
        dg_refs = refs[pos + 1:pos + 1 + n]
        step = pl.program_id(0)
        x_val = x_ref[...]
        acc = res_ref[...] if has_res else jnp.zeros_like(x_val)
        for g_ref, dn_ref, dg_ref in zip(g_refs, dn_refs, dg_refs):
            acc = acc + _norm_bwd_rows(x_val, g_ref[...], dn_ref[...].astype(F32), dg_ref, step)
        dx_ref[...] = acc.astype(dx_ref.dtype)
        if chained:
            dy_ref, dgy_ref = refs[pos + 1 + n], refs[pos + 2 + n]
            dy_ref[...] = _norm_bwd_rows(y_ref[...], gy_ref[...], acc, dgy_ref, step).astype(dy_ref.dtype)

    operands = [x]
    in_specs = [_row_spec()]
    for g, dn in pairs:
        operands += [g, dn]
        in_specs += [_vec_spec(), _row_spec()]
    if has_res:
        operands.append(dres)
        in_specs.append(_row_spec())
    if chained:
        operands += [then[0], then[1]]
        in_specs += [_row_spec(), _vec_spec()]
    extra = int(chained)
    outs = pl.pallas_call(
        body, name=name, grid=(S // ROW_TILE,),
        in_specs=in_specs,
        out_specs=[_row_spec()] + [_vec_spec(8)] * n + [_row_spec(), _vec_spec(8)] * extra,
        out_shape=[_sds((S, D), out_dtype)] + [_sds((8, D), F32)] * n + [_sds((S, D), BF16), _sds((8, D), F32)] * extra,
        compiler_params=_cparams(1),
    )(*operands)
    if chained:
        return outs[0], list(outs[1:1 + n]), outs[1 + n], outs[2 + n]
    return outs[0], list(outs[1:])


def _shift_down(u, prev8, k):
    r = pltpu.roll(u, k, 0)
    p = pltpu.roll(prev8, k, 0)
    row = lax.broadcasted_iota(jnp.int32, prev8.shape, 0)
    top = jnp.where(row < k, p, r[0:8])
    return jnp.concatenate([top, r[8:]], axis=0)


def _shift_up(u, next8, k):
    tm = u.shape[0]
    r = pltpu.roll(u, tm - k, 0)
    p = pltpu.roll(next8, 8 - k, 0)
    row = lax.broadcasted_iota(jnp.int32, next8.shape, 0)
    bot = jnp.where(row >= 8 - k, p, r[tm - 8:tm])
    return jnp.concatenate([r[:tm - 8], bot], axis=0)


CONV_TILE = 512


def _halo_prev(col):
    return pl.BlockSpec((8, D), lambda i: (jnp.maximum(i * (CONV_TILE // 8) - 1, 0), col))


def _halo_next(col):
    last = S // 8 - 1
    return pl.BlockSpec((8, D), lambda i: (jnp.minimum((i + 1) * (CONV_TILE // 8), last), col))


def _conv_fwd(name, z, cw):
    def body(b_ref, c_ref, h_ref, cp_ref, hp_ref, cw_ref, o_ref):
        i = pl.program_id(0)
        u = c_ref[...].astype(F32) * h_ref[...].astype(F32)
        up = cp_ref[...].astype(F32) * hp_ref[...].astype(F32)
        up = jnp.where(i > 0, up, 0.0)
        cv = cw_ref[0:1, :] * _shift_down(u, up, 2) + cw_ref[1:2, :] * _shift_down(u, up, 1) + cw_ref[2:3, :] * u
        o_ref[...] = (b_ref[...].astype(F32) * cv).astype(o_ref.dtype)

    col = lambda c: pl.BlockSpec((CONV_TILE, D), lambda i: (i, c))
    return pl.pallas_call(
        body, name=name, grid=(S // CONV_TILE,),
        in_specs=[col(0), col(1), col(2), _halo_prev(1), _halo_prev(2), _vec_spec(8)],
        out_specs=_row_spec(CONV_TILE), out_shape=_sds((S, D), BF16),
        compiler_params=_cparams(1),
    )(z, z, z, z, z, cw)


def _conv_bwd(name, z, dpre, cw):
    nsteps = S // CONV_TILE

    def body(b_ref, c_ref, h_ref, cp_ref, hp_ref, dp_ref, dpn_ref, bn_ref, cw_ref, dz_ref, dcw_ref):
        i = pl.program_id(0)
        b = b_ref[...].astype(F32)
        c = c_ref[...].astype(F32)
        h = h_ref[...].astype(F32)
        dp = dp_ref[...].astype(F32)
        u = c * h
        up = jnp.where(i > 0, cp_ref[...].astype(F32) * hp_ref[...].astype(F32), 0.0)
        s1 = _shift_down(u, up, 1)
        s2 = _shift_down(u, up, 2)
        w0, w1, w2 = cw_ref[0:1, :], cw_ref[1:2, :], cw_ref[2:3, :]
        cv = w0 * s2 + w1 * s1 + w2 * u
        dcv = dp * b
        dcvn = jnp.where(i < nsteps - 1, dpn_ref[...].astype(F32) * bn_ref[...].astype(F32), 0.0)
        du = w2 * dcv + w1 * _shift_up(dcv, dcvn, 1) + w0 * _shift_up(dcv, dcvn, 2)
        dz_ref[:, 0:D] = (dp * cv).astype(dz_ref.dtype)
        dz_ref[:, D:2 * D] = (du * h).astype(dz_ref.dtype)
        dz_ref[:, 2 * D:3 * D] = (du * c).astype(dz_ref.dtype)

        @pl.when(i == 0)
        def _():
            dcw_ref[...] = jnp.zeros_like(dcw_ref)

        dcw_ref[0:1, :] += jnp.sum(dcv * s2, axis=0, keepdims=True)
        dcw_ref[1:2, :] += jnp.sum(dcv * s1, axis=0, keepdims=True)
        dcw_ref[2:3, :] += jnp.sum(dcv * u, axis=0, keepdims=True)

    col = lambda c: pl.BlockSpec((CONV_TILE, D), lambda i: (i, c))
    return pl.pallas_call(
        body, name=name, grid=(nsteps,),
        in_specs=[col(0), col(1), col(2), _halo_prev(1), _halo_prev(2),
                  _row_spec(CONV_TILE), _halo_next(0), _halo_next(0), _vec_spec(8)],
        out_specs=[pl.BlockSpec((CONV_TILE, 3 * D), lambda i: (i, 0)), _vec_spec(8)],
        out_shape=[_sds((S, 3 * D), BF16), _sds((8, D), F32)],
        compiler_params=_cparams(1),
    )(z, z, z, z, z, dpre, dpre, z, cw)


FFN_TM = 2048
_GU_BLOCK = pl.BlockSpec((2, None, FFN_TM, FB), lambda i, j: (0, j, i, 0))


def _gate_up_act(name, a, wg):
    kdim = a.shape[1]

    def body(a_ref, wgate_ref, wup_ref, gu_ref, act_ref):
        x = a_ref[...]
        g = _dot_nn(x, wgate_ref[...])
        u = _dot_nn(x, wup_ref[...])
        gu_ref[0] = g.astype(gu_ref.dtype)
        gu_ref[1] = u.astype(gu_ref.dtype)
        act_ref[...] = (g * jax.nn.sigmoid(g) * u).astype(act_ref.dtype)

    return pl.pallas_call(
        body, name=name, grid=(S // FFN_TM, NFB),
        in_specs=[pl.BlockSpec((FFN_TM, kdim), lambda i, j: (i, 0)),
                  pl.BlockSpec((None, kdim, FB), lambda i, j: (j, 0, 0)),
                  pl.BlockSpec((None, kdim, FB), lambda i, j: (j + NFB, 0, 0))],
        out_specs=[_GU_BLOCK, pl.BlockSpec((None, FFN_TM, FB), lambda i, j: (j, i, 0))],
        out_shape=[_sds((2, NFB, S, FB), BF16), _sds((NFB, S, FB), BF16)],
        compiler_params=_cparams(2),
    )(a, wg, wg)


def _down_dx_act_bwd(name, df, w4, gu):
    _, kb, n = w4.shape

    def body(df_ref, w_ref, gu_ref, o_ref):
        d = _dot_nt(df_ref[...], w_ref[...])
        g = gu_ref[0].astype(F32)
        u = gu_ref[1].astype(F32)
        sg = jax.nn.sigmoid(g)
        o_ref[0] = (d * u * sg * (1.0 + g * (1.0 - sg))).astype(o_ref.dtype)
        o_ref[1] = (d * g * sg).astype(o_ref.dtype)

    return pl.pallas_call(
        body, name=name, grid=(S // FFN_TM, NFB),
        in_specs=[pl.BlockSpec((FFN_TM, n), lambda i, j: (i, 0)), pl.BlockSpec((None, kb, n), lambda i, j: (j, 0, 0)),
                  _GU_BLOCK],
        out_specs=_GU_BLOCK, out_shape=_sds((2, NFB, S, FB), BF16),
        compiler_params=_cparams(2),
    )(df, w4, gu)


def _rope_tables(name, pos_col, inv_freq_row):
    def body(pos_ref, f_ref, cos_ref, sin_ref):
        ang = pos_ref[...].astype(F32) * f_ref[...]
        lane = lax.broadcasted_iota(jnp.int32, ang.shape, 1)
        s = jnp.sin(ang)
        cos_ref[...] = jnp.cos(ang)
        sin_ref[...] = jnp.where((lane % HEAD_DIM) < HEAD_DIM // 2, -s, s)

    tab = pl.BlockSpec((ROW_TILE, 128), lambda i: (i, 0))
    return pl.pallas_call(
        body, name=name, grid=(S // ROW_TILE,),
        in_specs=[pl.BlockSpec((ROW_TILE, 1), lambda i: (i, 0)), _vec_spec(1, 128)],
        out_specs=[tab, tab], out_shape=[_sds((S, 128), F32)] * 2,
        compiler_params=_cparams(1),
    )(pos_col, inv_freq_row)


def _swap_halves(t):
    lane = lax.broadcasted_iota(jnp.int32, t.shape, 1)
    first = (lane % HEAD_DIM) < HEAD_DIM // 2
    return jnp.where(first, pltpu.roll(t, 128 - HEAD_DIM // 2, 1), pltpu.roll(t, HEAD_DIM // 2, 1))


NCHUNK = D // 128


def _chunk(c, base=0):
    return slice(base + c * 128, base + (c + 1) * 128)


def _class_rows(r, d, tm):
    return pl.ds(r, tm // d, stride=d) if d > 1 else slice(None)


def _class_block(d, tm):
    return pl.BlockSpec((tm // d, d * D), lambda i: (i, 0))


def _tokens_from_classes(blk_ref, tmp_ref, d, tm):
    for r in range(d):
        for c in range(NCHUNK):
            tmp_ref[c, _class_rows(r, d, tm), :] = blk_ref[:, _chunk(c, r * D)].astype(F32)


def _classes_from_tokens(tmp_ref, blk_ref, d, tm):
    for r in range(d):
        for c in range(NCHUNK):
            blk_ref[:, _chunk(c, r * D)] = tmp_ref[c, _class_rows(r, d, tm), :].astype(blk_ref.dtype)


def _qkv_classes(name, n2, nk, wq, wkv, g, d, tables):
    def emit(acc, cos_ref, sin_ref, o_ref, tmp_ref, scale):
        for c in range(NCHUNK):
            tmp_ref[c] = acc[:, _chunk(c)]
        for r in range(d):
            rows = _class_rows(r, d, TM)
            if scale is not None:
                cs = cos_ref[rows, :]
                sn = sin_ref[rows, :]
            for c in range(NCHUNK):
                x = tmp_ref[c, rows, :]
                if scale is not None:
                    x = (x * cs + _swap_halves(x) * sn) * scale
                o_ref[:, _chunk(c, r * D)] = x.astype(o_ref.dtype)

    def body(n2_ref, nk_ref, wq_ref, wk_ref, wv_ref, cos_ref, sin_ref, q_ref, k_ref, v_ref, tmp_ref):
        emit(_dot_nn(n2_ref[...], wq_ref[...]), cos_ref, sin_ref, q_ref, tmp_ref, HEAD_DIM ** -0.5)
        x = nk_ref[...]
        emit(_dot_nn(x, wk_ref[...]), cos_ref, sin_ref, k_ref, tmp_ref, 1.0)
        emit(_dot_nn(x, wv_ref[...]), cos_ref, sin_ref, v_ref, tmp_ref, None)

    nbr = len(DILATIONS)
    act = pl.BlockSpec((TM, D), lambda i: (i, 0))
    tab = pl.BlockSpec((TM, 128), lambda i: (i, 0))
    wcol = lambda col: pl.BlockSpec((D, D), lambda i: (0, col))
    return pl.pallas_call(
        body, name=name, grid=(S // TM,),
        in_specs=[act, act, wcol(g), wcol(g), wcol(nbr + g), tab, tab],
        out_specs=[_class_block(d, TM)] * 3, out_shape=[_sds((S // d, d * D), BF16)] * 3,
        scratch_shapes=[pltpu.VMEM((NCHUNK, TM, 128), F32)],
        compiler_params=_cparams(1),
    )(n2, nk, wq, wkv, wkv, *tables)


ATTN_CHAINS = 16


def _attn_units(d):
    nblk = S // d // BAND
    return max(1, 2 * ATTN_CHAINS // nblk)


def _class_spec(d):
    return pl.BlockSpec((S // d, 128 * _attn_units(d)), lambda cb: (0, cb))


def _dot_nt(a, b):
    return lax.dot_general(a, b, _DIMS["nt"], preferred_element_type=F32)


def _dot_tn(a, b):
    return lax.dot_general(a, b, _DIMS["tn"], preferred_element_type=F32)


def _dot_nn(a, b):
    return lax.dot_general(a, b, _DIMS["nn"], preferred_element_type=F32)


def _band_mask(nkeys):
    qi = lax.broadcasted_iota(jnp.int32, (2 * BAND, nkeys), 0) % BAND
    kj = lax.broadcasted_iota(jnp.int32, (2 * BAND, nkeys), 1)
    if nkeys == BAND:
        return kj <= qi
    dist = qi + BAND - kj
    return (dist >= 0) & (dist <= BAND)


def _stack_heads(x):
    row = lax.broadcasted_iota(jnp.int32, (2 * BAND, 128), 0)
    lane = lax.broadcasted_iota(jnp.int32, (2 * BAND, 128), 1)
    keep = (row < BAND) == (lane < HEAD_DIM)
    return jnp.where(keep, jnp.concatenate([x, x], axis=0), jnp.zeros((), x.dtype))


def _unstack(x2):
    first_head = lax.broadcasted_iota(jnp.int32, (BAND, 128), 1) < HEAD_DIM
    return jnp.where(first_head, x2[:BAND], x2[BAND:])


def _for_later_blocks(nblk, units, fn):
    all_lanes = [slice(u * 128, (u + 1) * 128) for u in range(units)]
    unroll = max(1, ATTN_CHAINS // units)
    trips = (nblk - 1) // unroll
    if trips > 1:
        def step(i, carry):
            for j in range(unroll):
                for lanes in all_lanes:
                    fn(pl.multiple_of((1 + i * unroll + j) * BAND, BAND), lanes)
            return carry

        lax.fori_loop(0, trips, step, 0)
    else:
        trips = 0
    for sb in range(1 + trips * unroll, nblk):
        for lanes in all_lanes:
            fn(sb * BAND, lanes)


def _attn_fwd(name, q, k, v, d):
    nblk = S // d // BAND
    units = _attn_units(d)

    def body(q_ref, k_ref, v_ref, o_ref, lse_ref):
        def block(r0, k0, nkeys, lanes):
            q2 = _stack_heads(q_ref[pl.ds(r0, BAND), lanes])
            s = jnp.where(_band_mask(nkeys), _dot_nt(q2, k_ref[pl.ds(k0, nkeys), lanes]), NEG_INF)
            m = jnp.max(s, axis=-1, keepdims=True)
            p = jnp.exp(s - m)
            l = jnp.sum(p, axis=-1, keepdims=True)
            o2 = _dot_nn(p.astype(BF16), v_ref[pl.ds(k0, nkeys), lanes]) / l
            lse2 = jnp.broadcast_to(m + jnp.log(l), (2 * BAND, 128))
            o_ref[pl.ds(r0, BAND), lanes] = _unstack(o2).astype(o_ref.dtype)
            lse_ref[pl.ds(r0, BAND), lanes] = _unstack(lse2)

        for u in range(units):
            block(0, 0, BAND, slice(u * 128, (u + 1) * 128))

        _for_later_blocks(nblk, units, lambda r0, lanes: block(r0, r0 - BAND, 2 * BAND, lanes))

    spec = _class_spec(d)
    return pl.pallas_call(
        body, name=name, grid=(8 * d // units,),
        in_specs=[spec] * 3, out_specs=[spec] * 2,
        out_shape=[_sds((S // d, d * D), BF16), _sds((S // d, d * D), F32)],
        compiler_params=_cparams(1),
    )(q, k, v)


def _attn_bwd(name, q, k, v, do, lse, dd, d):
    nblk = S // d // BAND
    units = _attn_units(d)

    def body(q_ref, k_ref, v_ref, do_ref, lse_ref, dd_ref, dq_ref, dk_out, dv_out, dk_ref, dv_ref):
        def column(ref, r0, lanes, nkeys):
            tile = ref[pl.ds(r0, BAND), lanes]
            other = pltpu.roll(tile, HEAD_DIM, 1)
            first_head = lax.broadcasted_iota(jnp.int32, tile.shape, 1) < HEAD_DIM
            both = jnp.concatenate([jnp.where(first_head, tile, other), jnp.where(first_head, other, tile)], axis=0)
            return both if nkeys == BAND else jnp.concatenate([both, both], axis=1)

        def block(r0, k0, nkeys, lanes, first):
            q2 = _stack_heads(q_ref[pl.ds(r0, BAND), lanes])
            do2 = _stack_heads(do_ref[pl.ds(r0, BAND), lanes])
            kk = k_ref[pl.ds(k0, nkeys), lanes]
            vv = v_ref[pl.ds(k0, nkeys), lanes]
            s = jnp.where(_band_mask(nkeys), _dot_nt(q2, kk), NEG_INF)
            p = jnp.exp(s - column(lse_ref, r0, lanes, nkeys))
            ds = (p * (_dot_nt(do2, vv) - column(dd_ref, r0, lanes, nkeys))).astype(BF16)
            dq_ref[pl.ds(r0, BAND), lanes] = _unstack(_dot_nn(ds, kk)).astype(dq_ref.dtype)
            dk_part = _dot_tn(ds, q2)
            dv_part = _dot_tn(p.astype(BF16), do2)
            if first:
                dk_ref[pl.ds(k0, nkeys), lanes] = dk_part
                dv_ref[pl.ds(k0, nkeys), lanes] = dv_part
            else:
                dk_ref[pl.ds(k0, BAND), lanes] += dk_part[:BAND]
                dv_ref[pl.ds(k0, BAND), lanes] += dv_part[:BAND]
                dk_ref[pl.ds(k0 + BAND, BAND), lanes] = dk_part[BAND:]
                dv_ref[pl.ds(k0 + BAND, BAND), lanes] = dv_part[BAND:]

        for u in range(units):
            block(0, 0, BAND, slice(u * 128, (u + 1) * 128), True)

        _for_later_blocks(nblk, units, lambda r0, lanes: block(r0, r0 - BAND, 2 * BAND, lanes, False))
        dk_out[...] = dk_ref[...].astype(dk_out.dtype)
        dv_out[...] = dv_ref[...].astype(dv_out.dtype)

    spec = _class_spec(d)
    return pl.pallas_call(
        body, name=name, grid=(8 * d // units,),
        in_specs=[spec] * 6, out_specs=[spec] * 3,
        out_shape=[_sds((S // d, d * D), BF16)] * 3,
        scratch_shapes=[pltpu.VMEM((S // d, 128 * units), F32)] * 2,
        compiler_params=_cparams(1),
    )(q, k, v, do, lse, dd)


MIX_TILE = 256
DILATIONS = tuple(d for _, d in BRANCHES)


def _branch_weights(la, lb, lc):
    m = jnp.maximum(jnp.maximum(la, lb), lc)
    ea, eb, ec = jnp.exp(la - m), jnp.exp(lb - m), jnp.exp(lc - m)
    den = ea + eb + ec
    return ea / den, eb / den, ec / den


def _mix_operands(outs, lses):
    specs = [_class_block(d, MIX_TILE) for d in DILATIONS] * 2
    scratch = [pltpu.VMEM((NCHUNK, MIX_TILE, 128), F32)] * 4
    return list(outs) + list(lses), specs, scratch


def _mix_fwd(name, outs, lses):
    def body(o0, o1, o2, l0, l1, l2, o_ref, to1, to2, tl1, tl2):
        for blk, tmp, d in ((o1, to1, DILATIONS[1]), (o2, to2, DILATIONS[2]), (l1, tl1, DILATIONS[1]), (l2, tl2, DILATIONS[2])):
            _tokens_from_classes(blk, tmp, d, MIX_TILE)
        for c in range(NCHUNK):
            wa, wb, wc = _branch_weights(l0[:, _chunk(c)], tl1[c], tl2[c])
            o_ref[:, _chunk(c)] = (wa * o0[:, _chunk(c)].astype(F32) + wb * to1[c] + wc * to2[c]).astype(o_ref.dtype)

    operands, specs, scratch = _mix_operands(outs, lses)
    return pl.pallas_call(
        body, name=name, grid=(S // MIX_TILE,),
        in_specs=specs, out_specs=_row_spec(MIX_TILE), out_shape=_sds((S, D), BF16),
        scratch_shapes=scratch, compiler_params=_cparams(1),
    )(*operands)


def _head_sum(x, ones_blockdiag):
    hi = x.astype(BF16)
    r1 = x - hi.astype(F32)
    mid = r1.astype(BF16)
    lo = (r1 - mid.astype(F32)).astype(BF16)
    return _dot_nn(hi, ones_blockdiag) + _dot_nn(mid, ones_blockdiag) + _dot_nn(lo, ones_blockdiag)


def _mix_bwd(name, do, outs, lses, ones_blockdiag):
    def body(do_ref, o0, o1, o2, l0, l1, l2, ones_ref, d0, d1, d2, t0, t1, t2,
             to1, to2, tl1, tl2, td1, td2, tt1, tt2):
        for blk, tmp, d in ((o1, to1, DILATIONS[1]), (o2, to2, DILATIONS[2]), (l1, tl1, DILATIONS[1]), (l2, tl2, DILATIONS[2])):
            _tokens_from_classes(blk, tmp, d, MIX_TILE)
        ones = ones_ref[...]
        for c in range(NCHUNK):
            w = _branch_weights(l0[:, _chunk(c)], tl1[c], tl2[c])
            dov = do_ref[:, _chunk(c)]
            o = w[0] * o0[:, _chunk(c)].astype(F32) + w[1] * to1[c] + w[2] * to2[c]
            t = _head_sum(dov * o, ones)
            d0[:, _chunk(c)] = (w[0] * dov).astype(d0.dtype)
            t0[:, _chunk(c)] = w[0] * t
            td1[c], tt1[c] = w[1] * dov, w[1] * t
            td2[c], tt2[c] = w[2] * dov, w[2] * t
        for tmp, blk, d in ((td1, d1, DILATIONS[1]), (tt1, t1, DILATIONS[1]), (td2, d2, DILATIONS[2]), (tt2, t2, DILATIONS[2])):
            _classes_from_tokens(tmp, blk, d, MIX_TILE)

    operands, specs, scratch = _mix_operands(outs, lses)
    out_specs = [_class_block(d, MIX_TILE) for d in DILATIONS] * 2
    out_shape = [_sds((S // d, d * D), BF16) for d in DILATIONS] + [_sds((S // d, d * D), F32) for d in DILATIONS]
    return pl.pallas_call(
        body, name=name, grid=(S // MIX_TILE,),
        in_specs=[_row_spec(MIX_TILE)] + specs + [_vec_spec(128, 128)],
        out_specs=out_specs, out_shape=out_shape,
        scratch_shapes=scratch + [pltpu.VMEM((NCHUNK, MIX_TILE, 128), F32)] * 4,
        compiler_params=_cparams(1),
    )(do, *operands, ones_blockdiag)


def _attn_bwd_post(name, grads, cos_t, sin_t):
    tm = MIX_TILE
    scale = HEAD_DIM ** -0.5

    def unrope(x, cs, sn):
        return x * cs - _swap_halves(x) * sn

    def body(*refs):
        in_refs = refs[:9]
        cos_ref, sin_ref, dq_ref, dkv_ref, tmp_ref = refs[9:]
        cs = cos_ref[...]
        sn = sin_ref[...]
        for g, d in enumerate(DILATIONS):
            for which, blk in enumerate(in_refs[3 * g:3 * g + 3]):
                if d > 1:
                    _tokens_from_classes(blk, tmp_ref, d, tm)
                for c in range(NCHUNK):
                    x = tmp_ref[c] if d > 1 else blk[:, _chunk(c)].astype(F32)
                    if which == 0:
                        dq_ref[:, _chunk(c, g * D)] = (unrope(x, cs, sn) * scale).astype(dq_ref.dtype)
                    elif which == 1:
                        dkv_ref[:, _chunk(c, g * D)] = unrope(x, cs, sn).astype(dkv_ref.dtype)
                    else:
                        dkv_ref[:, _chunk(c, QW + g * D)] = x.astype(dkv_ref.dtype)

    operands = [a for branch in grads for a in branch]
    tab = pl.BlockSpec((tm, 128), lambda i: (i, 0))
    return pl.pallas_call(
        body, name=name, grid=(S // tm,),
        in_specs=[_class_block(d, tm) for d in DILATIONS for _ in range(3)] + [tab, tab],
        out_specs=[pl.BlockSpec((tm, QW), lambda i: (i, 0)), pl.BlockSpec((tm, 2 * QW), lambda i: (i, 0))],
        out_shape=[_sds((S, QW), BF16), _sds((S, 2 * QW), BF16)],
        scratch_shapes=[pltpu.VMEM((NCHUNK, tm, 128), F32)],
        compiler_params=_cparams(1),
    )(*operands, cos_t, sin_t)


def _adamw(name, parts, w, m, v, layer=None, other=None):
    n, rows, cols = parts.shape
    tr = rows
    for cand in (256, 176, 128, 64, 32, 16, 8):
        if rows % cand == 0:
            tr = cand
            break
    n_other = 0 if other is None else len(other)

    def body(p_ref, w_ref, m_ref, v_ref, *refs):
        g_ref, d_ref, nm_ref, nv_ref = refs[n_other:]
        g = p_ref[0].astype(F32)
        for j in range(1, n):
            g = g + p_ref[j].astype(F32)
        g_ref[...] = g
        d_ref[...], nm_ref[...], nv_ref[...] = _adam_update(g, w_ref[...], m_ref[...], v_ref[...])

    if layer is None:
        blk = pl.BlockSpec((tr, cols), lambda i: (i, 0))
        shape = (rows, cols)
    else:
        blk = pl.BlockSpec((None, tr, cols), lambda i: (layer, i, 0))
        shape = w.shape
    return pl.pallas_call(
        body, name=name, grid=(rows // tr,),
        in_specs=[pl.BlockSpec((n, tr, cols), lambda i: (0, i, 0)), blk, blk, blk]
                 + [pl.BlockSpec(memory_space=pl.ANY)] * n_other,
        out_specs=[blk] * 4, out_shape=[_sds(shape, F32)] * 4,
        input_output_aliases={4 + i: i for i in range(n_other)},
        compiler_params=_cparams(1),
    )(parts, w, m, v, *(other or ()))


def _adam_update(g, w, m, v):
    c1 = 1.0 / (1.0 - ADAM_B1 ** ADAM_STEP)
    c2 = 1.0 / (1.0 - ADAM_B2 ** ADAM_STEP)
    nm = ADAM_B1 * m + (1.0 - ADAM_B1) * g
    nv = ADAM_B2 * v + (1.0 - ADAM_B2) * (g * g)
    return -ADAM_LR * ((nm * c1) / (jnp.sqrt(nv * c2) + ADAM_EPS) + ADAM_WD * w), nm, nv


GAIN_ROWS = 16


def _pack_small(name, gain_tiles, taps, sq):
    ng = len(gain_tiles)

    def body(*refs):
        o_ref = refs[-1]
        o_ref[...] = jnp.zeros_like(o_ref)
        for i in range(ng):
            o_ref[i:i + 1, :] = refs[i][0:1, :]
        o_ref[ng:ng + 3, :] = refs[ng][0:3, :]
        o_ref[ng + 3:ng + 4, :] = refs[ng + 1][...]

    return pl.pallas_call(body, name=name, out_shape=_sds((GAIN_ROWS, D), F32))(*gain_tiles, taps, sq)


def _adamw_gains(name, parts, params):
    np_ = len(params)
    shapes = [w.shape for w, _, _ in params]

    def body(p_ref, *refs):
        ins, outs = refs[:3 * np_], refs[3 * np_:]

        def total(lo, rows):
            g = p_ref[0, lo:lo + rows, :]
            for j in range(1, NDEV):
                g = g + p_ref[j, lo:lo + rows, :]
            return g

        lo = 0
        for i, shape in enumerate(shapes):
            g = total(lo, shape[0])
            lo += shape[0]
            w_ref, m_ref, v_ref = ins[3 * i:3 * i + 3]
            g_ref, d_ref, nm_ref, nv_ref = outs[4 * i:4 * i + 4]
            g_ref[...] = g
            d_ref[...], nm_ref[...], nv_ref[...] = _adam_update(g, w_ref[...], m_ref[...], v_ref[...])
        taps_ref, loss_ref = outs[-2], outs[-1]
        taps_ref[...] = jnp.zeros_like(taps_ref)
        taps_ref[0:3, :] = total(lo, 3)
        loss_ref[...] = jnp.sum(total(lo + 3, 1), axis=-1, keepdims=True) * (0.5 / D)

    out_shape = [_sds(shape, F32) for shape in shapes for _ in range(4)] + [_sds((8, D), F32), _sds((1, 1), F32)]
    outs = pl.pallas_call(body, name=name, out_shape=out_shape)(parts, *[a for p in params for a in p])
    return [list(outs[4 * i:4 * i + 4]) for i in range(np_)], outs[-2], outs[-1].reshape(())


def _exchange(name, arrays, kind, after):
    n = len(arrays)
    gather = kind == "gather"
    out_shape = [_sds((NDEV,) + a.shape if gather else a.shape, a.dtype) for a in arrays]

    def body(*refs):
        srcs, outs = refs[:n], refs[n + 1:2 * n + 1]
        send_sems, recv_sems, local_sems = refs[2 * n + 1:]
        x, y, c = lax.axis_index("x"), lax.axis_index("y"), lax.axis_index("c")
        me = 4 * x + 2 * y + c
        pending = []
        for t in range(n):
            own = pltpu.make_async_copy(srcs[t] if gather else srcs[t].at[me], outs[t].at[me], local_sems.at[t])
            own.start()
            pending.append(own)
            for rel in range(1, NDEV):
                px = 1 - x if rel & 4 else x
                py = 1 - y if rel & 2 else y
                pc = 1 - c if rel & 1 else c
                peer = 4 * px + 2 * py + pc
                send = pltpu.make_async_remote_copy(
                    src_ref=srcs[t] if gather else srcs[t].at[peer], dst_ref=outs[t].at[me],
                    send_sem=send_sems.at[t, rel - 1], recv_sem=recv_sems.at[t, rel - 1],
                    device_id=(px, py, pc), device_id_type=MESH)
                send.start()
                arrive = pltpu.make_async_remote_copy(
                    src_ref=srcs[t] if gather else srcs[t].at[me], dst_ref=outs[t].at[peer],
                    send_sem=send_sems.at[t, rel - 1], recv_sem=recv_sems.at[t, rel - 1],
                    device_id=(px, py, pc), device_id_type=MESH)
                pending.append((send, arrive))
        for item in pending:
            if isinstance(item, tuple):
                item[0].wait_send()
                item[1].wait_recv()
            else:
                item.wait()

    any_spec = pl.BlockSpec(memory_space=pl.ANY)
    outs = pl.pallas_call(
        body, name=name,
        in_specs=[any_spec] * (n + 1), out_specs=[any_spec] * n, out_shape=out_shape,
        scratch_shapes=[pltpu.SemaphoreType.DMA((n, NDEV - 1)), pltpu.SemaphoreType.DMA((n, NDEV - 1)),
                        pltpu.SemaphoreType.DMA((n,))],
    )(*arrays, after)
    return list(outs)


_HBM_SPEC = pl.BlockSpec(memory_space=pltpu.HBM)
_SEM_SPEC = pl.BlockSpec(memory_space=pltpu.SEMAPHORE)
_DATAFLOW = pltpu.SideEffectType.DATAFLOW_SIDE_EFFECTING


def _peers():
    x, y, c = lax.axis_index("x"), lax.axis_index("y"), lax.axis_index("c")
    out = []
    for rel in range(1, NDEV):
        px = 1 - x if rel & 4 else x
        py = 1 - y if rel & 2 else y
        pc = 1 - c if rel & 1 else c
        out.append((rel - 1, (px, py, pc), 4 * px + 2 * py + pc))
    return 4 * x + 2 * y + c, out


def _hbm(a):
    return pltpu.HBM(a.shape, a.dtype)


def _own_slot(a, me, kind):
    mine = a[None] if kind == "gather" else lax.dynamic_slice_in_dim(a, me, 1, axis=0)
    shape = (NDEV,) + mine.shape[1:]
    return lax.dynamic_update_slice_in_dim(lax.empty(shape, a.dtype), mine, me, axis=0)


def _exchange_start(name, arrays, me, kind):
    n = len(arrays)
    gather = kind == "gather"
    lands = [_own_slot(a, me, kind) for a in arrays]

    def body(*refs):
        src_refs, land_refs = refs[:n], refs[n:2 * n]
        send_sems, recv_sems = refs[2 * n], refs[2 * n + 1]
        token = refs[-1]
        my_block, peers = _peers()
        for t in range(n):
            for slot, dev, block in peers:
                pltpu.make_async_remote_copy(
                    src_ref=src_refs[t] if gather else src_refs[t].at[block], dst_ref=land_refs[t].at[my_block],
                    send_sem=send_sems.at[t * (NDEV - 1) + slot], recv_sem=recv_sems.at[t * (NDEV - 1) + slot],
                    device_id=dev, device_id_type=MESH).start()
        token[...] = jnp.zeros_like(token)

    operands = [pltpu.with_memory_space_constraint(a, pltpu.HBM) for a in list(arrays) + lands]
    outs = pl.pallas_call(
        body, name=name,
        out_shape=(pltpu.SemaphoreType.DMA((n * (NDEV - 1),)), pltpu.SemaphoreType.DMA((n * (NDEV - 1),)),
                   *[_hbm(a) for a in operands], _sds((8, 128), F32)),
        in_specs=[_HBM_SPEC] * (2 * n),
        out_specs=(_SEM_SPEC, _SEM_SPEC, *[_HBM_SPEC] * (2 * n), pl.BlockSpec(memory_space=pltpu.VMEM)),
        input_output_aliases={i: 2 + i for i in range(2 * n)},
        compiler_params=pltpu.CompilerParams(has_side_effects=_DATAFLOW),
    )(*operands)
    return (outs[0], outs[1], list(outs[2:2 + n]), list(outs[2 + n:2 + 2 * n])), outs[-1]


def _exchange_wait(name, started, t, after, kind):
    send_sems, recv_sems, srcs, lands = started
    gather = kind == "gather"

    def body(src_ref, land_ref, send_ref, recv_ref, after_ref, src_out, land_out):
        _, peers = _peers()
        for slot, dev, block in peers:
            copy = pltpu.make_async_remote_copy(
                src_ref=src_ref if gather else src_ref.at[block], dst_ref=land_ref.at[block],
                send_sem=send_ref.at[t * (NDEV - 1) + slot], recv_sem=recv_ref.at[t * (NDEV - 1) + slot],
                device_id=dev, device_id_type=MESH)
            copy.wait_send()
            copy.wait_recv()

    return pl.pallas_call(
        body, name=name, out_shape=(_hbm(srcs[t]), _hbm(lands[t])),
        in_specs=(_HBM_SPEC, _HBM_SPEC, _SEM_SPEC, _SEM_SPEC, pl.BlockSpec(memory_space=pl.ANY)),
        out_specs=(_HBM_SPEC, _HBM_SPEC), input_output_aliases={0: 0, 1: 1},
        compiler_params=pltpu.CompilerParams(has_side_effects=_DATAFLOW),
    )(srcs[t], lands[t], send_sems, recv_sems, after)[1]


DIRECT_RELS = (1, 2, 4, 6)
RELAY_RELS = (2, 4, 6)


def _rel_peer(rel):
    x, y, c = lax.axis_index("x"), lax.axis_index("y"), lax.axis_index("c")
    px = 1 - x if rel & 4 else x
    py = 1 - y if rel & 2 else y
    pc = 1 - c if rel & 1 else c
    return (px, py, pc), 4 * px + 2 * py + pc


def _gather_start(name, shards, me):
    n, nr = len(shards), len(DIRECT_RELS)
    lands = [_own_slot(a, me, "gather") for a in shards]

    def body(*refs):
        src_refs, land_refs = refs[:n], refs[n:2 * n]
        send_sems, recv_sems = refs[2 * n], refs[2 * n + 1]
        _, my_block = _rel_peer(0)
        for t in range(n):
            for s, rel in enumerate(DIRECT_RELS):
                dev, _ = _rel_peer(rel)
                pltpu.make_async_remote_copy(
                    src_ref=src_refs[t], dst_ref=land_refs[t].at[my_block],
                    send_sem=send_sems.at[t * nr + s], recv_sem=recv_sems.at[t * nr + s],
                    device_id=dev, device_id_type=MESH).start()

    operands = [pltpu.with_memory_space_constraint(a, pltpu.HBM) for a in list(shards) + lands]
    outs = pl.pallas_call(
        body, name=name,
        out_shape=(pltpu.SemaphoreType.DMA((n * nr,)), pltpu.SemaphoreType.DMA((n * nr,)), *[_hbm(a) for a in operands]),
        in_specs=[_HBM_SPEC] * (2 * n), out_specs=(_SEM_SPEC, _SEM_SPEC, *[_HBM_SPEC] * (2 * n)),
        input_output_aliases={i: 2 + i for i in range(2 * n)},
        compiler_params=pltpu.CompilerParams(has_side_effects=_DATAFLOW),
    )(*operands)
    return outs[0], outs[1], list(outs[2:2 + n]), list(outs[2 + n:2 + 2 * n])


def _gather_wait(name, started, ts, after):
    send_sems, recv_sems, srcs, lands = started
    m, nr = len(ts), len(DIRECT_RELS)

    def body(*refs):
        src_refs, land_refs = refs[:m], refs[m:2 * m]
        send_ref, recv_ref = refs[2 * m], refs[2 * m + 1]
        for i, t in enumerate(ts):
            for s, rel in enumerate(DIRECT_RELS):
                dev, block = _rel_peer(rel)
                copy = pltpu.make_async_remote_copy(
                    src_ref=src_refs[i], dst_ref=land_refs[i].at[block],
                    send_sem=send_ref.at[t * nr + s], recv_sem=recv_ref.at[t * nr + s],
                    device_id=dev, device_id_type=MESH)
                copy.wait_send()
                copy.wait_recv()

    operands = [srcs[t] for t in ts] + [lands[t] for t in ts]
    outs = pl.pallas_call(
        body, name=name, out_shape=tuple(_hbm(a) for a in operands),
        in_specs=[_HBM_SPEC] * (2 * m) + [_SEM_SPEC, _SEM_SPEC, pl.BlockSpec(memory_space=pl.ANY)],
        out_specs=tuple([_HBM_SPEC] * (2 * m)), input_output_aliases={i: i for i in range(2 * m)},
        compiler_params=pltpu.CompilerParams(has_side_effects=_DATAFLOW),
    )(*operands, send_sems, recv_sems, after)
    return list(outs[m:])


def _relay_start(name, lands):
    m, nr = len(lands), len(RELAY_RELS)

    def body(*refs):
        land_refs, send_sems, recv_sems = refs[:m], refs[m], refs[m + 1]
        sibling, _ = _rel_peer(1)
        for i in range(m):
            for s, rel in enumerate(RELAY_RELS):
                _, block = _rel_peer(rel)
                pltpu.make_async_remote_copy(
                    src_ref=land_refs[i].at[block], dst_ref=land_refs[i].at[block],
                    send_sem=send_sems.at[i * nr + s], recv_sem=recv_sems.at[i * nr + s],
                    device_id=sibling, device_id_type=MESH).start()

    outs = pl.pallas_call(
        body, name=name,
        out_shape=(pltpu.SemaphoreType.DMA((m * nr,)), pltpu.SemaphoreType.DMA((m * nr,)), *[_hbm(a) for a in lands]),
        in_specs=[_HBM_SPEC] * m, out_specs=(_SEM_SPEC, _SEM_SPEC, *[_HBM_SPEC] * m),
        input_output_aliases={i: 2 + i for i in range(m)},
        compiler_params=pltpu.CompilerParams(has_side_effects=_DATAFLOW),
    )(*lands)
    return outs[0], outs[1], list(outs[2:])


def _relay_wait(name, relayed, after):
    send_sems, recv_sems, lands = relayed
    m, nr = len(lands), len(RELAY_RELS)

    def body(*refs):
        land_refs, send_ref, recv_ref = refs[:m], refs[m], refs[m + 1]
        sibling, _ = _rel_peer(1)
        for i in range(m):
            for s, rel in enumerate(RELAY_RELS):
                _, sent = _rel_peer(rel)
                _, arriving = _rel_peer(rel ^ 1)
                copy = pltpu.make_async_remote_copy(
                    src_ref=land_refs[i].at[sent], dst_ref=land_refs[i].at[arriving],
                    send_sem=send_ref.at[i * nr + s], recv_sem=recv_ref.at[i * nr + s],
                    device_id=sibling, device_id_type=MESH)
                copy.wait_send()
                copy.wait_recv()

    outs = pl.pallas_call(
        body, name=name, out_shape=tuple(_hbm(a) for a in lands),
        in_specs=[_HBM_SPEC] * m + [_SEM_SPEC, _SEM_SPEC, pl.BlockSpec(memory_space=pl.ANY)],
        out_specs=tuple([_HBM_SPEC] * m), input_output_aliases={i: i for i in range(m)},
        compiler_params=pltpu.CompilerParams(has_side_effects=_DATAFLOW),
    )(*lands, send_sems, recv_sems, after)
    return list(outs)


def _ffn_fwd(tag, n, wg, wd):
    gu, act = _gate_up_act(f"ffn_gate_up_{tag}", n, wg)
    wd4 = wd.reshape(NFB, FB, D)
    f = _fwd_kblocked(f"ffn_down_{tag}", act, wd4)
    return (n, gu, act, wg, wd4), f


def _ffn_bwd(tag, dh_out, df, h_in, saved, g_pre, send, mixer):
    n, gu, act, wg, wd4 = saved
    dwd = _bwd_w_kblocked(f"ffn_down_dw_{tag}", act, df).reshape(NDEV, DFF // NDEV, D)
    dgu = _down_dx_act_bwd(f"ffn_down_dx_{tag}", df, wd4, gu).reshape(NDEV, S, FB)
    tok = send({f"down_{tag}": dwd, f"gate_up_{tag}": _bwd_w_cols_blocked(f"ffn_gate_up_dw_{tag}", n, dgu)})
    dn = _bwd_x_cols_blocked(f"ffn_gate_up_dx_{tag}", dgu, wg, after=tok)
    dh_in, (dg_pre,), dy, dg_mixer = _rms_bwd(f"ffn_prenorm_bwd_{tag}", h_in, [(g_pre, dn)], dh_out, F32, then=mixer)
    return dh_in, dg_pre, dy, dg_mixer


def kernel(x, positions, mix_norm_pre, mix_norm_post, ffn_norm_pre, ffn_norm_post, ffn_w_gate_up, ffn_w_down, conv_w_in, conv_w, conv_w_out, kv_norm, w_kv, w_q, w_o, loss_target, m_mix_norm_pre, m_mix_norm_post, m_ffn_norm_pre, m_ffn_norm_post, m_ffn_w_gate_up, m_ffn_w_down, m_conv_w_in, m_conv_w, m_conv_w_out, m_kv_norm, m_w_kv, m_w_q, m_w_o, v_mix_norm_pre, v_mix_norm_post, v_ffn_norm_pre, v_ffn_norm_post, v_ffn_w_gate_up, v_ffn_w_down, v_conv_w_in, v_conv_w, v_conv_w_out, v_kv_norm, v_w_kv, v_w_q, v_w_o):
    me = 4 * lax.axis_index("x") + 2 * lax.axis_index("y") + lax.axis_index("c")
    h0 = x.reshape(S, D)
    target = loss_target.reshape(S, D)
    row = lambda a, l: a[l].reshape(1, D)
    g_kv = kv_norm.reshape(1, D)

    cw_shard = jnp.pad(conv_w[0], ((0, 5), (0, 0)))
    names = ["conv_in", "conv_w", "conv_out", "gate_up_0", "down_0", "kv", "q", "o", "gate_up_1", "down_1"]
    shards = [conv_w_in[0], cw_shard, conv_w_out[0], ffn_w_gate_up[0], ffn_w_down[0],
              w_kv, w_q[0], w_o[0], ffn_w_gate_up[1], ffn_w_down[1]]
    shards = [s if n == "conv_w" else s.astype(BF16) for n, s in zip(names, shards)]
    first = 3
    gather_first = _gather_start("gather_start_conv", shards[:first], me)
    gather_rest = _gather_start("gather_start_rest", shards[first:], me)

    def direct(group, after):
        ts = [names.index(n) for n in group]
        started, ts = (gather_first, ts) if ts[0] < first else (gather_rest, [t - first for t in ts])
        lands = _gather_wait(f"gather_wait_{group[0]}", started, ts, after)
        return _relay_start(f"relay_start_{group[0]}", lands)

    def finish(group, relayed, after):
        return dict(zip(group, _relay_wait(f"relay_wait_{group[0]}", relayed, after)))

    sent = {}

    def send(grads):
        started, token = _exchange_start(f"scatter_start_{next(iter(grads))}", list(grads.values()), me, "scatter")
        for i, name in enumerate(grads):
            sent[name] = (started, i)
        return token

    groups = [["conv_in", "conv_w", "conv_out"], ["gate_up_0", "down_0"], ["kv", "q"], ["o", "gate_up_1", "down_1"]]
    n0 = _rms_fwd("mix_prenorm_0", h0, [row(mix_norm_pre, 0)])[0]
    half = HEAD_DIM // 2
    inv_freq = ROPE_THETA ** (-jnp.arange(half, dtype=F32) / half)
    tables = _rope_tables("rope_tables", positions.reshape(S, 1), jnp.tile(inv_freq, 4).reshape(1, 128))
    w = finish(groups[0], direct(groups[0], tables[0]), n0)
    win = w["conv_in"].transpose(1, 0, 2).reshape(D, 3 * D)
    cw = w["conv_w"].transpose(1, 0, 2).reshape(8, D)
    wout = w["conv_out"].reshape(D, D)
    z = _fwd_rows("conv_in", n0, win, BF16)
    pre = _conv_fwd("conv_gate", z, cw)
    relayed = direct(groups[1], pre)
    y0 = _fwd_rows("conv_out", pre, wout)
    h1, (n1,) = _resid_rms("mix_postnorm_0", h0, y0, row(mix_norm_post, 0), [row(ffn_norm_pre, 0)])
    w = finish(groups[1], relayed, n1)
    ffn0, f0 = _ffn_fwd("0", n1, w["gate_up_0"], w["down_0"])
    relayed = direct(groups[2], ffn0[2])
    h2, (nk, n2) = _resid_rms("ffn_postnorm_0", h1, f0, row(ffn_norm_post, 0), [g_kv, row(mix_norm_pre, 1)])

    w = finish(groups[2], relayed, nk)
    wkv = w["kv"].transpose(1, 0, 2).reshape(D, 2 * QW)
    wq = w["q"].transpose(1, 0, 2).reshape(D, QW)
    qc, kc, vc, o_c, lse_c = [], [], [], [], []
    for g, d in enumerate(DILATIONS):
        q_g, k_g, v_g = _qkv_classes(f"qkv_proj_{g}", n2, nk, wq, wkv, g, d, tables)
        qc.append(q_g)
        kc.append(k_g)
        vc.append(v_g)
    relayed = direct(groups[3], vc[-1])
    for g, d in enumerate(DILATIONS):
        o_g, lse_g = _attn_fwd(f"attn_fwd_{g}", qc[g], kc[g], vc[g], d)
        o_c.append(o_g)
        lse_c.append(lse_g)
    o_mix = _mix_fwd("attn_mix", o_c, lse_c)
    w = finish(groups[3], relayed, o_mix)
    wo = w["o"].reshape(D, D)
    y1 = _fwd_rows("attn_out", o_mix, wo)
    h3, (n3,) = _resid_rms("mix_postnorm_1", h2, y1, row(mix_norm_post, 1), [row(ffn_norm_pre, 1)])
    ffn1, f1 = _ffn_fwd("1", n3, w["gate_up_1"], w["down_1"])

    dh4, df1, dg_fpost1, sq = _resid_rms_loss("ffn_postnorm_1_loss", h3, f1, row(ffn_norm_post, 1), target)

    dh3, dg_fpre1, dy1, dg_mpost1 = _ffn_bwd(
        "1", dh4, df1, h3, ffn1, row(ffn_norm_pre, 1), send, (y1, row(mix_norm_post, 1)))
    dwo = _bwd_w_rows("attn_out_dw", o_mix, dy1).reshape(NDEV, D // NDEV, D)
    do = _bwd_x_rows("attn_out_dx", dy1, wo, F32)
    lane = jnp.arange(128)
    ones_blockdiag = (lane[:, None] // HEAD_DIM == lane[None, :] // HEAD_DIM).astype(BF16)
    mixed = _mix_bwd("attn_mix_bwd", do, o_c, lse_c, ones_blockdiag)
    branch_grads = [_attn_bwd(f"attn_bwd_{g}", qc[g], kc[g], vc[g], mixed[g], lse_c[g], mixed[3 + g], d)
                    for g, d in enumerate(DILATIONS)]
    dq_raw, dkv = _attn_bwd_post("attn_bwd_post", branch_grads, *tables)
    tok = send({"o": dwo, "kv": _bwd_w_cols("kv_proj_dw", nk, dkv, 2 * QW // NDEV),
                "q": _bwd_w_cols("q_proj_dw", n2, dq_raw, QW // NDEV)})
    dnk = _bwd_x_plain("kv_proj_dx", dkv, wkv, after=tok)
    dn2 = _bwd_x_plain("q_proj_dx", dq_raw, wq)
    dh2, (dg_kv, dg_mpre1), df0, dg_fpost0 = _rms_bwd(
        "kv_and_mix_prenorm_bwd_1", h2, [(g_kv, dnk), (row(mix_norm_pre, 1), dn2)], dh3, F32,
        then=(f0, row(ffn_norm_post, 0)))

    dh1, dg_fpre0, dy0, dg_mpost0 = _ffn_bwd(
        "0", dh2, df0, h1, ffn0, row(ffn_norm_pre, 0), send, (y0, row(mix_norm_post, 0)))
    dwout = _bwd_w_rows("conv_out_dw", pre, dy0).reshape(NDEV, D // NDEV, D)
    dpre = _bwd_x_rows("conv_out_dx", dy0, wout, BF16)
    dz, dcw = _conv_bwd("conv_gate_bwd", z, dpre, cw)
    tok = send({"conv_out": dwout, "conv_in": _bwd_w_cols("conv_in_dw", n0, dz, 3 * D // NDEV)})
    dn0 = _bwd_x_plain("conv_in_dx", dz, win, after=tok)
    dh0, (dg_mpre0,) = _rms_bwd("mix_prenorm_bwd_0", h0, [(row(mix_norm_pre, 0), dn0)], dh1, F32)

    small = _pack_small("pack_small_grads", [dg_mpre0, dg_mpre1, dg_mpost0, dg_mpost1, dg_fpre0, dg_fpre1,
                                             dg_fpost0, dg_fpost1, dg_kv], dcw, sq)

    done = [small]

    def upd(tag, w, m, v):
        parts = _exchange_wait(f"scatter_wait_{tag}", *sent[tag], done[-1], "scatter")
        shape = w.shape
        flat = lambda a: a.reshape(parts.shape[1:])
        res = _adamw(f"adamw_{tag}", parts, flat(w), flat(m), flat(v))
        done.append(res[0])
        return [r.reshape(shape) for r in res]

    def upd_layer(tag, l, w, m, v, other):
        parts = _exchange_wait(f"scatter_wait_{tag}_{l}", *sent[f"{tag}_{l}"], done[-1], "scatter")
        res = _adamw(f"adamw_{tag}_{l}", parts, w, m, v, layer=l, other=other)
        done.append(res[0])
        return list(res)

    res = {}
    down_1 = upd_layer("down", 1, ffn_w_down, m_ffn_w_down, v_ffn_w_down, None)
    gate_up_t = [jnp.swapaxes(a, 1, 2) for a in (ffn_w_gate_up, m_ffn_w_gate_up, v_ffn_w_gate_up)]
    gate_up_1 = upd_layer("gate_up", 1, *gate_up_t, None)
    res["w_o"] = upd("o", w_o, m_w_o, v_w_o)
    res["w_q"] = upd("q", w_q, m_w_q, v_w_q)
    res["w_kv"] = upd("kv", w_kv, m_w_kv, v_w_kv)

    small_all = _exchange("gather_small_grads", [small], "gather", done[-1])[0]
    vec = lambda a: a.reshape(1, D)
    gain_res, taps, loss = _adamw_gains("adamw_gains", small_all, [
        (mix_norm_pre, m_mix_norm_pre, v_mix_norm_pre), (mix_norm_post, m_mix_norm_post, v_mix_norm_post),
        (ffn_norm_pre, m_ffn_norm_pre, v_ffn_norm_pre), (ffn_norm_post, m_ffn_norm_post, v_ffn_norm_post),
        (vec(kv_norm), vec(m_kv_norm), vec(v_kv_norm))])
    dcw_mine = lax.dynamic_slice(taps, (0, me * 128), (8, 128))
    pad8 = lambda a, fill: jnp.pad(a[0], ((0, 5), (0, 0)), constant_values=fill)
    cw_res = [r[0:3].reshape(1, 3, 128) for r in
              _adamw("adamw_conv_w", dcw_mine.reshape(1, 8, 128), cw_shard, pad8(m_conv_w, 0.0), pad8(v_conv_w, 1.0))]

    res.update({
        "mix_norm_pre": gain_res[0],
        "mix_norm_post": gain_res[1],
        "ffn_norm_pre": gain_res[2],
        "ffn_norm_post": gain_res[3],
        "kv_norm": [r.reshape(D) for r in gain_res[4]],
        "conv_w": cw_res,
    })
    done.append(small_all)
    res["ffn_w_down"] = upd_layer("down", 0, ffn_w_down, m_ffn_w_down, v_ffn_w_down, down_1)
    res["ffn_w_gate_up"] = [jnp.swapaxes(r, 1, 2) for r in upd_layer("gate_up", 0, *gate_up_t, gate_up_1)]
    res["conv_w_out"] = upd("conv_out", conv_w_out, m_conv_w_out, v_conv_w_out)
    res["conv_w_in"] = upd("conv_in", conv_w_in, m_conv_w_in, v_conv_w_in)
    order = ["mix_norm_pre", "mix_norm_post", "ffn_norm_pre", "ffn_norm_post", "ffn_w_gate_up", "ffn_w_down",
             "conv_w_in", "conv_w", "conv_w_out", "kv_norm", "w_kv", "w_q", "w_o"]
    out = [loss, dh0.reshape(1, S, D)]
    for i in range(4):
        out += [res[name][i] for name in order]
    return tuple(out)
```

```python
import jax
import jax.numpy as jnp
from jax import lax
from jax.experimental import pallas as pl
from jax.experimental.pallas import tpu as pltpu

F32 = jnp.float32
BF16 = jnp.bfloat16

S = 4096
D = 1024
NDEV = 8
HEAD_DIM = 64
QW = 3072
DFF = 2816
FB = 704
NFB = 4
BRANCHES = ((128, 1), (512, 4), (2048, 16))
BAND = 128
ROPE_THETA = 10000.0
RMS_EPS = 1e-6
NEG_INF = -1e30
ADAM_LR, ADAM_B1, ADAM_B2, ADAM_EPS, ADAM_WD, ADAM_STEP = 0.001, 0.9, 0.999, 1e-08, 0.01, 10

VMEM_LIMIT_BYTES = 52 * 1024 * 1024
ROW_TILE = 512
MESH = pl.DeviceIdType.MESH


def _cparams(ngrid):
    return pltpu.CompilerParams(dimension_semantics=("arbitrary",) * ngrid,
                                vmem_limit_bytes=VMEM_LIMIT_BYTES)


def _sds(shape, dtype):
    return jax.ShapeDtypeStruct(tuple(shape), dtype)


_DIMS = {"nn": (((1,), (0,)), ((), ())),
         "nt": (((1,), (1,)), ((), ())),
         "tn": (((0,), (0,)), ((), ()))}


def _matmul(name, a, b, *, mode, grid, a_blk, a_map, b_blk, b_map, o_shape, o_blk, o_map, out_dtype, after=None,
            out_groups=1):
    nk = grid[2]
    dims = _DIMS[mode]
    acc_shape = tuple(s for s in o_blk if s is not None)
    if out_groups > 1:
        acc_shape = (acc_shape[1], out_groups * acc_shape[2])
    extra = [] if after is None else [after]

    def store(o_ref, val):
        if out_groups == 1:
            o_ref[...] = val.astype(o_ref.dtype)
        else:
            n = o_ref.shape[-1]
            for grp in range(out_groups):
                o_ref[grp] = val[:, grp * n:(grp + 1) * n].astype(o_ref.dtype)

    def body(a_ref, b_ref, *rest):
        o_ref, scratch = rest[len(extra)], rest[len(extra) + 1:]
        part = lax.dot_general(a_ref[...], b_ref[...], dims, preferred_element_type=F32)
        if nk == 1:
            store(o_ref, part)
            return
        acc_ref = scratch[0]
        k = pl.program_id(2)

        @pl.when(k == 0)
        def _():
            acc_ref[...] = part

        @pl.when(k > 0)
        def _():
            acc_ref[...] += part

        @pl.when(k == nk - 1)
        def _():
            store(o_ref, acc_ref[...])

    return pl.pallas_call(
        body, name=name, grid=grid,
        in_specs=[pl.BlockSpec(a_blk, a_map), pl.BlockSpec(b_blk, b_map)] + [pl.BlockSpec(memory_space=pl.ANY)] * len(extra),
        out_specs=pl.BlockSpec(o_blk, o_map),
        out_shape=_sds(o_shape, out_dtype),
        scratch_shapes=[] if nk == 1 else [pltpu.VMEM(acc_shape, F32)],
        compiler_params=_cparams(3),
    )(a, b, *extra)


TM = 1024
TK = S


def _fwd_rows(name, a, w, out_dtype=F32):
    kdim, n = w.shape
    tn = 512
    return _matmul(name, a, w, mode="nn", grid=(S // TM, n // tn, 1),
                   a_blk=(TM, kdim), a_map=lambda i, j, k: (i, 0),
                   b_blk=(kdim, tn), b_map=lambda i, j, k: (0, j),
                   o_shape=(S, n), o_blk=(TM, tn), o_map=lambda i, j, k: (i, j), out_dtype=out_dtype)


def _fwd_kblocked(name, a4, w4):
    nb, _, kb = a4.shape
    n = w4.shape[2]

    def body(a_ref, w_ref, o_ref):
        acc = _dot_nn(a_ref[0], w_ref[0])
        for j in range(1, nb):
            acc = acc + _dot_nn(a_ref[j], w_ref[j])
        o_ref[...] = acc

    return pl.pallas_call(
        body, name=name, grid=(S // TM,),
        in_specs=[pl.BlockSpec((nb, TM, kb), lambda i: (0, i, 0)), pl.BlockSpec((nb, kb, n), lambda i: (0, 0, 0))],
        out_specs=pl.BlockSpec((TM, n), lambda i: (i, 0)), out_shape=_sds((S, n), F32),
        compiler_params=_cparams(1),
    )(a4, w4)


def _bwd_x_cols_blocked(name, dy8, wg, after):
    _, kdim, n = wg.shape
    nk = NDEV // 2

    def body(a_ref, b_ref, after_ref, o_ref, acc_ref):
        k = pl.program_id(1)
        part = _dot_nt(a_ref[0], b_ref[0]) + _dot_nt(a_ref[1], b_ref[1])

        @pl.when(k == 0)
        def _():
            acc_ref[...] = part

        @pl.when(k > 0)
        def _():
            acc_ref[...] += part

        @pl.when(k == nk - 1)
        def _():
            o_ref[...] = acc_ref[...].astype(o_ref.dtype)

    return pl.pallas_call(
        body, name=name, grid=(S // TM, nk),
        in_specs=[pl.BlockSpec((2, None, TM, n), lambda i, k: (0, k, i, 0)),
                  pl.BlockSpec((2, None, kdim, n), lambda i, k: (0, k, 0, 0)),
                  pl.BlockSpec(memory_space=pl.ANY)],
        out_specs=pl.BlockSpec((TM, kdim), lambda i, k: (i, 0)), out_shape=_sds((S, kdim), BF16),
        scratch_shapes=[pltpu.VMEM((TM, kdim), F32)],
        compiler_params=_cparams(2),
    )(dy8.reshape(2, nk, S, n), wg.reshape(2, nk, kdim, n), after)


def _bwd_x_rows(name, dy, w, out_dtype, after=None):
    kdim, n = w.shape
    tkk = 512
    return _matmul(name, dy, w, mode="nt", grid=(S // TM, kdim // tkk, 1),
                   a_blk=(TM, n), a_map=lambda i, j, k: (i, 0),
                   b_blk=(tkk, n), b_map=lambda i, j, k: (j, 0),
                   o_shape=(S, kdim), o_blk=(TM, tkk), o_map=lambda i, j, k: (i, j), out_dtype=out_dtype, after=after)


DW_COLS = 768


def _bwd_w_cols(name, a, dy, n):
    kdim = a.shape[1]
    groups = DW_COLS // n
    return _matmul(name, a, dy, mode="tn", grid=(1, NDEV // groups, S // TK),
                   a_blk=(TK, kdim), a_map=lambda i, j, k: (k, 0),
                   b_blk=(TK, DW_COLS), b_map=lambda i, j, k: (k, j),
                   o_shape=(NDEV, kdim, n), o_blk=(groups, kdim, n) if groups > 1 else (None, kdim, n),
                   o_map=lambda i, j, k: (j, 0, 0), out_dtype=BF16, out_groups=groups)


def _bwd_x_plain(name, dy, w, after=None):
    kdim, n = w.shape
    tm = TM if n <= 3 * D else TM // 2
    return _matmul(name, dy, w, mode="nt", grid=(S // tm, 1, 1),
                   a_blk=(tm, n), a_map=lambda i, j, k: (i, 0),
                   b_blk=(kdim, n), b_map=lambda i, j, k: (0, 0),
                   o_shape=(S, kdim), o_blk=(tm, kdim), o_map=lambda i, j, k: (i, 0), out_dtype=BF16, after=after)


def _bwd_w_cols_blocked(name, a, dy8):
    kdim = a.shape[1]
    n = dy8.shape[2]
    return _matmul(name, dy8, a, mode="tn", grid=(1, NDEV, S // TK),
                   a_blk=(None, TK, n), a_map=lambda i, j, k: (j, k, 0),
                   b_blk=(TK, kdim), b_map=lambda i, j, k: (k, 0),
                   o_shape=(NDEV, n, kdim), o_blk=(None, n, kdim), o_map=lambda i, j, k: (j, 0, 0), out_dtype=BF16)


def _bwd_w_rows(name, a, dy):
    kdim = a.shape[1]
    n = dy.shape[1]
    tmm = 512
    return _matmul(name, a, dy, mode="tn", grid=(kdim // tmm, 1, S // TK),
                   a_blk=(TK, tmm), a_map=lambda i, j, k: (k, i),
                   b_blk=(TK, n), b_map=lambda i, j, k: (k, 0),
                   o_shape=(kdim, n), o_blk=(tmm, n), o_map=lambda i, j, k: (i, 0), out_dtype=BF16)


def _bwd_w_kblocked(name, a4, dy):
    nb, _, kb = a4.shape
    n = dy.shape[1]
    return _matmul(name, a4, dy, mode="tn", grid=(nb, 1, S // TK),
                   a_blk=(None, TK, kb), a_map=lambda i, j, k: (i, k, 0),
                   b_blk=(TK, n), b_map=lambda i, j, k: (k, 0),
                   o_shape=(nb, kb, n), o_blk=(None, kb, n), o_map=lambda i, j, k: (i, 0, 0), out_dtype=BF16)


def _rstd(x):
    return lax.rsqrt(jnp.mean(x * x, axis=-1, keepdims=True) + RMS_EPS)


def _row_spec(tm=ROW_TILE, width=D):
    return pl.BlockSpec((tm, width), lambda i: (i, 0))


def _vec_spec(rows=1, width=D):
    return pl.BlockSpec((rows, width), lambda i: (0, 0))


def _rms_fwd(name, x, gains):
    n = len(gains)

    def body(x_ref, *refs):
        x_val = x_ref[...]
        xh = x_val * _rstd(x_val)
        for g_ref, o_ref in zip(refs[:n], refs[n:]):
            o_ref[...] = (xh * g_ref[...]).astype(o_ref.dtype)

    outs = pl.pallas_call(
        body, name=name, grid=(S // ROW_TILE,),
        in_specs=[_row_spec()] + [_vec_spec()] * n,
        out_specs=[_row_spec()] * n,
        out_shape=[_sds((S, D), BF16)] * n,
        compiler_params=_cparams(1),
    )(x, *gains)
    return list(outs)


def _resid_rms(name, h, y, g, next_gains):
    n = len(next_gains)

    def body(h_ref, y_ref, g_ref, *refs):
        y_val = y_ref[...]
        h_new = h_ref[...] + (y_val * _rstd(y_val)) * g_ref[...]
        refs[n][...] = h_new
        hh = h_new * _rstd(h_new)
        for g2_ref, o_ref in zip(refs[:n], refs[n + 1:]):
            o_ref[...] = (hh * g2_ref[...]).astype(o_ref.dtype)

    outs = pl.pallas_call(
        body, name=name, grid=(S // ROW_TILE,),
        in_specs=[_row_spec(), _row_spec(), _vec_spec()] + [_vec_spec()] * n,
        out_specs=[_row_spec()] * (n + 1), out_shape=[_sds((S, D), F32)] + [_sds((S, D), BF16)] * n,
        compiler_params=_cparams(1),
    )(h, y, g, *next_gains)
    return outs[0], list(outs[1:])


def _resid_rms_loss(name, h, y, g, target):
    def body(h_ref, y_ref, g_ref, t_ref, dh_ref, dy_ref, dg_ref, part_ref):
        y_val = y_ref[...]
        gain = g_ref[...]
        e = h_ref[...] + (y_val * _rstd(y_val)) * gain - t_ref[...]
        dh = e * (1.0 / D)
        dh_ref[...] = dh
        step = pl.program_id(0)
        dy_ref[...] = _norm_bwd_rows(y_val, gain, dh, dg_ref, step).astype(dy_ref.dtype)
        part = jnp.sum(e * e, axis=0, keepdims=True)

        @pl.when(step == 0)
        def _():
            part_ref[...] = part

        @pl.when(step > 0)
        def _():
            part_ref[...] += part

    return pl.pallas_call(
        body, name=name, grid=(S // ROW_TILE,),
        in_specs=[_row_spec(), _row_spec(), _vec_spec(), _row_spec()],
        out_specs=[_row_spec(), _row_spec(), _vec_spec(8), _vec_spec()],
        out_shape=[_sds((S, D), F32), _sds((S, D), BF16), _sds((8, D), F32), _sds((1, D), F32)],
        compiler_params=_cparams(1),
    )(h, y, g, target)


def _norm_bwd_rows(x_val, g, dn, dg_ref, step):
    r = _rstd(x_val)
    xh = x_val * r
    dxh = dn * g
    part = jnp.sum(dn * xh, axis=0, keepdims=True)

    @pl.when(step == 0)
    def _():
        dg_ref[...] = jnp.zeros_like(dg_ref)

    dg_ref[0:1, :] += part
    return r * (dxh - xh * jnp.mean(dxh * xh, axis=-1, keepdims=True))


def _rms_bwd(name, x, pairs, dres, out_dtype, then=None):
    n = len(pairs)
    has_res = dres is not None
    chained = then is not None

    def body(x_ref, *refs):
        g_refs = refs[0:2 * n:2]
        dn_refs = refs[1:2 * n:2]
        pos = 2 * n
        res_ref = refs[pos] if has_res else None
        pos += int(has_res)
        if chained:
            y_ref, gy_ref = refs[pos], refs[pos + 1]
            pos += 2
        dx_ref = refs[pos]
        dg_refs = refs[pos + 1:pos + 1 + n]
        step = pl.program_id(0)
        x_val = x_ref[...]
        acc = res_ref[...] if has_res else jnp.zeros_like(x_val)
        for g_ref, dn_ref, dg_ref in zip(g_refs, dn_refs, dg_refs):
            acc = acc + _norm_bwd_rows(x_val, g_ref[...], dn_ref[...].astype(F32), dg_ref, step)
        dx_ref[...] = acc.astype(dx_ref.dtype)
        if chained:
            dy_ref, dgy_ref = refs[pos + 1 + n], refs[pos + 2 + n]
            dy_ref[...] = _norm_bwd_rows(y_ref[...], gy_ref[...], acc, dgy_ref, step).astype(dy_ref.dtype)

    operands = [x]
    in_specs = [_row_spec()]
    for g, dn in pairs:
        operands += [g, dn]
        in_specs += [_vec_spec(), _row_spec()]
    if has_res:
        operands.append(dres)
        in_specs.append(_row_spec())
    if chained:
        operands += [then[0], then[1]]
        in_specs += [_row_spec(), _vec_spec()]
    extra = int(chained)
    outs = pl.pallas_call(
        body, name=name, grid=(S // ROW_TILE,),
        in_specs=in_specs,
        out_specs=[_row_spec()] + [_vec_spec(8)] * n + [_row_spec(), _vec_spec(8)] * extra,
        out_shape=[_sds((S, D), out_dtype)] + [_sds((8, D), F32)] * n + [_sds((S, D), BF16), _sds((8, D), F32)] * extra,
        compiler_params=_cparams(1),
    )(*operands)
    if chained:
        return outs[0], list(outs[1:1 + n]), outs[1 + n], outs[2 + n]
    return outs[0], list(outs[1:])


def _shift_down(u, prev8, k):
    r = pltpu.roll(u, k, 0)
    p = pltpu.roll(prev8, k, 0)
    row = lax.broadcasted_iota(jnp.int32, prev8.shape, 0)
    top = jnp.where(row < k, p, r[0:8])
    return jnp.concatenate([top, r[8:]], axis=0)


def _shift_up(u, next8, k):
    tm = u.shape[0]
    r = pltpu.roll(u, tm - k, 0)
    p = pltpu.roll(next8, 8 - k, 0)
    row = lax.broadcasted_iota(jnp.int32, next8.shape, 0)
    bot = jnp.where(row >= 8 - k, p, r[tm - 8:tm])
    return jnp.concatenate([r[:tm - 8], bot], axis=0)


CONV_TILE = 512


def _halo_prev(col):
    return pl.BlockSpec((8, D), lambda i: (jnp.maximum(i * (CONV_TILE // 8) - 1, 0), col))


def _halo_next(col):
    last = S // 8 - 1
    return pl.BlockSpec((8, D), lambda i: (jnp.minimum((i + 1) * (CONV_TILE // 8), last), col))


def _conv_fwd(name, z, cw):
    def body(b_ref, c_ref, h_ref, cp_ref, hp_ref, cw_ref, o_ref):
        i = pl.program_id(0)
        u = c_ref[...].astype(F32) * h_ref[...].astype(F32)
        up = cp_ref[...].astype(F32) * hp_ref[...].astype(F32)
        up = jnp.where(i > 0, up, 0.0)
        cv = cw_ref[0:1, :] * _shift_down(u, up, 2) + cw_ref[1:2, :] * _shift_down(u, up, 1) + cw_ref[2:3, :] * u
        o_ref[...] = (b_ref[...].astype(F32) * cv).astype(o_ref.dtype)

    col = lambda c: pl.BlockSpec((CONV_TILE, D), lambda i: (i, c))
    return pl.pallas_call(
        body, name=name, grid=(S // CONV_TILE,),
        in_specs=[col(0), col(1), col(2), _halo_prev(1), _halo_prev(2), _vec_spec(8)],
        out_specs=_row_spec(CONV_TILE), out_shape=_sds((S, D), BF16),
        compiler_params=_cparams(1),
    )(z, z, z, z, z, cw)


def _conv_bwd(name, z, dpre, cw):
    nsteps = S // CONV_TILE

    def body(b_ref, c_ref, h_ref, cp_ref, hp_ref, dp_ref, dpn_ref, bn_ref, cw_ref, dz_ref, dcw_ref):
        i = pl.program_id(0)
        b = b_ref[...].astype(F32)
        c = c_ref[...].astype(F32)
        h = h_ref[...].astype(F32)
        dp = dp_ref[...].astype(F32)
        u = c * h
        up = jnp.where(i > 0, cp_ref[...].astype(F32) * hp_ref[...].astype(F32), 0.0)
        s1 = _shift_down(u, up, 1)
        s2 = _shift_down(u, up, 2)
        w0, w1, w2 = cw_ref[0:1, :], cw_ref[1:2, :], cw_ref[2:3, :]
        cv = w0 * s2 + w1 * s1 + w2 * u
        dcv = dp * b
        dcvn = jnp.where(i < nsteps - 1, dpn_ref[...].astype(F32) * bn_ref[...].astype(F32), 0.0)
        du = w2 * dcv + w1 * _shift_up(dcv, dcvn, 1) + w0 * _shift_up(dcv, dcvn, 2)
        dz_ref[:, 0:D] = (dp * cv).astype(dz_ref.dtype)
        dz_ref[:, D:2 * D] = (du * h).astype(dz_ref.dtype)
        dz_ref[:, 2 * D:3 * D] = (du * c).astype(dz_ref.dtype)

        @pl.when(i == 0)
        def _():
            dcw_ref[...] = jnp.zeros_like(dcw_ref)

        dcw_ref[0:1, :] += jnp.sum(dcv * s2, axis=0, keepdims=True)
        dcw_ref[1:2, :] += jnp.sum(dcv * s1, axis=0, keepdims=True)
        dcw_ref[2:3, :] += jnp.sum(dcv * u, axis=0, keepdims=True)

    col = lambda c: pl.BlockSpec((CONV_TILE, D), lambda i: (i, c))
    return pl.pallas_call(
        body, name=name, grid=(nsteps,),
        in_specs=[col(0), col(1), col(2), _halo_prev(1), _halo_prev(2),
                  _row_spec(CONV_TILE), _halo_next(0), _halo_next(0), _vec_spec(8)],
        out_specs=[pl.BlockSpec((CONV_TILE, 3 * D), lambda i: (i, 0)), _vec_spec(8)],
        out_shape=[_sds((S, 3 * D), BF16), _sds((8, D), F32)],
        compiler_params=_cparams(1),
    )(z, z, z, z, z, dpre, dpre, z, cw)


FFN_TM = 2048
_GU_BLOCK = pl.BlockSpec((2, None, FFN_TM, FB), lambda i, j: (0, j, i, 0))


def _gate_up_act(name, a, wg):
    kdim = a.shape[1]

    def body(a_ref, wgate_ref, wup_ref, gu_ref, act_ref):
        x = a_ref[...]
        g = _dot_nn(x, wgate_ref[...])
        u = _dot_nn(x, wup_ref[...])
        gu_ref[0] = g.astype(gu_ref.dtype)
        gu_ref[1] = u.astype(gu_ref.dtype)
        act_ref[...] = (g * jax.nn.sigmoid(g) * u).astype(act_ref.dtype)

    return pl.pallas_call(
        body, name=name, grid=(S // FFN_TM, NFB),
        in_specs=[pl.BlockSpec((FFN_TM, kdim), lambda i, j: (i, 0)),
                  pl.BlockSpec((None, kdim, FB), lambda i, j: (j, 0, 0)),
                  pl.BlockSpec((None, kdim, FB), lambda i, j: (j + NFB, 0, 0))],
        out_specs=[_GU_BLOCK, pl.BlockSpec((None, FFN_TM, FB), lambda i, j: (j, i, 0))],
        out_shape=[_sds((2, NFB, S, FB), BF16), _sds((NFB, S, FB), BF16)],
        compiler_params=_cparams(2),
    )(a, wg, wg)


def _down_dx_act_bwd(name, df, w4, gu):
    _, kb, n = w4.shape

    def body(df_ref, w_ref, gu_ref, o_ref):
        d = _dot_nt(df_ref[...], w_ref[...])
        g = gu_ref[0].astype(F32)
        u = gu_ref[1].astype(F32)
        sg = jax.nn.sigmoid(g)
        o_ref[0] = (d * u * sg * (1.0 + g * (1.0 - sg))).astype(o_ref.dtype)
        o_ref[1] = (d * g * sg).astype(o_ref.dtype)

    return pl.pallas_call(
        body, name=name, grid=(S // FFN_TM, NFB),
        in_specs=[pl.BlockSpec((FFN_TM, n), lambda i, j: (i, 0)), pl.BlockSpec((None, kb, n), lambda i, j: (j, 0, 0)),
                  _GU_BLOCK],
        out_specs=_GU_BLOCK, out_shape=_sds((2, NFB, S, FB), BF16),
        compiler_params=_cparams(2),
    )(df, w4, gu)


def _rope_tables(name, pos_col, inv_freq_row):
    def body(pos_ref, f_ref, cos_ref, sin_ref):
        ang = pos_ref[...].astype(F32) * f_ref[...]
        lane = lax.broadcasted_iota(jnp.int32, ang.shape, 1)
        s = jnp.sin(ang)
        cos_ref[...] = jnp.cos(ang)
        sin_ref[...] = jnp.where((lane % HEAD_DIM) < HEAD_DIM // 2, -s, s)

    tab = pl.BlockSpec((ROW_TILE, 128), lambda i: (i, 0))
    return pl.pallas_call(
        body, name=name, grid=(S // ROW_TILE,),
        in_specs=[pl.BlockSpec((ROW_TILE, 1), lambda i: (i, 0)), _vec_spec(1, 128)],
        out_specs=[tab, tab], out_shape=[_sds((S, 128), F32)] * 2,
        compiler_params=_cparams(1),
    )(pos_col, inv_freq_row)


def _swap_halves(t):
    lane = lax.broadcasted_iota(jnp.int32, t.shape, 1)
    first = (lane % HEAD_DIM) < HEAD_DIM // 2
    return jnp.where(first, pltpu.roll(t, 128 - HEAD_DIM // 2, 1), pltpu.roll(t, HEAD_DIM // 2, 1))


NCHUNK = D // 128


def _chunk(c, base=0):
    return slice(base + c * 128, base + (c + 1) * 128)


def _class_rows(r, d, tm):
    return pl.ds(r, tm // d, stride=d) if d > 1 else slice(None)


def _class_block(d, tm):
    return pl.BlockSpec((tm // d, d * D), lambda i: (i, 0))


def _tokens_from_classes(blk_ref, tmp_ref, d, tm):
    for r in range(d):
        for c in range(NCHUNK):
            tmp_ref[c, _class_rows(r, d, tm), :] = blk_ref[:, _chunk(c, r * D)].astype(F32)


def _classes_from_tokens(tmp_ref, blk_ref, d, tm):
    for r in range(d):
        for c in range(NCHUNK):
            blk_ref[:, _chunk(c, r * D)] = tmp_ref[c, _class_rows(r, d, tm), :].astype(blk_ref.dtype)


def _qkv_classes(name, n2, nk, wq, wkv, g, d, tables):
    def emit(acc, cos_ref, sin_ref, o_ref, tmp_ref, scale):
        for c in range(NCHUNK):
            tmp_ref[c] = acc[:, _chunk(c)]
        for r in range(d):
            rows = _class_rows(r, d, TM)
            if scale is not None:
                cs = cos_ref[rows, :]
                sn = sin_ref[rows, :]
            for c in range(NCHUNK):
                x = tmp_ref[c, rows, :]
                if scale is not None:
                    x = (x * cs + _swap_halves(x) * sn) * scale
                o_ref[:, _chunk(c, r * D)] = x.astype(o_ref.dtype)

    def body(n2_ref, nk_ref, wq_ref, wk_ref, wv_ref, cos_ref, sin_ref, q_ref, k_ref, v_ref, tmp_ref):
        emit(_dot_nn(n2_ref[...], wq_ref[...]), cos_ref, sin_ref, q_ref, tmp_ref, HEAD_DIM ** -0.5)
        x = nk_ref[...]
        emit(_dot_nn(x, wk_ref[...]), cos_ref, sin_ref, k_ref, tmp_ref, 1.0)
        emit(_dot_nn(x, wv_ref[...]), cos_ref, sin_ref, v_ref, tmp_ref, None)

    nbr = len(DILATIONS)
    act = pl.BlockSpec((TM, D), lambda i: (i, 0))
    tab = pl.BlockSpec((TM, 128), lambda i: (i, 0))
    wcol = lambda col: pl.BlockSpec((D, D), lambda i: (0, col))
    return pl.pallas_call(
        body, name=name, grid=(S // TM,),
        in_specs=[act, act, wcol(g), wcol(g), wcol(nbr + g), tab, tab],
        out_specs=[_class_block(d, TM)] * 3, out_shape=[_sds((S // d, d * D), BF16)] * 3,
        scratch_shapes=[pltpu.VMEM((NCHUNK, TM, 128), F32)],
        compiler_params=_cparams(1),
    )(n2, nk, wq, wkv, wkv, *tables)


ATTN_CHAINS = 16


def _attn_units(d):
    nblk = S // d // BAND
    return max(1, 2 * ATTN_CHAINS // nblk)


def _class_spec(d):
    return pl.BlockSpec((S // d, 128 * _attn_units(d)), lambda cb: (0, cb))


def _dot_nt(a, b):
    return lax.dot_general(a, b, _DIMS["nt"], preferred_element_type=F32)


def _dot_tn(a, b):
    return lax.dot_general(a, b, _DIMS["tn"], preferred_element_type=F32)


def _dot_nn(a, b):
    return lax.dot_general(a, b, _DIMS["nn"], preferred_element_type=F32)


def _band_mask(nkeys):
    qi = lax.broadcasted_iota(jnp.int32, (2 * BAND, nkeys), 0) % BAND
    kj = lax.broadcasted_iota(jnp.int32, (2 * BAND, nkeys), 1)
    if nkeys == BAND:
        return kj <= qi
    dist = qi + BAND - kj
    return (dist >= 0) & (dist <= BAND)


def _stack_heads(x):
    row = lax.broadcasted_iota(jnp.int32, (2 * BAND, 128), 0)
    lane = lax.broadcasted_iota(jnp.int32, (2 * BAND, 128), 1)
    keep = (row < BAND) == (lane < HEAD_DIM)
    return jnp.where(keep, jnp.concatenate([x, x], axis=0), jnp.zeros((), x.dtype))


def _unstack(x2):
    first_head = lax.broadcasted_iota(jnp.int32, (BAND, 128), 1) < HEAD_DIM
    return jnp.where(first_head, x2[:BAND], x2[BAND:])


def _for_later_blocks(nblk, units, fn):
    all_lanes = [slice(u * 128, (u + 1) * 128) for u in range(units)]
    unroll = max(1, ATTN_CHAINS // units)
    trips = (nblk - 1) // unroll
    if trips > 1:
        def step(i, carry):
            for j in range(unroll):
                for lanes in all_lanes:
                    fn(pl.multiple_of((1 + i * unroll + j) * BAND, BAND), lanes)
            return carry

        lax.fori_loop(0, trips, step, 0)
    else:
        trips = 0
    for sb in range(1 + trips * unroll, nblk):
        for lanes in all_lanes:
            fn(sb * BAND, lanes)


def _attn_fwd(name, q, k, v, d):
    nblk = S // d // BAND
    units = _attn_units(d)

    def body(q_ref, k_ref, v_ref, o_ref, lse_ref):
        def block(r0, k0, nkeys, lanes):
            q2 = _stack_heads(q_ref[pl.ds(r0, BAND), lanes])
            s = jnp.where(_band_mask(nkeys), _dot_nt(q2, k_ref[pl.ds(k0, nkeys), lanes]), NEG_INF)
            m = jnp.max(s, axis=-1, keepdims=True)
            p = jnp.exp(s - m)
            l = jnp.sum(p, axis=-1, keepdims=True)
            o2 = _dot_nn(p.astype(BF16), v_ref[pl.ds(k0, nkeys), lanes]) / l
            lse2 = jnp.broadcast_to(m + jnp.log(l), (2 * BAND, 128))
            o_ref[pl.ds(r0, BAND), lanes] = _unstack(o2).astype(o_ref.dtype)
            lse_ref[pl.ds(r0, BAND), lanes] = _unstack(lse2)

        for u in range(units):
            block(0, 0, BAND, slice(u * 128, (u + 1) * 128))

        _for_later_blocks(nblk, units, lambda r0, lanes: block(r0, r0 - BAND, 2 * BAND, lanes))

    spec = _class_spec(d)
    return pl.pallas_call(
        body, name=name, grid=(8 * d // units,),
        in_specs=[spec] * 3, out_specs=[spec] * 2,
        out_shape=[_sds((S // d, d * D), BF16), _sds((S // d, d * D), F32)],
        compiler_params=_cparams(1),
    )(q, k, v)


def _attn_bwd(name, q, k, v, do, lse, dd, d):
    nblk = S // d // BAND
    units = _attn_units(d)

    def body(q_ref, k_ref, v_ref, do_ref, lse_ref, dd_ref, dq_ref, dk_out, dv_out, dk_ref, dv_ref):
        def column(ref, r0, lanes, nkeys):
            tile = ref[pl.ds(r0, BAND), lanes]
            other = pltpu.roll(tile, HEAD_DIM, 1)
            first_head = lax.broadcasted_iota(jnp.int32, tile.shape, 1) < HEAD_DIM
            both = jnp.concatenate([jnp.where(first_head, tile, other), jnp.where(first_head, other, tile)], axis=0)
            return both if nkeys == BAND else jnp.concatenate([both, both], axis=1)

        def block(r0, k0, nkeys, lanes, first):
            q2 = _stack_heads(q_ref[pl.ds(r0, BAND), lanes])
            do2 = _stack_heads(do_ref[pl.ds(r0, BAND), lanes])
            kk = k_ref[pl.ds(k0, nkeys), lanes]
            vv = v_ref[pl.ds(k0, nkeys), lanes]
            s = jnp.where(_band_mask(nkeys), _dot_nt(q2, kk), NEG_INF)
            p = jnp.exp(s - column(lse_ref, r0, lanes, nkeys))
            ds = (p * (_dot_nt(do2, vv) - column(dd_ref, r0, lanes, nkeys))).astype(BF16)
            dq_ref[pl.ds(r0, BAND), lanes] = _unstack(_dot_nn(ds, kk)).astype(dq_ref.dtype)
            dk_part = _dot_tn(ds, q2)
            dv_part = _dot_tn(p.astype(BF16), do2)
            if first:
                dk_ref[pl.ds(k0, nkeys), lanes] = dk_part
                dv_ref[pl.ds(k0, nkeys), lanes] = dv_part
            else:
                dk_ref[pl.ds(k0, BAND), lanes] += dk_part[:BAND]
                dv_ref[pl.ds(k0, BAND), lanes] += dv_part[:BAND]
                dk_ref[pl.ds(k0 + BAND, BAND), lanes] = dk_part[BAND:]
                dv_ref[pl.ds(k0 + BAND, BAND), lanes] = dv_part[BAND:]

        for u in range(units):
            block(0, 0, BAND, slice(u * 128, (u + 1) * 128), True)

        _for_later_blocks(nblk, units, lambda r0, lanes: block(r0, r0 - BAND, 2 * BAND, lanes, False))
        dk_out[...] = dk_ref[...].astype(dk_out.dtype)
        dv_out[...] = dv_ref[...].astype(dv_out.dtype)

    spec = _class_spec(d)
    return pl.pallas_call(
        body, name=name, grid=(8 * d // units,),
        in_specs=[spec] * 6, out_specs=[spec] * 3,
        out_shape=[_sds((S // d, d * D), BF16)] * 3,
        scratch_shapes=[pltpu.VMEM((S // d, 128 * units), F32)] * 2,
        compiler_params=_cparams(1),
    )(q, k, v, do, lse, dd)


MIX_TILE = 256
DILATIONS = tuple(d for _, d in BRANCHES)


def _branch_weights(la, lb, lc):
    m = jnp.maximum(jnp.maximum(la, lb), lc)
    ea, eb, ec = jnp.exp(la - m), jnp.exp(lb - m), jnp.exp(lc - m)
    den = ea + eb + ec
    return ea / den, eb / den, ec / den


def _mix_operands(outs, lses):
    specs = [_class_block(d, MIX_TILE) for d in DILATIONS] * 2
    scratch = [pltpu.VMEM((NCHUNK, MIX_TILE, 128), F32)] * 4
    return list(outs) + list(lses), specs, scratch


def _mix_fwd(name, outs, lses):
    def body(o0, o1, o2, l0, l1, l2, o_ref, to1, to2, tl1, tl2):
        for blk, tmp, d in ((o1, to1, DILATIONS[1]), (o2, to2, DILATIONS[2]), (l1, tl1, DILATIONS[1]), (l2, tl2, DILATIONS[2])):
            _tokens_from_classes(blk, tmp, d, MIX_TILE)
        for c in range(NCHUNK):
            wa, wb, wc = _branch_weights(l0[:, _chunk(c)], tl1[c], tl2[c])
            o_ref[:, _chunk(c)] = (wa * o0[:, _chunk(c)].astype(F32) + wb * to1[c] + wc * to2[c]).astype(o_ref.dtype)

    operands, specs, scratch = _mix_operands(outs, lses)
    return pl.pallas_call(
        body, name=name, grid=(S // MIX_TILE,),
        in_specs=specs, out_specs=_row_spec(MIX_TILE), out_shape=_sds((S, D), BF16),
        scratch_shapes=scratch, compiler_params=_cparams(1),
    )(*operands)


def _head_sum(x, ones_blockdiag):
    hi = x.astype(BF16)
    r1 = x - hi.astype(F32)
    mid = r1.astype(BF16)
    lo = (r1 - mid.astype(F32)).astype(BF16)
    return _dot_nn(hi, ones_blockdiag) + _dot_nn(mid, ones_blockdiag) + _dot_nn(lo, ones_blockdiag)


def _mix_bwd(name, do, outs, lses, ones_blockdiag):
    def body(do_ref, o0, o1, o2, l0, l1, l2, ones_ref, d0, d1, d2, t0, t1, t2,
             to1, to2, tl1, tl2, td1, td2, tt1, tt2):
        for blk, tmp, d in ((o1, to1, DILATIONS[1]), (o2, to2, DILATIONS[2]), (l1, tl1, DILATIONS[1]), (l2, tl2, DILATIONS[2])):
            _tokens_from_classes(blk, tmp, d, MIX_TILE)
        ones = ones_ref[...]
        for c in range(NCHUNK):
            w = _branch_weights(l0[:, _chunk(c)], tl1[c], tl2[c])
            dov = do_ref[:, _chunk(c)]
            o = w[0] * o0[:, _chunk(c)].astype(F32) + w[1] * to1[c] + w[2] * to2[c]
            t = _head_sum(dov * o, ones)
            d0[:, _chunk(c)] = (w[0] * dov).astype(d0.dtype)
            t0[:, _chunk(c)] = w[0] * t
            td1[c], tt1[c] = w[1] * dov, w[1] * t
            td2[c], tt2[c] = w[2] * dov, w[2] * t
        for tmp, blk, d in ((td1, d1, DILATIONS[1]), (tt1, t1, DILATIONS[1]), (td2, d2, DILATIONS[2]), (tt2, t2, DILATIONS[2])):
            _classes_from_tokens(tmp, blk, d, MIX_TILE)

    operands, specs, scratch = _mix_operands(outs, lses)
    out_specs = [_class_block(d, MIX_TILE) for d in DILATIONS] * 2
    out_shape = [_sds((S // d, d * D), BF16) for d in DILATIONS] + [_sds((S // d, d * D), F32) for d in DILATIONS]
    return pl.pallas_call(
        body, name=name, grid=(S // MIX_TILE,),
        in_specs=[_row_spec(MIX_TILE)] + specs + [_vec_spec(128, 128)],
        out_specs=out_specs, out_shape=out_shape,
        scratch_shapes=scratch + [pltpu.VMEM((NCHUNK, MIX_TILE, 128), F32)] * 4,
        compiler_params=_cparams(1),
    )(do, *operands, ones_blockdiag)


def _attn_bwd_post(name, grads, cos_t, sin_t):
    tm = MIX_TILE
    scale = HEAD_DIM ** -0.5

    def unrope(x, cs, sn):
        return x * cs - _swap_halves(x) * sn

    def body(*refs):
        in_refs = refs[:9]
        cos_ref, sin_ref, dq_ref, dkv_ref, tmp_ref = refs[9:]
        cs = cos_ref[...]
        sn = sin_ref[...]
        for g, d in enumerate(DILATIONS):
            for which, blk in enumerate(in_refs[3 * g:3 * g + 3]):
                if d > 1:
                    _tokens_from_classes(blk, tmp_ref, d, tm)
                for c in range(NCHUNK):
                    x = tmp_ref[c] if d > 1 else blk[:, _chunk(c)].astype(F32)
                    if which == 0:
                        dq_ref[:, _chunk(c, g * D)] = (unrope(x, cs, sn) * scale).astype(dq_ref.dtype)
                    elif which == 1:
                        dkv_ref[:, _chunk(c, g * D)] = unrope(x, cs, sn).astype(dkv_ref.dtype)
                    else:
                        dkv_ref[:, _chunk(c, QW + g * D)] = x.astype(dkv_ref.dtype)

    operands = [a for branch in grads for a in branch]
    tab = pl.BlockSpec((tm, 128), lambda i: (i, 0))
    return pl.pallas_call(
        body, name=name, grid=(S // tm,),
        in_specs=[_class_block(d, tm) for d in DILATIONS for _ in range(3)] + [tab, tab],
        out_specs=[pl.BlockSpec((tm, QW), lambda i: (i, 0)), pl.BlockSpec((tm, 2 * QW), lambda i: (i, 0))],
        out_shape=[_sds((S, QW), BF16), _sds((S, 2 * QW), BF16)],
        scratch_shapes=[pltpu.VMEM((NCHUNK, tm, 128), F32)],
        compiler_params=_cparams(1),
    )(*operands, cos_t, sin_t)


def _adamw(name, parts, w, m, v, layer=None, other=None):
    n, rows, cols = parts.shape
    tr = rows
    for cand in (256, 176, 128, 64, 32, 16, 8):
        if rows % cand == 0:
            tr = cand
            break
    n_other = 0 if other is None else len(other)

    def body(p_ref, w_ref, m_ref, v_ref, *refs):
        g_ref, d_ref, nm_ref, nv_ref = refs[n_other:]
        g = p_ref[0].astype(F32)
        for j in range(1, n):
            g = g + p_ref[j].astype(F32)
        g_ref[...] = g
        d_ref[...], nm_ref[...], nv_ref[...] = _adam_update(g, w_ref[...], m_ref[...], v_ref[...])

    if layer is None:
        blk = pl.BlockSpec((tr, cols), lambda i: (i, 0))
        shape = (rows, cols)
    else:
        blk = pl.BlockSpec((None, tr, cols), lambda i: (layer, i, 0))
        shape = w.shape
    return pl.pallas_call(
        body, name=name, grid=(rows // tr,),
        in_specs=[pl.BlockSpec((n, tr, cols), lambda i: (0, i, 0)), blk, blk, blk]
                 + [pl.BlockSpec(memory_space=pl.ANY)] * n_other,
        out_specs=[blk] * 4, out_shape=[_sds(shape, F32)] * 4,
        input_output_aliases={4 + i: i for i in range(n_other)},
        compiler_params=_cparams(1),
    )(parts, w, m, v, *(other or ()))


def _adam_update(g, w, m, v):
    c1 = 1.0 / (1.0 - ADAM_B1 ** ADAM_STEP)
    c2 = 1.0 / (1.0 - ADAM_B2 ** ADAM_STEP)
    nm = ADAM_B1 * m + (1.0 - ADAM_B1) * g
    nv = ADAM_B2 * v + (1.0 - ADAM_B2) * (g * g)
    return -ADAM_LR * ((nm * c1) / (jnp.sqrt(nv * c2) + ADAM_EPS) + ADAM_WD * w), nm, nv


GAIN_ROWS = 16


def _pack_small(name, gain_tiles, taps, sq):
    ng = len(gain_tiles)

    def body(*refs):
        o_ref = refs[-1]
        o_ref[...] = jnp.zeros_like(o_ref)
        for i in range(ng):
            o_ref[i:i + 1, :] = refs[i][0:1, :]
        o_ref[ng:ng + 3, :] = refs[ng][0:3, :]
        o_ref[ng + 3:ng + 4, :] = refs[ng + 1][...]

    return pl.pallas_call(body, name=name, out_shape=_sds((GAIN_ROWS, D), F32))(*gain_tiles, taps, sq)


def _adamw_gains(name, parts, params):
    np_ = len(params)
    shapes = [w.shape for w, _, _ in params]

    def body(p_ref, *refs):
        ins, outs = refs[:3 * np_], refs[3 * np_:]

        def total(lo, rows):
            g = p_ref[0, lo:lo + rows, :]
            for j in range(1, NDEV):
                g = g + p_ref[j, lo:lo + rows, :]
            return g

        lo = 0
        for i, shape in enumerate(shapes):
            g = total(lo, shape[0])
            lo += shape[0]
            w_ref, m_ref, v_ref = ins[3 * i:3 * i + 3]
            g_ref, d_ref, nm_ref, nv_ref = outs[4 * i:4 * i + 4]
            g_ref[...] = g
            d_ref[...], nm_ref[...], nv_ref[...] = _adam_update(g, w_ref[...], m_ref[...], v_ref[...])
        taps_ref, loss_ref = outs[-2], outs[-1]
        taps_ref[...] = jnp.zeros_like(taps_ref)
        taps_ref[0:3, :] = total(lo, 3)
        loss_ref[...] = jnp.sum(total(lo + 3, 1), axis=-1, keepdims=True) * (0.5 / D)

    out_shape = [_sds(shape, F32) for shape in shapes for _ in range(4)] + [_sds((8, D), F32), _sds((1, 1), F32)]
    outs = pl.pallas_call(body, name=name, out_shape=out_shape)(parts, *[a for p in params for a in p])
    return [list(outs[4 * i:4 * i + 4]) for i in range(np_)], outs[-2], outs[-1].reshape(())


def _exchange(name, arrays, kind, after):
    n = len(arrays)
    gather = kind == "gather"
    out_shape = [_sds((NDEV,) + a.shape if gather else a.shape, a.dtype) for a in arrays]

    def body(*refs):
        srcs, outs = refs[:n], refs[n + 1:2 * n + 1]
        send_sems, recv_sems, local_sems = refs[2 * n + 1:]
        x, y, c = lax.axis_index("x"), lax.axis_index("y"), lax.axis_index("c")
        me = 4 * x + 2 * y + c
        pending = []
        for t in range(n):
            own = pltpu.make_async_copy(srcs[t] if gather else srcs[t].at[me], outs[t].at[me], local_sems.at[t])
            own.start()
            pending.append(own)
            for rel in range(1, NDEV):
                px = 1 - x if rel & 4 else x
                py = 1 - y if rel & 2 else y
                pc = 1 - c if rel & 1 else c
                peer = 4 * px + 2 * py + pc
                send = pltpu.make_async_remote_copy(
                    src_ref=srcs[t] if gather else srcs[t].at[peer], dst_ref=outs[t].at[me],
                    send_sem=send_sems.at[t, rel - 1], recv_sem=recv_sems.at[t, rel - 1],
                    device_id=(px, py, pc), device_id_type=MESH)
                send.start()
                arrive = pltpu.make_async_remote_copy(
                    src_ref=srcs[t] if gather else srcs[t].at[me], dst_ref=outs[t].at[peer],
                    send_sem=send_sems.at[t, rel - 1], recv_sem=recv_sems.at[t, rel - 1],
                    device_id=(px, py, pc), device_id_type=MESH)
                pending.append((send, arrive))
        for item in pending:
            if isinstance(item, tuple):
                item[0].wait_send()
                item[1].wait_recv()
            else:
                item.wait()

    any_spec = pl.BlockSpec(memory_space=pl.ANY)
    outs = pl.pallas_call(
        body, name=name,
        in_specs=[any_spec] * (n + 1), out_specs=[any_spec] * n, out_shape=out_shape,
        scratch_shapes=[pltpu.SemaphoreType.DMA((n, NDEV - 1)), pltpu.SemaphoreType.DMA((n, NDEV - 1)),
                        pltpu.SemaphoreType.DMA((n,))],
    )(*arrays, after)
    return list(outs)


_HBM_SPEC = pl.BlockSpec(memory_space=pltpu.HBM)
_SEM_SPEC = pl.BlockSpec(memory_space=pltpu.SEMAPHORE)
_DATAFLOW = pltpu.SideEffectType.DATAFLOW_SIDE_EFFECTING


def _peers():
    x, y, c = lax.axis_index("x"), lax.axis_index("y"), lax.axis_index("c")
    out = []
    for rel in range(1, NDEV):
        px = 1 - x if rel & 4 else x
        py = 1 - y if rel & 2 else y
        pc = 1 - c if rel & 1 else c
        out.append((rel - 1, (px, py, pc), 4 * px + 2 * py + pc))
    return 4 * x + 2 * y + c, out


def _hbm(a):
    return pltpu.HBM(a.shape, a.dtype)


def _own_slot(a, me, kind):
    mine = a[None] if kind == "gather" else lax.dynamic_slice_in_dim(a, me, 1, axis=0)
    shape = (NDEV,) + mine.shape[1:]
    return lax.dynamic_update_slice_in_dim(lax.empty(shape, a.dtype), mine, me, axis=0)


def _exchange_start(name, arrays, me, kind):
    n = len(arrays)
    gather = kind == "gather"
    lands = [_own_slot(a, me, kind) for a in arrays]

    def body(*refs):
        src_refs, land_refs = refs[:n], refs[n:2 * n]
        send_sems, recv_sems = refs[2 * n], refs[2 * n + 1]
        token = refs[-1]
        my_block, peers = _peers()
        for t in range(n):
            for slot, dev, block in peers:
                pltpu.make_async_remote_copy(
                    src_ref=src_refs[t] if gather else src_refs[t].at[block], dst_ref=land_refs[t].at[my_block],
                    send_sem=send_sems.at[t * (NDEV - 1) + slot], recv_sem=recv_sems.at[t * (NDEV - 1) + slot],
                    device_id=dev, device_id_type=MESH).start()
        token[...] = jnp.zeros_like(token)

    operands = [pltpu.with_memory_space_constraint(a, pltpu.HBM) for a in list(arrays) + lands]
    outs = pl.pallas_call(
        body, name=name,
        out_shape=(pltpu.SemaphoreType.DMA((n * (NDEV - 1),)), pltpu.SemaphoreType.DMA((n * (NDEV - 1),)),
                   *[_hbm(a) for a in operands], _sds((8, 128), F32)),
        in_specs=[_HBM_SPEC] * (2 * n),
        out_specs=(_SEM_SPEC, _SEM_SPEC, *[_HBM_SPEC] * (2 * n), pl.BlockSpec(memory_space=pltpu.VMEM)),
        input_output_aliases={i: 2 + i for i in range(2 * n)},
        compiler_params=pltpu.CompilerParams(has_side_effects=_DATAFLOW),
    )(*operands)
    return (outs[0], outs[1], list(outs[2:2 + n]), list(outs[2 + n:2 + 2 * n])), outs[-1]


def _exchange_wait(name, started, t, after, kind):
    send_sems, recv_sems, srcs, lands = started
    gather = kind == "gather"

    def body(src_ref, land_ref, send_ref, recv_ref, after_ref, src_out, land_out):
        _, peers = _peers()
        for slot, dev, block in peers:
            copy = pltpu.make_async_remote_copy(
                src_ref=src_ref if gather else src_ref.at[block], dst_ref=land_ref.at[block],
                send_sem=send_ref.at[t * (NDEV - 1) + slot], recv_sem=recv_ref.at[t * (NDEV - 1) + slot],
                device_id=dev, device_id_type=MESH)
            copy.wait_send()
            copy.wait_recv()

    return pl.pallas_call(
        body, name=name, out_shape=(_hbm(srcs[t]), _hbm(lands[t])),
        in_specs=(_HBM_SPEC, _HBM_SPEC, _SEM_SPEC, _SEM_SPEC, pl.BlockSpec(memory_space=pl.ANY)),
        out_specs=(_HBM_SPEC, _HBM_SPEC), input_output_aliases={0: 0, 1: 1},
        compiler_params=pltpu.CompilerParams(has_side_effects=_DATAFLOW),
    )(srcs[t], lands[t], send_sems, recv_sems, after)[1]


DIRECT_RELS = (1, 2, 4, 6)
RELAY_RELS = (2, 4, 6)


def _rel_peer(rel):
    x, y, c = lax.axis_index("x"), lax.axis_index("y"), lax.axis_index("c")
    px = 1 - x if rel & 4 else x
    py = 1 - y if rel & 2 else y
    pc = 1 - c if rel & 1 else c
    return (px, py, pc), 4 * px + 2 * py + pc


def _gather_start(name, shards, me):
    n, nr = len(shards), len(DIRECT_RELS)
    lands = [_own_slot(a, me, "gather") for a in shards]

    def body(*refs):
        src_refs, land_refs = refs[:n], refs[n:2 * n]
        send_sems, recv_sems = refs[2 * n], refs[2 * n + 1]
        _, my_block = _rel_peer(0)
        for t in range(n):
            for s, rel in enumerate(DIRECT_RELS):
                dev, _ = _rel_peer(rel)
                pltpu.make_async_remote_copy(
                    src_ref=src_refs[t], dst_ref=land_refs[t].at[my_block],
                    send_sem=send_sems.at[t * nr + s], recv_sem=recv_sems.at[t * nr + s],
                    device_id=dev, device_id_type=MESH).start()

    operands = [pltpu.with_memory_space_constraint(a, pltpu.HBM) for a in list(shards) + lands]
    outs = pl.pallas_call(
        body, name=name,
        out_shape=(pltpu.SemaphoreType.DMA((n * nr,)), pltpu.SemaphoreType.DMA((n * nr,)), *[_hbm(a) for a in operands]),
        in_specs=[_HBM_SPEC] * (2 * n), out_specs=(_SEM_SPEC, _SEM_SPEC, *[_HBM_SPEC] * (2 * n)),
        input_output_aliases={i: 2 + i for i in range(2 * n)},
        compiler_params=pltpu.CompilerParams(has_side_effects=_DATAFLOW),
    )(*operands)
    return outs[0], outs[1], list(outs[2:2 + n]), list(outs[2 + n:2 + 2 * n])


def _gather_wait(name, started, ts, after):
    send_sems, recv_sems, srcs, lands = started
    m, nr = len(ts), len(DIRECT_RELS)

    def body(*refs):
        src_refs, land_refs = refs[:m], refs[m:2 * m]
        send_ref, recv_ref = refs[2 * m], refs[2 * m + 1]
        for i, t in enumerate(ts):
            for s, rel in enumerate(DIRECT_RELS):
                dev, block = _rel_peer(rel)
                copy = pltpu.make_async_remote_copy(
                    src_ref=src_refs[i], dst_ref=land_refs[i].at[block],
                    send_sem=send_ref.at[t * nr + s], recv_sem=recv_ref.at[t * nr + s],
                    device_id=dev, device_id_type=MESH)
                copy.wait_send()
                copy.wait_recv()

    operands = [srcs[t] for t in ts] + [lands[t] for t in ts]
    outs = pl.pallas_call(
        body, name=name, out_shape=tuple(_hbm(a) for a in operands),
        in_specs=[_HBM_SPEC] * (2 * m) + [_SEM_SPEC, _SEM_SPEC, pl.BlockSpec(memory_space=pl.ANY)],
        out_specs=tuple([_HBM_SPEC] * (2 * m)), input_output_aliases={i: i for i in range(2 * m)},
        compiler_params=pltpu.CompilerParams(has_side_effects=_DATAFLOW),
    )(*operands, send_sems, recv_sems, after)
    return list(outs[m:])


def _relay_start(name, lands):
    m, nr = len(lands), len(RELAY_RELS)

    def body(*refs):
        land_refs, send_sems, recv_sems = refs[:m], refs[m], refs[m + 1]
        sibling, _ = _rel_peer(1)
        for i in range(m):
            for s, rel in enumerate(RELAY_RELS):
                _, block = _rel_peer(rel)
                pltpu.make_async_remote_copy(
                    src_ref=land_refs[i].at[block], dst_ref=land_refs[i].at[block],
                    send_sem=send_sems.at[i * nr + s], recv_sem=recv_sems.at[i * nr + s],
                    device_id=sibling, device_id_type=MESH).start()

    outs = pl.pallas_call(
        body, name=name,
        out_shape=(pltpu.SemaphoreType.DMA((m * nr,)), pltpu.SemaphoreType.DMA((m * nr,)), *[_hbm(a) for a in lands]),
        in_specs=[_HBM_SPEC] * m, out_specs=(_SEM_SPEC, _SEM_SPEC, *[_HBM_SPEC] * m),
        input_output_aliases={i: 2 + i for i in range(m)},
        compiler_params=pltpu.CompilerParams(has_side_effects=_DATAFLOW),
    )(*lands)
    return outs[0], outs[1], list(outs[2:])


def _relay_wait(name, relayed, after):
    send_sems, recv_sems, lands = relayed
    m, nr = len(lands), len(RELAY_RELS)

    def body(*refs):
        land_refs, send_ref, recv_ref = refs[:m], refs[m], refs[m + 1]
        sibling, _ = _rel_peer(1)
        for i in range(m):
            for s, rel in enumerate(RELAY_RELS):
                _, sent = _rel_peer(rel)
                _, arriving = _rel_peer(rel ^ 1)
                copy = pltpu.make_async_remote_copy(
                    src_ref=land_refs[i].at[sent], dst_ref=land_refs[i].at[arriving],
                    send_sem=send_ref.at[i * nr + s], recv_sem=recv_ref.at[i * nr + s],
                    device_id=sibling, device_id_type=MESH)
                copy.wait_send()
                copy.wait_recv()

    outs = pl.pallas_call(
        body, name=name, out_shape=tuple(_hbm(a) for a in lands),
        in_specs=[_HBM_SPEC] * m + [_SEM_SPEC, _SEM_SPEC, pl.BlockSpec(memory_space=pl.ANY)],
        out_specs=tuple([_HBM_SPEC] * m), input_output_aliases={i: i for i in range(m)},
        compiler_params=pltpu.CompilerParams(has_side_effects=_DATAFLOW),
    )(*lands, send_sems, recv_sems, after)
    return list(outs)


def _ffn_fwd(tag, n, wg, wd):
    gu, act = _gate_up_act(f"ffn_gate_up_{tag}", n, wg)
    wd4 = wd.reshape(NFB, FB, D)
    f = _fwd_kblocked(f"ffn_down_{tag}", act, wd4)
    return (n, gu, act, wg, wd4), f


def _ffn_bwd(tag, dh_out, df, h_in, saved, g_pre, send, mixer):
    n, gu, act, wg, wd4 = saved
    dwd = _bwd_w_kblocked(f"ffn_down_dw_{tag}", act, df).reshape(NDEV, DFF // NDEV, D)
    dgu = _down_dx_act_bwd(f"ffn_down_dx_{tag}", df, wd4, gu).reshape(NDEV, S, FB)
    tok = send({f"down_{tag}": dwd, f"gate_up_{tag}": _bwd_w_cols_blocked(f"ffn_gate_up_dw_{tag}", n, dgu)})
    dn = _bwd_x_cols_blocked(f"ffn_gate_up_dx_{tag}", dgu, wg, after=tok)
    dh_in, (dg_pre,), dy, dg_mixer = _rms_bwd(f"ffn_prenorm_bwd_{tag}", h_in, [(g_pre, dn)], dh_out, F32, then=mixer)
    return dh_in, dg_pre, dy, dg_mixer


def kernel(x, positions, mix_norm_pre, mix_norm_post, ffn_norm_pre, ffn_norm_post, ffn_w_gate_up, ffn_w_down, conv_w_in, conv_w, conv_w_out, kv_norm, w_kv, w_q, w_o, loss_target, m_mix_norm_pre, m_mix_norm_post, m_ffn_norm_pre, m_ffn_norm_post, m_ffn_w_gate_up, m_ffn_w_down, m_conv_w_in, m_conv_w, m_conv_w_out, m_kv_norm, m_w_kv, m_w_q, m_w_o, v_mix_norm_pre, v_mix_norm_post, v_ffn_norm_pre, v_ffn_norm_post, v_ffn_w_gate_up, v_ffn_w_down, v_conv_w_in, v_conv_w, v_conv_w_out, v_kv_norm, v_w_kv, v_w_q, v_w_o):
    me = 4 * lax.axis_index("x") + 2 * lax.axis_index("y") + lax.axis_index("c")
    h0 = x.reshape(S, D)
    target = loss_target.reshape(S, D)
    row = lambda a, l: a[l].reshape(1, D)
    g_kv = kv_norm.reshape(1, D)

    cw_shard = jnp.pad(conv_w[0], ((0, 5), (0, 0)))
    names = ["conv_in", "conv_w", "conv_out", "gate_up_0", "down_0", "kv", "q", "o", "gate_up_1", "down_1"]
    shards = [conv_w_in[0], cw_shard, conv_w_out[0], ffn_w_gate_up[0], ffn_w_down[0],
              w_kv, w_q[0], w_o[0], ffn_w_gate_up[1], ffn_w_down[1]]
    shards = [s if n == "conv_w" else s.astype(BF16) for n, s in zip(names, shards)]
    first = 3
    gather_first = _gather_start("gather_start_conv", shards[:first], me)
    gather_rest = _gather_start("gather_start_rest", shards[first:], me)

    def direct(group, after):
        ts = [names.index(n) for n in group]
        started, ts = (gather_first, ts) if ts[0] < first else (gather_rest, [t - first for t in ts])
        lands = _gather_wait(f"gather_wait_{group[0]}", started, ts, after)
        return _relay_start(f"relay_start_{group[0]}", lands)

    def finish(group, relayed, after):
        return dict(zip(group, _relay_wait(f"relay_wait_{group[0]}", relayed, after)))

    sent = {}

    def send(grads):
        started, token = _exchange_start(f"scatter_start_{next(iter(grads))}", list(grads.values()), me, "scatter")
        for i, name in enumerate(grads):
            sent[name] = (started, i)
        return token

    groups = [["conv_in", "conv_w", "conv_out"], ["gate_up_0", "down_0"], ["kv", "q"], ["o", "gate_up_1", "down_1"]]
    n0 = _rms_fwd("mix_prenorm_0", h0, [row(mix_norm_pre, 0)])[0]
    half = HEAD_DIM // 2
    inv_freq = ROPE_THETA ** (-jnp.arange(half, dtype=F32) / half)
    tables = _rope_tables("rope_tables", positions.reshape(S, 1), jnp.tile(inv_freq, 4).reshape(1, 128))
    w = finish(groups[0], direct(groups[0], tables[0]), n0)
    win = w["conv_in"].transpose(1, 0, 2).reshape(D, 3 * D)
    cw = w["conv_w"].transpose(1, 0, 2).reshape(8, D)
    wout = w["conv_out"].reshape(D, D)
    z = _fwd_rows("conv_in", n0, win, BF16)
    pre = _conv_fwd("conv_gate", z, cw)
    relayed = direct(groups[1], pre)
    y0 = _fwd_rows("conv_out", pre, wout)
    h1, (n1,) = _resid_rms("mix_postnorm_0", h0, y0, row(mix_norm_post, 0), [row(ffn_norm_pre, 0)])
    w = finish(groups[1], relayed, n1)
    ffn0, f0 = _ffn_fwd("0", n1, w["gate_up_0"], w["down_0"])
    relayed = direct(groups[2], ffn0[2])
    h2, (nk, n2) = _resid_rms("ffn_postnorm_0", h1, f0, row(ffn_norm_post, 0), [g_kv, row(mix_norm_pre, 1)])

    w = finish(groups[2], relayed, nk)
    wkv = w["kv"].transpose(1, 0, 2).reshape(D, 2 * QW)
    wq = w["q"].transpose(1, 0, 2).reshape(D, QW)
    qc, kc, vc, o_c, lse_c = [], [], [], [], []
    for g, d in enumerate(DILATIONS):
        q_g, k_g, v_g = _qkv_classes(f"qkv_proj_{g}", n2, nk, wq, wkv, g, d, tables)
        qc.append(q_g)
        kc.append(k_g)
        vc.append(v_g)
    relayed = direct(groups[3], vc[-1])
    for g, d in enumerate(DILATIONS):
        o_g, lse_g = _attn_fwd(f"attn_fwd_{g}", qc[g], kc[g], vc[g], d)
        o_c.append(o_g)
        lse_c.append(lse_g)
    o_mix = _mix_fwd("attn_mix", o_c, lse_c)
    w = finish(groups[3], relayed, o_mix)
    wo = w["o"].reshape(D, D)
    y1 = _fwd_rows("attn_out", o_mix, wo)
    h3, (n3,) = _resid_rms("mix_postnorm_1", h2, y1, row(mix_norm_post, 1), [row(ffn_norm_pre, 1)])
    ffn1, f1 = _ffn_fwd("1", n3, w["gate_up_1"], w["down_1"])

    dh4, df1, dg_fpost1, sq = _resid_rms_loss("ffn_postnorm_1_loss", h3, f1, row(ffn_norm_post, 1), target)

    dh3, dg_fpre1, dy1, dg_mpost1 = _ffn_bwd(
        "1", dh4, df1, h3, ffn1, row(ffn_norm_pre, 1), send, (y1, row(mix_norm_post, 1)))
    dwo = _bwd_w_rows("attn_out_dw", o_mix, dy1).reshape(NDEV, D // NDEV, D)
    do = _bwd_x_rows("attn_out_dx", dy1, wo, BF16)
    lane = jnp.arange(128)
    ones_blockdiag = (lane[:, None] // HEAD_DIM == lane[None, :] // HEAD_DIM).astype(BF16)
    mixed = _mix_bwd("attn_mix_bwd", do, o_c, lse_c, ones_blockdiag)
    branch_grads = [_attn_bwd(f"attn_bwd_{g}", qc[g], kc[g], vc[g], mixed[g], lse_c[g], mixed[3 + g], d)
                    for g, d in enumerate(DILATIONS)]
    dq_raw, dkv = _attn_bwd_post("attn_bwd_post", branch_grads, *tables)
    tok = send({"o": dwo, "kv": _bwd_w_cols("kv_proj_dw", nk, dkv, 2 * QW // NDEV),
                "q": _bwd_w_cols("q_proj_dw", n2, dq_raw, QW // NDEV)})
    dnk = _bwd_x_plain("kv_proj_dx", dkv, wkv, after=tok)
    dn2 = _bwd_x_plain("q_proj_dx", dq_raw, wq)
    dh2, (dg_kv, dg_mpre1), df0, dg_fpost0 = _rms_bwd(
        "kv_and_mix_prenorm_bwd_1", h2, [(g_kv, dnk), (row(mix_norm_pre, 1), dn2)], dh3, F32,
        then=(f0, row(ffn_norm_post, 0)))

    dh1, dg_fpre0, dy0, dg_mpost0 = _ffn_bwd(
        "0", dh2, df0, h1, ffn0, row(ffn_norm_pre, 0), send, (y0, row(mix_norm_post, 0)))
    dwout = _bwd_w_rows("conv_out_dw", pre, dy0).reshape(NDEV, D // NDEV, D)
    dpre = _bwd_x_rows("conv_out_dx", dy0, wout, BF16)
    dz, dcw = _conv_bwd("conv_gate_bwd", z, dpre, cw)
    tok = send({"conv_out": dwout, "conv_in": _bwd_w_cols("conv_in_dw", n0, dz, 3 * D // NDEV)})
    dn0 = _bwd_x_plain("conv_in_dx", dz, win, after=tok)
    dh0, (dg_mpre0,) = _rms_bwd("mix_prenorm_bwd_0", h0, [(row(mix_norm_pre, 0), dn0)], dh1, F32)

    small = _pack_small("pack_small_grads", [dg_mpre0, dg_mpre1, dg_mpost0, dg_mpost1, dg_fpre0, dg_fpre1,
                                             dg_fpost0, dg_fpost1, dg_kv], dcw, sq)

    done = [small]

    def upd(tag, w, m, v):
        parts = _exchange_wait(f"scatter_wait_{tag}", *sent[tag], done[-1], "scatter")
        shape = w.shape
        flat = lambda a: a.reshape(parts.shape[1:])
        res = _adamw(f"adamw_{tag}", parts, flat(w), flat(m), flat(v))
        done.append(res[0])
        return [r.reshape(shape) for r in res]

    def upd_layer(tag, l, w, m, v, other):
        parts = _exchange_wait(f"scatter_wait_{tag}_{l}", *sent[f"{tag}_{l}"], done[-1], "scatter")
        res = _adamw(f"adamw_{tag}_{l}", parts, w, m, v, layer=l, other=other)
        done.append(res[0])
        return list(res)

    res = {}
    down_1 = upd_layer("down", 1, ffn_w_down, m_ffn_w_down, v_ffn_w_down, None)
    gate_up_t = [jnp.swapaxes(a, 1, 2) for a in (ffn_w_gate_up, m_ffn_w_gate_up, v_ffn_w_gate_up)]
    gate_up_1 = upd_layer("gate_up", 1, *gate_up_t, None)
    res["w_o"] = upd("o", w_o, m_w_o, v_w_o)
    res["w_q"] = upd("q", w_q, m_w_q, v_w_q)
    res["w_kv"] = upd("kv", w_kv, m_w_kv, v_w_kv)

    small_all = _exchange("gather_small_grads", [small], "gather", done[-1])[0]
    vec = lambda a: a.reshape(1, D)
    gain_res, taps, loss = _adamw_gains("adamw_gains", small_all, [
        (mix_norm_pre, m_mix_norm_pre, v_mix_norm_pre), (mix_norm_post, m_mix_norm_post, v_mix_norm_post),
        (ffn_norm_pre, m_ffn_norm_pre, v_ffn_norm_pre), (ffn_norm_post, m_ffn_norm_post, v_ffn_norm_post),
        (vec(kv_norm), vec(m_kv_norm), vec(v_kv_norm))])
    dcw_mine = lax.dynamic_slice(taps, (0, me * 128), (8, 128))
    pad8 = lambda a, fill: jnp.pad(a[0], ((0, 5), (0, 0)), constant_values=fill)
    cw_res = [r[0:3].reshape(1, 3, 128) for r in
              _adamw("adamw_conv_w", dcw_mine.reshape(1, 8, 128), cw_shard, pad8(m_conv_w, 0.0), pad8(v_conv_w, 1.0))]

    res.update({
        "mix_norm_pre": gain_res[0],
        "mix_norm_post": gain_res[1],
        "ffn_norm_pre": gain_res[2],
        "ffn_norm_post": gain_res[3],
        "kv_norm": [r.reshape(D) for r in gain_res[4]],
        "conv_w": cw_res,
    })
    done.append(small_all)
    res["ffn_w_down"] = upd_layer("down", 0, ffn_w_down, m_ffn_w_down, v_ffn_w_down, down_1)
    res["ffn_w_gate_up"] = [jnp.swapaxes(r, 1, 2) for r in upd_layer("gate_up", 0, *gate_up_t, gate_up_1)]
    res["conv_w_out"] = upd("conv_out", conv_w_out, m_conv_w_out, v_conv_w_out)
    res["conv_w_in"] = upd("conv_in", conv_w_in, m_conv_w_in, v_conv_w_in)
    order = ["mix_norm_pre", "mix_norm_post", "ffn_norm_pre", "ffn_norm_post", "ffn_w_gate_up", "ffn_w_down",
             "conv_w_in", "conv_w", "conv_w_out", "kv_norm", "w_kv", "w_q", "w_o"]
    out = [loss, dh0.reshape(1, S, D)]
    for i in range(4):
        out += [res[name][i] for name in order]
    return tuple(out)
```

```python
import jax
import jax.numpy as jnp
from jax import lax
from jax.experimental import pallas as pl
from jax.experimental.pallas import tpu as pltpu

F32 = jnp.float32
BF16 = jnp.bfloat16

S = 4096
D = 1024
NDEV = 8
HEAD_DIM = 64
QW = 3072
DFF = 2816
FB = 704
NFB = 4
BRANCHES = ((128, 1), (512, 4), (2048, 16))
BAND = 128
ROPE_THETA = 10000.0
RMS_EPS = 1e-6
NEG_INF = -1e30
ADAM_LR, ADAM_B1, ADAM_B2, ADAM_EPS, ADAM_WD, ADAM_STEP = 0.001, 0.9, 0.999, 1e-08, 0.01, 10

VMEM_LIMIT_BYTES = 52 * 1024 * 1024
ROW_TILE = 512
MESH = pl.DeviceIdType.MESH


def _cparams(ngrid):
    return pltpu.CompilerParams(dimension_semantics=("arbitrary",) * ngrid,
                                vmem_limit_bytes=VMEM_LIMIT_BYTES)


def _sds(shape, dtype):
    return jax.ShapeDtypeStruct(tuple(shape), dtype)


_DIMS = {"nn": (((1,), (0,)), ((), ())),
         "nt": (((1,), (1,)), ((), ())),
         "tn": (((0,), (0,)), ((), ()))}


def _matmul(name, a, b, *, mode, grid, a_blk, a_map, b_blk, b_map, o_shape, o_blk, o_map, out_dtype, after=None,
            out_groups=1):
    nk = grid[2]
    dims = _DIMS[mode]
    acc_shape = tuple(s for s in o_blk if s is not None)
    if out_groups > 1:
        acc_shape = (acc_shape[1], out_groups * acc_shape[2])
    extra = [] if after is None else [after]

    def store(o_ref, val):
        if out_groups == 1:
            o_ref[...] = val.astype(o_ref.dtype)
        else:
            n = o_ref.shape[-1]
            for grp in range(out_groups):
                o_ref[grp] = val[:, grp * n:(grp + 1) * n].astype(o_ref.dtype)

    def body(a_ref, b_ref, *rest):
        o_ref, scratch = rest[len(extra)], rest[len(extra) + 1:]
        part = lax.dot_general(a_ref[...], b_ref[...], dims, preferred_element_type=F32)
        if nk == 1:
            store(o_ref, part)
            return
        acc_ref = scratch[0]
        k = pl.program_id(2)

        @pl.when(k == 0)
        def _():
            acc_ref[...] = part

        @pl.when(k > 0)
        def _():
            acc_ref[...] += part

        @pl.when(k == nk - 1)
        def _():
            store(o_ref, acc_ref[...])

    return pl.pallas_call(
        body, name=name, grid=grid,
        in_specs=[pl.BlockSpec(a_blk, a_map), pl.BlockSpec(b_blk, b_map)] + [pl.BlockSpec(memory_space=pl.ANY)] * len(extra),
        out_specs=pl.BlockSpec(o_blk, o_map),
        out_shape=_sds(o_shape, out_dtype),
        scratch_shapes=[] if nk == 1 else [pltpu.VMEM(acc_shape, F32)],
        compiler_params=_cparams(3),
    )(a, b, *extra)


TM = 1024
TK = S


def _fwd_rows(name, a, w, out_dtype=F32):
    kdim, n = w.shape
    tn = 512
    return _matmul(name, a, w, mode="nn", grid=(S // TM, n // tn, 1),
                   a_blk=(TM, kdim), a_map=lambda i, j, k: (i, 0),
                   b_blk=(kdim, tn), b_map=lambda i, j, k: (0, j),
                   o_shape=(S, n), o_blk=(TM, tn), o_map=lambda i, j, k: (i, j), out_dtype=out_dtype)


def _fwd_kblocked(name, a4, w4):
    nb, _, kb = a4.shape
    n = w4.shape[2]

    def body(a_ref, w_ref, o_ref):
        acc = _dot_nn(a_ref[0], w_ref[0])
        for j in range(1, nb):
            acc = acc + _dot_nn(a_ref[j], w_ref[j])
        o_ref[...] = acc

    return pl.pallas_call(
        body, name=name, grid=(S // TM,),
        in_specs=[pl.BlockSpec((nb, TM, kb), lambda i: (0, i, 0)), pl.BlockSpec((nb, kb, n), lambda i: (0, 0, 0))],
        out_specs=pl.BlockSpec((TM, n), lambda i: (i, 0)), out_shape=_sds((S, n), F32),
        compiler_params=_cparams(1),
    )(a4, w4)


def _bwd_x_cols_blocked(name, dy8, wg, after):
    _, kdim, n = wg.shape
    nk = NDEV // 2

    def body(a_ref, b_ref, after_ref, o_ref, acc_ref):
        k = pl.program_id(1)
        part = _dot_nt(a_ref[0], b_ref[0]) + _dot_nt(a_ref[1], b_ref[1])

        @pl.when(k == 0)
        def _():
            acc_ref[...] = part

        @pl.when(k > 0)
        def _():
            acc_ref[...] += part

        @pl.when(k == nk - 1)
        def _():
            o_ref[...] = acc_ref[...].astype(o_ref.dtype)

    return pl.pallas_call(
        body, name=name, grid=(S // TM, nk),
        in_specs=[pl.BlockSpec((2, None, TM, n), lambda i, k: (0, k, i, 0)),
                  pl.BlockSpec((2, None, kdim, n), lambda i, k: (0, k, 0, 0)),
                  pl.BlockSpec(memory_space=pl.ANY)],
        out_specs=pl.BlockSpec((TM, kdim), lambda i, k: (i, 0)), out_shape=_sds((S, kdim), BF16),
        scratch_shapes=[pltpu.VMEM((TM, kdim), F32)],
        compiler_params=_cparams(2),
    )(dy8.reshape(2, nk, S, n), wg.reshape(2, nk, kdim, n), after)


def _bwd_x_rows(name, dy, w, out_dtype, after=None):
    kdim, n = w.shape
    tkk = 512
    return _matmul(name, dy, w, mode="nt", grid=(S // TM, kdim // tkk, 1),
                   a_blk=(TM, n), a_map=lambda i, j, k: (i, 0),
                   b_blk=(tkk, n), b_map=lambda i, j, k: (j, 0),
                   o_shape=(S, kdim), o_blk=(TM, tkk), o_map=lambda i, j, k: (i, j), out_dtype=out_dtype, after=after)


DW_COLS = 768


def _bwd_w_cols(name, a, dy, n):
    kdim = a.shape[1]
    groups = DW_COLS // n
    return _matmul(name, a, dy, mode="tn", grid=(1, NDEV // groups, S // TK),
                   a_blk=(TK, kdim), a_map=lambda i, j, k: (k, 0),
                   b_blk=(TK, DW_COLS), b_map=lambda i, j, k: (k, j),
                   o_shape=(NDEV, kdim, n), o_blk=(groups, kdim, n) if groups > 1 else (None, kdim, n),
                   o_map=lambda i, j, k: (j, 0, 0), out_dtype=BF16, out_groups=groups)


def _bwd_x_plain(name, dy, w, after=None):
    kdim, n = w.shape
    tm = TM if n <= 3 * D else TM // 2
    return _matmul(name, dy, w, mode="nt", grid=(S // tm, 1, 1),
                   a_blk=(tm, n), a_map=lambda i, j, k: (i, 0),
                   b_blk=(kdim, n), b_map=lambda i, j, k: (0, 0),
                   o_shape=(S, kdim), o_blk=(tm, kdim), o_map=lambda i, j, k: (i, 0), out_dtype=BF16, after=after)


def _bwd_w_cols_blocked(name, a, dy8):
    kdim = a.shape[1]
    n = dy8.shape[2]
    return _matmul(name, dy8, a, mode="tn", grid=(1, NDEV, S // TK),
                   a_blk=(None, TK, n), a_map=lambda i, j, k: (j, k, 0),
                   b_blk=(TK, kdim), b_map=lambda i, j, k: (k, 0),
                   o_shape=(NDEV, n, kdim), o_blk=(None, n, kdim), o_map=lambda i, j, k: (j, 0, 0), out_dtype=BF16)


def _bwd_w_rows(name, a, dy):
    kdim = a.shape[1]
    n = dy.shape[1]
    tmm = 512
    return _matmul(name, a, dy, mode="tn", grid=(kdim // tmm, 1, S // TK),
                   a_blk=(TK, tmm), a_map=lambda i, j, k: (k, i),
                   b_blk=(TK, n), b_map=lambda i, j, k: (k, 0),
                   o_shape=(kdim, n), o_blk=(tmm, n), o_map=lambda i, j, k: (i, 0), out_dtype=BF16)


def _bwd_w_kblocked(name, a4, dy):
    nb, _, kb = a4.shape
    n = dy.shape[1]
    return _matmul(name, a4, dy, mode="tn", grid=(nb, 1, S // TK),
                   a_blk=(None, TK, kb), a_map=lambda i, j, k: (i, k, 0),
                   b_blk=(TK, n), b_map=lambda i, j, k: (k, 0),
                   o_shape=(nb, kb, n), o_blk=(None, kb, n), o_map=lambda i, j, k: (i, 0, 0), out_dtype=BF16)


def _rstd(x):
    return lax.rsqrt(jnp.mean(x * x, axis=-1, keepdims=True) + RMS_EPS)


def _row_spec(tm=ROW_TILE, width=D):
    return pl.BlockSpec((tm, width), lambda i: (i, 0))


def _vec_spec(rows=1, width=D):
    return pl.BlockSpec((rows, width), lambda i: (0, 0))


def _rms_fwd(name, x, gains):
    n = len(gains)

    def body(x_ref, *refs):
        x_val = x_ref[...]
        xh = x_val * _rstd(x_val)
        for g_ref, o_ref in zip(refs[:n], refs[n:]):
            o_ref[...] = (xh * g_ref[...]).astype(o_ref.dtype)

    outs = pl.pallas_call(
        body, name=name, grid=(S // ROW_TILE,),
        in_specs=[_row_spec()] + [_vec_spec()] * n,
        out_specs=[_row_spec()] * n,
        out_shape=[_sds((S, D), BF16)] * n,
        compiler_params=_cparams(1),
    )(x, *gains)
    return list(outs)


def _resid_rms(name, h, y, g, next_gains):
    n = len(next_gains)

    def body(h_ref, y_ref, g_ref, *refs):
        y_val = y_ref[...]
        h_new = h_ref[...] + (y_val * _rstd(y_val)) * g_ref[...]
        refs[n][...] = h_new
        hh = h_new * _rstd(h_new)
        for g2_ref, o_ref in zip(refs[:n], refs[n + 1:]):
            o_ref[...] = (hh * g2_ref[...]).astype(o_ref.dtype)

    outs = pl.pallas_call(
        body, name=name, grid=(S // ROW_TILE,),
        in_specs=[_row_spec(), _row_spec(), _vec_spec()] + [_vec_spec()] * n,
        out_specs=[_row_spec()] * (n + 1), out_shape=[_sds((S, D), F32)] + [_sds((S, D), BF16)] * n,
        compiler_params=_cparams(1),
    )(h, y, g, *next_gains)
    return outs[0], list(outs[1:])


def _resid_rms_loss(name, h, y, g, target):
    def body(h_ref, y_ref, g_ref, t_ref, dh_ref, dy_ref, dg_ref, part_ref):
        y_val = y_ref[...]
        gain = g_ref[...]
        e = h_ref[...] + (y_val * _rstd(y_val)) * gain - t_ref[...]
        dh = e * (1.0 / D)
        dh_ref[...] = dh
        step = pl.program_id(0)
        dy_ref[...] = _norm_bwd_rows(y_val, gain, dh, dg_ref, step).astype(dy_ref.dtype)
        part = jnp.sum(e * e, axis=0, keepdims=True)

        @pl.when(step == 0)
        def _():
            part_ref[...] = part

        @pl.when(step > 0)
        def _():
            part_ref[...] += part

    return pl.pallas_call(
        body, name=name, grid=(S // ROW_TILE,),
        in_specs=[_row_spec(), _row_spec(), _vec_spec(), _row_spec()],
        out_specs=[_row_spec(), _row_spec(), _vec_spec(8), _vec_spec()],
        out_shape=[_sds((S, D), F32), _sds((S, D), BF16), _sds((8, D), F32), _sds((1, D), F32)],
        compiler_params=_cparams(1),
    )(h, y, g, target)


def _norm_bwd_rows(x_val, g, dn, dg_ref, step):
    r = _rstd(x_val)
    xh = x_val * r
    dxh = dn * g
    part = jnp.sum(dn * xh, axis=0, keepdims=True)

    @pl.when(step == 0)
    def _():
        dg_ref[...] = jnp.zeros_like(dg_ref)

    dg_ref[0:1, :] += part
    return r * (dxh - xh * jnp.mean(dxh * xh, axis=-1, keepdims=True))


def _rms_bwd(name, x, pairs, dres, out_dtype, then=None):
    n = len(pairs)
    has_res = dres is not None
    chained = then is not None

    def body(x_ref, *refs):
        g_refs = refs[0:2 * n:2]
        dn_refs = refs[1:2 * n:2]
        pos = 2 * n
        res_ref = refs[pos] if has_res else None
        pos += int(has_res)
        if chained:
            y_ref, gy_ref = refs[pos], refs[pos + 1]
            pos += 2
        dx_ref = refs[pos]
        dg_refs = refs[pos + 1:pos + 1 + n]
        step = pl.program_id(0)
        x_val = x_ref[...]
        acc = res_ref[...] if has_res else jnp.zeros_like(x_val)
        for g_ref, dn_ref, dg_ref in zip(g_refs, dn_refs, dg_refs):
            acc = acc + _norm_bwd_rows(x_val, g_ref[...], dn_ref[...].astype(F32), dg_ref, step)
        dx_ref[...] = acc.astype(dx_ref.dtype)
        if chained:
            dy_ref, dgy_ref = refs[pos + 1 + n], refs[pos + 2 + n]
            dy_ref[...] = _norm_bwd_rows(y_ref[...], gy_ref[...], acc, dgy_ref, step).astype(dy_ref.dtype)

    operands = [x]
    in_specs = [_row_spec()]
    for g, dn in pairs:
        operands += [g, dn]
        in_specs += [_vec_spec(), _row_spec()]
    if has_res:
        operands.append(dres)
        in_specs.append(_row_spec())
    if chained:
        operands += [then[0], then[1]]
        in_specs += [_row_spec(), _vec_spec()]
    extra = int(chained)
    outs = pl.pallas_call(
        body, name=name, grid=(S // ROW_TILE,),
        in_specs=in_specs,
        out_specs=[_row_spec()] + [_vec_spec(8)] * n + [_row_spec(), _vec_spec(8)] * extra,
        out_shape=[_sds((S, D), out_dtype)] + [_sds((8, D), F32)] * n + [_sds((S, D), BF16), _sds((8, D), F32)] * extra,
        compiler_params=_cparams(1),
    )(*operands)
    if chained:
        return outs[0], list(outs[1:1 + n]), outs[1 + n], outs[2 + n]
    return outs[0], list(outs[1:])


def _shift_down(u, prev8, k):
    r = pltpu.roll(u, k, 0)
    p = pltpu.roll(prev8, k, 0)
    row = lax.broadcasted_iota(jnp.int32, prev8.shape, 0)
    top = jnp.where(row < k, p, r[0:8])
    return jnp.concatenate([top, r[8:]], axis=0)


def _shift_up(u, next8, k):
    tm = u.shape[0]
    r = pltpu.roll(u, tm - k, 0)
    p = pltpu.roll(next8, 8 - k, 0)
    row = lax.broadcasted_iota(jnp.int32, next8.shape, 0)
    bot = jnp.where(row >= 8 - k, p, r[tm - 8:tm])
    return jnp.concatenate([r[:tm - 8], bot], axis=0)


CONV_TILE = 512


def _halo_prev(col):
    return pl.BlockSpec((8, D), lambda i: (jnp.maximum(i * (CONV_TILE // 8) - 1, 0), col))


def _halo_next(col):
    last = S // 8 - 1
    return pl.BlockSpec((8, D), lambda i: (jnp.minimum((i + 1) * (CONV_TILE // 8), last), col))


def _conv_fwd(name, z, cw):
    def body(b_ref, c_ref, h_ref, cp_ref, hp_ref, cw_ref, o_ref):
        i = pl.program_id(0)
        u = c_ref[...].astype(F32) * h_ref[...].astype(F32)
        up = cp_ref[...].astype(F32) * hp_ref[...].astype(F32)
        up = jnp.where(i > 0, up, 0.0)
        cv = cw_ref[0:1, :] * _shift_down(u, up, 2) + cw_ref[1:2, :] * _shift_down(u, up, 1) + cw_ref[2:3, :] * u
        o_ref[...] = (b_ref[...].astype(F32) * cv).astype(o_ref.dtype)

    col = lambda c: pl.BlockSpec((CONV_TILE, D), lambda i: (i, c))
    return pl.pallas_call(
        body, name=name, grid=(S // CONV_TILE,),
        in_specs=[col(0), col(1), col(2), _halo_prev(1), _halo_prev(2), _vec_spec(8)],
        out_specs=_row_spec(CONV_TILE), out_shape=_sds((S, D), BF16),
        compiler_params=_cparams(1),
    )(z, z, z, z, z, cw)


def _conv_bwd(name, z, dpre, cw):
    nsteps = S // CONV_TILE

    def body(b_ref, c_ref, h_ref, cp_ref, hp_ref, dp_ref, dpn_ref, bn_ref, cw_ref, dz_ref, dcw_ref):
        i = pl.program_id(0)
        b = b_ref[...].astype(F32)
        c = c_ref[...].astype(F32)
        h = h_ref[...].astype(F32)
        dp = dp_ref[...].astype(F32)
        u = c * h
        up = jnp.where(i > 0, cp_ref[...].astype(F32) * hp_ref[...].astype(F32), 0.0)
        s1 = _shift_down(u, up, 1)
        s2 = _shift_down(u, up, 2)
        w0, w1, w2 = cw_ref[0:1, :], cw_ref[1:2, :], cw_ref[2:3, :]
        cv = w0 * s2 + w1 * s1 + w2 * u
        dcv = dp * b
        dcvn = jnp.where(i < nsteps - 1, dpn_ref[...].astype(F32) * bn_ref[...].astype(F32), 0.0)
        du = w2 * dcv + w1 * _shift_up(dcv, dcvn, 1) + w0 * _shift_up(dcv, dcvn, 2)
        dz_ref[:, 0:D] = (dp * cv).astype(dz_ref.dtype)
        dz_ref[:, D:2 * D] = (du * h).astype(dz_ref.dtype)
        dz_ref[:, 2 * D:3 * D] = (du * c).astype(dz_ref.dtype)

        @pl.when(i == 0)
        def _():
            dcw_ref[...] = jnp.zeros_like(dcw_ref)

        dcw_ref[0:1, :] += jnp.sum(dcv * s2, axis=0, keepdims=True)
        dcw_ref[1:2, :] += jnp.sum(dcv * s1, axis=0, keepdims=True)
        dcw_ref[2:3, :] += jnp.sum(dcv * u, axis=0, keepdims=True)

    col = lambda c: pl.BlockSpec((CONV_TILE, D), lambda i: (i, c))
    return pl.pallas_call(
        body, name=name, grid=(nsteps,),
        in_specs=[col(0), col(1), col(2), _halo_prev(1), _halo_prev(2),
                  _row_spec(CONV_TILE), _halo_next(0), _halo_next(0), _vec_spec(8)],
        out_specs=[pl.BlockSpec((CONV_TILE, 3 * D), lambda i: (i, 0)), _vec_spec(8)],
        out_shape=[_sds((S, 3 * D), BF16), _sds((8, D), F32)],
        compiler_params=_cparams(1),
    )(z, z, z, z, z, dpre, dpre, z, cw)


FFN_TM = 2048
_GU_BLOCK = pl.BlockSpec((2, None, FFN_TM, FB), lambda i, j: (0, j, i, 0))


def _gate_up_act(name, a, wg):
    kdim = a.shape[1]

    def body(a_ref, wgate_ref, wup_ref, gu_ref, act_ref):
        x = a_ref[...]
        g = _dot_nn(x, wgate_ref[...])
        u = _dot_nn(x, wup_ref[...])
        gu_ref[0] = g.astype(gu_ref.dtype)
        gu_ref[1] = u.astype(gu_ref.dtype)
        act_ref[...] = (g * jax.nn.sigmoid(g) * u).astype(act_ref.dtype)

    return pl.pallas_call(
        body, name=name, grid=(S // FFN_TM, NFB),
        in_specs=[pl.BlockSpec((FFN_TM, kdim), lambda i, j: (i, 0)),
                  pl.BlockSpec((None, kdim, FB), lambda i, j: (j, 0, 0)),
                  pl.BlockSpec((None, kdim, FB), lambda i, j: (j + NFB, 0, 0))],
        out_specs=[_GU_BLOCK, pl.BlockSpec((None, FFN_TM, FB), lambda i, j: (j, i, 0))],
        out_shape=[_sds((2, NFB, S, FB), BF16), _sds((NFB, S, FB), BF16)],
        compiler_params=_cparams(2),
    )(a, wg, wg)


def _down_dx_act_bwd(name, df, w4, gu):
    _, kb, n = w4.shape

    def body(df_ref, w_ref, gu_ref, o_ref):
        d = _dot_nt(df_ref[...], w_ref[...])
        g = gu_ref[0].astype(F32)
        u = gu_ref[1].astype(F32)
        sg = jax.nn.sigmoid(g)
        o_ref[0] = (d * u * sg * (1.0 + g * (1.0 - sg))).astype(o_ref.dtype)
        o_ref[1] = (d * g * sg).astype(o_ref.dtype)

    return pl.pallas_call(
        body, name=name, grid=(S // FFN_TM, NFB),
        in_specs=[pl.BlockSpec((FFN_TM, n), lambda i, j: (i, 0)), pl.BlockSpec((None, kb, n), lambda i, j: (j, 0, 0)),
                  _GU_BLOCK],
        out_specs=_GU_BLOCK, out_shape=_sds((2, NFB, S, FB), BF16),
        compiler_params=_cparams(2),
    )(df, w4, gu)


def _rope_tables(name, pos_col, inv_freq_row):
    def body(pos_ref, f_ref, cos_ref, sin_ref):
        ang = pos_ref[...].astype(F32) * f_ref[...]
        lane = lax.broadcasted_iota(jnp.int32, ang.shape, 1)
        s = jnp.sin(ang)
        cos_ref[...] = jnp.cos(ang)
        sin_ref[...] = jnp.where((lane % HEAD_DIM) < HEAD_DIM // 2, -s, s)

    tab = pl.BlockSpec((ROW_TILE, 128), lambda i: (i, 0))
    return pl.pallas_call(
        body, name=name, grid=(S // ROW_TILE,),
        in_specs=[pl.BlockSpec((ROW_TILE, 1), lambda i: (i, 0)), _vec_spec(1, 128)],
        out_specs=[tab, tab], out_shape=[_sds((S, 128), F32)] * 2,
        compiler_params=_cparams(1),
    )(pos_col, inv_freq_row)


def _swap_halves(t):
    lane = lax.broadcasted_iota(jnp.int32, t.shape, 1)
    first = (lane % HEAD_DIM) < HEAD_DIM // 2
    return jnp.where(first, pltpu.roll(t, 128 - HEAD_DIM // 2, 1), pltpu.roll(t, HEAD_DIM // 2, 1))


NCHUNK = D // 128


def _chunk(c, base=0):
    return slice(base + c * 128, base + (c + 1) * 128)


def _class_rows(r, d, tm):
    return pl.ds(r, tm // d, stride=d) if d > 1 else slice(None)


def _class_block(d, tm):
    return pl.BlockSpec((tm // d, d * D), lambda i: (i, 0))


def _tokens_from_classes(blk_ref, tmp_ref, d, tm):
    for r in range(d):
        for c in range(NCHUNK):
            tmp_ref[c, _class_rows(r, d, tm), :] = blk_ref[:, _chunk(c, r * D)].astype(F32)


def _classes_from_tokens(tmp_ref, blk_ref, d, tm):
    for r in range(d):
        for c in range(NCHUNK):
            blk_ref[:, _chunk(c, r * D)] = tmp_ref[c, _class_rows(r, d, tm), :].astype(blk_ref.dtype)


def _qkv_classes(name, n2, nk, wq, wkv, g, d, tables):
    def emit(acc, cos_ref, sin_ref, o_ref, tmp_ref, scale):
        for c in range(NCHUNK):
            tmp_ref[c] = acc[:, _chunk(c)]
        for r in range(d):
            rows = _class_rows(r, d, TM)
            if scale is not None:
                cs = cos_ref[rows, :]
                sn = sin_ref[rows, :]
            for c in range(NCHUNK):
                x = tmp_ref[c, rows, :]
                if scale is not None:
                    x = (x * cs + _swap_halves(x) * sn) * scale
                o_ref[:, _chunk(c, r * D)] = x.astype(o_ref.dtype)

    def body(n2_ref, nk_ref, wq_ref, wk_ref, wv_ref, cos_ref, sin_ref, q_ref, k_ref, v_ref, tmp_ref):
        emit(_dot_nn(n2_ref[...], wq_ref[...]), cos_ref, sin_ref, q_ref, tmp_ref, HEAD_DIM ** -0.5)
        x = nk_ref[...]
        emit(_dot_nn(x, wk_ref[...]), cos_ref, sin_ref, k_ref, tmp_ref, 1.0)
        emit(_dot_nn(x, wv_ref[...]), cos_ref, sin_ref, v_ref, tmp_ref, None)

    nbr = len(DILATIONS)
    act = pl.BlockSpec((TM, D), lambda i: (i, 0))
    tab = pl.BlockSpec((TM, 128), lambda i: (i, 0))
    wcol = lambda col: pl.BlockSpec((D, D), lambda i: (0, col))
    return pl.pallas_call(
        body, name=name, grid=(S // TM,),
        in_specs=[act, act, wcol(g), wcol(g), wcol(nbr + g), tab, tab],
        out_specs=[_class_block(d, TM)] * 3, out_shape=[_sds((S // d, d * D), BF16)] * 3,
        scratch_shapes=[pltpu.VMEM((NCHUNK, TM, 128), F32)],
        compiler_params=_cparams(1),
    )(n2, nk, wq, wkv, wkv, *tables)


ATTN_CHAINS = 16


def _attn_units(d):
    nblk = S // d // BAND
    return max(1, 2 * ATTN_CHAINS // nblk)


def _class_spec(d):
    return pl.BlockSpec((S // d, 128 * _attn_units(d)), lambda cb: (0, cb))


def _dot_nt(a, b):
    return lax.dot_general(a, b, _DIMS["nt"], preferred_element_type=F32)


def _dot_tn(a, b):
    return lax.dot_general(a, b, _DIMS["tn"], preferred_element_type=F32)


def _dot_nn(a, b):
    return lax.dot_general(a, b, _DIMS["nn"], preferred_element_type=F32)


def _band_mask(nkeys):
    qi = lax.broadcasted_iota(jnp.int32, (2 * BAND, nkeys), 0) % BAND
    kj = lax.broadcasted_iota(jnp.int32, (2 * BAND, nkeys), 1)
    if nkeys == BAND:
        return kj <= qi
    dist = qi + BAND - kj
    return (dist >= 0) & (dist <= BAND)


def _band_bias():
    return {n: jnp.where(_band_mask(n), 0.0, NEG_INF).astype(F32) for n in (BAND, 2 * BAND)}


def _stack_heads(x):
    row = lax.broadcasted_iota(jnp.int32, (2 * BAND, 128), 0)
    lane = lax.broadcasted_iota(jnp.int32, (2 * BAND, 128), 1)
    keep = (row < BAND) == (lane < HEAD_DIM)
    return jnp.where(keep, jnp.concatenate([x, x], axis=0), jnp.zeros((), x.dtype))


def _unstack(x2):
    first_head = lax.broadcasted_iota(jnp.int32, (BAND, 128), 1) < HEAD_DIM
    return jnp.where(first_head, x2[:BAND], x2[BAND:])


def _for_later_blocks(nblk, units, fn):
    all_lanes = [slice(u * 128, (u + 1) * 128) for u in range(units)]
    unroll = max(1, ATTN_CHAINS // units)
    trips = (nblk - 1) // unroll
    if trips > 1:
        def step(i, carry):
            for j in range(unroll):
                for lanes in all_lanes:
                    fn(pl.multiple_of((1 + i * unroll + j) * BAND, BAND), lanes)
            return carry

        lax.fori_loop(0, trips, step, 0)
    else:
        trips = 0
    for sb in range(1 + trips * unroll, nblk):
        for lanes in all_lanes:
            fn(sb * BAND, lanes)


def _attn_fwd(name, q, k, v, d):
    nblk = S // d // BAND
    units = _attn_units(d)

    def body(q_ref, k_ref, v_ref, o_ref, lse_ref):
        bias = _band_bias()

        def block(r0, k0, nkeys, lanes):
            q2 = _stack_heads(q_ref[pl.ds(r0, BAND), lanes])
            s = _dot_nt(q2, k_ref[pl.ds(k0, nkeys), lanes]) + bias[nkeys]
            m = jnp.max(s, axis=-1, keepdims=True)
            p = jnp.exp(s - m)
            l = jnp.sum(p, axis=-1, keepdims=True)
            o2 = _dot_nn(p.astype(BF16), v_ref[pl.ds(k0, nkeys), lanes])
            l_tile = _unstack(jnp.broadcast_to(l, (2 * BAND, 128)))
            m_tile = _unstack(jnp.broadcast_to(m, (2 * BAND, 128)))
            o_ref[pl.ds(r0, BAND), lanes] = (_unstack(o2) / l_tile).astype(o_ref.dtype)
            lse_ref[pl.ds(r0, BAND), lanes] = m_tile + jnp.log(l_tile)

        for u in range(units):
            block(0, 0, BAND, slice(u * 128, (u + 1) * 128))

        _for_later_blocks(nblk, units, lambda r0, lanes: block(r0, r0 - BAND, 2 * BAND, lanes))

    spec = _class_spec(d)
    return pl.pallas_call(
        body, name=name, grid=(8 * d // units,),
        in_specs=[spec] * 3, out_specs=[spec] * 2,
        out_shape=[_sds((S // d, d * D), BF16), _sds((S // d, d * D), F32)],
        compiler_params=_cparams(1),
    )(q, k, v)


def _attn_bwd(name, q, k, v, do, lse, dd, d):
    nblk = S // d // BAND
    units = _attn_units(d)

    def body(q_ref, k_ref, v_ref, do_ref, lse_ref, dd_ref, dq_ref, dk_out, dv_out, dk_ref, dv_ref):
        bias = _band_bias()
        def column(ref, r0, lanes, nkeys):
            tile = ref[pl.ds(r0, BAND), lanes]
            other = pltpu.roll(tile, HEAD_DIM, 1)
            first_head = lax.broadcasted_iota(jnp.int32, tile.shape, 1) < HEAD_DIM
            both = jnp.concatenate([jnp.where(first_head, tile, other), jnp.where(first_head, other, tile)], axis=0)
            return both if nkeys == BAND else jnp.concatenate([both, both], axis=1)

        def block(r0, k0, nkeys, lanes, first):
            q2 = _stack_heads(q_ref[pl.ds(r0, BAND), lanes])
            do2 = _stack_heads(do_ref[pl.ds(r0, BAND), lanes])
            kk = k_ref[pl.ds(k0, nkeys), lanes]
            vv = v_ref[pl.ds(k0, nkeys), lanes]
            s = _dot_nt(q2, kk) + bias[nkeys]
            p = jnp.exp(s - column(lse_ref, r0, lanes, nkeys))
            ds = (p * (_dot_nt(do2, vv) - column(dd_ref, r0, lanes, nkeys))).astype(BF16)
            dq_ref[pl.ds(r0, BAND), lanes] = _unstack(_dot_nn(ds, kk)).astype(dq_ref.dtype)
            dk_part = _dot_tn(ds, q2)
            dv_part = _dot_tn(p.astype(BF16), do2)
            if first:
                dk_ref[pl.ds(k0, nkeys), lanes] = dk_part
                dv_ref[pl.ds(k0, nkeys), lanes] = dv_part
            else:
                dk_ref[pl.ds(k0, BAND), lanes] += dk_part[:BAND]
                dv_ref[pl.ds(k0, BAND), lanes] += dv_part[:BAND]
                dk_ref[pl.ds(k0 + BAND, BAND), lanes] = dk_part[BAND:]
                dv_ref[pl.ds(k0 + BAND, BAND), lanes] = dv_part[BAND:]

        for u in range(units):
            block(0, 0, BAND, slice(u * 128, (u + 1) * 128), True)

        _for_later_blocks(nblk, units, lambda r0, lanes: block(r0, r0 - BAND, 2 * BAND, lanes, False))
        dk_out[...] = dk_ref[...].astype(dk_out.dtype)
        dv_out[...] = dv_ref[...].astype(dv_out.dtype)

    spec = _class_spec(d)
    return pl.pallas_call(
        body, name=name, grid=(8 * d // units,),
        in_specs=[spec] * 6, out_specs=[spec] * 3,
        out_shape=[_sds((S // d, d * D), BF16)] * 3,
        scratch_shapes=[pltpu.VMEM((S // d, 128 * units), F32)] * 2,
        compiler_params=_cparams(1),
    )(q, k, v, do, lse, dd)


MIX_TILE = 256
DILATIONS = tuple(d for _, d in BRANCHES)


def _branch_weights(la, lb, lc):
    m = jnp.maximum(jnp.maximum(la, lb), lc)
    ea, eb, ec = jnp.exp(la - m), jnp.exp(lb - m), jnp.exp(lc - m)
    den = ea + eb + ec
    return ea / den, eb / den, ec / den


def _mix_operands(outs, lses):
    specs = [_class_block(d, MIX_TILE) for d in DILATIONS] * 2
    scratch = [pltpu.VMEM((NCHUNK, MIX_TILE, 128), F32)] * 4
    return list(outs) + list(lses), specs, scratch


def _mix_fwd(name, outs, lses):
    def body(o0, o1, o2, l0, l1, l2, o_ref, to1, to2, tl1, tl2):
        for blk, tmp, d in ((o1, to1, DILATIONS[1]), (o2, to2, DILATIONS[2]), (l1, tl1, DILATIONS[1]), (l2, tl2, DILATIONS[2])):
            _tokens_from_classes(blk, tmp, d, MIX_TILE)
        for c in range(NCHUNK):
            wa, wb, wc = _branch_weights(l0[:, _chunk(c)], tl1[c], tl2[c])
            o_ref[:, _chunk(c)] = (wa * o0[:, _chunk(c)].astype(F32) + wb * to1[c] + wc * to2[c]).astype(o_ref.dtype)

    operands, specs, scratch = _mix_operands(outs, lses)
    return pl.pallas_call(
        body, name=name, grid=(S // MIX_TILE,),
        in_specs=specs, out_specs=_row_spec(MIX_TILE), out_shape=_sds((S, D), BF16),
        scratch_shapes=scratch, compiler_params=_cparams(1),
    )(*operands)


def _head_sum(x, ones_blockdiag):
    hi = x.astype(BF16)
    r1 = x - hi.astype(F32)
    mid = r1.astype(BF16)
    lo = (r1 - mid.astype(F32)).astype(BF16)
    return _dot_nn(hi, ones_blockdiag) + _dot_nn(mid, ones_blockdiag) + _dot_nn(lo, ones_blockdiag)


def _mix_bwd(name, do, outs, lses, ones_blockdiag):
    def body(do_ref, o0, o1, o2, l0, l1, l2, ones_ref, d0, d1, d2, t0, t1, t2,
             to1, to2, tl1, tl2, td1, td2, tt1, tt2):
        for blk, tmp, d in ((o1, to1, DILATIONS[1]), (o2, to2, DILATIONS[2]), (l1, tl1, DILATIONS[1]), (l2, tl2, DILATIONS[2])):
            _tokens_from_classes(blk, tmp, d, MIX_TILE)
        ones = ones_ref[...]
        for c in range(NCHUNK):
            w = _branch_weights(l0[:, _chunk(c)], tl1[c], tl2[c])
            dov = do_ref[:, _chunk(c)]
            o = w[0] * o0[:, _chunk(c)].astype(F32) + w[1] * to1[c] + w[2] * to2[c]
            t = _head_sum(dov * o, ones)
            d0[:, _chunk(c)] = (w[0] * dov).astype(d0.dtype)
            t0[:, _chunk(c)] = w[0] * t
            td1[c], tt1[c] = w[1] * dov, w[1] * t
            td2[c], tt2[c] = w[2] * dov, w[2] * t
        for tmp, blk, d in ((td1, d1, DILATIONS[1]), (tt1, t1, DILATIONS[1]), (td2, d2, DILATIONS[2]), (tt2, t2, DILATIONS[2])):
            _classes_from_tokens(tmp, blk, d, MIX_TILE)

    operands, specs, scratch = _mix_operands(outs, lses)
    out_specs = [_class_block(d, MIX_TILE) for d in DILATIONS] * 2
    out_shape = [_sds((S // d, d * D), BF16) for d in DILATIONS] + [_sds((S // d, d * D), F32) for d in DILATIONS]
    return pl.pallas_call(
        body, name=name, grid=(S // MIX_TILE,),
        in_specs=[_row_spec(MIX_TILE)] + specs + [_vec_spec(128, 128)],
        out_specs=out_specs, out_shape=out_shape,
        scratch_shapes=scratch + [pltpu.VMEM((NCHUNK, MIX_TILE, 128), F32)] * 4,
        compiler_params=_cparams(1),
    )(do, *operands, ones_blockdiag)


def _attn_bwd_post(name, grads, cos_t, sin_t):
    tm = MIX_TILE
    scale = HEAD_DIM ** -0.5

    def unrope(x, cs, sn):
        return x * cs - _swap_halves(x) * sn

    def body(*refs):
        in_refs = refs[:9]
        cos_ref, sin_ref, dq_ref, dkv_ref, tmp_ref = refs[9:]
        cs = cos_ref[...]
        sn = sin_ref[...]
        for g, d in enumerate(DILATIONS):
            for which, blk in enumerate(in_refs[3 * g:3 * g + 3]):
                if d > 1:
                    _tokens_from_classes(blk, tmp_ref, d, tm)
                for c in range(NCHUNK):
                    x = tmp_ref[c] if d > 1 else blk[:, _chunk(c)].astype(F32)
                    if which == 0:
                        dq_ref[:, _chunk(c, g * D)] = (unrope(x, cs, sn) * scale).astype(dq_ref.dtype)
                    elif which == 1:
                        dkv_ref[:, _chunk(c, g * D)] = unrope(x, cs, sn).astype(dkv_ref.dtype)
                    else:
                        dkv_ref[:, _chunk(c, QW + g * D)] = x.astype(dkv_ref.dtype)

    operands = [a for branch in grads for a in branch]
    tab = pl.BlockSpec((tm, 128), lambda i: (i, 0))
    return pl.pallas_call(
        body, name=name, grid=(S // tm,),
        in_specs=[_class_block(d, tm) for d in DILATIONS for _ in range(3)] + [tab, tab],
        out_specs=[pl.BlockSpec((tm, QW), lambda i: (i, 0)), pl.BlockSpec((tm, 2 * QW), lambda i: (i, 0))],
        out_shape=[_sds((S, QW), BF16), _sds((S, 2 * QW), BF16)],
        scratch_shapes=[pltpu.VMEM((NCHUNK, tm, 128), F32)],
        compiler_params=_cparams(1),
    )(*operands, cos_t, sin_t)


def _adamw(name, parts, w, m, v, layer=None, other=None):
    n, rows, cols = parts.shape
    tr = rows
    for cand in (256, 176, 128, 64, 32, 16, 8):
        if rows % cand == 0:
            tr = cand
            break
    n_other = 0 if other is None else len(other)

    def body(p_ref, w_ref, m_ref, v_ref, *refs):
        g_ref, d_ref, nm_ref, nv_ref = refs[n_other:]
        g = p_ref[0].astype(F32)
        for j in range(1, n):
            g = g + p_ref[j].astype(F32)
        g_ref[...] = g
        d_ref[...], nm_ref[...], nv_ref[...] = _adam_update(g, w_ref[...], m_ref[...], v_ref[...])

    if layer is None:
        blk = pl.BlockSpec((tr, cols), lambda i: (i, 0))
        shape = (rows, cols)
    else:
        blk = pl.BlockSpec((None, tr, cols), lambda i: (layer, i, 0))
        shape = w.shape
    return pl.pallas_call(
        body, name=name, grid=(rows // tr,),
        in_specs=[pl.BlockSpec((n, tr, cols), lambda i: (0, i, 0)), blk, blk, blk]
                 + [pl.BlockSpec(memory_space=pl.ANY)] * n_other,
        out_specs=[blk] * 4, out_shape=[_sds(shape, F32)] * 4,
        input_output_aliases={4 + i: i for i in range(n_other)},
        compiler_params=_cparams(1),
    )(parts, w, m, v, *(other or ()))


def _adam_update(g, w, m, v):
    c1 = 1.0 / (1.0 - ADAM_B1 ** ADAM_STEP)
    c2 = 1.0 / (1.0 - ADAM_B2 ** ADAM_STEP)
    nm = ADAM_B1 * m + (1.0 - ADAM_B1) * g
    nv = ADAM_B2 * v + (1.0 - ADAM_B2) * (g * g)
    return -ADAM_LR * ((nm * c1) / (jnp.sqrt(nv * c2) + ADAM_EPS) + ADAM_WD * w), nm, nv


GAIN_ROWS = 16


def _pack_small(name, gain_tiles, taps, sq):
    ng = len(gain_tiles)

    def body(*refs):
        o_ref = refs[-1]
        o_ref[...] = jnp.zeros_like(o_ref)
        for i in range(ng):
            o_ref[i:i + 1, :] = refs[i][0:1, :]
        o_ref[ng:ng + 3, :] = refs[ng][0:3, :]
        o_ref[ng + 3:ng + 4, :] = refs[ng + 1][...]

    return pl.pallas_call(body, name=name, out_shape=_sds((GAIN_ROWS, D), F32))(*gain_tiles, taps, sq)


def _adamw_gains(name, parts, params):
    np_ = len(params)
    shapes = [w.shape for w, _, _ in params]

    def body(p_ref, *refs):
        ins, outs = refs[:3 * np_], refs[3 * np_:]

        def total(lo, rows):
            g = p_ref[0, lo:lo + rows, :]
            for j in range(1, NDEV):
                g = g + p_ref[j, lo:lo + rows, :]
            return g

        lo = 0
        for i, shape in enumerate(shapes):
            g = total(lo, shape[0])
            lo += shape[0]
            w_ref, m_ref, v_ref = ins[3 * i:3 * i + 3]
            g_ref, d_ref, nm_ref, nv_ref = outs[4 * i:4 * i + 4]
            g_ref[...] = g
            d_ref[...], nm_ref[...], nv_ref[...] = _adam_update(g, w_ref[...], m_ref[...], v_ref[...])
        taps_ref, loss_ref = outs[-2], outs[-1]
        taps_ref[...] = jnp.zeros_like(taps_ref)
        taps_ref[0:3, :] = total(lo, 3)
        loss_ref[...] = jnp.sum(total(lo + 3, 1), axis=-1, keepdims=True) * (0.5 / D)

    out_shape = [_sds(shape, F32) for shape in shapes for _ in range(4)] + [_sds((8, D), F32), _sds((1, 1), F32)]
    outs = pl.pallas_call(body, name=name, out_shape=out_shape)(parts, *[a for p in params for a in p])
    return [list(outs[4 * i:4 * i + 4]) for i in range(np_)], outs[-2], outs[-1].reshape(())


def _exchange(name, arrays, kind, after):
    n = len(arrays)
    gather = kind == "gather"
    out_shape = [_sds((NDEV,) + a.shape if gather else a.shape, a.dtype) for a in arrays]

    def body(*refs):
        srcs, outs = refs[:n], refs[n + 1:2 * n + 1]
        send_sems, recv_sems, local_sems = refs[2 * n + 1:]
        x, y, c = lax.axis_index("x"), lax.axis_index("y"), lax.axis_index("c")
        me = 4 * x + 2 * y + c
        pending = []
        for t in range(n):
            own = pltpu.make_async_copy(srcs[t] if gather else srcs[t].at[me], outs[t].at[me], local_sems.at[t])
            own.start()
            pending.append(own)
            for rel in range(1, NDEV):
                px = 1 - x if rel & 4 else x
                py = 1 - y if rel & 2 else y
                pc = 1 - c if rel & 1 else c
                peer = 4 * px + 2 * py + pc
                send = pltpu.make_async_remote_copy(
                    src_ref=srcs[t] if gather else srcs[t].at[peer], dst_ref=outs[t].at[me],
                    send_sem=send_sems.at[t, rel - 1], recv_sem=recv_sems.at[t, rel - 1],
                    device_id=(px, py, pc), device_id_type=MESH)
                send.start()
                arrive = pltpu.make_async_remote_copy(
                    src_ref=srcs[t] if gather else srcs[t].at[me], dst_ref=outs[t].at[peer],
                    send_sem=send_sems.at[t, rel - 1], recv_sem=recv_sems.at[t, rel - 1],
                    device_id=(px, py, pc), device_id_type=MESH)
                pending.append((send, arrive))
        for item in pending:
            if isinstance(item, tuple):
                item[0].wait_send()
                item[1].wait_recv()
            else:
                item.wait()

    any_spec = pl.BlockSpec(memory_space=pl.ANY)
    outs = pl.pallas_call(
        body, name=name,
        in_specs=[any_spec] * (n + 1), out_specs=[any_spec] * n, out_shape=out_shape,
        scratch_shapes=[pltpu.SemaphoreType.DMA((n, NDEV - 1)), pltpu.SemaphoreType.DMA((n, NDEV - 1)),
                        pltpu.SemaphoreType.DMA((n,))],
    )(*arrays, after)
    return list(outs)


_HBM_SPEC = pl.BlockSpec(memory_space=pltpu.HBM)
_SEM_SPEC = pl.BlockSpec(memory_space=pltpu.SEMAPHORE)
_DATAFLOW = pltpu.SideEffectType.DATAFLOW_SIDE_EFFECTING


def _peers():
    x, y, c = lax.axis_index("x"), lax.axis_index("y"), lax.axis_index("c")
    out = []
    for rel in range(1, NDEV):
        px = 1 - x if rel & 4 else x
        py = 1 - y if rel & 2 else y
        pc = 1 - c if rel & 1 else c
        out.append((rel - 1, (px, py, pc), 4 * px + 2 * py + pc))
    return 4 * x + 2 * y + c, out


def _hbm(a):
    return pltpu.HBM(a.shape, a.dtype)


def _own_slot(a, me, kind):
    mine = a[None] if kind == "gather" else lax.dynamic_slice_in_dim(a, me, 1, axis=0)
    shape = (NDEV,) + mine.shape[1:]
    return lax.dynamic_update_slice_in_dim(lax.empty(shape, a.dtype), mine, me, axis=0)


def _exchange_start(name, arrays, me, kind):
    n = len(arrays)
    gather = kind == "gather"
    lands = [_own_slot(a, me, kind) for a in arrays]

    def body(*refs):
        src_refs, land_refs = refs[:n], refs[n:2 * n]
        send_sems, recv_sems = refs[2 * n], refs[2 * n + 1]
        token = refs[-1]
        my_block, peers = _peers()
        for t in range(n):
            for slot, dev, block in peers:
                pltpu.make_async_remote_copy(
                    src_ref=src_refs[t] if gather else src_refs[t].at[block], dst_ref=land_refs[t].at[my_block],
                    send_sem=send_sems.at[t * (NDEV - 1) + slot], recv_sem=recv_sems.at[t * (NDEV - 1) + slot],
                    device_id=dev, device_id_type=MESH).start()
        token[...] = jnp.zeros_like(token)

    operands = [pltpu.with_memory_space_constraint(a, pltpu.HBM) for a in list(arrays) + lands]
    outs = pl.pallas_call(
        body, name=name,
        out_shape=(pltpu.SemaphoreType.DMA((n * (NDEV - 1),)), pltpu.SemaphoreType.DMA((n * (NDEV - 1),)),
                   *[_hbm(a) for a in operands], _sds((8, 128), F32)),
        in_specs=[_HBM_SPEC] * (2 * n),
        out_specs=(_SEM_SPEC, _SEM_SPEC, *[_HBM_SPEC] * (2 * n), pl.BlockSpec(memory_space=pltpu.VMEM)),
        input_output_aliases={i: 2 + i for i in range(2 * n)},
        compiler_params=pltpu.CompilerParams(has_side_effects=_DATAFLOW),
    )(*operands)
    return (outs[0], outs[1], list(outs[2:2 + n]), list(outs[2 + n:2 + 2 * n])), outs[-1]


def _exchange_wait(name, started, t, after, kind):
    send_sems, recv_sems, srcs, lands = started
    gather = kind == "gather"

    def body(src_ref, land_ref, send_ref, recv_ref, after_ref, src_out, land_out):
        _, peers = _peers()
        for slot, dev, block in peers:
            copy = pltpu.make_async_remote_copy(
                src_ref=src_ref if gather else src_ref.at[block], dst_ref=land_ref.at[block],
                send_sem=send_ref.at[t * (NDEV - 1) + slot], recv_sem=recv_ref.at[t * (NDEV - 1) + slot],
                device_id=dev, device_id_type=MESH)
            copy.wait_send()
            copy.wait_recv()

    return pl.pallas_call(
        body, name=name, out_shape=(_hbm(srcs[t]), _hbm(lands[t])),
        in_specs=(_HBM_SPEC, _HBM_SPEC, _SEM_SPEC, _SEM_SPEC, pl.BlockSpec(memory_space=pl.ANY)),
        out_specs=(_HBM_SPEC, _HBM_SPEC), input_output_aliases={0: 0, 1: 1},
        compiler_params=pltpu.CompilerParams(has_side_effects=_DATAFLOW),
    )(srcs[t], lands[t], send_sems, recv_sems, after)[1]


DIRECT_RELS = (1, 2, 4, 6)
RELAY_RELS = (2, 4, 6)


def _rel_peer(rel):
    x, y, c = lax.axis_index("x"), lax.axis_index("y"), lax.axis_index("c")
    px = 1 - x if rel & 4 else x
    py = 1 - y if rel & 2 else y
    pc = 1 - c if rel & 1 else c
    return (px, py, pc), 4 * px + 2 * py + pc


def _gather_start(name, shards, me):
    n, nr = len(shards), len(DIRECT_RELS)
    lands = [_own_slot(a, me, "gather") for a in shards]

    def body(*refs):
        src_refs, land_refs = refs[:n], refs[n:2 * n]
        send_sems, recv_sems = refs[2 * n], refs[2 * n + 1]
        _, my_block = _rel_peer(0)
        for t in range(n):
            for s, rel in enumerate(DIRECT_RELS):
                dev, _ = _rel_peer(rel)
                pltpu.make_async_remote_copy(
                    src_ref=src_refs[t], dst_ref=land_refs[t].at[my_block],
                    send_sem=send_sems.at[t * nr + s], recv_sem=recv_sems.at[t * nr + s],
                    device_id=dev, device_id_type=MESH).start()

    operands = [pltpu.with_memory_space_constraint(a, pltpu.HBM) for a in list(shards) + lands]
    outs = pl.pallas_call(
        body, name=name,
        out_shape=(pltpu.SemaphoreType.DMA((n * nr,)), pltpu.SemaphoreType.DMA((n * nr,)), *[_hbm(a) for a in operands]),
        in_specs=[_HBM_SPEC] * (2 * n), out_specs=(_SEM_SPEC, _SEM_SPEC, *[_HBM_SPEC] * (2 * n)),
        input_output_aliases={i: 2 + i for i in range(2 * n)},
        compiler_params=pltpu.CompilerParams(has_side_effects=_DATAFLOW),
    )(*operands)
    return outs[0], outs[1], list(outs[2:2 + n]), list(outs[2 + n:2 + 2 * n])


def _gather_wait(name, started, ts, after):
    send_sems, recv_sems, srcs, lands = started
    m, nr = len(ts), len(DIRECT_RELS)

    def body(*refs):
        src_refs, land_refs = refs[:m], refs[m:2 * m]
        send_ref, recv_ref = refs[2 * m], refs[2 * m + 1]
        for i, t in enumerate(ts):
            for s, rel in enumerate(DIRECT_RELS):
                dev, block = _rel_peer(rel)
                copy = pltpu.make_async_remote_copy(
                    src_ref=src_refs[i], dst_ref=land_refs[i].at[block],
                    send_sem=send_ref.at[t * nr + s], recv_sem=recv_ref.at[t * nr + s],
                    device_id=dev, device_id_type=MESH)
                copy.wait_send()
                copy.wait_recv()

    operands = [srcs[t] for t in ts] + [lands[t] for t in ts]
    outs = pl.pallas_call(
        body, name=name, out_shape=tuple(_hbm(a) for a in operands),
        in_specs=[_HBM_SPEC] * (2 * m) + [_SEM_SPEC, _SEM_SPEC, pl.BlockSpec(memory_space=pl.ANY)],
        out_specs=tuple([_HBM_SPEC] * (2 * m)), input_output_aliases={i: i for i in range(2 * m)},
        compiler_params=pltpu.CompilerParams(has_side_effects=_DATAFLOW),
    )(*operands, send_sems, recv_sems, after)
    return list(outs[m:])


def _relay_start(name, lands):
    m, nr = len(lands), len(RELAY_RELS)

    def body(*refs):
        land_refs, send_sems, recv_sems = refs[:m], refs[m], refs[m + 1]
        sibling, _ = _rel_peer(1)
        for i in range(m):
            for s, rel in enumerate(RELAY_RELS):
                _, block = _rel_peer(rel)
                pltpu.make_async_remote_copy(
                    src_ref=land_refs[i].at[block], dst_ref=land_refs[i].at[block],
                    send_sem=send_sems.at[i * nr + s], recv_sem=recv_sems.at[i * nr + s],
                    device_id=sibling, device_id_type=MESH).start()

    outs = pl.pallas_call(
        body, name=name,
        out_shape=(pltpu.SemaphoreType.DMA((m * nr,)), pltpu.SemaphoreType.DMA((m * nr,)), *[_hbm(a) for a in lands]),
        in_specs=[_HBM_SPEC] * m, out_specs=(_SEM_SPEC, _SEM_SPEC, *[_HBM_SPEC] * m),
        input_output_aliases={i: 2 + i for i in range(m)},
        compiler_params=pltpu.CompilerParams(has_side_effects=_DATAFLOW),
    )(*lands)
    return outs[0], outs[1], list(outs[2:])


def _relay_wait(name, relayed, after):
    send_sems, recv_sems, lands = relayed
    m, nr = len(lands), len(RELAY_RELS)

    def body(*refs):
        land_refs, send_ref, recv_ref = refs[:m], refs[m], refs[m + 1]
        sibling, _ = _rel_peer(1)
        for i in range(m):
            for s, rel in enumerate(RELAY_RELS):
                _, sent = _rel_peer(rel)
                _, arriving = _rel_peer(rel ^ 1)
                copy = pltpu.make_async_remote_copy(
                    src_ref=land_refs[i].at[sent], dst_ref=land_refs[i].at[arriving],
                    send_sem=send_ref.at[i * nr + s], recv_sem=recv_ref.at[i * nr + s],
                    device_id=sibling, device_id_type=MESH)
                copy.wait_send()
                copy.wait_recv()

    outs = pl.pallas_call(
        body, name=name, out_shape=tuple(_hbm(a) for a in lands),
        in_specs=[_HBM_SPEC] * m + [_SEM_SPEC, _SEM_SPEC, pl.BlockSpec(memory_space=pl.ANY)],
        out_specs=tuple([_HBM_SPEC] * m), input_output_aliases={i: i for i in range(m)},
        compiler_params=pltpu.CompilerParams(has_side_effects=_DATAFLOW),
    )(*lands, send_sems, recv_sems, after)
    return list(outs)


def _ffn_fwd(tag, n, wg, wd):
    gu, act = _gate_up_act(f"ffn_gate_up_{tag}", n, wg)
    wd4 = wd.reshape(NFB, FB, D)
    f = _fwd_kblocked(f"ffn_down_{tag}", act, wd4)
    return (n, gu, act, wg, wd4), f


def _ffn_bwd(tag, dh_out, df, h_in, saved, g_pre, send, mixer):
    n, gu, act, wg, wd4 = saved
    dwd = _bwd_w_kblocked(f"ffn_down_dw_{tag}", act, df).reshape(NDEV, DFF // NDEV, D)
    dgu = _down_dx_act_bwd(f"ffn_down_dx_{tag}", df, wd4, gu).reshape(NDEV, S, FB)
    tok = send({f"down_{tag}": dwd, f"gate_up_{tag}": _bwd_w_cols_blocked(f"ffn_gate_up_dw_{tag}", n, dgu)})
    dn = _bwd_x_cols_blocked(f"ffn_gate_up_dx_{tag}", dgu, wg, after=tok)
    dh_in, (dg_pre,), dy, dg_mixer = _rms_bwd(f"ffn_prenorm_bwd_{tag}", h_in, [(g_pre, dn)], dh_out, F32, then=mixer)
    return dh_in, dg_pre, dy, dg_mixer


def kernel(x, positions, mix_norm_pre, mix_norm_post, ffn_norm_pre, ffn_norm_post, ffn_w_gate_up, ffn_w_down, conv_w_in, conv_w, conv_w_out, kv_norm, w_kv, w_q, w_o, loss_target, m_mix_norm_pre, m_mix_norm_post, m_ffn_norm_pre, m_ffn_norm_post, m_ffn_w_gate_up, m_ffn_w_down, m_conv_w_in, m_conv_w, m_conv_w_out, m_kv_norm, m_w_kv, m_w_q, m_w_o, v_mix_norm_pre, v_mix_norm_post, v_ffn_norm_pre, v_ffn_norm_post, v_ffn_w_gate_up, v_ffn_w_down, v_conv_w_in, v_conv_w, v_conv_w_out, v_kv_norm, v_w_kv, v_w_q, v_w_o):
    me = 4 * lax.axis_index("x") + 2 * lax.axis_index("y") + lax.axis_index("c")
    h0 = x.reshape(S, D)
    target = loss_target.reshape(S, D)
    row = lambda a, l: a[l].reshape(1, D)
    g_kv = kv_norm.reshape(1, D)

    cw_shard = jnp.pad(conv_w[0], ((0, 5), (0, 0)))
    names = ["conv_in", "conv_w", "conv_out", "gate_up_0", "down_0", "kv", "q", "o", "gate_up_1", "down_1"]
    shards = [conv_w_in[0], cw_shard, conv_w_out[0], ffn_w_gate_up[0], ffn_w_down[0],
              w_kv, w_q[0], w_o[0], ffn_w_gate_up[1], ffn_w_down[1]]
    shards = [s if n == "conv_w" else s.astype(BF16) for n, s in zip(names, shards)]
    first = 3
    gather_first = _gather_start("gather_start_conv", shards[:first], me)
    gather_rest = _gather_start("gather_start_rest", shards[first:], me)

    def direct(group, after):
        ts = [names.index(n) for n in group]
        started, ts = (gather_first, ts) if ts[0] < first else (gather_rest, [t - first for t in ts])
        lands = _gather_wait(f"gather_wait_{group[0]}", started, ts, after)
        return _relay_start(f"relay_start_{group[0]}", lands)

    def finish(group, relayed, after):
        return dict(zip(group, _relay_wait(f"relay_wait_{group[0]}", relayed, after)))

    sent = {}

    def send(grads):
        started, token = _exchange_start(f"scatter_start_{next(iter(grads))}", list(grads.values()), me, "scatter")
        for i, name in enumerate(grads):
            sent[name] = (started, i)
        return token

    groups = [["conv_in", "conv_w", "conv_out"], ["gate_up_0", "down_0"], ["kv", "q"], ["o", "gate_up_1", "down_1"]]
    n0 = _rms_fwd("mix_prenorm_0", h0, [row(mix_norm_pre, 0)])[0]
    half = HEAD_DIM // 2
    inv_freq = ROPE_THETA ** (-jnp.arange(half, dtype=F32) / half)
    tables = _rope_tables("rope_tables", positions.reshape(S, 1), jnp.tile(inv_freq, 4).reshape(1, 128))
    w = finish(groups[0], direct(groups[0], tables[0]), n0)
    win = w["conv_in"].transpose(1, 0, 2).reshape(D, 3 * D)
    cw = w["conv_w"].transpose(1, 0, 2).reshape(8, D)
    wout = w["conv_out"].reshape(D, D)
    z = _fwd_rows("conv_in", n0, win, BF16)
    pre = _conv_fwd("conv_gate", z, cw)
    relayed = direct(groups[1], pre)
    y0 = _fwd_rows("conv_out", pre, wout)
    h1, (n1,) = _resid_rms("mix_postnorm_0", h0, y0, row(mix_norm_post, 0), [row(ffn_norm_pre, 0)])
    w = finish(groups[1], relayed, n1)
    ffn0, f0 = _ffn_fwd("0", n1, w["gate_up_0"], w["down_0"])
    relayed = direct(groups[2], ffn0[2])
    h2, (nk, n2) = _resid_rms("ffn_postnorm_0", h1, f0, row(ffn_norm_post, 0), [g_kv, row(mix_norm_pre, 1)])

    w = finish(groups[2], relayed, nk)
    wkv = w["kv"].transpose(1, 0, 2).reshape(D, 2 * QW)
    wq = w["q"].transpose(1, 0, 2).reshape(D, QW)
    qc, kc, vc, o_c, lse_c = [], [], [], [], []
    for g, d in enumerate(DILATIONS):
        q_g, k_g, v_g = _qkv_classes(f"qkv_proj_{g}", n2, nk, wq, wkv, g, d, tables)
        qc.append(q_g)
        kc.append(k_g)
        vc.append(v_g)
    relayed = direct(groups[3], vc[-1])
    for g, d in enumerate(DILATIONS):
        o_g, lse_g = _attn_fwd(f"attn_fwd_{g}", qc[g], kc[g], vc[g], d)
        o_c.append(o_g)
        lse_c.append(lse_g)
    o_mix = _mix_fwd("attn_mix", o_c, lse_c)
    w = finish(groups[3], relayed, o_mix)
    wo = w["o"].reshape(D, D)
    y1 = _fwd_rows("attn_out", o_mix, wo)
    h3, (n3,) = _resid_rms("mix_postnorm_1", h2, y1, row(mix_norm_post, 1), [row(ffn_norm_pre, 1)])
    ffn1, f1 = _ffn_fwd("1", n3, w["gate_up_1"], w["down_1"])

    dh4, df1, dg_fpost1, sq = _resid_rms_loss("ffn_postnorm_1_loss", h3, f1, row(ffn_norm_post, 1), target)

    dh3, dg_fpre1, dy1, dg_mpost1 = _ffn_bwd(
        "1", dh4, df1, h3, ffn1, row(ffn_norm_pre, 1), send, (y1, row(mix_norm_post, 1)))
    dwo = _bwd_w_rows("attn_out_dw", o_mix, dy1).reshape(NDEV, D // NDEV, D)
    do = _bwd_x_rows("attn_out_dx", dy1, wo, BF16)
    lane = jnp.arange(128)
    ones_blockdiag = (lane[:, None] // HEAD_DIM == lane[None, :] // HEAD_DIM).astype(BF16)
    mixed = _mix_bwd("attn_mix_bwd", do, o_c, lse_c, ones_blockdiag)
    branch_grads = [_attn_bwd(f"attn_bwd_{g}", qc[g], kc[g], vc[g], mixed[g], lse_c[g], mixed[3 + g], d)
                    for g, d in enumerate(DILATIONS)]
    dq_raw, dkv = _attn_bwd_post("attn_bwd_post", branch_grads, *tables)
    tok = send({"o": dwo, "kv": _bwd_w_cols("kv_proj_dw", nk, dkv, 2 * QW // NDEV),
                "q": _bwd_w_cols("q_proj_dw", n2, dq_raw, QW // NDEV)})
    dnk = _bwd_x_plain("kv_proj_dx", dkv, wkv, after=tok)
    dn2 = _bwd_x_plain("q_proj_dx", dq_raw, wq)
    dh2, (dg_kv, dg_mpre1), df0, dg_fpost0 = _rms_bwd(
        "kv_and_mix_prenorm_bwd_1", h2, [(g_kv, dnk), (row(mix_norm_pre, 1), dn2)], dh3, F32,
        then=(f0, row(ffn_norm_post, 0)))

    dh1, dg_fpre0, dy0, dg_mpost0 = _ffn_bwd(
        "0", dh2, df0, h1, ffn0, row(ffn_norm_pre, 0), send, (y0, row(mix_norm_post, 0)))
    dwout = _bwd_w_rows("conv_out_dw", pre, dy0).reshape(NDEV, D // NDEV, D)
    dpre = _bwd_x_rows("conv_out_dx", dy0, wout, BF16)
    dz, dcw = _conv_bwd("conv_gate_bwd", z, dpre, cw)
    tok = send({"conv_out": dwout, "conv_in": _bwd_w_cols("conv_in_dw", n0, dz, 3 * D // NDEV)})
    dn0 = _bwd_x_plain("conv_in_dx", dz, win, after=tok)
    dh0, (dg_mpre0,) = _rms_bwd("mix_prenorm_bwd_0", h0, [(row(mix_norm_pre, 0), dn0)], dh1, F32)

    small = _pack_small("pack_small_grads", [dg_mpre0, dg_mpre1, dg_mpost0, dg_mpost1, dg_fpre0, dg_fpre1,
                                             dg_fpost0, dg_fpost1, dg_kv], dcw, sq)

    done = [small]

    def upd(tag, w, m, v):
        parts = _exchange_wait(f"scatter_wait_{tag}", *sent[tag], done[-1], "scatter")
        shape = w.shape
        flat = lambda a: a.reshape(parts.shape[1:])
        res = _adamw(f"adamw_{tag}", parts, flat(w), flat(m), flat(v))
        done.append(res[0])
        return [r.reshape(shape) for r in res]

    def upd_layer(tag, l, w, m, v, other):
        parts = _exchange_wait(f"scatter_wait_{tag}_{l}", *sent[f"{tag}_{l}"], done[-1], "scatter")
        res = _adamw(f"adamw_{tag}_{l}", parts, w, m, v, layer=l, other=other)
        done.append(res[0])
        return list(res)

    res = {}
    down_1 = upd_layer("down", 1, ffn_w_down, m_ffn_w_down, v_ffn_w_down, None)
    gate_up_t = [jnp.swapaxes(a, 1, 2) for a in (ffn_w_gate_up, m_ffn_w_gate_up, v_ffn_w_gate_up)]
    gate_up_1 = upd_layer("gate_up", 1, *gate_up_t, None)
    res["w_o"] = upd("o", w_o, m_w_o, v_w_o)
    res["w_q"] = upd("q", w_q, m_w_q, v_w_q)
    res["w_kv"] = upd("kv", w_kv, m_w_kv, v_w_kv)

    small_all = _exchange("gather_small_grads", [small], "gather", done[-1])[0]
    vec = lambda a: a.reshape(1, D)
    gain_res, taps, loss = _adamw_gains("adamw_gains", small_all, [
        (mix_norm_pre, m_mix_norm_pre, v_mix_norm_pre), (mix_norm_post, m_mix_norm_post, v_mix_norm_post),
        (ffn_norm_pre, m_ffn_norm_pre, v_ffn_norm_pre), (ffn_norm_post, m_ffn_norm_post, v_ffn_norm_post),
        (vec(kv_norm), vec(m_kv_norm), vec(v_kv_norm))])
    dcw_mine = lax.dynamic_slice(taps, (0, me * 128), (8, 128))
    pad8 = lambda a, fill: jnp.pad(a[0], ((0, 5), (0, 0)), constant_values=fill)
    cw_res = [r[0:3].reshape(1, 3, 128) for r in
              _adamw("adamw_conv_w", dcw_mine.reshape(1, 8, 128), cw_shard, pad8(m_conv_w, 0.0), pad8(v_conv_w, 1.0))]

    res.update({
        "mix_norm_pre": gain_res[0],
        "mix_norm_post": gain_res[1],
        "ffn_norm_pre": gain_res[2],
        "ffn_norm_post": gain_res[3],
        "kv_norm": [r.reshape(D) for r in gain_res[4]],
        "conv_w": cw_res,
    })
    done.append(small_all)
    res["ffn_w_down"] = upd_layer("down", 0, ffn_w_down, m_ffn_w_down, v_ffn_w_down, down_1)
    res["ffn_w_gate_up"] = [jnp.swapaxes(r, 1, 2) for r in upd_layer("gate_up", 0, *gate_up_t, gate_up_1)]
    res["conv_w_out"] = upd("conv_out", conv_w_out, m_conv_w_out, v_conv_w_out)
    res["conv_w_in"] = upd("conv_in", conv_w_in, m_conv_w_in, v_conv_w_in)
    order = ["mix_norm_pre", "mix_norm_post", "ffn_norm_pre", "ffn_norm_post", "ffn_w_gate_up", "ffn_w_down",
             "conv_w_in", "conv_w", "conv_w_out", "kv_norm", "w_kv", "w_q", "w_o"]
    out = [loss, dh0.reshape(1, S, D)]
    for i in range(4):
        out += [res[name][i] for name in order]
    return tuple(out)
```

```python
import jax
import jax.numpy as jnp
from jax import lax
from jax.experimental import pallas as pl
from jax.experimental.pallas import tpu as pltpu

F32 = jnp.float32
BF16 = jnp.bfloat16

S = 4096
D = 1024
NDEV = 8
HEAD_DIM = 64
QW = 3072
DFF = 2816
FB = 704
NFB = 4
BRANCHES = ((128, 1), (512, 4), (2048, 16))
BAND = 128
ROPE_THETA = 10000.0
RMS_EPS = 1e-6
NEG_INF = -1e30
ADAM_LR, ADAM_B1, ADAM_B2, ADAM_EPS, ADAM_WD, ADAM_STEP = 0.001, 0.9, 0.999, 1e-08, 0.01, 10

VMEM_LIMIT_BYTES = 52 * 1024 * 1024
ROW_TILE = 512
MESH = pl.DeviceIdType.MESH


def _cparams(ngrid):
    return pltpu.CompilerParams(dimension_semantics=("arbitrary",) * ngrid,
                                vmem_limit_bytes=VMEM_LIMIT_BYTES)


def _sds(shape, dtype):
    return jax.ShapeDtypeStruct(tuple(shape), dtype)


_DIMS = {"nn": (((1,), (0,)), ((), ())),
         "nt": (((1,), (1,)), ((), ())),
         "tn": (((0,), (0,)), ((), ()))}


def _matmul(name, a, b, *, mode, grid, a_blk, a_map, b_blk, b_map, o_shape, o_blk, o_map, out_dtype, after=None,
            out_groups=1):
    nk = grid[2]
    dims = _DIMS[mode]
    acc_shape = tuple(s for s in o_blk if s is not None)
    if out_groups > 1:
        acc_shape = (acc_shape[1], out_groups * acc_shape[2])
    extra = [] if after is None else [after]

    def store(o_ref, val):
        if out_groups == 1:
            o_ref[...] = val.astype(o_ref.dtype)
        else:
            n = o_ref.shape[-1]
            for grp in range(out_groups):
                o_ref[grp] = val[:, grp * n:(grp + 1) * n].astype(o_ref.dtype)

    def body(a_ref, b_ref, *rest):
        o_ref, scratch = rest[len(extra)], rest[len(extra) + 1:]
        part = lax.dot_general(a_ref[...], b_ref[...], dims, preferred_element_type=F32)
        if nk == 1:
            store(o_ref, part)
            return
        acc_ref = scratch[0]
        k = pl.program_id(2)

        @pl.when(k == 0)
        def _():
            acc_ref[...] = part

        @pl.when(k > 0)
        def _():
            acc_ref[...] += part

        @pl.when(k == nk - 1)
        def _():
            store(o_ref, acc_ref[...])

    return pl.pallas_call(
        body, name=name, grid=grid,
        in_specs=[pl.BlockSpec(a_blk, a_map), pl.BlockSpec(b_blk, b_map)] + [pl.BlockSpec(memory_space=pl.ANY)] * len(extra),
        out_specs=pl.BlockSpec(o_blk, o_map),
        out_shape=_sds(o_shape, out_dtype),
        scratch_shapes=[] if nk == 1 else [pltpu.VMEM(acc_shape, F32)],
        compiler_params=_cparams(3),
    )(a, b, *extra)


TM = 1024
TK = S


def _fwd_rows(name, a, w, out_dtype=F32):
    kdim, n = w.shape
    tn = 512
    return _matmul(name, a, w, mode="nn", grid=(S // TM, n // tn, 1),
                   a_blk=(TM, kdim), a_map=lambda i, j, k: (i, 0),
                   b_blk=(kdim, tn), b_map=lambda i, j, k: (0, j),
                   o_shape=(S, n), o_blk=(TM, tn), o_map=lambda i, j, k: (i, j), out_dtype=out_dtype)


def _fwd_kblocked(name, a4, w4):
    nb, _, kb = a4.shape
    n = w4.shape[2]

    def body(a_ref, w_ref, o_ref):
        acc = _dot_nn(a_ref[0], w_ref[0])
        for j in range(1, nb):
            acc = acc + _dot_nn(a_ref[j], w_ref[j])
        o_ref[...] = acc

    return pl.pallas_call(
        body, name=name, grid=(S // TM,),
        in_specs=[pl.BlockSpec((nb, TM, kb), lambda i: (0, i, 0)), pl.BlockSpec((nb, kb, n), lambda i: (0, 0, 0))],
        out_specs=pl.BlockSpec((TM, n), lambda i: (i, 0)), out_shape=_sds((S, n), F32),
        compiler_params=_cparams(1),
    )(a4, w4)


def _bwd_x_cols_blocked(name, dy8, wg, after):
    _, kdim, n = wg.shape
    nk = NDEV // 2

    def body(a_ref, b_ref, after_ref, o_ref, acc_ref):
        k = pl.program_id(1)
        part = _dot_nt(a_ref[0], b_ref[0]) + _dot_nt(a_ref[1], b_ref[1])

        @pl.when(k == 0)
        def _():
            acc_ref[...] = part

        @pl.when(k > 0)
        def _():
            acc_ref[...] += part

        @pl.when(k == nk - 1)
        def _():
            o_ref[...] = acc_ref[...].astype(o_ref.dtype)

    return pl.pallas_call(
        body, name=name, grid=(S // TM, nk),
        in_specs=[pl.BlockSpec((2, None, TM, n), lambda i, k: (0, k, i, 0)),
                  pl.BlockSpec((2, None, kdim, n), lambda i, k: (0, k, 0, 0)),
                  pl.BlockSpec(memory_space=pl.ANY)],
        out_specs=pl.BlockSpec((TM, kdim), lambda i, k: (i, 0)), out_shape=_sds((S, kdim), BF16),
        scratch_shapes=[pltpu.VMEM((TM, kdim), F32)],
        compiler_params=_cparams(2),
    )(dy8.reshape(2, nk, S, n), wg.reshape(2, nk, kdim, n), after)


def _bwd_x_rows(name, dy, w, out_dtype, after=None):
    kdim, n = w.shape
    tkk = 512
    return _matmul(name, dy, w, mode="nt", grid=(S // TM, kdim // tkk, 1),
                   a_blk=(TM, n), a_map=lambda i, j, k: (i, 0),
                   b_blk=(tkk, n), b_map=lambda i, j, k: (j, 0),
                   o_shape=(S, kdim), o_blk=(TM, tkk), o_map=lambda i, j, k: (i, j), out_dtype=out_dtype, after=after)


DW_COLS = 768


def _bwd_w_cols(name, a, dy, n):
    kdim = a.shape[1]
    groups = DW_COLS // n
    return _matmul(name, a, dy, mode="tn", grid=(1, NDEV // groups, S // TK),
                   a_blk=(TK, kdim), a_map=lambda i, j, k: (k, 0),
                   b_blk=(TK, DW_COLS), b_map=lambda i, j, k: (k, j),
                   o_shape=(NDEV, kdim, n), o_blk=(groups, kdim, n) if groups > 1 else (None, kdim, n),
                   o_map=lambda i, j, k: (j, 0, 0), out_dtype=BF16, out_groups=groups)


def _bwd_x_plain(name, dy, w, after=None):
    kdim, n = w.shape
    tm = TM if n <= 3 * D else TM // 2
    return _matmul(name, dy, w, mode="nt", grid=(S // tm, 1, 1),
                   a_blk=(tm, n), a_map=lambda i, j, k: (i, 0),
                   b_blk=(kdim, n), b_map=lambda i, j, k: (0, 0),
                   o_shape=(S, kdim), o_blk=(tm, kdim), o_map=lambda i, j, k: (i, 0), out_dtype=BF16, after=after)


def _bwd_w_cols_blocked(name, a, dy8):
    kdim = a.shape[1]
    n = dy8.shape[2]
    return _matmul(name, dy8, a, mode="tn", grid=(1, NDEV, S // TK),
                   a_blk=(None, TK, n), a_map=lambda i, j, k: (j, k, 0),
                   b_blk=(TK, kdim), b_map=lambda i, j, k: (k, 0),
                   o_shape=(NDEV, n, kdim), o_blk=(None, n, kdim), o_map=lambda i, j, k: (j, 0, 0), out_dtype=BF16)


def _bwd_w_rows(name, a, dy):
    kdim = a.shape[1]
    n = dy.shape[1]
    tmm = 512
    return _matmul(name, a, dy, mode="tn", grid=(kdim // tmm, 1, S // TK),
                   a_blk=(TK, tmm), a_map=lambda i, j, k: (k, i),
                   b_blk=(TK, n), b_map=lambda i, j, k: (k, 0),
                   o_shape=(kdim, n), o_blk=(tmm, n), o_map=lambda i, j, k: (i, 0), out_dtype=BF16)


def _bwd_w_kblocked(name, a4, dy):
    nb, _, kb = a4.shape
    n = dy.shape[1]
    return _matmul(name, a4, dy, mode="tn", grid=(nb, 1, S // TK),
                   a_blk=(None, TK, kb), a_map=lambda i, j, k: (i, k, 0),
                   b_blk=(TK, n), b_map=lambda i, j, k: (k, 0),
                   o_shape=(nb, kb, n), o_blk=(None, kb, n), o_map=lambda i, j, k: (i, 0, 0), out_dtype=BF16)


def _rstd(x):
    return lax.rsqrt(jnp.mean(x * x, axis=-1, keepdims=True) + RMS_EPS)


def _row_spec(tm=ROW_TILE, width=D):
    return pl.BlockSpec((tm, width), lambda i: (i, 0))


def _vec_spec(rows=1, width=D):
    return pl.BlockSpec((rows, width), lambda i: (0, 0))


def _rms_fwd(name, x, gains):
    n = len(gains)

    def body(x_ref, *refs):
        x_val = x_ref[...]
        xh = x_val * _rstd(x_val)
        for g_ref, o_ref in zip(refs[:n], refs[n:]):
            o_ref[...] = (xh * g_ref[...]).astype(o_ref.dtype)

    outs = pl.pallas_call(
        body, name=name, grid=(S // ROW_TILE,),
        in_specs=[_row_spec()] + [_vec_spec()] * n,
        out_specs=[_row_spec()] * n,
        out_shape=[_sds((S, D), BF16)] * n,
        compiler_params=_cparams(1),
    )(x, *gains)
    return list(outs)


def _resid_rms(name, h, y, g, next_gains):
    n = len(next_gains)

    def body(h_ref, y_ref, g_ref, *refs):
        y_val = y_ref[...]
        h_new = h_ref[...] + (y_val * _rstd(y_val)) * g_ref[...]
        refs[n][...] = h_new
        hh = h_new * _rstd(h_new)
        for g2_ref, o_ref in zip(refs[:n], refs[n + 1:]):
            o_ref[...] = (hh * g2_ref[...]).astype(o_ref.dtype)

    outs = pl.pallas_call(
        body, name=name, grid=(S // ROW_TILE,),
        in_specs=[_row_spec(), _row_spec(), _vec_spec()] + [_vec_spec()] * n,
        out_specs=[_row_spec()] * (n + 1), out_shape=[_sds((S, D), F32)] + [_sds((S, D), BF16)] * n,
        compiler_params=_cparams(1),
    )(h, y, g, *next_gains)
    return outs[0], list(outs[1:])


def _resid_rms_loss(name, h, y, g, target):
    def body(h_ref, y_ref, g_ref, t_ref, dh_ref, dy_ref, dg_ref, part_ref):
        y_val = y_ref[...]
        gain = g_ref[...]
        e = h_ref[...] + (y_val * _rstd(y_val)) * gain - t_ref[...]
        dh = e * (1.0 / D)
        dh_ref[...] = dh
        step = pl.program_id(0)
        dy_ref[...] = _norm_bwd_rows(y_val, gain, dh, dg_ref, step).astype(dy_ref.dtype)
        part = jnp.sum(e * e, axis=0, keepdims=True)

        @pl.when(step == 0)
        def _():
            part_ref[...] = part

        @pl.when(step > 0)
        def _():
            part_ref[...] += part

    return pl.pallas_call(
        body, name=name, grid=(S // ROW_TILE,),
        in_specs=[_row_spec(), _row_spec(), _vec_spec(), _row_spec()],
        out_specs=[_row_spec(), _row_spec(), _vec_spec(8), _vec_spec()],
        out_shape=[_sds((S, D), F32), _sds((S, D), BF16), _sds((8, D), F32), _sds((1, D), F32)],
        compiler_params=_cparams(1),
    )(h, y, g, target)


def _norm_bwd_rows(x_val, g, dn, dg_ref, step):
    r = _rstd(x_val)
    xh = x_val * r
    dxh = dn * g
    part = jnp.sum(dn * xh, axis=0, keepdims=True)

    @pl.when(step == 0)
    def _():
        dg_ref[...] = jnp.zeros_like(dg_ref)

    dg_ref[0:1, :] += part
    return r * (dxh - xh * jnp.mean(dxh * xh, axis=-1, keepdims=True))


def _rms_bwd(name, x, pairs, dres, out_dtype, then=None):
    n = len(pairs)
    has_res = dres is not None
    chained = then is not None

    def body(x_ref, *refs):
        g_refs = refs[0:2 * n:2]
        dn_refs = refs[1:2 * n:2]
        pos = 2 * n
        res_ref = refs[pos] if has_res else None
        pos += int(has_res)
        if chained:
            y_ref, gy_ref = refs[pos], refs[pos + 1]
            pos += 2
        dx_ref = refs[pos]
        dg_refs = refs[pos + 1:pos + 1 + n]
        step = pl.program_id(0)
        x_val = x_ref[...]
        acc = res_ref[...] if has_res else jnp.zeros_like(x_val)
        for g_ref, dn_ref, dg_ref in zip(g_refs, dn_refs, dg_refs):
            acc = acc + _norm_bwd_rows(x_val, g_ref[...], dn_ref[...].astype(F32), dg_ref, step)
        dx_ref[...] = acc.astype(dx_ref.dtype)
        if chained:
            dy_ref, dgy_ref = refs[pos + 1 + n], refs[pos + 2 + n]
            dy_ref[...] = _norm_bwd_rows(y_ref[...], gy_ref[...], acc, dgy_ref, step).astype(dy_ref.dtype)

    operands = [x]
    in_specs = [_row_spec()]
    for g, dn in pairs:
        operands += [g, dn]
        in_specs += [_vec_spec(), _row_spec()]
    if has_res:
        operands.append(dres)
        in_specs.append(_row_spec())
    if chained:
        operands += [then[0], then[1]]
        in_specs += [_row_spec(), _vec_spec()]
    extra = int(chained)
    outs = pl.pallas_call(
        body, name=name, grid=(S // ROW_TILE,),
        in_specs=in_specs,
        out_specs=[_row_spec()] + [_vec_spec(8)] * n + [_row_spec(), _vec_spec(8)] * extra,
        out_shape=[_sds((S, D), out_dtype)] + [_sds((8, D), F32)] * n + [_sds((S, D), BF16), _sds((8, D), F32)] * extra,
        compiler_params=_cparams(1),
    )(*operands)
    if chained:
        return outs[0], list(outs[1:1 + n]), outs[1 + n], outs[2 + n]
    return outs[0], list(outs[1:])


def _shift_down(u, prev8, k):
    r = pltpu.roll(u, k, 0)
    p = pltpu.roll(prev8, k, 0)
    row = lax.broadcasted_iota(jnp.int32, prev8.shape, 0)
    top = jnp.where(row < k, p, r[0:8])
    return jnp.concatenate([top, r[8:]], axis=0)


def _shift_up(u, next8, k):
    tm = u.shape[0]
    r = pltpu.roll(u, tm - k, 0)
    p = pltpu.roll(next8, 8 - k, 0)
    row = lax.broadcasted_iota(jnp.int32, next8.shape, 0)
    bot = jnp.where(row >= 8 - k, p, r[tm - 8:tm])
    return jnp.concatenate([r[:tm - 8], bot], axis=0)


CONV_TILE = 512


def _halo_prev(col):
    return pl.BlockSpec((8, D), lambda i: (jnp.maximum(i * (CONV_TILE // 8) - 1, 0), col))


def _halo_next(col):
    last = S // 8 - 1
    return pl.BlockSpec((8, D), lambda i: (jnp.minimum((i + 1) * (CONV_TILE // 8), last), col))


def _conv_fwd(name, z, cw):
    def body(b_ref, c_ref, h_ref, cp_ref, hp_ref, cw_ref, o_ref):
        i = pl.program_id(0)
        u = c_ref[...].astype(F32) * h_ref[...].astype(F32)
        up = cp_ref[...].astype(F32) * hp_ref[...].astype(F32)
        up = jnp.where(i > 0, up, 0.0)
        cv = cw_ref[0:1, :] * _shift_down(u, up, 2) + cw_ref[1:2, :] * _shift_down(u, up, 1) + cw_ref[2:3, :] * u
        o_ref[...] = (b_ref[...].astype(F32) * cv).astype(o_ref.dtype)

    col = lambda c: pl.BlockSpec((CONV_TILE, D), lambda i: (i, c))
    return pl.pallas_call(
        body, name=name, grid=(S // CONV_TILE,),
        in_specs=[col(0), col(1), col(2), _halo_prev(1), _halo_prev(2), _vec_spec(8)],
        out_specs=_row_spec(CONV_TILE), out_shape=_sds((S, D), BF16),
        compiler_params=_cparams(1),
    )(z, z, z, z, z, cw)


def _conv_bwd(name, z, dpre, cw):
    nsteps = S // CONV_TILE

    def body(b_ref, c_ref, h_ref, cp_ref, hp_ref, dp_ref, dpn_ref, bn_ref, cw_ref, dz_ref, dcw_ref):
        i = pl.program_id(0)
        b = b_ref[...].astype(F32)
        c = c_ref[...].astype(F32)
        h = h_ref[...].astype(F32)
        dp = dp_ref[...].astype(F32)
        u = c * h
        up = jnp.where(i > 0, cp_ref[...].astype(F32) * hp_ref[...].astype(F32), 0.0)
        s1 = _shift_down(u, up, 1)
        s2 = _shift_down(u, up, 2)
        w0, w1, w2 = cw_ref[0:1, :], cw_ref[1:2, :], cw_ref[2:3, :]
        cv = w0 * s2 + w1 * s1 + w2 * u
        dcv = dp * b
        dcvn = jnp.where(i < nsteps - 1, dpn_ref[...].astype(F32) * bn_ref[...].astype(F32), 0.0)
        du = w2 * dcv + w1 * _shift_up(dcv, dcvn, 1) + w0 * _shift_up(dcv, dcvn, 2)
        dz_ref[:, 0:D] = (dp * cv).astype(dz_ref.dtype)
        dz_ref[:, D:2 * D] = (du * h).astype(dz_ref.dtype)
        dz_ref[:, 2 * D:3 * D] = (du * c).astype(dz_ref.dtype)

        @pl.when(i == 0)
        def _():
            dcw_ref[...] = jnp.zeros_like(dcw_ref)

        dcw_ref[0:1, :] += jnp.sum(dcv * s2, axis=0, keepdims=True)
        dcw_ref[1:2, :] += jnp.sum(dcv * s1, axis=0, keepdims=True)
        dcw_ref[2:3, :] += jnp.sum(dcv * u, axis=0, keepdims=True)

    col = lambda c: pl.BlockSpec((CONV_TILE, D), lambda i: (i, c))
    return pl.pallas_call(
        body, name=name, grid=(nsteps,),
        in_specs=[col(0), col(1), col(2), _halo_prev(1), _halo_prev(2),
                  _row_spec(CONV_TILE), _halo_next(0), _halo_next(0), _vec_spec(8)],
        out_specs=[pl.BlockSpec((CONV_TILE, 3 * D), lambda i: (i, 0)), _vec_spec(8)],
        out_shape=[_sds((S, 3 * D), BF16), _sds((8, D), F32)],
        compiler_params=_cparams(1),
    )(z, z, z, z, z, dpre, dpre, z, cw)


FFN_TM = 2048
_GU_BLOCK = pl.BlockSpec((2, None, FFN_TM, FB), lambda i, j: (0, j, i, 0))


def _gate_up_act(name, a, wg):
    kdim = a.shape[1]

    def body(a_ref, wgate_ref, wup_ref, gu_ref, act_ref):
        x = a_ref[...]
        g = _dot_nn(x, wgate_ref[...])
        u = _dot_nn(x, wup_ref[...])
        gu_ref[0] = g.astype(gu_ref.dtype)
        gu_ref[1] = u.astype(gu_ref.dtype)
        act_ref[...] = (g * jax.nn.sigmoid(g) * u).astype(act_ref.dtype)

    return pl.pallas_call(
        body, name=name, grid=(S // FFN_TM, NFB),
        in_specs=[pl.BlockSpec((FFN_TM, kdim), lambda i, j: (i, 0)),
                  pl.BlockSpec((None, kdim, FB), lambda i, j: (j, 0, 0)),
                  pl.BlockSpec((None, kdim, FB), lambda i, j: (j + NFB, 0, 0))],
        out_specs=[_GU_BLOCK, pl.BlockSpec((None, FFN_TM, FB), lambda i, j: (j, i, 0))],
        out_shape=[_sds((2, NFB, S, FB), BF16), _sds((NFB, S, FB), BF16)],
        compiler_params=_cparams(2),
    )(a, wg, wg)


def _down_dx_act_bwd(name, df, w4, gu):
    _, kb, n = w4.shape

    def body(df_ref, w_ref, gu_ref, o_ref):
        d = _dot_nt(df_ref[...], w_ref[...])
        g = gu_ref[0].astype(F32)
        u = gu_ref[1].astype(F32)
        sg = jax.nn.sigmoid(g)
        o_ref[0] = (d * u * sg * (1.0 + g * (1.0 - sg))).astype(o_ref.dtype)
        o_ref[1] = (d * g * sg).astype(o_ref.dtype)

    return pl.pallas_call(
        body, name=name, grid=(S // FFN_TM, NFB),
        in_specs=[pl.BlockSpec((FFN_TM, n), lambda i, j: (i, 0)), pl.BlockSpec((None, kb, n), lambda i, j: (j, 0, 0)),
                  _GU_BLOCK],
        out_specs=_GU_BLOCK, out_shape=_sds((2, NFB, S, FB), BF16),
        compiler_params=_cparams(2),
    )(df, w4, gu)


def _rope_tables(name, pos_col, inv_freq_row):
    def body(pos_ref, f_ref, cos_ref, sin_ref):
        ang = pos_ref[...].astype(F32) * f_ref[...]
        lane = lax.broadcasted_iota(jnp.int32, ang.shape, 1)
        s = jnp.sin(ang)
        cos_ref[...] = jnp.cos(ang)
        sin_ref[...] = jnp.where((lane % HEAD_DIM) < HEAD_DIM // 2, -s, s)

    tab = pl.BlockSpec((ROW_TILE, 128), lambda i: (i, 0))
    return pl.pallas_call(
        body, name=name, grid=(S // ROW_TILE,),
        in_specs=[pl.BlockSpec((ROW_TILE, 1), lambda i: (i, 0)), _vec_spec(1, 128)],
        out_specs=[tab, tab], out_shape=[_sds((S, 128), F32)] * 2,
        compiler_params=_cparams(1),
    )(pos_col, inv_freq_row)


def _swap_halves(t):
    lane = lax.broadcasted_iota(jnp.int32, t.shape, 1)
    first = (lane % HEAD_DIM) < HEAD_DIM // 2
    return jnp.where(first, pltpu.roll(t, 128 - HEAD_DIM // 2, 1), pltpu.roll(t, HEAD_DIM // 2, 1))


NCHUNK = D // 128


def _chunk(c, base=0):
    return slice(base + c * 128, base + (c + 1) * 128)


def _class_rows(r, d, tm):
    return pl.ds(r, tm // d, stride=d) if d > 1 else slice(None)


def _class_block(d, tm):
    return pl.BlockSpec((tm // d, d * D), lambda i: (i, 0))


def _tokens_from_classes(blk_ref, tmp_ref, d, tm):
    for r in range(d):
        for c in range(NCHUNK):
            tmp_ref[c, _class_rows(r, d, tm), :] = blk_ref[:, _chunk(c, r * D)].astype(F32)


def _classes_from_tokens(tmp_ref, blk_ref, d, tm):
    for r in range(d):
        for c in range(NCHUNK):
            blk_ref[:, _chunk(c, r * D)] = tmp_ref[c, _class_rows(r, d, tm), :].astype(blk_ref.dtype)


def _qkv_classes(name, n2, nk, wq, wkv, g, d, tables):
    def emit(acc, cos_ref, sin_ref, o_ref, tmp_ref, scale):
        for c in range(NCHUNK):
            tmp_ref[c] = acc[:, _chunk(c)]
        for r in range(d):
            rows = _class_rows(r, d, TM)
            if scale is not None:
                cs = cos_ref[rows, :]
                sn = sin_ref[rows, :]
            for c in range(NCHUNK):
                x = tmp_ref[c, rows, :]
                if scale is not None:
                    x = (x * cs + _swap_halves(x) * sn) * scale
                o_ref[:, _chunk(c, r * D)] = x.astype(o_ref.dtype)

    def body(n2_ref, nk_ref, wq_ref, wk_ref, wv_ref, cos_ref, sin_ref, q_ref, k_ref, v_ref, tmp_ref):
        emit(_dot_nn(n2_ref[...], wq_ref[...]), cos_ref, sin_ref, q_ref, tmp_ref, HEAD_DIM ** -0.5)
        x = nk_ref[...]
        emit(_dot_nn(x, wk_ref[...]), cos_ref, sin_ref, k_ref, tmp_ref, 1.0)
        emit(_dot_nn(x, wv_ref[...]), cos_ref, sin_ref, v_ref, tmp_ref, None)

    nbr = len(DILATIONS)
    act = pl.BlockSpec((TM, D), lambda i: (i, 0))
    tab = pl.BlockSpec((TM, 128), lambda i: (i, 0))
    wcol = lambda col: pl.BlockSpec((D, D), lambda i: (0, col))
    return pl.pallas_call(
        body, name=name, grid=(S // TM,),
        in_specs=[act, act, wcol(g), wcol(g), wcol(nbr + g), tab, tab],
        out_specs=[_class_block(d, TM)] * 3, out_shape=[_sds((S // d, d * D), BF16)] * 3,
        scratch_shapes=[pltpu.VMEM((NCHUNK, TM, 128), F32)],
        compiler_params=_cparams(1),
    )(n2, nk, wq, wkv, wkv, *tables)


ATTN_CHAINS = 16


def _attn_units(d):
    nblk = S // d // BAND
    return max(1, 2 * ATTN_CHAINS // nblk)


def _class_spec(d):
    return pl.BlockSpec((S // d, 128 * _attn_units(d)), lambda cb: (0, cb))


def _dot_nt(a, b):
    return lax.dot_general(a, b, _DIMS["nt"], preferred_element_type=F32)


def _dot_tn(a, b):
    return lax.dot_general(a, b, _DIMS["tn"], preferred_element_type=F32)


def _dot_nn(a, b):
    return lax.dot_general(a, b, _DIMS["nn"], preferred_element_type=F32)


def _band_mask(nkeys):
    qi = lax.broadcasted_iota(jnp.int32, (2 * BAND, nkeys), 0) % BAND
    kj = lax.broadcasted_iota(jnp.int32, (2 * BAND, nkeys), 1)
    if nkeys == BAND:
        return kj <= qi
    dist = qi + BAND - kj
    return (dist >= 0) & (dist <= BAND)


def _band_bias():
    return {n: jnp.where(_band_mask(n), 0.0, NEG_INF).astype(F32) for n in (BAND, 2 * BAND)}


def _stack_heads(x):
    row = lax.broadcasted_iota(jnp.int32, (2 * BAND, 128), 0)
    lane = lax.broadcasted_iota(jnp.int32, (2 * BAND, 128), 1)
    keep = (row < BAND) == (lane < HEAD_DIM)
    return jnp.where(keep, jnp.concatenate([x, x], axis=0), jnp.zeros((), x.dtype))


def _unstack(x2):
    first_head = lax.broadcasted_iota(jnp.int32, (BAND, 128), 1) < HEAD_DIM
    return jnp.where(first_head, x2[:BAND], x2[BAND:])


def _for_later_blocks(nblk, units, fn):
    all_lanes = [slice(u * 128, (u + 1) * 128) for u in range(units)]
    unroll = max(1, ATTN_CHAINS // units)
    trips = (nblk - 1) // unroll
    if trips > 1:
        def step(i, carry):
            for j in range(unroll):
                for lanes in all_lanes:
                    fn(pl.multiple_of((1 + i * unroll + j) * BAND, BAND), lanes)
            return carry

        lax.fori_loop(0, trips, step, 0)
    else:
        trips = 0
    for sb in range(1 + trips * unroll, nblk):
        for lanes in all_lanes:
            fn(sb * BAND, lanes)


def _attn_fwd(name, q, k, v, d):
    nblk = S // d // BAND
    units = _attn_units(d)

    def body(q_ref, k_ref, v_ref, o_ref, lse_ref):
        bias = _band_bias()

        def block(r0, k0, nkeys, lanes):
            q2 = _stack_heads(q_ref[pl.ds(r0, BAND), lanes])
            s = _dot_nt(q2, k_ref[pl.ds(k0, nkeys), lanes]) + bias[nkeys]
            m = jnp.max(s, axis=-1, keepdims=True)
            p = jnp.exp(s - m)
            l = jnp.sum(p, axis=-1, keepdims=True)
            o2 = _dot_nn(p.astype(BF16), v_ref[pl.ds(k0, nkeys), lanes])
            l_tile = _unstack(jnp.broadcast_to(l, (2 * BAND, 128)))
            m_tile = _unstack(jnp.broadcast_to(m, (2 * BAND, 128)))
            o_ref[pl.ds(r0, BAND), lanes] = (_unstack(o2) / l_tile).astype(o_ref.dtype)
            lse_ref[pl.ds(r0, BAND), lanes] = m_tile + jnp.log(l_tile)

        for u in range(units):
            block(0, 0, BAND, slice(u * 128, (u + 1) * 128))

        _for_later_blocks(nblk, units, lambda r0, lanes: block(r0, r0 - BAND, 2 * BAND, lanes))

    spec = _class_spec(d)
    return pl.pallas_call(
        body, name=name, grid=(8 * d // units,),
        in_specs=[spec] * 3, out_specs=[spec] * 2,
        out_shape=[_sds((S // d, d * D), BF16), _sds((S // d, d * D), F32)],
        compiler_params=_cparams(1),
    )(q, k, v)


def _attn_bwd(name, q, k, v, do, lse, dd, d):
    nblk = S // d // BAND
    units = _attn_units(d)

    def body(q_ref, k_ref, v_ref, do_ref, lse_ref, dd_ref, dq_ref, dk_out, dv_out, dk_ref, dv_ref):
        bias = _band_bias()
        def column(ref, r0, lanes, nkeys):
            tile = ref[pl.ds(r0, BAND), lanes]
            other = pltpu.roll(tile, HEAD_DIM, 1)
            first_head = lax.broadcasted_iota(jnp.int32, tile.shape, 1) < HEAD_DIM
            both = jnp.concatenate([jnp.where(first_head, tile, other), jnp.where(first_head, other, tile)], axis=0)
            return both if nkeys == BAND else jnp.concatenate([both, both], axis=1)

        def block(r0, k0, nkeys, lanes, first):
            q2 = _stack_heads(q_ref[pl.ds(r0, BAND), lanes])
            do2 = _stack_heads(do_ref[pl.ds(r0, BAND), lanes])
            kk = k_ref[pl.ds(k0, nkeys), lanes]
            vv = v_ref[pl.ds(k0, nkeys), lanes]
            s = _dot_nt(q2, kk) + bias[nkeys]
            p = jnp.exp(s - column(lse_ref, r0, lanes, nkeys))
            ds = (p * (_dot_nt(do2, vv) - column(dd_ref, r0, lanes, nkeys))).astype(BF16)
            dq_ref[pl.ds(r0, BAND), lanes] = _unstack(_dot_nn(ds, kk)).astype(dq_ref.dtype)
            dk_part = _dot_tn(ds, q2)
            dv_part = _dot_tn(p.astype(BF16), do2)
            if first:
                dk_ref[pl.ds(k0, nkeys), lanes] = dk_part
                dv_ref[pl.ds(k0, nkeys), lanes] = dv_part
            else:
                dk_ref[pl.ds(k0, BAND), lanes] += dk_part[:BAND]
                dv_ref[pl.ds(k0, BAND), lanes] += dv_part[:BAND]
                dk_ref[pl.ds(k0 + BAND, BAND), lanes] = dk_part[BAND:]
                dv_ref[pl.ds(k0 + BAND, BAND), lanes] = dv_part[BAND:]

        for u in range(units):
            block(0, 0, BAND, slice(u * 128, (u + 1) * 128), True)

        _for_later_blocks(nblk, units, lambda r0, lanes: block(r0, r0 - BAND, 2 * BAND, lanes, False))
        dk_out[...] = dk_ref[...].astype(dk_out.dtype)
        dv_out[...] = dv_ref[...].astype(dv_out.dtype)

    spec = _class_spec(d)
    return pl.pallas_call(
        body, name=name, grid=(8 * d // units,),
        in_specs=[spec] * 6, out_specs=[spec] * 3,
        out_shape=[_sds((S // d, d * D), BF16)] * 3,
        scratch_shapes=[pltpu.VMEM((S // d, 128 * units), F32)] * 2,
        compiler_params=_cparams(1),
    )(q, k, v, do, lse, dd)


MIX_TILE = 256
DILATIONS = tuple(d for _, d in BRANCHES)


def _branch_weights(la, lb, lc):
    m = jnp.maximum(jnp.maximum(la, lb), lc)
    ea, eb, ec = jnp.exp(la - m), jnp.exp(lb - m), jnp.exp(lc - m)
    den = ea + eb + ec
    return ea / den, eb / den, ec / den


def _mix_operands(outs, lses):
    specs = [_class_block(d, MIX_TILE) for d in DILATIONS] * 2
    scratch = [pltpu.VMEM((NCHUNK, MIX_TILE, 128), F32)] * 4
    return list(outs) + list(lses), specs, scratch


def _mix_fwd(name, outs, lses):
    def body(o0, o1, o2, l0, l1, l2, o_ref, to1, to2, tl1, tl2):
        for blk, tmp, d in ((o1, to1, DILATIONS[1]), (o2, to2, DILATIONS[2]), (l1, tl1, DILATIONS[1]), (l2, tl2, DILATIONS[2])):
            _tokens_from_classes(blk, tmp, d, MIX_TILE)
        for c in range(NCHUNK):
            wa, wb, wc = _branch_weights(l0[:, _chunk(c)], tl1[c], tl2[c])
            o_ref[:, _chunk(c)] = (wa * o0[:, _chunk(c)].astype(F32) + wb * to1[c] + wc * to2[c]).astype(o_ref.dtype)

    operands, specs, scratch = _mix_operands(outs, lses)
    return pl.pallas_call(
        body, name=name, grid=(S // MIX_TILE,),
        in_specs=specs, out_specs=_row_spec(MIX_TILE), out_shape=_sds((S, D), BF16),
        scratch_shapes=scratch, compiler_params=_cparams(1),
    )(*operands)


def _head_sum(x, ones_blockdiag):
    hi = x.astype(BF16)
    r1 = x - hi.astype(F32)
    mid = r1.astype(BF16)
    lo = (r1 - mid.astype(F32)).astype(BF16)
    return _dot_nn(hi, ones_blockdiag) + _dot_nn(mid, ones_blockdiag) + _dot_nn(lo, ones_blockdiag)


def _mix_bwd(name, do, outs, lses, ones_blockdiag):
    def body(do_ref, o0, o1, o2, l0, l1, l2, ones_ref, d0, d1, d2, t0, t1, t2,
             to1, to2, tl1, tl2, td1, td2, tt1, tt2):
        for blk, tmp, d in ((o1, to1, DILATIONS[1]), (o2, to2, DILATIONS[2]), (l1, tl1, DILATIONS[1]), (l2, tl2, DILATIONS[2])):
            _tokens_from_classes(blk, tmp, d, MIX_TILE)
        ones = ones_ref[...]
        for c in range(NCHUNK):
            w = _branch_weights(l0[:, _chunk(c)], tl1[c], tl2[c])
            dov = do_ref[:, _chunk(c)]
            o = w[0] * o0[:, _chunk(c)].astype(F32) + w[1] * to1[c] + w[2] * to2[c]
            t = _head_sum(dov * o, ones)
            d0[:, _chunk(c)] = (w[0] * dov).astype(d0.dtype)
            t0[:, _chunk(c)] = w[0] * t
            td1[c], tt1[c] = w[1] * dov, w[1] * t
            td2[c], tt2[c] = w[2] * dov, w[2] * t
        for tmp, blk, d in ((td1, d1, DILATIONS[1]), (tt1, t1, DILATIONS[1]), (td2, d2, DILATIONS[2]), (tt2, t2, DILATIONS[2])):
            _classes_from_tokens(tmp, blk, d, MIX_TILE)

    operands, specs, scratch = _mix_operands(outs, lses)
    out_specs = [_class_block(d, MIX_TILE) for d in DILATIONS] * 2
    out_shape = [_sds((S // d, d * D), BF16) for d in DILATIONS] + [_sds((S // d, d * D), F32) for d in DILATIONS]
    return pl.pallas_call(
        body, name=name, grid=(S // MIX_TILE,),
        in_specs=[_row_spec(MIX_TILE)] + specs + [_vec_spec(128, 128)],
        out_specs=out_specs, out_shape=out_shape,
        scratch_shapes=scratch + [pltpu.VMEM((NCHUNK, MIX_TILE, 128), F32)] * 4,
        compiler_params=_cparams(1),
    )(do, *operands, ones_blockdiag)


def _attn_bwd_post(name, grads, cos_t, sin_t):
    tm = 2 * MIX_TILE
    scale = HEAD_DIM ** -0.5

    def unrope(x, cs, sn):
        return x * cs - _swap_halves(x) * sn

    def body(*refs):
        in_refs = refs[:9]
        cos_ref, sin_ref, dq_ref, dkv_ref, tmp_ref = refs[9:]
        cs = cos_ref[...]
        sn = sin_ref[...]
        for g, d in enumerate(DILATIONS):
            for which, blk in enumerate(in_refs[3 * g:3 * g + 3]):
                if d > 1:
                    _tokens_from_classes(blk, tmp_ref, d, tm)
                for c in range(NCHUNK):
                    x = tmp_ref[c] if d > 1 else blk[:, _chunk(c)].astype(F32)
                    if which == 0:
                        dq_ref[:, _chunk(c, g * D)] = (unrope(x, cs, sn) * scale).astype(dq_ref.dtype)
                    elif which == 1:
                        dkv_ref[:, _chunk(c, g * D)] = unrope(x, cs, sn).astype(dkv_ref.dtype)
                    else:
                        dkv_ref[:, _chunk(c, QW + g * D)] = x.astype(dkv_ref.dtype)

    operands = [a for branch in grads for a in branch]
    tab = pl.BlockSpec((tm, 128), lambda i: (i, 0))
    return pl.pallas_call(
        body, name=name, grid=(S // tm,),
        in_specs=[_class_block(d, tm) for d in DILATIONS for _ in range(3)] + [tab, tab],
        out_specs=[pl.BlockSpec((tm, QW), lambda i: (i, 0)), pl.BlockSpec((tm, 2 * QW), lambda i: (i, 0))],
        out_shape=[_sds((S, QW), BF16), _sds((S, 2 * QW), BF16)],
        scratch_shapes=[pltpu.VMEM((NCHUNK, tm, 128), F32)],
        compiler_params=_cparams(1),
    )(*operands, cos_t, sin_t)


def _adamw(name, parts, w, m, v, layer=None, other=None):
    n, rows, cols = parts.shape
    tr = rows
    for cand in (256, 176, 128, 64, 32, 16, 8):
        if rows % cand == 0:
            tr = cand
            break
    n_other = 0 if other is None else len(other)

    def body(p_ref, w_ref, m_ref, v_ref, *refs):
        g_ref, d_ref, nm_ref, nv_ref = refs[n_other:]
        g = p_ref[0].astype(F32)
        for j in range(1, n):
            g = g + p_ref[j].astype(F32)
        g_ref[...] = g
        d_ref[...], nm_ref[...], nv_ref[...] = _adam_update(g, w_ref[...], m_ref[...], v_ref[...])

    if layer is None:
        blk = pl.BlockSpec((tr, cols), lambda i: (i, 0))
        shape = (rows, cols)
    else:
        blk = pl.BlockSpec((None, tr, cols), lambda i: (layer, i, 0))
        shape = w.shape
    return pl.pallas_call(
        body, name=name, grid=(rows // tr,),
        in_specs=[pl.BlockSpec((n, tr, cols), lambda i: (0, i, 0)), blk, blk, blk]
                 + [pl.BlockSpec(memory_space=pl.ANY)] * n_other,
        out_specs=[blk] * 4, out_shape=[_sds(shape, F32)] * 4,
        input_output_aliases={4 + i: i for i in range(n_other)},
        compiler_params=_cparams(1),
    )(parts, w, m, v, *(other or ()))


def _adam_update(g, w, m, v):
    c1 = 1.0 / (1.0 - ADAM_B1 ** ADAM_STEP)
    c2 = 1.0 / (1.0 - ADAM_B2 ** ADAM_STEP)
    nm = ADAM_B1 * m + (1.0 - ADAM_B1) * g
    nv = ADAM_B2 * v + (1.0 - ADAM_B2) * (g * g)
    return -ADAM_LR * ((nm * c1) / (jnp.sqrt(nv * c2) + ADAM_EPS) + ADAM_WD * w), nm, nv


GAIN_ROWS = 16


def _pack_small(name, gain_tiles, taps, sq):
    ng = len(gain_tiles)

    def body(*refs):
        o_ref = refs[-1]
        o_ref[...] = jnp.zeros_like(o_ref)
        for i in range(ng):
            o_ref[i:i + 1, :] = refs[i][0:1, :]
        o_ref[ng:ng + 3, :] = refs[ng][0:3, :]
        o_ref[ng + 3:ng + 4, :] = refs[ng + 1][...]

    return pl.pallas_call(body, name=name, out_shape=_sds((GAIN_ROWS, D), F32))(*gain_tiles, taps, sq)


def _adamw_gains(name, parts, params):
    np_ = len(params)
    shapes = [w.shape for w, _, _ in params]

    def body(p_ref, *refs):
        ins, outs = refs[:3 * np_], refs[3 * np_:]

        def total(lo, rows):
            g = p_ref[0, lo:lo + rows, :]
            for j in range(1, NDEV):
                g = g + p_ref[j, lo:lo + rows, :]
            return g

        lo = 0
        for i, shape in enumerate(shapes):
            g = total(lo, shape[0])
            lo += shape[0]
            w_ref, m_ref, v_ref = ins[3 * i:3 * i + 3]
            g_ref, d_ref, nm_ref, nv_ref = outs[4 * i:4 * i + 4]
            g_ref[...] = g
            d_ref[...], nm_ref[...], nv_ref[...] = _adam_update(g, w_ref[...], m_ref[...], v_ref[...])
        taps_ref, loss_ref = outs[-2], outs[-1]
        taps_ref[...] = jnp.zeros_like(taps_ref)
        taps_ref[0:3, :] = total(lo, 3)
        loss_ref[...] = jnp.sum(total(lo + 3, 1), axis=-1, keepdims=True) * (0.5 / D)

    out_shape = [_sds(shape, F32) for shape in shapes for _ in range(4)] + [_sds((8, D), F32), _sds((1, 1), F32)]
    outs = pl.pallas_call(body, name=name, out_shape=out_shape)(parts, *[a for p in params for a in p])
    return [list(outs[4 * i:4 * i + 4]) for i in range(np_)], outs[-2], outs[-1].reshape(())


def _exchange(name, arrays, kind, after):
    n = len(arrays)
    gather = kind == "gather"
    out_shape = [_sds((NDEV,) + a.shape if gather else a.shape, a.dtype) for a in arrays]

    def body(*refs):
        srcs, outs = refs[:n], refs[n + 1:2 * n + 1]
        send_sems, recv_sems, local_sems = refs[2 * n + 1:]
        x, y, c = lax.axis_index("x"), lax.axis_index("y"), lax.axis_index("c")
        me = 4 * x + 2 * y + c
        pending = []
        for t in range(n):
            own = pltpu.make_async_copy(srcs[t] if gather else srcs[t].at[me], outs[t].at[me], local_sems.at[t])
            own.start()
            pending.append(own)
            for rel in range(1, NDEV):
                px = 1 - x if rel & 4 else x
                py = 1 - y if rel & 2 else y
                pc = 1 - c if rel & 1 else c
                peer = 4 * px + 2 * py + pc
                send = pltpu.make_async_remote_copy(
                    src_ref=srcs[t] if gather else srcs[t].at[peer], dst_ref=outs[t].at[me],
                    send_sem=send_sems.at[t, rel - 1], recv_sem=recv_sems.at[t, rel - 1],
                    device_id=(px, py, pc), device_id_type=MESH)
                send.start()
                arrive = pltpu.make_async_remote_copy(
                    src_ref=srcs[t] if gather else srcs[t].at[me], dst_ref=outs[t].at[peer],
                    send_sem=send_sems.at[t, rel - 1], recv_sem=recv_sems.at[t, rel - 1],
                    device_id=(px, py, pc), device_id_type=MESH)
                pending.append((send, arrive))
        for item in pending:
            if isinstance(item, tuple):
                item[0].wait_send()
                item[1].wait_recv()
            else:
                item.wait()

    any_spec = pl.BlockSpec(memory_space=pl.ANY)
    outs = pl.pallas_call(
        body, name=name,
        in_specs=[any_spec] * (n + 1), out_specs=[any_spec] * n, out_shape=out_shape,
        scratch_shapes=[pltpu.SemaphoreType.DMA((n, NDEV - 1)), pltpu.SemaphoreType.DMA((n, NDEV - 1)),
                        pltpu.SemaphoreType.DMA((n,))],
    )(*arrays, after)
    return list(outs)


_HBM_SPEC = pl.BlockSpec(memory_space=pltpu.HBM)
_SEM_SPEC = pl.BlockSpec(memory_space=pltpu.SEMAPHORE)
_DATAFLOW = pltpu.SideEffectType.DATAFLOW_SIDE_EFFECTING


def _peers():
    x, y, c = lax.axis_index("x"), lax.axis_index("y"), lax.axis_index("c")
    out = []
    for rel in range(1, NDEV):
        px = 1 - x if rel & 4 else x
        py = 1 - y if rel & 2 else y
        pc = 1 - c if rel & 1 else c
        out.append((rel - 1, (px, py, pc), 4 * px + 2 * py + pc))
    return 4 * x + 2 * y + c, out


def _hbm(a):
    return pltpu.HBM(a.shape, a.dtype)


def _own_slot(a, me, kind):
    mine = a[None] if kind == "gather" else lax.dynamic_slice_in_dim(a, me, 1, axis=0)
    shape = (NDEV,) + mine.shape[1:]
    return lax.dynamic_update_slice_in_dim(lax.empty(shape, a.dtype), mine, me, axis=0)


def _exchange_start(name, arrays, me, kind):
    n = len(arrays)
    gather = kind == "gather"
    lands = [_own_slot(a, me, kind) for a in arrays]

    def body(*refs):
        src_refs, land_refs = refs[:n], refs[n:2 * n]
        send_sems, recv_sems = refs[2 * n], refs[2 * n + 1]
        token = refs[-1]
        my_block, peers = _peers()
        for t in range(n):
            for slot, dev, block in peers:
                pltpu.make_async_remote_copy(
                    src_ref=src_refs[t] if gather else src_refs[t].at[block], dst_ref=land_refs[t].at[my_block],
                    send_sem=send_sems.at[t * (NDEV - 1) + slot], recv_sem=recv_sems.at[t * (NDEV - 1) + slot],
                    device_id=dev, device_id_type=MESH).start()
        token[...] = jnp.zeros_like(token)

    operands = [pltpu.with_memory_space_constraint(a, pltpu.HBM) for a in list(arrays) + lands]
    outs = pl.pallas_call(
        body, name=name,
        out_shape=(pltpu.SemaphoreType.DMA((n * (NDEV - 1),)), pltpu.SemaphoreType.DMA((n * (NDEV - 1),)),
                   *[_hbm(a) for a in operands], _sds((8, 128), F32)),
        in_specs=[_HBM_SPEC] * (2 * n),
        out_specs=(_SEM_SPEC, _SEM_SPEC, *[_HBM_SPEC] * (2 * n), pl.BlockSpec(memory_space=pltpu.VMEM)),
        input_output_aliases={i: 2 + i for i in range(2 * n)},
        compiler_params=pltpu.CompilerParams(has_side_effects=_DATAFLOW),
    )(*operands)
    return (outs[0], outs[1], list(outs[2:2 + n]), list(outs[2 + n:2 + 2 * n])), outs[-1]


def _exchange_wait(name, started, t, after, kind):
    send_sems, recv_sems, srcs, lands = started
    gather = kind == "gather"

    def body(src_ref, land_ref, send_ref, recv_ref, after_ref, src_out, land_out):
        _, peers = _peers()
        for slot, dev, block in peers:
            copy = pltpu.make_async_remote_copy(
                src_ref=src_ref if gather else src_ref.at[block], dst_ref=land_ref.at[block],
                send_sem=send_ref.at[t * (NDEV - 1) + slot], recv_sem=recv_ref.at[t * (NDEV - 1) + slot],
                device_id=dev, device_id_type=MESH)
            copy.wait_send()
            copy.wait_recv()

    return pl.pallas_call(
        body, name=name, out_shape=(_hbm(srcs[t]), _hbm(lands[t])),
        in_specs=(_HBM_SPEC, _HBM_SPEC, _SEM_SPEC, _SEM_SPEC, pl.BlockSpec(memory_space=pl.ANY)),
        out_specs=(_HBM_SPEC, _HBM_SPEC), input_output_aliases={0: 0, 1: 1},
        compiler_params=pltpu.CompilerParams(has_side_effects=_DATAFLOW),
    )(srcs[t], lands[t], send_sems, recv_sems, after)[1]


DIRECT_RELS = (1, 2, 4, 6)
RELAY_RELS = (2, 4, 6)


def _rel_peer(rel):
    x, y, c = lax.axis_index("x"), lax.axis_index("y"), lax.axis_index("c")
    px = 1 - x if rel & 4 else x
    py = 1 - y if rel & 2 else y
    pc = 1 - c if rel & 1 else c
    return (px, py, pc), 4 * px + 2 * py + pc


def _gather_start(name, shards, me):
    n, nr = len(shards), len(DIRECT_RELS)
    lands = [_own_slot(a, me, "gather") for a in shards]

    def body(*refs):
        src_refs, land_refs = refs[:n], refs[n:2 * n]
        send_sems, recv_sems = refs[2 * n], refs[2 * n + 1]
        _, my_block = _rel_peer(0)
        for t in range(n):
            for s, rel in enumerate(DIRECT_RELS):
                dev, _ = _rel_peer(rel)
                pltpu.make_async_remote_copy(
                    src_ref=src_refs[t], dst_ref=land_refs[t].at[my_block],
                    send_sem=send_sems.at[t * nr + s], recv_sem=recv_sems.at[t * nr + s],
                    device_id=dev, device_id_type=MESH).start()

    operands = [pltpu.with_memory_space_constraint(a, pltpu.HBM) for a in list(shards) + lands]
    outs = pl.pallas_call(
        body, name=name,
        out_shape=(pltpu.SemaphoreType.DMA((n * nr,)), pltpu.SemaphoreType.DMA((n * nr,)), *[_hbm(a) for a in operands]),
        in_specs=[_HBM_SPEC] * (2 * n), out_specs=(_SEM_SPEC, _SEM_SPEC, *[_HBM_SPEC] * (2 * n)),
        input_output_aliases={i: 2 + i for i in range(2 * n)},
        compiler_params=pltpu.CompilerParams(has_side_effects=_DATAFLOW),
    )(*operands)
    return outs[0], outs[1], list(outs[2:2 + n]), list(outs[2 + n:2 + 2 * n])


def _gather_wait(name, started, ts, after):
    send_sems, recv_sems, srcs, lands = started
    m, nr = len(ts), len(DIRECT_RELS)

    def body(*refs):
        src_refs, land_refs = refs[:m], refs[m:2 * m]
        send_ref, recv_ref = refs[2 * m], refs[2 * m + 1]
        for i, t in enumerate(ts):
            for s, rel in enumerate(DIRECT_RELS):
                dev, block = _rel_peer(rel)
                copy = pltpu.make_async_remote_copy(
                    src_ref=src_refs[i], dst_ref=land_refs[i].at[block],
                    send_sem=send_ref.at[t * nr + s], recv_sem=recv_ref.at[t * nr + s],
                    device_id=dev, device_id_type=MESH)
                copy.wait_send()
                copy.wait_recv()

    operands = [srcs[t] for t in ts] + [lands[t] for t in ts]
    outs = pl.pallas_call(
        body, name=name, out_shape=tuple(_hbm(a) for a in operands),
        in_specs=[_HBM_SPEC] * (2 * m) + [_SEM_SPEC, _SEM_SPEC, pl.BlockSpec(memory_space=pl.ANY)],
        out_specs=tuple([_HBM_SPEC] * (2 * m)), input_output_aliases={i: i for i in range(2 * m)},
        compiler_params=pltpu.CompilerParams(has_side_effects=_DATAFLOW),
    )(*operands, send_sems, recv_sems, after)
    return list(outs[m:])


def _relay_start(name, lands):
    m, nr = len(lands), len(RELAY_RELS)

    def body(*refs):
        land_refs, send_sems, recv_sems = refs[:m], refs[m], refs[m + 1]
        sibling, _ = _rel_peer(1)
        for i in range(m):
            for s, rel in enumerate(RELAY_RELS):
                _, block = _rel_peer(rel)
                pltpu.make_async_remote_copy(
                    src_ref=land_refs[i].at[block], dst_ref=land_refs[i].at[block],
                    send_sem=send_sems.at[i * nr + s], recv_sem=recv_sems.at[i * nr + s],
                    device_id=sibling, device_id_type=MESH).start()

    outs = pl.pallas_call(
        body, name=name,
        out_shape=(pltpu.SemaphoreType.DMA((m * nr,)), pltpu.SemaphoreType.DMA((m * nr,)), *[_hbm(a) for a in lands]),
        in_specs=[_HBM_SPEC] * m, out_specs=(_SEM_SPEC, _SEM_SPEC, *[_HBM_SPEC] * m),
        input_output_aliases={i: 2 + i for i in range(m)},
        compiler_params=pltpu.CompilerParams(has_side_effects=_DATAFLOW),
    )(*lands)
    return outs[0], outs[1], list(outs[2:])


def _relay_wait(name, relayed, after):
    send_sems, recv_sems, lands = relayed
    m, nr = len(lands), len(RELAY_RELS)

    def body(*refs):
        land_refs, send_ref, recv_ref = refs[:m], refs[m], refs[m + 1]
        sibling, _ = _rel_peer(1)
        for i in range(m):
            for s, rel in enumerate(RELAY_RELS):
                _, sent = _rel_peer(rel)
                _, arriving = _rel_peer(rel ^ 1)
                copy = pltpu.make_async_remote_copy(
                    src_ref=land_refs[i].at[sent], dst_ref=land_refs[i].at[arriving],
                    send_sem=send_ref.at[i * nr + s], recv_sem=recv_ref.at[i * nr + s],
                    device_id=sibling, device_id_type=MESH)
                copy.wait_send()
                copy.wait_recv()

    outs = pl.pallas_call(
        body, name=name, out_shape=tuple(_hbm(a) for a in lands),
        in_specs=[_HBM_SPEC] * m + [_SEM_SPEC, _SEM_SPEC, pl.BlockSpec(memory_space=pl.ANY)],
        out_specs=tuple([_HBM_SPEC] * m), input_output_aliases={i: i for i in range(m)},
        compiler_params=pltpu.CompilerParams(has_side_effects=_DATAFLOW),
    )(*lands, send_sems, recv_sems, after)
    return list(outs)


def _ffn_fwd(tag, n, wg, wd):
    gu, act = _gate_up_act(f"ffn_gate_up_{tag}", n, wg)
    wd4 = wd.reshape(NFB, FB, D)
    f = _fwd_kblocked(f"ffn_down_{tag}", act, wd4)
    return (n, gu, act, wg, wd4), f


def _ffn_bwd(tag, dh_out, df, h_in, saved, g_pre, send, mixer):
    n, gu, act, wg, wd4 = saved
    dwd = _bwd_w_kblocked(f"ffn_down_dw_{tag}", act, df).reshape(NDEV, DFF // NDEV, D)
    dgu = _down_dx_act_bwd(f"ffn_down_dx_{tag}", df, wd4, gu).reshape(NDEV, S, FB)
    tok = send({f"down_{tag}": dwd, f"gate_up_{tag}": _bwd_w_cols_blocked(f"ffn_gate_up_dw_{tag}", n, dgu)})
    dn = _bwd_x_cols_blocked(f"ffn_gate_up_dx_{tag}", dgu, wg, after=tok)
    dh_in, (dg_pre,), dy, dg_mixer = _rms_bwd(f"ffn_prenorm_bwd_{tag}", h_in, [(g_pre, dn)], dh_out, F32, then=mixer)
    return dh_in, dg_pre, dy, dg_mixer


def kernel(x, positions, mix_norm_pre, mix_norm_post, ffn_norm_pre, ffn_norm_post, ffn_w_gate_up, ffn_w_down, conv_w_in, conv_w, conv_w_out, kv_norm, w_kv, w_q, w_o, loss_target, m_mix_norm_pre, m_mix_norm_post, m_ffn_norm_pre, m_ffn_norm_post, m_ffn_w_gate_up, m_ffn_w_down, m_conv_w_in, m_conv_w, m_conv_w_out, m_kv_norm, m_w_kv, m_w_q, m_w_o, v_mix_norm_pre, v_mix_norm_post, v_ffn_norm_pre, v_ffn_norm_post, v_ffn_w_gate_up, v_ffn_w_down, v_conv_w_in, v_conv_w, v_conv_w_out, v_kv_norm, v_w_kv, v_w_q, v_w_o):
    me = 4 * lax.axis_index("x") + 2 * lax.axis_index("y") + lax.axis_index("c")
    h0 = x.reshape(S, D)
    target = loss_target.reshape(S, D)
    row = lambda a, l: a[l].reshape(1, D)
    g_kv = kv_norm.reshape(1, D)

    cw_shard = jnp.pad(conv_w[0], ((0, 5), (0, 0)))
    names = ["conv_in", "conv_w", "conv_out", "gate_up_0", "down_0", "kv", "q", "o", "gate_up_1", "down_1"]
    shards = [conv_w_in[0], cw_shard, conv_w_out[0], ffn_w_gate_up[0], ffn_w_down[0],
              w_kv, w_q[0], w_o[0], ffn_w_gate_up[1], ffn_w_down[1]]
    shards = [s if n == "conv_w" else s.astype(BF16) for n, s in zip(names, shards)]
    first = 3
    gather_first = _gather_start("gather_start_conv", shards[:first], me)
    gather_rest = _gather_start("gather_start_rest", shards[first:], me)

    def direct(group, after):
        ts = [names.index(n) for n in group]
        started, ts = (gather_first, ts) if ts[0] < first else (gather_rest, [t - first for t in ts])
        lands = _gather_wait(f"gather_wait_{group[0]}", started, ts, after)
        return _relay_start(f"relay_start_{group[0]}", lands)

    def finish(group, relayed, after):
        return dict(zip(group, _relay_wait(f"relay_wait_{group[0]}", relayed, after)))

    sent = {}

    def send(grads):
        started, token = _exchange_start(f"scatter_start_{next(iter(grads))}", list(grads.values()), me, "scatter")
        for i, name in enumerate(grads):
            sent[name] = (started, i)
        return token

    groups = [["conv_in", "conv_w", "conv_out"], ["gate_up_0", "down_0"], ["kv", "q"], ["o", "gate_up_1", "down_1"]]
    n0 = _rms_fwd("mix_prenorm_0", h0, [row(mix_norm_pre, 0)])[0]
    half = HEAD_DIM // 2
    inv_freq = ROPE_THETA ** (-jnp.arange(half, dtype=F32) / half)
    tables = _rope_tables("rope_tables", positions.reshape(S, 1), jnp.tile(inv_freq, 4).reshape(1, 128))
    w = finish(groups[0], direct(groups[0], tables[0]), n0)
    win = w["conv_in"].transpose(1, 0, 2).reshape(D, 3 * D)
    cw = w["conv_w"].transpose(1, 0, 2).reshape(8, D)
    wout = w["conv_out"].reshape(D, D)
    z = _fwd_rows("conv_in", n0, win, BF16)
    pre = _conv_fwd("conv_gate", z, cw)
    relayed = direct(groups[1], pre)
    y0 = _fwd_rows("conv_out", pre, wout)
    h1, (n1,) = _resid_rms("mix_postnorm_0", h0, y0, row(mix_norm_post, 0), [row(ffn_norm_pre, 0)])
    w = finish(groups[1], relayed, n1)
    ffn0, f0 = _ffn_fwd("0", n1, w["gate_up_0"], w["down_0"])
    relayed = direct(groups[2], ffn0[2])
    h2, (nk, n2) = _resid_rms("ffn_postnorm_0", h1, f0, row(ffn_norm_post, 0), [g_kv, row(mix_norm_pre, 1)])

    w = finish(groups[2], relayed, nk)
    wkv = w["kv"].transpose(1, 0, 2).reshape(D, 2 * QW)
    wq = w["q"].transpose(1, 0, 2).reshape(D, QW)
    qc, kc, vc, o_c, lse_c = [], [], [], [], []
    for g, d in enumerate(DILATIONS):
        q_g, k_g, v_g = _qkv_classes(f"qkv_proj_{g}", n2, nk, wq, wkv, g, d, tables)
        qc.append(q_g)
        kc.append(k_g)
        vc.append(v_g)
    relayed = direct(groups[3], vc[-1])
    for g, d in enumerate(DILATIONS):
        o_g, lse_g = _attn_fwd(f"attn_fwd_{g}", qc[g], kc[g], vc[g], d)
        o_c.append(o_g)
        lse_c.append(lse_g)
    o_mix = _mix_fwd("attn_mix", o_c, lse_c)
    w = finish(groups[3], relayed, o_mix)
    wo = w["o"].reshape(D, D)
    y1 = _fwd_rows("attn_out", o_mix, wo)
    h3, (n3,) = _resid_rms("mix_postnorm_1", h2, y1, row(mix_norm_post, 1), [row(ffn_norm_pre, 1)])
    ffn1, f1 = _ffn_fwd("1", n3, w["gate_up_1"], w["down_1"])

    dh4, df1, dg_fpost1, sq = _resid_rms_loss("ffn_postnorm_1_loss", h3, f1, row(ffn_norm_post, 1), target)

    dh3, dg_fpre1, dy1, dg_mpost1 = _ffn_bwd(
        "1", dh4, df1, h3, ffn1, row(ffn_norm_pre, 1), send, (y1, row(mix_norm_post, 1)))
    dwo = _bwd_w_rows("attn_out_dw", o_mix, dy1).reshape(NDEV, D // NDEV, D)
    do = _bwd_x_rows("attn_out_dx", dy1, wo, BF16)
    lane = jnp.arange(128)
    ones_blockdiag = (lane[:, None] // HEAD_DIM == lane[None, :] // HEAD_DIM).astype(BF16)
    mixed = _mix_bwd("attn_mix_bwd", do, o_c, lse_c, ones_blockdiag)
    branch_grads = [_attn_bwd(f"attn_bwd_{g}", qc[g], kc[g], vc[g], mixed[g], lse_c[g], mixed[3 + g], d)
                    for g, d in enumerate(DILATIONS)]
    dq_raw, dkv = _attn_bwd_post("attn_bwd_post", branch_grads, *tables)
    tok = send({"o": dwo, "kv": _bwd_w_cols("kv_proj_dw", nk, dkv, 2 * QW // NDEV),
                "q": _bwd_w_cols("q_proj_dw", n2, dq_raw, QW // NDEV)})
    dnk = _bwd_x_plain("kv_proj_dx", dkv, wkv, after=tok)
    dn2 = _bwd_x_plain("q_proj_dx", dq_raw, wq)
    dh2, (dg_kv, dg_mpre1), df0, dg_fpost0 = _rms_bwd(
        "kv_and_mix_prenorm_bwd_1", h2, [(g_kv, dnk), (row(mix_norm_pre, 1), dn2)], dh3, F32,
        then=(f0, row(ffn_norm_post, 0)))

    dh1, dg_fpre0, dy0, dg_mpost0 = _ffn_bwd(
        "0", dh2, df0, h1, ffn0, row(ffn_norm_pre, 0), send, (y0, row(mix_norm_post, 0)))
    dwout = _bwd_w_rows("conv_out_dw", pre, dy0).reshape(NDEV, D // NDEV, D)
    dpre = _bwd_x_rows("conv_out_dx", dy0, wout, BF16)
    dz, dcw = _conv_bwd("conv_gate_bwd", z, dpre, cw)
    tok = send({"conv_out": dwout, "conv_in": _bwd_w_cols("conv_in_dw", n0, dz, 3 * D // NDEV)})
    dn0 = _bwd_x_plain("conv_in_dx", dz, win, after=tok)
    dh0, (dg_mpre0,) = _rms_bwd("mix_prenorm_bwd_0", h0, [(row(mix_norm_pre, 0), dn0)], dh1, F32)

    small = _pack_small("pack_small_grads", [dg_mpre0, dg_mpre1, dg_mpost0, dg_mpost1, dg_fpre0, dg_fpre1,
                                             dg_fpost0, dg_fpost1, dg_kv], dcw, sq)

    done = [small]

    def upd(tag, w, m, v):
        parts = _exchange_wait(f"scatter_wait_{tag}", *sent[tag], done[-1], "scatter")
        shape = w.shape
        flat = lambda a: a.reshape(parts.shape[1:])
        res = _adamw(f"adamw_{tag}", parts, flat(w), flat(m), flat(v))
        done.append(res[0])
        return [r.reshape(shape) for r in res]

    def upd_layer(tag, l, w, m, v, other):
        parts = _exchange_wait(f"scatter_wait_{tag}_{l}", *sent[f"{tag}_{l}"], done[-1], "scatter")
        res = _adamw(f"adamw_{tag}_{l}", parts, w, m, v, layer=l, other=other)
        done.append(res[0])
        return list(res)

    res = {}
    down_1 = upd_layer("down", 1, ffn_w_down, m_ffn_w_down, v_ffn_w_down, None)
    gate_up_t = [jnp.swapaxes(a, 1, 2) for a in (ffn_w_gate_up, m_ffn_w_gate_up, v_ffn_w_gate_up)]
    gate_up_1 = upd_layer("gate_up", 1, *gate_up_t, None)
    res["w_o"] = upd("o", w_o, m_w_o, v_w_o)
    res["w_q"] = upd("q", w_q, m_w_q, v_w_q)
    res["w_kv"] = upd("kv", w_kv, m_w_kv, v_w_kv)

    small_all = _exchange("gather_small_grads", [small], "gather", done[-1])[0]
    vec = lambda a: a.reshape(1, D)
    gain_res, taps, loss = _adamw_gains("adamw_gains", small_all, [
        (mix_norm_pre, m_mix_norm_pre, v_mix_norm_pre), (mix_norm_post, m_mix_norm_post, v_mix_norm_post),
        (ffn_norm_pre, m_ffn_norm_pre, v_ffn_norm_pre), (ffn_norm_post, m_ffn_norm_post, v_ffn_norm_post),
        (vec(kv_norm), vec(m_kv_norm), vec(v_kv_norm))])
    dcw_mine = lax.dynamic_slice(taps, (0, me * 128), (8, 128))
    pad8 = lambda a, fill: jnp.pad(a[0], ((0, 5), (0, 0)), constant_values=fill)
    cw_res = [r[0:3].reshape(1, 3, 128) for r in
              _adamw("adamw_conv_w", dcw_mine.reshape(1, 8, 128), cw_shard, pad8(m_conv_w, 0.0), pad8(v_conv_w, 1.0))]

    res.update({
        "mix_norm_pre": gain_res[0],
        "mix_norm_post": gain_res[1],
        "ffn_norm_pre": gain_res[2],
        "ffn_norm_post": gain_res[3],
        "kv_norm": [r.reshape(D) for r in gain_res[4]],
        "conv_w": cw_res,
    })
    done.append(small_all)
    res["ffn_w_down"] = upd_layer("down", 0, ffn_w_down, m_ffn_w_down, v_ffn_w_down, down_1)
    res["ffn_w_gate_up"] = [jnp.swapaxes(r, 1, 2) for r in upd_layer("gate_up", 0, *gate_up_t, gate_up_1)]
    res["conv_w_out"] = upd("conv_out", conv_w_out, m_conv_w_out, v_conv_w_out)
    res["conv_w_in"] = upd("conv_in", conv_w_in, m_conv_w_in, v_conv_w_in)
    order = ["mix_norm_pre", "mix_norm_post", "ffn_norm_pre", "ffn_norm_post", "ffn_w_gate_up", "ffn_w_down",
             "conv_w_in", "conv_w", "conv_w_out", "kv_norm", "w_kv", "w_q", "w_o"]
    out = [loss, dh0.reshape(1, S, D)]
    for i in range(4):
        out += [res[name][i] for name in order]
    return tuple(out)
```

```python
import jax
import jax.numpy as jnp
from jax import lax
from jax.experimental import pallas as pl
from jax.experimental.pallas import tpu as pltpu

F32 = jnp.float32
BF16 = jnp.bfloat16

S = 4096
D = 1024
NDEV = 8
HEAD_DIM = 64
QW = 3072
DFF = 2816
FB = 704
NFB = 4
BRANCHES = ((128, 1), (512, 4), (2048, 16))
BAND = 128
ROPE_THETA = 10000.0
RMS_EPS = 1e-6
NEG_INF = -1e30
ADAM_LR, ADAM_B1, ADAM_B2, ADAM_EPS, ADAM_WD, ADAM_STEP = 0.001, 0.9, 0.999, 1e-08, 0.01, 10

VMEM_LIMIT_BYTES = 52 * 1024 * 1024
ROW_TILE = 512
MESH = pl.DeviceIdType.MESH


def _cparams(ngrid):
    return pltpu.CompilerParams(dimension_semantics=("arbitrary",) * ngrid,
                                vmem_limit_bytes=VMEM_LIMIT_BYTES)


def _sds(shape, dtype):
    return jax.ShapeDtypeStruct(tuple(shape), dtype)


_DIMS = {"nn": (((1,), (0,)), ((), ())),
         "nt": (((1,), (1,)), ((), ())),
         "tn": (((0,), (0,)), ((), ()))}


def _matmul(name, a, b, *, mode, grid, a_blk, a_map, b_blk, b_map, o_shape, o_blk, o_map, out_dtype, after=None,
            out_groups=1):
    nk = grid[2]
    dims = _DIMS[mode]
    acc_shape = tuple(s for s in o_blk if s is not None)
    if out_groups > 1:
        acc_shape = (acc_shape[1], out_groups * acc_shape[2])
    extra = [] if after is None else [after]

    def store(o_ref, val):
        if out_groups == 1:
            o_ref[...] = val.astype(o_ref.dtype)
        else:
            n = o_ref.shape[-1]
            for grp in range(out_groups):
                o_ref[grp] = val[:, grp * n:(grp + 1) * n].astype(o_ref.dtype)

    def body(a_ref, b_ref, *rest):
        o_ref, scratch = rest[len(extra)], rest[len(extra) + 1:]
        part = lax.dot_general(a_ref[...], b_ref[...], dims, preferred_element_type=F32)
        if nk == 1:
            store(o_ref, part)
            return
        acc_ref = scratch[0]
        k = pl.program_id(2)

        @pl.when(k == 0)
        def _():
            acc_ref[...] = part

        @pl.when(k > 0)
        def _():
            acc_ref[...] += part

        @pl.when(k == nk - 1)
        def _():
            store(o_ref, acc_ref[...])

    return pl.pallas_call(
        body, name=name, grid=grid,
        in_specs=[pl.BlockSpec(a_blk, a_map), pl.BlockSpec(b_blk, b_map)] + [pl.BlockSpec(memory_space=pl.ANY)] * len(extra),
        out_specs=pl.BlockSpec(o_blk, o_map),
        out_shape=_sds(o_shape, out_dtype),
        scratch_shapes=[] if nk == 1 else [pltpu.VMEM(acc_shape, F32)],
        compiler_params=_cparams(3),
    )(a, b, *extra)


TM = 1024
TK = S


def _fwd_rows(name, a, w, out_dtype=F32):
    kdim, n = w.shape
    tn = 512
    return _matmul(name, a, w, mode="nn", grid=(S // TM, n // tn, 1),
                   a_blk=(TM, kdim), a_map=lambda i, j, k: (i, 0),
                   b_blk=(kdim, tn), b_map=lambda i, j, k: (0, j),
                   o_shape=(S, n), o_blk=(TM, tn), o_map=lambda i, j, k: (i, j), out_dtype=out_dtype)


def _fwd_kblocked(name, a4, w4):
    nb, _, kb = a4.shape
    n = w4.shape[2]

    def body(a_ref, w_ref, o_ref):
        acc = _dot_nn(a_ref[0], w_ref[0])
        for j in range(1, nb):
            acc = acc + _dot_nn(a_ref[j], w_ref[j])
        o_ref[...] = acc

    return pl.pallas_call(
        body, name=name, grid=(S // TM,),
        in_specs=[pl.BlockSpec((nb, TM, kb), lambda i: (0, i, 0)), pl.BlockSpec((nb, kb, n), lambda i: (0, 0, 0))],
        out_specs=pl.BlockSpec((TM, n), lambda i: (i, 0)), out_shape=_sds((S, n), F32),
        compiler_params=_cparams(1),
    )(a4, w4)


def _bwd_x_cols_blocked(name, dy8, wg, after):
    _, kdim, n = wg.shape
    nk = NDEV // 2

    def body(a_ref, b_ref, after_ref, o_ref, acc_ref):
        k = pl.program_id(1)
        part = _dot_nt(a_ref[0], b_ref[0]) + _dot_nt(a_ref[1], b_ref[1])

        @pl.when(k == 0)
        def _():
            acc_ref[...] = part

        @pl.when(k > 0)
        def _():
            acc_ref[...] += part

        @pl.when(k == nk - 1)
        def _():
            o_ref[...] = acc_ref[...].astype(o_ref.dtype)

    return pl.pallas_call(
        body, name=name, grid=(S // TM, nk),
        in_specs=[pl.BlockSpec((2, None, TM, n), lambda i, k: (0, k, i, 0)),
                  pl.BlockSpec((2, None, kdim, n), lambda i, k: (0, k, 0, 0)),
                  pl.BlockSpec(memory_space=pl.ANY)],
        out_specs=pl.BlockSpec((TM, kdim), lambda i, k: (i, 0)), out_shape=_sds((S, kdim), BF16),
        scratch_shapes=[pltpu.VMEM((TM, kdim), F32)],
        compiler_params=_cparams(2),
    )(dy8.reshape(2, nk, S, n), wg.reshape(2, nk, kdim, n), after)


def _bwd_x_rows(name, dy, w, out_dtype, after=None):
    kdim, n = w.shape
    tkk = 512
    return _matmul(name, dy, w, mode="nt", grid=(S // TM, kdim // tkk, 1),
                   a_blk=(TM, n), a_map=lambda i, j, k: (i, 0),
                   b_blk=(tkk, n), b_map=lambda i, j, k: (j, 0),
                   o_shape=(S, kdim), o_blk=(TM, tkk), o_map=lambda i, j, k: (i, j), out_dtype=out_dtype, after=after)


DW_COLS = 768


def _bwd_w_cols(name, a, dy, n):
    kdim = a.shape[1]
    groups = DW_COLS // n
    return _matmul(name, a, dy, mode="tn", grid=(1, NDEV // groups, S // TK),
                   a_blk=(TK, kdim), a_map=lambda i, j, k: (k, 0),
                   b_blk=(TK, DW_COLS), b_map=lambda i, j, k: (k, j),
                   o_shape=(NDEV, kdim, n), o_blk=(groups, kdim, n) if groups > 1 else (None, kdim, n),
                   o_map=lambda i, j, k: (j, 0, 0), out_dtype=BF16, out_groups=groups)


def _bwd_x_plain(name, dy, w, after=None):
    kdim, n = w.shape
    tm = TM if n <= 3 * D else TM // 2
    return _matmul(name, dy, w, mode="nt", grid=(S // tm, 1, 1),
                   a_blk=(tm, n), a_map=lambda i, j, k: (i, 0),
                   b_blk=(kdim, n), b_map=lambda i, j, k: (0, 0),
                   o_shape=(S, kdim), o_blk=(tm, kdim), o_map=lambda i, j, k: (i, 0), out_dtype=BF16, after=after)


def _bwd_w_cols_blocked(name, a, dy8):
    kdim = a.shape[1]
    n = dy8.shape[2]
    return _matmul(name, dy8, a, mode="tn", grid=(1, NDEV, S // TK),
                   a_blk=(None, TK, n), a_map=lambda i, j, k: (j, k, 0),
                   b_blk=(TK, kdim), b_map=lambda i, j, k: (k, 0),
                   o_shape=(NDEV, n, kdim), o_blk=(None, n, kdim), o_map=lambda i, j, k: (j, 0, 0), out_dtype=BF16)


def _bwd_w_rows(name, a, dy):
    kdim = a.shape[1]
    n = dy.shape[1]
    tmm = 512
    return _matmul(name, a, dy, mode="tn", grid=(kdim // tmm, 1, S // TK),
                   a_blk=(TK, tmm), a_map=lambda i, j, k: (k, i),
                   b_blk=(TK, n), b_map=lambda i, j, k: (k, 0),
                   o_shape=(kdim, n), o_blk=(tmm, n), o_map=lambda i, j, k: (i, 0), out_dtype=BF16)


def _bwd_w_kblocked(name, a4, dy):
    nb, _, kb = a4.shape
    n = dy.shape[1]
    return _matmul(name, a4, dy, mode="tn", grid=(nb, 1, S // TK),
                   a_blk=(None, TK, kb), a_map=lambda i, j, k: (i, k, 0),
                   b_blk=(TK, n), b_map=lambda i, j, k: (k, 0),
                   o_shape=(nb, kb, n), o_blk=(None, kb, n), o_map=lambda i, j, k: (i, 0, 0), out_dtype=BF16)


def _rstd(x):
    return lax.rsqrt(jnp.mean(x * x, axis=-1, keepdims=True) + RMS_EPS)


def _row_spec(tm=ROW_TILE, width=D):
    return pl.BlockSpec((tm, width), lambda i: (i, 0))


def _vec_spec(rows=1, width=D):
    return pl.BlockSpec((rows, width), lambda i: (0, 0))


def _rms_fwd(name, x, gains):
    n = len(gains)

    def body(x_ref, *refs):
        x_val = x_ref[...]
        xh = x_val * _rstd(x_val)
        for g_ref, o_ref in zip(refs[:n], refs[n:]):
            o_ref[...] = (xh * g_ref[...]).astype(o_ref.dtype)

    outs = pl.pallas_call(
        body, name=name, grid=(S // ROW_TILE,),
        in_specs=[_row_spec()] + [_vec_spec()] * n,
        out_specs=[_row_spec()] * n,
        out_shape=[_sds((S, D), BF16)] * n,
        compiler_params=_cparams(1),
    )(x, *gains)
    return list(outs)


def _resid_rms(name, h, y, g, next_gains):
    n = len(next_gains)

    def body(h_ref, y_ref, g_ref, *refs):
        y_val = y_ref[...]
        h_new = h_ref[...] + (y_val * _rstd(y_val)) * g_ref[...]
        refs[n][...] = h_new
        hh = h_new * _rstd(h_new)
        for g2_ref, o_ref in zip(refs[:n], refs[n + 1:]):
            o_ref[...] = (hh * g2_ref[...]).astype(o_ref.dtype)

    outs = pl.pallas_call(
        body, name=name, grid=(S // ROW_TILE,),
        in_specs=[_row_spec(), _row_spec(), _vec_spec()] + [_vec_spec()] * n,
        out_specs=[_row_spec()] * (n + 1), out_shape=[_sds((S, D), F32)] + [_sds((S, D), BF16)] * n,
        compiler_params=_cparams(1),
    )(h, y, g, *next_gains)
    return outs[0], list(outs[1:])


def _resid_rms_loss(name, h, y, g, target):
    def body(h_ref, y_ref, g_ref, t_ref, dh_ref, dy_ref, dg_ref, part_ref):
        y_val = y_ref[...]
        gain = g_ref[...]
        e = h_ref[...] + (y_val * _rstd(y_val)) * gain - t_ref[...]
        dh = e * (1.0 / D)
        dh_ref[...] = dh
        step = pl.program_id(0)
        dy_ref[...] = _norm_bwd_rows(y_val, gain, dh, dg_ref, step).astype(dy_ref.dtype)
        part = jnp.sum(e * e, axis=0, keepdims=True)

        @pl.when(step == 0)
        def _():
            part_ref[...] = part

        @pl.when(step > 0)
        def _():
            part_ref[...] += part

    return pl.pallas_call(
        body, name=name, grid=(S // ROW_TILE,),
        in_specs=[_row_spec(), _row_spec(), _vec_spec(), _row_spec()],
        out_specs=[_row_spec(), _row_spec(), _vec_spec(8), _vec_spec()],
        out_shape=[_sds((S, D), F32), _sds((S, D), BF16), _sds((8, D), F32), _sds((1, D), F32)],
        compiler_params=_cparams(1),
    )(h, y, g, target)


def _norm_bwd_rows(x_val, g, dn, dg_ref, step):
    r = _rstd(x_val)
    xh = x_val * r
    dxh = dn * g
    part = jnp.sum(dn * xh, axis=0, keepdims=True)

    @pl.when(step == 0)
    def _():
        dg_ref[...] = jnp.zeros_like(dg_ref)

    dg_ref[0:1, :] += part
    return r * (dxh - xh * jnp.mean(dxh * xh, axis=-1, keepdims=True))


def _rms_bwd(name, x, pairs, dres, out_dtype, then=None):
    n = len(pairs)
    has_res = dres is not None
    chained = then is not None

    def body(x_ref, *refs):
        g_refs = refs[0:2 * n:2]
        dn_refs = refs[1:2 * n:2]
        pos = 2 * n
        res_ref = refs[pos] if has_res else None
        pos += int(has_res)
        if chained:
            y_ref, gy_ref = refs[pos], refs[pos + 1]
            pos += 2
        dx_ref = refs[pos]
        dg_refs = refs[pos + 1:pos + 1 + n]
        step = pl.program_id(0)
        x_val = x_ref[...]
        acc = res_ref[...] if has_res else jnp.zeros_like(x_val)
        for g_ref, dn_ref, dg_ref in zip(g_refs, dn_refs, dg_refs):
            acc = acc + _norm_bwd_rows(x_val, g_ref[...], dn_ref[...].astype(F32), dg_ref, step)
        dx_ref[...] = acc.astype(dx_ref.dtype)
        if chained:
            dy_ref, dgy_ref = refs[pos + 1 + n], refs[pos + 2 + n]
            dy_ref[...] = _norm_bwd_rows(y_ref[...], gy_ref[...], acc, dgy_ref, step).astype(dy_ref.dtype)

    operands = [x]
    in_specs = [_row_spec()]
    for g, dn in pairs:
        operands += [g, dn]
        in_specs += [_vec_spec(), _row_spec()]
    if has_res:
        operands.append(dres)
        in_specs.append(_row_spec())
    if chained:
        operands += [then[0], then[1]]
        in_specs += [_row_spec(), _vec_spec()]
    extra = int(chained)
    outs = pl.pallas_call(
        body, name=name, grid=(S // ROW_TILE,),
        in_specs=in_specs,
        out_specs=[_row_spec()] + [_vec_spec(8)] * n + [_row_spec(), _vec_spec(8)] * extra,
        out_shape=[_sds((S, D), out_dtype)] + [_sds((8, D), F32)] * n + [_sds((S, D), BF16), _sds((8, D), F32)] * extra,
        compiler_params=_cparams(1),
    )(*operands)
    if chained:
        return outs[0], list(outs[1:1 + n]), outs[1 + n], outs[2 + n]
    return outs[0], list(outs[1:])


def _shift_down(u, prev8, k):
    r = pltpu.roll(u, k, 0)
    p = pltpu.roll(prev8, k, 0)
    row = lax.broadcasted_iota(jnp.int32, prev8.shape, 0)
    top = jnp.where(row < k, p, r[0:8])
    return jnp.concatenate([top, r[8:]], axis=0)


def _shift_up(u, next8, k):
    tm = u.shape[0]
    r = pltpu.roll(u, tm - k, 0)
    p = pltpu.roll(next8, 8 - k, 0)
    row = lax.broadcasted_iota(jnp.int32, next8.shape, 0)
    bot = jnp.where(row >= 8 - k, p, r[tm - 8:tm])
    return jnp.concatenate([r[:tm - 8], bot], axis=0)


CONV_TILE = 512


def _halo_prev(col):
    return pl.BlockSpec((8, D), lambda i: (jnp.maximum(i * (CONV_TILE // 8) - 1, 0), col))


def _halo_next(col):
    last = S // 8 - 1
    return pl.BlockSpec((8, D), lambda i: (jnp.minimum((i + 1) * (CONV_TILE // 8), last), col))


def _conv_fwd(name, z, cw):
    def body(b_ref, c_ref, h_ref, cp_ref, hp_ref, cw_ref, o_ref):
        i = pl.program_id(0)
        u = c_ref[...].astype(F32) * h_ref[...].astype(F32)
        up = cp_ref[...].astype(F32) * hp_ref[...].astype(F32)
        up = jnp.where(i > 0, up, 0.0)
        cv = cw_ref[0:1, :] * _shift_down(u, up, 2) + cw_ref[1:2, :] * _shift_down(u, up, 1) + cw_ref[2:3, :] * u
        o_ref[...] = (b_ref[...].astype(F32) * cv).astype(o_ref.dtype)

    col = lambda c: pl.BlockSpec((CONV_TILE, D), lambda i: (i, c))
    return pl.pallas_call(
        body, name=name, grid=(S // CONV_TILE,),
        in_specs=[col(0), col(1), col(2), _halo_prev(1), _halo_prev(2), _vec_spec(8)],
        out_specs=_row_spec(CONV_TILE), out_shape=_sds((S, D), BF16),
        compiler_params=_cparams(1),
    )(z, z, z, z, z, cw)


def _conv_bwd(name, z, dpre, cw):
    nsteps = S // CONV_TILE

    def body(b_ref, c_ref, h_ref, cp_ref, hp_ref, dp_ref, dpn_ref, bn_ref, cw_ref, dz_ref, dcw_ref):
        i = pl.program_id(0)
        b = b_ref[...].astype(F32)
        c = c_ref[...].astype(F32)
        h = h_ref[...].astype(F32)
        dp = dp_ref[...].astype(F32)
        u = c * h
        up = jnp.where(i > 0, cp_ref[...].astype(F32) * hp_ref[...].astype(F32), 0.0)
        s1 = _shift_down(u, up, 1)
        s2 = _shift_down(u, up, 2)
        w0, w1, w2 = cw_ref[0:1, :], cw_ref[1:2, :], cw_ref[2:3, :]
        cv = w0 * s2 + w1 * s1 + w2 * u
        dcv = dp * b
        dcvn = jnp.where(i < nsteps - 1, dpn_ref[...].astype(F32) * bn_ref[...].astype(F32), 0.0)
        du = w2 * dcv + w1 * _shift_up(dcv, dcvn, 1) + w0 * _shift_up(dcv, dcvn, 2)
        dz_ref[:, 0:D] = (dp * cv).astype(dz_ref.dtype)
        dz_ref[:, D:2 * D] = (du * h).astype(dz_ref.dtype)
        dz_ref[:, 2 * D:3 * D] = (du * c).astype(dz_ref.dtype)

        @pl.when(i == 0)
        def _():
            dcw_ref[...] = jnp.zeros_like(dcw_ref)

        dcw_ref[0:1, :] += jnp.sum(dcv * s2, axis=0, keepdims=True)
        dcw_ref[1:2, :] += jnp.sum(dcv * s1, axis=0, keepdims=True)
        dcw_ref[2:3, :] += jnp.sum(dcv * u, axis=0, keepdims=True)

    col = lambda c: pl.BlockSpec((CONV_TILE, D), lambda i: (i, c))
    return pl.pallas_call(
        body, name=name, grid=(nsteps,),
        in_specs=[col(0), col(1), col(2), _halo_prev(1), _halo_prev(2),
                  _row_spec(CONV_TILE), _halo_next(0), _halo_next(0), _vec_spec(8)],
        out_specs=[pl.BlockSpec((CONV_TILE, 3 * D), lambda i: (i, 0)), _vec_spec(8)],
        out_shape=[_sds((S, 3 * D), BF16), _sds((8, D), F32)],
        compiler_params=_cparams(1),
    )(z, z, z, z, z, dpre, dpre, z, cw)


FFN_TM = 2048
_GU_BLOCK = pl.BlockSpec((2, None, FFN_TM, FB), lambda i, j: (0, j, i, 0))


def _gate_up_act(name, a, wg):
    kdim = a.shape[1]

    def body(a_ref, wgate_ref, wup_ref, gu_ref, act_ref):
        x = a_ref[...]
        g = _dot_nn(x, wgate_ref[...])
        u = _dot_nn(x, wup_ref[...])
        gu_ref[0] = g.astype(gu_ref.dtype)
        gu_ref[1] = u.astype(gu_ref.dtype)
        act_ref[...] = (g * jax.nn.sigmoid(g) * u).astype(act_ref.dtype)

    return pl.pallas_call(
        body, name=name, grid=(S // FFN_TM, NFB),
        in_specs=[pl.BlockSpec((FFN_TM, kdim), lambda i, j: (i, 0)),
                  pl.BlockSpec((None, kdim, FB), lambda i, j: (j, 0, 0)),
                  pl.BlockSpec((None, kdim, FB), lambda i, j: (j + NFB, 0, 0))],
        out_specs=[_GU_BLOCK, pl.BlockSpec((None, FFN_TM, FB), lambda i, j: (j, i, 0))],
        out_shape=[_sds((2, NFB, S, FB), BF16), _sds((NFB, S, FB), BF16)],
        compiler_params=_cparams(2),
    )(a, wg, wg)


def _down_dx_act_bwd(name, df, w4, gu):
    _, kb, n = w4.shape

    def body(df_ref, w_ref, gu_ref, o_ref):
        d = _dot_nt(df_ref[...], w_ref[...])
        g = gu_ref[0].astype(F32)
        u = gu_ref[1].astype(F32)
        sg = jax.nn.sigmoid(g)
        o_ref[0] = (d * u * sg * (1.0 + g * (1.0 - sg))).astype(o_ref.dtype)
        o_ref[1] = (d * g * sg).astype(o_ref.dtype)

    return pl.pallas_call(
        body, name=name, grid=(S // FFN_TM, NFB),
        in_specs=[pl.BlockSpec((FFN_TM, n), lambda i, j: (i, 0)), pl.BlockSpec((None, kb, n), lambda i, j: (j, 0, 0)),
                  _GU_BLOCK],
        out_specs=_GU_BLOCK, out_shape=_sds((2, NFB, S, FB), BF16),
        compiler_params=_cparams(2),
    )(df, w4, gu)


def _rope_tables(name, pos_col, inv_freq_row):
    def body(pos_ref, f_ref, cos_ref, sin_ref):
        ang = pos_ref[...].astype(F32) * f_ref[...]
        lane = lax.broadcasted_iota(jnp.int32, ang.shape, 1)
        s = jnp.sin(ang)
        cos_ref[...] = jnp.cos(ang)
        sin_ref[...] = jnp.where((lane % HEAD_DIM) < HEAD_DIM // 2, -s, s)

    tab = pl.BlockSpec((ROW_TILE, 128), lambda i: (i, 0))
    return pl.pallas_call(
        body, name=name, grid=(S // ROW_TILE,),
        in_specs=[pl.BlockSpec((ROW_TILE, 1), lambda i: (i, 0)), _vec_spec(1, 128)],
        out_specs=[tab, tab], out_shape=[_sds((S, 128), F32)] * 2,
        compiler_params=_cparams(1),
    )(pos_col, inv_freq_row)


def _swap_halves(t):
    lane = lax.broadcasted_iota(jnp.int32, t.shape, 1)
    first = (lane % HEAD_DIM) < HEAD_DIM // 2
    return jnp.where(first, pltpu.roll(t, 128 - HEAD_DIM // 2, 1), pltpu.roll(t, HEAD_DIM // 2, 1))


NCHUNK = D // 128


def _chunk(c, base=0):
    return slice(base + c * 128, base + (c + 1) * 128)


def _class_rows(r, d, tm):
    return pl.ds(r, tm // d, stride=d) if d > 1 else slice(None)


def _class_block(d, tm):
    return pl.BlockSpec((tm // d, d * D), lambda i: (i, 0))


def _tokens_from_classes(blk_ref, tmp_ref, d, tm):
    for r in range(d):
        for c in range(NCHUNK):
            tmp_ref[c, _class_rows(r, d, tm), :] = blk_ref[:, _chunk(c, r * D)].astype(F32)


def _classes_from_tokens(tmp_ref, blk_ref, d, tm):
    for r in range(d):
        for c in range(NCHUNK):
            blk_ref[:, _chunk(c, r * D)] = tmp_ref[c, _class_rows(r, d, tm), :].astype(blk_ref.dtype)


def _qkv_classes(name, n2, nk, wq, wkv, g, d, tables):
    def emit(acc, cos_ref, sin_ref, o_ref, tmp_ref, scale):
        for c in range(NCHUNK):
            tmp_ref[c] = acc[:, _chunk(c)]
        for r in range(d):
            rows = _class_rows(r, d, TM)
            if scale is not None:
                cs = cos_ref[rows, :]
                sn = sin_ref[rows, :]
            for c in range(NCHUNK):
                x = tmp_ref[c, rows, :]
                if scale is not None:
                    x = (x * cs + _swap_halves(x) * sn) * scale
                o_ref[:, _chunk(c, r * D)] = x.astype(o_ref.dtype)

    def body(n2_ref, nk_ref, wq_ref, wk_ref, wv_ref, cos_ref, sin_ref, q_ref, k_ref, v_ref, tmp_ref):
        emit(_dot_nn(n2_ref[...], wq_ref[...]), cos_ref, sin_ref, q_ref, tmp_ref, HEAD_DIM ** -0.5)
        x = nk_ref[...]
        emit(_dot_nn(x, wk_ref[...]), cos_ref, sin_ref, k_ref, tmp_ref, 1.0)
        emit(_dot_nn(x, wv_ref[...]), cos_ref, sin_ref, v_ref, tmp_ref, None)

    nbr = len(DILATIONS)
    act = pl.BlockSpec((TM, D), lambda i: (i, 0))
    tab = pl.BlockSpec((TM, 128), lambda i: (i, 0))
    wcol = lambda col: pl.BlockSpec((D, D), lambda i: (0, col))
    return pl.pallas_call(
        body, name=name, grid=(S // TM,),
        in_specs=[act, act, wcol(g), wcol(g), wcol(nbr + g), tab, tab],
        out_specs=[_class_block(d, TM)] * 3, out_shape=[_sds((S // d, d * D), BF16)] * 3,
        scratch_shapes=[pltpu.VMEM((NCHUNK, TM, 128), F32)],
        compiler_params=_cparams(1),
    )(n2, nk, wq, wkv, wkv, *tables)


ATTN_CHAINS = 16


def _attn_units(d):
    nblk = S // d // BAND
    return max(1, 2 * ATTN_CHAINS // nblk)


def _class_spec(d):
    return pl.BlockSpec((S // d, 128 * _attn_units(d)), lambda cb: (0, cb))


def _dot_nt(a, b):
    return lax.dot_general(a, b, _DIMS["nt"], preferred_element_type=F32)


def _dot_tn(a, b):
    return lax.dot_general(a, b, _DIMS["tn"], preferred_element_type=F32)


def _dot_nn(a, b):
    return lax.dot_general(a, b, _DIMS["nn"], preferred_element_type=F32)


def _band_mask(nkeys):
    qi = lax.broadcasted_iota(jnp.int32, (2 * BAND, nkeys), 0) % BAND
    kj = lax.broadcasted_iota(jnp.int32, (2 * BAND, nkeys), 1)
    if nkeys == BAND:
        return kj <= qi
    dist = qi + BAND - kj
    return (dist >= 0) & (dist <= BAND)


def _band_bias():
    return {n: jnp.where(_band_mask(n), 0.0, NEG_INF).astype(F32) for n in (BAND, 2 * BAND)}


def _stack_heads(x):
    row = lax.broadcasted_iota(jnp.int32, (2 * BAND, 128), 0)
    lane = lax.broadcasted_iota(jnp.int32, (2 * BAND, 128), 1)
    keep = (row < BAND) == (lane < HEAD_DIM)
    return jnp.where(keep, jnp.concatenate([x, x], axis=0), jnp.zeros((), x.dtype))


def _unstack(x2):
    first_head = lax.broadcasted_iota(jnp.int32, (BAND, 128), 1) < HEAD_DIM
    return jnp.where(first_head, x2[:BAND], x2[BAND:])


def _for_later_blocks(nblk, units, fn):
    all_lanes = [slice(u * 128, (u + 1) * 128) for u in range(units)]
    unroll = max(1, ATTN_CHAINS // units)
    trips = (nblk - 1) // unroll
    if trips > 1:
        def step(i, carry):
            for j in range(unroll):
                for lanes in all_lanes:
                    fn(pl.multiple_of((1 + i * unroll + j) * BAND, BAND), lanes)
            return carry

        lax.fori_loop(0, trips, step, 0)
    else:
        trips = 0
    for sb in range(1 + trips * unroll, nblk):
        for lanes in all_lanes:
            fn(sb * BAND, lanes)


def _attn_fwd(name, q, k, v, d):
    nblk = S // d // BAND
    units = _attn_units(d)

    def body(q_ref, k_ref, v_ref, o_ref, lse_ref):
        bias = _band_bias()

        def block(r0, k0, nkeys, lanes):
            q2 = _stack_heads(q_ref[pl.ds(r0, BAND), lanes])
            s = _dot_nt(q2, k_ref[pl.ds(k0, nkeys), lanes]) + bias[nkeys]
            m = jnp.max(s, axis=-1, keepdims=True)
            p = jnp.exp(s - m)
            l = jnp.sum(p, axis=-1, keepdims=True)
            o2 = _dot_nn(p.astype(BF16), v_ref[pl.ds(k0, nkeys), lanes])
            l_tile = _unstack(jnp.broadcast_to(l, (2 * BAND, 128)))
            m_tile = _unstack(jnp.broadcast_to(m, (2 * BAND, 128)))
            o_ref[pl.ds(r0, BAND), lanes] = (_unstack(o2) / l_tile).astype(o_ref.dtype)
            lse_ref[pl.ds(r0, BAND), lanes] = m_tile + jnp.log(l_tile)

        for u in range(units):
            block(0, 0, BAND, slice(u * 128, (u + 1) * 128))

        _for_later_blocks(nblk, units, lambda r0, lanes: block(r0, r0 - BAND, 2 * BAND, lanes))

    spec = _class_spec(d)
    return pl.pallas_call(
        body, name=name, grid=(8 * d // units,),
        in_specs=[spec] * 3, out_specs=[spec] * 2,
        out_shape=[_sds((S // d, d * D), BF16), _sds((S // d, d * D), F32)],
        compiler_params=_cparams(1),
    )(q, k, v)


def _attn_bwd(name, q, k, v, do, lse, dd, d):
    nblk = S // d // BAND
    units = _attn_units(d)

    def body(q_ref, k_ref, v_ref, do_ref, lse_ref, dd_ref, dq_ref, dk_out, dv_out, dk_ref, dv_ref):
        bias = _band_bias()
        def column(ref, r0, lanes, nkeys):
            tile = ref[pl.ds(r0, BAND), lanes]
            other = pltpu.roll(tile, HEAD_DIM, 1)
            first_head = lax.broadcasted_iota(jnp.int32, tile.shape, 1) < HEAD_DIM
            both = jnp.concatenate([jnp.where(first_head, tile, other), jnp.where(first_head, other, tile)], axis=0)
            return both if nkeys == BAND else jnp.concatenate([both, both], axis=1)

        def block(r0, k0, nkeys, lanes, first):
            q2 = _stack_heads(q_ref[pl.ds(r0, BAND), lanes])
            do2 = _stack_heads(do_ref[pl.ds(r0, BAND), lanes])
            kk = k_ref[pl.ds(k0, nkeys), lanes]
            vv = v_ref[pl.ds(k0, nkeys), lanes]
            s = _dot_nt(q2, kk) + bias[nkeys]
            p = jnp.exp(s - column(lse_ref, r0, lanes, nkeys))
            ds = (p * (_dot_nt(do2, vv) - column(dd_ref, r0, lanes, nkeys))).astype(BF16)
            dq_ref[pl.ds(r0, BAND), lanes] = _unstack(_dot_nn(ds, kk)).astype(dq_ref.dtype)
            dk_part = _dot_tn(ds, q2)
            dv_part = _dot_tn(p.astype(BF16), do2)
            if first:
                dk_ref[pl.ds(k0, nkeys), lanes] = dk_part
                dv_ref[pl.ds(k0, nkeys), lanes] = dv_part
            else:
                dk_ref[pl.ds(k0, BAND), lanes] += dk_part[:BAND]
                dv_ref[pl.ds(k0, BAND), lanes] += dv_part[:BAND]
                dk_ref[pl.ds(k0 + BAND, BAND), lanes] = dk_part[BAND:]
                dv_ref[pl.ds(k0 + BAND, BAND), lanes] = dv_part[BAND:]

        for u in range(units):
            block(0, 0, BAND, slice(u * 128, (u + 1) * 128), True)

        _for_later_blocks(nblk, units, lambda r0, lanes: block(r0, r0 - BAND, 2 * BAND, lanes, False))
        dk_out[...] = dk_ref[...].astype(dk_out.dtype)
        dv_out[...] = dv_ref[...].astype(dv_out.dtype)

    spec = _class_spec(d)
    return pl.pallas_call(
        body, name=name, grid=(8 * d // units,),
        in_specs=[spec] * 6, out_specs=[spec] * 3,
        out_shape=[_sds((S // d, d * D), BF16)] * 3,
        scratch_shapes=[pltpu.VMEM((S // d, 128 * units), F32)] * 2,
        compiler_params=_cparams(1),
    )(q, k, v, do, lse, dd)


MIX_TILE = 256
DILATIONS = tuple(d for _, d in BRANCHES)


def _branch_weights(la, lb, lc):
    m = jnp.maximum(jnp.maximum(la, lb), lc)
    ea, eb, ec = jnp.exp(la - m), jnp.exp(lb - m), jnp.exp(lc - m)
    den = ea + eb + ec
    return ea / den, eb / den, ec / den


def _mix_operands(outs, lses):
    specs = [_class_block(d, MIX_TILE) for d in DILATIONS] * 2
    scratch = [pltpu.VMEM((NCHUNK, MIX_TILE, 128), F32)] * 4
    return list(outs) + list(lses), specs, scratch


def _mix_fwd(name, outs, lses):
    def body(o0, o1, o2, l0, l1, l2, o_ref, to1, to2, tl1, tl2):
        for blk, tmp, d in ((o1, to1, DILATIONS[1]), (o2, to2, DILATIONS[2]), (l1, tl1, DILATIONS[1]), (l2, tl2, DILATIONS[2])):
            _tokens_from_classes(blk, tmp, d, MIX_TILE)
        for c in range(NCHUNK):
            wa, wb, wc = _branch_weights(l0[:, _chunk(c)], tl1[c], tl2[c])
            o_ref[:, _chunk(c)] = (wa * o0[:, _chunk(c)].astype(F32) + wb * to1[c] + wc * to2[c]).astype(o_ref.dtype)

    operands, specs, scratch = _mix_operands(outs, lses)
    return pl.pallas_call(
        body, name=name, grid=(S // MIX_TILE,),
        in_specs=specs, out_specs=_row_spec(MIX_TILE), out_shape=_sds((S, D), BF16),
        scratch_shapes=scratch, compiler_params=_cparams(1),
    )(*operands)


def _head_sum(x, ones_blockdiag):
    hi = x.astype(BF16)
    lo = (x - hi.astype(F32)).astype(BF16)
    return _dot_nn(hi, ones_blockdiag) + _dot_nn(lo, ones_blockdiag)


def _mix_bwd(name, do, outs, lses, ones_blockdiag):
    def body(do_ref, o0, o1, o2, l0, l1, l2, ones_ref, d0, d1, d2, t0, t1, t2,
             to1, to2, tl1, tl2, td1, td2, tt1, tt2):
        for blk, tmp, d in ((o1, to1, DILATIONS[1]), (o2, to2, DILATIONS[2]), (l1, tl1, DILATIONS[1]), (l2, tl2, DILATIONS[2])):
            _tokens_from_classes(blk, tmp, d, MIX_TILE)
        ones = ones_ref[...]
        for c in range(NCHUNK):
            w = _branch_weights(l0[:, _chunk(c)], tl1[c], tl2[c])
            dov = do_ref[:, _chunk(c)]
            o = w[0] * o0[:, _chunk(c)].astype(F32) + w[1] * to1[c] + w[2] * to2[c]
            t = _head_sum(dov * o, ones)
            d0[:, _chunk(c)] = (w[0] * dov).astype(d0.dtype)
            t0[:, _chunk(c)] = w[0] * t
            td1[c], tt1[c] = w[1] * dov, w[1] * t
            td2[c], tt2[c] = w[2] * dov, w[2] * t
        for tmp, blk, d in ((td1, d1, DILATIONS[1]), (tt1, t1, DILATIONS[1]), (td2, d2, DILATIONS[2]), (tt2, t2, DILATIONS[2])):
            _classes_from_tokens(tmp, blk, d, MIX_TILE)

    operands, specs, scratch = _mix_operands(outs, lses)
    out_specs = [_class_block(d, MIX_TILE) for d in DILATIONS] * 2
    out_shape = [_sds((S // d, d * D), BF16) for d in DILATIONS] + [_sds((S // d, d * D), F32) for d in DILATIONS]
    return pl.pallas_call(
        body, name=name, grid=(S // MIX_TILE,),
        in_specs=[_row_spec(MIX_TILE)] + specs + [_vec_spec(128, 128)],
        out_specs=out_specs, out_shape=out_shape,
        scratch_shapes=scratch + [pltpu.VMEM((NCHUNK, MIX_TILE, 128), F32)] * 4,
        compiler_params=_cparams(1),
    )(do, *operands, ones_blockdiag)


def _attn_bwd_post(name, grads, cos_t, sin_t):
    tm = MIX_TILE
    scale = HEAD_DIM ** -0.5

    def unrope(x, cs, sn):
        return x * cs - _swap_halves(x) * sn

    def body(*refs):
        in_refs = refs[:9]
        cos_ref, sin_ref, dq_ref, dkv_ref, tmp_ref = refs[9:]
        cs = cos_ref[...]
        sn = sin_ref[...]
        for g, d in enumerate(DILATIONS):
            for which, blk in enumerate(in_refs[3 * g:3 * g + 3]):
                if d > 1:
                    _tokens_from_classes(blk, tmp_ref, d, tm)
                for c in range(NCHUNK):
                    x = tmp_ref[c] if d > 1 else blk[:, _chunk(c)].astype(F32)
                    if which == 0:
                        dq_ref[:, _chunk(c, g * D)] = (unrope(x, cs, sn) * scale).astype(dq_ref.dtype)
                    elif which == 1:
                        dkv_ref[:, _chunk(c, g * D)] = unrope(x, cs, sn).astype(dkv_ref.dtype)
                    else:
                        dkv_ref[:, _chunk(c, QW + g * D)] = x.astype(dkv_ref.dtype)

    operands = [a for branch in grads for a in branch]
    tab = pl.BlockSpec((tm, 128), lambda i: (i, 0))
    return pl.pallas_call(
        body, name=name, grid=(S // tm,),
        in_specs=[_class_block(d, tm) for d in DILATIONS for _ in range(3)] + [tab, tab],
        out_specs=[pl.BlockSpec((tm, QW), lambda i: (i, 0)), pl.BlockSpec((tm, 2 * QW), lambda i: (i, 0))],
        out_shape=[_sds((S, QW), BF16), _sds((S, 2 * QW), BF16)],
        scratch_shapes=[pltpu.VMEM((NCHUNK, tm, 128), F32)],
        compiler_params=_cparams(1),
    )(*operands, cos_t, sin_t)


def _adamw(name, parts, w, m, v, layer=None, other=None):
    n, rows, cols = parts.shape
    tr = rows
    for cand in (256, 176, 128, 64, 32, 16, 8):
        if rows % cand == 0:
            tr = cand
            break
    n_other = 0 if other is None else len(other)

    def body(p_ref, w_ref, m_ref, v_ref, *refs):
        g_ref, d_ref, nm_ref, nv_ref = refs[n_other:]
        g = p_ref[0].astype(F32)
        for j in range(1, n):
            g = g + p_ref[j].astype(F32)
        g_ref[...] = g
        d_ref[...], nm_ref[...], nv_ref[...] = _adam_update(g, w_ref[...], m_ref[...], v_ref[...])

    if layer is None:
        blk = pl.BlockSpec((tr, cols), lambda i: (i, 0))
        shape = (rows, cols)
    else:
        blk = pl.BlockSpec((None, tr, cols), lambda i: (layer, i, 0))
        shape = w.shape
    return pl.pallas_call(
        body, name=name, grid=(rows // tr,),
        in_specs=[pl.BlockSpec((n, tr, cols), lambda i: (0, i, 0)), blk, blk, blk]
                 + [pl.BlockSpec(memory_space=pl.ANY)] * n_other,
        out_specs=[blk] * 4, out_shape=[_sds(shape, F32)] * 4,
        input_output_aliases={4 + i: i for i in range(n_other)},
        compiler_params=_cparams(1),
    )(parts, w, m, v, *(other or ()))


def _adam_update(g, w, m, v):
    c1 = 1.0 / (1.0 - ADAM_B1 ** ADAM_STEP)
    c2 = 1.0 / (1.0 - ADAM_B2 ** ADAM_STEP)
    nm = ADAM_B1 * m + (1.0 - ADAM_B1) * g
    nv = ADAM_B2 * v + (1.0 - ADAM_B2) * (g * g)
    return -ADAM_LR * ((nm * c1) / (jnp.sqrt(nv * c2) + ADAM_EPS) + ADAM_WD * w), nm, nv


GAIN_ROWS = 16


def _pack_small(name, gain_tiles, taps, sq):
    ng = len(gain_tiles)

    def body(*refs):
        o_ref = refs[-1]
        o_ref[...] = jnp.zeros_like(o_ref)
        for i in range(ng):
            o_ref[i:i + 1, :] = refs[i][0:1, :]
        o_ref[ng:ng + 3, :] = refs[ng][0:3, :]
        o_ref[ng + 3:ng + 4, :] = refs[ng + 1][...]

    return pl.pallas_call(body, name=name, out_shape=_sds((GAIN_ROWS, D), F32))(*gain_tiles, taps, sq)


def _adamw_gains(name, parts, params):
    np_ = len(params)
    shapes = [w.shape for w, _, _ in params]

    def body(p_ref, *refs):
        ins, outs = refs[:3 * np_], refs[3 * np_:]

        def total(lo, rows):
            g = p_ref[0, lo:lo + rows, :]
            for j in range(1, NDEV):
                g = g + p_ref[j, lo:lo + rows, :]
            return g

        lo = 0
        for i, shape in enumerate(shapes):
            g = total(lo, shape[0])
            lo += shape[0]
            w_ref, m_ref, v_ref = ins[3 * i:3 * i + 3]
            g_ref, d_ref, nm_ref, nv_ref = outs[4 * i:4 * i + 4]
            g_ref[...] = g
            d_ref[...], nm_ref[...], nv_ref[...] = _adam_update(g, w_ref[...], m_ref[...], v_ref[...])
        taps_ref, loss_ref = outs[-2], outs[-1]
        taps_ref[...] = jnp.zeros_like(taps_ref)
        taps_ref[0:3, :] = total(lo, 3)
        loss_ref[...] = jnp.sum(total(lo + 3, 1), axis=-1, keepdims=True) * (0.5 / D)

    out_shape = [_sds(shape, F32) for shape in shapes for _ in range(4)] + [_sds((8, D), F32), _sds((1, 1), F32)]
    outs = pl.pallas_call(body, name=name, out_shape=out_shape)(parts, *[a for p in params for a in p])
    return [list(outs[4 * i:4 * i + 4]) for i in range(np_)], outs[-2], outs[-1].reshape(())


def _exchange(name, arrays, kind, after):
    n = len(arrays)
    gather = kind == "gather"
    out_shape = [_sds((NDEV,) + a.shape if gather else a.shape, a.dtype) for a in arrays]

    def body(*refs):
        srcs, outs = refs[:n], refs[n + 1:2 * n + 1]
        send_sems, recv_sems, local_sems = refs[2 * n + 1:]
        x, y, c = lax.axis_index("x"), lax.axis_index("y"), lax.axis_index("c")
        me = 4 * x + 2 * y + c
        pending = []
        for t in range(n):
            own = pltpu.make_async_copy(srcs[t] if gather else srcs[t].at[me], outs[t].at[me], local_sems.at[t])
            own.start()
            pending.append(own)
            for rel in range(1, NDEV):
                px = 1 - x if rel & 4 else x
                py = 1 - y if rel & 2 else y
                pc = 1 - c if rel & 1 else c
                peer = 4 * px + 2 * py + pc
                send = pltpu.make_async_remote_copy(
                    src_ref=srcs[t] if gather else srcs[t].at[peer], dst_ref=outs[t].at[me],
                    send_sem=send_sems.at[t, rel - 1], recv_sem=recv_sems.at[t, rel - 1],
                    device_id=(px, py, pc), device_id_type=MESH)
                send.start()
                arrive = pltpu.make_async_remote_copy(
                    src_ref=srcs[t] if gather else srcs[t].at[me], dst_ref=outs[t].at[peer],
                    send_sem=send_sems.at[t, rel - 1], recv_sem=recv_sems.at[t, rel - 1],
                    device_id=(px, py, pc), device_id_type=MESH)
                pending.append((send, arrive))
        for item in pending:
            if isinstance(item, tuple):
                item[0].wait_send()
                item[1].wait_recv()
            else:
                item.wait()

    any_spec = pl.BlockSpec(memory_space=pl.ANY)
    outs = pl.pallas_call(
        body, name=name,
        in_specs=[any_spec] * (n + 1), out_specs=[any_spec] * n, out_shape=out_shape,
        scratch_shapes=[pltpu.SemaphoreType.DMA((n, NDEV - 1)), pltpu.SemaphoreType.DMA((n, NDEV - 1)),
                        pltpu.SemaphoreType.DMA((n,))],
    )(*arrays, after)
    return list(outs)


_HBM_SPEC = pl.BlockSpec(memory_space=pltpu.HBM)
_SEM_SPEC = pl.BlockSpec(memory_space=pltpu.SEMAPHORE)
_DATAFLOW = pltpu.SideEffectType.DATAFLOW_SIDE_EFFECTING


def _peers():
    x, y, c = lax.axis_index("x"), lax.axis_index("y"), lax.axis_index("c")
    out = []
    for rel in range(1, NDEV):
        px = 1 - x if rel & 4 else x
        py = 1 - y if rel & 2 else y
        pc = 1 - c if rel & 1 else c
        out.append((rel - 1, (px, py, pc), 4 * px + 2 * py + pc))
    return 4 * x + 2 * y + c, out


def _hbm(a):
    return pltpu.HBM(a.shape, a.dtype)


def _own_slot(a, me, kind):
    mine = a[None] if kind == "gather" else lax.dynamic_slice_in_dim(a, me, 1, axis=0)
    shape = (NDEV,) + mine.shape[1:]
    return lax.dynamic_update_slice_in_dim(lax.empty(shape, a.dtype), mine, me, axis=0)


def _exchange_start(name, arrays, me, kind):
    n = len(arrays)
    gather = kind == "gather"
    lands = [_own_slot(a, me, kind) for a in arrays]

    def body(*refs):
        src_refs, land_refs = refs[:n], refs[n:2 * n]
        send_sems, recv_sems = refs[2 * n], refs[2 * n + 1]
        token = refs[-1]
        my_block, peers = _peers()
        for t in range(n):
            for slot, dev, block in peers:
                pltpu.make_async_remote_copy(
                    src_ref=src_refs[t] if gather else src_refs[t].at[block], dst_ref=land_refs[t].at[my_block],
                    send_sem=send_sems.at[t * (NDEV - 1) + slot], recv_sem=recv_sems.at[t * (NDEV - 1) + slot],
                    device_id=dev, device_id_type=MESH).start()
        token[...] = jnp.zeros_like(token)

    operands = [pltpu.with_memory_space_constraint(a, pltpu.HBM) for a in list(arrays) + lands]
    outs = pl.pallas_call(
        body, name=name,
        out_shape=(pltpu.SemaphoreType.DMA((n * (NDEV - 1),)), pltpu.SemaphoreType.DMA((n * (NDEV - 1),)),
                   *[_hbm(a) for a in operands], _sds((8, 128), F32)),
        in_specs=[_HBM_SPEC] * (2 * n),
        out_specs=(_SEM_SPEC, _SEM_SPEC, *[_HBM_SPEC] * (2 * n), pl.BlockSpec(memory_space=pltpu.VMEM)),
        input_output_aliases={i: 2 + i for i in range(2 * n)},
        compiler_params=pltpu.CompilerParams(has_side_effects=_DATAFLOW),
    )(*operands)
    return (outs[0], outs[1], list(outs[2:2 + n]), list(outs[2 + n:2 + 2 * n])), outs[-1]


def _exchange_wait(name, started, t, after, kind):
    send_sems, recv_sems, srcs, lands = started
    gather = kind == "gather"

    def body(src_ref, land_ref, send_ref, recv_ref, after_ref, src_out, land_out):
        _, peers = _peers()
        for slot, dev, block in peers:
            copy = pltpu.make_async_remote_copy(
                src_ref=src_ref if gather else src_ref.at[block], dst_ref=land_ref.at[block],
                send_sem=send_ref.at[t * (NDEV - 1) + slot], recv_sem=recv_ref.at[t * (NDEV - 1) + slot],
                device_id=dev, device_id_type=MESH)
            copy.wait_send()
            copy.wait_recv()

    return pl.pallas_call(
        body, name=name, out_shape=(_hbm(srcs[t]), _hbm(lands[t])),
        in_specs=(_HBM_SPEC, _HBM_SPEC, _SEM_SPEC, _SEM_SPEC, pl.BlockSpec(memory_space=pl.ANY)),
        out_specs=(_HBM_SPEC, _HBM_SPEC), input_output_aliases={0: 0, 1: 1},
        compiler_params=pltpu.CompilerParams(has_side_effects=_DATAFLOW),
    )(srcs[t], lands[t], send_sems, recv_sems, after)[1]


DIRECT_RELS = (1, 2, 4, 6)
RELAY_RELS = (2, 4, 6)


def _rel_peer(rel):
    x, y, c = lax.axis_index("x"), lax.axis_index("y"), lax.axis_index("c")
    px = 1 - x if rel & 4 else x
    py = 1 - y if rel & 2 else y
    pc = 1 - c if rel & 1 else c
    return (px, py, pc), 4 * px + 2 * py + pc


def _gather_start(name, shards, me):
    n, nr = len(shards), len(DIRECT_RELS)
    lands = [_own_slot(a, me, "gather") for a in shards]

    def body(*refs):
        src_refs, land_refs = refs[:n], refs[n:2 * n]
        send_sems, recv_sems = refs[2 * n], refs[2 * n + 1]
        _, my_block = _rel_peer(0)
        for t in range(n):
            for s, rel in enumerate(DIRECT_RELS):
                dev, _ = _rel_peer(rel)
                pltpu.make_async_remote_copy(
                    src_ref=src_refs[t], dst_ref=land_refs[t].at[my_block],
                    send_sem=send_sems.at[t * nr + s], recv_sem=recv_sems.at[t * nr + s],
                    device_id=dev, device_id_type=MESH).start()

    operands = [pltpu.with_memory_space_constraint(a, pltpu.HBM) for a in list(shards) + lands]
    outs = pl.pallas_call(
        body, name=name,
        out_shape=(pltpu.SemaphoreType.DMA((n * nr,)), pltpu.SemaphoreType.DMA((n * nr,)), *[_hbm(a) for a in operands]),
        in_specs=[_HBM_SPEC] * (2 * n), out_specs=(_SEM_SPEC, _SEM_SPEC, *[_HBM_SPEC] * (2 * n)),
        input_output_aliases={i: 2 + i for i in range(2 * n)},
        compiler_params=pltpu.CompilerParams(has_side_effects=_DATAFLOW),
    )(*operands)
    return outs[0], outs[1], list(outs[2:2 + n]), list(outs[2 + n:2 + 2 * n])


def _gather_wait(name, started, ts, after):
    send_sems, recv_sems, srcs, lands = started
    m, nr = len(ts), len(DIRECT_RELS)

    def body(*refs):
        src_refs, land_refs = refs[:m], refs[m:2 * m]
        send_ref, recv_ref = refs[2 * m], refs[2 * m + 1]
        for i, t in enumerate(ts):
            for s, rel in enumerate(DIRECT_RELS):
                dev, block = _rel_peer(rel)
                copy = pltpu.make_async_remote_copy(
                    src_ref=src_refs[i], dst_ref=land_refs[i].at[block],
                    send_sem=send_ref.at[t * nr + s], recv_sem=recv_ref.at[t * nr + s],
                    device_id=dev, device_id_type=MESH)
                copy.wait_send()
                copy.wait_recv()

    operands = [srcs[t] for t in ts] + [lands[t] for t in ts]
    outs = pl.pallas_call(
        body, name=name, out_shape=tuple(_hbm(a) for a in operands),
        in_specs=[_HBM_SPEC] * (2 * m) + [_SEM_SPEC, _SEM_SPEC, pl.BlockSpec(memory_space=pl.ANY)],
        out_specs=tuple([_HBM_SPEC] * (2 * m)), input_output_aliases={i: i for i in range(2 * m)},
        compiler_params=pltpu.CompilerParams(has_side_effects=_DATAFLOW),
    )(*operands, send_sems, recv_sems, after)
    return list(outs[m:])


def _relay_start(name, lands):
    m, nr = len(lands), len(RELAY_RELS)

    def body(*refs):
        land_refs, send_sems, recv_sems = refs[:m], refs[m], refs[m + 1]
        sibling, _ = _rel_peer(1)
        for i in range(m):
            for s, rel in enumerate(RELAY_RELS):
                _, block = _rel_peer(rel)
                pltpu.make_async_remote_copy(
                    src_ref=land_refs[i].at[block], dst_ref=land_refs[i].at[block],
                    send_sem=send_sems.at[i * nr + s], recv_sem=recv_sems.at[i * nr + s],
                    device_id=sibling, device_id_type=MESH).start()

    outs = pl.pallas_call(
        body, name=name,
        out_shape=(pltpu.SemaphoreType.DMA((m * nr,)), pltpu.SemaphoreType.DMA((m * nr,)), *[_hbm(a) for a in lands]),
        in_specs=[_HBM_SPEC] * m, out_specs=(_SEM_SPEC, _SEM_SPEC, *[_HBM_SPEC] * m),
        input_output_aliases={i: 2 + i for i in range(m)},
        compiler_params=pltpu.CompilerParams(has_side_effects=_DATAFLOW),
    )(*lands)
    return outs[0], outs[1], list(outs[2:])


def _relay_wait(name, relayed, after):
    send_sems, recv_sems, lands = relayed
    m, nr = len(lands), len(RELAY_RELS)

    def body(*refs):
        land_refs, send_ref, recv_ref = refs[:m], refs[m], refs[m + 1]
        sibling, _ = _rel_peer(1)
        for i in range(m):
            for s, rel in enumerate(RELAY_RELS):
                _, sent = _rel_peer(rel)
                _, arriving = _rel_peer(rel ^ 1)
                copy = pltpu.make_async_remote_copy(
                    src_ref=land_refs[i].at[sent], dst_ref=land_refs[i].at[arriving],
                    send_sem=send_ref.at[i * nr + s], recv_sem=recv_ref.at[i * nr + s],
                    device_id=sibling, device_id_type=MESH)
                copy.wait_send()
                copy.wait_recv()

    outs = pl.pallas_call(
        body, name=name, out_shape=tuple(_hbm(a) for a in lands),
        in_specs=[_HBM_SPEC] * m + [_SEM_SPEC, _SEM_SPEC, pl.BlockSpec(memory_space=pl.ANY)],
        out_specs=tuple([_HBM_SPEC] * m), input_output_aliases={i: i for i in range(m)},
        compiler_params=pltpu.CompilerParams(has_side_effects=_DATAFLOW),
    )(*lands, send_sems, recv_sems, after)
    return list(outs)


def _ffn_fwd(tag, n, wg, wd):
    gu, act = _gate_up_act(f"ffn_gate_up_{tag}", n, wg)
    wd4 = wd.reshape(NFB, FB, D)
    f = _fwd_kblocked(f"ffn_down_{tag}", act, wd4)
    return (n, gu, act, wg, wd4), f


def _ffn_bwd(tag, dh_out, df, h_in, saved, g_pre, send, mixer):
    n, gu, act, wg, wd4 = saved
    dwd = _bwd_w_kblocked(f"ffn_down_dw_{tag}", act, df).reshape(NDEV, DFF // NDEV, D)
    dgu = _down_dx_act_bwd(f"ffn_down_dx_{tag}", df, wd4, gu).reshape(NDEV, S, FB)
    tok = send({f"down_{tag}": dwd, f"gate_up_{tag}": _bwd_w_cols_blocked(f"ffn_gate_up_dw_{tag}", n, dgu)})
    dn = _bwd_x_cols_blocked(f"ffn_gate_up_dx_{tag}", dgu, wg, after=tok)
    dh_in, (dg_pre,), dy, dg_mixer = _rms_bwd(f"ffn_prenorm_bwd_{tag}", h_in, [(g_pre, dn)], dh_out, F32, then=mixer)
    return dh_in, dg_pre, dy, dg_mixer


def kernel(x, positions, mix_norm_pre, mix_norm_post, ffn_norm_pre, ffn_norm_post, ffn_w_gate_up, ffn_w_down, conv_w_in, conv_w, conv_w_out, kv_norm, w_kv, w_q, w_o, loss_target, m_mix_norm_pre, m_mix_norm_post, m_ffn_norm_pre, m_ffn_norm_post, m_ffn_w_gate_up, m_ffn_w_down, m_conv_w_in, m_conv_w, m_conv_w_out, m_kv_norm, m_w_kv, m_w_q, m_w_o, v_mix_norm_pre, v_mix_norm_post, v_ffn_norm_pre, v_ffn_norm_post, v_ffn_w_gate_up, v_ffn_w_down, v_conv_w_in, v_conv_w, v_conv_w_out, v_kv_norm, v_w_kv, v_w_q, v_w_o):
    me = 4 * lax.axis_index("x") + 2 * lax.axis_index("y") + lax.axis_index("c")
    h0 = x.reshape(S, D)
    target = loss_target.reshape(S, D)
    row = lambda a, l: a[l].reshape(1, D)
    g_kv = kv_norm.reshape(1, D)

    cw_shard = jnp.pad(conv_w[0], ((0, 5), (0, 0)))
    names = ["conv_in", "conv_w", "conv_out", "gate_up_0", "down_0", "kv", "q", "o", "gate_up_1", "down_1"]
    shards = [conv_w_in[0], cw_shard, conv_w_out[0], ffn_w_gate_up[0], ffn_w_down[0],
              w_kv, w_q[0], w_o[0], ffn_w_gate_up[1], ffn_w_down[1]]
    shards = [s if n == "conv_w" else s.astype(BF16) for n, s in zip(names, shards)]
    first = 3
    gather_first = _gather_start("gather_start_conv", shards[:first], me)
    gather_rest = _gather_start("gather_start_rest", shards[first:], me)

    def direct(group, after):
        ts = [names.index(n) for n in group]
        started, ts = (gather_first, ts) if ts[0] < first else (gather_rest, [t - first for t in ts])
        lands = _gather_wait(f"gather_wait_{group[0]}", started, ts, after)
        return _relay_start(f"relay_start_{group[0]}", lands)

    def finish(group, relayed, after):
        return dict(zip(group, _relay_wait(f"relay_wait_{group[0]}", relayed, after)))

    sent = {}

    def send(grads):
        started, token = _exchange_start(f"scatter_start_{next(iter(grads))}", list(grads.values()), me, "scatter")
        for i, name in enumerate(grads):
            sent[name] = (started, i)
        return token

    groups = [["conv_in", "conv_w", "conv_out"], ["gate_up_0", "down_0"], ["kv", "q"], ["o", "gate_up_1", "down_1"]]
    n0 = _rms_fwd("mix_prenorm_0", h0, [row(mix_norm_pre, 0)])[0]
    half = HEAD_DIM // 2
    inv_freq = ROPE_THETA ** (-jnp.arange(half, dtype=F32) / half)
    tables = _rope_tables("rope_tables", positions.reshape(S, 1), jnp.tile(inv_freq, 4).reshape(1, 128))
    w = finish(groups[0], direct(groups[0], tables[0]), n0)
    win = w["conv_in"].transpose(1, 0, 2).reshape(D, 3 * D)
    cw = w["conv_w"].transpose(1, 0, 2).reshape(8, D)
    wout = w["conv_out"].reshape(D, D)
    z = _fwd_rows("conv_in", n0, win, BF16)
    pre = _conv_fwd("conv_gate", z, cw)
    relayed = direct(groups[1], pre)
    y0 = _fwd_rows("conv_out", pre, wout)
    h1, (n1,) = _resid_rms("mix_postnorm_0", h0, y0, row(mix_norm_post, 0), [row(ffn_norm_pre, 0)])
    w = finish(groups[1], relayed, n1)
    ffn0, f0 = _ffn_fwd("0", n1, w["gate_up_0"], w["down_0"])
    relayed = direct(groups[2], ffn0[2])
    h2, (nk, n2) = _resid_rms("ffn_postnorm_0", h1, f0, row(ffn_norm_post, 0), [g_kv, row(mix_norm_pre, 1)])

    w = finish(groups[2], relayed, nk)
    wkv = w["kv"].transpose(1, 0, 2).reshape(D, 2 * QW)
    wq = w["q"].transpose(1, 0, 2).reshape(D, QW)
    qc, kc, vc, o_c, lse_c = [], [], [], [], []
    for g, d in enumerate(DILATIONS):
        q_g, k_g, v_g = _qkv_classes(f"qkv_proj_{g}", n2, nk, wq, wkv, g, d, tables)
        qc.append(q_g)
        kc.append(k_g)
        vc.append(v_g)
    relayed = direct(groups[3], vc[-1])
    for g, d in enumerate(DILATIONS):
        o_g, lse_g = _attn_fwd(f"attn_fwd_{g}", qc[g], kc[g], vc[g], d)
        o_c.append(o_g)
        lse_c.append(lse_g)
    o_mix = _mix_fwd("attn_mix", o_c, lse_c)
    w = finish(groups[3], relayed, o_mix)
    wo = w["o"].reshape(D, D)
    y1 = _fwd_rows("attn_out", o_mix, wo)
    h3, (n3,) = _resid_rms("mix_postnorm_1", h2, y1, row(mix_norm_post, 1), [row(ffn_norm_pre, 1)])
    ffn1, f1 = _ffn_fwd("1", n3, w["gate_up_1"], w["down_1"])

    dh4, df1, dg_fpost1, sq = _resid_rms_loss("ffn_postnorm_1_loss", h3, f1, row(ffn_norm_post, 1), target)

    dh3, dg_fpre1, dy1, dg_mpost1 = _ffn_bwd(
        "1", dh4, df1, h3, ffn1, row(ffn_norm_pre, 1), send, (y1, row(mix_norm_post, 1)))
    dwo = _bwd_w_rows("attn_out_dw", o_mix, dy1).reshape(NDEV, D // NDEV, D)
    do = _bwd_x_rows("attn_out_dx", dy1, wo, BF16)
    lane = jnp.arange(128)
    ones_blockdiag = (lane[:, None] // HEAD_DIM == lane[None, :] // HEAD_DIM).astype(BF16)
    mixed = _mix_bwd("attn_mix_bwd", do, o_c, lse_c, ones_blockdiag)
    branch_grads = [_attn_bwd(f"attn_bwd_{g}", qc[g], kc[g], vc[g], mixed[g], lse_c[g], mixed[3 + g], d)
                    for g, d in enumerate(DILATIONS)]
    dq_raw, dkv = _attn_bwd_post("attn_bwd_post", branch_grads, *tables)
    tok = send({"o": dwo, "kv": _bwd_w_cols("kv_proj_dw", nk, dkv, 2 * QW // NDEV),
                "q": _bwd_w_cols("q_proj_dw", n2, dq_raw, QW // NDEV)})
    dnk = _bwd_x_plain("kv_proj_dx", dkv, wkv, after=tok)
    dn2 = _bwd_x_plain("q_proj_dx", dq_raw, wq)
    dh2, (dg_kv, dg_mpre1), df0, dg_fpost0 = _rms_bwd(
        "kv_and_mix_prenorm_bwd_1", h2, [(g_kv, dnk), (row(mix_norm_pre, 1), dn2)], dh3, F32,
        then=(f0, row(ffn_norm_post, 0)))

    dh1, dg_fpre0, dy0, dg_mpost0 = _ffn_bwd(
        "0", dh2, df0, h1, ffn0, row(ffn_norm_pre, 0), send, (y0, row(mix_norm_post, 0)))
    dwout = _bwd_w_rows("conv_out_dw", pre, dy0).reshape(NDEV, D // NDEV, D)
    dpre = _bwd_x_rows("conv_out_dx", dy0, wout, BF16)
    dz, dcw = _conv_bwd("conv_gate_bwd", z, dpre, cw)
    tok = send({"conv_out": dwout, "conv_in": _bwd_w_cols("conv_in_dw", n0, dz, 3 * D // NDEV)})
    dn0 = _bwd_x_plain("conv_in_dx", dz, win, after=tok)
    dh0, (dg_mpre0,) = _rms_bwd("mix_prenorm_bwd_0", h0, [(row(mix_norm_pre, 0), dn0)], dh1, F32)

    small = _pack_small("pack_small_grads", [dg_mpre0, dg_mpre1, dg_mpost0, dg_mpost1, dg_fpre0, dg_fpre1,
                                             dg_fpost0, dg_fpost1, dg_kv], dcw, sq)

    done = [small]

    def upd(tag, w, m, v):
        parts = _exchange_wait(f"scatter_wait_{tag}", *sent[tag], done[-1], "scatter")
        shape = w.shape
        flat = lambda a: a.reshape(parts.shape[1:])
        res = _adamw(f"adamw_{tag}", parts, flat(w), flat(m), flat(v))
        done.append(res[0])
        return [r.reshape(shape) for r in res]

    def upd_layer(tag, l, w, m, v, other):
        parts = _exchange_wait(f"scatter_wait_{tag}_{l}", *sent[f"{tag}_{l}"], done[-1], "scatter")
        res = _adamw(f"adamw_{tag}_{l}", parts, w, m, v, layer=l, other=other)
        done.append(res[0])
        return list(res)

    res = {}
    down_1 = upd_layer("down", 1, ffn_w_down, m_ffn_w_down, v_ffn_w_down, None)
    gate_up_t = [jnp.swapaxes(a, 1, 2) for a in (ffn_w_gate_up, m_ffn_w_gate_up, v_ffn_w_gate_up)]
    gate_up_1 = upd_layer("gate_up", 1, *gate_up_t, None)
    res["w_o"] = upd("o", w_o, m_w_o, v_w_o)
    res["w_q"] = upd("q", w_q, m_w_q, v_w_q)
    res["w_kv"] = upd("kv", w_kv, m_w_kv, v_w_kv)

    small_all = _exchange("gather_small_grads", [small], "gather", done[-1])[0]
    vec = lambda a: a.reshape(1, D)
    gain_res, taps, loss = _adamw_gains("adamw_gains", small_all, [
        (mix_norm_pre, m_mix_norm_pre, v_mix_norm_pre), (mix_norm_post, m_mix_norm_post, v_mix_norm_post),
        (ffn_norm_pre, m_ffn_norm_pre, v_ffn_norm_pre), (ffn_norm_post, m_ffn_norm_post, v_ffn_norm_post),
        (vec(kv_norm), vec(m_kv_norm), vec(v_kv_norm))])
    dcw_mine = lax.dynamic_slice(taps, (0, me * 128), (8, 128))
    pad8 = lambda a, fill: jnp.pad(a[0], ((0, 5), (0, 0)), constant_values=fill)
    cw_res = [r[0:3].reshape(1, 3, 128) for r in
              _adamw("adamw_conv_w", dcw_mine.reshape(1, 8, 128), cw_shard, pad8(m_conv_w, 0.0), pad8(v_conv_w, 1.0))]

    res.update({
        "mix_norm_pre": gain_res[0],
        "mix_norm_post": gain_res[1],
        "ffn_norm_pre": gain_res[2],
        "ffn_norm_post": gain_res[3],
        "kv_norm": [r.reshape(D) for r in gain_res[4]],
        "conv_w": cw_res,
    })
    done.append(small_all)
    res["ffn_w_down"] = upd_layer("down", 0, ffn_w_down, m_ffn_w_down, v_ffn_w_down, down_1)
    res["ffn_w_gate_up"] = [jnp.swapaxes(r, 1, 2) for r in upd_layer("gate_up", 0, *gate_up_t, gate_up_1)]
    res["conv_w_out"] = upd("conv_out", conv_w_out, m_conv_w_out, v_conv_w_out)
    res["conv_w_in"] = upd("conv_in", conv_w_in, m_conv_w_in, v_conv_w_in)
    order = ["mix_norm_pre", "mix_norm_post", "ffn_norm_pre", "ffn_norm_post", "ffn_w_gate_up", "ffn_w_down",
             "conv_w_in", "conv_w", "conv_w_out", "kv_norm", "w_kv", "w_q", "w_o"]
    out = [loss, dh0.reshape(1, S, D)]
    for i in range(4):
        out += [res[name][i] for name in order]
    return tuple(out)
```

```python
import jax
import jax.numpy as jnp
from jax import lax
from jax.experimental import pallas as pl
from jax.experimental.pallas import tpu as pltpu

F32 = jnp.float32
BF16 = jnp.bfloat16

S = 4096
D = 1024
NDEV = 8
HEAD_DIM = 64
QW = 3072
DFF = 2816
FB = 704
NFB = 4
BRANCHES = ((128, 1), (512, 4), (2048, 16))
BAND = 128
ROPE_THETA = 10000.0
RMS_EPS = 1e-6
NEG_INF = -1e30
ADAM_LR, ADAM_B1, ADAM_B2, ADAM_EPS, ADAM_WD, ADAM_STEP = 0.001, 0.9, 0.999, 1e-08, 0.01, 10

VMEM_LIMIT_BYTES = 52 * 1024 * 1024
ROW_TILE = 512
MESH = pl.DeviceIdType.MESH


def _cparams(ngrid):
    return pltpu.CompilerParams(dimension_semantics=("arbitrary",) * ngrid,
                                vmem_limit_bytes=VMEM_LIMIT_BYTES)


def _sds(shape, dtype):
    return jax.ShapeDtypeStruct(tuple(shape), dtype)


_DIMS = {"nn": (((1,), (0,)), ((), ())),
         "nt": (((1,), (1,)), ((), ())),
         "tn": (((0,), (0,)), ((), ()))}


def _matmul(name, a, b, *, mode, grid, a_blk, a_map, b_blk, b_map, o_shape, o_blk, o_map, out_dtype, after=None,
            out_groups=1):
    nk = grid[2]
    dims = _DIMS[mode]
    acc_shape = tuple(s for s in o_blk if s is not None)
    if out_groups > 1:
        acc_shape = (acc_shape[1], out_groups * acc_shape[2])
    extra = [] if after is None else [after]

    def store(o_ref, val):
        if out_groups == 1:
            o_ref[...] = val.astype(o_ref.dtype)
        else:
            n = o_ref.shape[-1]
            for grp in range(out_groups):
                o_ref[grp] = val[:, grp * n:(grp + 1) * n].astype(o_ref.dtype)

    def body(a_ref, b_ref, *rest):
        o_ref, scratch = rest[len(extra)], rest[len(extra) + 1:]
        part = lax.dot_general(a_ref[...], b_ref[...], dims, preferred_element_type=F32)
        if nk == 1:
            store(o_ref, part)
            return
        acc_ref = scratch[0]
        k = pl.program_id(2)

        @pl.when(k == 0)
        def _():
            acc_ref[...] = part

        @pl.when(k > 0)
        def _():
            acc_ref[...] += part

        @pl.when(k == nk - 1)
        def _():
            store(o_ref, acc_ref[...])

    return pl.pallas_call(
        body, name=name, grid=grid,
        in_specs=[pl.BlockSpec(a_blk, a_map), pl.BlockSpec(b_blk, b_map)] + [pl.BlockSpec(memory_space=pl.ANY)] * len(extra),
        out_specs=pl.BlockSpec(o_blk, o_map),
        out_shape=_sds(o_shape, out_dtype),
        scratch_shapes=[] if nk == 1 else [pltpu.VMEM(acc_shape, F32)],
        compiler_params=_cparams(3),
    )(a, b, *extra)


TM = 1024
TK = S


def _fwd_rows(name, a, w, out_dtype=F32):
    kdim, n = w.shape
    tn = 1024
    return _matmul(name, a, w, mode="nn", grid=(S // TM, n // tn, 1),
                   a_blk=(TM, kdim), a_map=lambda i, j, k: (i, 0),
                   b_blk=(kdim, tn), b_map=lambda i, j, k: (0, j),
                   o_shape=(S, n), o_blk=(TM, tn), o_map=lambda i, j, k: (i, j), out_dtype=out_dtype)


def _fwd_kblocked(name, a4, w4):
    nb, _, kb = a4.shape
    n = w4.shape[2]

    def body(a_ref, w_ref, o_ref):
        acc = _dot_nn(a_ref[0], w_ref[0])
        for j in range(1, nb):
            acc = acc + _dot_nn(a_ref[j], w_ref[j])
        o_ref[...] = acc

    return pl.pallas_call(
        body, name=name, grid=(S // TM,),
        in_specs=[pl.BlockSpec((nb, TM, kb), lambda i: (0, i, 0)), pl.BlockSpec((nb, kb, n), lambda i: (0, 0, 0))],
        out_specs=pl.BlockSpec((TM, n), lambda i: (i, 0)), out_shape=_sds((S, n), F32),
        compiler_params=_cparams(1),
    )(a4, w4)


def _bwd_x_cols_blocked(name, dy8, wg, after):
    _, kdim, n = wg.shape
    nk = NDEV // 2

    def body(a_ref, b_ref, after_ref, o_ref, acc_ref):
        k = pl.program_id(1)
        part = _dot_nt(a_ref[0], b_ref[0]) + _dot_nt(a_ref[1], b_ref[1])

        @pl.when(k == 0)
        def _():
            acc_ref[...] = part

        @pl.when(k > 0)
        def _():
            acc_ref[...] += part

        @pl.when(k == nk - 1)
        def _():
            o_ref[...] = acc_ref[...].astype(o_ref.dtype)

    return pl.pallas_call(
        body, name=name, grid=(S // TM, nk),
        in_specs=[pl.BlockSpec((2, None, TM, n), lambda i, k: (0, k, i, 0)),
                  pl.BlockSpec((2, None, kdim, n), lambda i, k: (0, k, 0, 0)),
                  pl.BlockSpec(memory_space=pl.ANY)],
        out_specs=pl.BlockSpec((TM, kdim), lambda i, k: (i, 0)), out_shape=_sds((S, kdim), BF16),
        scratch_shapes=[pltpu.VMEM((TM, kdim), F32)],
        compiler_params=_cparams(2),
    )(dy8.reshape(2, nk, S, n), wg.reshape(2, nk, kdim, n), after)


def _bwd_x_rows(name, dy, w, out_dtype, after=None):
    kdim, n = w.shape
    tkk = 512
    return _matmul(name, dy, w, mode="nt", grid=(S // TM, kdim // tkk, 1),
                   a_blk=(TM, n), a_map=lambda i, j, k: (i, 0),
                   b_blk=(tkk, n), b_map=lambda i, j, k: (j, 0),
                   o_shape=(S, kdim), o_blk=(TM, tkk), o_map=lambda i, j, k: (i, j), out_dtype=out_dtype, after=after)


DW_COLS = 768


def _bwd_w_cols(name, a, dy, n):
    kdim = a.shape[1]
    groups = DW_COLS // n
    return _matmul(name, a, dy, mode="tn", grid=(1, NDEV // groups, S // TK),
                   a_blk=(TK, kdim), a_map=lambda i, j, k: (k, 0),
                   b_blk=(TK, DW_COLS), b_map=lambda i, j, k: (k, j),
                   o_shape=(NDEV, kdim, n), o_blk=(groups, kdim, n) if groups > 1 else (None, kdim, n),
                   o_map=lambda i, j, k: (j, 0, 0), out_dtype=BF16, out_groups=groups)


def _bwd_x_plain(name, dy, w, after=None):
    kdim, n = w.shape
    tm = TM if n <= 3 * D else TM // 2
    return _matmul(name, dy, w, mode="nt", grid=(S // tm, 1, 1),
                   a_blk=(tm, n), a_map=lambda i, j, k: (i, 0),
                   b_blk=(kdim, n), b_map=lambda i, j, k: (0, 0),
                   o_shape=(S, kdim), o_blk=(tm, kdim), o_map=lambda i, j, k: (i, 0), out_dtype=BF16, after=after)


def _bwd_w_cols_blocked(name, a, dy8):
    kdim = a.shape[1]
    n = dy8.shape[2]
    return _matmul(name, dy8, a, mode="tn", grid=(1, NDEV, S // TK),
                   a_blk=(None, TK, n), a_map=lambda i, j, k: (j, k, 0),
                   b_blk=(TK, kdim), b_map=lambda i, j, k: (k, 0),
                   o_shape=(NDEV, n, kdim), o_blk=(None, n, kdim), o_map=lambda i, j, k: (j, 0, 0), out_dtype=BF16)


def _bwd_w_rows(name, a, dy):
    kdim = a.shape[1]
    n = dy.shape[1]
    tmm = 512
    return _matmul(name, a, dy, mode="tn", grid=(kdim // tmm, 1, S // TK),
                   a_blk=(TK, tmm), a_map=lambda i, j, k: (k, i),
                   b_blk=(TK, n), b_map=lambda i, j, k: (k, 0),
                   o_shape=(kdim, n), o_blk=(tmm, n), o_map=lambda i, j, k: (i, 0), out_dtype=BF16)


def _bwd_w_kblocked(name, a4, dy):
    nb, _, kb = a4.shape
    n = dy.shape[1]
    return _matmul(name, a4, dy, mode="tn", grid=(nb, 1, S // TK),
                   a_blk=(None, TK, kb), a_map=lambda i, j, k: (i, k, 0),
                   b_blk=(TK, n), b_map=lambda i, j, k: (k, 0),
                   o_shape=(nb, kb, n), o_blk=(None, kb, n), o_map=lambda i, j, k: (i, 0, 0), out_dtype=BF16)


def _rstd(x):
    return lax.rsqrt(jnp.mean(x * x, axis=-1, keepdims=True) + RMS_EPS)


def _row_spec(tm=ROW_TILE, width=D):
    return pl.BlockSpec((tm, width), lambda i: (i, 0))


def _vec_spec(rows=1, width=D):
    return pl.BlockSpec((rows, width), lambda i: (0, 0))


def _rms_fwd(name, x, gains):
    n = len(gains)

    def body(x_ref, *refs):
        x_val = x_ref[...]
        xh = x_val * _rstd(x_val)
        for g_ref, o_ref in zip(refs[:n], refs[n:]):
            o_ref[...] = (xh * g_ref[...]).astype(o_ref.dtype)

    outs = pl.pallas_call(
        body, name=name, grid=(S // ROW_TILE,),
        in_specs=[_row_spec()] + [_vec_spec()] * n,
        out_specs=[_row_spec()] * n,
        out_shape=[_sds((S, D), BF16)] * n,
        compiler_params=_cparams(1),
    )(x, *gains)
    return list(outs)


def _resid_rms(name, h, y, g, next_gains):
    n = len(next_gains)

    def body(h_ref, y_ref, g_ref, *refs):
        y_val = y_ref[...]
        h_new = h_ref[...] + (y_val * _rstd(y_val)) * g_ref[...]
        refs[n][...] = h_new
        hh = h_new * _rstd(h_new)
        for g2_ref, o_ref in zip(refs[:n], refs[n + 1:]):
            o_ref[...] = (hh * g2_ref[...]).astype(o_ref.dtype)

    outs = pl.pallas_call(
        body, name=name, grid=(S // ROW_TILE,),
        in_specs=[_row_spec(), _row_spec(), _vec_spec()] + [_vec_spec()] * n,
        out_specs=[_row_spec()] * (n + 1), out_shape=[_sds((S, D), F32)] + [_sds((S, D), BF16)] * n,
        compiler_params=_cparams(1),
    )(h, y, g, *next_gains)
    return outs[0], list(outs[1:])


def _resid_rms_loss(name, h, y, g, target):
    def body(h_ref, y_ref, g_ref, t_ref, dh_ref, dy_ref, dg_ref, part_ref):
        y_val = y_ref[...]
        gain = g_ref[...]
        e = h_ref[...] + (y_val * _rstd(y_val)) * gain - t_ref[...]
        dh = e * (1.0 / D)
        dh_ref[...] = dh
        step = pl.program_id(0)
        dy_ref[...] = _norm_bwd_rows(y_val, gain, dh, dg_ref, step).astype(dy_ref.dtype)
        part = jnp.sum(e * e, axis=0, keepdims=True)

        @pl.when(step == 0)
        def _():
            part_ref[...] = part

        @pl.when(step > 0)
        def _():
            part_ref[...] += part

    return pl.pallas_call(
        body, name=name, grid=(S // ROW_TILE,),
        in_specs=[_row_spec(), _row_spec(), _vec_spec(), _row_spec()],
        out_specs=[_row_spec(), _row_spec(), _vec_spec(8), _vec_spec()],
        out_shape=[_sds((S, D), F32), _sds((S, D), BF16), _sds((8, D), F32), _sds((1, D), F32)],
        compiler_params=_cparams(1),
    )(h, y, g, target)


def _norm_bwd_rows(x_val, g, dn, dg_ref, step):
    r = _rstd(x_val)
    xh = x_val * r
    dxh = dn * g
    part = jnp.sum(dn * xh, axis=0, keepdims=True)

    @pl.when(step == 0)
    def _():
        dg_ref[...] = jnp.zeros_like(dg_ref)

    dg_ref[0:1, :] += part
    return r * (dxh - xh * jnp.mean(dxh * xh, axis=-1, keepdims=True))


def _rms_bwd(name, x, pairs, dres, out_dtype, then=None):
    n = len(pairs)
    has_res = dres is not None
    chained = then is not None

    def body(x_ref, *refs):
        g_refs = refs[0:2 * n:2]
        dn_refs = refs[1:2 * n:2]
        pos = 2 * n
        res_ref = refs[pos] if has_res else None
        pos += int(has_res)
        if chained:
            y_ref, gy_ref = refs[pos], refs[pos + 1]
            pos += 2
        dx_ref = refs[pos]
        dg_refs = refs[pos + 1:pos + 1 + n]
        step = pl.program_id(0)
        x_val = x_ref[...]
        acc = res_ref[...] if has_res else jnp.zeros_like(x_val)
        for g_ref, dn_ref, dg_ref in zip(g_refs, dn_refs, dg_refs):
            acc = acc + _norm_bwd_rows(x_val, g_ref[...], dn_ref[...].astype(F32), dg_ref, step)
        dx_ref[...] = acc.astype(dx_ref.dtype)
        if chained:
            dy_ref, dgy_ref = refs[pos + 1 + n], refs[pos + 2 + n]
            dy_ref[...] = _norm_bwd_rows(y_ref[...], gy_ref[...], acc, dgy_ref, step).astype(dy_ref.dtype)

    operands = [x]
    in_specs = [_row_spec()]
    for g, dn in pairs:
        operands += [g, dn]
        in_specs += [_vec_spec(), _row_spec()]
    if has_res:
        operands.append(dres)
        in_specs.append(_row_spec())
    if chained:
        operands += [then[0], then[1]]
        in_specs += [_row_spec(), _vec_spec()]
    extra = int(chained)
    outs = pl.pallas_call(
        body, name=name, grid=(S // ROW_TILE,),
        in_specs=in_specs,
        out_specs=[_row_spec()] + [_vec_spec(8)] * n + [_row_spec(), _vec_spec(8)] * extra,
        out_shape=[_sds((S, D), out_dtype)] + [_sds((8, D), F32)] * n + [_sds((S, D), BF16), _sds((8, D), F32)] * extra,
        compiler_params=_cparams(1),
    )(*operands)
    if chained:
        return outs[0], list(outs[1:1 + n]), outs[1 + n], outs[2 + n]
    return outs[0], list(outs[1:])


def _shift_down(u, prev8, k):
    r = pltpu.roll(u, k, 0)
    p = pltpu.roll(prev8, k, 0)
    row = lax.broadcasted_iota(jnp.int32, prev8.shape, 0)
    top = jnp.where(row < k, p, r[0:8])
    return jnp.concatenate([top, r[8:]], axis=0)


def _shift_up(u, next8, k):
    tm = u.shape[0]
    r = pltpu.roll(u, tm - k, 0)
    p = pltpu.roll(next8, 8 - k, 0)
    row = lax.broadcasted_iota(jnp.int32, next8.shape, 0)
    bot = jnp.where(row >= 8 - k, p, r[tm - 8:tm])
    return jnp.concatenate([r[:tm - 8], bot], axis=0)


CONV_TILE = 512


def _halo_prev(col):
    return pl.BlockSpec((8, D), lambda i: (jnp.maximum(i * (CONV_TILE // 8) - 1, 0), col))


def _halo_next(col):
    last = S // 8 - 1
    return pl.BlockSpec((8, D), lambda i: (jnp.minimum((i + 1) * (CONV_TILE // 8), last), col))


def _conv_fwd(name, z, cw):
    def body(b_ref, c_ref, h_ref, cp_ref, hp_ref, cw_ref, o_ref):
        i = pl.program_id(0)
        u = c_ref[...].astype(F32) * h_ref[...].astype(F32)
        up = cp_ref[...].astype(F32) * hp_ref[...].astype(F32)
        up = jnp.where(i > 0, up, 0.0)
        cv = cw_ref[0:1, :] * _shift_down(u, up, 2) + cw_ref[1:2, :] * _shift_down(u, up, 1) + cw_ref[2:3, :] * u
        o_ref[...] = (b_ref[...].astype(F32) * cv).astype(o_ref.dtype)

    col = lambda c: pl.BlockSpec((CONV_TILE, D), lambda i: (i, c))
    return pl.pallas_call(
        body, name=name, grid=(S // CONV_TILE,),
        in_specs=[col(0), col(1), col(2), _halo_prev(1), _halo_prev(2), _vec_spec(8)],
        out_specs=_row_spec(CONV_TILE), out_shape=_sds((S, D), BF16),
        compiler_params=_cparams(1),
    )(z, z, z, z, z, cw)


def _conv_bwd(name, z, dpre, cw):
    nsteps = S // CONV_TILE

    def body(b_ref, c_ref, h_ref, cp_ref, hp_ref, dp_ref, dpn_ref, bn_ref, cw_ref, dz_ref, dcw_ref):
        i = pl.program_id(0)
        b = b_ref[...].astype(F32)
        c = c_ref[...].astype(F32)
        h = h_ref[...].astype(F32)
        dp = dp_ref[...].astype(F32)
        u = c * h
        up = jnp.where(i > 0, cp_ref[...].astype(F32) * hp_ref[...].astype(F32), 0.0)
        s1 = _shift_down(u, up, 1)
        s2 = _shift_down(u, up, 2)
        w0, w1, w2 = cw_ref[0:1, :], cw_ref[1:2, :], cw_ref[2:3, :]
        cv = w0 * s2 + w1 * s1 + w2 * u
        dcv = dp * b
        dcvn = jnp.where(i < nsteps - 1, dpn_ref[...].astype(F32) * bn_ref[...].astype(F32), 0.0)
        du = w2 * dcv + w1 * _shift_up(dcv, dcvn, 1) + w0 * _shift_up(dcv, dcvn, 2)
        dz_ref[:, 0:D] = (dp * cv).astype(dz_ref.dtype)
        dz_ref[:, D:2 * D] = (du * h).astype(dz_ref.dtype)
        dz_ref[:, 2 * D:3 * D] = (du * c).astype(dz_ref.dtype)

        @pl.when(i == 0)
        def _():
            dcw_ref[...] = jnp.zeros_like(dcw_ref)

        dcw_ref[0:1, :] += jnp.sum(dcv * s2, axis=0, keepdims=True)
        dcw_ref[1:2, :] += jnp.sum(dcv * s1, axis=0, keepdims=True)
        dcw_ref[2:3, :] += jnp.sum(dcv * u, axis=0, keepdims=True)

    col = lambda c: pl.BlockSpec((CONV_TILE, D), lambda i: (i, c))
    return pl.pallas_call(
        body, name=name, grid=(nsteps,),
        in_specs=[col(0), col(1), col(2), _halo_prev(1), _halo_prev(2),
                  _row_spec(CONV_TILE), _halo_next(0), _halo_next(0), _vec_spec(8)],
        out_specs=[pl.BlockSpec((CONV_TILE, 3 * D), lambda i: (i, 0)), _vec_spec(8)],
        out_shape=[_sds((S, 3 * D), BF16), _sds((8, D), F32)],
        compiler_params=_cparams(1),
    )(z, z, z, z, z, dpre, dpre, z, cw)


FFN_TM = 2048
_GU_BLOCK = pl.BlockSpec((2, None, FFN_TM, FB), lambda i, j: (0, j, i, 0))


def _gate_up_act(name, a, wg):
    kdim = a.shape[1]

    def body(a_ref, wgate_ref, wup_ref, gu_ref, act_ref):
        x = a_ref[...]
        g = _dot_nn(x, wgate_ref[...])
        u = _dot_nn(x, wup_ref[...])
        gu_ref[0] = g.astype(gu_ref.dtype)
        gu_ref[1] = u.astype(gu_ref.dtype)
        act_ref[...] = (g * jax.nn.sigmoid(g) * u).astype(act_ref.dtype)

    return pl.pallas_call(
        body, name=name, grid=(S // FFN_TM, NFB),
        in_specs=[pl.BlockSpec((FFN_TM, kdim), lambda i, j: (i, 0)),
                  pl.BlockSpec((None, kdim, FB), lambda i, j: (j, 0, 0)),
                  pl.BlockSpec((None, kdim, FB), lambda i, j: (j + NFB, 0, 0))],
        out_specs=[_GU_BLOCK, pl.BlockSpec((None, FFN_TM, FB), lambda i, j: (j, i, 0))],
        out_shape=[_sds((2, NFB, S, FB), BF16), _sds((NFB, S, FB), BF16)],
        compiler_params=_cparams(2),
    )(a, wg, wg)


def _down_dx_act_bwd(name, df, w4, gu):
    _, kb, n = w4.shape

    def body(df_ref, w_ref, gu_ref, o_ref):
        d = _dot_nt(df_ref[...], w_ref[...])
        g = gu_ref[0].astype(F32)
        u = gu_ref[1].astype(F32)
        sg = jax.nn.sigmoid(g)
        o_ref[0] = (d * u * sg * (1.0 + g * (1.0 - sg))).astype(o_ref.dtype)
        o_ref[1] = (d * g * sg).astype(o_ref.dtype)

    return pl.pallas_call(
        body, name=name, grid=(S // FFN_TM, NFB),
        in_specs=[pl.BlockSpec((FFN_TM, n), lambda i, j: (i, 0)), pl.BlockSpec((None, kb, n), lambda i, j: (j, 0, 0)),
                  _GU_BLOCK],
        out_specs=_GU_BLOCK, out_shape=_sds((2, NFB, S, FB), BF16),
        compiler_params=_cparams(2),
    )(df, w4, gu)


def _rope_tables(name, pos_col, inv_freq_row):
    def body(pos_ref, f_ref, cos_ref, sin_ref):
        ang = pos_ref[...].astype(F32) * f_ref[...]
        lane = lax.broadcasted_iota(jnp.int32, ang.shape, 1)
        s = jnp.sin(ang)
        cos_ref[...] = jnp.cos(ang)
        sin_ref[...] = jnp.where((lane % HEAD_DIM) < HEAD_DIM // 2, -s, s)

    tab = pl.BlockSpec((ROW_TILE, 128), lambda i: (i, 0))
    return pl.pallas_call(
        body, name=name, grid=(S // ROW_TILE,),
        in_specs=[pl.BlockSpec((ROW_TILE, 1), lambda i: (i, 0)), _vec_spec(1, 128)],
        out_specs=[tab, tab], out_shape=[_sds((S, 128), F32)] * 2,
        compiler_params=_cparams(1),
    )(pos_col, inv_freq_row)


def _swap_halves(t):
    lane = lax.broadcasted_iota(jnp.int32, t.shape, 1)
    first = (lane % HEAD_DIM) < HEAD_DIM // 2
    return jnp.where(first, pltpu.roll(t, 128 - HEAD_DIM // 2, 1), pltpu.roll(t, HEAD_DIM // 2, 1))


NCHUNK = D // 128


def _chunk(c, base=0):
    return slice(base + c * 128, base + (c + 1) * 128)


def _class_rows(r, d, tm):
    return pl.ds(r, tm // d, stride=d) if d > 1 else slice(None)


def _class_block(d, tm):
    return pl.BlockSpec((tm // d, d * D), lambda i: (i, 0))


def _tokens_from_classes(blk_ref, tmp_ref, d, tm):
    for r in range(d):
        for c in range(NCHUNK):
            tmp_ref[c, _class_rows(r, d, tm), :] = blk_ref[:, _chunk(c, r * D)].astype(F32)


def _classes_from_tokens(tmp_ref, blk_ref, d, tm):
    for r in range(d):
        for c in range(NCHUNK):
            blk_ref[:, _chunk(c, r * D)] = tmp_ref[c, _class_rows(r, d, tm), :].astype(blk_ref.dtype)


def _qkv_classes(name, n2, nk, wq, wkv, g, d, tables):
    def emit(acc, cos_ref, sin_ref, o_ref, tmp_ref, scale):
        for c in range(NCHUNK):
            tmp_ref[c] = acc[:, _chunk(c)]
        for r in range(d):
            rows = _class_rows(r, d, TM)
            if scale is not None:
                cs = cos_ref[rows, :]
                sn = sin_ref[rows, :]
            for c in range(NCHUNK):
                x = tmp_ref[c, rows, :]
                if scale is not None:
                    x = (x * cs + _swap_halves(x) * sn) * scale
                o_ref[:, _chunk(c, r * D)] = x.astype(o_ref.dtype)

    def body(n2_ref, nk_ref, wq_ref, wk_ref, wv_ref, cos_ref, sin_ref, q_ref, k_ref, v_ref, tmp_ref):
        emit(_dot_nn(n2_ref[...], wq_ref[...]), cos_ref, sin_ref, q_ref, tmp_ref, HEAD_DIM ** -0.5)
        x = nk_ref[...]
        emit(_dot_nn(x, wk_ref[...]), cos_ref, sin_ref, k_ref, tmp_ref, 1.0)
        emit(_dot_nn(x, wv_ref[...]), cos_ref, sin_ref, v_ref, tmp_ref, None)

    nbr = len(DILATIONS)
    act = pl.BlockSpec((TM, D), lambda i: (i, 0))
    tab = pl.BlockSpec((TM, 128), lambda i: (i, 0))
    wcol = lambda col: pl.BlockSpec((D, D), lambda i: (0, col))
    return pl.pallas_call(
        body, name=name, grid=(S // TM,),
        in_specs=[act, act, wcol(g), wcol(g), wcol(nbr + g), tab, tab],
        out_specs=[_class_block(d, TM)] * 3, out_shape=[_sds((S // d, d * D), BF16)] * 3,
        scratch_shapes=[pltpu.VMEM((NCHUNK, TM, 128), F32)],
        compiler_params=_cparams(1),
    )(n2, nk, wq, wkv, wkv, *tables)


ATTN_CHAINS = 16


def _attn_units(d):
    nblk = S // d // BAND
    return max(1, 2 * ATTN_CHAINS // nblk)


def _class_spec(d):
    return pl.BlockSpec((S // d, 128 * _attn_units(d)), lambda cb: (0, cb))


def _dot_nt(a, b):
    return lax.dot_general(a, b, _DIMS["nt"], preferred_element_type=F32)


def _dot_tn(a, b):
    return lax.dot_general(a, b, _DIMS["tn"], preferred_element_type=F32)


def _dot_nn(a, b):
    return lax.dot_general(a, b, _DIMS["nn"], preferred_element_type=F32)


def _band_mask(nkeys):
    qi = lax.broadcasted_iota(jnp.int32, (2 * BAND, nkeys), 0) % BAND
    kj = lax.broadcasted_iota(jnp.int32, (2 * BAND, nkeys), 1)
    if nkeys == BAND:
        return kj <= qi
    dist = qi + BAND - kj
    return (dist >= 0) & (dist <= BAND)


def _band_bias():
    return {n: jnp.where(_band_mask(n), 0.0, NEG_INF).astype(F32) for n in (BAND, 2 * BAND)}


def _stack_heads(x):
    row = lax.broadcasted_iota(jnp.int32, (2 * BAND, 128), 0)
    lane = lax.broadcasted_iota(jnp.int32, (2 * BAND, 128), 1)
    keep = (row < BAND) == (lane < HEAD_DIM)
    return jnp.where(keep, jnp.concatenate([x, x], axis=0), jnp.zeros((), x.dtype))


def _unstack(x2):
    first_head = lax.broadcasted_iota(jnp.int32, (BAND, 128), 1) < HEAD_DIM
    return jnp.where(first_head, x2[:BAND], x2[BAND:])


def _for_later_blocks(nblk, units, fn):
    all_lanes = [slice(u * 128, (u + 1) * 128) for u in range(units)]
    unroll = max(1, ATTN_CHAINS // units)
    trips = (nblk - 1) // unroll
    if trips > 1:
        def step(i, carry):
            for j in range(unroll):
                for lanes in all_lanes:
                    fn(pl.multiple_of((1 + i * unroll + j) * BAND, BAND), lanes)
            return carry

        lax.fori_loop(0, trips, step, 0)
    else:
        trips = 0
    for sb in range(1 + trips * unroll, nblk):
        for lanes in all_lanes:
            fn(sb * BAND, lanes)


def _attn_fwd(name, q, k, v, d):
    nblk = S // d // BAND
    units = _attn_units(d)

    def body(q_ref, k_ref, v_ref, o_ref, lse_ref):
        bias = _band_bias()

        def block(r0, k0, nkeys, lanes):
            q2 = _stack_heads(q_ref[pl.ds(r0, BAND), lanes])
            s = _dot_nt(q2, k_ref[pl.ds(k0, nkeys), lanes]) + bias[nkeys]
            m = jnp.max(s, axis=-1, keepdims=True)
            p = jnp.exp(s - m)
            l = jnp.sum(p, axis=-1, keepdims=True)
            o2 = _dot_nn(p.astype(BF16), v_ref[pl.ds(k0, nkeys), lanes])
            l_tile = _unstack(jnp.broadcast_to(l, (2 * BAND, 128)))
            m_tile = _unstack(jnp.broadcast_to(m, (2 * BAND, 128)))
            o_ref[pl.ds(r0, BAND), lanes] = (_unstack(o2) / l_tile).astype(o_ref.dtype)
            lse_ref[pl.ds(r0, BAND), lanes] = m_tile + jnp.log(l_tile)

        for u in range(units):
            block(0, 0, BAND, slice(u * 128, (u + 1) * 128))

        _for_later_blocks(nblk, units, lambda r0, lanes: block(r0, r0 - BAND, 2 * BAND, lanes))

    spec = _class_spec(d)
    return pl.pallas_call(
        body, name=name, grid=(8 * d // units,),
        in_specs=[spec] * 3, out_specs=[spec] * 2,
        out_shape=[_sds((S // d, d * D), BF16), _sds((S // d, d * D), F32)],
        compiler_params=_cparams(1),
    )(q, k, v)


def _attn_bwd(name, q, k, v, do, lse, dd, d):
    nblk = S // d // BAND
    units = _attn_units(d)

    def body(q_ref, k_ref, v_ref, do_ref, lse_ref, dd_ref, dq_ref, dk_out, dv_out, dk_ref, dv_ref):
        bias = _band_bias()
        def column(ref, r0, lanes, nkeys):
            tile = ref[pl.ds(r0, BAND), lanes]
            other = pltpu.roll(tile, HEAD_DIM, 1)
            first_head = lax.broadcasted_iota(jnp.int32, tile.shape, 1) < HEAD_DIM
            both = jnp.concatenate([jnp.where(first_head, tile, other), jnp.where(first_head, other, tile)], axis=0)
            return both if nkeys == BAND else jnp.concatenate([both, both], axis=1)

        def block(r0, k0, nkeys, lanes, first):
            q2 = _stack_heads(q_ref[pl.ds(r0, BAND), lanes])
            do2 = _stack_heads(do_ref[pl.ds(r0, BAND), lanes])
            kk = k_ref[pl.ds(k0, nkeys), lanes]
            vv = v_ref[pl.ds(k0, nkeys), lanes]
            s = _dot_nt(q2, kk) + bias[nkeys]
            p = jnp.exp(s - column(lse_ref, r0, lanes, nkeys))
            ds = (p * (_dot_nt(do2, vv) - column(dd_ref, r0, lanes, nkeys))).astype(BF16)
            dq_ref[pl.ds(r0, BAND), lanes] = _unstack(_dot_nn(ds, kk)).astype(dq_ref.dtype)
            dk_part = _dot_tn(ds, q2)
            dv_part = _dot_tn(p.astype(BF16), do2)
            if first:
                dk_ref[pl.ds(k0, nkeys), lanes] = dk_part
                dv_ref[pl.ds(k0, nkeys), lanes] = dv_part
            else:
                dk_ref[pl.ds(k0, BAND), lanes] += dk_part[:BAND]
                dv_ref[pl.ds(k0, BAND), lanes] += dv_part[:BAND]
                dk_ref[pl.ds(k0 + BAND, BAND), lanes] = dk_part[BAND:]
                dv_ref[pl.ds(k0 + BAND, BAND), lanes] = dv_part[BAND:]

        for u in range(units):
            block(0, 0, BAND, slice(u * 128, (u + 1) * 128), True)

        _for_later_blocks(nblk, units, lambda r0, lanes: block(r0, r0 - BAND, 2 * BAND, lanes, False))
        dk_out[...] = dk_ref[...].astype(dk_out.dtype)
        dv_out[...] = dv_ref[...].astype(dv_out.dtype)

    spec = _class_spec(d)
    return pl.pallas_call(
        body, name=name, grid=(8 * d // units,),
        in_specs=[spec] * 6, out_specs=[spec] * 3,
        out_shape=[_sds((S // d, d * D), BF16)] * 3,
        scratch_shapes=[pltpu.VMEM((S // d, 128 * units), F32)] * 2,
        compiler_params=_cparams(1),
    )(q, k, v, do, lse, dd)


MIX_TILE = 256
DILATIONS = tuple(d for _, d in BRANCHES)


def _branch_weights(la, lb, lc):
    m = jnp.maximum(jnp.maximum(la, lb), lc)
    ea, eb, ec = jnp.exp(la - m), jnp.exp(lb - m), jnp.exp(lc - m)
    inv = 1.0 / (ea + eb + ec)
    return ea * inv, eb * inv, ec * inv


def _mix_operands(outs, lses):
    specs = [_class_block(d, MIX_TILE) for d in DILATIONS] * 2
    scratch = [pltpu.VMEM((NCHUNK, MIX_TILE, 128), F32)] * 4
    return list(outs) + list(lses), specs, scratch


def _mix_fwd(name, outs, lses):
    def body(o0, o1, o2, l0, l1, l2, o_ref, to1, to2, tl1, tl2):
        for blk, tmp, d in ((o1, to1, DILATIONS[1]), (o2, to2, DILATIONS[2]), (l1, tl1, DILATIONS[1]), (l2, tl2, DILATIONS[2])):
            _tokens_from_classes(blk, tmp, d, MIX_TILE)
        for c in range(NCHUNK):
            wa, wb, wc = _branch_weights(l0[:, _chunk(c)], tl1[c], tl2[c])
            o_ref[:, _chunk(c)] = (wa * o0[:, _chunk(c)].astype(F32) + wb * to1[c] + wc * to2[c]).astype(o_ref.dtype)

    operands, specs, scratch = _mix_operands(outs, lses)
    return pl.pallas_call(
        body, name=name, grid=(S // MIX_TILE,),
        in_specs=specs, out_specs=_row_spec(MIX_TILE), out_shape=_sds((S, D), BF16),
        scratch_shapes=scratch, compiler_params=_cparams(1),
    )(*operands)


def _head_sum(x, ones_blockdiag):
    hi = x.astype(BF16)
    lo = (x - hi.astype(F32)).astype(BF16)
    return _dot_nn(hi, ones_blockdiag) + _dot_nn(lo, ones_blockdiag)


def _mix_bwd(name, do, outs, lses, ones_blockdiag):
    def body(do_ref, o0, o1, o2, l0, l1, l2, ones_ref, d0, d1, d2, t0, t1, t2,
             to1, to2, tl1, tl2, td1, td2, tt1, tt2):
        for blk, tmp, d in ((o1, to1, DILATIONS[1]), (o2, to2, DILATIONS[2]), (l1, tl1, DILATIONS[1]), (l2, tl2, DILATIONS[2])):
            _tokens_from_classes(blk, tmp, d, MIX_TILE)
        ones = ones_ref[...]
        for c in range(NCHUNK):
            w = _branch_weights(l0[:, _chunk(c)], tl1[c], tl2[c])
            dov = do_ref[:, _chunk(c)]
            o = w[0] * o0[:, _chunk(c)].astype(F32) + w[1] * to1[c] + w[2] * to2[c]
            t = _head_sum(dov * o, ones)
            d0[:, _chunk(c)] = (w[0] * dov).astype(d0.dtype)
            t0[:, _chunk(c)] = w[0] * t
            td1[c], tt1[c] = w[1] * dov, w[1] * t
            td2[c], tt2[c] = w[2] * dov, w[2] * t
        for tmp, blk, d in ((td1, d1, DILATIONS[1]), (tt1, t1, DILATIONS[1]), (td2, d2, DILATIONS[2]), (tt2, t2, DILATIONS[2])):
            _classes_from_tokens(tmp, blk, d, MIX_TILE)

    operands, specs, scratch = _mix_operands(outs, lses)
    out_specs = [_class_block(d, MIX_TILE) for d in DILATIONS] * 2
    out_shape = [_sds((S // d, d * D), BF16) for d in DILATIONS] + [_sds((S // d, d * D), F32) for d in DILATIONS]
    return pl.pallas_call(
        body, name=name, grid=(S // MIX_TILE,),
        in_specs=[_row_spec(MIX_TILE)] + specs + [_vec_spec(128, 128)],
        out_specs=out_specs, out_shape=out_shape,
        scratch_shapes=scratch + [pltpu.VMEM((NCHUNK, MIX_TILE, 128), F32)] * 4,
        compiler_params=_cparams(1),
    )(do, *operands, ones_blockdiag)


def _attn_bwd_post(name, grads, cos_t, sin_t):
    tm = MIX_TILE
    scale = HEAD_DIM ** -0.5

    def unrope(x, cs, sn):
        return x * cs - _swap_halves(x) * sn

    def body(*refs):
        in_refs = refs[:9]
        cos_ref, sin_ref, dq_ref, dkv_ref, tmp_ref = refs[9:]
        cs = cos_ref[...]
        sn = sin_ref[...]
        for g, d in enumerate(DILATIONS):
            for which, blk in enumerate(in_refs[3 * g:3 * g + 3]):
                if d > 1:
                    _tokens_from_classes(blk, tmp_ref, d, tm)
                for c in range(NCHUNK):
                    x = tmp_ref[c] if d > 1 else blk[:, _chunk(c)].astype(F32)
                    if which == 0:
                        dq_ref[:, _chunk(c, g * D)] = (unrope(x, cs, sn) * scale).astype(dq_ref.dtype)
                    elif which == 1:
                        dkv_ref[:, _chunk(c, g * D)] = unrope(x, cs, sn).astype(dkv_ref.dtype)
                    else:
                        dkv_ref[:, _chunk(c, QW + g * D)] = x.astype(dkv_ref.dtype)

    operands = [a for branch in grads for a in branch]
    tab = pl.BlockSpec((tm, 128), lambda i: (i, 0))
    return pl.pallas_call(
        body, name=name, grid=(S // tm,),
        in_specs=[_class_block(d, tm) for d in DILATIONS for _ in range(3)] + [tab, tab],
        out_specs=[pl.BlockSpec((tm, QW), lambda i: (i, 0)), pl.BlockSpec((tm, 2 * QW), lambda i: (i, 0))],
        out_shape=[_sds((S, QW), BF16), _sds((S, 2 * QW), BF16)],
        scratch_shapes=[pltpu.VMEM((NCHUNK, tm, 128), F32)],
        compiler_params=_cparams(1),
    )(*operands, cos_t, sin_t)


def _adamw(name, parts, w, m, v, layer=None, other=None):
    n, rows, cols = parts.shape
    tr = rows
    for cand in (256, 176, 128, 64, 32, 16, 8):
        if rows % cand == 0:
            tr = cand
            break
    n_other = 0 if other is None else len(other)

    def body(p_ref, w_ref, m_ref, v_ref, *refs):
        g_ref, d_ref, nm_ref, nv_ref = refs[n_other:]
        g = p_ref[0].astype(F32)
        for j in range(1, n):
            g = g + p_ref[j].astype(F32)
        g_ref[...] = g
        d_ref[...], nm_ref[...], nv_ref[...] = _adam_update(g, w_ref[...], m_ref[...], v_ref[...])

    if layer is None:
        blk = pl.BlockSpec((tr, cols), lambda i: (i, 0))
        shape = (rows, cols)
    else:
        blk = pl.BlockSpec((None, tr, cols), lambda i: (layer, i, 0))
        shape = w.shape
    return pl.pallas_call(
        body, name=name, grid=(rows // tr,),
        in_specs=[pl.BlockSpec((n, tr, cols), lambda i: (0, i, 0)), blk, blk, blk]
                 + [pl.BlockSpec(memory_space=pl.ANY)] * n_other,
        out_specs=[blk] * 4, out_shape=[_sds(shape, F32)] * 4,
        input_output_aliases={4 + i: i for i in range(n_other)},
        compiler_params=_cparams(1),
    )(parts, w, m, v, *(other or ()))


def _adam_update(g, w, m, v):
    c1 = 1.0 / (1.0 - ADAM_B1 ** ADAM_STEP)
    c2 = 1.0 / (1.0 - ADAM_B2 ** ADAM_STEP)
    nm = ADAM_B1 * m + (1.0 - ADAM_B1) * g
    nv = ADAM_B2 * v + (1.0 - ADAM_B2) * (g * g)
    return -ADAM_LR * ((nm * c1) / (jnp.sqrt(nv * c2) + ADAM_EPS) + ADAM_WD * w), nm, nv


GAIN_ROWS = 16


def _pack_small(name, gain_tiles, taps, sq):
    ng = len(gain_tiles)

    def body(*refs):
        o_ref = refs[-1]
        o_ref[...] = jnp.zeros_like(o_ref)
        for i in range(ng):
            o_ref[i:i + 1, :] = refs[i][0:1, :]
        o_ref[ng:ng + 3, :] = refs[ng][0:3, :]
        o_ref[ng + 3:ng + 4, :] = refs[ng + 1][...]

    return pl.pallas_call(body, name=name, out_shape=_sds((GAIN_ROWS, D), F32))(*gain_tiles, taps, sq)


def _adamw_gains(name, parts, params):
    np_ = len(params)
    shapes = [w.shape for w, _, _ in params]

    def body(p_ref, *refs):
        ins, outs = refs[:3 * np_], refs[3 * np_:]

        def total(lo, rows):
            g = p_ref[0, lo:lo + rows, :]
            for j in range(1, NDEV):
                g = g + p_ref[j, lo:lo + rows, :]
            return g

        lo = 0
        for i, shape in enumerate(shapes):
            g = total(lo, shape[0])
            lo += shape[0]
            w_ref, m_ref, v_ref = ins[3 * i:3 * i + 3]
            g_ref, d_ref, nm_ref, nv_ref = outs[4 * i:4 * i + 4]
            g_ref[...] = g
            d_ref[...], nm_ref[...], nv_ref[...] = _adam_update(g, w_ref[...], m_ref[...], v_ref[...])
        taps_ref, loss_ref = outs[-2], outs[-1]
        taps_ref[...] = jnp.zeros_like(taps_ref)
        taps_ref[0:3, :] = total(lo, 3)
        loss_ref[...] = jnp.sum(total(lo + 3, 1), axis=-1, keepdims=True) * (0.5 / D)

    out_shape = [_sds(shape, F32) for shape in shapes for _ in range(4)] + [_sds((8, D), F32), _sds((1, 1), F32)]
    outs = pl.pallas_call(body, name=name, out_shape=out_shape)(parts, *[a for p in params for a in p])
    return [list(outs[4 * i:4 * i + 4]) for i in range(np_)], outs[-2], outs[-1].reshape(())


def _exchange(name, arrays, kind, after):
    n = len(arrays)
    gather = kind == "gather"
    out_shape = [_sds((NDEV,) + a.shape if gather else a.shape, a.dtype) for a in arrays]

    def body(*refs):
        srcs, outs = refs[:n], refs[n + 1:2 * n + 1]
        send_sems, recv_sems, local_sems = refs[2 * n + 1:]
        x, y, c = lax.axis_index("x"), lax.axis_index("y"), lax.axis_index("c")
        me = 4 * x + 2 * y + c
        pending = []
        for t in range(n):
            own = pltpu.make_async_copy(srcs[t] if gather else srcs[t].at[me], outs[t].at[me], local_sems.at[t])
            own.start()
            pending.append(own)
            for rel in range(1, NDEV):
                px = 1 - x if rel & 4 else x
                py = 1 - y if rel & 2 else y
                pc = 1 - c if rel & 1 else c
                peer = 4 * px + 2 * py + pc
                send = pltpu.make_async_remote_copy(
                    src_ref=srcs[t] if gather else srcs[t].at[peer], dst_ref=outs[t].at[me],
                    send_sem=send_sems.at[t, rel - 1], recv_sem=recv_sems.at[t, rel - 1],
                    device_id=(px, py, pc), device_id_type=MESH)
                send.start()
                arrive = pltpu.make_async_remote_copy(
                    src_ref=srcs[t] if gather else srcs[t].at[me], dst_ref=outs[t].at[peer],
                    send_sem=send_sems.at[t, rel - 1], recv_sem=recv_sems.at[t, rel - 1],
                    device_id=(px, py, pc), device_id_type=MESH)
                pending.append((send, arrive))
        for item in pending:
            if isinstance(item, tuple):
                item[0].wait_send()
                item[1].wait_recv()
            else:
                item.wait()

    any_spec = pl.BlockSpec(memory_space=pl.ANY)
    outs = pl.pallas_call(
        body, name=name,
        in_specs=[any_spec] * (n + 1), out_specs=[any_spec] * n, out_shape=out_shape,
        scratch_shapes=[pltpu.SemaphoreType.DMA((n, NDEV - 1)), pltpu.SemaphoreType.DMA((n, NDEV - 1)),
                        pltpu.SemaphoreType.DMA((n,))],
    )(*arrays, after)
    return list(outs)


_HBM_SPEC = pl.BlockSpec(memory_space=pltpu.HBM)
_SEM_SPEC = pl.BlockSpec(memory_space=pltpu.SEMAPHORE)
_DATAFLOW = pltpu.SideEffectType.DATAFLOW_SIDE_EFFECTING


def _peers():
    x, y, c = lax.axis_index("x"), lax.axis_index("y"), lax.axis_index("c")
    out = []
    for rel in range(1, NDEV):
        px = 1 - x if rel & 4 else x
        py = 1 - y if rel & 2 else y
        pc = 1 - c if rel & 1 else c
        out.append((rel - 1, (px, py, pc), 4 * px + 2 * py + pc))
    return 4 * x + 2 * y + c, out


def _hbm(a):
    return pltpu.HBM(a.shape, a.dtype)


def _own_slot(a, me, kind):
    mine = a[None] if kind == "gather" else lax.dynamic_slice_in_dim(a, me, 1, axis=0)
    shape = (NDEV,) + mine.shape[1:]
    return lax.dynamic_update_slice_in_dim(lax.empty(shape, a.dtype), mine, me, axis=0)


def _exchange_start(name, arrays, me, kind):
    n = len(arrays)
    gather = kind == "gather"
    lands = [_own_slot(a, me, kind) for a in arrays]

    def body(*refs):
        src_refs, land_refs = refs[:n], refs[n:2 * n]
        send_sems, recv_sems = refs[2 * n], refs[2 * n + 1]
        token = refs[-1]
        my_block, peers = _peers()
        for t in range(n):
            for slot, dev, block in peers:
                pltpu.make_async_remote_copy(
                    src_ref=src_refs[t] if gather else src_refs[t].at[block], dst_ref=land_refs[t].at[my_block],
                    send_sem=send_sems.at[t * (NDEV - 1) + slot], recv_sem=recv_sems.at[t * (NDEV - 1) + slot],
                    device_id=dev, device_id_type=MESH).start()
        token[...] = jnp.zeros_like(token)

    operands = [pltpu.with_memory_space_constraint(a, pltpu.HBM) for a in list(arrays) + lands]
    outs = pl.pallas_call(
        body, name=name,
        out_shape=(pltpu.SemaphoreType.DMA((n * (NDEV - 1),)), pltpu.SemaphoreType.DMA((n * (NDEV - 1),)),
                   *[_hbm(a) for a in operands], _sds((8, 128), F32)),
        in_specs=[_HBM_SPEC] * (2 * n),
        out_specs=(_SEM_SPEC, _SEM_SPEC, *[_HBM_SPEC] * (2 * n), pl.BlockSpec(memory_space=pltpu.VMEM)),
        input_output_aliases={i: 2 + i for i in range(2 * n)},
        compiler_params=pltpu.CompilerParams(has_side_effects=_DATAFLOW),
    )(*operands)
    return (outs[0], outs[1], list(outs[2:2 + n]), list(outs[2 + n:2 + 2 * n])), outs[-1]


def _exchange_wait(name, started, t, after, kind):
    send_sems, recv_sems, srcs, lands = started
    gather = kind == "gather"

    def body(src_ref, land_ref, send_ref, recv_ref, after_ref, src_out, land_out):
        _, peers = _peers()
        for slot, dev, block in peers:
            copy = pltpu.make_async_remote_copy(
                src_ref=src_ref if gather else src_ref.at[block], dst_ref=land_ref.at[block],
                send_sem=send_ref.at[t * (NDEV - 1) + slot], recv_sem=recv_ref.at[t * (NDEV - 1) + slot],
                device_id=dev, device_id_type=MESH)
            copy.wait_send()
            copy.wait_recv()

    return pl.pallas_call(
        body, name=name, out_shape=(_hbm(srcs[t]), _hbm(lands[t])),
        in_specs=(_HBM_SPEC, _HBM_SPEC, _SEM_SPEC, _SEM_SPEC, pl.BlockSpec(memory_space=pl.ANY)),
        out_specs=(_HBM_SPEC, _HBM_SPEC), input_output_aliases={0: 0, 1: 1},
        compiler_params=pltpu.CompilerParams(has_side_effects=_DATAFLOW),
    )(srcs[t], lands[t], send_sems, recv_sems, after)[1]


DIRECT_RELS = (1, 2, 4, 6)
RELAY_RELS = (2, 4, 6)


def _rel_peer(rel):
    x, y, c = lax.axis_index("x"), lax.axis_index("y"), lax.axis_index("c")
    px = 1 - x if rel & 4 else x
    py = 1 - y if rel & 2 else y
    pc = 1 - c if rel & 1 else c
    return (px, py, pc), 4 * px + 2 * py + pc


def _gather_start(name, shards, me):
    n, nr = len(shards), len(DIRECT_RELS)
    lands = [_own_slot(a, me, "gather") for a in shards]

    def body(*refs):
        src_refs, land_refs = refs[:n], refs[n:2 * n]
        send_sems, recv_sems = refs[2 * n], refs[2 * n + 1]
        _, my_block = _rel_peer(0)
        for t in range(n):
            for s, rel in enumerate(DIRECT_RELS):
                dev, _ = _rel_peer(rel)
                pltpu.make_async_remote_copy(
                    src_ref=src_refs[t], dst_ref=land_refs[t].at[my_block],
                    send_sem=send_sems.at[t * nr + s], recv_sem=recv_sems.at[t * nr + s],
                    device_id=dev, device_id_type=MESH).start()

    operands = [pltpu.with_memory_space_constraint(a, pltpu.HBM) for a in list(shards) + lands]
    outs = pl.pallas_call(
        body, name=name,
        out_shape=(pltpu.SemaphoreType.DMA((n * nr,)), pltpu.SemaphoreType.DMA((n * nr,)), *[_hbm(a) for a in operands]),
        in_specs=[_HBM_SPEC] * (2 * n), out_specs=(_SEM_SPEC, _SEM_SPEC, *[_HBM_SPEC] * (2 * n)),
        input_output_aliases={i: 2 + i for i in range(2 * n)},
        compiler_params=pltpu.CompilerParams(has_side_effects=_DATAFLOW),
    )(*operands)
    return outs[0], outs[1], list(outs[2:2 + n]), list(outs[2 + n:2 + 2 * n])


def _gather_wait(name, started, ts, after):
    send_sems, recv_sems, srcs, lands = started
    m, nr = len(ts), len(DIRECT_RELS)

    def body(*refs):
        src_refs, land_refs = refs[:m], refs[m:2 * m]
        send_ref, recv_ref = refs[2 * m], refs[2 * m + 1]
        for i, t in enumerate(ts):
            for s, rel in enumerate(DIRECT_RELS):
                dev, block = _rel_peer(rel)
                copy = pltpu.make_async_remote_copy(
                    src_ref=src_refs[i], dst_ref=land_refs[i].at[block],
                    send_sem=send_ref.at[t * nr + s], recv_sem=recv_ref.at[t * nr + s],
                    device_id=dev, device_id_type=MESH)
                copy.wait_send()
                copy.wait_recv()

    operands = [srcs[t] for t in ts] + [lands[t] for t in ts]
    outs = pl.pallas_call(
        body, name=name, out_shape=tuple(_hbm(a) for a in operands),
        in_specs=[_HBM_SPEC] * (2 * m) + [_SEM_SPEC, _SEM_SPEC, pl.BlockSpec(memory_space=pl.ANY)],
        out_specs=tuple([_HBM_SPEC] * (2 * m)), input_output_aliases={i: i for i in range(2 * m)},
        compiler_params=pltpu.CompilerParams(has_side_effects=_DATAFLOW),
    )(*operands, send_sems, recv_sems, after)
    return list(outs[m:])


def _relay_start(name, lands):
    m, nr = len(lands), len(RELAY_RELS)

    def body(*refs):
        land_refs, send_sems, recv_sems = refs[:m], refs[m], refs[m + 1]
        sibling, _ = _rel_peer(1)
        for i in range(m):
            for s, rel in enumerate(RELAY_RELS):
                _, block = _rel_peer(rel)
                pltpu.make_async_remote_copy(
                    src_ref=land_refs[i].at[block], dst_ref=land_refs[i].at[block],
                    send_sem=send_sems.at[i * nr + s], recv_sem=recv_sems.at[i * nr + s],
                    device_id=sibling, device_id_type=MESH).start()

    outs = pl.pallas_call(
        body, name=name,
        out_shape=(pltpu.SemaphoreType.DMA((m * nr,)), pltpu.SemaphoreType.DMA((m * nr,)), *[_hbm(a) for a in lands]),
        in_specs=[_HBM_SPEC] * m, out_specs=(_SEM_SPEC, _SEM_SPEC, *[_HBM_SPEC] * m),
        input_output_aliases={i: 2 + i for i in range(m)},
        compiler_params=pltpu.CompilerParams(has_side_effects=_DATAFLOW),
    )(*lands)
    return outs[0], outs[1], list(outs[2:])


def _relay_wait(name, relayed, after):
    send_sems, recv_sems, lands = relayed
    m, nr = len(lands), len(RELAY_RELS)

    def body(*refs):
        land_refs, send_ref, recv_ref = refs[:m], refs[m], refs[m + 1]
        sibling, _ = _rel_peer(1)
        for i in range(m):
            for s, rel in enumerate(RELAY_RELS):
                _, sent = _rel_peer(rel)
                _, arriving = _rel_peer(rel ^ 1)
                copy = pltpu.make_async_remote_copy(
                    src_ref=land_refs[i].at[sent], dst_ref=land_refs[i].at[arriving],
                    send_sem=send_ref.at[i * nr + s], recv_sem=recv_ref.at[i * nr + s],
                    device_id=sibling, device_id_type=MESH)
                copy.wait_send()
                copy.wait_recv()

    outs = pl.pallas_call(
        body, name=name, out_shape=tuple(_hbm(a) for a in lands),
        in_specs=[_HBM_SPEC] * m + [_SEM_SPEC, _SEM_SPEC, pl.BlockSpec(memory_space=pl.ANY)],
        out_specs=tuple([_HBM_SPEC] * m), input_output_aliases={i: i for i in range(m)},
        compiler_params=pltpu.CompilerParams(has_side_effects=_DATAFLOW),
    )(*lands, send_sems, recv_sems, after)
    return list(outs)


def _ffn_fwd(tag, n, wg, wd):
    gu, act = _gate_up_act(f"ffn_gate_up_{tag}", n, wg)
    wd4 = wd.reshape(NFB, FB, D)
    f = _fwd_kblocked(f"ffn_down_{tag}", act, wd4)
    return (n, gu, act, wg, wd4), f


def _ffn_bwd(tag, dh_out, df, h_in, saved, g_pre, send, mixer):
    n, gu, act, wg, wd4 = saved
    dwd = _bwd_w_kblocked(f"ffn_down_dw_{tag}", act, df).reshape(NDEV, DFF // NDEV, D)
    dgu = _down_dx_act_bwd(f"ffn_down_dx_{tag}", df, wd4, gu).reshape(NDEV, S, FB)
    tok = send({f"down_{tag}": dwd, f"gate_up_{tag}": _bwd_w_cols_blocked(f"ffn_gate_up_dw_{tag}", n, dgu)})
    dn = _bwd_x_cols_blocked(f"ffn_gate_up_dx_{tag}", dgu, wg, after=tok)
    dh_in, (dg_pre,), dy, dg_mixer = _rms_bwd(f"ffn_prenorm_bwd_{tag}", h_in, [(g_pre, dn)], dh_out, F32, then=mixer)
    return dh_in, dg_pre, dy, dg_mixer


def kernel(x, positions, mix_norm_pre, mix_norm_post, ffn_norm_pre, ffn_norm_post, ffn_w_gate_up, ffn_w_down, conv_w_in, conv_w, conv_w_out, kv_norm, w_kv, w_q, w_o, loss_target, m_mix_norm_pre, m_mix_norm_post, m_ffn_norm_pre, m_ffn_norm_post, m_ffn_w_gate_up, m_ffn_w_down, m_conv_w_in, m_conv_w, m_conv_w_out, m_kv_norm, m_w_kv, m_w_q, m_w_o, v_mix_norm_pre, v_mix_norm_post, v_ffn_norm_pre, v_ffn_norm_post, v_ffn_w_gate_up, v_ffn_w_down, v_conv_w_in, v_conv_w, v_conv_w_out, v_kv_norm, v_w_kv, v_w_q, v_w_o):
    me = 4 * lax.axis_index("x") + 2 * lax.axis_index("y") + lax.axis_index("c")
    h0 = x.reshape(S, D)
    target = loss_target.reshape(S, D)
    row = lambda a, l: a[l].reshape(1, D)
    g_kv = kv_norm.reshape(1, D)

    cw_shard = jnp.pad(conv_w[0], ((0, 5), (0, 0)))
    names = ["conv_in", "conv_w", "conv_out", "gate_up_0", "down_0", "kv", "q", "o", "gate_up_1", "down_1"]
    shards = [conv_w_in[0], cw_shard, conv_w_out[0], ffn_w_gate_up[0], ffn_w_down[0],
              w_kv, w_q[0], w_o[0], ffn_w_gate_up[1], ffn_w_down[1]]
    shards = [s if n == "conv_w" else s.astype(BF16) for n, s in zip(names, shards)]
    first = 3
    gather_first = _gather_start("gather_start_conv", shards[:first], me)
    gather_rest = _gather_start("gather_start_rest", shards[first:], me)

    def direct(group, after):
        ts = [names.index(n) for n in group]
        started, ts = (gather_first, ts) if ts[0] < first else (gather_rest, [t - first for t in ts])
        lands = _gather_wait(f"gather_wait_{group[0]}", started, ts, after)
        return _relay_start(f"relay_start_{group[0]}", lands)

    def finish(group, relayed, after):
        return dict(zip(group, _relay_wait(f"relay_wait_{group[0]}", relayed, after)))

    sent = {}

    def send(grads):
        started, token = _exchange_start(f"scatter_start_{next(iter(grads))}", list(grads.values()), me, "scatter")
        for i, name in enumerate(grads):
            sent[name] = (started, i)
        return token

    groups = [["conv_in", "conv_w", "conv_out"], ["gate_up_0", "down_0"], ["kv", "q"], ["o", "gate_up_1", "down_1"]]
    n0 = _rms_fwd("mix_prenorm_0", h0, [row(mix_norm_pre, 0)])[0]
    half = HEAD_DIM // 2
    inv_freq = ROPE_THETA ** (-jnp.arange(half, dtype=F32) / half)
    tables = _rope_tables("rope_tables", positions.reshape(S, 1), jnp.tile(inv_freq, 4).reshape(1, 128))
    w = finish(groups[0], direct(groups[0], tables[0]), n0)
    win = w["conv_in"].transpose(1, 0, 2).reshape(D, 3 * D)
    cw = w["conv_w"].transpose(1, 0, 2).reshape(8, D)
    wout = w["conv_out"].reshape(D, D)
    z = _fwd_rows("conv_in", n0, win, BF16)
    pre = _conv_fwd("conv_gate", z, cw)
    relayed = direct(groups[1], pre)
    y0 = _fwd_rows("conv_out", pre, wout)
    h1, (n1,) = _resid_rms("mix_postnorm_0", h0, y0, row(mix_norm_post, 0), [row(ffn_norm_pre, 0)])
    w = finish(groups[1], relayed, n1)
    ffn0, f0 = _ffn_fwd("0", n1, w["gate_up_0"], w["down_0"])
    relayed = direct(groups[2], ffn0[2])
    h2, (nk, n2) = _resid_rms("ffn_postnorm_0", h1, f0, row(ffn_norm_post, 0), [g_kv, row(mix_norm_pre, 1)])

    w = finish(groups[2], relayed, nk)
    wkv = w["kv"].transpose(1, 0, 2).reshape(D, 2 * QW)
    wq = w["q"].transpose(1, 0, 2).reshape(D, QW)
    qc, kc, vc, o_c, lse_c = [], [], [], [], []
    for g, d in enumerate(DILATIONS):
        q_g, k_g, v_g = _qkv_classes(f"qkv_proj_{g}", n2, nk, wq, wkv, g, d, tables)
        qc.append(q_g)
        kc.append(k_g)
        vc.append(v_g)
    relayed = direct(groups[3], vc[-1])
    for g, d in enumerate(DILATIONS):
        o_g, lse_g = _attn_fwd(f"attn_fwd_{g}", qc[g], kc[g], vc[g], d)
        o_c.append(o_g)
        lse_c.append(lse_g)
    o_mix = _mix_fwd("attn_mix", o_c, lse_c)
    w = finish(groups[3], relayed, o_mix)
    wo = w["o"].reshape(D, D)
    y1 = _fwd_rows("attn_out", o_mix, wo)
    h3, (n3,) = _resid_rms("mix_postnorm_1", h2, y1, row(mix_norm_post, 1), [row(ffn_norm_pre, 1)])
    ffn1, f1 = _ffn_fwd("1", n3, w["gate_up_1"], w["down_1"])

    dh4, df1, dg_fpost1, sq = _resid_rms_loss("ffn_postnorm_1_loss", h3, f1, row(ffn_norm_post, 1), target)

    dh3, dg_fpre1, dy1, dg_mpost1 = _ffn_bwd(
        "1", dh4, df1, h3, ffn1, row(ffn_norm_pre, 1), send, (y1, row(mix_norm_post, 1)))
    dwo = _bwd_w_rows("attn_out_dw", o_mix, dy1).reshape(NDEV, D // NDEV, D)
    do = _bwd_x_rows("attn_out_dx", dy1, wo, BF16)
    lane = jnp.arange(128)
    ones_blockdiag = (lane[:, None] // HEAD_DIM == lane[None, :] // HEAD_DIM).astype(BF16)
    mixed = _mix_bwd("attn_mix_bwd", do, o_c, lse_c, ones_blockdiag)
    branch_grads = [_attn_bwd(f"attn_bwd_{g}", qc[g], kc[g], vc[g], mixed[g], lse_c[g], mixed[3 + g], d)
                    for g, d in enumerate(DILATIONS)]
    dq_raw, dkv = _attn_bwd_post("attn_bwd_post", branch_grads, *tables)
    tok = send({"o": dwo, "kv": _bwd_w_cols("kv_proj_dw", nk, dkv, 2 * QW // NDEV),
                "q": _bwd_w_cols("q_proj_dw", n2, dq_raw, QW // NDEV)})
    dnk = _bwd_x_plain("kv_proj_dx", dkv, wkv, after=tok)
    dn2 = _bwd_x_plain("q_proj_dx", dq_raw, wq)
    dh2, (dg_kv, dg_mpre1), df0, dg_fpost0 = _rms_bwd(
        "kv_and_mix_prenorm_bwd_1", h2, [(g_kv, dnk), (row(mix_norm_pre, 1), dn2)], dh3, F32,
        then=(f0, row(ffn_norm_post, 0)))

    dh1, dg_fpre0, dy0, dg_mpost0 = _ffn_bwd(
        "0", dh2, df0, h1, ffn0, row(ffn_norm_pre, 0), send, (y0, row(mix_norm_post, 0)))
    dwout = _bwd_w_rows("conv_out_dw", pre, dy0).reshape(NDEV, D // NDEV, D)
    dpre = _bwd_x_rows("conv_out_dx", dy0, wout, BF16)
    dz, dcw = _conv_bwd("conv_gate_bwd", z, dpre, cw)
    tok = send({"conv_out": dwout, "conv_in": _bwd_w_cols("conv_in_dw", n0, dz, 3 * D // NDEV)})
    dn0 = _bwd_x_plain("conv_in_dx", dz, win, after=tok)
    dh0, (dg_mpre0,) = _rms_bwd("mix_prenorm_bwd_0", h0, [(row(mix_norm_pre, 0), dn0)], dh1, F32)

    small = _pack_small("pack_small_grads", [dg_mpre0, dg_mpre1, dg_mpost0, dg_mpost1, dg_fpre0, dg_fpre1,
                                             dg_fpost0, dg_fpost1, dg_kv], dcw, sq)

    done = [small]

    def upd(tag, w, m, v):
        parts = _exchange_wait(f"scatter_wait_{tag}", *sent[tag], done[-1], "scatter")
        shape = w.shape
        flat = lambda a: a.reshape(parts.shape[1:])
        res = _adamw(f"adamw_{tag}", parts, flat(w), flat(m), flat(v))
        done.append(res[0])
        return [r.reshape(shape) for r in res]

    def upd_layer(tag, l, w, m, v, other):
        parts = _exchange_wait(f"scatter_wait_{tag}_{l}", *sent[f"{tag}_{l}"], done[-1], "scatter")
        res = _adamw(f"adamw_{tag}_{l}", parts, w, m, v, layer=l, other=other)
        done.append(res[0])
        return list(res)

    res = {}
    down_1 = upd_layer("down", 1, ffn_w_down, m_ffn_w_down, v_ffn_w_down, None)
    gate_up_t = [jnp.swapaxes(a, 1, 2) for a in (ffn_w_gate_up, m_ffn_w_gate_up, v_ffn_w_gate_up)]
    gate_up_1 = upd_layer("gate_up", 1, *gate_up_t, None)
    res["w_o"] = upd("o", w_o, m_w_o, v_w_o)
    res["w_q"] = upd("q", w_q, m_w_q, v_w_q)
    res["w_kv"] = upd("kv", w_kv, m_w_kv, v_w_kv)

    small_all = _exchange("gather_small_grads", [small], "gather", done[-1])[0]
    vec = lambda a: a.reshape(1, D)
    gain_res, taps, loss = _adamw_gains("adamw_gains", small_all, [
        (mix_norm_pre, m_mix_norm_pre, v_mix_norm_pre), (mix_norm_post, m_mix_norm_post, v_mix_norm_post),
        (ffn_norm_pre, m_ffn_norm_pre, v_ffn_norm_pre), (ffn_norm_post, m_ffn_norm_post, v_ffn_norm_post),
        (vec(kv_norm), vec(m_kv_norm), vec(v_kv_norm))])
    dcw_mine = lax.dynamic_slice(taps, (0, me * 128), (8, 128))
    pad8 = lambda a, fill: jnp.pad(a[0], ((0, 5), (0, 0)), constant_values=fill)
    cw_res = [r[0:3].reshape(1, 3, 128) for r in
              _adamw("adamw_conv_w", dcw_mine.reshape(1, 8, 128), cw_shard, pad8(m_conv_w, 0.0), pad8(v_conv_w, 1.0))]

    res.update({
        "mix_norm_pre": gain_res[0],
        "mix_norm_post": gain_res[1],
        "ffn_norm_pre": gain_res[2],
        "ffn_norm_post": gain_res[3],
        "kv_norm": [r.reshape(D) for r in gain_res[4]],
        "conv_w": cw_res,
    })
    done.append(small_all)
    res["ffn_w_down"] = upd_layer("down", 0, ffn_w_down, m_ffn_w_down, v_ffn_w_down, down_1)
    res["ffn_w_gate_up"] = [jnp.swapaxes(r, 1, 2) for r in upd_layer("gate_up", 0, *gate_up_t, gate_up_1)]
    res["conv_w_out"] = upd("conv_out", conv_w_out, m_conv_w_out, v_conv_w_out)
    res["conv_w_in"] = upd("conv_in", conv_w_in, m_conv_w_in, v_conv_w_in)
    order = ["mix_norm_pre", "mix_norm_post", "ffn_norm_pre", "ffn_norm_post", "ffn_w_gate_up", "ffn_w_down",
             "conv_w_in", "conv_w", "conv_w_out", "kv_norm", "w_kv", "w_q", "w_o"]
    out = [loss, dh0.reshape(1, S, D)]
    for i in range(4):
        out += [res[name][i] for name in order]
    return tuple(out)
```

```python
import jax
import jax.numpy as jnp
from jax import lax
from jax.experimental import pallas as pl
from jax.experimental.pallas import tpu as pltpu

F32 = jnp.float32
BF16 = jnp.bfloat16

S = 4096
D = 1024
NDEV = 8
HEAD_DIM = 64
QW = 3072
DFF = 2816
FB = 704
NFB = 4
BRANCHES = ((128, 1), (512, 4), (2048, 16))
BAND = 128
ROPE_THETA = 10000.0
RMS_EPS = 1e-6
NEG_INF = -1e30
ADAM_LR, ADAM_B1, ADAM_B2, ADAM_EPS, ADAM_WD, ADAM_STEP = 0.001, 0.9, 0.999, 1e-08, 0.01, 10

VMEM_LIMIT_BYTES = 52 * 1024 * 1024
ROW_TILE = 512
MESH = pl.DeviceIdType.MESH


def _cparams(ngrid):
    return pltpu.CompilerParams(dimension_semantics=("arbitrary",) * ngrid,
                                vmem_limit_bytes=VMEM_LIMIT_BYTES)


def _sds(shape, dtype):
    return jax.ShapeDtypeStruct(tuple(shape), dtype)


_DIMS = {"nn": (((1,), (0,)), ((), ())),
         "nt": (((1,), (1,)), ((), ())),
         "tn": (((0,), (0,)), ((), ()))}


def _matmul(name, a, b, *, mode, grid, a_blk, a_map, b_blk, b_map, o_shape, o_blk, o_map, out_dtype, after=None,
            out_groups=1):
    nk = grid[2]
    dims = _DIMS[mode]
    acc_shape = tuple(s for s in o_blk if s is not None)
    if out_groups > 1:
        acc_shape = (acc_shape[1], out_groups * acc_shape[2])
    extra = [] if after is None else [after]

    def store(o_ref, val):
        if out_groups == 1:
            o_ref[...] = val.astype(o_ref.dtype)
        else:
            n = o_ref.shape[-1]
            for grp in range(out_groups):
                o_ref[grp] = val[:, grp * n:(grp + 1) * n].astype(o_ref.dtype)

    def body(a_ref, b_ref, *rest):
        o_ref, scratch = rest[len(extra)], rest[len(extra) + 1:]
        part = lax.dot_general(a_ref[...], b_ref[...], dims, preferred_element_type=F32)
        if nk == 1:
            store(o_ref, part)
            return
        acc_ref = scratch[0]
        k = pl.program_id(2)

        @pl.when(k == 0)
        def _():
            acc_ref[...] = part

        @pl.when(k > 0)
        def _():
            acc_ref[...] += part

        @pl.when(k == nk - 1)
        def _():
            store(o_ref, acc_ref[...])

    return pl.pallas_call(
        body, name=name, grid=grid,
        in_specs=[pl.BlockSpec(a_blk, a_map), pl.BlockSpec(b_blk, b_map)] + [pl.BlockSpec(memory_space=pl.ANY)] * len(extra),
        out_specs=pl.BlockSpec(o_blk, o_map),
        out_shape=_sds(o_shape, out_dtype),
        scratch_shapes=[] if nk == 1 else [pltpu.VMEM(acc_shape, F32)],
        compiler_params=_cparams(3),
    )(a, b, *extra)


TM = 1024
TK = S


def _fwd_rows(name, a, w, out_dtype=F32):
    kdim, n = w.shape
    tn = 1024
    return _matmul(name, a, w, mode="nn", grid=(S // TM, n // tn, 1),
                   a_blk=(TM, kdim), a_map=lambda i, j, k: (i, 0),
                   b_blk=(kdim, tn), b_map=lambda i, j, k: (0, j),
                   o_shape=(S, n), o_blk=(TM, tn), o_map=lambda i, j, k: (i, j), out_dtype=out_dtype)


def _fwd_kblocked(name, a4, w4):
    nb, _, kb = a4.shape
    n = w4.shape[2]

    def body(a_ref, w_ref, o_ref):
        acc = _dot_nn(a_ref[0], w_ref[0])
        for j in range(1, nb):
            acc = acc + _dot_nn(a_ref[j], w_ref[j])
        o_ref[...] = acc

    return pl.pallas_call(
        body, name=name, grid=(S // TM,),
        in_specs=[pl.BlockSpec((nb, TM, kb), lambda i: (0, i, 0)), pl.BlockSpec((nb, kb, n), lambda i: (0, 0, 0))],
        out_specs=pl.BlockSpec((TM, n), lambda i: (i, 0)), out_shape=_sds((S, n), F32),
        compiler_params=_cparams(1),
    )(a4, w4)


def _bwd_x_cols_blocked(name, dy8, wg, after):
    _, kdim, n = wg.shape
    nk = NDEV // 2

    def body(a_ref, b_ref, after_ref, o_ref, acc_ref):
        k = pl.program_id(1)
        part = _dot_nt(a_ref[0], b_ref[0]) + _dot_nt(a_ref[1], b_ref[1])

        @pl.when(k == 0)
        def _():
            acc_ref[...] = part

        @pl.when(k > 0)
        def _():
            acc_ref[...] += part

        @pl.when(k == nk - 1)
        def _():
            o_ref[...] = acc_ref[...].astype(o_ref.dtype)

    return pl.pallas_call(
        body, name=name, grid=(S // FFN_TM, nk),
        in_specs=[pl.BlockSpec((2, None, FFN_TM, n), lambda i, k: (0, k, i, 0)),
                  pl.BlockSpec((2, None, kdim, n), lambda i, k: (0, k, 0, 0)),
                  pl.BlockSpec(memory_space=pl.ANY)],
        out_specs=pl.BlockSpec((FFN_TM, kdim), lambda i, k: (i, 0)), out_shape=_sds((S, kdim), BF16),
        scratch_shapes=[pltpu.VMEM((FFN_TM, kdim), F32)],
        compiler_params=_cparams(2),
    )(dy8.reshape(2, nk, S, n), wg.reshape(2, nk, kdim, n), after)


def _bwd_x_rows(name, dy, w, out_dtype, after=None):
    kdim, n = w.shape
    tkk = 512
    return _matmul(name, dy, w, mode="nt", grid=(S // TM, kdim // tkk, 1),
                   a_blk=(TM, n), a_map=lambda i, j, k: (i, 0),
                   b_blk=(tkk, n), b_map=lambda i, j, k: (j, 0),
                   o_shape=(S, kdim), o_blk=(TM, tkk), o_map=lambda i, j, k: (i, j), out_dtype=out_dtype, after=after)


DW_COLS = 768


def _bwd_w_cols(name, a, dy, n):
    kdim = a.shape[1]
    groups = DW_COLS // n
    return _matmul(name, a, dy, mode="tn", grid=(1, NDEV // groups, S // TK),
                   a_blk=(TK, kdim), a_map=lambda i, j, k: (k, 0),
                   b_blk=(TK, DW_COLS), b_map=lambda i, j, k: (k, j),
                   o_shape=(NDEV, kdim, n), o_blk=(groups, kdim, n) if groups > 1 else (None, kdim, n),
                   o_map=lambda i, j, k: (j, 0, 0), out_dtype=BF16, out_groups=groups)


def _bwd_x_plain(name, dy, w, after=None):
    kdim, n = w.shape
    tm = TM if n <= 3 * D else TM // 2
    return _matmul(name, dy, w, mode="nt", grid=(S // tm, 1, 1),
                   a_blk=(tm, n), a_map=lambda i, j, k: (i, 0),
                   b_blk=(kdim, n), b_map=lambda i, j, k: (0, 0),
                   o_shape=(S, kdim), o_blk=(tm, kdim), o_map=lambda i, j, k: (i, 0), out_dtype=BF16, after=after)


def _bwd_w_cols_blocked(name, a, dy8):
    kdim = a.shape[1]
    n = dy8.shape[2]
    return _matmul(name, dy8, a, mode="tn", grid=(1, NDEV, S // TK),
                   a_blk=(None, TK, n), a_map=lambda i, j, k: (j, k, 0),
                   b_blk=(TK, kdim), b_map=lambda i, j, k: (k, 0),
                   o_shape=(NDEV, n, kdim), o_blk=(None, n, kdim), o_map=lambda i, j, k: (j, 0, 0), out_dtype=BF16)


def _bwd_w_rows(name, a, dy):
    kdim = a.shape[1]
    n = dy.shape[1]
    tmm = 512
    return _matmul(name, a, dy, mode="tn", grid=(kdim // tmm, 1, S // TK),
                   a_blk=(TK, tmm), a_map=lambda i, j, k: (k, i),
                   b_blk=(TK, n), b_map=lambda i, j, k: (k, 0),
                   o_shape=(kdim, n), o_blk=(tmm, n), o_map=lambda i, j, k: (i, 0), out_dtype=BF16)


def _bwd_w_kblocked(name, a4, dy):
    nb, _, kb = a4.shape
    n = dy.shape[1]
    return _matmul(name, a4, dy, mode="tn", grid=(nb, 1, S // TK),
                   a_blk=(None, TK, kb), a_map=lambda i, j, k: (i, k, 0),
                   b_blk=(TK, n), b_map=lambda i, j, k: (k, 0),
                   o_shape=(nb, kb, n), o_blk=(None, kb, n), o_map=lambda i, j, k: (i, 0, 0), out_dtype=BF16)


def _rstd(x):
    return lax.rsqrt(jnp.mean(x * x, axis=-1, keepdims=True) + RMS_EPS)


def _row_spec(tm=ROW_TILE, width=D):
    return pl.BlockSpec((tm, width), lambda i: (i, 0))


def _vec_spec(rows=1, width=D):
    return pl.BlockSpec((rows, width), lambda i: (0, 0))


def _rms_fwd(name, x, gains):
    n = len(gains)

    def body(x_ref, *refs):
        x_val = x_ref[...]
        xh = x_val * _rstd(x_val)
        for g_ref, o_ref in zip(refs[:n], refs[n:]):
            o_ref[...] = (xh * g_ref[...]).astype(o_ref.dtype)

    outs = pl.pallas_call(
        body, name=name, grid=(S // ROW_TILE,),
        in_specs=[_row_spec()] + [_vec_spec()] * n,
        out_specs=[_row_spec()] * n,
        out_shape=[_sds((S, D), BF16)] * n,
        compiler_params=_cparams(1),
    )(x, *gains)
    return list(outs)


def _resid_rms(name, h, y, g, next_gains):
    n = len(next_gains)

    def body(h_ref, y_ref, g_ref, *refs):
        y_val = y_ref[...]
        h_new = h_ref[...] + (y_val * _rstd(y_val)) * g_ref[...]
        refs[n][...] = h_new
        hh = h_new * _rstd(h_new)
        for g2_ref, o_ref in zip(refs[:n], refs[n + 1:]):
            o_ref[...] = (hh * g2_ref[...]).astype(o_ref.dtype)

    outs = pl.pallas_call(
        body, name=name, grid=(S // ROW_TILE,),
        in_specs=[_row_spec(), _row_spec(), _vec_spec()] + [_vec_spec()] * n,
        out_specs=[_row_spec()] * (n + 1), out_shape=[_sds((S, D), F32)] + [_sds((S, D), BF16)] * n,
        compiler_params=_cparams(1),
    )(h, y, g, *next_gains)
    return outs[0], list(outs[1:])


def _resid_rms_loss(name, h, y, g, target):
    def body(h_ref, y_ref, g_ref, t_ref, dh_ref, dy_ref, dg_ref, part_ref):
        y_val = y_ref[...]
        gain = g_ref[...]
        e = h_ref[...] + (y_val * _rstd(y_val)) * gain - t_ref[...]
        dh = e * (1.0 / D)
        dh_ref[...] = dh
        step = pl.program_id(0)
        dy_ref[...] = _norm_bwd_rows(y_val, gain, dh, dg_ref, step).astype(dy_ref.dtype)
        part = jnp.sum(e * e, axis=0, keepdims=True)

        @pl.when(step == 0)
        def _():
            part_ref[...] = part

        @pl.when(step > 0)
        def _():
            part_ref[...] += part

    return pl.pallas_call(
        body, name=name, grid=(S // ROW_TILE,),
        in_specs=[_row_spec(), _row_spec(), _vec_spec(), _row_spec()],
        out_specs=[_row_spec(), _row_spec(), _vec_spec(8), _vec_spec()],
        out_shape=[_sds((S, D), F32), _sds((S, D), BF16), _sds((8, D), F32), _sds((1, D), F32)],
        compiler_params=_cparams(1),
    )(h, y, g, target)


def _norm_bwd_rows(x_val, g, dn, dg_ref, step):
    r = _rstd(x_val)
    xh = x_val * r
    dxh = dn * g
    part = jnp.sum(dn * xh, axis=0, keepdims=True)

    @pl.when(step == 0)
    def _():
        dg_ref[...] = jnp.zeros_like(dg_ref)

    dg_ref[0:1, :] += part
    return r * (dxh - xh * jnp.mean(dxh * xh, axis=-1, keepdims=True))


def _rms_bwd(name, x, pairs, dres, out_dtype, then=None):
    n = len(pairs)
    has_res = dres is not None
    chained = then is not None

    def body(x_ref, *refs):
        g_refs = refs[0:2 * n:2]
        dn_refs = refs[1:2 * n:2]
        pos = 2 * n
        res_ref = refs[pos] if has_res else None
        pos += int(has_res)
        if chained:
            y_ref, gy_ref = refs[pos], refs[pos + 1]
            pos += 2
        dx_ref = refs[pos]
        dg_refs = refs[pos + 1:pos + 1 + n]
        step = pl.program_id(0)
        x_val = x_ref[...]
        acc = res_ref[...] if has_res else jnp.zeros_like(x_val)
        for g_ref, dn_ref, dg_ref in zip(g_refs, dn_refs, dg_refs):
            acc = acc + _norm_bwd_rows(x_val, g_ref[...], dn_ref[...].astype(F32), dg_ref, step)
        dx_ref[...] = acc.astype(dx_ref.dtype)
        if chained:
            dy_ref, dgy_ref = refs[pos + 1 + n], refs[pos + 2 + n]
            dy_ref[...] = _norm_bwd_rows(y_ref[...], gy_ref[...], acc, dgy_ref, step).astype(dy_ref.dtype)

    operands = [x]
    in_specs = [_row_spec()]
    for g, dn in pairs:
        operands += [g, dn]
        in_specs += [_vec_spec(), _row_spec()]
    if has_res:
        operands.append(dres)
        in_specs.append(_row_spec())
    if chained:
        operands += [then[0], then[1]]
        in_specs += [_row_spec(), _vec_spec()]
    extra = int(chained)
    outs = pl.pallas_call(
        body, name=name, grid=(S // ROW_TILE,),
        in_specs=in_specs,
        out_specs=[_row_spec()] + [_vec_spec(8)] * n + [_row_spec(), _vec_spec(8)] * extra,
        out_shape=[_sds((S, D), out_dtype)] + [_sds((8, D), F32)] * n + [_sds((S, D), BF16), _sds((8, D), F32)] * extra,
        compiler_params=_cparams(1),
    )(*operands)
    if chained:
        return outs[0], list(outs[1:1 + n]), outs[1 + n], outs[2 + n]
    return outs[0], list(outs[1:])


def _shift_down(u, prev8, k):
    r = pltpu.roll(u, k, 0)
    p = pltpu.roll(prev8, k, 0)
    row = lax.broadcasted_iota(jnp.int32, prev8.shape, 0)
    top = jnp.where(row < k, p, r[0:8])
    return jnp.concatenate([top, r[8:]], axis=0)


def _shift_up(u, next8, k):
    tm = u.shape[0]
    r = pltpu.roll(u, tm - k, 0)
    p = pltpu.roll(next8, 8 - k, 0)
    row = lax.broadcasted_iota(jnp.int32, next8.shape, 0)
    bot = jnp.where(row >= 8 - k, p, r[tm - 8:tm])
    return jnp.concatenate([r[:tm - 8], bot], axis=0)


CONV_TILE = 512


def _halo_prev(col):
    return pl.BlockSpec((8, D), lambda i: (jnp.maximum(i * (CONV_TILE // 8) - 1, 0), col))


def _halo_next(col):
    last = S // 8 - 1
    return pl.BlockSpec((8, D), lambda i: (jnp.minimum((i + 1) * (CONV_TILE // 8), last), col))


def _conv_fwd(name, z, cw):
    def body(b_ref, c_ref, h_ref, cp_ref, hp_ref, cw_ref, o_ref):
        i = pl.program_id(0)
        u = c_ref[...].astype(F32) * h_ref[...].astype(F32)
        up = cp_ref[...].astype(F32) * hp_ref[...].astype(F32)
        up = jnp.where(i > 0, up, 0.0)
        cv = cw_ref[0:1, :] * _shift_down(u, up, 2) + cw_ref[1:2, :] * _shift_down(u, up, 1) + cw_ref[2:3, :] * u
        o_ref[...] = (b_ref[...].astype(F32) * cv).astype(o_ref.dtype)

    col = lambda c: pl.BlockSpec((CONV_TILE, D), lambda i: (i, c))
    return pl.pallas_call(
        body, name=name, grid=(S // CONV_TILE,),
        in_specs=[col(0), col(1), col(2), _halo_prev(1), _halo_prev(2), _vec_spec(8)],
        out_specs=_row_spec(CONV_TILE), out_shape=_sds((S, D), BF16),
        compiler_params=_cparams(1),
    )(z, z, z, z, z, cw)


def _conv_bwd(name, z, dpre, cw):
    nsteps = S // CONV_TILE

    def body(b_ref, c_ref, h_ref, cp_ref, hp_ref, dp_ref, dpn_ref, bn_ref, cw_ref, dz_ref, dcw_ref):
        i = pl.program_id(0)
        b = b_ref[...].astype(F32)
        c = c_ref[...].astype(F32)
        h = h_ref[...].astype(F32)
        dp = dp_ref[...].astype(F32)
        u = c * h
        up = jnp.where(i > 0, cp_ref[...].astype(F32) * hp_ref[...].astype(F32), 0.0)
        s1 = _shift_down(u, up, 1)
        s2 = _shift_down(u, up, 2)
        w0, w1, w2 = cw_ref[0:1, :], cw_ref[1:2, :], cw_ref[2:3, :]
        cv = w0 * s2 + w1 * s1 + w2 * u
        dcv = dp * b
        dcvn = jnp.where(i < nsteps - 1, dpn_ref[...].astype(F32) * bn_ref[...].astype(F32), 0.0)
        du = w2 * dcv + w1 * _shift_up(dcv, dcvn, 1) + w0 * _shift_up(dcv, dcvn, 2)
        dz_ref[:, 0:D] = (dp * cv).astype(dz_ref.dtype)
        dz_ref[:, D:2 * D] = (du * h).astype(dz_ref.dtype)
        dz_ref[:, 2 * D:3 * D] = (du * c).astype(dz_ref.dtype)

        @pl.when(i == 0)
        def _():
            dcw_ref[...] = jnp.zeros_like(dcw_ref)

        dcw_ref[0:1, :] += jnp.sum(dcv * s2, axis=0, keepdims=True)
        dcw_ref[1:2, :] += jnp.sum(dcv * s1, axis=0, keepdims=True)
        dcw_ref[2:3, :] += jnp.sum(dcv * u, axis=0, keepdims=True)

    col = lambda c: pl.BlockSpec((CONV_TILE, D), lambda i: (i, c))
    return pl.pallas_call(
        body, name=name, grid=(nsteps,),
        in_specs=[col(0), col(1), col(2), _halo_prev(1), _halo_prev(2),
                  _row_spec(CONV_TILE), _halo_next(0), _halo_next(0), _vec_spec(8)],
        out_specs=[pl.BlockSpec((CONV_TILE, 3 * D), lambda i: (i, 0)), _vec_spec(8)],
        out_shape=[_sds((S, 3 * D), BF16), _sds((8, D), F32)],
        compiler_params=_cparams(1),
    )(z, z, z, z, z, dpre, dpre, z, cw)


FFN_TM = 2048
_GU_BLOCK = pl.BlockSpec((2, None, FFN_TM, FB), lambda i, j: (0, j, i, 0))


def _gate_up_act(name, a, wg):
    kdim = a.shape[1]

    def body(a_ref, wgate_ref, wup_ref, gu_ref, act_ref):
        x = a_ref[...]
        g = _dot_nn(x, wgate_ref[...])
        u = _dot_nn(x, wup_ref[...])
        gu_ref[0] = g.astype(gu_ref.dtype)
        gu_ref[1] = u.astype(gu_ref.dtype)
        act_ref[...] = (g * jax.nn.sigmoid(g) * u).astype(act_ref.dtype)

    return pl.pallas_call(
        body, name=name, grid=(S // FFN_TM, NFB),
        in_specs=[pl.BlockSpec((FFN_TM, kdim), lambda i, j: (i, 0)),
                  pl.BlockSpec((None, kdim, FB), lambda i, j: (j, 0, 0)),
                  pl.BlockSpec((None, kdim, FB), lambda i, j: (j + NFB, 0, 0))],
        out_specs=[_GU_BLOCK, pl.BlockSpec((None, FFN_TM, FB), lambda i, j: (j, i, 0))],
        out_shape=[_sds((2, NFB, S, FB), BF16), _sds((NFB, S, FB), BF16)],
        compiler_params=_cparams(2),
    )(a, wg, wg)


def _down_dx_act_bwd(name, df, w4, gu):
    _, kb, n = w4.shape

    def body(df_ref, w_ref, gu_ref, o_ref):
        d = _dot_nt(df_ref[...], w_ref[...])
        g = gu_ref[0].astype(F32)
        u = gu_ref[1].astype(F32)
        sg = jax.nn.sigmoid(g)
        o_ref[0] = (d * u * sg * (1.0 + g * (1.0 - sg))).astype(o_ref.dtype)
        o_ref[1] = (d * g * sg).astype(o_ref.dtype)

    return pl.pallas_call(
        body, name=name, grid=(S // FFN_TM, NFB),
        in_specs=[pl.BlockSpec((FFN_TM, n), lambda i, j: (i, 0)), pl.BlockSpec((None, kb, n), lambda i, j: (j, 0, 0)),
                  _GU_BLOCK],
        out_specs=_GU_BLOCK, out_shape=_sds((2, NFB, S, FB), BF16),
        compiler_params=_cparams(2),
    )(df, w4, gu)


def _rope_tables(name, pos_col, inv_freq_row):
    def body(pos_ref, f_ref, cos_ref, sin_ref):
        ang = pos_ref[...].astype(F32) * f_ref[...]
        lane = lax.broadcasted_iota(jnp.int32, ang.shape, 1)
        s = jnp.sin(ang)
        cos_ref[...] = jnp.cos(ang)
        sin_ref[...] = jnp.where((lane % HEAD_DIM) < HEAD_DIM // 2, -s, s)

    tab = pl.BlockSpec((ROW_TILE, 128), lambda i: (i, 0))
    return pl.pallas_call(
        body, name=name, grid=(S // ROW_TILE,),
        in_specs=[pl.BlockSpec((ROW_TILE, 1), lambda i: (i, 0)), _vec_spec(1, 128)],
        out_specs=[tab, tab], out_shape=[_sds((S, 128), F32)] * 2,
        compiler_params=_cparams(1),
    )(pos_col, inv_freq_row)


def _swap_halves(t):
    lane = lax.broadcasted_iota(jnp.int32, t.shape, 1)
    first = (lane % HEAD_DIM) < HEAD_DIM // 2
    return jnp.where(first, pltpu.roll(t, 128 - HEAD_DIM // 2, 1), pltpu.roll(t, HEAD_DIM // 2, 1))


NCHUNK = D // 128


def _chunk(c, base=0):
    return slice(base + c * 128, base + (c + 1) * 128)


def _class_rows(r, d, tm):
    return pl.ds(r, tm // d, stride=d) if d > 1 else slice(None)


def _class_block(d, tm):
    return pl.BlockSpec((tm // d, d * D), lambda i: (i, 0))


def _tokens_from_classes(blk_ref, tmp_ref, d, tm):
    for r in range(d):
        for c in range(NCHUNK):
            tmp_ref[c, _class_rows(r, d, tm), :] = blk_ref[:, _chunk(c, r * D)].astype(F32)


def _classes_from_tokens(tmp_ref, blk_ref, d, tm):
    for r in range(d):
        for c in range(NCHUNK):
            blk_ref[:, _chunk(c, r * D)] = tmp_ref[c, _class_rows(r, d, tm), :].astype(blk_ref.dtype)


def _qkv_classes(name, n2, nk, wq, wkv, g, d, tables):
    def emit(acc, cos_ref, sin_ref, o_ref, tmp_ref, scale):
        for c in range(NCHUNK):
            tmp_ref[c] = acc[:, _chunk(c)]
        for r in range(d):
            rows = _class_rows(r, d, TM)
            if scale is not None:
                cs = cos_ref[rows, :]
                sn = sin_ref[rows, :]
            for c in range(NCHUNK):
                x = tmp_ref[c, rows, :]
                if scale is not None:
                    x = (x * cs + _swap_halves(x) * sn) * scale
                o_ref[:, _chunk(c, r * D)] = x.astype(o_ref.dtype)

    def body(n2_ref, nk_ref, wq_ref, wk_ref, wv_ref, cos_ref, sin_ref, q_ref, k_ref, v_ref, tmp_ref):
        emit(_dot_nn(n2_ref[...], wq_ref[...]), cos_ref, sin_ref, q_ref, tmp_ref, HEAD_DIM ** -0.5)
        x = nk_ref[...]
        emit(_dot_nn(x, wk_ref[...]), cos_ref, sin_ref, k_ref, tmp_ref, 1.0)
        emit(_dot_nn(x, wv_ref[...]), cos_ref, sin_ref, v_ref, tmp_ref, None)

    nbr = len(DILATIONS)
    act = pl.BlockSpec((TM, D), lambda i: (i, 0))
    tab = pl.BlockSpec((TM, 128), lambda i: (i, 0))
    wcol = lambda col: pl.BlockSpec((D, D), lambda i: (0, col))
    return pl.pallas_call(
        body, name=name, grid=(S // TM,),
        in_specs=[act, act, wcol(g), wcol(g), wcol(nbr + g), tab, tab],
        out_specs=[_class_block(d, TM)] * 3, out_shape=[_sds((S // d, d * D), BF16)] * 3,
        scratch_shapes=[pltpu.VMEM((NCHUNK, TM, 128), F32)],
        compiler_params=_cparams(1),
    )(n2, nk, wq, wkv, wkv, *tables)


ATTN_CHAINS = 16


def _attn_units(d):
    nblk = S // d // BAND
    return max(1, 2 * ATTN_CHAINS // nblk)


def _class_spec(d):
    return pl.BlockSpec((S // d, 128 * _attn_units(d)), lambda cb: (0, cb))


def _dot_nt(a, b):
    return lax.dot_general(a, b, _DIMS["nt"], preferred_element_type=F32)


def _dot_tn(a, b):
    return lax.dot_general(a, b, _DIMS["tn"], preferred_element_type=F32)


def _dot_nn(a, b):
    return lax.dot_general(a, b, _DIMS["nn"], preferred_element_type=F32)


def _band_mask(nkeys):
    qi = lax.broadcasted_iota(jnp.int32, (2 * BAND, nkeys), 0) % BAND
    kj = lax.broadcasted_iota(jnp.int32, (2 * BAND, nkeys), 1)
    if nkeys == BAND:
        return kj <= qi
    dist = qi + BAND - kj
    return (dist >= 0) & (dist <= BAND)


def _band_bias():
    return {n: jnp.where(_band_mask(n), 0.0, NEG_INF).astype(F32) for n in (BAND, 2 * BAND)}


def _stack_heads(x):
    row = lax.broadcasted_iota(jnp.int32, (2 * BAND, 128), 0)
    lane = lax.broadcasted_iota(jnp.int32, (2 * BAND, 128), 1)
    keep = (row < BAND) == (lane < HEAD_DIM)
    return jnp.where(keep, jnp.concatenate([x, x], axis=0), jnp.zeros((), x.dtype))


def _unstack(x2):
    first_head = lax.broadcasted_iota(jnp.int32, (BAND, 128), 1) < HEAD_DIM
    return jnp.where(first_head, x2[:BAND], x2[BAND:])


def _for_later_blocks(nblk, units, fn):
    all_lanes = [slice(u * 128, (u + 1) * 128) for u in range(units)]
    unroll = max(1, ATTN_CHAINS // units)
    trips = (nblk - 1) // unroll
    if trips > 1:
        def step(i, carry):
            for j in range(unroll):
                for lanes in all_lanes:
                    fn(pl.multiple_of((1 + i * unroll + j) * BAND, BAND), lanes)
            return carry

        lax.fori_loop(0, trips, step, 0)
    else:
        trips = 0
    for sb in range(1 + trips * unroll, nblk):
        for lanes in all_lanes:
            fn(sb * BAND, lanes)


def _attn_fwd(name, q, k, v, d):
    nblk = S // d // BAND
    units = _attn_units(d)

    def body(q_ref, k_ref, v_ref, o_ref, lse_ref):
        bias = _band_bias()

        def block(r0, k0, nkeys, lanes):
            q2 = _stack_heads(q_ref[pl.ds(r0, BAND), lanes])
            s = _dot_nt(q2, k_ref[pl.ds(k0, nkeys), lanes]) + bias[nkeys]
            m = jnp.max(s, axis=-1, keepdims=True)
            p = jnp.exp(s - m)
            l = jnp.sum(p, axis=-1, keepdims=True)
            o2 = _dot_nn(p.astype(BF16), v_ref[pl.ds(k0, nkeys), lanes])
            l_tile = _unstack(jnp.broadcast_to(l, (2 * BAND, 128)))
            m_tile = _unstack(jnp.broadcast_to(m, (2 * BAND, 128)))
            o_ref[pl.ds(r0, BAND), lanes] = (_unstack(o2) / l_tile).astype(o_ref.dtype)
            lse_ref[pl.ds(r0, BAND), lanes] = m_tile + jnp.log(l_tile)

        for u in range(units):
            block(0, 0, BAND, slice(u * 128, (u + 1) * 128))

        _for_later_blocks(nblk, units, lambda r0, lanes: block(r0, r0 - BAND, 2 * BAND, lanes))

    spec = _class_spec(d)
    return pl.pallas_call(
        body, name=name, grid=(8 * d // units,),
        in_specs=[spec] * 3, out_specs=[spec] * 2,
        out_shape=[_sds((S // d, d * D), BF16), _sds((S // d, d * D), F32)],
        compiler_params=_cparams(1),
    )(q, k, v)


def _attn_bwd(name, q, k, v, do, lse, dd, d):
    nblk = S // d // BAND
    units = _attn_units(d)

    def body(q_ref, k_ref, v_ref, do_ref, lse_ref, dd_ref, dq_ref, dk_out, dv_out, dk_ref, dv_ref):
        bias = _band_bias()
        def column(ref, r0, lanes, nkeys):
            tile = ref[pl.ds(r0, BAND), lanes]
            other = pltpu.roll(tile, HEAD_DIM, 1)
            first_head = lax.broadcasted_iota(jnp.int32, tile.shape, 1) < HEAD_DIM
            both = jnp.concatenate([jnp.where(first_head, tile, other), jnp.where(first_head, other, tile)], axis=0)
            return both if nkeys == BAND else jnp.concatenate([both, both], axis=1)

        def block(r0, k0, nkeys, lanes, first):
            q2 = _stack_heads(q_ref[pl.ds(r0, BAND), lanes])
            do2 = _stack_heads(do_ref[pl.ds(r0, BAND), lanes])
            kk = k_ref[pl.ds(k0, nkeys), lanes]
            vv = v_ref[pl.ds(k0, nkeys), lanes]
            s = _dot_nt(q2, kk) + bias[nkeys]
            p = jnp.exp(s - column(lse_ref, r0, lanes, nkeys))
            ds = (p * (_dot_nt(do2, vv) - column(dd_ref, r0, lanes, nkeys))).astype(BF16)
            dq_ref[pl.ds(r0, BAND), lanes] = _unstack(_dot_nn(ds, kk)).astype(dq_ref.dtype)
            dk_part = _dot_tn(ds, q2)
            dv_part = _dot_tn(p.astype(BF16), do2)
            if first:
                dk_ref[pl.ds(k0, nkeys), lanes] = dk_part
                dv_ref[pl.ds(k0, nkeys), lanes] = dv_part
            else:
                dk_ref[pl.ds(k0, BAND), lanes] += dk_part[:BAND]
                dv_ref[pl.ds(k0, BAND), lanes] += dv_part[:BAND]
                dk_ref[pl.ds(k0 + BAND, BAND), lanes] = dk_part[BAND:]
                dv_ref[pl.ds(k0 + BAND, BAND), lanes] = dv_part[BAND:]

        for u in range(units):
            block(0, 0, BAND, slice(u * 128, (u + 1) * 128), True)

        _for_later_blocks(nblk, units, lambda r0, lanes: block(r0, r0 - BAND, 2 * BAND, lanes, False))
        dk_out[...] = dk_ref[...].astype(dk_out.dtype)
        dv_out[...] = dv_ref[...].astype(dv_out.dtype)

    spec = _class_spec(d)
    return pl.pallas_call(
        body, name=name, grid=(8 * d // units,),
        in_specs=[spec] * 6, out_specs=[spec] * 3,
        out_shape=[_sds((S // d, d * D), BF16)] * 3,
        scratch_shapes=[pltpu.VMEM((S // d, 128 * units), F32)] * 2,
        compiler_params=_cparams(1),
    )(q, k, v, do, lse, dd)


MIX_TILE = 256
DILATIONS = tuple(d for _, d in BRANCHES)


def _branch_weights(la, lb, lc):
    m = jnp.maximum(jnp.maximum(la, lb), lc)
    ea, eb, ec = jnp.exp(la - m), jnp.exp(lb - m), jnp.exp(lc - m)
    inv = 1.0 / (ea + eb + ec)
    return ea * inv, eb * inv, ec * inv


def _mix_operands(outs, lses):
    specs = [_class_block(d, MIX_TILE) for d in DILATIONS] * 2
    scratch = [pltpu.VMEM((NCHUNK, MIX_TILE, 128), F32)] * 4
    return list(outs) + list(lses), specs, scratch


def _mix_fwd(name, outs, lses):
    def body(o0, o1, o2, l0, l1, l2, o_ref, to1, to2, tl1, tl2):
        for blk, tmp, d in ((o1, to1, DILATIONS[1]), (o2, to2, DILATIONS[2]), (l1, tl1, DILATIONS[1]), (l2, tl2, DILATIONS[2])):
            _tokens_from_classes(blk, tmp, d, MIX_TILE)
        for c in range(NCHUNK):
            wa, wb, wc = _branch_weights(l0[:, _chunk(c)], tl1[c], tl2[c])
            o_ref[:, _chunk(c)] = (wa * o0[:, _chunk(c)].astype(F32) + wb * to1[c] + wc * to2[c]).astype(o_ref.dtype)

    operands, specs, scratch = _mix_operands(outs, lses)
    return pl.pallas_call(
        body, name=name, grid=(S // MIX_TILE,),
        in_specs=specs, out_specs=_row_spec(MIX_TILE), out_shape=_sds((S, D), BF16),
        scratch_shapes=scratch, compiler_params=_cparams(1),
    )(*operands)


def _head_sum(x, ones_blockdiag):
    hi = x.astype(BF16)
    lo = (x - hi.astype(F32)).astype(BF16)
    return _dot_nn(hi, ones_blockdiag) + _dot_nn(lo, ones_blockdiag)


def _mix_bwd(name, do, outs, lses, ones_blockdiag):
    def body(do_ref, o0, o1, o2, l0, l1, l2, ones_ref, d0, d1, d2, t0, t1, t2,
             to1, to2, tl1, tl2, td1, td2, tt1, tt2):
        for blk, tmp, d in ((o1, to1, DILATIONS[1]), (o2, to2, DILATIONS[2]), (l1, tl1, DILATIONS[1]), (l2, tl2, DILATIONS[2])):
            _tokens_from_classes(blk, tmp, d, MIX_TILE)
        ones = ones_ref[...]
        for c in range(NCHUNK):
            w = _branch_weights(l0[:, _chunk(c)], tl1[c], tl2[c])
            dov = do_ref[:, _chunk(c)]
            o = w[0] * o0[:, _chunk(c)].astype(F32) + w[1] * to1[c] + w[2] * to2[c]
            t = _head_sum(dov * o, ones)
            d0[:, _chunk(c)] = (w[0] * dov).astype(d0.dtype)
            t0[:, _chunk(c)] = w[0] * t
            td1[c], tt1[c] = w[1] * dov, w[1] * t
            td2[c], tt2[c] = w[2] * dov, w[2] * t
        for tmp, blk, d in ((td1, d1, DILATIONS[1]), (tt1, t1, DILATIONS[1]), (td2, d2, DILATIONS[2]), (tt2, t2, DILATIONS[2])):
            _classes_from_tokens(tmp, blk, d, MIX_TILE)

    operands, specs, scratch = _mix_operands(outs, lses)
    out_specs = [_class_block(d, MIX_TILE) for d in DILATIONS] * 2
    out_shape = [_sds((S // d, d * D), BF16) for d in DILATIONS] + [_sds((S // d, d * D), F32) for d in DILATIONS]
    return pl.pallas_call(
        body, name=name, grid=(S // MIX_TILE,),
        in_specs=[_row_spec(MIX_TILE)] + specs + [_vec_spec(128, 128)],
        out_specs=out_specs, out_shape=out_shape,
        scratch_shapes=scratch + [pltpu.VMEM((NCHUNK, MIX_TILE, 128), F32)] * 4,
        compiler_params=_cparams(1),
    )(do, *operands, ones_blockdiag)


def _attn_bwd_post(name, grads, cos_t, sin_t):
    tm = MIX_TILE
    scale = HEAD_DIM ** -0.5

    def unrope(x, cs, sn):
        return x * cs - _swap_halves(x) * sn

    def body(*refs):
        in_refs = refs[:9]
        cos_ref, sin_ref, dq_ref, dkv_ref, tmp_ref = refs[9:]
        cs = cos_ref[...]
        sn = sin_ref[...]
        for g, d in enumerate(DILATIONS):
            for which, blk in enumerate(in_refs[3 * g:3 * g + 3]):
                if d > 1:
                    _tokens_from_classes(blk, tmp_ref, d, tm)
                for c in range(NCHUNK):
                    x = tmp_ref[c] if d > 1 else blk[:, _chunk(c)].astype(F32)
                    if which == 0:
                        dq_ref[:, _chunk(c, g * D)] = (unrope(x, cs, sn) * scale).astype(dq_ref.dtype)
                    elif which == 1:
                        dkv_ref[:, _chunk(c, g * D)] = unrope(x, cs, sn).astype(dkv_ref.dtype)
                    else:
                        dkv_ref[:, _chunk(c, QW + g * D)] = x.astype(dkv_ref.dtype)

    operands = [a for branch in grads for a in branch]
    tab = pl.BlockSpec((tm, 128), lambda i: (i, 0))
    return pl.pallas_call(
        body, name=name, grid=(S // tm,),
        in_specs=[_class_block(d, tm) for d in DILATIONS for _ in range(3)] + [tab, tab],
        out_specs=[pl.BlockSpec((tm, QW), lambda i: (i, 0)), pl.BlockSpec((tm, 2 * QW), lambda i: (i, 0))],
        out_shape=[_sds((S, QW), BF16), _sds((S, 2 * QW), BF16)],
        scratch_shapes=[pltpu.VMEM((NCHUNK, tm, 128), F32)],
        compiler_params=_cparams(1),
    )(*operands, cos_t, sin_t)


def _adamw(name, parts, w, m, v, layer=None, other=None):
    n, rows, cols = parts.shape
    tr = rows
    for cand in (256, 176, 128, 64, 32, 16, 8):
        if rows % cand == 0:
            tr = cand
            break
    n_other = 0 if other is None else len(other)

    def body(p_ref, w_ref, m_ref, v_ref, *refs):
        g_ref, d_ref, nm_ref, nv_ref = refs[n_other:]
        g = p_ref[0].astype(F32)
        for j in range(1, n):
            g = g + p_ref[j].astype(F32)
        g_ref[...] = g
        d_ref[...], nm_ref[...], nv_ref[...] = _adam_update(g, w_ref[...], m_ref[...], v_ref[...])

    if layer is None:
        blk = pl.BlockSpec((tr, cols), lambda i: (i, 0))
        shape = (rows, cols)
    else:
        blk = pl.BlockSpec((None, tr, cols), lambda i: (layer, i, 0))
        shape = w.shape
    return pl.pallas_call(
        body, name=name, grid=(rows // tr,),
        in_specs=[pl.BlockSpec((n, tr, cols), lambda i: (0, i, 0)), blk, blk, blk]
                 + [pl.BlockSpec(memory_space=pl.ANY)] * n_other,
        out_specs=[blk] * 4, out_shape=[_sds(shape, F32)] * 4,
        input_output_aliases={4 + i: i for i in range(n_other)},
        compiler_params=_cparams(1),
    )(parts, w, m, v, *(other or ()))


def _adam_update(g, w, m, v):
    c1 = 1.0 / (1.0 - ADAM_B1 ** ADAM_STEP)
    c2 = 1.0 / (1.0 - ADAM_B2 ** ADAM_STEP)
    nm = ADAM_B1 * m + (1.0 - ADAM_B1) * g
    nv = ADAM_B2 * v + (1.0 - ADAM_B2) * (g * g)
    return -ADAM_LR * ((nm * c1) / (jnp.sqrt(nv * c2) + ADAM_EPS) + ADAM_WD * w), nm, nv


GAIN_ROWS = 16


def _pack_small(name, gain_tiles, taps, sq):
    ng = len(gain_tiles)

    def body(*refs):
        o_ref = refs[-1]
        o_ref[...] = jnp.zeros_like(o_ref)
        for i in range(ng):
            o_ref[i:i + 1, :] = refs[i][0:1, :]
        o_ref[ng:ng + 3, :] = refs[ng][0:3, :]
        o_ref[ng + 3:ng + 4, :] = refs[ng + 1][...]

    return pl.pallas_call(body, name=name, out_shape=_sds((GAIN_ROWS, D), F32))(*gain_tiles, taps, sq)


def _adamw_gains(name, parts, params):
    np_ = len(params)
    shapes = [w.shape for w, _, _ in params]

    def body(p_ref, *refs):
        ins, outs = refs[:3 * np_], refs[3 * np_:]

        def total(lo, rows):
            g = p_ref[0, lo:lo + rows, :]
            for j in range(1, NDEV):
                g = g + p_ref[j, lo:lo + rows, :]
            return g

        lo = 0
        for i, shape in enumerate(shapes):
            g = total(lo, shape[0])
            lo += shape[0]
            w_ref, m_ref, v_ref = ins[3 * i:3 * i + 3]
            g_ref, d_ref, nm_ref, nv_ref = outs[4 * i:4 * i + 4]
            g_ref[...] = g
            d_ref[...], nm_ref[...], nv_ref[...] = _adam_update(g, w_ref[...], m_ref[...], v_ref[...])
        taps_ref, loss_ref = outs[-2], outs[-1]
        taps_ref[...] = jnp.zeros_like(taps_ref)
        taps_ref[0:3, :] = total(lo, 3)
        loss_ref[...] = jnp.sum(total(lo + 3, 1), axis=-1, keepdims=True) * (0.5 / D)

    out_shape = [_sds(shape, F32) for shape in shapes for _ in range(4)] + [_sds((8, D), F32), _sds((1, 1), F32)]
    outs = pl.pallas_call(body, name=name, out_shape=out_shape)(parts, *[a for p in params for a in p])
    return [list(outs[4 * i:4 * i + 4]) for i in range(np_)], outs[-2], outs[-1].reshape(())


def _exchange(name, arrays, kind, after):
    n = len(arrays)
    gather = kind == "gather"
    out_shape = [_sds((NDEV,) + a.shape if gather else a.shape, a.dtype) for a in arrays]

    def body(*refs):
        srcs, outs = refs[:n], refs[n + 1:2 * n + 1]
        send_sems, recv_sems, local_sems = refs[2 * n + 1:]
        x, y, c = lax.axis_index("x"), lax.axis_index("y"), lax.axis_index("c")
        me = 4 * x + 2 * y + c
        pending = []
        for t in range(n):
            own = pltpu.make_async_copy(srcs[t] if gather else srcs[t].at[me], outs[t].at[me], local_sems.at[t])
            own.start()
            pending.append(own)
            for rel in range(1, NDEV):
                px = 1 - x if rel & 4 else x
                py = 1 - y if rel & 2 else y
                pc = 1 - c if rel & 1 else c
                peer = 4 * px + 2 * py + pc
                send = pltpu.make_async_remote_copy(
                    src_ref=srcs[t] if gather else srcs[t].at[peer], dst_ref=outs[t].at[me],
                    send_sem=send_sems.at[t, rel - 1], recv_sem=recv_sems.at[t, rel - 1],
                    device_id=(px, py, pc), device_id_type=MESH)
                send.start()
                arrive = pltpu.make_async_remote_copy(
                    src_ref=srcs[t] if gather else srcs[t].at[me], dst_ref=outs[t].at[peer],
                    send_sem=send_sems.at[t, rel - 1], recv_sem=recv_sems.at[t, rel - 1],
                    device_id=(px, py, pc), device_id_type=MESH)
                pending.append((send, arrive))
        for item in pending:
            if isinstance(item, tuple):
                item[0].wait_send()
                item[1].wait_recv()
            else:
                item.wait()

    any_spec = pl.BlockSpec(memory_space=pl.ANY)
    outs = pl.pallas_call(
        body, name=name,
        in_specs=[any_spec] * (n + 1), out_specs=[any_spec] * n, out_shape=out_shape,
        scratch_shapes=[pltpu.SemaphoreType.DMA((n, NDEV - 1)), pltpu.SemaphoreType.DMA((n, NDEV - 1)),
                        pltpu.SemaphoreType.DMA((n,))],
    )(*arrays, after)
    return list(outs)


_HBM_SPEC = pl.BlockSpec(memory_space=pltpu.HBM)
_SEM_SPEC = pl.BlockSpec(memory_space=pltpu.SEMAPHORE)
_DATAFLOW = pltpu.SideEffectType.DATAFLOW_SIDE_EFFECTING


def _peers():
    x, y, c = lax.axis_index("x"), lax.axis_index("y"), lax.axis_index("c")
    out = []
    for rel in range(1, NDEV):
        px = 1 - x if rel & 4 else x
        py = 1 - y if rel & 2 else y
        pc = 1 - c if rel & 1 else c
        out.append((rel - 1, (px, py, pc), 4 * px + 2 * py + pc))
    return 4 * x + 2 * y + c, out


def _hbm(a):
    return pltpu.HBM(a.shape, a.dtype)


def _own_slot(a, me, kind):
    mine = a[None] if kind == "gather" else lax.dynamic_slice_in_dim(a, me, 1, axis=0)
    shape = (NDEV,) + mine.shape[1:]
    return lax.dynamic_update_slice_in_dim(lax.empty(shape, a.dtype), mine, me, axis=0)


def _exchange_start(name, arrays, me, kind):
    n = len(arrays)
    gather = kind == "gather"
    lands = [_own_slot(a, me, kind) for a in arrays]

    def body(*refs):
        src_refs, land_refs = refs[:n], refs[n:2 * n]
        send_sems, recv_sems = refs[2 * n], refs[2 * n + 1]
        token = refs[-1]
        my_block, peers = _peers()
        for t in range(n):
            for slot, dev, block in peers:
                pltpu.make_async_remote_copy(
                    src_ref=src_refs[t] if gather else src_refs[t].at[block], dst_ref=land_refs[t].at[my_block],
                    send_sem=send_sems.at[t * (NDEV - 1) + slot], recv_sem=recv_sems.at[t * (NDEV - 1) + slot],
                    device_id=dev, device_id_type=MESH).start()
        token[...] = jnp.zeros_like(token)

    operands = [pltpu.with_memory_space_constraint(a, pltpu.HBM) for a in list(arrays) + lands]
    outs = pl.pallas_call(
        body, name=name,
        out_shape=(pltpu.SemaphoreType.DMA((n * (NDEV - 1),)), pltpu.SemaphoreType.DMA((n * (NDEV - 1),)),
                   *[_hbm(a) for a in operands], _sds((8, 128), F32)),
        in_specs=[_HBM_SPEC] * (2 * n),
        out_specs=(_SEM_SPEC, _SEM_SPEC, *[_HBM_SPEC] * (2 * n), pl.BlockSpec(memory_space=pltpu.VMEM)),
        input_output_aliases={i: 2 + i for i in range(2 * n)},
        compiler_params=pltpu.CompilerParams(has_side_effects=_DATAFLOW),
    )(*operands)
    return (outs[0], outs[1], list(outs[2:2 + n]), list(outs[2 + n:2 + 2 * n])), outs[-1]


def _exchange_wait(name, started, t, after, kind):
    send_sems, recv_sems, srcs, lands = started
    gather = kind == "gather"

    def body(src_ref, land_ref, send_ref, recv_ref, after_ref, src_out, land_out):
        _, peers = _peers()
        for slot, dev, block in peers:
            copy = pltpu.make_async_remote_copy(
                src_ref=src_ref if gather else src_ref.at[block], dst_ref=land_ref.at[block],
                send_sem=send_ref.at[t * (NDEV - 1) + slot], recv_sem=recv_ref.at[t * (NDEV - 1) + slot],
                device_id=dev, device_id_type=MESH)
            copy.wait_send()
            copy.wait_recv()

    return pl.pallas_call(
        body, name=name, out_shape=(_hbm(srcs[t]), _hbm(lands[t])),
        in_specs=(_HBM_SPEC, _HBM_SPEC, _SEM_SPEC, _SEM_SPEC, pl.BlockSpec(memory_space=pl.ANY)),
        out_specs=(_HBM_SPEC, _HBM_SPEC), input_output_aliases={0: 0, 1: 1},
        compiler_params=pltpu.CompilerParams(has_side_effects=_DATAFLOW),
    )(srcs[t], lands[t], send_sems, recv_sems, after)[1]


DIRECT_RELS = (1, 2, 4, 6)
RELAY_RELS = (2, 4, 6)


def _rel_peer(rel):
    x, y, c = lax.axis_index("x"), lax.axis_index("y"), lax.axis_index("c")
    px = 1 - x if rel & 4 else x
    py = 1 - y if rel & 2 else y
    pc = 1 - c if rel & 1 else c
    return (px, py, pc), 4 * px + 2 * py + pc


def _gather_start(name, shards, me):
    n, nr = len(shards), len(DIRECT_RELS)
    lands = [_own_slot(a, me, "gather") for a in shards]

    def body(*refs):
        src_refs, land_refs = refs[:n], refs[n:2 * n]
        send_sems, recv_sems = refs[2 * n], refs[2 * n + 1]
        _, my_block = _rel_peer(0)
        for t in range(n):
            for s, rel in enumerate(DIRECT_RELS):
                dev, _ = _rel_peer(rel)
                pltpu.make_async_remote_copy(
                    src_ref=src_refs[t], dst_ref=land_refs[t].at[my_block],
                    send_sem=send_sems.at[t * nr + s], recv_sem=recv_sems.at[t * nr + s],
                    device_id=dev, device_id_type=MESH).start()

    operands = [pltpu.with_memory_space_constraint(a, pltpu.HBM) for a in list(shards) + lands]
    outs = pl.pallas_call(
        body, name=name,
        out_shape=(pltpu.SemaphoreType.DMA((n * nr,)), pltpu.SemaphoreType.DMA((n * nr,)), *[_hbm(a) for a in operands]),
        in_specs=[_HBM_SPEC] * (2 * n), out_specs=(_SEM_SPEC, _SEM_SPEC, *[_HBM_SPEC] * (2 * n)),
        input_output_aliases={i: 2 + i for i in range(2 * n)},
        compiler_params=pltpu.CompilerParams(has_side_effects=_DATAFLOW),
    )(*operands)
    return outs[0], outs[1], list(outs[2:2 + n]), list(outs[2 + n:2 + 2 * n])


def _gather_wait(name, started, ts, after):
    send_sems, recv_sems, srcs, lands = started
    m, nr = len(ts), len(DIRECT_RELS)

    def body(*refs):
        src_refs, land_refs = refs[:m], refs[m:2 * m]
        send_ref, recv_ref = refs[2 * m], refs[2 * m + 1]
        for i, t in enumerate(ts):
            for s, rel in enumerate(DIRECT_RELS):
                dev, block = _rel_peer(rel)
                copy = pltpu.make_async_remote_copy(
                    src_ref=src_refs[i], dst_ref=land_refs[i].at[block],
                    send_sem=send_ref.at[t * nr + s], recv_sem=recv_ref.at[t * nr + s],
                    device_id=dev, device_id_type=MESH)
                copy.wait_send()
                copy.wait_recv()

    operands = [srcs[t] for t in ts] + [lands[t] for t in ts]
    outs = pl.pallas_call(
        body, name=name, out_shape=tuple(_hbm(a) for a in operands),
        in_specs=[_HBM_SPEC] * (2 * m) + [_SEM_SPEC, _SEM_SPEC, pl.BlockSpec(memory_space=pl.ANY)],
        out_specs=tuple([_HBM_SPEC] * (2 * m)), input_output_aliases={i: i for i in range(2 * m)},
        compiler_params=pltpu.CompilerParams(has_side_effects=_DATAFLOW),
    )(*operands, send_sems, recv_sems, after)
    return list(outs[m:])


def _relay_start(name, lands):
    m, nr = len(lands), len(RELAY_RELS)

    def body(*refs):
        land_refs, send_sems, recv_sems = refs[:m], refs[m], refs[m + 1]
        sibling, _ = _rel_peer(1)
        for i in range(m):
            for s, rel in enumerate(RELAY_RELS):
                _, block = _rel_peer(rel)
                pltpu.make_async_remote_copy(
                    src_ref=land_refs[i].at[block], dst_ref=land_refs[i].at[block],
                    send_sem=send_sems.at[i * nr + s], recv_sem=recv_sems.at[i * nr + s],
                    device_id=sibling, device_id_type=MESH).start()

    outs = pl.pallas_call(
        body, name=name,
        out_shape=(pltpu.SemaphoreType.DMA((m * nr,)), pltpu.SemaphoreType.DMA((m * nr,)), *[_hbm(a) for a in lands]),
        in_specs=[_HBM_SPEC] * m, out_specs=(_SEM_SPEC, _SEM_SPEC, *[_HBM_SPEC] * m),
        input_output_aliases={i: 2 + i for i in range(m)},
        compiler_params=pltpu.CompilerParams(has_side_effects=_DATAFLOW),
    )(*lands)
    return outs[0], outs[1], list(outs[2:])


def _relay_wait(name, relayed, after):
    send_sems, recv_sems, lands = relayed
    m, nr = len(lands), len(RELAY_RELS)

    def body(*refs):
        land_refs, send_ref, recv_ref = refs[:m], refs[m], refs[m + 1]
        sibling, _ = _rel_peer(1)
        for i in range(m):
            for s, rel in enumerate(RELAY_RELS):
                _, sent = _rel_peer(rel)
                _, arriving = _rel_peer(rel ^ 1)
                copy = pltpu.make_async_remote_copy(
                    src_ref=land_refs[i].at[sent], dst_ref=land_refs[i].at[arriving],
                    send_sem=send_ref.at[i * nr + s], recv_sem=recv_ref.at[i * nr + s],
                    device_id=sibling, device_id_type=MESH)
                copy.wait_send()
                copy.wait_recv()

    outs = pl.pallas_call(
        body, name=name, out_shape=tuple(_hbm(a) for a in lands),
        in_specs=[_HBM_SPEC] * m + [_SEM_SPEC, _SEM_SPEC, pl.BlockSpec(memory_space=pl.ANY)],
        out_specs=tuple([_HBM_SPEC] * m), input_output_aliases={i: i for i in range(m)},
        compiler_params=pltpu.CompilerParams(has_side_effects=_DATAFLOW),
    )(*lands, send_sems, recv_sems, after)
    return list(outs)


def _ffn_fwd(tag, n, wg, wd):
    gu, act = _gate_up_act(f"ffn_gate_up_{tag}", n, wg)
    wd4 = wd.reshape(NFB, FB, D)
    f = _fwd_kblocked(f"ffn_down_{tag}", act, wd4)
    return (n, gu, act, wg, wd4), f


def _ffn_bwd(tag, dh_out, df, h_in, saved, g_pre, send, mixer):
    n, gu, act, wg, wd4 = saved
    dwd = _bwd_w_kblocked(f"ffn_down_dw_{tag}", act, df).reshape(NDEV, DFF // NDEV, D)
    dgu = _down_dx_act_bwd(f"ffn_down_dx_{tag}", df, wd4, gu).reshape(NDEV, S, FB)
    tok = send({f"down_{tag}": dwd, f"gate_up_{tag}": _bwd_w_cols_blocked(f"ffn_gate_up_dw_{tag}", n, dgu)})
    dn = _bwd_x_cols_blocked(f"ffn_gate_up_dx_{tag}", dgu, wg, after=tok)
    dh_in, (dg_pre,), dy, dg_mixer = _rms_bwd(f"ffn_prenorm_bwd_{tag}", h_in, [(g_pre, dn)], dh_out, F32, then=mixer)
    return dh_in, dg_pre, dy, dg_mixer


def kernel(x, positions, mix_norm_pre, mix_norm_post, ffn_norm_pre, ffn_norm_post, ffn_w_gate_up, ffn_w_down, conv_w_in, conv_w, conv_w_out, kv_norm, w_kv, w_q, w_o, loss_target, m_mix_norm_pre, m_mix_norm_post, m_ffn_norm_pre, m_ffn_norm_post, m_ffn_w_gate_up, m_ffn_w_down, m_conv_w_in, m_conv_w, m_conv_w_out, m_kv_norm, m_w_kv, m_w_q, m_w_o, v_mix_norm_pre, v_mix_norm_post, v_ffn_norm_pre, v_ffn_norm_post, v_ffn_w_gate_up, v_ffn_w_down, v_conv_w_in, v_conv_w, v_conv_w_out, v_kv_norm, v_w_kv, v_w_q, v_w_o):
    me = 4 * lax.axis_index("x") + 2 * lax.axis_index("y") + lax.axis_index("c")
    h0 = x.reshape(S, D)
    target = loss_target.reshape(S, D)
    row = lambda a, l: a[l].reshape(1, D)
    g_kv = kv_norm.reshape(1, D)

    cw_shard = jnp.pad(conv_w[0], ((0, 5), (0, 0)))
    names = ["conv_in", "conv_w", "conv_out", "gate_up_0", "down_0", "kv", "q", "o", "gate_up_1", "down_1"]
    shards = [conv_w_in[0], cw_shard, conv_w_out[0], ffn_w_gate_up[0], ffn_w_down[0],
              w_kv, w_q[0], w_o[0], ffn_w_gate_up[1], ffn_w_down[1]]
    shards = [s if n == "conv_w" else s.astype(BF16) for n, s in zip(names, shards)]
    first = 3
    gather_first = _gather_start("gather_start_conv", shards[:first], me)
    gather_rest = _gather_start("gather_start_rest", shards[first:], me)

    def direct(group, after):
        ts = [names.index(n) for n in group]
        started, ts = (gather_first, ts) if ts[0] < first else (gather_rest, [t - first for t in ts])
        lands = _gather_wait(f"gather_wait_{group[0]}", started, ts, after)
        return _relay_start(f"relay_start_{group[0]}", lands)

    def finish(group, relayed, after):
        return dict(zip(group, _relay_wait(f"relay_wait_{group[0]}", relayed, after)))

    sent = {}

    def send(grads):
        started, token = _exchange_start(f"scatter_start_{next(iter(grads))}", list(grads.values()), me, "scatter")
        for i, name in enumerate(grads):
            sent[name] = (started, i)
        return token

    groups = [["conv_in", "conv_w", "conv_out"], ["gate_up_0", "down_0"], ["kv", "q"], ["o", "gate_up_1", "down_1"]]
    n0 = _rms_fwd("mix_prenorm_0", h0, [row(mix_norm_pre, 0)])[0]
    half = HEAD_DIM // 2
    inv_freq = ROPE_THETA ** (-jnp.arange(half, dtype=F32) / half)
    tables = _rope_tables("rope_tables", positions.reshape(S, 1), jnp.tile(inv_freq, 4).reshape(1, 128))
    w = finish(groups[0], direct(groups[0], tables[0]), n0)
    win = w["conv_in"].transpose(1, 0, 2).reshape(D, 3 * D)
    cw = w["conv_w"].transpose(1, 0, 2).reshape(8, D)
    wout = w["conv_out"].reshape(D, D)
    z = _fwd_rows("conv_in", n0, win, BF16)
    pre = _conv_fwd("conv_gate", z, cw)
    relayed = direct(groups[1], pre)
    y0 = _fwd_rows("conv_out", pre, wout)
    h1, (n1,) = _resid_rms("mix_postnorm_0", h0, y0, row(mix_norm_post, 0), [row(ffn_norm_pre, 0)])
    w = finish(groups[1], relayed, n1)
    ffn0, f0 = _ffn_fwd("0", n1, w["gate_up_0"], w["down_0"])
    relayed = direct(groups[2], ffn0[2])
    h2, (nk, n2) = _resid_rms("ffn_postnorm_0", h1, f0, row(ffn_norm_post, 0), [g_kv, row(mix_norm_pre, 1)])

    w = finish(groups[2], relayed, nk)
    wkv = w["kv"].transpose(1, 0, 2).reshape(D, 2 * QW)
    wq = w["q"].transpose(1, 0, 2).reshape(D, QW)
    qc, kc, vc, o_c, lse_c = [], [], [], [], []
    for g, d in enumerate(DILATIONS):
        q_g, k_g, v_g = _qkv_classes(f"qkv_proj_{g}", n2, nk, wq, wkv, g, d, tables)
        qc.append(q_g)
        kc.append(k_g)
        vc.append(v_g)
    relayed = direct(groups[3], vc[-1])
    for g, d in enumerate(DILATIONS):
        o_g, lse_g = _attn_fwd(f"attn_fwd_{g}", qc[g], kc[g], vc[g], d)
        o_c.append(o_g)
        lse_c.append(lse_g)
    o_mix = _mix_fwd("attn_mix", o_c, lse_c)
    w = finish(groups[3], relayed, o_mix)
    wo = w["o"].reshape(D, D)
    y1 = _fwd_rows("attn_out", o_mix, wo)
    h3, (n3,) = _resid_rms("mix_postnorm_1", h2, y1, row(mix_norm_post, 1), [row(ffn_norm_pre, 1)])
    ffn1, f1 = _ffn_fwd("1", n3, w["gate_up_1"], w["down_1"])

    dh4, df1, dg_fpost1, sq = _resid_rms_loss("ffn_postnorm_1_loss", h3, f1, row(ffn_norm_post, 1), target)

    dh3, dg_fpre1, dy1, dg_mpost1 = _ffn_bwd(
        "1", dh4, df1, h3, ffn1, row(ffn_norm_pre, 1), send, (y1, row(mix_norm_post, 1)))
    dwo = _bwd_w_rows("attn_out_dw", o_mix, dy1).reshape(NDEV, D // NDEV, D)
    do = _bwd_x_rows("attn_out_dx", dy1, wo, BF16)
    lane = jnp.arange(128)
    ones_blockdiag = (lane[:, None] // HEAD_DIM == lane[None, :] // HEAD_DIM).astype(BF16)
    mixed = _mix_bwd("attn_mix_bwd", do, o_c, lse_c, ones_blockdiag)
    branch_grads = [_attn_bwd(f"attn_bwd_{g}", qc[g], kc[g], vc[g], mixed[g], lse_c[g], mixed[3 + g], d)
                    for g, d in enumerate(DILATIONS)]
    dq_raw, dkv = _attn_bwd_post("attn_bwd_post", branch_grads, *tables)
    tok = send({"o": dwo, "kv": _bwd_w_cols("kv_proj_dw", nk, dkv, 2 * QW // NDEV),
                "q": _bwd_w_cols("q_proj_dw", n2, dq_raw, QW // NDEV)})
    dnk = _bwd_x_plain("kv_proj_dx", dkv, wkv, after=tok)
    dn2 = _bwd_x_plain("q_proj_dx", dq_raw, wq)
    dh2, (dg_kv, dg_mpre1), df0, dg_fpost0 = _rms_bwd(
        "kv_and_mix_prenorm_bwd_1", h2, [(g_kv, dnk), (row(mix_norm_pre, 1), dn2)], dh3, F32,
        then=(f0, row(ffn_norm_post, 0)))

    dh1, dg_fpre0, dy0, dg_mpost0 = _ffn_bwd(
        "0", dh2, df0, h1, ffn0, row(ffn_norm_pre, 0), send, (y0, row(mix_norm_post, 0)))
    dwout = _bwd_w_rows("conv_out_dw", pre, dy0).reshape(NDEV, D // NDEV, D)
    dpre = _bwd_x_rows("conv_out_dx", dy0, wout, BF16)
    dz, dcw = _conv_bwd("conv_gate_bwd", z, dpre, cw)
    tok = send({"conv_out": dwout, "conv_in": _bwd_w_cols("conv_in_dw", n0, dz, 3 * D // NDEV)})
    dn0 = _bwd_x_plain("conv_in_dx", dz, win, after=tok)
    dh0, (dg_mpre0,) = _rms_bwd("mix_prenorm_bwd_0", h0, [(row(mix_norm_pre, 0), dn0)], dh1, F32)

    small = _pack_small("pack_small_grads", [dg_mpre0, dg_mpre1, dg_mpost0, dg_mpost1, dg_fpre0, dg_fpre1,
                                             dg_fpost0, dg_fpost1, dg_kv], dcw, sq)

    done = [small]

    def upd(tag, w, m, v):
        parts = _exchange_wait(f"scatter_wait_{tag}", *sent[tag], done[-1], "scatter")
        shape = w.shape
        flat = lambda a: a.reshape(parts.shape[1:])
        res = _adamw(f"adamw_{tag}", parts, flat(w), flat(m), flat(v))
        done.append(res[0])
        return [r.reshape(shape) for r in res]

    def upd_layer(tag, l, w, m, v, other):
        parts = _exchange_wait(f"scatter_wait_{tag}_{l}", *sent[f"{tag}_{l}"], done[-1], "scatter")
        res = _adamw(f"adamw_{tag}_{l}", parts, w, m, v, layer=l, other=other)
        done.append(res[0])
        return list(res)

    res = {}
    down_1 = upd_layer("down", 1, ffn_w_down, m_ffn_w_down, v_ffn_w_down, None)
    gate_up_t = [jnp.swapaxes(a, 1, 2) for a in (ffn_w_gate_up, m_ffn_w_gate_up, v_ffn_w_gate_up)]
    gate_up_1 = upd_layer("gate_up", 1, *gate_up_t, None)
    res["w_o"] = upd("o", w_o, m_w_o, v_w_o)
    res["w_q"] = upd("q", w_q, m_w_q, v_w_q)
    res["w_kv"] = upd("kv", w_kv, m_w_kv, v_w_kv)

    small_all = _exchange("gather_small_grads", [small], "gather", done[-1])[0]
    vec = lambda a: a.reshape(1, D)
    gain_res, taps, loss = _adamw_gains("adamw_gains", small_all, [
        (mix_norm_pre, m_mix_norm_pre, v_mix_norm_pre), (mix_norm_post, m_mix_norm_post, v_mix_norm_post),
        (ffn_norm_pre, m_ffn_norm_pre, v_ffn_norm_pre), (ffn_norm_post, m_ffn_norm_post, v_ffn_norm_post),
        (vec(kv_norm), vec(m_kv_norm), vec(v_kv_norm))])
    dcw_mine = lax.dynamic_slice(taps, (0, me * 128), (8, 128))
    pad8 = lambda a, fill: jnp.pad(a[0], ((0, 5), (0, 0)), constant_values=fill)
    cw_res = [r[0:3].reshape(1, 3, 128) for r in
              _adamw("adamw_conv_w", dcw_mine.reshape(1, 8, 128), cw_shard, pad8(m_conv_w, 0.0), pad8(v_conv_w, 1.0))]

    res.update({
        "mix_norm_pre": gain_res[0],
        "mix_norm_post": gain_res[1],
        "ffn_norm_pre": gain_res[2],
        "ffn_norm_post": gain_res[3],
        "kv_norm": [r.reshape(D) for r in gain_res[4]],
        "conv_w": cw_res,
    })
    done.append(small_all)
    res["ffn_w_down"] = upd_layer("down", 0, ffn_w_down, m_ffn_w_down, v_ffn_w_down, down_1)
    res["ffn_w_gate_up"] = [jnp.swapaxes(r, 1, 2) for r in upd_layer("gate_up", 0, *gate_up_t, gate_up_1)]
    res["conv_w_out"] = upd("conv_out", conv_w_out, m_conv_w_out, v_conv_w_out)
    res["conv_w_in"] = upd("conv_in", conv_w_in, m_conv_w_in, v_conv_w_in)
    order = ["mix_norm_pre", "mix_norm_post", "ffn_norm_pre", "ffn_norm_post", "ffn_w_gate_up", "ffn_w_down",
             "conv_w_in", "conv_w", "conv_w_out", "kv_norm", "w_kv", "w_q", "w_o"]
    out = [loss, dh0.reshape(1, S, D)]
    for i in range(4):
        out += [res[name][i] for name in order]
    return tuple(out)
```

```python
import jax
import jax.numpy as jnp
from jax import lax
from jax.experimental import pallas as pl
from jax.experimental.pallas import tpu as pltpu

F32 = jnp.float32
BF16 = jnp.bfloat16

S = 4096
D = 1024
NDEV = 8
HEAD_DIM = 64
QW = 3072
DFF = 2816
FB = 704
NFB = 4
BRANCHES = ((128, 1), (512, 4), (2048, 16))
BAND = 128
ROPE_THETA = 10000.0
RMS_EPS = 1e-6
NEG_INF = -1e30
ADAM_LR, ADAM_B1, ADAM_B2, ADAM_EPS, ADAM_WD, ADAM_STEP = 0.001, 0.9, 0.999, 1e-08, 0.01, 10

VMEM_LIMIT_BYTES = 52 * 1024 * 1024
ROW_TILE = 512
MESH = pl.DeviceIdType.MESH


def _cparams(ngrid):
    return pltpu.CompilerParams(dimension_semantics=("arbitrary",) * ngrid,
                                vmem_limit_bytes=VMEM_LIMIT_BYTES)


def _sds(shape, dtype):
    return jax.ShapeDtypeStruct(tuple(shape), dtype)


_DIMS = {"nn": (((1,), (0,)), ((), ())),
         "nt": (((1,), (1,)), ((), ())),
         "tn": (((0,), (0,)), ((), ()))}


def _matmul(name, a, b, *, mode, grid, a_blk, a_map, b_blk, b_map, o_shape, o_blk, o_map, out_dtype, after=None,
            out_groups=1):
    nk = grid[2]
    dims = _DIMS[mode]
    acc_shape = tuple(s for s in o_blk if s is not None)
    if out_groups > 1:
        acc_shape = (acc_shape[1], out_groups * acc_shape[2])
    extra = [] if after is None else [after]

    def store(o_ref, val):
        if out_groups == 1:
            o_ref[...] = val.astype(o_ref.dtype)
        else:
            n = o_ref.shape[-1]
            for grp in range(out_groups):
                o_ref[grp] = val[:, grp * n:(grp + 1) * n].astype(o_ref.dtype)

    def body(a_ref, b_ref, *rest):
        o_ref, scratch = rest[len(extra)], rest[len(extra) + 1:]
        part = lax.dot_general(a_ref[...], b_ref[...], dims, preferred_element_type=F32)
        if nk == 1:
            store(o_ref, part)
            return
        acc_ref = scratch[0]
        k = pl.program_id(2)

        @pl.when(k == 0)
        def _():
            acc_ref[...] = part

        @pl.when(k > 0)
        def _():
            acc_ref[...] += part

        @pl.when(k == nk - 1)
        def _():
            store(o_ref, acc_ref[...])

    return pl.pallas_call(
        body, name=name, grid=grid,
        in_specs=[pl.BlockSpec(a_blk, a_map), pl.BlockSpec(b_blk, b_map)] + [pl.BlockSpec(memory_space=pl.ANY)] * len(extra),
        out_specs=pl.BlockSpec(o_blk, o_map),
        out_shape=_sds(o_shape, out_dtype),
        scratch_shapes=[] if nk == 1 else [pltpu.VMEM(acc_shape, F32)],
        compiler_params=_cparams(3),
    )(a, b, *extra)


TM = 1024
TK = S


def _fwd_rows(name, a, w, out_dtype=F32):
    kdim, n = w.shape
    tn = 1024
    return _matmul(name, a, w, mode="nn", grid=(S // TM, n // tn, 1),
                   a_blk=(TM, kdim), a_map=lambda i, j, k: (i, 0),
                   b_blk=(kdim, tn), b_map=lambda i, j, k: (0, j),
                   o_shape=(S, n), o_blk=(TM, tn), o_map=lambda i, j, k: (i, j), out_dtype=out_dtype)


def _fwd_kblocked(name, a4, w4):
    nb, _, kb = a4.shape
    n = w4.shape[2]

    def body(a_ref, w_ref, o_ref):
        acc = _dot_nn(a_ref[0], w_ref[0])
        for j in range(1, nb):
            acc = acc + _dot_nn(a_ref[j], w_ref[j])
        o_ref[...] = acc

    return pl.pallas_call(
        body, name=name, grid=(S // TM,),
        in_specs=[pl.BlockSpec((nb, TM, kb), lambda i: (0, i, 0)), pl.BlockSpec((nb, kb, n), lambda i: (0, 0, 0))],
        out_specs=pl.BlockSpec((TM, n), lambda i: (i, 0)), out_shape=_sds((S, n), F32),
        compiler_params=_cparams(1),
    )(a4, w4)


def _bwd_x_cols_blocked(name, dy8, wg, after):
    _, kdim, n = wg.shape
    nk = NDEV // 2

    def body(a_ref, b_ref, after_ref, o_ref, acc_ref):
        k = pl.program_id(1)
        part = _dot_nt(a_ref[0], b_ref[0]) + _dot_nt(a_ref[1], b_ref[1])

        @pl.when(k == 0)
        def _():
            acc_ref[...] = part

        @pl.when(k > 0)
        def _():
            acc_ref[...] += part

        @pl.when(k == nk - 1)
        def _():
            o_ref[...] = acc_ref[...].astype(o_ref.dtype)

    return pl.pallas_call(
        body, name=name, grid=(S // FFN_TM, nk),
        in_specs=[pl.BlockSpec((2, None, FFN_TM, n), lambda i, k: (0, k, i, 0)),
                  pl.BlockSpec((2, None, kdim, n), lambda i, k: (0, k, 0, 0)),
                  pl.BlockSpec(memory_space=pl.ANY)],
        out_specs=pl.BlockSpec((FFN_TM, kdim), lambda i, k: (i, 0)), out_shape=_sds((S, kdim), BF16),
        scratch_shapes=[pltpu.VMEM((FFN_TM, kdim), F32)],
        compiler_params=_cparams(2),
    )(dy8.reshape(2, nk, S, n), wg.reshape(2, nk, kdim, n), after)


def _bwd_x_rows(name, dy, w, out_dtype, after=None):
    kdim, n = w.shape
    tkk = 512
    return _matmul(name, dy, w, mode="nt", grid=(S // TM, kdim // tkk, 1),
                   a_blk=(TM, n), a_map=lambda i, j, k: (i, 0),
                   b_blk=(tkk, n), b_map=lambda i, j, k: (j, 0),
                   o_shape=(S, kdim), o_blk=(TM, tkk), o_map=lambda i, j, k: (i, j), out_dtype=out_dtype, after=after)


DW_COLS = 768


def _bwd_w_cols(name, a, dy, n):
    kdim = a.shape[1]
    groups = DW_COLS // n
    return _matmul(name, a, dy, mode="tn", grid=(1, NDEV // groups, S // TK),
                   a_blk=(TK, kdim), a_map=lambda i, j, k: (k, 0),
                   b_blk=(TK, DW_COLS), b_map=lambda i, j, k: (k, j),
                   o_shape=(NDEV, kdim, n), o_blk=(groups, kdim, n) if groups > 1 else (None, kdim, n),
                   o_map=lambda i, j, k: (j, 0, 0), out_dtype=BF16, out_groups=groups)


def _bwd_x_plain(name, dy, w, after=None):
    kdim, n = w.shape
    tm = TM if n <= 3 * D else TM // 2
    return _matmul(name, dy, w, mode="nt", grid=(S // tm, 1, 1),
                   a_blk=(tm, n), a_map=lambda i, j, k: (i, 0),
                   b_blk=(kdim, n), b_map=lambda i, j, k: (0, 0),
                   o_shape=(S, kdim), o_blk=(tm, kdim), o_map=lambda i, j, k: (i, 0), out_dtype=BF16, after=after)


def _bwd_w_cols_blocked(name, a, dy8):
    kdim = a.shape[1]
    n = dy8.shape[2]
    return _matmul(name, dy8, a, mode="tn", grid=(1, NDEV, S // TK),
                   a_blk=(None, TK, n), a_map=lambda i, j, k: (j, k, 0),
                   b_blk=(TK, kdim), b_map=lambda i, j, k: (k, 0),
                   o_shape=(NDEV, n, kdim), o_blk=(None, n, kdim), o_map=lambda i, j, k: (j, 0, 0), out_dtype=BF16)


def _bwd_w_rows(name, a, dy):
    kdim = a.shape[1]
    n = dy.shape[1]
    tmm = 512
    return _matmul(name, a, dy, mode="tn", grid=(kdim // tmm, 1, S // TK),
                   a_blk=(TK, tmm), a_map=lambda i, j, k: (k, i),
                   b_blk=(TK, n), b_map=lambda i, j, k: (k, 0),
                   o_shape=(kdim, n), o_blk=(tmm, n), o_map=lambda i, j, k: (i, 0), out_dtype=BF16)


def _bwd_w_kblocked(name, a4, dy):
    nb, _, kb = a4.shape
    n = dy.shape[1]
    return _matmul(name, a4, dy, mode="tn", grid=(nb, 1, S // TK),
                   a_blk=(None, TK, kb), a_map=lambda i, j, k: (i, k, 0),
                   b_blk=(TK, n), b_map=lambda i, j, k: (k, 0),
                   o_shape=(nb, kb, n), o_blk=(None, kb, n), o_map=lambda i, j, k: (i, 0, 0), out_dtype=BF16)


def _rstd(x):
    return lax.rsqrt(jnp.mean(x * x, axis=-1, keepdims=True) + RMS_EPS)


def _row_spec(tm=ROW_TILE, width=D):
    return pl.BlockSpec((tm, width), lambda i: (i, 0))


def _vec_spec(rows=1, width=D):
    return pl.BlockSpec((rows, width), lambda i: (0, 0))


def _rms_fwd(name, x, gains):
    n = len(gains)

    def body(x_ref, *refs):
        x_val = x_ref[...]
        xh = x_val * _rstd(x_val)
        for g_ref, o_ref in zip(refs[:n], refs[n:]):
            o_ref[...] = (xh * g_ref[...]).astype(o_ref.dtype)

    outs = pl.pallas_call(
        body, name=name, grid=(S // ROW_TILE,),
        in_specs=[_row_spec()] + [_vec_spec()] * n,
        out_specs=[_row_spec()] * n,
        out_shape=[_sds((S, D), BF16)] * n,
        compiler_params=_cparams(1),
    )(x, *gains)
    return list(outs)


def _resid_rms(name, h, y, g, next_gains):
    n = len(next_gains)

    def body(h_ref, y_ref, g_ref, *refs):
        y_val = y_ref[...]
        h_new = h_ref[...] + (y_val * _rstd(y_val)) * g_ref[...]
        refs[n][...] = h_new
        hh = h_new * _rstd(h_new)
        for g2_ref, o_ref in zip(refs[:n], refs[n + 1:]):
            o_ref[...] = (hh * g2_ref[...]).astype(o_ref.dtype)

    outs = pl.pallas_call(
        body, name=name, grid=(S // TM,),
        in_specs=[_row_spec(TM), _row_spec(TM), _vec_spec()] + [_vec_spec()] * n,
        out_specs=[_row_spec(TM)] * (n + 1), out_shape=[_sds((S, D), F32)] + [_sds((S, D), BF16)] * n,
        compiler_params=_cparams(1),
    )(h, y, g, *next_gains)
    return outs[0], list(outs[1:])


def _resid_rms_loss(name, h, y, g, target):
    def body(h_ref, y_ref, g_ref, t_ref, dh_ref, dy_ref, dg_ref, part_ref):
        y_val = y_ref[...]
        gain = g_ref[...]
        e = h_ref[...] + (y_val * _rstd(y_val)) * gain - t_ref[...]
        dh = e * (1.0 / D)
        dh_ref[...] = dh
        step = pl.program_id(0)
        dy_ref[...] = _norm_bwd_rows(y_val, gain, dh, dg_ref, step).astype(dy_ref.dtype)
        part = jnp.sum(e * e, axis=0, keepdims=True)

        @pl.when(step == 0)
        def _():
            part_ref[...] = part

        @pl.when(step > 0)
        def _():
            part_ref[...] += part

    return pl.pallas_call(
        body, name=name, grid=(S // ROW_TILE,),
        in_specs=[_row_spec(), _row_spec(), _vec_spec(), _row_spec()],
        out_specs=[_row_spec(), _row_spec(), _vec_spec(8), _vec_spec()],
        out_shape=[_sds((S, D), F32), _sds((S, D), BF16), _sds((8, D), F32), _sds((1, D), F32)],
        compiler_params=_cparams(1),
    )(h, y, g, target)


def _norm_bwd_rows(x_val, g, dn, dg_ref, step):
    r = _rstd(x_val)
    xh = x_val * r
    dxh = dn * g
    part = jnp.sum(dn * xh, axis=0, keepdims=True)

    @pl.when(step == 0)
    def _():
        dg_ref[...] = jnp.zeros_like(dg_ref)

    dg_ref[0:1, :] += part
    return r * (dxh - xh * jnp.mean(dxh * xh, axis=-1, keepdims=True))


def _rms_bwd(name, x, pairs, dres, out_dtype, then=None):
    n = len(pairs)
    has_res = dres is not None
    chained = then is not None

    def body(x_ref, *refs):
        g_refs = refs[0:2 * n:2]
        dn_refs = refs[1:2 * n:2]
        pos = 2 * n
        res_ref = refs[pos] if has_res else None
        pos += int(has_res)
        if chained:
            y_ref, gy_ref = refs[pos], refs[pos + 1]
            pos += 2
        dx_ref = refs[pos]
        dg_refs = refs[pos + 1:pos + 1 + n]
        step = pl.program_id(0)
        x_val = x_ref[...]
        acc = res_ref[...] if has_res else jnp.zeros_like(x_val)
        for g_ref, dn_ref, dg_ref in zip(g_refs, dn_refs, dg_refs):
            acc = acc + _norm_bwd_rows(x_val, g_ref[...], dn_ref[...].astype(F32), dg_ref, step)
        dx_ref[...] = acc.astype(dx_ref.dtype)
        if chained:
            dy_ref, dgy_ref = refs[pos + 1 + n], refs[pos + 2 + n]
            dy_ref[...] = _norm_bwd_rows(y_ref[...], gy_ref[...], acc, dgy_ref, step).astype(dy_ref.dtype)

    operands = [x]
    in_specs = [_row_spec()]
    for g, dn in pairs:
        operands += [g, dn]
        in_specs += [_vec_spec(), _row_spec()]
    if has_res:
        operands.append(dres)
        in_specs.append(_row_spec())
    if chained:
        operands += [then[0], then[1]]
        in_specs += [_row_spec(), _vec_spec()]
    extra = int(chained)
    outs = pl.pallas_call(
        body, name=name, grid=(S // ROW_TILE,),
        in_specs=in_specs,
        out_specs=[_row_spec()] + [_vec_spec(8)] * n + [_row_spec(), _vec_spec(8)] * extra,
        out_shape=[_sds((S, D), out_dtype)] + [_sds((8, D), F32)] * n + [_sds((S, D), BF16), _sds((8, D), F32)] * extra,
        compiler_params=_cparams(1),
    )(*operands)
    if chained:
        return outs[0], list(outs[1:1 + n]), outs[1 + n], outs[2 + n]
    return outs[0], list(outs[1:])


def _shift_down(u, prev8, k):
    r = pltpu.roll(u, k, 0)
    p = pltpu.roll(prev8, k, 0)
    row = lax.broadcasted_iota(jnp.int32, prev8.shape, 0)
    top = jnp.where(row < k, p, r[0:8])
    return jnp.concatenate([top, r[8:]], axis=0)


def _shift_up(u, next8, k):
    tm = u.shape[0]
    r = pltpu.roll(u, tm - k, 0)
    p = pltpu.roll(next8, 8 - k, 0)
    row = lax.broadcasted_iota(jnp.int32, next8.shape, 0)
    bot = jnp.where(row >= 8 - k, p, r[tm - 8:tm])
    return jnp.concatenate([r[:tm - 8], bot], axis=0)


CONV_TILE = 512


def _halo_prev(col):
    return pl.BlockSpec((8, D), lambda i: (jnp.maximum(i * (CONV_TILE // 8) - 1, 0), col))


def _halo_next(col):
    last = S // 8 - 1
    return pl.BlockSpec((8, D), lambda i: (jnp.minimum((i + 1) * (CONV_TILE // 8), last), col))


def _conv_fwd(name, z, cw):
    def body(b_ref, c_ref, h_ref, cp_ref, hp_ref, cw_ref, o_ref):
        i = pl.program_id(0)
        u = c_ref[...].astype(F32) * h_ref[...].astype(F32)
        up = cp_ref[...].astype(F32) * hp_ref[...].astype(F32)
        up = jnp.where(i > 0, up, 0.0)
        cv = cw_ref[0:1, :] * _shift_down(u, up, 2) + cw_ref[1:2, :] * _shift_down(u, up, 1) + cw_ref[2:3, :] * u
        o_ref[...] = (b_ref[...].astype(F32) * cv).astype(o_ref.dtype)

    col = lambda c: pl.BlockSpec((CONV_TILE, D), lambda i: (i, c))
    return pl.pallas_call(
        body, name=name, grid=(S // CONV_TILE,),
        in_specs=[col(0), col(1), col(2), _halo_prev(1), _halo_prev(2), _vec_spec(8)],
        out_specs=_row_spec(CONV_TILE), out_shape=_sds((S, D), BF16),
        compiler_params=_cparams(1),
    )(z, z, z, z, z, cw)


def _conv_bwd(name, z, dpre, cw):
    nsteps = S // CONV_TILE

    def body(b_ref, c_ref, h_ref, cp_ref, hp_ref, dp_ref, dpn_ref, bn_ref, cw_ref, dz_ref, dcw_ref):
        i = pl.program_id(0)
        b = b_ref[...].astype(F32)
        c = c_ref[...].astype(F32)
        h = h_ref[...].astype(F32)
        dp = dp_ref[...].astype(F32)
        u = c * h
        up = jnp.where(i > 0, cp_ref[...].astype(F32) * hp_ref[...].astype(F32), 0.0)
        s1 = _shift_down(u, up, 1)
        s2 = _shift_down(u, up, 2)
        w0, w1, w2 = cw_ref[0:1, :], cw_ref[1:2, :], cw_ref[2:3, :]
        cv = w0 * s2 + w1 * s1 + w2 * u
        dcv = dp * b
        dcvn = jnp.where(i < nsteps - 1, dpn_ref[...].astype(F32) * bn_ref[...].astype(F32), 0.0)
        du = w2 * dcv + w1 * _shift_up(dcv, dcvn, 1) + w0 * _shift_up(dcv, dcvn, 2)
        dz_ref[:, 0:D] = (dp * cv).astype(dz_ref.dtype)
        dz_ref[:, D:2 * D] = (du * h).astype(dz_ref.dtype)
        dz_ref[:, 2 * D:3 * D] = (du * c).astype(dz_ref.dtype)

        @pl.when(i == 0)
        def _():
            dcw_ref[...] = jnp.zeros_like(dcw_ref)

        dcw_ref[0:1, :] += jnp.sum(dcv * s2, axis=0, keepdims=True)
        dcw_ref[1:2, :] += jnp.sum(dcv * s1, axis=0, keepdims=True)
        dcw_ref[2:3, :] += jnp.sum(dcv * u, axis=0, keepdims=True)

    col = lambda c: pl.BlockSpec((CONV_TILE, D), lambda i: (i, c))
    return pl.pallas_call(
        body, name=name, grid=(nsteps,),
        in_specs=[col(0), col(1), col(2), _halo_prev(1), _halo_prev(2),
                  _row_spec(CONV_TILE), _halo_next(0), _halo_next(0), _vec_spec(8)],
        out_specs=[pl.BlockSpec((CONV_TILE, 3 * D), lambda i: (i, 0)), _vec_spec(8)],
        out_shape=[_sds((S, 3 * D), BF16), _sds((8, D), F32)],
        compiler_params=_cparams(1),
    )(z, z, z, z, z, dpre, dpre, z, cw)


FFN_TM = 2048
_GU_BLOCK = pl.BlockSpec((2, None, FFN_TM, FB), lambda i, j: (0, j, i, 0))


def _gate_up_act(name, a, wg):
    kdim = a.shape[1]

    def body(a_ref, wgate_ref, wup_ref, gu_ref, act_ref):
        x = a_ref[...]
        g = _dot_nn(x, wgate_ref[...])
        u = _dot_nn(x, wup_ref[...])
        gu_ref[0] = g.astype(gu_ref.dtype)
        gu_ref[1] = u.astype(gu_ref.dtype)
        act_ref[...] = (g * jax.nn.sigmoid(g) * u).astype(act_ref.dtype)

    return pl.pallas_call(
        body, name=name, grid=(S // FFN_TM, NFB),
        in_specs=[pl.BlockSpec((FFN_TM, kdim), lambda i, j: (i, 0)),
                  pl.BlockSpec((None, kdim, FB), lambda i, j: (j, 0, 0)),
                  pl.BlockSpec((None, kdim, FB), lambda i, j: (j + NFB, 0, 0))],
        out_specs=[_GU_BLOCK, pl.BlockSpec((None, FFN_TM, FB), lambda i, j: (j, i, 0))],
        out_shape=[_sds((2, NFB, S, FB), BF16), _sds((NFB, S, FB), BF16)],
        compiler_params=_cparams(2),
    )(a, wg, wg)


def _down_dx_act_bwd(name, df, w4, gu):
    _, kb, n = w4.shape

    def body(df_ref, w_ref, gu_ref, o_ref):
        d = _dot_nt(df_ref[...], w_ref[...])
        g = gu_ref[0].astype(F32)
        u = gu_ref[1].astype(F32)
        sg = jax.nn.sigmoid(g)
        o_ref[0] = (d * u * sg * (1.0 + g * (1.0 - sg))).astype(o_ref.dtype)
        o_ref[1] = (d * g * sg).astype(o_ref.dtype)

    return pl.pallas_call(
        body, name=name, grid=(S // FFN_TM, NFB),
        in_specs=[pl.BlockSpec((FFN_TM, n), lambda i, j: (i, 0)), pl.BlockSpec((None, kb, n), lambda i, j: (j, 0, 0)),
                  _GU_BLOCK],
        out_specs=_GU_BLOCK, out_shape=_sds((2, NFB, S, FB), BF16),
        compiler_params=_cparams(2),
    )(df, w4, gu)


def _rope_tables(name, pos_col, inv_freq_row):
    def body(pos_ref, f_ref, cos_ref, sin_ref):
        ang = pos_ref[...].astype(F32) * f_ref[...]
        lane = lax.broadcasted_iota(jnp.int32, ang.shape, 1)
        s = jnp.sin(ang)
        cos_ref[...] = jnp.cos(ang)
        sin_ref[...] = jnp.where((lane % HEAD_DIM) < HEAD_DIM // 2, -s, s)

    tab = pl.BlockSpec((ROW_TILE, 128), lambda i: (i, 0))
    return pl.pallas_call(
        body, name=name, grid=(S // ROW_TILE,),
        in_specs=[pl.BlockSpec((ROW_TILE, 1), lambda i: (i, 0)), _vec_spec(1, 128)],
        out_specs=[tab, tab], out_shape=[_sds((S, 128), F32)] * 2,
        compiler_params=_cparams(1),
    )(pos_col, inv_freq_row)


def _swap_halves(t):
    lane = lax.broadcasted_iota(jnp.int32, t.shape, 1)
    first = (lane % HEAD_DIM) < HEAD_DIM // 2
    return jnp.where(first, pltpu.roll(t, 128 - HEAD_DIM // 2, 1), pltpu.roll(t, HEAD_DIM // 2, 1))


NCHUNK = D // 128


def _chunk(c, base=0):
    return slice(base + c * 128, base + (c + 1) * 128)


def _class_rows(r, d, tm):
    return pl.ds(r, tm // d, stride=d) if d > 1 else slice(None)


def _class_block(d, tm):
    return pl.BlockSpec((tm // d, d * D), lambda i: (i, 0))


def _tokens_from_classes(blk_ref, tmp_ref, d, tm):
    for r in range(d):
        for c in range(NCHUNK):
            tmp_ref[c, _class_rows(r, d, tm), :] = blk_ref[:, _chunk(c, r * D)].astype(F32)


def _classes_from_tokens(tmp_ref, blk_ref, d, tm):
    for r in range(d):
        for c in range(NCHUNK):
            blk_ref[:, _chunk(c, r * D)] = tmp_ref[c, _class_rows(r, d, tm), :].astype(blk_ref.dtype)


def _qkv_classes(name, n2, nk, wq, wkv, g, d, tables):
    def emit(acc, cos_ref, sin_ref, o_ref, tmp_ref, scale):
        for c in range(NCHUNK):
            tmp_ref[c] = acc[:, _chunk(c)]
        for r in range(d):
            rows = _class_rows(r, d, TM)
            if scale is not None:
                cs = cos_ref[rows, :]
                sn = sin_ref[rows, :]
            for c in range(NCHUNK):
                x = tmp_ref[c, rows, :]
                if scale is not None:
                    x = (x * cs + _swap_halves(x) * sn) * scale
                o_ref[:, _chunk(c, r * D)] = x.astype(o_ref.dtype)

    def body(n2_ref, nk_ref, wq_ref, wk_ref, wv_ref, cos_ref, sin_ref, q_ref, k_ref, v_ref, tmp_ref):
        emit(_dot_nn(n2_ref[...], wq_ref[...]), cos_ref, sin_ref, q_ref, tmp_ref, HEAD_DIM ** -0.5)
        x = nk_ref[...]
        emit(_dot_nn(x, wk_ref[...]), cos_ref, sin_ref, k_ref, tmp_ref, 1.0)
        emit(_dot_nn(x, wv_ref[...]), cos_ref, sin_ref, v_ref, tmp_ref, None)

    nbr = len(DILATIONS)
    act = pl.BlockSpec((TM, D), lambda i: (i, 0))
    tab = pl.BlockSpec((TM, 128), lambda i: (i, 0))
    wcol = lambda col: pl.BlockSpec((D, D), lambda i: (0, col))
    return pl.pallas_call(
        body, name=name, grid=(S // TM,),
        in_specs=[act, act, wcol(g), wcol(g), wcol(nbr + g), tab, tab],
        out_specs=[_class_block(d, TM)] * 3, out_shape=[_sds((S // d, d * D), BF16)] * 3,
        scratch_shapes=[pltpu.VMEM((NCHUNK, TM, 128), F32)],
        compiler_params=_cparams(1),
    )(n2, nk, wq, wkv, wkv, *tables)


ATTN_CHAINS = 16


def _attn_units(d):
    nblk = S // d // BAND
    return max(1, 2 * ATTN_CHAINS // nblk)


def _class_spec(d):
    return pl.BlockSpec((S // d, 128 * _attn_units(d)), lambda cb: (0, cb))


def _dot_nt(a, b):
    return lax.dot_general(a, b, _DIMS["nt"], preferred_element_type=F32)


def _dot_tn(a, b):
    return lax.dot_general(a, b, _DIMS["tn"], preferred_element_type=F32)


def _dot_nn(a, b):
    return lax.dot_general(a, b, _DIMS["nn"], preferred_element_type=F32)


def _band_mask(nkeys):
    qi = lax.broadcasted_iota(jnp.int32, (2 * BAND, nkeys), 0) % BAND
    kj = lax.broadcasted_iota(jnp.int32, (2 * BAND, nkeys), 1)
    if nkeys == BAND:
        return kj <= qi
    dist = qi + BAND - kj
    return (dist >= 0) & (dist <= BAND)


def _band_bias():
    return {n: jnp.where(_band_mask(n), 0.0, NEG_INF).astype(F32) for n in (BAND, 2 * BAND)}


def _stack_heads(x):
    row = lax.broadcasted_iota(jnp.int32, (2 * BAND, 128), 0)
    lane = lax.broadcasted_iota(jnp.int32, (2 * BAND, 128), 1)
    keep = (row < BAND) == (lane < HEAD_DIM)
    return jnp.where(keep, jnp.concatenate([x, x], axis=0), jnp.zeros((), x.dtype))


def _unstack(x2):
    first_head = lax.broadcasted_iota(jnp.int32, (BAND, 128), 1) < HEAD_DIM
    return jnp.where(first_head, x2[:BAND], x2[BAND:])


def _for_later_blocks(nblk, units, fn):
    all_lanes = [slice(u * 128, (u + 1) * 128) for u in range(units)]
    unroll = max(1, ATTN_CHAINS // units)
    trips = (nblk - 1) // unroll
    if trips > 1:
        def step(i, carry):
            for j in range(unroll):
                for lanes in all_lanes:
                    fn(pl.multiple_of((1 + i * unroll + j) * BAND, BAND), lanes)
            return carry

        lax.fori_loop(0, trips, step, 0)
    else:
        trips = 0
    for sb in range(1 + trips * unroll, nblk):
        for lanes in all_lanes:
            fn(sb * BAND, lanes)


def _attn_fwd(name, q, k, v, d):
    nblk = S // d // BAND
    units = _attn_units(d)

    def body(q_ref, k_ref, v_ref, o_ref, lse_ref):
        bias = _band_bias()

        def block(r0, k0, nkeys, lanes):
            q2 = _stack_heads(q_ref[pl.ds(r0, BAND), lanes])
            s = _dot_nt(q2, k_ref[pl.ds(k0, nkeys), lanes]) + bias[nkeys]
            m = jnp.max(s, axis=-1, keepdims=True)
            p = jnp.exp(s - m)
            l = jnp.sum(p, axis=-1, keepdims=True)
            o2 = _dot_nn(p.astype(BF16), v_ref[pl.ds(k0, nkeys), lanes])
            l_tile = _unstack(jnp.broadcast_to(l, (2 * BAND, 128)))
            m_tile = _unstack(jnp.broadcast_to(m, (2 * BAND, 128)))
            o_ref[pl.ds(r0, BAND), lanes] = (_unstack(o2) / l_tile).astype(o_ref.dtype)
            lse_ref[pl.ds(r0, BAND), lanes] = m_tile + jnp.log(l_tile)

        for u in range(units):
            block(0, 0, BAND, slice(u * 128, (u + 1) * 128))

        _for_later_blocks(nblk, units, lambda r0, lanes: block(r0, r0 - BAND, 2 * BAND, lanes))

    spec = _class_spec(d)
    return pl.pallas_call(
        body, name=name, grid=(8 * d // units,),
        in_specs=[spec] * 3, out_specs=[spec] * 2,
        out_shape=[_sds((S // d, d * D), BF16), _sds((S // d, d * D), F32)],
        compiler_params=_cparams(1),
    )(q, k, v)


def _attn_bwd(name, q, k, v, do, lse, dd, d):
    nblk = S // d // BAND
    units = _attn_units(d)

    def body(q_ref, k_ref, v_ref, do_ref, lse_ref, dd_ref, dq_ref, dk_out, dv_out, dk_ref, dv_ref):
        bias = _band_bias()
        def column(ref, r0, lanes, nkeys):
            tile = ref[pl.ds(r0, BAND), lanes]
            other = pltpu.roll(tile, HEAD_DIM, 1)
            first_head = lax.broadcasted_iota(jnp.int32, tile.shape, 1) < HEAD_DIM
            both = jnp.concatenate([jnp.where(first_head, tile, other), jnp.where(first_head, other, tile)], axis=0)
            return both if nkeys == BAND else jnp.concatenate([both, both], axis=1)

        def block(r0, k0, nkeys, lanes, first):
            q2 = _stack_heads(q_ref[pl.ds(r0, BAND), lanes])
            do2 = _stack_heads(do_ref[pl.ds(r0, BAND), lanes])
            kk = k_ref[pl.ds(k0, nkeys), lanes]
            vv = v_ref[pl.ds(k0, nkeys), lanes]
            s = _dot_nt(q2, kk) + bias[nkeys]
            p = jnp.exp(s - column(lse_ref, r0, lanes, nkeys))
            ds = (p * (_dot_nt(do2, vv) - column(dd_ref, r0, lanes, nkeys))).astype(BF16)
            dq_ref[pl.ds(r0, BAND), lanes] = _unstack(_dot_nn(ds, kk)).astype(dq_ref.dtype)
            dk_part = _dot_tn(ds, q2)
            dv_part = _dot_tn(p.astype(BF16), do2)
            if first:
                dk_ref[pl.ds(k0, nkeys), lanes] = dk_part
                dv_ref[pl.ds(k0, nkeys), lanes] = dv_part
            else:
                dk_ref[pl.ds(k0, BAND), lanes] += dk_part[:BAND]
                dv_ref[pl.ds(k0, BAND), lanes] += dv_part[:BAND]
                dk_ref[pl.ds(k0 + BAND, BAND), lanes] = dk_part[BAND:]
                dv_ref[pl.ds(k0 + BAND, BAND), lanes] = dv_part[BAND:]

        for u in range(units):
            block(0, 0, BAND, slice(u * 128, (u + 1) * 128), True)

        _for_later_blocks(nblk, units, lambda r0, lanes: block(r0, r0 - BAND, 2 * BAND, lanes, False))
        dk_out[...] = dk_ref[...].astype(dk_out.dtype)
        dv_out[...] = dv_ref[...].astype(dv_out.dtype)

    spec = _class_spec(d)
    return pl.pallas_call(
        body, name=name, grid=(8 * d // units,),
        in_specs=[spec] * 6, out_specs=[spec] * 3,
        out_shape=[_sds((S // d, d * D), BF16)] * 3,
        scratch_shapes=[pltpu.VMEM((S // d, 128 * units), F32)] * 2,
        compiler_params=_cparams(1),
    )(q, k, v, do, lse, dd)


MIX_TILE = 256
DILATIONS = tuple(d for _, d in BRANCHES)


def _branch_weights(la, lb, lc):
    m = jnp.maximum(jnp.maximum(la, lb), lc)
    ea, eb, ec = jnp.exp(la - m), jnp.exp(lb - m), jnp.exp(lc - m)
    inv = 1.0 / (ea + eb + ec)
    return ea * inv, eb * inv, ec * inv


def _mix_operands(outs, lses):
    specs = [_class_block(d, MIX_TILE) for d in DILATIONS] * 2
    scratch = [pltpu.VMEM((NCHUNK, MIX_TILE, 128), F32)] * 4
    return list(outs) + list(lses), specs, scratch


def _mix_fwd(name, outs, lses):
    def body(o0, o1, o2, l0, l1, l2, o_ref, to1, to2, tl1, tl2):
        for blk, tmp, d in ((o1, to1, DILATIONS[1]), (o2, to2, DILATIONS[2]), (l1, tl1, DILATIONS[1]), (l2, tl2, DILATIONS[2])):
            _tokens_from_classes(blk, tmp, d, MIX_TILE)
        for c in range(NCHUNK):
            wa, wb, wc = _branch_weights(l0[:, _chunk(c)], tl1[c], tl2[c])
            o_ref[:, _chunk(c)] = (wa * o0[:, _chunk(c)].astype(F32) + wb * to1[c] + wc * to2[c]).astype(o_ref.dtype)

    operands, specs, scratch = _mix_operands(outs, lses)
    return pl.pallas_call(
        body, name=name, grid=(S // MIX_TILE,),
        in_specs=specs, out_specs=_row_spec(MIX_TILE), out_shape=_sds((S, D), BF16),
        scratch_shapes=scratch, compiler_params=_cparams(1),
    )(*operands)


def _head_sum(x, ones_blockdiag):
    hi = x.astype(BF16)
    lo = (x - hi.astype(F32)).astype(BF16)
    return _dot_nn(hi, ones_blockdiag) + _dot_nn(lo, ones_blockdiag)


def _mix_bwd(name, do, outs, lses, ones_blockdiag):
    def body(do_ref, o0, o1, o2, l0, l1, l2, ones_ref, d0, d1, d2, t0, t1, t2,
             to1, to2, tl1, tl2, td1, td2, tt1, tt2):
        for blk, tmp, d in ((o1, to1, DILATIONS[1]), (o2, to2, DILATIONS[2]), (l1, tl1, DILATIONS[1]), (l2, tl2, DILATIONS[2])):
            _tokens_from_classes(blk, tmp, d, MIX_TILE)
        ones = ones_ref[...]
        for c in range(NCHUNK):
            w = _branch_weights(l0[:, _chunk(c)], tl1[c], tl2[c])
            dov = do_ref[:, _chunk(c)]
            o = w[0] * o0[:, _chunk(c)].astype(F32) + w[1] * to1[c] + w[2] * to2[c]
            t = _head_sum(dov * o, ones)
            d0[:, _chunk(c)] = (w[0] * dov).astype(d0.dtype)
            t0[:, _chunk(c)] = w[0] * t
            td1[c], tt1[c] = w[1] * dov, w[1] * t
            td2[c], tt2[c] = w[2] * dov, w[2] * t
        for tmp, blk, d in ((td1, d1, DILATIONS[1]), (tt1, t1, DILATIONS[1]), (td2, d2, DILATIONS[2]), (tt2, t2, DILATIONS[2])):
            _classes_from_tokens(tmp, blk, d, MIX_TILE)

    operands, specs, scratch = _mix_operands(outs, lses)
    out_specs = [_class_block(d, MIX_TILE) for d in DILATIONS] * 2
    out_shape = [_sds((S // d, d * D), BF16) for d in DILATIONS] + [_sds((S // d, d * D), F32) for d in DILATIONS]
    return pl.pallas_call(
        body, name=name, grid=(S // MIX_TILE,),
        in_specs=[_row_spec(MIX_TILE)] + specs + [_vec_spec(128, 128)],
        out_specs=out_specs, out_shape=out_shape,
        scratch_shapes=scratch + [pltpu.VMEM((NCHUNK, MIX_TILE, 128), F32)] * 4,
        compiler_params=_cparams(1),
    )(do, *operands, ones_blockdiag)


def _attn_bwd_post(name, grads, cos_t, sin_t):
    tm = MIX_TILE
    scale = HEAD_DIM ** -0.5

    def unrope(x, cs, sn):
        return x * cs - _swap_halves(x) * sn

    def body(*refs):
        in_refs = refs[:9]
        cos_ref, sin_ref, dq_ref, dkv_ref, tmp_ref = refs[9:]
        cs = cos_ref[...]
        sn = sin_ref[...]
        for g, d in enumerate(DILATIONS):
            for which, blk in enumerate(in_refs[3 * g:3 * g + 3]):
                if d > 1:
                    _tokens_from_classes(blk, tmp_ref, d, tm)
                for c in range(NCHUNK):
                    x = tmp_ref[c] if d > 1 else blk[:, _chunk(c)].astype(F32)
                    if which == 0:
                        dq_ref[:, _chunk(c, g * D)] = (unrope(x, cs, sn) * scale).astype(dq_ref.dtype)
                    elif which == 1:
                        dkv_ref[:, _chunk(c, g * D)] = unrope(x, cs, sn).astype(dkv_ref.dtype)
                    else:
                        dkv_ref[:, _chunk(c, QW + g * D)] = x.astype(dkv_ref.dtype)

    operands = [a for branch in grads for a in branch]
    tab = pl.BlockSpec((tm, 128), lambda i: (i, 0))
    return pl.pallas_call(
        body, name=name, grid=(S // tm,),
        in_specs=[_class_block(d, tm) for d in DILATIONS for _ in range(3)] + [tab, tab],
        out_specs=[pl.BlockSpec((tm, QW), lambda i: (i, 0)), pl.BlockSpec((tm, 2 * QW), lambda i: (i, 0))],
        out_shape=[_sds((S, QW), BF16), _sds((S, 2 * QW), BF16)],
        scratch_shapes=[pltpu.VMEM((NCHUNK, tm, 128), F32)],
        compiler_params=_cparams(1),
    )(*operands, cos_t, sin_t)


def _adamw(name, parts, w, m, v, layer=None, other=None):
    n, rows, cols = parts.shape
    tr = rows
    for cand in (256, 176, 128, 64, 32, 16, 8):
        if rows % cand == 0:
            tr = cand
            break
    n_other = 0 if other is None else len(other)

    def body(p_ref, w_ref, m_ref, v_ref, *refs):
        g_ref, d_ref, nm_ref, nv_ref = refs[n_other:]
        g = p_ref[0].astype(F32)
        for j in range(1, n):
            g = g + p_ref[j].astype(F32)
        g_ref[...] = g
        d_ref[...], nm_ref[...], nv_ref[...] = _adam_update(g, w_ref[...], m_ref[...], v_ref[...])

    if layer is None:
        blk = pl.BlockSpec((tr, cols), lambda i: (i, 0))
        shape = (rows, cols)
    else:
        blk = pl.BlockSpec((None, tr, cols), lambda i: (layer, i, 0))
        shape = w.shape
    return pl.pallas_call(
        body, name=name, grid=(rows // tr,),
        in_specs=[pl.BlockSpec((n, tr, cols), lambda i: (0, i, 0)), blk, blk, blk]
                 + [pl.BlockSpec(memory_space=pl.ANY)] * n_other,
        out_specs=[blk] * 4, out_shape=[_sds(shape, F32)] * 4,
        input_output_aliases={4 + i: i for i in range(n_other)},
        compiler_params=_cparams(1),
    )(parts, w, m, v, *(other or ()))


def _adam_update(g, w, m, v):
    c1 = 1.0 / (1.0 - ADAM_B1 ** ADAM_STEP)
    c2 = 1.0 / (1.0 - ADAM_B2 ** ADAM_STEP)
    nm = ADAM_B1 * m + (1.0 - ADAM_B1) * g
    nv = ADAM_B2 * v + (1.0 - ADAM_B2) * (g * g)
    return -ADAM_LR * ((nm * c1) / (jnp.sqrt(nv * c2) + ADAM_EPS) + ADAM_WD * w), nm, nv


GAIN_ROWS = 16


def _pack_small(name, gain_tiles, taps, sq):
    ng = len(gain_tiles)

    def body(*refs):
        o_ref = refs[-1]
        o_ref[...] = jnp.zeros_like(o_ref)
        for i in range(ng):
            o_ref[i:i + 1, :] = refs[i][0:1, :]
        o_ref[ng:ng + 3, :] = refs[ng][0:3, :]
        o_ref[ng + 3:ng + 4, :] = refs[ng + 1][...]

    return pl.pallas_call(body, name=name, out_shape=_sds((GAIN_ROWS, D), F32))(*gain_tiles, taps, sq)


def _adamw_gains(name, parts, params):
    np_ = len(params)
    shapes = [w.shape for w, _, _ in params]

    def body(p_ref, *refs):
        ins, outs = refs[:3 * np_], refs[3 * np_:]

        def total(lo, rows):
            g = p_ref[0, lo:lo + rows, :]
            for j in range(1, NDEV):
                g = g + p_ref[j, lo:lo + rows, :]
            return g

        lo = 0
        for i, shape in enumerate(shapes):
            g = total(lo, shape[0])
            lo += shape[0]
            w_ref, m_ref, v_ref = ins[3 * i:3 * i + 3]
            g_ref, d_ref, nm_ref, nv_ref = outs[4 * i:4 * i + 4]
            g_ref[...] = g
            d_ref[...], nm_ref[...], nv_ref[...] = _adam_update(g, w_ref[...], m_ref[...], v_ref[...])
        taps_ref, loss_ref = outs[-2], outs[-1]
        taps_ref[...] = jnp.zeros_like(taps_ref)
        taps_ref[0:3, :] = total(lo, 3)
        loss_ref[...] = jnp.sum(total(lo + 3, 1), axis=-1, keepdims=True) * (0.5 / D)

    out_shape = [_sds(shape, F32) for shape in shapes for _ in range(4)] + [_sds((8, D), F32), _sds((1, 1), F32)]
    outs = pl.pallas_call(body, name=name, out_shape=out_shape)(parts, *[a for p in params for a in p])
    return [list(outs[4 * i:4 * i + 4]) for i in range(np_)], outs[-2], outs[-1].reshape(())


def _exchange(name, arrays, kind, after):
    n = len(arrays)
    gather = kind == "gather"
    out_shape = [_sds((NDEV,) + a.shape if gather else a.shape, a.dtype) for a in arrays]

    def body(*refs):
        srcs, outs = refs[:n], refs[n + 1:2 * n + 1]
        send_sems, recv_sems, local_sems = refs[2 * n + 1:]
        x, y, c = lax.axis_index("x"), lax.axis_index("y"), lax.axis_index("c")
        me = 4 * x + 2 * y + c
        pending = []
        for t in range(n):
            own = pltpu.make_async_copy(srcs[t] if gather else srcs[t].at[me], outs[t].at[me], local_sems.at[t])
            own.start()
            pending.append(own)
            for rel in range(1, NDEV):
                px = 1 - x if rel & 4 else x
                py = 1 - y if rel & 2 else y
                pc = 1 - c if rel & 1 else c
                peer = 4 * px + 2 * py + pc
                send = pltpu.make_async_remote_copy(
                    src_ref=srcs[t] if gather else srcs[t].at[peer], dst_ref=outs[t].at[me],
                    send_sem=send_sems.at[t, rel - 1], recv_sem=recv_sems.at[t, rel - 1],
                    device_id=(px, py, pc), device_id_type=MESH)
                send.start()
                arrive = pltpu.make_async_remote_copy(
                    src_ref=srcs[t] if gather else srcs[t].at[me], dst_ref=outs[t].at[peer],
                    send_sem=send_sems.at[t, rel - 1], recv_sem=recv_sems.at[t, rel - 1],
                    device_id=(px, py, pc), device_id_type=MESH)
                pending.append((send, arrive))
        for item in pending:
            if isinstance(item, tuple):
                item[0].wait_send()
                item[1].wait_recv()
            else:
                item.wait()

    any_spec = pl.BlockSpec(memory_space=pl.ANY)
    outs = pl.pallas_call(
        body, name=name,
        in_specs=[any_spec] * (n + 1), out_specs=[any_spec] * n, out_shape=out_shape,
        scratch_shapes=[pltpu.SemaphoreType.DMA((n, NDEV - 1)), pltpu.SemaphoreType.DMA((n, NDEV - 1)),
                        pltpu.SemaphoreType.DMA((n,))],
    )(*arrays, after)
    return list(outs)


_HBM_SPEC = pl.BlockSpec(memory_space=pltpu.HBM)
_SEM_SPEC = pl.BlockSpec(memory_space=pltpu.SEMAPHORE)
_DATAFLOW = pltpu.SideEffectType.DATAFLOW_SIDE_EFFECTING


def _peers():
    x, y, c = lax.axis_index("x"), lax.axis_index("y"), lax.axis_index("c")
    out = []
    for rel in range(1, NDEV):
        px = 1 - x if rel & 4 else x
        py = 1 - y if rel & 2 else y
        pc = 1 - c if rel & 1 else c
        out.append((rel - 1, (px, py, pc), 4 * px + 2 * py + pc))
    return 4 * x + 2 * y + c, out


def _hbm(a):
    return pltpu.HBM(a.shape, a.dtype)


def _own_slot(a, me, kind):
    mine = a[None] if kind == "gather" else lax.dynamic_slice_in_dim(a, me, 1, axis=0)
    shape = (NDEV,) + mine.shape[1:]
    return lax.dynamic_update_slice_in_dim(lax.empty(shape, a.dtype), mine, me, axis=0)


def _exchange_start(name, arrays, me, kind):
    n = len(arrays)
    gather = kind == "gather"
    lands = [_own_slot(a, me, kind) for a in arrays]

    def body(*refs):
        src_refs, land_refs = refs[:n], refs[n:2 * n]
        send_sems, recv_sems = refs[2 * n], refs[2 * n + 1]
        token = refs[-1]
        my_block, peers = _peers()
        for t in range(n):
            for slot, dev, block in peers:
                pltpu.make_async_remote_copy(
                    src_ref=src_refs[t] if gather else src_refs[t].at[block], dst_ref=land_refs[t].at[my_block],
                    send_sem=send_sems.at[t * (NDEV - 1) + slot], recv_sem=recv_sems.at[t * (NDEV - 1) + slot],
                    device_id=dev, device_id_type=MESH).start()
        token[...] = jnp.zeros_like(token)

    operands = [pltpu.with_memory_space_constraint(a, pltpu.HBM) for a in list(arrays) + lands]
    outs = pl.pallas_call(
        body, name=name,
        out_shape=(pltpu.SemaphoreType.DMA((n * (NDEV - 1),)), pltpu.SemaphoreType.DMA((n * (NDEV - 1),)),
                   *[_hbm(a) for a in operands], _sds((8, 128), F32)),
        in_specs=[_HBM_SPEC] * (2 * n),
        out_specs=(_SEM_SPEC, _SEM_SPEC, *[_HBM_SPEC] * (2 * n), pl.BlockSpec(memory_space=pltpu.VMEM)),
        input_output_aliases={i: 2 + i for i in range(2 * n)},
        compiler_params=pltpu.CompilerParams(has_side_effects=_DATAFLOW),
    )(*operands)
    return (outs[0], outs[1], list(outs[2:2 + n]), list(outs[2 + n:2 + 2 * n])), outs[-1]


def _exchange_wait(name, started, t, after, kind):
    send_sems, recv_sems, srcs, lands = started
    gather = kind == "gather"

    def body(src_ref, land_ref, send_ref, recv_ref, after_ref, src_out, land_out):
        _, peers = _peers()
        for slot, dev, block in peers:
            copy = pltpu.make_async_remote_copy(
                src_ref=src_ref if gather else src_ref.at[block], dst_ref=land_ref.at[block],
                send_sem=send_ref.at[t * (NDEV - 1) + slot], recv_sem=recv_ref.at[t * (NDEV - 1) + slot],
                device_id=dev, device_id_type=MESH)
            copy.wait_send()
            copy.wait_recv()

    return pl.pallas_call(
        body, name=name, out_shape=(_hbm(srcs[t]), _hbm(lands[t])),
        in_specs=(_HBM_SPEC, _HBM_SPEC, _SEM_SPEC, _SEM_SPEC, pl.BlockSpec(memory_space=pl.ANY)),
        out_specs=(_HBM_SPEC, _HBM_SPEC), input_output_aliases={0: 0, 1: 1},
        compiler_params=pltpu.CompilerParams(has_side_effects=_DATAFLOW),
    )(srcs[t], lands[t], send_sems, recv_sems, after)[1]


DIRECT_RELS = (1, 2, 4, 6)
RELAY_RELS = (2, 4, 6)


def _rel_peer(rel):
    x, y, c = lax.axis_index("x"), lax.axis_index("y"), lax.axis_index("c")
    px = 1 - x if rel & 4 else x
    py = 1 - y if rel & 2 else y
    pc = 1 - c if rel & 1 else c
    return (px, py, pc), 4 * px + 2 * py + pc


def _gather_start(name, shards, me):
    n, nr = len(shards), len(DIRECT_RELS)
    lands = [_own_slot(a, me, "gather") for a in shards]

    def body(*refs):
        src_refs, land_refs = refs[:n], refs[n:2 * n]
        send_sems, recv_sems = refs[2 * n], refs[2 * n + 1]
        _, my_block = _rel_peer(0)
        for t in range(n):
            for s, rel in enumerate(DIRECT_RELS):
                dev, _ = _rel_peer(rel)
                pltpu.make_async_remote_copy(
                    src_ref=src_refs[t], dst_ref=land_refs[t].at[my_block],
                    send_sem=send_sems.at[t * nr + s], recv_sem=recv_sems.at[t * nr + s],
                    device_id=dev, device_id_type=MESH).start()

    operands = [pltpu.with_memory_space_constraint(a, pltpu.HBM) for a in list(shards) + lands]
    outs = pl.pallas_call(
        body, name=name,
        out_shape=(pltpu.SemaphoreType.DMA((n * nr,)), pltpu.SemaphoreType.DMA((n * nr,)), *[_hbm(a) for a in operands]),
        in_specs=[_HBM_SPEC] * (2 * n), out_specs=(_SEM_SPEC, _SEM_SPEC, *[_HBM_SPEC] * (2 * n)),
        input_output_aliases={i: 2 + i for i in range(2 * n)},
        compiler_params=pltpu.CompilerParams(has_side_effects=_DATAFLOW),
    )(*operands)
    return outs[0], outs[1], list(outs[2:2 + n]), list(outs[2 + n:2 + 2 * n])


def _gather_wait(name, started, ts, after):
    send_sems, recv_sems, srcs, lands = started
    m, nr = len(ts), len(DIRECT_RELS)

    def body(*refs):
        src_refs, land_refs = refs[:m], refs[m:2 * m]
        send_ref, recv_ref = refs[2 * m], refs[2 * m + 1]
        for i, t in enumerate(ts):
            for s, rel in enumerate(DIRECT_RELS):
                dev, block = _rel_peer(rel)
                copy = pltpu.make_async_remote_copy(
                    src_ref=src_refs[i], dst_ref=land_refs[i].at[block],
                    send_sem=send_ref.at[t * nr + s], recv_sem=recv_ref.at[t * nr + s],
                    device_id=dev, device_id_type=MESH)
                copy.wait_send()
                copy.wait_recv()

    operands = [srcs[t] for t in ts] + [lands[t] for t in ts]
    outs = pl.pallas_call(
        body, name=name, out_shape=tuple(_hbm(a) for a in operands),
        in_specs=[_HBM_SPEC] * (2 * m) + [_SEM_SPEC, _SEM_SPEC, pl.BlockSpec(memory_space=pl.ANY)],
        out_specs=tuple([_HBM_SPEC] * (2 * m)), input_output_aliases={i: i for i in range(2 * m)},
        compiler_params=pltpu.CompilerParams(has_side_effects=_DATAFLOW),
    )(*operands, send_sems, recv_sems, after)
    return list(outs[m:])


def _relay_start(name, lands):
    m, nr = len(lands), len(RELAY_RELS)

    def body(*refs):
        land_refs, send_sems, recv_sems = refs[:m], refs[m], refs[m + 1]
        sibling, _ = _rel_peer(1)
        for i in range(m):
            for s, rel in enumerate(RELAY_RELS):
                _, block = _rel_peer(rel)
                pltpu.make_async_remote_copy(
                    src_ref=land_refs[i].at[block], dst_ref=land_refs[i].at[block],
                    send_sem=send_sems.at[i * nr + s], recv_sem=recv_sems.at[i * nr + s],
                    device_id=sibling, device_id_type=MESH).start()

    outs = pl.pallas_call(
        body, name=name,
        out_shape=(pltpu.SemaphoreType.DMA((m * nr,)), pltpu.SemaphoreType.DMA((m * nr,)), *[_hbm(a) for a in lands]),
        in_specs=[_HBM_SPEC] * m, out_specs=(_SEM_SPEC, _SEM_SPEC, *[_HBM_SPEC] * m),
        input_output_aliases={i: 2 + i for i in range(m)},
        compiler_params=pltpu.CompilerParams(has_side_effects=_DATAFLOW),
    )(*lands)
    return outs[0], outs[1], list(outs[2:])


def _relay_wait(name, relayed, after):
    send_sems, recv_sems, lands = relayed
    m, nr = len(lands), len(RELAY_RELS)

    def body(*refs):
        land_refs, send_ref, recv_ref = refs[:m], refs[m], refs[m + 1]
        sibling, _ = _rel_peer(1)
        for i in range(m):
            for s, rel in enumerate(RELAY_RELS):
                _, sent = _rel_peer(rel)
                _, arriving = _rel_peer(rel ^ 1)
                copy = pltpu.make_async_remote_copy(
                    src_ref=land_refs[i].at[sent], dst_ref=land_refs[i].at[arriving],
                    send_sem=send_ref.at[i * nr + s], recv_sem=recv_ref.at[i * nr + s],
                    device_id=sibling, device_id_type=MESH)
                copy.wait_send()
                copy.wait_recv()

    outs = pl.pallas_call(
        body, name=name, out_shape=tuple(_hbm(a) for a in lands),
        in_specs=[_HBM_SPEC] * m + [_SEM_SPEC, _SEM_SPEC, pl.BlockSpec(memory_space=pl.ANY)],
        out_specs=tuple([_HBM_SPEC] * m), input_output_aliases={i: i for i in range(m)},
        compiler_params=pltpu.CompilerParams(has_side_effects=_DATAFLOW),
    )(*lands, send_sems, recv_sems, after)
    return list(outs)


def _ffn_fwd(tag, n, wg, wd):
    gu, act = _gate_up_act(f"ffn_gate_up_{tag}", n, wg)
    wd4 = wd.reshape(NFB, FB, D)
    f = _fwd_kblocked(f"ffn_down_{tag}", act, wd4)
    return (n, gu, act, wg, wd4), f


def _ffn_bwd(tag, dh_out, df, h_in, saved, g_pre, send, mixer):
    n, gu, act, wg, wd4 = saved
    dwd = _bwd_w_kblocked(f"ffn_down_dw_{tag}", act, df).reshape(NDEV, DFF // NDEV, D)
    dgu = _down_dx_act_bwd(f"ffn_down_dx_{tag}", df, wd4, gu).reshape(NDEV, S, FB)
    tok = send({f"down_{tag}": dwd, f"gate_up_{tag}": _bwd_w_cols_blocked(f"ffn_gate_up_dw_{tag}", n, dgu)})
    dn = _bwd_x_cols_blocked(f"ffn_gate_up_dx_{tag}", dgu, wg, after=tok)
    dh_in, (dg_pre,), dy, dg_mixer = _rms_bwd(f"ffn_prenorm_bwd_{tag}", h_in, [(g_pre, dn)], dh_out, F32, then=mixer)
    return dh_in, dg_pre, dy, dg_mixer


def kernel(x, positions, mix_norm_pre, mix_norm_post, ffn_norm_pre, ffn_norm_post, ffn_w_gate_up, ffn_w_down, conv_w_in, conv_w, conv_w_out, kv_norm, w_kv, w_q, w_o, loss_target, m_mix_norm_pre, m_mix_norm_post, m_ffn_norm_pre, m_ffn_norm_post, m_ffn_w_gate_up, m_ffn_w_down, m_conv_w_in, m_conv_w, m_conv_w_out, m_kv_norm, m_w_kv, m_w_q, m_w_o, v_mix_norm_pre, v_mix_norm_post, v_ffn_norm_pre, v_ffn_norm_post, v_ffn_w_gate_up, v_ffn_w_down, v_conv_w_in, v_conv_w, v_conv_w_out, v_kv_norm, v_w_kv, v_w_q, v_w_o):
    me = 4 * lax.axis_index("x") + 2 * lax.axis_index("y") + lax.axis_index("c")
    h0 = x.reshape(S, D)
    target = loss_target.reshape(S, D)
    row = lambda a, l: a[l].reshape(1, D)
    g_kv = kv_norm.reshape(1, D)

    cw_shard = jnp.pad(conv_w[0], ((0, 5), (0, 0)))
    names = ["conv_in", "conv_w", "conv_out", "gate_up_0", "down_0", "kv", "q", "o", "gate_up_1", "down_1"]
    shards = [conv_w_in[0], cw_shard, conv_w_out[0], ffn_w_gate_up[0], ffn_w_down[0],
              w_kv, w_q[0], w_o[0], ffn_w_gate_up[1], ffn_w_down[1]]
    shards = [s if n == "conv_w" else s.astype(BF16) for n, s in zip(names, shards)]
    first = 3
    gather_first = _gather_start("gather_start_conv", shards[:first], me)
    gather_rest = _gather_start("gather_start_rest", shards[first:], me)

    def direct(group, after):
        ts = [names.index(n) for n in group]
        started, ts = (gather_first, ts) if ts[0] < first else (gather_rest, [t - first for t in ts])
        lands = _gather_wait(f"gather_wait_{group[0]}", started, ts, after)
        return _relay_start(f"relay_start_{group[0]}", lands)

    def finish(group, relayed, after):
        return dict(zip(group, _relay_wait(f"relay_wait_{group[0]}", relayed, after)))

    sent = {}

    def send(grads):
        started, token = _exchange_start(f"scatter_start_{next(iter(grads))}", list(grads.values()), me, "scatter")
        for i, name in enumerate(grads):
            sent[name] = (started, i)
        return token

    groups = [["conv_in", "conv_w", "conv_out"], ["gate_up_0", "down_0"], ["kv", "q"], ["o", "gate_up_1", "down_1"]]
    n0 = _rms_fwd("mix_prenorm_0", h0, [row(mix_norm_pre, 0)])[0]
    half = HEAD_DIM // 2
    inv_freq = ROPE_THETA ** (-jnp.arange(half, dtype=F32) / half)
    tables = _rope_tables("rope_tables", positions.reshape(S, 1), jnp.tile(inv_freq, 4).reshape(1, 128))
    w = finish(groups[0], direct(groups[0], tables[0]), n0)
    win = w["conv_in"].transpose(1, 0, 2).reshape(D, 3 * D)
    cw = w["conv_w"].transpose(1, 0, 2).reshape(8, D)
    wout = w["conv_out"].reshape(D, D)
    z = _fwd_rows("conv_in", n0, win, BF16)
    pre = _conv_fwd("conv_gate", z, cw)
    relayed = direct(groups[1], pre)
    y0 = _fwd_rows("conv_out", pre, wout)
    h1, (n1,) = _resid_rms("mix_postnorm_0", h0, y0, row(mix_norm_post, 0), [row(ffn_norm_pre, 0)])
    w = finish(groups[1], relayed, n1)
    ffn0, f0 = _ffn_fwd("0", n1, w["gate_up_0"], w["down_0"])
    relayed = direct(groups[2], ffn0[2])
    h2, (nk, n2) = _resid_rms("ffn_postnorm_0", h1, f0, row(ffn_norm_post, 0), [g_kv, row(mix_norm_pre, 1)])

    w = finish(groups[2], relayed, nk)
    wkv = w["kv"].transpose(1, 0, 2).reshape(D, 2 * QW)
    wq = w["q"].transpose(1, 0, 2).reshape(D, QW)
    qc, kc, vc, o_c, lse_c = [], [], [], [], []
    for g, d in enumerate(DILATIONS):
        q_g, k_g, v_g = _qkv_classes(f"qkv_proj_{g}", n2, nk, wq, wkv, g, d, tables)
        qc.append(q_g)
        kc.append(k_g)
        vc.append(v_g)
    relayed = direct(groups[3], vc[-1])
    for g, d in enumerate(DILATIONS):
        o_g, lse_g = _attn_fwd(f"attn_fwd_{g}", qc[g], kc[g], vc[g], d)
        o_c.append(o_g)
        lse_c.append(lse_g)
    o_mix = _mix_fwd("attn_mix", o_c, lse_c)
    w = finish(groups[3], relayed, o_mix)
    wo = w["o"].reshape(D, D)
    y1 = _fwd_rows("attn_out", o_mix, wo)
    h3, (n3,) = _resid_rms("mix_postnorm_1", h2, y1, row(mix_norm_post, 1), [row(ffn_norm_pre, 1)])
    ffn1, f1 = _ffn_fwd("1", n3, w["gate_up_1"], w["down_1"])

    dh4, df1, dg_fpost1, sq = _resid_rms_loss("ffn_postnorm_1_loss", h3, f1, row(ffn_norm_post, 1), target)

    dh3, dg_fpre1, dy1, dg_mpost1 = _ffn_bwd(
        "1", dh4, df1, h3, ffn1, row(ffn_norm_pre, 1), send, (y1, row(mix_norm_post, 1)))
    dwo = _bwd_w_rows("attn_out_dw", o_mix, dy1).reshape(NDEV, D // NDEV, D)
    do = _bwd_x_rows("attn_out_dx", dy1, wo, BF16)
    lane = jnp.arange(128)
    ones_blockdiag = (lane[:, None] // HEAD_DIM == lane[None, :] // HEAD_DIM).astype(BF16)
    mixed = _mix_bwd("attn_mix_bwd", do, o_c, lse_c, ones_blockdiag)
    branch_grads = [_attn_bwd(f"attn_bwd_{g}", qc[g], kc[g], vc[g], mixed[g], lse_c[g], mixed[3 + g], d)
                    for g, d in enumerate(DILATIONS)]
    dq_raw, dkv = _attn_bwd_post("attn_bwd_post", branch_grads, *tables)
    tok = send({"o": dwo, "kv": _bwd_w_cols("kv_proj_dw", nk, dkv, 2 * QW // NDEV),
                "q": _bwd_w_cols("q_proj_dw", n2, dq_raw, QW // NDEV)})
    dnk = _bwd_x_plain("kv_proj_dx", dkv, wkv, after=tok)
    dn2 = _bwd_x_plain("q_proj_dx", dq_raw, wq)
    dh2, (dg_kv, dg_mpre1), df0, dg_fpost0 = _rms_bwd(
        "kv_and_mix_prenorm_bwd_1", h2, [(g_kv, dnk), (row(mix_norm_pre, 1), dn2)], dh3, F32,
        then=(f0, row(ffn_norm_post, 0)))

    dh1, dg_fpre0, dy0, dg_mpost0 = _ffn_bwd(
        "0", dh2, df0, h1, ffn0, row(ffn_norm_pre, 0), send, (y0, row(mix_norm_post, 0)))
    dwout = _bwd_w_rows("conv_out_dw", pre, dy0).reshape(NDEV, D // NDEV, D)
    dpre = _bwd_x_rows("conv_out_dx", dy0, wout, BF16)
    dz, dcw = _conv_bwd("conv_gate_bwd", z, dpre, cw)
    tok = send({"conv_out": dwout, "conv_in": _bwd_w_cols("conv_in_dw", n0, dz, 3 * D // NDEV)})
    dn0 = _bwd_x_plain("conv_in_dx", dz, win, after=tok)
    dh0, (dg_mpre0,) = _rms_bwd("mix_prenorm_bwd_0", h0, [(row(mix_norm_pre, 0), dn0)], dh1, F32)

    small = _pack_small("pack_small_grads", [dg_mpre0, dg_mpre1, dg_mpost0, dg_mpost1, dg_fpre0, dg_fpre1,
                                             dg_fpost0, dg_fpost1, dg_kv], dcw, sq)

    done = [small]

    def upd(tag, w, m, v):
        parts = _exchange_wait(f"scatter_wait_{tag}", *sent[tag], done[-1], "scatter")
        shape = w.shape
        flat = lambda a: a.reshape(parts.shape[1:])
        res = _adamw(f"adamw_{tag}", parts, flat(w), flat(m), flat(v))
        done.append(res[0])
        return [r.reshape(shape) for r in res]

    def upd_layer(tag, l, w, m, v, other):
        parts = _exchange_wait(f"scatter_wait_{tag}_{l}", *sent[f"{tag}_{l}"], done[-1], "scatter")
        res = _adamw(f"adamw_{tag}_{l}", parts, w, m, v, layer=l, other=other)
        done.append(res[0])
        return list(res)

    res = {}
    down_1 = upd_layer("down", 1, ffn_w_down, m_ffn_w_down, v_ffn_w_down, None)
    gate_up_t = [jnp.swapaxes(a, 1, 2) for a in (ffn_w_gate_up, m_ffn_w_gate_up, v_ffn_w_gate_up)]
    gate_up_1 = upd_layer("gate_up", 1, *gate_up_t, None)
    res["w_o"] = upd("o", w_o, m_w_o, v_w_o)
    res["w_q"] = upd("q", w_q, m_w_q, v_w_q)
    res["w_kv"] = upd("kv", w_kv, m_w_kv, v_w_kv)

    small_all = _exchange("gather_small_grads", [small], "gather", done[-1])[0]
    vec = lambda a: a.reshape(1, D)
    gain_res, taps, loss = _adamw_gains("adamw_gains", small_all, [
        (mix_norm_pre, m_mix_norm_pre, v_mix_norm_pre), (mix_norm_post, m_mix_norm_post, v_mix_norm_post),
        (ffn_norm_pre, m_ffn_norm_pre, v_ffn_norm_pre), (ffn_norm_post, m_ffn_norm_post, v_ffn_norm_post),
        (vec(kv_norm), vec(m_kv_norm), vec(v_kv_norm))])
    dcw_mine = lax.dynamic_slice(taps, (0, me * 128), (8, 128))
    pad8 = lambda a, fill: jnp.pad(a[0], ((0, 5), (0, 0)), constant_values=fill)
    cw_res = [r[0:3].reshape(1, 3, 128) for r in
              _adamw("adamw_conv_w", dcw_mine.reshape(1, 8, 128), cw_shard, pad8(m_conv_w, 0.0), pad8(v_conv_w, 1.0))]

    res.update({
        "mix_norm_pre": gain_res[0],
        "mix_norm_post": gain_res[1],
        "ffn_norm_pre": gain_res[2],
        "ffn_norm_post": gain_res[3],
        "kv_norm": [r.reshape(D) for r in gain_res[4]],
        "conv_w": cw_res,
    })
    done.append(small_all)
    res["ffn_w_down"] = upd_layer("down", 0, ffn_w_down, m_ffn_w_down, v_ffn_w_down, down_1)
    res["ffn_w_gate_up"] = [jnp.swapaxes(r, 1, 2) for r in upd_layer("gate_up", 0, *gate_up_t, gate_up_1)]
    res["conv_w_out"] = upd("conv_out", conv_w_out, m_conv_w_out, v_conv_w_out)
    res["conv_w_in"] = upd("conv_in", conv_w_in, m_conv_w_in, v_conv_w_in)
    order = ["mix_norm_pre", "mix_norm_post", "ffn_norm_pre", "ffn_norm_post", "ffn_w_gate_up", "ffn_w_down",
             "conv_w_in", "conv_w", "conv_w_out", "kv_norm", "w_kv", "w_q", "w_o"]
    out = [loss, dh0.reshape(1, S, D)]
    for i in range(4):
        out += [res[name][i] for name in order]
    return tuple(out)
```

```python
import jax
import jax.numpy as jnp
from jax import lax
from jax.experimental import pallas as pl
from jax.experimental.pallas import tpu as pltpu

F32 = jnp.float32
BF16 = jnp.bfloat16

S = 4096
D = 1024
NDEV = 8
HEAD_DIM = 64
QW = 3072
DFF = 2816
FB = 704
NFB = 4
BRANCHES = ((128, 1), (512, 4), (2048, 16))
BAND = 128
ROPE_THETA = 10000.0
RMS_EPS = 1e-6
NEG_INF = -1e30
ADAM_LR, ADAM_B1, ADAM_B2, ADAM_EPS, ADAM_WD, ADAM_STEP = 0.001, 0.9, 0.999, 1e-08, 0.01, 10

VMEM_LIMIT_BYTES = 52 * 1024 * 1024
ROW_TILE = 512
MESH = pl.DeviceIdType.MESH


def _cparams(ngrid):
    return pltpu.CompilerParams(dimension_semantics=("arbitrary",) * ngrid,
                                vmem_limit_bytes=VMEM_LIMIT_BYTES)


def _sds(shape, dtype):
    return jax.ShapeDtypeStruct(tuple(shape), dtype)


_DIMS = {"nn": (((1,), (0,)), ((), ())),
         "nt": (((1,), (1,)), ((), ())),
         "tn": (((0,), (0,)), ((), ()))}


def _matmul(name, a, b, *, mode, grid, a_blk, a_map, b_blk, b_map, o_shape, o_blk, o_map, out_dtype, after=None,
            out_groups=1):
    nk = grid[2]
    dims = _DIMS[mode]
    acc_shape = tuple(s for s in o_blk if s is not None)
    if out_groups > 1:
        acc_shape = (acc_shape[1], out_groups * acc_shape[2])
    extra = [] if after is None else [after]

    def store(o_ref, val):
        if out_groups == 1:
            o_ref[...] = val.astype(o_ref.dtype)
        else:
            n = o_ref.shape[-1]
            for grp in range(out_groups):
                o_ref[grp] = val[:, grp * n:(grp + 1) * n].astype(o_ref.dtype)

    def body(a_ref, b_ref, *rest):
        o_ref, scratch = rest[len(extra)], rest[len(extra) + 1:]
        part = lax.dot_general(a_ref[...], b_ref[...], dims, preferred_element_type=F32)
        if nk == 1:
            store(o_ref, part)
            return
        acc_ref = scratch[0]
        k = pl.program_id(2)

        @pl.when(k == 0)
        def _():
            acc_ref[...] = part

        @pl.when(k > 0)
        def _():
            acc_ref[...] += part

        @pl.when(k == nk - 1)
        def _():
            store(o_ref, acc_ref[...])

    return pl.pallas_call(
        body, name=name, grid=grid,
        in_specs=[pl.BlockSpec(a_blk, a_map), pl.BlockSpec(b_blk, b_map)] + [pl.BlockSpec(memory_space=pl.ANY)] * len(extra),
        out_specs=pl.BlockSpec(o_blk, o_map),
        out_shape=_sds(o_shape, out_dtype),
        scratch_shapes=[] if nk == 1 else [pltpu.VMEM(acc_shape, F32)],
        compiler_params=_cparams(3),
    )(a, b, *extra)


TM = 1024
TK = S


def _fwd_rows(name, a, w, out_dtype=F32):
    kdim, n = w.shape
    tn = 1024
    return _matmul(name, a, w, mode="nn", grid=(S // TM, n // tn, 1),
                   a_blk=(TM, kdim), a_map=lambda i, j, k: (i, 0),
                   b_blk=(kdim, tn), b_map=lambda i, j, k: (0, j),
                   o_shape=(S, n), o_blk=(TM, tn), o_map=lambda i, j, k: (i, j), out_dtype=out_dtype)


def _fwd_kblocked(name, a4, w4):
    nb, _, kb = a4.shape
    n = w4.shape[2]

    def body(a_ref, w_ref, o_ref):
        acc = _dot_nn(a_ref[0], w_ref[0])
        for j in range(1, nb):
            acc = acc + _dot_nn(a_ref[j], w_ref[j])
        o_ref[...] = acc

    return pl.pallas_call(
        body, name=name, grid=(S // TM,),
        in_specs=[pl.BlockSpec((nb, TM, kb), lambda i: (0, i, 0)), pl.BlockSpec((nb, kb, n), lambda i: (0, 0, 0))],
        out_specs=pl.BlockSpec((TM, n), lambda i: (i, 0)), out_shape=_sds((S, n), F32),
        compiler_params=_cparams(1),
    )(a4, w4)


def _down_loss(name, a4, w4, h, g, target):
    nb, _, kb = a4.shape
    tm = ROW_TILE

    def body(a_ref, w_ref, h_ref, g_ref, t_ref, dh_ref, df_ref, dg_ref, part_ref):
        f = _dot_nn(a_ref[0], w_ref[0])
        for j in range(1, nb):
            f = f + _dot_nn(a_ref[j], w_ref[j])
        gain = g_ref[...]
        e = h_ref[...] + (f * _rstd(f)) * gain - t_ref[...]
        dh = e * (1.0 / D)
        dh_ref[...] = dh
        step = pl.program_id(0)
        df_ref[...] = _norm_bwd_rows(f, gain, dh, dg_ref, step).astype(df_ref.dtype)
        part = jnp.sum(e * e, axis=0, keepdims=True)

        @pl.when(step == 0)
        def _():
            part_ref[...] = part

        @pl.when(step > 0)
        def _():
            part_ref[...] += part

    return pl.pallas_call(
        body, name=name, grid=(S // tm,),
        in_specs=[pl.BlockSpec((nb, tm, kb), lambda i: (0, i, 0)), pl.BlockSpec((nb, kb, D), lambda i: (0, 0, 0)),
                  _row_spec(tm), _vec_spec(), _row_spec(tm)],
        out_specs=[_row_spec(tm), _row_spec(tm), _vec_spec(8), _vec_spec()],
        out_shape=[_sds((S, D), F32), _sds((S, D), BF16), _sds((8, D), F32), _sds((1, D), F32)],
        compiler_params=_cparams(1),
    )(a4, w4, h, g, target)


def _bwd_x_cols_blocked(name, dy8, wg, after):
    _, kdim, n = wg.shape
    nk = NDEV // 2

    def body(a_ref, b_ref, after_ref, o_ref, acc_ref):
        k = pl.program_id(1)
        part = _dot_nt(a_ref[0], b_ref[0]) + _dot_nt(a_ref[1], b_ref[1])

        @pl.when(k == 0)
        def _():
            acc_ref[...] = part

        @pl.when(k > 0)
        def _():
            acc_ref[...] += part

        @pl.when(k == nk - 1)
        def _():
            o_ref[...] = acc_ref[...].astype(o_ref.dtype)

    return pl.pallas_call(
        body, name=name, grid=(S // FFN_TM, nk),
        in_specs=[pl.BlockSpec((2, None, FFN_TM, n), lambda i, k: (0, k, i, 0)),
                  pl.BlockSpec((2, None, kdim, n), lambda i, k: (0, k, 0, 0)),
                  pl.BlockSpec(memory_space=pl.ANY)],
        out_specs=pl.BlockSpec((FFN_TM, kdim), lambda i, k: (i, 0)), out_shape=_sds((S, kdim), BF16),
        scratch_shapes=[pltpu.VMEM((FFN_TM, kdim), F32)],
        compiler_params=_cparams(2),
    )(dy8.reshape(2, nk, S, n), wg.reshape(2, nk, kdim, n), after)


def _bwd_x_rows(name, dy, w, out_dtype, after=None):
    kdim, n = w.shape
    tkk = 512
    return _matmul(name, dy, w, mode="nt", grid=(S // TM, kdim // tkk, 1),
                   a_blk=(TM, n), a_map=lambda i, j, k: (i, 0),
                   b_blk=(tkk, n), b_map=lambda i, j, k: (j, 0),
                   o_shape=(S, kdim), o_blk=(TM, tkk), o_map=lambda i, j, k: (i, j), out_dtype=out_dtype, after=after)


DW_COLS = 768


def _bwd_w_cols(name, a, dy, n):
    kdim = a.shape[1]
    groups = DW_COLS // n
    return _matmul(name, a, dy, mode="tn", grid=(1, NDEV // groups, S // TK),
                   a_blk=(TK, kdim), a_map=lambda i, j, k: (k, 0),
                   b_blk=(TK, DW_COLS), b_map=lambda i, j, k: (k, j),
                   o_shape=(NDEV, kdim, n), o_blk=(groups, kdim, n) if groups > 1 else (None, kdim, n),
                   o_map=lambda i, j, k: (j, 0, 0), out_dtype=BF16, out_groups=groups)


def _bwd_x_plain(name, dy, w, after=None):
    kdim, n = w.shape
    tm = TM if n <= 3 * D else TM // 2
    return _matmul(name, dy, w, mode="nt", grid=(S // tm, 1, 1),
                   a_blk=(tm, n), a_map=lambda i, j, k: (i, 0),
                   b_blk=(kdim, n), b_map=lambda i, j, k: (0, 0),
                   o_shape=(S, kdim), o_blk=(tm, kdim), o_map=lambda i, j, k: (i, 0), out_dtype=BF16, after=after)


def _bwd_w_cols_blocked(name, a, dy8):
    kdim = a.shape[1]
    n = dy8.shape[2]
    return _matmul(name, dy8, a, mode="tn", grid=(1, NDEV, S // TK),
                   a_blk=(None, TK, n), a_map=lambda i, j, k: (j, k, 0),
                   b_blk=(TK, kdim), b_map=lambda i, j, k: (k, 0),
                   o_shape=(NDEV, n, kdim), o_blk=(None, n, kdim), o_map=lambda i, j, k: (j, 0, 0), out_dtype=BF16)


def _bwd_w_rows(name, a, dy):
    kdim = a.shape[1]
    n = dy.shape[1]
    tmm = 512
    return _matmul(name, a, dy, mode="tn", grid=(kdim // tmm, 1, S // TK),
                   a_blk=(TK, tmm), a_map=lambda i, j, k: (k, i),
                   b_blk=(TK, n), b_map=lambda i, j, k: (k, 0),
                   o_shape=(kdim, n), o_blk=(tmm, n), o_map=lambda i, j, k: (i, 0), out_dtype=BF16)


def _bwd_w_kblocked(name, a4, dy):
    nb, _, kb = a4.shape
    n = dy.shape[1]
    return _matmul(name, a4, dy, mode="tn", grid=(nb, 1, S // TK),
                   a_blk=(None, TK, kb), a_map=lambda i, j, k: (i, k, 0),
                   b_blk=(TK, n), b_map=lambda i, j, k: (k, 0),
                   o_shape=(nb, kb, n), o_blk=(None, kb, n), o_map=lambda i, j, k: (i, 0, 0), out_dtype=BF16)


def _rstd(x):
    return lax.rsqrt(jnp.mean(x * x, axis=-1, keepdims=True) + RMS_EPS)


def _row_spec(tm=ROW_TILE, width=D):
    return pl.BlockSpec((tm, width), lambda i: (i, 0))


def _vec_spec(rows=1, width=D):
    return pl.BlockSpec((rows, width), lambda i: (0, 0))


def _rms_fwd(name, x, gains):
    n = len(gains)

    def body(x_ref, *refs):
        x_val = x_ref[...]
        xh = x_val * _rstd(x_val)
        for g_ref, o_ref in zip(refs[:n], refs[n:]):
            o_ref[...] = (xh * g_ref[...]).astype(o_ref.dtype)

    outs = pl.pallas_call(
        body, name=name, grid=(S // ROW_TILE,),
        in_specs=[_row_spec()] + [_vec_spec()] * n,
        out_specs=[_row_spec()] * n,
        out_shape=[_sds((S, D), BF16)] * n,
        compiler_params=_cparams(1),
    )(x, *gains)
    return list(outs)


def _resid_rms(name, h, y, g, next_gains):
    n = len(next_gains)

    def body(h_ref, y_ref, g_ref, *refs):
        y_val = y_ref[...]
        h_new = h_ref[...] + (y_val * _rstd(y_val)) * g_ref[...]
        refs[n][...] = h_new
        hh = h_new * _rstd(h_new)
        for g2_ref, o_ref in zip(refs[:n], refs[n + 1:]):
            o_ref[...] = (hh * g2_ref[...]).astype(o_ref.dtype)

    outs = pl.pallas_call(
        body, name=name, grid=(S // ROW_TILE,),
        in_specs=[_row_spec(), _row_spec(), _vec_spec()] + [_vec_spec()] * n,
        out_specs=[_row_spec()] * (n + 1), out_shape=[_sds((S, D), F32)] + [_sds((S, D), BF16)] * n,
        compiler_params=_cparams(1),
    )(h, y, g, *next_gains)
    return outs[0], list(outs[1:])


def _resid_rms_loss(name, h, y, g, target):
    def body(h_ref, y_ref, g_ref, t_ref, dh_ref, dy_ref, dg_ref, part_ref):
        y_val = y_ref[...]
        gain = g_ref[...]
        e = h_ref[...] + (y_val * _rstd(y_val)) * gain - t_ref[...]
        dh = e * (1.0 / D)
        dh_ref[...] = dh
        step = pl.program_id(0)
        dy_ref[...] = _norm_bwd_rows(y_val, gain, dh, dg_ref, step).astype(dy_ref.dtype)
        part = jnp.sum(e * e, axis=0, keepdims=True)

        @pl.when(step == 0)
        def _():
            part_ref[...] = part

        @pl.when(step > 0)
        def _():
            part_ref[...] += part

    return pl.pallas_call(
        body, name=name, grid=(S // ROW_TILE,),
        in_specs=[_row_spec(), _row_spec(), _vec_spec(), _row_spec()],
        out_specs=[_row_spec(), _row_spec(), _vec_spec(8), _vec_spec()],
        out_shape=[_sds((S, D), F32), _sds((S, D), BF16), _sds((8, D), F32), _sds((1, D), F32)],
        compiler_params=_cparams(1),
    )(h, y, g, target)


def _norm_bwd_rows(x_val, g, dn, dg_ref, step):
    r = _rstd(x_val)
    xh = x_val * r
    dxh = dn * g
    part = jnp.sum(dn * xh, axis=0, keepdims=True)

    @pl.when(step == 0)
    def _():
        dg_ref[...] = jnp.zeros_like(dg_ref)

    dg_ref[0:1, :] += part
    return r * (dxh - xh * jnp.mean(dxh * xh, axis=-1, keepdims=True))


def _rms_bwd(name, x, pairs, dres, out_dtype, then=None):
    n = len(pairs)
    has_res = dres is not None
    chained = then is not None

    def body(x_ref, *refs):
        g_refs = refs[0:2 * n:2]
        dn_refs = refs[1:2 * n:2]
        pos = 2 * n
        res_ref = refs[pos] if has_res else None
        pos += int(has_res)
        if chained:
            y_ref, gy_ref = refs[pos], refs[pos + 1]
            pos += 2
        dx_ref = refs[pos]
        dg_refs = refs[pos + 1:pos + 1 + n]
        step = pl.program_id(0)
        x_val = x_ref[...]
        acc = res_ref[...] if has_res else jnp.zeros_like(x_val)
        for g_ref, dn_ref, dg_ref in zip(g_refs, dn_refs, dg_refs):
            acc = acc + _norm_bwd_rows(x_val, g_ref[...], dn_ref[...].astype(F32), dg_ref, step)
        dx_ref[...] = acc.astype(dx_ref.dtype)
        if chained:
            dy_ref, dgy_ref = refs[pos + 1 + n], refs[pos + 2 + n]
            dy_ref[...] = _norm_bwd_rows(y_ref[...], gy_ref[...], acc, dgy_ref, step).astype(dy_ref.dtype)

    operands = [x]
    in_specs = [_row_spec()]
    for g, dn in pairs:
        operands += [g, dn]
        in_specs += [_vec_spec(), _row_spec()]
    if has_res:
        operands.append(dres)
        in_specs.append(_row_spec())
    if chained:
        operands += [then[0], then[1]]
        in_specs += [_row_spec(), _vec_spec()]
    extra = int(chained)
    outs = pl.pallas_call(
        body, name=name, grid=(S // ROW_TILE,),
        in_specs=in_specs,
        out_specs=[_row_spec()] + [_vec_spec(8)] * n + [_row_spec(), _vec_spec(8)] * extra,
        out_shape=[_sds((S, D), out_dtype)] + [_sds((8, D), F32)] * n + [_sds((S, D), BF16), _sds((8, D), F32)] * extra,
        compiler_params=_cparams(1),
    )(*operands)
    if chained:
        return outs[0], list(outs[1:1 + n]), outs[1 + n], outs[2 + n]
    return outs[0], list(outs[1:])


def _shift_down(u, prev8, k):
    r = pltpu.roll(u, k, 0)
    p = pltpu.roll(prev8, k, 0)
    row = lax.broadcasted_iota(jnp.int32, prev8.shape, 0)
    top = jnp.where(row < k, p, r[0:8])
    return jnp.concatenate([top, r[8:]], axis=0)


def _shift_up(u, next8, k):
    tm = u.shape[0]
    r = pltpu.roll(u, tm - k, 0)
    p = pltpu.roll(next8, 8 - k, 0)
    row = lax.broadcasted_iota(jnp.int32, next8.shape, 0)
    bot = jnp.where(row >= 8 - k, p, r[tm - 8:tm])
    return jnp.concatenate([r[:tm - 8], bot], axis=0)


CONV_TILE = 512


def _halo_prev(col):
    return pl.BlockSpec((8, D), lambda i: (jnp.maximum(i * (CONV_TILE // 8) - 1, 0), col))


def _halo_next(col):
    last = S // 8 - 1
    return pl.BlockSpec((8, D), lambda i: (jnp.minimum((i + 1) * (CONV_TILE // 8), last), col))


def _conv_fwd(name, z, cw):
    def body(b_ref, c_ref, h_ref, cp_ref, hp_ref, cw_ref, o_ref):
        i = pl.program_id(0)
        u = c_ref[...].astype(F32) * h_ref[...].astype(F32)
        up = cp_ref[...].astype(F32) * hp_ref[...].astype(F32)
        up = jnp.where(i > 0, up, 0.0)
        cv = cw_ref[0:1, :] * _shift_down(u, up, 2) + cw_ref[1:2, :] * _shift_down(u, up, 1) + cw_ref[2:3, :] * u
        o_ref[...] = (b_ref[...].astype(F32) * cv).astype(o_ref.dtype)

    col = lambda c: pl.BlockSpec((CONV_TILE, D), lambda i: (i, c))
    return pl.pallas_call(
        body, name=name, grid=(S // CONV_TILE,),
        in_specs=[col(0), col(1), col(2), _halo_prev(1), _halo_prev(2), _vec_spec(8)],
        out_specs=_row_spec(CONV_TILE), out_shape=_sds((S, D), BF16),
        compiler_params=_cparams(1),
    )(z, z, z, z, z, cw)


def _conv_bwd(name, z, dpre, cw):
    nsteps = S // CONV_TILE

    def body(b_ref, c_ref, h_ref, cp_ref, hp_ref, dp_ref, dpn_ref, bn_ref, cw_ref, dz_ref, dcw_ref):
        i = pl.program_id(0)
        b = b_ref[...].astype(F32)
        c = c_ref[...].astype(F32)
        h = h_ref[...].astype(F32)
        dp = dp_ref[...].astype(F32)
        u = c * h
        up = jnp.where(i > 0, cp_ref[...].astype(F32) * hp_ref[...].astype(F32), 0.0)
        s1 = _shift_down(u, up, 1)
        s2 = _shift_down(u, up, 2)
        w0, w1, w2 = cw_ref[0:1, :], cw_ref[1:2, :], cw_ref[2:3, :]
        cv = w0 * s2 + w1 * s1 + w2 * u
        dcv = dp * b
        dcvn = jnp.where(i < nsteps - 1, dpn_ref[...].astype(F32) * bn_ref[...].astype(F32), 0.0)
        du = w2 * dcv + w1 * _shift_up(dcv, dcvn, 1) + w0 * _shift_up(dcv, dcvn, 2)
        dz_ref[:, 0:D] = (dp * cv).astype(dz_ref.dtype)
        dz_ref[:, D:2 * D] = (du * h).astype(dz_ref.dtype)
        dz_ref[:, 2 * D:3 * D] = (du * c).astype(dz_ref.dtype)

        @pl.when(i == 0)
        def _():
            dcw_ref[...] = jnp.zeros_like(dcw_ref)

        dcw_ref[0:1, :] += jnp.sum(dcv * s2, axis=0, keepdims=True)
        dcw_ref[1:2, :] += jnp.sum(dcv * s1, axis=0, keepdims=True)
        dcw_ref[2:3, :] += jnp.sum(dcv * u, axis=0, keepdims=True)

    col = lambda c: pl.BlockSpec((CONV_TILE, D), lambda i: (i, c))
    return pl.pallas_call(
        body, name=name, grid=(nsteps,),
        in_specs=[col(0), col(1), col(2), _halo_prev(1), _halo_prev(2),
                  _row_spec(CONV_TILE), _halo_next(0), _halo_next(0), _vec_spec(8)],
        out_specs=[pl.BlockSpec((CONV_TILE, 3 * D), lambda i: (i, 0)), _vec_spec(8)],
        out_shape=[_sds((S, 3 * D), BF16), _sds((8, D), F32)],
        compiler_params=_cparams(1),
    )(z, z, z, z, z, dpre, dpre, z, cw)


FFN_TM = 2048
_GU_BLOCK = pl.BlockSpec((2, None, FFN_TM, FB), lambda i, j: (0, j, i, 0))


def _gate_up_act(name, a, wg):
    kdim = a.shape[1]

    def body(a_ref, wgate_ref, wup_ref, gu_ref, act_ref):
        x = a_ref[...]
        g = _dot_nn(x, wgate_ref[...])
        u = _dot_nn(x, wup_ref[...])
        gu_ref[0] = g.astype(gu_ref.dtype)
        gu_ref[1] = u.astype(gu_ref.dtype)
        act_ref[...] = (g * jax.nn.sigmoid(g) * u).astype(act_ref.dtype)

    return pl.pallas_call(
        body, name=name, grid=(S // FFN_TM, NFB),
        in_specs=[pl.BlockSpec((FFN_TM, kdim), lambda i, j: (i, 0)),
                  pl.BlockSpec((None, kdim, FB), lambda i, j: (j, 0, 0)),
                  pl.BlockSpec((None, kdim, FB), lambda i, j: (j + NFB, 0, 0))],
        out_specs=[_GU_BLOCK, pl.BlockSpec((None, FFN_TM, FB), lambda i, j: (j, i, 0))],
        out_shape=[_sds((2, NFB, S, FB), BF16), _sds((NFB, S, FB), BF16)],
        compiler_params=_cparams(2),
    )(a, wg, wg)


def _down_dx_act_bwd(name, df, w4, gu):
    _, kb, n = w4.shape

    def body(df_ref, w_ref, gu_ref, o_ref):
        d = _dot_nt(df_ref[...], w_ref[...])
        g = gu_ref[0].astype(F32)
        u = gu_ref[1].astype(F32)
        sg = jax.nn.sigmoid(g)
        o_ref[0] = (d * u * sg * (1.0 + g * (1.0 - sg))).astype(o_ref.dtype)
        o_ref[1] = (d * g * sg).astype(o_ref.dtype)

    return pl.pallas_call(
        body, name=name, grid=(S // FFN_TM, NFB),
        in_specs=[pl.BlockSpec((FFN_TM, n), lambda i, j: (i, 0)), pl.BlockSpec((None, kb, n), lambda i, j: (j, 0, 0)),
                  _GU_BLOCK],
        out_specs=_GU_BLOCK, out_shape=_sds((2, NFB, S, FB), BF16),
        compiler_params=_cparams(2),
    )(df, w4, gu)


def _rope_tables(name, pos_col, inv_freq_row):
    def body(pos_ref, f_ref, cos_ref, sin_ref):
        ang = pos_ref[...].astype(F32) * f_ref[...]
        lane = lax.broadcasted_iota(jnp.int32, ang.shape, 1)
        s = jnp.sin(ang)
        cos_ref[...] = jnp.cos(ang)
        sin_ref[...] = jnp.where((lane % HEAD_DIM) < HEAD_DIM // 2, -s, s)

    tab = pl.BlockSpec((ROW_TILE, 128), lambda i: (i, 0))
    return pl.pallas_call(
        body, name=name, grid=(S // ROW_TILE,),
        in_specs=[pl.BlockSpec((ROW_TILE, 1), lambda i: (i, 0)), _vec_spec(1, 128)],
        out_specs=[tab, tab], out_shape=[_sds((S, 128), F32)] * 2,
        compiler_params=_cparams(1),
    )(pos_col, inv_freq_row)


def _swap_halves(t):
    lane = lax.broadcasted_iota(jnp.int32, t.shape, 1)
    first = (lane % HEAD_DIM) < HEAD_DIM // 2
    return jnp.where(first, pltpu.roll(t, 128 - HEAD_DIM // 2, 1), pltpu.roll(t, HEAD_DIM // 2, 1))


NCHUNK = D // 128


def _chunk(c, base=0):
    return slice(base + c * 128, base + (c + 1) * 128)


def _class_rows(r, d, tm):
    return pl.ds(r, tm // d, stride=d) if d > 1 else slice(None)


def _class_block(d, tm):
    return pl.BlockSpec((tm // d, d * D), lambda i: (i, 0))


def _tokens_from_classes(blk_ref, tmp_ref, d, tm):
    for r in range(d):
        for c in range(NCHUNK):
            tmp_ref[c, _class_rows(r, d, tm), :] = blk_ref[:, _chunk(c, r * D)].astype(F32)


def _classes_from_tokens(tmp_ref, blk_ref, d, tm):
    for r in range(d):
        for c in range(NCHUNK):
            blk_ref[:, _chunk(c, r * D)] = tmp_ref[c, _class_rows(r, d, tm), :].astype(blk_ref.dtype)


def _qkv_classes(name, n2, nk, wq, wkv, g, d, tables):
    def emit(acc, cos_ref, sin_ref, o_ref, tmp_ref, scale):
        for c in range(NCHUNK):
            tmp_ref[c] = acc[:, _chunk(c)]
        for r in range(d):
            rows = _class_rows(r, d, TM)
            if scale is not None:
                cs = cos_ref[rows, :]
                sn = sin_ref[rows, :]
            for c in range(NCHUNK):
                x = tmp_ref[c, rows, :]
                if scale is not None:
                    x = (x * cs + _swap_halves(x) * sn) * scale
                o_ref[:, _chunk(c, r * D)] = x.astype(o_ref.dtype)

    def body(n2_ref, nk_ref, wq_ref, wk_ref, wv_ref, cos_ref, sin_ref, q_ref, k_ref, v_ref, tmp_ref):
        emit(_dot_nn(n2_ref[...], wq_ref[...]), cos_ref, sin_ref, q_ref, tmp_ref, HEAD_DIM ** -0.5)
        x = nk_ref[...]
        emit(_dot_nn(x, wk_ref[...]), cos_ref, sin_ref, k_ref, tmp_ref, 1.0)
        emit(_dot_nn(x, wv_ref[...]), cos_ref, sin_ref, v_ref, tmp_ref, None)

    nbr = len(DILATIONS)
    act = pl.BlockSpec((TM, D), lambda i: (i, 0))
    tab = pl.BlockSpec((TM, 128), lambda i: (i, 0))
    wcol = lambda col: pl.BlockSpec((D, D), lambda i: (0, col))
    return pl.pallas_call(
        body, name=name, grid=(S // TM,),
        in_specs=[act, act, wcol(g), wcol(g), wcol(nbr + g), tab, tab],
        out_specs=[_class_block(d, TM)] * 3, out_shape=[_sds((S // d, d * D), BF16)] * 3,
        scratch_shapes=[pltpu.VMEM((NCHUNK, TM, 128), F32)],
        compiler_params=_cparams(1),
    )(n2, nk, wq, wkv, wkv, *tables)


ATTN_CHAINS = 16


def _attn_units(d):
    nblk = S // d // BAND
    return max(1, 2 * ATTN_CHAINS // nblk)


def _class_spec(d):
    return pl.BlockSpec((S // d, 128 * _attn_units(d)), lambda cb: (0, cb))


def _dot_nt(a, b):
    return lax.dot_general(a, b, _DIMS["nt"], preferred_element_type=F32)


def _dot_tn(a, b):
    return lax.dot_general(a, b, _DIMS["tn"], preferred_element_type=F32)


def _dot_nn(a, b):
    return lax.dot_general(a, b, _DIMS["nn"], preferred_element_type=F32)


def _band_mask(nkeys):
    qi = lax.broadcasted_iota(jnp.int32, (2 * BAND, nkeys), 0) % BAND
    kj = lax.broadcasted_iota(jnp.int32, (2 * BAND, nkeys), 1)
    if nkeys == BAND:
        return kj <= qi
    dist = qi + BAND - kj
    return (dist >= 0) & (dist <= BAND)


def _band_bias():
    return {n: jnp.where(_band_mask(n), 0.0, NEG_INF).astype(F32) for n in (BAND, 2 * BAND)}


def _stack_heads(x):
    row = lax.broadcasted_iota(jnp.int32, (2 * BAND, 128), 0)
    lane = lax.broadcasted_iota(jnp.int32, (2 * BAND, 128), 1)
    keep = (row < BAND) == (lane < HEAD_DIM)
    return jnp.where(keep, jnp.concatenate([x, x], axis=0), jnp.zeros((), x.dtype))


def _unstack(x2):
    first_head = lax.broadcasted_iota(jnp.int32, (BAND, 128), 1) < HEAD_DIM
    return jnp.where(first_head, x2[:BAND], x2[BAND:])


def _for_later_blocks(nblk, units, fn):
    all_lanes = [slice(u * 128, (u + 1) * 128) for u in range(units)]
    unroll = max(1, ATTN_CHAINS // units)
    trips = (nblk - 1) // unroll
    if trips > 1:
        def step(i, carry):
            for j in range(unroll):
                for lanes in all_lanes:
                    fn(pl.multiple_of((1 + i * unroll + j) * BAND, BAND), lanes)
            return carry

        lax.fori_loop(0, trips, step, 0)
    else:
        trips = 0
    for sb in range(1 + trips * unroll, nblk):
        for lanes in all_lanes:
            fn(sb * BAND, lanes)


def _attn_fwd(name, q, k, v, d):
    nblk = S // d // BAND
    units = _attn_units(d)

    def body(q_ref, k_ref, v_ref, o_ref, lse_ref):
        bias = _band_bias()

        def block(r0, k0, nkeys, lanes):
            q2 = _stack_heads(q_ref[pl.ds(r0, BAND), lanes])
            s = _dot_nt(q2, k_ref[pl.ds(k0, nkeys), lanes]) + bias[nkeys]
            m = jnp.max(s, axis=-1, keepdims=True)
            p = jnp.exp(s - m)
            l = jnp.sum(p, axis=-1, keepdims=True)
            o2 = _dot_nn(p.astype(BF16), v_ref[pl.ds(k0, nkeys), lanes])
            l_tile = _unstack(jnp.broadcast_to(l, (2 * BAND, 128)))
            m_tile = _unstack(jnp.broadcast_to(m, (2 * BAND, 128)))
            o_ref[pl.ds(r0, BAND), lanes] = (_unstack(o2) / l_tile).astype(o_ref.dtype)
            lse_ref[pl.ds(r0, BAND), lanes] = m_tile + jnp.log(l_tile)

        for u in range(units):
            block(0, 0, BAND, slice(u * 128, (u + 1) * 128))

        _for_later_blocks(nblk, units, lambda r0, lanes: block(r0, r0 - BAND, 2 * BAND, lanes))

    spec = _class_spec(d)
    return pl.pallas_call(
        body, name=name, grid=(8 * d // units,),
        in_specs=[spec] * 3, out_specs=[spec] * 2,
        out_shape=[_sds((S // d, d * D), BF16), _sds((S // d, d * D), F32)],
        compiler_params=_cparams(1),
    )(q, k, v)


def _attn_bwd(name, q, k, v, do, lse, dd, d):
    nblk = S // d // BAND
    units = _attn_units(d)

    def body(q_ref, k_ref, v_ref, do_ref, lse_ref, dd_ref, dq_ref, dk_out, dv_out, dk_ref, dv_ref):
        bias = _band_bias()
        def column(ref, r0, lanes, nkeys):
            tile = ref[pl.ds(r0, BAND), lanes]
            other = pltpu.roll(tile, HEAD_DIM, 1)
            first_head = lax.broadcasted_iota(jnp.int32, tile.shape, 1) < HEAD_DIM
            both = jnp.concatenate([jnp.where(first_head, tile, other), jnp.where(first_head, other, tile)], axis=0)
            return both if nkeys == BAND else jnp.concatenate([both, both], axis=1)

        def block(r0, k0, nkeys, lanes, first):
            q2 = _stack_heads(q_ref[pl.ds(r0, BAND), lanes])
            do2 = _stack_heads(do_ref[pl.ds(r0, BAND), lanes])
            kk = k_ref[pl.ds(k0, nkeys), lanes]
            vv = v_ref[pl.ds(k0, nkeys), lanes]
            s = _dot_nt(q2, kk) + bias[nkeys]
            p = jnp.exp(s - column(lse_ref, r0, lanes, nkeys))
            ds = (p * (_dot_nt(do2, vv) - column(dd_ref, r0, lanes, nkeys))).astype(BF16)
            dq_ref[pl.ds(r0, BAND), lanes] = _unstack(_dot_nn(ds, kk)).astype(dq_ref.dtype)
            dk_part = _dot_tn(ds, q2)
            dv_part = _dot_tn(p.astype(BF16), do2)
            if first:
                dk_ref[pl.ds(k0, nkeys), lanes] = dk_part
                dv_ref[pl.ds(k0, nkeys), lanes] = dv_part
            else:
                dk_ref[pl.ds(k0, BAND), lanes] += dk_part[:BAND]
                dv_ref[pl.ds(k0, BAND), lanes] += dv_part[:BAND]
                dk_ref[pl.ds(k0 + BAND, BAND), lanes] = dk_part[BAND:]
                dv_ref[pl.ds(k0 + BAND, BAND), lanes] = dv_part[BAND:]

        for u in range(units):
            block(0, 0, BAND, slice(u * 128, (u + 1) * 128), True)

        _for_later_blocks(nblk, units, lambda r0, lanes: block(r0, r0 - BAND, 2 * BAND, lanes, False))
        dk_out[...] = dk_ref[...].astype(dk_out.dtype)
        dv_out[...] = dv_ref[...].astype(dv_out.dtype)

    spec = _class_spec(d)
    return pl.pallas_call(
        body, name=name, grid=(8 * d // units,),
        in_specs=[spec] * 6, out_specs=[spec] * 3,
        out_shape=[_sds((S // d, d * D), BF16)] * 3,
        scratch_shapes=[pltpu.VMEM((S // d, 128 * units), F32)] * 2,
        compiler_params=_cparams(1),
    )(q, k, v, do, lse, dd)


MIX_TILE = 256
DILATIONS = tuple(d for _, d in BRANCHES)


def _branch_weights(la, lb, lc):
    m = jnp.maximum(jnp.maximum(la, lb), lc)
    ea, eb, ec = jnp.exp(la - m), jnp.exp(lb - m), jnp.exp(lc - m)
    inv = 1.0 / (ea + eb + ec)
    return ea * inv, eb * inv, ec * inv


def _mix_operands(outs, lses):
    specs = [_class_block(d, MIX_TILE) for d in DILATIONS] * 2
    scratch = [pltpu.VMEM((NCHUNK, MIX_TILE, 128), F32)] * 4
    return list(outs) + list(lses), specs, scratch


def _mix_fwd(name, outs, lses):
    def body(o0, o1, o2, l0, l1, l2, o_ref, to1, to2, tl1, tl2):
        for blk, tmp, d in ((o1, to1, DILATIONS[1]), (o2, to2, DILATIONS[2]), (l1, tl1, DILATIONS[1]), (l2, tl2, DILATIONS[2])):
            _tokens_from_classes(blk, tmp, d, MIX_TILE)
        for c in range(NCHUNK):
            wa, wb, wc = _branch_weights(l0[:, _chunk(c)], tl1[c], tl2[c])
            o_ref[:, _chunk(c)] = (wa * o0[:, _chunk(c)].astype(F32) + wb * to1[c] + wc * to2[c]).astype(o_ref.dtype)

    operands, specs, scratch = _mix_operands(outs, lses)
    return pl.pallas_call(
        body, name=name, grid=(S // MIX_TILE,),
        in_specs=specs, out_specs=_row_spec(MIX_TILE), out_shape=_sds((S, D), BF16),
        scratch_shapes=scratch, compiler_params=_cparams(1),
    )(*operands)


def _head_sum(x, ones_blockdiag):
    hi = x.astype(BF16)
    lo = (x - hi.astype(F32)).astype(BF16)
    return _dot_nn(hi, ones_blockdiag) + _dot_nn(lo, ones_blockdiag)


def _mix_bwd(name, do, outs, lses, ones_blockdiag):
    def body(do_ref, o0, o1, o2, l0, l1, l2, ones_ref, d0, d1, d2, t0, t1, t2,
             to1, to2, tl1, tl2, td1, td2, tt1, tt2):
        for blk, tmp, d in ((o1, to1, DILATIONS[1]), (o2, to2, DILATIONS[2]), (l1, tl1, DILATIONS[1]), (l2, tl2, DILATIONS[2])):
            _tokens_from_classes(blk, tmp, d, MIX_TILE)
        ones = ones_ref[...]
        for c in range(NCHUNK):
            w = _branch_weights(l0[:, _chunk(c)], tl1[c], tl2[c])
            dov = do_ref[:, _chunk(c)]
            o = w[0] * o0[:, _chunk(c)].astype(F32) + w[1] * to1[c] + w[2] * to2[c]
            t = _head_sum(dov * o, ones)
            d0[:, _chunk(c)] = (w[0] * dov).astype(d0.dtype)
            t0[:, _chunk(c)] = w[0] * t
            td1[c], tt1[c] = w[1] * dov, w[1] * t
            td2[c], tt2[c] = w[2] * dov, w[2] * t
        for tmp, blk, d in ((td1, d1, DILATIONS[1]), (tt1, t1, DILATIONS[1]), (td2, d2, DILATIONS[2]), (tt2, t2, DILATIONS[2])):
            _classes_from_tokens(tmp, blk, d, MIX_TILE)

    operands, specs, scratch = _mix_operands(outs, lses)
    out_specs = [_class_block(d, MIX_TILE) for d in DILATIONS] * 2
    out_shape = [_sds((S // d, d * D), BF16) for d in DILATIONS] + [_sds((S // d, d * D), F32) for d in DILATIONS]
    return pl.pallas_call(
        body, name=name, grid=(S // MIX_TILE,),
        in_specs=[_row_spec(MIX_TILE)] + specs + [_vec_spec(128, 128)],
        out_specs=out_specs, out_shape=out_shape,
        scratch_shapes=scratch + [pltpu.VMEM((NCHUNK, MIX_TILE, 128), F32)] * 4,
        compiler_params=_cparams(1),
    )(do, *operands, ones_blockdiag)


def _attn_bwd_post(name, grads, cos_t, sin_t):
    tm = MIX_TILE
    scale = HEAD_DIM ** -0.5

    def unrope(x, cs, sn):
        return x * cs - _swap_halves(x) * sn

    def body(*refs):
        in_refs = refs[:9]
        cos_ref, sin_ref, dq_ref, dkv_ref, tmp_ref = refs[9:]
        cs = cos_ref[...]
        sn = sin_ref[...]
        for g, d in enumerate(DILATIONS):
            for which, blk in enumerate(in_refs[3 * g:3 * g + 3]):
                if d > 1:
                    _tokens_from_classes(blk, tmp_ref, d, tm)
                for c in range(NCHUNK):
                    x = tmp_ref[c] if d > 1 else blk[:, _chunk(c)].astype(F32)
                    if which == 0:
                        dq_ref[:, _chunk(c, g * D)] = (unrope(x, cs, sn) * scale).astype(dq_ref.dtype)
                    elif which == 1:
                        dkv_ref[:, _chunk(c, g * D)] = unrope(x, cs, sn).astype(dkv_ref.dtype)
                    else:
                        dkv_ref[:, _chunk(c, QW + g * D)] = x.astype(dkv_ref.dtype)

    operands = [a for branch in grads for a in branch]
    tab = pl.BlockSpec((tm, 128), lambda i: (i, 0))
    return pl.pallas_call(
        body, name=name, grid=(S // tm,),
        in_specs=[_class_block(d, tm) for d in DILATIONS for _ in range(3)] + [tab, tab],
        out_specs=[pl.BlockSpec((tm, QW), lambda i: (i, 0)), pl.BlockSpec((tm, 2 * QW), lambda i: (i, 0))],
        out_shape=[_sds((S, QW), BF16), _sds((S, 2 * QW), BF16)],
        scratch_shapes=[pltpu.VMEM((NCHUNK, tm, 128), F32)],
        compiler_params=_cparams(1),
    )(*operands, cos_t, sin_t)


def _adamw(name, parts, w, m, v, layer=None, other=None):
    n, rows, cols = parts.shape
    tr = rows
    for cand in (256, 176, 128, 64, 32, 16, 8):
        if rows % cand == 0:
            tr = cand
            break
    n_other = 0 if other is None else len(other)

    def body(p_ref, w_ref, m_ref, v_ref, *refs):
        g_ref, d_ref, nm_ref, nv_ref = refs[n_other:]
        g = p_ref[0].astype(F32)
        for j in range(1, n):
            g = g + p_ref[j].astype(F32)
        g_ref[...] = g
        d_ref[...], nm_ref[...], nv_ref[...] = _adam_update(g, w_ref[...], m_ref[...], v_ref[...])

    if layer is None:
        blk = pl.BlockSpec((tr, cols), lambda i: (i, 0))
        shape = (rows, cols)
    else:
        blk = pl.BlockSpec((None, tr, cols), lambda i: (layer, i, 0))
        shape = w.shape
    return pl.pallas_call(
        body, name=name, grid=(rows // tr,),
        in_specs=[pl.BlockSpec((n, tr, cols), lambda i: (0, i, 0)), blk, blk, blk]
                 + [pl.BlockSpec(memory_space=pl.ANY)] * n_other,
        out_specs=[blk] * 4, out_shape=[_sds(shape, F32)] * 4,
        input_output_aliases={4 + i: i for i in range(n_other)},
        compiler_params=_cparams(1),
    )(parts, w, m, v, *(other or ()))


def _adam_update(g, w, m, v):
    c1 = 1.0 / (1.0 - ADAM_B1 ** ADAM_STEP)
    c2 = 1.0 / (1.0 - ADAM_B2 ** ADAM_STEP)
    nm = ADAM_B1 * m + (1.0 - ADAM_B1) * g
    nv = ADAM_B2 * v + (1.0 - ADAM_B2) * (g * g)
    return -ADAM_LR * ((nm * c1) / (jnp.sqrt(nv * c2) + ADAM_EPS) + ADAM_WD * w), nm, nv


GAIN_ROWS = 16


def _pack_small(name, gain_tiles, taps, sq):
    ng = len(gain_tiles)

    def body(*refs):
        o_ref = refs[-1]
        o_ref[...] = jnp.zeros_like(o_ref)
        for i in range(ng):
            o_ref[i:i + 1, :] = refs[i][0:1, :]
        o_ref[ng:ng + 3, :] = refs[ng][0:3, :]
        o_ref[ng + 3:ng + 4, :] = refs[ng + 1][...]

    return pl.pallas_call(body, name=name, out_shape=_sds((GAIN_ROWS, D), F32))(*gain_tiles, taps, sq)


def _adamw_gains(name, parts, params):
    np_ = len(params)
    shapes = [w.shape for w, _, _ in params]

    def body(p_ref, *refs):
        ins, outs = refs[:3 * np_], refs[3 * np_:]

        def total(lo, rows):
            g = p_ref[0, lo:lo + rows, :]
            for j in range(1, NDEV):
                g = g + p_ref[j, lo:lo + rows, :]
            return g

        lo = 0
        for i, shape in enumerate(shapes):
            g = total(lo, shape[0])
            lo += shape[0]
            w_ref, m_ref, v_ref = ins[3 * i:3 * i + 3]
            g_ref, d_ref, nm_ref, nv_ref = outs[4 * i:4 * i + 4]
            g_ref[...] = g
            d_ref[...], nm_ref[...], nv_ref[...] = _adam_update(g, w_ref[...], m_ref[...], v_ref[...])
        taps_ref, loss_ref = outs[-2], outs[-1]
        taps_ref[...] = jnp.zeros_like(taps_ref)
        taps_ref[0:3, :] = total(lo, 3)
        loss_ref[...] = jnp.sum(total(lo + 3, 1), axis=-1, keepdims=True) * (0.5 / D)

    out_shape = [_sds(shape, F32) for shape in shapes for _ in range(4)] + [_sds((8, D), F32), _sds((1, 1), F32)]
    outs = pl.pallas_call(body, name=name, out_shape=out_shape)(parts, *[a for p in params for a in p])
    return [list(outs[4 * i:4 * i + 4]) for i in range(np_)], outs[-2], outs[-1].reshape(())


def _exchange(name, arrays, kind, after):
    n = len(arrays)
    gather = kind == "gather"
    out_shape = [_sds((NDEV,) + a.shape if gather else a.shape, a.dtype) for a in arrays]

    def body(*refs):
        srcs, outs = refs[:n], refs[n + 1:2 * n + 1]
        send_sems, recv_sems, local_sems = refs[2 * n + 1:]
        x, y, c = lax.axis_index("x"), lax.axis_index("y"), lax.axis_index("c")
        me = 4 * x + 2 * y + c
        pending = []
        for t in range(n):
            own = pltpu.make_async_copy(srcs[t] if gather else srcs[t].at[me], outs[t].at[me], local_sems.at[t])
            own.start()
            pending.append(own)
            for rel in range(1, NDEV):
                px = 1 - x if rel & 4 else x
                py = 1 - y if rel & 2 else y
                pc = 1 - c if rel & 1 else c
                peer = 4 * px + 2 * py + pc
                send = pltpu.make_async_remote_copy(
                    src_ref=srcs[t] if gather else srcs[t].at[peer], dst_ref=outs[t].at[me],
                    send_sem=send_sems.at[t, rel - 1], recv_sem=recv_sems.at[t, rel - 1],
                    device_id=(px, py, pc), device_id_type=MESH)
                send.start()
                arrive = pltpu.make_async_remote_copy(
                    src_ref=srcs[t] if gather else srcs[t].at[me], dst_ref=outs[t].at[peer],
                    send_sem=send_sems.at[t, rel - 1], recv_sem=recv_sems.at[t, rel - 1],
                    device_id=(px, py, pc), device_id_type=MESH)
                pending.append((send, arrive))
        for item in pending:
            if isinstance(item, tuple):
                item[0].wait_send()
                item[1].wait_recv()
            else:
                item.wait()

    any_spec = pl.BlockSpec(memory_space=pl.ANY)
    outs = pl.pallas_call(
        body, name=name,
        in_specs=[any_spec] * (n + 1), out_specs=[any_spec] * n, out_shape=out_shape,
        scratch_shapes=[pltpu.SemaphoreType.DMA((n, NDEV - 1)), pltpu.SemaphoreType.DMA((n, NDEV - 1)),
                        pltpu.SemaphoreType.DMA((n,))],
    )(*arrays, after)
    return list(outs)


_HBM_SPEC = pl.BlockSpec(memory_space=pltpu.HBM)
_SEM_SPEC = pl.BlockSpec(memory_space=pltpu.SEMAPHORE)
_DATAFLOW = pltpu.SideEffectType.DATAFLOW_SIDE_EFFECTING


def _peers():
    x, y, c = lax.axis_index("x"), lax.axis_index("y"), lax.axis_index("c")
    out = []
    for rel in range(1, NDEV):
        px = 1 - x if rel & 4 else x
        py = 1 - y if rel & 2 else y
        pc = 1 - c if rel & 1 else c
        out.append((rel - 1, (px, py, pc), 4 * px + 2 * py + pc))
    return 4 * x + 2 * y + c, out


def _hbm(a):
    return pltpu.HBM(a.shape, a.dtype)


def _own_slot(a, me, kind):
    mine = a[None] if kind == "gather" else lax.dynamic_slice_in_dim(a, me, 1, axis=0)
    shape = (NDEV,) + mine.shape[1:]
    return lax.dynamic_update_slice_in_dim(lax.empty(shape, a.dtype), mine, me, axis=0)


def _exchange_start(name, arrays, me, kind):
    n = len(arrays)
    gather = kind == "gather"
    lands = [_own_slot(a, me, kind) for a in arrays]

    def body(*refs):
        src_refs, land_refs = refs[:n], refs[n:2 * n]
        send_sems, recv_sems = refs[2 * n], refs[2 * n + 1]
        token = refs[-1]
        my_block, peers = _peers()
        for t in range(n):
            for slot, dev, block in peers:
                pltpu.make_async_remote_copy(
                    src_ref=src_refs[t] if gather else src_refs[t].at[block], dst_ref=land_refs[t].at[my_block],
                    send_sem=send_sems.at[t * (NDEV - 1) + slot], recv_sem=recv_sems.at[t * (NDEV - 1) + slot],
                    device_id=dev, device_id_type=MESH).start()
        token[...] = jnp.zeros_like(token)

    operands = [pltpu.with_memory_space_constraint(a, pltpu.HBM) for a in list(arrays) + lands]
    outs = pl.pallas_call(
        body, name=name,
        out_shape=(pltpu.SemaphoreType.DMA((n * (NDEV - 1),)), pltpu.SemaphoreType.DMA((n * (NDEV - 1),)),
                   *[_hbm(a) for a in operands], _sds((8, 128), F32)),
        in_specs=[_HBM_SPEC] * (2 * n),
        out_specs=(_SEM_SPEC, _SEM_SPEC, *[_HBM_SPEC] * (2 * n), pl.BlockSpec(memory_space=pltpu.VMEM)),
        input_output_aliases={i: 2 + i for i in range(2 * n)},
        compiler_params=pltpu.CompilerParams(has_side_effects=_DATAFLOW),
    )(*operands)
    return (outs[0], outs[1], list(outs[2:2 + n]), list(outs[2 + n:2 + 2 * n])), outs[-1]


def _exchange_wait(name, started, t, after, kind):
    send_sems, recv_sems, srcs, lands = started
    gather = kind == "gather"

    def body(src_ref, land_ref, send_ref, recv_ref, after_ref, src_out, land_out):
        _, peers = _peers()
        for slot, dev, block in peers:
            copy = pltpu.make_async_remote_copy(
                src_ref=src_ref if gather else src_ref.at[block], dst_ref=land_ref.at[block],
                send_sem=send_ref.at[t * (NDEV - 1) + slot], recv_sem=recv_ref.at[t * (NDEV - 1) + slot],
                device_id=dev, device_id_type=MESH)
            copy.wait_send()
            copy.wait_recv()

    return pl.pallas_call(
        body, name=name, out_shape=(_hbm(srcs[t]), _hbm(lands[t])),
        in_specs=(_HBM_SPEC, _HBM_SPEC, _SEM_SPEC, _SEM_SPEC, pl.BlockSpec(memory_space=pl.ANY)),
        out_specs=(_HBM_SPEC, _HBM_SPEC), input_output_aliases={0: 0, 1: 1},
        compiler_params=pltpu.CompilerParams(has_side_effects=_DATAFLOW),
    )(srcs[t], lands[t], send_sems, recv_sems, after)[1]


DIRECT_RELS = (1, 2, 4, 6)
RELAY_RELS = (2, 4, 6)


def _rel_peer(rel):
    x, y, c = lax.axis_index("x"), lax.axis_index("y"), lax.axis_index("c")
    px = 1 - x if rel & 4 else x
    py = 1 - y if rel & 2 else y
    pc = 1 - c if rel & 1 else c
    return (px, py, pc), 4 * px + 2 * py + pc


def _gather_start(name, shards, me):
    n, nr = len(shards), len(DIRECT_RELS)
    lands = [_own_slot(a, me, "gather") for a in shards]

    def body(*refs):
        src_refs, land_refs = refs[:n], refs[n:2 * n]
        send_sems, recv_sems = refs[2 * n], refs[2 * n + 1]
        _, my_block = _rel_peer(0)
        for t in range(n):
            for s, rel in enumerate(DIRECT_RELS):
                dev, _ = _rel_peer(rel)
                pltpu.make_async_remote_copy(
                    src_ref=src_refs[t], dst_ref=land_refs[t].at[my_block],
                    send_sem=send_sems.at[t * nr + s], recv_sem=recv_sems.at[t * nr + s],
                    device_id=dev, device_id_type=MESH).start()

    operands = [pltpu.with_memory_space_constraint(a, pltpu.HBM) for a in list(shards) + lands]
    outs = pl.pallas_call(
        body, name=name,
        out_shape=(pltpu.SemaphoreType.DMA((n * nr,)), pltpu.SemaphoreType.DMA((n * nr,)), *[_hbm(a) for a in operands]),
        in_specs=[_HBM_SPEC] * (2 * n), out_specs=(_SEM_SPEC, _SEM_SPEC, *[_HBM_SPEC] * (2 * n)),
        input_output_aliases={i: 2 + i for i in range(2 * n)},
        compiler_params=pltpu.CompilerParams(has_side_effects=_DATAFLOW),
    )(*operands)
    return outs[0], outs[1], list(outs[2:2 + n]), list(outs[2 + n:2 + 2 * n])


def _gather_wait(name, started, ts, after):
    send_sems, recv_sems, srcs, lands = started
    m, nr = len(ts), len(DIRECT_RELS)

    def body(*refs):
        src_refs, land_refs = refs[:m], refs[m:2 * m]
        send_ref, recv_ref = refs[2 * m], refs[2 * m + 1]
        for i, t in enumerate(ts):
            for s, rel in enumerate(DIRECT_RELS):
                dev, block = _rel_peer(rel)
                copy = pltpu.make_async_remote_copy(
                    src_ref=src_refs[i], dst_ref=land_refs[i].at[block],
                    send_sem=send_ref.at[t * nr + s], recv_sem=recv_ref.at[t * nr + s],
                    device_id=dev, device_id_type=MESH)
                copy.wait_send()
                copy.wait_recv()

    operands = [srcs[t] for t in ts] + [lands[t] for t in ts]
    outs = pl.pallas_call(
        body, name=name, out_shape=tuple(_hbm(a) for a in operands),
        in_specs=[_HBM_SPEC] * (2 * m) + [_SEM_SPEC, _SEM_SPEC, pl.BlockSpec(memory_space=pl.ANY)],
        out_specs=tuple([_HBM_SPEC] * (2 * m)), input_output_aliases={i: i for i in range(2 * m)},
        compiler_params=pltpu.CompilerParams(has_side_effects=_DATAFLOW),
    )(*operands, send_sems, recv_sems, after)
    return list(outs[m:])


def _relay_start(name, lands):
    m, nr = len(lands), len(RELAY_RELS)

    def body(*refs):
        land_refs, send_sems, recv_sems = refs[:m], refs[m], refs[m + 1]
        sibling, _ = _rel_peer(1)
        for i in range(m):
            for s, rel in enumerate(RELAY_RELS):
                _, block = _rel_peer(rel)
                pltpu.make_async_remote_copy(
                    src_ref=land_refs[i].at[block], dst_ref=land_refs[i].at[block],
                    send_sem=send_sems.at[i * nr + s], recv_sem=recv_sems.at[i * nr + s],
                    device_id=sibling, device_id_type=MESH).start()

    outs = pl.pallas_call(
        body, name=name,
        out_shape=(pltpu.SemaphoreType.DMA((m * nr,)), pltpu.SemaphoreType.DMA((m * nr,)), *[_hbm(a) for a in lands]),
        in_specs=[_HBM_SPEC] * m, out_specs=(_SEM_SPEC, _SEM_SPEC, *[_HBM_SPEC] * m),
        input_output_aliases={i: 2 + i for i in range(m)},
        compiler_params=pltpu.CompilerParams(has_side_effects=_DATAFLOW),
    )(*lands)
    return outs[0], outs[1], list(outs[2:])


def _relay_wait(name, relayed, after):
    send_sems, recv_sems, lands = relayed
    m, nr = len(lands), len(RELAY_RELS)

    def body(*refs):
        land_refs, send_ref, recv_ref = refs[:m], refs[m], refs[m + 1]
        sibling, _ = _rel_peer(1)
        for i in range(m):
            for s, rel in enumerate(RELAY_RELS):
                _, sent = _rel_peer(rel)
                _, arriving = _rel_peer(rel ^ 1)
                copy = pltpu.make_async_remote_copy(
                    src_ref=land_refs[i].at[sent], dst_ref=land_refs[i].at[arriving],
                    send_sem=send_ref.at[i * nr + s], recv_sem=recv_ref.at[i * nr + s],
                    device_id=sibling, device_id_type=MESH)
                copy.wait_send()
                copy.wait_recv()

    outs = pl.pallas_call(
        body, name=name, out_shape=tuple(_hbm(a) for a in lands),
        in_specs=[_HBM_SPEC] * m + [_SEM_SPEC, _SEM_SPEC, pl.BlockSpec(memory_space=pl.ANY)],
        out_specs=tuple([_HBM_SPEC] * m), input_output_aliases={i: i for i in range(m)},
        compiler_params=pltpu.CompilerParams(has_side_effects=_DATAFLOW),
    )(*lands, send_sems, recv_sems, after)
    return list(outs)


def _ffn_fwd(tag, n, wg, wd):
    gu, act = _gate_up_act(f"ffn_gate_up_{tag}", n, wg)
    wd4 = wd.reshape(NFB, FB, D)
    f = _fwd_kblocked(f"ffn_down_{tag}", act, wd4)
    return (n, gu, act, wg, wd4), f


def _ffn_bwd(tag, dh_out, df, h_in, saved, g_pre, send, mixer):
    n, gu, act, wg, wd4 = saved
    dwd = _bwd_w_kblocked(f"ffn_down_dw_{tag}", act, df).reshape(NDEV, DFF // NDEV, D)
    dgu = _down_dx_act_bwd(f"ffn_down_dx_{tag}", df, wd4, gu).reshape(NDEV, S, FB)
    tok = send({f"down_{tag}": dwd, f"gate_up_{tag}": _bwd_w_cols_blocked(f"ffn_gate_up_dw_{tag}", n, dgu)})
    dn = _bwd_x_cols_blocked(f"ffn_gate_up_dx_{tag}", dgu, wg, after=tok)
    dh_in, (dg_pre,), dy, dg_mixer = _rms_bwd(f"ffn_prenorm_bwd_{tag}", h_in, [(g_pre, dn)], dh_out, F32, then=mixer)
    return dh_in, dg_pre, dy, dg_mixer


def kernel(x, positions, mix_norm_pre, mix_norm_post, ffn_norm_pre, ffn_norm_post, ffn_w_gate_up, ffn_w_down, conv_w_in, conv_w, conv_w_out, kv_norm, w_kv, w_q, w_o, loss_target, m_mix_norm_pre, m_mix_norm_post, m_ffn_norm_pre, m_ffn_norm_post, m_ffn_w_gate_up, m_ffn_w_down, m_conv_w_in, m_conv_w, m_conv_w_out, m_kv_norm, m_w_kv, m_w_q, m_w_o, v_mix_norm_pre, v_mix_norm_post, v_ffn_norm_pre, v_ffn_norm_post, v_ffn_w_gate_up, v_ffn_w_down, v_conv_w_in, v_conv_w, v_conv_w_out, v_kv_norm, v_w_kv, v_w_q, v_w_o):
    me = 4 * lax.axis_index("x") + 2 * lax.axis_index("y") + lax.axis_index("c")
    h0 = x.reshape(S, D)
    target = loss_target.reshape(S, D)
    row = lambda a, l: a[l].reshape(1, D)
    g_kv = kv_norm.reshape(1, D)

    cw_shard = jnp.pad(conv_w[0], ((0, 5), (0, 0)))
    names = ["conv_in", "conv_w", "conv_out", "gate_up_0", "down_0", "kv", "q", "o", "gate_up_1", "down_1"]
    shards = [conv_w_in[0], cw_shard, conv_w_out[0], ffn_w_gate_up[0], ffn_w_down[0],
              w_kv, w_q[0], w_o[0], ffn_w_gate_up[1], ffn_w_down[1]]
    shards = [s if n == "conv_w" else s.astype(BF16) for n, s in zip(names, shards)]
    first = 3
    gather_first = _gather_start("gather_start_conv", shards[:first], me)
    gather_rest = _gather_start("gather_start_rest", shards[first:], me)

    def direct(group, after):
        ts = [names.index(n) for n in group]
        started, ts = (gather_first, ts) if ts[0] < first else (gather_rest, [t - first for t in ts])
        lands = _gather_wait(f"gather_wait_{group[0]}", started, ts, after)
        return _relay_start(f"relay_start_{group[0]}", lands)

    def finish(group, relayed, after):
        return dict(zip(group, _relay_wait(f"relay_wait_{group[0]}", relayed, after)))

    sent = {}

    def send(grads):
        started, token = _exchange_start(f"scatter_start_{next(iter(grads))}", list(grads.values()), me, "scatter")
        for i, name in enumerate(grads):
            sent[name] = (started, i)
        return token

    groups = [["conv_in", "conv_w", "conv_out"], ["gate_up_0", "down_0"], ["kv", "q"], ["o", "gate_up_1", "down_1"]]
    n0 = _rms_fwd("mix_prenorm_0", h0, [row(mix_norm_pre, 0)])[0]
    half = HEAD_DIM // 2
    inv_freq = ROPE_THETA ** (-jnp.arange(half, dtype=F32) / half)
    tables = _rope_tables("rope_tables", positions.reshape(S, 1), jnp.tile(inv_freq, 4).reshape(1, 128))
    w = finish(groups[0], direct(groups[0], tables[0]), n0)
    win = w["conv_in"].transpose(1, 0, 2).reshape(D, 3 * D)
    cw = w["conv_w"].transpose(1, 0, 2).reshape(8, D)
    wout = w["conv_out"].reshape(D, D)
    z = _fwd_rows("conv_in", n0, win, BF16)
    pre = _conv_fwd("conv_gate", z, cw)
    relayed = direct(groups[1], pre)
    y0 = _fwd_rows("conv_out", pre, wout)
    h1, (n1,) = _resid_rms("mix_postnorm_0", h0, y0, row(mix_norm_post, 0), [row(ffn_norm_pre, 0)])
    w = finish(groups[1], relayed, n1)
    ffn0, f0 = _ffn_fwd("0", n1, w["gate_up_0"], w["down_0"])
    relayed = direct(groups[2], ffn0[2])
    h2, (nk, n2) = _resid_rms("ffn_postnorm_0", h1, f0, row(ffn_norm_post, 0), [g_kv, row(mix_norm_pre, 1)])

    w = finish(groups[2], relayed, nk)
    wkv = w["kv"].transpose(1, 0, 2).reshape(D, 2 * QW)
    wq = w["q"].transpose(1, 0, 2).reshape(D, QW)
    qc, kc, vc, o_c, lse_c = [], [], [], [], []
    for g, d in enumerate(DILATIONS):
        q_g, k_g, v_g = _qkv_classes(f"qkv_proj_{g}", n2, nk, wq, wkv, g, d, tables)
        qc.append(q_g)
        kc.append(k_g)
        vc.append(v_g)
    relayed = direct(groups[3], vc[-1])
    for g, d in enumerate(DILATIONS):
        o_g, lse_g = _attn_fwd(f"attn_fwd_{g}", qc[g], kc[g], vc[g], d)
        o_c.append(o_g)
        lse_c.append(lse_g)
    o_mix = _mix_fwd("attn_mix", o_c, lse_c)
    w = finish(groups[3], relayed, o_mix)
    wo = w["o"].reshape(D, D)
    y1 = _fwd_rows("attn_out", o_mix, wo)
    h3, (n3,) = _resid_rms("mix_postnorm_1", h2, y1, row(mix_norm_post, 1), [row(ffn_norm_pre, 1)])
    gu1, act1 = _gate_up_act("ffn_gate_up_1", n3, w["gate_up_1"])
    wd1 = w["down_1"].reshape(NFB, FB, D)
    ffn1 = (n3, gu1, act1, w["gate_up_1"], wd1)

    dh4, df1, dg_fpost1, sq = _down_loss("ffn_down_1_loss", act1, wd1, h3, row(ffn_norm_post, 1), target)

    dh3, dg_fpre1, dy1, dg_mpost1 = _ffn_bwd(
        "1", dh4, df1, h3, ffn1, row(ffn_norm_pre, 1), send, (y1, row(mix_norm_post, 1)))
    dwo = _bwd_w_rows("attn_out_dw", o_mix, dy1).reshape(NDEV, D // NDEV, D)
    do = _bwd_x_rows("attn_out_dx", dy1, wo, BF16)
    lane = jnp.arange(128)
    ones_blockdiag = (lane[:, None] // HEAD_DIM == lane[None, :] // HEAD_DIM).astype(BF16)
    mixed = _mix_bwd("attn_mix_bwd", do, o_c, lse_c, ones_blockdiag)
    branch_grads = [_attn_bwd(f"attn_bwd_{g}", qc[g], kc[g], vc[g], mixed[g], lse_c[g], mixed[3 + g], d)
                    for g, d in enumerate(DILATIONS)]
    dq_raw, dkv = _attn_bwd_post("attn_bwd_post", branch_grads, *tables)
    tok = send({"o": dwo, "kv": _bwd_w_cols("kv_proj_dw", nk, dkv, 2 * QW // NDEV),
                "q": _bwd_w_cols("q_proj_dw", n2, dq_raw, QW // NDEV)})
    dnk = _bwd_x_plain("kv_proj_dx", dkv, wkv, after=tok)
    dn2 = _bwd_x_plain("q_proj_dx", dq_raw, wq)
    dh2, (dg_kv, dg_mpre1), df0, dg_fpost0 = _rms_bwd(
        "kv_and_mix_prenorm_bwd_1", h2, [(g_kv, dnk), (row(mix_norm_pre, 1), dn2)], dh3, F32,
        then=(f0, row(ffn_norm_post, 0)))

    dh1, dg_fpre0, dy0, dg_mpost0 = _ffn_bwd(
        "0", dh2, df0, h1, ffn0, row(ffn_norm_pre, 0), send, (y0, row(mix_norm_post, 0)))
    dwout = _bwd_w_rows("conv_out_dw", pre, dy0).reshape(NDEV, D // NDEV, D)
    dpre = _bwd_x_rows("conv_out_dx", dy0, wout, BF16)
    dz, dcw = _conv_bwd("conv_gate_bwd", z, dpre, cw)
    tok = send({"conv_out": dwout, "conv_in": _bwd_w_cols("conv_in_dw", n0, dz, 3 * D // NDEV)})
    dn0 = _bwd_x_plain("conv_in_dx", dz, win, after=tok)
    dh0, (dg_mpre0,) = _rms_bwd("mix_prenorm_bwd_0", h0, [(row(mix_norm_pre, 0), dn0)], dh1, F32)

    small = _pack_small("pack_small_grads", [dg_mpre0, dg_mpre1, dg_mpost0, dg_mpost1, dg_fpre0, dg_fpre1,
                                             dg_fpost0, dg_fpost1, dg_kv], dcw, sq)

    done = [small]

    def upd(tag, w, m, v):
        parts = _exchange_wait(f"scatter_wait_{tag}", *sent[tag], done[-1], "scatter")
        shape = w.shape
        flat = lambda a: a.reshape(parts.shape[1:])
        res = _adamw(f"adamw_{tag}", parts, flat(w), flat(m), flat(v))
        done.append(res[0])
        return [r.reshape(shape) for r in res]

    def upd_layer(tag, l, w, m, v, other):
        parts = _exchange_wait(f"scatter_wait_{tag}_{l}", *sent[f"{tag}_{l}"], done[-1], "scatter")
        res = _adamw(f"adamw_{tag}_{l}", parts, w, m, v, layer=l, other=other)
        done.append(res[0])
        return list(res)

    res = {}
    down_1 = upd_layer("down", 1, ffn_w_down, m_ffn_w_down, v_ffn_w_down, None)
    gate_up_t = [jnp.swapaxes(a, 1, 2) for a in (ffn_w_gate_up, m_ffn_w_gate_up, v_ffn_w_gate_up)]
    gate_up_1 = upd_layer("gate_up", 1, *gate_up_t, None)
    res["w_o"] = upd("o", w_o, m_w_o, v_w_o)
    res["w_q"] = upd("q", w_q, m_w_q, v_w_q)
    res["w_kv"] = upd("kv", w_kv, m_w_kv, v_w_kv)

    small_all = _exchange("gather_small_grads", [small], "gather", done[-1])[0]
    vec = lambda a: a.reshape(1, D)
    gain_res, taps, loss = _adamw_gains("adamw_gains", small_all, [
        (mix_norm_pre, m_mix_norm_pre, v_mix_norm_pre), (mix_norm_post, m_mix_norm_post, v_mix_norm_post),
        (ffn_norm_pre, m_ffn_norm_pre, v_ffn_norm_pre), (ffn_norm_post, m_ffn_norm_post, v_ffn_norm_post),
        (vec(kv_norm), vec(m_kv_norm), vec(v_kv_norm))])
    dcw_mine = lax.dynamic_slice(taps, (0, me * 128), (8, 128))
    pad8 = lambda a, fill: jnp.pad(a[0], ((0, 5), (0, 0)), constant_values=fill)
    cw_res = [r[0:3].reshape(1, 3, 128) for r in
              _adamw("adamw_conv_w", dcw_mine.reshape(1, 8, 128), cw_shard, pad8(m_conv_w, 0.0), pad8(v_conv_w, 1.0))]

    res.update({
        "mix_norm_pre": gain_res[0],
        "mix_norm_post": gain_res[1],
        "ffn_norm_pre": gain_res[2],
        "ffn_norm_post": gain_res[3],
        "kv_norm": [r.reshape(D) for r in gain_res[4]],
        "conv_w": cw_res,
    })
    done.append(small_all)
    res["ffn_w_down"] = upd_layer("down", 0, ffn_w_down, m_ffn_w_down, v_ffn_w_down, down_1)
    res["ffn_w_gate_up"] = [jnp.swapaxes(r, 1, 2) for r in upd_layer("gate_up", 0, *gate_up_t, gate_up_1)]
    res["conv_w_out"] = upd("conv_out", conv_w_out, m_conv_w_out, v_conv_w_out)
    res["conv_w_in"] = upd("conv_in", conv_w_in, m_conv_w_in, v_conv_w_in)
    order = ["mix_norm_pre", "mix_norm_post", "ffn_norm_pre", "ffn_norm_post", "ffn_w_gate_up", "ffn_w_down",
             "conv_w_in", "conv_w", "conv_w_out", "kv_norm", "w_kv", "w_q", "w_o"]
    out = [loss, dh0.reshape(1, S, D)]
    for i in range(4):
        out += [res[name][i] for name in order]
    return tuple(out)
```

```python
import jax
import jax.numpy as jnp
from jax import lax
from jax.experimental import pallas as pl
from jax.experimental.pallas import tpu as pltpu

F32 = jnp.float32
BF16 = jnp.bfloat16

S = 4096
D = 1024
NDEV = 8
HEAD_DIM = 64
QW = 3072
DFF = 2816
FB = 704
NFB = 4
BRANCHES = ((128, 1), (512, 4), (2048, 16))
BAND = 128
ROPE_THETA = 10000.0
RMS_EPS = 1e-6
NEG_INF = -1e30
ADAM_LR, ADAM_B1, ADAM_B2, ADAM_EPS, ADAM_WD, ADAM_STEP = 0.001, 0.9, 0.999, 1e-08, 0.01, 10

VMEM_LIMIT_BYTES = 52 * 1024 * 1024
ROW_TILE = 512
MESH = pl.DeviceIdType.MESH


def _cparams(ngrid):
    return pltpu.CompilerParams(dimension_semantics=("arbitrary",) * ngrid,
                                vmem_limit_bytes=VMEM_LIMIT_BYTES)


def _sds(shape, dtype):
    return jax.ShapeDtypeStruct(tuple(shape), dtype)


_DIMS = {"nn": (((1,), (0,)), ((), ())),
         "nt": (((1,), (1,)), ((), ())),
         "tn": (((0,), (0,)), ((), ()))}


def _matmul(name, a, b, *, mode, grid, a_blk, a_map, b_blk, b_map, o_shape, o_blk, o_map, out_dtype, after=None,
            out_groups=1):
    nk = grid[2]
    dims = _DIMS[mode]
    acc_shape = tuple(s for s in o_blk if s is not None)
    if out_groups > 1:
        acc_shape = (acc_shape[1], out_groups * acc_shape[2])
    extra = [] if after is None else [after]

    def store(o_ref, val):
        if out_groups == 1:
            o_ref[...] = val.astype(o_ref.dtype)
        else:
            n = o_ref.shape[-1]
            for grp in range(out_groups):
                o_ref[grp] = val[:, grp * n:(grp + 1) * n].astype(o_ref.dtype)

    def body(a_ref, b_ref, *rest):
        o_ref, scratch = rest[len(extra)], rest[len(extra) + 1:]
        part = lax.dot_general(a_ref[...], b_ref[...], dims, preferred_element_type=F32)
        if nk == 1:
            store(o_ref, part)
            return
        acc_ref = scratch[0]
        k = pl.program_id(2)

        @pl.when(k == 0)
        def _():
            acc_ref[...] = part

        @pl.when(k > 0)
        def _():
            acc_ref[...] += part

        @pl.when(k == nk - 1)
        def _():
            store(o_ref, acc_ref[...])

    return pl.pallas_call(
        body, name=name, grid=grid,
        in_specs=[pl.BlockSpec(a_blk, a_map), pl.BlockSpec(b_blk, b_map)] + [pl.BlockSpec(memory_space=pl.ANY)] * len(extra),
        out_specs=pl.BlockSpec(o_blk, o_map),
        out_shape=_sds(o_shape, out_dtype),
        scratch_shapes=[] if nk == 1 else [pltpu.VMEM(acc_shape, F32)],
        compiler_params=_cparams(3),
    )(a, b, *extra)


TM = 1024
TK = S


def _fwd_rows(name, a, w, out_dtype=F32):
    kdim, n = w.shape
    tn = 1024
    return _matmul(name, a, w, mode="nn", grid=(S // TM, n // tn, 1),
                   a_blk=(TM, kdim), a_map=lambda i, j, k: (i, 0),
                   b_blk=(kdim, tn), b_map=lambda i, j, k: (0, j),
                   o_shape=(S, n), o_blk=(TM, tn), o_map=lambda i, j, k: (i, j), out_dtype=out_dtype)


def _fwd_kblocked(name, a4, w4):
    nb, _, kb = a4.shape
    n = w4.shape[2]

    def body(a_ref, w_ref, o_ref):
        acc = _dot_nn(a_ref[0], w_ref[0])
        for j in range(1, nb):
            acc = acc + _dot_nn(a_ref[j], w_ref[j])
        o_ref[...] = acc

    return pl.pallas_call(
        body, name=name, grid=(S // TM,),
        in_specs=[pl.BlockSpec((nb, TM, kb), lambda i: (0, i, 0)), pl.BlockSpec((nb, kb, n), lambda i: (0, 0, 0))],
        out_specs=pl.BlockSpec((TM, n), lambda i: (i, 0)), out_shape=_sds((S, n), F32),
        compiler_params=_cparams(1),
    )(a4, w4)


def _down_resid(name, a4, w4, h, g, next_gains):
    nb, _, kb = a4.shape
    n = len(next_gains)
    tm = ROW_TILE

    def body(a_ref, w_ref, h_ref, g_ref, *refs):
        f = _dot_nn(a_ref[0], w_ref[0])
        for j in range(1, nb):
            f = f + _dot_nn(a_ref[j], w_ref[j])
        f_ref, hnew_ref = refs[n], refs[n + 1]
        f_ref[...] = f
        h_new = h_ref[...] + (f * _rstd(f)) * g_ref[...]
        hnew_ref[...] = h_new
        hh = h_new * _rstd(h_new)
        for g2_ref, o_ref in zip(refs[:n], refs[n + 2:]):
            o_ref[...] = (hh * g2_ref[...]).astype(o_ref.dtype)

    outs = pl.pallas_call(
        body, name=name, grid=(S // tm,),
        in_specs=[pl.BlockSpec((nb, tm, kb), lambda i: (0, i, 0)), pl.BlockSpec((nb, kb, D), lambda i: (0, 0, 0)),
                  _row_spec(tm), _vec_spec()] + [_vec_spec()] * n,
        out_specs=[_row_spec(tm)] * (n + 2),
        out_shape=[_sds((S, D), F32)] * 2 + [_sds((S, D), BF16)] * n,
        compiler_params=_cparams(1),
    )(a4, w4, h, g, *next_gains)
    return outs[0], outs[1], list(outs[2:])


def _down_loss(name, a4, w4, h, g, target):
    nb, _, kb = a4.shape
    tm = ROW_TILE

    def body(a_ref, w_ref, h_ref, g_ref, t_ref, dh_ref, df_ref, dg_ref, part_ref):
        f = _dot_nn(a_ref[0], w_ref[0])
        for j in range(1, nb):
            f = f + _dot_nn(a_ref[j], w_ref[j])
        gain = g_ref[...]
        e = h_ref[...] + (f * _rstd(f)) * gain - t_ref[...]
        dh = e * (1.0 / D)
        dh_ref[...] = dh
        step = pl.program_id(0)
        df_ref[...] = _norm_bwd_rows(f, gain, dh, dg_ref, step).astype(df_ref.dtype)
        part = jnp.sum(e * e, axis=0, keepdims=True)

        @pl.when(step == 0)
        def _():
            part_ref[...] = part

        @pl.when(step > 0)
        def _():
            part_ref[...] += part

    return pl.pallas_call(
        body, name=name, grid=(S // tm,),
        in_specs=[pl.BlockSpec((nb, tm, kb), lambda i: (0, i, 0)), pl.BlockSpec((nb, kb, D), lambda i: (0, 0, 0)),
                  _row_spec(tm), _vec_spec(), _row_spec(tm)],
        out_specs=[_row_spec(tm), _row_spec(tm), _vec_spec(8), _vec_spec()],
        out_shape=[_sds((S, D), F32), _sds((S, D), BF16), _sds((8, D), F32), _sds((1, D), F32)],
        compiler_params=_cparams(1),
    )(a4, w4, h, g, target)


def _bwd_x_cols_blocked(name, dy8, wg, after):
    _, kdim, n = wg.shape
    nk = NDEV // 2

    def body(a_ref, b_ref, after_ref, o_ref, acc_ref):
        k = pl.program_id(1)
        part = _dot_nt(a_ref[0], b_ref[0]) + _dot_nt(a_ref[1], b_ref[1])

        @pl.when(k == 0)
        def _():
            acc_ref[...] = part

        @pl.when(k > 0)
        def _():
            acc_ref[...] += part

        @pl.when(k == nk - 1)
        def _():
            o_ref[...] = acc_ref[...].astype(o_ref.dtype)

    return pl.pallas_call(
        body, name=name, grid=(S // FFN_TM, nk),
        in_specs=[pl.BlockSpec((2, None, FFN_TM, n), lambda i, k: (0, k, i, 0)),
                  pl.BlockSpec((2, None, kdim, n), lambda i, k: (0, k, 0, 0)),
                  pl.BlockSpec(memory_space=pl.ANY)],
        out_specs=pl.BlockSpec((FFN_TM, kdim), lambda i, k: (i, 0)), out_shape=_sds((S, kdim), BF16),
        scratch_shapes=[pltpu.VMEM((FFN_TM, kdim), F32)],
        compiler_params=_cparams(2),
    )(dy8.reshape(2, nk, S, n), wg.reshape(2, nk, kdim, n), after)


def _bwd_x_rows(name, dy, w, out_dtype, after=None):
    kdim, n = w.shape
    tkk = 512
    return _matmul(name, dy, w, mode="nt", grid=(S // TM, kdim // tkk, 1),
                   a_blk=(TM, n), a_map=lambda i, j, k: (i, 0),
                   b_blk=(tkk, n), b_map=lambda i, j, k: (j, 0),
                   o_shape=(S, kdim), o_blk=(TM, tkk), o_map=lambda i, j, k: (i, j), out_dtype=out_dtype, after=after)


DW_COLS = 768


def _bwd_w_cols(name, a, dy, n):
    kdim = a.shape[1]
    groups = DW_COLS // n
    return _matmul(name, a, dy, mode="tn", grid=(1, NDEV // groups, S // TK),
                   a_blk=(TK, kdim), a_map=lambda i, j, k: (k, 0),
                   b_blk=(TK, DW_COLS), b_map=lambda i, j, k: (k, j),
                   o_shape=(NDEV, kdim, n), o_blk=(groups, kdim, n) if groups > 1 else (None, kdim, n),
                   o_map=lambda i, j, k: (j, 0, 0), out_dtype=BF16, out_groups=groups)


def _bwd_x_plain(name, dy, w, after=None):
    kdim, n = w.shape
    tm = TM if n <= 3 * D else TM // 2
    return _matmul(name, dy, w, mode="nt", grid=(S // tm, 1, 1),
                   a_blk=(tm, n), a_map=lambda i, j, k: (i, 0),
                   b_blk=(kdim, n), b_map=lambda i, j, k: (0, 0),
                   o_shape=(S, kdim), o_blk=(tm, kdim), o_map=lambda i, j, k: (i, 0), out_dtype=BF16, after=after)


def _bwd_w_cols_blocked(name, a, dy8):
    kdim = a.shape[1]
    n = dy8.shape[2]
    return _matmul(name, dy8, a, mode="tn", grid=(1, NDEV, S // TK),
                   a_blk=(None, TK, n), a_map=lambda i, j, k: (j, k, 0),
                   b_blk=(TK, kdim), b_map=lambda i, j, k: (k, 0),
                   o_shape=(NDEV, n, kdim), o_blk=(None, n, kdim), o_map=lambda i, j, k: (j, 0, 0), out_dtype=BF16)


def _bwd_w_rows(name, a, dy):
    kdim = a.shape[1]
    n = dy.shape[1]
    tmm = 512
    return _matmul(name, a, dy, mode="tn", grid=(kdim // tmm, 1, S // TK),
                   a_blk=(TK, tmm), a_map=lambda i, j, k: (k, i),
                   b_blk=(TK, n), b_map=lambda i, j, k: (k, 0),
                   o_shape=(kdim, n), o_blk=(tmm, n), o_map=lambda i, j, k: (i, 0), out_dtype=BF16)


def _bwd_w_kblocked(name, a4, dy):
    nb, _, kb = a4.shape
    n = dy.shape[1]
    return _matmul(name, a4, dy, mode="tn", grid=(nb, 1, S // TK),
                   a_blk=(None, TK, kb), a_map=lambda i, j, k: (i, k, 0),
                   b_blk=(TK, n), b_map=lambda i, j, k: (k, 0),
                   o_shape=(nb, kb, n), o_blk=(None, kb, n), o_map=lambda i, j, k: (i, 0, 0), out_dtype=BF16)


def _rstd(x):
    return lax.rsqrt(jnp.mean(x * x, axis=-1, keepdims=True) + RMS_EPS)


def _row_spec(tm=ROW_TILE, width=D):
    return pl.BlockSpec((tm, width), lambda i: (i, 0))


def _vec_spec(rows=1, width=D):
    return pl.BlockSpec((rows, width), lambda i: (0, 0))


def _rms_fwd(name, x, gains):
    n = len(gains)

    def body(x_ref, *refs):
        x_val = x_ref[...]
        xh = x_val * _rstd(x_val)
        for g_ref, o_ref in zip(refs[:n], refs[n:]):
            o_ref[...] = (xh * g_ref[...]).astype(o_ref.dtype)

    outs = pl.pallas_call(
        body, name=name, grid=(S // ROW_TILE,),
        in_specs=[_row_spec()] + [_vec_spec()] * n,
        out_specs=[_row_spec()] * n,
        out_shape=[_sds((S, D), BF16)] * n,
        compiler_params=_cparams(1),
    )(x, *gains)
    return list(outs)


def _resid_rms(name, h, y, g, next_gains):
    n = len(next_gains)

    def body(h_ref, y_ref, g_ref, *refs):
        y_val = y_ref[...]
        h_new = h_ref[...] + (y_val * _rstd(y_val)) * g_ref[...]
        refs[n][...] = h_new
        hh = h_new * _rstd(h_new)
        for g2_ref, o_ref in zip(refs[:n], refs[n + 1:]):
            o_ref[...] = (hh * g2_ref[...]).astype(o_ref.dtype)

    outs = pl.pallas_call(
        body, name=name, grid=(S // ROW_TILE,),
        in_specs=[_row_spec(), _row_spec(), _vec_spec()] + [_vec_spec()] * n,
        out_specs=[_row_spec()] * (n + 1), out_shape=[_sds((S, D), F32)] + [_sds((S, D), BF16)] * n,
        compiler_params=_cparams(1),
    )(h, y, g, *next_gains)
    return outs[0], list(outs[1:])


def _resid_rms_loss(name, h, y, g, target):
    def body(h_ref, y_ref, g_ref, t_ref, dh_ref, dy_ref, dg_ref, part_ref):
        y_val = y_ref[...]
        gain = g_ref[...]
        e = h_ref[...] + (y_val * _rstd(y_val)) * gain - t_ref[...]
        dh = e * (1.0 / D)
        dh_ref[...] = dh
        step = pl.program_id(0)
        dy_ref[...] = _norm_bwd_rows(y_val, gain, dh, dg_ref, step).astype(dy_ref.dtype)
        part = jnp.sum(e * e, axis=0, keepdims=True)

        @pl.when(step == 0)
        def _():
            part_ref[...] = part

        @pl.when(step > 0)
        def _():
            part_ref[...] += part

    return pl.pallas_call(
        body, name=name, grid=(S // ROW_TILE,),
        in_specs=[_row_spec(), _row_spec(), _vec_spec(), _row_spec()],
        out_specs=[_row_spec(), _row_spec(), _vec_spec(8), _vec_spec()],
        out_shape=[_sds((S, D), F32), _sds((S, D), BF16), _sds((8, D), F32), _sds((1, D), F32)],
        compiler_params=_cparams(1),
    )(h, y, g, target)


def _norm_bwd_rows(x_val, g, dn, dg_ref, step):
    r = _rstd(x_val)
    xh = x_val * r
    dxh = dn * g
    part = jnp.sum(dn * xh, axis=0, keepdims=True)

    @pl.when(step == 0)
    def _():
        dg_ref[...] = jnp.zeros_like(dg_ref)

    dg_ref[0:1, :] += part
    return r * (dxh - xh * jnp.mean(dxh * xh, axis=-1, keepdims=True))


def _rms_bwd(name, x, pairs, dres, out_dtype, then=None):
    n = len(pairs)
    has_res = dres is not None
    chained = then is not None

    def body(x_ref, *refs):
        g_refs = refs[0:2 * n:2]
        dn_refs = refs[1:2 * n:2]
        pos = 2 * n
        res_ref = refs[pos] if has_res else None
        pos += int(has_res)
        if chained:
            y_ref, gy_ref = refs[pos], refs[pos + 1]
            pos += 2
        dx_ref = refs[pos]
        dg_refs = refs[pos + 1:pos + 1 + n]
        step = pl.program_id(0)
        x_val = x_ref[...]
        acc = res_ref[...] if has_res else jnp.zeros_like(x_val)
        for g_ref, dn_ref, dg_ref in zip(g_refs, dn_refs, dg_refs):
            acc = acc + _norm_bwd_rows(x_val, g_ref[...], dn_ref[...].astype(F32), dg_ref, step)
        dx_ref[...] = acc.astype(dx_ref.dtype)
        if chained:
            dy_ref, dgy_ref = refs[pos + 1 + n], refs[pos + 2 + n]
            dy_ref[...] = _norm_bwd_rows(y_ref[...], gy_ref[...], acc, dgy_ref, step).astype(dy_ref.dtype)

    operands = [x]
    in_specs = [_row_spec()]
    for g, dn in pairs:
        operands += [g, dn]
        in_specs += [_vec_spec(), _row_spec()]
    if has_res:
        operands.append(dres)
        in_specs.append(_row_spec())
    if chained:
        operands += [then[0], then[1]]
        in_specs += [_row_spec(), _vec_spec()]
    extra = int(chained)
    outs = pl.pallas_call(
        body, name=name, grid=(S // ROW_TILE,),
        in_specs=in_specs,
        out_specs=[_row_spec()] + [_vec_spec(8)] * n + [_row_spec(), _vec_spec(8)] * extra,
        out_shape=[_sds((S, D), out_dtype)] + [_sds((8, D), F32)] * n + [_sds((S, D), BF16), _sds((8, D), F32)] * extra,
        compiler_params=_cparams(1),
    )(*operands)
    if chained:
        return outs[0], list(outs[1:1 + n]), outs[1 + n], outs[2 + n]
    return outs[0], list(outs[1:])


def _shift_down(u, prev8, k):
    r = pltpu.roll(u, k, 0)
    p = pltpu.roll(prev8, k, 0)
    row = lax.broadcasted_iota(jnp.int32, prev8.shape, 0)
    top = jnp.where(row < k, p, r[0:8])
    return jnp.concatenate([top, r[8:]], axis=0)


def _shift_up(u, next8, k):
    tm = u.shape[0]
    r = pltpu.roll(u, tm - k, 0)
    p = pltpu.roll(next8, 8 - k, 0)
    row = lax.broadcasted_iota(jnp.int32, next8.shape, 0)
    bot = jnp.where(row >= 8 - k, p, r[tm - 8:tm])
    return jnp.concatenate([r[:tm - 8], bot], axis=0)


CONV_TILE = 512


def _halo_prev(col):
    return pl.BlockSpec((8, D), lambda i: (jnp.maximum(i * (CONV_TILE // 8) - 1, 0), col))


def _halo_next(col):
    last = S // 8 - 1
    return pl.BlockSpec((8, D), lambda i: (jnp.minimum((i + 1) * (CONV_TILE // 8), last), col))


def _conv_fwd(name, z, cw):
    def body(b_ref, c_ref, h_ref, cp_ref, hp_ref, cw_ref, o_ref):
        i = pl.program_id(0)
        u = c_ref[...].astype(F32) * h_ref[...].astype(F32)
        up = cp_ref[...].astype(F32) * hp_ref[...].astype(F32)
        up = jnp.where(i > 0, up, 0.0)
        cv = cw_ref[0:1, :] * _shift_down(u, up, 2) + cw_ref[1:2, :] * _shift_down(u, up, 1) + cw_ref[2:3, :] * u
        o_ref[...] = (b_ref[...].astype(F32) * cv).astype(o_ref.dtype)

    col = lambda c: pl.BlockSpec((CONV_TILE, D), lambda i: (i, c))
    return pl.pallas_call(
        body, name=name, grid=(S // CONV_TILE,),
        in_specs=[col(0), col(1), col(2), _halo_prev(1), _halo_prev(2), _vec_spec(8)],
        out_specs=_row_spec(CONV_TILE), out_shape=_sds((S, D), BF16),
        compiler_params=_cparams(1),
    )(z, z, z, z, z, cw)


def _conv_bwd(name, z, dpre, cw):
    nsteps = S // CONV_TILE

    def body(b_ref, c_ref, h_ref, cp_ref, hp_ref, dp_ref, dpn_ref, bn_ref, cw_ref, dz_ref, dcw_ref):
        i = pl.program_id(0)
        b = b_ref[...].astype(F32)
        c = c_ref[...].astype(F32)
        h = h_ref[...].astype(F32)
        dp = dp_ref[...].astype(F32)
        u = c * h
        up = jnp.where(i > 0, cp_ref[...].astype(F32) * hp_ref[...].astype(F32), 0.0)
        s1 = _shift_down(u, up, 1)
        s2 = _shift_down(u, up, 2)
        w0, w1, w2 = cw_ref[0:1, :], cw_ref[1:2, :], cw_ref[2:3, :]
        cv = w0 * s2 + w1 * s1 + w2 * u
        dcv = dp * b
        dcvn = jnp.where(i < nsteps - 1, dpn_ref[...].astype(F32) * bn_ref[...].astype(F32), 0.0)
        du = w2 * dcv + w1 * _shift_up(dcv, dcvn, 1) + w0 * _shift_up(dcv, dcvn, 2)
        dz_ref[:, 0:D] = (dp * cv).astype(dz_ref.dtype)
        dz_ref[:, D:2 * D] = (du * h).astype(dz_ref.dtype)
        dz_ref[:, 2 * D:3 * D] = (du * c).astype(dz_ref.dtype)

        @pl.when(i == 0)
        def _():
            dcw_ref[...] = jnp.zeros_like(dcw_ref)

        dcw_ref[0:1, :] += jnp.sum(dcv * s2, axis=0, keepdims=True)
        dcw_ref[1:2, :] += jnp.sum(dcv * s1, axis=0, keepdims=True)
        dcw_ref[2:3, :] += jnp.sum(dcv * u, axis=0, keepdims=True)

    col = lambda c: pl.BlockSpec((CONV_TILE, D), lambda i: (i, c))
    return pl.pallas_call(
        body, name=name, grid=(nsteps,),
        in_specs=[col(0), col(1), col(2), _halo_prev(1), _halo_prev(2),
                  _row_spec(CONV_TILE), _halo_next(0), _halo_next(0), _vec_spec(8)],
        out_specs=[pl.BlockSpec((CONV_TILE, 3 * D), lambda i: (i, 0)), _vec_spec(8)],
        out_shape=[_sds((S, 3 * D), BF16), _sds((8, D), F32)],
        compiler_params=_cparams(1),
    )(z, z, z, z, z, dpre, dpre, z, cw)


FFN_TM = 2048
_GU_BLOCK = pl.BlockSpec((2, None, FFN_TM, FB), lambda i, j: (0, j, i, 0))


def _gate_up_act(name, a, wg):
    kdim = a.shape[1]

    def body(a_ref, wgate_ref, wup_ref, gu_ref, act_ref):
        x = a_ref[...]
        g = _dot_nn(x, wgate_ref[...])
        u = _dot_nn(x, wup_ref[...])
        gu_ref[0] = g.astype(gu_ref.dtype)
        gu_ref[1] = u.astype(gu_ref.dtype)
        act_ref[...] = (g * jax.nn.sigmoid(g) * u).astype(act_ref.dtype)

    return pl.pallas_call(
        body, name=name, grid=(S // FFN_TM, NFB),
        in_specs=[pl.BlockSpec((FFN_TM, kdim), lambda i, j: (i, 0)),
                  pl.BlockSpec((None, kdim, FB), lambda i, j: (j, 0, 0)),
                  pl.BlockSpec((None, kdim, FB), lambda i, j: (j + NFB, 0, 0))],
        out_specs=[_GU_BLOCK, pl.BlockSpec((None, FFN_TM, FB), lambda i, j: (j, i, 0))],
        out_shape=[_sds((2, NFB, S, FB), BF16), _sds((NFB, S, FB), BF16)],
        compiler_params=_cparams(2),
    )(a, wg, wg)


def _down_dx_act_bwd(name, df, w4, gu):
    _, kb, n = w4.shape

    def body(df_ref, w_ref, gu_ref, o_ref):
        d = _dot_nt(df_ref[...], w_ref[...])
        g = gu_ref[0].astype(F32)
        u = gu_ref[1].astype(F32)
        sg = jax.nn.sigmoid(g)
        o_ref[0] = (d * u * sg * (1.0 + g * (1.0 - sg))).astype(o_ref.dtype)
        o_ref[1] = (d * g * sg).astype(o_ref.dtype)

    return pl.pallas_call(
        body, name=name, grid=(S // FFN_TM, NFB),
        in_specs=[pl.BlockSpec((FFN_TM, n), lambda i, j: (i, 0)), pl.BlockSpec((None, kb, n), lambda i, j: (j, 0, 0)),
                  _GU_BLOCK],
        out_specs=_GU_BLOCK, out_shape=_sds((2, NFB, S, FB), BF16),
        compiler_params=_cparams(2),
    )(df, w4, gu)


def _rope_tables(name, pos_col, inv_freq_row):
    def body(pos_ref, f_ref, cos_ref, sin_ref):
        ang = pos_ref[...].astype(F32) * f_ref[...]
        lane = lax.broadcasted_iota(jnp.int32, ang.shape, 1)
        s = jnp.sin(ang)
        cos_ref[...] = jnp.cos(ang)
        sin_ref[...] = jnp.where((lane % HEAD_DIM) < HEAD_DIM // 2, -s, s)

    tab = pl.BlockSpec((ROW_TILE, 128), lambda i: (i, 0))
    return pl.pallas_call(
        body, name=name, grid=(S // ROW_TILE,),
        in_specs=[pl.BlockSpec((ROW_TILE, 1), lambda i: (i, 0)), _vec_spec(1, 128)],
        out_specs=[tab, tab], out_shape=[_sds((S, 128), F32)] * 2,
        compiler_params=_cparams(1),
    )(pos_col, inv_freq_row)


def _swap_halves(t):
    lane = lax.broadcasted_iota(jnp.int32, t.shape, 1)
    first = (lane % HEAD_DIM) < HEAD_DIM // 2
    return jnp.where(first, pltpu.roll(t, 128 - HEAD_DIM // 2, 1), pltpu.roll(t, HEAD_DIM // 2, 1))


NCHUNK = D // 128


def _chunk(c, base=0):
    return slice(base + c * 128, base + (c + 1) * 128)


def _class_rows(r, d, tm):
    return pl.ds(r, tm // d, stride=d) if d > 1 else slice(None)


def _class_block(d, tm):
    return pl.BlockSpec((tm // d, d * D), lambda i: (i, 0))


def _tokens_from_classes(blk_ref, tmp_ref, d, tm):
    for r in range(d):
        for c in range(NCHUNK):
            tmp_ref[c, _class_rows(r, d, tm), :] = blk_ref[:, _chunk(c, r * D)].astype(F32)


def _classes_from_tokens(tmp_ref, blk_ref, d, tm):
    for r in range(d):
        for c in range(NCHUNK):
            blk_ref[:, _chunk(c, r * D)] = tmp_ref[c, _class_rows(r, d, tm), :].astype(blk_ref.dtype)


def _qkv_classes(name, n2, nk, wq, wkv, g, d, tables):
    def emit(acc, cos_ref, sin_ref, o_ref, tmp_ref, scale):
        for c in range(NCHUNK):
            tmp_ref[c] = acc[:, _chunk(c)]
        for r in range(d):
            rows = _class_rows(r, d, TM)
            if scale is not None:
                cs = cos_ref[rows, :]
                sn = sin_ref[rows, :]
            for c in range(NCHUNK):
                x = tmp_ref[c, rows, :]
                if scale is not None:
                    x = (x * cs + _swap_halves(x) * sn) * scale
                o_ref[:, _chunk(c, r * D)] = x.astype(o_ref.dtype)

    def body(n2_ref, nk_ref, wq_ref, wk_ref, wv_ref, cos_ref, sin_ref, q_ref, k_ref, v_ref, tmp_ref):
        emit(_dot_nn(n2_ref[...], wq_ref[...]), cos_ref, sin_ref, q_ref, tmp_ref, HEAD_DIM ** -0.5)
        x = nk_ref[...]
        emit(_dot_nn(x, wk_ref[...]), cos_ref, sin_ref, k_ref, tmp_ref, 1.0)
        emit(_dot_nn(x, wv_ref[...]), cos_ref, sin_ref, v_ref, tmp_ref, None)

    nbr = len(DILATIONS)
    act = pl.BlockSpec((TM, D), lambda i: (i, 0))
    tab = pl.BlockSpec((TM, 128), lambda i: (i, 0))
    wcol = lambda col: pl.BlockSpec((D, D), lambda i: (0, col))
    return pl.pallas_call(
        body, name=name, grid=(S // TM,),
        in_specs=[act, act, wcol(g), wcol(g), wcol(nbr + g), tab, tab],
        out_specs=[_class_block(d, TM)] * 3, out_shape=[_sds((S // d, d * D), BF16)] * 3,
        scratch_shapes=[pltpu.VMEM((NCHUNK, TM, 128), F32)],
        compiler_params=_cparams(1),
    )(n2, nk, wq, wkv, wkv, *tables)


ATTN_CHAINS = 16


def _attn_units(d):
    nblk = S // d // BAND
    return max(1, 2 * ATTN_CHAINS // nblk)


def _class_spec(d):
    return pl.BlockSpec((S // d, 128 * _attn_units(d)), lambda cb: (0, cb))


def _dot_nt(a, b):
    return lax.dot_general(a, b, _DIMS["nt"], preferred_element_type=F32)


def _dot_tn(a, b):
    return lax.dot_general(a, b, _DIMS["tn"], preferred_element_type=F32)


def _dot_nn(a, b):
    return lax.dot_general(a, b, _DIMS["nn"], preferred_element_type=F32)


def _band_mask(nkeys):
    qi = lax.broadcasted_iota(jnp.int32, (2 * BAND, nkeys), 0) % BAND
    kj = lax.broadcasted_iota(jnp.int32, (2 * BAND, nkeys), 1)
    if nkeys == BAND:
        return kj <= qi
    dist = qi + BAND - kj
    return (dist >= 0) & (dist <= BAND)


def _band_bias():
    return {n: jnp.where(_band_mask(n), 0.0, NEG_INF).astype(F32) for n in (BAND, 2 * BAND)}


def _stack_heads(x):
    row = lax.broadcasted_iota(jnp.int32, (2 * BAND, 128), 0)
    lane = lax.broadcasted_iota(jnp.int32, (2 * BAND, 128), 1)
    keep = (row < BAND) == (lane < HEAD_DIM)
    return jnp.where(keep, jnp.concatenate([x, x], axis=0), jnp.zeros((), x.dtype))


def _unstack(x2):
    first_head = lax.broadcasted_iota(jnp.int32, (BAND, 128), 1) < HEAD_DIM
    return jnp.where(first_head, x2[:BAND], x2[BAND:])


def _for_later_blocks(nblk, units, fn):
    all_lanes = [slice(u * 128, (u + 1) * 128) for u in range(units)]
    unroll = max(1, ATTN_CHAINS // units)
    trips = (nblk - 1) // unroll
    if trips > 1:
        def step(i, carry):
            for j in range(unroll):
                for lanes in all_lanes:
                    fn(pl.multiple_of((1 + i * unroll + j) * BAND, BAND), lanes)
            return carry

        lax.fori_loop(0, trips, step, 0)
    else:
        trips = 0
    for sb in range(1 + trips * unroll, nblk):
        for lanes in all_lanes:
            fn(sb * BAND, lanes)


def _attn_fwd(name, q, k, v, d):
    nblk = S // d // BAND
    units = _attn_units(d)

    def body(q_ref, k_ref, v_ref, o_ref, lse_ref):
        bias = _band_bias()

        def block(r0, k0, nkeys, lanes):
            q2 = _stack_heads(q_ref[pl.ds(r0, BAND), lanes])
            s = _dot_nt(q2, k_ref[pl.ds(k0, nkeys), lanes]) + bias[nkeys]
            m = jnp.max(s, axis=-1, keepdims=True)
            p = jnp.exp(s - m)
            l = jnp.sum(p, axis=-1, keepdims=True)
            o2 = _dot_nn(p.astype(BF16), v_ref[pl.ds(k0, nkeys), lanes])
            l_tile = _unstack(jnp.broadcast_to(l, (2 * BAND, 128)))
            m_tile = _unstack(jnp.broadcast_to(m, (2 * BAND, 128)))
            o_ref[pl.ds(r0, BAND), lanes] = (_unstack(o2) / l_tile).astype(o_ref.dtype)
            lse_ref[pl.ds(r0, BAND), lanes] = m_tile + jnp.log(l_tile)

        for u in range(units):
            block(0, 0, BAND, slice(u * 128, (u + 1) * 128))

        _for_later_blocks(nblk, units, lambda r0, lanes: block(r0, r0 - BAND, 2 * BAND, lanes))

    spec = _class_spec(d)
    return pl.pallas_call(
        body, name=name, grid=(8 * d // units,),
        in_specs=[spec] * 3, out_specs=[spec] * 2,
        out_shape=[_sds((S // d, d * D), BF16), _sds((S // d, d * D), F32)],
        compiler_params=_cparams(1),
    )(q, k, v)


def _attn_bwd(name, q, k, v, do, lse, dd, d):
    nblk = S // d // BAND
    units = _attn_units(d)

    def body(q_ref, k_ref, v_ref, do_ref, lse_ref, dd_ref, dq_ref, dk_out, dv_out, dk_ref, dv_ref):
        bias = _band_bias()
        def column(ref, r0, lanes, nkeys):
            tile = ref[pl.ds(r0, BAND), lanes]
            other = pltpu.roll(tile, HEAD_DIM, 1)
            first_head = lax.broadcasted_iota(jnp.int32, tile.shape, 1) < HEAD_DIM
            both = jnp.concatenate([jnp.where(first_head, tile, other), jnp.where(first_head, other, tile)], axis=0)
            return both if nkeys == BAND else jnp.concatenate([both, both], axis=1)

        def block(r0, k0, nkeys, lanes, first):
            q2 = _stack_heads(q_ref[pl.ds(r0, BAND), lanes])
            do2 = _stack_heads(do_ref[pl.ds(r0, BAND), lanes])
            kk = k_ref[pl.ds(k0, nkeys), lanes]
            vv = v_ref[pl.ds(k0, nkeys), lanes]
            s = _dot_nt(q2, kk) + bias[nkeys]
            p = jnp.exp(s - column(lse_ref, r0, lanes, nkeys))
            ds = (p * (_dot_nt(do2, vv) - column(dd_ref, r0, lanes, nkeys))).astype(BF16)
            dq_ref[pl.ds(r0, BAND), lanes] = _unstack(_dot_nn(ds, kk)).astype(dq_ref.dtype)
            dk_part = _dot_tn(ds, q2)
            dv_part = _dot_tn(p.astype(BF16), do2)
            if first:
                dk_ref[pl.ds(k0, nkeys), lanes] = dk_part
                dv_ref[pl.ds(k0, nkeys), lanes] = dv_part
            else:
                dk_ref[pl.ds(k0, BAND), lanes] += dk_part[:BAND]
                dv_ref[pl.ds(k0, BAND), lanes] += dv_part[:BAND]
                dk_ref[pl.ds(k0 + BAND, BAND), lanes] = dk_part[BAND:]
                dv_ref[pl.ds(k0 + BAND, BAND), lanes] = dv_part[BAND:]

        for u in range(units):
            block(0, 0, BAND, slice(u * 128, (u + 1) * 128), True)

        _for_later_blocks(nblk, units, lambda r0, lanes: block(r0, r0 - BAND, 2 * BAND, lanes, False))
        dk_out[...] = dk_ref[...].astype(dk_out.dtype)
        dv_out[...] = dv_ref[...].astype(dv_out.dtype)

    spec = _class_spec(d)
    return pl.pallas_call(
        body, name=name, grid=(8 * d // units,),
        in_specs=[spec] * 6, out_specs=[spec] * 3,
        out_shape=[_sds((S // d, d * D), BF16)] * 3,
        scratch_shapes=[pltpu.VMEM((S // d, 128 * units), F32)] * 2,
        compiler_params=_cparams(1),
    )(q, k, v, do, lse, dd)


MIX_TILE = 256
DILATIONS = tuple(d for _, d in BRANCHES)


def _branch_weights(la, lb, lc):
    m = jnp.maximum(jnp.maximum(la, lb), lc)
    ea, eb, ec = jnp.exp(la - m), jnp.exp(lb - m), jnp.exp(lc - m)
    inv = 1.0 / (ea + eb + ec)
    return ea * inv, eb * inv, ec * inv


def _mix_operands(outs, lses):
    specs = [_class_block(d, MIX_TILE) for d in DILATIONS] * 2
    scratch = [pltpu.VMEM((NCHUNK, MIX_TILE, 128), F32)] * 4
    return list(outs) + list(lses), specs, scratch


def _mix_fwd(name, outs, lses):
    def body(o0, o1, o2, l0, l1, l2, o_ref, to1, to2, tl1, tl2):
        for blk, tmp, d in ((o1, to1, DILATIONS[1]), (o2, to2, DILATIONS[2]), (l1, tl1, DILATIONS[1]), (l2, tl2, DILATIONS[2])):
            _tokens_from_classes(blk, tmp, d, MIX_TILE)
        for c in range(NCHUNK):
            wa, wb, wc = _branch_weights(l0[:, _chunk(c)], tl1[c], tl2[c])
            o_ref[:, _chunk(c)] = (wa * o0[:, _chunk(c)].astype(F32) + wb * to1[c] + wc * to2[c]).astype(o_ref.dtype)

    operands, specs, scratch = _mix_operands(outs, lses)
    return pl.pallas_call(
        body, name=name, grid=(S // MIX_TILE,),
        in_specs=specs, out_specs=_row_spec(MIX_TILE), out_shape=_sds((S, D), BF16),
        scratch_shapes=scratch, compiler_params=_cparams(1),
    )(*operands)


def _head_sum(x, ones_blockdiag):
    hi = x.astype(BF16)
    lo = (x - hi.astype(F32)).astype(BF16)
    return _dot_nn(hi, ones_blockdiag) + _dot_nn(lo, ones_blockdiag)


def _mix_bwd(name, do, outs, lses, ones_blockdiag):
    def body(do_ref, o0, o1, o2, l0, l1, l2, ones_ref, d0, d1, d2, t0, t1, t2,
             to1, to2, tl1, tl2, td1, td2, tt1, tt2):
        for blk, tmp, d in ((o1, to1, DILATIONS[1]), (o2, to2, DILATIONS[2]), (l1, tl1, DILATIONS[1]), (l2, tl2, DILATIONS[2])):
            _tokens_from_classes(blk, tmp, d, MIX_TILE)
        ones = ones_ref[...]
        for c in range(NCHUNK):
            w = _branch_weights(l0[:, _chunk(c)], tl1[c], tl2[c])
            dov = do_ref[:, _chunk(c)]
            o = w[0] * o0[:, _chunk(c)].astype(F32) + w[1] * to1[c] + w[2] * to2[c]
            t = _head_sum(dov * o, ones)
            d0[:, _chunk(c)] = (w[0] * dov).astype(d0.dtype)
            t0[:, _chunk(c)] = w[0] * t
            td1[c], tt1[c] = w[1] * dov, w[1] * t
            td2[c], tt2[c] = w[2] * dov, w[2] * t
        for tmp, blk, d in ((td1, d1, DILATIONS[1]), (tt1, t1, DILATIONS[1]), (td2, d2, DILATIONS[2]), (tt2, t2, DILATIONS[2])):
            _classes_from_tokens(tmp, blk, d, MIX_TILE)

    operands, specs, scratch = _mix_operands(outs, lses)
    out_specs = [_class_block(d, MIX_TILE) for d in DILATIONS] * 2
    out_shape = [_sds((S // d, d * D), BF16) for d in DILATIONS] + [_sds((S // d, d * D), F32) for d in DILATIONS]
    return pl.pallas_call(
        body, name=name, grid=(S // MIX_TILE,),
        in_specs=[_row_spec(MIX_TILE)] + specs + [_vec_spec(128, 128)],
        out_specs=out_specs, out_shape=out_shape,
        scratch_shapes=scratch + [pltpu.VMEM((NCHUNK, MIX_TILE, 128), F32)] * 4,
        compiler_params=_cparams(1),
    )(do, *operands, ones_blockdiag)


def _attn_bwd_post(name, grads, cos_t, sin_t):
    tm = MIX_TILE
    scale = HEAD_DIM ** -0.5

    def unrope(x, cs, sn):
        return x * cs - _swap_halves(x) * sn

    def body(*refs):
        in_refs = refs[:9]
        cos_ref, sin_ref, dq_ref, dkv_ref, tmp_ref = refs[9:]
        cs = cos_ref[...]
        sn = sin_ref[...]
        for g, d in enumerate(DILATIONS):
            for which, blk in enumerate(in_refs[3 * g:3 * g + 3]):
                if d > 1:
                    _tokens_from_classes(blk, tmp_ref, d, tm)
                for c in range(NCHUNK):
                    x = tmp_ref[c] if d > 1 else blk[:, _chunk(c)].astype(F32)
                    if which == 0:
                        dq_ref[:, _chunk(c, g * D)] = (unrope(x, cs, sn) * scale).astype(dq_ref.dtype)
                    elif which == 1:
                        dkv_ref[:, _chunk(c, g * D)] = unrope(x, cs, sn).astype(dkv_ref.dtype)
                    else:
                        dkv_ref[:, _chunk(c, QW + g * D)] = x.astype(dkv_ref.dtype)

    operands = [a for branch in grads for a in branch]
    tab = pl.BlockSpec((tm, 128), lambda i: (i, 0))
    return pl.pallas_call(
        body, name=name, grid=(S // tm,),
        in_specs=[_class_block(d, tm) for d in DILATIONS for _ in range(3)] + [tab, tab],
        out_specs=[pl.BlockSpec((tm, QW), lambda i: (i, 0)), pl.BlockSpec((tm, 2 * QW), lambda i: (i, 0))],
        out_shape=[_sds((S, QW), BF16), _sds((S, 2 * QW), BF16)],
        scratch_shapes=[pltpu.VMEM((NCHUNK, tm, 128), F32)],
        compiler_params=_cparams(1),
    )(*operands, cos_t, sin_t)


def _adamw(name, parts, w, m, v, layer=None, other=None):
    n, rows, cols = parts.shape
    tr = rows
    for cand in (256, 176, 128, 64, 32, 16, 8):
        if rows % cand == 0:
            tr = cand
            break
    n_other = 0 if other is None else len(other)

    def body(p_ref, w_ref, m_ref, v_ref, *refs):
        g_ref, d_ref, nm_ref, nv_ref = refs[n_other:]
        g = p_ref[0].astype(F32)
        for j in range(1, n):
            g = g + p_ref[j].astype(F32)
        g_ref[...] = g
        d_ref[...], nm_ref[...], nv_ref[...] = _adam_update(g, w_ref[...], m_ref[...], v_ref[...])

    if layer is None:
        blk = pl.BlockSpec((tr, cols), lambda i: (i, 0))
        shape = (rows, cols)
    else:
        blk = pl.BlockSpec((None, tr, cols), lambda i: (layer, i, 0))
        shape = w.shape
    return pl.pallas_call(
        body, name=name, grid=(rows // tr,),
        in_specs=[pl.BlockSpec((n, tr, cols), lambda i: (0, i, 0)), blk, blk, blk]
                 + [pl.BlockSpec(memory_space=pl.ANY)] * n_other,
        out_specs=[blk] * 4, out_shape=[_sds(shape, F32)] * 4,
        input_output_aliases={4 + i: i for i in range(n_other)},
        compiler_params=_cparams(1),
    )(parts, w, m, v, *(other or ()))


def _adam_update(g, w, m, v):
    c1 = 1.0 / (1.0 - ADAM_B1 ** ADAM_STEP)
    c2 = 1.0 / (1.0 - ADAM_B2 ** ADAM_STEP)
    nm = ADAM_B1 * m + (1.0 - ADAM_B1) * g
    nv = ADAM_B2 * v + (1.0 - ADAM_B2) * (g * g)
    return -ADAM_LR * ((nm * c1) / (jnp.sqrt(nv * c2) + ADAM_EPS) + ADAM_WD * w), nm, nv


GAIN_ROWS = 16


def _pack_small(name, gain_tiles, taps, sq):
    ng = len(gain_tiles)

    def body(*refs):
        o_ref = refs[-1]
        o_ref[...] = jnp.zeros_like(o_ref)
        for i in range(ng):
            o_ref[i:i + 1, :] = refs[i][0:1, :]
        o_ref[ng:ng + 3, :] = refs[ng][0:3, :]
        o_ref[ng + 3:ng + 4, :] = refs[ng + 1][...]

    return pl.pallas_call(body, name=name, out_shape=_sds((GAIN_ROWS, D), F32))(*gain_tiles, taps, sq)


def _adamw_gains(name, parts, params):
    np_ = len(params)
    shapes = [w.shape for w, _, _ in params]

    def body(p_ref, *refs):
        ins, outs = refs[:3 * np_], refs[3 * np_:]

        def total(lo, rows):
            g = p_ref[0, lo:lo + rows, :]
            for j in range(1, NDEV):
                g = g + p_ref[j, lo:lo + rows, :]
            return g

        lo = 0
        for i, shape in enumerate(shapes):
            g = total(lo, shape[0])
            lo += shape[0]
            w_ref, m_ref, v_ref = ins[3 * i:3 * i + 3]
            g_ref, d_ref, nm_ref, nv_ref = outs[4 * i:4 * i + 4]
            g_ref[...] = g
            d_ref[...], nm_ref[...], nv_ref[...] = _adam_update(g, w_ref[...], m_ref[...], v_ref[...])
        taps_ref, loss_ref = outs[-2], outs[-1]
        taps_ref[...] = jnp.zeros_like(taps_ref)
        taps_ref[0:3, :] = total(lo, 3)
        loss_ref[...] = jnp.sum(total(lo + 3, 1), axis=-1, keepdims=True) * (0.5 / D)

    out_shape = [_sds(shape, F32) for shape in shapes for _ in range(4)] + [_sds((8, D), F32), _sds((1, 1), F32)]
    outs = pl.pallas_call(body, name=name, out_shape=out_shape)(parts, *[a for p in params for a in p])
    return [list(outs[4 * i:4 * i + 4]) for i in range(np_)], outs[-2], outs[-1].reshape(())


def _exchange(name, arrays, kind, after):
    n = len(arrays)
    gather = kind == "gather"
    out_shape = [_sds((NDEV,) + a.shape if gather else a.shape, a.dtype) for a in arrays]

    def body(*refs):
        srcs, outs = refs[:n], refs[n + 1:2 * n + 1]
        send_sems, recv_sems, local_sems = refs[2 * n + 1:]
        x, y, c = lax.axis_index("x"), lax.axis_index("y"), lax.axis_index("c")
        me = 4 * x + 2 * y + c
        pending = []
        for t in range(n):
            own = pltpu.make_async_copy(srcs[t] if gather else srcs[t].at[me], outs[t].at[me], local_sems.at[t])
            own.start()
            pending.append(own)
            for rel in range(1, NDEV):
                px = 1 - x if rel & 4 else x
                py = 1 - y if rel & 2 else y
                pc = 1 - c if rel & 1 else c
                peer = 4 * px + 2 * py + pc
                send = pltpu.make_async_remote_copy(
                    src_ref=srcs[t] if gather else srcs[t].at[peer], dst_ref=outs[t].at[me],
                    send_sem=send_sems.at[t, rel - 1], recv_sem=recv_sems.at[t, rel - 1],
                    device_id=(px, py, pc), device_id_type=MESH)
                send.start()
                arrive = pltpu.make_async_remote_copy(
                    src_ref=srcs[t] if gather else srcs[t].at[me], dst_ref=outs[t].at[peer],
                    send_sem=send_sems.at[t, rel - 1], recv_sem=recv_sems.at[t, rel - 1],
                    device_id=(px, py, pc), device_id_type=MESH)
                pending.append((send, arrive))
        for item in pending:
            if isinstance(item, tuple):
                item[0].wait_send()
                item[1].wait_recv()
            else:
                item.wait()

    any_spec = pl.BlockSpec(memory_space=pl.ANY)
    outs = pl.pallas_call(
        body, name=name,
        in_specs=[any_spec] * (n + 1), out_specs=[any_spec] * n, out_shape=out_shape,
        scratch_shapes=[pltpu.SemaphoreType.DMA((n, NDEV - 1)), pltpu.SemaphoreType.DMA((n, NDEV - 1)),
                        pltpu.SemaphoreType.DMA((n,))],
    )(*arrays, after)
    return list(outs)


_HBM_SPEC = pl.BlockSpec(memory_space=pltpu.HBM)
_SEM_SPEC = pl.BlockSpec(memory_space=pltpu.SEMAPHORE)
_DATAFLOW = pltpu.SideEffectType.DATAFLOW_SIDE_EFFECTING


def _peers():
    x, y, c = lax.axis_index("x"), lax.axis_index("y"), lax.axis_index("c")
    out = []
    for rel in range(1, NDEV):
        px = 1 - x if rel & 4 else x
        py = 1 - y if rel & 2 else y
        pc = 1 - c if rel & 1 else c
        out.append((rel - 1, (px, py, pc), 4 * px + 2 * py + pc))
    return 4 * x + 2 * y + c, out


def _hbm(a):
    return pltpu.HBM(a.shape, a.dtype)


def _own_slot(a, me, kind):
    mine = a[None] if kind == "gather" else lax.dynamic_slice_in_dim(a, me, 1, axis=0)
    shape = (NDEV,) + mine.shape[1:]
    return lax.dynamic_update_slice_in_dim(lax.empty(shape, a.dtype), mine, me, axis=0)


def _exchange_start(name, arrays, me, kind):
    n = len(arrays)
    gather = kind == "gather"
    lands = [_own_slot(a, me, kind) for a in arrays]

    def body(*refs):
        src_refs, land_refs = refs[:n], refs[n:2 * n]
        send_sems, recv_sems = refs[2 * n], refs[2 * n + 1]
        token = refs[-1]
        my_block, peers = _peers()
        for t in range(n):
            for slot, dev, block in peers:
                pltpu.make_async_remote_copy(
                    src_ref=src_refs[t] if gather else src_refs[t].at[block], dst_ref=land_refs[t].at[my_block],
                    send_sem=send_sems.at[t * (NDEV - 1) + slot], recv_sem=recv_sems.at[t * (NDEV - 1) + slot],
                    device_id=dev, device_id_type=MESH).start()
        token[...] = jnp.zeros_like(token)

    operands = [pltpu.with_memory_space_constraint(a, pltpu.HBM) for a in list(arrays) + lands]
    outs = pl.pallas_call(
        body, name=name,
        out_shape=(pltpu.SemaphoreType.DMA((n * (NDEV - 1),)), pltpu.SemaphoreType.DMA((n * (NDEV - 1),)),
                   *[_hbm(a) for a in operands], _sds((8, 128), F32)),
        in_specs=[_HBM_SPEC] * (2 * n),
        out_specs=(_SEM_SPEC, _SEM_SPEC, *[_HBM_SPEC] * (2 * n), pl.BlockSpec(memory_space=pltpu.VMEM)),
        input_output_aliases={i: 2 + i for i in range(2 * n)},
        compiler_params=pltpu.CompilerParams(has_side_effects=_DATAFLOW),
    )(*operands)
    return (outs[0], outs[1], list(outs[2:2 + n]), list(outs[2 + n:2 + 2 * n])), outs[-1]


def _exchange_wait(name, started, t, after, kind):
    send_sems, recv_sems, srcs, lands = started
    gather = kind == "gather"

    def body(src_ref, land_ref, send_ref, recv_ref, after_ref, src_out, land_out):
        _, peers = _peers()
        for slot, dev, block in peers:
            copy = pltpu.make_async_remote_copy(
                src_ref=src_ref if gather else src_ref.at[block], dst_ref=land_ref.at[block],
                send_sem=send_ref.at[t * (NDEV - 1) + slot], recv_sem=recv_ref.at[t * (NDEV - 1) + slot],
                device_id=dev, device_id_type=MESH)
            copy.wait_send()
            copy.wait_recv()

    return pl.pallas_call(
        body, name=name, out_shape=(_hbm(srcs[t]), _hbm(lands[t])),
        in_specs=(_HBM_SPEC, _HBM_SPEC, _SEM_SPEC, _SEM_SPEC, pl.BlockSpec(memory_space=pl.ANY)),
        out_specs=(_HBM_SPEC, _HBM_SPEC), input_output_aliases={0: 0, 1: 1},
        compiler_params=pltpu.CompilerParams(has_side_effects=_DATAFLOW),
    )(srcs[t], lands[t], send_sems, recv_sems, after)[1]


DIRECT_RELS = (1, 2, 4, 6)
RELAY_RELS = (2, 4, 6)


def _rel_peer(rel):
    x, y, c = lax.axis_index("x"), lax.axis_index("y"), lax.axis_index("c")
    px = 1 - x if rel & 4 else x
    py = 1 - y if rel & 2 else y
    pc = 1 - c if rel & 1 else c
    return (px, py, pc), 4 * px + 2 * py + pc


def _gather_start(name, shards, me):
    n, nr = len(shards), len(DIRECT_RELS)
    lands = [_own_slot(a, me, "gather") for a in shards]

    def body(*refs):
        src_refs, land_refs = refs[:n], refs[n:2 * n]
        send_sems, recv_sems = refs[2 * n], refs[2 * n + 1]
        _, my_block = _rel_peer(0)
        for t in range(n):
            for s, rel in enumerate(DIRECT_RELS):
                dev, _ = _rel_peer(rel)
                pltpu.make_async_remote_copy(
                    src_ref=src_refs[t], dst_ref=land_refs[t].at[my_block],
                    send_sem=send_sems.at[t * nr + s], recv_sem=recv_sems.at[t * nr + s],
                    device_id=dev, device_id_type=MESH).start()

    operands = [pltpu.with_memory_space_constraint(a, pltpu.HBM) for a in list(shards) + lands]
    outs = pl.pallas_call(
        body, name=name,
        out_shape=(pltpu.SemaphoreType.DMA((n * nr,)), pltpu.SemaphoreType.DMA((n * nr,)), *[_hbm(a) for a in operands]),
        in_specs=[_HBM_SPEC] * (2 * n), out_specs=(_SEM_SPEC, _SEM_SPEC, *[_HBM_SPEC] * (2 * n)),
        input_output_aliases={i: 2 + i for i in range(2 * n)},
        compiler_params=pltpu.CompilerParams(has_side_effects=_DATAFLOW),
    )(*operands)
    return outs[0], outs[1], list(outs[2:2 + n]), list(outs[2 + n:2 + 2 * n])


def _gather_wait(name, started, ts, after):
    send_sems, recv_sems, srcs, lands = started
    m, nr = len(ts), len(DIRECT_RELS)

    def body(*refs):
        src_refs, land_refs = refs[:m], refs[m:2 * m]
        send_ref, recv_ref = refs[2 * m], refs[2 * m + 1]
        for i, t in enumerate(ts):
            for s, rel in enumerate(DIRECT_RELS):
                dev, block = _rel_peer(rel)
                copy = pltpu.make_async_remote_copy(
                    src_ref=src_refs[i], dst_ref=land_refs[i].at[block],
                    send_sem=send_ref.at[t * nr + s], recv_sem=recv_ref.at[t * nr + s],
                    device_id=dev, device_id_type=MESH)
                copy.wait_send()
                copy.wait_recv()

    operands = [srcs[t] for t in ts] + [lands[t] for t in ts]
    outs = pl.pallas_call(
        body, name=name, out_shape=tuple(_hbm(a) for a in operands),
        in_specs=[_HBM_SPEC] * (2 * m) + [_SEM_SPEC, _SEM_SPEC, pl.BlockSpec(memory_space=pl.ANY)],
        out_specs=tuple([_HBM_SPEC] * (2 * m)), input_output_aliases={i: i for i in range(2 * m)},
        compiler_params=pltpu.CompilerParams(has_side_effects=_DATAFLOW),
    )(*operands, send_sems, recv_sems, after)
    return list(outs[m:])


def _relay_start(name, lands):
    m, nr = len(lands), len(RELAY_RELS)

    def body(*refs):
        land_refs, send_sems, recv_sems = refs[:m], refs[m], refs[m + 1]
        sibling, _ = _rel_peer(1)
        for i in range(m):
            for s, rel in enumerate(RELAY_RELS):
                _, block = _rel_peer(rel)
                pltpu.make_async_remote_copy(
                    src_ref=land_refs[i].at[block], dst_ref=land_refs[i].at[block],
                    send_sem=send_sems.at[i * nr + s], recv_sem=recv_sems.at[i * nr + s],
                    device_id=sibling, device_id_type=MESH).start()

    outs = pl.pallas_call(
        body, name=name,
        out_shape=(pltpu.SemaphoreType.DMA((m * nr,)), pltpu.SemaphoreType.DMA((m * nr,)), *[_hbm(a) for a in lands]),
        in_specs=[_HBM_SPEC] * m, out_specs=(_SEM_SPEC, _SEM_SPEC, *[_HBM_SPEC] * m),
        input_output_aliases={i: 2 + i for i in range(m)},
        compiler_params=pltpu.CompilerParams(has_side_effects=_DATAFLOW),
    )(*lands)
    return outs[0], outs[1], list(outs[2:])


def _relay_wait(name, relayed, after):
    send_sems, recv_sems, lands = relayed
    m, nr = len(lands), len(RELAY_RELS)

    def body(*refs):
        land_refs, send_ref, recv_ref = refs[:m], refs[m], refs[m + 1]
        sibling, _ = _rel_peer(1)
        for i in range(m):
            for s, rel in enumerate(RELAY_RELS):
                _, sent = _rel_peer(rel)
                _, arriving = _rel_peer(rel ^ 1)
                copy = pltpu.make_async_remote_copy(
                    src_ref=land_refs[i].at[sent], dst_ref=land_refs[i].at[arriving],
                    send_sem=send_ref.at[i * nr + s], recv_sem=recv_ref.at[i * nr + s],
                    device_id=sibling, device_id_type=MESH)
                copy.wait_send()
                copy.wait_recv()

    outs = pl.pallas_call(
        body, name=name, out_shape=tuple(_hbm(a) for a in lands),
        in_specs=[_HBM_SPEC] * m + [_SEM_SPEC, _SEM_SPEC, pl.BlockSpec(memory_space=pl.ANY)],
        out_specs=tuple([_HBM_SPEC] * m), input_output_aliases={i: i for i in range(m)},
        compiler_params=pltpu.CompilerParams(has_side_effects=_DATAFLOW),
    )(*lands, send_sems, recv_sems, after)
    return list(outs)


def _ffn_fwd(tag, n, wg, wd):
    gu, act = _gate_up_act(f"ffn_gate_up_{tag}", n, wg)
    wd4 = wd.reshape(NFB, FB, D)
    f = _fwd_kblocked(f"ffn_down_{tag}", act, wd4)
    return (n, gu, act, wg, wd4), f


def _ffn_bwd(tag, dh_out, df, h_in, saved, g_pre, send, mixer):
    n, gu, act, wg, wd4 = saved
    dwd = _bwd_w_kblocked(f"ffn_down_dw_{tag}", act, df).reshape(NDEV, DFF // NDEV, D)
    dgu = _down_dx_act_bwd(f"ffn_down_dx_{tag}", df, wd4, gu).reshape(NDEV, S, FB)
    tok = send({f"down_{tag}": dwd, f"gate_up_{tag}": _bwd_w_cols_blocked(f"ffn_gate_up_dw_{tag}", n, dgu)})
    dn = _bwd_x_cols_blocked(f"ffn_gate_up_dx_{tag}", dgu, wg, after=tok)
    dh_in, (dg_pre,), dy, dg_mixer = _rms_bwd(f"ffn_prenorm_bwd_{tag}", h_in, [(g_pre, dn)], dh_out, F32, then=mixer)
    return dh_in, dg_pre, dy, dg_mixer


def kernel(x, positions, mix_norm_pre, mix_norm_post, ffn_norm_pre, ffn_norm_post, ffn_w_gate_up, ffn_w_down, conv_w_in, conv_w, conv_w_out, kv_norm, w_kv, w_q, w_o, loss_target, m_mix_norm_pre, m_mix_norm_post, m_ffn_norm_pre, m_ffn_norm_post, m_ffn_w_gate_up, m_ffn_w_down, m_conv_w_in, m_conv_w, m_conv_w_out, m_kv_norm, m_w_kv, m_w_q, m_w_o, v_mix_norm_pre, v_mix_norm_post, v_ffn_norm_pre, v_ffn_norm_post, v_ffn_w_gate_up, v_ffn_w_down, v_conv_w_in, v_conv_w, v_conv_w_out, v_kv_norm, v_w_kv, v_w_q, v_w_o):
    me = 4 * lax.axis_index("x") + 2 * lax.axis_index("y") + lax.axis_index("c")
    h0 = x.reshape(S, D)
    target = loss_target.reshape(S, D)
    row = lambda a, l: a[l].reshape(1, D)
    g_kv = kv_norm.reshape(1, D)

    cw_shard = jnp.pad(conv_w[0], ((0, 5), (0, 0)))
    names = ["conv_in", "conv_w", "conv_out", "gate_up_0", "down_0", "kv", "q", "o", "gate_up_1", "down_1"]
    shards = [conv_w_in[0], cw_shard, conv_w_out[0], ffn_w_gate_up[0], ffn_w_down[0],
              w_kv, w_q[0], w_o[0], ffn_w_gate_up[1], ffn_w_down[1]]
    shards = [s if n == "conv_w" else s.astype(BF16) for n, s in zip(names, shards)]
    first = 3
    gather_first = _gather_start("gather_start_conv", shards[:first], me)
    gather_rest = _gather_start("gather_start_rest", shards[first:], me)

    def direct(group, after):
        ts = [names.index(n) for n in group]
        started, ts = (gather_first, ts) if ts[0] < first else (gather_rest, [t - first for t in ts])
        lands = _gather_wait(f"gather_wait_{group[0]}", started, ts, after)
        return _relay_start(f"relay_start_{group[0]}", lands)

    def finish(group, relayed, after):
        return dict(zip(group, _relay_wait(f"relay_wait_{group[0]}", relayed, after)))

    sent = {}

    def send(grads):
        started, token = _exchange_start(f"scatter_start_{next(iter(grads))}", list(grads.values()), me, "scatter")
        for i, name in enumerate(grads):
            sent[name] = (started, i)
        return token

    groups = [["conv_in", "conv_w", "conv_out"], ["gate_up_0", "down_0"], ["kv", "q"], ["o", "gate_up_1", "down_1"]]
    n0 = _rms_fwd("mix_prenorm_0", h0, [row(mix_norm_pre, 0)])[0]
    half = HEAD_DIM // 2
    inv_freq = ROPE_THETA ** (-jnp.arange(half, dtype=F32) / half)
    tables = _rope_tables("rope_tables", positions.reshape(S, 1), jnp.tile(inv_freq, 4).reshape(1, 128))
    w = finish(groups[0], direct(groups[0], tables[0]), n0)
    win = w["conv_in"].transpose(1, 0, 2).reshape(D, 3 * D)
    cw = w["conv_w"].transpose(1, 0, 2).reshape(8, D)
    wout = w["conv_out"].reshape(D, D)
    z = _fwd_rows("conv_in", n0, win, BF16)
    pre = _conv_fwd("conv_gate", z, cw)
    relayed = direct(groups[1], pre)
    y0 = _fwd_rows("conv_out", pre, wout)
    h1, (n1,) = _resid_rms("mix_postnorm_0", h0, y0, row(mix_norm_post, 0), [row(ffn_norm_pre, 0)])
    w = finish(groups[1], relayed, n1)
    gu0, act0 = _gate_up_act("ffn_gate_up_0", n1, w["gate_up_0"])
    wd0 = w["down_0"].reshape(NFB, FB, D)
    ffn0 = (n1, gu0, act0, w["gate_up_0"], wd0)
    relayed = direct(groups[2], act0)
    f0, h2, (nk, n2) = _down_resid("ffn_down_0_postnorm", act0, wd0, h1, row(ffn_norm_post, 0),
                                   [g_kv, row(mix_norm_pre, 1)])

    w = finish(groups[2], relayed, nk)
    wkv = w["kv"].transpose(1, 0, 2).reshape(D, 2 * QW)
    wq = w["q"].transpose(1, 0, 2).reshape(D, QW)
    qc, kc, vc, o_c, lse_c = [], [], [], [], []
    for g, d in enumerate(DILATIONS):
        q_g, k_g, v_g = _qkv_classes(f"qkv_proj_{g}", n2, nk, wq, wkv, g, d, tables)
        qc.append(q_g)
        kc.append(k_g)
        vc.append(v_g)
    relayed = direct(groups[3], vc[-1])
    for g, d in enumerate(DILATIONS):
        o_g, lse_g = _attn_fwd(f"attn_fwd_{g}", qc[g], kc[g], vc[g], d)
        o_c.append(o_g)
        lse_c.append(lse_g)
    o_mix = _mix_fwd("attn_mix", o_c, lse_c)
    w = finish(groups[3], relayed, o_mix)
    wo = w["o"].reshape(D, D)
    y1 = _fwd_rows("attn_out", o_mix, wo)
    h3, (n3,) = _resid_rms("mix_postnorm_1", h2, y1, row(mix_norm_post, 1), [row(ffn_norm_pre, 1)])
    gu1, act1 = _gate_up_act("ffn_gate_up_1", n3, w["gate_up_1"])
    wd1 = w["down_1"].reshape(NFB, FB, D)
    ffn1 = (n3, gu1, act1, w["gate_up_1"], wd1)

    dh4, df1, dg_fpost1, sq = _down_loss("ffn_down_1_loss", act1, wd1, h3, row(ffn_norm_post, 1), target)

    dh3, dg_fpre1, dy1, dg_mpost1 = _ffn_bwd(
        "1", dh4, df1, h3, ffn1, row(ffn_norm_pre, 1), send, (y1, row(mix_norm_post, 1)))
    dwo = _bwd_w_rows("attn_out_dw", o_mix, dy1).reshape(NDEV, D // NDEV, D)
    do = _bwd_x_rows("attn_out_dx", dy1, wo, BF16)
    lane = jnp.arange(128)
    ones_blockdiag = (lane[:, None] // HEAD_DIM == lane[None, :] // HEAD_DIM).astype(BF16)
    mixed = _mix_bwd("attn_mix_bwd", do, o_c, lse_c, ones_blockdiag)
    branch_grads = [_attn_bwd(f"attn_bwd_{g}", qc[g], kc[g], vc[g], mixed[g], lse_c[g], mixed[3 + g], d)
                    for g, d in enumerate(DILATIONS)]
    dq_raw, dkv = _attn_bwd_post("attn_bwd_post", branch_grads, *tables)
    tok = send({"o": dwo, "kv": _bwd_w_cols("kv_proj_dw", nk, dkv, 2 * QW // NDEV),
                "q": _bwd_w_cols("q_proj_dw", n2, dq_raw, QW // NDEV)})
    dnk = _bwd_x_plain("kv_proj_dx", dkv, wkv, after=tok)
    dn2 = _bwd_x_plain("q_proj_dx", dq_raw, wq)
    dh2, (dg_kv, dg_mpre1), df0, dg_fpost0 = _rms_bwd(
        "kv_and_mix_prenorm_bwd_1", h2, [(g_kv, dnk), (row(mix_norm_pre, 1), dn2)], dh3, F32,
        then=(f0, row(ffn_norm_post, 0)))

    dh1, dg_fpre0, dy0, dg_mpost0 = _ffn_bwd(
        "0", dh2, df0, h1, ffn0, row(ffn_norm_pre, 0), send, (y0, row(mix_norm_post, 0)))
    dwout = _bwd_w_rows("conv_out_dw", pre, dy0).reshape(NDEV, D // NDEV, D)
    dpre = _bwd_x_rows("conv_out_dx", dy0, wout, BF16)
    dz, dcw = _conv_bwd("conv_gate_bwd", z, dpre, cw)
    tok = send({"conv_out": dwout, "conv_in": _bwd_w_cols("conv_in_dw", n0, dz, 3 * D // NDEV)})
    dn0 = _bwd_x_plain("conv_in_dx", dz, win, after=tok)
    dh0, (dg_mpre0,) = _rms_bwd("mix_prenorm_bwd_0", h0, [(row(mix_norm_pre, 0), dn0)], dh1, F32)

    small = _pack_small("pack_small_grads", [dg_mpre0, dg_mpre1, dg_mpost0, dg_mpost1, dg_fpre0, dg_fpre1,
                                             dg_fpost0, dg_fpost1, dg_kv], dcw, sq)

    done = [small]

    def upd(tag, w, m, v):
        parts = _exchange_wait(f"scatter_wait_{tag}", *sent[tag], done[-1], "scatter")
        shape = w.shape
        flat = lambda a: a.reshape(parts.shape[1:])
        res = _adamw(f"adamw_{tag}", parts, flat(w), flat(m), flat(v))
        done.append(res[0])
        return [r.reshape(shape) for r in res]

    def upd_layer(tag, l, w, m, v, other):
        parts = _exchange_wait(f"scatter_wait_{tag}_{l}", *sent[f"{tag}_{l}"], done[-1], "scatter")
        res = _adamw(f"adamw_{tag}_{l}", parts, w, m, v, layer=l, other=other)
        done.append(res[0])
        return list(res)

    res = {}
    down_1 = upd_layer("down", 1, ffn_w_down, m_ffn_w_down, v_ffn_w_down, None)
    gate_up_t = [jnp.swapaxes(a, 1, 2) for a in (ffn_w_gate_up, m_ffn_w_gate_up, v_ffn_w_gate_up)]
    gate_up_1 = upd_layer("gate_up", 1, *gate_up_t, None)
    res["w_o"] = upd("o", w_o, m_w_o, v_w_o)
    res["w_q"] = upd("q", w_q, m_w_q, v_w_q)
    res["w_kv"] = upd("kv", w_kv, m_w_kv, v_w_kv)

    small_all = _exchange("gather_small_grads", [small], "gather", done[-1])[0]
    vec = lambda a: a.reshape(1, D)
    gain_res, taps, loss = _adamw_gains("adamw_gains", small_all, [
        (mix_norm_pre, m_mix_norm_pre, v_mix_norm_pre), (mix_norm_post, m_mix_norm_post, v_mix_norm_post),
        (ffn_norm_pre, m_ffn_norm_pre, v_ffn_norm_pre), (ffn_norm_post, m_ffn_norm_post, v_ffn_norm_post),
        (vec(kv_norm), vec(m_kv_norm), vec(v_kv_norm))])
    dcw_mine = lax.dynamic_slice(taps, (0, me * 128), (8, 128))
    pad8 = lambda a, fill: jnp.pad(a[0], ((0, 5), (0, 0)), constant_values=fill)
    cw_res = [r[0:3].reshape(1, 3, 128) for r in
              _adamw("adamw_conv_w", dcw_mine.reshape(1, 8, 128), cw_shard, pad8(m_conv_w, 0.0), pad8(v_conv_w, 1.0))]

    res.update({
        "mix_norm_pre": gain_res[0],
        "mix_norm_post": gain_res[1],
        "ffn_norm_pre": gain_res[2],
        "ffn_norm_post": gain_res[3],
        "kv_norm": [r.reshape(D) for r in gain_res[4]],
        "conv_w": cw_res,
    })
    done.append(small_all)
    res["ffn_w_down"] = upd_layer("down", 0, ffn_w_down, m_ffn_w_down, v_ffn_w_down, down_1)
    res["ffn_w_gate_up"] = [jnp.swapaxes(r, 1, 2) for r in upd_layer("gate_up", 0, *gate_up_t, gate_up_1)]
    res["conv_w_out"] = upd("conv_out", conv_w_out, m_conv_w_out, v_conv_w_out)
    res["conv_w_in"] = upd("conv_in", conv_w_in, m_conv_w_in, v_conv_w_in)
    order = ["mix_norm_pre", "mix_norm_post", "ffn_norm_pre", "ffn_norm_post", "ffn_w_gate_up", "ffn_w_down",
             "conv_w_in", "conv_w", "conv_w_out", "kv_norm", "w_kv", "w_q", "w_o"]
    out = [loss, dh0.reshape(1, S, D)]
    for i in range(4):
        out += [res[name][i] for name in order]
    return tuple(out)
```
